```python
import jax, jax.numpy as jnp
from jax import lax
import numpy as np

D_MODEL = 1024
BATCH = 8
SEQ = 8192
DEPTH = 1

D_INNER = 2 * D_MODEL
SSD_HEAD_DIM = 64
SSD_HEADS = D_INNER // SSD_HEAD_DIM
SSD_GROUPS = 4
SSD_HEADS_PER_GROUP = SSD_HEADS // SSD_GROUPS
SSD_STATE = 128
CONV_WIDTH = 4
CHUNK = 128
XBC_WIDTH = D_INNER + 2 * SSD_GROUPS * SSD_STATE
DT_MIN = 0.001
DT_MAX = 0.1
POOL_WINDOWS = (2, 4, 8, 16)
POOL_GROUPS = len(POOL_WINDOWS)
POOL_WIDTH = D_MODEL
POOL_GROUP_WIDTH = POOL_WIDTH // POOL_GROUPS
N_BRANCHES = 2
D_FF = 4 * D_MODEL
N_MOD = 6
NORM_EPS = 1e-5
IN_SPLITS = (D_INNER,
             D_INNER + XBC_WIDTH,
             D_INNER + XBC_WIDTH + SSD_HEADS,
             D_INNER + XBC_WIDTH + SSD_HEADS + POOL_WIDTH)
IN_COLS = IN_SPLITS[-1] + N_BRANCHES * D_MODEL

kernel_name = "hybrid_ssd_pool_gated_block"


def rmsnorm(x, w, eps=NORM_EPS):
    x32 = x.astype(jnp.float32)
    y = x32 * lax.rsqrt(jnp.mean(x32 * x32, axis=-1, keepdims=True) + eps)
    return y.astype(x.dtype) * w


def causal_depthwise_conv(u, w, b):
    k_width = w.shape[0]
    seq = u.shape[1]
    up = jnp.pad(u, ((0, 0), (k_width - 1, 0), (0, 0)))
    out = b
    for k in range(k_width):
        out = out + up[:, k:k + seq] * w[k]
    return out


def segsum_decay(a_cs):
    q = a_cs.shape[-1]
    seg = a_cs[..., :, None] - a_cs[..., None, :]
    mask = jnp.tril(jnp.ones((q, q), dtype=bool))
    return jnp.exp(jnp.where(mask, seg, -jnp.inf))


def ssd_chunked_scan(xdt, dtA, bmat, cmat):
    b, seq, _, p = xdt.shape
    g, hg, n, q = SSD_GROUPS, SSD_HEADS_PER_GROUP, SSD_STATE, CHUNK
    nc = seq // q
    x = xdt.astype(jnp.float32).reshape(b, nc, q, g, hg, p)
    a = dtA.astype(jnp.float32).reshape(b, nc, q, g, hg).transpose(0, 3, 4, 1, 2)
    bc = bmat.astype(jnp.float32).reshape(b, nc, q, g, n)
    cc = cmat.astype(jnp.float32).reshape(b, nc, q, g, n)
    a_cs = jnp.cumsum(a, axis=-1)
    scores = jnp.einsum('bclgn,bcsgn->bgcls', cc, bc)
    mmat = scores[:, :, None] * segsum_decay(a_cs)
    y_diag = jnp.einsum('bghcls,bcsghp->bclghp', mmat, x)
    decay_states = jnp.exp(a_cs[..., -1:] - a_cs)
    states = jnp.einsum('bcsgn,bghcs,bcsghp->bcghpn', bc, decay_states, x)
    chunk_decay = jnp.exp(a_cs[..., -1])

    def step(h, inp):
        s_c, d_c = inp
        return h * d_c[..., None, None] + s_c, h

    h0 = jnp.zeros((b, g, hg, p, n), jnp.float32)
    _, prev = lax.scan(step, h0, (jnp.moveaxis(states, 1, 0), jnp.moveaxis(chunk_decay, 3, 0)))
    prev = jnp.moveaxis(prev, 0, 1)
    y_off = jnp.einsum('bclgn,bcghpn,bghcl->bclghp', cc, prev, jnp.exp(a_cs))
    return (y_diag + y_off).reshape(b, seq, SSD_HEADS, p)


def causal_multiscale_pool(u, pool_w, pool_scale):
    b, seq, _ = u.shape
    gw = POOL_GROUP_WIDTH
    u32 = u.astype(jnp.float32)
    cs = jnp.concatenate([jnp.zeros((b, 1, POOL_WIDTH), jnp.float32), jnp.cumsum(u32, axis=1)], axis=1)
    count = jnp.arange(1, seq + 1, dtype=jnp.float32)[:, None]
    outs = []
    for gi, win in enumerate(POOL_WINDOWS):
        csg = cs[:, :, gi * gw:(gi + 1) * gw]
        start = jnp.concatenate([jnp.zeros((b, win - 1, gw), jnp.float32), csg[:, :seq - win + 1]], axis=1)
        mean = (csg[:, 1:] - start) / jnp.minimum(count, float(win))
        outs.append(mean - u32[..., gi * gw:(gi + 1) * gw])
    pooled = jnp.stack(outs, axis=2).astype(u.dtype)
    y = jnp.einsum('blgc,gcd->blgd', pooled, pool_w).reshape(b, seq, POOL_WIDTH)
    return y * pool_scale


def hybrid_mixer(h, w_in, conv_w, conv_b, dt_bias, a_log, d_skip, ssd_norm_w,
                 w_branch_ssd, pool_w, pool_scale, w_branch_pool, w_out):
    b, seq, _ = h.shape
    proj = h @ w_in
    z, xbc, dt_raw, u_pool, gate_logits = jnp.split(proj, IN_SPLITS, axis=-1)
    xbc = jax.nn.silu(causal_depthwise_conv(xbc, conv_w, conv_b))
    xs, bmat, cmat = jnp.split(xbc, (D_INNER, D_INNER + SSD_GROUPS * SSD_STATE), axis=-1)
    xs = xs.reshape(b, seq, SSD_HEADS, SSD_HEAD_DIM)
    bmat = bmat.reshape(b, seq, SSD_GROUPS, SSD_STATE)
    cmat = cmat.reshape(b, seq, SSD_GROUPS, SSD_STATE)
    dt = jax.nn.softplus(dt_raw.astype(jnp.float32) + dt_bias.astype(jnp.float32))
    a_cont = -jnp.exp(a_log.astype(jnp.float32))
    x32 = xs.astype(jnp.float32)
    y = ssd_chunked_scan(x32 * dt[..., None], dt * a_cont, bmat, cmat)
    y = y + d_skip.astype(jnp.float32)[:, None] * x32
    y = y.reshape(b, seq, D_INNER).astype(h.dtype) * jax.nn.silu(z)
    y = rmsnorm(y.reshape(b, seq, SSD_GROUPS, D_INNER // SSD_GROUPS),
                ssd_norm_w.reshape(SSD_GROUPS, D_INNER // SSD_GROUPS))
    y_ssd = y.reshape(b, seq, D_INNER) @ w_branch_ssd
    y_pool = causal_multiscale_pool(u_pool, pool_w, pool_scale) @ w_branch_pool
    g_ssd, g_pool = jnp.split(jax.nn.sigmoid(gate_logits), N_BRANCHES, axis=-1)
    return (g_ssd * y_ssd + g_pool * y_pool) @ w_out


def _fwd_setup_inputs(seed: int = 0) -> dict:
    key = jax.random.key(seed)
    ks = jax.random.split(key, 24)
    f32 = jnp.float32
    nrm = lambda k, shape, s: jax.random.normal(k, shape, f32) * s
    dt0 = jnp.exp(jax.random.uniform(ks[6], (DEPTH, SSD_HEADS), f32)
                  * (np.log(DT_MAX) - np.log(DT_MIN)) + np.log(DT_MIN))
    return {
        "x": nrm(ks[0], (BATCH, SEQ, D_MODEL), 1.0),
        "c": nrm(ks[1], (BATCH, D_MODEL), 1.0),
        "w_ada": nrm(ks[2], (DEPTH, D_MODEL, N_MOD * D_MODEL), D_MODEL ** -0.5),
        "b_ada": nrm(ks[3], (DEPTH, N_MOD * D_MODEL), 0.01),
        "norm_mix_w": 1.0 + nrm(ks[4], (DEPTH, D_MODEL), 0.05),
        "w_in": nrm(ks[5], (DEPTH, D_MODEL, IN_COLS), D_MODEL ** -0.5),
        "conv_w": nrm(ks[7], (DEPTH, CONV_WIDTH, XBC_WIDTH), CONV_WIDTH ** -0.5),
        "conv_b": nrm(ks[8], (DEPTH, XBC_WIDTH), 0.01),
        "dt_bias": dt0 + jnp.log(-jnp.expm1(-dt0)),
        "a_log": jnp.log(jax.random.uniform(ks[9], (DEPTH, SSD_HEADS), f32, 1.0, 16.0)),
        "d_skip": 1.0 + nrm(ks[10], (DEPTH, SSD_HEADS), 0.1),
        "ssd_norm_w": 1.0 + nrm(ks[11], (DEPTH, D_INNER), 0.05),
        "w_branch_ssd": nrm(ks[12], (DEPTH, D_INNER, D_MODEL), D_INNER ** -0.5),
        "pool_w": nrm(ks[13], (DEPTH, POOL_GROUPS, POOL_GROUP_WIDTH, POOL_GROUP_WIDTH), POOL_GROUP_WIDTH ** -0.5),
        "pool_scale": 1.0 + nrm(ks[14], (DEPTH, POOL_WIDTH), 0.1),
        "w_branch_pool": nrm(ks[15], (DEPTH, POOL_WIDTH, D_MODEL), POOL_WIDTH ** -0.5),
        "w_out": nrm(ks[16], (DEPTH, D_MODEL, D_MODEL), D_MODEL ** -0.5),
        "norm_mlp_w": 1.0 + nrm(ks[17], (DEPTH, D_MODEL), 0.05),
        "w_up": nrm(ks[18], (DEPTH, D_MODEL, D_FF), D_MODEL ** -0.5),
        "w_down": nrm(ks[19], (DEPTH, D_FF, D_MODEL), D_FF ** -0.5),
        "norm_final_w": 1.0 + nrm(ks[20], (D_MODEL,), 0.05),
    }


def _fwd_reference(x, c, w_ada, b_ada, norm_mix_w, w_in, conv_w, conv_b, dt_bias, a_log, d_skip,
              ssd_norm_w, w_branch_ssd, pool_w, pool_scale, w_branch_pool, w_out,
              norm_mlp_w, w_up, w_down, norm_final_w):
    for i in range(DEPTH):
        mod = (jax.nn.silu(c) @ w_ada[i] + b_ada[i])[:, None, :]
        shift_m, scale_m, gate_m, shift_f, scale_f, gate_f = jnp.split(mod, N_MOD, axis=-1)
        h = rmsnorm(x, norm_mix_w[i]) * (1.0 + scale_m) + shift_m
        x = x + gate_m * hybrid_mixer(h, w_in[i], conv_w[i], conv_b[i], dt_bias[i], a_log[i],
                                      d_skip[i], ssd_norm_w[i], w_branch_ssd[i], pool_w[i],
                                      pool_scale[i], w_branch_pool[i], w_out[i])
        h = rmsnorm(x, norm_mlp_w[i]) * (1.0 + scale_f) + shift_f
        x = x + gate_f * (jnp.square(jax.nn.relu(h @ w_up[i])) @ w_down[i])
    return rmsnorm(x, norm_final_w)


import jax as _jax
import jax.numpy as _jnp

TWIN_FORMAT = 'train_step'
FWD_PARAMS = ['x', 'c', 'w_ada', 'b_ada', 'norm_mix_w', 'w_in', 'conv_w', 'conv_b', 'dt_bias', 'a_log', 'd_skip', 'ssd_norm_w', 'w_branch_ssd', 'pool_w', 'pool_scale', 'w_branch_pool', 'w_out', 'norm_mlp_w', 'w_up', 'w_down', 'norm_final_w']
TWIN_WEIGHTS = ['w_ada', 'b_ada', 'norm_mix_w', 'w_in', 'conv_w', 'conv_b', 'dt_bias', 'a_log', 'd_skip', 'ssd_norm_w', 'w_branch_ssd', 'pool_w', 'pool_scale', 'w_branch_pool', 'w_out', 'norm_mlp_w', 'w_up', 'w_down', 'norm_final_w']
TWIN_DIFF_INPUT = 'x'
TWIN_INPUTS = ['x', 'c', 'w_ada', 'b_ada', 'norm_mix_w', 'w_in', 'conv_w', 'conv_b', 'dt_bias', 'a_log', 'd_skip', 'ssd_norm_w', 'w_branch_ssd', 'pool_w', 'pool_scale', 'w_branch_pool', 'w_out', 'norm_mlp_w', 'w_up', 'w_down', 'norm_final_w', 'loss_target', 'm_w_ada', 'm_b_ada', 'm_norm_mix_w', 'm_w_in', 'm_conv_w', 'm_conv_b', 'm_dt_bias', 'm_a_log', 'm_d_skip', 'm_ssd_norm_w', 'm_w_branch_ssd', 'm_pool_w', 'm_pool_scale', 'm_w_branch_pool', 'm_w_out', 'm_norm_mlp_w', 'm_w_up', 'm_w_down', 'm_norm_final_w', 'v_w_ada', 'v_b_ada', 'v_norm_mix_w', 'v_w_in', 'v_conv_w', 'v_conv_b', 'v_dt_bias', 'v_a_log', 'v_d_skip', 'v_ssd_norm_w', 'v_w_branch_ssd', 'v_pool_w', 'v_pool_scale', 'v_w_branch_pool', 'v_w_out', 'v_norm_mlp_w', 'v_w_up', 'v_w_down', 'v_norm_final_w']
TWIN_OUTPUTS = ['loss', 'grad_x', 'grad_w_ada', 'grad_b_ada', 'grad_norm_mix_w', 'grad_w_in', 'grad_conv_w', 'grad_conv_b', 'grad_dt_bias', 'grad_a_log', 'grad_d_skip', 'grad_ssd_norm_w', 'grad_w_branch_ssd', 'grad_pool_w', 'grad_pool_scale', 'grad_w_branch_pool', 'grad_w_out', 'grad_norm_mlp_w', 'grad_w_up', 'grad_w_down', 'grad_norm_final_w', 'delta_w_ada', 'delta_b_ada', 'delta_norm_mix_w', 'delta_w_in', 'delta_conv_w', 'delta_conv_b', 'delta_dt_bias', 'delta_a_log', 'delta_d_skip', 'delta_ssd_norm_w', 'delta_w_branch_ssd', 'delta_pool_w', 'delta_pool_scale', 'delta_w_branch_pool', 'delta_w_out', 'delta_norm_mlp_w', 'delta_w_up', 'delta_w_down', 'delta_norm_final_w', 'new_m_w_ada', 'new_m_b_ada', 'new_m_norm_mix_w', 'new_m_w_in', 'new_m_conv_w', 'new_m_conv_b', 'new_m_dt_bias', 'new_m_a_log', 'new_m_d_skip', 'new_m_ssd_norm_w', 'new_m_w_branch_ssd', 'new_m_pool_w', 'new_m_pool_scale', 'new_m_w_branch_pool', 'new_m_w_out', 'new_m_norm_mlp_w', 'new_m_w_up', 'new_m_w_down', 'new_m_norm_final_w', 'new_v_w_ada', 'new_v_b_ada', 'new_v_norm_mix_w', 'new_v_w_in', 'new_v_conv_w', 'new_v_conv_b', 'new_v_dt_bias', 'new_v_a_log', 'new_v_d_skip', 'new_v_ssd_norm_w', 'new_v_w_branch_ssd', 'new_v_pool_w', 'new_v_pool_scale', 'new_v_w_branch_pool', 'new_v_w_out', 'new_v_norm_mlp_w', 'new_v_w_up', 'new_v_w_down', 'new_v_norm_final_w']
TWIN_LEAF_KINDS = {'loss': 'loss', 'grad_x': 'grad_x', 'grad_w_ada': 'grad_w', 'grad_b_ada': 'grad_w', 'grad_norm_mix_w': 'grad_w', 'grad_w_in': 'grad_w', 'grad_conv_w': 'grad_w', 'grad_conv_b': 'grad_w', 'grad_dt_bias': 'grad_w', 'grad_a_log': 'grad_w', 'grad_d_skip': 'grad_w', 'grad_ssd_norm_w': 'grad_w', 'grad_w_branch_ssd': 'grad_w', 'grad_pool_w': 'grad_w', 'grad_pool_scale': 'grad_w', 'grad_w_branch_pool': 'grad_w', 'grad_w_out': 'grad_w', 'grad_norm_mlp_w': 'grad_w', 'grad_w_up': 'grad_w', 'grad_w_down': 'grad_w', 'grad_norm_final_w': 'grad_w', 'delta_w_ada': 'delta_w', 'delta_b_ada': 'delta_w', 'delta_norm_mix_w': 'delta_w', 'delta_w_in': 'delta_w', 'delta_conv_w': 'delta_w', 'delta_conv_b': 'delta_w', 'delta_dt_bias': 'delta_w', 'delta_a_log': 'delta_w', 'delta_d_skip': 'delta_w', 'delta_ssd_norm_w': 'delta_w', 'delta_w_branch_ssd': 'delta_w', 'delta_pool_w': 'delta_w', 'delta_pool_scale': 'delta_w', 'delta_w_branch_pool': 'delta_w', 'delta_w_out': 'delta_w', 'delta_norm_mlp_w': 'delta_w', 'delta_w_up': 'delta_w', 'delta_w_down': 'delta_w', 'delta_norm_final_w': 'delta_w', 'new_m_w_ada': 'new_m', 'new_m_b_ada': 'new_m', 'new_m_norm_mix_w': 'new_m', 'new_m_w_in': 'new_m', 'new_m_conv_w': 'new_m', 'new_m_conv_b': 'new_m', 'new_m_dt_bias': 'new_m', 'new_m_a_log': 'new_m', 'new_m_d_skip': 'new_m', 'new_m_ssd_norm_w': 'new_m', 'new_m_w_branch_ssd': 'new_m', 'new_m_pool_w': 'new_m', 'new_m_pool_scale': 'new_m', 'new_m_w_branch_pool': 'new_m', 'new_m_w_out': 'new_m', 'new_m_norm_mlp_w': 'new_m', 'new_m_w_up': 'new_m', 'new_m_w_down': 'new_m', 'new_m_norm_final_w': 'new_m', 'new_v_w_ada': 'new_v', 'new_v_b_ada': 'new_v', 'new_v_norm_mix_w': 'new_v', 'new_v_w_in': 'new_v', 'new_v_conv_w': 'new_v', 'new_v_conv_b': 'new_v', 'new_v_dt_bias': 'new_v', 'new_v_a_log': 'new_v', 'new_v_d_skip': 'new_v', 'new_v_ssd_norm_w': 'new_v', 'new_v_w_branch_ssd': 'new_v', 'new_v_pool_w': 'new_v', 'new_v_pool_scale': 'new_v', 'new_v_w_branch_pool': 'new_v', 'new_v_w_out': 'new_v', 'new_v_norm_mlp_w': 'new_v', 'new_v_w_up': 'new_v', 'new_v_w_down': 'new_v', 'new_v_norm_final_w': 'new_v'}


def _forward(args):
    return _fwd_reference(*[args[k] for k in FWD_PARAMS])


def _output_shape():
    def fwd():
        inp = _fwd_setup_inputs(0)
        return _fwd_reference(*[inp[k] for k in FWD_PARAMS])
    out = _jax.eval_shape(fwd)
    return out.shape, out.dtype

N_MICROBATCH = 1
ADAM_LR = 0.001
ADAM_B1 = 0.9
ADAM_B2 = 0.999
ADAM_EPS = 1e-08
ADAM_WD = 0.01
ADAM_STEP = 10
PER_EXAMPLE_BATCH_AXIS = {'x': 0, 'c': 0, 'loss_target': 0}
SHARED_INPUTS = []
_WEIGHT_DTYPES = {'w_ada': _jnp.float32, 'b_ada': _jnp.float32, 'norm_mix_w': _jnp.float32, 'w_in': _jnp.float32, 'conv_w': _jnp.float32, 'conv_b': _jnp.float32, 'dt_bias': _jnp.float32, 'a_log': _jnp.float32, 'd_skip': _jnp.float32, 'ssd_norm_w': _jnp.float32, 'w_branch_ssd': _jnp.float32, 'pool_w': _jnp.float32, 'pool_scale': _jnp.float32, 'w_branch_pool': _jnp.float32, 'w_out': _jnp.float32, 'norm_mlp_w': _jnp.float32, 'w_up': _jnp.float32, 'w_down': _jnp.float32, 'norm_final_w': _jnp.float32}
MOMENT_SCALE = {'w_ada': 4.666255e-01, 'b_ada': 1.000011e+00, 'norm_mix_w': 1.490926e-01, 'w_in': 5.944310e-02, 'conv_w': 5.860614e-02, 'conv_b': 6.294131e-02, 'dt_bias': 1.723341e-01, 'a_log': 1.756843e-01, 'd_skip': 2.217906e-01, 'ssd_norm_w': 7.301473e-02, 'w_branch_ssd': 9.977387e-02, 'pool_w': 8.184483e-02, 'pool_scale': 8.191187e-02, 'w_branch_pool': 8.263642e-02, 'w_out': 1.308863e-01, 'norm_mlp_w': 2.112796e-01, 'w_up': 1.697291e-01, 'w_down': 6.316476e-01, 'norm_final_w': 6.872228e+01}


def _to_microbatches(a, axis):
    t = _jnp.moveaxis(a, axis, 0)
    t = t.reshape((N_MICROBATCH, t.shape[0] // N_MICROBATCH) + t.shape[1:])
    return _jnp.moveaxis(t, 1, axis + 1)


def setup_inputs(seed: int = 0) -> dict:
    inp = _fwd_setup_inputs(seed)
    key = _jax.random.fold_in(_jax.random.key(seed), 7919)
    shape, _ = _output_shape()
    out = dict(inp)
    out["loss_target"] = _jax.random.normal(_jax.random.fold_in(key, 0), shape, _jnp.float32)
    for i, name in enumerate(TWIN_WEIGHTS):
        w = inp[name].astype(_jnp.float32)
        if MOMENT_SCALE is None:
            s = _jnp.sqrt(_jnp.mean(_jnp.square(w)) + 1e-30)
        else:
            s = MOMENT_SCALE[name]
        km, kv = _jax.random.split(_jax.random.fold_in(key, i + 1))
        out[name] = w
        out["m_" + name] = s * _jax.random.normal(km, w.shape, _jnp.float32)
        out["v_" + name] = (s * s) * _jax.random.uniform(kv, w.shape, _jnp.float32, 0.5, 1.5)
    if N_MICROBATCH > 1:
        for name, axis in PER_EXAMPLE_BATCH_AXIS.items():
            out[name] = _to_microbatches(out[name], axis)
    return {'x': out['x'], 'c': out['c'], 'w_ada': out['w_ada'], 'b_ada': out['b_ada'], 'norm_mix_w': out['norm_mix_w'], 'w_in': out['w_in'], 'conv_w': out['conv_w'], 'conv_b': out['conv_b'], 'dt_bias': out['dt_bias'], 'a_log': out['a_log'], 'd_skip': out['d_skip'], 'ssd_norm_w': out['ssd_norm_w'], 'w_branch_ssd': out['w_branch_ssd'], 'pool_w': out['pool_w'], 'pool_scale': out['pool_scale'], 'w_branch_pool': out['w_branch_pool'], 'w_out': out['w_out'], 'norm_mlp_w': out['norm_mlp_w'], 'w_up': out['w_up'], 'w_down': out['w_down'], 'norm_final_w': out['norm_final_w'], 'loss_target': out['loss_target'], 'm_w_ada': out['m_w_ada'], 'm_b_ada': out['m_b_ada'], 'm_norm_mix_w': out['m_norm_mix_w'], 'm_w_in': out['m_w_in'], 'm_conv_w': out['m_conv_w'], 'm_conv_b': out['m_conv_b'], 'm_dt_bias': out['m_dt_bias'], 'm_a_log': out['m_a_log'], 'm_d_skip': out['m_d_skip'], 'm_ssd_norm_w': out['m_ssd_norm_w'], 'm_w_branch_ssd': out['m_w_branch_ssd'], 'm_pool_w': out['m_pool_w'], 'm_pool_scale': out['m_pool_scale'], 'm_w_branch_pool': out['m_w_branch_pool'], 'm_w_out': out['m_w_out'], 'm_norm_mlp_w': out['m_norm_mlp_w'], 'm_w_up': out['m_w_up'], 'm_w_down': out['m_w_down'], 'm_norm_final_w': out['m_norm_final_w'], 'v_w_ada': out['v_w_ada'], 'v_b_ada': out['v_b_ada'], 'v_norm_mix_w': out['v_norm_mix_w'], 'v_w_in': out['v_w_in'], 'v_conv_w': out['v_conv_w'], 'v_conv_b': out['v_conv_b'], 'v_dt_bias': out['v_dt_bias'], 'v_a_log': out['v_a_log'], 'v_d_skip': out['v_d_skip'], 'v_ssd_norm_w': out['v_ssd_norm_w'], 'v_w_branch_ssd': out['v_w_branch_ssd'], 'v_pool_w': out['v_pool_w'], 'v_pool_scale': out['v_pool_scale'], 'v_w_branch_pool': out['v_w_branch_pool'], 'v_w_out': out['v_w_out'], 'v_norm_mlp_w': out['v_norm_mlp_w'], 'v_w_up': out['v_w_up'], 'v_w_down': out['v_w_down'], 'v_norm_final_w': out['v_norm_final_w']}


def _loss(weights, diff, rest, loss_target):
    with _jax.named_scope("forward"):
        args = {**rest, TWIN_DIFF_INPUT: diff, **{k: w.astype(_WEIGHT_DTYPES[k]) for k, w in weights.items()}}
        y = _forward(args)
    with _jax.named_scope("loss_head"):
        err = _jnp.square(y.astype(_jnp.float32) - loss_target)
        return 0.5 * _jnp.sum(_jnp.mean(err, axis=-1)) if err.ndim else 0.5 * err


def _adamw(w, g, m, v):
    m = ADAM_B1 * m + (1.0 - ADAM_B1) * g
    v = ADAM_B2 * v + (1.0 - ADAM_B2) * _jnp.square(g)
    m_hat = m / (1.0 - ADAM_B1 ** ADAM_STEP)
    v_hat = v / (1.0 - ADAM_B2 ** ADAM_STEP)
    delta = -ADAM_LR * (m_hat / (_jnp.sqrt(v_hat) + ADAM_EPS) + ADAM_WD * w)
    return delta, m, v


def reference(x, c, w_ada, b_ada, norm_mix_w, w_in, conv_w, conv_b, dt_bias, a_log, d_skip, ssd_norm_w, w_branch_ssd, pool_w, pool_scale, w_branch_pool, w_out, norm_mlp_w, w_up, w_down, norm_final_w, loss_target, m_w_ada, m_b_ada, m_norm_mix_w, m_w_in, m_conv_w, m_conv_b, m_dt_bias, m_a_log, m_d_skip, m_ssd_norm_w, m_w_branch_ssd, m_pool_w, m_pool_scale, m_w_branch_pool, m_w_out, m_norm_mlp_w, m_w_up, m_w_down, m_norm_final_w, v_w_ada, v_b_ada, v_norm_mix_w, v_w_in, v_conv_w, v_conv_b, v_dt_bias, v_a_log, v_d_skip, v_ssd_norm_w, v_w_branch_ssd, v_pool_w, v_pool_scale, v_w_branch_pool, v_w_out, v_norm_mlp_w, v_w_up, v_w_down, v_norm_final_w):
    given = dict(x=x, c=c, w_ada=w_ada, b_ada=b_ada, norm_mix_w=norm_mix_w, w_in=w_in, conv_w=conv_w, conv_b=conv_b, dt_bias=dt_bias, a_log=a_log, d_skip=d_skip, ssd_norm_w=ssd_norm_w, w_branch_ssd=w_branch_ssd, pool_w=pool_w, pool_scale=pool_scale, w_branch_pool=w_branch_pool, w_out=w_out, norm_mlp_w=norm_mlp_w, w_up=w_up, w_down=w_down, norm_final_w=norm_final_w, loss_target=loss_target, m_w_ada=m_w_ada, m_b_ada=m_b_ada, m_norm_mix_w=m_norm_mix_w, m_w_in=m_w_in, m_conv_w=m_conv_w, m_conv_b=m_conv_b, m_dt_bias=m_dt_bias, m_a_log=m_a_log, m_d_skip=m_d_skip, m_ssd_norm_w=m_ssd_norm_w, m_w_branch_ssd=m_w_branch_ssd, m_pool_w=m_pool_w, m_pool_scale=m_pool_scale, m_w_branch_pool=m_w_branch_pool, m_w_out=m_w_out, m_norm_mlp_w=m_norm_mlp_w, m_w_up=m_w_up, m_w_down=m_w_down, m_norm_final_w=m_norm_final_w, v_w_ada=v_w_ada, v_b_ada=v_b_ada, v_norm_mix_w=v_norm_mix_w, v_w_in=v_w_in, v_conv_w=v_conv_w, v_conv_b=v_conv_b, v_dt_bias=v_dt_bias, v_a_log=v_a_log, v_d_skip=v_d_skip, v_ssd_norm_w=v_ssd_norm_w, v_w_branch_ssd=v_w_branch_ssd, v_pool_w=v_pool_w, v_pool_scale=v_pool_scale, v_w_branch_pool=v_w_branch_pool, v_w_out=v_w_out, v_norm_mlp_w=v_norm_mlp_w, v_w_up=v_w_up, v_w_down=v_w_down, v_norm_final_w=v_norm_final_w)
    weights = {n: given[n] for n in TWIN_WEIGHTS}
    shared = {n: given[n] for n in SHARED_INPUTS}
    per_example = {n: given[n] for n in ['x', 'c']}
    grad_fn = _jax.value_and_grad(_loss, argnums=(0, 1))

    def one_microbatch(ex, loss_target):
        ex = dict(ex)
        diff = ex.pop(TWIN_DIFF_INPUT)
        return grad_fn(weights, diff, {**shared, **ex}, loss_target)

    if N_MICROBATCH == 1:
        loss, (grad_w, grad_x) = one_microbatch(per_example, given["loss_target"])
    else:
        def body(carry, xs):
            loss_sum, grad_sum = carry
            l_k, (gw_k, gx_k) = one_microbatch(xs[0], xs[1])
            with _jax.named_scope("update"):
                return (loss_sum + l_k, _jax.tree.map(_jnp.add, grad_sum, gw_k)), gx_k

        init = (_jnp.zeros((), _jnp.float32), _jax.tree.map(_jnp.zeros_like, weights))
        (loss, grad_w), grad_x = _jax.lax.scan(body, init, (per_example, given["loss_target"]))
    with _jax.named_scope("update"):
        delta_w, new_m, new_v = {}, {}, {}
        for n in TWIN_WEIGHTS:
            delta_w[n], new_m[n], new_v[n] = _adamw(weights[n], grad_w[n], given["m_" + n], given["v_" + n])
    return (loss, grad_x, *[grad_w[n] for n in TWIN_WEIGHTS], *[delta_w[n] for n in TWIN_WEIGHTS],
            *[new_m[n] for n in TWIN_WEIGHTS], *[new_v[n] for n in TWIN_WEIGHTS])
```

```python
import functools

import numpy as np
import jax
import jax.numpy as jnp
from jax import lax
from jax.experimental import pallas as pl
from jax.experimental.pallas import tpu as pltpu

F32 = jnp.float32
BF16 = jnp.bfloat16
SLAB_DT = jnp.bfloat16
_MXU_DTYPE = jnp.bfloat16

N_DEV = 8
D = 1024
DI = 2048
NH = 32
HP = 64
NG = 4
NS = 128
Q = 128
XBC = DI + 2 * NG * NS
DFF = 4096
N_IN = 8224
EPS = 1e-5
POOL_W = 1024
PGW = 256

C_XBC, C_POOL, C_Z, C_GATE, C_DT = 0, 3072, 4096, 6144, 8192
DT_PAD = 256
NPROJ = C_DT + DT_PAD

IN_ROWS = N_IN // N_DEV
IN_ROWS_P = 1040
SLAB_PARTS = (("in", IN_ROWS_P), ("bssd", 256), ("pool", 32), ("bpool", 128), ("out", 128),
              ("up", 512), ("down", 512), ("conv", 16))
SLAB_ROWS = 2688
SLAB_TILE = 384
SLAB_OFF = {}
_o = 0
for _n, _r in SLAB_PARTS:
    SLAB_OFF[_n] = _o
    _o += _r

SV_PARTS = (("b_ada", 6144), ("norm_mix_w", 1024), ("conv_b", 3072), ("dt_bias", 128), ("a_log", 128),
            ("d_skip", 128), ("ssd_norm_w", 2048), ("pool_scale", 1024), ("norm_mlp_w", 1024),
            ("norm_final_w", 1024), ("conv_w", 4 * XBC), ("loss", 128))
SV_OFF = {}
_o = 0
for _n, _r in SV_PARTS:
    SV_OFF[_n] = _o
    _o += _r
SV_ROWS = 224
assert _o <= SV_ROWS * 128

ADAM_LR, ADAM_B1, ADAM_B2, ADAM_EPS, ADAM_WD, ADAM_STEP = 0.001, 0.9, 0.999, 1e-08, 0.01, 10

VMEM_BIG = 56 * 1024 * 1024
NEG = -1e30

NN = ((1,), (0,))
NT = ((1,), (1,))
TN = ((0,), (0,))


def _dot(a, b, dims=NN):
    return lax.dot_general(a.astype(_MXU_DTYPE), b.astype(_MXU_DTYPE), (dims, ((), ())),
                           preferred_element_type=F32)


def _dot_hi(a, b, dims=NN):
    return lax.dot_general(a.astype(F32), b.astype(F32), (dims, ((), ())),
                           precision=lax.Precision.HIGHEST, preferred_element_type=F32)


def _pick(n, cands):
    for c in cands:
        if n % c == 0:
            return c
    return n


def _sigmoid(x):
    return 1.0 / (1.0 + jnp.exp(-x))


def _silu(x):
    return x * _sigmoid(x)


def _dsilu(x):
    s = _sigmoid(x)
    return s * (1.0 + x * (1.0 - s))


def _softplus(x):
    return jnp.maximum(x, 0.0) + jnp.log(1.0 + jnp.exp(-jnp.abs(x)))


def _params(sem, vmem=None):
    return pltpu.CompilerParams(dimension_semantics=sem, vmem_limit_bytes=vmem)


def _mm(a, b, mode, *, name, outs, tm, tn, tk, extras=(), epilogue=None):
    if mode == "tn":
        K, M = a.shape
        N = b.shape[1]
        a_spec = pl.BlockSpec((tk, tm), lambda i, j, k: (k, i))
        b_spec = pl.BlockSpec((tk, tn), lambda i, j, k: (k, j))
        dims = TN
    else:
        M, K = a.shape
        a_spec = pl.BlockSpec((tm, tk), lambda i, j, k: (i, k))
        if mode == "nn":
            N = b.shape[1]
            b_spec = pl.BlockSpec((tk, tn), lambda i, j, k: (k, j))
            dims = NN
        else:
            N = b.shape[0]
            b_spec = pl.BlockSpec((tn, tk), lambda i, j, k: (j, k))
            dims = NT
    assert M % tm == 0 and N % tn == 0 and K % tk == 0, (name, M, N, K, tm, tn, tk)
    nk = K // tk
    ne, no = len(extras), len(outs)
    if epilogue is None:
        def epilogue(acc, ex, out_refs):
            out_refs[0][...] = acc.astype(out_refs[0].dtype)

    def body(a_ref, b_ref, *rest):
        ex, out_refs = rest[:ne], rest[ne:ne + no]
        p = _dot(a_ref[...], b_ref[...], dims)
        if nk == 1:
            epilogue(p, ex, out_refs)
        else:
            acc = rest[-1]
            k = pl.program_id(2)

            @pl.when(k == 0)
            def _():
                acc[...] = p

            @pl.when(k > 0)
            def _():
                acc[...] += p

            @pl.when(k == nk - 1)
            def _():
                epilogue(acc[...], ex, out_refs)

    res = pl.pallas_call(
        body, name=name,
        grid=(M // tm, N // tn, nk),
        in_specs=[a_spec, b_spec] + [pl.BlockSpec(bs, im) for _, bs, im in extras],
        out_specs=[pl.BlockSpec((tm, tn), lambda i, j, k: (i, j)) for _ in outs],
        out_shape=[jax.ShapeDtypeStruct((M, N), dt) for dt in outs],
        scratch_shapes=[pltpu.VMEM((tm, tn), F32)] if nk > 1 else [],
        compiler_params=_params(("parallel", "parallel", "arbitrary"), VMEM_BIG),
    )(a, b, *[e[0] for e in extras])
    return res if no > 1 else res[0]


def _mm_pool(a, w, scale, *, name, tm, transpose_w):
    L = a.shape[0]
    dims = NT if transpose_w else NN

    def body(a_ref, w_ref, *rest):
        p = _dot(a_ref[...], w_ref[...], dims)
        if transpose_w:
            rest[0][...] = p
        else:
            s_ref, o0, o1 = rest
            o0[...] = p
            o1[...] = (p * s_ref[...]).astype(o1.dtype)

    blk = pl.BlockSpec((tm, PGW), lambda i, j: (i, j))
    in_specs = [blk, pl.BlockSpec((PGW, PGW), lambda i, j: (j, 0))]
    args = [a, w]
    if transpose_w:
        out_specs, out_shape = [blk], [jax.ShapeDtypeStruct((L, POOL_W), F32)]
    else:
        in_specs.append(pl.BlockSpec((1, PGW), lambda i, j: (0, j)))
        args.append(scale)
        out_specs = [blk, blk]
        out_shape = [jax.ShapeDtypeStruct((L, POOL_W), F32), jax.ShapeDtypeStruct((L, POOL_W), BF16)]
    res = pl.pallas_call(body, name=name, grid=(L // tm, 4), in_specs=in_specs, out_specs=out_specs,
                         out_shape=out_shape, compiler_params=_params(("parallel", "parallel")))(*args)
    return res[0] if transpose_w else res


def _mm_pool_tn(a, b, *, name, tk):
    L = a.shape[0]

    def body(a_ref, b_ref, o_ref):
        p = _dot(a_ref[...], b_ref[...], TN)

        @pl.when(pl.program_id(1) == 0)
        def _():
            o_ref[...] = p

        @pl.when(pl.program_id(1) > 0)
        def _():
            o_ref[...] += p

    blk = pl.BlockSpec((tk, PGW), lambda g, k: (k, g))
    return pl.pallas_call(body, name=name, grid=(4, L // tk), in_specs=[blk, blk],
                          out_specs=pl.BlockSpec((PGW, PGW), lambda g, k: (g, 0)),
                          out_shape=jax.ShapeDtypeStruct((POOL_W, PGW), F32),
                          compiler_params=_params(("parallel", "arbitrary")))(a, b)


def _row(tl, w, col=0):
    return pl.BlockSpec((tl, w), lambda i, c=col: (i, c))


def _vec(w, col=0):
    return pl.BlockSpec((1, w), lambda i, c=col: (0, c))


def _acc_out(ref, val, i):
    @pl.when(i == 0)
    def _():
        ref[...] = val

    @pl.when(i > 0)
    def _():
        ref[...] += val


def _colsum(v):
    return jnp.sum(v, axis=0, keepdims=True)


def _norm_fwd(x, nw, scale, shift, *, name):
    L = x.shape[0]
    tl = _pick(L, (512, 256, 128))

    def body(x_ref, nw_ref, sc_ref, sh_ref, h_ref):
        xv = x_ref[...]
        r = lax.rsqrt(jnp.mean(xv * xv, axis=-1, keepdims=True) + EPS)
        h_ref[...] = (xv * r * nw_ref[...] * (1.0 + sc_ref[...]) + sh_ref[...]).astype(h_ref.dtype)

    return pl.pallas_call(body, name=name, grid=(L // tl,),
                          in_specs=[_row(tl, D), _vec(D), _vec(D), _vec(D)], out_specs=_row(tl, D),
                          out_shape=jax.ShapeDtypeStruct((L, D), BF16),
                          compiler_params=_params(("parallel",)))(x, nw, scale, shift)


def _resid_norm_fwd(x, mix, gate, nw, scale, shift, *, name):
    L = x.shape[0]
    tl = _pick(L, (512, 256, 128))

    def body(x_ref, m_ref, g_ref, nw_ref, sc_ref, sh_ref, x1_ref, h_ref):
        xv = x_ref[...] + g_ref[...] * m_ref[...]
        x1_ref[...] = xv
        r = lax.rsqrt(jnp.mean(xv * xv, axis=-1, keepdims=True) + EPS)
        h_ref[...] = (xv * r * nw_ref[...] * (1.0 + sc_ref[...]) + sh_ref[...]).astype(h_ref.dtype)

    return pl.pallas_call(body, name=name, grid=(L // tl,),
                          in_specs=[_row(tl, D), _row(tl, D), _vec(D), _vec(D), _vec(D), _vec(D)],
                          out_specs=[_row(tl, D), _row(tl, D)],
                          out_shape=[jax.ShapeDtypeStruct((L, D), F32), jax.ShapeDtypeStruct((L, D), BF16)],
                          compiler_params=_params(("parallel",)))(x, mix, gate, nw, scale, shift)


def _final_bwd(x1, down, gate_f, nwf, tgt, *, name):
    L = x1.shape[0]
    tl = _pick(L, (512, 256, 128))

    def body(x1_ref, dn_ref, g_ref, nw_ref, t_ref, dx2_ref, dd_ref, loss_ref, dnw_ref, dg_ref):
        i = pl.program_id(0)
        dn = dn_ref[...]
        x2 = x1_ref[...] + g_ref[...] * dn
        r = lax.rsqrt(jnp.mean(x2 * x2, axis=-1, keepdims=True) + EPS)
        xh = x2 * r
        e = xh * nw_ref[...] - t_ref[...]
        part = 0.5 * jnp.sum(jnp.mean(e * e, axis=-1, keepdims=True), axis=0, keepdims=True)
        dy = e * (1.0 / D)
        g = dy * nw_ref[...]
        dx2 = r * (g - xh * jnp.mean(g * xh, axis=-1, keepdims=True))
        dx2_ref[...] = dx2
        dd_ref[...] = (dx2 * g_ref[...]).astype(dd_ref.dtype)
        _acc_out(loss_ref, jnp.broadcast_to(part, (1, 128)), i)
        _acc_out(dnw_ref, _colsum(dy * xh), i)
        _acc_out(dg_ref, _colsum(dx2 * dn), i)

    return pl.pallas_call(
        body, name=name, grid=(L // tl,),
        in_specs=[_row(tl, D), _row(tl, D), _vec(D), _vec(D), _row(tl, D)],
        out_specs=[_row(tl, D), _row(tl, D), _vec(128), _vec(D), _vec(D)],
        out_shape=[jax.ShapeDtypeStruct((L, D), F32), jax.ShapeDtypeStruct((L, D), BF16),
                   jax.ShapeDtypeStruct((1, 128), F32), jax.ShapeDtypeStruct((1, D), F32),
                   jax.ShapeDtypeStruct((1, D), F32)],
        compiler_params=_params(("arbitrary",)))(x1, down, gate_f, nwf, tgt)


def _norm_bwd(xin, dh, dres, nw, scale, mix=None, gate=None, *, name):
    L = xin.shape[0]
    tl = _pick(L, (512, 256, 128))
    with_mix = mix is not None

    def body(*refs):
        if with_mix:
            x_ref, dh_ref, dr_ref, nw_ref, sc_ref, m_ref, g_ref, dx_ref, p_ref, q_ref, dm_ref, dg_ref = refs
        else:
            x_ref, dh_ref, dr_ref, nw_ref, sc_ref, dx_ref, p_ref, q_ref = refs
        i = pl.program_id(0)
        xv = x_ref[...]
        dh_v = dh_ref[...]
        r = lax.rsqrt(jnp.mean(xv * xv, axis=-1, keepdims=True) + EPS)
        xh = xv * r
        g = dh_v * (nw_ref[...] * (1.0 + sc_ref[...]))
        dx = dr_ref[...] + r * (g - xh * jnp.mean(g * xh, axis=-1, keepdims=True))
        dx_ref[...] = dx
        _acc_out(p_ref, _colsum(dh_v * xh), i)
        _acc_out(q_ref, _colsum(dh_v), i)
        if with_mix:
            dm_ref[...] = (dx * g_ref[...]).astype(dm_ref.dtype)
            _acc_out(dg_ref, _colsum(dx * m_ref[...]), i)

    in_specs = [_row(tl, D), _row(tl, D), _row(tl, D), _vec(D), _vec(D)]
    out_specs = [_row(tl, D), _vec(D), _vec(D)]
    out_shape = [jax.ShapeDtypeStruct((L, D), F32), jax.ShapeDtypeStruct((1, D), F32),
                 jax.ShapeDtypeStruct((1, D), F32)]
    args = [xin, dh, dres, nw, scale]
    if with_mix:
        in_specs += [_row(tl, D), _vec(D)]
        out_specs += [_row(tl, D), _vec(D)]
        out_shape += [jax.ShapeDtypeStruct((L, D), BF16), jax.ShapeDtypeStruct((1, D), F32)]
        args += [mix, gate]
    return pl.pallas_call(body, name=name, grid=(L // tl,), in_specs=in_specs, out_specs=out_specs,
                          out_shape=out_shape, compiler_params=_params(("arbitrary",)))(*args)


def _merge_fwd(y_ssd, y_pool, proj, *, name):
    L = y_ssd.shape[0]
    tl = _pick(L, (512, 256, 128))

    def body(a_ref, b_ref, gl_ref, m_ref):
        s = _sigmoid(gl_ref[...])
        m_ref[...] = (s[:, :D] * a_ref[...] + s[:, D:] * b_ref[...]).astype(m_ref.dtype)

    return pl.pallas_call(body, name=name, grid=(L // tl,),
                          in_specs=[_row(tl, D), _row(tl, D), _row(tl, 2 * D, C_GATE // (2 * D))],
                          out_specs=_row(tl, D), out_shape=jax.ShapeDtypeStruct((L, D), BF16),
                          compiler_params=_params(("parallel",)))(y_ssd, y_pool, proj)


def _merge_bwd(dm, y_ssd, y_pool, proj, *, name):
    L = dm.shape[0]
    tl = _pick(L, (512, 256, 128))

    def body(dm_ref, a_ref, b_ref, gl_ref, da_ref, db_ref, dgl_ref):
        s = _sigmoid(gl_ref[...])
        dmv = dm_ref[...]
        s1, s2 = s[:, :D], s[:, D:]
        da_ref[...] = (dmv * s1).astype(da_ref.dtype)
        db_ref[...] = (dmv * s2).astype(db_ref.dtype)
        dgl_ref[:, :D] = (dmv * a_ref[...] * s1 * (1.0 - s1)).astype(dgl_ref.dtype)
        dgl_ref[:, D:] = (dmv * b_ref[...] * s2 * (1.0 - s2)).astype(dgl_ref.dtype)

    gcol = C_GATE // (2 * D)
    return pl.pallas_call(
        body, name=name, grid=(L // tl,),
        in_specs=[_row(tl, D), _row(tl, D), _row(tl, D), _row(tl, 2 * D, gcol)],
        out_specs=[_row(tl, D), _row(tl, D), _row(tl, 2 * D, gcol)],
        out_shape=[jax.ShapeDtypeStruct((L, D), BF16), jax.ShapeDtypeStruct((L, D), BF16),
                   jax.ShapeDtypeStruct((L, NPROJ), BF16)],
        compiler_params=_params(("parallel",)))(dm, y_ssd, y_pool, proj)


GW = DI // NG


def _gated_norm_fwd(y, proj, w, *, name):
    L = y.shape[0]
    tl = _pick(L, (256, 128))

    def body(y_ref, z_ref, w_ref, o_ref):
        yg = y_ref[...] * _silu(z_ref[...])
        for k in range(NG):
            seg = yg[:, k * GW:(k + 1) * GW]
            r = lax.rsqrt(jnp.mean(seg * seg, axis=-1, keepdims=True) + EPS)
            o_ref[:, k * GW:(k + 1) * GW] = (seg * r * w_ref[:, k * GW:(k + 1) * GW]).astype(o_ref.dtype)

    return pl.pallas_call(body, name=name, grid=(L // tl,),
                          in_specs=[_row(tl, DI), _row(tl, DI, C_Z // DI), _vec(DI)],
                          out_specs=_row(tl, DI), out_shape=jax.ShapeDtypeStruct((L, DI), BF16),
                          compiler_params=_params(("parallel",)))(y, proj, w)


def _gated_norm_bwd(dyn, y, proj, w, dproj, *, name):
    L = y.shape[0]
    tl = _pick(L, (256, 128))

    def body(dyn_ref, y_ref, z_ref, w_ref, dp_in, dy_ref, dz_ref, dw_ref):
        del dp_in
        i = pl.program_id(0)
        zv = z_ref[...]
        yv = y_ref[...]
        sz = _silu(zv)
        yg = yv * sz
        dsz = _dsilu(zv)
        dws = []
        for k in range(NG):
            sl = slice(k * GW, (k + 1) * GW)
            seg = yg[:, sl]
            r = lax.rsqrt(jnp.mean(seg * seg, axis=-1, keepdims=True) + EPS)
            sh = seg * r
            dn = dyn_ref[:, sl]
            g = dn * w_ref[:, sl]
            dyg = r * (g - sh * jnp.mean(g * sh, axis=-1, keepdims=True))
            dy_ref[:, sl] = dyg * sz[:, sl]
            dz_ref[:, sl] = (dyg * yv[:, sl] * dsz[:, sl]).astype(dz_ref.dtype)
            dws.append(_colsum(dn * sh))
        _acc_out(dw_ref, jnp.concatenate(dws, axis=1), i)

    zc = C_Z // DI
    res = pl.pallas_call(
        body, name=name, grid=(L // tl,),
        in_specs=[_row(tl, DI), _row(tl, DI), _row(tl, DI, zc), _vec(DI), pl.BlockSpec(memory_space=pl.ANY)],
        out_specs=[_row(tl, DI), _row(tl, DI, zc), _vec(DI)],
        out_shape=[jax.ShapeDtypeStruct((L, DI), F32), jax.ShapeDtypeStruct((L, NPROJ), BF16),
                   jax.ShapeDtypeStruct((1, DI), F32)],
        input_output_aliases={4: 1},
        compiler_params=_params(("arbitrary",)))(dyn, y, proj, w, dproj)
    return res


def _pscale_bwd(dyp1, yp0, scale, *, name):
    L = dyp1.shape[0]
    tl = _pick(L, (512, 256, 128))

    def body(d_ref, y_ref, s_ref, o_ref, ds_ref):
        dv = d_ref[...]
        o_ref[...] = (dv * s_ref[...]).astype(o_ref.dtype)
        _acc_out(ds_ref, _colsum(dv * y_ref[...]), pl.program_id(0))

    return pl.pallas_call(body, name=name, grid=(L // tl,),
                          in_specs=[_row(tl, POOL_W), _row(tl, POOL_W), _vec(POOL_W)],
                          out_specs=[_row(tl, POOL_W), _vec(POOL_W)],
                          out_shape=[jax.ShapeDtypeStruct((L, POOL_W), BF16),
                                     jax.ShapeDtypeStruct((1, POOL_W), F32)],
                          compiler_params=_params(("arbitrary",)))(dyp1, yp0, scale)


CONV_CB = 128
HALO = 16


def _time_chunk(L):
    return _pick(L, (256, 128))


def _conv_fwd(proj, w, b, *, name):
    L = proj.shape[0]
    rc = _time_chunk(L)
    n = L // rc

    def body(x_ref, w_ref, b_ref, o_ref, pad):
        pad[0:HALO, :] = jnp.zeros((HALO, CONV_CB), F32)
        wv = w_ref[...]
        bv = b_ref[...]

        def fill(i, c):
            r0 = pl.multiple_of(i * rc, rc)
            pad[pl.ds(r0 + HALO, rc), :] = x_ref[pl.ds(r0, rc), :]
            return c

        lax.fori_loop(0, n, fill, 0)

        def step(i, c):
            r0 = pl.multiple_of(i * rc, rc)
            ext = pad[pl.ds(r0, rc + HALO), :]
            acc = bv + ext * wv[3:4]
            for j in (1, 2, 3):
                acc = acc + pltpu.roll(ext, j, 0) * wv[3 - j:4 - j]
            acc = acc[HALO:]
            o_ref[pl.ds(r0, rc), :] = acc * _sigmoid(acc)
            return c

        lax.fori_loop(0, n, step, 0)

    return pl.pallas_call(
        body, name=name, grid=(XBC // CONV_CB,),
        in_specs=[pl.BlockSpec((L, CONV_CB), lambda j: (0, j + C_XBC // CONV_CB)),
                  pl.BlockSpec((4, CONV_CB), lambda j: (0, j)), pl.BlockSpec((1, CONV_CB), lambda j: (0, j))],
        out_specs=pl.BlockSpec((L, CONV_CB), lambda j: (0, j)),
        out_shape=jax.ShapeDtypeStruct((L, XBC), F32),
        scratch_shapes=[pltpu.VMEM((L + HALO, CONV_CB), F32)],
        compiler_params=_params(("parallel",), VMEM_BIG))(proj, w, b)


def _conv_bwd(proj, dy, w, b, dproj, *, name):
    L = proj.shape[0]
    rc = _time_chunk(L)
    n = L // rc

    def body(x_ref, dy_ref, w_ref, b_ref, dp_in, dx_ref, dw_ref, db_ref, pad, dpad):
        del dp_in
        pad[0:HALO, :] = jnp.zeros((HALO, CONV_CB), F32)
        dpad[L:L + HALO, :] = jnp.zeros((HALO, CONV_CB), F32)
        wv = w_ref[...]
        bv = b_ref[...]

        def fill(i, c):
            r0 = pl.multiple_of(i * rc, rc)
            pad[pl.ds(r0 + HALO, rc), :] = x_ref[pl.ds(r0, rc), :]
            return c

        lax.fori_loop(0, n, fill, 0)

        def p1(i, carry):
            r0 = pl.multiple_of(i * rc, rc)
            ext = pad[pl.ds(r0, rc + HALO), :]
            xk = [ext[HALO:]] + [pltpu.roll(ext, j, 0)[HALO:] for j in (1, 2, 3)]
            pre = bv
            for j in range(4):
                pre = pre + xk[j] * wv[3 - j:4 - j]
            dpre = dy_ref[pl.ds(r0, rc), :] * _dsilu(pre)
            dpad[pl.ds(r0, rc), :] = dpre
            db, d0, d1, d2, d3 = carry
            return (db + _colsum(dpre), d0 + _colsum(dpre * xk[3]), d1 + _colsum(dpre * xk[2]),
                    d2 + _colsum(dpre * xk[1]), d3 + _colsum(dpre * xk[0]))

        z = jnp.zeros((1, CONV_CB), F32)
        db, d0, d1, d2, d3 = lax.fori_loop(0, n, p1, (z, z, z, z, z))
        db_ref[...] = db
        dw_ref[...] = jnp.concatenate([d0, d1, d2, d3], axis=0)

        def p2(i, c):
            r0 = pl.multiple_of(i * rc, rc)
            ext = dpad[pl.ds(r0, rc + HALO), :]
            acc = ext * wv[3:4]
            for j in (1, 2, 3):
                acc = acc + pltpu.roll(ext, rc + HALO - j, 0) * wv[3 - j:4 - j]
            dx_ref[pl.ds(r0, rc), :] = acc[:rc].astype(dx_ref.dtype)
            return c

        lax.fori_loop(0, n, p2, 0)

    nb = XBC // CONV_CB
    return pl.pallas_call(
        body, name=name, grid=(nb,),
        in_specs=[pl.BlockSpec((L, CONV_CB), lambda j: (0, j + C_XBC // CONV_CB)),
                  pl.BlockSpec((L, CONV_CB), lambda j: (0, j)),
                  pl.BlockSpec((4, CONV_CB), lambda j: (0, j)), pl.BlockSpec((1, CONV_CB), lambda j: (0, j)),
                  pl.BlockSpec(memory_space=pl.ANY)],
        out_specs=[pl.BlockSpec((L, CONV_CB), lambda j: (0, j + C_XBC // CONV_CB)),
                   pl.BlockSpec((4, CONV_CB), lambda j: (0, j)), pl.BlockSpec((1, CONV_CB), lambda j: (0, j))],
        out_shape=[jax.ShapeDtypeStruct((L, NPROJ), BF16), jax.ShapeDtypeStruct((4, XBC), F32),
                   jax.ShapeDtypeStruct((1, XBC), F32)],
        scratch_shapes=[pltpu.VMEM((L + HALO, CONV_CB), F32), pltpu.VMEM((L + HALO, CONV_CB), F32)],
        input_output_aliases={4: 0},
        compiler_params=_params(("parallel",), VMEM_BIG))(proj, dy, w, b, dproj)


def _pool_fwd(proj, *, name):
    L = proj.shape[0]
    rc = _time_chunk(L)
    n = L // rc

    def body(x_ref, o_ref, pad):
        g = pl.program_id(0)
        pad[0:HALO, :] = jnp.zeros((HALO, PGW), F32)

        def fill(i, c):
            r0 = pl.multiple_of(i * rc, rc)
            pad[pl.ds(r0 + HALO, rc), :] = x_ref[pl.ds(r0, rc), :]
            return c

        lax.fori_loop(0, n, fill, 0)
        rows = lax.broadcasted_iota(jnp.int32, (rc, PGW), 0)

        for gi in range(4):
            win = 2 << gi

            @pl.when(g == gi)
            def _(gi=gi, win=win):
                def step(i, c):
                    r0 = pl.multiple_of(i * rc, rc)
                    ext = pad[pl.ds(r0, rc + HALO), :]
                    s = ext
                    sh = 1
                    while sh < win:
                        s = s + pltpu.roll(s, sh, 0)
                        sh *= 2
                    cnt = jnp.minimum(rows + (r0 + 1), win).astype(F32)
                    o_ref[pl.ds(r0, rc), :] = (s[HALO:] / cnt - ext[HALO:]).astype(o_ref.dtype)
                    return c

                lax.fori_loop(0, n, step, 0)

    return pl.pallas_call(
        body, name=name, grid=(4,),
        in_specs=[pl.BlockSpec((L, PGW), lambda j: (0, j + C_POOL // PGW))],
        out_specs=pl.BlockSpec((L, PGW), lambda j: (0, j)),
        out_shape=jax.ShapeDtypeStruct((L, POOL_W), BF16),
        scratch_shapes=[pltpu.VMEM((L + HALO, PGW), F32)],
        compiler_params=_params(("parallel",), VMEM_BIG))(proj)


def _pool_bwd(dpooled, dproj, *, name):
    L = dpooled.shape[0]
    rc = _time_chunk(L)
    n = L // rc

    def body(d_ref, dp_in, o_ref, pad):
        del dp_in
        g = pl.program_id(0)
        pad[L:L + HALO, :] = jnp.zeros((HALO, PGW), F32)
        rows = lax.broadcasted_iota(jnp.int32, (rc, PGW), 0)

        for gi in range(4):
            win = 2 << gi

            @pl.when(g == gi)
            def _(gi=gi, win=win):
                def fill(i, c):
                    r0 = pl.multiple_of(i * rc, rc)
                    cnt = jnp.minimum(rows + (r0 + 1), win).astype(F32)
                    pad[pl.ds(r0, rc), :] = d_ref[pl.ds(r0, rc), :] / cnt
                    return c

                lax.fori_loop(0, n, fill, 0)

                def step(i, c):
                    r0 = pl.multiple_of(i * rc, rc)
                    s = pad[pl.ds(r0, rc + HALO), :]
                    sh = 1
                    while sh < win:
                        s = s + pltpu.roll(s, rc + HALO - sh, 0)
                        sh *= 2
                    o_ref[pl.ds(r0, rc), :] = (s[:rc] - d_ref[pl.ds(r0, rc), :]).astype(o_ref.dtype)
                    return c

                lax.fori_loop(0, n, step, 0)

    return pl.pallas_call(
        body, name=name, grid=(4,),
        in_specs=[pl.BlockSpec((L, PGW), lambda j: (0, j)), pl.BlockSpec(memory_space=pl.ANY)],
        out_specs=pl.BlockSpec((L, PGW), lambda j: (0, j + C_POOL // PGW)),
        out_shape=jax.ShapeDtypeStruct((L, NPROJ), BF16),
        scratch_shapes=[pltpu.VMEM((L + HALO, PGW), F32)],
        input_output_aliases={1: 0},
        compiler_params=_params(("parallel",), VMEM_BIG))(dpooled, dproj)


def _ssd_consts():
    tri = np.tril(np.ones((Q, Q), np.float32))
    exp = np.zeros((128, DI), np.float32)
    for h in range(NH):
        exp[h, h * HP:(h + 1) * HP] = 1.0
    return jnp.asarray(tri), jnp.asarray(tri.T.copy()), jnp.asarray(exp)


def _ssd_common(xbc_ref, dtw_ref, dtb_ref, arow_ref, t_ref, e_ref):
    pre = dtw_ref[:, :128] + dtb_ref[...]
    dt = _softplus(pre)
    acs = _dot_hi(t_ref[...], dt * arow_ref[...])
    acs_x = _dot_hi(acs, e_ref[...])
    dt_x = _dot_hi(dt, e_ref[...])
    xs = xbc_ref[:, 0:DI]
    return pre, dt, acs, acs.T, acs_x, dt_x, xs


def _ssd_fwd(xbc, proj, dtb, arow, dsk_x, *, name):
    L = xbc.shape[0]
    nc = L // Q
    tri, _, expand = _ssd_consts()

    def body(xbc_ref, dtw_ref, dtb_ref, arow_ref, dsk_ref, t_ref, e_ref, y_ref, hs_ref, h_scr):
        @pl.when(pl.program_id(0) == 0)
        def _():
            h_scr[...] = jnp.zeros_like(h_scr)

        _, dt, acs, acs_t, acs_x, dt_x, xs = _ssd_common(xbc_ref, dtw_ref, dtb_ref, arow_ref, t_ref, e_ref)
        xdt = xs * dt_x
        eacs = jnp.exp(acs_x)
        acs_last = acs_x[Q - 1:Q, :]
        dec = jnp.exp(acs_last - acs_x)
        hs_ref[0] = h_scr[...]
        causal = lax.broadcasted_iota(jnp.int32, (Q, Q), 0) >= lax.broadcasted_iota(jnp.int32, (Q, Q), 1)
        first = lax.broadcasted_iota(jnp.int32, (Q, 128), 1) < HP
        for g in range(NG):
            bg = xbc_ref[:, DI + g * NS:DI + (g + 1) * NS]
            cg = xbc_ref[:, DI + NG * NS + g * NS:DI + NG * NS + (g + 1) * NS]
            s = _dot(cg, bg, NT)
            sl = slice(g * GW, (g + 1) * GW)
            hg = h_scr[:, sl]
            yoff = _dot(cg, hg, NN) * eacs[:, sl]
            st = _dot(bg, xdt[:, sl] * dec[:, sl], TN)
            h_scr[:, sl] = hg * eacs[Q - 1:Q, sl] + st
            for j in range(4):
                lo = g * GW + j * 128
                xb = xdt[:, lo:lo + 128]
                yp = yoff[:, j * 128:(j + 1) * 128] + dsk_ref[:, lo:lo + 128] * xs[:, lo:lo + 128]
                for e in range(2):
                    h = g * 8 + j * 2 + e
                    lm = jnp.exp(jnp.where(causal, acs[:, h:h + 1] - acs_t[h:h + 1, :], NEG))
                    xm = jnp.where(first if e == 0 else jnp.logical_not(first), xb, 0.0)
                    yp = yp + _dot(s * lm, xm, NN)
                y_ref[:, lo:lo + 128] = yp

    return pl.pallas_call(
        body, name=name, grid=(nc,),
        in_specs=[pl.BlockSpec((Q, XBC), lambda c: (c, 0)),
                  pl.BlockSpec((Q, DT_PAD), lambda c: (c, C_DT // DT_PAD)),
                  pl.BlockSpec((1, 128), lambda c: (0, 0)), pl.BlockSpec((1, 128), lambda c: (0, 0)),
                  pl.BlockSpec((1, DI), lambda c: (0, 0)),
                  pl.BlockSpec((Q, Q), lambda c: (0, 0)), pl.BlockSpec((128, DI), lambda c: (0, 0))],
        out_specs=[pl.BlockSpec((Q, DI), lambda c: (c, 0)), pl.BlockSpec((1, NS, DI), lambda c: (c, 0, 0))],
        out_shape=[jax.ShapeDtypeStruct((L, DI), F32), jax.ShapeDtypeStruct((nc, NS, DI), F32)],
        scratch_shapes=[pltpu.VMEM((NS, DI), F32)],
        compiler_params=_params(("arbitrary",), VMEM_BIG))(xbc, proj, dtb, arow, dsk_x, tri, expand)


def _ssd_bwd(dy, xbc, proj, hs, dtb, arow, dsk_x, dproj, *, name):
    L = xbc.shape[0]
    nc = L // Q
    tri, triu, expand = _ssd_consts()

    def body(dy_ref, xbc_ref, dtw_ref, hs_ref, dtb_ref, arow_ref, dsk_ref, t_ref, u_ref, e_ref, dp_in,
             dxbc_ref, ddtw_ref, da_ref, ddx_ref, ddtb_ref, dh_scr):
        del dp_in
        i = pl.program_id(0)

        @pl.when(i == 0)
        def _():
            dh_scr[...] = jnp.zeros_like(dh_scr)

        pre, dt, acs, acs_t, acs_x, dt_x, xs = _ssd_common(xbc_ref, dtw_ref, dtb_ref, arow_ref, t_ref, e_ref)
        dyv = dy_ref[...]
        xdt = xs * dt_x
        eacs = jnp.exp(acs_x)
        acs_last = acs_x[Q - 1:Q, :]
        dec = jnp.exp(acs_last - acs_x)
        gy = dyv * eacs
        causal = lax.broadcasted_iota(jnp.int32, (Q, Q), 0) >= lax.broadcasted_iota(jnp.int32, (Q, Q), 1)
        first = lax.broadcasted_iota(jnp.int32, (Q, 128), 1) < HP
        lane_h = lax.broadcasted_iota(jnp.int32, (Q, 128), 1)
        sub_h = lax.broadcasted_iota(jnp.int32, (128, Q), 0)
        last_row = lax.broadcasted_iota(jnp.int32, (Q, GW), 0) == Q - 1
        dacs = jnp.zeros((Q, 128), F32)
        dacs_t = jnp.zeros((128, Q), F32)
        ddt = jnp.zeros((Q, 128), F32)
        for g in range(NG):
            bg = xbc_ref[:, DI + g * NS:DI + (g + 1) * NS]
            cg = xbc_ref[:, DI + NG * NS + g * NS:DI + NG * NS + (g + 1) * NS]
            s = _dot(cg, bg, NT)
            sl = slice(g * GW, (g + 1) * GW)
            hg = hs_ref[0, :, sl]
            dhn = dh_scr[:, sl]
            eal = eacs[Q - 1:Q, sl]
            gg = gy[:, sl]
            dax = gg * _dot(cg, hg, NN)
            dcg = _dot(gg, hg, NT)
            dh_scr[:, sl] = _dot(cg, gg, TN) + dhn * eal
            dal = eal * _colsum(dhn * hg)
            xdd = xdt[:, sl] * dec[:, sl]
            dbg = _dot(xdd, dhn, NT)
            wv = _dot(bg, dhn, NN)
            dd = wv * xdd
            dax = dax - dd
            dal = dal + _colsum(dd)
            dax = dax + jnp.where(last_row, dal, 0.0)
            dxdt_g = wv * dec[:, sl]
            ds = jnp.zeros((Q, Q), F32)
            dxdt_blocks = []
            for j in range(4):
                lo = g * GW + j * 128
                xb = xdt[:, lo:lo + 128]
                dyb = dyv[:, lo:lo + 128]
                dxb = dxdt_g[:, j * 128:(j + 1) * 128]
                for e in range(2):
                    h = g * 8 + j * 2 + e
                    lm = jnp.exp(jnp.where(causal, acs[:, h:h + 1] - acs_t[h:h + 1, :], NEG))
                    m = s * lm
                    dym = jnp.where(first if e == 0 else jnp.logical_not(first), dyb, 0.0)
                    dm = _dot(dym, xb, NT)
                    r = dm * m
                    dacs = dacs + jnp.where(lane_h == h, jnp.sum(r, axis=1, keepdims=True), 0.0)
                    dacs_t = dacs_t + jnp.where(sub_h == h, _colsum(r), 0.0)
                    ds = ds + dm * lm
                    dxb = dxb + _dot(m, dym, TN)
                dxdt_blocks.append(dxb)
            dxdt = jnp.concatenate(dxdt_blocks, axis=1)
            dcg = dcg + _dot(ds, bg, NN)
            dbg = dbg + _dot(ds, cg, TN)
            dxbc_ref[:, DI + g * NS:DI + (g + 1) * NS] = dbg
            dxbc_ref[:, DI + NG * NS + g * NS:DI + NG * NS + (g + 1) * NS] = dcg
            dxbc_ref[:, sl] = dsk_ref[:, sl] * dyv[:, sl] + dxdt * dt_x[:, sl]
            eg = e_ref[:, sl]
            ddt = ddt + _dot_hi(dxdt * xs[:, sl], eg, NT)
            dacs = dacs + _dot_hi(dax, eg, NT)
        dacs = dacs - dacs_t.T
        ddta = _dot_hi(u_ref[...], dacs)
        ddt = ddt + ddta * arow_ref[...]
        ddtw = jnp.where(lane_h < NH, ddt * _sigmoid(pre), 0.0)
        ddtw_ref[...] = jnp.concatenate([ddtw, jnp.zeros((Q, DT_PAD - 128), F32)], axis=1).astype(ddtw_ref.dtype)
        _acc_out(da_ref, _colsum(ddta * dt), i)
        _acc_out(ddx_ref, _colsum(dyv * xs), i)
        _acc_out(ddtb_ref, _colsum(ddtw), i)

    rev = lambda c: (nc - 1 - c, 0)
    const = lambda c: (0, 0)
    return pl.pallas_call(
        body, name=name, grid=(nc,),
        in_specs=[pl.BlockSpec((Q, DI), rev), pl.BlockSpec((Q, XBC), rev),
                  pl.BlockSpec((Q, DT_PAD), lambda c: (nc - 1 - c, C_DT // DT_PAD)),
                  pl.BlockSpec((1, NS, DI), lambda c: (nc - 1 - c, 0, 0)),
                  pl.BlockSpec((1, 128), const), pl.BlockSpec((1, 128), const), pl.BlockSpec((1, DI), const),
                  pl.BlockSpec((Q, Q), const), pl.BlockSpec((Q, Q), const), pl.BlockSpec((128, DI), const),
                  pl.BlockSpec(memory_space=pl.ANY)],
        out_specs=[pl.BlockSpec((Q, XBC), rev),
                   pl.BlockSpec((Q, DT_PAD), lambda c: (nc - 1 - c, C_DT // DT_PAD)),
                   pl.BlockSpec((1, 128), const), pl.BlockSpec((1, DI), const), pl.BlockSpec((1, 128), const)],
        out_shape=[jax.ShapeDtypeStruct((L, XBC), F32), jax.ShapeDtypeStruct((L, NPROJ), BF16),
                   jax.ShapeDtypeStruct((1, 128), F32), jax.ShapeDtypeStruct((1, DI), F32),
                   jax.ShapeDtypeStruct((1, 128), F32)],
        scratch_shapes=[pltpu.VMEM((NS, DI), F32)],
        input_output_aliases={10: 1},
        compiler_params=_params(("arbitrary",), VMEM_BIG))(dy, xbc, proj, hs, dtb, arow, dsk_x, tri, triu,
                                                          expand, dproj)


def _adamw(w, g, m, v, *, name):
    R, C = w.shape
    tr = _pick(R, (256, 128, 64, 32, 16, 8))
    c1 = 1.0 - ADAM_B1 ** ADAM_STEP
    c2 = 1.0 - ADAM_B2 ** ADAM_STEP

    def body(w_ref, g_ref, m_ref, v_ref, d_ref, mo_ref, vo_ref):
        gv = g_ref[...]
        mn = ADAM_B1 * m_ref[...] + (1.0 - ADAM_B1) * gv
        vn = ADAM_B2 * v_ref[...] + (1.0 - ADAM_B2) * (gv * gv)
        mo_ref[...] = mn
        vo_ref[...] = vn
        d_ref[...] = -ADAM_LR * ((mn / c1) / (jnp.sqrt(vn / c2) + ADAM_EPS) + ADAM_WD * w_ref[...])

    spec = pl.BlockSpec((tr, C), lambda i: (i, 0))
    return pl.pallas_call(body, name=name, grid=(R // tr,), in_specs=[spec] * 4, out_specs=[spec] * 3,
                          out_shape=[jax.ShapeDtypeStruct((R, C), F32)] * 3,
                          compiler_params=_params(("parallel",)))(w, g, m, v)


def _slab_sum(recv, *, name):
    def body(r_ref, o_ref):
        acc = r_ref[0].astype(F32)
        for j in range(1, N_DEV):
            acc = acc + r_ref[j].astype(F32)
        o_ref[...] = acc

    return pl.pallas_call(body, name=name, grid=(SLAB_ROWS // SLAB_TILE,),
                          in_specs=[pl.BlockSpec((N_DEV, SLAB_TILE, D), lambda i: (0, i, 0))],
                          out_specs=pl.BlockSpec((SLAB_TILE, D), lambda i: (i, 0)),
                          out_shape=jax.ShapeDtypeStruct((SLAB_ROWS, D), F32),
                          compiler_params=_params(("parallel",)))(recv)


MESH = pl.DeviceIdType.MESH


def _coords():
    return lax.axis_index("x"), lax.axis_index("y"), lax.axis_index("c")


def _peer(k):
    x, y, c = _coords()
    px = 1 - x if k & 4 else x
    py = 1 - y if k & 2 else y
    pc = 1 - c if k & 1 else c
    return (px, py, pc), 4 * px + 2 * py + pc


def _rcopy(src, dst, ssem, rsem, dev):
    return pltpu.make_async_remote_copy(src_ref=src, dst_ref=dst, send_sem=ssem, recv_sem=rsem,
                                        device_id=dev, device_id_type=MESH)


def _exchange_all(src_of, dst_slot, send_sems, recv_sems):
    x, y, c = _coords()
    me = 4 * x + 2 * y + c
    sent = []
    for k in range(1, N_DEV):
        dev, pidx = _peer(k)
        cp = _rcopy(src_of(pidx), dst_slot(me), send_sems.at[k - 1], recv_sems.at[k - 1], dev)
        cp.start()
        sent.append(cp)
    for k in range(1, N_DEV):
        dev, pidx = _peer(k)
        _rcopy(src_of(pidx), dst_slot(pidx), send_sems.at[k - 1], recv_sems.at[k - 1], dev).wait_recv()
    for cp in sent:
        cp.wait_send()


def _rows_of_slots(buf, nslots):
    rows = lax.broadcasted_iota(jnp.int32, (8, buf.shape[-1]), 0)
    out = jnp.zeros((8, buf.shape[-1]), F32)
    for j in range(nslots):
        out = out + jnp.where(rows == j, buf[j], 0.0)
    return out


def _ada_fwd(c, w_ada, b_r, *, name):
    wloc = w_ada.shape[1]

    def body(c_ref, w_ref, b_ref, mod_ref, call_ref, csrc, cbuf, psrc, pbuf, s1, r1, s2, r2):
        x, y, cc = _coords()
        me = 4 * x + 2 * y + cc
        csrc[...] = jnp.broadcast_to(c_ref[...], (8, D))
        cbuf[me] = csrc[...]
        _exchange_all(lambda p: csrc, lambda s: cbuf.at[s], s1, r1)
        call = _rows_of_slots(cbuf, N_DEV)
        call_ref[...] = call
        prod = _dot_hi(_silu(call), w_ref[...])
        for b in range(N_DEV):
            psrc[b] = jnp.broadcast_to(prod[b:b + 1, :], (8, wloc))
        pbuf[me] = psrc[me]
        _exchange_all(lambda p: psrc.at[p], lambda s: pbuf.at[s], s2, r2)
        mod_ref[...] = _rows_of_slots(pbuf, N_DEV) + b_ref[...]

    vm = pl.BlockSpec(memory_space=pltpu.VMEM)
    return pl.pallas_call(
        body, name=name, in_specs=[vm, vm, vm], out_specs=[vm, vm],
        out_shape=[jax.ShapeDtypeStruct((N_DEV, wloc), F32), jax.ShapeDtypeStruct((N_DEV, D), F32)],
        scratch_shapes=[pltpu.VMEM((8, D), F32), pltpu.VMEM((N_DEV, 8, D), F32),
                        pltpu.VMEM((N_DEV, 8, wloc), F32), pltpu.VMEM((N_DEV, 8, wloc), F32),
                        pltpu.SemaphoreType.DMA((N_DEV - 1,)), pltpu.SemaphoreType.DMA((N_DEV - 1,)),
                        pltpu.SemaphoreType.DMA((N_DEV - 1,)), pltpu.SemaphoreType.DMA((N_DEV - 1,))],
        compiler_params=pltpu.CompilerParams(vmem_limit_bytes=VMEM_BIG))(c, w_ada, b_r)


def _gather_slabs(slab, *, name):
    def body(x_ref, out_ref, send_sems, recv_sems, local_sem):
        x, y, c = _coords()
        me, sibling = (x, y, c), (x, y, 1 - c)
        chips = [(1 - x, y), (x, 1 - y), (1 - x, 1 - y)]

        def slot(px, py, pc):
            return out_ref.at[4 * px + 2 * py + pc]

        def copy(k, block, to, src=None):
            return _rcopy(slot(*block) if src is None else src, slot(*block), send_sems.at[k], recv_sems.at[k], to)

        mine = pltpu.make_async_copy(x_ref, slot(*me), local_sem)
        mine.start()
        first = [copy(0, me, sibling, src=x_ref)]
        first += [copy(1 + j, me, (*chip, c), src=x_ref) for j, chip in enumerate(chips)]
        for cp in first:
            cp.start()
        passed = [copy(4 + j, (*chip, c), sibling) for j, chip in enumerate(chips)]
        for j, chip in enumerate(chips):
            copy(1 + j, (*chip, c), me).wait_recv()
            passed[j].start()
        copy(0, sibling, me).wait_recv()
        for j, chip in enumerate(chips):
            copy(4 + j, (*chip, 1 - c), me).wait_recv()
        for cp in first + passed:
            cp.wait_send()
        mine.wait()

    anyspec = pl.BlockSpec(memory_space=pl.ANY)
    return pl.pallas_call(
        body, name=name, in_specs=[anyspec], out_specs=anyspec,
        out_shape=jax.ShapeDtypeStruct((N_DEV,) + slab.shape, slab.dtype),
        scratch_shapes=[pltpu.SemaphoreType.DMA((7,)), pltpu.SemaphoreType.DMA((7,)), pltpu.SemaphoreType.DMA],
    )(slab)


def _scatter_slabs(g, *, name):
    def body(g_ref, out_ref, send_sems, recv_sems, local_sem):
        x, y, c = _coords()
        me = 4 * x + 2 * y + c
        mine = pltpu.make_async_copy(g_ref.at[me], out_ref.at[me], local_sem)
        mine.start()
        _exchange_all(lambda p: g_ref.at[p], lambda s: out_ref.at[s], send_sems, recv_sems)
        mine.wait()

    anyspec = pl.BlockSpec(memory_space=pl.ANY)
    return pl.pallas_call(
        body, name=name, in_specs=[anyspec], out_specs=anyspec,
        out_shape=jax.ShapeDtypeStruct(g.shape, g.dtype),
        scratch_shapes=[pltpu.SemaphoreType.DMA((7,)), pltpu.SemaphoreType.DMA((7,)), pltpu.SemaphoreType.DMA],
    )(g)


def _small_allsum(sv, *, name):
    def body(sv_ref, all_ref, sum_ref, send_sems, recv_sems):
        x, y, c = _coords()
        me = 4 * x + 2 * y + c
        all_ref[me] = sv_ref[...]
        _exchange_all(lambda p: sv_ref, lambda s: all_ref.at[s], send_sems, recv_sems)
        acc = all_ref[0]
        for j in range(1, N_DEV):
            acc = acc + all_ref[j]
        sum_ref[...] = acc

    vm = pl.BlockSpec(memory_space=pltpu.VMEM)
    return pl.pallas_call(
        body, name=name, in_specs=[vm], out_specs=[vm, vm],
        out_shape=[jax.ShapeDtypeStruct((N_DEV, SV_ROWS, 128), F32), jax.ShapeDtypeStruct((SV_ROWS, 128), F32)],
        scratch_shapes=[pltpu.SemaphoreType.DMA((7,)), pltpu.SemaphoreType.DMA((7,))],
    )(sv)


def _ada_bwd(call, dmod_loc, *, name):
    wloc = dmod_loc.shape[1]

    def body(c_ref, d_ref, o_ref):
        o_ref[...] = _dot_hi(_silu(c_ref[...]), d_ref[...], TN)

    vm = pl.BlockSpec(memory_space=pltpu.VMEM)
    return pl.pallas_call(body, name=name, in_specs=[vm, vm], out_specs=vm,
                          out_shape=jax.ShapeDtypeStruct((D, wloc), F32),
                          compiler_params=pltpu.CompilerParams(vmem_limit_bytes=VMEM_BIG))(call, dmod_loc)


def _pad_rows(a, rows):
    return jnp.pad(a, ((0, rows - a.shape[0]), (0, 0)))


def _reorder_in_rows(wt):
    z, xbc, dt, pool, gates = wt[0:2048], wt[2048:5120], wt[5120:5152], wt[5152:6176], wt[6176:8224]
    return jnp.concatenate([xbc, pool, z, gates, dt, jnp.zeros((DT_PAD - 32, D), wt.dtype)], axis=0)


def _restore_in_rows(d):
    return jnp.concatenate([d[C_Z:C_Z + 2048], d[C_XBC:C_XBC + XBC], d[C_DT:C_DT + 32],
                            d[C_POOL:C_POOL + 1024], d[C_GATE:C_GATE + 2048]], axis=0)


def _pack_sv(parts):
    flat = []
    for n, size in SV_PARTS:
        v = parts[n].reshape(-1).astype(F32)
        flat.append(jnp.pad(v, (0, size - v.shape[0])))
    v = jnp.concatenate(flat)
    return jnp.pad(v, (0, SV_ROWS * 128 - v.shape[0])).reshape(SV_ROWS, 128)


def _sv_get(flat, n, size):
    return flat[SV_OFF[n]:SV_OFF[n] + size]


def kernel(x, c, w_ada, b_ada, norm_mix_w, w_in, conv_w, conv_b, dt_bias, a_log, d_skip, ssd_norm_w, w_branch_ssd, pool_w, pool_scale, w_branch_pool, w_out, norm_mlp_w, w_up, w_down, norm_final_w, loss_target, m_w_ada, m_b_ada, m_norm_mix_w, m_w_in, m_conv_w, m_conv_b, m_dt_bias, m_a_log, m_d_skip, m_ssd_norm_w, m_w_branch_ssd, m_pool_w, m_pool_scale, m_w_branch_pool, m_w_out, m_norm_mlp_w, m_w_up, m_w_down, m_norm_final_w, v_w_ada, v_b_ada, v_norm_mix_w, v_w_in, v_conv_w, v_conv_b, v_dt_bias, v_a_log, v_d_skip, v_ssd_norm_w, v_w_branch_ssd, v_pool_w, v_pool_scale, v_w_branch_pool, v_w_out, v_norm_mlp_w, v_w_up, v_w_down, v_norm_final_w):
    xs_ = x[0]
    tgt = loss_target[0]
    L = xs_.shape[0]
    me = 4 * lax.axis_index("x") + 2 * lax.axis_index("y") + lax.axis_index("c")
    wloc = w_ada.shape[2]

    mod_p, c_all = _ada_fwd(c, w_ada[0], b_ada.reshape(N_DEV, wloc), name="ada_fwd")
    mod = mod_p.reshape(6, D)
    shift_m, scale_m, gate_m, shift_f, scale_f, gate_f = [mod[i:i + 1] for i in range(6)]

    conv_bits = lax.bitcast_convert_type(conv_w[0], SLAB_DT).reshape(3, D)
    slab = jnp.concatenate([
        _pad_rows(w_in[0].T.astype(SLAB_DT), IN_ROWS_P),
        w_branch_ssd[0].astype(SLAB_DT),
        pool_w[0].reshape(32, D).astype(SLAB_DT),
        w_branch_pool[0].astype(SLAB_DT),
        w_out[0].astype(SLAB_DT),
        w_up[0].T.astype(SLAB_DT),
        w_down[0].astype(SLAB_DT),
        _pad_rows(conv_bits, 16),
        jnp.zeros((SLAB_ROWS - 2624, D), SLAB_DT)], axis=0)
    gs = _gather_slabs(slab, name="gather_weights")

    def part(n, rows):
        return gs[:, SLAB_OFF[n]:SLAB_OFF[n] + rows]

    w_in_t = _reorder_in_rows(part("in", IN_ROWS).reshape(N_IN, D))
    w_bssd = part("bssd", 256).reshape(DI, D)
    w_pool = part("pool", 32).reshape(N_DEV, 4, 32, PGW).transpose(1, 0, 2, 3).reshape(POOL_W, PGW)
    w_bpool = part("bpool", 128).reshape(POOL_W, D)
    w_o = part("out", 128).reshape(D, D)
    w_up_t = part("up", 512).reshape(DFF, D)
    w_dn = part("down", 512).reshape(DFF, D)
    conv_full = lax.bitcast_convert_type(part("conv", 3).reshape(N_DEV, 4, XBC // N_DEV, 2), F32)
    conv_full = conv_full.transpose(1, 0, 2).reshape(4, XBC)

    dtb = jnp.pad(dt_bias, ((0, 0), (0, 128 - NH)))
    arow = jnp.pad(-jnp.exp(a_log), ((0, 0), (0, 128 - NH)))
    dsk_x = jnp.repeat(d_skip, HP, axis=1)

    tm = _pick(L, (1024, 512, 256, 128))
    tkl = _pick(L, (1024, 512, 256, 128))

    h1 = _norm_fwd(xs_, norm_mix_w, scale_m, shift_m, name="norm1_fwd")
    proj = _mm(h1, w_in_t, "nt", name="in_proj", outs=[F32], tm=tm, tn=768, tk=D)
    xbc = _conv_fwd(proj, conv_full, conv_b, name="conv_fwd")
    y_ssm, hs = _ssd_fwd(xbc, proj, dtb, arow, dsk_x, name="ssd_fwd")
    yn = _gated_norm_fwd(y_ssm, proj, ssd_norm_w, name="gated_norm_fwd")
    y_ssd = _mm(yn, w_bssd, "nn", name="branch_ssd", outs=[F32], tm=tm, tn=D, tk=DI)
    pooled = _pool_fwd(proj, name="pool_fwd")
    yp0, yp1 = _mm_pool(pooled, w_pool, pool_scale, name="pool_mix", tm=tm, transpose_w=False)
    y_pool = _mm(yp1, w_bpool, "nn", name="branch_pool", outs=[F32], tm=tm, tn=D, tk=D)
    m = _merge_fwd(y_ssd, y_pool, proj, name="merge_fwd")
    mix = _mm(m, w_o, "nn", name="out_proj", outs=[F32], tm=tm, tn=D, tk=D)
    x1, h2 = _resid_norm_fwd(xs_, mix, gate_m, norm_mlp_w, scale_f, shift_f, name="norm2_fwd")

    def relu2(acc, ex, outs):
        r = jnp.maximum(acc, 0.0)
        outs[0][...] = acc.astype(BF16)
        outs[1][...] = (r * r).astype(BF16)

    up, act = _mm(h2, w_up_t, "nt", name="mlp_up", outs=[BF16, BF16], tm=tm, tn=1024, tk=D, epilogue=relu2)
    down = _mm(act, w_dn, "nn", name="mlp_down", outs=[F32], tm=tm, tn=D, tk=2048)

    dx2, ddown, loss_p, dnwf, dgate_f = _final_bwd(x1, down, gate_f, norm_final_w.reshape(1, D), tgt,
                                                   name="final_bwd")

    def drelu2(acc, ex, outs):
        outs[0][...] = (acc * (2.0 * jnp.maximum(ex[0][...].astype(F32), 0.0))).astype(BF16)

    dup = _mm(ddown, w_dn, "nt", name="mlp_down_dx", outs=[BF16], tm=tm, tn=1024, tk=D,
              extras=[(up, (tm, 1024), lambda i, j, k: (i, j))], epilogue=drelu2)
    g_dn = _mm(act, ddown, "tn", name="mlp_down_dw", outs=[F32], tm=1024, tn=D, tk=tkl)
    dh2 = _mm(dup, w_up_t, "nn", name="mlp_up_dx", outs=[F32], tm=tm, tn=D, tk=2048)
    g_up_t = _mm(dup, h2, "tn", name="mlp_up_dw", outs=[F32], tm=1024, tn=D, tk=tkl)
    dx1, p2, q2, dmix, dgate_m = _norm_bwd(x1, dh2, dx2, norm_mlp_w, scale_f, mix, gate_m, name="norm2_bwd")
    dm = _mm(dmix, w_o, "nt", name="out_proj_dx", outs=[F32], tm=tm, tn=D, tk=D)
    g_o = _mm(m, dmix, "tn", name="out_proj_dw", outs=[F32], tm=D, tn=D, tk=tkl)
    dy_ssd, dy_pool, dproj = _merge_bwd(dm, y_ssd, y_pool, proj, name="merge_bwd")
    dyn = _mm(dy_ssd, w_bssd, "nt", name="branch_ssd_dx", outs=[F32], tm=tm, tn=1024, tk=D)
    g_bssd = _mm(yn, dy_ssd, "tn", name="branch_ssd_dw", outs=[F32], tm=1024, tn=D, tk=tkl)
    dy_ssm, dproj, d_snw = _gated_norm_bwd(dyn, y_ssm, proj, ssd_norm_w, dproj, name="gated_norm_bwd")
    dxbc, dproj, d_a, d_dx, d_dtb = _ssd_bwd(dy_ssm, xbc, proj, hs, dtb, arow, dsk_x, dproj, name="ssd_bwd")
    dproj, d_cw, d_cb = _conv_bwd(proj, dxbc, conv_full, conv_b, dproj, name="conv_bwd")
    dyp1 = _mm(dy_pool, w_bpool, "nt", name="branch_pool_dx", outs=[F32], tm=tm, tn=D, tk=D)
    g_bpool = _mm(yp1, dy_pool, "tn", name="branch_pool_dw", outs=[F32], tm=D, tn=D, tk=tkl)
    dyp0, d_ps = _pscale_bwd(dyp1, yp0, pool_scale, name="pool_scale_bwd")
    dpooled = _mm_pool(dyp0, w_pool, None, name="pool_mix_dx", tm=tm, transpose_w=True)
    g_pool = _mm_pool_tn(pooled, dyp0, name="pool_mix_dw", tk=tkl)
    dproj = _pool_bwd(dpooled, dproj, name="pool_bwd")
    dh1 = _mm(dproj, w_in_t, "nn", name="in_proj_dx", outs=[F32], tm=tm, tn=D, tk=768)
    g_in_t = _mm(dproj, h1, "tn", name="in_proj_dw", outs=[F32], tm=768, tn=D, tk=tkl)
    grad_x, p1, q1 = _norm_bwd(xs_, dh1, dx1, norm_mix_w, scale_m, name="norm1_bwd")

    g_in = _restore_in_rows(g_in_t).reshape(N_DEV, IN_ROWS, D)
    gslab = jnp.concatenate([
        jnp.pad(g_in, ((0, 0), (0, IN_ROWS_P - IN_ROWS), (0, 0))),
        g_bssd.reshape(N_DEV, 256, D),
        g_pool.reshape(4, N_DEV, 32, PGW).transpose(1, 0, 2, 3).reshape(N_DEV, 32, D),
        g_bpool.reshape(N_DEV, 128, D),
        g_o.reshape(N_DEV, 128, D),
        g_up_t.reshape(N_DEV, 512, D),
        g_dn.reshape(N_DEV, 512, D),
        jnp.zeros((N_DEV, SLAB_ROWS - 2608, D), F32)], axis=1).astype(SLAB_DT)
    recv = _scatter_slabs(gslab, name="scatter_grads")
    gsum = _slab_sum(recv, name="sum_grads")

    dmod = jnp.concatenate([q1, p1 * norm_mix_w, dgate_m, q2, p2 * norm_mlp_w, dgate_f], axis=1)
    d_alog = d_a[:, :NH] * (-jnp.exp(a_log))
    sv = _pack_sv({
        "b_ada": dmod, "norm_mix_w": p1 * (1.0 + scale_m), "conv_b": d_cb, "dt_bias": d_dtb[:, :NH],
        "a_log": d_alog, "d_skip": d_dx.reshape(NH, HP).sum(axis=1), "ssd_norm_w": d_snw,
        "pool_scale": d_ps, "norm_mlp_w": p2 * (1.0 + scale_f), "norm_final_w": dnwf, "conv_w": d_cw,
        "loss": loss_p[:, :1]})
    sv_all, sv_sum = _small_allsum(sv, name="small_allsum")
    flat = sv_sum.reshape(-1)
    loss = flat[SV_OFF["loss"]]
    dmod_all = sv_all.reshape(N_DEV, SV_ROWS * 128)[:, :6 * D]
    g_w_ada = _ada_bwd(c_all, lax.dynamic_slice_in_dim(dmod_all, me * wloc, wloc, axis=1), name="ada_bwd")

    g_conv_w = lax.dynamic_slice_in_dim(_sv_get(flat, "conv_w", 4 * XBC).reshape(4, XBC),
                                        me * (XBC // N_DEV), XBC // N_DEV, axis=1)
    small = {
        "b_ada": (b_ada, m_b_ada, v_b_ada, _sv_get(flat, "b_ada", 6 * D)),
        "norm_mix_w": (norm_mix_w, m_norm_mix_w, v_norm_mix_w, _sv_get(flat, "norm_mix_w", D)),
        "conv_b": (conv_b, m_conv_b, v_conv_b, _sv_get(flat, "conv_b", XBC)),
        "dt_bias": (dt_bias, m_dt_bias, v_dt_bias, _sv_get(flat, "dt_bias", NH)),
        "a_log": (a_log, m_a_log, v_a_log, _sv_get(flat, "a_log", NH)),
        "d_skip": (d_skip, m_d_skip, v_d_skip, _sv_get(flat, "d_skip", NH)),
        "ssd_norm_w": (ssd_norm_w, m_ssd_norm_w, v_ssd_norm_w, _sv_get(flat, "ssd_norm_w", DI)),
        "pool_scale": (pool_scale, m_pool_scale, v_pool_scale, _sv_get(flat, "pool_scale", POOL_W)),
        "norm_mlp_w": (norm_mlp_w, m_norm_mlp_w, v_norm_mlp_w, _sv_get(flat, "norm_mlp_w", D)),
        "norm_final_w": (norm_final_w, m_norm_final_w, v_norm_final_w, _sv_get(flat, "norm_final_w", D)),
        "conv_w": (conv_w, m_conv_w, v_conv_w, g_conv_w),
    }
    names = list(small)
    sizes = [int(np.prod(small[n][0].shape)) for n in names]
    tot = sum(sizes)
    rows = -(-tot // 1024) * 8

    def pack(idx):
        v = jnp.concatenate([small[n][idx].reshape(-1) for n in names])
        return jnp.pad(v, (0, rows * 128 - tot)).reshape(rows, 128)

    sd, sm, sv2 = _adamw(pack(0), pack(3), pack(1), pack(2), name="adamw_small")
    small_out = {}
    off = 0
    for n, sz in zip(names, sizes):
        shp = small[n][0].shape
        small_out[n] = (small[n][3].reshape(shp), sd.reshape(-1)[off:off + sz].reshape(shp),
                        sm.reshape(-1)[off:off + sz].reshape(shp), sv2.reshape(-1)[off:off + sz].reshape(shp))
        off += sz

    def gpart(n, rows_):
        return gsum[SLAB_OFF[n]:SLAB_OFF[n] + rows_]

    big = {
        "w_ada": (w_ada, m_w_ada, v_w_ada, g_w_ada, (D, wloc)),
        "w_in": (w_in, m_w_in, v_w_in, gpart("in", IN_ROWS).T, (D, IN_ROWS)),
        "w_branch_ssd": (w_branch_ssd, m_w_branch_ssd, v_w_branch_ssd, gpart("bssd", 256), (256, D)),
        "pool_w": (pool_w, m_pool_w, v_pool_w, gpart("pool", 32).reshape(128, PGW), (128, PGW)),
        "w_branch_pool": (w_branch_pool, m_w_branch_pool, v_w_branch_pool, gpart("bpool", 128), (128, D)),
        "w_out": (w_out, m_w_out, v_w_out, gpart("out", 128), (128, D)),
        "w_up": (w_up, m_w_up, v_w_up, gpart("up", 512).T, (D, 512)),
        "w_down": (w_down, m_w_down, v_w_down, gpart("down", 512), (512, D)),
    }
    big_out = {}
    for n, (w, mm_, vv, g, shp2) in big.items():
        dlt, mn, vn = _adamw(w.reshape(shp2), g, mm_.reshape(shp2), vv.reshape(shp2), name="adamw_" + n)
        big_out[n] = (g.reshape(w.shape), dlt.reshape(w.shape), mn.reshape(w.shape), vn.reshape(w.shape))

    order = ["w_ada", "b_ada", "norm_mix_w", "w_in", "conv_w", "conv_b", "dt_bias", "a_log", "d_skip",
             "ssd_norm_w", "w_branch_ssd", "pool_w", "pool_scale", "w_branch_pool", "w_out", "norm_mlp_w",
             "w_up", "w_down", "norm_final_w"]
    res = {**small_out, **big_out}
    outs = [loss, grad_x.reshape(x.shape)]
    for k in range(4):
        outs += [res[n][k] for n in order]
    return tuple(outs)
```

```python
import functools

import numpy as np
import jax
import jax.numpy as jnp
from jax import lax
from jax.experimental import pallas as pl
from jax.experimental.pallas import tpu as pltpu

F32 = jnp.float32
BF16 = jnp.bfloat16
SLAB_DT = jnp.bfloat16
_MXU_DTYPE = jnp.bfloat16

N_DEV = 8
D = 1024
DI = 2048
NH = 32
HP = 64
NG = 4
NS = 128
Q = 128
XBC = DI + 2 * NG * NS
DFF = 4096
N_IN = 8224
EPS = 1e-5
POOL_W = 1024
PGW = 256

C_XBC, C_POOL, C_Z, C_GATE, C_DT = 0, 3072, 4096, 6144, 8192
DT_PAD = 256
NPROJ = C_DT + DT_PAD

IN_ROWS = N_IN // N_DEV
IN_ROWS_P = 1040
CONV_ROWS = 16
REST_PARTS = (("bssd", 256), ("pool", 32), ("bpool", 128), ("out", 128), ("up", 512), ("down", 512))
REST_OFF = {}
_o = 0
for _n, _r in REST_PARTS:
    REST_OFF[_n] = _o
    _o += _r
REST_ROWS = _o
MIX_PARTS = (("bssd", 256), ("pool", 32), ("bpool", 128), ("out", 128))
MIX_OFF = {}
_o = 0
for _n, _r in MIX_PARTS:
    MIX_OFF[_n] = _o
    _o += _r
MIX_ROWS = _o

SV_PARTS = (("b_ada", 6144), ("norm_mix_w", 1024), ("conv_b", 3072), ("dt_bias", 128), ("a_log", 128),
            ("d_skip", 128), ("ssd_norm_w", 2048), ("pool_scale", 1024), ("norm_mlp_w", 1024),
            ("norm_final_w", 1024), ("conv_w", 4 * XBC), ("loss", 128))
SV_OFF = {}
_o = 0
for _n, _r in SV_PARTS:
    SV_OFF[_n] = _o
    _o += _r
SV_ROWS = 224
assert _o <= SV_ROWS * 128

ADAM_LR, ADAM_B1, ADAM_B2, ADAM_EPS, ADAM_WD, ADAM_STEP = 0.001, 0.9, 0.999, 1e-08, 0.01, 10

VMEM_BIG = 56 * 1024 * 1024
NEG = -1e30

NN = ((1,), (0,))
NT = ((1,), (1,))
TN = ((0,), (0,))


def _dot(a, b, dims=NN):
    return lax.dot_general(a.astype(_MXU_DTYPE), b.astype(_MXU_DTYPE), (dims, ((), ())),
                           preferred_element_type=F32)


def _dot_hi(a, b, dims=NN):
    return lax.dot_general(a.astype(F32), b.astype(F32), (dims, ((), ())),
                           precision=lax.Precision.HIGHEST, preferred_element_type=F32)


def _pick(n, cands):
    for c in cands:
        if n % c == 0:
            return c
    return n


def _sigmoid(x):
    return 1.0 / (1.0 + jnp.exp(-x))


def _silu(x):
    return x * _sigmoid(x)


def _dsilu(x):
    s = _sigmoid(x)
    return s * (1.0 + x * (1.0 - s))


def _softplus(x):
    return jnp.maximum(x, 0.0) + jnp.log(1.0 + jnp.exp(-jnp.abs(x)))


def _params(sem, vmem=None):
    return pltpu.CompilerParams(dimension_semantics=sem, vmem_limit_bytes=vmem)


def _mm(a, b, mode, *, name, outs, tm, tn, tk, extras=(), epilogue=None):
    if mode == "tn":
        K, M = a.shape
        N = b.shape[1]
        a_spec = pl.BlockSpec((tk, tm), lambda i, j, k: (k, i))
        b_spec = pl.BlockSpec((tk, tn), lambda i, j, k: (k, j))
        dims = TN
    else:
        M, K = a.shape
        a_spec = pl.BlockSpec((tm, tk), lambda i, j, k: (i, k))
        if mode == "nn":
            N = b.shape[1]
            b_spec = pl.BlockSpec((tk, tn), lambda i, j, k: (k, j))
            dims = NN
        else:
            N = b.shape[0]
            b_spec = pl.BlockSpec((tn, tk), lambda i, j, k: (j, k))
            dims = NT
    assert M % tm == 0 and N % tn == 0 and K % tk == 0, (name, M, N, K, tm, tn, tk)
    nk = K // tk
    ne, no = len(extras), len(outs)
    if epilogue is None:
        def epilogue(acc, ex, out_refs):
            out_refs[0][...] = acc.astype(out_refs[0].dtype)

    def body(a_ref, b_ref, *rest):
        ex, out_refs = rest[:ne], rest[ne:ne + no]
        p = _dot(a_ref[...], b_ref[...], dims)
        if nk == 1:
            epilogue(p, ex, out_refs)
        else:
            acc = rest[-1]
            k = pl.program_id(2)

            @pl.when(k == 0)
            def _():
                acc[...] = p

            @pl.when(k > 0)
            def _():
                acc[...] += p

            @pl.when(k == nk - 1)
            def _():
                epilogue(acc[...], ex, out_refs)

    res = pl.pallas_call(
        body, name=name,
        grid=(M // tm, N // tn, nk),
        in_specs=[a_spec, b_spec] + [pl.BlockSpec(bs, im) for _, bs, im in extras],
        out_specs=[pl.BlockSpec((tm, tn), lambda i, j, k: (i, j)) for _ in outs],
        out_shape=[jax.ShapeDtypeStruct((M, N), dt) for dt in outs],
        scratch_shapes=[pltpu.VMEM((tm, tn), F32)] if nk > 1 else [],
        compiler_params=_params(("parallel", "parallel", "arbitrary"), VMEM_BIG),
    )(a, b, *[e[0] for e in extras])
    return res if no > 1 else res[0]


def _mm_pool(a, w, scale, *, name, tm, transpose_w):
    L = a.shape[0]
    dims = NT if transpose_w else NN

    def body(a_ref, w_ref, *rest):
        p = _dot(a_ref[...], w_ref[...], dims)
        if transpose_w:
            rest[0][...] = p
        else:
            s_ref, o0, o1 = rest
            o0[...] = p
            o1[...] = (p * s_ref[...]).astype(o1.dtype)

    blk = pl.BlockSpec((tm, PGW), lambda i, j: (i, j))
    in_specs = [blk, pl.BlockSpec((PGW, PGW), lambda i, j: (j, 0))]
    args = [a, w]
    if transpose_w:
        out_specs, out_shape = [blk], [jax.ShapeDtypeStruct((L, POOL_W), F32)]
    else:
        in_specs.append(pl.BlockSpec((1, PGW), lambda i, j: (0, j)))
        args.append(scale)
        out_specs = [blk, blk]
        out_shape = [jax.ShapeDtypeStruct((L, POOL_W), F32), jax.ShapeDtypeStruct((L, POOL_W), BF16)]
    res = pl.pallas_call(body, name=name, grid=(L // tm, 4), in_specs=in_specs, out_specs=out_specs,
                         out_shape=out_shape, compiler_params=_params(("parallel", "parallel")))(*args)
    return res[0] if transpose_w else res


def _mm_pool_tn(a, b, *, name, tk):
    L = a.shape[0]

    def body(a_ref, b_ref, o_ref):
        p = _dot(a_ref[...], b_ref[...], TN)

        @pl.when(pl.program_id(1) == 0)
        def _():
            o_ref[...] = p

        @pl.when(pl.program_id(1) > 0)
        def _():
            o_ref[...] += p

    blk = pl.BlockSpec((tk, PGW), lambda g, k: (k, g))
    return pl.pallas_call(body, name=name, grid=(4, L // tk), in_specs=[blk, blk],
                          out_specs=pl.BlockSpec((PGW, PGW), lambda g, k: (g, 0)),
                          out_shape=jax.ShapeDtypeStruct((POOL_W, PGW), F32),
                          compiler_params=_params(("parallel", "arbitrary")))(a, b)


def _row(tl, w, col=0):
    return pl.BlockSpec((tl, w), lambda i, c=col: (i, c))


def _vec(w, col=0):
    return pl.BlockSpec((1, w), lambda i, c=col: (0, c))


def _acc_out(ref, val, i):
    @pl.when(i == 0)
    def _():
        ref[...] = val

    @pl.when(i > 0)
    def _():
        ref[...] += val


def _colsum(v):
    return jnp.sum(v, axis=0, keepdims=True)


def _norm_fwd(x, nw, scale, shift, *, name):
    L = x.shape[0]
    tl = _pick(L, (512, 256, 128))

    def body(x_ref, nw_ref, sc_ref, sh_ref, h_ref):
        xv = x_ref[...]
        r = lax.rsqrt(jnp.mean(xv * xv, axis=-1, keepdims=True) + EPS)
        h_ref[...] = (xv * r * nw_ref[...] * (1.0 + sc_ref[...]) + sh_ref[...]).astype(h_ref.dtype)

    return pl.pallas_call(body, name=name, grid=(L // tl,),
                          in_specs=[_row(tl, D), _vec(D), _vec(D), _vec(D)], out_specs=_row(tl, D),
                          out_shape=jax.ShapeDtypeStruct((L, D), BF16),
                          compiler_params=_params(("parallel",)))(x, nw, scale, shift)


def _resid_norm_fwd(x, mix, gate, nw, scale, shift, *, name):
    L = x.shape[0]
    tl = _pick(L, (512, 256, 128))

    def body(x_ref, m_ref, g_ref, nw_ref, sc_ref, sh_ref, x1_ref, h_ref):
        xv = x_ref[...] + g_ref[...] * m_ref[...]
        x1_ref[...] = xv
        r = lax.rsqrt(jnp.mean(xv * xv, axis=-1, keepdims=True) + EPS)
        h_ref[...] = (xv * r * nw_ref[...] * (1.0 + sc_ref[...]) + sh_ref[...]).astype(h_ref.dtype)

    return pl.pallas_call(body, name=name, grid=(L // tl,),
                          in_specs=[_row(tl, D), _row(tl, D), _vec(D), _vec(D), _vec(D), _vec(D)],
                          out_specs=[_row(tl, D), _row(tl, D)],
                          out_shape=[jax.ShapeDtypeStruct((L, D), F32), jax.ShapeDtypeStruct((L, D), BF16)],
                          compiler_params=_params(("parallel",)))(x, mix, gate, nw, scale, shift)


def _final_bwd(x1, down, gate_f, nwf, tgt, *, name):
    L = x1.shape[0]
    tl = _pick(L, (512, 256, 128))

    def body(x1_ref, dn_ref, g_ref, nw_ref, t_ref, dx2_ref, dd_ref, loss_ref, dnw_ref, dg_ref):
        i = pl.program_id(0)
        dn = dn_ref[...]
        x2 = x1_ref[...] + g_ref[...] * dn
        r = lax.rsqrt(jnp.mean(x2 * x2, axis=-1, keepdims=True) + EPS)
        xh = x2 * r
        e = xh * nw_ref[...] - t_ref[...]
        part = 0.5 * jnp.sum(jnp.mean(e * e, axis=-1, keepdims=True), axis=0, keepdims=True)
        dy = e * (1.0 / D)
        g = dy * nw_ref[...]
        dx2 = r * (g - xh * jnp.mean(g * xh, axis=-1, keepdims=True))
        dx2_ref[...] = dx2
        dd_ref[...] = (dx2 * g_ref[...]).astype(dd_ref.dtype)
        _acc_out(loss_ref, jnp.broadcast_to(part, (1, 128)), i)
        _acc_out(dnw_ref, _colsum(dy * xh), i)
        _acc_out(dg_ref, _colsum(dx2 * dn), i)

    return pl.pallas_call(
        body, name=name, grid=(L // tl,),
        in_specs=[_row(tl, D), _row(tl, D), _vec(D), _vec(D), _row(tl, D)],
        out_specs=[_row(tl, D), _row(tl, D), _vec(128), _vec(D), _vec(D)],
        out_shape=[jax.ShapeDtypeStruct((L, D), F32), jax.ShapeDtypeStruct((L, D), BF16),
                   jax.ShapeDtypeStruct((1, 128), F32), jax.ShapeDtypeStruct((1, D), F32),
                   jax.ShapeDtypeStruct((1, D), F32)],
        compiler_params=_params(("arbitrary",)))(x1, down, gate_f, nwf, tgt)


def _norm_bwd(xin, dh, dres, nw, scale, mix=None, gate=None, *, name):
    L = xin.shape[0]
    tl = _pick(L, (512, 256, 128))
    with_mix = mix is not None

    def body(*refs):
        if with_mix:
            x_ref, dh_ref, dr_ref, nw_ref, sc_ref, m_ref, g_ref, dx_ref, p_ref, q_ref, dm_ref, dg_ref = refs
        else:
            x_ref, dh_ref, dr_ref, nw_ref, sc_ref, dx_ref, p_ref, q_ref = refs
        i = pl.program_id(0)
        xv = x_ref[...]
        dh_v = dh_ref[...]
        r = lax.rsqrt(jnp.mean(xv * xv, axis=-1, keepdims=True) + EPS)
        xh = xv * r
        g = dh_v * (nw_ref[...] * (1.0 + sc_ref[...]))
        dx = dr_ref[...] + r * (g - xh * jnp.mean(g * xh, axis=-1, keepdims=True))
        dx_ref[...] = dx
        _acc_out(p_ref, _colsum(dh_v * xh), i)
        _acc_out(q_ref, _colsum(dh_v), i)
        if with_mix:
            dm_ref[...] = (dx * g_ref[...]).astype(dm_ref.dtype)
            _acc_out(dg_ref, _colsum(dx * m_ref[...]), i)

    in_specs = [_row(tl, D), _row(tl, D), _row(tl, D), _vec(D), _vec(D)]
    out_specs = [_row(tl, D), _vec(D), _vec(D)]
    out_shape = [jax.ShapeDtypeStruct((L, D), F32), jax.ShapeDtypeStruct((1, D), F32),
                 jax.ShapeDtypeStruct((1, D), F32)]
    args = [xin, dh, dres, nw, scale]
    if with_mix:
        in_specs += [_row(tl, D), _vec(D)]
        out_specs += [_row(tl, D), _vec(D)]
        out_shape += [jax.ShapeDtypeStruct((L, D), BF16), jax.ShapeDtypeStruct((1, D), F32)]
        args += [mix, gate]
    return pl.pallas_call(body, name=name, grid=(L // tl,), in_specs=in_specs, out_specs=out_specs,
                          out_shape=out_shape, compiler_params=_params(("arbitrary",)))(*args)


def _merge_fwd(y_ssd, y_pool, proj, *, name):
    L = y_ssd.shape[0]
    tl = _pick(L, (512, 256, 128))

    def body(a_ref, b_ref, gl_ref, m_ref):
        s = _sigmoid(gl_ref[...])
        m_ref[...] = (s[:, :D] * a_ref[...] + s[:, D:] * b_ref[...]).astype(m_ref.dtype)

    return pl.pallas_call(body, name=name, grid=(L // tl,),
                          in_specs=[_row(tl, D), _row(tl, D), _row(tl, 2 * D, C_GATE // (2 * D))],
                          out_specs=_row(tl, D), out_shape=jax.ShapeDtypeStruct((L, D), BF16),
                          compiler_params=_params(("parallel",)))(y_ssd, y_pool, proj)


def _merge_bwd(dm, y_ssd, y_pool, proj, *, name):
    L = dm.shape[0]
    tl = _pick(L, (512, 256, 128))

    def body(dm_ref, a_ref, b_ref, gl_ref, da_ref, db_ref, dgl_ref):
        s = _sigmoid(gl_ref[...])
        dmv = dm_ref[...]
        s1, s2 = s[:, :D], s[:, D:]
        da_ref[...] = (dmv * s1).astype(da_ref.dtype)
        db_ref[...] = (dmv * s2).astype(db_ref.dtype)
        dgl_ref[:, :D] = (dmv * a_ref[...] * s1 * (1.0 - s1)).astype(dgl_ref.dtype)
        dgl_ref[:, D:] = (dmv * b_ref[...] * s2 * (1.0 - s2)).astype(dgl_ref.dtype)

    gcol = C_GATE // (2 * D)
    return pl.pallas_call(
        body, name=name, grid=(L // tl,),
        in_specs=[_row(tl, D), _row(tl, D), _row(tl, D), _row(tl, 2 * D, gcol)],
        out_specs=[_row(tl, D), _row(tl, D), _row(tl, 2 * D, gcol)],
        out_shape=[jax.ShapeDtypeStruct((L, D), BF16), jax.ShapeDtypeStruct((L, D), BF16),
                   jax.ShapeDtypeStruct((L, NPROJ), BF16)],
        compiler_params=_params(("parallel",)))(dm, y_ssd, y_pool, proj)


GW = DI // NG


def _gated_norm_fwd(y, proj, w, *, name):
    L = y.shape[0]
    tl = _pick(L, (256, 128))

    def body(y_ref, z_ref, w_ref, o_ref):
        yg = y_ref[...] * _silu(z_ref[...])
        for k in range(NG):
            seg = yg[:, k * GW:(k + 1) * GW]
            r = lax.rsqrt(jnp.mean(seg * seg, axis=-1, keepdims=True) + EPS)
            o_ref[:, k * GW:(k + 1) * GW] = (seg * r * w_ref[:, k * GW:(k + 1) * GW]).astype(o_ref.dtype)

    return pl.pallas_call(body, name=name, grid=(L // tl,),
                          in_specs=[_row(tl, DI), _row(tl, DI, C_Z // DI), _vec(DI)],
                          out_specs=_row(tl, DI), out_shape=jax.ShapeDtypeStruct((L, DI), BF16),
                          compiler_params=_params(("parallel",)))(y, proj, w)


def _gated_norm_bwd(dyn, y, proj, w, dproj, *, name):
    L = y.shape[0]
    tl = _pick(L, (256, 128))

    def body(dyn_ref, y_ref, z_ref, w_ref, dp_in, dy_ref, dz_ref, dw_ref):
        del dp_in
        i = pl.program_id(0)
        zv = z_ref[...]
        yv = y_ref[...]
        sz = _silu(zv)
        yg = yv * sz
        dsz = _dsilu(zv)
        dws = []
        for k in range(NG):
            sl = slice(k * GW, (k + 1) * GW)
            seg = yg[:, sl]
            r = lax.rsqrt(jnp.mean(seg * seg, axis=-1, keepdims=True) + EPS)
            sh = seg * r
            dn = dyn_ref[:, sl]
            g = dn * w_ref[:, sl]
            dyg = r * (g - sh * jnp.mean(g * sh, axis=-1, keepdims=True))
            dy_ref[:, sl] = dyg * sz[:, sl]
            dz_ref[:, sl] = (dyg * yv[:, sl] * dsz[:, sl]).astype(dz_ref.dtype)
            dws.append(_colsum(dn * sh))
        _acc_out(dw_ref, jnp.concatenate(dws, axis=1), i)

    zc = C_Z // DI
    res = pl.pallas_call(
        body, name=name, grid=(L // tl,),
        in_specs=[_row(tl, DI), _row(tl, DI), _row(tl, DI, zc), _vec(DI), pl.BlockSpec(memory_space=pl.ANY)],
        out_specs=[_row(tl, DI), _row(tl, DI, zc), _vec(DI)],
        out_shape=[jax.ShapeDtypeStruct((L, DI), F32), jax.ShapeDtypeStruct((L, NPROJ), BF16),
                   jax.ShapeDtypeStruct((1, DI), F32)],
        input_output_aliases={4: 1},
        compiler_params=_params(("arbitrary",)))(dyn, y, proj, w, dproj)
    return res


def _pscale_bwd(dyp1, yp0, scale, *, name):
    L = dyp1.shape[0]
    tl = _pick(L, (512, 256, 128))

    def body(d_ref, y_ref, s_ref, o_ref, ds_ref):
        dv = d_ref[...]
        o_ref[...] = (dv * s_ref[...]).astype(o_ref.dtype)
        _acc_out(ds_ref, _colsum(dv * y_ref[...]), pl.program_id(0))

    return pl.pallas_call(body, name=name, grid=(L // tl,),
                          in_specs=[_row(tl, POOL_W), _row(tl, POOL_W), _vec(POOL_W)],
                          out_specs=[_row(tl, POOL_W), _vec(POOL_W)],
                          out_shape=[jax.ShapeDtypeStruct((L, POOL_W), BF16),
                                     jax.ShapeDtypeStruct((1, POOL_W), F32)],
                          compiler_params=_params(("arbitrary",)))(dyp1, yp0, scale)


CONV_CB = 128
HALO = 16


def _time_chunk(L):
    return _pick(L, (256, 128))


def _conv_fwd(proj, w, b, *, name):
    L = proj.shape[0]
    rc = _time_chunk(L)
    n = L // rc

    def body(x_ref, w_ref, b_ref, o_ref, pad):
        pad[0:HALO, :] = jnp.zeros((HALO, CONV_CB), F32)
        wv = w_ref[...]
        bv = b_ref[...]

        def fill(i, c):
            r0 = pl.multiple_of(i * rc, rc)
            pad[pl.ds(r0 + HALO, rc), :] = x_ref[pl.ds(r0, rc), :]
            return c

        lax.fori_loop(0, n, fill, 0)

        def step(i, c):
            r0 = pl.multiple_of(i * rc, rc)
            ext = pad[pl.ds(r0, rc + HALO), :]
            acc = bv + ext * wv[3:4]
            for j in (1, 2, 3):
                acc = acc + pltpu.roll(ext, j, 0) * wv[3 - j:4 - j]
            acc = acc[HALO:]
            o_ref[pl.ds(r0, rc), :] = acc * _sigmoid(acc)
            return c

        lax.fori_loop(0, n, step, 0)

    return pl.pallas_call(
        body, name=name, grid=(XBC // CONV_CB,),
        in_specs=[pl.BlockSpec((L, CONV_CB), lambda j: (0, j + C_XBC // CONV_CB)),
                  pl.BlockSpec((4, CONV_CB), lambda j: (0, j)), pl.BlockSpec((1, CONV_CB), lambda j: (0, j))],
        out_specs=pl.BlockSpec((L, CONV_CB), lambda j: (0, j)),
        out_shape=jax.ShapeDtypeStruct((L, XBC), F32),
        scratch_shapes=[pltpu.VMEM((L + HALO, CONV_CB), F32)],
        compiler_params=_params(("parallel",), VMEM_BIG))(proj, w, b)


def _conv_bwd(proj, dy, w, b, dproj, *, name):
    L = proj.shape[0]
    rc = _time_chunk(L)
    n = L // rc

    def body(x_ref, dy_ref, w_ref, b_ref, dp_in, dx_ref, dw_ref, db_ref, pad, dpad):
        del dp_in
        pad[0:HALO, :] = jnp.zeros((HALO, CONV_CB), F32)
        dpad[L:L + HALO, :] = jnp.zeros((HALO, CONV_CB), F32)
        wv = w_ref[...]
        bv = b_ref[...]

        def fill(i, c):
            r0 = pl.multiple_of(i * rc, rc)
            pad[pl.ds(r0 + HALO, rc), :] = x_ref[pl.ds(r0, rc), :]
            return c

        lax.fori_loop(0, n, fill, 0)

        def p1(i, carry):
            r0 = pl.multiple_of(i * rc, rc)
            ext = pad[pl.ds(r0, rc + HALO), :]
            xk = [ext[HALO:]] + [pltpu.roll(ext, j, 0)[HALO:] for j in (1, 2, 3)]
            pre = bv
            for j in range(4):
                pre = pre + xk[j] * wv[3 - j:4 - j]
            dpre = dy_ref[pl.ds(r0, rc), :] * _dsilu(pre)
            dpad[pl.ds(r0, rc), :] = dpre
            db, d0, d1, d2, d3 = carry
            return (db + _colsum(dpre), d0 + _colsum(dpre * xk[3]), d1 + _colsum(dpre * xk[2]),
                    d2 + _colsum(dpre * xk[1]), d3 + _colsum(dpre * xk[0]))

        z = jnp.zeros((1, CONV_CB), F32)
        db, d0, d1, d2, d3 = lax.fori_loop(0, n, p1, (z, z, z, z, z))
        db_ref[...] = db
        dw_ref[...] = jnp.concatenate([d0, d1, d2, d3], axis=0)

        def p2(i, c):
            r0 = pl.multiple_of(i * rc, rc)
            ext = dpad[pl.ds(r0, rc + HALO), :]
            acc = ext * wv[3:4]
            for j in (1, 2, 3):
                acc = acc + pltpu.roll(ext, rc + HALO - j, 0) * wv[3 - j:4 - j]
            dx_ref[pl.ds(r0, rc), :] = acc[:rc].astype(dx_ref.dtype)
            return c

        lax.fori_loop(0, n, p2, 0)

    nb = XBC // CONV_CB
    return pl.pallas_call(
        body, name=name, grid=(nb,),
        in_specs=[pl.BlockSpec((L, CONV_CB), lambda j: (0, j + C_XBC // CONV_CB)),
                  pl.BlockSpec((L, CONV_CB), lambda j: (0, j)),
                  pl.BlockSpec((4, CONV_CB), lambda j: (0, j)), pl.BlockSpec((1, CONV_CB), lambda j: (0, j)),
                  pl.BlockSpec(memory_space=pl.ANY)],
        out_specs=[pl.BlockSpec((L, CONV_CB), lambda j: (0, j + C_XBC // CONV_CB)),
                   pl.BlockSpec((4, CONV_CB), lambda j: (0, j)), pl.BlockSpec((1, CONV_CB), lambda j: (0, j))],
        out_shape=[jax.ShapeDtypeStruct((L, NPROJ), BF16), jax.ShapeDtypeStruct((4, XBC), F32),
                   jax.ShapeDtypeStruct((1, XBC), F32)],
        scratch_shapes=[pltpu.VMEM((L + HALO, CONV_CB), F32), pltpu.VMEM((L + HALO, CONV_CB), F32)],
        input_output_aliases={4: 0},
        compiler_params=_params(("parallel",), VMEM_BIG))(proj, dy, w, b, dproj)


def _pool_fwd(proj, *, name):
    L = proj.shape[0]
    rc = _time_chunk(L)
    n = L // rc

    def body(x_ref, o_ref, pad):
        g = pl.program_id(0)
        pad[0:HALO, :] = jnp.zeros((HALO, PGW), F32)

        def fill(i, c):
            r0 = pl.multiple_of(i * rc, rc)
            pad[pl.ds(r0 + HALO, rc), :] = x_ref[pl.ds(r0, rc), :]
            return c

        lax.fori_loop(0, n, fill, 0)
        rows = lax.broadcasted_iota(jnp.int32, (rc, PGW), 0)

        for gi in range(4):
            win = 2 << gi

            @pl.when(g == gi)
            def _(gi=gi, win=win):
                def step(i, c):
                    r0 = pl.multiple_of(i * rc, rc)
                    ext = pad[pl.ds(r0, rc + HALO), :]
                    s = ext
                    sh = 1
                    while sh < win:
                        s = s + pltpu.roll(s, sh, 0)
                        sh *= 2
                    cnt = jnp.minimum(rows + (r0 + 1), win).astype(F32)
                    o_ref[pl.ds(r0, rc), :] = (s[HALO:] / cnt - ext[HALO:]).astype(o_ref.dtype)
                    return c

                lax.fori_loop(0, n, step, 0)

    return pl.pallas_call(
        body, name=name, grid=(4,),
        in_specs=[pl.BlockSpec((L, PGW), lambda j: (0, j + C_POOL // PGW))],
        out_specs=pl.BlockSpec((L, PGW), lambda j: (0, j)),
        out_shape=jax.ShapeDtypeStruct((L, POOL_W), BF16),
        scratch_shapes=[pltpu.VMEM((L + HALO, PGW), F32)],
        compiler_params=_params(("parallel",), VMEM_BIG))(proj)


def _pool_bwd(dpooled, dproj, *, name):
    L = dpooled.shape[0]
    rc = _time_chunk(L)
    n = L // rc

    def body(d_ref, dp_in, o_ref, pad):
        del dp_in
        g = pl.program_id(0)
        pad[L:L + HALO, :] = jnp.zeros((HALO, PGW), F32)
        rows = lax.broadcasted_iota(jnp.int32, (rc, PGW), 0)

        for gi in range(4):
            win = 2 << gi

            @pl.when(g == gi)
            def _(gi=gi, win=win):
                def fill(i, c):
                    r0 = pl.multiple_of(i * rc, rc)
                    cnt = jnp.minimum(rows + (r0 + 1), win).astype(F32)
                    pad[pl.ds(r0, rc), :] = d_ref[pl.ds(r0, rc), :] / cnt
                    return c

                lax.fori_loop(0, n, fill, 0)

                def step(i, c):
                    r0 = pl.multiple_of(i * rc, rc)
                    s = pad[pl.ds(r0, rc + HALO), :]
                    sh = 1
                    while sh < win:
                        s = s + pltpu.roll(s, rc + HALO - sh, 0)
                        sh *= 2
                    o_ref[pl.ds(r0, rc), :] = (s[:rc] - d_ref[pl.ds(r0, rc), :]).astype(o_ref.dtype)
                    return c

                lax.fori_loop(0, n, step, 0)

    return pl.pallas_call(
        body, name=name, grid=(4,),
        in_specs=[pl.BlockSpec((L, PGW), lambda j: (0, j)), pl.BlockSpec(memory_space=pl.ANY)],
        out_specs=pl.BlockSpec((L, PGW), lambda j: (0, j + C_POOL // PGW)),
        out_shape=jax.ShapeDtypeStruct((L, NPROJ), BF16),
        scratch_shapes=[pltpu.VMEM((L + HALO, PGW), F32)],
        input_output_aliases={1: 0},
        compiler_params=_params(("parallel",), VMEM_BIG))(dpooled, dproj)


def _ssd_consts():
    tri = np.tril(np.ones((Q, Q), np.float32))
    exp = np.zeros((128, DI), np.float32)
    for h in range(NH):
        exp[h, h * HP:(h + 1) * HP] = 1.0
    return jnp.asarray(tri), jnp.asarray(tri.T.copy()), jnp.asarray(exp)


def _ssd_common(xbc_ref, dtw_ref, dtb_ref, arow_ref, t_ref, e_ref):
    pre = dtw_ref[:, :128] + dtb_ref[...]
    dt = _softplus(pre)
    acs = _dot_hi(t_ref[...], dt * arow_ref[...])
    acs_x = _dot_hi(acs, e_ref[...])
    dt_x = _dot_hi(dt, e_ref[...])
    xs = xbc_ref[:, 0:DI]
    return pre, dt, acs, acs.T, acs_x, dt_x, xs


def _ssd_fwd(xbc, proj, dtb, arow, dsk_x, *, name):
    L = xbc.shape[0]
    nc = L // Q
    tri, _, expand = _ssd_consts()

    def body(xbc_ref, dtw_ref, dtb_ref, arow_ref, dsk_ref, t_ref, e_ref, y_ref, hs_ref, h_scr):
        @pl.when(pl.program_id(0) == 0)
        def _():
            h_scr[...] = jnp.zeros_like(h_scr)

        _, dt, acs, acs_t, acs_x, dt_x, xs = _ssd_common(xbc_ref, dtw_ref, dtb_ref, arow_ref, t_ref, e_ref)
        xdt = xs * dt_x
        eacs = jnp.exp(acs_x)
        acs_last = acs_x[Q - 1:Q, :]
        dec = jnp.exp(acs_last - acs_x)
        hs_ref[0] = h_scr[...]
        causal = lax.broadcasted_iota(jnp.int32, (Q, Q), 0) >= lax.broadcasted_iota(jnp.int32, (Q, Q), 1)
        first = lax.broadcasted_iota(jnp.int32, (Q, 128), 1) < HP
        for g in range(NG):
            bg = xbc_ref[:, DI + g * NS:DI + (g + 1) * NS]
            cg = xbc_ref[:, DI + NG * NS + g * NS:DI + NG * NS + (g + 1) * NS]
            s = _dot(cg, bg, NT)
            sl = slice(g * GW, (g + 1) * GW)
            hg = h_scr[:, sl]
            yoff = _dot(cg, hg, NN) * eacs[:, sl]
            st = _dot(bg, xdt[:, sl] * dec[:, sl], TN)
            h_scr[:, sl] = hg * eacs[Q - 1:Q, sl] + st
            for j in range(4):
                lo = g * GW + j * 128
                xb = xdt[:, lo:lo + 128]
                yp = yoff[:, j * 128:(j + 1) * 128] + dsk_ref[:, lo:lo + 128] * xs[:, lo:lo + 128]
                for e in range(2):
                    h = g * 8 + j * 2 + e
                    lm = jnp.exp(jnp.where(causal, acs[:, h:h + 1] - acs_t[h:h + 1, :], NEG))
                    xm = jnp.where(first if e == 0 else jnp.logical_not(first), xb, 0.0)
                    yp = yp + _dot(s * lm, xm, NN)
                y_ref[:, lo:lo + 128] = yp

    return pl.pallas_call(
        body, name=name, grid=(nc,),
        in_specs=[pl.BlockSpec((Q, XBC), lambda c: (c, 0)),
                  pl.BlockSpec((Q, DT_PAD), lambda c: (c, C_DT // DT_PAD)),
                  pl.BlockSpec((1, 128), lambda c: (0, 0)), pl.BlockSpec((1, 128), lambda c: (0, 0)),
                  pl.BlockSpec((1, DI), lambda c: (0, 0)),
                  pl.BlockSpec((Q, Q), lambda c: (0, 0)), pl.BlockSpec((128, DI), lambda c: (0, 0))],
        out_specs=[pl.BlockSpec((Q, DI), lambda c: (c, 0)), pl.BlockSpec((1, NS, DI), lambda c: (c, 0, 0))],
        out_shape=[jax.ShapeDtypeStruct((L, DI), F32), jax.ShapeDtypeStruct((nc, NS, DI), F32)],
        scratch_shapes=[pltpu.VMEM((NS, DI), F32)],
        compiler_params=_params(("arbitrary",), VMEM_BIG))(xbc, proj, dtb, arow, dsk_x, tri, expand)


def _ssd_bwd(dy, xbc, proj, hs, dtb, arow, dsk_x, dproj, *, name):
    L = xbc.shape[0]
    nc = L // Q
    tri, triu, expand = _ssd_consts()

    def body(dy_ref, xbc_ref, dtw_ref, hs_ref, dtb_ref, arow_ref, dsk_ref, t_ref, u_ref, e_ref, dp_in,
             dxbc_ref, ddtw_ref, da_ref, ddx_ref, ddtb_ref, dh_scr):
        del dp_in
        i = pl.program_id(0)

        @pl.when(i == 0)
        def _():
            dh_scr[...] = jnp.zeros_like(dh_scr)

        pre, dt, acs, acs_t, acs_x, dt_x, xs = _ssd_common(xbc_ref, dtw_ref, dtb_ref, arow_ref, t_ref, e_ref)
        dyv = dy_ref[...]
        xdt = xs * dt_x
        eacs = jnp.exp(acs_x)
        acs_last = acs_x[Q - 1:Q, :]
        dec = jnp.exp(acs_last - acs_x)
        gy = dyv * eacs
        causal = lax.broadcasted_iota(jnp.int32, (Q, Q), 0) >= lax.broadcasted_iota(jnp.int32, (Q, Q), 1)
        first = lax.broadcasted_iota(jnp.int32, (Q, 128), 1) < HP
        lane_h = lax.broadcasted_iota(jnp.int32, (Q, 128), 1)
        sub_h = lax.broadcasted_iota(jnp.int32, (128, Q), 0)
        last_row = lax.broadcasted_iota(jnp.int32, (Q, GW), 0) == Q - 1
        dacs = jnp.zeros((Q, 128), F32)
        dacs_t = jnp.zeros((128, Q), F32)
        ddt = jnp.zeros((Q, 128), F32)
        for g in range(NG):
            bg = xbc_ref[:, DI + g * NS:DI + (g + 1) * NS]
            cg = xbc_ref[:, DI + NG * NS + g * NS:DI + NG * NS + (g + 1) * NS]
            s = _dot(cg, bg, NT)
            sl = slice(g * GW, (g + 1) * GW)
            hg = hs_ref[0, :, sl]
            dhn = dh_scr[:, sl]
            eal = eacs[Q - 1:Q, sl]
            gg = gy[:, sl]
            dax = gg * _dot(cg, hg, NN)
            dcg = _dot(gg, hg, NT)
            dh_scr[:, sl] = _dot(cg, gg, TN) + dhn * eal
            dal = eal * _colsum(dhn * hg)
            xdd = xdt[:, sl] * dec[:, sl]
            dbg = _dot(xdd, dhn, NT)
            wv = _dot(bg, dhn, NN)
            dd = wv * xdd
            dax = dax - dd
            dal = dal + _colsum(dd)
            dax = dax + jnp.where(last_row, dal, 0.0)
            dxdt_g = wv * dec[:, sl]
            ds = jnp.zeros((Q, Q), F32)
            dxdt_blocks = []
            for j in range(4):
                lo = g * GW + j * 128
                xb = xdt[:, lo:lo + 128]
                dyb = dyv[:, lo:lo + 128]
                dxb = dxdt_g[:, j * 128:(j + 1) * 128]
                for e in range(2):
                    h = g * 8 + j * 2 + e
                    lm = jnp.exp(jnp.where(causal, acs[:, h:h + 1] - acs_t[h:h + 1, :], NEG))
                    m = s * lm
                    dym = jnp.where(first if e == 0 else jnp.logical_not(first), dyb, 0.0)
                    dm = _dot(dym, xb, NT)
                    r = dm * m
                    dacs = dacs + jnp.where(lane_h == h, jnp.sum(r, axis=1, keepdims=True), 0.0)
                    dacs_t = dacs_t + jnp.where(sub_h == h, _colsum(r), 0.0)
                    ds = ds + dm * lm
                    dxb = dxb + _dot(m, dym, TN)
                dxdt_blocks.append(dxb)
            dxdt = jnp.concatenate(dxdt_blocks, axis=1)
            dcg = dcg + _dot(ds, bg, NN)
            dbg = dbg + _dot(ds, cg, TN)
            dxbc_ref[:, DI + g * NS:DI + (g + 1) * NS] = dbg
            dxbc_ref[:, DI + NG * NS + g * NS:DI + NG * NS + (g + 1) * NS] = dcg
            dxbc_ref[:, sl] = dsk_ref[:, sl] * dyv[:, sl] + dxdt * dt_x[:, sl]
            eg = e_ref[:, sl]
            ddt = ddt + _dot_hi(dxdt * xs[:, sl], eg, NT)
            dacs = dacs + _dot_hi(dax, eg, NT)
        dacs = dacs - dacs_t.T
        ddta = _dot_hi(u_ref[...], dacs)
        ddt = ddt + ddta * arow_ref[...]
        ddtw = jnp.where(lane_h < NH, ddt * _sigmoid(pre), 0.0)
        ddtw_ref[...] = jnp.concatenate([ddtw, jnp.zeros((Q, DT_PAD - 128), F32)], axis=1).astype(ddtw_ref.dtype)
        _acc_out(da_ref, _colsum(ddta * dt), i)
        _acc_out(ddx_ref, _colsum(dyv * xs), i)
        _acc_out(ddtb_ref, _colsum(ddtw), i)

    rev = lambda c: (nc - 1 - c, 0)
    const = lambda c: (0, 0)
    return pl.pallas_call(
        body, name=name, grid=(nc,),
        in_specs=[pl.BlockSpec((Q, DI), rev), pl.BlockSpec((Q, XBC), rev),
                  pl.BlockSpec((Q, DT_PAD), lambda c: (nc - 1 - c, C_DT // DT_PAD)),
                  pl.BlockSpec((1, NS, DI), lambda c: (nc - 1 - c, 0, 0)),
                  pl.BlockSpec((1, 128), const), pl.BlockSpec((1, 128), const), pl.BlockSpec((1, DI), const),
                  pl.BlockSpec((Q, Q), const), pl.BlockSpec((Q, Q), const), pl.BlockSpec((128, DI), const),
                  pl.BlockSpec(memory_space=pl.ANY)],
        out_specs=[pl.BlockSpec((Q, XBC), rev),
                   pl.BlockSpec((Q, DT_PAD), lambda c: (nc - 1 - c, C_DT // DT_PAD)),
                   pl.BlockSpec((1, 128), const), pl.BlockSpec((1, DI), const), pl.BlockSpec((1, 128), const)],
        out_shape=[jax.ShapeDtypeStruct((L, XBC), F32), jax.ShapeDtypeStruct((L, NPROJ), BF16),
                   jax.ShapeDtypeStruct((1, 128), F32), jax.ShapeDtypeStruct((1, DI), F32),
                   jax.ShapeDtypeStruct((1, 128), F32)],
        scratch_shapes=[pltpu.VMEM((NS, DI), F32)],
        input_output_aliases={10: 1},
        compiler_params=_params(("arbitrary",), VMEM_BIG))(dy, xbc, proj, hs, dtb, arow, dsk_x, tri, triu,
                                                          expand, dproj)


def _adamw(w, g, m, v, *, name):
    R, C = w.shape
    tr = _pick(R, (256, 128, 64, 32, 16, 8))
    c1 = 1.0 - ADAM_B1 ** ADAM_STEP
    c2 = 1.0 - ADAM_B2 ** ADAM_STEP

    def body(w_ref, g_ref, m_ref, v_ref, d_ref, mo_ref, vo_ref):
        gv = g_ref[...]
        mn = ADAM_B1 * m_ref[...] + (1.0 - ADAM_B1) * gv
        vn = ADAM_B2 * v_ref[...] + (1.0 - ADAM_B2) * (gv * gv)
        mo_ref[...] = mn
        vo_ref[...] = vn
        d_ref[...] = -ADAM_LR * ((mn / c1) / (jnp.sqrt(vn / c2) + ADAM_EPS) + ADAM_WD * w_ref[...])

    spec = pl.BlockSpec((tr, C), lambda i: (i, 0))
    return pl.pallas_call(body, name=name, grid=(R // tr,), in_specs=[spec] * 4, out_specs=[spec] * 3,
                          out_shape=[jax.ShapeDtypeStruct((R, C), F32)] * 3,
                          compiler_params=_params(("parallel",)))(w, g, m, v)


def _slab_sum(recv, *, tile, name):
    rows = recv.shape[1]
    assert rows % tile == 0 and tile % 16 == 0

    def body(r_ref, o_ref):
        acc = r_ref[0].astype(F32)
        for j in range(1, N_DEV):
            acc = acc + r_ref[j].astype(F32)
        o_ref[...] = acc

    return pl.pallas_call(body, name=name, grid=(rows // tile,),
                          in_specs=[pl.BlockSpec((N_DEV, tile, D), lambda i: (0, i, 0))],
                          out_specs=pl.BlockSpec((tile, D), lambda i: (i, 0)),
                          out_shape=jax.ShapeDtypeStruct((rows, D), F32),
                          compiler_params=_params(("parallel",)))(recv)


MESH = pl.DeviceIdType.MESH


def _coords():
    return lax.axis_index("x"), lax.axis_index("y"), lax.axis_index("c")


def _peer(k):
    x, y, c = _coords()
    px = 1 - x if k & 4 else x
    py = 1 - y if k & 2 else y
    pc = 1 - c if k & 1 else c
    return (px, py, pc), 4 * px + 2 * py + pc


def _rcopy(src, dst, ssem, rsem, dev):
    return pltpu.make_async_remote_copy(src_ref=src, dst_ref=dst, send_sem=ssem, recv_sem=rsem,
                                        device_id=dev, device_id_type=MESH)


def _exchange_all(src_of, dst_slot, send_sems, recv_sems):
    x, y, c = _coords()
    me = 4 * x + 2 * y + c
    sent = []
    for k in range(1, N_DEV):
        dev, pidx = _peer(k)
        cp = _rcopy(src_of(pidx), dst_slot(me), send_sems.at[k - 1], recv_sems.at[k - 1], dev)
        cp.start()
        sent.append(cp)
    for k in range(1, N_DEV):
        dev, pidx = _peer(k)
        _rcopy(src_of(pidx), dst_slot(pidx), send_sems.at[k - 1], recv_sems.at[k - 1], dev).wait_recv()
    for cp in sent:
        cp.wait_send()


def _rows_of_slots(buf, nslots):
    rows = lax.broadcasted_iota(jnp.int32, (8, buf.shape[-1]), 0)
    out = jnp.zeros((8, buf.shape[-1]), F32)
    for j in range(nslots):
        out = out + jnp.where(rows == j, buf[j], 0.0)
    return out


def _ada_fwd(c, w_ada, b_r, *, name):
    wloc = w_ada.shape[1]

    def body(c_ref, w_ref, b_ref, mod_ref, call_ref, csrc, cbuf, psrc, pbuf, s1, r1, s2, r2):
        x, y, cc = _coords()
        me = 4 * x + 2 * y + cc
        csrc[...] = jnp.broadcast_to(c_ref[...], (8, D))
        cbuf[me] = csrc[...]
        _exchange_all(lambda p: csrc, lambda s: cbuf.at[s], s1, r1)
        call = _rows_of_slots(cbuf, N_DEV)
        call_ref[...] = call
        prod = _dot_hi(_silu(call), w_ref[...])
        for b in range(N_DEV):
            psrc[b] = jnp.broadcast_to(prod[b:b + 1, :], (8, wloc))
        pbuf[me] = psrc[me]
        _exchange_all(lambda p: psrc.at[p], lambda s: pbuf.at[s], s2, r2)
        mod_ref[...] = _rows_of_slots(pbuf, N_DEV) + b_ref[...]

    vm = pl.BlockSpec(memory_space=pltpu.VMEM)
    return pl.pallas_call(
        body, name=name, in_specs=[vm, vm, vm], out_specs=[vm, vm],
        out_shape=[jax.ShapeDtypeStruct((N_DEV, wloc), F32), jax.ShapeDtypeStruct((N_DEV, D), F32)],
        scratch_shapes=[pltpu.VMEM((8, D), F32), pltpu.VMEM((N_DEV, 8, D), F32),
                        pltpu.VMEM((N_DEV, 8, wloc), F32), pltpu.VMEM((N_DEV, 8, wloc), F32),
                        pltpu.SemaphoreType.DMA((N_DEV - 1,)), pltpu.SemaphoreType.DMA((N_DEV - 1,)),
                        pltpu.SemaphoreType.DMA((N_DEV - 1,)), pltpu.SemaphoreType.DMA((N_DEV - 1,))],
        compiler_params=pltpu.CompilerParams(vmem_limit_bytes=VMEM_BIG))(c, w_ada, b_r)


def _gather_slabs(slab, *, name):
    def body(x_ref, out_ref, send_sems, recv_sems, local_sem):
        x, y, c = _coords()
        me, sibling = (x, y, c), (x, y, 1 - c)
        chips = [(1 - x, y), (x, 1 - y), (1 - x, 1 - y)]

        def slot(px, py, pc):
            return out_ref.at[4 * px + 2 * py + pc]

        def copy(k, block, to, src=None):
            return _rcopy(slot(*block) if src is None else src, slot(*block), send_sems.at[k], recv_sems.at[k], to)

        mine = pltpu.make_async_copy(x_ref, slot(*me), local_sem)
        mine.start()
        first = [copy(0, me, sibling, src=x_ref)]
        first += [copy(1 + j, me, (*chip, c), src=x_ref) for j, chip in enumerate(chips)]
        for cp in first:
            cp.start()
        passed = [copy(4 + j, (*chip, c), sibling) for j, chip in enumerate(chips)]
        for j, chip in enumerate(chips):
            copy(1 + j, (*chip, c), me).wait_recv()
            passed[j].start()
        copy(0, sibling, me).wait_recv()
        for j, chip in enumerate(chips):
            copy(4 + j, (*chip, 1 - c), me).wait_recv()
        for cp in first + passed:
            cp.wait_send()
        mine.wait()

    anyspec = pl.BlockSpec(memory_space=pl.ANY)
    return pl.pallas_call(
        body, name=name, in_specs=[anyspec], out_specs=anyspec,
        out_shape=jax.ShapeDtypeStruct((N_DEV,) + slab.shape, slab.dtype),
        scratch_shapes=[pltpu.SemaphoreType.DMA((7,)), pltpu.SemaphoreType.DMA((7,)), pltpu.SemaphoreType.DMA],
    )(slab)


_HBM =pl.BlockSpec(memory_space=pltpu.HBM)
_SEM = pl.BlockSpec(memory_space=pltpu.SEMAPHORE)
_EFFECT = pltpu.SideEffectType.DATAFLOW_SIDE_EFFECTING


def _xchg_src(src_ref, pidx, per_peer):
    return src_ref.at[pidx] if per_peer else src_ref


def _xchg_start(src, *, per_peer, name):
    rows = src.shape[-2]
    land_shape = (N_DEV, rows, D)

    def body(src_ref, land_ref, send_sems, recv_sems, src_thru, land_thru, token):
        del src_thru, land_thru
        x, y, c = _coords()
        me = 4 * x + 2 * y + c
        for k in range(1, N_DEV):
            dev, pidx = _peer(k)
            _rcopy(_xchg_src(src_ref, pidx, per_peer), land_ref.at[me], send_sems.at[k - 1],
                   recv_sems.at[k - 1], dev).start()
        token[...] = jnp.zeros_like(token)

    return pl.pallas_call(
        body, name=name,
        out_shape=(pltpu.SemaphoreType.DMA((N_DEV - 1,)), pltpu.SemaphoreType.DMA((N_DEV - 1,)),
                   pltpu.HBM(src.shape, src.dtype), pltpu.HBM(land_shape, src.dtype),
                   jax.ShapeDtypeStruct((8, 128), F32)),
        in_specs=(_HBM, _HBM),
        out_specs=(_SEM, _SEM, _HBM, _HBM, pl.BlockSpec(memory_space=pltpu.VMEM)),
        input_output_aliases={0: 2, 1: 3},
        compiler_params=pltpu.CompilerParams(has_side_effects=_EFFECT),
    )(pltpu.with_memory_space_constraint(src, pltpu.HBM),
      pltpu.with_memory_space_constraint(lax.empty(land_shape, src.dtype), pltpu.HBM))


def _xchg_wait(started, after, *, per_peer, name):
    send_sems, recv_sems, src_thru, land_thru, _ = started

    def body(src_ref, land_ref, send_sems, recv_sems, after_ref, src_dead, got_ref):
        del after_ref, src_dead, got_ref
        for k in range(1, N_DEV):
            dev, pidx = _peer(k)
            cp = _rcopy(_xchg_src(src_ref, pidx, per_peer), land_ref.at[pidx], send_sems.at[k - 1],
                        recv_sems.at[k - 1], dev)
            cp.wait_send()
            cp.wait_recv()

    return pl.pallas_call(
        body, name=name,
        out_shape=(pltpu.HBM(src_thru.shape, src_thru.dtype), pltpu.HBM(land_thru.shape, land_thru.dtype)),
        in_specs=(_HBM, _HBM, _SEM, _SEM, pl.BlockSpec(memory_space=pl.ANY)),
        out_specs=(_HBM, _HBM),
        input_output_aliases={0: 0, 1: 1},
        compiler_params=pltpu.CompilerParams(has_side_effects=_EFFECT),
    )(src_thru, land_thru, send_sems, recv_sems, after)


def _dep(token):
    return (token, (8, 128), lambda i, j, k: (0, 0))


def _small_allsum(sv, *, name):
    def body(sv_ref, all_ref, sum_ref, send_sems, recv_sems):
        x, y, c = _coords()
        me = 4 * x + 2 * y + c
        all_ref[me] = sv_ref[...]
        _exchange_all(lambda p: sv_ref, lambda s: all_ref.at[s], send_sems, recv_sems)
        acc = all_ref[0]
        for j in range(1, N_DEV):
            acc = acc + all_ref[j]
        sum_ref[...] = acc

    vm = pl.BlockSpec(memory_space=pltpu.VMEM)
    return pl.pallas_call(
        body, name=name, in_specs=[vm], out_specs=[vm, vm],
        out_shape=[jax.ShapeDtypeStruct((N_DEV, SV_ROWS, 128), F32), jax.ShapeDtypeStruct((SV_ROWS, 128), F32)],
        scratch_shapes=[pltpu.SemaphoreType.DMA((7,)), pltpu.SemaphoreType.DMA((7,))],
    )(sv)


def _ada_bwd(call, dmod_loc, *, name):
    wloc = dmod_loc.shape[1]

    def body(c_ref, d_ref, o_ref):
        o_ref[...] = _dot_hi(_silu(c_ref[...]), d_ref[...], TN)

    vm = pl.BlockSpec(memory_space=pltpu.VMEM)
    return pl.pallas_call(body, name=name, in_specs=[vm, vm], out_specs=vm,
                          out_shape=jax.ShapeDtypeStruct((D, wloc), F32),
                          compiler_params=pltpu.CompilerParams(vmem_limit_bytes=VMEM_BIG))(call, dmod_loc)


def _pad_rows(a, rows):
    return jnp.pad(a, ((0, rows - a.shape[0]), (0, 0)))


def _reorder_in_rows(wt):
    z, xbc, dt, pool, gates = wt[0:2048], wt[2048:5120], wt[5120:5152], wt[5152:6176], wt[6176:8224]
    return jnp.concatenate([xbc, pool, z, gates, dt, jnp.zeros((DT_PAD - 32, D), wt.dtype)], axis=0)


def _restore_in_rows(d):
    return jnp.concatenate([d[C_Z:C_Z + 2048], d[C_XBC:C_XBC + XBC], d[C_DT:C_DT + 32],
                            d[C_POOL:C_POOL + 1024], d[C_GATE:C_GATE + 2048]], axis=0)


def _pack_sv(parts):
    flat = []
    for n, size in SV_PARTS:
        v = parts[n].reshape(-1).astype(F32)
        flat.append(jnp.pad(v, (0, size - v.shape[0])))
    v = jnp.concatenate(flat)
    return jnp.pad(v, (0, SV_ROWS * 128 - v.shape[0])).reshape(SV_ROWS, 128)


def _sv_get(flat, n, size):
    return flat[SV_OFF[n]:SV_OFF[n] + size]


def kernel(x, c, w_ada, b_ada, norm_mix_w, w_in, conv_w, conv_b, dt_bias, a_log, d_skip, ssd_norm_w, w_branch_ssd, pool_w, pool_scale, w_branch_pool, w_out, norm_mlp_w, w_up, w_down, norm_final_w, loss_target, m_w_ada, m_b_ada, m_norm_mix_w, m_w_in, m_conv_w, m_conv_b, m_dt_bias, m_a_log, m_d_skip, m_ssd_norm_w, m_w_branch_ssd, m_pool_w, m_pool_scale, m_w_branch_pool, m_w_out, m_norm_mlp_w, m_w_up, m_w_down, m_norm_final_w, v_w_ada, v_b_ada, v_norm_mix_w, v_w_in, v_conv_w, v_conv_b, v_dt_bias, v_a_log, v_d_skip, v_ssd_norm_w, v_w_branch_ssd, v_pool_w, v_pool_scale, v_w_branch_pool, v_w_out, v_norm_mlp_w, v_w_up, v_w_down, v_norm_final_w):
    xs_ = x[0]
    tgt = loss_target[0]
    L = xs_.shape[0]
    me = 4 * lax.axis_index("x") + 2 * lax.axis_index("y") + lax.axis_index("c")
    wloc = w_ada.shape[2]

    mod_p, c_all = _ada_fwd(c, w_ada[0], b_ada.reshape(N_DEV, wloc), name="ada_fwd")
    mod = mod_p.reshape(6, D)
    shift_m, scale_m, gate_m, shift_f, scale_f, gate_f = [mod[i:i + 1] for i in range(6)]

    conv_bits = lax.bitcast_convert_type(conv_w[0], SLAB_DT).reshape(3, D)
    slab_in = jnp.concatenate([_pad_rows(w_in[0].T.astype(SLAB_DT), IN_ROWS_P),
                               _pad_rows(conv_bits, CONV_ROWS)], axis=0)
    slab_rest = jnp.concatenate([
        w_branch_ssd[0].astype(SLAB_DT),
        pool_w[0].reshape(32, D).astype(SLAB_DT),
        w_branch_pool[0].astype(SLAB_DT),
        w_out[0].astype(SLAB_DT),
        w_up[0].T.astype(SLAB_DT),
        w_down[0].astype(SLAB_DT)], axis=0)
    gs_in = _gather_slabs(slab_in, name="gather_w_in")
    slab_rest, gs_in = lax.optimization_barrier((slab_rest, gs_in))
    rest_started = _xchg_start(slab_rest, per_peer=False, name="gather_rest_start")
    gather_token = rest_started[4]

    w_in_t = _reorder_in_rows(gs_in[:, :IN_ROWS].reshape(N_IN, D))
    conv_full = lax.bitcast_convert_type(
        gs_in[:, IN_ROWS_P:IN_ROWS_P + 3].reshape(N_DEV, 4, XBC // N_DEV, 2), F32)
    conv_full = conv_full.transpose(1, 0, 2).reshape(4, XBC)

    dtb = jnp.pad(dt_bias, ((0, 0), (0, 128 - NH)))
    arow = jnp.pad(-jnp.exp(a_log), ((0, 0), (0, 128 - NH)))
    dsk_x = jnp.repeat(d_skip, HP, axis=1)

    tm = _pick(L, (1024, 512, 256, 128))
    tkl = _pick(L, (1024, 512, 256, 128))

    h1 = _norm_fwd(xs_, norm_mix_w, scale_m, shift_m, name="norm1_fwd")
    proj = _mm(h1, w_in_t, "nt", name="in_proj", outs=[F32], tm=tm, tn=768, tk=D,
               extras=[_dep(gather_token)])
    xbc = _conv_fwd(proj, conv_full, conv_b, name="conv_fwd")
    y_ssm, hs = _ssd_fwd(xbc, proj, dtb, arow, dsk_x, name="ssd_fwd")
    yn = _gated_norm_fwd(y_ssm, proj, ssd_norm_w, name="gated_norm_fwd")

    slab_rest, gs = _xchg_wait(rest_started, yn, per_peer=False, name="gather_rest_wait")
    gs = lax.dynamic_update_slice(gs, slab_rest[None], (me, 0, 0))

    def part(n, rows):
        return gs[:, REST_OFF[n]:REST_OFF[n] + rows]

    w_bssd = part("bssd", 256).reshape(DI, D)
    w_pool = part("pool", 32).reshape(N_DEV, 4, 32, PGW).transpose(1, 0, 2, 3).reshape(POOL_W, PGW)
    w_bpool = part("bpool", 128).reshape(POOL_W, D)
    w_o = part("out", 128).reshape(D, D)
    w_up_t = part("up", 512).reshape(DFF, D)
    w_dn = part("down", 512).reshape(DFF, D)

    y_ssd = _mm(yn, w_bssd, "nn", name="branch_ssd", outs=[F32], tm=tm, tn=D, tk=DI)
    pooled = _pool_fwd(proj, name="pool_fwd")
    yp0, yp1 = _mm_pool(pooled, w_pool, pool_scale, name="pool_mix", tm=tm, transpose_w=False)
    y_pool = _mm(yp1, w_bpool, "nn", name="branch_pool", outs=[F32], tm=tm, tn=D, tk=D)
    m = _merge_fwd(y_ssd, y_pool, proj, name="merge_fwd")
    mix = _mm(m, w_o, "nn", name="out_proj", outs=[F32], tm=tm, tn=D, tk=D)
    x1, h2 = _resid_norm_fwd(xs_, mix, gate_m, norm_mlp_w, scale_f, shift_f, name="norm2_fwd")

    def relu2(acc, ex, outs):
        r = jnp.maximum(acc, 0.0)
        outs[0][...] = acc.astype(BF16)
        outs[1][...] = (r * r).astype(BF16)

    up, act = _mm(h2, w_up_t, "nt", name="mlp_up", outs=[BF16, BF16], tm=tm, tn=1024, tk=D, epilogue=relu2)
    down = _mm(act, w_dn, "nn", name="mlp_down", outs=[F32], tm=tm, tn=D, tk=2048)

    dx2, ddown, loss_p, dnwf, dgate_f = _final_bwd(x1, down, gate_f, norm_final_w.reshape(1, D), tgt,
                                                   name="final_bwd")

    def drelu2(acc, ex, outs):
        outs[0][...] = (acc * (2.0 * jnp.maximum(ex[0][...].astype(F32), 0.0))).astype(BF16)

    dup = _mm(ddown, w_dn, "nt", name="mlp_down_dx", outs=[BF16], tm=tm, tn=1024, tk=D,
              extras=[(up, (tm, 1024), lambda i, j, k: (i, j))], epilogue=drelu2)
    g_dn = _mm(act, ddown, "tn", name="mlp_down_dw", outs=[SLAB_DT], tm=1024, tn=D, tk=tkl)
    dh2 = _mm(dup, w_up_t, "nn", name="mlp_up_dx", outs=[F32], tm=tm, tn=D, tk=2048)
    g_up_t = _mm(dup, h2, "tn", name="mlp_up_dw", outs=[SLAB_DT], tm=1024, tn=D, tk=tkl)
    gslab_mlp = jnp.concatenate([g_up_t.reshape(N_DEV, 512, D), g_dn.reshape(N_DEV, 512, D)], axis=1)
    mlp_started = _xchg_start(gslab_mlp, per_peer=True, name="scatter_mlp_start")
    dx1, p2, q2, dmix, dgate_m = _norm_bwd(x1, dh2, dx2, norm_mlp_w, scale_f, mix, gate_m, name="norm2_bwd")
    dm = _mm(dmix, w_o, "nt", name="out_proj_dx", outs=[F32], tm=tm, tn=D, tk=D,
             extras=[_dep(mlp_started[4])])
    g_o = _mm(m, dmix, "tn", name="out_proj_dw", outs=[SLAB_DT], tm=D, tn=D, tk=tkl)
    dy_ssd, dy_pool, dproj = _merge_bwd(dm, y_ssd, y_pool, proj, name="merge_bwd")
    dyn = _mm(dy_ssd, w_bssd, "nt", name="branch_ssd_dx", outs=[F32], tm=tm, tn=1024, tk=D)
    g_bssd = _mm(yn, dy_ssd, "tn", name="branch_ssd_dw", outs=[SLAB_DT], tm=1024, tn=D, tk=tkl)
    dy_ssm, dproj, d_snw = _gated_norm_bwd(dyn, y_ssm, proj, ssd_norm_w, dproj, name="gated_norm_bwd")
    dxbc, dproj, d_a, d_dx, d_dtb = _ssd_bwd(dy_ssm, xbc, proj, hs, dtb, arow, dsk_x, dproj, name="ssd_bwd")
    dproj, d_cw, d_cb = _conv_bwd(proj, dxbc, conv_full, conv_b, dproj, name="conv_bwd")
    dyp1 = _mm(dy_pool, w_bpool, "nt", name="branch_pool_dx", outs=[F32], tm=tm, tn=D, tk=D)
    g_bpool = _mm(yp1, dy_pool, "tn", name="branch_pool_dw", outs=[SLAB_DT], tm=D, tn=D, tk=tkl)
    dyp0, d_ps = _pscale_bwd(dyp1, yp0, pool_scale, name="pool_scale_bwd")
    dpooled = _mm_pool(dyp0, w_pool, None, name="pool_mix_dx", tm=tm, transpose_w=True)
    g_pool = _mm_pool_tn(pooled, dyp0, name="pool_mix_dw", tk=tkl)
    gslab_mix = jnp.concatenate([
        g_bssd.reshape(N_DEV, 256, D),
        g_pool.reshape(4, N_DEV, 32, PGW).transpose(1, 0, 2, 3).reshape(N_DEV, 32, D).astype(SLAB_DT),
        g_bpool.reshape(N_DEV, 128, D),
        g_o.reshape(N_DEV, 128, D)], axis=1)
    mix_started = _xchg_start(gslab_mix, per_peer=True, name="scatter_mix_start")
    dproj = _pool_bwd(dpooled, dproj, name="pool_bwd")
    g_in_t = _mm(dproj, h1, "tn", name="in_proj_dw", outs=[SLAB_DT], tm=768, tn=D, tk=tkl,
                 extras=[_dep(mix_started[4])])
    gslab_in = jnp.pad(_restore_in_rows(g_in_t).reshape(N_DEV, IN_ROWS, D),
                       ((0, 0), (0, IN_ROWS_P - IN_ROWS), (0, 0)))
    in_started = _xchg_start(gslab_in, per_peer=True, name="scatter_in_start")
    dh1 = _mm(dproj, w_in_t, "nn", name="in_proj_dx", outs=[F32], tm=tm, tn=D, tk=768,
              extras=[_dep(in_started[4])])
    grad_x, p1, q1 = _norm_bwd(xs_, dh1, dx1, norm_mix_w, scale_m, name="norm1_bwd")

    def landed(started, after, tile, name):
        src, land = _xchg_wait(started, after, per_peer=True, name=name + "_wait")
        own = lax.dynamic_slice_in_dim(src, me, 1, axis=0)
        return _slab_sum(lax.dynamic_update_slice(land, own, (me, 0, 0)), tile=tile, name=name + "_sum")

    gsum_mlp = landed(mlp_started, grad_x, 256, "scatter_mlp")
    gsum_mix = landed(mix_started, grad_x, 272, "scatter_mix")
    gsum_in = landed(in_started, grad_x, 208, "scatter_in")

    dmod = jnp.concatenate([q1, p1 * norm_mix_w, dgate_m, q2, p2 * norm_mlp_w, dgate_f], axis=1)
    d_alog = d_a[:, :NH] * (-jnp.exp(a_log))
    sv = _pack_sv({
        "b_ada": dmod, "norm_mix_w": p1 * (1.0 + scale_m), "conv_b": d_cb, "dt_bias": d_dtb[:, :NH],
        "a_log": d_alog, "d_skip": d_dx.reshape(NH, HP).sum(axis=1), "ssd_norm_w": d_snw,
        "pool_scale": d_ps, "norm_mlp_w": p2 * (1.0 + scale_f), "norm_final_w": dnwf, "conv_w": d_cw,
        "loss": loss_p[:, :1]})
    sv_all, sv_sum = _small_allsum(sv, name="small_allsum")
    flat = sv_sum.reshape(-1)
    loss = flat[SV_OFF["loss"]]
    dmod_all = sv_all.reshape(N_DEV, SV_ROWS * 128)[:, :6 * D]
    g_w_ada = _ada_bwd(c_all, lax.dynamic_slice_in_dim(dmod_all, me * wloc, wloc, axis=1), name="ada_bwd")

    g_conv_w = lax.dynamic_slice_in_dim(_sv_get(flat, "conv_w", 4 * XBC).reshape(4, XBC),
                                        me * (XBC // N_DEV), XBC // N_DEV, axis=1)
    small = {
        "b_ada": (b_ada, m_b_ada, v_b_ada, _sv_get(flat, "b_ada", 6 * D)),
        "norm_mix_w": (norm_mix_w, m_norm_mix_w, v_norm_mix_w, _sv_get(flat, "norm_mix_w", D)),
        "conv_b": (conv_b, m_conv_b, v_conv_b, _sv_get(flat, "conv_b", XBC)),
        "dt_bias": (dt_bias, m_dt_bias, v_dt_bias, _sv_get(flat, "dt_bias", NH)),
        "a_log": (a_log, m_a_log, v_a_log, _sv_get(flat, "a_log", NH)),
        "d_skip": (d_skip, m_d_skip, v_d_skip, _sv_get(flat, "d_skip", NH)),
        "ssd_norm_w": (ssd_norm_w, m_ssd_norm_w, v_ssd_norm_w, _sv_get(flat, "ssd_norm_w", DI)),
        "pool_scale": (pool_scale, m_pool_scale, v_pool_scale, _sv_get(flat, "pool_scale", POOL_W)),
        "norm_mlp_w": (norm_mlp_w, m_norm_mlp_w, v_norm_mlp_w, _sv_get(flat, "norm_mlp_w", D)),
        "norm_final_w": (norm_final_w, m_norm_final_w, v_norm_final_w, _sv_get(flat, "norm_final_w", D)),
        "conv_w": (conv_w, m_conv_w, v_conv_w, g_conv_w),
    }
    names = list(small)
    sizes = [int(np.prod(small[n][0].shape)) for n in names]
    tot = sum(sizes)
    rows = -(-tot // 1024) * 8

    def pack(idx):
        v = jnp.concatenate([small[n][idx].reshape(-1) for n in names])
        return jnp.pad(v, (0, rows * 128 - tot)).reshape(rows, 128)

    sd, sm, sv2 = _adamw(pack(0), pack(3), pack(1), pack(2), name="adamw_small")
    small_out = {}
    off = 0
    for n, sz in zip(names, sizes):
        shp = small[n][0].shape
        small_out[n] = (small[n][3].reshape(shp), sd.reshape(-1)[off:off + sz].reshape(shp),
                        sm.reshape(-1)[off:off + sz].reshape(shp), sv2.reshape(-1)[off:off + sz].reshape(shp))
        off += sz

    def gpart(n, rows_):
        return gsum_mix[MIX_OFF[n]:MIX_OFF[n] + rows_]

    big = {
        "w_ada": (w_ada, m_w_ada, v_w_ada, g_w_ada, (D, wloc)),
        "w_in": (w_in, m_w_in, v_w_in, gsum_in[:IN_ROWS].T, (D, IN_ROWS)),
        "w_branch_ssd": (w_branch_ssd, m_w_branch_ssd, v_w_branch_ssd, gpart("bssd", 256), (256, D)),
        "pool_w": (pool_w, m_pool_w, v_pool_w, gpart("pool", 32).reshape(128, PGW), (128, PGW)),
        "w_branch_pool": (w_branch_pool, m_w_branch_pool, v_w_branch_pool, gpart("bpool", 128), (128, D)),
        "w_out": (w_out, m_w_out, v_w_out, gpart("out", 128), (128, D)),
        "w_up": (w_up, m_w_up, v_w_up, gsum_mlp[:512].T, (D, 512)),
        "w_down": (w_down, m_w_down, v_w_down, gsum_mlp[512:], (512, D)),
    }
    big_out = {}
    for n, (w, mm_, vv, g, shp2) in big.items():
        dlt, mn, vn = _adamw(w.reshape(shp2), g, mm_.reshape(shp2), vv.reshape(shp2), name="adamw_" + n)
        big_out[n] = (g.reshape(w.shape), dlt.reshape(w.shape), mn.reshape(w.shape), vn.reshape(w.shape))

    order = ["w_ada", "b_ada", "norm_mix_w", "w_in", "conv_w", "conv_b", "dt_bias", "a_log", "d_skip",
             "ssd_norm_w", "w_branch_ssd", "pool_w", "pool_scale", "w_branch_pool", "w_out", "norm_mlp_w",
             "w_up", "w_down", "norm_final_w"]
    res = {**small_out, **big_out}
    outs = [loss, grad_x.reshape(x.shape)]
    for k in range(4):
        outs += [res[n][k] for n in order]
    return tuple(outs)
```

```python
import functools

import numpy as np
import jax
import jax.numpy as jnp
from jax import lax
from jax.experimental import pallas as pl
from jax.experimental.pallas import tpu as pltpu

F32 = jnp.float32
BF16 = jnp.bfloat16
SLAB_DT = jnp.bfloat16
_MXU_DTYPE = jnp.bfloat16

N_DEV = 8
D = 1024
DI = 2048
NH = 32
HP = 64
NG = 4
NS = 128
Q = 128
XBC = DI + 2 * NG * NS
DFF = 4096
N_IN = 8224
EPS = 1e-5
POOL_W = 1024
PGW = 256

C_XBC, C_POOL, C_Z, C_GATE, C_DT = 0, 3072, 4096, 6144, 8192
DT_PAD = 256
NPROJ = C_DT + DT_PAD

IN_ROWS = N_IN // N_DEV
IN_ROWS_P = 1040
CONV_ROWS = 16
REST_PARTS = (("bssd", 256), ("pool", 32), ("bpool", 128), ("out", 128), ("up", 512), ("down", 512))
REST_OFF = {}
_o = 0
for _n, _r in REST_PARTS:
    REST_OFF[_n] = _o
    _o += _r
REST_ROWS = _o
MIX_PARTS = (("bssd", 256), ("pool", 32), ("bpool", 128), ("out", 128))
MIX_OFF = {}
_o = 0
for _n, _r in MIX_PARTS:
    MIX_OFF[_n] = _o
    _o += _r
MIX_ROWS = _o

SV_PARTS = (("b_ada", 6144), ("norm_mix_w", 1024), ("conv_b", 3072), ("dt_bias", 128), ("a_log", 128),
            ("d_skip", 128), ("ssd_norm_w", 2048), ("pool_scale", 1024), ("norm_mlp_w", 1024),
            ("norm_final_w", 1024), ("conv_w", 4 * XBC), ("loss", 128))
SV_OFF = {}
_o = 0
for _n, _r in SV_PARTS:
    SV_OFF[_n] = _o
    _o += _r
SV_ROWS = 224
assert _o <= SV_ROWS * 128

ADAM_LR, ADAM_B1, ADAM_B2, ADAM_EPS, ADAM_WD, ADAM_STEP = 0.001, 0.9, 0.999, 1e-08, 0.01, 10

VMEM_BIG = 56 * 1024 * 1024
NEG = -1e30

NN = ((1,), (0,))
NT = ((1,), (1,))
TN = ((0,), (0,))


def _dot(a, b, dims=NN):
    return lax.dot_general(a.astype(_MXU_DTYPE), b.astype(_MXU_DTYPE), (dims, ((), ())),
                           preferred_element_type=F32)


def _dot_hi(a, b, dims=NN):
    return lax.dot_general(a.astype(F32), b.astype(F32), (dims, ((), ())),
                           precision=lax.Precision.HIGHEST, preferred_element_type=F32)


def _pick(n, cands):
    for c in cands:
        if n % c == 0:
            return c
    return n


def _sigmoid(x):
    return 1.0 / (1.0 + jnp.exp(-x))


def _silu(x):
    return x * _sigmoid(x)


def _dsilu(x):
    s = _sigmoid(x)
    return s * (1.0 + x * (1.0 - s))


def _softplus(x):
    return jnp.maximum(x, 0.0) + jnp.log(1.0 + jnp.exp(-jnp.abs(x)))


def _params(sem, vmem=None):
    return pltpu.CompilerParams(dimension_semantics=sem, vmem_limit_bytes=vmem)


def _mm(a, b, mode, *, name, outs, tm, tn, tk, extras=(), epilogue=None):
    if mode == "tn":
        K, M = a.shape
        N = b.shape[1]
        a_spec = pl.BlockSpec((tk, tm), lambda i, j, k: (k, i))
        b_spec = pl.BlockSpec((tk, tn), lambda i, j, k: (k, j))
        dims = TN
    else:
        M, K = a.shape
        a_spec = pl.BlockSpec((tm, tk), lambda i, j, k: (i, k))
        if mode == "nn":
            N = b.shape[1]
            b_spec = pl.BlockSpec((tk, tn), lambda i, j, k: (k, j))
            dims = NN
        else:
            N = b.shape[0]
            b_spec = pl.BlockSpec((tn, tk), lambda i, j, k: (j, k))
            dims = NT
    assert M % tm == 0 and N % tn == 0 and K % tk == 0, (name, M, N, K, tm, tn, tk)
    nk = K // tk
    ne, no = len(extras), len(outs)
    if epilogue is None:
        def epilogue(acc, ex, out_refs):
            out_refs[0][...] = acc.astype(out_refs[0].dtype)

    def body(a_ref, b_ref, *rest):
        ex, out_refs = rest[:ne], rest[ne:ne + no]
        p = _dot(a_ref[...], b_ref[...], dims)
        if nk == 1:
            epilogue(p, ex, out_refs)
        else:
            acc = rest[-1]
            k = pl.program_id(2)

            @pl.when(k == 0)
            def _():
                acc[...] = p

            @pl.when(k > 0)
            def _():
                acc[...] += p

            @pl.when(k == nk - 1)
            def _():
                epilogue(acc[...], ex, out_refs)

    res = pl.pallas_call(
        body, name=name,
        grid=(M // tm, N // tn, nk),
        in_specs=[a_spec, b_spec] + [pl.BlockSpec(bs, im) for _, bs, im in extras],
        out_specs=[pl.BlockSpec((tm, tn), lambda i, j, k: (i, j)) for _ in outs],
        out_shape=[jax.ShapeDtypeStruct((M, N), dt) for dt in outs],
        scratch_shapes=[pltpu.VMEM((tm, tn), F32)] if nk > 1 else [],
        compiler_params=_params(("parallel", "parallel", "arbitrary"), VMEM_BIG),
    )(a, b, *[e[0] for e in extras])
    return res if no > 1 else res[0]


def _mm_pool(a, w, scale, *, name, tm, transpose_w):
    L = a.shape[0]
    dims = NT if transpose_w else NN

    def body(a_ref, w_ref, *rest):
        p = _dot(a_ref[...], w_ref[...], dims)
        if transpose_w:
            rest[0][...] = p
        else:
            s_ref, o0, o1 = rest
            o0[...] = p
            o1[...] = (p * s_ref[...]).astype(o1.dtype)

    blk = pl.BlockSpec((tm, PGW), lambda i, j: (i, j))
    in_specs = [blk, pl.BlockSpec((PGW, PGW), lambda i, j: (j, 0))]
    args = [a, w]
    if transpose_w:
        out_specs, out_shape = [blk], [jax.ShapeDtypeStruct((L, POOL_W), F32)]
    else:
        in_specs.append(pl.BlockSpec((1, PGW), lambda i, j: (0, j)))
        args.append(scale)
        out_specs = [blk, blk]
        out_shape = [jax.ShapeDtypeStruct((L, POOL_W), F32), jax.ShapeDtypeStruct((L, POOL_W), BF16)]
    res = pl.pallas_call(body, name=name, grid=(L // tm, 4), in_specs=in_specs, out_specs=out_specs,
                         out_shape=out_shape, compiler_params=_params(("parallel", "parallel")))(*args)
    return res[0] if transpose_w else res


def _mm_pool_tn(a, b, *, name, tk):
    L = a.shape[0]

    def body(a_ref, b_ref, o_ref):
        p = _dot(a_ref[...], b_ref[...], TN)

        @pl.when(pl.program_id(1) == 0)
        def _():
            o_ref[...] = p

        @pl.when(pl.program_id(1) > 0)
        def _():
            o_ref[...] += p

    blk = pl.BlockSpec((tk, PGW), lambda g, k: (k, g))
    return pl.pallas_call(body, name=name, grid=(4, L // tk), in_specs=[blk, blk],
                          out_specs=pl.BlockSpec((PGW, PGW), lambda g, k: (g, 0)),
                          out_shape=jax.ShapeDtypeStruct((POOL_W, PGW), F32),
                          compiler_params=_params(("parallel", "arbitrary")))(a, b)


def _row(tl, w, col=0):
    return pl.BlockSpec((tl, w), lambda i, c=col: (i, c))


def _vec(w, col=0):
    return pl.BlockSpec((1, w), lambda i, c=col: (0, c))


def _acc_out(ref, val, i):
    @pl.when(i == 0)
    def _():
        ref[...] = val

    @pl.when(i > 0)
    def _():
        ref[...] += val


def _colsum(v):
    return jnp.sum(v, axis=0, keepdims=True)


def _norm_fwd(x, nw, scale, shift, *, name):
    L = x.shape[0]
    tl = _pick(L, (512, 256, 128))

    def body(x_ref, nw_ref, sc_ref, sh_ref, h_ref):
        xv = x_ref[...]
        r = lax.rsqrt(jnp.mean(xv * xv, axis=-1, keepdims=True) + EPS)
        h_ref[...] = (xv * r * nw_ref[...] * (1.0 + sc_ref[...]) + sh_ref[...]).astype(h_ref.dtype)

    return pl.pallas_call(body, name=name, grid=(L // tl,),
                          in_specs=[_row(tl, D), _vec(D), _vec(D), _vec(D)], out_specs=_row(tl, D),
                          out_shape=jax.ShapeDtypeStruct((L, D), BF16),
                          compiler_params=_params(("parallel",)))(x, nw, scale, shift)


def _resid_norm_fwd(x, mix, gate, nw, scale, shift, *, name):
    L = x.shape[0]
    tl = _pick(L, (512, 256, 128))

    def body(x_ref, m_ref, g_ref, nw_ref, sc_ref, sh_ref, x1_ref, h_ref):
        xv = x_ref[...] + g_ref[...] * m_ref[...]
        x1_ref[...] = xv
        r = lax.rsqrt(jnp.mean(xv * xv, axis=-1, keepdims=True) + EPS)
        h_ref[...] = (xv * r * nw_ref[...] * (1.0 + sc_ref[...]) + sh_ref[...]).astype(h_ref.dtype)

    return pl.pallas_call(body, name=name, grid=(L // tl,),
                          in_specs=[_row(tl, D), _row(tl, D), _vec(D), _vec(D), _vec(D), _vec(D)],
                          out_specs=[_row(tl, D), _row(tl, D)],
                          out_shape=[jax.ShapeDtypeStruct((L, D), F32), jax.ShapeDtypeStruct((L, D), BF16)],
                          compiler_params=_params(("parallel",)))(x, mix, gate, nw, scale, shift)


def _final_bwd(x1, down, gate_f, nwf, tgt, *, name):
    L = x1.shape[0]
    tl = _pick(L, (512, 256, 128))

    def body(x1_ref, dn_ref, g_ref, nw_ref, t_ref, dx2_ref, dd_ref, loss_ref, dnw_ref, dg_ref):
        i = pl.program_id(0)
        dn = dn_ref[...]
        x2 = x1_ref[...] + g_ref[...] * dn
        r = lax.rsqrt(jnp.mean(x2 * x2, axis=-1, keepdims=True) + EPS)
        xh = x2 * r
        e = xh * nw_ref[...] - t_ref[...]
        part = 0.5 * jnp.sum(jnp.mean(e * e, axis=-1, keepdims=True), axis=0, keepdims=True)
        dy = e * (1.0 / D)
        g = dy * nw_ref[...]
        dx2 = r * (g - xh * jnp.mean(g * xh, axis=-1, keepdims=True))
        dx2_ref[...] = dx2
        dd_ref[...] = (dx2 * g_ref[...]).astype(dd_ref.dtype)
        _acc_out(loss_ref, jnp.broadcast_to(part, (1, 128)), i)
        _acc_out(dnw_ref, _colsum(dy * xh), i)
        _acc_out(dg_ref, _colsum(dx2 * dn), i)

    return pl.pallas_call(
        body, name=name, grid=(L // tl,),
        in_specs=[_row(tl, D), _row(tl, D), _vec(D), _vec(D), _row(tl, D)],
        out_specs=[_row(tl, D), _row(tl, D), _vec(128), _vec(D), _vec(D)],
        out_shape=[jax.ShapeDtypeStruct((L, D), F32), jax.ShapeDtypeStruct((L, D), BF16),
                   jax.ShapeDtypeStruct((1, 128), F32), jax.ShapeDtypeStruct((1, D), F32),
                   jax.ShapeDtypeStruct((1, D), F32)],
        compiler_params=_params(("arbitrary",)))(x1, down, gate_f, nwf, tgt)


def _norm_bwd(xin, dh, dres, nw, scale, mix=None, gate=None, *, name):
    L = xin.shape[0]
    tl = _pick(L, (512, 256, 128))
    with_mix = mix is not None

    def body(*refs):
        if with_mix:
            x_ref, dh_ref, dr_ref, nw_ref, sc_ref, m_ref, g_ref, dx_ref, p_ref, q_ref, dm_ref, dg_ref = refs
        else:
            x_ref, dh_ref, dr_ref, nw_ref, sc_ref, dx_ref, p_ref, q_ref = refs
        i = pl.program_id(0)
        xv = x_ref[...]
        dh_v = dh_ref[...]
        r = lax.rsqrt(jnp.mean(xv * xv, axis=-1, keepdims=True) + EPS)
        xh = xv * r
        g = dh_v * (nw_ref[...] * (1.0 + sc_ref[...]))
        dx = dr_ref[...] + r * (g - xh * jnp.mean(g * xh, axis=-1, keepdims=True))
        dx_ref[...] = dx
        _acc_out(p_ref, _colsum(dh_v * xh), i)
        _acc_out(q_ref, _colsum(dh_v), i)
        if with_mix:
            dm_ref[...] = (dx * g_ref[...]).astype(dm_ref.dtype)
            _acc_out(dg_ref, _colsum(dx * m_ref[...]), i)

    in_specs = [_row(tl, D), _row(tl, D), _row(tl, D), _vec(D), _vec(D)]
    out_specs = [_row(tl, D), _vec(D), _vec(D)]
    out_shape = [jax.ShapeDtypeStruct((L, D), F32), jax.ShapeDtypeStruct((1, D), F32),
                 jax.ShapeDtypeStruct((1, D), F32)]
    args = [xin, dh, dres, nw, scale]
    if with_mix:
        in_specs += [_row(tl, D), _vec(D)]
        out_specs += [_row(tl, D), _vec(D)]
        out_shape += [jax.ShapeDtypeStruct((L, D), BF16), jax.ShapeDtypeStruct((1, D), F32)]
        args += [mix, gate]
    return pl.pallas_call(body, name=name, grid=(L // tl,), in_specs=in_specs, out_specs=out_specs,
                          out_shape=out_shape, compiler_params=_params(("arbitrary",)))(*args)


def _merge_fwd(y_ssd, y_pool, proj, *, name):
    L = y_ssd.shape[0]
    tl = _pick(L, (512, 256, 128))

    def body(a_ref, b_ref, gl_ref, m_ref):
        s = _sigmoid(gl_ref[...])
        m_ref[...] = (s[:, :D] * a_ref[...] + s[:, D:] * b_ref[...]).astype(m_ref.dtype)

    return pl.pallas_call(body, name=name, grid=(L // tl,),
                          in_specs=[_row(tl, D), _row(tl, D), _row(tl, 2 * D, C_GATE // (2 * D))],
                          out_specs=_row(tl, D), out_shape=jax.ShapeDtypeStruct((L, D), BF16),
                          compiler_params=_params(("parallel",)))(y_ssd, y_pool, proj)


def _merge_bwd(dm, y_ssd, y_pool, proj, *, name):
    L = dm.shape[0]
    tl = _pick(L, (512, 256, 128))

    def body(dm_ref, a_ref, b_ref, gl_ref, da_ref, db_ref, dgl_ref):
        s = _sigmoid(gl_ref[...])
        dmv = dm_ref[...]
        s1, s2 = s[:, :D], s[:, D:]
        da_ref[...] = (dmv * s1).astype(da_ref.dtype)
        db_ref[...] = (dmv * s2).astype(db_ref.dtype)
        dgl_ref[:, :D] = (dmv * a_ref[...] * s1 * (1.0 - s1)).astype(dgl_ref.dtype)
        dgl_ref[:, D:] = (dmv * b_ref[...] * s2 * (1.0 - s2)).astype(dgl_ref.dtype)

    gcol = C_GATE // (2 * D)
    return pl.pallas_call(
        body, name=name, grid=(L // tl,),
        in_specs=[_row(tl, D), _row(tl, D), _row(tl, D), _row(tl, 2 * D, gcol)],
        out_specs=[_row(tl, D), _row(tl, D), _row(tl, 2 * D, gcol)],
        out_shape=[jax.ShapeDtypeStruct((L, D), BF16), jax.ShapeDtypeStruct((L, D), BF16),
                   jax.ShapeDtypeStruct((L, NPROJ), BF16)],
        compiler_params=_params(("parallel",)))(dm, y_ssd, y_pool, proj)


GW = DI // NG


def _gated_norm_fwd(y, proj, w, *, name):
    L = y.shape[0]
    tl = _pick(L, (256, 128))

    def body(y_ref, z_ref, w_ref, o_ref):
        yg = y_ref[...] * _silu(z_ref[...])
        for k in range(NG):
            seg = yg[:, k * GW:(k + 1) * GW]
            r = lax.rsqrt(jnp.mean(seg * seg, axis=-1, keepdims=True) + EPS)
            o_ref[:, k * GW:(k + 1) * GW] = (seg * r * w_ref[:, k * GW:(k + 1) * GW]).astype(o_ref.dtype)

    return pl.pallas_call(body, name=name, grid=(L // tl,),
                          in_specs=[_row(tl, DI), _row(tl, DI, C_Z // DI), _vec(DI)],
                          out_specs=_row(tl, DI), out_shape=jax.ShapeDtypeStruct((L, DI), BF16),
                          compiler_params=_params(("parallel",)))(y, proj, w)


def _gated_norm_bwd(dyn, y, proj, w, dproj, *, name):
    L = y.shape[0]
    tl = _pick(L, (256, 128))

    def body(dyn_ref, y_ref, z_ref, w_ref, dp_in, dy_ref, dz_ref, dw_ref):
        del dp_in
        i = pl.program_id(0)
        zv = z_ref[...]
        yv = y_ref[...]
        sz = _silu(zv)
        yg = yv * sz
        dsz = _dsilu(zv)
        dws = []
        for k in range(NG):
            sl = slice(k * GW, (k + 1) * GW)
            seg = yg[:, sl]
            r = lax.rsqrt(jnp.mean(seg * seg, axis=-1, keepdims=True) + EPS)
            sh = seg * r
            dn = dyn_ref[:, sl]
            g = dn * w_ref[:, sl]
            dyg = r * (g - sh * jnp.mean(g * sh, axis=-1, keepdims=True))
            dy_ref[:, sl] = dyg * sz[:, sl]
            dz_ref[:, sl] = (dyg * yv[:, sl] * dsz[:, sl]).astype(dz_ref.dtype)
            dws.append(_colsum(dn * sh))
        _acc_out(dw_ref, jnp.concatenate(dws, axis=1), i)

    zc = C_Z // DI
    res = pl.pallas_call(
        body, name=name, grid=(L // tl,),
        in_specs=[_row(tl, DI), _row(tl, DI), _row(tl, DI, zc), _vec(DI), pl.BlockSpec(memory_space=pl.ANY)],
        out_specs=[_row(tl, DI), _row(tl, DI, zc), _vec(DI)],
        out_shape=[jax.ShapeDtypeStruct((L, DI), F32), jax.ShapeDtypeStruct((L, NPROJ), BF16),
                   jax.ShapeDtypeStruct((1, DI), F32)],
        input_output_aliases={4: 1},
        compiler_params=_params(("arbitrary",)))(dyn, y, proj, w, dproj)
    return res


def _pscale_bwd(dyp1, yp0, scale, *, name):
    L = dyp1.shape[0]
    tl = _pick(L, (512, 256, 128))

    def body(d_ref, y_ref, s_ref, o_ref, ds_ref):
        dv = d_ref[...]
        o_ref[...] = (dv * s_ref[...]).astype(o_ref.dtype)
        _acc_out(ds_ref, _colsum(dv * y_ref[...]), pl.program_id(0))

    return pl.pallas_call(body, name=name, grid=(L // tl,),
                          in_specs=[_row(tl, POOL_W), _row(tl, POOL_W), _vec(POOL_W)],
                          out_specs=[_row(tl, POOL_W), _vec(POOL_W)],
                          out_shape=[jax.ShapeDtypeStruct((L, POOL_W), BF16),
                                     jax.ShapeDtypeStruct((1, POOL_W), F32)],
                          compiler_params=_params(("arbitrary",)))(dyp1, yp0, scale)


CONV_CB = 128
HALO = 16


def _time_chunk(L):
    return _pick(L, (256, 128))


def _conv_fwd(proj, w, b, *, name):
    L = proj.shape[0]
    rc = _time_chunk(L)
    n = L // rc

    def body(x_ref, w_ref, b_ref, o_ref, pad):
        pad[0:HALO, :] = jnp.zeros((HALO, CONV_CB), F32)
        wv = w_ref[...]
        bv = b_ref[...]

        def fill(i, c):
            r0 = pl.multiple_of(i * rc, rc)
            pad[pl.ds(r0 + HALO, rc), :] = x_ref[pl.ds(r0, rc), :]
            return c

        lax.fori_loop(0, n, fill, 0)

        def step(i, c):
            r0 = pl.multiple_of(i * rc, rc)
            ext = pad[pl.ds(r0, rc + HALO), :]
            acc = bv + ext * wv[3:4]
            for j in (1, 2, 3):
                acc = acc + pltpu.roll(ext, j, 0) * wv[3 - j:4 - j]
            acc = acc[HALO:]
            o_ref[pl.ds(r0, rc), :] = acc * _sigmoid(acc)
            return c

        lax.fori_loop(0, n, step, 0)

    return pl.pallas_call(
        body, name=name, grid=(XBC // CONV_CB,),
        in_specs=[pl.BlockSpec((L, CONV_CB), lambda j: (0, j + C_XBC // CONV_CB)),
                  pl.BlockSpec((4, CONV_CB), lambda j: (0, j)), pl.BlockSpec((1, CONV_CB), lambda j: (0, j))],
        out_specs=pl.BlockSpec((L, CONV_CB), lambda j: (0, j)),
        out_shape=jax.ShapeDtypeStruct((L, XBC), F32),
        scratch_shapes=[pltpu.VMEM((L + HALO, CONV_CB), F32)],
        compiler_params=_params(("parallel",), VMEM_BIG))(proj, w, b)


def _conv_bwd(proj, dy, w, b, dproj, *, name):
    L = proj.shape[0]
    rc = _time_chunk(L)
    n = L // rc

    def body(x_ref, dy_ref, w_ref, b_ref, dp_in, dx_ref, dw_ref, db_ref, pad, dpad):
        del dp_in
        pad[0:HALO, :] = jnp.zeros((HALO, CONV_CB), F32)
        dpad[L:L + HALO, :] = jnp.zeros((HALO, CONV_CB), F32)
        wv = w_ref[...]
        bv = b_ref[...]

        def fill(i, c):
            r0 = pl.multiple_of(i * rc, rc)
            pad[pl.ds(r0 + HALO, rc), :] = x_ref[pl.ds(r0, rc), :]
            return c

        lax.fori_loop(0, n, fill, 0)

        def p1(i, carry):
            r0 = pl.multiple_of(i * rc, rc)
            ext = pad[pl.ds(r0, rc + HALO), :]
            xk = [ext[HALO:]] + [pltpu.roll(ext, j, 0)[HALO:] for j in (1, 2, 3)]
            pre = bv
            for j in range(4):
                pre = pre + xk[j] * wv[3 - j:4 - j]
            dpre = dy_ref[pl.ds(r0, rc), :] * _dsilu(pre)
            dpad[pl.ds(r0, rc), :] = dpre
            db, d0, d1, d2, d3 = carry
            return (db + _colsum(dpre), d0 + _colsum(dpre * xk[3]), d1 + _colsum(dpre * xk[2]),
                    d2 + _colsum(dpre * xk[1]), d3 + _colsum(dpre * xk[0]))

        z = jnp.zeros((1, CONV_CB), F32)
        db, d0, d1, d2, d3 = lax.fori_loop(0, n, p1, (z, z, z, z, z))
        db_ref[...] = db
        dw_ref[...] = jnp.concatenate([d0, d1, d2, d3], axis=0)

        def p2(i, c):
            r0 = pl.multiple_of(i * rc, rc)
            ext = dpad[pl.ds(r0, rc + HALO), :]
            acc = ext * wv[3:4]
            for j in (1, 2, 3):
                acc = acc + pltpu.roll(ext, rc + HALO - j, 0) * wv[3 - j:4 - j]
            dx_ref[pl.ds(r0, rc), :] = acc[:rc].astype(dx_ref.dtype)
            return c

        lax.fori_loop(0, n, p2, 0)

    nb = XBC // CONV_CB
    return pl.pallas_call(
        body, name=name, grid=(nb,),
        in_specs=[pl.BlockSpec((L, CONV_CB), lambda j: (0, j + C_XBC // CONV_CB)),
                  pl.BlockSpec((L, CONV_CB), lambda j: (0, j)),
                  pl.BlockSpec((4, CONV_CB), lambda j: (0, j)), pl.BlockSpec((1, CONV_CB), lambda j: (0, j)),
                  pl.BlockSpec(memory_space=pl.ANY)],
        out_specs=[pl.BlockSpec((L, CONV_CB), lambda j: (0, j + C_XBC // CONV_CB)),
                   pl.BlockSpec((4, CONV_CB), lambda j: (0, j)), pl.BlockSpec((1, CONV_CB), lambda j: (0, j))],
        out_shape=[jax.ShapeDtypeStruct((L, NPROJ), BF16), jax.ShapeDtypeStruct((4, XBC), F32),
                   jax.ShapeDtypeStruct((1, XBC), F32)],
        scratch_shapes=[pltpu.VMEM((L + HALO, CONV_CB), F32), pltpu.VMEM((L + HALO, CONV_CB), F32)],
        input_output_aliases={4: 0},
        compiler_params=_params(("parallel",), VMEM_BIG))(proj, dy, w, b, dproj)


def _pool_fwd(proj, *, name):
    L = proj.shape[0]
    rc = _time_chunk(L)
    n = L // rc

    def body(x_ref, o_ref, pad):
        g = pl.program_id(0)
        pad[0:HALO, :] = jnp.zeros((HALO, PGW), F32)

        def fill(i, c):
            r0 = pl.multiple_of(i * rc, rc)
            pad[pl.ds(r0 + HALO, rc), :] = x_ref[pl.ds(r0, rc), :]
            return c

        lax.fori_loop(0, n, fill, 0)
        rows = lax.broadcasted_iota(jnp.int32, (rc, PGW), 0)

        for gi in range(4):
            win = 2 << gi

            @pl.when(g == gi)
            def _(gi=gi, win=win):
                def step(i, c):
                    r0 = pl.multiple_of(i * rc, rc)
                    ext = pad[pl.ds(r0, rc + HALO), :]
                    s = ext
                    sh = 1
                    while sh < win:
                        s = s + pltpu.roll(s, sh, 0)
                        sh *= 2
                    cnt = jnp.minimum(rows + (r0 + 1), win).astype(F32)
                    o_ref[pl.ds(r0, rc), :] = (s[HALO:] / cnt - ext[HALO:]).astype(o_ref.dtype)
                    return c

                lax.fori_loop(0, n, step, 0)

    return pl.pallas_call(
        body, name=name, grid=(4,),
        in_specs=[pl.BlockSpec((L, PGW), lambda j: (0, j + C_POOL // PGW))],
        out_specs=pl.BlockSpec((L, PGW), lambda j: (0, j)),
        out_shape=jax.ShapeDtypeStruct((L, POOL_W), BF16),
        scratch_shapes=[pltpu.VMEM((L + HALO, PGW), F32)],
        compiler_params=_params(("parallel",), VMEM_BIG))(proj)


def _pool_bwd(dpooled, dproj, *, name):
    L = dpooled.shape[0]
    rc = _time_chunk(L)
    n = L // rc

    def body(d_ref, dp_in, o_ref, pad):
        del dp_in
        g = pl.program_id(0)
        pad[L:L + HALO, :] = jnp.zeros((HALO, PGW), F32)
        rows = lax.broadcasted_iota(jnp.int32, (rc, PGW), 0)

        for gi in range(4):
            win = 2 << gi

            @pl.when(g == gi)
            def _(gi=gi, win=win):
                def fill(i, c):
                    r0 = pl.multiple_of(i * rc, rc)
                    cnt = jnp.minimum(rows + (r0 + 1), win).astype(F32)
                    pad[pl.ds(r0, rc), :] = d_ref[pl.ds(r0, rc), :] / cnt
                    return c

                lax.fori_loop(0, n, fill, 0)

                def step(i, c):
                    r0 = pl.multiple_of(i * rc, rc)
                    s = pad[pl.ds(r0, rc + HALO), :]
                    sh = 1
                    while sh < win:
                        s = s + pltpu.roll(s, rc + HALO - sh, 0)
                        sh *= 2
                    o_ref[pl.ds(r0, rc), :] = (s[:rc] - d_ref[pl.ds(r0, rc), :]).astype(o_ref.dtype)
                    return c

                lax.fori_loop(0, n, step, 0)

    return pl.pallas_call(
        body, name=name, grid=(4,),
        in_specs=[pl.BlockSpec((L, PGW), lambda j: (0, j)), pl.BlockSpec(memory_space=pl.ANY)],
        out_specs=pl.BlockSpec((L, PGW), lambda j: (0, j + C_POOL // PGW)),
        out_shape=jax.ShapeDtypeStruct((L, NPROJ), BF16),
        scratch_shapes=[pltpu.VMEM((L + HALO, PGW), F32)],
        input_output_aliases={1: 0},
        compiler_params=_params(("parallel",), VMEM_BIG))(dpooled, dproj)


_SPLIT_DT = jnp.bfloat16


def _ssd_consts():
    tri = np.tril(np.ones((Q, Q), np.float32))
    exp = np.zeros((128, DI), np.float32)
    for h in range(NH):
        exp[h, h * HP:(h + 1) * HP] = 1.0
    exp2 = np.concatenate([exp, exp], axis=0)
    return (jnp.asarray(tri, dtype=_SPLIT_DT), jnp.asarray(tri.T.copy(), dtype=_SPLIT_DT),
            jnp.asarray(exp2, dtype=_SPLIT_DT))


def _split(v, n):
    parts, r = [], v
    for _ in range(n):
        p = r.astype(_SPLIT_DT)
        parts.append(p)
        r = r - p.astype(F32)
    return parts


def _bdot(a, b, dims):
    return lax.dot_general(a, b, (dims, ((), ())), preferred_element_type=F32)


def _tri_sum(t_ref, v):
    r = _bdot(t_ref[...], jnp.concatenate(_split(v, 3), axis=1), NN)
    return r[:, :128] + r[:, 128:256] + r[:, 256:]


def _expand(v, e2_ref):
    return _bdot(jnp.concatenate(_split(v, 2), axis=1), e2_ref[...], NN)


def _reduce_heads(vals, eg):
    parts = []
    for v in vals:
        parts += _split(v, 2)
    r = _bdot(jnp.concatenate(parts, axis=0), eg, NT)
    return [r[2 * i * Q:(2 * i + 1) * Q] + r[(2 * i + 1) * Q:(2 * i + 2) * Q] for i in range(len(vals))]


def _ssd_common(xbc_ref, dtw_ref, dtb_ref, arow_ref, t_ref, e_ref):
    pre = dtw_ref[:, :128] + dtb_ref[...]
    dt = _softplus(pre)
    acs = _tri_sum(t_ref, dt * arow_ref[...])
    acs_x = _expand(acs, e_ref)
    dt_x = _expand(dt, e_ref)
    xs = xbc_ref[:, 0:DI]
    return pre, dt, acs, acs.T, acs_x, dt_x, xs


def _ssd_fwd(xbc, proj, dtb, arow, dsk_x, *, name):
    L = xbc.shape[0]
    nc = L // Q
    tri, _, expand = _ssd_consts()

    def body(xbc_ref, dtw_ref, dtb_ref, arow_ref, dsk_ref, t_ref, e_ref, y_ref, hs_ref, h_scr):
        @pl.when(pl.program_id(0) == 0)
        def _():
            h_scr[...] = jnp.zeros_like(h_scr)

        _, dt, acs, acs_t, acs_x, dt_x, xs = _ssd_common(xbc_ref, dtw_ref, dtb_ref, arow_ref, t_ref, e_ref)
        xdt = xs * dt_x
        eacs = jnp.exp(acs_x)
        acs_last = acs_x[Q - 1:Q, :]
        dec = jnp.exp(acs_last - acs_x)
        hs_ref[0] = h_scr[...]
        causal = lax.broadcasted_iota(jnp.int32, (Q, Q), 0) >= lax.broadcasted_iota(jnp.int32, (Q, Q), 1)
        first = lax.broadcasted_iota(jnp.int32, (Q, 128), 1) < HP
        for g in range(NG):
            bg = xbc_ref[:, DI + g * NS:DI + (g + 1) * NS]
            cg = xbc_ref[:, DI + NG * NS + g * NS:DI + NG * NS + (g + 1) * NS]
            s = _dot(cg, bg, NT)
            sl = slice(g * GW, (g + 1) * GW)
            hg = h_scr[:, sl]
            yoff = _dot(cg, hg, NN) * eacs[:, sl]
            st = _dot(bg, xdt[:, sl] * dec[:, sl], TN)
            h_scr[:, sl] = hg * eacs[Q - 1:Q, sl] + st
            for j in range(4):
                lo = g * GW + j * 128
                xb = xdt[:, lo:lo + 128]
                yp = yoff[:, j * 128:(j + 1) * 128] + dsk_ref[:, lo:lo + 128] * xs[:, lo:lo + 128]
                for e in range(2):
                    h = g * 8 + j * 2 + e
                    lm = jnp.exp(jnp.where(causal, acs[:, h:h + 1] - acs_t[h:h + 1, :], NEG))
                    xm = jnp.where(first if e == 0 else jnp.logical_not(first), xb, 0.0)
                    yp = yp + _dot(s * lm, xm, NN)
                y_ref[:, lo:lo + 128] = yp

    return pl.pallas_call(
        body, name=name, grid=(nc,),
        in_specs=[pl.BlockSpec((Q, XBC), lambda c: (c, 0)),
                  pl.BlockSpec((Q, DT_PAD), lambda c: (c, C_DT // DT_PAD)),
                  pl.BlockSpec((1, 128), lambda c: (0, 0)), pl.BlockSpec((1, 128), lambda c: (0, 0)),
                  pl.BlockSpec((1, DI), lambda c: (0, 0)),
                  pl.BlockSpec((Q, Q), lambda c: (0, 0)), pl.BlockSpec((256, DI), lambda c: (0, 0))],
        out_specs=[pl.BlockSpec((Q, DI), lambda c: (c, 0)), pl.BlockSpec((1, NS, DI), lambda c: (c, 0, 0))],
        out_shape=[jax.ShapeDtypeStruct((L, DI), F32), jax.ShapeDtypeStruct((nc, NS, DI), F32)],
        scratch_shapes=[pltpu.VMEM((NS, DI), F32)],
        compiler_params=_params(("arbitrary",), VMEM_BIG))(xbc, proj, dtb, arow, dsk_x, tri, expand)


def _ssd_bwd(dy, xbc, proj, hs, dtb, arow, dsk_x, dproj, *, name):
    L = xbc.shape[0]
    nc = L // Q
    tri, triu, expand = _ssd_consts()

    def body(dy_ref, xbc_ref, dtw_ref, hs_ref, dtb_ref, arow_ref, dsk_ref, t_ref, u_ref, e_ref, dp_in,
             dxbc_ref, ddtw_ref, da_ref, ddx_ref, ddtb_ref, dh_scr):
        del dp_in
        i = pl.program_id(0)

        @pl.when(i == 0)
        def _():
            dh_scr[...] = jnp.zeros_like(dh_scr)

        pre, dt, acs, acs_t, acs_x, dt_x, xs = _ssd_common(xbc_ref, dtw_ref, dtb_ref, arow_ref, t_ref, e_ref)
        dyv = dy_ref[...]
        xdt = xs * dt_x
        eacs = jnp.exp(acs_x)
        acs_last = acs_x[Q - 1:Q, :]
        dec = jnp.exp(acs_last - acs_x)
        gy = dyv * eacs
        causal = lax.broadcasted_iota(jnp.int32, (Q, Q), 0) >= lax.broadcasted_iota(jnp.int32, (Q, Q), 1)
        first = lax.broadcasted_iota(jnp.int32, (Q, 128), 1) < HP
        lane_h = lax.broadcasted_iota(jnp.int32, (Q, 128), 1)
        sub_h = lax.broadcasted_iota(jnp.int32, (128, Q), 0)
        last_row = lax.broadcasted_iota(jnp.int32, (Q, GW), 0) == Q - 1
        dacs = jnp.zeros((Q, 128), F32)
        dacs_t = jnp.zeros((128, Q), F32)
        ddt = jnp.zeros((Q, 128), F32)
        for g in range(NG):
            bg = xbc_ref[:, DI + g * NS:DI + (g + 1) * NS]
            cg = xbc_ref[:, DI + NG * NS + g * NS:DI + NG * NS + (g + 1) * NS]
            s = _dot(cg, bg, NT)
            sl = slice(g * GW, (g + 1) * GW)
            hg = hs_ref[0, :, sl]
            dhn = dh_scr[:, sl]
            eal = eacs[Q - 1:Q, sl]
            gg = gy[:, sl]
            dax = gg * _dot(cg, hg, NN)
            dcg = _dot(gg, hg, NT)
            dh_scr[:, sl] = _dot(cg, gg, TN) + dhn * eal
            dal = eal * _colsum(dhn * hg)
            xdd = xdt[:, sl] * dec[:, sl]
            dbg = _dot(xdd, dhn, NT)
            wv = _dot(bg, dhn, NN)
            dd = wv * xdd
            dax = dax - dd
            dal = dal + _colsum(dd)
            dax = dax + jnp.where(last_row, dal, 0.0)
            dxdt_g = wv * dec[:, sl]
            ds = jnp.zeros((Q, Q), F32)
            dxdt_blocks = []
            for j in range(4):
                lo = g * GW + j * 128
                xb = xdt[:, lo:lo + 128]
                dyb = dyv[:, lo:lo + 128]
                dxb = dxdt_g[:, j * 128:(j + 1) * 128]
                for e in range(2):
                    h = g * 8 + j * 2 + e
                    lm = jnp.exp(jnp.where(causal, acs[:, h:h + 1] - acs_t[h:h + 1, :], NEG))
                    m = s * lm
                    dym = jnp.where(first if e == 0 else jnp.logical_not(first), dyb, 0.0)
                    dm = _dot(dym, xb, NT)
                    r = dm * m
                    dacs = dacs + jnp.where(lane_h == h, jnp.sum(r, axis=1, keepdims=True), 0.0)
                    dacs_t = dacs_t + jnp.where(sub_h == h, _colsum(r), 0.0)
                    ds = ds + dm * lm
                    dxb = dxb + _dot(m, dym, TN)
                dxdt_blocks.append(dxb)
            dxdt = jnp.concatenate(dxdt_blocks, axis=1)
            dcg = dcg + _dot(ds, bg, NN)
            dbg = dbg + _dot(ds, cg, TN)
            dxbc_ref[:, DI + g * NS:DI + (g + 1) * NS] = dbg
            dxbc_ref[:, DI + NG * NS + g * NS:DI + NG * NS + (g + 1) * NS] = dcg
            dxbc_ref[:, sl] = dsk_ref[:, sl] * dyv[:, sl] + dxdt * dt_x[:, sl]
            ddt_g, dacs_g = _reduce_heads([dxdt * xs[:, sl], dax], e_ref[0:128, sl])
            ddt = ddt + ddt_g
            dacs = dacs + dacs_g
        dacs = dacs - dacs_t.T
        ddta = _tri_sum(u_ref, dacs)
        ddt = ddt + ddta * arow_ref[...]
        ddtw = jnp.where(lane_h < NH, ddt * _sigmoid(pre), 0.0)
        ddtw_ref[...] = jnp.concatenate([ddtw, jnp.zeros((Q, DT_PAD - 128), F32)], axis=1).astype(ddtw_ref.dtype)
        _acc_out(da_ref, _colsum(ddta * dt), i)
        _acc_out(ddx_ref, _colsum(dyv * xs), i)
        _acc_out(ddtb_ref, _colsum(ddtw), i)

    rev = lambda c: (nc - 1 - c, 0)
    const = lambda c: (0, 0)
    return pl.pallas_call(
        body, name=name, grid=(nc,),
        in_specs=[pl.BlockSpec((Q, DI), rev), pl.BlockSpec((Q, XBC), rev),
                  pl.BlockSpec((Q, DT_PAD), lambda c: (nc - 1 - c, C_DT // DT_PAD)),
                  pl.BlockSpec((1, NS, DI), lambda c: (nc - 1 - c, 0, 0)),
                  pl.BlockSpec((1, 128), const), pl.BlockSpec((1, 128), const), pl.BlockSpec((1, DI), const),
                  pl.BlockSpec((Q, Q), const), pl.BlockSpec((Q, Q), const), pl.BlockSpec((256, DI), const),
                  pl.BlockSpec(memory_space=pl.ANY)],
        out_specs=[pl.BlockSpec((Q, XBC), rev),
                   pl.BlockSpec((Q, DT_PAD), lambda c: (nc - 1 - c, C_DT // DT_PAD)),
                   pl.BlockSpec((1, 128), const), pl.BlockSpec((1, DI), const), pl.BlockSpec((1, 128), const)],
        out_shape=[jax.ShapeDtypeStruct((L, XBC), F32), jax.ShapeDtypeStruct((L, NPROJ), BF16),
                   jax.ShapeDtypeStruct((1, 128), F32), jax.ShapeDtypeStruct((1, DI), F32),
                   jax.ShapeDtypeStruct((1, 128), F32)],
        scratch_shapes=[pltpu.VMEM((NS, DI), F32)],
        input_output_aliases={10: 1},
        compiler_params=_params(("arbitrary",), VMEM_BIG))(dy, xbc, proj, hs, dtb, arow, dsk_x, tri, triu,
                                                          expand, dproj)


def _adamw(w, g, m, v, *, name):
    R, C = w.shape
    tr = _pick(R, (256, 128, 64, 32, 16, 8))
    if tr == R and R > 256:
        tr = 256
    c1 = 1.0 - ADAM_B1 ** ADAM_STEP
    c2 = 1.0 - ADAM_B2 ** ADAM_STEP

    def body(w_ref, g_ref, m_ref, v_ref, d_ref, mo_ref, vo_ref):
        gv = g_ref[...]
        mn = ADAM_B1 * m_ref[...] + (1.0 - ADAM_B1) * gv
        vn = ADAM_B2 * v_ref[...] + (1.0 - ADAM_B2) * (gv * gv)
        mo_ref[...] = mn
        vo_ref[...] = vn
        d_ref[...] = -ADAM_LR * ((mn / c1) / (jnp.sqrt(vn / c2) + ADAM_EPS) + ADAM_WD * w_ref[...])

    spec = pl.BlockSpec((tr, C), lambda i: (i, 0))
    return pl.pallas_call(body, name=name, grid=(pl.cdiv(R, tr),), in_specs=[spec] * 4, out_specs=[spec] * 3,
                          out_shape=[jax.ShapeDtypeStruct((R, C), F32)] * 3,
                          compiler_params=_params(("parallel",)))(w, g, m, v)


def _slab_sum(recv, *, tile, name):
    rows = recv.shape[1]
    assert rows % tile == 0 and tile % 16 == 0

    def body(r_ref, o_ref):
        acc = r_ref[0].astype(F32)
        for j in range(1, N_DEV):
            acc = acc + r_ref[j].astype(F32)
        o_ref[...] = acc

    return pl.pallas_call(body, name=name, grid=(rows // tile,),
                          in_specs=[pl.BlockSpec((N_DEV, tile, D), lambda i: (0, i, 0))],
                          out_specs=pl.BlockSpec((tile, D), lambda i: (i, 0)),
                          out_shape=jax.ShapeDtypeStruct((rows, D), F32),
                          compiler_params=_params(("parallel",)))(recv)


MESH = pl.DeviceIdType.MESH


def _coords():
    return lax.axis_index("x"), lax.axis_index("y"), lax.axis_index("c")


def _peer(k):
    x, y, c = _coords()
    px = 1 - x if k & 4 else x
    py = 1 - y if k & 2 else y
    pc = 1 - c if k & 1 else c
    return (px, py, pc), 4 * px + 2 * py + pc


def _rcopy(src, dst, ssem, rsem, dev):
    return pltpu.make_async_remote_copy(src_ref=src, dst_ref=dst, send_sem=ssem, recv_sem=rsem,
                                        device_id=dev, device_id_type=MESH)


def _exchange_all(src_of, dst_slot, send_sems, recv_sems):
    x, y, c = _coords()
    me = 4 * x + 2 * y + c
    sent = []
    for k in range(1, N_DEV):
        dev, pidx = _peer(k)
        cp = _rcopy(src_of(pidx), dst_slot(me), send_sems.at[k - 1], recv_sems.at[k - 1], dev)
        cp.start()
        sent.append(cp)
    for k in range(1, N_DEV):
        dev, pidx = _peer(k)
        _rcopy(src_of(pidx), dst_slot(pidx), send_sems.at[k - 1], recv_sems.at[k - 1], dev).wait_recv()
    for cp in sent:
        cp.wait_send()


def _rows_of_slots(buf, nslots):
    rows = lax.broadcasted_iota(jnp.int32, (8, buf.shape[-1]), 0)
    out = jnp.zeros((8, buf.shape[-1]), F32)
    for j in range(nslots):
        out = out + jnp.where(rows == j, buf[j], 0.0)
    return out


def _ada_fwd(c, w_ada, b_r, *, name):
    wloc = w_ada.shape[1]

    def body(c_ref, w_ref, b_ref, mod_ref, call_ref, csrc, cbuf, psrc, pbuf, s1, r1, s2, r2):
        x, y, cc = _coords()
        me = 4 * x + 2 * y + cc
        csrc[...] = jnp.broadcast_to(c_ref[...], (8, D))
        cbuf[me] = csrc[...]
        _exchange_all(lambda p: csrc, lambda s: cbuf.at[s], s1, r1)
        call = _rows_of_slots(cbuf, N_DEV)
        call_ref[...] = call
        prod = _dot_hi(_silu(call), w_ref[...])
        for b in range(N_DEV):
            psrc[b] = jnp.broadcast_to(prod[b:b + 1, :], (8, wloc))
        pbuf[me] = psrc[me]
        _exchange_all(lambda p: psrc.at[p], lambda s: pbuf.at[s], s2, r2)
        mod_ref[...] = _rows_of_slots(pbuf, N_DEV) + b_ref[...]

    vm = pl.BlockSpec(memory_space=pltpu.VMEM)
    return pl.pallas_call(
        body, name=name, in_specs=[vm, vm, vm], out_specs=[vm, vm],
        out_shape=[jax.ShapeDtypeStruct((N_DEV, wloc), F32), jax.ShapeDtypeStruct((N_DEV, D), F32)],
        scratch_shapes=[pltpu.VMEM((8, D), F32), pltpu.VMEM((N_DEV, 8, D), F32),
                        pltpu.VMEM((N_DEV, 8, wloc), F32), pltpu.VMEM((N_DEV, 8, wloc), F32),
                        pltpu.SemaphoreType.DMA((N_DEV - 1,)), pltpu.SemaphoreType.DMA((N_DEV - 1,)),
                        pltpu.SemaphoreType.DMA((N_DEV - 1,)), pltpu.SemaphoreType.DMA((N_DEV - 1,))],
        compiler_params=pltpu.CompilerParams(vmem_limit_bytes=VMEM_BIG))(c, w_ada, b_r)


def _gather_slabs(slab, *, name):
    def body(x_ref, out_ref, send_sems, recv_sems, local_sem):
        x, y, c = _coords()
        me, sibling = (x, y, c), (x, y, 1 - c)
        chips = [(1 - x, y), (x, 1 - y), (1 - x, 1 - y)]

        def slot(px, py, pc):
            return out_ref.at[4 * px + 2 * py + pc]

        def copy(k, block, to, src=None):
            return _rcopy(slot(*block) if src is None else src, slot(*block), send_sems.at[k], recv_sems.at[k], to)

        mine = pltpu.make_async_copy(x_ref, slot(*me), local_sem)
        mine.start()
        first = [copy(0, me, sibling, src=x_ref)]
        first += [copy(1 + j, me, (*chip, c), src=x_ref) for j, chip in enumerate(chips)]
        for cp in first:
            cp.start()
        passed = [copy(4 + j, (*chip, c), sibling) for j, chip in enumerate(chips)]
        for j, chip in enumerate(chips):
            copy(1 + j, (*chip, c), me).wait_recv()
            passed[j].start()
        copy(0, sibling, me).wait_recv()
        for j, chip in enumerate(chips):
            copy(4 + j, (*chip, 1 - c), me).wait_recv()
        for cp in first + passed:
            cp.wait_send()
        mine.wait()

    anyspec = pl.BlockSpec(memory_space=pl.ANY)
    return pl.pallas_call(
        body, name=name, in_specs=[anyspec], out_specs=anyspec,
        out_shape=jax.ShapeDtypeStruct((N_DEV,) + slab.shape, slab.dtype),
        scratch_shapes=[pltpu.SemaphoreType.DMA((7,)), pltpu.SemaphoreType.DMA((7,)), pltpu.SemaphoreType.DMA],
    )(slab)


_HBM =pl.BlockSpec(memory_space=pltpu.HBM)
_SEM = pl.BlockSpec(memory_space=pltpu.SEMAPHORE)
_EFFECT = pltpu.SideEffectType.DATAFLOW_SIDE_EFFECTING


def _xchg_src(src_ref, pidx, per_peer):
    return src_ref.at[pidx] if per_peer else src_ref


def _xchg_start(src, *, per_peer, name):
    rows = src.shape[-2]
    land_shape = (N_DEV, rows, D)

    def body(src_ref, land_ref, send_sems, recv_sems, src_thru, land_thru, token):
        del src_thru, land_thru
        x, y, c = _coords()
        me = 4 * x + 2 * y + c
        for k in range(1, N_DEV):
            dev, pidx = _peer(k)
            _rcopy(_xchg_src(src_ref, pidx, per_peer), land_ref.at[me], send_sems.at[k - 1],
                   recv_sems.at[k - 1], dev).start()
        token[...] = jnp.zeros_like(token)

    return pl.pallas_call(
        body, name=name,
        out_shape=(pltpu.SemaphoreType.DMA((N_DEV - 1,)), pltpu.SemaphoreType.DMA((N_DEV - 1,)),
                   pltpu.HBM(src.shape, src.dtype), pltpu.HBM(land_shape, src.dtype),
                   jax.ShapeDtypeStruct((8, 128), F32)),
        in_specs=(_HBM, _HBM),
        out_specs=(_SEM, _SEM, _HBM, _HBM, pl.BlockSpec(memory_space=pltpu.VMEM)),
        input_output_aliases={0: 2, 1: 3},
        compiler_params=pltpu.CompilerParams(has_side_effects=_EFFECT),
    )(pltpu.with_memory_space_constraint(src, pltpu.HBM),
      pltpu.with_memory_space_constraint(lax.empty(land_shape, src.dtype), pltpu.HBM))


def _xchg_wait(started, after, *, per_peer, name):
    send_sems, recv_sems, src_thru, land_thru, _ = started

    def body(src_ref, land_ref, send_sems, recv_sems, after_ref, src_dead, got_ref):
        del after_ref, src_dead, got_ref
        for k in range(1, N_DEV):
            dev, pidx = _peer(k)
            cp = _rcopy(_xchg_src(src_ref, pidx, per_peer), land_ref.at[pidx], send_sems.at[k - 1],
                        recv_sems.at[k - 1], dev)
            cp.wait_send()
            cp.wait_recv()

    return pl.pallas_call(
        body, name=name,
        out_shape=(pltpu.HBM(src_thru.shape, src_thru.dtype), pltpu.HBM(land_thru.shape, land_thru.dtype)),
        in_specs=(_HBM, _HBM, _SEM, _SEM, pl.BlockSpec(memory_space=pl.ANY)),
        out_specs=(_HBM, _HBM),
        input_output_aliases={0: 0, 1: 1},
        compiler_params=pltpu.CompilerParams(has_side_effects=_EFFECT),
    )(src_thru, land_thru, send_sems, recv_sems, after)


def _dep(token):
    return (token, (8, 128), lambda i, j, k: (0, 0))


def _small_allsum(sv, *, name):
    def body(sv_ref, all_ref, sum_ref, send_sems, recv_sems):
        x, y, c = _coords()
        me = 4 * x + 2 * y + c
        all_ref[me] = sv_ref[...]
        _exchange_all(lambda p: sv_ref, lambda s: all_ref.at[s], send_sems, recv_sems)
        acc = all_ref[0]
        for j in range(1, N_DEV):
            acc = acc + all_ref[j]
        sum_ref[...] = acc

    vm = pl.BlockSpec(memory_space=pltpu.VMEM)
    return pl.pallas_call(
        body, name=name, in_specs=[vm], out_specs=[vm, vm],
        out_shape=[jax.ShapeDtypeStruct((N_DEV, SV_ROWS, 128), F32), jax.ShapeDtypeStruct((SV_ROWS, 128), F32)],
        scratch_shapes=[pltpu.SemaphoreType.DMA((7,)), pltpu.SemaphoreType.DMA((7,))],
    )(sv)


def _ada_bwd(call, dmod_loc, *, name):
    wloc = dmod_loc.shape[1]

    def body(c_ref, d_ref, o_ref):
        o_ref[...] = _dot_hi(_silu(c_ref[...]), d_ref[...], TN)

    vm = pl.BlockSpec(memory_space=pltpu.VMEM)
    return pl.pallas_call(body, name=name, in_specs=[vm, vm], out_specs=vm,
                          out_shape=jax.ShapeDtypeStruct((D, wloc), F32),
                          compiler_params=pltpu.CompilerParams(vmem_limit_bytes=VMEM_BIG))(call, dmod_loc)


def _pad_rows(a, rows):
    return jnp.pad(a, ((0, rows - a.shape[0]), (0, 0)))


def _reorder_in_rows(wt):
    z, xbc, dt, pool, gates = wt[0:2048], wt[2048:5120], wt[5120:5152], wt[5152:6176], wt[6176:8224]
    return jnp.concatenate([xbc, pool, z, gates, dt, jnp.zeros((DT_PAD - 32, D), wt.dtype)], axis=0)


def _restore_in_rows(d):
    return jnp.concatenate([d[C_Z:C_Z + 2048], d[C_XBC:C_XBC + XBC], d[C_DT:C_DT + 32],
                            d[C_POOL:C_POOL + 1024], d[C_GATE:C_GATE + 2048]], axis=0)


def _pack_sv(parts):
    flat = []
    for n, size in SV_PARTS:
        v = parts[n].reshape(-1).astype(F32)
        flat.append(jnp.pad(v, (0, size - v.shape[0])))
    v = jnp.concatenate(flat)
    return jnp.pad(v, (0, SV_ROWS * 128 - v.shape[0])).reshape(SV_ROWS, 128)


def _sv_get(flat, n, size):
    return flat[SV_OFF[n]:SV_OFF[n] + size]


def kernel(x, c, w_ada, b_ada, norm_mix_w, w_in, conv_w, conv_b, dt_bias, a_log, d_skip, ssd_norm_w, w_branch_ssd, pool_w, pool_scale, w_branch_pool, w_out, norm_mlp_w, w_up, w_down, norm_final_w, loss_target, m_w_ada, m_b_ada, m_norm_mix_w, m_w_in, m_conv_w, m_conv_b, m_dt_bias, m_a_log, m_d_skip, m_ssd_norm_w, m_w_branch_ssd, m_pool_w, m_pool_scale, m_w_branch_pool, m_w_out, m_norm_mlp_w, m_w_up, m_w_down, m_norm_final_w, v_w_ada, v_b_ada, v_norm_mix_w, v_w_in, v_conv_w, v_conv_b, v_dt_bias, v_a_log, v_d_skip, v_ssd_norm_w, v_w_branch_ssd, v_pool_w, v_pool_scale, v_w_branch_pool, v_w_out, v_norm_mlp_w, v_w_up, v_w_down, v_norm_final_w):
    xs_ = x[0]
    tgt = loss_target[0]
    L = xs_.shape[0]
    me = 4 * lax.axis_index("x") + 2 * lax.axis_index("y") + lax.axis_index("c")
    wloc = w_ada.shape[2]

    mod_p, c_all = _ada_fwd(c, w_ada[0], b_ada.reshape(N_DEV, wloc), name="ada_fwd")
    mod = mod_p.reshape(6, D)
    shift_m, scale_m, gate_m, shift_f, scale_f, gate_f = [mod[i:i + 1] for i in range(6)]

    conv_bits = lax.bitcast_convert_type(conv_w[0], SLAB_DT).reshape(3, D)
    slab_in = jnp.concatenate([_pad_rows(w_in[0].T.astype(SLAB_DT), IN_ROWS_P),
                               _pad_rows(conv_bits, CONV_ROWS)], axis=0)
    slab_rest = jnp.concatenate([
        w_branch_ssd[0].astype(SLAB_DT),
        pool_w[0].reshape(32, D).astype(SLAB_DT),
        w_branch_pool[0].astype(SLAB_DT),
        w_out[0].astype(SLAB_DT),
        w_up[0].T.astype(SLAB_DT),
        w_down[0].astype(SLAB_DT)], axis=0)
    gs_in = _gather_slabs(slab_in, name="gather_w_in")
    slab_rest, gs_in = lax.optimization_barrier((slab_rest, gs_in))
    rest_started = _xchg_start(slab_rest, per_peer=False, name="gather_rest_start")
    gather_token = rest_started[4]

    w_in_t = _reorder_in_rows(gs_in[:, :IN_ROWS].reshape(N_IN, D))
    conv_full = lax.bitcast_convert_type(
        gs_in[:, IN_ROWS_P:IN_ROWS_P + 3].reshape(N_DEV, 4, XBC // N_DEV, 2), F32)
    conv_full = conv_full.transpose(1, 0, 2).reshape(4, XBC)

    dtb = jnp.pad(dt_bias, ((0, 0), (0, 128 - NH)))
    arow = jnp.pad(-jnp.exp(a_log), ((0, 0), (0, 128 - NH)))
    dsk_x = jnp.repeat(d_skip, HP, axis=1)

    tm = _pick(L, (1024, 512, 256, 128))
    tm2 = _pick(L, (2048, 1024, 512, 256, 128))
    tkl = _pick(L, (4096, 2048, 1024, 512, 256, 128))
    tkl2 = _pick(L, (2048, 1024, 512, 256, 128))

    h1 = _norm_fwd(xs_, norm_mix_w, scale_m, shift_m, name="norm1_fwd")
    proj = _mm(h1, w_in_t, "nt", name="in_proj", outs=[F32], tm=tm2, tn=768, tk=D,
               extras=[_dep(gather_token)])
    xbc = _conv_fwd(proj, conv_full, conv_b, name="conv_fwd")
    y_ssm, hs = _ssd_fwd(xbc, proj, dtb, arow, dsk_x, name="ssd_fwd")
    yn = _gated_norm_fwd(y_ssm, proj, ssd_norm_w, name="gated_norm_fwd")

    slab_rest, gs = _xchg_wait(rest_started, yn, per_peer=False, name="gather_rest_wait")
    gs = lax.dynamic_update_slice(gs, slab_rest[None], (me, 0, 0))

    def part(n, rows):
        return gs[:, REST_OFF[n]:REST_OFF[n] + rows]

    w_bssd = part("bssd", 256).reshape(DI, D)
    w_pool = part("pool", 32).reshape(N_DEV, 4, 32, PGW).transpose(1, 0, 2, 3).reshape(POOL_W, PGW)
    w_bpool = part("bpool", 128).reshape(POOL_W, D)
    w_o = part("out", 128).reshape(D, D)
    w_up_t = part("up", 512).reshape(DFF, D)
    w_dn = part("down", 512).reshape(DFF, D)

    y_ssd = _mm(yn, w_bssd, "nn", name="branch_ssd", outs=[F32], tm=tm2, tn=D, tk=DI)
    pooled = _pool_fwd(proj, name="pool_fwd")
    yp0, yp1 = _mm_pool(pooled, w_pool, pool_scale, name="pool_mix", tm=tm, transpose_w=False)
    y_pool = _mm(yp1, w_bpool, "nn", name="branch_pool", outs=[F32], tm=tm2, tn=D, tk=D)
    m = _merge_fwd(y_ssd, y_pool, proj, name="merge_fwd")
    mix = _mm(m, w_o, "nn", name="out_proj", outs=[F32], tm=tm2, tn=D, tk=D)
    x1, h2 = _resid_norm_fwd(xs_, mix, gate_m, norm_mlp_w, scale_f, shift_f, name="norm2_fwd")

    def relu2(acc, ex, outs):
        r = jnp.maximum(acc, 0.0)
        outs[0][...] = acc.astype(BF16)
        outs[1][...] = (r * r).astype(BF16)

    up, act = _mm(h2, w_up_t, "nt", name="mlp_up", outs=[BF16, BF16], tm=tm2, tn=1024, tk=D, epilogue=relu2)
    down = _mm(act, w_dn, "nn", name="mlp_down", outs=[F32], tm=tm, tn=D, tk=DFF)

    dx2, ddown, loss_p, dnwf, dgate_f = _final_bwd(x1, down, gate_f, norm_final_w.reshape(1, D), tgt,
                                                   name="final_bwd")

    def drelu2(acc, ex, outs):
        outs[0][...] = (acc * (2.0 * jnp.maximum(ex[0][...].astype(F32), 0.0))).astype(BF16)

    dup = _mm(ddown, w_dn, "nt", name="mlp_down_dx", outs=[BF16], tm=tm2, tn=1024, tk=D,
              extras=[(up, (tm2, 1024), lambda i, j, k: (i, j))], epilogue=drelu2)
    g_dn = _mm(act, ddown, "tn", name="mlp_down_dw", outs=[SLAB_DT], tm=1024, tn=D, tk=tkl)
    dh2 = _mm(dup, w_up_t, "nn", name="mlp_up_dx", outs=[F32], tm=tm, tn=D, tk=DFF)
    g_up_t = _mm(dup, h2, "tn", name="mlp_up_dw", outs=[SLAB_DT], tm=1024, tn=D, tk=tkl)
    gslab_mlp = jnp.concatenate([g_up_t.reshape(N_DEV, 512, D), g_dn.reshape(N_DEV, 512, D)], axis=1)
    mlp_started = _xchg_start(gslab_mlp, per_peer=True, name="scatter_mlp_start")
    dx1, p2, q2, dmix, dgate_m = _norm_bwd(x1, dh2, dx2, norm_mlp_w, scale_f, mix, gate_m, name="norm2_bwd")
    dm = _mm(dmix, w_o, "nt", name="out_proj_dx", outs=[F32], tm=tm2, tn=D, tk=D,
             extras=[_dep(mlp_started[4])])
    g_o = _mm(m, dmix, "tn", name="out_proj_dw", outs=[SLAB_DT], tm=D, tn=D, tk=tkl)
    dy_ssd, dy_pool, dproj = _merge_bwd(dm, y_ssd, y_pool, proj, name="merge_bwd")
    dyn = _mm(dy_ssd, w_bssd, "nt", name="branch_ssd_dx", outs=[F32], tm=tm2, tn=1024, tk=D)
    g_bssd = _mm(yn, dy_ssd, "tn", name="branch_ssd_dw", outs=[SLAB_DT], tm=1024, tn=D, tk=tkl)
    dy_ssm, dproj, d_snw = _gated_norm_bwd(dyn, y_ssm, proj, ssd_norm_w, dproj, name="gated_norm_bwd")
    dxbc, dproj, d_a, d_dx, d_dtb = _ssd_bwd(dy_ssm, xbc, proj, hs, dtb, arow, dsk_x, dproj, name="ssd_bwd")
    dproj, d_cw, d_cb = _conv_bwd(proj, dxbc, conv_full, conv_b, dproj, name="conv_bwd")
    dyp1 = _mm(dy_pool, w_bpool, "nt", name="branch_pool_dx", outs=[F32], tm=tm2, tn=D, tk=D)
    g_bpool = _mm(yp1, dy_pool, "tn", name="branch_pool_dw", outs=[SLAB_DT], tm=D, tn=D, tk=tkl)
    dyp0, d_ps = _pscale_bwd(dyp1, yp0, pool_scale, name="pool_scale_bwd")
    dpooled = _mm_pool(dyp0, w_pool, None, name="pool_mix_dx", tm=tm, transpose_w=True)
    g_pool = _mm_pool_tn(pooled, dyp0, name="pool_mix_dw", tk=tkl)
    gslab_mix = jnp.concatenate([
        g_bssd.reshape(N_DEV, 256, D),
        g_pool.reshape(4, N_DEV, 32, PGW).transpose(1, 0, 2, 3).reshape(N_DEV, 32, D).astype(SLAB_DT),
        g_bpool.reshape(N_DEV, 128, D),
        g_o.reshape(N_DEV, 128, D)], axis=1)
    mix_started = _xchg_start(gslab_mix, per_peer=True, name="scatter_mix_start")
    dproj = _pool_bwd(dpooled, dproj, name="pool_bwd")
    g_in_t = _mm(dproj, h1, "tn", name="in_proj_dw", outs=[SLAB_DT], tm=1408, tn=D, tk=tkl2,
                 extras=[_dep(mix_started[4])])
    gslab_in = jnp.pad(_restore_in_rows(g_in_t).reshape(N_DEV, IN_ROWS, D),
                       ((0, 0), (0, IN_ROWS_P - IN_ROWS), (0, 0)))
    in_started = _xchg_start(gslab_in, per_peer=True, name="scatter_in_start")
    dh1 = _mm(dproj, w_in_t, "nn", name="in_proj_dx", outs=[F32], tm=tm, tn=D, tk=2816,
              extras=[_dep(in_started[4])])
    grad_x, p1, q1 = _norm_bwd(xs_, dh1, dx1, norm_mix_w, scale_m, name="norm1_bwd")

    def landed(started, after, tile, name):
        src, land = _xchg_wait(started, after, per_peer=True, name=name + "_wait")
        own = lax.dynamic_slice_in_dim(src, me, 1, axis=0)
        return _slab_sum(lax.dynamic_update_slice(land, own, (me, 0, 0)), tile=tile, name=name + "_sum")

    gsum_mlp = landed(mlp_started, grad_x, 256, "scatter_mlp")
    gsum_mix = landed(mix_started, grad_x, 272, "scatter_mix")
    gsum_in = landed(in_started, grad_x, 208, "scatter_in")

    dmod = jnp.concatenate([q1, p1 * norm_mix_w, dgate_m, q2, p2 * norm_mlp_w, dgate_f], axis=1)
    d_alog = d_a[:, :NH] * (-jnp.exp(a_log))
    sv = _pack_sv({
        "b_ada": dmod, "norm_mix_w": p1 * (1.0 + scale_m), "conv_b": d_cb, "dt_bias": d_dtb[:, :NH],
        "a_log": d_alog, "d_skip": d_dx.reshape(NH, HP).sum(axis=1), "ssd_norm_w": d_snw,
        "pool_scale": d_ps, "norm_mlp_w": p2 * (1.0 + scale_f), "norm_final_w": dnwf, "conv_w": d_cw,
        "loss": loss_p[:, :1]})
    sv_all, sv_sum = _small_allsum(sv, name="small_allsum")
    flat = sv_sum.reshape(-1)
    loss = flat[SV_OFF["loss"]]
    dmod_all = sv_all.reshape(N_DEV, SV_ROWS * 128)[:, :6 * D]
    g_w_ada = _ada_bwd(c_all, lax.dynamic_slice_in_dim(dmod_all, me * wloc, wloc, axis=1), name="ada_bwd")

    g_conv_w = lax.dynamic_slice_in_dim(_sv_get(flat, "conv_w", 4 * XBC).reshape(4, XBC),
                                        me * (XBC // N_DEV), XBC // N_DEV, axis=1)
    small = {
        "b_ada": (b_ada, m_b_ada, v_b_ada, _sv_get(flat, "b_ada", 6 * D)),
        "norm_mix_w": (norm_mix_w, m_norm_mix_w, v_norm_mix_w, _sv_get(flat, "norm_mix_w", D)),
        "conv_b": (conv_b, m_conv_b, v_conv_b, _sv_get(flat, "conv_b", XBC)),
        "dt_bias": (dt_bias, m_dt_bias, v_dt_bias, _sv_get(flat, "dt_bias", NH)),
        "a_log": (a_log, m_a_log, v_a_log, _sv_get(flat, "a_log", NH)),
        "d_skip": (d_skip, m_d_skip, v_d_skip, _sv_get(flat, "d_skip", NH)),
        "ssd_norm_w": (ssd_norm_w, m_ssd_norm_w, v_ssd_norm_w, _sv_get(flat, "ssd_norm_w", DI)),
        "pool_scale": (pool_scale, m_pool_scale, v_pool_scale, _sv_get(flat, "pool_scale", POOL_W)),
        "norm_mlp_w": (norm_mlp_w, m_norm_mlp_w, v_norm_mlp_w, _sv_get(flat, "norm_mlp_w", D)),
        "norm_final_w": (norm_final_w, m_norm_final_w, v_norm_final_w, _sv_get(flat, "norm_final_w", D)),
        "conv_w": (conv_w, m_conv_w, v_conv_w, g_conv_w),
    }
    names = list(small)
    sizes = [int(np.prod(small[n][0].shape)) for n in names]
    tot = sum(sizes)
    rows = -(-tot // 1024) * 8

    def pack(idx):
        v = jnp.concatenate([small[n][idx].reshape(-1) for n in names])
        return jnp.pad(v, (0, rows * 128 - tot)).reshape(rows, 128)

    sd, sm, sv2 = _adamw(pack(0), pack(3), pack(1), pack(2), name="adamw_small")
    small_out = {}
    off = 0
    for n, sz in zip(names, sizes):
        shp = small[n][0].shape
        small_out[n] = (small[n][3].reshape(shp), sd.reshape(-1)[off:off + sz].reshape(shp),
                        sm.reshape(-1)[off:off + sz].reshape(shp), sv2.reshape(-1)[off:off + sz].reshape(shp))
        off += sz

    def gpart(n, rows_):
        return gsum_mix[MIX_OFF[n]:MIX_OFF[n] + rows_]

    dlt, mn, vn = _adamw(w_in[0].T, gsum_in[:IN_ROWS], m_w_in[0].T, v_w_in[0].T, name="adamw_w_in")
    big_in = tuple(a.T[None] for a in (gsum_in[:IN_ROWS], dlt, mn, vn))

    big = {
        "w_ada": (w_ada, m_w_ada, v_w_ada, g_w_ada, (D, wloc)),
        "w_branch_ssd": (w_branch_ssd, m_w_branch_ssd, v_w_branch_ssd, gpart("bssd", 256), (256, D)),
        "pool_w": (pool_w, m_pool_w, v_pool_w, gpart("pool", 32).reshape(128, PGW), (128, PGW)),
        "w_branch_pool": (w_branch_pool, m_w_branch_pool, v_w_branch_pool, gpart("bpool", 128), (128, D)),
        "w_out": (w_out, m_w_out, v_w_out, gpart("out", 128), (128, D)),
        "w_up": (w_up, m_w_up, v_w_up, gsum_mlp[:512].T, (D, 512)),
        "w_down": (w_down, m_w_down, v_w_down, gsum_mlp[512:], (512, D)),
    }
    big_out = {}
    for n, (w, mm_, vv, g, shp2) in big.items():
        dlt, mn, vn = _adamw(w.reshape(shp2), g, mm_.reshape(shp2), vv.reshape(shp2), name="adamw_" + n)
        big_out[n] = (g.reshape(w.shape), dlt.reshape(w.shape), mn.reshape(w.shape), vn.reshape(w.shape))

    order = ["w_ada", "b_ada", "norm_mix_w", "w_in", "conv_w", "conv_b", "dt_bias", "a_log", "d_skip",
             "ssd_norm_w", "w_branch_ssd", "pool_w", "pool_scale", "w_branch_pool", "w_out", "norm_mlp_w",
             "w_up", "w_down", "norm_final_w"]
    big_out["w_in"] = big_in
    res = {**small_out, **big_out}
    outs = [loss, grad_x.reshape(x.shape)]
    for k in range(4):
        outs += [res[n][k] for n in order]
    return tuple(outs)
```

```python
import functools

import numpy as np
import jax
import jax.numpy as jnp
from jax import lax
from jax.experimental import pallas as pl
from jax.experimental.pallas import tpu as pltpu

F32 = jnp.float32
BF16 = jnp.bfloat16
SLAB_DT = jnp.bfloat16
_MXU_DTYPE = jnp.bfloat16

N_DEV = 8
D = 1024
DI = 2048
NH = 32
HP = 64
NG = 4
NS = 128
Q = 128
XBC = DI + 2 * NG * NS
DFF = 4096
N_IN = 8224
EPS = 1e-5
POOL_W = 1024
PGW = 256

C_XBC, C_POOL, C_Z, C_GATE, C_DT = 0, 3072, 4096, 6144, 8192
DT_PAD = 256
NPROJ = C_DT + DT_PAD

IN_ROWS = N_IN // N_DEV
IN_ROWS_P = 1040
CONV_ROWS = 16
REST_PARTS = (("bssd", 256), ("pool", 32), ("bpool", 128), ("out", 128), ("up", 512), ("down", 512))
REST_OFF = {}
_o = 0
for _n, _r in REST_PARTS:
    REST_OFF[_n] = _o
    _o += _r
REST_ROWS = _o
MIX_PARTS = (("bssd", 256), ("pool", 32), ("bpool", 128), ("out", 128))
MIX_OFF = {}
_o = 0
for _n, _r in MIX_PARTS:
    MIX_OFF[_n] = _o
    _o += _r
MIX_ROWS = _o

SV_PARTS = (("b_ada", 6144), ("norm_mix_w", 1024), ("conv_b", 3072), ("dt_bias", 128), ("a_log", 128),
            ("d_skip", 128), ("ssd_norm_w", 2048), ("pool_scale", 1024), ("norm_mlp_w", 1024),
            ("norm_final_w", 1024), ("conv_w", 4 * XBC), ("loss", 128))
SV_OFF = {}
_o = 0
for _n, _r in SV_PARTS:
    SV_OFF[_n] = _o
    _o += _r
SV_ROWS = 224
assert _o <= SV_ROWS * 128

ADAM_LR, ADAM_B1, ADAM_B2, ADAM_EPS, ADAM_WD, ADAM_STEP = 0.001, 0.9, 0.999, 1e-08, 0.01, 10

VMEM_BIG = 56 * 1024 * 1024
NEG = -1e30

NN = ((1,), (0,))
NT = ((1,), (1,))
TN = ((0,), (0,))


def _dot(a, b, dims=NN):
    return lax.dot_general(a.astype(_MXU_DTYPE), b.astype(_MXU_DTYPE), (dims, ((), ())),
                           preferred_element_type=F32)


def _dot_hi(a, b, dims=NN):
    return lax.dot_general(a.astype(F32), b.astype(F32), (dims, ((), ())),
                           precision=lax.Precision.HIGHEST, preferred_element_type=F32)


def _pick(n, cands):
    for c in cands:
        if n % c == 0:
            return c
    return n


def _sigmoid(x):
    return 1.0 / (1.0 + jnp.exp(-x))


def _silu(x):
    return x * _sigmoid(x)


def _dsilu(x):
    s = _sigmoid(x)
    return s * (1.0 + x * (1.0 - s))


def _softplus(x):
    return jnp.maximum(x, 0.0) + jnp.log(1.0 + jnp.exp(-jnp.abs(x)))


def _params(sem, vmem=None):
    return pltpu.CompilerParams(dimension_semantics=sem, vmem_limit_bytes=vmem)


def _mm(a, b, mode, *, name, outs, tm, tn, tk, extras=(), epilogue=None, aliases=None):
    if mode == "tn":
        K, M = a.shape
        N = b.shape[1]
        a_spec = pl.BlockSpec((tk, tm), lambda i, j, k: (k, i))
        b_spec = pl.BlockSpec((tk, tn), lambda i, j, k: (k, j))
        dims = TN
    else:
        M, K = a.shape
        a_spec = pl.BlockSpec((tm, tk), lambda i, j, k: (i, k))
        if mode == "nn":
            N = b.shape[1]
            b_spec = pl.BlockSpec((tk, tn), lambda i, j, k: (k, j))
            dims = NN
        else:
            N = b.shape[0]
            b_spec = pl.BlockSpec((tn, tk), lambda i, j, k: (j, k))
            dims = NT
    assert M % tm == 0 and N % tn == 0 and K % tk == 0, (name, M, N, K, tm, tn, tk)
    nk = K // tk
    ne, no = len(extras), len(outs)
    if epilogue is None:
        def epilogue(acc, ex, out_refs):
            out_refs[0][...] = acc.astype(out_refs[0].dtype)

    def body(a_ref, b_ref, *rest):
        ex, out_refs = rest[:ne], rest[ne:ne + no]
        p = _dot(a_ref[...], b_ref[...], dims)
        if nk == 1:
            epilogue(p, ex, out_refs)
        else:
            acc = rest[-1]
            k = pl.program_id(2)

            @pl.when(k == 0)
            def _():
                acc[...] = p

            @pl.when(jnp.logical_and(k > 0, k < nk - 1))
            def _():
                acc[...] += p

            @pl.when(k == nk - 1)
            def _():
                epilogue(acc[...] + p, ex, out_refs)

    out_specs, out_shape = [], []
    for o in outs:
        if isinstance(o, tuple):
            shape, dt, bs, im = o
            out_specs.append(pl.BlockSpec(bs, im))
            out_shape.append(jax.ShapeDtypeStruct(shape, dt))
        else:
            out_specs.append(pl.BlockSpec((tm, tn), lambda i, j, k: (i, j)))
            out_shape.append(jax.ShapeDtypeStruct((M, N), o))
    in_specs = [a_spec, b_spec]
    for _, bs, im in extras:
        in_specs.append(pl.BlockSpec(memory_space=pl.ANY) if bs is None else pl.BlockSpec(bs, im))
    res = pl.pallas_call(
        body, name=name,
        grid=(M // tm, N // tn, nk),
        in_specs=in_specs, out_specs=out_specs, out_shape=out_shape,
        scratch_shapes=[pltpu.VMEM((tm, tn), F32)] if nk > 1 else [],
        input_output_aliases={2 + e: o for e, o in (aliases or {}).items()},
        compiler_params=_params(("arbitrary", "arbitrary", "arbitrary"), VMEM_BIG),
    )(a, b, *[e[0] for e in extras])
    return res if no > 1 else res[0]


def _rows(tm, w=D, col=0):
    return (tm, w), lambda i, j, k, c=col: (i, c)


def _vecs(w=D, col=0):
    return (1, w), lambda i, j, k, c=col: (0, c)


def _sum_out(w=D):
    return ((1, w), F32, (1, w), lambda i, j, k: (0, 0))


def _mm_pool(a, w, scale, *, name, tm, transpose_w):
    L = a.shape[0]
    dims = NT if transpose_w else NN

    def body(a_ref, w_ref, *rest):
        p = _dot(a_ref[...], w_ref[...], dims)
        if transpose_w:
            rest[0][...] = p
        else:
            s_ref, o0, o1 = rest
            o0[...] = p
            o1[...] = (p * s_ref[...]).astype(o1.dtype)

    blk = pl.BlockSpec((tm, PGW), lambda i, j: (i, j))
    in_specs = [blk, pl.BlockSpec((PGW, PGW), lambda i, j: (j, 0))]
    args = [a, w]
    if transpose_w:
        out_specs, out_shape = [blk], [jax.ShapeDtypeStruct((L, POOL_W), F32)]
    else:
        in_specs.append(pl.BlockSpec((1, PGW), lambda i, j: (0, j)))
        args.append(scale)
        out_specs = [blk, blk]
        out_shape = [jax.ShapeDtypeStruct((L, POOL_W), F32), jax.ShapeDtypeStruct((L, POOL_W), BF16)]
    res = pl.pallas_call(body, name=name, grid=(L // tm, 4), in_specs=in_specs, out_specs=out_specs,
                         out_shape=out_shape, compiler_params=_params(("parallel", "parallel")))(*args)
    return res[0] if transpose_w else res


def _mm_pool_tn(a, b, *, name, tk):
    L = a.shape[0]

    def body(a_ref, b_ref, o_ref):
        p = _dot(a_ref[...], b_ref[...], TN)

        @pl.when(pl.program_id(1) == 0)
        def _():
            o_ref[...] = p

        @pl.when(pl.program_id(1) > 0)
        def _():
            o_ref[...] += p

    blk = pl.BlockSpec((tk, PGW), lambda g, k: (k, g))
    return pl.pallas_call(body, name=name, grid=(4, L // tk), in_specs=[blk, blk],
                          out_specs=pl.BlockSpec((PGW, PGW), lambda g, k: (g, 0)),
                          out_shape=jax.ShapeDtypeStruct((POOL_W, PGW), F32),
                          compiler_params=_params(("parallel", "arbitrary")))(a, b)


def _row(tl, w, col=0):
    return pl.BlockSpec((tl, w), lambda i, c=col: (i, c))


def _vec(w, col=0):
    return pl.BlockSpec((1, w), lambda i, c=col: (0, c))


def _acc_out(ref, val, i):
    @pl.when(i == 0)
    def _():
        ref[...] = val

    @pl.when(i > 0)
    def _():
        ref[...] += val


def _colsum(v):
    return jnp.sum(v, axis=0, keepdims=True)


def _norm_fwd(x, nw, scale, shift, *, name):
    L = x.shape[0]
    tl = _pick(L, (512, 256, 128))

    def body(x_ref, nw_ref, sc_ref, sh_ref, h_ref):
        xv = x_ref[...]
        r = lax.rsqrt(jnp.mean(xv * xv, axis=-1, keepdims=True) + EPS)
        h_ref[...] = (xv * r * nw_ref[...] * (1.0 + sc_ref[...]) + sh_ref[...]).astype(h_ref.dtype)

    return pl.pallas_call(body, name=name, grid=(L // tl,),
                          in_specs=[_row(tl, D), _vec(D), _vec(D), _vec(D)], out_specs=_row(tl, D),
                          out_shape=jax.ShapeDtypeStruct((L, D), BF16),
                          compiler_params=_params(("parallel",)))(x, nw, scale, shift)


def _ep_resid_norm(acc, ex, outs):
    x_ref, g_ref, nw_ref, sc_ref, sh_ref = ex
    mix_ref, x1_ref, h_ref = outs
    mix_ref[...] = acc.astype(mix_ref.dtype)
    xv = x_ref[...] + g_ref[...] * acc
    x1_ref[...] = xv
    r = lax.rsqrt(jnp.mean(xv * xv, axis=-1, keepdims=True) + EPS)
    h_ref[...] = (xv * r * nw_ref[...] * (1.0 + sc_ref[...]) + sh_ref[...]).astype(h_ref.dtype)


def _ep_final(acc, ex, outs):
    x1_ref, t_ref, g_ref, nw_ref = ex
    dx2_ref, dd_ref, loss_ref, dnw_ref, dg_ref = outs
    i = pl.program_id(0)
    x2 = x1_ref[...] + g_ref[...] * acc
    r = lax.rsqrt(jnp.mean(x2 * x2, axis=-1, keepdims=True) + EPS)
    xh = x2 * r
    e = xh * nw_ref[...] - t_ref[...]
    part = 0.5 * jnp.sum(jnp.mean(e * e, axis=-1, keepdims=True), axis=0, keepdims=True)
    dy = e * (1.0 / D)
    g = dy * nw_ref[...]
    dx2 = r * (g - xh * jnp.mean(g * xh, axis=-1, keepdims=True))
    dx2_ref[...] = dx2
    dd_ref[...] = (dx2 * g_ref[...]).astype(dd_ref.dtype)
    _acc_out(loss_ref, jnp.broadcast_to(part, (1, 128)), i)
    _acc_out(dnw_ref, _colsum(dy * xh), i)
    _acc_out(dg_ref, _colsum(dx2 * acc), i)


def _ep_norm_bwd(acc, ex, outs):
    x_ref, dr_ref, nw_ref, sc_ref = ex[:4]
    dx_ref, p_ref, q_ref = outs[:3]
    i = pl.program_id(0)
    xv = x_ref[...]
    r = lax.rsqrt(jnp.mean(xv * xv, axis=-1, keepdims=True) + EPS)
    xh = xv * r
    g = acc * (nw_ref[...] * (1.0 + sc_ref[...]))
    dx = dr_ref[...] + r * (g - xh * jnp.mean(g * xh, axis=-1, keepdims=True))
    dx_ref[...] = dx
    _acc_out(p_ref, _colsum(acc * xh), i)
    _acc_out(q_ref, _colsum(acc), i)
    if len(ex) > 4:
        m_ref, g_ref = ex[4:]
        dm_ref, dg_ref = outs[3:]
        dm_ref[...] = (dx * g_ref[...]).astype(dm_ref.dtype)
        _acc_out(dg_ref, _colsum(dx * m_ref[...]), i)


def _ep_merge_bwd(acc, ex, outs):
    a_ref, b_ref, gl_ref = ex
    da_ref, db_ref, dgl_ref = outs
    s = _sigmoid(gl_ref[...])
    s1, s2 = s[:, :D], s[:, D:]
    da_ref[...] = (acc * s1).astype(da_ref.dtype)
    db_ref[...] = (acc * s2).astype(db_ref.dtype)
    dgl_ref[:, :D] = (acc * a_ref[...] * s1 * (1.0 - s1)).astype(dgl_ref.dtype)
    dgl_ref[:, D:] = (acc * b_ref[...] * s2 * (1.0 - s2)).astype(dgl_ref.dtype)


def _ep_pscale_bwd(acc, ex, outs):
    y_ref, s_ref = ex
    o_ref, ds_ref = outs
    o_ref[...] = (acc * s_ref[...]).astype(o_ref.dtype)
    _acc_out(ds_ref, _colsum(acc * y_ref[...]), pl.program_id(0))


def _merge_fwd(y_ssd, y_pool, proj, *, name):
    L = y_ssd.shape[0]
    tl = _pick(L, (512, 256, 128))

    def body(a_ref, b_ref, gl_ref, m_ref):
        s = _sigmoid(gl_ref[...])
        m_ref[...] = (s[:, :D] * a_ref[...] + s[:, D:] * b_ref[...]).astype(m_ref.dtype)

    return pl.pallas_call(body, name=name, grid=(L // tl,),
                          in_specs=[_row(tl, D), _row(tl, D), _row(tl, 2 * D, C_GATE // (2 * D))],
                          out_specs=_row(tl, D), out_shape=jax.ShapeDtypeStruct((L, D), BF16),
                          compiler_params=_params(("parallel",)))(y_ssd, y_pool, proj)


GW = DI // NG


def _gated_norm_fwd(y, proj, w, *, name):
    L = y.shape[0]
    tl = _pick(L, (256, 128))

    def body(y_ref, z_ref, w_ref, o_ref):
        yg = y_ref[...] * _silu(z_ref[...])
        for k in range(NG):
            seg = yg[:, k * GW:(k + 1) * GW]
            r = lax.rsqrt(jnp.mean(seg * seg, axis=-1, keepdims=True) + EPS)
            o_ref[:, k * GW:(k + 1) * GW] = (seg * r * w_ref[:, k * GW:(k + 1) * GW]).astype(o_ref.dtype)

    return pl.pallas_call(body, name=name, grid=(L // tl,),
                          in_specs=[_row(tl, DI), _row(tl, DI, C_Z // DI), _vec(DI)],
                          out_specs=_row(tl, DI), out_shape=jax.ShapeDtypeStruct((L, DI), BF16),
                          compiler_params=_params(("parallel",)))(y, proj, w)


def _gated_norm_bwd(dyn, y, proj, w, dproj, *, name):
    L = y.shape[0]
    tl = _pick(L, (256, 128))

    def body(dyn_ref, y_ref, z_ref, w_ref, dp_in, dy_ref, dz_ref, dw_ref):
        del dp_in
        i = pl.program_id(0)
        zv = z_ref[...]
        yv = y_ref[...]
        sz = _silu(zv)
        yg = yv * sz
        dsz = _dsilu(zv)
        dws = []
        for k in range(NG):
            sl = slice(k * GW, (k + 1) * GW)
            seg = yg[:, sl]
            r = lax.rsqrt(jnp.mean(seg * seg, axis=-1, keepdims=True) + EPS)
            sh = seg * r
            dn = dyn_ref[:, sl]
            g = dn * w_ref[:, sl]
            dyg = r * (g - sh * jnp.mean(g * sh, axis=-1, keepdims=True))
            dy_ref[:, sl] = dyg * sz[:, sl]
            dz_ref[:, sl] = (dyg * yv[:, sl] * dsz[:, sl]).astype(dz_ref.dtype)
            dws.append(_colsum(dn * sh))
        _acc_out(dw_ref, jnp.concatenate(dws, axis=1), i)

    zc = C_Z // DI
    res = pl.pallas_call(
        body, name=name, grid=(L // tl,),
        in_specs=[_row(tl, DI), _row(tl, DI), _row(tl, DI, zc), _vec(DI), pl.BlockSpec(memory_space=pl.ANY)],
        out_specs=[_row(tl, DI), _row(tl, DI, zc), _vec(DI)],
        out_shape=[jax.ShapeDtypeStruct((L, DI), F32), jax.ShapeDtypeStruct((L, NPROJ), BF16),
                   jax.ShapeDtypeStruct((1, DI), F32)],
        input_output_aliases={4: 1},
        compiler_params=_params(("arbitrary",)))(dyn, y, proj, w, dproj)
    return res


CONV_CB = 128
HALO = 16


def _time_chunk(L):
    return _pick(L, (256, 128))


def _conv_fwd(proj, w, b, *, name):
    L = proj.shape[0]
    rc = _time_chunk(L)
    n = L // rc

    def body(x_ref, w_ref, b_ref, o_ref, pad):
        pad[0:HALO, :] = jnp.zeros((HALO, CONV_CB), F32)
        wv = w_ref[...]
        bv = b_ref[...]

        def fill(i, c):
            r0 = pl.multiple_of(i * rc, rc)
            pad[pl.ds(r0 + HALO, rc), :] = x_ref[pl.ds(r0, rc), :]
            return c

        lax.fori_loop(0, n, fill, 0)

        def step(i, c):
            r0 = pl.multiple_of(i * rc, rc)
            ext = pad[pl.ds(r0, rc + HALO), :]
            acc = bv + ext * wv[3:4]
            for j in (1, 2, 3):
                acc = acc + pltpu.roll(ext, j, 0) * wv[3 - j:4 - j]
            acc = acc[HALO:]
            o_ref[pl.ds(r0, rc), :] = acc * _sigmoid(acc)
            return c

        lax.fori_loop(0, n, step, 0)

    return pl.pallas_call(
        body, name=name, grid=(XBC // CONV_CB,),
        in_specs=[pl.BlockSpec((L, CONV_CB), lambda j: (0, j + C_XBC // CONV_CB)),
                  pl.BlockSpec((4, CONV_CB), lambda j: (0, j)), pl.BlockSpec((1, CONV_CB), lambda j: (0, j))],
        out_specs=pl.BlockSpec((L, CONV_CB), lambda j: (0, j)),
        out_shape=jax.ShapeDtypeStruct((L, XBC), F32),
        scratch_shapes=[pltpu.VMEM((L + HALO, CONV_CB), F32)],
        compiler_params=_params(("parallel",), VMEM_BIG))(proj, w, b)


def _conv_bwd(proj, dy, w, b, dproj, *, name):
    L = proj.shape[0]
    rc = _time_chunk(L)
    n = L // rc

    def body(x_ref, dy_ref, w_ref, b_ref, dp_in, dx_ref, dw_ref, db_ref, pad, dpad):
        del dp_in
        pad[0:HALO, :] = jnp.zeros((HALO, CONV_CB), F32)
        dpad[L:L + HALO, :] = jnp.zeros((HALO, CONV_CB), F32)
        wv = w_ref[...]
        bv = b_ref[...]

        def fill(i, c):
            r0 = pl.multiple_of(i * rc, rc)
            pad[pl.ds(r0 + HALO, rc), :] = x_ref[pl.ds(r0, rc), :]
            return c

        lax.fori_loop(0, n, fill, 0)

        def p1(i, carry):
            r0 = pl.multiple_of(i * rc, rc)
            ext = pad[pl.ds(r0, rc + HALO), :]
            xk = [ext[HALO:]] + [pltpu.roll(ext, j, 0)[HALO:] for j in (1, 2, 3)]
            pre = bv
            for j in range(4):
                pre = pre + xk[j] * wv[3 - j:4 - j]
            dpre = dy_ref[pl.ds(r0, rc), :] * _dsilu(pre)
            dpad[pl.ds(r0, rc), :] = dpre
            db, d0, d1, d2, d3 = carry
            return (db + _colsum(dpre), d0 + _colsum(dpre * xk[3]), d1 + _colsum(dpre * xk[2]),
                    d2 + _colsum(dpre * xk[1]), d3 + _colsum(dpre * xk[0]))

        z = jnp.zeros((1, CONV_CB), F32)
        db, d0, d1, d2, d3 = lax.fori_loop(0, n, p1, (z, z, z, z, z))
        db_ref[...] = db
        dw_ref[...] = jnp.concatenate([d0, d1, d2, d3], axis=0)

        def p2(i, c):
            r0 = pl.multiple_of(i * rc, rc)
            ext = dpad[pl.ds(r0, rc + HALO), :]
            acc = ext * wv[3:4]
            for j in (1, 2, 3):
                acc = acc + pltpu.roll(ext, rc + HALO - j, 0) * wv[3 - j:4 - j]
            dx_ref[pl.ds(r0, rc), :] = acc[:rc].astype(dx_ref.dtype)
            return c

        lax.fori_loop(0, n, p2, 0)

    nb = XBC // CONV_CB
    return pl.pallas_call(
        body, name=name, grid=(nb,),
        in_specs=[pl.BlockSpec((L, CONV_CB), lambda j: (0, j + C_XBC // CONV_CB)),
                  pl.BlockSpec((L, CONV_CB), lambda j: (0, j)),
                  pl.BlockSpec((4, CONV_CB), lambda j: (0, j)), pl.BlockSpec((1, CONV_CB), lambda j: (0, j)),
                  pl.BlockSpec(memory_space=pl.ANY)],
        out_specs=[pl.BlockSpec((L, CONV_CB), lambda j: (0, j + C_XBC // CONV_CB)),
                   pl.BlockSpec((4, CONV_CB), lambda j: (0, j)), pl.BlockSpec((1, CONV_CB), lambda j: (0, j))],
        out_shape=[jax.ShapeDtypeStruct((L, NPROJ), BF16), jax.ShapeDtypeStruct((4, XBC), F32),
                   jax.ShapeDtypeStruct((1, XBC), F32)],
        scratch_shapes=[pltpu.VMEM((L + HALO, CONV_CB), F32), pltpu.VMEM((L + HALO, CONV_CB), F32)],
        input_output_aliases={4: 0},
        compiler_params=_params(("parallel",), VMEM_BIG))(proj, dy, w, b, dproj)


def _pool_fwd(proj, *, name):
    L = proj.shape[0]
    rc = _time_chunk(L)
    n = L // rc

    def body(x_ref, o_ref, pad):
        g = pl.program_id(0)
        pad[0:HALO, :] = jnp.zeros((HALO, PGW), F32)

        def fill(i, c):
            r0 = pl.multiple_of(i * rc, rc)
            pad[pl.ds(r0 + HALO, rc), :] = x_ref[pl.ds(r0, rc), :]
            return c

        lax.fori_loop(0, n, fill, 0)
        rows = lax.broadcasted_iota(jnp.int32, (rc, PGW), 0)

        for gi in range(4):
            win = 2 << gi

            @pl.when(g == gi)
            def _(gi=gi, win=win):
                def step(i, c):
                    r0 = pl.multiple_of(i * rc, rc)
                    ext = pad[pl.ds(r0, rc + HALO), :]
                    s = ext
                    sh = 1
                    while sh < win:
                        s = s + pltpu.roll(s, sh, 0)
                        sh *= 2
                    cnt = jnp.minimum(rows + (r0 + 1), win).astype(F32)
                    o_ref[pl.ds(r0, rc), :] = (s[HALO:] / cnt - ext[HALO:]).astype(o_ref.dtype)
                    return c

                lax.fori_loop(0, n, step, 0)

    return pl.pallas_call(
        body, name=name, grid=(4,),
        in_specs=[pl.BlockSpec((L, PGW), lambda j: (0, j + C_POOL // PGW))],
        out_specs=pl.BlockSpec((L, PGW), lambda j: (0, j)),
        out_shape=jax.ShapeDtypeStruct((L, POOL_W), BF16),
        scratch_shapes=[pltpu.VMEM((L + HALO, PGW), F32)],
        compiler_params=_params(("parallel",), VMEM_BIG))(proj)


def _pool_bwd(dpooled, dproj, *, name):
    L = dpooled.shape[0]
    rc = _time_chunk(L)
    n = L // rc

    def body(d_ref, dp_in, o_ref, pad):
        del dp_in
        g = pl.program_id(0)
        pad[L:L + HALO, :] = jnp.zeros((HALO, PGW), F32)
        rows = lax.broadcasted_iota(jnp.int32, (rc, PGW), 0)

        for gi in range(4):
            win = 2 << gi

            @pl.when(g == gi)
            def _(gi=gi, win=win):
                def fill(i, c):
                    r0 = pl.multiple_of(i * rc, rc)
                    cnt = jnp.minimum(rows + (r0 + 1), win).astype(F32)
                    pad[pl.ds(r0, rc), :] = d_ref[pl.ds(r0, rc), :] / cnt
                    return c

                lax.fori_loop(0, n, fill, 0)

                def step(i, c):
                    r0 = pl.multiple_of(i * rc, rc)
                    s = pad[pl.ds(r0, rc + HALO), :]
                    sh = 1
                    while sh < win:
                        s = s + pltpu.roll(s, rc + HALO - sh, 0)
                        sh *= 2
                    o_ref[pl.ds(r0, rc), :] = (s[:rc] - d_ref[pl.ds(r0, rc), :]).astype(o_ref.dtype)
                    return c

                lax.fori_loop(0, n, step, 0)

    return pl.pallas_call(
        body, name=name, grid=(4,),
        in_specs=[pl.BlockSpec((L, PGW), lambda j: (0, j)), pl.BlockSpec(memory_space=pl.ANY)],
        out_specs=pl.BlockSpec((L, PGW), lambda j: (0, j + C_POOL // PGW)),
        out_shape=jax.ShapeDtypeStruct((L, NPROJ), BF16),
        scratch_shapes=[pltpu.VMEM((L + HALO, PGW), F32)],
        input_output_aliases={1: 0},
        compiler_params=_params(("parallel",), VMEM_BIG))(dpooled, dproj)


_SPLIT_DT = jnp.bfloat16


def _ssd_consts():
    tri = np.tril(np.ones((Q, Q), np.float32))
    exp = np.zeros((128, DI), np.float32)
    for h in range(NH):
        exp[h, h * HP:(h + 1) * HP] = 1.0
    exp2 = np.concatenate([exp, exp], axis=0)
    return (jnp.asarray(tri, dtype=_SPLIT_DT), jnp.asarray(tri.T.copy(), dtype=_SPLIT_DT),
            jnp.asarray(exp2, dtype=_SPLIT_DT))


def _split(v, n):
    parts, r = [], v
    for _ in range(n):
        p = r.astype(_SPLIT_DT)
        parts.append(p)
        r = r - p.astype(F32)
    return parts


def _bdot(a, b, dims):
    return lax.dot_general(a, b, (dims, ((), ())), preferred_element_type=F32)


def _tri_sum(t_ref, v):
    r = _bdot(t_ref[...], jnp.concatenate(_split(v, 3), axis=1), NN)
    return r[:, :128] + r[:, 128:256] + r[:, 256:]


def _expand(v, e2_ref):
    return _bdot(jnp.concatenate(_split(v, 2), axis=1), e2_ref[...], NN)


def _reduce_heads(vals, eg):
    parts = []
    for v in vals:
        parts += _split(v, 2)
    r = _bdot(jnp.concatenate(parts, axis=0), eg, NT)
    return [r[2 * i * Q:(2 * i + 1) * Q] + r[(2 * i + 1) * Q:(2 * i + 2) * Q] for i in range(len(vals))]


def _ssd_common(xbc_ref, dtw_ref, dtb_ref, arow_ref, t_ref, e_ref):
    pre = dtw_ref[:, :128] + dtb_ref[...]
    dt = _softplus(pre)
    acs = _tri_sum(t_ref, dt * arow_ref[...])
    acs_x = _expand(acs, e_ref)
    dt_x = _expand(dt, e_ref)
    xs = xbc_ref[:, 0:DI]
    return pre, dt, acs, acs.T, acs_x, dt_x, xs


def _ssd_fwd(xbc, proj, dtb, arow, dsk_x, *, name):
    L = xbc.shape[0]
    nc = L // Q
    tri, _, expand = _ssd_consts()

    def body(xbc_ref, dtw_ref, dtb_ref, arow_ref, dsk_ref, t_ref, e_ref, y_ref, hs_ref, h_scr):
        @pl.when(pl.program_id(0) == 0)
        def _():
            h_scr[...] = jnp.zeros_like(h_scr)

        _, dt, acs, acs_t, acs_x, dt_x, xs = _ssd_common(xbc_ref, dtw_ref, dtb_ref, arow_ref, t_ref, e_ref)
        xdt = xs * dt_x
        eacs = jnp.exp(acs_x)
        acs_last = acs_x[Q - 1:Q, :]
        dec = jnp.exp(acs_last - acs_x)
        hs_ref[0] = h_scr[...]
        causal = lax.broadcasted_iota(jnp.int32, (Q, Q), 0) >= lax.broadcasted_iota(jnp.int32, (Q, Q), 1)
        first = lax.broadcasted_iota(jnp.int32, (Q, 128), 1) < HP
        for g in range(NG):
            bg = xbc_ref[:, DI + g * NS:DI + (g + 1) * NS]
            cg = xbc_ref[:, DI + NG * NS + g * NS:DI + NG * NS + (g + 1) * NS]
            s = _dot(cg, bg, NT)
            sl = slice(g * GW, (g + 1) * GW)
            hg = h_scr[:, sl]
            yoff = _dot(cg, hg, NN) * eacs[:, sl]
            st = _dot(bg, xdt[:, sl] * dec[:, sl], TN)
            h_scr[:, sl] = hg * eacs[Q - 1:Q, sl] + st
            for j in range(4):
                lo = g * GW + j * 128
                xb = xdt[:, lo:lo + 128]
                yp = yoff[:, j * 128:(j + 1) * 128] + dsk_ref[:, lo:lo + 128] * xs[:, lo:lo + 128]
                for e in range(2):
                    h = g * 8 + j * 2 + e
                    lm = jnp.exp(jnp.where(causal, acs[:, h:h + 1] - acs_t[h:h + 1, :], NEG))
                    xm = jnp.where(first if e == 0 else jnp.logical_not(first), xb, 0.0)
                    yp = yp + _dot(s * lm, xm, NN)
                y_ref[:, lo:lo + 128] = yp

    return pl.pallas_call(
        body, name=name, grid=(nc,),
        in_specs=[pl.BlockSpec((Q, XBC), lambda c: (c, 0)),
                  pl.BlockSpec((Q, DT_PAD), lambda c: (c, C_DT // DT_PAD)),
                  pl.BlockSpec((1, 128), lambda c: (0, 0)), pl.BlockSpec((1, 128), lambda c: (0, 0)),
                  pl.BlockSpec((1, DI), lambda c: (0, 0)),
                  pl.BlockSpec((Q, Q), lambda c: (0, 0)), pl.BlockSpec((256, DI), lambda c: (0, 0))],
        out_specs=[pl.BlockSpec((Q, DI), lambda c: (c, 0)), pl.BlockSpec((1, NS, DI), lambda c: (c, 0, 0))],
        out_shape=[jax.ShapeDtypeStruct((L, DI), F32), jax.ShapeDtypeStruct((nc, NS, DI), F32)],
        scratch_shapes=[pltpu.VMEM((NS, DI), F32)],
        compiler_params=_params(("arbitrary",), VMEM_BIG))(xbc, proj, dtb, arow, dsk_x, tri, expand)


def _ssd_bwd(dy, xbc, proj, hs, dtb, arow, dsk_x, dproj, *, name):
    L = xbc.shape[0]
    nc = L // Q
    tri, triu, expand = _ssd_consts()

    def body(dy_ref, xbc_ref, dtw_ref, hs_ref, dtb_ref, arow_ref, dsk_ref, t_ref, u_ref, e_ref, dp_in,
             dxbc_ref, ddtw_ref, da_ref, ddx_ref, ddtb_ref, dh_scr):
        del dp_in
        i = pl.program_id(0)

        @pl.when(i == 0)
        def _():
            dh_scr[...] = jnp.zeros_like(dh_scr)

        pre, dt, acs, acs_t, acs_x, dt_x, xs = _ssd_common(xbc_ref, dtw_ref, dtb_ref, arow_ref, t_ref, e_ref)
        dyv = dy_ref[...]
        xdt = xs * dt_x
        eacs = jnp.exp(acs_x)
        acs_last = acs_x[Q - 1:Q, :]
        dec = jnp.exp(acs_last - acs_x)
        gy = dyv * eacs
        causal = lax.broadcasted_iota(jnp.int32, (Q, Q), 0) >= lax.broadcasted_iota(jnp.int32, (Q, Q), 1)
        first = lax.broadcasted_iota(jnp.int32, (Q, 128), 1) < HP
        lane_h = lax.broadcasted_iota(jnp.int32, (Q, 128), 1)
        sub_h = lax.broadcasted_iota(jnp.int32, (128, Q), 0)
        last_row = lax.broadcasted_iota(jnp.int32, (Q, GW), 0) == Q - 1
        dacs = jnp.zeros((Q, 128), F32)
        dacs_t = jnp.zeros((128, Q), F32)
        ddt = jnp.zeros((Q, 128), F32)
        for g in range(NG):
            bg = xbc_ref[:, DI + g * NS:DI + (g + 1) * NS]
            cg = xbc_ref[:, DI + NG * NS + g * NS:DI + NG * NS + (g + 1) * NS]
            s = _dot(cg, bg, NT)
            sl = slice(g * GW, (g + 1) * GW)
            hg = hs_ref[0, :, sl]
            dhn = dh_scr[:, sl]
            eal = eacs[Q - 1:Q, sl]
            gg = gy[:, sl]
            dax = gg * _dot(cg, hg, NN)
            dcg = _dot(gg, hg, NT)
            dh_scr[:, sl] = _dot(cg, gg, TN) + dhn * eal
            dal = eal * _colsum(dhn * hg)
            xdd = xdt[:, sl] * dec[:, sl]
            dbg = _dot(xdd, dhn, NT)
            wv = _dot(bg, dhn, NN)
            dd = wv * xdd
            dax = dax - dd
            dal = dal + _colsum(dd)
            dax = dax + jnp.where(last_row, dal, 0.0)
            dxdt_g = wv * dec[:, sl]
            ds = jnp.zeros((Q, Q), F32)
            dxdt_blocks = []
            for j in range(4):
                lo = g * GW + j * 128
                xb = xdt[:, lo:lo + 128]
                dyb = dyv[:, lo:lo + 128]
                dxb = dxdt_g[:, j * 128:(j + 1) * 128]
                for e in range(2):
                    h = g * 8 + j * 2 + e
                    lm = jnp.exp(jnp.where(causal, acs[:, h:h + 1] - acs_t[h:h + 1, :], NEG))
                    m = s * lm
                    dym = jnp.where(first if e == 0 else jnp.logical_not(first), dyb, 0.0)
                    dm = _dot(dym, xb, NT)
                    r = dm * m
                    dacs = dacs + jnp.where(lane_h == h, jnp.sum(r, axis=1, keepdims=True), 0.0)
                    dacs_t = dacs_t + jnp.where(sub_h == h, _colsum(r), 0.0)
                    ds = ds + dm * lm
                    dxb = dxb + _dot(m, dym, TN)
                dxdt_blocks.append(dxb)
            dxdt = jnp.concatenate(dxdt_blocks, axis=1)
            dcg = dcg + _dot(ds, bg, NN)
            dbg = dbg + _dot(ds, cg, TN)
            dxbc_ref[:, DI + g * NS:DI + (g + 1) * NS] = dbg
            dxbc_ref[:, DI + NG * NS + g * NS:DI + NG * NS + (g + 1) * NS] = dcg
            dxbc_ref[:, sl] = dsk_ref[:, sl] * dyv[:, sl] + dxdt * dt_x[:, sl]
            ddt_g, dacs_g = _reduce_heads([dxdt * xs[:, sl], dax], e_ref[0:128, sl])
            ddt = ddt + ddt_g
            dacs = dacs + dacs_g
        dacs = dacs - dacs_t.T
        ddta = _tri_sum(u_ref, dacs)
        ddt = ddt + ddta * arow_ref[...]
        ddtw = jnp.where(lane_h < NH, ddt * _sigmoid(pre), 0.0)
        ddtw_ref[...] = jnp.concatenate([ddtw, jnp.zeros((Q, DT_PAD - 128), F32)], axis=1).astype(ddtw_ref.dtype)
        _acc_out(da_ref, _colsum(ddta * dt), i)
        _acc_out(ddx_ref, _colsum(dyv * xs), i)
        _acc_out(ddtb_ref, _colsum(ddtw), i)

    rev = lambda c: (nc - 1 - c, 0)
    const = lambda c: (0, 0)
    return pl.pallas_call(
        body, name=name, grid=(nc,),
        in_specs=[pl.BlockSpec((Q, DI), rev), pl.BlockSpec((Q, XBC), rev),
                  pl.BlockSpec((Q, DT_PAD), lambda c: (nc - 1 - c, C_DT // DT_PAD)),
                  pl.BlockSpec((1, NS, DI), lambda c: (nc - 1 - c, 0, 0)),
                  pl.BlockSpec((1, 128), const), pl.BlockSpec((1, 128), const), pl.BlockSpec((1, DI), const),
                  pl.BlockSpec((Q, Q), const), pl.BlockSpec((Q, Q), const), pl.BlockSpec((256, DI), const),
                  pl.BlockSpec(memory_space=pl.ANY)],
        out_specs=[pl.BlockSpec((Q, XBC), rev),
                   pl.BlockSpec((Q, DT_PAD), lambda c: (nc - 1 - c, C_DT // DT_PAD)),
                   pl.BlockSpec((1, 128), const), pl.BlockSpec((1, DI), const), pl.BlockSpec((1, 128), const)],
        out_shape=[jax.ShapeDtypeStruct((L, XBC), F32), jax.ShapeDtypeStruct((L, NPROJ), BF16),
                   jax.ShapeDtypeStruct((1, 128), F32), jax.ShapeDtypeStruct((1, DI), F32),
                   jax.ShapeDtypeStruct((1, 128), F32)],
        scratch_shapes=[pltpu.VMEM((NS, DI), F32)],
        input_output_aliases={10: 1},
        compiler_params=_params(("arbitrary",), VMEM_BIG))(dy, xbc, proj, hs, dtb, arow, dsk_x, tri, triu,
                                                          expand, dproj)


def _adamw(w, g, m, v, *, name, tr=None):
    R, C = w.shape
    if tr is None:
        tr = _pick(R, (256, 128, 64, 32, 16, 8))
    assert R % tr == 0
    c1 = 1.0 - ADAM_B1 ** ADAM_STEP
    c2 = 1.0 - ADAM_B2 ** ADAM_STEP

    def body(w_ref, g_ref, m_ref, v_ref, d_ref, mo_ref, vo_ref):
        gv = g_ref[...]
        mn = ADAM_B1 * m_ref[...] + (1.0 - ADAM_B1) * gv
        vn = ADAM_B2 * v_ref[...] + (1.0 - ADAM_B2) * (gv * gv)
        mo_ref[...] = mn
        vo_ref[...] = vn
        d_ref[...] = -ADAM_LR * ((mn / c1) / (jnp.sqrt(vn / c2) + ADAM_EPS) + ADAM_WD * w_ref[...])

    spec = pl.BlockSpec((tr, C), lambda i: (i, 0))
    return pl.pallas_call(body, name=name, grid=(pl.cdiv(R, tr),), in_specs=[spec] * 4, out_specs=[spec] * 3,
                          out_shape=[jax.ShapeDtypeStruct((R, C), F32)] * 3,
                          compiler_params=_params(("parallel",)))(w, g, m, v)


def _slab_sum(recv, *, tile, name):
    rows = recv.shape[1]
    assert rows % tile == 0 and tile % 16 == 0

    def body(r_ref, o_ref):
        acc = r_ref[0].astype(F32)
        for j in range(1, N_DEV):
            acc = acc + r_ref[j].astype(F32)
        o_ref[...] = acc

    return pl.pallas_call(body, name=name, grid=(rows // tile,),
                          in_specs=[pl.BlockSpec((N_DEV, tile, D), lambda i: (0, i, 0))],
                          out_specs=pl.BlockSpec((tile, D), lambda i: (i, 0)),
                          out_shape=jax.ShapeDtypeStruct((rows, D), F32),
                          compiler_params=_params(("parallel",)))(recv)


MESH = pl.DeviceIdType.MESH


def _coords():
    return lax.axis_index("x"), lax.axis_index("y"), lax.axis_index("c")


def _peer(k):
    x, y, c = _coords()
    px = 1 - x if k & 4 else x
    py = 1 - y if k & 2 else y
    pc = 1 - c if k & 1 else c
    return (px, py, pc), 4 * px + 2 * py + pc


def _rcopy(src, dst, ssem, rsem, dev):
    return pltpu.make_async_remote_copy(src_ref=src, dst_ref=dst, send_sem=ssem, recv_sem=rsem,
                                        device_id=dev, device_id_type=MESH)


def _exchange_all(src_of, dst_slot, send_sems, recv_sems):
    x, y, c = _coords()
    me = 4 * x + 2 * y + c
    sent = []
    for k in range(1, N_DEV):
        dev, pidx = _peer(k)
        cp = _rcopy(src_of(pidx), dst_slot(me), send_sems.at[k - 1], recv_sems.at[k - 1], dev)
        cp.start()
        sent.append(cp)
    for k in range(1, N_DEV):
        dev, pidx = _peer(k)
        _rcopy(src_of(pidx), dst_slot(pidx), send_sems.at[k - 1], recv_sems.at[k - 1], dev).wait_recv()
    for cp in sent:
        cp.wait_send()


def _rows_of_slots(buf, nslots):
    rows = lax.broadcasted_iota(jnp.int32, (8, buf.shape[-1]), 0)
    out = jnp.zeros((8, buf.shape[-1]), F32)
    for j in range(nslots):
        out = out + jnp.where(rows == j, buf[j], 0.0)
    return out


def _ada_fwd(c, w_ada, b_r, *, name):
    wloc = w_ada.shape[1]

    def body(c_ref, w_ref, b_ref, mod_ref, call_ref, csrc, cbuf, psrc, pbuf, s1, r1, s2, r2):
        x, y, cc = _coords()
        me = 4 * x + 2 * y + cc
        csrc[...] = jnp.broadcast_to(c_ref[...], (8, D))
        cbuf[me] = csrc[...]
        _exchange_all(lambda p: csrc, lambda s: cbuf.at[s], s1, r1)
        call = _rows_of_slots(cbuf, N_DEV)
        call_ref[...] = call
        prod = _dot_hi(_silu(call), w_ref[...])
        for b in range(N_DEV):
            psrc[b] = jnp.broadcast_to(prod[b:b + 1, :], (8, wloc))
        pbuf[me] = psrc[me]
        _exchange_all(lambda p: psrc.at[p], lambda s: pbuf.at[s], s2, r2)
        mod_ref[...] = _rows_of_slots(pbuf, N_DEV) + b_ref[...]

    vm = pl.BlockSpec(memory_space=pltpu.VMEM)
    return pl.pallas_call(
        body, name=name, in_specs=[vm, vm, vm], out_specs=[vm, vm],
        out_shape=[jax.ShapeDtypeStruct((N_DEV, wloc), F32), jax.ShapeDtypeStruct((N_DEV, D), F32)],
        scratch_shapes=[pltpu.VMEM((8, D), F32), pltpu.VMEM((N_DEV, 8, D), F32),
                        pltpu.VMEM((N_DEV, 8, wloc), F32), pltpu.VMEM((N_DEV, 8, wloc), F32),
                        pltpu.SemaphoreType.DMA((N_DEV - 1,)), pltpu.SemaphoreType.DMA((N_DEV - 1,)),
                        pltpu.SemaphoreType.DMA((N_DEV - 1,)), pltpu.SemaphoreType.DMA((N_DEV - 1,))],
        compiler_params=pltpu.CompilerParams(vmem_limit_bytes=VMEM_BIG))(c, w_ada, b_r)


def _gather_slabs(slab, *, name):
    def body(x_ref, out_ref, send_sems, recv_sems, local_sem):
        x, y, c = _coords()
        me, sibling = (x, y, c), (x, y, 1 - c)
        chips = [(1 - x, y), (x, 1 - y), (1 - x, 1 - y)]

        def slot(px, py, pc):
            return out_ref.at[4 * px + 2 * py + pc]

        def copy(k, block, to, src=None):
            return _rcopy(slot(*block) if src is None else src, slot(*block), send_sems.at[k], recv_sems.at[k], to)

        mine = pltpu.make_async_copy(x_ref, slot(*me), local_sem)
        mine.start()
        first = [copy(0, me, sibling, src=x_ref)]
        first += [copy(1 + j, me, (*chip, c), src=x_ref) for j, chip in enumerate(chips)]
        for cp in first:
            cp.start()
        passed = [copy(4 + j, (*chip, c), sibling) for j, chip in enumerate(chips)]
        for j, chip in enumerate(chips):
            copy(1 + j, (*chip, c), me).wait_recv()
            passed[j].start()
        copy(0, sibling, me).wait_recv()
        for j, chip in enumerate(chips):
            copy(4 + j, (*chip, 1 - c), me).wait_recv()
        for cp in first + passed:
            cp.wait_send()
        mine.wait()

    anyspec = pl.BlockSpec(memory_space=pl.ANY)
    return pl.pallas_call(
        body, name=name, in_specs=[anyspec], out_specs=anyspec,
        out_shape=jax.ShapeDtypeStruct((N_DEV,) + slab.shape, slab.dtype),
        scratch_shapes=[pltpu.SemaphoreType.DMA((7,)), pltpu.SemaphoreType.DMA((7,)), pltpu.SemaphoreType.DMA],
    )(slab)


_HBM =pl.BlockSpec(memory_space=pltpu.HBM)
_SEM = pl.BlockSpec(memory_space=pltpu.SEMAPHORE)
_EFFECT = pltpu.SideEffectType.DATAFLOW_SIDE_EFFECTING


def _xchg_src(src_ref, pidx, per_peer):
    return src_ref.at[pidx] if per_peer else src_ref


def _xchg_start(src, *, per_peer, name):
    rows = src.shape[-2]
    land_shape = (N_DEV, rows, D)

    def body(src_ref, land_ref, send_sems, recv_sems, src_thru, land_thru, token):
        del src_thru, land_thru
        x, y, c = _coords()
        me = 4 * x + 2 * y + c
        for k in range(1, N_DEV):
            dev, pidx = _peer(k)
            _rcopy(_xchg_src(src_ref, pidx, per_peer), land_ref.at[me], send_sems.at[k - 1],
                   recv_sems.at[k - 1], dev).start()
        token[...] = jnp.zeros_like(token)

    return pl.pallas_call(
        body, name=name,
        out_shape=(pltpu.SemaphoreType.DMA((N_DEV - 1,)), pltpu.SemaphoreType.DMA((N_DEV - 1,)),
                   pltpu.HBM(src.shape, src.dtype), pltpu.HBM(land_shape, src.dtype),
                   jax.ShapeDtypeStruct((8, 128), F32)),
        in_specs=(_HBM, _HBM),
        out_specs=(_SEM, _SEM, _HBM, _HBM, pl.BlockSpec(memory_space=pltpu.VMEM)),
        input_output_aliases={0: 2, 1: 3},
        compiler_params=pltpu.CompilerParams(has_side_effects=_EFFECT),
    )(pltpu.with_memory_space_constraint(src, pltpu.HBM),
      pltpu.with_memory_space_constraint(lax.empty(land_shape, src.dtype), pltpu.HBM))


def _xchg_wait(started, after, *, per_peer, name):
    send_sems, recv_sems, src_thru, land_thru, _ = started

    def body(src_ref, land_ref, send_sems, recv_sems, after_ref, src_dead, got_ref):
        del after_ref, src_dead, got_ref
        for k in range(1, N_DEV):
            dev, pidx = _peer(k)
            cp = _rcopy(_xchg_src(src_ref, pidx, per_peer), land_ref.at[pidx], send_sems.at[k - 1],
                        recv_sems.at[k - 1], dev)
            cp.wait_send()
            cp.wait_recv()

    return pl.pallas_call(
        body, name=name,
        out_shape=(pltpu.HBM(src_thru.shape, src_thru.dtype), pltpu.HBM(land_thru.shape, land_thru.dtype)),
        in_specs=(_HBM, _HBM, _SEM, _SEM, pl.BlockSpec(memory_space=pl.ANY)),
        out_specs=(_HBM, _HBM),
        input_output_aliases={0: 0, 1: 1},
        compiler_params=pltpu.CompilerParams(has_side_effects=_EFFECT),
    )(src_thru, land_thru, send_sems, recv_sems, after)


def _dep(token):
    return (token, (8, 128), lambda i, j, k: (0, 0))


def _small_allsum(sv, *, name):
    def body(sv_ref, all_ref, sum_ref, send_sems, recv_sems):
        x, y, c = _coords()
        me = 4 * x + 2 * y + c
        all_ref[me] = sv_ref[...]
        _exchange_all(lambda p: sv_ref, lambda s: all_ref.at[s], send_sems, recv_sems)
        acc = all_ref[0]
        for j in range(1, N_DEV):
            acc = acc + all_ref[j]
        sum_ref[...] = acc

    vm = pl.BlockSpec(memory_space=pltpu.VMEM)
    return pl.pallas_call(
        body, name=name, in_specs=[vm], out_specs=[vm, vm],
        out_shape=[jax.ShapeDtypeStruct((N_DEV, SV_ROWS, 128), F32), jax.ShapeDtypeStruct((SV_ROWS, 128), F32)],
        scratch_shapes=[pltpu.SemaphoreType.DMA((7,)), pltpu.SemaphoreType.DMA((7,))],
    )(sv)


def _ada_bwd(call, dmod_loc, *, name):
    wloc = dmod_loc.shape[1]

    def body(c_ref, d_ref, o_ref):
        o_ref[...] = _dot_hi(_silu(c_ref[...]), d_ref[...], TN)

    vm = pl.BlockSpec(memory_space=pltpu.VMEM)
    return pl.pallas_call(body, name=name, in_specs=[vm, vm], out_specs=vm,
                          out_shape=jax.ShapeDtypeStruct((D, wloc), F32),
                          compiler_params=pltpu.CompilerParams(vmem_limit_bytes=VMEM_BIG))(call, dmod_loc)


def _pad_rows(a, rows):
    return jnp.pad(a, ((0, rows - a.shape[0]), (0, 0)))


def _reorder_in_rows(wt):
    z, xbc, dt, pool, gates = wt[0:2048], wt[2048:5120], wt[5120:5152], wt[5152:6176], wt[6176:8224]
    return jnp.concatenate([xbc, pool, z, gates, dt, jnp.zeros((DT_PAD - 32, D), wt.dtype)], axis=0)


def _restore_in_rows(d):
    return jnp.concatenate([d[C_Z:C_Z + 2048], d[C_XBC:C_XBC + XBC], d[C_DT:C_DT + 32],
                            d[C_POOL:C_POOL + 1024], d[C_GATE:C_GATE + 2048]], axis=0)


def _pack_sv(parts):
    flat = []
    for n, size in SV_PARTS:
        v = parts[n].reshape(-1).astype(F32)
        flat.append(jnp.pad(v, (0, size - v.shape[0])))
    v = jnp.concatenate(flat)
    return jnp.pad(v, (0, SV_ROWS * 128 - v.shape[0])).reshape(SV_ROWS, 128)


def _sv_get(flat, n, size):
    return flat[SV_OFF[n]:SV_OFF[n] + size]


def kernel(x, c, w_ada, b_ada, norm_mix_w, w_in, conv_w, conv_b, dt_bias, a_log, d_skip, ssd_norm_w, w_branch_ssd, pool_w, pool_scale, w_branch_pool, w_out, norm_mlp_w, w_up, w_down, norm_final_w, loss_target, m_w_ada, m_b_ada, m_norm_mix_w, m_w_in, m_conv_w, m_conv_b, m_dt_bias, m_a_log, m_d_skip, m_ssd_norm_w, m_w_branch_ssd, m_pool_w, m_pool_scale, m_w_branch_pool, m_w_out, m_norm_mlp_w, m_w_up, m_w_down, m_norm_final_w, v_w_ada, v_b_ada, v_norm_mix_w, v_w_in, v_conv_w, v_conv_b, v_dt_bias, v_a_log, v_d_skip, v_ssd_norm_w, v_w_branch_ssd, v_pool_w, v_pool_scale, v_w_branch_pool, v_w_out, v_norm_mlp_w, v_w_up, v_w_down, v_norm_final_w):
    xs_ = x[0]
    tgt = loss_target[0]
    L = xs_.shape[0]
    me = 4 * lax.axis_index("x") + 2 * lax.axis_index("y") + lax.axis_index("c")
    wloc = w_ada.shape[2]

    mod_p, c_all = _ada_fwd(c, w_ada[0], b_ada.reshape(N_DEV, wloc), name="ada_fwd")
    mod = mod_p.reshape(6, D)
    shift_m, scale_m, gate_m, shift_f, scale_f, gate_f = [mod[i:i + 1] for i in range(6)]

    conv_bits = lax.bitcast_convert_type(conv_w[0], SLAB_DT).reshape(3, D)
    slab_in = jnp.concatenate([_pad_rows(w_in[0].T.astype(SLAB_DT), IN_ROWS_P),
                               _pad_rows(conv_bits, CONV_ROWS)], axis=0)
    slab_rest = jnp.concatenate([
        w_branch_ssd[0].astype(SLAB_DT),
        pool_w[0].reshape(32, D).astype(SLAB_DT),
        w_branch_pool[0].astype(SLAB_DT),
        w_out[0].astype(SLAB_DT),
        w_up[0].T.astype(SLAB_DT),
        w_down[0].astype(SLAB_DT)], axis=0)
    gs_in = _gather_slabs(slab_in, name="gather_w_in")
    slab_rest, gs_in = lax.optimization_barrier((slab_rest, gs_in))
    rest_started = _xchg_start(slab_rest, per_peer=False, name="gather_rest_start")
    gather_token = rest_started[4]

    w_in_t = _reorder_in_rows(gs_in[:, :IN_ROWS].reshape(N_IN, D))
    conv_full = lax.bitcast_convert_type(
        gs_in[:, IN_ROWS_P:IN_ROWS_P + 3].reshape(N_DEV, 4, XBC // N_DEV, 2), F32)
    conv_full = conv_full.transpose(1, 0, 2).reshape(4, XBC)

    dtb = jnp.pad(dt_bias, ((0, 0), (0, 128 - NH)))
    arow = jnp.pad(-jnp.exp(a_log), ((0, 0), (0, 128 - NH)))
    dsk_x = jnp.repeat(d_skip, HP, axis=1)

    tm = _pick(L, (1024, 512, 256, 128))
    tm2 = _pick(L, (2048, 1024, 512, 256, 128))
    tkl = _pick(L, (4096, 2048, 1024, 512, 256, 128))
    tkl2 = _pick(L, (2048, 1024, 512, 256, 128))

    h1 = _norm_fwd(xs_, norm_mix_w, scale_m, shift_m, name="norm1_fwd")
    proj = _mm(h1, w_in_t, "nt", name="in_proj", outs=[F32], tm=tm2, tn=768, tk=D,
               extras=[_dep(gather_token)])
    xbc = _conv_fwd(proj, conv_full, conv_b, name="conv_fwd")
    y_ssm, hs = _ssd_fwd(xbc, proj, dtb, arow, dsk_x, name="ssd_fwd")
    yn = _gated_norm_fwd(y_ssm, proj, ssd_norm_w, name="gated_norm_fwd")

    slab_rest, gs = _xchg_wait(rest_started, yn, per_peer=False, name="gather_rest_wait")
    gs = lax.dynamic_update_slice(gs, slab_rest[None], (me, 0, 0))

    def part(n, rows):
        return gs[:, REST_OFF[n]:REST_OFF[n] + rows]

    w_bssd = part("bssd", 256).reshape(DI, D)
    w_pool = part("pool", 32).reshape(N_DEV, 4, 32, PGW).transpose(1, 0, 2, 3).reshape(POOL_W, PGW)
    w_bpool = part("bpool", 128).reshape(POOL_W, D)
    w_o = part("out", 128).reshape(D, D)
    w_up_t = part("up", 512).reshape(DFF, D)
    w_dn = part("down", 512).reshape(DFF, D)

    y_ssd = _mm(yn, w_bssd, "nn", name="branch_ssd", outs=[F32], tm=tm2, tn=D, tk=DI)
    pooled = _pool_fwd(proj, name="pool_fwd")
    yp0, yp1 = _mm_pool(pooled, w_pool, pool_scale, name="pool_mix", tm=tm, transpose_w=False)
    y_pool = _mm(yp1, w_bpool, "nn", name="branch_pool", outs=[F32], tm=tm2, tn=D, tk=D)
    m = _merge_fwd(y_ssd, y_pool, proj, name="merge_fwd")
    tmh = _pick(L, (512, 256, 128))
    mix, x1, h2 = _mm(m, w_o, "nn", name="out_proj", tm=tm, tn=D, tk=D,
                      extras=[(xs_, *_rows(tm)), (gate_m, *_vecs()), (norm_mlp_w, *_vecs()),
                              (scale_f, *_vecs()), (shift_f, *_vecs())],
                      outs=[F32, F32, BF16], epilogue=_ep_resid_norm)

    def relu2(acc, ex, outs):
        r = jnp.maximum(acc, 0.0)
        outs[0][...] = acc.astype(BF16)
        outs[1][...] = (r * r).astype(BF16)

    up, act = _mm(h2, w_up_t, "nt", name="mlp_up", outs=[BF16, BF16], tm=tm2, tn=1024, tk=D, epilogue=relu2)

    dx2, ddown, loss_p, dnwf, dgate_f = _mm(
        act, w_dn, "nn", name="mlp_down", tm=tmh, tn=D, tk=DFF,
        extras=[(x1, *_rows(tmh)), (tgt, *_rows(tmh)), (gate_f, *_vecs()), (norm_final_w.reshape(1, D), *_vecs())],
        outs=[F32, BF16, _sum_out(128), _sum_out(), _sum_out()], epilogue=_ep_final)

    def drelu2(acc, ex, outs):
        outs[0][...] = (acc * (2.0 * jnp.maximum(ex[0][...].astype(F32), 0.0))).astype(BF16)

    def dep_last(ep):
        return lambda acc, ex, outs: ep(acc, ex[:-1], outs)

    dup = _mm(ddown, w_dn, "nt", name="mlp_down_dx", outs=[BF16], tm=tm2, tn=1024, tk=D,
              extras=[(up, (tm2, 1024), lambda i, j, k: (i, j))], epilogue=drelu2)
    g_dn = _mm(act, ddown, "tn", name="mlp_down_dw", outs=[SLAB_DT], tm=1024, tn=D, tk=tkl)
    g_up_t = _mm(dup, h2, "tn", name="mlp_up_dw", outs=[SLAB_DT], tm=1024, tn=D, tk=tkl)
    gslab_mlp = jnp.concatenate([g_up_t.reshape(N_DEV, 512, D), g_dn.reshape(N_DEV, 512, D)], axis=1)
    mlp_started = _xchg_start(gslab_mlp, per_peer=True, name="scatter_mlp_start")
    dx1, p2, q2, dmix, dgate_m = _mm(
        dup, w_up_t, "nn", name="mlp_up_dx", tm=tmh, tn=D, tk=DFF,
        extras=[(x1, *_rows(tmh)), (dx2, *_rows(tmh)), (norm_mlp_w, *_vecs()), (scale_f, *_vecs()),
                (mix, *_rows(tmh)), (gate_m, *_vecs()), _dep(mlp_started[4])],
        outs=[F32, _sum_out(), _sum_out(), BF16, _sum_out()], epilogue=dep_last(_ep_norm_bwd))
    gcol = C_GATE // (2 * D)
    dy_ssd, dy_pool, dproj = _mm(
        dmix, w_o, "nt", name="out_proj_dx", tm=tmh, tn=D, tk=D,
        extras=[(y_ssd, *_rows(tmh)), (y_pool, *_rows(tmh)), (proj, *_rows(tmh, 2 * D, gcol))],
        outs=[BF16, BF16, ((L, NPROJ), BF16, *_rows(tmh, 2 * D, gcol))], epilogue=_ep_merge_bwd)
    g_o = _mm(m, dmix, "tn", name="out_proj_dw", outs=[SLAB_DT], tm=D, tn=D, tk=tkl)
    dyn = _mm(dy_ssd, w_bssd, "nt", name="branch_ssd_dx", outs=[F32], tm=tm2, tn=1024, tk=D)
    g_bssd = _mm(yn, dy_ssd, "tn", name="branch_ssd_dw", outs=[SLAB_DT], tm=1024, tn=D, tk=tkl)
    dy_ssm, dproj, d_snw = _gated_norm_bwd(dyn, y_ssm, proj, ssd_norm_w, dproj, name="gated_norm_bwd")
    dxbc, dproj, d_a, d_dx, d_dtb = _ssd_bwd(dy_ssm, xbc, proj, hs, dtb, arow, dsk_x, dproj, name="ssd_bwd")
    dproj, d_cw, d_cb = _conv_bwd(proj, dxbc, conv_full, conv_b, dproj, name="conv_bwd")
    dyp0, d_ps = _mm(dy_pool, w_bpool, "nt", name="branch_pool_dx", tm=tm, tn=D, tk=D,
                     extras=[(yp0, *_rows(tm)), (pool_scale, *_vecs())],
                     outs=[BF16, _sum_out()], epilogue=_ep_pscale_bwd)
    g_bpool = _mm(yp1, dy_pool, "tn", name="branch_pool_dw", outs=[SLAB_DT], tm=D, tn=D, tk=tkl)
    dpooled = _mm_pool(dyp0, w_pool, None, name="pool_mix_dx", tm=tm, transpose_w=True)
    g_pool = _mm_pool_tn(pooled, dyp0, name="pool_mix_dw", tk=tkl)
    gslab_mix = jnp.concatenate([
        g_bssd.reshape(N_DEV, 256, D),
        g_pool.reshape(4, N_DEV, 32, PGW).transpose(1, 0, 2, 3).reshape(N_DEV, 32, D).astype(SLAB_DT),
        g_bpool.reshape(N_DEV, 128, D),
        g_o.reshape(N_DEV, 128, D)], axis=1)
    mix_started = _xchg_start(gslab_mix, per_peer=True, name="scatter_mix_start")
    dproj = _pool_bwd(dpooled, dproj, name="pool_bwd")
    g_in_t = _mm(dproj, h1, "tn", name="in_proj_dw", outs=[SLAB_DT], tm=1408, tn=D, tk=tkl2,
                 extras=[_dep(mix_started[4])])
    gslab_in = jnp.pad(_restore_in_rows(g_in_t).reshape(N_DEV, IN_ROWS, D),
                       ((0, 0), (0, IN_ROWS_P - IN_ROWS), (0, 0)))
    in_started = _xchg_start(gslab_in, per_peer=True, name="scatter_in_start")
    grad_x, p1, q1 = _mm(
        dproj, w_in_t, "nn", name="in_proj_dx", tm=tmh, tn=D, tk=2816,
        extras=[(xs_, *_rows(tmh)), (dx1, *_rows(tmh)), (norm_mix_w, *_vecs()), (scale_m, *_vecs()),
                _dep(in_started[4])],
        outs=[F32, _sum_out(), _sum_out()], epilogue=dep_last(_ep_norm_bwd))

    def landed(started, after, tile, name):
        src, land = _xchg_wait(started, after, per_peer=True, name=name + "_wait")
        own = lax.dynamic_slice_in_dim(src, me, 1, axis=0)
        return _slab_sum(lax.dynamic_update_slice(land, own, (me, 0, 0)), tile=tile, name=name + "_sum")

    gsum_mlp = landed(mlp_started, grad_x, 256, "scatter_mlp")
    gsum_mix = landed(mix_started, grad_x, 272, "scatter_mix")
    gsum_in = landed(in_started, grad_x, 208, "scatter_in")

    dmod = jnp.concatenate([q1, p1 * norm_mix_w, dgate_m, q2, p2 * norm_mlp_w, dgate_f], axis=1)
    d_alog = d_a[:, :NH] * (-jnp.exp(a_log))
    sv = _pack_sv({
        "b_ada": dmod, "norm_mix_w": p1 * (1.0 + scale_m), "conv_b": d_cb, "dt_bias": d_dtb[:, :NH],
        "a_log": d_alog, "d_skip": d_dx.reshape(NH, HP).sum(axis=1), "ssd_norm_w": d_snw,
        "pool_scale": d_ps, "norm_mlp_w": p2 * (1.0 + scale_f), "norm_final_w": dnwf, "conv_w": d_cw,
        "loss": loss_p[:, :1]})
    sv_all, sv_sum = _small_allsum(sv, name="small_allsum")
    flat = sv_sum.reshape(-1)
    loss = flat[SV_OFF["loss"]]
    dmod_all = sv_all.reshape(N_DEV, SV_ROWS * 128)[:, :6 * D]
    g_w_ada = _ada_bwd(c_all, lax.dynamic_slice_in_dim(dmod_all, me * wloc, wloc, axis=1), name="ada_bwd")

    g_conv_w = lax.dynamic_slice_in_dim(_sv_get(flat, "conv_w", 4 * XBC).reshape(4, XBC),
                                        me * (XBC // N_DEV), XBC // N_DEV, axis=1)
    small = {
        "b_ada": (b_ada, m_b_ada, v_b_ada, _sv_get(flat, "b_ada", 6 * D)),
        "norm_mix_w": (norm_mix_w, m_norm_mix_w, v_norm_mix_w, _sv_get(flat, "norm_mix_w", D)),
        "conv_b": (conv_b, m_conv_b, v_conv_b, _sv_get(flat, "conv_b", XBC)),
        "dt_bias": (dt_bias, m_dt_bias, v_dt_bias, _sv_get(flat, "dt_bias", NH)),
        "a_log": (a_log, m_a_log, v_a_log, _sv_get(flat, "a_log", NH)),
        "d_skip": (d_skip, m_d_skip, v_d_skip, _sv_get(flat, "d_skip", NH)),
        "ssd_norm_w": (ssd_norm_w, m_ssd_norm_w, v_ssd_norm_w, _sv_get(flat, "ssd_norm_w", DI)),
        "pool_scale": (pool_scale, m_pool_scale, v_pool_scale, _sv_get(flat, "pool_scale", POOL_W)),
        "norm_mlp_w": (norm_mlp_w, m_norm_mlp_w, v_norm_mlp_w, _sv_get(flat, "norm_mlp_w", D)),
        "norm_final_w": (norm_final_w, m_norm_final_w, v_norm_final_w, _sv_get(flat, "norm_final_w", D)),
        "conv_w": (conv_w, m_conv_w, v_conv_w, g_conv_w),
    }
    names = list(small)
    sizes = [int(np.prod(small[n][0].shape)) for n in names]
    tot = sum(sizes)
    rows = -(-tot // 1024) * 8

    def pack(idx):
        v = jnp.concatenate([small[n][idx].reshape(-1) for n in names])
        return jnp.pad(v, (0, rows * 128 - tot)).reshape(rows, 128)

    sd, sm, sv2 = _adamw(pack(0), pack(3), pack(1), pack(2), name="adamw_small")
    small_out = {}
    off = 0
    for n, sz in zip(names, sizes):
        shp = small[n][0].shape
        small_out[n] = (small[n][3].reshape(shp), sd.reshape(-1)[off:off + sz].reshape(shp),
                        sm.reshape(-1)[off:off + sz].reshape(shp), sv2.reshape(-1)[off:off + sz].reshape(shp))
        off += sz

    def gpart(n, rows_):
        return gsum_mix[MIX_OFF[n]:MIX_OFF[n] + rows_]

    def lin(a):
        return a[0].T.reshape(IN_ROWS * 8, 128)

    g_lin = gsum_in[:IN_ROWS].reshape(IN_ROWS * 8, 128)
    dlt, mn, vn = _adamw(lin(w_in), g_lin, lin(m_w_in), lin(v_w_in), name="adamw_w_in", tr=IN_ROWS * 2)
    big_in = tuple(a.reshape(IN_ROWS, D).T[None] for a in (g_lin, dlt, mn, vn))

    big = {
        "w_ada": (w_ada, m_w_ada, v_w_ada, g_w_ada, (D, wloc)),
        "w_branch_ssd": (w_branch_ssd, m_w_branch_ssd, v_w_branch_ssd, gpart("bssd", 256), (256, D)),
        "pool_w": (pool_w, m_pool_w, v_pool_w, gpart("pool", 32).reshape(128, PGW), (128, PGW)),
        "w_branch_pool": (w_branch_pool, m_w_branch_pool, v_w_branch_pool, gpart("bpool", 128), (128, D)),
        "w_out": (w_out, m_w_out, v_w_out, gpart("out", 128), (128, D)),
        "w_up": (w_up, m_w_up, v_w_up, gsum_mlp[:512].T, (D, 512)),
        "w_down": (w_down, m_w_down, v_w_down, gsum_mlp[512:], (512, D)),
    }
    big_out = {}
    for n, (w, mm_, vv, g, shp2) in big.items():
        dlt, mn, vn = _adamw(w.reshape(shp2), g, mm_.reshape(shp2), vv.reshape(shp2), name="adamw_" + n)
        big_out[n] = (g.reshape(w.shape), dlt.reshape(w.shape), mn.reshape(w.shape), vn.reshape(w.shape))

    order = ["w_ada", "b_ada", "norm_mix_w", "w_in", "conv_w", "conv_b", "dt_bias", "a_log", "d_skip",
             "ssd_norm_w", "w_branch_ssd", "pool_w", "pool_scale", "w_branch_pool", "w_out", "norm_mlp_w",
             "w_up", "w_down", "norm_final_w"]
    big_out["w_in"] = big_in
    res = {**small_out, **big_out}
    outs = [loss, grad_x.reshape(x.shape)]
    for k in range(4):
        outs += [res[n][k] for n in order]
    return tuple(outs)
```

```python
import functools

import numpy as np
import jax
import jax.numpy as jnp
from jax import lax
from jax.experimental import pallas as pl
from jax.experimental.pallas import tpu as pltpu

F32 = jnp.float32
BF16 = jnp.bfloat16
SLAB_DT = jnp.bfloat16
_MXU_DTYPE = jnp.bfloat16

N_DEV = 8
D = 1024
DI = 2048
NH = 32
HP = 64
NG = 4
NS = 128
Q = 128
XBC = DI + 2 * NG * NS
DFF = 4096
N_IN = 8224
EPS = 1e-5
POOL_W = 1024
PGW = 256

C_XBC, C_POOL, C_Z, C_GATE, C_DT = 0, 3072, 4096, 6144, 8192
DT_PAD = 256
NPROJ = C_DT + DT_PAD

IN_ROWS = N_IN // N_DEV
IN_ROWS_P = 1040
CONV_ROWS = 16
REST_PARTS = (("bssd", 256), ("pool", 32), ("bpool", 128), ("out", 128), ("up", 512), ("down", 512))
REST_OFF = {}
_o = 0
for _n, _r in REST_PARTS:
    REST_OFF[_n] = _o
    _o += _r
REST_ROWS = _o
MIX_PARTS = (("bssd", 256), ("pool", 32), ("bpool", 128), ("out", 128))
MIX_OFF = {}
_o = 0
for _n, _r in MIX_PARTS:
    MIX_OFF[_n] = _o
    _o += _r
MIX_ROWS = _o

SV_PARTS = (("b_ada", 6144), ("norm_mix_w", 1024), ("conv_b", 3072), ("dt_bias", 128), ("a_log", 128),
            ("d_skip", 128), ("ssd_norm_w", 2048), ("pool_scale", 1024), ("norm_mlp_w", 1024),
            ("norm_final_w", 1024), ("conv_w", 4 * XBC), ("loss", 128))
SV_OFF = {}
_o = 0
for _n, _r in SV_PARTS:
    SV_OFF[_n] = _o
    _o += _r
SV_ROWS = 224
assert _o <= SV_ROWS * 128

ADAM_LR, ADAM_B1, ADAM_B2, ADAM_EPS, ADAM_WD, ADAM_STEP = 0.001, 0.9, 0.999, 1e-08, 0.01, 10

VMEM_BIG = 56 * 1024 * 1024
NEG = -1e30

NN = ((1,), (0,))
NT = ((1,), (1,))
TN = ((0,), (0,))


def _dot(a, b, dims=NN):
    return lax.dot_general(a.astype(_MXU_DTYPE), b.astype(_MXU_DTYPE), (dims, ((), ())),
                           preferred_element_type=F32)


def _dot_hi(a, b, dims=NN):
    return lax.dot_general(a.astype(F32), b.astype(F32), (dims, ((), ())),
                           precision=lax.Precision.HIGHEST, preferred_element_type=F32)


def _pick(n, cands):
    for c in cands:
        if n % c == 0:
            return c
    return n


def _sigmoid(x):
    return 1.0 / (1.0 + jnp.exp(-x))


def _silu(x):
    return x * _sigmoid(x)


def _dsilu(x):
    s = _sigmoid(x)
    return s * (1.0 + x * (1.0 - s))


def _softplus(x):
    return jnp.maximum(x, 0.0) + jnp.log(1.0 + jnp.exp(-jnp.abs(x)))


def _params(sem, vmem=None):
    return pltpu.CompilerParams(dimension_semantics=sem, vmem_limit_bytes=vmem)


def _mm(a, b, mode, *, name, outs, tm, tn, tk, extras=(), epilogue=None, aliases=None, prologue=None):
    if mode == "tn":
        K, M = a.shape
        N = b.shape[1]
        a_spec = pl.BlockSpec((tk, tm), lambda i, j, k: (k, i))
        b_spec = pl.BlockSpec((tk, tn), lambda i, j, k: (k, j))
        dims = TN
    else:
        M = a.shape[0]
        K = b.shape[0] if mode == "nn" else b.shape[1]
        if prologue is None:
            assert a.shape[1] == K
            a_spec = pl.BlockSpec((tm, tk), lambda i, j, k: (i, k))
        else:
            assert tk == K
            a_spec = pl.BlockSpec((tm, a.shape[1]), lambda i, j, k: (i, 0))
        if mode == "nn":
            N = b.shape[1]
            b_spec = pl.BlockSpec((tk, tn), lambda i, j, k: (k, j))
            dims = NN
        else:
            N = b.shape[0]
            b_spec = pl.BlockSpec((tn, tk), lambda i, j, k: (j, k))
            dims = NT
    assert M % tm == 0 and N % tn == 0 and K % tk == 0, (name, M, N, K, tm, tn, tk)
    nk = K // tk
    ne, no = len(extras), len(outs)
    if epilogue is None:
        def epilogue(acc, ex, out_refs):
            out_refs[0][...] = acc.astype(out_refs[0].dtype)

    def body(a_ref, b_ref, *rest):
        ex, out_refs = rest[:ne], rest[ne:ne + no]
        lhs = a_ref[...] if prologue is None else prologue(a_ref, ex, out_refs, pl.program_id(1))
        p = _dot(lhs, b_ref[...], dims)
        if nk == 1:
            epilogue(p, ex, out_refs)
        else:
            acc = rest[-1]
            k = pl.program_id(2)

            @pl.when(k == 0)
            def _():
                acc[...] = p

            @pl.when(jnp.logical_and(k > 0, k < nk - 1))
            def _():
                acc[...] += p

            @pl.when(k == nk - 1)
            def _():
                epilogue(acc[...] + p, ex, out_refs)

    out_specs, out_shape = [], []
    for o in outs:
        if isinstance(o, tuple):
            shape, dt, bs, im = o
            out_specs.append(pl.BlockSpec(bs, im))
            out_shape.append(jax.ShapeDtypeStruct(shape, dt))
        else:
            out_specs.append(pl.BlockSpec((tm, tn), lambda i, j, k: (i, j)))
            out_shape.append(jax.ShapeDtypeStruct((M, N), o))
    in_specs = [a_spec, b_spec]
    for _, bs, im in extras:
        in_specs.append(pl.BlockSpec(memory_space=pl.ANY) if bs is None else pl.BlockSpec(bs, im))
    res = pl.pallas_call(
        body, name=name,
        grid=(M // tm, N // tn, nk),
        in_specs=in_specs, out_specs=out_specs, out_shape=out_shape,
        scratch_shapes=[pltpu.VMEM((tm, tn), F32)] if nk > 1 else [],
        input_output_aliases={2 + e: o for e, o in (aliases or {}).items()},
        compiler_params=_params(("arbitrary", "arbitrary", "arbitrary"), VMEM_BIG),
    )(a, b, *[e[0] for e in extras])
    return res if no > 1 else res[0]


def _rows(tm, w=D, col=0):
    return (tm, w), lambda i, j, k, c=col: (i, c)


def _vecs(w=D, col=0):
    return (1, w), lambda i, j, k, c=col: (0, c)


def _sum_out(w=D):
    return ((1, w), F32, (1, w), lambda i, j, k: (0, 0))


def _mm_pool(a, w, scale, *, name, tm, transpose_w):
    L = a.shape[0]
    dims = NT if transpose_w else NN

    def body(a_ref, w_ref, *rest):
        p = _dot(a_ref[...], w_ref[...], dims)
        if transpose_w:
            rest[0][...] = p
        else:
            s_ref, o0, o1 = rest
            o0[...] = p.astype(o0.dtype)
            o1[...] = (p * s_ref[...]).astype(o1.dtype)

    blk = pl.BlockSpec((tm, PGW), lambda i, j: (i, j))
    in_specs = [blk, pl.BlockSpec((PGW, PGW), lambda i, j: (j, 0))]
    args = [a, w]
    if transpose_w:
        out_specs, out_shape = [blk], [jax.ShapeDtypeStruct((L, POOL_W), F32)]
    else:
        in_specs.append(pl.BlockSpec((1, PGW), lambda i, j: (0, j)))
        args.append(scale)
        out_specs = [blk, blk]
        out_shape = [jax.ShapeDtypeStruct((L, POOL_W), BF16), jax.ShapeDtypeStruct((L, POOL_W), BF16)]
    res = pl.pallas_call(body, name=name, grid=(L // tm, 4), in_specs=in_specs, out_specs=out_specs,
                         out_shape=out_shape, compiler_params=_params(("parallel", "parallel")))(*args)
    return res[0] if transpose_w else res


def _mm_pool_tn(a, b, *, name, tk):
    L = a.shape[0]

    def body(a_ref, b_ref, o_ref):
        p = _dot(a_ref[...], b_ref[...], TN)

        @pl.when(pl.program_id(1) == 0)
        def _():
            o_ref[...] = p

        @pl.when(pl.program_id(1) > 0)
        def _():
            o_ref[...] += p

    blk = pl.BlockSpec((tk, PGW), lambda g, k: (k, g))
    return pl.pallas_call(body, name=name, grid=(4, L // tk), in_specs=[blk, blk],
                          out_specs=pl.BlockSpec((PGW, PGW), lambda g, k: (g, 0)),
                          out_shape=jax.ShapeDtypeStruct((POOL_W, PGW), F32),
                          compiler_params=_params(("parallel", "arbitrary")))(a, b)


def _acc_out(ref, val, i):
    @pl.when(i == 0)
    def _():
        ref[...] = val

    @pl.when(i > 0)
    def _():
        ref[...] += val


def _colsum(v):
    return jnp.sum(v, axis=0, keepdims=True)


def _ep_resid_norm(acc, ex, outs):
    x_ref, g_ref, nw_ref, sc_ref, sh_ref = ex
    mix_ref, x1_ref, h_ref = outs
    mix_ref[...] = acc.astype(mix_ref.dtype)
    xv = x_ref[...] + g_ref[...] * acc
    x1_ref[...] = xv
    r = lax.rsqrt(jnp.mean(xv * xv, axis=-1, keepdims=True) + EPS)
    h_ref[...] = (xv * r * nw_ref[...] * (1.0 + sc_ref[...]) + sh_ref[...]).astype(h_ref.dtype)


def _ep_final(acc, ex, outs):
    x1_ref, t_ref, g_ref, nw_ref = ex
    dx2_ref, dd_ref, loss_ref, dnw_ref, dg_ref = outs
    i = pl.program_id(0)
    x2 = x1_ref[...] + g_ref[...] * acc
    r = lax.rsqrt(jnp.mean(x2 * x2, axis=-1, keepdims=True) + EPS)
    xh = x2 * r
    e = xh * nw_ref[...] - t_ref[...]
    part = 0.5 * jnp.sum(jnp.mean(e * e, axis=-1, keepdims=True), axis=0, keepdims=True)
    dy = e * (1.0 / D)
    g = dy * nw_ref[...]
    dx2 = r * (g - xh * jnp.mean(g * xh, axis=-1, keepdims=True))
    dx2_ref[...] = dx2
    dd_ref[...] = (dx2 * g_ref[...]).astype(dd_ref.dtype)
    _acc_out(loss_ref, jnp.broadcast_to(part, (1, 128)), i)
    _acc_out(dnw_ref, _colsum(dy * xh), i)
    _acc_out(dg_ref, _colsum(dx2 * acc), i)


def _ep_norm_bwd(acc, ex, outs):
    x_ref, dr_ref, nw_ref, sc_ref = ex[:4]
    dx_ref, p_ref, q_ref = outs[:3]
    i = pl.program_id(0)
    xv = x_ref[...]
    r = lax.rsqrt(jnp.mean(xv * xv, axis=-1, keepdims=True) + EPS)
    xh = xv * r
    g = acc * (nw_ref[...] * (1.0 + sc_ref[...]))
    dx = dr_ref[...] + r * (g - xh * jnp.mean(g * xh, axis=-1, keepdims=True))
    dx_ref[...] = dx
    _acc_out(p_ref, _colsum(acc * xh), i)
    _acc_out(q_ref, _colsum(acc), i)
    if len(ex) > 4:
        m_ref, g_ref = ex[4:]
        dm_ref, dg_ref = outs[3:]
        dm_ref[...] = (dx * g_ref[...]).astype(dm_ref.dtype)
        _acc_out(dg_ref, _colsum(dx * m_ref[...].astype(F32)), i)


def _ep_merge_bwd(acc, ex, outs):
    a_ref, b_ref, gl_ref = ex
    da_ref, db_ref, dgl_ref = outs
    s = _sigmoid(gl_ref[...])
    s1, s2 = s[:, :D], s[:, D:]
    da_ref[...] = (acc * s1).astype(da_ref.dtype)
    db_ref[...] = (acc * s2).astype(db_ref.dtype)
    dgl_ref[:, :D] = (acc * a_ref[...] * s1 * (1.0 - s1)).astype(dgl_ref.dtype)
    dgl_ref[:, D:] = (acc * b_ref[...] * s2 * (1.0 - s2)).astype(dgl_ref.dtype)


def _ep_pscale_bwd(acc, ex, outs):
    y_ref, s_ref = ex
    o_ref, ds_ref = outs
    o_ref[...] = (acc * s_ref[...]).astype(o_ref.dtype)
    _acc_out(ds_ref, _colsum(acc * y_ref[...].astype(F32)), pl.program_id(0))


GW = DI // NG


def _ep_gated_norm_bwd(acc, ex, outs):
    y_ref, z_ref, w_ref, _ = ex
    dy_ref, dz_ref, dw_ref = outs
    zv = z_ref[...]
    yv = y_ref[...]
    sz = _silu(zv)
    yg = yv * sz
    dsz = _dsilu(zv)
    dws = []
    for k in range(NG):
        sl = slice(k * GW, (k + 1) * GW)
        seg = yg[:, sl]
        r = lax.rsqrt(jnp.mean(seg * seg, axis=-1, keepdims=True) + EPS)
        sh = seg * r
        dn = acc[:, sl]
        g = dn * w_ref[:, sl]
        dyg = r * (g - sh * jnp.mean(g * sh, axis=-1, keepdims=True))
        dy_ref[:, sl] = dyg * sz[:, sl]
        dz_ref[:, sl] = (dyg * yv[:, sl] * dsz[:, sl]).astype(dz_ref.dtype)
        dws.append(_colsum(dn * sh))
    _acc_out(dw_ref, jnp.concatenate(dws, axis=1), pl.program_id(0))


CONV_CB = 128
HALO = 16


def _time_chunk(L):
    return _pick(L, (256, 128))


def _conv_fwd(proj, w, b, *, name):
    L = proj.shape[0]
    rc = _time_chunk(L)
    n = L // rc

    def body(x_ref, w_ref, b_ref, o_ref, pad):
        pad[0:HALO, :] = jnp.zeros((HALO, CONV_CB), F32)
        wv = w_ref[...]
        bv = b_ref[...]

        def fill(i, c):
            r0 = pl.multiple_of(i * rc, rc)
            pad[pl.ds(r0 + HALO, rc), :] = x_ref[pl.ds(r0, rc), :]
            return c

        lax.fori_loop(0, n, fill, 0)

        def step(i, c):
            r0 = pl.multiple_of(i * rc, rc)
            ext = pad[pl.ds(r0, rc + HALO), :]
            acc = bv + ext * wv[3:4]
            for j in (1, 2, 3):
                acc = acc + pltpu.roll(ext, j, 0) * wv[3 - j:4 - j]
            acc = acc[HALO:]
            o_ref[pl.ds(r0, rc), :] = acc * _sigmoid(acc)
            return c

        lax.fori_loop(0, n, step, 0)

    return pl.pallas_call(
        body, name=name, grid=(XBC // CONV_CB,),
        in_specs=[pl.BlockSpec((L, CONV_CB), lambda j: (0, j + C_XBC // CONV_CB)),
                  pl.BlockSpec((4, CONV_CB), lambda j: (0, j)), pl.BlockSpec((1, CONV_CB), lambda j: (0, j))],
        out_specs=pl.BlockSpec((L, CONV_CB), lambda j: (0, j)),
        out_shape=jax.ShapeDtypeStruct((L, XBC), F32),
        scratch_shapes=[pltpu.VMEM((L + HALO, CONV_CB), F32)],
        compiler_params=_params(("parallel",), VMEM_BIG))(proj, w, b)


def _conv_bwd(proj, dy, w, b, dproj, *, name):
    L = proj.shape[0]
    rc = _time_chunk(L)
    n = L // rc

    def body(x_ref, dy_ref, w_ref, b_ref, dp_in, dx_ref, dw_ref, db_ref, pad, dpad):
        del dp_in
        pad[0:HALO, :] = jnp.zeros((HALO, CONV_CB), F32)
        dpad[L:L + HALO, :] = jnp.zeros((HALO, CONV_CB), F32)
        wv = w_ref[...]
        bv = b_ref[...]

        def fill(i, c):
            r0 = pl.multiple_of(i * rc, rc)
            pad[pl.ds(r0 + HALO, rc), :] = x_ref[pl.ds(r0, rc), :]
            return c

        lax.fori_loop(0, n, fill, 0)

        def p1(i, carry):
            r0 = pl.multiple_of(i * rc, rc)
            ext = pad[pl.ds(r0, rc + HALO), :]
            xk = [ext[HALO:]] + [pltpu.roll(ext, j, 0)[HALO:] for j in (1, 2, 3)]
            pre = bv
            for j in range(4):
                pre = pre + xk[j] * wv[3 - j:4 - j]
            dpre = dy_ref[pl.ds(r0, rc), :] * _dsilu(pre)
            dpad[pl.ds(r0, rc), :] = dpre
            db, d0, d1, d2, d3 = carry
            return (db + _colsum(dpre), d0 + _colsum(dpre * xk[3]), d1 + _colsum(dpre * xk[2]),
                    d2 + _colsum(dpre * xk[1]), d3 + _colsum(dpre * xk[0]))

        z = jnp.zeros((1, CONV_CB), F32)
        db, d0, d1, d2, d3 = lax.fori_loop(0, n, p1, (z, z, z, z, z))
        db_ref[...] = db
        dw_ref[...] = jnp.concatenate([d0, d1, d2, d3], axis=0)

        def p2(i, c):
            r0 = pl.multiple_of(i * rc, rc)
            ext = dpad[pl.ds(r0, rc + HALO), :]
            acc = ext * wv[3:4]
            for j in (1, 2, 3):
                acc = acc + pltpu.roll(ext, rc + HALO - j, 0) * wv[3 - j:4 - j]
            dx_ref[pl.ds(r0, rc), :] = acc[:rc].astype(dx_ref.dtype)
            return c

        lax.fori_loop(0, n, p2, 0)

    nb = XBC // CONV_CB
    return pl.pallas_call(
        body, name=name, grid=(nb,),
        in_specs=[pl.BlockSpec((L, CONV_CB), lambda j: (0, j + C_XBC // CONV_CB)),
                  pl.BlockSpec((L, CONV_CB), lambda j: (0, j)),
                  pl.BlockSpec((4, CONV_CB), lambda j: (0, j)), pl.BlockSpec((1, CONV_CB), lambda j: (0, j)),
                  pl.BlockSpec(memory_space=pl.ANY)],
        out_specs=[pl.BlockSpec((L, CONV_CB), lambda j: (0, j + C_XBC // CONV_CB)),
                   pl.BlockSpec((4, CONV_CB), lambda j: (0, j)), pl.BlockSpec((1, CONV_CB), lambda j: (0, j))],
        out_shape=[jax.ShapeDtypeStruct((L, NPROJ), BF16), jax.ShapeDtypeStruct((4, XBC), F32),
                   jax.ShapeDtypeStruct((1, XBC), F32)],
        scratch_shapes=[pltpu.VMEM((L + HALO, CONV_CB), F32), pltpu.VMEM((L + HALO, CONV_CB), F32)],
        input_output_aliases={4: 0},
        compiler_params=_params(("parallel",), VMEM_BIG))(proj, dy, w, b, dproj)


def _pool_fwd(proj, *, name):
    L = proj.shape[0]
    rc = _time_chunk(L)
    n = L // rc

    def body(x_ref, o_ref, pad):
        g = pl.program_id(0)
        pad[0:HALO, :] = jnp.zeros((HALO, PGW), F32)

        def fill(i, c):
            r0 = pl.multiple_of(i * rc, rc)
            pad[pl.ds(r0 + HALO, rc), :] = x_ref[pl.ds(r0, rc), :]
            return c

        lax.fori_loop(0, n, fill, 0)
        rows = lax.broadcasted_iota(jnp.int32, (rc, PGW), 0)

        for gi in range(4):
            win = 2 << gi

            @pl.when(g == gi)
            def _(gi=gi, win=win):
                def step(i, c):
                    r0 = pl.multiple_of(i * rc, rc)
                    ext = pad[pl.ds(r0, rc + HALO), :]
                    s = ext
                    sh = 1
                    while sh < win:
                        s = s + pltpu.roll(s, sh, 0)
                        sh *= 2
                    cnt = jnp.minimum(rows + (r0 + 1), win).astype(F32)
                    o_ref[pl.ds(r0, rc), :] = (s[HALO:] / cnt - ext[HALO:]).astype(o_ref.dtype)
                    return c

                lax.fori_loop(0, n, step, 0)

    return pl.pallas_call(
        body, name=name, grid=(4,),
        in_specs=[pl.BlockSpec((L, PGW), lambda j: (0, j + C_POOL // PGW))],
        out_specs=pl.BlockSpec((L, PGW), lambda j: (0, j)),
        out_shape=jax.ShapeDtypeStruct((L, POOL_W), BF16),
        scratch_shapes=[pltpu.VMEM((L + HALO, PGW), F32)],
        compiler_params=_params(("parallel",), VMEM_BIG))(proj)


def _pool_bwd(dpooled, dproj, *, name):
    L = dpooled.shape[0]
    rc = _time_chunk(L)
    n = L // rc

    def body(d_ref, dp_in, o_ref, pad):
        del dp_in
        g = pl.program_id(0)
        pad[L:L + HALO, :] = jnp.zeros((HALO, PGW), F32)
        rows = lax.broadcasted_iota(jnp.int32, (rc, PGW), 0)

        for gi in range(4):
            win = 2 << gi

            @pl.when(g == gi)
            def _(gi=gi, win=win):
                def fill(i, c):
                    r0 = pl.multiple_of(i * rc, rc)
                    cnt = jnp.minimum(rows + (r0 + 1), win).astype(F32)
                    pad[pl.ds(r0, rc), :] = d_ref[pl.ds(r0, rc), :] / cnt
                    return c

                lax.fori_loop(0, n, fill, 0)

                def step(i, c):
                    r0 = pl.multiple_of(i * rc, rc)
                    s = pad[pl.ds(r0, rc + HALO), :]
                    sh = 1
                    while sh < win:
                        s = s + pltpu.roll(s, rc + HALO - sh, 0)
                        sh *= 2
                    o_ref[pl.ds(r0, rc), :] = (s[:rc] - d_ref[pl.ds(r0, rc), :]).astype(o_ref.dtype)
                    return c

                lax.fori_loop(0, n, step, 0)

    return pl.pallas_call(
        body, name=name, grid=(4,),
        in_specs=[pl.BlockSpec((L, PGW), lambda j: (0, j)), pl.BlockSpec(memory_space=pl.ANY)],
        out_specs=pl.BlockSpec((L, PGW), lambda j: (0, j + C_POOL // PGW)),
        out_shape=jax.ShapeDtypeStruct((L, NPROJ), BF16),
        scratch_shapes=[pltpu.VMEM((L + HALO, PGW), F32)],
        input_output_aliases={1: 0},
        compiler_params=_params(("parallel",), VMEM_BIG))(dpooled, dproj)


_SPLIT_DT = jnp.bfloat16


def _ssd_consts():
    tri = np.tril(np.ones((Q, Q), np.float32))
    exp = np.zeros((128, DI), np.float32)
    for h in range(NH):
        exp[h, h * HP:(h + 1) * HP] = 1.0
    exp2 = np.concatenate([exp, exp], axis=0)
    return (jnp.asarray(tri, dtype=_SPLIT_DT), jnp.asarray(tri.T.copy(), dtype=_SPLIT_DT),
            jnp.asarray(exp2, dtype=_SPLIT_DT))


def _split(v, n):
    parts, r = [], v
    for _ in range(n):
        p = r.astype(_SPLIT_DT)
        parts.append(p)
        r = r - p.astype(F32)
    return parts


def _bdot(a, b, dims):
    return lax.dot_general(a, b, (dims, ((), ())), preferred_element_type=F32)


def _tri_sum(t_ref, v):
    r = _bdot(t_ref[...], jnp.concatenate(_split(v, 3), axis=1), NN)
    return r[:, :128] + r[:, 128:256] + r[:, 256:]


def _expand(v, e2_ref):
    return _bdot(jnp.concatenate(_split(v, 2), axis=1), e2_ref[...], NN)


def _reduce_heads(vals, eg):
    parts = []
    for v in vals:
        parts += _split(v, 2)
    r = _bdot(jnp.concatenate(parts, axis=0), eg, NT)
    return [r[2 * i * Q:(2 * i + 1) * Q] + r[(2 * i + 1) * Q:(2 * i + 2) * Q] for i in range(len(vals))]


def _ssd_common(xbc_ref, dtw_ref, dtb_ref, arow_ref, t_ref, e_ref):
    pre = dtw_ref[:, :128] + dtb_ref[...]
    dt = _softplus(pre)
    acs = _tri_sum(t_ref, dt * arow_ref[...])
    acs_x = _expand(acs, e_ref)
    dt_x = _expand(dt, e_ref)
    xs = xbc_ref[:, 0:DI]
    return pre, dt, acs, acs.T, acs_x, dt_x, xs


def _ssd_fwd(xbc, proj, dtb, arow, dsk_x, *, name):
    L = xbc.shape[0]
    nc = L // Q
    tri, _, expand = _ssd_consts()

    def body(xbc_ref, dtw_ref, dtb_ref, arow_ref, dsk_ref, t_ref, e_ref, y_ref, hs_ref, h_scr):
        @pl.when(pl.program_id(0) == 0)
        def _():
            h_scr[...] = jnp.zeros_like(h_scr)

        _, dt, acs, acs_t, acs_x, dt_x, xs = _ssd_common(xbc_ref, dtw_ref, dtb_ref, arow_ref, t_ref, e_ref)
        xdt = xs * dt_x
        eacs = jnp.exp(acs_x)
        acs_last = acs_x[Q - 1:Q, :]
        dec = jnp.exp(acs_last - acs_x)
        hs_ref[0] = h_scr[...]
        causal = lax.broadcasted_iota(jnp.int32, (Q, Q), 0) >= lax.broadcasted_iota(jnp.int32, (Q, Q), 1)
        first = lax.broadcasted_iota(jnp.int32, (Q, 128), 1) < HP
        for g in range(NG):
            bg = xbc_ref[:, DI + g * NS:DI + (g + 1) * NS]
            cg = xbc_ref[:, DI + NG * NS + g * NS:DI + NG * NS + (g + 1) * NS]
            s = _dot(cg, bg, NT)
            sl = slice(g * GW, (g + 1) * GW)
            hg = h_scr[:, sl]
            yoff = _dot(cg, hg, NN) * eacs[:, sl]
            st = _dot(bg, xdt[:, sl] * dec[:, sl], TN)
            h_scr[:, sl] = hg * eacs[Q - 1:Q, sl] + st
            for j in range(4):
                lo = g * GW + j * 128
                xb = xdt[:, lo:lo + 128]
                yp = yoff[:, j * 128:(j + 1) * 128] + dsk_ref[:, lo:lo + 128] * xs[:, lo:lo + 128]
                for e in range(2):
                    h = g * 8 + j * 2 + e
                    lm = jnp.exp(jnp.where(causal, acs[:, h:h + 1] - acs_t[h:h + 1, :], NEG))
                    xm = jnp.where(first if e == 0 else jnp.logical_not(first), xb, 0.0)
                    yp = yp + _dot(s * lm, xm, NN)
                y_ref[:, lo:lo + 128] = yp

    return pl.pallas_call(
        body, name=name, grid=(nc,),
        in_specs=[pl.BlockSpec((Q, XBC), lambda c: (c, 0)),
                  pl.BlockSpec((Q, DT_PAD), lambda c: (c, C_DT // DT_PAD)),
                  pl.BlockSpec((1, 128), lambda c: (0, 0)), pl.BlockSpec((1, 128), lambda c: (0, 0)),
                  pl.BlockSpec((1, DI), lambda c: (0, 0)),
                  pl.BlockSpec((Q, Q), lambda c: (0, 0)), pl.BlockSpec((256, DI), lambda c: (0, 0))],
        out_specs=[pl.BlockSpec((Q, DI), lambda c: (c, 0)), pl.BlockSpec((1, NS, DI), lambda c: (c, 0, 0))],
        out_shape=[jax.ShapeDtypeStruct((L, DI), F32), jax.ShapeDtypeStruct((nc, NS, DI), F32)],
        scratch_shapes=[pltpu.VMEM((NS, DI), F32)],
        compiler_params=_params(("arbitrary",), VMEM_BIG))(xbc, proj, dtb, arow, dsk_x, tri, expand)


def _ssd_bwd(dy, xbc, proj, hs, dtb, arow, dsk_x, dproj, *, name):
    L = xbc.shape[0]
    nc = L // Q
    tri, triu, expand = _ssd_consts()

    def body(dy_ref, xbc_ref, dtw_ref, hs_ref, dtb_ref, arow_ref, dsk_ref, t_ref, u_ref, e_ref, dp_in,
             dxbc_ref, ddtw_ref, da_ref, ddx_ref, ddtb_ref, dh_scr):
        del dp_in
        i = pl.program_id(0)

        @pl.when(i == 0)
        def _():
            dh_scr[...] = jnp.zeros_like(dh_scr)

        pre, dt, acs, acs_t, acs_x, dt_x, xs = _ssd_common(xbc_ref, dtw_ref, dtb_ref, arow_ref, t_ref, e_ref)
        dyv = dy_ref[...]
        xdt = xs * dt_x
        eacs = jnp.exp(acs_x)
        acs_last = acs_x[Q - 1:Q, :]
        dec = jnp.exp(acs_last - acs_x)
        gy = dyv * eacs
        causal = lax.broadcasted_iota(jnp.int32, (Q, Q), 0) >= lax.broadcasted_iota(jnp.int32, (Q, Q), 1)
        first = lax.broadcasted_iota(jnp.int32, (Q, 128), 1) < HP
        lane_h = lax.broadcasted_iota(jnp.int32, (Q, 128), 1)
        sub_h = lax.broadcasted_iota(jnp.int32, (128, Q), 0)
        last_row = lax.broadcasted_iota(jnp.int32, (Q, GW), 0) == Q - 1
        dacs = jnp.zeros((Q, 128), F32)
        dacs_t = jnp.zeros((128, Q), F32)
        ddt = jnp.zeros((Q, 128), F32)
        for g in range(NG):
            bg = xbc_ref[:, DI + g * NS:DI + (g + 1) * NS]
            cg = xbc_ref[:, DI + NG * NS + g * NS:DI + NG * NS + (g + 1) * NS]
            s = _dot(cg, bg, NT)
            sl = slice(g * GW, (g + 1) * GW)
            hg = hs_ref[0, :, sl]
            dhn = dh_scr[:, sl]
            eal = eacs[Q - 1:Q, sl]
            gg = gy[:, sl]
            dax = gg * _dot(cg, hg, NN)
            dcg = _dot(gg, hg, NT)
            dh_scr[:, sl] = _dot(cg, gg, TN) + dhn * eal
            dal = eal * _colsum(dhn * hg)
            xdd = xdt[:, sl] * dec[:, sl]
            dbg = _dot(xdd, dhn, NT)
            wv = _dot(bg, dhn, NN)
            dd = wv * xdd
            dax = dax - dd
            dal = dal + _colsum(dd)
            dax = dax + jnp.where(last_row, dal, 0.0)
            dxdt_g = wv * dec[:, sl]
            ds = jnp.zeros((Q, Q), F32)
            dxdt_blocks = []
            for j in range(4):
                lo = g * GW + j * 128
                xb = xdt[:, lo:lo + 128]
                dyb = dyv[:, lo:lo + 128]
                dxb = dxdt_g[:, j * 128:(j + 1) * 128]
                for e in range(2):
                    h = g * 8 + j * 2 + e
                    lm = jnp.exp(jnp.where(causal, acs[:, h:h + 1] - acs_t[h:h + 1, :], NEG))
                    m = s * lm
                    dym = jnp.where(first if e == 0 else jnp.logical_not(first), dyb, 0.0)
                    dm = _dot(dym, xb, NT)
                    r = dm * m
                    dacs = dacs + jnp.where(lane_h == h, jnp.sum(r, axis=1, keepdims=True), 0.0)
                    dacs_t = dacs_t + jnp.where(sub_h == h, _colsum(r), 0.0)
                    ds = ds + dm * lm
                    dxb = dxb + _dot(m, dym, TN)
                dxdt_blocks.append(dxb)
            dxdt = jnp.concatenate(dxdt_blocks, axis=1)
            dcg = dcg + _dot(ds, bg, NN)
            dbg = dbg + _dot(ds, cg, TN)
            dxbc_ref[:, DI + g * NS:DI + (g + 1) * NS] = dbg
            dxbc_ref[:, DI + NG * NS + g * NS:DI + NG * NS + (g + 1) * NS] = dcg
            dxbc_ref[:, sl] = dsk_ref[:, sl] * dyv[:, sl] + dxdt * dt_x[:, sl]
            ddt_g, dacs_g = _reduce_heads([dxdt * xs[:, sl], dax], e_ref[0:128, sl])
            ddt = ddt + ddt_g
            dacs = dacs + dacs_g
        dacs = dacs - dacs_t.T
        ddta = _tri_sum(u_ref, dacs)
        ddt = ddt + ddta * arow_ref[...]
        ddtw = jnp.where(lane_h < NH, ddt * _sigmoid(pre), 0.0)
        ddtw_ref[...] = jnp.concatenate([ddtw, jnp.zeros((Q, DT_PAD - 128), F32)], axis=1).astype(ddtw_ref.dtype)
        _acc_out(da_ref, _colsum(ddta * dt), i)
        _acc_out(ddx_ref, _colsum(dyv * xs), i)
        _acc_out(ddtb_ref, _colsum(ddtw), i)

    rev = lambda c: (nc - 1 - c, 0)
    const = lambda c: (0, 0)
    return pl.pallas_call(
        body, name=name, grid=(nc,),
        in_specs=[pl.BlockSpec((Q, DI), rev), pl.BlockSpec((Q, XBC), rev),
                  pl.BlockSpec((Q, DT_PAD), lambda c: (nc - 1 - c, C_DT // DT_PAD)),
                  pl.BlockSpec((1, NS, DI), lambda c: (nc - 1 - c, 0, 0)),
                  pl.BlockSpec((1, 128), const), pl.BlockSpec((1, 128), const), pl.BlockSpec((1, DI), const),
                  pl.BlockSpec((Q, Q), const), pl.BlockSpec((Q, Q), const), pl.BlockSpec((256, DI), const),
                  pl.BlockSpec(memory_space=pl.ANY)],
        out_specs=[pl.BlockSpec((Q, XBC), rev),
                   pl.BlockSpec((Q, DT_PAD), lambda c: (nc - 1 - c, C_DT // DT_PAD)),
                   pl.BlockSpec((1, 128), const), pl.BlockSpec((1, DI), const), pl.BlockSpec((1, 128), const)],
        out_shape=[jax.ShapeDtypeStruct((L, XBC), F32), jax.ShapeDtypeStruct((L, NPROJ), BF16),
                   jax.ShapeDtypeStruct((1, 128), F32), jax.ShapeDtypeStruct((1, DI), F32),
                   jax.ShapeDtypeStruct((1, 128), F32)],
        scratch_shapes=[pltpu.VMEM((NS, DI), F32)],
        input_output_aliases={10: 1},
        compiler_params=_params(("arbitrary",), VMEM_BIG))(dy, xbc, proj, hs, dtb, arow, dsk_x, tri, triu,
                                                          expand, dproj)


def _adamw(w, g, m, v, *, name, tr=None):
    R, C = w.shape
    if tr is None:
        tr = _pick(R, (256, 128, 64, 32, 16, 8))
    assert R % tr == 0
    c1 = 1.0 - ADAM_B1 ** ADAM_STEP
    c2 = 1.0 - ADAM_B2 ** ADAM_STEP

    def body(w_ref, g_ref, m_ref, v_ref, d_ref, mo_ref, vo_ref):
        gv = g_ref[...]
        mn = ADAM_B1 * m_ref[...] + (1.0 - ADAM_B1) * gv
        vn = ADAM_B2 * v_ref[...] + (1.0 - ADAM_B2) * (gv * gv)
        mo_ref[...] = mn
        vo_ref[...] = vn
        d_ref[...] = -ADAM_LR * ((mn / c1) / (jnp.sqrt(vn / c2) + ADAM_EPS) + ADAM_WD * w_ref[...])

    spec = pl.BlockSpec((tr, C), lambda i: (i, 0))
    return pl.pallas_call(body, name=name, grid=(pl.cdiv(R, tr),), in_specs=[spec] * 4, out_specs=[spec] * 3,
                          out_shape=[jax.ShapeDtypeStruct((R, C), F32)] * 3,
                          compiler_params=_params(("parallel",)))(w, g, m, v)


def _slab_sum(recv, *, tile, name):
    rows = recv.shape[1]
    assert rows % tile == 0 and tile % 16 == 0

    def body(r_ref, o_ref):
        acc = r_ref[0].astype(F32)
        for j in range(1, N_DEV):
            acc = acc + r_ref[j].astype(F32)
        o_ref[...] = acc

    return pl.pallas_call(body, name=name, grid=(rows // tile,),
                          in_specs=[pl.BlockSpec((N_DEV, tile, D), lambda i: (0, i, 0))],
                          out_specs=pl.BlockSpec((tile, D), lambda i: (i, 0)),
                          out_shape=jax.ShapeDtypeStruct((rows, D), F32),
                          compiler_params=_params(("parallel",)))(recv)


MESH = pl.DeviceIdType.MESH


def _coords():
    return lax.axis_index("x"), lax.axis_index("y"), lax.axis_index("c")


def _peer(k):
    x, y, c = _coords()
    px = 1 - x if k & 4 else x
    py = 1 - y if k & 2 else y
    pc = 1 - c if k & 1 else c
    return (px, py, pc), 4 * px + 2 * py + pc


def _rcopy(src, dst, ssem, rsem, dev):
    return pltpu.make_async_remote_copy(src_ref=src, dst_ref=dst, send_sem=ssem, recv_sem=rsem,
                                        device_id=dev, device_id_type=MESH)


def _exchange_all(src_of, dst_slot, send_sems, recv_sems):
    x, y, c = _coords()
    me = 4 * x + 2 * y + c
    sent = []
    for k in range(1, N_DEV):
        dev, pidx = _peer(k)
        cp = _rcopy(src_of(pidx), dst_slot(me), send_sems.at[k - 1], recv_sems.at[k - 1], dev)
        cp.start()
        sent.append(cp)
    for k in range(1, N_DEV):
        dev, pidx = _peer(k)
        _rcopy(src_of(pidx), dst_slot(pidx), send_sems.at[k - 1], recv_sems.at[k - 1], dev).wait_recv()
    for cp in sent:
        cp.wait_send()


def _rows_of_slots(buf, nslots):
    rows = lax.broadcasted_iota(jnp.int32, (8, buf.shape[-1]), 0)
    out = jnp.zeros((8, buf.shape[-1]), F32)
    for j in range(nslots):
        out = out + jnp.where(rows == j, buf[j], 0.0)
    return out


def _ada_fwd(c, w_ada, b_r, *, name):
    wloc = w_ada.shape[1]

    def body(c_ref, w_ref, b_ref, mod_ref, call_ref, csrc, cbuf, psrc, pbuf, s1, r1, s2, r2):
        x, y, cc = _coords()
        me = 4 * x + 2 * y + cc
        csrc[...] = jnp.broadcast_to(c_ref[...], (8, D))
        cbuf[me] = csrc[...]
        _exchange_all(lambda p: csrc, lambda s: cbuf.at[s], s1, r1)
        call = _rows_of_slots(cbuf, N_DEV)
        call_ref[...] = call
        prod = _dot_hi(_silu(call), w_ref[...])
        for b in range(N_DEV):
            psrc[b] = jnp.broadcast_to(prod[b:b + 1, :], (8, wloc))
        pbuf[me] = psrc[me]
        _exchange_all(lambda p: psrc.at[p], lambda s: pbuf.at[s], s2, r2)
        mod_ref[...] = _rows_of_slots(pbuf, N_DEV) + b_ref[...]

    vm = pl.BlockSpec(memory_space=pltpu.VMEM)
    return pl.pallas_call(
        body, name=name, in_specs=[vm, vm, vm], out_specs=[vm, vm],
        out_shape=[jax.ShapeDtypeStruct((N_DEV, wloc), F32), jax.ShapeDtypeStruct((N_DEV, D), F32)],
        scratch_shapes=[pltpu.VMEM((8, D), F32), pltpu.VMEM((N_DEV, 8, D), F32),
                        pltpu.VMEM((N_DEV, 8, wloc), F32), pltpu.VMEM((N_DEV, 8, wloc), F32),
                        pltpu.SemaphoreType.DMA((N_DEV - 1,)), pltpu.SemaphoreType.DMA((N_DEV - 1,)),
                        pltpu.SemaphoreType.DMA((N_DEV - 1,)), pltpu.SemaphoreType.DMA((N_DEV - 1,))],
        compiler_params=pltpu.CompilerParams(vmem_limit_bytes=VMEM_BIG))(c, w_ada, b_r)


def _gather_slabs(slab, *, name):
    def body(x_ref, out_ref, send_sems, recv_sems, local_sem):
        x, y, c = _coords()
        me, sibling = (x, y, c), (x, y, 1 - c)
        chips = [(1 - x, y), (x, 1 - y), (1 - x, 1 - y)]

        def slot(px, py, pc):
            return out_ref.at[4 * px + 2 * py + pc]

        def copy(k, block, to, src=None):
            return _rcopy(slot(*block) if src is None else src, slot(*block), send_sems.at[k], recv_sems.at[k], to)

        mine = pltpu.make_async_copy(x_ref, slot(*me), local_sem)
        mine.start()
        first = [copy(0, me, sibling, src=x_ref)]
        first += [copy(1 + j, me, (*chip, c), src=x_ref) for j, chip in enumerate(chips)]
        for cp in first:
            cp.start()
        passed = [copy(4 + j, (*chip, c), sibling) for j, chip in enumerate(chips)]
        for j, chip in enumerate(chips):
            copy(1 + j, (*chip, c), me).wait_recv()
            passed[j].start()
        copy(0, sibling, me).wait_recv()
        for j, chip in enumerate(chips):
            copy(4 + j, (*chip, 1 - c), me).wait_recv()
        for cp in first + passed:
            cp.wait_send()
        mine.wait()

    anyspec = pl.BlockSpec(memory_space=pl.ANY)
    return pl.pallas_call(
        body, name=name, in_specs=[anyspec], out_specs=anyspec,
        out_shape=jax.ShapeDtypeStruct((N_DEV,) + slab.shape, slab.dtype),
        scratch_shapes=[pltpu.SemaphoreType.DMA((7,)), pltpu.SemaphoreType.DMA((7,)), pltpu.SemaphoreType.DMA],
    )(slab)


_HBM =pl.BlockSpec(memory_space=pltpu.HBM)
_SEM = pl.BlockSpec(memory_space=pltpu.SEMAPHORE)
_EFFECT = pltpu.SideEffectType.DATAFLOW_SIDE_EFFECTING


def _xchg_src(src_ref, pidx, per_peer):
    return src_ref.at[pidx] if per_peer else src_ref


def _xchg_start(src, *, per_peer, name):
    rows = src.shape[-2]
    land_shape = (N_DEV, rows, D)

    def body(src_ref, land_ref, send_sems, recv_sems, src_thru, land_thru, token):
        del src_thru, land_thru
        x, y, c = _coords()
        me = 4 * x + 2 * y + c
        for k in range(1, N_DEV):
            dev, pidx = _peer(k)
            _rcopy(_xchg_src(src_ref, pidx, per_peer), land_ref.at[me], send_sems.at[k - 1],
                   recv_sems.at[k - 1], dev).start()
        token[...] = jnp.zeros_like(token)

    return pl.pallas_call(
        body, name=name,
        out_shape=(pltpu.SemaphoreType.DMA((N_DEV - 1,)), pltpu.SemaphoreType.DMA((N_DEV - 1,)),
                   pltpu.HBM(src.shape, src.dtype), pltpu.HBM(land_shape, src.dtype),
                   jax.ShapeDtypeStruct((8, 128), F32)),
        in_specs=(_HBM, _HBM),
        out_specs=(_SEM, _SEM, _HBM, _HBM, pl.BlockSpec(memory_space=pltpu.VMEM)),
        input_output_aliases={0: 2, 1: 3},
        compiler_params=pltpu.CompilerParams(has_side_effects=_EFFECT),
    )(pltpu.with_memory_space_constraint(src, pltpu.HBM),
      pltpu.with_memory_space_constraint(lax.empty(land_shape, src.dtype), pltpu.HBM))


def _xchg_wait(started, after, *, per_peer, name):
    send_sems, recv_sems, src_thru, land_thru, _ = started

    def body(src_ref, land_ref, send_sems, recv_sems, after_ref, src_dead, got_ref):
        del after_ref, src_dead, got_ref
        for k in range(1, N_DEV):
            dev, pidx = _peer(k)
            cp = _rcopy(_xchg_src(src_ref, pidx, per_peer), land_ref.at[pidx], send_sems.at[k - 1],
                        recv_sems.at[k - 1], dev)
            cp.wait_send()
            cp.wait_recv()

    return pl.pallas_call(
        body, name=name,
        out_shape=(pltpu.HBM(src_thru.shape, src_thru.dtype), pltpu.HBM(land_thru.shape, land_thru.dtype)),
        in_specs=(_HBM, _HBM, _SEM, _SEM, pl.BlockSpec(memory_space=pl.ANY)),
        out_specs=(_HBM, _HBM),
        input_output_aliases={0: 0, 1: 1},
        compiler_params=pltpu.CompilerParams(has_side_effects=_EFFECT),
    )(src_thru, land_thru, send_sems, recv_sems, after)


def _dep(token):
    return (token, (8, 128), lambda i, j, k: (0, 0))


def _small_allsum(sv, *, name):
    def body(sv_ref, all_ref, sum_ref, send_sems, recv_sems):
        x, y, c = _coords()
        me = 4 * x + 2 * y + c
        all_ref[me] = sv_ref[...]
        _exchange_all(lambda p: sv_ref, lambda s: all_ref.at[s], send_sems, recv_sems)
        acc = all_ref[0]
        for j in range(1, N_DEV):
            acc = acc + all_ref[j]
        sum_ref[...] = acc

    vm = pl.BlockSpec(memory_space=pltpu.VMEM)
    return pl.pallas_call(
        body, name=name, in_specs=[vm], out_specs=[vm, vm],
        out_shape=[jax.ShapeDtypeStruct((N_DEV, SV_ROWS, 128), F32), jax.ShapeDtypeStruct((SV_ROWS, 128), F32)],
        scratch_shapes=[pltpu.SemaphoreType.DMA((7,)), pltpu.SemaphoreType.DMA((7,))],
    )(sv)


def _ada_bwd(call, dmod_loc, *, name):
    wloc = dmod_loc.shape[1]

    def body(c_ref, d_ref, o_ref):
        o_ref[...] = _dot_hi(_silu(c_ref[...]), d_ref[...], TN)

    vm = pl.BlockSpec(memory_space=pltpu.VMEM)
    return pl.pallas_call(body, name=name, in_specs=[vm, vm], out_specs=vm,
                          out_shape=jax.ShapeDtypeStruct((D, wloc), F32),
                          compiler_params=pltpu.CompilerParams(vmem_limit_bytes=VMEM_BIG))(call, dmod_loc)


def _pad_rows(a, rows):
    return jnp.pad(a, ((0, rows - a.shape[0]), (0, 0)))


def _reorder_in_rows(gs):
    wt = gs[:, :IN_ROWS].reshape(N_IN, D)
    z, xbc, dt, pool, gates = wt[0:2048], wt[2048:5120], wt[5120:5152], wt[5152:6176], wt[6176:8224]
    return jnp.concatenate([xbc, pool, z, gates, dt, jnp.zeros((DT_PAD - 32, D), wt.dtype)], axis=0)


def _restore_in_shards(d):
    wt = jnp.concatenate([d[C_Z:C_Z + 2048], d[C_XBC:C_XBC + XBC], d[C_DT:C_DT + 32],
                          d[C_POOL:C_POOL + 1024], d[C_GATE:C_GATE + 2048]], axis=0)
    return jnp.pad(wt.reshape(N_DEV, IN_ROWS, D), ((0, 0), (0, IN_ROWS_P - IN_ROWS), (0, 0)))


def _pack_sv(parts):
    flat = []
    for n, size in SV_PARTS:
        v = parts[n].reshape(-1).astype(F32)
        flat.append(jnp.pad(v, (0, size - v.shape[0])))
    v = jnp.concatenate(flat)
    return jnp.pad(v, (0, SV_ROWS * 128 - v.shape[0])).reshape(SV_ROWS, 128)


def _sv_get(flat, n, size):
    return flat[SV_OFF[n]:SV_OFF[n] + size]


def kernel(x, c, w_ada, b_ada, norm_mix_w, w_in, conv_w, conv_b, dt_bias, a_log, d_skip, ssd_norm_w, w_branch_ssd, pool_w, pool_scale, w_branch_pool, w_out, norm_mlp_w, w_up, w_down, norm_final_w, loss_target, m_w_ada, m_b_ada, m_norm_mix_w, m_w_in, m_conv_w, m_conv_b, m_dt_bias, m_a_log, m_d_skip, m_ssd_norm_w, m_w_branch_ssd, m_pool_w, m_pool_scale, m_w_branch_pool, m_w_out, m_norm_mlp_w, m_w_up, m_w_down, m_norm_final_w, v_w_ada, v_b_ada, v_norm_mix_w, v_w_in, v_conv_w, v_conv_b, v_dt_bias, v_a_log, v_d_skip, v_ssd_norm_w, v_w_branch_ssd, v_pool_w, v_pool_scale, v_w_branch_pool, v_w_out, v_norm_mlp_w, v_w_up, v_w_down, v_norm_final_w):
    xs_ = x[0]
    tgt = loss_target[0]
    L = xs_.shape[0]
    me = 4 * lax.axis_index("x") + 2 * lax.axis_index("y") + lax.axis_index("c")
    wloc = w_ada.shape[2]

    mod_p, c_all = _ada_fwd(c, w_ada[0], b_ada.reshape(N_DEV, wloc), name="ada_fwd")
    mod = mod_p.reshape(6, D)
    shift_m, scale_m, gate_m, shift_f, scale_f, gate_f = [mod[i:i + 1] for i in range(6)]

    conv_bits = lax.bitcast_convert_type(conv_w[0], SLAB_DT).reshape(3, D)
    slab_in = jnp.concatenate([_pad_rows(w_in[0].T.astype(SLAB_DT), IN_ROWS_P),
                               _pad_rows(conv_bits, CONV_ROWS)], axis=0)
    slab_rest = jnp.concatenate([
        w_branch_ssd[0].astype(SLAB_DT),
        pool_w[0].reshape(32, D).astype(SLAB_DT),
        w_branch_pool[0].astype(SLAB_DT),
        w_out[0].astype(SLAB_DT),
        w_up[0].T.astype(SLAB_DT),
        w_down[0].astype(SLAB_DT)], axis=0)
    gs_in = _gather_slabs(slab_in, name="gather_w_in")
    slab_rest, gs_in = lax.optimization_barrier((slab_rest, gs_in))
    rest_started = _xchg_start(slab_rest, per_peer=False, name="gather_rest_start")
    gather_token = rest_started[4]

    w_in_t = _reorder_in_rows(gs_in)
    conv_full = lax.bitcast_convert_type(
        gs_in[:, IN_ROWS_P:IN_ROWS_P + 3].reshape(N_DEV, 4, XBC // N_DEV, 2), F32)
    conv_full = conv_full.transpose(1, 0, 2).reshape(4, XBC)

    dtb = jnp.pad(dt_bias, ((0, 0), (0, 128 - NH)))
    arow = jnp.pad(-jnp.exp(a_log), ((0, 0), (0, 128 - NH)))
    dsk_x = jnp.repeat(d_skip, HP, axis=1)

    tm = _pick(L, (1024, 512, 256, 128))
    tm2 = _pick(L, (2048, 1024, 512, 256, 128))
    tkl = _pick(L, (4096, 2048, 1024, 512, 256, 128))
    tkl2 = _pick(L, (2048, 1024, 512, 256, 128))

    tmh = _pick(L, (512, 256, 128))
    zcol = C_Z // DI
    gcol = C_GATE // (2 * D)

    def whole_rows(w):
        return lambda t: ((L, w), BF16, (t, w), lambda i, j, k: (i, 0))

    def norm1_pro(x_ref, ex, outs, j):
        @pl.when(j == 0)
        def _():
            xv = x_ref[...]
            r = lax.rsqrt(jnp.mean(xv * xv, axis=-1, keepdims=True) + EPS)
            outs[1][...] = (xv * r * ex[0][...] * (1.0 + ex[1][...]) + ex[2][...]).astype(outs[1].dtype)

        return outs[1][...]

    proj, h1 = _mm(xs_, w_in_t, "nt", name="in_proj", tm=tm2, tn=768, tk=D,
                   extras=[(norm_mix_w, *_vecs()), (scale_m, *_vecs()), (shift_m, *_vecs()), _dep(gather_token)],
                   outs=[F32, whole_rows(D)(tm2)], prologue=norm1_pro)
    xbc = _conv_fwd(proj, conv_full, conv_b, name="conv_fwd")
    y_ssm, hs = _ssd_fwd(xbc, proj, dtb, arow, dsk_x, name="ssd_fwd")

    slab_rest, gs = _xchg_wait(rest_started, y_ssm, per_peer=False, name="gather_rest_wait")
    gs = lax.dynamic_update_slice(gs, slab_rest[None], (me, 0, 0))

    def part(n, rows):
        return gs[:, REST_OFF[n]:REST_OFF[n] + rows]

    w_bssd = part("bssd", 256).reshape(DI, D)
    w_pool = part("pool", 32).reshape(N_DEV, 4, 32, PGW).transpose(1, 0, 2, 3).reshape(POOL_W, PGW)
    w_bpool = part("bpool", 128).reshape(POOL_W, D)
    w_o = part("out", 128).reshape(D, D)
    w_up_t = part("up", 512).reshape(DFF, D)
    w_dn = part("down", 512).reshape(DFF, D)

    def gnorm_pro(y_ref, ex, outs, j):
        z_ref, w_ref = ex
        yg = y_ref[...] * _silu(z_ref[...])
        segs = []
        for k in range(NG):
            sl = slice(k * GW, (k + 1) * GW)
            seg = yg[:, sl]
            r = lax.rsqrt(jnp.mean(seg * seg, axis=-1, keepdims=True) + EPS)
            segs.append((seg * r * w_ref[:, sl]).astype(BF16))
        yn_v = jnp.concatenate(segs, axis=1)
        outs[1][...] = yn_v
        return yn_v

    y_ssd, yn = _mm(y_ssm, w_bssd, "nn", name="branch_ssd", tm=tmh, tn=D, tk=DI,
                    extras=[(proj, *_rows(tmh, DI, zcol)), (ssd_norm_w, *_vecs(DI))],
                    outs=[F32, whole_rows(DI)(tmh)], prologue=gnorm_pro)
    pooled = _pool_fwd(proj, name="pool_fwd")
    yp0, yp1 = _mm_pool(pooled, w_pool, pool_scale, name="pool_mix", tm=tm, transpose_w=False)
    y_pool = _mm(yp1, w_bpool, "nn", name="branch_pool", outs=[F32], tm=tm2, tn=D, tk=D)

    def merge_pro(a_ref, ex, outs, j):
        s = _sigmoid(ex[1][...])
        mv = (s[:, :D] * a_ref[...] + s[:, D:] * ex[0][...]).astype(BF16)
        outs[3][...] = mv
        return mv

    mix, x1, h2, m = _mm(y_ssd, w_o, "nn", name="out_proj", tm=tmh, tn=D, tk=D,
                         extras=[(y_pool, *_rows(tmh)), (proj, *_rows(tmh, 2 * D, gcol)),
                                 (xs_, *_rows(tmh)), (gate_m, *_vecs()), (norm_mlp_w, *_vecs()),
                                 (scale_f, *_vecs()), (shift_f, *_vecs())],
                         outs=[BF16, F32, BF16, whole_rows(D)(tmh)], prologue=merge_pro,
                         epilogue=lambda acc, ex, outs: _ep_resid_norm(acc, ex[2:], outs[:3]))

    def relu2(acc, ex, outs):
        r = jnp.maximum(acc, 0.0)
        outs[0][...] = acc.astype(BF16)
        outs[1][...] = (r * r).astype(BF16)

    up, act = _mm(h2, w_up_t, "nt", name="mlp_up", outs=[BF16, BF16], tm=tm2, tn=1024, tk=D, epilogue=relu2)

    dx2, ddown, loss_p, dnwf, dgate_f = _mm(
        act, w_dn, "nn", name="mlp_down", tm=tmh, tn=D, tk=DFF,
        extras=[(x1, *_rows(tmh)), (tgt, *_rows(tmh)), (gate_f, *_vecs()), (norm_final_w.reshape(1, D), *_vecs())],
        outs=[F32, BF16, _sum_out(128), _sum_out(), _sum_out()], epilogue=_ep_final)

    def drelu2(acc, ex, outs):
        outs[0][...] = (acc * (2.0 * jnp.maximum(ex[0][...].astype(F32), 0.0))).astype(BF16)

    def dep_last(ep):
        return lambda acc, ex, outs: ep(acc, ex[:-1], outs)

    dup = _mm(ddown, w_dn, "nt", name="mlp_down_dx", outs=[BF16], tm=tm2, tn=1024, tk=D,
              extras=[(up, (tm2, 1024), lambda i, j, k: (i, j))], epilogue=drelu2)
    g_dn = _mm(act, ddown, "tn", name="mlp_down_dw", outs=[SLAB_DT], tm=1024, tn=D, tk=tkl)
    g_up_t = _mm(dup, h2, "tn", name="mlp_up_dw", outs=[SLAB_DT], tm=1024, tn=D, tk=tkl)
    gslab_mlp = jnp.concatenate([g_up_t.reshape(N_DEV, 512, D), g_dn.reshape(N_DEV, 512, D)], axis=1)
    mlp_started = _xchg_start(gslab_mlp, per_peer=True, name="scatter_mlp_start")
    dx1, p2, q2, dmix, dgate_m = _mm(
        dup, w_up_t, "nn", name="mlp_up_dx", tm=tmh, tn=D, tk=DFF,
        extras=[(x1, *_rows(tmh)), (dx2, *_rows(tmh)), (norm_mlp_w, *_vecs()), (scale_f, *_vecs()),
                (mix, *_rows(tmh)), (gate_m, *_vecs()), _dep(mlp_started[4])],
        outs=[F32, _sum_out(), _sum_out(), BF16, _sum_out()], epilogue=dep_last(_ep_norm_bwd))
    gcol = C_GATE // (2 * D)
    dy_ssd, dy_pool, dproj = _mm(
        dmix, w_o, "nt", name="out_proj_dx", tm=tmh, tn=D, tk=D,
        extras=[(y_ssd, *_rows(tmh)), (y_pool, *_rows(tmh)), (proj, *_rows(tmh, 2 * D, gcol))],
        outs=[BF16, BF16, ((L, NPROJ), BF16, *_rows(tmh, 2 * D, gcol))], epilogue=_ep_merge_bwd)
    g_o = _mm(m, dmix, "tn", name="out_proj_dw", outs=[SLAB_DT], tm=D, tn=D, tk=tkl)
    zcol = C_Z // DI
    dy_ssm, dproj, d_snw = _mm(
        dy_ssd, w_bssd, "nt", name="branch_ssd_dx", tm=tmh, tn=DI, tk=D,
        extras=[(y_ssm, *_rows(tmh, DI)), (proj, *_rows(tmh, DI, zcol)), (ssd_norm_w, *_vecs(DI)),
                (dproj, None, None)],
        outs=[F32, ((L, NPROJ), BF16, *_rows(tmh, DI, zcol)), _sum_out(DI)],
        epilogue=_ep_gated_norm_bwd, aliases={3: 1})
    g_bssd = _mm(yn, dy_ssd, "tn", name="branch_ssd_dw", outs=[SLAB_DT], tm=1024, tn=D, tk=tkl)
    dxbc, dproj, d_a, d_dx, d_dtb = _ssd_bwd(dy_ssm, xbc, proj, hs, dtb, arow, dsk_x, dproj, name="ssd_bwd")
    dproj, d_cw, d_cb = _conv_bwd(proj, dxbc, conv_full, conv_b, dproj, name="conv_bwd")
    dyp0, d_ps = _mm(dy_pool, w_bpool, "nt", name="branch_pool_dx", tm=tm, tn=D, tk=D,
                     extras=[(yp0, *_rows(tm)), (pool_scale, *_vecs())],
                     outs=[BF16, _sum_out()], epilogue=_ep_pscale_bwd)
    g_bpool = _mm(yp1, dy_pool, "tn", name="branch_pool_dw", outs=[SLAB_DT], tm=D, tn=D, tk=tkl)
    dpooled = _mm_pool(dyp0, w_pool, None, name="pool_mix_dx", tm=tm, transpose_w=True)
    g_pool = _mm_pool_tn(pooled, dyp0, name="pool_mix_dw", tk=tkl)
    gslab_mix = jnp.concatenate([
        g_bssd.reshape(N_DEV, 256, D),
        g_pool.reshape(4, N_DEV, 32, PGW).transpose(1, 0, 2, 3).reshape(N_DEV, 32, D).astype(SLAB_DT),
        g_bpool.reshape(N_DEV, 128, D),
        g_o.reshape(N_DEV, 128, D)], axis=1)
    mix_started = _xchg_start(gslab_mix, per_peer=True, name="scatter_mix_start")
    dproj = _pool_bwd(dpooled, dproj, name="pool_bwd")
    g_in_t = _mm(dproj, h1, "tn", name="in_proj_dw", outs=[SLAB_DT], tm=1408, tn=D, tk=tkl2,
                 extras=[_dep(mix_started[4])])
    gslab_in = _restore_in_shards(g_in_t)
    in_started = _xchg_start(gslab_in, per_peer=True, name="scatter_in_start")
    grad_x, p1, q1 = _mm(
        dproj, w_in_t, "nn", name="in_proj_dx", tm=tmh, tn=D, tk=2816,
        extras=[(xs_, *_rows(tmh)), (dx1, *_rows(tmh)), (norm_mix_w, *_vecs()), (scale_m, *_vecs()),
                _dep(in_started[4])],
        outs=[F32, _sum_out(), _sum_out()], epilogue=dep_last(_ep_norm_bwd))

    def landed(started, after, tile, name):
        src, land = _xchg_wait(started, after, per_peer=True, name=name + "_wait")
        own = lax.dynamic_slice_in_dim(src, me, 1, axis=0)
        return _slab_sum(lax.dynamic_update_slice(land, own, (me, 0, 0)), tile=tile, name=name + "_sum")

    gsum_mlp = landed(mlp_started, grad_x, 256, "scatter_mlp")
    gsum_mix = landed(mix_started, grad_x, 272, "scatter_mix")
    gsum_in = landed(in_started, grad_x, 208, "scatter_in")

    dmod = jnp.concatenate([q1, p1 * norm_mix_w, dgate_m, q2, p2 * norm_mlp_w, dgate_f], axis=1)
    d_alog = d_a[:, :NH] * (-jnp.exp(a_log))
    sv = _pack_sv({
        "b_ada": dmod, "norm_mix_w": p1 * (1.0 + scale_m), "conv_b": d_cb, "dt_bias": d_dtb[:, :NH],
        "a_log": d_alog, "d_skip": d_dx.reshape(NH, HP).sum(axis=1), "ssd_norm_w": d_snw,
        "pool_scale": d_ps, "norm_mlp_w": p2 * (1.0 + scale_f), "norm_final_w": dnwf, "conv_w": d_cw,
        "loss": loss_p[:, :1]})
    sv_all, sv_sum = _small_allsum(sv, name="small_allsum")
    flat = sv_sum.reshape(-1)
    loss = flat[SV_OFF["loss"]]
    dmod_all = sv_all.reshape(N_DEV, SV_ROWS * 128)[:, :6 * D]
    g_w_ada = _ada_bwd(c_all, lax.dynamic_slice_in_dim(dmod_all, me * wloc, wloc, axis=1), name="ada_bwd")

    g_conv_w = lax.dynamic_slice_in_dim(_sv_get(flat, "conv_w", 4 * XBC).reshape(4, XBC),
                                        me * (XBC // N_DEV), XBC // N_DEV, axis=1)
    small = {
        "b_ada": (b_ada, m_b_ada, v_b_ada, _sv_get(flat, "b_ada", 6 * D)),
        "norm_mix_w": (norm_mix_w, m_norm_mix_w, v_norm_mix_w, _sv_get(flat, "norm_mix_w", D)),
        "conv_b": (conv_b, m_conv_b, v_conv_b, _sv_get(flat, "conv_b", XBC)),
        "dt_bias": (dt_bias, m_dt_bias, v_dt_bias, _sv_get(flat, "dt_bias", NH)),
        "a_log": (a_log, m_a_log, v_a_log, _sv_get(flat, "a_log", NH)),
        "d_skip": (d_skip, m_d_skip, v_d_skip, _sv_get(flat, "d_skip", NH)),
        "ssd_norm_w": (ssd_norm_w, m_ssd_norm_w, v_ssd_norm_w, _sv_get(flat, "ssd_norm_w", DI)),
        "pool_scale": (pool_scale, m_pool_scale, v_pool_scale, _sv_get(flat, "pool_scale", POOL_W)),
        "norm_mlp_w": (norm_mlp_w, m_norm_mlp_w, v_norm_mlp_w, _sv_get(flat, "norm_mlp_w", D)),
        "norm_final_w": (norm_final_w, m_norm_final_w, v_norm_final_w, _sv_get(flat, "norm_final_w", D)),
        "conv_w": (conv_w, m_conv_w, v_conv_w, g_conv_w),
    }
    names = list(small)
    sizes = [int(np.prod(small[n][0].shape)) for n in names]
    tot = sum(sizes)
    rows = -(-tot // 1024) * 8

    def pack(idx):
        v = jnp.concatenate([small[n][idx].reshape(-1) for n in names])
        return jnp.pad(v, (0, rows * 128 - tot)).reshape(rows, 128)

    sd, sm, sv2 = _adamw(pack(0), pack(3), pack(1), pack(2), name="adamw_small")
    small_out = {}
    off = 0
    for n, sz in zip(names, sizes):
        shp = small[n][0].shape
        small_out[n] = (small[n][3].reshape(shp), sd.reshape(-1)[off:off + sz].reshape(shp),
                        sm.reshape(-1)[off:off + sz].reshape(shp), sv2.reshape(-1)[off:off + sz].reshape(shp))
        off += sz

    def gpart(n, rows_):
        return gsum_mix[MIX_OFF[n]:MIX_OFF[n] + rows_]

    def lin(a):
        return a[0].T.reshape(IN_ROWS * 8, 128)

    g_lin = gsum_in[:IN_ROWS].reshape(IN_ROWS * 8, 128)
    dlt, mn, vn = _adamw(lin(w_in), g_lin, lin(m_w_in), lin(v_w_in), name="adamw_w_in", tr=IN_ROWS * 2)
    big_in = tuple(a.reshape(IN_ROWS, D).T[None] for a in (g_lin, dlt, mn, vn))

    big = {
        "w_ada": (w_ada, m_w_ada, v_w_ada, g_w_ada, (D, wloc)),
        "w_branch_ssd": (w_branch_ssd, m_w_branch_ssd, v_w_branch_ssd, gpart("bssd", 256), (256, D)),
        "pool_w": (pool_w, m_pool_w, v_pool_w, gpart("pool", 32).reshape(128, PGW), (128, PGW)),
        "w_branch_pool": (w_branch_pool, m_w_branch_pool, v_w_branch_pool, gpart("bpool", 128), (128, D)),
        "w_out": (w_out, m_w_out, v_w_out, gpart("out", 128), (128, D)),
        "w_up": (w_up, m_w_up, v_w_up, gsum_mlp[:512].T, (D, 512)),
        "w_down": (w_down, m_w_down, v_w_down, gsum_mlp[512:], (512, D)),
    }
    big_out = {}
    for n, (w, mm_, vv, g, shp2) in big.items():
        dlt, mn, vn = _adamw(w.reshape(shp2), g, mm_.reshape(shp2), vv.reshape(shp2), name="adamw_" + n)
        big_out[n] = (g.reshape(w.shape), dlt.reshape(w.shape), mn.reshape(w.shape), vn.reshape(w.shape))

    order = ["w_ada", "b_ada", "norm_mix_w", "w_in", "conv_w", "conv_b", "dt_bias", "a_log", "d_skip",
             "ssd_norm_w", "w_branch_ssd", "pool_w", "pool_scale", "w_branch_pool", "w_out", "norm_mlp_w",
             "w_up", "w_down", "norm_final_w"]
    big_out["w_in"] = big_in
    res = {**small_out, **big_out}
    outs = [loss, grad_x.reshape(x.shape)]
    for k in range(4):
        outs += [res[n][k] for n in order]
    return tuple(outs)
```

```python
import functools

import numpy as np
import jax
import jax.numpy as jnp
from jax import lax
from jax.experimental import pallas as pl
from jax.experimental.pallas import tpu as pltpu

F32 = jnp.float32
BF16 = jnp.bfloat16
SLAB_DT = jnp.bfloat16
_MXU_DTYPE = jnp.bfloat16

N_DEV = 8
D = 1024
DI = 2048
NH = 32
HP = 64
NG = 4
NS = 128
Q = 128
XBC = DI + 2 * NG * NS
DFF = 4096
N_IN = 8224
EPS = 1e-5
POOL_W = 1024
PGW = 256

C_XBC, C_POOL, C_Z, C_GATE, C_DT = 0, 3072, 4096, 6144, 8192
DT_PAD = 256
NPROJ = C_DT + DT_PAD

IN_ROWS = N_IN // N_DEV
IN_ROWS_P = 1040
CONV_ROWS = 16
REST_PARTS = (("bssd", 256), ("pool", 32), ("bpool", 128), ("out", 128), ("up", 512), ("down", 512))
REST_OFF = {}
_o = 0
for _n, _r in REST_PARTS:
    REST_OFF[_n] = _o
    _o += _r
REST_ROWS = _o
MIX_PARTS = (("bssd", 256), ("pool", 32), ("bpool", 128), ("out", 128))
MIX_OFF = {}
_o = 0
for _n, _r in MIX_PARTS:
    MIX_OFF[_n] = _o
    _o += _r
MIX_ROWS = _o

SV_PARTS = (("b_ada", 6144), ("norm_mix_w", 1024), ("conv_b", 3072), ("dt_bias", 128), ("a_log", 128),
            ("d_skip", 128), ("ssd_norm_w", 2048), ("pool_scale", 1024), ("norm_mlp_w", 1024),
            ("norm_final_w", 1024), ("conv_w", 4 * XBC), ("loss", 128))
SV_OFF = {}
_o = 0
for _n, _r in SV_PARTS:
    SV_OFF[_n] = _o
    _o += _r
SV_ROWS = 224
assert _o <= SV_ROWS * 128

ADAM_LR, ADAM_B1, ADAM_B2, ADAM_EPS, ADAM_WD, ADAM_STEP = 0.001, 0.9, 0.999, 1e-08, 0.01, 10

VMEM_BIG = 56 * 1024 * 1024
NEG = -1e30

NN = ((1,), (0,))
NT = ((1,), (1,))
TN = ((0,), (0,))


def _dot(a, b, dims=NN):
    return lax.dot_general(a.astype(_MXU_DTYPE), b.astype(_MXU_DTYPE), (dims, ((), ())),
                           preferred_element_type=F32)


def _dot_hi(a, b, dims=NN):
    return lax.dot_general(a.astype(F32), b.astype(F32), (dims, ((), ())),
                           precision=lax.Precision.HIGHEST, preferred_element_type=F32)


def _pick(n, cands):
    for c in cands:
        if n % c == 0:
            return c
    return n


def _sigmoid(x):
    return 1.0 / (1.0 + jnp.exp(-x))


def _silu(x):
    return x * _sigmoid(x)


def _dsilu(x):
    s = _sigmoid(x)
    return s * (1.0 + x * (1.0 - s))


def _softplus(x):
    return jnp.maximum(x, 0.0) + jnp.log(1.0 + jnp.exp(-jnp.abs(x)))


def _params(sem, vmem=None):
    return pltpu.CompilerParams(dimension_semantics=sem, vmem_limit_bytes=vmem)


def _mm(a, b, mode, *, name, outs, tm, tn, tk, extras=(), epilogue=None, aliases=None, prologue=None):
    if mode == "tn":
        K, M = a.shape
        N = b.shape[1]
        a_spec = pl.BlockSpec((tk, tm), lambda i, j, k: (k, i))
        b_spec = pl.BlockSpec((tk, tn), lambda i, j, k: (k, j))
        dims = TN
    else:
        M = a.shape[0]
        K = b.shape[0] if mode == "nn" else b.shape[1]
        if prologue is None:
            assert a.shape[1] == K
            a_spec = pl.BlockSpec((tm, tk), lambda i, j, k: (i, k))
        else:
            assert tk == K
            a_spec = pl.BlockSpec((tm, a.shape[1]), lambda i, j, k: (i, 0))
        if mode == "nn":
            N = b.shape[1]
            b_spec = pl.BlockSpec((tk, tn), lambda i, j, k: (k, j))
            dims = NN
        else:
            N = b.shape[0]
            b_spec = pl.BlockSpec((tn, tk), lambda i, j, k: (j, k))
            dims = NT
    assert M % tm == 0 and N % tn == 0 and K % tk == 0, (name, M, N, K, tm, tn, tk)
    nk = K // tk
    ne, no = len(extras), len(outs)
    if epilogue is None:
        def epilogue(acc, ex, out_refs):
            out_refs[0][...] = acc.astype(out_refs[0].dtype)

    def body(a_ref, b_ref, *rest):
        ex, out_refs = rest[:ne], rest[ne:ne + no]
        lhs = a_ref[...] if prologue is None else prologue(a_ref, ex, out_refs, pl.program_id(1))
        p = _dot(lhs, b_ref[...], dims)
        if nk == 1:
            epilogue(p, ex, out_refs)
        else:
            acc = rest[-1]
            k = pl.program_id(2)

            @pl.when(k == 0)
            def _():
                acc[...] = p

            @pl.when(jnp.logical_and(k > 0, k < nk - 1))
            def _():
                acc[...] += p

            @pl.when(k == nk - 1)
            def _():
                epilogue(acc[...] + p, ex, out_refs)

    out_specs, out_shape = [], []
    for o in outs:
        if isinstance(o, tuple):
            shape, dt, bs, im = o
            out_specs.append(pl.BlockSpec(bs, im))
            out_shape.append(jax.ShapeDtypeStruct(shape, dt))
        else:
            out_specs.append(pl.BlockSpec((tm, tn), lambda i, j, k: (i, j)))
            out_shape.append(jax.ShapeDtypeStruct((M, N), o))
    in_specs = [a_spec, b_spec]
    for _, bs, im in extras:
        in_specs.append(pl.BlockSpec(memory_space=pl.ANY) if bs is None else pl.BlockSpec(bs, im))
    res = pl.pallas_call(
        body, name=name,
        grid=(M // tm, N // tn, nk),
        in_specs=in_specs, out_specs=out_specs, out_shape=out_shape,
        scratch_shapes=[pltpu.VMEM((tm, tn), F32)] if nk > 1 else [],
        input_output_aliases={2 + e: o for e, o in (aliases or {}).items()},
        compiler_params=_params(("arbitrary", "arbitrary", "arbitrary"), VMEM_BIG),
    )(a, b, *[e[0] for e in extras])
    return res if no > 1 else res[0]


def _rows(tm, w=D, col=0):
    return (tm, w), lambda i, j, k, c=col: (i, c)


def _vecs(w=D, col=0):
    return (1, w), lambda i, j, k, c=col: (0, c)


def _sum_out(w=D):
    return ((1, w), F32, (1, w), lambda i, j, k: (0, 0))


def _mm_pool(a, w, scale, *, name, tm, transpose_w):
    L = a.shape[0]
    dims = NT if transpose_w else NN

    def body(a_ref, w_ref, *rest):
        p = _dot(a_ref[...], w_ref[...], dims)
        if transpose_w:
            rest[0][...] = p
        else:
            s_ref, o0, o1 = rest
            o0[...] = p.astype(o0.dtype)
            o1[...] = (p * s_ref[...]).astype(o1.dtype)

    blk = pl.BlockSpec((tm, PGW), lambda i, j: (i, j))
    in_specs = [blk, pl.BlockSpec((PGW, PGW), lambda i, j: (j, 0))]
    args = [a, w]
    if transpose_w:
        out_specs, out_shape = [blk], [jax.ShapeDtypeStruct((L, POOL_W), F32)]
    else:
        in_specs.append(pl.BlockSpec((1, PGW), lambda i, j: (0, j)))
        args.append(scale)
        out_specs = [blk, blk]
        out_shape = [jax.ShapeDtypeStruct((L, POOL_W), BF16), jax.ShapeDtypeStruct((L, POOL_W), BF16)]
    res = pl.pallas_call(body, name=name, grid=(L // tm, 4), in_specs=in_specs, out_specs=out_specs,
                         out_shape=out_shape, compiler_params=_params(("parallel", "parallel")))(*args)
    return res[0] if transpose_w else res


def _mm_pool_tn(a, b, *, name, tk):
    L = a.shape[0]

    def body(a_ref, b_ref, o_ref):
        p = _dot(a_ref[...], b_ref[...], TN)

        @pl.when(pl.program_id(1) == 0)
        def _():
            o_ref[...] = p

        @pl.when(pl.program_id(1) > 0)
        def _():
            o_ref[...] += p

    blk = pl.BlockSpec((tk, PGW), lambda g, k: (k, g))
    return pl.pallas_call(body, name=name, grid=(4, L // tk), in_specs=[blk, blk],
                          out_specs=pl.BlockSpec((PGW, PGW), lambda g, k: (g, 0)),
                          out_shape=jax.ShapeDtypeStruct((POOL_W, PGW), F32),
                          compiler_params=_params(("parallel", "arbitrary")))(a, b)


def _acc_out(ref, val, i):
    @pl.when(i == 0)
    def _():
        ref[...] = val

    @pl.when(i > 0)
    def _():
        ref[...] += val


def _colsum(v):
    return jnp.sum(v, axis=0, keepdims=True)


def _ep_resid_norm(acc, ex, outs):
    x_ref, g_ref, nw_ref, sc_ref, sh_ref = ex
    mix_ref, x1_ref, h_ref = outs
    mix_ref[...] = acc.astype(mix_ref.dtype)
    xv = x_ref[...] + g_ref[...] * acc
    x1_ref[...] = xv
    r = lax.rsqrt(jnp.mean(xv * xv, axis=-1, keepdims=True) + EPS)
    h_ref[...] = (xv * r * nw_ref[...] * (1.0 + sc_ref[...]) + sh_ref[...]).astype(h_ref.dtype)


def _ep_final(acc, ex, outs):
    x1_ref, t_ref, g_ref, nw_ref = ex
    dx2_ref, dd_ref, loss_ref, dnw_ref, dg_ref = outs
    i = pl.program_id(0)
    x2 = x1_ref[...] + g_ref[...] * acc
    r = lax.rsqrt(jnp.mean(x2 * x2, axis=-1, keepdims=True) + EPS)
    xh = x2 * r
    e = xh * nw_ref[...] - t_ref[...]
    part = 0.5 * jnp.sum(jnp.mean(e * e, axis=-1, keepdims=True), axis=0, keepdims=True)
    dy = e * (1.0 / D)
    g = dy * nw_ref[...]
    dx2 = r * (g - xh * jnp.mean(g * xh, axis=-1, keepdims=True))
    dx2_ref[...] = dx2
    dd_ref[...] = (dx2 * g_ref[...]).astype(dd_ref.dtype)
    _acc_out(loss_ref, jnp.broadcast_to(part, (1, 128)), i)
    _acc_out(dnw_ref, _colsum(dy * xh), i)
    _acc_out(dg_ref, _colsum(dx2 * acc), i)


def _ep_norm_bwd(acc, ex, outs):
    x_ref, dr_ref, nw_ref, sc_ref = ex[:4]
    dx_ref, p_ref, q_ref = outs[:3]
    i = pl.program_id(0)
    xv = x_ref[...]
    r = lax.rsqrt(jnp.mean(xv * xv, axis=-1, keepdims=True) + EPS)
    xh = xv * r
    g = acc * (nw_ref[...] * (1.0 + sc_ref[...]))
    dx = dr_ref[...] + r * (g - xh * jnp.mean(g * xh, axis=-1, keepdims=True))
    dx_ref[...] = dx
    _acc_out(p_ref, _colsum(acc * xh), i)
    _acc_out(q_ref, _colsum(acc), i)
    if len(ex) > 4:
        m_ref, g_ref = ex[4:]
        dm_ref, dg_ref = outs[3:]
        dm_ref[...] = (dx * g_ref[...]).astype(dm_ref.dtype)
        _acc_out(dg_ref, _colsum(dx * m_ref[...].astype(F32)), i)


def _ep_merge_bwd(acc, ex, outs):
    a_ref, b_ref, gl_ref = ex
    da_ref, db_ref, dgl_ref = outs
    s = _sigmoid(gl_ref[...].astype(F32))
    s1, s2 = s[:, :D], s[:, D:]
    da_ref[...] = (acc * s1).astype(da_ref.dtype)
    db_ref[...] = (acc * s2).astype(db_ref.dtype)
    dgl_ref[:, :D] = (acc * a_ref[...] * s1 * (1.0 - s1)).astype(dgl_ref.dtype)
    dgl_ref[:, D:] = (acc * b_ref[...] * s2 * (1.0 - s2)).astype(dgl_ref.dtype)


def _ep_pscale_bwd(acc, ex, outs):
    y_ref, s_ref = ex
    o_ref, ds_ref = outs
    o_ref[...] = (acc * s_ref[...]).astype(o_ref.dtype)
    _acc_out(ds_ref, _colsum(acc * y_ref[...].astype(F32)), pl.program_id(0))


GW = DI // NG


def _ep_gated_norm_bwd(acc, ex, outs):
    y_ref, z_ref, w_ref, _ = ex
    dy_ref, dz_ref, dw_ref = outs
    zv = z_ref[...].astype(F32)
    yv = y_ref[...]
    sz = _silu(zv)
    yg = yv * sz
    dsz = _dsilu(zv)
    dws = []
    for k in range(NG):
        sl = slice(k * GW, (k + 1) * GW)
        seg = yg[:, sl]
        r = lax.rsqrt(jnp.mean(seg * seg, axis=-1, keepdims=True) + EPS)
        sh = seg * r
        dn = acc[:, sl]
        g = dn * w_ref[:, sl]
        dyg = r * (g - sh * jnp.mean(g * sh, axis=-1, keepdims=True))
        dy_ref[:, sl] = dyg * sz[:, sl]
        dz_ref[:, sl] = (dyg * yv[:, sl] * dsz[:, sl]).astype(dz_ref.dtype)
        dws.append(_colsum(dn * sh))
    _acc_out(dw_ref, jnp.concatenate(dws, axis=1), pl.program_id(0))


CONV_CB = 128
HALO = 16


def _time_chunk(L):
    return _pick(L, (256, 128))


def _with_halo(x_ref, i, r0, rc):
    p0 = pl.multiple_of(jnp.maximum(r0 - HALO, 0), HALO)
    prev = jnp.where(i > 0, x_ref[pl.ds(p0, HALO), :].astype(F32), 0.0)
    return jnp.concatenate([prev, x_ref[pl.ds(r0, rc), :].astype(F32)], axis=0)


def _conv_fwd(proj, w, b, *, name):
    L = proj.shape[0]
    rc = _time_chunk(L)
    n = L // rc

    def body(x_ref, w_ref, b_ref, o_ref):
        wv = w_ref[...]
        bv = b_ref[...]

        def step(i, c):
            r0 = pl.multiple_of(i * rc, rc)
            ext = _with_halo(x_ref, i, r0, rc)
            acc = bv + ext * wv[3:4]
            for j in (1, 2, 3):
                acc = acc + pltpu.roll(ext, j, 0) * wv[3 - j:4 - j]
            acc = acc[HALO:]
            o_ref[pl.ds(r0, rc), :] = acc * _sigmoid(acc)
            return c

        lax.fori_loop(0, n, step, 0)

    return pl.pallas_call(
        body, name=name, grid=(XBC // CONV_CB,),
        in_specs=[pl.BlockSpec((L, CONV_CB), lambda j: (0, j + C_XBC // CONV_CB)),
                  pl.BlockSpec((4, CONV_CB), lambda j: (0, j)), pl.BlockSpec((1, CONV_CB), lambda j: (0, j))],
        out_specs=pl.BlockSpec((L, CONV_CB), lambda j: (0, j)),
        out_shape=jax.ShapeDtypeStruct((L, XBC), F32),
        compiler_params=_params(("parallel",), VMEM_BIG))(proj, w, b)


def _conv_bwd(proj, dy, w, b, dproj, *, name):
    L = proj.shape[0]
    rc = _time_chunk(L)
    n = L // rc

    def body(x_ref, dy_ref, w_ref, b_ref, dp_in, dx_ref, dw_ref, db_ref):
        del dp_in
        wv = w_ref[...]
        bv = b_ref[...]

        def step(k, carry):
            nxt, db, d0, d1, d2, d3 = carry
            i = n - 1 - k
            r0 = pl.multiple_of(i * rc, rc)
            ext = _with_halo(x_ref, i, r0, rc)
            xk = [ext[HALO:]] + [pltpu.roll(ext, j, 0)[HALO:] for j in (1, 2, 3)]
            pre = bv
            for j in range(4):
                pre = pre + xk[j] * wv[3 - j:4 - j]
            dpre = dy_ref[pl.ds(r0, rc), :] * _dsilu(pre)
            dext = jnp.concatenate([dpre, nxt], axis=0)
            acc = dext * wv[3:4]
            for j in (1, 2, 3):
                acc = acc + pltpu.roll(dext, rc + HALO - j, 0) * wv[3 - j:4 - j]
            dx_ref[pl.ds(r0, rc), :] = acc[:rc].astype(dx_ref.dtype)
            return (dpre[:HALO], db + _colsum(dpre), d0 + _colsum(dpre * xk[3]), d1 + _colsum(dpre * xk[2]),
                    d2 + _colsum(dpre * xk[1]), d3 + _colsum(dpre * xk[0]))

        z = jnp.zeros((1, CONV_CB), F32)
        _, db, d0, d1, d2, d3 = lax.fori_loop(0, n, step, (jnp.zeros((HALO, CONV_CB), F32), z, z, z, z, z))
        db_ref[...] = db
        dw_ref[...] = jnp.concatenate([d0, d1, d2, d3], axis=0)

    nb = XBC // CONV_CB
    return pl.pallas_call(
        body, name=name, grid=(nb,),
        in_specs=[pl.BlockSpec((L, CONV_CB), lambda j: (0, j + C_XBC // CONV_CB)),
                  pl.BlockSpec((L, CONV_CB), lambda j: (0, j)),
                  pl.BlockSpec((4, CONV_CB), lambda j: (0, j)), pl.BlockSpec((1, CONV_CB), lambda j: (0, j)),
                  pl.BlockSpec(memory_space=pl.ANY)],
        out_specs=[pl.BlockSpec((L, CONV_CB), lambda j: (0, j + C_XBC // CONV_CB)),
                   pl.BlockSpec((4, CONV_CB), lambda j: (0, j)), pl.BlockSpec((1, CONV_CB), lambda j: (0, j))],
        out_shape=[jax.ShapeDtypeStruct((L, NPROJ), BF16), jax.ShapeDtypeStruct((4, XBC), F32),
                   jax.ShapeDtypeStruct((1, XBC), F32)],
        input_output_aliases={4: 0},
        compiler_params=_params(("parallel",), VMEM_BIG))(proj, dy, w, b, dproj)


def _pool_fwd(proj, *, name):
    L = proj.shape[0]
    rc = _time_chunk(L)
    n = L // rc

    def body(x_ref, o_ref, pad):
        g = pl.program_id(0)
        pad[0:HALO, :] = jnp.zeros((HALO, PGW), F32)

        def fill(i, c):
            r0 = pl.multiple_of(i * rc, rc)
            pad[pl.ds(r0 + HALO, rc), :] = x_ref[pl.ds(r0, rc), :].astype(F32)
            return c

        lax.fori_loop(0, n, fill, 0)
        rows = lax.broadcasted_iota(jnp.int32, (rc, PGW), 0)

        for gi in range(4):
            win = 2 << gi

            @pl.when(g == gi)
            def _(gi=gi, win=win):
                def step(i, c):
                    r0 = pl.multiple_of(i * rc, rc)
                    ext = pad[pl.ds(r0, rc + HALO), :]
                    s = ext
                    sh = 1
                    while sh < win:
                        s = s + pltpu.roll(s, sh, 0)
                        sh *= 2
                    cnt = jnp.minimum(rows + (r0 + 1), win).astype(F32)
                    o_ref[pl.ds(r0, rc), :] = (s[HALO:] / cnt - ext[HALO:]).astype(o_ref.dtype)
                    return c

                lax.fori_loop(0, n, step, 0)

    return pl.pallas_call(
        body, name=name, grid=(4,),
        in_specs=[pl.BlockSpec((L, PGW), lambda j: (0, j + C_POOL // PGW))],
        out_specs=pl.BlockSpec((L, PGW), lambda j: (0, j)),
        out_shape=jax.ShapeDtypeStruct((L, POOL_W), BF16),
        scratch_shapes=[pltpu.VMEM((L + HALO, PGW), F32)],
        compiler_params=_params(("parallel",), VMEM_BIG))(proj)


def _pool_bwd(dpooled, dproj, *, name):
    L = dpooled.shape[0]
    rc = _time_chunk(L)
    n = L // rc

    def body(d_ref, dp_in, o_ref, pad):
        del dp_in
        g = pl.program_id(0)
        pad[L:L + HALO, :] = jnp.zeros((HALO, PGW), F32)
        rows = lax.broadcasted_iota(jnp.int32, (rc, PGW), 0)

        for gi in range(4):
            win = 2 << gi

            @pl.when(g == gi)
            def _(gi=gi, win=win):
                def fill(i, c):
                    r0 = pl.multiple_of(i * rc, rc)
                    cnt = jnp.minimum(rows + (r0 + 1), win).astype(F32)
                    pad[pl.ds(r0, rc), :] = d_ref[pl.ds(r0, rc), :] / cnt
                    return c

                lax.fori_loop(0, n, fill, 0)

                def step(i, c):
                    r0 = pl.multiple_of(i * rc, rc)
                    s = pad[pl.ds(r0, rc + HALO), :]
                    sh = 1
                    while sh < win:
                        s = s + pltpu.roll(s, rc + HALO - sh, 0)
                        sh *= 2
                    o_ref[pl.ds(r0, rc), :] = (s[:rc] - d_ref[pl.ds(r0, rc), :]).astype(o_ref.dtype)
                    return c

                lax.fori_loop(0, n, step, 0)

    return pl.pallas_call(
        body, name=name, grid=(4,),
        in_specs=[pl.BlockSpec((L, PGW), lambda j: (0, j)), pl.BlockSpec(memory_space=pl.ANY)],
        out_specs=pl.BlockSpec((L, PGW), lambda j: (0, j + C_POOL // PGW)),
        out_shape=jax.ShapeDtypeStruct((L, NPROJ), BF16),
        scratch_shapes=[pltpu.VMEM((L + HALO, PGW), F32)],
        input_output_aliases={1: 0},
        compiler_params=_params(("parallel",), VMEM_BIG))(dpooled, dproj)


_SPLIT_DT = jnp.bfloat16


def _ssd_consts():
    tri = np.tril(np.ones((Q, Q), np.float32))
    exp = np.zeros((128, DI), np.float32)
    for h in range(NH):
        exp[h, h * HP:(h + 1) * HP] = 1.0
    exp2 = np.concatenate([exp, exp], axis=0)
    return (jnp.asarray(tri, dtype=_SPLIT_DT), jnp.asarray(tri.T.copy(), dtype=_SPLIT_DT),
            jnp.asarray(exp2, dtype=_SPLIT_DT))


def _split(v, n):
    parts, r = [], v
    for _ in range(n):
        p = r.astype(_SPLIT_DT)
        parts.append(p)
        r = r - p.astype(F32)
    return parts


def _bdot(a, b, dims):
    return lax.dot_general(a, b, (dims, ((), ())), preferred_element_type=F32)


def _tri_sum(t_ref, v):
    r = _bdot(t_ref[...], jnp.concatenate(_split(v, 3), axis=1), NN)
    return r[:, :128] + r[:, 128:256] + r[:, 256:]


def _expand(v, e2_ref):
    return _bdot(jnp.concatenate(_split(v, 2), axis=1), e2_ref[...], NN)


def _reduce_heads(vals, eg):
    parts = []
    for v in vals:
        parts += _split(v, 2)
    r = _bdot(jnp.concatenate(parts, axis=0), eg, NT)
    return [r[2 * i * Q:(2 * i + 1) * Q] + r[(2 * i + 1) * Q:(2 * i + 2) * Q] for i in range(len(vals))]


def _ssd_common(xbc_ref, dtw_ref, dtb_ref, arow_ref, t_ref, e_ref):
    pre = dtw_ref[:, :128] + dtb_ref[...]
    dt = _softplus(pre)
    acs = _tri_sum(t_ref, dt * arow_ref[...])
    acs_x = _expand(acs, e_ref)
    dt_x = _expand(dt, e_ref)
    xs = xbc_ref[:, 0:DI]
    return pre, dt, acs, acs.T, acs_x, dt_x, xs


def _ssd_fwd(xbc, proj, dtb, arow, dsk_x, *, name):
    L = xbc.shape[0]
    nc = L // Q
    tri, _, expand = _ssd_consts()

    def body(xbc_ref, dtw_ref, dtb_ref, arow_ref, dsk_ref, t_ref, e_ref, y_ref, hs_ref, h_scr):
        @pl.when(pl.program_id(0) == 0)
        def _():
            h_scr[...] = jnp.zeros_like(h_scr)

        _, dt, acs, acs_t, acs_x, dt_x, xs = _ssd_common(xbc_ref, dtw_ref, dtb_ref, arow_ref, t_ref, e_ref)
        xdt = xs * dt_x
        eacs = jnp.exp(acs_x)
        acs_last = acs_x[Q - 1:Q, :]
        dec = jnp.exp(acs_last - acs_x)
        hs_ref[0] = h_scr[...].astype(hs_ref.dtype)
        causal = lax.broadcasted_iota(jnp.int32, (Q, Q), 0) >= lax.broadcasted_iota(jnp.int32, (Q, Q), 1)
        first = lax.broadcasted_iota(jnp.int32, (Q, 128), 1) < HP
        for g in range(NG):
            bg = xbc_ref[:, DI + g * NS:DI + (g + 1) * NS]
            cg = xbc_ref[:, DI + NG * NS + g * NS:DI + NG * NS + (g + 1) * NS]
            s = _dot(cg, bg, NT)
            sl = slice(g * GW, (g + 1) * GW)
            hg = h_scr[:, sl]
            yoff = _dot(cg, hg, NN) * eacs[:, sl]
            st = _dot(bg, xdt[:, sl] * dec[:, sl], TN)
            h_scr[:, sl] = hg * eacs[Q - 1:Q, sl] + st
            for j in range(4):
                lo = g * GW + j * 128
                xb = xdt[:, lo:lo + 128]
                yp = yoff[:, j * 128:(j + 1) * 128] + dsk_ref[:, lo:lo + 128] * xs[:, lo:lo + 128]
                for e in range(2):
                    h = g * 8 + j * 2 + e
                    lm = jnp.exp(jnp.where(causal, acs[:, h:h + 1] - acs_t[h:h + 1, :], NEG))
                    xm = jnp.where(first if e == 0 else jnp.logical_not(first), xb, 0.0)
                    yp = yp + _dot(s * lm, xm, NN)
                y_ref[:, lo:lo + 128] = yp

    return pl.pallas_call(
        body, name=name, grid=(nc,),
        in_specs=[pl.BlockSpec((Q, XBC), lambda c: (c, 0)),
                  pl.BlockSpec((Q, DT_PAD), lambda c: (c, 0)),
                  pl.BlockSpec((1, 128), lambda c: (0, 0)), pl.BlockSpec((1, 128), lambda c: (0, 0)),
                  pl.BlockSpec((1, DI), lambda c: (0, 0)),
                  pl.BlockSpec((Q, Q), lambda c: (0, 0)), pl.BlockSpec((256, DI), lambda c: (0, 0))],
        out_specs=[pl.BlockSpec((Q, DI), lambda c: (c, 0)), pl.BlockSpec((1, NS, DI), lambda c: (c, 0, 0))],
        out_shape=[jax.ShapeDtypeStruct((L, DI), F32), jax.ShapeDtypeStruct((nc, NS, DI), F32)],
        scratch_shapes=[pltpu.VMEM((NS, DI), F32)],
        compiler_params=_params(("arbitrary",), VMEM_BIG))(xbc, proj, dtb, arow, dsk_x, tri, expand)


def _ssd_bwd(dy, xbc, proj, hs, dtb, arow, dsk_x, dproj, *, name):
    L = xbc.shape[0]
    nc = L // Q
    tri, triu, expand = _ssd_consts()

    def body(dy_ref, xbc_ref, dtw_ref, hs_ref, dtb_ref, arow_ref, dsk_ref, t_ref, u_ref, e_ref, dp_in,
             dxbc_ref, ddtw_ref, da_ref, ddx_ref, ddtb_ref, dh_scr):
        del dp_in
        i = pl.program_id(0)

        @pl.when(i == 0)
        def _():
            dh_scr[...] = jnp.zeros_like(dh_scr)

        pre, dt, acs, acs_t, acs_x, dt_x, xs = _ssd_common(xbc_ref, dtw_ref, dtb_ref, arow_ref, t_ref, e_ref)
        dyv = dy_ref[...]
        xdt = xs * dt_x
        eacs = jnp.exp(acs_x)
        acs_last = acs_x[Q - 1:Q, :]
        dec = jnp.exp(acs_last - acs_x)
        gy = dyv * eacs
        causal = lax.broadcasted_iota(jnp.int32, (Q, Q), 0) >= lax.broadcasted_iota(jnp.int32, (Q, Q), 1)
        first = lax.broadcasted_iota(jnp.int32, (Q, 128), 1) < HP
        lane_h = lax.broadcasted_iota(jnp.int32, (Q, 128), 1)
        sub_h = lax.broadcasted_iota(jnp.int32, (128, Q), 0)
        last_row = lax.broadcasted_iota(jnp.int32, (Q, GW), 0) == Q - 1
        dacs = jnp.zeros((Q, 128), F32)
        dacs_t = jnp.zeros((128, Q), F32)
        ddt = jnp.zeros((Q, 128), F32)
        for g in range(NG):
            bg = xbc_ref[:, DI + g * NS:DI + (g + 1) * NS]
            cg = xbc_ref[:, DI + NG * NS + g * NS:DI + NG * NS + (g + 1) * NS]
            s = _dot(cg, bg, NT)
            sl = slice(g * GW, (g + 1) * GW)
            hg = hs_ref[0, :, sl].astype(F32)
            dhn = dh_scr[:, sl]
            eal = eacs[Q - 1:Q, sl]
            gg = gy[:, sl]
            dax = gg * _dot(cg, hg, NN)
            dcg = _dot(gg, hg, NT)
            dh_scr[:, sl] = _dot(cg, gg, TN) + dhn * eal
            dal = eal * _colsum(dhn * hg)
            xdd = xdt[:, sl] * dec[:, sl]
            dbg = _dot(xdd, dhn, NT)
            wv = _dot(bg, dhn, NN)
            dd = wv * xdd
            dax = dax - dd
            dal = dal + _colsum(dd)
            dax = dax + jnp.where(last_row, dal, 0.0)
            dxdt_g = wv * dec[:, sl]
            ds = jnp.zeros((Q, Q), F32)
            dxdt_blocks = []
            for j in range(4):
                lo = g * GW + j * 128
                xb = xdt[:, lo:lo + 128]
                dyb = dyv[:, lo:lo + 128]
                dxb = dxdt_g[:, j * 128:(j + 1) * 128]
                for e in range(2):
                    h = g * 8 + j * 2 + e
                    lm = jnp.exp(jnp.where(causal, acs[:, h:h + 1] - acs_t[h:h + 1, :], NEG))
                    m = s * lm
                    dym = jnp.where(first if e == 0 else jnp.logical_not(first), dyb, 0.0)
                    dm = _dot(dym, xb, NT)
                    r = dm * m
                    dacs = dacs + jnp.where(lane_h == h, jnp.sum(r, axis=1, keepdims=True), 0.0)
                    dacs_t = dacs_t + jnp.where(sub_h == h, _colsum(r), 0.0)
                    ds = ds + dm * lm
                    dxb = dxb + _dot(m, dym, TN)
                dxdt_blocks.append(dxb)
            dxdt = jnp.concatenate(dxdt_blocks, axis=1)
            dcg = dcg + _dot(ds, bg, NN)
            dbg = dbg + _dot(ds, cg, TN)
            dxbc_ref[:, DI + g * NS:DI + (g + 1) * NS] = dbg
            dxbc_ref[:, DI + NG * NS + g * NS:DI + NG * NS + (g + 1) * NS] = dcg
            dxbc_ref[:, sl] = dsk_ref[:, sl] * dyv[:, sl] + dxdt * dt_x[:, sl]
            ddt_g, dacs_g = _reduce_heads([dxdt * xs[:, sl], dax], e_ref[0:128, sl])
            ddt = ddt + ddt_g
            dacs = dacs + dacs_g
        dacs = dacs - dacs_t.T
        ddta = _tri_sum(u_ref, dacs)
        ddt = ddt + ddta * arow_ref[...]
        ddtw = jnp.where(lane_h < NH, ddt * _sigmoid(pre), 0.0)
        ddtw_ref[...] = jnp.concatenate([ddtw, jnp.zeros((Q, DT_PAD - 128), F32)], axis=1).astype(ddtw_ref.dtype)
        _acc_out(da_ref, _colsum(ddta * dt), i)
        _acc_out(ddx_ref, _colsum(dyv * xs), i)
        _acc_out(ddtb_ref, _colsum(ddtw), i)

    rev = lambda c: (nc - 1 - c, 0)
    const = lambda c: (0, 0)
    return pl.pallas_call(
        body, name=name, grid=(nc,),
        in_specs=[pl.BlockSpec((Q, DI), rev), pl.BlockSpec((Q, XBC), rev),
                  pl.BlockSpec((Q, DT_PAD), rev),
                  pl.BlockSpec((1, NS, DI), lambda c: (nc - 1 - c, 0, 0)),
                  pl.BlockSpec((1, 128), const), pl.BlockSpec((1, 128), const), pl.BlockSpec((1, DI), const),
                  pl.BlockSpec((Q, Q), const), pl.BlockSpec((Q, Q), const), pl.BlockSpec((256, DI), const),
                  pl.BlockSpec(memory_space=pl.ANY)],
        out_specs=[pl.BlockSpec((Q, XBC), rev),
                   pl.BlockSpec((Q, DT_PAD), lambda c: (nc - 1 - c, C_DT // DT_PAD)),
                   pl.BlockSpec((1, 128), const), pl.BlockSpec((1, DI), const), pl.BlockSpec((1, 128), const)],
        out_shape=[jax.ShapeDtypeStruct((L, XBC), F32), jax.ShapeDtypeStruct((L, NPROJ), BF16),
                   jax.ShapeDtypeStruct((1, 128), F32), jax.ShapeDtypeStruct((1, DI), F32),
                   jax.ShapeDtypeStruct((1, 128), F32)],
        scratch_shapes=[pltpu.VMEM((NS, DI), F32)],
        input_output_aliases={10: 1},
        compiler_params=_params(("arbitrary",), VMEM_BIG))(dy, xbc, proj, hs, dtb, arow, dsk_x, tri, triu,
                                                          expand, dproj)


def _adamw(w, g, m, v, *, name, tr=None):
    R, C = w.shape
    if tr is None:
        tr = _pick(R, (256, 128, 64, 32, 16, 8))
    assert R % tr == 0
    c1 = 1.0 - ADAM_B1 ** ADAM_STEP
    c2 = 1.0 - ADAM_B2 ** ADAM_STEP

    def body(w_ref, g_ref, m_ref, v_ref, d_ref, mo_ref, vo_ref):
        gv = g_ref[...]
        mn = ADAM_B1 * m_ref[...] + (1.0 - ADAM_B1) * gv
        vn = ADAM_B2 * v_ref[...] + (1.0 - ADAM_B2) * (gv * gv)
        mo_ref[...] = mn
        vo_ref[...] = vn
        d_ref[...] = -ADAM_LR * ((mn / c1) / (jnp.sqrt(vn / c2) + ADAM_EPS) + ADAM_WD * w_ref[...])

    spec = pl.BlockSpec((tr, C), lambda i: (i, 0))
    return pl.pallas_call(body, name=name, grid=(pl.cdiv(R, tr),), in_specs=[spec] * 4, out_specs=[spec] * 3,
                          out_shape=[jax.ShapeDtypeStruct((R, C), F32)] * 3,
                          compiler_params=_params(("parallel",)))(w, g, m, v)


def _slab_sum(recv, *, tile, name):
    rows = recv.shape[1]
    assert rows % tile == 0 and tile % 16 == 0

    def body(r_ref, o_ref):
        acc = r_ref[0].astype(F32)
        for j in range(1, N_DEV):
            acc = acc + r_ref[j].astype(F32)
        o_ref[...] = acc

    return pl.pallas_call(body, name=name, grid=(rows // tile,),
                          in_specs=[pl.BlockSpec((N_DEV, tile, D), lambda i: (0, i, 0))],
                          out_specs=pl.BlockSpec((tile, D), lambda i: (i, 0)),
                          out_shape=jax.ShapeDtypeStruct((rows, D), F32),
                          compiler_params=_params(("parallel",)))(recv)


MESH = pl.DeviceIdType.MESH


def _coords():
    return lax.axis_index("x"), lax.axis_index("y"), lax.axis_index("c")


def _peer(k):
    x, y, c = _coords()
    px = 1 - x if k & 4 else x
    py = 1 - y if k & 2 else y
    pc = 1 - c if k & 1 else c
    return (px, py, pc), 4 * px + 2 * py + pc


def _rcopy(src, dst, ssem, rsem, dev):
    return pltpu.make_async_remote_copy(src_ref=src, dst_ref=dst, send_sem=ssem, recv_sem=rsem,
                                        device_id=dev, device_id_type=MESH)


def _exchange_all(src_of, dst_slot, send_sems, recv_sems):
    x, y, c = _coords()
    me = 4 * x + 2 * y + c
    sent = []
    for k in range(1, N_DEV):
        dev, pidx = _peer(k)
        cp = _rcopy(src_of(pidx), dst_slot(me), send_sems.at[k - 1], recv_sems.at[k - 1], dev)
        cp.start()
        sent.append(cp)
    for k in range(1, N_DEV):
        dev, pidx = _peer(k)
        _rcopy(src_of(pidx), dst_slot(pidx), send_sems.at[k - 1], recv_sems.at[k - 1], dev).wait_recv()
    for cp in sent:
        cp.wait_send()


def _rows_of_slots(buf, nslots):
    rows = lax.broadcasted_iota(jnp.int32, (8, buf.shape[-1]), 0)
    out = jnp.zeros((8, buf.shape[-1]), F32)
    for j in range(nslots):
        out = out + jnp.where(rows == j, buf[j], 0.0)
    return out


def _ada_fwd(c, w_ada, b_r, *, name):
    wloc = w_ada.shape[1]

    def body(c_ref, w_ref, b_ref, mod_ref, call_ref, csrc, cbuf, psrc, pbuf, s1, r1, s2, r2):
        x, y, cc = _coords()
        me = 4 * x + 2 * y + cc
        csrc[...] = jnp.broadcast_to(c_ref[...], (8, D))
        cbuf[me] = csrc[...]
        _exchange_all(lambda p: csrc, lambda s: cbuf.at[s], s1, r1)
        call = _rows_of_slots(cbuf, N_DEV)
        call_ref[...] = call
        prod = _dot_hi(_silu(call), w_ref[...])
        for b in range(N_DEV):
            psrc[b] = jnp.broadcast_to(prod[b:b + 1, :], (8, wloc))
        pbuf[me] = psrc[me]
        _exchange_all(lambda p: psrc.at[p], lambda s: pbuf.at[s], s2, r2)
        mod_ref[...] = _rows_of_slots(pbuf, N_DEV) + b_ref[...]

    vm = pl.BlockSpec(memory_space=pltpu.VMEM)
    return pl.pallas_call(
        body, name=name, in_specs=[vm, vm, vm], out_specs=[vm, vm],
        out_shape=[jax.ShapeDtypeStruct((N_DEV, wloc), F32), jax.ShapeDtypeStruct((N_DEV, D), F32)],
        scratch_shapes=[pltpu.VMEM((8, D), F32), pltpu.VMEM((N_DEV, 8, D), F32),
                        pltpu.VMEM((N_DEV, 8, wloc), F32), pltpu.VMEM((N_DEV, 8, wloc), F32),
                        pltpu.SemaphoreType.DMA((N_DEV - 1,)), pltpu.SemaphoreType.DMA((N_DEV - 1,)),
                        pltpu.SemaphoreType.DMA((N_DEV - 1,)), pltpu.SemaphoreType.DMA((N_DEV - 1,))],
        compiler_params=pltpu.CompilerParams(vmem_limit_bytes=VMEM_BIG))(c, w_ada, b_r)


def _gather_slabs(slab, *, name):
    def body(x_ref, out_ref, send_sems, recv_sems, local_sem):
        x, y, c = _coords()
        me, sibling = (x, y, c), (x, y, 1 - c)
        chips = [(1 - x, y), (x, 1 - y), (1 - x, 1 - y)]

        def slot(px, py, pc):
            return out_ref.at[4 * px + 2 * py + pc]

        def copy(k, block, to, src=None):
            return _rcopy(slot(*block) if src is None else src, slot(*block), send_sems.at[k], recv_sems.at[k], to)

        mine = pltpu.make_async_copy(x_ref, slot(*me), local_sem)
        mine.start()
        first = [copy(0, me, sibling, src=x_ref)]
        first += [copy(1 + j, me, (*chip, c), src=x_ref) for j, chip in enumerate(chips)]
        for cp in first:
            cp.start()
        passed = [copy(4 + j, (*chip, c), sibling) for j, chip in enumerate(chips)]
        for j, chip in enumerate(chips):
            copy(1 + j, (*chip, c), me).wait_recv()
            passed[j].start()
        copy(0, sibling, me).wait_recv()
        for j, chip in enumerate(chips):
            copy(4 + j, (*chip, 1 - c), me).wait_recv()
        for cp in first + passed:
            cp.wait_send()
        mine.wait()

    anyspec = pl.BlockSpec(memory_space=pl.ANY)
    return pl.pallas_call(
        body, name=name, in_specs=[anyspec], out_specs=anyspec,
        out_shape=jax.ShapeDtypeStruct((N_DEV,) + slab.shape, slab.dtype),
        scratch_shapes=[pltpu.SemaphoreType.DMA((7,)), pltpu.SemaphoreType.DMA((7,)), pltpu.SemaphoreType.DMA],
    )(slab)


_HBM =pl.BlockSpec(memory_space=pltpu.HBM)
_SEM = pl.BlockSpec(memory_space=pltpu.SEMAPHORE)
_EFFECT = pltpu.SideEffectType.DATAFLOW_SIDE_EFFECTING


def _xchg_src(src_ref, pidx, per_peer):
    return src_ref.at[pidx] if per_peer else src_ref


def _xchg_start(src, *, per_peer, name):
    rows = src.shape[-2]
    land_shape = (N_DEV, rows, D)

    def body(src_ref, land_ref, send_sems, recv_sems, src_thru, land_thru, token):
        del src_thru, land_thru
        x, y, c = _coords()
        me = 4 * x + 2 * y + c
        for k in range(1, N_DEV):
            dev, pidx = _peer(k)
            _rcopy(_xchg_src(src_ref, pidx, per_peer), land_ref.at[me], send_sems.at[k - 1],
                   recv_sems.at[k - 1], dev).start()
        token[...] = jnp.zeros_like(token)

    return pl.pallas_call(
        body, name=name,
        out_shape=(pltpu.SemaphoreType.DMA((N_DEV - 1,)), pltpu.SemaphoreType.DMA((N_DEV - 1,)),
                   pltpu.HBM(src.shape, src.dtype), pltpu.HBM(land_shape, src.dtype),
                   jax.ShapeDtypeStruct((8, 128), F32)),
        in_specs=(_HBM, _HBM),
        out_specs=(_SEM, _SEM, _HBM, _HBM, pl.BlockSpec(memory_space=pltpu.VMEM)),
        input_output_aliases={0: 2, 1: 3},
        compiler_params=pltpu.CompilerParams(has_side_effects=_EFFECT),
    )(pltpu.with_memory_space_constraint(src, pltpu.HBM),
      pltpu.with_memory_space_constraint(lax.empty(land_shape, src.dtype), pltpu.HBM))


def _xchg_wait(started, after, *, per_peer, name):
    send_sems, recv_sems, src_thru, land_thru, _ = started

    def body(src_ref, land_ref, send_sems, recv_sems, after_ref, src_dead, got_ref):
        del after_ref, src_dead, got_ref
        for k in range(1, N_DEV):
            dev, pidx = _peer(k)
            cp = _rcopy(_xchg_src(src_ref, pidx, per_peer), land_ref.at[pidx], send_sems.at[k - 1],
                        recv_sems.at[k - 1], dev)
            cp.wait_send()
            cp.wait_recv()

    return pl.pallas_call(
        body, name=name,
        out_shape=(pltpu.HBM(src_thru.shape, src_thru.dtype), pltpu.HBM(land_thru.shape, land_thru.dtype)),
        in_specs=(_HBM, _HBM, _SEM, _SEM, pl.BlockSpec(memory_space=pl.ANY)),
        out_specs=(_HBM, _HBM),
        input_output_aliases={0: 0, 1: 1},
        compiler_params=pltpu.CompilerParams(has_side_effects=_EFFECT),
    )(src_thru, land_thru, send_sems, recv_sems, after)


def _dep(token):
    return (token, (8, 128), lambda i, j, k: (0, 0))


def _small_allsum(sv, *, name):
    def body(sv_ref, all_ref, sum_ref, send_sems, recv_sems):
        x, y, c = _coords()
        me = 4 * x + 2 * y + c
        all_ref[me] = sv_ref[...]
        _exchange_all(lambda p: sv_ref, lambda s: all_ref.at[s], send_sems, recv_sems)
        acc = all_ref[0]
        for j in range(1, N_DEV):
            acc = acc + all_ref[j]
        sum_ref[...] = acc

    vm = pl.BlockSpec(memory_space=pltpu.VMEM)
    return pl.pallas_call(
        body, name=name, in_specs=[vm], out_specs=[vm, vm],
        out_shape=[jax.ShapeDtypeStruct((N_DEV, SV_ROWS, 128), F32), jax.ShapeDtypeStruct((SV_ROWS, 128), F32)],
        scratch_shapes=[pltpu.SemaphoreType.DMA((7,)), pltpu.SemaphoreType.DMA((7,))],
    )(sv)


def _ada_bwd(call, dmod_loc, *, name):
    wloc = dmod_loc.shape[1]

    def body(c_ref, d_ref, o_ref):
        o_ref[...] = _dot_hi(_silu(c_ref[...]), d_ref[...], TN)

    vm = pl.BlockSpec(memory_space=pltpu.VMEM)
    return pl.pallas_call(body, name=name, in_specs=[vm, vm], out_specs=vm,
                          out_shape=jax.ShapeDtypeStruct((D, wloc), F32),
                          compiler_params=pltpu.CompilerParams(vmem_limit_bytes=VMEM_BIG))(call, dmod_loc)


def _pad_rows(a, rows):
    return jnp.pad(a, ((0, rows - a.shape[0]), (0, 0)))


def _reorder_in_rows(gs):
    wt = gs[:, :IN_ROWS].reshape(N_IN, D)
    z, xbc, dt, pool, gates = wt[0:2048], wt[2048:5120], wt[5120:5152], wt[5152:6176], wt[6176:8224]
    return jnp.concatenate([xbc, pool, z, gates, dt, jnp.zeros((DT_PAD - 32, D), wt.dtype)], axis=0)


def _restore_in_shards(d):
    wt = jnp.concatenate([d[C_Z:C_Z + 2048], d[C_XBC:C_XBC + XBC], d[C_DT:C_DT + 32],
                          d[C_POOL:C_POOL + 1024], d[C_GATE:C_GATE + 2048]], axis=0)
    return jnp.pad(wt.reshape(N_DEV, IN_ROWS, D), ((0, 0), (0, IN_ROWS_P - IN_ROWS), (0, 0)))


def _pack_sv(parts):
    flat = []
    for n, size in SV_PARTS:
        v = parts[n].reshape(-1).astype(F32)
        flat.append(jnp.pad(v, (0, size - v.shape[0])))
    v = jnp.concatenate(flat)
    return jnp.pad(v, (0, SV_ROWS * 128 - v.shape[0])).reshape(SV_ROWS, 128)


def _sv_get(flat, n, size):
    return flat[SV_OFF[n]:SV_OFF[n] + size]


def kernel(x, c, w_ada, b_ada, norm_mix_w, w_in, conv_w, conv_b, dt_bias, a_log, d_skip, ssd_norm_w, w_branch_ssd, pool_w, pool_scale, w_branch_pool, w_out, norm_mlp_w, w_up, w_down, norm_final_w, loss_target, m_w_ada, m_b_ada, m_norm_mix_w, m_w_in, m_conv_w, m_conv_b, m_dt_bias, m_a_log, m_d_skip, m_ssd_norm_w, m_w_branch_ssd, m_pool_w, m_pool_scale, m_w_branch_pool, m_w_out, m_norm_mlp_w, m_w_up, m_w_down, m_norm_final_w, v_w_ada, v_b_ada, v_norm_mix_w, v_w_in, v_conv_w, v_conv_b, v_dt_bias, v_a_log, v_d_skip, v_ssd_norm_w, v_w_branch_ssd, v_pool_w, v_pool_scale, v_w_branch_pool, v_w_out, v_norm_mlp_w, v_w_up, v_w_down, v_norm_final_w):
    xs_ = x[0]
    tgt = loss_target[0]
    L = xs_.shape[0]
    me = 4 * lax.axis_index("x") + 2 * lax.axis_index("y") + lax.axis_index("c")
    wloc = w_ada.shape[2]

    mod_p, c_all = _ada_fwd(c, w_ada[0], b_ada.reshape(N_DEV, wloc), name="ada_fwd")
    mod = mod_p.reshape(6, D)
    shift_m, scale_m, gate_m, shift_f, scale_f, gate_f = [mod[i:i + 1] for i in range(6)]

    conv_bits = lax.bitcast_convert_type(conv_w[0], SLAB_DT).reshape(3, D)
    slab_in = jnp.concatenate([_pad_rows(w_in[0].T.astype(SLAB_DT), IN_ROWS_P),
                               _pad_rows(conv_bits, CONV_ROWS)], axis=0)
    slab_rest = jnp.concatenate([
        w_branch_ssd[0].astype(SLAB_DT),
        pool_w[0].reshape(32, D).astype(SLAB_DT),
        w_branch_pool[0].astype(SLAB_DT),
        w_out[0].astype(SLAB_DT),
        w_up[0].T.astype(SLAB_DT),
        w_down[0].astype(SLAB_DT)], axis=0)
    gs_in = _gather_slabs(slab_in, name="gather_w_in")
    slab_rest, gs_in = lax.optimization_barrier((slab_rest, gs_in))
    rest_started = _xchg_start(slab_rest, per_peer=False, name="gather_rest_start")
    gather_token = rest_started[4]

    w_in_t = _reorder_in_rows(gs_in)
    conv_full = lax.bitcast_convert_type(
        gs_in[:, IN_ROWS_P:IN_ROWS_P + 3].reshape(N_DEV, 4, XBC // N_DEV, 2), F32)
    conv_full = conv_full.transpose(1, 0, 2).reshape(4, XBC)

    dtb = jnp.pad(dt_bias, ((0, 0), (0, 128 - NH)))
    arow = jnp.pad(-jnp.exp(a_log), ((0, 0), (0, 128 - NH)))
    dsk_x = jnp.repeat(d_skip, HP, axis=1)

    tm = _pick(L, (1024, 512, 256, 128))
    tm2 = _pick(L, (2048, 1024, 512, 256, 128))
    tkl = _pick(L, (4096, 2048, 1024, 512, 256, 128))
    tkl2 = _pick(L, (2048, 1024, 512, 256, 128))

    tmh = _pick(L, (512, 256, 128))
    zcol = C_Z // DI
    gcol = C_GATE // (2 * D)

    def whole_rows(w):
        return lambda t: ((L, w), BF16, (t, w), lambda i, j, k: (i, 0))

    def norm1_pro(x_ref, ex, outs, j):
        @pl.when(j == 0)
        def _():
            xv = x_ref[...]
            r = lax.rsqrt(jnp.mean(xv * xv, axis=-1, keepdims=True) + EPS)
            outs[1][...] = (xv * r * ex[0][...] * (1.0 + ex[1][...]) + ex[2][...]).astype(outs[1].dtype)

        return outs[1][...]

    def proj_ep(acc, ex, outs):
        outs[0][...] = acc

        @pl.when(pl.program_id(1) == NPROJ // 768 - 1)
        def _():
            outs[2][...] = acc[:, 768 - DT_PAD:]

    proj, h1, dtp = _mm(
        xs_, w_in_t, "nt", name="in_proj", tm=tm2, tn=768, tk=D,
        extras=[(norm_mix_w, *_vecs()), (scale_m, *_vecs()), (shift_m, *_vecs()), _dep(gather_token)],
        outs=[F32, whole_rows(D)(tm2), ((L, DT_PAD), F32, (tm2, DT_PAD), lambda i, j, k: (i, 0))],
        prologue=norm1_pro, epilogue=proj_ep)
    xbc_raw = proj
    xbc = _conv_fwd(xbc_raw, conv_full, conv_b, name="conv_fwd")
    y_ssm, hs = _ssd_fwd(xbc, dtp, dtb, arow, dsk_x, name="ssd_fwd")

    slab_rest, gs = _xchg_wait(rest_started, y_ssm, per_peer=False, name="gather_rest_wait")
    gs = lax.dynamic_update_slice(gs, slab_rest[None], (me, 0, 0))

    def part(n, rows):
        return gs[:, REST_OFF[n]:REST_OFF[n] + rows]

    w_bssd = part("bssd", 256).reshape(DI, D)
    w_pool = part("pool", 32).reshape(N_DEV, 4, 32, PGW).transpose(1, 0, 2, 3).reshape(POOL_W, PGW)
    w_bpool = part("bpool", 128).reshape(POOL_W, D)
    w_o = part("out", 128).reshape(D, D)
    w_up_t = part("up", 512).reshape(DFF, D)
    w_dn = part("down", 512).reshape(DFF, D)

    def gnorm_pro(y_ref, ex, outs, j):
        z_ref, w_ref = ex
        yg = y_ref[...] * _silu(z_ref[...].astype(F32))
        segs = []
        for k in range(NG):
            sl = slice(k * GW, (k + 1) * GW)
            seg = yg[:, sl]
            r = lax.rsqrt(jnp.mean(seg * seg, axis=-1, keepdims=True) + EPS)
            segs.append((seg * r * w_ref[:, sl]).astype(BF16))
        yn_v = jnp.concatenate(segs, axis=1)
        outs[1][...] = yn_v
        return yn_v

    y_ssd, yn = _mm(y_ssm, w_bssd, "nn", name="branch_ssd", tm=tmh, tn=D, tk=DI,
                    extras=[(proj, *_rows(tmh, DI, zcol)), (ssd_norm_w, *_vecs(DI))],
                    outs=[F32, whole_rows(DI)(tmh)], prologue=gnorm_pro)
    pooled = _pool_fwd(proj, name="pool_fwd")
    yp0, yp1 = _mm_pool(pooled, w_pool, pool_scale, name="pool_mix", tm=tm, transpose_w=False)
    y_pool = _mm(yp1, w_bpool, "nn", name="branch_pool", outs=[F32], tm=tm2, tn=D, tk=D)

    def merge_pro(a_ref, ex, outs, j):
        s = _sigmoid(ex[1][...].astype(F32))
        mv = (s[:, :D] * a_ref[...] + s[:, D:] * ex[0][...]).astype(BF16)
        outs[3][...] = mv
        return mv

    mix, x1, h2, m = _mm(y_ssd, w_o, "nn", name="out_proj", tm=tmh, tn=D, tk=D,
                         extras=[(y_pool, *_rows(tmh)), (proj, *_rows(tmh, 2 * D, gcol)),
                                 (xs_, *_rows(tmh)), (gate_m, *_vecs()), (norm_mlp_w, *_vecs()),
                                 (scale_f, *_vecs()), (shift_f, *_vecs())],
                         outs=[BF16, F32, BF16, whole_rows(D)(tmh)], prologue=merge_pro,
                         epilogue=lambda acc, ex, outs: _ep_resid_norm(acc, ex[2:], outs[:3]))

    def relu2(acc, ex, outs):
        r = jnp.maximum(acc, 0.0)
        outs[0][...] = acc.astype(BF16)
        outs[1][...] = (r * r).astype(BF16)

    up, act = _mm(h2, w_up_t, "nt", name="mlp_up", outs=[BF16, BF16], tm=tm2, tn=1024, tk=D, epilogue=relu2)

    dx2, ddown, loss_p, dnwf, dgate_f = _mm(
        act, w_dn, "nn", name="mlp_down", tm=tmh, tn=D, tk=DFF,
        extras=[(x1, *_rows(tmh)), (tgt, *_rows(tmh)), (gate_f, *_vecs()), (norm_final_w.reshape(1, D), *_vecs())],
        outs=[F32, BF16, _sum_out(128), _sum_out(), _sum_out()], epilogue=_ep_final)

    def drelu2(acc, ex, outs):
        outs[0][...] = (acc * (2.0 * jnp.maximum(ex[0][...].astype(F32), 0.0))).astype(BF16)

    def dep_last(ep):
        return lambda acc, ex, outs: ep(acc, ex[:-1], outs)

    dup = _mm(ddown, w_dn, "nt", name="mlp_down_dx", outs=[BF16], tm=tm2, tn=1024, tk=D,
              extras=[(up, (tm2, 1024), lambda i, j, k: (i, j))], epilogue=drelu2)
    g_dn = _mm(act, ddown, "tn", name="mlp_down_dw", outs=[SLAB_DT], tm=1024, tn=D, tk=tkl)
    g_up_t = _mm(dup, h2, "tn", name="mlp_up_dw", outs=[SLAB_DT], tm=1024, tn=D, tk=tkl)
    gslab_mlp = jnp.concatenate([g_up_t.reshape(N_DEV, 512, D), g_dn.reshape(N_DEV, 512, D)], axis=1)
    mlp_started = _xchg_start(gslab_mlp, per_peer=True, name="scatter_mlp_start")
    dx1, p2, q2, dmix, dgate_m = _mm(
        dup, w_up_t, "nn", name="mlp_up_dx", tm=tmh, tn=D, tk=DFF,
        extras=[(x1, *_rows(tmh)), (dx2, *_rows(tmh)), (norm_mlp_w, *_vecs()), (scale_f, *_vecs()),
                (mix, *_rows(tmh)), (gate_m, *_vecs()), _dep(mlp_started[4])],
        outs=[F32, _sum_out(), _sum_out(), BF16, _sum_out()], epilogue=dep_last(_ep_norm_bwd))
    gcol = C_GATE // (2 * D)
    dy_ssd, dy_pool, dproj = _mm(
        dmix, w_o, "nt", name="out_proj_dx", tm=tmh, tn=D, tk=D,
        extras=[(y_ssd, *_rows(tmh)), (y_pool, *_rows(tmh)), (proj, *_rows(tmh, 2 * D, gcol))],
        outs=[BF16, BF16, ((L, NPROJ), BF16, *_rows(tmh, 2 * D, gcol))], epilogue=_ep_merge_bwd)
    g_o = _mm(m, dmix, "tn", name="out_proj_dw", outs=[SLAB_DT], tm=D, tn=D, tk=tkl)
    zcol = C_Z // DI
    dy_ssm, dproj, d_snw = _mm(
        dy_ssd, w_bssd, "nt", name="branch_ssd_dx", tm=tmh, tn=DI, tk=D,
        extras=[(y_ssm, *_rows(tmh, DI)), (proj, *_rows(tmh, DI, zcol)), (ssd_norm_w, *_vecs(DI)),
                (dproj, None, None)],
        outs=[F32, ((L, NPROJ), BF16, *_rows(tmh, DI, zcol)), _sum_out(DI)],
        epilogue=_ep_gated_norm_bwd, aliases={3: 1})
    g_bssd = _mm(yn, dy_ssd, "tn", name="branch_ssd_dw", outs=[SLAB_DT], tm=1024, tn=D, tk=tkl)
    dxbc, dproj, d_a, d_dx, d_dtb = _ssd_bwd(dy_ssm, xbc, dtp, hs, dtb, arow, dsk_x, dproj, name="ssd_bwd")
    dproj, d_cw, d_cb = _conv_bwd(xbc_raw, dxbc, conv_full, conv_b, dproj, name="conv_bwd")
    dyp0, d_ps = _mm(dy_pool, w_bpool, "nt", name="branch_pool_dx", tm=tm, tn=D, tk=D,
                     extras=[(yp0, *_rows(tm)), (pool_scale, *_vecs())],
                     outs=[BF16, _sum_out()], epilogue=_ep_pscale_bwd)
    g_bpool = _mm(yp1, dy_pool, "tn", name="branch_pool_dw", outs=[SLAB_DT], tm=D, tn=D, tk=tkl)
    dpooled = _mm_pool(dyp0, w_pool, None, name="pool_mix_dx", tm=tm, transpose_w=True)
    g_pool = _mm_pool_tn(pooled, dyp0, name="pool_mix_dw", tk=tkl)
    gslab_mix = jnp.concatenate([
        g_bssd.reshape(N_DEV, 256, D),
        g_pool.reshape(4, N_DEV, 32, PGW).transpose(1, 0, 2, 3).reshape(N_DEV, 32, D).astype(SLAB_DT),
        g_bpool.reshape(N_DEV, 128, D),
        g_o.reshape(N_DEV, 128, D)], axis=1)
    mix_started = _xchg_start(gslab_mix, per_peer=True, name="scatter_mix_start")
    dproj = _pool_bwd(dpooled, dproj, name="pool_bwd")
    g_in_t = _mm(dproj, h1, "tn", name="in_proj_dw", outs=[SLAB_DT], tm=1408, tn=D, tk=tkl2,
                 extras=[_dep(mix_started[4])])
    gslab_in = _restore_in_shards(g_in_t)
    in_started = _xchg_start(gslab_in, per_peer=True, name="scatter_in_start")
    grad_x, p1, q1 = _mm(
        dproj, w_in_t, "nn", name="in_proj_dx", tm=tm, tn=D, tk=1408,
        extras=[(xs_, *_rows(tm)), (dx1, *_rows(tm)), (norm_mix_w, *_vecs()), (scale_m, *_vecs()),
                _dep(in_started[4])],
        outs=[F32, _sum_out(), _sum_out()], epilogue=dep_last(_ep_norm_bwd))

    def landed(started, after, tile, name):
        src, land = _xchg_wait(started, after, per_peer=True, name=name + "_wait")
        own = lax.dynamic_slice_in_dim(src, me, 1, axis=0)
        return _slab_sum(lax.dynamic_update_slice(land, own, (me, 0, 0)), tile=tile, name=name + "_sum")

    gsum_mlp = landed(mlp_started, grad_x, 256, "scatter_mlp")
    gsum_mix = landed(mix_started, grad_x, 272, "scatter_mix")
    gsum_in = landed(in_started, grad_x, 208, "scatter_in")

    dmod = jnp.concatenate([q1, p1 * norm_mix_w, dgate_m, q2, p2 * norm_mlp_w, dgate_f], axis=1)
    d_alog = d_a[:, :NH] * (-jnp.exp(a_log))
    sv = _pack_sv({
        "b_ada": dmod, "norm_mix_w": p1 * (1.0 + scale_m), "conv_b": d_cb, "dt_bias": d_dtb[:, :NH],
        "a_log": d_alog, "d_skip": d_dx.reshape(NH, HP).sum(axis=1), "ssd_norm_w": d_snw,
        "pool_scale": d_ps, "norm_mlp_w": p2 * (1.0 + scale_f), "norm_final_w": dnwf, "conv_w": d_cw,
        "loss": loss_p[:, :1]})
    sv_all, sv_sum = _small_allsum(sv, name="small_allsum")
    flat = sv_sum.reshape(-1)
    loss = flat[SV_OFF["loss"]]
    dmod_all = sv_all.reshape(N_DEV, SV_ROWS * 128)[:, :6 * D]
    g_w_ada = _ada_bwd(c_all, lax.dynamic_slice_in_dim(dmod_all, me * wloc, wloc, axis=1), name="ada_bwd")

    g_conv_w = lax.dynamic_slice_in_dim(_sv_get(flat, "conv_w", 4 * XBC).reshape(4, XBC),
                                        me * (XBC // N_DEV), XBC // N_DEV, axis=1)
    small = {
        "b_ada": (b_ada, m_b_ada, v_b_ada, _sv_get(flat, "b_ada", 6 * D)),
        "norm_mix_w": (norm_mix_w, m_norm_mix_w, v_norm_mix_w, _sv_get(flat, "norm_mix_w", D)),
        "conv_b": (conv_b, m_conv_b, v_conv_b, _sv_get(flat, "conv_b", XBC)),
        "dt_bias": (dt_bias, m_dt_bias, v_dt_bias, _sv_get(flat, "dt_bias", NH)),
        "a_log": (a_log, m_a_log, v_a_log, _sv_get(flat, "a_log", NH)),
        "d_skip": (d_skip, m_d_skip, v_d_skip, _sv_get(flat, "d_skip", NH)),
        "ssd_norm_w": (ssd_norm_w, m_ssd_norm_w, v_ssd_norm_w, _sv_get(flat, "ssd_norm_w", DI)),
        "pool_scale": (pool_scale, m_pool_scale, v_pool_scale, _sv_get(flat, "pool_scale", POOL_W)),
        "norm_mlp_w": (norm_mlp_w, m_norm_mlp_w, v_norm_mlp_w, _sv_get(flat, "norm_mlp_w", D)),
        "norm_final_w": (norm_final_w, m_norm_final_w, v_norm_final_w, _sv_get(flat, "norm_final_w", D)),
        "conv_w": (conv_w, m_conv_w, v_conv_w, g_conv_w),
    }
    names = list(small)
    sizes = [int(np.prod(small[n][0].shape)) for n in names]
    tot = sum(sizes)
    rows = -(-tot // 1024) * 8

    def pack(idx):
        v = jnp.concatenate([small[n][idx].reshape(-1) for n in names])
        return jnp.pad(v, (0, rows * 128 - tot)).reshape(rows, 128)

    sd, sm, sv2 = _adamw(pack(0), pack(3), pack(1), pack(2), name="adamw_small")
    small_out = {}
    off = 0
    for n, sz in zip(names, sizes):
        shp = small[n][0].shape
        small_out[n] = (small[n][3].reshape(shp), sd.reshape(-1)[off:off + sz].reshape(shp),
                        sm.reshape(-1)[off:off + sz].reshape(shp), sv2.reshape(-1)[off:off + sz].reshape(shp))
        off += sz

    def gpart(n, rows_):
        return gsum_mix[MIX_OFF[n]:MIX_OFF[n] + rows_]

    def lin(a):
        return a[0].T.reshape(IN_ROWS * 8, 128)

    g_lin = gsum_in[:IN_ROWS].reshape(IN_ROWS * 8, 128)
    dlt, mn, vn = _adamw(lin(w_in), g_lin, lin(m_w_in), lin(v_w_in), name="adamw_w_in", tr=IN_ROWS * 2)
    big_in = tuple(a.reshape(IN_ROWS, D).T[None] for a in (g_lin, dlt, mn, vn))

    big = {
        "w_ada": (w_ada, m_w_ada, v_w_ada, g_w_ada, (D, wloc)),
        "w_branch_ssd": (w_branch_ssd, m_w_branch_ssd, v_w_branch_ssd, gpart("bssd", 256), (256, D)),
        "pool_w": (pool_w, m_pool_w, v_pool_w, gpart("pool", 32).reshape(128, PGW), (128, PGW)),
        "w_branch_pool": (w_branch_pool, m_w_branch_pool, v_w_branch_pool, gpart("bpool", 128), (128, D)),
        "w_out": (w_out, m_w_out, v_w_out, gpart("out", 128), (128, D)),
        "w_up": (w_up, m_w_up, v_w_up, gsum_mlp[:512].T, (D, 512)),
        "w_down": (w_down, m_w_down, v_w_down, gsum_mlp[512:], (512, D)),
    }
    big_out = {}
    for n, (w, mm_, vv, g, shp2) in big.items():
        dlt, mn, vn = _adamw(w.reshape(shp2), g, mm_.reshape(shp2), vv.reshape(shp2), name="adamw_" + n)
        big_out[n] = (g.reshape(w.shape), dlt.reshape(w.shape), mn.reshape(w.shape), vn.reshape(w.shape))

    order = ["w_ada", "b_ada", "norm_mix_w", "w_in", "conv_w", "conv_b", "dt_bias", "a_log", "d_skip",
             "ssd_norm_w", "w_branch_ssd", "pool_w", "pool_scale", "w_branch_pool", "w_out", "norm_mlp_w",
             "w_up", "w_down", "norm_final_w"]
    big_out["w_in"] = big_in
    res = {**small_out, **big_out}
    outs = [loss, grad_x.reshape(x.shape)]
    for k in range(4):
        outs += [res[n][k] for n in order]
    return tuple(outs)
```

```python
import functools

import numpy as np
import jax
import jax.numpy as jnp
from jax import lax
from jax.experimental import pallas as pl
from jax.experimental.pallas import tpu as pltpu

F32 = jnp.float32
BF16 = jnp.bfloat16
SLAB_DT = jnp.bfloat16
_MXU_DTYPE = jnp.bfloat16

N_DEV = 8
D = 1024
DI = 2048
NH = 32
HP = 64
NG = 4
NS = 128
Q = 128
XBC = DI + 2 * NG * NS
DFF = 4096
N_IN = 8224
EPS = 1e-5
POOL_W = 1024
PGW = 256

C_XBC, C_POOL, C_Z, C_GATE, C_DT = 0, 3072, 4096, 6144, 8192
DT_PAD = 256
NPROJ = C_DT + DT_PAD

IN_ROWS = N_IN // N_DEV
IN_ROWS_P = 1040
CONV_ROWS = 16
REST_PARTS = (("bssd", 256), ("pool", 32), ("bpool", 128), ("out", 128), ("up", 512), ("down", 512))
REST_OFF = {}
_o = 0
for _n, _r in REST_PARTS:
    REST_OFF[_n] = _o
    _o += _r
REST_ROWS = _o
MIX_PARTS = (("bssd", 256), ("pool", 32), ("bpool", 128), ("out", 128))
MIX_OFF = {}
_o = 0
for _n, _r in MIX_PARTS:
    MIX_OFF[_n] = _o
    _o += _r
MIX_ROWS = _o

SV_PARTS = (("b_ada", 6144), ("norm_mix_w", 1024), ("conv_b", 3072), ("dt_bias", 128), ("a_log", 128),
            ("d_skip", 128), ("ssd_norm_w", 2048), ("pool_scale", 1024), ("norm_mlp_w", 1024),
            ("norm_final_w", 1024), ("conv_w", 4 * XBC), ("loss", 128))
SV_OFF = {}
_o = 0
for _n, _r in SV_PARTS:
    SV_OFF[_n] = _o
    _o += _r
SV_ROWS = 224
assert _o <= SV_ROWS * 128

ADAM_LR, ADAM_B1, ADAM_B2, ADAM_EPS, ADAM_WD, ADAM_STEP = 0.001, 0.9, 0.999, 1e-08, 0.01, 10

VMEM_BIG = 56 * 1024 * 1024
NEG = -1e30

NN = ((1,), (0,))
NT = ((1,), (1,))
TN = ((0,), (0,))


def _dot(a, b, dims=NN):
    return lax.dot_general(a.astype(_MXU_DTYPE), b.astype(_MXU_DTYPE), (dims, ((), ())),
                           preferred_element_type=F32)


def _dot_hi(a, b, dims=NN):
    return lax.dot_general(a.astype(F32), b.astype(F32), (dims, ((), ())),
                           precision=lax.Precision.HIGHEST, preferred_element_type=F32)


def _pick(n, cands):
    for c in cands:
        if n % c == 0:
            return c
    return n


def _sigmoid(x):
    return 1.0 / (1.0 + jnp.exp(-x))


def _silu(x):
    return x * _sigmoid(x)


def _dsilu(x):
    s = _sigmoid(x)
    return s * (1.0 + x * (1.0 - s))


def _softplus(x):
    return jnp.maximum(x, 0.0) + jnp.log(1.0 + jnp.exp(-jnp.abs(x)))


def _params(sem, vmem=None):
    return pltpu.CompilerParams(dimension_semantics=sem, vmem_limit_bytes=vmem)


def _mm(a, b, mode, *, name, outs, tm, tn, tk, extras=(), epilogue=None, aliases=None, prologue=None):
    if mode == "tn":
        K, M = a.shape
        N = b.shape[1]
        a_spec = pl.BlockSpec((tk, tm), lambda i, j, k: (k, i))
        b_spec = pl.BlockSpec((tk, tn), lambda i, j, k: (k, j))
        dims = TN
    else:
        M = a.shape[0]
        K = b.shape[0] if mode == "nn" else b.shape[1]
        if prologue is None:
            assert a.shape[1] == K
            a_spec = pl.BlockSpec((tm, tk), lambda i, j, k: (i, k))
        else:
            assert tk == K
            a_spec = pl.BlockSpec((tm, a.shape[1]), lambda i, j, k: (i, 0))
        if mode == "nn":
            N = b.shape[1]
            b_spec = pl.BlockSpec((tk, tn), lambda i, j, k: (k, j))
            dims = NN
        else:
            N = b.shape[0]
            b_spec = pl.BlockSpec((tn, tk), lambda i, j, k: (j, k))
            dims = NT
    assert M % tm == 0 and N % tn == 0 and K % tk == 0, (name, M, N, K, tm, tn, tk)
    nk = K // tk
    ne, no = len(extras), len(outs)
    if epilogue is None:
        def epilogue(acc, ex, out_refs):
            out_refs[0][...] = acc.astype(out_refs[0].dtype)

    def body(a_ref, b_ref, *rest):
        ex, out_refs = rest[:ne], rest[ne:ne + no]
        lhs = a_ref[...] if prologue is None else prologue(a_ref, ex, out_refs, pl.program_id(1))
        p = _dot(lhs, b_ref[...], dims)
        if nk == 1:
            epilogue(p, ex, out_refs)
        else:
            acc = rest[-1]
            k = pl.program_id(2)

            @pl.when(k == 0)
            def _():
                acc[...] = p

            @pl.when(jnp.logical_and(k > 0, k < nk - 1))
            def _():
                acc[...] += p

            @pl.when(k == nk - 1)
            def _():
                epilogue(acc[...] + p, ex, out_refs)

    out_specs, out_shape = [], []
    for o in outs:
        if isinstance(o, tuple):
            shape, dt, bs, im = o
            out_specs.append(pl.BlockSpec(bs, im))
            out_shape.append(jax.ShapeDtypeStruct(shape, dt))
        else:
            out_specs.append(pl.BlockSpec((tm, tn), lambda i, j, k: (i, j)))
            out_shape.append(jax.ShapeDtypeStruct((M, N), o))
    in_specs = [a_spec, b_spec]
    for _, bs, im in extras:
        in_specs.append(pl.BlockSpec(memory_space=pl.ANY) if bs is None else pl.BlockSpec(bs, im))
    res = pl.pallas_call(
        body, name=name,
        grid=(M // tm, N // tn, nk),
        in_specs=in_specs, out_specs=out_specs, out_shape=out_shape,
        scratch_shapes=[pltpu.VMEM((tm, tn), F32)] if nk > 1 else [],
        input_output_aliases={2 + e: o for e, o in (aliases or {}).items()},
        compiler_params=_params(("arbitrary", "arbitrary", "arbitrary"), VMEM_BIG),
    )(a, b, *[e[0] for e in extras])
    return res if no > 1 else res[0]


def _rows(tm, w=D, col=0):
    return (tm, w), lambda i, j, k, c=col: (i, c)


def _vecs(w=D, col=0):
    return (1, w), lambda i, j, k, c=col: (0, c)


def _sum_out(w=D):
    return ((1, w), F32, (1, w), lambda i, j, k: (0, 0))


def _mm_pool(a, w, scale, *, name, tm, transpose_w):
    L = a.shape[0]
    dims = NT if transpose_w else NN

    def body(a_ref, w_ref, *rest):
        p = _dot(a_ref[...], w_ref[...], dims)
        if transpose_w:
            rest[0][...] = p
        else:
            s_ref, o0, o1 = rest
            o0[...] = p.astype(o0.dtype)
            o1[...] = (p * s_ref[...]).astype(o1.dtype)

    blk = pl.BlockSpec((tm, PGW), lambda i, j: (i, j))
    in_specs = [blk, pl.BlockSpec((PGW, PGW), lambda i, j: (j, 0))]
    args = [a, w]
    if transpose_w:
        out_specs, out_shape = [blk], [jax.ShapeDtypeStruct((L, POOL_W), F32)]
    else:
        in_specs.append(pl.BlockSpec((1, PGW), lambda i, j: (0, j)))
        args.append(scale)
        out_specs = [blk, blk]
        out_shape = [jax.ShapeDtypeStruct((L, POOL_W), BF16), jax.ShapeDtypeStruct((L, POOL_W), BF16)]
    res = pl.pallas_call(body, name=name, grid=(L // tm, 4), in_specs=in_specs, out_specs=out_specs,
                         out_shape=out_shape, compiler_params=_params(("parallel", "parallel")))(*args)
    return res[0] if transpose_w else res


def _mm_pool_tn(a, b, *, name, tk):
    L = a.shape[0]

    def body(a_ref, b_ref, o_ref):
        p = _dot(a_ref[...], b_ref[...], TN)

        @pl.when(pl.program_id(1) == 0)
        def _():
            o_ref[...] = p

        @pl.when(pl.program_id(1) > 0)
        def _():
            o_ref[...] += p

    blk = pl.BlockSpec((tk, PGW), lambda g, k: (k, g))
    return pl.pallas_call(body, name=name, grid=(4, L // tk), in_specs=[blk, blk],
                          out_specs=pl.BlockSpec((PGW, PGW), lambda g, k: (g, 0)),
                          out_shape=jax.ShapeDtypeStruct((POOL_W, PGW), F32),
                          compiler_params=_params(("parallel", "arbitrary")))(a, b)


def _acc_out(ref, val, i):
    @pl.when(i == 0)
    def _():
        ref[...] = val

    @pl.when(i > 0)
    def _():
        ref[...] += val


def _colsum(v):
    return jnp.sum(v, axis=0, keepdims=True)


def _ep_resid_norm(acc, ex, outs):
    x_ref, g_ref, nw_ref, sc_ref, sh_ref = ex
    mix_ref, x1_ref, h_ref = outs
    mix_ref[...] = acc.astype(mix_ref.dtype)
    xv = x_ref[...] + g_ref[...] * acc
    x1_ref[...] = xv
    r = lax.rsqrt(jnp.mean(xv * xv, axis=-1, keepdims=True) + EPS)
    h_ref[...] = (xv * r * nw_ref[...] * (1.0 + sc_ref[...]) + sh_ref[...]).astype(h_ref.dtype)


def _ep_final(acc, ex, outs):
    x1_ref, t_ref, g_ref, nw_ref = ex
    dx2_ref, dd_ref, loss_ref, dnw_ref, dg_ref = outs
    i = pl.program_id(0)
    x2 = x1_ref[...] + g_ref[...] * acc
    r = lax.rsqrt(jnp.mean(x2 * x2, axis=-1, keepdims=True) + EPS)
    xh = x2 * r
    e = xh * nw_ref[...] - t_ref[...]
    part = 0.5 * jnp.sum(jnp.mean(e * e, axis=-1, keepdims=True), axis=0, keepdims=True)
    dy = e * (1.0 / D)
    g = dy * nw_ref[...]
    dx2 = r * (g - xh * jnp.mean(g * xh, axis=-1, keepdims=True))
    dx2_ref[...] = dx2
    dd_ref[...] = (dx2 * g_ref[...]).astype(dd_ref.dtype)
    _acc_out(loss_ref, jnp.broadcast_to(part, (1, 128)), i)
    _acc_out(dnw_ref, _colsum(dy * xh), i)
    _acc_out(dg_ref, _colsum(dx2 * acc), i)


def _ep_norm_bwd(acc, ex, outs):
    x_ref, dr_ref, nw_ref, sc_ref = ex[:4]
    dx_ref, p_ref, q_ref = outs[:3]
    i = pl.program_id(0)
    xv = x_ref[...]
    r = lax.rsqrt(jnp.mean(xv * xv, axis=-1, keepdims=True) + EPS)
    xh = xv * r
    g = acc * (nw_ref[...] * (1.0 + sc_ref[...]))
    dx = dr_ref[...] + r * (g - xh * jnp.mean(g * xh, axis=-1, keepdims=True))
    dx_ref[...] = dx
    _acc_out(p_ref, _colsum(acc * xh), i)
    _acc_out(q_ref, _colsum(acc), i)
    if len(ex) > 4:
        m_ref, g_ref = ex[4:]
        dm_ref, dg_ref = outs[3:]
        dm_ref[...] = (dx * g_ref[...]).astype(dm_ref.dtype)
        _acc_out(dg_ref, _colsum(dx * m_ref[...].astype(F32)), i)


def _ep_merge_bwd(acc, ex, outs):
    a_ref, b_ref, gl_ref = ex
    da_ref, db_ref, dgl_ref = outs
    s = _sigmoid(gl_ref[...].astype(F32))
    s1, s2 = s[:, :D], s[:, D:]
    da_ref[...] = (acc * s1).astype(da_ref.dtype)
    db_ref[...] = (acc * s2).astype(db_ref.dtype)
    dgl_ref[:, :D] = (acc * a_ref[...] * s1 * (1.0 - s1)).astype(dgl_ref.dtype)
    dgl_ref[:, D:] = (acc * b_ref[...] * s2 * (1.0 - s2)).astype(dgl_ref.dtype)


def _ep_pscale_bwd(acc, ex, outs):
    y_ref, s_ref = ex
    o_ref, ds_ref = outs
    o_ref[...] = (acc * s_ref[...]).astype(o_ref.dtype)
    _acc_out(ds_ref, _colsum(acc * y_ref[...].astype(F32)), pl.program_id(0))


GW = DI // NG


def _ep_gated_norm_bwd(acc, ex, outs):
    y_ref, z_ref, w_ref, _ = ex
    dy_ref, dz_ref, dw_ref = outs
    zv = z_ref[...].astype(F32)
    yv = y_ref[...]
    sz = _silu(zv)
    yg = yv * sz
    dsz = _dsilu(zv)
    dws = []
    for k in range(NG):
        sl = slice(k * GW, (k + 1) * GW)
        seg = yg[:, sl]
        r = lax.rsqrt(jnp.mean(seg * seg, axis=-1, keepdims=True) + EPS)
        sh = seg * r
        dn = acc[:, sl]
        g = dn * w_ref[:, sl]
        dyg = r * (g - sh * jnp.mean(g * sh, axis=-1, keepdims=True))
        dy_ref[:, sl] = dyg * sz[:, sl]
        dz_ref[:, sl] = (dyg * yv[:, sl] * dsz[:, sl]).astype(dz_ref.dtype)
        dws.append(_colsum(dn * sh))
    _acc_out(dw_ref, jnp.concatenate(dws, axis=1), pl.program_id(0))


CONV_CB = 128
HALO = 16


def _time_chunk(L):
    return _pick(L, (256, 128))


def _with_halo(x_ref, i, r0, rc):
    p0 = pl.multiple_of(jnp.maximum(r0 - HALO, 0), HALO)
    prev = jnp.where(i > 0, x_ref[pl.ds(p0, HALO), :].astype(F32), 0.0)
    return jnp.concatenate([prev, x_ref[pl.ds(r0, rc), :].astype(F32)], axis=0)


def _conv_fwd(proj, w, b, *, name):
    L = proj.shape[0]
    rc = _time_chunk(L)
    n = L // rc

    def body(x_ref, w_ref, b_ref, o_ref):
        wv = w_ref[...]
        bv = b_ref[...]

        def step(i, c):
            r0 = pl.multiple_of(i * rc, rc)
            ext = _with_halo(x_ref, i, r0, rc)
            acc = bv + ext * wv[3:4]
            for j in (1, 2, 3):
                acc = acc + pltpu.roll(ext, j, 0) * wv[3 - j:4 - j]
            acc = acc[HALO:]
            o_ref[pl.ds(r0, rc), :] = acc * _sigmoid(acc)
            return c

        lax.fori_loop(0, n, step, 0)

    return pl.pallas_call(
        body, name=name, grid=(XBC // CONV_CB,),
        in_specs=[pl.BlockSpec((L, CONV_CB), lambda j: (0, j + C_XBC // CONV_CB)),
                  pl.BlockSpec((4, CONV_CB), lambda j: (0, j)), pl.BlockSpec((1, CONV_CB), lambda j: (0, j))],
        out_specs=pl.BlockSpec((L, CONV_CB), lambda j: (0, j)),
        out_shape=jax.ShapeDtypeStruct((L, XBC), F32),
        compiler_params=_params(("parallel",), VMEM_BIG))(proj, w, b)


def _conv_bwd(proj, dy, w, b, dproj, *, name):
    L = proj.shape[0]
    rc = _time_chunk(L)
    n = L // rc

    def body(x_ref, dy_ref, w_ref, b_ref, dp_in, dx_ref, dw_ref, db_ref):
        del dp_in
        wv = w_ref[...]
        bv = b_ref[...]

        def step(k, carry):
            nxt, db, d0, d1, d2, d3 = carry
            i = n - 1 - k
            r0 = pl.multiple_of(i * rc, rc)
            ext = _with_halo(x_ref, i, r0, rc)
            xk = [ext[HALO:]] + [pltpu.roll(ext, j, 0)[HALO:] for j in (1, 2, 3)]
            pre = bv
            for j in range(4):
                pre = pre + xk[j] * wv[3 - j:4 - j]
            dpre = dy_ref[pl.ds(r0, rc), :] * _dsilu(pre)
            dext = jnp.concatenate([dpre, nxt], axis=0)
            acc = dext * wv[3:4]
            for j in (1, 2, 3):
                acc = acc + pltpu.roll(dext, rc + HALO - j, 0) * wv[3 - j:4 - j]
            dx_ref[pl.ds(r0, rc), :] = acc[:rc].astype(dx_ref.dtype)
            return (dpre[:HALO], db + _colsum(dpre), d0 + _colsum(dpre * xk[3]), d1 + _colsum(dpre * xk[2]),
                    d2 + _colsum(dpre * xk[1]), d3 + _colsum(dpre * xk[0]))

        z = jnp.zeros((1, CONV_CB), F32)
        _, db, d0, d1, d2, d3 = lax.fori_loop(0, n, step, (jnp.zeros((HALO, CONV_CB), F32), z, z, z, z, z))
        db_ref[...] = db
        dw_ref[...] = jnp.concatenate([d0, d1, d2, d3], axis=0)

    nb = XBC // CONV_CB
    return pl.pallas_call(
        body, name=name, grid=(nb,),
        in_specs=[pl.BlockSpec((L, CONV_CB), lambda j: (0, j + C_XBC // CONV_CB)),
                  pl.BlockSpec((L, CONV_CB), lambda j: (0, j)),
                  pl.BlockSpec((4, CONV_CB), lambda j: (0, j)), pl.BlockSpec((1, CONV_CB), lambda j: (0, j)),
                  pl.BlockSpec(memory_space=pl.ANY)],
        out_specs=[pl.BlockSpec((L, CONV_CB), lambda j: (0, j + C_XBC // CONV_CB)),
                   pl.BlockSpec((4, CONV_CB), lambda j: (0, j)), pl.BlockSpec((1, CONV_CB), lambda j: (0, j))],
        out_shape=[jax.ShapeDtypeStruct((L, NPROJ), BF16), jax.ShapeDtypeStruct((4, XBC), F32),
                   jax.ShapeDtypeStruct((1, XBC), F32)],
        input_output_aliases={4: 0},
        compiler_params=_params(("parallel",), VMEM_BIG))(proj, dy, w, b, dproj)


def _pool_fwd(proj, *, name):
    L = proj.shape[0]
    rc = _time_chunk(L)
    n = L // rc

    def body(x_ref, o_ref, pad):
        g = pl.program_id(0)
        pad[0:HALO, :] = jnp.zeros((HALO, PGW), F32)

        def fill(i, c):
            r0 = pl.multiple_of(i * rc, rc)
            pad[pl.ds(r0 + HALO, rc), :] = x_ref[pl.ds(r0, rc), :].astype(F32)
            return c

        lax.fori_loop(0, n, fill, 0)
        rows = lax.broadcasted_iota(jnp.int32, (rc, PGW), 0)

        for gi in range(4):
            win = 2 << gi

            @pl.when(g == gi)
            def _(gi=gi, win=win):
                def step(i, c):
                    r0 = pl.multiple_of(i * rc, rc)
                    ext = pad[pl.ds(r0, rc + HALO), :]
                    s = ext
                    sh = 1
                    while sh < win:
                        s = s + pltpu.roll(s, sh, 0)
                        sh *= 2
                    cnt = jnp.minimum(rows + (r0 + 1), win).astype(F32)
                    o_ref[pl.ds(r0, rc), :] = (s[HALO:] / cnt - ext[HALO:]).astype(o_ref.dtype)
                    return c

                lax.fori_loop(0, n, step, 0)

    return pl.pallas_call(
        body, name=name, grid=(4,),
        in_specs=[pl.BlockSpec((L, PGW), lambda j: (0, j + C_POOL // PGW))],
        out_specs=pl.BlockSpec((L, PGW), lambda j: (0, j)),
        out_shape=jax.ShapeDtypeStruct((L, POOL_W), BF16),
        scratch_shapes=[pltpu.VMEM((L + HALO, PGW), F32)],
        compiler_params=_params(("parallel",), VMEM_BIG))(proj)


def _pool_bwd(dpooled, dproj, *, name):
    L = dpooled.shape[0]
    rc = _time_chunk(L)
    n = L // rc

    def body(d_ref, dp_in, o_ref, pad):
        del dp_in
        g = pl.program_id(0)
        pad[L:L + HALO, :] = jnp.zeros((HALO, PGW), F32)
        rows = lax.broadcasted_iota(jnp.int32, (rc, PGW), 0)

        for gi in range(4):
            win = 2 << gi

            @pl.when(g == gi)
            def _(gi=gi, win=win):
                def fill(i, c):
                    r0 = pl.multiple_of(i * rc, rc)
                    cnt = jnp.minimum(rows + (r0 + 1), win).astype(F32)
                    pad[pl.ds(r0, rc), :] = d_ref[pl.ds(r0, rc), :] / cnt
                    return c

                lax.fori_loop(0, n, fill, 0)

                def step(i, c):
                    r0 = pl.multiple_of(i * rc, rc)
                    s = pad[pl.ds(r0, rc + HALO), :]
                    sh = 1
                    while sh < win:
                        s = s + pltpu.roll(s, rc + HALO - sh, 0)
                        sh *= 2
                    o_ref[pl.ds(r0, rc), :] = (s[:rc] - d_ref[pl.ds(r0, rc), :]).astype(o_ref.dtype)
                    return c

                lax.fori_loop(0, n, step, 0)

    return pl.pallas_call(
        body, name=name, grid=(4,),
        in_specs=[pl.BlockSpec((L, PGW), lambda j: (0, j)), pl.BlockSpec(memory_space=pl.ANY)],
        out_specs=pl.BlockSpec((L, PGW), lambda j: (0, j + C_POOL // PGW)),
        out_shape=jax.ShapeDtypeStruct((L, NPROJ), BF16),
        scratch_shapes=[pltpu.VMEM((L + HALO, PGW), F32)],
        input_output_aliases={1: 0},
        compiler_params=_params(("parallel",), VMEM_BIG))(dpooled, dproj)


_SPLIT_DT = jnp.bfloat16


def _ssd_consts():
    tri = np.tril(np.ones((Q, Q), np.float32))
    exp = np.zeros((128, DI), np.float32)
    for h in range(NH):
        exp[h, h * HP:(h + 1) * HP] = 1.0
    exp2 = np.concatenate([exp, exp], axis=0)
    return (jnp.asarray(tri, dtype=_SPLIT_DT), jnp.asarray(tri.T.copy(), dtype=_SPLIT_DT),
            jnp.asarray(exp2, dtype=_SPLIT_DT))


def _split(v, n):
    parts, r = [], v
    for _ in range(n):
        p = r.astype(_SPLIT_DT)
        parts.append(p)
        r = r - p.astype(F32)
    return parts


def _bdot(a, b, dims):
    return lax.dot_general(a, b, (dims, ((), ())), preferred_element_type=F32)


def _tri_sum(t_ref, v):
    r = _bdot(t_ref[...], jnp.concatenate(_split(v, 3), axis=1), NN)
    return r[:, :128] + r[:, 128:256] + r[:, 256:]


def _expand(v, e2_ref):
    return _bdot(jnp.concatenate(_split(v, 2), axis=1), e2_ref[...], NN)


def _reduce_heads(vals, eg):
    parts = []
    for v in vals:
        parts += _split(v, 2)
    r = _bdot(jnp.concatenate(parts, axis=0), eg, NT)
    return [r[2 * i * Q:(2 * i + 1) * Q] + r[(2 * i + 1) * Q:(2 * i + 2) * Q] for i in range(len(vals))]


def _ssd_common(xbc_ref, dtw_ref, dtb_ref, arow_ref, t_ref, e_ref):
    pre = dtw_ref[:, :128] + dtb_ref[...]
    dt = _softplus(pre)
    acs = _tri_sum(t_ref, dt * arow_ref[...])
    acs_x = _expand(acs, e_ref)
    dt_x = _expand(dt, e_ref)
    xs = xbc_ref[:, 0:DI]
    return pre, dt, acs, acs.T, acs_x, dt_x, xs


def _ssd_fwd(xbc, proj, dtb, arow, dsk_x, *, name):
    L = xbc.shape[0]
    nc = L // Q
    tri, _, expand = _ssd_consts()

    def body(xbc_ref, dtw_ref, dtb_ref, arow_ref, dsk_ref, t_ref, e_ref, y_ref, hs_ref, h_scr):
        @pl.when(pl.program_id(0) == 0)
        def _():
            h_scr[...] = jnp.zeros_like(h_scr)

        _, dt, acs, acs_t, acs_x, dt_x, xs = _ssd_common(xbc_ref, dtw_ref, dtb_ref, arow_ref, t_ref, e_ref)
        xdt = xs * dt_x
        eacs = jnp.exp(acs_x)
        acs_last = acs_x[Q - 1:Q, :]
        dec = jnp.exp(acs_last - acs_x)
        hs_ref[0] = h_scr[...].astype(hs_ref.dtype)
        causal = lax.broadcasted_iota(jnp.int32, (Q, Q), 0) >= lax.broadcasted_iota(jnp.int32, (Q, Q), 1)
        first = lax.broadcasted_iota(jnp.int32, (Q, 128), 1) < HP
        for g in range(NG):
            bg = xbc_ref[:, DI + g * NS:DI + (g + 1) * NS]
            cg = xbc_ref[:, DI + NG * NS + g * NS:DI + NG * NS + (g + 1) * NS]
            s = _dot(cg, bg, NT)
            sl = slice(g * GW, (g + 1) * GW)
            hg = h_scr[:, sl]
            yoff = _dot(cg, hg, NN) * eacs[:, sl]
            st = _dot(bg, xdt[:, sl] * dec[:, sl], TN)
            h_scr[:, sl] = hg * eacs[Q - 1:Q, sl] + st
            for j in range(4):
                lo = g * GW + j * 128
                xb = xdt[:, lo:lo + 128]
                yp = yoff[:, j * 128:(j + 1) * 128] + dsk_ref[:, lo:lo + 128] * xs[:, lo:lo + 128]
                for e in range(2):
                    h = g * 8 + j * 2 + e
                    lm = jnp.exp(jnp.where(causal, acs[:, h:h + 1] - acs_t[h:h + 1, :], NEG))
                    xm = jnp.where(first if e == 0 else jnp.logical_not(first), xb, 0.0)
                    yp = yp + _dot(s * lm, xm, NN)
                y_ref[:, lo:lo + 128] = yp

    return pl.pallas_call(
        body, name=name, grid=(nc,),
        in_specs=[pl.BlockSpec((Q, XBC), lambda c: (c, 0)),
                  pl.BlockSpec((Q, DT_PAD), lambda c: (c, 0)),
                  pl.BlockSpec((1, 128), lambda c: (0, 0)), pl.BlockSpec((1, 128), lambda c: (0, 0)),
                  pl.BlockSpec((1, DI), lambda c: (0, 0)),
                  pl.BlockSpec((Q, Q), lambda c: (0, 0)), pl.BlockSpec((256, DI), lambda c: (0, 0))],
        out_specs=[pl.BlockSpec((Q, DI), lambda c: (c, 0)), pl.BlockSpec((1, NS, DI), lambda c: (c, 0, 0))],
        out_shape=[jax.ShapeDtypeStruct((L, DI), F32), jax.ShapeDtypeStruct((nc, NS, DI), F32)],
        scratch_shapes=[pltpu.VMEM((NS, DI), F32)],
        compiler_params=_params(("arbitrary",), VMEM_BIG))(xbc, proj, dtb, arow, dsk_x, tri, expand)


def _ssd_bwd(dy, xbc, proj, hs, dtb, arow, dsk_x, dproj, *, name):
    L = xbc.shape[0]
    nc = L // Q
    tri, triu, expand = _ssd_consts()

    def body(dy_ref, xbc_ref, dtw_ref, hs_ref, dtb_ref, arow_ref, dsk_ref, t_ref, u_ref, e_ref, dp_in,
             dxbc_ref, ddtw_ref, da_ref, ddx_ref, ddtb_ref, dh_scr):
        del dp_in
        i = pl.program_id(0)

        @pl.when(i == 0)
        def _():
            dh_scr[...] = jnp.zeros_like(dh_scr)

        pre, dt, acs, acs_t, acs_x, dt_x, xs = _ssd_common(xbc_ref, dtw_ref, dtb_ref, arow_ref, t_ref, e_ref)
        dyv = dy_ref[...]
        xdt = xs * dt_x
        eacs = jnp.exp(acs_x)
        acs_last = acs_x[Q - 1:Q, :]
        dec = jnp.exp(acs_last - acs_x)
        gy = dyv * eacs
        causal = lax.broadcasted_iota(jnp.int32, (Q, Q), 0) >= lax.broadcasted_iota(jnp.int32, (Q, Q), 1)
        first = lax.broadcasted_iota(jnp.int32, (Q, 128), 1) < HP
        lane_h = lax.broadcasted_iota(jnp.int32, (Q, 128), 1)
        sub_h = lax.broadcasted_iota(jnp.int32, (128, Q), 0)
        last_row = lax.broadcasted_iota(jnp.int32, (Q, GW), 0) == Q - 1
        dacs = jnp.zeros((Q, 128), F32)
        dacs_t = jnp.zeros((128, Q), F32)
        ddt = jnp.zeros((Q, 128), F32)
        for g in range(NG):
            bg = xbc_ref[:, DI + g * NS:DI + (g + 1) * NS]
            cg = xbc_ref[:, DI + NG * NS + g * NS:DI + NG * NS + (g + 1) * NS]
            s = _dot(cg, bg, NT)
            sl = slice(g * GW, (g + 1) * GW)
            hg = hs_ref[0, :, sl].astype(F32)
            dhn = dh_scr[:, sl]
            eal = eacs[Q - 1:Q, sl]
            gg = gy[:, sl]
            dax = gg * _dot(cg, hg, NN)
            dcg = _dot(gg, hg, NT)
            dh_scr[:, sl] = _dot(cg, gg, TN) + dhn * eal
            dal = eal * _colsum(dhn * hg)
            xdd = xdt[:, sl] * dec[:, sl]
            dbg = _dot(xdd, dhn, NT)
            wv = _dot(bg, dhn, NN)
            dd = wv * xdd
            dax = dax - dd
            dal = dal + _colsum(dd)
            dax = dax + jnp.where(last_row, dal, 0.0)
            dxdt_g = wv * dec[:, sl]
            ds = jnp.zeros((Q, Q), F32)
            dxdt_blocks = []
            for j in range(4):
                lo = g * GW + j * 128
                xb = xdt[:, lo:lo + 128]
                dyb = dyv[:, lo:lo + 128]
                dxb = dxdt_g[:, j * 128:(j + 1) * 128]
                for e in range(2):
                    h = g * 8 + j * 2 + e
                    lm = jnp.exp(jnp.where(causal, acs[:, h:h + 1] - acs_t[h:h + 1, :], NEG))
                    m = s * lm
                    dym = jnp.where(first if e == 0 else jnp.logical_not(first), dyb, 0.0)
                    dm = _dot(dym, xb, NT)
                    r = dm * m
                    dacs = dacs + jnp.where(lane_h == h, jnp.sum(r, axis=1, keepdims=True), 0.0)
                    dacs_t = dacs_t + jnp.where(sub_h == h, _colsum(r), 0.0)
                    ds = ds + dm * lm
                    dxb = dxb + _dot(m, dym, TN)
                dxdt_blocks.append(dxb)
            dxdt = jnp.concatenate(dxdt_blocks, axis=1)
            dcg = dcg + _dot(ds, bg, NN)
            dbg = dbg + _dot(ds, cg, TN)
            dxbc_ref[:, DI + g * NS:DI + (g + 1) * NS] = dbg
            dxbc_ref[:, DI + NG * NS + g * NS:DI + NG * NS + (g + 1) * NS] = dcg
            dxbc_ref[:, sl] = dsk_ref[:, sl] * dyv[:, sl] + dxdt * dt_x[:, sl]
            ddt_g, dacs_g = _reduce_heads([dxdt * xs[:, sl], dax], e_ref[0:128, sl])
            ddt = ddt + ddt_g
            dacs = dacs + dacs_g
        dacs = dacs - dacs_t.T
        ddta = _tri_sum(u_ref, dacs)
        ddt = ddt + ddta * arow_ref[...]
        ddtw = jnp.where(lane_h < NH, ddt * _sigmoid(pre), 0.0)
        ddtw_ref[...] = jnp.concatenate([ddtw, jnp.zeros((Q, DT_PAD - 128), F32)], axis=1).astype(ddtw_ref.dtype)
        _acc_out(da_ref, _colsum(ddta * dt), i)
        _acc_out(ddx_ref, _colsum(dyv * xs), i)
        _acc_out(ddtb_ref, _colsum(ddtw), i)

    rev = lambda c: (nc - 1 - c, 0)
    const = lambda c: (0, 0)
    return pl.pallas_call(
        body, name=name, grid=(nc,),
        in_specs=[pl.BlockSpec((Q, DI), rev), pl.BlockSpec((Q, XBC), rev),
                  pl.BlockSpec((Q, DT_PAD), rev),
                  pl.BlockSpec((1, NS, DI), lambda c: (nc - 1 - c, 0, 0)),
                  pl.BlockSpec((1, 128), const), pl.BlockSpec((1, 128), const), pl.BlockSpec((1, DI), const),
                  pl.BlockSpec((Q, Q), const), pl.BlockSpec((Q, Q), const), pl.BlockSpec((256, DI), const),
                  pl.BlockSpec(memory_space=pl.ANY)],
        out_specs=[pl.BlockSpec((Q, XBC), rev),
                   pl.BlockSpec((Q, DT_PAD), lambda c: (nc - 1 - c, C_DT // DT_PAD)),
                   pl.BlockSpec((1, 128), const), pl.BlockSpec((1, DI), const), pl.BlockSpec((1, 128), const)],
        out_shape=[jax.ShapeDtypeStruct((L, XBC), F32), jax.ShapeDtypeStruct((L, NPROJ), BF16),
                   jax.ShapeDtypeStruct((1, 128), F32), jax.ShapeDtypeStruct((1, DI), F32),
                   jax.ShapeDtypeStruct((1, 128), F32)],
        scratch_shapes=[pltpu.VMEM((NS, DI), F32)],
        input_output_aliases={10: 1},
        compiler_params=_params(("arbitrary",), VMEM_BIG))(dy, xbc, proj, hs, dtb, arow, dsk_x, tri, triu,
                                                          expand, dproj)


def _adam_update(wv, gv, mv, vv):
    c1 = 1.0 - ADAM_B1 ** ADAM_STEP
    c2 = 1.0 - ADAM_B2 ** ADAM_STEP
    mn = ADAM_B1 * mv + (1.0 - ADAM_B1) * gv
    vn = ADAM_B2 * vv + (1.0 - ADAM_B2) * (gv * gv)
    return -ADAM_LR * ((mn / c1) / (jnp.sqrt(vn / c2) + ADAM_EPS) + ADAM_WD * wv), mn, vn


def _adamw(w, g, m, v, *, name, tr=None):
    R = w.shape[0]
    rest = tuple(w.shape[1:])
    if tr is None:
        tr = _pick(R, (256, 128, 64, 32, 16, 8))
    assert R % tr == 0

    def body(w_ref, g_ref, m_ref, v_ref, d_ref, mo_ref, vo_ref):
        d_ref[...], mo_ref[...], vo_ref[...] = _adam_update(w_ref[...], g_ref[...], m_ref[...], v_ref[...])

    zeros = (0,) * len(rest)
    spec = pl.BlockSpec((tr,) + rest, lambda i: (i,) + zeros)
    return pl.pallas_call(body, name=name, grid=(R // tr,), in_specs=[spec] * 4, out_specs=[spec] * 3,
                          out_shape=[jax.ShapeDtypeStruct(w.shape, F32)] * 3,
                          compiler_params=_params(("parallel",)))(w, g, m, v)


def _adamw_small(svrow, g_conv, params, *, name):
    n = len(params)

    def body(*refs):
        sv_ref, gc_ref = refs[0], refs[1]
        ins, outs = refs[2:2 + 3 * n], refs[2 + 3 * n:]
        for p, (key, w, _, _) in enumerate(params):
            w_ref, m_ref, v_ref = ins[3 * p:3 * p + 3]
            g_ref, d_ref, mo_ref, vo_ref = outs[4 * p:4 * p + 4]
            gv = gc_ref[...] if key == "conv_w" else sv_ref[:, SV_OFF[key]:SV_OFF[key] + w.shape[1]]
            g_ref[...] = gv
            d_ref[...], mo_ref[...], vo_ref[...] = _adam_update(w_ref[...], gv, m_ref[...], v_ref[...])

    vm = pl.BlockSpec(memory_space=pltpu.VMEM)
    args = [svrow, g_conv]
    shapes = []
    for _, w, m, v in params:
        args += [w, m, v]
        shapes += [jax.ShapeDtypeStruct(w.shape, F32)] * 4
    res = pl.pallas_call(body, name=name, in_specs=[vm] * len(args), out_specs=[vm] * len(shapes),
                         out_shape=shapes)(*args)
    return {key: tuple(res[4 * p:4 * p + 4]) for p, (key, _, _, _) in enumerate(params)}


def _slab_sum(recv, *, tile, name):
    rows = recv.shape[1]
    assert rows % tile == 0 and tile % 16 == 0

    def body(r_ref, o_ref):
        acc = r_ref[0].astype(F32)
        for j in range(1, N_DEV):
            acc = acc + r_ref[j].astype(F32)
        o_ref[...] = acc

    return pl.pallas_call(body, name=name, grid=(rows // tile,),
                          in_specs=[pl.BlockSpec((N_DEV, tile, D), lambda i: (0, i, 0))],
                          out_specs=pl.BlockSpec((tile, D), lambda i: (i, 0)),
                          out_shape=jax.ShapeDtypeStruct((rows, D), F32),
                          compiler_params=_params(("parallel",)))(recv)


MESH = pl.DeviceIdType.MESH


def _coords():
    return lax.axis_index("x"), lax.axis_index("y"), lax.axis_index("c")


def _peer(k):
    x, y, c = _coords()
    px = 1 - x if k & 4 else x
    py = 1 - y if k & 2 else y
    pc = 1 - c if k & 1 else c
    return (px, py, pc), 4 * px + 2 * py + pc


def _rcopy(src, dst, ssem, rsem, dev):
    return pltpu.make_async_remote_copy(src_ref=src, dst_ref=dst, send_sem=ssem, recv_sem=rsem,
                                        device_id=dev, device_id_type=MESH)


def _exchange_all(src_of, dst_slot, send_sems, recv_sems):
    x, y, c = _coords()
    me = 4 * x + 2 * y + c
    sent = []
    for k in range(1, N_DEV):
        dev, pidx = _peer(k)
        cp = _rcopy(src_of(pidx), dst_slot(me), send_sems.at[k - 1], recv_sems.at[k - 1], dev)
        cp.start()
        sent.append(cp)
    for k in range(1, N_DEV):
        dev, pidx = _peer(k)
        _rcopy(src_of(pidx), dst_slot(pidx), send_sems.at[k - 1], recv_sems.at[k - 1], dev).wait_recv()
    for cp in sent:
        cp.wait_send()


def _rows_of_slots(buf, nslots):
    rows = lax.broadcasted_iota(jnp.int32, (8, buf.shape[-1]), 0)
    out = jnp.zeros((8, buf.shape[-1]), F32)
    for j in range(nslots):
        out = out + jnp.where(rows == j, buf[j], 0.0)
    return out


def _ada_fwd(c, w_ada, b_r, *, name):
    wloc = w_ada.shape[1]

    def body(c_ref, w_ref, b_ref, mod_ref, call_ref, csrc, cbuf, psrc, pbuf, s1, r1, s2, r2):
        x, y, cc = _coords()
        me = 4 * x + 2 * y + cc
        csrc[...] = jnp.broadcast_to(c_ref[...], (8, D))
        cbuf[me] = csrc[...]
        _exchange_all(lambda p: csrc, lambda s: cbuf.at[s], s1, r1)
        call = _rows_of_slots(cbuf, N_DEV)
        call_ref[...] = call
        prod = _dot_hi(_silu(call), w_ref[...])
        for b in range(N_DEV):
            psrc[b] = jnp.broadcast_to(prod[b:b + 1, :], (8, wloc))
        pbuf[me] = psrc[me]
        _exchange_all(lambda p: psrc.at[p], lambda s: pbuf.at[s], s2, r2)
        mod_ref[...] = _rows_of_slots(pbuf, N_DEV) + b_ref[...]

    vm = pl.BlockSpec(memory_space=pltpu.VMEM)
    return pl.pallas_call(
        body, name=name, in_specs=[vm, vm, vm], out_specs=[vm, vm],
        out_shape=[jax.ShapeDtypeStruct((N_DEV, wloc), F32), jax.ShapeDtypeStruct((N_DEV, D), F32)],
        scratch_shapes=[pltpu.VMEM((8, D), F32), pltpu.VMEM((N_DEV, 8, D), F32),
                        pltpu.VMEM((N_DEV, 8, wloc), F32), pltpu.VMEM((N_DEV, 8, wloc), F32),
                        pltpu.SemaphoreType.DMA((N_DEV - 1,)), pltpu.SemaphoreType.DMA((N_DEV - 1,)),
                        pltpu.SemaphoreType.DMA((N_DEV - 1,)), pltpu.SemaphoreType.DMA((N_DEV - 1,))],
        compiler_params=pltpu.CompilerParams(vmem_limit_bytes=VMEM_BIG))(c, w_ada, b_r)


def _gather_slabs(slab, *, name):
    def body(x_ref, out_ref, send_sems, recv_sems, local_sem):
        x, y, c = _coords()
        me, sibling = (x, y, c), (x, y, 1 - c)
        chips = [(1 - x, y), (x, 1 - y), (1 - x, 1 - y)]

        def slot(px, py, pc):
            return out_ref.at[4 * px + 2 * py + pc]

        def copy(k, block, to, src=None):
            return _rcopy(slot(*block) if src is None else src, slot(*block), send_sems.at[k], recv_sems.at[k], to)

        mine = pltpu.make_async_copy(x_ref, slot(*me), local_sem)
        mine.start()
        first = [copy(0, me, sibling, src=x_ref)]
        first += [copy(1 + j, me, (*chip, c), src=x_ref) for j, chip in enumerate(chips)]
        for cp in first:
            cp.start()
        passed = [copy(4 + j, (*chip, c), sibling) for j, chip in enumerate(chips)]
        for j, chip in enumerate(chips):
            copy(1 + j, (*chip, c), me).wait_recv()
            passed[j].start()
        copy(0, sibling, me).wait_recv()
        for j, chip in enumerate(chips):
            copy(4 + j, (*chip, 1 - c), me).wait_recv()
        for cp in first + passed:
            cp.wait_send()
        mine.wait()

    anyspec = pl.BlockSpec(memory_space=pl.ANY)
    return pl.pallas_call(
        body, name=name, in_specs=[anyspec], out_specs=anyspec,
        out_shape=jax.ShapeDtypeStruct((N_DEV,) + slab.shape, slab.dtype),
        scratch_shapes=[pltpu.SemaphoreType.DMA((7,)), pltpu.SemaphoreType.DMA((7,)), pltpu.SemaphoreType.DMA],
    )(slab)


_HBM =pl.BlockSpec(memory_space=pltpu.HBM)
_SEM = pl.BlockSpec(memory_space=pltpu.SEMAPHORE)
_EFFECT = pltpu.SideEffectType.DATAFLOW_SIDE_EFFECTING


def _xchg_src(src_ref, pidx, per_peer):
    return src_ref.at[pidx] if per_peer else src_ref


def _xchg_start(src, *, per_peer, name):
    rows = src.shape[-2]
    land_shape = (N_DEV, rows, D)

    def body(src_ref, land_ref, send_sems, recv_sems, src_thru, land_thru, token):
        del src_thru, land_thru
        x, y, c = _coords()
        me = 4 * x + 2 * y + c
        for k in range(1, N_DEV):
            dev, pidx = _peer(k)
            _rcopy(_xchg_src(src_ref, pidx, per_peer), land_ref.at[me], send_sems.at[k - 1],
                   recv_sems.at[k - 1], dev).start()
        token[...] = jnp.zeros_like(token)

    return pl.pallas_call(
        body, name=name,
        out_shape=(pltpu.SemaphoreType.DMA((N_DEV - 1,)), pltpu.SemaphoreType.DMA((N_DEV - 1,)),
                   pltpu.HBM(src.shape, src.dtype), pltpu.HBM(land_shape, src.dtype),
                   jax.ShapeDtypeStruct((8, 128), F32)),
        in_specs=(_HBM, _HBM),
        out_specs=(_SEM, _SEM, _HBM, _HBM, pl.BlockSpec(memory_space=pltpu.VMEM)),
        input_output_aliases={0: 2, 1: 3},
        compiler_params=pltpu.CompilerParams(has_side_effects=_EFFECT),
    )(pltpu.with_memory_space_constraint(src, pltpu.HBM),
      pltpu.with_memory_space_constraint(lax.empty(land_shape, src.dtype), pltpu.HBM))


def _xchg_wait(started, after, *, per_peer, name):
    send_sems, recv_sems, src_thru, land_thru, _ = started

    def body(src_ref, land_ref, send_sems, recv_sems, after_ref, src_dead, got_ref):
        del after_ref, src_dead, got_ref
        for k in range(1, N_DEV):
            dev, pidx = _peer(k)
            cp = _rcopy(_xchg_src(src_ref, pidx, per_peer), land_ref.at[pidx], send_sems.at[k - 1],
                        recv_sems.at[k - 1], dev)
            cp.wait_send()
            cp.wait_recv()

    return pl.pallas_call(
        body, name=name,
        out_shape=(pltpu.HBM(src_thru.shape, src_thru.dtype), pltpu.HBM(land_thru.shape, land_thru.dtype)),
        in_specs=(_HBM, _HBM, _SEM, _SEM, pl.BlockSpec(memory_space=pl.ANY)),
        out_specs=(_HBM, _HBM),
        input_output_aliases={0: 0, 1: 1},
        compiler_params=pltpu.CompilerParams(has_side_effects=_EFFECT),
    )(src_thru, land_thru, send_sems, recv_sems, after)


def _dep(token):
    return (token, (8, 128), lambda i, j, k: (0, 0))


def _small_allsum(sv, *, name):
    def body(sv_ref, all_ref, sum_ref, send_sems, recv_sems):
        x, y, c = _coords()
        me = 4 * x + 2 * y + c
        all_ref[me] = sv_ref[...]
        _exchange_all(lambda p: sv_ref, lambda s: all_ref.at[s], send_sems, recv_sems)
        acc = all_ref[0]
        for j in range(1, N_DEV):
            acc = acc + all_ref[j]
        sum_ref[...] = acc

    vm = pl.BlockSpec(memory_space=pltpu.VMEM)
    return pl.pallas_call(
        body, name=name, in_specs=[vm], out_specs=[vm, vm],
        out_shape=[jax.ShapeDtypeStruct((N_DEV, SV_ROWS, 128), F32), jax.ShapeDtypeStruct((SV_ROWS, 128), F32)],
        scratch_shapes=[pltpu.SemaphoreType.DMA((7,)), pltpu.SemaphoreType.DMA((7,))],
    )(sv)


def _ada_bwd(call, dmod_loc, *, name):
    wloc = dmod_loc.shape[1]

    def body(c_ref, d_ref, o_ref):
        o_ref[...] = _dot_hi(_silu(c_ref[...]), d_ref[...], TN)

    vm = pl.BlockSpec(memory_space=pltpu.VMEM)
    return pl.pallas_call(body, name=name, in_specs=[vm, vm], out_specs=vm,
                          out_shape=jax.ShapeDtypeStruct((D, wloc), F32),
                          compiler_params=pltpu.CompilerParams(vmem_limit_bytes=VMEM_BIG))(call, dmod_loc)


def _pad_rows(a, rows):
    return jnp.pad(a, ((0, rows - a.shape[0]), (0, 0)))


IN_SHIFT = tuple((IN_ROWS * j) % 16 for j in range(N_DEV))
IN_BASE = tuple(IN_ROWS * j - IN_SHIFT[j] for j in range(N_DEV))
IN_SEGMENTS = ((2048, XBC, C_XBC), (5152, 1024, C_POOL), (0, 2048, C_Z), (6176, 2048, C_GATE), (5120, 32, C_DT))


def _global_pieces(gs):
    pieces = []
    for j in range(N_DEV):
        lo, hi = 0, IN_ROWS_P
        if j > 0 and IN_BASE[j - 1] + IN_ROWS_P > IN_BASE[j]:
            pieces.append((IN_BASE[j], 16, gs[j - 1, IN_ROWS_P - 16:IN_ROWS_P] + gs[j, 0:16]))
            lo = 16
        if j + 1 < N_DEV and IN_BASE[j] + IN_ROWS_P > IN_BASE[j + 1]:
            hi = IN_ROWS_P - 16
        pieces.append((IN_BASE[j] + lo, hi - lo, gs[j, lo:hi]))
    return pieces


def _reorder_in_rows(gs):
    pieces = _global_pieces(gs)
    parts = []
    for lo, n, _ in IN_SEGMENTS:
        for p0, pn, arr in pieces:
            a, b = max(lo, p0), min(lo + n, p0 + pn)
            if a < b:
                parts.append(arr[a - p0:b - p0])
    parts.append(jnp.zeros((DT_PAD - 32, D), gs.dtype))
    return jnp.concatenate(parts, axis=0)


def _restore_in_shards(d):
    slabs = []
    for j in range(N_DEV):
        parts = []
        r, end = IN_BASE[j], IN_BASE[j] + IN_ROWS_P
        while r < end:
            lo, n, new = next(s for s in IN_SEGMENTS if s[0] <= r < s[0] + s[1])
            e = min(end, lo + n)
            parts.append(d[new + r - lo:new + e - lo])
            r = e
        slabs.append(jnp.concatenate(parts, axis=0))
    return jnp.stack(slabs, axis=0)


def _pack_sv(parts):
    flat = []
    for n, size in SV_PARTS:
        v = parts[n].reshape(-1).astype(F32)
        flat.append(jnp.pad(v, (0, size - v.shape[0])))
    v = jnp.concatenate(flat)
    return jnp.pad(v, (0, SV_ROWS * 128 - v.shape[0])).reshape(SV_ROWS, 128)


def _sv_get(flat, n, size):
    return flat[SV_OFF[n]:SV_OFF[n] + size]


def kernel(x, c, w_ada, b_ada, norm_mix_w, w_in, conv_w, conv_b, dt_bias, a_log, d_skip, ssd_norm_w, w_branch_ssd, pool_w, pool_scale, w_branch_pool, w_out, norm_mlp_w, w_up, w_down, norm_final_w, loss_target, m_w_ada, m_b_ada, m_norm_mix_w, m_w_in, m_conv_w, m_conv_b, m_dt_bias, m_a_log, m_d_skip, m_ssd_norm_w, m_w_branch_ssd, m_pool_w, m_pool_scale, m_w_branch_pool, m_w_out, m_norm_mlp_w, m_w_up, m_w_down, m_norm_final_w, v_w_ada, v_b_ada, v_norm_mix_w, v_w_in, v_conv_w, v_conv_b, v_dt_bias, v_a_log, v_d_skip, v_ssd_norm_w, v_w_branch_ssd, v_pool_w, v_pool_scale, v_w_branch_pool, v_w_out, v_norm_mlp_w, v_w_up, v_w_down, v_norm_final_w):
    xs_ = x[0]
    tgt = loss_target[0]
    L = xs_.shape[0]
    me = 4 * lax.axis_index("x") + 2 * lax.axis_index("y") + lax.axis_index("c")
    wloc = w_ada.shape[2]

    mod_p, c_all = _ada_fwd(c, w_ada[0], b_ada.reshape(N_DEV, wloc), name="ada_fwd")
    mod = mod_p.reshape(6, D)
    shift_m, scale_m, gate_m, shift_f, scale_f, gate_f = [mod[i:i + 1] for i in range(6)]

    conv_bits = lax.bitcast_convert_type(conv_w[0], SLAB_DT).reshape(3, D)
    in_shift = (IN_ROWS * me) % 16
    slab_in = lax.dynamic_update_slice(jnp.zeros((IN_ROWS_P, D), SLAB_DT), w_in[0].T.astype(SLAB_DT),
                                       (in_shift, 0))
    slab_in = jnp.concatenate([slab_in, _pad_rows(conv_bits, CONV_ROWS)], axis=0)
    slab_rest = jnp.concatenate([
        w_branch_ssd[0].astype(SLAB_DT),
        pool_w[0].reshape(32, D).astype(SLAB_DT),
        w_branch_pool[0].astype(SLAB_DT),
        w_out[0].astype(SLAB_DT),
        w_up[0].T.astype(SLAB_DT),
        w_down[0].astype(SLAB_DT)], axis=0)
    gs_in = _gather_slabs(slab_in, name="gather_w_in")
    slab_rest, gs_in = lax.optimization_barrier((slab_rest, gs_in))
    rest_started = _xchg_start(slab_rest, per_peer=False, name="gather_rest_start")
    gather_token = rest_started[4]

    w_in_t = _reorder_in_rows(gs_in)
    conv_full = lax.bitcast_convert_type(
        gs_in[:, IN_ROWS_P:IN_ROWS_P + 3].reshape(N_DEV, 4, XBC // N_DEV, 2), F32)
    conv_full = conv_full.transpose(1, 0, 2).reshape(4, XBC)

    dtb = jnp.pad(dt_bias, ((0, 0), (0, 128 - NH)))
    arow = jnp.pad(-jnp.exp(a_log), ((0, 0), (0, 128 - NH)))
    dsk_x = jnp.repeat(d_skip, HP, axis=1)

    tm = _pick(L, (1024, 512, 256, 128))
    tm2 = _pick(L, (2048, 1024, 512, 256, 128))
    tkl = _pick(L, (4096, 2048, 1024, 512, 256, 128))
    tkl2 = _pick(L, (2048, 1024, 512, 256, 128))

    tmh = _pick(L, (512, 256, 128))
    zcol = C_Z // DI
    gcol = C_GATE // (2 * D)

    def whole_rows(w):
        return lambda t: ((L, w), BF16, (t, w), lambda i, j, k: (i, 0))

    def norm1_pro(x_ref, ex, outs, j):
        @pl.when(j == 0)
        def _():
            xv = x_ref[...]
            r = lax.rsqrt(jnp.mean(xv * xv, axis=-1, keepdims=True) + EPS)
            outs[1][...] = (xv * r * ex[0][...] * (1.0 + ex[1][...]) + ex[2][...]).astype(outs[1].dtype)

        return outs[1][...]

    def proj_ep(acc, ex, outs):
        outs[0][...] = acc

        @pl.when(pl.program_id(1) == NPROJ // 768 - 1)
        def _():
            outs[2][...] = acc[:, 768 - DT_PAD:]

    proj, h1, dtp = _mm(
        xs_, w_in_t, "nt", name="in_proj", tm=tm2, tn=768, tk=D,
        extras=[(norm_mix_w, *_vecs()), (scale_m, *_vecs()), (shift_m, *_vecs()), _dep(gather_token)],
        outs=[F32, whole_rows(D)(tm2), ((L, DT_PAD), F32, (tm2, DT_PAD), lambda i, j, k: (i, 0))],
        prologue=norm1_pro, epilogue=proj_ep)
    xbc_raw = proj
    xbc = _conv_fwd(xbc_raw, conv_full, conv_b, name="conv_fwd")
    y_ssm, hs = _ssd_fwd(xbc, dtp, dtb, arow, dsk_x, name="ssd_fwd")

    slab_rest, gs = _xchg_wait(rest_started, y_ssm, per_peer=False, name="gather_rest_wait")
    gs = lax.dynamic_update_slice(gs, slab_rest[None], (me, 0, 0))

    def part(n, rows):
        return gs[:, REST_OFF[n]:REST_OFF[n] + rows]

    w_bssd = part("bssd", 256).reshape(DI, D)
    w_pool = part("pool", 32).reshape(N_DEV, 4, 32, PGW).transpose(1, 0, 2, 3).reshape(POOL_W, PGW)
    w_bpool = part("bpool", 128).reshape(POOL_W, D)
    w_o = part("out", 128).reshape(D, D)
    w_up_t = part("up", 512).reshape(DFF, D)
    w_dn = part("down", 512).reshape(DFF, D)

    def gnorm_pro(y_ref, ex, outs, j):
        z_ref, w_ref = ex
        yg = y_ref[...] * _silu(z_ref[...].astype(F32))
        segs = []
        for k in range(NG):
            sl = slice(k * GW, (k + 1) * GW)
            seg = yg[:, sl]
            r = lax.rsqrt(jnp.mean(seg * seg, axis=-1, keepdims=True) + EPS)
            segs.append((seg * r * w_ref[:, sl]).astype(BF16))
        yn_v = jnp.concatenate(segs, axis=1)
        outs[1][...] = yn_v
        return yn_v

    y_ssd, yn = _mm(y_ssm, w_bssd, "nn", name="branch_ssd", tm=tmh, tn=D, tk=DI,
                    extras=[(proj, *_rows(tmh, DI, zcol)), (ssd_norm_w, *_vecs(DI))],
                    outs=[F32, whole_rows(DI)(tmh)], prologue=gnorm_pro)
    pooled = _pool_fwd(proj, name="pool_fwd")
    yp0, yp1 = _mm_pool(pooled, w_pool, pool_scale, name="pool_mix", tm=tm, transpose_w=False)
    y_pool = _mm(yp1, w_bpool, "nn", name="branch_pool", outs=[F32], tm=tm2, tn=D, tk=D)

    def merge_pro(a_ref, ex, outs, j):
        s = _sigmoid(ex[1][...].astype(F32))
        mv = (s[:, :D] * a_ref[...] + s[:, D:] * ex[0][...]).astype(BF16)
        outs[3][...] = mv
        return mv

    mix, x1, h2, m = _mm(y_ssd, w_o, "nn", name="out_proj", tm=tmh, tn=D, tk=D,
                         extras=[(y_pool, *_rows(tmh)), (proj, *_rows(tmh, 2 * D, gcol)),
                                 (xs_, *_rows(tmh)), (gate_m, *_vecs()), (norm_mlp_w, *_vecs()),
                                 (scale_f, *_vecs()), (shift_f, *_vecs())],
                         outs=[BF16, F32, BF16, whole_rows(D)(tmh)], prologue=merge_pro,
                         epilogue=lambda acc, ex, outs: _ep_resid_norm(acc, ex[2:], outs[:3]))

    def relu2(acc, ex, outs):
        r = jnp.maximum(acc, 0.0)
        outs[0][...] = acc.astype(BF16)
        outs[1][...] = (r * r).astype(BF16)

    up, act = _mm(h2, w_up_t, "nt", name="mlp_up", outs=[BF16, BF16], tm=tm2, tn=1024, tk=D, epilogue=relu2)

    dx2, ddown, loss_p, dnwf, dgate_f = _mm(
        act, w_dn, "nn", name="mlp_down", tm=tmh, tn=D, tk=DFF,
        extras=[(x1, *_rows(tmh)), (tgt, *_rows(tmh)), (gate_f, *_vecs()), (norm_final_w.reshape(1, D), *_vecs())],
        outs=[F32, BF16, _sum_out(128), _sum_out(), _sum_out()], epilogue=_ep_final)

    def drelu2(acc, ex, outs):
        outs[0][...] = (acc * (2.0 * jnp.maximum(ex[0][...].astype(F32), 0.0))).astype(BF16)

    def dep_last(ep):
        return lambda acc, ex, outs: ep(acc, ex[:-1], outs)

    dup = _mm(ddown, w_dn, "nt", name="mlp_down_dx", outs=[BF16], tm=tm2, tn=1024, tk=D,
              extras=[(up, (tm2, 1024), lambda i, j, k: (i, j))], epilogue=drelu2)
    g_dn = _mm(act, ddown, "tn", name="mlp_down_dw", outs=[SLAB_DT], tm=1024, tn=D, tk=tkl)
    g_up_t = _mm(dup, h2, "tn", name="mlp_up_dw", outs=[SLAB_DT], tm=1024, tn=D, tk=tkl)
    gslab_mlp = jnp.concatenate([g_up_t.reshape(N_DEV, 512, D), g_dn.reshape(N_DEV, 512, D)], axis=1)
    mlp_started = _xchg_start(gslab_mlp, per_peer=True, name="scatter_mlp_start")
    dx1, p2, q2, dmix, dgate_m = _mm(
        dup, w_up_t, "nn", name="mlp_up_dx", tm=tmh, tn=D, tk=DFF,
        extras=[(x1, *_rows(tmh)), (dx2, *_rows(tmh)), (norm_mlp_w, *_vecs()), (scale_f, *_vecs()),
                (mix, *_rows(tmh)), (gate_m, *_vecs()), _dep(mlp_started[4])],
        outs=[F32, _sum_out(), _sum_out(), BF16, _sum_out()], epilogue=dep_last(_ep_norm_bwd))
    gcol = C_GATE // (2 * D)
    dy_ssd, dy_pool, dproj = _mm(
        dmix, w_o, "nt", name="out_proj_dx", tm=tmh, tn=D, tk=D,
        extras=[(y_ssd, *_rows(tmh)), (y_pool, *_rows(tmh)), (proj, *_rows(tmh, 2 * D, gcol))],
        outs=[BF16, BF16, ((L, NPROJ), BF16, *_rows(tmh, 2 * D, gcol))], epilogue=_ep_merge_bwd)
    g_o = _mm(m, dmix, "tn", name="out_proj_dw", outs=[SLAB_DT], tm=D, tn=D, tk=tkl)
    zcol = C_Z // DI
    dy_ssm, dproj, d_snw = _mm(
        dy_ssd, w_bssd, "nt", name="branch_ssd_dx", tm=tmh, tn=DI, tk=D,
        extras=[(y_ssm, *_rows(tmh, DI)), (proj, *_rows(tmh, DI, zcol)), (ssd_norm_w, *_vecs(DI)),
                (dproj, None, None)],
        outs=[F32, ((L, NPROJ), BF16, *_rows(tmh, DI, zcol)), _sum_out(DI)],
        epilogue=_ep_gated_norm_bwd, aliases={3: 1})
    g_bssd = _mm(yn, dy_ssd, "tn", name="branch_ssd_dw", outs=[SLAB_DT], tm=1024, tn=D, tk=tkl)
    dxbc, dproj, d_a, d_dx, d_dtb = _ssd_bwd(dy_ssm, xbc, dtp, hs, dtb, arow, dsk_x, dproj, name="ssd_bwd")
    dproj, d_cw, d_cb = _conv_bwd(xbc_raw, dxbc, conv_full, conv_b, dproj, name="conv_bwd")
    dyp0, d_ps = _mm(dy_pool, w_bpool, "nt", name="branch_pool_dx", tm=tm, tn=D, tk=D,
                     extras=[(yp0, *_rows(tm)), (pool_scale, *_vecs())],
                     outs=[BF16, _sum_out()], epilogue=_ep_pscale_bwd)
    g_bpool = _mm(yp1, dy_pool, "tn", name="branch_pool_dw", outs=[SLAB_DT], tm=D, tn=D, tk=tkl)
    dpooled = _mm_pool(dyp0, w_pool, None, name="pool_mix_dx", tm=tm, transpose_w=True)
    g_pool = _mm_pool_tn(pooled, dyp0, name="pool_mix_dw", tk=tkl)
    gslab_mix = jnp.concatenate([
        g_bssd.reshape(N_DEV, 256, D),
        g_pool.reshape(4, N_DEV, 32, PGW).transpose(1, 0, 2, 3).reshape(N_DEV, 32, D).astype(SLAB_DT),
        g_bpool.reshape(N_DEV, 128, D),
        g_o.reshape(N_DEV, 128, D)], axis=1)
    mix_started = _xchg_start(gslab_mix, per_peer=True, name="scatter_mix_start")
    dproj = _pool_bwd(dpooled, dproj, name="pool_bwd")
    g_in_t = _mm(dproj, h1, "tn", name="in_proj_dw", outs=[SLAB_DT], tm=1408, tn=D, tk=tkl2,
                 extras=[_dep(mix_started[4])])
    gslab_in = _restore_in_shards(g_in_t)
    in_started = _xchg_start(gslab_in, per_peer=True, name="scatter_in_start")
    grad_x, p1, q1 = _mm(
        dproj, w_in_t, "nn", name="in_proj_dx", tm=tmh, tn=D, tk=2816,
        extras=[(xs_, *_rows(tmh)), (dx1, *_rows(tmh)), (norm_mix_w, *_vecs()), (scale_m, *_vecs()),
                _dep(in_started[4])],
        outs=[F32, _sum_out(), _sum_out()], epilogue=dep_last(_ep_norm_bwd))

    def landed(started, after, tile, name):
        src, land = _xchg_wait(started, after, per_peer=True, name=name + "_wait")
        own = lax.dynamic_slice_in_dim(src, me, 1, axis=0)
        return _slab_sum(lax.dynamic_update_slice(land, own, (me, 0, 0)), tile=tile, name=name + "_sum")

    gsum_mlp = landed(mlp_started, grad_x, 256, "scatter_mlp")
    gsum_mix = landed(mix_started, grad_x, 272, "scatter_mix")
    gsum_in = landed(in_started, grad_x, 208, "scatter_in")

    dmod = jnp.concatenate([q1, p1 * norm_mix_w, dgate_m, q2, p2 * norm_mlp_w, dgate_f], axis=1)
    d_alog = d_a[:, :NH] * (-jnp.exp(a_log))
    sv = _pack_sv({
        "b_ada": dmod, "norm_mix_w": p1 * (1.0 + scale_m), "conv_b": d_cb, "dt_bias": d_dtb[:, :NH],
        "a_log": d_alog, "d_skip": d_dx.reshape(NH, HP).sum(axis=1), "ssd_norm_w": d_snw,
        "pool_scale": d_ps, "norm_mlp_w": p2 * (1.0 + scale_f), "norm_final_w": dnwf, "conv_w": d_cw,
        "loss": loss_p[:, :1]})
    sv_all, sv_sum = _small_allsum(sv, name="small_allsum")
    flat = sv_sum.reshape(-1)
    loss = flat[SV_OFF["loss"]]
    dmod_all = sv_all.reshape(N_DEV, SV_ROWS * 128)[:, :6 * D]
    g_w_ada = _ada_bwd(c_all, lax.dynamic_slice_in_dim(dmod_all, me * wloc, wloc, axis=1), name="ada_bwd")

    g_conv_w = lax.dynamic_slice_in_dim(_sv_get(flat, "conv_w", 4 * XBC).reshape(4, XBC),
                                        me * (XBC // N_DEV), XBC // N_DEV, axis=1)
    small = [("b_ada", b_ada, m_b_ada, v_b_ada), ("norm_mix_w", norm_mix_w, m_norm_mix_w, v_norm_mix_w),
             ("conv_b", conv_b, m_conv_b, v_conv_b), ("dt_bias", dt_bias, m_dt_bias, v_dt_bias),
             ("a_log", a_log, m_a_log, v_a_log), ("d_skip", d_skip, m_d_skip, v_d_skip),
             ("ssd_norm_w", ssd_norm_w, m_ssd_norm_w, v_ssd_norm_w),
             ("pool_scale", pool_scale, m_pool_scale, v_pool_scale),
             ("norm_mlp_w", norm_mlp_w, m_norm_mlp_w, v_norm_mlp_w),
             ("norm_final_w", norm_final_w[None], m_norm_final_w[None], v_norm_final_w[None]),
             ("conv_w", conv_w[0], m_conv_w[0], v_conv_w[0])]
    small_out = _adamw_small(sv_sum.reshape(1, SV_ROWS * 128), g_conv_w, small, name="adamw_small")
    small_out["norm_final_w"] = tuple(a[0] for a in small_out["norm_final_w"])
    small_out["conv_w"] = tuple(a[None] for a in small_out["conv_w"])

    def gpart(n, rows_):
        return gsum_mix[MIX_OFF[n]:MIX_OFF[n] + rows_]

    def lin(a):
        return a[0].T[:, None, :]

    g_lin = lax.dynamic_slice_in_dim(gsum_in, in_shift, IN_ROWS, axis=0)[:, None, :]
    dlt, mn, vn = _adamw(lin(w_in), g_lin, lin(m_w_in), lin(v_w_in), name="adamw_w_in", tr=IN_ROWS // 4)
    big_in = tuple(a[:, 0, :].T[None] for a in (g_lin, dlt, mn, vn))

    big = {
        "w_ada": (w_ada, m_w_ada, v_w_ada, g_w_ada, (D, wloc)),
        "w_branch_ssd": (w_branch_ssd, m_w_branch_ssd, v_w_branch_ssd, gpart("bssd", 256), (256, D)),
        "pool_w": (pool_w, m_pool_w, v_pool_w, gpart("pool", 32).reshape(128, PGW), (128, PGW)),
        "w_branch_pool": (w_branch_pool, m_w_branch_pool, v_w_branch_pool, gpart("bpool", 128), (128, D)),
        "w_out": (w_out, m_w_out, v_w_out, gpart("out", 128), (128, D)),
        "w_up": (w_up, m_w_up, v_w_up, gsum_mlp[:512].T, (D, 512)),
        "w_down": (w_down, m_w_down, v_w_down, gsum_mlp[512:], (512, D)),
    }
    big_out = {}
    for n, (w, mm_, vv, g, shp2) in big.items():
        dlt, mn, vn = _adamw(w.reshape(shp2), g, mm_.reshape(shp2), vv.reshape(shp2), name="adamw_" + n)
        big_out[n] = (g.reshape(w.shape), dlt.reshape(w.shape), mn.reshape(w.shape), vn.reshape(w.shape))

    order = ["w_ada", "b_ada", "norm_mix_w", "w_in", "conv_w", "conv_b", "dt_bias", "a_log", "d_skip",
             "ssd_norm_w", "w_branch_ssd", "pool_w", "pool_scale", "w_branch_pool", "w_out", "norm_mlp_w",
             "w_up", "w_down", "norm_final_w"]
    big_out["w_in"] = big_in
    res = {**small_out, **big_out}
    outs = [loss, grad_x.reshape(x.shape)]
    for k in range(4):
        outs += [res[n][k] for n in order]
    return tuple(outs)
```

```python
import functools

import numpy as np
import jax
import jax.numpy as jnp
from jax import lax
from jax.experimental import pallas as pl
from jax.experimental.pallas import tpu as pltpu

F32 = jnp.float32
BF16 = jnp.bfloat16
SLAB_DT = jnp.bfloat16
_MXU_DTYPE = jnp.bfloat16

N_DEV = 8
D = 1024
DI = 2048
NH = 32
HP = 64
NG = 4
NS = 128
Q = 128
XBC = DI + 2 * NG * NS
DFF = 4096
N_IN = 8224
EPS = 1e-5
POOL_W = 1024
PGW = 256

C_XBC, C_POOL, C_Z, C_GATE, C_DT = 0, 3072, 4096, 6144, 8192
DT_PAD = 256
NPROJ = C_DT + DT_PAD

IN_ROWS = N_IN // N_DEV
IN_ROWS_P = 1040
CONV_ROWS = 16
REST_PARTS = (("bssd", 256), ("pool", 32), ("bpool", 128), ("out", 128), ("up", 512), ("down", 512))
REST_OFF = {}
_o = 0
for _n, _r in REST_PARTS:
    REST_OFF[_n] = _o
    _o += _r
REST_ROWS = _o
MIX_PARTS = (("bssd", 256), ("pool", 32), ("bpool", 128), ("out", 128))
MIX_OFF = {}
_o = 0
for _n, _r in MIX_PARTS:
    MIX_OFF[_n] = _o
    _o += _r
MIX_ROWS = _o

SV_PARTS = (("b_ada", 6144), ("norm_mix_w", 1024), ("conv_b", 3072), ("dt_bias", 128), ("a_log", 128),
            ("d_skip", 128), ("ssd_norm_w", 2048), ("pool_scale", 1024), ("norm_mlp_w", 1024),
            ("norm_final_w", 1024), ("conv_w", 4 * XBC), ("loss", 128))
SV_OFF = {}
_o = 0
for _n, _r in SV_PARTS:
    SV_OFF[_n] = _o
    _o += _r
SV_ROWS = 224
assert _o <= SV_ROWS * 128

ADAM_LR, ADAM_B1, ADAM_B2, ADAM_EPS, ADAM_WD, ADAM_STEP = 0.001, 0.9, 0.999, 1e-08, 0.01, 10

VMEM_BIG = 56 * 1024 * 1024
NEG = -1e30

NN = ((1,), (0,))
NT = ((1,), (1,))
TN = ((0,), (0,))


def _dot(a, b, dims=NN):
    return lax.dot_general(a.astype(_MXU_DTYPE), b.astype(_MXU_DTYPE), (dims, ((), ())),
                           preferred_element_type=F32)


def _dot_hi(a, b, dims=NN):
    return lax.dot_general(a.astype(F32), b.astype(F32), (dims, ((), ())),
                           precision=lax.Precision.HIGHEST, preferred_element_type=F32)


def _pick(n, cands):
    for c in cands:
        if n % c == 0:
            return c
    return n


def _sigmoid(x):
    return 1.0 / (1.0 + jnp.exp(-x))


def _silu(x):
    return x * _sigmoid(x)


def _dsilu(x):
    s = _sigmoid(x)
    return s * (1.0 + x * (1.0 - s))


def _softplus(x):
    return jnp.maximum(x, 0.0) + jnp.log(1.0 + jnp.exp(-jnp.abs(x)))


def _params(sem, vmem=None):
    return pltpu.CompilerParams(dimension_semantics=sem, vmem_limit_bytes=vmem)


def _mm(a, b, mode, *, name, outs, tm, tn, tk, extras=(), epilogue=None, aliases=None, prologue=None):
    if mode == "tn":
        K, M = a.shape
        N = b.shape[1]
        a_spec = pl.BlockSpec((tk, tm), lambda i, j, k: (k, i))
        b_spec = pl.BlockSpec((tk, tn), lambda i, j, k: (k, j))
        dims = TN
    else:
        M = a.shape[0]
        K = b.shape[0] if mode == "nn" else b.shape[1]
        if prologue is None:
            assert a.shape[1] == K
            a_spec = pl.BlockSpec((tm, tk), lambda i, j, k: (i, k))
        else:
            assert tk == K
            a_spec = pl.BlockSpec((tm, a.shape[1]), lambda i, j, k: (i, 0))
        if mode == "nn":
            N = b.shape[1]
            b_spec = pl.BlockSpec((tk, tn), lambda i, j, k: (k, j))
            dims = NN
        else:
            N = b.shape[0]
            b_spec = pl.BlockSpec((tn, tk), lambda i, j, k: (j, k))
            dims = NT
    assert M % tm == 0 and N % tn == 0 and K % tk == 0, (name, M, N, K, tm, tn, tk)
    nk = K // tk
    ne, no = len(extras), len(outs)
    if epilogue is None:
        def epilogue(acc, ex, out_refs):
            out_refs[0][...] = acc.astype(out_refs[0].dtype)

    def body(a_ref, b_ref, *rest):
        ex, out_refs = rest[:ne], rest[ne:ne + no]
        lhs = a_ref[...] if prologue is None else prologue(a_ref, ex, out_refs, pl.program_id(1))
        p = _dot(lhs, b_ref[...], dims)
        if nk == 1:
            epilogue(p, ex, out_refs)
        else:
            acc = rest[-1]
            k = pl.program_id(2)

            @pl.when(k == 0)
            def _():
                acc[...] = p

            @pl.when(jnp.logical_and(k > 0, k < nk - 1))
            def _():
                acc[...] += p

            @pl.when(k == nk - 1)
            def _():
                epilogue(acc[...] + p, ex, out_refs)

    out_specs, out_shape = [], []
    for o in outs:
        if isinstance(o, tuple):
            shape, dt, bs, im = o
            out_specs.append(pl.BlockSpec(bs, im))
            out_shape.append(jax.ShapeDtypeStruct(shape, dt))
        else:
            out_specs.append(pl.BlockSpec((tm, tn), lambda i, j, k: (i, j)))
            out_shape.append(jax.ShapeDtypeStruct((M, N), o))
    in_specs = [a_spec, b_spec]
    for _, bs, im in extras:
        in_specs.append(pl.BlockSpec(memory_space=pl.ANY) if bs is None else pl.BlockSpec(bs, im))
    res = pl.pallas_call(
        body, name=name,
        grid=(M // tm, N // tn, nk),
        in_specs=in_specs, out_specs=out_specs, out_shape=out_shape,
        scratch_shapes=[pltpu.VMEM((tm, tn), F32)] if nk > 1 else [],
        input_output_aliases={2 + e: o for e, o in (aliases or {}).items()},
        compiler_params=_params(("arbitrary", "arbitrary", "arbitrary"), VMEM_BIG),
    )(a, b, *[e[0] for e in extras])
    return res if no > 1 else res[0]


def _rows(tm, w=D, col=0):
    return (tm, w), lambda i, j, k, c=col: (i, c)


def _vecs(w=D, col=0):
    return (1, w), lambda i, j, k, c=col: (0, c)


def _sum_out(w=D):
    return ((1, w), F32, (1, w), lambda i, j, k: (0, 0))


def _mm_pool(a, w, scale, *, name, tm, transpose_w):
    L = a.shape[0]
    dims = NT if transpose_w else NN

    def body(a_ref, w_ref, *rest):
        p = _dot(a_ref[...], w_ref[...], dims)
        if transpose_w:
            rest[0][...] = p
        else:
            s_ref, o0, o1 = rest
            o0[...] = p.astype(o0.dtype)
            o1[...] = (p * s_ref[...]).astype(o1.dtype)

    blk = pl.BlockSpec((tm, PGW), lambda i, j: (i, j))
    in_specs = [blk, pl.BlockSpec((PGW, PGW), lambda i, j: (j, 0))]
    args = [a, w]
    if transpose_w:
        out_specs, out_shape = [blk], [jax.ShapeDtypeStruct((L, POOL_W), F32)]
    else:
        in_specs.append(pl.BlockSpec((1, PGW), lambda i, j: (0, j)))
        args.append(scale)
        out_specs = [blk, blk]
        out_shape = [jax.ShapeDtypeStruct((L, POOL_W), BF16), jax.ShapeDtypeStruct((L, POOL_W), BF16)]
    res = pl.pallas_call(body, name=name, grid=(L // tm, 4), in_specs=in_specs, out_specs=out_specs,
                         out_shape=out_shape, compiler_params=_params(("parallel", "parallel")))(*args)
    return res[0] if transpose_w else res


def _mm_pool_tn(a, b, *, name, tk):
    L = a.shape[0]

    def body(a_ref, b_ref, o_ref):
        p = _dot(a_ref[...], b_ref[...], TN)

        @pl.when(pl.program_id(1) == 0)
        def _():
            o_ref[...] = p

        @pl.when(pl.program_id(1) > 0)
        def _():
            o_ref[...] += p

    blk = pl.BlockSpec((tk, PGW), lambda g, k: (k, g))
    return pl.pallas_call(body, name=name, grid=(4, L // tk), in_specs=[blk, blk],
                          out_specs=pl.BlockSpec((PGW, PGW), lambda g, k: (g, 0)),
                          out_shape=jax.ShapeDtypeStruct((POOL_W, PGW), F32),
                          compiler_params=_params(("parallel", "arbitrary")))(a, b)


def _acc_out(ref, val, i):
    @pl.when(i == 0)
    def _():
        ref[...] = val

    @pl.when(i > 0)
    def _():
        ref[...] += val


def _colsum(v):
    return jnp.sum(v, axis=0, keepdims=True)


def _ep_resid_norm(acc, ex, outs):
    x_ref, g_ref, nw_ref, sc_ref, sh_ref = ex
    mix_ref, x1_ref, h_ref = outs
    mix_ref[...] = acc.astype(mix_ref.dtype)
    xv = x_ref[...] + g_ref[...] * acc
    x1_ref[...] = xv
    r = lax.rsqrt(jnp.mean(xv * xv, axis=-1, keepdims=True) + EPS)
    h_ref[...] = (xv * r * nw_ref[...] * (1.0 + sc_ref[...]) + sh_ref[...]).astype(h_ref.dtype)


def _ep_final(acc, ex, outs):
    x1_ref, t_ref, g_ref, nw_ref = ex
    dx2_ref, dd_ref, loss_ref, dnw_ref, dg_ref = outs
    i = pl.program_id(0)
    x2 = x1_ref[...] + g_ref[...] * acc
    r = lax.rsqrt(jnp.mean(x2 * x2, axis=-1, keepdims=True) + EPS)
    xh = x2 * r
    e = xh * nw_ref[...] - t_ref[...]
    part = 0.5 * jnp.sum(jnp.mean(e * e, axis=-1, keepdims=True), axis=0, keepdims=True)
    dy = e * (1.0 / D)
    g = dy * nw_ref[...]
    dx2 = r * (g - xh * jnp.mean(g * xh, axis=-1, keepdims=True))
    dx2_ref[...] = dx2
    dd_ref[...] = (dx2 * g_ref[...]).astype(dd_ref.dtype)
    _acc_out(loss_ref, jnp.broadcast_to(part, (1, 128)), i)
    _acc_out(dnw_ref, _colsum(dy * xh), i)
    _acc_out(dg_ref, _colsum(dx2 * acc), i)


def _ep_norm_bwd(acc, ex, outs):
    x_ref, dr_ref, nw_ref, sc_ref = ex[:4]
    dx_ref, p_ref, q_ref = outs[:3]
    i = pl.program_id(0)
    xv = x_ref[...]
    r = lax.rsqrt(jnp.mean(xv * xv, axis=-1, keepdims=True) + EPS)
    xh = xv * r
    g = acc * (nw_ref[...] * (1.0 + sc_ref[...]))
    dx = dr_ref[...] + r * (g - xh * jnp.mean(g * xh, axis=-1, keepdims=True))
    dx_ref[...] = dx
    _acc_out(p_ref, _colsum(acc * xh), i)
    _acc_out(q_ref, _colsum(acc), i)
    if len(ex) > 4:
        m_ref, g_ref = ex[4:]
        dm_ref, dg_ref = outs[3:]
        dm_ref[...] = (dx * g_ref[...]).astype(dm_ref.dtype)
        _acc_out(dg_ref, _colsum(dx * m_ref[...].astype(F32)), i)


def _ep_merge_bwd(acc, ex, outs):
    a_ref, b_ref, gl_ref = ex
    da_ref, db_ref, dgl_ref = outs
    s = _sigmoid(gl_ref[...].astype(F32))
    s1, s2 = s[:, :D], s[:, D:]
    da_ref[...] = (acc * s1).astype(da_ref.dtype)
    db_ref[...] = (acc * s2).astype(db_ref.dtype)
    dgl_ref[:, :D] = (acc * a_ref[...] * s1 * (1.0 - s1)).astype(dgl_ref.dtype)
    dgl_ref[:, D:] = (acc * b_ref[...] * s2 * (1.0 - s2)).astype(dgl_ref.dtype)


def _ep_pscale_bwd(acc, ex, outs):
    y_ref, s_ref = ex
    o_ref, ds_ref = outs
    o_ref[...] = (acc * s_ref[...]).astype(o_ref.dtype)
    _acc_out(ds_ref, _colsum(acc * y_ref[...].astype(F32)), pl.program_id(0))


GW = DI // NG


def _ep_gated_norm_bwd(acc, ex, outs):
    y_ref, z_ref, w_ref, _ = ex
    dy_ref, dz_ref, dw_ref = outs
    zv = z_ref[...].astype(F32)
    yv = y_ref[...]
    sz = _silu(zv)
    yg = yv * sz
    dsz = _dsilu(zv)
    dws = []
    for k in range(NG):
        sl = slice(k * GW, (k + 1) * GW)
        seg = yg[:, sl]
        r = lax.rsqrt(jnp.mean(seg * seg, axis=-1, keepdims=True) + EPS)
        sh = seg * r
        dn = acc[:, sl]
        g = dn * w_ref[:, sl]
        dyg = r * (g - sh * jnp.mean(g * sh, axis=-1, keepdims=True))
        dy_ref[:, sl] = dyg * sz[:, sl]
        dz_ref[:, sl] = (dyg * yv[:, sl] * dsz[:, sl]).astype(dz_ref.dtype)
        dws.append(_colsum(dn * sh))
    _acc_out(dw_ref, jnp.concatenate(dws, axis=1), pl.program_id(0))


CONV_CB = 128
HALO = 16


def _time_chunk(L):
    return _pick(L, (256, 128))


def _with_halo(x_ref, i, r0, rc):
    p0 = pl.multiple_of(jnp.maximum(r0 - HALO, 0), HALO)
    prev = jnp.where(i > 0, x_ref[pl.ds(p0, HALO), :].astype(F32), 0.0)
    return jnp.concatenate([prev, x_ref[pl.ds(r0, rc), :].astype(F32)], axis=0)


def _conv_fwd(proj, w, b, *, name):
    L = proj.shape[0]
    rc = _time_chunk(L)
    n = L // rc

    def body(x_ref, w_ref, b_ref, o_ref):
        wv = w_ref[...]
        bv = b_ref[...]

        def step(i, c):
            r0 = pl.multiple_of(i * rc, rc)
            ext = _with_halo(x_ref, i, r0, rc)
            acc = bv + ext * wv[3:4]
            for j in (1, 2, 3):
                acc = acc + pltpu.roll(ext, j, 0) * wv[3 - j:4 - j]
            acc = acc[HALO:]
            o_ref[pl.ds(r0, rc), :] = acc * _sigmoid(acc)
            return c

        lax.fori_loop(0, n, step, 0)

    return pl.pallas_call(
        body, name=name, grid=(XBC // CONV_CB,),
        in_specs=[pl.BlockSpec((L, CONV_CB), lambda j: (0, j + C_XBC // CONV_CB)),
                  pl.BlockSpec((4, CONV_CB), lambda j: (0, j)), pl.BlockSpec((1, CONV_CB), lambda j: (0, j))],
        out_specs=pl.BlockSpec((L, CONV_CB), lambda j: (0, j)),
        out_shape=jax.ShapeDtypeStruct((L, XBC), F32),
        compiler_params=_params(("parallel",), VMEM_BIG))(proj, w, b)


def _conv_bwd(proj, dy, w, b, dproj, *, name):
    L = proj.shape[0]
    rc = _time_chunk(L)
    n = L // rc

    def body(x_ref, dy_ref, w_ref, b_ref, dp_in, dx_ref, dw_ref, db_ref):
        del dp_in
        wv = w_ref[...]
        bv = b_ref[...]

        def step(k, carry):
            nxt, db, d0, d1, d2, d3 = carry
            i = n - 1 - k
            r0 = pl.multiple_of(i * rc, rc)
            ext = _with_halo(x_ref, i, r0, rc)
            xk = [ext[HALO:]] + [pltpu.roll(ext, j, 0)[HALO:] for j in (1, 2, 3)]
            pre = bv
            for j in range(4):
                pre = pre + xk[j] * wv[3 - j:4 - j]
            dpre = dy_ref[pl.ds(r0, rc), :] * _dsilu(pre)
            dext = jnp.concatenate([dpre, nxt], axis=0)
            acc = dext * wv[3:4]
            for j in (1, 2, 3):
                acc = acc + pltpu.roll(dext, rc + HALO - j, 0) * wv[3 - j:4 - j]
            dx_ref[pl.ds(r0, rc), :] = acc[:rc].astype(dx_ref.dtype)
            return (dpre[:HALO], db + _colsum(dpre), d0 + _colsum(dpre * xk[3]), d1 + _colsum(dpre * xk[2]),
                    d2 + _colsum(dpre * xk[1]), d3 + _colsum(dpre * xk[0]))

        z = jnp.zeros((1, CONV_CB), F32)
        _, db, d0, d1, d2, d3 = lax.fori_loop(0, n, step, (jnp.zeros((HALO, CONV_CB), F32), z, z, z, z, z))
        db_ref[...] = db
        dw_ref[...] = jnp.concatenate([d0, d1, d2, d3], axis=0)

    nb = XBC // CONV_CB
    return pl.pallas_call(
        body, name=name, grid=(nb,),
        in_specs=[pl.BlockSpec((L, CONV_CB), lambda j: (0, j + C_XBC // CONV_CB)),
                  pl.BlockSpec((L, CONV_CB), lambda j: (0, j)),
                  pl.BlockSpec((4, CONV_CB), lambda j: (0, j)), pl.BlockSpec((1, CONV_CB), lambda j: (0, j)),
                  pl.BlockSpec(memory_space=pl.ANY)],
        out_specs=[pl.BlockSpec((L, CONV_CB), lambda j: (0, j + C_XBC // CONV_CB)),
                   pl.BlockSpec((4, CONV_CB), lambda j: (0, j)), pl.BlockSpec((1, CONV_CB), lambda j: (0, j))],
        out_shape=[jax.ShapeDtypeStruct((L, NPROJ), BF16), jax.ShapeDtypeStruct((4, XBC), F32),
                   jax.ShapeDtypeStruct((1, XBC), F32)],
        input_output_aliases={4: 0},
        compiler_params=_params(("parallel",), VMEM_BIG))(proj, dy, w, b, dproj)


def _pool_fwd(proj, *, name):
    L = proj.shape[0]
    rc = _time_chunk(L)
    n = L // rc

    def body(x_ref, o_ref, pad):
        g = pl.program_id(0)
        pad[0:HALO, :] = jnp.zeros((HALO, PGW), F32)

        def fill(i, c):
            r0 = pl.multiple_of(i * rc, rc)
            pad[pl.ds(r0 + HALO, rc), :] = x_ref[pl.ds(r0, rc), :].astype(F32)
            return c

        lax.fori_loop(0, n, fill, 0)
        rows = lax.broadcasted_iota(jnp.int32, (rc, PGW), 0)

        for gi in range(4):
            win = 2 << gi

            @pl.when(g == gi)
            def _(gi=gi, win=win):
                def step(i, c):
                    r0 = pl.multiple_of(i * rc, rc)
                    ext = pad[pl.ds(r0, rc + HALO), :]
                    s = ext
                    sh = 1
                    while sh < win:
                        s = s + pltpu.roll(s, sh, 0)
                        sh *= 2
                    cnt = jnp.minimum(rows + (r0 + 1), win).astype(F32)
                    o_ref[pl.ds(r0, rc), :] = (s[HALO:] / cnt - ext[HALO:]).astype(o_ref.dtype)
                    return c

                lax.fori_loop(0, n, step, 0)

    return pl.pallas_call(
        body, name=name, grid=(4,),
        in_specs=[pl.BlockSpec((L, PGW), lambda j: (0, j + C_POOL // PGW))],
        out_specs=pl.BlockSpec((L, PGW), lambda j: (0, j)),
        out_shape=jax.ShapeDtypeStruct((L, POOL_W), BF16),
        scratch_shapes=[pltpu.VMEM((L + HALO, PGW), F32)],
        compiler_params=_params(("parallel",), VMEM_BIG))(proj)


def _pool_bwd(dpooled, dproj, *, name):
    L = dpooled.shape[0]
    rc = _time_chunk(L)
    n = L // rc

    def body(d_ref, dp_in, o_ref, pad):
        del dp_in
        g = pl.program_id(0)
        pad[L:L + HALO, :] = jnp.zeros((HALO, PGW), F32)
        rows = lax.broadcasted_iota(jnp.int32, (rc, PGW), 0)

        for gi in range(4):
            win = 2 << gi

            @pl.when(g == gi)
            def _(gi=gi, win=win):
                def fill(i, c):
                    r0 = pl.multiple_of(i * rc, rc)
                    cnt = jnp.minimum(rows + (r0 + 1), win).astype(F32)
                    pad[pl.ds(r0, rc), :] = d_ref[pl.ds(r0, rc), :] / cnt
                    return c

                lax.fori_loop(0, n, fill, 0)

                def step(i, c):
                    r0 = pl.multiple_of(i * rc, rc)
                    s = pad[pl.ds(r0, rc + HALO), :]
                    sh = 1
                    while sh < win:
                        s = s + pltpu.roll(s, rc + HALO - sh, 0)
                        sh *= 2
                    o_ref[pl.ds(r0, rc), :] = (s[:rc] - d_ref[pl.ds(r0, rc), :]).astype(o_ref.dtype)
                    return c

                lax.fori_loop(0, n, step, 0)

    return pl.pallas_call(
        body, name=name, grid=(4,),
        in_specs=[pl.BlockSpec((L, PGW), lambda j: (0, j)), pl.BlockSpec(memory_space=pl.ANY)],
        out_specs=pl.BlockSpec((L, PGW), lambda j: (0, j + C_POOL // PGW)),
        out_shape=jax.ShapeDtypeStruct((L, NPROJ), BF16),
        scratch_shapes=[pltpu.VMEM((L + HALO, PGW), F32)],
        input_output_aliases={1: 0},
        compiler_params=_params(("parallel",), VMEM_BIG))(dpooled, dproj)


_SPLIT_DT = jnp.bfloat16


def _ssd_consts():
    tri = np.tril(np.ones((Q, Q), np.float32))
    exp = np.zeros((128, DI), np.float32)
    for h in range(NH):
        exp[h, h * HP:(h + 1) * HP] = 1.0
    exp2 = np.concatenate([exp, exp], axis=0)
    return (jnp.asarray(tri, dtype=_SPLIT_DT), jnp.asarray(tri.T.copy(), dtype=_SPLIT_DT),
            jnp.asarray(exp2, dtype=_SPLIT_DT))


def _split(v, n):
    parts, r = [], v
    for _ in range(n):
        p = r.astype(_SPLIT_DT)
        parts.append(p)
        r = r - p.astype(F32)
    return parts


def _bdot(a, b, dims):
    return lax.dot_general(a, b, (dims, ((), ())), preferred_element_type=F32)


def _tri_sum(t_ref, v):
    r = _bdot(t_ref[...], jnp.concatenate(_split(v, 3), axis=1), NN)
    return r[:, :128] + r[:, 128:256] + r[:, 256:]


def _expand(v, e2_ref):
    return _bdot(jnp.concatenate(_split(v, 2), axis=1), e2_ref[...], NN)


def _reduce_heads(vals, eg):
    parts = []
    for v in vals:
        parts += _split(v, 2)
    r = _bdot(jnp.concatenate(parts, axis=0), eg, NT)
    return [r[2 * i * Q:(2 * i + 1) * Q] + r[(2 * i + 1) * Q:(2 * i + 2) * Q] for i in range(len(vals))]


def _ssd_common(xbc_ref, dtw_ref, dtb_ref, arow_ref, t_ref, e_ref):
    pre = dtw_ref[:, :128] + dtb_ref[...]
    dt = _softplus(pre)
    acs = _tri_sum(t_ref, dt * arow_ref[...])
    acs_x = _expand(acs, e_ref)
    dt_x = _expand(dt, e_ref)
    xs = xbc_ref[:, 0:DI]
    return pre, dt, acs, acs.T, acs_x, dt_x, xs


def _ssd_fwd(xbc, proj, dtb, arow, dsk_x, *, name):
    L = xbc.shape[0]
    nc = L // Q
    tri, _, expand = _ssd_consts()

    def body(xbc_ref, dtw_ref, dtb_ref, arow_ref, dsk_ref, t_ref, e_ref, y_ref, hs_ref, h_scr):
        @pl.when(pl.program_id(0) == 0)
        def _():
            h_scr[...] = jnp.zeros_like(h_scr)

        _, dt, acs, acs_t, acs_x, dt_x, xs = _ssd_common(xbc_ref, dtw_ref, dtb_ref, arow_ref, t_ref, e_ref)
        xdt = xs * dt_x
        eacs = jnp.exp(acs_x)
        acs_last = acs_x[Q - 1:Q, :]
        dec = jnp.exp(acs_last - acs_x)
        hs_ref[0] = h_scr[...].astype(hs_ref.dtype)
        causal = lax.broadcasted_iota(jnp.int32, (Q, Q), 0) >= lax.broadcasted_iota(jnp.int32, (Q, Q), 1)
        first = lax.broadcasted_iota(jnp.int32, (Q, 128), 1) < HP
        for g in range(NG):
            bg = xbc_ref[:, DI + g * NS:DI + (g + 1) * NS]
            cg = xbc_ref[:, DI + NG * NS + g * NS:DI + NG * NS + (g + 1) * NS]
            s = _dot(cg, bg, NT)
            sl = slice(g * GW, (g + 1) * GW)
            hg = h_scr[:, sl]
            yoff = _dot(cg, hg, NN) * eacs[:, sl]
            st = _dot(bg, xdt[:, sl] * dec[:, sl], TN)
            h_scr[:, sl] = hg * eacs[Q - 1:Q, sl] + st
            for j in range(4):
                lo = g * GW + j * 128
                xb = xdt[:, lo:lo + 128]
                yp = yoff[:, j * 128:(j + 1) * 128] + dsk_ref[:, lo:lo + 128] * xs[:, lo:lo + 128]
                for e in range(2):
                    h = g * 8 + j * 2 + e
                    lm = jnp.exp(jnp.where(causal, acs[:, h:h + 1] - acs_t[h:h + 1, :], NEG))
                    xm = jnp.where(first if e == 0 else jnp.logical_not(first), xb, 0.0)
                    yp = yp + _dot(s * lm, xm, NN)
                y_ref[:, lo:lo + 128] = yp

    return pl.pallas_call(
        body, name=name, grid=(nc,),
        in_specs=[pl.BlockSpec((Q, XBC), lambda c: (c, 0)),
                  pl.BlockSpec((Q, DT_PAD), lambda c: (c, 0)),
                  pl.BlockSpec((1, 128), lambda c: (0, 0)), pl.BlockSpec((1, 128), lambda c: (0, 0)),
                  pl.BlockSpec((1, DI), lambda c: (0, 0)),
                  pl.BlockSpec((Q, Q), lambda c: (0, 0)), pl.BlockSpec((256, DI), lambda c: (0, 0))],
        out_specs=[pl.BlockSpec((Q, DI), lambda c: (c, 0)), pl.BlockSpec((1, NS, DI), lambda c: (c, 0, 0))],
        out_shape=[jax.ShapeDtypeStruct((L, DI), F32), jax.ShapeDtypeStruct((nc, NS, DI), F32)],
        scratch_shapes=[pltpu.VMEM((NS, DI), F32)],
        compiler_params=_params(("arbitrary",), VMEM_BIG))(xbc, proj, dtb, arow, dsk_x, tri, expand)


def _ssd_bwd(dy, xbc, proj, hs, dtb, arow, dsk_x, dproj, *, name):
    L = xbc.shape[0]
    nc = L // Q
    tri, triu, expand = _ssd_consts()

    def body(dy_ref, xbc_ref, dtw_ref, hs_ref, dtb_ref, arow_ref, dsk_ref, t_ref, u_ref, e_ref, dp_in,
             dxbc_ref, ddtw_ref, da_ref, ddx_ref, ddtb_ref, dh_scr):
        del dp_in
        i = pl.program_id(0)

        @pl.when(i == 0)
        def _():
            dh_scr[...] = jnp.zeros_like(dh_scr)

        pre, dt, acs, acs_t, acs_x, dt_x, xs = _ssd_common(xbc_ref, dtw_ref, dtb_ref, arow_ref, t_ref, e_ref)
        dyv = dy_ref[...]
        xdt = xs * dt_x
        eacs = jnp.exp(acs_x)
        acs_last = acs_x[Q - 1:Q, :]
        dec = jnp.exp(acs_last - acs_x)
        gy = dyv * eacs
        causal = lax.broadcasted_iota(jnp.int32, (Q, Q), 0) >= lax.broadcasted_iota(jnp.int32, (Q, Q), 1)
        first = lax.broadcasted_iota(jnp.int32, (Q, 128), 1) < HP
        lane_h = lax.broadcasted_iota(jnp.int32, (Q, 128), 1)
        sub_h = lax.broadcasted_iota(jnp.int32, (128, Q), 0)
        last_row = lax.broadcasted_iota(jnp.int32, (Q, GW), 0) == Q - 1
        dacs = jnp.zeros((Q, 128), F32)
        dacs_t = jnp.zeros((128, Q), F32)
        ddt = jnp.zeros((Q, 128), F32)
        for g in range(NG):
            bg = xbc_ref[:, DI + g * NS:DI + (g + 1) * NS]
            cg = xbc_ref[:, DI + NG * NS + g * NS:DI + NG * NS + (g + 1) * NS]
            s = _dot(cg, bg, NT)
            sl = slice(g * GW, (g + 1) * GW)
            hg = hs_ref[0, :, sl].astype(F32)
            dhn = dh_scr[:, sl]
            eal = eacs[Q - 1:Q, sl]
            gg = gy[:, sl]
            dax = gg * _dot(cg, hg, NN)
            dcg = _dot(gg, hg, NT)
            dh_scr[:, sl] = _dot(cg, gg, TN) + dhn * eal
            dal = eal * _colsum(dhn * hg)
            xdd = xdt[:, sl] * dec[:, sl]
            dbg = _dot(xdd, dhn, NT)
            wv = _dot(bg, dhn, NN)
            dd = wv * xdd
            dax = dax - dd
            dal = dal + _colsum(dd)
            dax = dax + jnp.where(last_row, dal, 0.0)
            dxdt_g = wv * dec[:, sl]
            ds = jnp.zeros((Q, Q), F32)
            dxdt_blocks = []
            for j in range(4):
                lo = g * GW + j * 128
                xb = xdt[:, lo:lo + 128]
                dyb = dyv[:, lo:lo + 128]
                dxb = dxdt_g[:, j * 128:(j + 1) * 128]
                for e in range(2):
                    h = g * 8 + j * 2 + e
                    lm = jnp.exp(jnp.where(causal, acs[:, h:h + 1] - acs_t[h:h + 1, :], NEG))
                    m = s * lm
                    dym = jnp.where(first if e == 0 else jnp.logical_not(first), dyb, 0.0)
                    dm = _dot(dym, xb, NT)
                    r = dm * m
                    dacs = dacs + jnp.where(lane_h == h, jnp.sum(r, axis=1, keepdims=True), 0.0)
                    dacs_t = dacs_t + jnp.where(sub_h == h, _colsum(r), 0.0)
                    ds = ds + dm * lm
                    dxb = dxb + _dot(m, dym, TN)
                dxdt_blocks.append(dxb)
            dxdt = jnp.concatenate(dxdt_blocks, axis=1)
            dcg = dcg + _dot(ds, bg, NN)
            dbg = dbg + _dot(ds, cg, TN)
            dxbc_ref[:, DI + g * NS:DI + (g + 1) * NS] = dbg
            dxbc_ref[:, DI + NG * NS + g * NS:DI + NG * NS + (g + 1) * NS] = dcg
            dxbc_ref[:, sl] = dsk_ref[:, sl] * dyv[:, sl] + dxdt * dt_x[:, sl]
            ddt_g, dacs_g = _reduce_heads([dxdt * xs[:, sl], dax], e_ref[0:128, sl])
            ddt = ddt + ddt_g
            dacs = dacs + dacs_g
        dacs = dacs - dacs_t.T
        ddta = _tri_sum(u_ref, dacs)
        ddt = ddt + ddta * arow_ref[...]
        ddtw = jnp.where(lane_h < NH, ddt * _sigmoid(pre), 0.0)
        ddtw_ref[...] = jnp.concatenate([ddtw, jnp.zeros((Q, DT_PAD - 128), F32)], axis=1).astype(ddtw_ref.dtype)
        _acc_out(da_ref, _colsum(ddta * dt), i)
        _acc_out(ddx_ref, _colsum(dyv * xs), i)
        _acc_out(ddtb_ref, _colsum(ddtw), i)

    rev = lambda c: (nc - 1 - c, 0)
    const = lambda c: (0, 0)
    return pl.pallas_call(
        body, name=name, grid=(nc,),
        in_specs=[pl.BlockSpec((Q, DI), rev), pl.BlockSpec((Q, XBC), rev),
                  pl.BlockSpec((Q, DT_PAD), rev),
                  pl.BlockSpec((1, NS, DI), lambda c: (nc - 1 - c, 0, 0)),
                  pl.BlockSpec((1, 128), const), pl.BlockSpec((1, 128), const), pl.BlockSpec((1, DI), const),
                  pl.BlockSpec((Q, Q), const), pl.BlockSpec((Q, Q), const), pl.BlockSpec((256, DI), const),
                  pl.BlockSpec(memory_space=pl.ANY)],
        out_specs=[pl.BlockSpec((Q, XBC), rev),
                   pl.BlockSpec((Q, DT_PAD), lambda c: (nc - 1 - c, C_DT // DT_PAD)),
                   pl.BlockSpec((1, 128), const), pl.BlockSpec((1, DI), const), pl.BlockSpec((1, 128), const)],
        out_shape=[jax.ShapeDtypeStruct((L, XBC), F32), jax.ShapeDtypeStruct((L, NPROJ), BF16),
                   jax.ShapeDtypeStruct((1, 128), F32), jax.ShapeDtypeStruct((1, DI), F32),
                   jax.ShapeDtypeStruct((1, 128), F32)],
        scratch_shapes=[pltpu.VMEM((NS, DI), F32)],
        input_output_aliases={10: 1},
        compiler_params=_params(("arbitrary",), VMEM_BIG))(dy, xbc, proj, hs, dtb, arow, dsk_x, tri, triu,
                                                          expand, dproj)


def _adam_update(wv, gv, mv, vv):
    c1 = 1.0 - ADAM_B1 ** ADAM_STEP
    c2 = 1.0 - ADAM_B2 ** ADAM_STEP
    mn = ADAM_B1 * mv + (1.0 - ADAM_B1) * gv
    vn = ADAM_B2 * vv + (1.0 - ADAM_B2) * (gv * gv)
    return -ADAM_LR * ((mn / c1) / (jnp.sqrt(vn / c2) + ADAM_EPS) + ADAM_WD * wv), mn, vn


def _adamw(w, g, m, v, *, name, tr=None):
    R = w.shape[0]
    rest = tuple(w.shape[1:])
    if tr is None:
        tr = _pick(R, (256, 128, 64, 32, 16, 8))
    assert R % tr == 0

    def body(w_ref, g_ref, m_ref, v_ref, d_ref, mo_ref, vo_ref):
        d_ref[...], mo_ref[...], vo_ref[...] = _adam_update(w_ref[...], g_ref[...], m_ref[...], v_ref[...])

    zeros = (0,) * len(rest)
    spec = pl.BlockSpec((tr,) + rest, lambda i: (i,) + zeros)
    return pl.pallas_call(body, name=name, grid=(R // tr,), in_specs=[spec] * 4, out_specs=[spec] * 3,
                          out_shape=[jax.ShapeDtypeStruct(w.shape, F32)] * 3,
                          compiler_params=_params(("parallel",)))(w, g, m, v)


def _adamw_small(svrow, g_conv, params, *, name):
    n = len(params)

    def body(*refs):
        sv_ref, gc_ref = refs[0], refs[1]
        ins, outs = refs[2:2 + 3 * n], refs[2 + 3 * n:]
        for p, (key, w, _, _) in enumerate(params):
            w_ref, m_ref, v_ref = ins[3 * p:3 * p + 3]
            g_ref, d_ref, mo_ref, vo_ref = outs[4 * p:4 * p + 4]
            gv = gc_ref[...] if key == "conv_w" else sv_ref[:, SV_OFF[key]:SV_OFF[key] + w.shape[1]]
            g_ref[...] = gv
            d_ref[...], mo_ref[...], vo_ref[...] = _adam_update(w_ref[...], gv, m_ref[...], v_ref[...])

    vm = pl.BlockSpec(memory_space=pltpu.VMEM)
    args = [svrow, g_conv]
    shapes = []
    for _, w, m, v in params:
        args += [w, m, v]
        shapes += [jax.ShapeDtypeStruct(w.shape, F32)] * 4
    res = pl.pallas_call(body, name=name, in_specs=[vm] * len(args), out_specs=[vm] * len(shapes),
                         out_shape=shapes)(*args)
    return {key: tuple(res[4 * p:4 * p + 4]) for p, (key, _, _, _) in enumerate(params)}


def _slab_sum(recv, *, tile, name):
    rows = recv.shape[1]
    assert rows % tile == 0 and tile % 16 == 0

    def body(r_ref, o_ref):
        acc = r_ref[0].astype(F32)
        for j in range(1, N_DEV):
            acc = acc + r_ref[j].astype(F32)
        o_ref[...] = acc

    return pl.pallas_call(body, name=name, grid=(rows // tile,),
                          in_specs=[pl.BlockSpec((N_DEV, tile, D), lambda i: (0, i, 0))],
                          out_specs=pl.BlockSpec((tile, D), lambda i: (i, 0)),
                          out_shape=jax.ShapeDtypeStruct((rows, D), F32),
                          compiler_params=_params(("parallel",)))(recv)


MESH = pl.DeviceIdType.MESH


def _coords():
    return lax.axis_index("x"), lax.axis_index("y"), lax.axis_index("c")


def _peer(k):
    x, y, c = _coords()
    px = 1 - x if k & 4 else x
    py = 1 - y if k & 2 else y
    pc = 1 - c if k & 1 else c
    return (px, py, pc), 4 * px + 2 * py + pc


def _rcopy(src, dst, ssem, rsem, dev):
    return pltpu.make_async_remote_copy(src_ref=src, dst_ref=dst, send_sem=ssem, recv_sem=rsem,
                                        device_id=dev, device_id_type=MESH)


def _exchange_all(src_of, dst_slot, send_sems, recv_sems):
    x, y, c = _coords()
    me = 4 * x + 2 * y + c
    sent = []
    for k in range(1, N_DEV):
        dev, pidx = _peer(k)
        cp = _rcopy(src_of(pidx), dst_slot(me), send_sems.at[k - 1], recv_sems.at[k - 1], dev)
        cp.start()
        sent.append(cp)
    for k in range(1, N_DEV):
        dev, pidx = _peer(k)
        _rcopy(src_of(pidx), dst_slot(pidx), send_sems.at[k - 1], recv_sems.at[k - 1], dev).wait_recv()
    for cp in sent:
        cp.wait_send()


def _rows_of_slots(buf, nslots):
    rows = lax.broadcasted_iota(jnp.int32, (8, buf.shape[-1]), 0)
    out = jnp.zeros((8, buf.shape[-1]), F32)
    for j in range(nslots):
        out = out + jnp.where(rows == j, buf[j], 0.0)
    return out


def _ada_fwd(c, w_ada, b_r, *, name):
    wloc = w_ada.shape[1]

    def body(c_ref, w_ref, b_ref, mod_ref, call_ref, csrc, cbuf, psrc, pbuf, s1, r1, s2, r2):
        x, y, cc = _coords()
        me = 4 * x + 2 * y + cc
        csrc[...] = jnp.broadcast_to(c_ref[...], (8, D))
        cbuf[me] = csrc[...]
        _exchange_all(lambda p: csrc, lambda s: cbuf.at[s], s1, r1)
        call = _rows_of_slots(cbuf, N_DEV)
        call_ref[...] = call
        prod = _dot_hi(_silu(call), w_ref[...])
        for b in range(N_DEV):
            psrc[b] = jnp.broadcast_to(prod[b:b + 1, :], (8, wloc))
        pbuf[me] = psrc[me]
        _exchange_all(lambda p: psrc.at[p], lambda s: pbuf.at[s], s2, r2)
        mod_ref[...] = _rows_of_slots(pbuf, N_DEV) + b_ref[...]

    vm = pl.BlockSpec(memory_space=pltpu.VMEM)
    return pl.pallas_call(
        body, name=name, in_specs=[vm, vm, vm], out_specs=[vm, vm],
        out_shape=[jax.ShapeDtypeStruct((N_DEV, wloc), F32), jax.ShapeDtypeStruct((N_DEV, D), F32)],
        scratch_shapes=[pltpu.VMEM((8, D), F32), pltpu.VMEM((N_DEV, 8, D), F32),
                        pltpu.VMEM((N_DEV, 8, wloc), F32), pltpu.VMEM((N_DEV, 8, wloc), F32),
                        pltpu.SemaphoreType.DMA((N_DEV - 1,)), pltpu.SemaphoreType.DMA((N_DEV - 1,)),
                        pltpu.SemaphoreType.DMA((N_DEV - 1,)), pltpu.SemaphoreType.DMA((N_DEV - 1,))],
        compiler_params=pltpu.CompilerParams(vmem_limit_bytes=VMEM_BIG))(c, w_ada, b_r)


def _gather_slabs(slab, *, name):
    def body(x_ref, out_ref, send_sems, recv_sems, local_sem):
        x, y, c = _coords()
        me, sibling = (x, y, c), (x, y, 1 - c)
        chips = [(1 - x, y), (x, 1 - y), (1 - x, 1 - y)]

        def slot(px, py, pc):
            return out_ref.at[4 * px + 2 * py + pc]

        def copy(k, block, to, src=None):
            return _rcopy(slot(*block) if src is None else src, slot(*block), send_sems.at[k], recv_sems.at[k], to)

        mine = pltpu.make_async_copy(x_ref, slot(*me), local_sem)
        mine.start()
        first = [copy(0, me, sibling, src=x_ref)]
        first += [copy(1 + j, me, (*chip, c), src=x_ref) for j, chip in enumerate(chips)]
        for cp in first:
            cp.start()
        passed = [copy(4 + j, (*chip, c), sibling) for j, chip in enumerate(chips)]
        for j, chip in enumerate(chips):
            copy(1 + j, (*chip, c), me).wait_recv()
            passed[j].start()
        copy(0, sibling, me).wait_recv()
        for j, chip in enumerate(chips):
            copy(4 + j, (*chip, 1 - c), me).wait_recv()
        for cp in first + passed:
            cp.wait_send()
        mine.wait()

    anyspec = pl.BlockSpec(memory_space=pl.ANY)
    return pl.pallas_call(
        body, name=name, in_specs=[anyspec], out_specs=anyspec,
        out_shape=jax.ShapeDtypeStruct((N_DEV,) + slab.shape, slab.dtype),
        scratch_shapes=[pltpu.SemaphoreType.DMA((7,)), pltpu.SemaphoreType.DMA((7,)), pltpu.SemaphoreType.DMA],
    )(slab)


_HBM =pl.BlockSpec(memory_space=pltpu.HBM)
_SEM = pl.BlockSpec(memory_space=pltpu.SEMAPHORE)
_EFFECT = pltpu.SideEffectType.DATAFLOW_SIDE_EFFECTING


def _xchg_src(src_ref, pidx, per_peer):
    return src_ref.at[pidx] if per_peer else src_ref


def _xchg_start(src, *, per_peer, name):
    rows = src.shape[-2]
    land_shape = (N_DEV, rows, D)

    def body(src_ref, land_ref, send_sems, recv_sems, src_thru, land_thru, token):
        del src_thru, land_thru
        x, y, c = _coords()
        me = 4 * x + 2 * y + c
        for k in range(1, N_DEV):
            dev, pidx = _peer(k)
            _rcopy(_xchg_src(src_ref, pidx, per_peer), land_ref.at[me], send_sems.at[k - 1],
                   recv_sems.at[k - 1], dev).start()
        token[...] = jnp.zeros_like(token)

    return pl.pallas_call(
        body, name=name,
        out_shape=(pltpu.SemaphoreType.DMA((N_DEV - 1,)), pltpu.SemaphoreType.DMA((N_DEV - 1,)),
                   pltpu.HBM(src.shape, src.dtype), pltpu.HBM(land_shape, src.dtype),
                   jax.ShapeDtypeStruct((8, 128), F32)),
        in_specs=(_HBM, _HBM),
        out_specs=(_SEM, _SEM, _HBM, _HBM, pl.BlockSpec(memory_space=pltpu.VMEM)),
        input_output_aliases={0: 2, 1: 3},
        compiler_params=pltpu.CompilerParams(has_side_effects=_EFFECT),
    )(pltpu.with_memory_space_constraint(src, pltpu.HBM),
      pltpu.with_memory_space_constraint(lax.empty(land_shape, src.dtype), pltpu.HBM))


def _xchg_wait(started, after, *, per_peer, name):
    send_sems, recv_sems, src_thru, land_thru, _ = started

    def body(src_ref, land_ref, send_sems, recv_sems, after_ref, src_dead, got_ref):
        del after_ref, src_dead, got_ref
        for k in range(1, N_DEV):
            dev, pidx = _peer(k)
            cp = _rcopy(_xchg_src(src_ref, pidx, per_peer), land_ref.at[pidx], send_sems.at[k - 1],
                        recv_sems.at[k - 1], dev)
            cp.wait_send()
            cp.wait_recv()

    return pl.pallas_call(
        body, name=name,
        out_shape=(pltpu.HBM(src_thru.shape, src_thru.dtype), pltpu.HBM(land_thru.shape, land_thru.dtype)),
        in_specs=(_HBM, _HBM, _SEM, _SEM, pl.BlockSpec(memory_space=pl.ANY)),
        out_specs=(_HBM, _HBM),
        input_output_aliases={0: 0, 1: 1},
        compiler_params=pltpu.CompilerParams(has_side_effects=_EFFECT),
    )(src_thru, land_thru, send_sems, recv_sems, after)


def _dep(token):
    return (token, (8, 128), lambda i, j, k: (0, 0))


def _small_allsum(sv, *, name):
    def body(sv_ref, all_ref, sum_ref, send_sems, recv_sems):
        x, y, c = _coords()
        me = 4 * x + 2 * y + c
        all_ref[me] = sv_ref[...]
        _exchange_all(lambda p: sv_ref, lambda s: all_ref.at[s], send_sems, recv_sems)
        acc = all_ref[0]
        for j in range(1, N_DEV):
            acc = acc + all_ref[j]
        sum_ref[...] = acc

    vm = pl.BlockSpec(memory_space=pltpu.VMEM)
    return pl.pallas_call(
        body, name=name, in_specs=[vm], out_specs=[vm, vm],
        out_shape=[jax.ShapeDtypeStruct((N_DEV, SV_ROWS, 128), F32), jax.ShapeDtypeStruct((SV_ROWS, 128), F32)],
        scratch_shapes=[pltpu.SemaphoreType.DMA((7,)), pltpu.SemaphoreType.DMA((7,))],
    )(sv)


def _ada_bwd(call, dmod_loc, *, name):
    wloc = dmod_loc.shape[1]

    def body(c_ref, d_ref, o_ref):
        o_ref[...] = _dot_hi(_silu(c_ref[...]), d_ref[...], TN)

    vm = pl.BlockSpec(memory_space=pltpu.VMEM)
    return pl.pallas_call(body, name=name, in_specs=[vm, vm], out_specs=vm,
                          out_shape=jax.ShapeDtypeStruct((D, wloc), F32),
                          compiler_params=pltpu.CompilerParams(vmem_limit_bytes=VMEM_BIG))(call, dmod_loc)


def _pad_rows(a, rows):
    return jnp.pad(a, ((0, rows - a.shape[0]), (0, 0)))


IN_SHIFT = tuple((IN_ROWS * j) % 16 for j in range(N_DEV))
IN_BASE = tuple(IN_ROWS * j - IN_SHIFT[j] for j in range(N_DEV))
IN_SEGMENTS = ((2048, XBC, C_XBC), (5152, 1024, C_POOL), (0, 2048, C_Z), (6176, 2048, C_GATE), (5120, 32, C_DT))


def _global_pieces(gs):
    pieces = []
    for j in range(N_DEV):
        lo, hi = 0, IN_ROWS_P
        if j > 0 and IN_BASE[j - 1] + IN_ROWS_P > IN_BASE[j]:
            pieces.append((IN_BASE[j], 16, gs[j - 1, IN_ROWS_P - 16:IN_ROWS_P] + gs[j, 0:16]))
            lo = 16
        if j + 1 < N_DEV and IN_BASE[j] + IN_ROWS_P > IN_BASE[j + 1]:
            hi = IN_ROWS_P - 16
        pieces.append((IN_BASE[j] + lo, hi - lo, gs[j, lo:hi]))
    return pieces


def _reorder_in_rows(gs):
    pieces = _global_pieces(gs)
    parts = []
    for lo, n, _ in IN_SEGMENTS:
        for p0, pn, arr in pieces:
            a, b = max(lo, p0), min(lo + n, p0 + pn)
            if a < b:
                parts.append(arr[a - p0:b - p0])
    parts.append(jnp.zeros((DT_PAD - 32, D), gs.dtype))
    return jnp.concatenate(parts, axis=0)


def _restore_in_shards(d):
    slabs = []
    for j in range(N_DEV):
        parts = []
        r, end = IN_BASE[j], IN_BASE[j] + IN_ROWS_P
        while r < end:
            lo, n, new = next(s for s in IN_SEGMENTS if s[0] <= r < s[0] + s[1])
            e = min(end, lo + n)
            parts.append(d[new + r - lo:new + e - lo])
            r = e
        slabs.append(jnp.concatenate(parts, axis=0))
    return jnp.stack(slabs, axis=0)


def _pack_sv(parts):
    flat = []
    for n, size in SV_PARTS:
        v = parts[n].reshape(-1).astype(F32)
        flat.append(jnp.pad(v, (0, size - v.shape[0])))
    v = jnp.concatenate(flat)
    return jnp.pad(v, (0, SV_ROWS * 128 - v.shape[0])).reshape(SV_ROWS, 128)


def _sv_get(flat, n, size):
    return flat[SV_OFF[n]:SV_OFF[n] + size]


def kernel(x, c, w_ada, b_ada, norm_mix_w, w_in, conv_w, conv_b, dt_bias, a_log, d_skip, ssd_norm_w, w_branch_ssd, pool_w, pool_scale, w_branch_pool, w_out, norm_mlp_w, w_up, w_down, norm_final_w, loss_target, m_w_ada, m_b_ada, m_norm_mix_w, m_w_in, m_conv_w, m_conv_b, m_dt_bias, m_a_log, m_d_skip, m_ssd_norm_w, m_w_branch_ssd, m_pool_w, m_pool_scale, m_w_branch_pool, m_w_out, m_norm_mlp_w, m_w_up, m_w_down, m_norm_final_w, v_w_ada, v_b_ada, v_norm_mix_w, v_w_in, v_conv_w, v_conv_b, v_dt_bias, v_a_log, v_d_skip, v_ssd_norm_w, v_w_branch_ssd, v_pool_w, v_pool_scale, v_w_branch_pool, v_w_out, v_norm_mlp_w, v_w_up, v_w_down, v_norm_final_w):
    xs_ = x[0]
    tgt = loss_target[0]
    L = xs_.shape[0]
    me = 4 * lax.axis_index("x") + 2 * lax.axis_index("y") + lax.axis_index("c")
    wloc = w_ada.shape[2]

    mod_p, c_all = _ada_fwd(c, w_ada[0], b_ada.reshape(N_DEV, wloc), name="ada_fwd")
    mod = mod_p.reshape(6, D)
    shift_m, scale_m, gate_m, shift_f, scale_f, gate_f = [mod[i:i + 1] for i in range(6)]

    conv_bits = lax.bitcast_convert_type(conv_w[0], SLAB_DT).reshape(3, D)
    in_shift = (IN_ROWS * me) % 16
    slab_in = lax.dynamic_update_slice(jnp.zeros((IN_ROWS_P, D), SLAB_DT), w_in[0].T.astype(SLAB_DT),
                                       (in_shift, 0))
    slab_in = jnp.concatenate([slab_in, _pad_rows(conv_bits, CONV_ROWS)], axis=0)
    slab_rest = jnp.concatenate([
        w_branch_ssd[0].astype(SLAB_DT),
        pool_w[0].reshape(32, D).astype(SLAB_DT),
        w_branch_pool[0].astype(SLAB_DT),
        w_out[0].astype(SLAB_DT),
        w_up[0].T.astype(SLAB_DT),
        w_down[0].astype(SLAB_DT)], axis=0)
    slab_in, mod_p = lax.optimization_barrier((slab_in, mod_p))
    gs_in = _gather_slabs(slab_in, name="gather_w_in")
    slab_rest, gs_in = lax.optimization_barrier((slab_rest, gs_in))
    rest_started = _xchg_start(slab_rest, per_peer=False, name="gather_rest_start")
    gather_token = rest_started[4]

    w_in_t = _reorder_in_rows(gs_in)
    conv_full = lax.bitcast_convert_type(
        gs_in[:, IN_ROWS_P:IN_ROWS_P + 3].reshape(N_DEV, 4, XBC // N_DEV, 2), F32)
    conv_full = conv_full.transpose(1, 0, 2).reshape(4, XBC)

    dtb = jnp.pad(dt_bias, ((0, 0), (0, 128 - NH)))
    arow = jnp.pad(-jnp.exp(a_log), ((0, 0), (0, 128 - NH)))
    dsk_x = jnp.repeat(d_skip, HP, axis=1)

    tm = _pick(L, (1024, 512, 256, 128))
    tm2 = _pick(L, (2048, 1024, 512, 256, 128))
    tkl = _pick(L, (4096, 2048, 1024, 512, 256, 128))
    tkl2 = _pick(L, (2048, 1024, 512, 256, 128))

    tmh = _pick(L, (512, 256, 128))
    zcol = C_Z // DI
    gcol = C_GATE // (2 * D)

    def whole_rows(w):
        return lambda t: ((L, w), BF16, (t, w), lambda i, j, k: (i, 0))

    def norm1_pro(x_ref, ex, outs, j):
        @pl.when(j == 0)
        def _():
            xv = x_ref[...]
            r = lax.rsqrt(jnp.mean(xv * xv, axis=-1, keepdims=True) + EPS)
            outs[1][...] = (xv * r * ex[0][...] * (1.0 + ex[1][...]) + ex[2][...]).astype(outs[1].dtype)

        return outs[1][...]

    def proj_ep(acc, ex, outs):
        outs[0][...] = acc

        @pl.when(pl.program_id(1) == NPROJ // 768 - 1)
        def _():
            outs[2][...] = acc[:, 768 - DT_PAD:]

    proj, h1, dtp = _mm(
        xs_, w_in_t, "nt", name="in_proj", tm=tm2, tn=768, tk=D,
        extras=[(norm_mix_w, *_vecs()), (scale_m, *_vecs()), (shift_m, *_vecs()), _dep(gather_token)],
        outs=[F32, whole_rows(D)(tm2), ((L, DT_PAD), F32, (tm2, DT_PAD), lambda i, j, k: (i, 0))],
        prologue=norm1_pro, epilogue=proj_ep)
    xbc_raw = proj
    xbc = _conv_fwd(xbc_raw, conv_full, conv_b, name="conv_fwd")
    y_ssm, hs = _ssd_fwd(xbc, dtp, dtb, arow, dsk_x, name="ssd_fwd")

    slab_rest, gs = _xchg_wait(rest_started, y_ssm, per_peer=False, name="gather_rest_wait")
    gs = lax.dynamic_update_slice(gs, slab_rest[None], (me, 0, 0))

    def part(n, rows):
        return gs[:, REST_OFF[n]:REST_OFF[n] + rows]

    w_bssd = part("bssd", 256).reshape(DI, D)
    w_pool = part("pool", 32).reshape(N_DEV, 4, 32, PGW).transpose(1, 0, 2, 3).reshape(POOL_W, PGW)
    w_bpool = part("bpool", 128).reshape(POOL_W, D)
    w_o = part("out", 128).reshape(D, D)
    w_up_t = part("up", 512).reshape(DFF, D)
    w_dn = part("down", 512).reshape(DFF, D)

    def gnorm_pro(y_ref, ex, outs, j):
        z_ref, w_ref = ex
        yg = y_ref[...] * _silu(z_ref[...].astype(F32))
        segs = []
        for k in range(NG):
            sl = slice(k * GW, (k + 1) * GW)
            seg = yg[:, sl]
            r = lax.rsqrt(jnp.mean(seg * seg, axis=-1, keepdims=True) + EPS)
            segs.append((seg * r * w_ref[:, sl]).astype(BF16))
        yn_v = jnp.concatenate(segs, axis=1)
        outs[1][...] = yn_v
        return yn_v

    y_ssd, yn = _mm(y_ssm, w_bssd, "nn", name="branch_ssd", tm=tmh, tn=D, tk=DI,
                    extras=[(proj, *_rows(tmh, DI, zcol)), (ssd_norm_w, *_vecs(DI))],
                    outs=[F32, whole_rows(DI)(tmh)], prologue=gnorm_pro)
    pooled = _pool_fwd(proj, name="pool_fwd")
    yp0, yp1 = _mm_pool(pooled, w_pool, pool_scale, name="pool_mix", tm=tm, transpose_w=False)
    y_pool = _mm(yp1, w_bpool, "nn", name="branch_pool", outs=[F32], tm=tm2, tn=D, tk=D)

    def merge_pro(a_ref, ex, outs, j):
        s = _sigmoid(ex[1][...].astype(F32))
        mv = (s[:, :D] * a_ref[...] + s[:, D:] * ex[0][...]).astype(BF16)
        outs[3][...] = mv
        return mv

    mix, x1, h2, m = _mm(y_ssd, w_o, "nn", name="out_proj", tm=tmh, tn=D, tk=D,
                         extras=[(y_pool, *_rows(tmh)), (proj, *_rows(tmh, 2 * D, gcol)),
                                 (xs_, *_rows(tmh)), (gate_m, *_vecs()), (norm_mlp_w, *_vecs()),
                                 (scale_f, *_vecs()), (shift_f, *_vecs())],
                         outs=[BF16, F32, BF16, whole_rows(D)(tmh)], prologue=merge_pro,
                         epilogue=lambda acc, ex, outs: _ep_resid_norm(acc, ex[2:], outs[:3]))

    def relu2(acc, ex, outs):
        r = jnp.maximum(acc, 0.0)
        outs[0][...] = acc.astype(BF16)
        outs[1][...] = (r * r).astype(BF16)

    up, act = _mm(h2, w_up_t, "nt", name="mlp_up", outs=[BF16, BF16], tm=tm2, tn=1024, tk=D, epilogue=relu2)

    dx2, ddown, loss_p, dnwf, dgate_f = _mm(
        act, w_dn, "nn", name="mlp_down", tm=tmh, tn=D, tk=DFF,
        extras=[(x1, *_rows(tmh)), (tgt, *_rows(tmh)), (gate_f, *_vecs()), (norm_final_w.reshape(1, D), *_vecs())],
        outs=[F32, BF16, _sum_out(128), _sum_out(), _sum_out()], epilogue=_ep_final)

    def drelu2(acc, ex, outs):
        outs[0][...] = (acc * (2.0 * jnp.maximum(ex[0][...].astype(F32), 0.0))).astype(BF16)

    def dep_last(ep):
        return lambda acc, ex, outs: ep(acc, ex[:-1], outs)

    dup = _mm(ddown, w_dn, "nt", name="mlp_down_dx", outs=[BF16], tm=tm2, tn=1024, tk=D,
              extras=[(up, (tm2, 1024), lambda i, j, k: (i, j))], epilogue=drelu2)
    g_dn = _mm(act, ddown, "tn", name="mlp_down_dw", outs=[SLAB_DT], tm=1024, tn=D, tk=tkl)
    g_up_t = _mm(dup, h2, "tn", name="mlp_up_dw", outs=[SLAB_DT], tm=1024, tn=D, tk=tkl)
    gslab_mlp = jnp.concatenate([g_up_t.reshape(N_DEV, 512, D), g_dn.reshape(N_DEV, 512, D)], axis=1)
    mlp_started = _xchg_start(gslab_mlp, per_peer=True, name="scatter_mlp_start")
    dx1, p2, q2, dmix, dgate_m = _mm(
        dup, w_up_t, "nn", name="mlp_up_dx", tm=tmh, tn=D, tk=DFF,
        extras=[(x1, *_rows(tmh)), (dx2, *_rows(tmh)), (norm_mlp_w, *_vecs()), (scale_f, *_vecs()),
                (mix, *_rows(tmh)), (gate_m, *_vecs()), _dep(mlp_started[4])],
        outs=[F32, _sum_out(), _sum_out(), BF16, _sum_out()], epilogue=dep_last(_ep_norm_bwd))
    gcol = C_GATE // (2 * D)
    dy_ssd, dy_pool, dproj = _mm(
        dmix, w_o, "nt", name="out_proj_dx", tm=tmh, tn=D, tk=D,
        extras=[(y_ssd, *_rows(tmh)), (y_pool, *_rows(tmh)), (proj, *_rows(tmh, 2 * D, gcol))],
        outs=[BF16, BF16, ((L, NPROJ), BF16, *_rows(tmh, 2 * D, gcol))], epilogue=_ep_merge_bwd)
    g_o = _mm(m, dmix, "tn", name="out_proj_dw", outs=[SLAB_DT], tm=D, tn=D, tk=tkl)
    zcol = C_Z // DI
    dy_ssm, dproj, d_snw = _mm(
        dy_ssd, w_bssd, "nt", name="branch_ssd_dx", tm=tmh, tn=DI, tk=D,
        extras=[(y_ssm, *_rows(tmh, DI)), (proj, *_rows(tmh, DI, zcol)), (ssd_norm_w, *_vecs(DI)),
                (dproj, None, None)],
        outs=[F32, ((L, NPROJ), BF16, *_rows(tmh, DI, zcol)), _sum_out(DI)],
        epilogue=_ep_gated_norm_bwd, aliases={3: 1})
    g_bssd = _mm(yn, dy_ssd, "tn", name="branch_ssd_dw", outs=[SLAB_DT], tm=1024, tn=D, tk=tkl)
    dxbc, dproj, d_a, d_dx, d_dtb = _ssd_bwd(dy_ssm, xbc, dtp, hs, dtb, arow, dsk_x, dproj, name="ssd_bwd")
    dproj, d_cw, d_cb = _conv_bwd(xbc_raw, dxbc, conv_full, conv_b, dproj, name="conv_bwd")
    dyp0, d_ps = _mm(dy_pool, w_bpool, "nt", name="branch_pool_dx", tm=tm, tn=D, tk=D,
                     extras=[(yp0, *_rows(tm)), (pool_scale, *_vecs())],
                     outs=[BF16, _sum_out()], epilogue=_ep_pscale_bwd)
    g_bpool = _mm(yp1, dy_pool, "tn", name="branch_pool_dw", outs=[SLAB_DT], tm=D, tn=D, tk=tkl)
    dpooled = _mm_pool(dyp0, w_pool, None, name="pool_mix_dx", tm=tm, transpose_w=True)
    g_pool = _mm_pool_tn(pooled, dyp0, name="pool_mix_dw", tk=tkl)
    gslab_mix = jnp.concatenate([
        g_bssd.reshape(N_DEV, 256, D),
        g_pool.reshape(4, N_DEV, 32, PGW).transpose(1, 0, 2, 3).reshape(N_DEV, 32, D).astype(SLAB_DT),
        g_bpool.reshape(N_DEV, 128, D),
        g_o.reshape(N_DEV, 128, D)], axis=1)
    mix_started = _xchg_start(gslab_mix, per_peer=True, name="scatter_mix_start")
    dproj = _pool_bwd(dpooled, dproj, name="pool_bwd")
    g_in_t = _mm(dproj, h1, "tn", name="in_proj_dw", outs=[SLAB_DT], tm=1408, tn=D, tk=tkl2,
                 extras=[_dep(mix_started[4])])
    gslab_in = _restore_in_shards(g_in_t)
    in_started = _xchg_start(gslab_in, per_peer=True, name="scatter_in_start")
    grad_x, p1, q1 = _mm(
        dproj, w_in_t, "nn", name="in_proj_dx", tm=tmh, tn=D, tk=2816,
        extras=[(xs_, *_rows(tmh)), (dx1, *_rows(tmh)), (norm_mix_w, *_vecs()), (scale_m, *_vecs()),
                _dep(in_started[4])],
        outs=[F32, _sum_out(), _sum_out()], epilogue=dep_last(_ep_norm_bwd))

    def landed(started, after, tile, name):
        src, land = _xchg_wait(started, after, per_peer=True, name=name + "_wait")
        own = lax.dynamic_slice_in_dim(src, me, 1, axis=0)
        return _slab_sum(lax.dynamic_update_slice(land, own, (me, 0, 0)), tile=tile, name=name + "_sum")

    gsum_mlp = landed(mlp_started, grad_x, 256, "scatter_mlp")
    gsum_mix = landed(mix_started, grad_x, 272, "scatter_mix")
    gsum_in = landed(in_started, grad_x, 208, "scatter_in")

    dmod = jnp.concatenate([q1, p1 * norm_mix_w, dgate_m, q2, p2 * norm_mlp_w, dgate_f], axis=1)
    d_alog = d_a[:, :NH] * (-jnp.exp(a_log))
    sv = _pack_sv({
        "b_ada": dmod, "norm_mix_w": p1 * (1.0 + scale_m), "conv_b": d_cb, "dt_bias": d_dtb[:, :NH],
        "a_log": d_alog, "d_skip": d_dx.reshape(NH, HP).sum(axis=1), "ssd_norm_w": d_snw,
        "pool_scale": d_ps, "norm_mlp_w": p2 * (1.0 + scale_f), "norm_final_w": dnwf, "conv_w": d_cw,
        "loss": loss_p[:, :1]})
    sv_all, sv_sum = _small_allsum(sv, name="small_allsum")
    flat = sv_sum.reshape(-1)
    loss = flat[SV_OFF["loss"]]
    dmod_all = sv_all.reshape(N_DEV, SV_ROWS * 128)[:, :6 * D]
    g_w_ada = _ada_bwd(c_all, lax.dynamic_slice_in_dim(dmod_all, me * wloc, wloc, axis=1), name="ada_bwd")

    g_conv_w = lax.dynamic_slice_in_dim(_sv_get(flat, "conv_w", 4 * XBC).reshape(4, XBC),
                                        me * (XBC // N_DEV), XBC // N_DEV, axis=1)
    small = [("b_ada", b_ada, m_b_ada, v_b_ada), ("norm_mix_w", norm_mix_w, m_norm_mix_w, v_norm_mix_w),
             ("conv_b", conv_b, m_conv_b, v_conv_b), ("dt_bias", dt_bias, m_dt_bias, v_dt_bias),
             ("a_log", a_log, m_a_log, v_a_log), ("d_skip", d_skip, m_d_skip, v_d_skip),
             ("ssd_norm_w", ssd_norm_w, m_ssd_norm_w, v_ssd_norm_w),
             ("pool_scale", pool_scale, m_pool_scale, v_pool_scale),
             ("norm_mlp_w", norm_mlp_w, m_norm_mlp_w, v_norm_mlp_w),
             ("norm_final_w", norm_final_w[None], m_norm_final_w[None], v_norm_final_w[None]),
             ("conv_w", conv_w[0], m_conv_w[0], v_conv_w[0])]
    small_out = _adamw_small(sv_sum.reshape(1, SV_ROWS * 128), g_conv_w, small, name="adamw_small")
    small_out["norm_final_w"] = tuple(a[0] for a in small_out["norm_final_w"])
    small_out["conv_w"] = tuple(a[None] for a in small_out["conv_w"])

    def gpart(n, rows_):
        return gsum_mix[MIX_OFF[n]:MIX_OFF[n] + rows_]

    def lin(a):
        return a[0].T.reshape(IN_ROWS * 8, 128)

    g_lin = lax.dynamic_slice_in_dim(gsum_in, in_shift, IN_ROWS, axis=0).reshape(IN_ROWS * 8, 128)
    dlt, mn, vn = _adamw(lin(w_in), g_lin, lin(m_w_in), lin(v_w_in), name="adamw_w_in", tr=IN_ROWS * 2)
    big_in = tuple(a.reshape(IN_ROWS, D).T[None] for a in (g_lin, dlt, mn, vn))

    big = {
        "w_ada": (w_ada, m_w_ada, v_w_ada, g_w_ada, (D, wloc)),
        "w_branch_ssd": (w_branch_ssd, m_w_branch_ssd, v_w_branch_ssd, gpart("bssd", 256), (256, D)),
        "pool_w": (pool_w, m_pool_w, v_pool_w, gpart("pool", 32).reshape(128, PGW), (128, PGW)),
        "w_branch_pool": (w_branch_pool, m_w_branch_pool, v_w_branch_pool, gpart("bpool", 128), (128, D)),
        "w_out": (w_out, m_w_out, v_w_out, gpart("out", 128), (128, D)),
        "w_up": (w_up, m_w_up, v_w_up, gsum_mlp[:512].T, (D, 512)),
        "w_down": (w_down, m_w_down, v_w_down, gsum_mlp[512:], (512, D)),
    }
    big_out = {}
    for n, (w, mm_, vv, g, shp2) in big.items():
        dlt, mn, vn = _adamw(w.reshape(shp2), g, mm_.reshape(shp2), vv.reshape(shp2), name="adamw_" + n)
        big_out[n] = (g.reshape(w.shape), dlt.reshape(w.shape), mn.reshape(w.shape), vn.reshape(w.shape))

    order = ["w_ada", "b_ada", "norm_mix_w", "w_in", "conv_w", "conv_b", "dt_bias", "a_log", "d_skip",
             "ssd_norm_w", "w_branch_ssd", "pool_w", "pool_scale", "w_branch_pool", "w_out", "norm_mlp_w",
             "w_up", "w_down", "norm_final_w"]
    big_out["w_in"] = big_in
    res = {**small_out, **big_out}
    outs = [loss, grad_x.reshape(x.shape)]
    for k in range(4):
        outs += [res[n][k] for n in order]
    return tuple(outs)
```

```python
import functools

import numpy as np
import jax
import jax.numpy as jnp
from jax import lax
from jax.experimental import pallas as pl
from jax.experimental.pallas import tpu as pltpu

F32 = jnp.float32
BF16 = jnp.bfloat16
SLAB_DT = jnp.bfloat16
_MXU_DTYPE = jnp.bfloat16

N_DEV = 8
D = 1024
DI = 2048
NH = 32
HP = 64
NG = 4
NS = 128
Q = 128
XBC = DI + 2 * NG * NS
DFF = 4096
N_IN = 8224
EPS = 1e-5
POOL_W = 1024
PGW = 256

C_XBC, C_POOL, C_Z, C_GATE, C_DT = 0, 3072, 4096, 6144, 8192
DT_PAD = 256
NPROJ = C_DT + DT_PAD

IN_ROWS = N_IN // N_DEV
IN_ROWS_P = 1040
CONV_ROWS = 16
REST_PARTS = (("bssd", 256), ("pool", 32), ("bpool", 128), ("out", 128), ("up", 512), ("down", 512))
REST_OFF = {}
_o = 0
for _n, _r in REST_PARTS:
    REST_OFF[_n] = _o
    _o += _r
REST_ROWS = _o
MIX_PARTS = (("bssd", 256), ("pool", 32), ("bpool", 128), ("out", 128))
MIX_OFF = {}
_o = 0
for _n, _r in MIX_PARTS:
    MIX_OFF[_n] = _o
    _o += _r
MIX_ROWS = _o

SV_PARTS = (("b_ada", 6144), ("norm_mix_w", 1024), ("conv_b", 3072), ("dt_bias", 128), ("a_log", 128),
            ("d_skip", 128), ("ssd_norm_w", 2048), ("pool_scale", 1024), ("norm_mlp_w", 1024),
            ("norm_final_w", 1024), ("conv_w", 4 * XBC), ("loss", 128))
SV_OFF = {}
_o = 0
for _n, _r in SV_PARTS:
    SV_OFF[_n] = _o
    _o += _r
SV_ROWS = 224
assert _o <= SV_ROWS * 128

ADAM_LR, ADAM_B1, ADAM_B2, ADAM_EPS, ADAM_WD, ADAM_STEP = 0.001, 0.9, 0.999, 1e-08, 0.01, 10

VMEM_BIG = 56 * 1024 * 1024
NEG = -1e30

NN = ((1,), (0,))
NT = ((1,), (1,))
TN = ((0,), (0,))


def _dot(a, b, dims=NN):
    return lax.dot_general(a.astype(_MXU_DTYPE), b.astype(_MXU_DTYPE), (dims, ((), ())),
                           preferred_element_type=F32)


def _dot_hi(a, b, dims=NN):
    return lax.dot_general(a.astype(F32), b.astype(F32), (dims, ((), ())),
                           precision=lax.Precision.HIGHEST, preferred_element_type=F32)


def _pick(n, cands):
    for c in cands:
        if n % c == 0:
            return c
    return n


def _sigmoid(x):
    return 1.0 / (1.0 + jnp.exp(-x))


def _silu(x):
    return x * _sigmoid(x)


def _dsilu(x):
    s = _sigmoid(x)
    return s * (1.0 + x * (1.0 - s))


def _softplus(x):
    return jnp.maximum(x, 0.0) + jnp.log(1.0 + jnp.exp(-jnp.abs(x)))


def _params(sem, vmem=None):
    return pltpu.CompilerParams(dimension_semantics=sem, vmem_limit_bytes=vmem)


def _row_step():
    return pl.program_id(0)


def _mm(a, b, mode, *, name, outs, tm, tn, tk, extras=(), epilogue=None, aliases=None, prologue=None):
    if mode == "tn":
        K, M = a.shape
        N = b.shape[1]
        a_spec = pl.BlockSpec((tk, tm), lambda i, j, k: (k, i))
        b_spec = pl.BlockSpec((tk, tn), lambda i, j, k: (k, j))
        dims = TN
    else:
        M = a.shape[0]
        K = b.shape[0] if mode == "nn" else b.shape[1]
        if prologue is None:
            assert a.shape[1] == K
            a_spec = pl.BlockSpec((tm, tk), lambda i, j, k: (i, k))
        else:
            assert tk == K
            a_spec = pl.BlockSpec((tm, a.shape[1]), lambda i, j, k: (i, 0))
        if mode == "nn":
            N = b.shape[1]
            b_spec = pl.BlockSpec((tk, tn), lambda i, j, k: (k, j))
            dims = NN
        else:
            N = b.shape[0]
            b_spec = pl.BlockSpec((tn, tk), lambda i, j, k: (j, k))
            dims = NT
    assert M % tm == 0 and N % tn == 0 and K % tk == 0, (name, M, N, K, tm, tn, tk)
    nk = K // tk
    ne, no = len(extras), len(outs)
    if epilogue is None:
        def epilogue(acc, ex, out_refs):
            out_refs[0][...] = acc.astype(out_refs[0].dtype)

    def body(a_ref, b_ref, *rest):
        ex, out_refs = rest[:ne], rest[ne:ne + no]
        lhs = a_ref[...] if prologue is None else prologue(a_ref, ex, out_refs, pl.program_id(1))
        p = _dot(lhs, b_ref[...], dims)
        if nk == 1:
            epilogue(p, ex, out_refs)
        else:
            acc = rest[-1]
            k = pl.program_id(2)

            @pl.when(k == 0)
            def _():
                acc[...] = p

            @pl.when(jnp.logical_and(k > 0, k < nk - 1))
            def _():
                acc[...] += p

            @pl.when(k == nk - 1)
            def _():
                epilogue(acc[...] + p, ex, out_refs)

    out_specs, out_shape = [], []
    for o in outs:
        if isinstance(o, tuple):
            shape, dt, bs, im = o
            out_specs.append(pl.BlockSpec(bs, im))
            out_shape.append(jax.ShapeDtypeStruct(shape, dt))
        else:
            out_specs.append(pl.BlockSpec((tm, tn), lambda i, j, k: (i, j)))
            out_shape.append(jax.ShapeDtypeStruct((M, N), o))
    in_specs = [a_spec, b_spec]
    for _, bs, im in extras:
        in_specs.append(pl.BlockSpec(memory_space=pl.ANY) if bs is None else pl.BlockSpec(bs, im))
    res = pl.pallas_call(
        body, name=name,
        grid=(M // tm, N // tn, nk),
        in_specs=in_specs, out_specs=out_specs, out_shape=out_shape,
        scratch_shapes=[pltpu.VMEM((tm, tn), F32)] if nk > 1 else [],
        input_output_aliases={2 + e: o for e, o in (aliases or {}).items()},
        compiler_params=_params(("arbitrary", "arbitrary", "arbitrary"), VMEM_BIG),
    )(a, b, *[e[0] for e in extras])
    return res if no > 1 else res[0]


def _rows(tm, w=D, col=0):
    return (tm, w), lambda i, j, k, c=col: (i, c)


def _vecs(w=D, col=0):
    return (1, w), lambda i, j, k, c=col: (0, c)


def _sum_out(w=D):
    return ((1, w), F32, (1, w), lambda i, j, k: (0, 0))


def _mm_pool_tn(a, b, *, name, tk):
    L = a.shape[0]

    def body(a_ref, b_ref, o_ref):
        p = _dot(a_ref[...], b_ref[...], TN)

        @pl.when(pl.program_id(1) == 0)
        def _():
            o_ref[...] = p

        @pl.when(pl.program_id(1) > 0)
        def _():
            o_ref[...] += p

    blk = pl.BlockSpec((tk, PGW), lambda g, k: (k, g))
    return pl.pallas_call(body, name=name, grid=(4, L // tk), in_specs=[blk, blk],
                          out_specs=pl.BlockSpec((PGW, PGW), lambda g, k: (g, 0)),
                          out_shape=jax.ShapeDtypeStruct((POOL_W, PGW), F32),
                          compiler_params=_params(("parallel", "arbitrary")))(a, b)


def _acc_out(ref, val, i):
    @pl.when(i == 0)
    def _():
        ref[...] = val

    @pl.when(i > 0)
    def _():
        ref[...] += val


def _colsum(v):
    return jnp.sum(v, axis=0, keepdims=True)


def _ep_resid_norm(acc, ex, outs):
    x_ref, g_ref, nw_ref, sc_ref, sh_ref = ex
    mix_ref, x1_ref, h_ref = outs
    mix_ref[...] = acc.astype(mix_ref.dtype)
    xv = x_ref[...] + g_ref[...] * acc
    x1_ref[...] = xv
    r = lax.rsqrt(jnp.mean(xv * xv, axis=-1, keepdims=True) + EPS)
    h_ref[...] = (xv * r * nw_ref[...] * (1.0 + sc_ref[...]) + sh_ref[...]).astype(h_ref.dtype)


def _ep_final(acc, ex, outs):
    x1_ref, t_ref, g_ref, nw_ref = ex
    dx2_ref, dd_ref, loss_ref, dnw_ref, dg_ref = outs
    i = _row_step()
    x2 = x1_ref[...] + g_ref[...] * acc
    r = lax.rsqrt(jnp.mean(x2 * x2, axis=-1, keepdims=True) + EPS)
    xh = x2 * r
    e = xh * nw_ref[...] - t_ref[...]
    part = 0.5 * jnp.sum(jnp.mean(e * e, axis=-1, keepdims=True), axis=0, keepdims=True)
    dy = e * (1.0 / D)
    g = dy * nw_ref[...]
    dx2 = r * (g - xh * jnp.mean(g * xh, axis=-1, keepdims=True))
    dx2_ref[...] = dx2
    dd_ref[...] = (dx2 * g_ref[...]).astype(dd_ref.dtype)
    _acc_out(loss_ref, jnp.broadcast_to(part, (1, 128)), i)
    _acc_out(dnw_ref, _colsum(dy * xh), i)
    _acc_out(dg_ref, _colsum(dx2 * acc), i)


def _ep_norm_bwd(acc, ex, outs):
    x_ref, dr_ref, nw_ref, sc_ref = ex[:4]
    dx_ref, p_ref, q_ref = outs[:3]
    i = _row_step()
    xv = x_ref[...]
    r = lax.rsqrt(jnp.mean(xv * xv, axis=-1, keepdims=True) + EPS)
    xh = xv * r
    g = acc * (nw_ref[...] * (1.0 + sc_ref[...]))
    dx = dr_ref[...] + r * (g - xh * jnp.mean(g * xh, axis=-1, keepdims=True))
    dx_ref[...] = dx
    _acc_out(p_ref, _colsum(acc * xh), i)
    _acc_out(q_ref, _colsum(acc), i)
    if len(ex) > 4:
        m_ref, g_ref = ex[4:]
        dm_ref, dg_ref = outs[3:]
        dm_ref[...] = (dx * g_ref[...]).astype(dm_ref.dtype)
        _acc_out(dg_ref, _colsum(dx * m_ref[...].astype(F32)), i)


def _ep_merge_bwd(acc, ex, outs):
    a_ref, b_ref, gl_ref = ex
    da_ref, db_ref, dgl_ref = outs
    s = _sigmoid(gl_ref[...].astype(F32))
    s1, s2 = s[:, :D], s[:, D:]
    da_ref[...] = (acc * s1).astype(da_ref.dtype)
    db_ref[...] = (acc * s2).astype(db_ref.dtype)
    dgl_ref[:, :D] = (acc * a_ref[...] * s1 * (1.0 - s1)).astype(dgl_ref.dtype)
    dgl_ref[:, D:] = (acc * b_ref[...] * s2 * (1.0 - s2)).astype(dgl_ref.dtype)


GW = DI // NG


def _ep_gated_norm_bwd(acc, ex, outs):
    y_ref, z_ref, w_ref, _ = ex
    dy_ref, dz_ref, dw_ref = outs
    zv = z_ref[...].astype(F32)
    yv = y_ref[...]
    sz = _silu(zv)
    yg = yv * sz
    dsz = _dsilu(zv)
    dws = []
    for k in range(NG):
        sl = slice(k * GW, (k + 1) * GW)
        seg = yg[:, sl]
        r = lax.rsqrt(jnp.mean(seg * seg, axis=-1, keepdims=True) + EPS)
        sh = seg * r
        dn = acc[:, sl]
        g = dn * w_ref[:, sl]
        dyg = r * (g - sh * jnp.mean(g * sh, axis=-1, keepdims=True))
        dy_ref[:, sl] = dyg * sz[:, sl]
        dz_ref[:, sl] = (dyg * yv[:, sl] * dsz[:, sl]).astype(dz_ref.dtype)
        dws.append(_colsum(dn * sh))
    _acc_out(dw_ref, jnp.concatenate(dws, axis=1), _row_step())


CONV_CB = 128
HALO = 16


def _time_chunk(L):
    return _pick(L, (256, 128))


def _with_halo(x_ref, i, r0, rc):
    p0 = pl.multiple_of(jnp.maximum(r0 - HALO, 0), HALO)
    prev = jnp.where(i > 0, x_ref[pl.ds(p0, HALO), :].astype(F32), 0.0)
    return jnp.concatenate([prev, x_ref[pl.ds(r0, rc), :].astype(F32)], axis=0)


def _conv_fwd(proj, w, b, *, name):
    L = proj.shape[0]
    rc = _time_chunk(L)
    n = L // rc

    def body(x_ref, w_ref, b_ref, o_ref):
        wv = w_ref[...]
        bv = b_ref[...]

        def step(i, c):
            r0 = pl.multiple_of(i * rc, rc)
            ext = _with_halo(x_ref, i, r0, rc)
            acc = bv + ext * wv[3:4]
            for j in (1, 2, 3):
                acc = acc + pltpu.roll(ext, j, 0) * wv[3 - j:4 - j]
            acc = acc[HALO:]
            o_ref[pl.ds(r0, rc), :] = acc * _sigmoid(acc)
            return c

        lax.fori_loop(0, n, step, 0)

    return pl.pallas_call(
        body, name=name, grid=(XBC // CONV_CB,),
        in_specs=[pl.BlockSpec((L, CONV_CB), lambda j: (0, j + C_XBC // CONV_CB)),
                  pl.BlockSpec((4, CONV_CB), lambda j: (0, j)), pl.BlockSpec((1, CONV_CB), lambda j: (0, j))],
        out_specs=pl.BlockSpec((L, CONV_CB), lambda j: (0, j)),
        out_shape=jax.ShapeDtypeStruct((L, XBC), F32),
        compiler_params=_params(("parallel",), VMEM_BIG))(proj, w, b)


def _conv_bwd(proj, dy, w, b, dproj, *, name):
    L = proj.shape[0]
    rc = _time_chunk(L)
    n = L // rc

    def body(x_ref, dy_ref, w_ref, b_ref, dp_in, dx_ref, dw_ref, db_ref):
        del dp_in
        wv = w_ref[...]
        bv = b_ref[...]

        def step(k, carry):
            nxt, db, d0, d1, d2, d3 = carry
            i = n - 1 - k
            r0 = pl.multiple_of(i * rc, rc)
            ext = _with_halo(x_ref, i, r0, rc)
            xk = [ext[HALO:]] + [pltpu.roll(ext, j, 0)[HALO:] for j in (1, 2, 3)]
            pre = bv
            for j in range(4):
                pre = pre + xk[j] * wv[3 - j:4 - j]
            dpre = dy_ref[pl.ds(r0, rc), :] * _dsilu(pre)
            dext = jnp.concatenate([dpre, nxt], axis=0)
            acc = dext * wv[3:4]
            for j in (1, 2, 3):
                acc = acc + pltpu.roll(dext, rc + HALO - j, 0) * wv[3 - j:4 - j]
            dx_ref[pl.ds(r0, rc), :] = acc[:rc].astype(dx_ref.dtype)
            return (dpre[:HALO], db + _colsum(dpre), d0 + _colsum(dpre * xk[3]), d1 + _colsum(dpre * xk[2]),
                    d2 + _colsum(dpre * xk[1]), d3 + _colsum(dpre * xk[0]))

        z = jnp.zeros((1, CONV_CB), F32)
        _, db, d0, d1, d2, d3 = lax.fori_loop(0, n, step, (jnp.zeros((HALO, CONV_CB), F32), z, z, z, z, z))
        db_ref[...] = db
        dw_ref[...] = jnp.concatenate([d0, d1, d2, d3], axis=0)

    nb = XBC // CONV_CB
    return pl.pallas_call(
        body, name=name, grid=(nb,),
        in_specs=[pl.BlockSpec((L, CONV_CB), lambda j: (0, j + C_XBC // CONV_CB)),
                  pl.BlockSpec((L, CONV_CB), lambda j: (0, j)),
                  pl.BlockSpec((4, CONV_CB), lambda j: (0, j)), pl.BlockSpec((1, CONV_CB), lambda j: (0, j)),
                  pl.BlockSpec(memory_space=pl.ANY)],
        out_specs=[pl.BlockSpec((L, CONV_CB), lambda j: (0, j + C_XBC // CONV_CB)),
                   pl.BlockSpec((4, CONV_CB), lambda j: (0, j)), pl.BlockSpec((1, CONV_CB), lambda j: (0, j))],
        out_shape=[jax.ShapeDtypeStruct((L, NPROJ), BF16), jax.ShapeDtypeStruct((4, XBC), F32),
                   jax.ShapeDtypeStruct((1, XBC), F32)],
        input_output_aliases={4: 0},
        compiler_params=_params(("parallel",), VMEM_BIG))(proj, dy, w, b, dproj)


def _pool_fwd(proj, *, name):
    L = proj.shape[0]
    rc = _time_chunk(L)
    n = L // rc

    def body(x_ref, o_ref, pad):
        g = pl.program_id(0)
        pad[0:HALO, :] = jnp.zeros((HALO, PGW), F32)

        def fill(i, c):
            r0 = pl.multiple_of(i * rc, rc)
            pad[pl.ds(r0 + HALO, rc), :] = x_ref[pl.ds(r0, rc), :].astype(F32)
            return c

        lax.fori_loop(0, n, fill, 0)
        rows = lax.broadcasted_iota(jnp.int32, (rc, PGW), 0)

        for gi in range(4):
            win = 2 << gi

            @pl.when(g == gi)
            def _(gi=gi, win=win):
                def step(i, c):
                    r0 = pl.multiple_of(i * rc, rc)
                    ext = pad[pl.ds(r0, rc + HALO), :]
                    s = ext
                    sh = 1
                    while sh < win:
                        s = s + pltpu.roll(s, sh, 0)
                        sh *= 2
                    cnt = jnp.minimum(rows + (r0 + 1), win).astype(F32)
                    o_ref[pl.ds(r0, rc), :] = (s[HALO:] / cnt - ext[HALO:]).astype(o_ref.dtype)
                    return c

                lax.fori_loop(0, n, step, 0)

    return pl.pallas_call(
        body, name=name, grid=(4,),
        in_specs=[pl.BlockSpec((L, PGW), lambda j: (0, j + C_POOL // PGW))],
        out_specs=pl.BlockSpec((L, PGW), lambda j: (0, j)),
        out_shape=jax.ShapeDtypeStruct((L, POOL_W), BF16),
        scratch_shapes=[pltpu.VMEM((L + HALO, PGW), F32)],
        compiler_params=_params(("parallel",), VMEM_BIG))(proj)


def _pool_bwd(dpooled, dproj, *, name):
    L = dpooled.shape[0]
    rc = _time_chunk(L)
    n = L // rc

    def body(d_ref, dp_in, o_ref, pad):
        del dp_in
        g = pl.program_id(0)
        pad[L:L + HALO, :] = jnp.zeros((HALO, PGW), F32)
        rows = lax.broadcasted_iota(jnp.int32, (rc, PGW), 0)

        for gi in range(4):
            win = 2 << gi

            @pl.when(g == gi)
            def _(gi=gi, win=win):
                def fill(i, c):
                    r0 = pl.multiple_of(i * rc, rc)
                    cnt = jnp.minimum(rows + (r0 + 1), win).astype(F32)
                    pad[pl.ds(r0, rc), :] = d_ref[pl.ds(r0, rc), :] / cnt
                    return c

                lax.fori_loop(0, n, fill, 0)

                def step(i, c):
                    r0 = pl.multiple_of(i * rc, rc)
                    s = pad[pl.ds(r0, rc + HALO), :]
                    sh = 1
                    while sh < win:
                        s = s + pltpu.roll(s, rc + HALO - sh, 0)
                        sh *= 2
                    o_ref[pl.ds(r0, rc), :] = (s[:rc] - d_ref[pl.ds(r0, rc), :]).astype(o_ref.dtype)
                    return c

                lax.fori_loop(0, n, step, 0)

    return pl.pallas_call(
        body, name=name, grid=(4,),
        in_specs=[pl.BlockSpec((L, PGW), lambda j: (0, j)), pl.BlockSpec(memory_space=pl.ANY)],
        out_specs=pl.BlockSpec((L, PGW), lambda j: (0, j + C_POOL // PGW)),
        out_shape=jax.ShapeDtypeStruct((L, NPROJ), BF16),
        scratch_shapes=[pltpu.VMEM((L + HALO, PGW), F32)],
        input_output_aliases={1: 0},
        compiler_params=_params(("parallel",), VMEM_BIG))(dpooled, dproj)


_SPLIT_DT = jnp.bfloat16


def _ssd_consts():
    tri = np.tril(np.ones((Q, Q), np.float32))
    exp = np.zeros((128, DI), np.float32)
    for h in range(NH):
        exp[h, h * HP:(h + 1) * HP] = 1.0
    exp2 = np.concatenate([exp, exp], axis=0)
    return (jnp.asarray(tri, dtype=_SPLIT_DT), jnp.asarray(tri.T.copy(), dtype=_SPLIT_DT),
            jnp.asarray(exp2, dtype=_SPLIT_DT))


def _split(v, n):
    parts, r = [], v
    for _ in range(n):
        p = r.astype(_SPLIT_DT)
        parts.append(p)
        r = r - p.astype(F32)
    return parts


def _bdot(a, b, dims):
    return lax.dot_general(a, b, (dims, ((), ())), preferred_element_type=F32)


def _tri_sum(t_ref, v):
    r = _bdot(t_ref[...], jnp.concatenate(_split(v, 3), axis=1), NN)
    return r[:, :128] + r[:, 128:256] + r[:, 256:]


def _expand(v, e2_ref):
    return _bdot(jnp.concatenate(_split(v, 2), axis=1), e2_ref[...], NN)


def _reduce_heads(vals, eg):
    parts = []
    for v in vals:
        parts += _split(v, 2)
    r = _bdot(jnp.concatenate(parts, axis=0), eg, NT)
    return [r[2 * i * Q:(2 * i + 1) * Q] + r[(2 * i + 1) * Q:(2 * i + 2) * Q] for i in range(len(vals))]


def _ssd_common(xbc_ref, dtw_ref, dtb_ref, arow_ref, t_ref, e_ref):
    pre = dtw_ref[:, :128] + dtb_ref[...]
    dt = _softplus(pre)
    acs = _tri_sum(t_ref, dt * arow_ref[...])
    acs_x = _expand(acs, e_ref)
    dt_x = _expand(dt, e_ref)
    xs = xbc_ref[:, 0:DI]
    return pre, dt, acs, acs.T, acs_x, dt_x, xs


def _ssd_fwd(xbc, proj, dtb, arow, dsk_x, *, name):
    L = xbc.shape[0]
    nc = L // Q
    tri, _, expand = _ssd_consts()

    def body(xbc_ref, dtw_ref, dtb_ref, arow_ref, dsk_ref, t_ref, e_ref, y_ref, hs_ref, h_scr):
        @pl.when(pl.program_id(0) == 0)
        def _():
            h_scr[...] = jnp.zeros_like(h_scr)

        _, dt, acs, acs_t, acs_x, dt_x, xs = _ssd_common(xbc_ref, dtw_ref, dtb_ref, arow_ref, t_ref, e_ref)
        xdt = xs * dt_x
        eacs = jnp.exp(acs_x)
        acs_last = acs_x[Q - 1:Q, :]
        dec = jnp.exp(acs_last - acs_x)
        hs_ref[0] = h_scr[...].astype(hs_ref.dtype)
        causal = lax.broadcasted_iota(jnp.int32, (Q, Q), 0) >= lax.broadcasted_iota(jnp.int32, (Q, Q), 1)
        first = lax.broadcasted_iota(jnp.int32, (Q, 128), 1) < HP
        for g in range(NG):
            bg = xbc_ref[:, DI + g * NS:DI + (g + 1) * NS]
            cg = xbc_ref[:, DI + NG * NS + g * NS:DI + NG * NS + (g + 1) * NS]
            s = _dot(cg, bg, NT)
            sl = slice(g * GW, (g + 1) * GW)
            hg = h_scr[:, sl]
            yoff = _dot(cg, hg, NN) * eacs[:, sl]
            st = _dot(bg, xdt[:, sl] * dec[:, sl], TN)
            h_scr[:, sl] = hg * eacs[Q - 1:Q, sl] + st
            for j in range(4):
                lo = g * GW + j * 128
                xb = xdt[:, lo:lo + 128]
                yp = yoff[:, j * 128:(j + 1) * 128] + dsk_ref[:, lo:lo + 128] * xs[:, lo:lo + 128]
                for e in range(2):
                    h = g * 8 + j * 2 + e
                    lm = jnp.exp(jnp.where(causal, acs[:, h:h + 1] - acs_t[h:h + 1, :], NEG))
                    xm = jnp.where(first if e == 0 else jnp.logical_not(first), xb, 0.0)
                    yp = yp + _dot(s * lm, xm, NN)
                y_ref[:, lo:lo + 128] = yp

    return pl.pallas_call(
        body, name=name, grid=(nc,),
        in_specs=[pl.BlockSpec((Q, XBC), lambda c: (c, 0)),
                  pl.BlockSpec((Q, DT_PAD), lambda c: (c, 0)),
                  pl.BlockSpec((1, 128), lambda c: (0, 0)), pl.BlockSpec((1, 128), lambda c: (0, 0)),
                  pl.BlockSpec((1, DI), lambda c: (0, 0)),
                  pl.BlockSpec((Q, Q), lambda c: (0, 0)), pl.BlockSpec((256, DI), lambda c: (0, 0))],
        out_specs=[pl.BlockSpec((Q, DI), lambda c: (c, 0)), pl.BlockSpec((1, NS, DI), lambda c: (c, 0, 0))],
        out_shape=[jax.ShapeDtypeStruct((L, DI), F32), jax.ShapeDtypeStruct((nc, NS, DI), F32)],
        scratch_shapes=[pltpu.VMEM((NS, DI), F32)],
        compiler_params=_params(("arbitrary",), VMEM_BIG))(xbc, proj, dtb, arow, dsk_x, tri, expand)


def _ssd_bwd(dy, xbc, proj, hs, dtb, arow, dsk_x, dproj, *, name):
    L = xbc.shape[0]
    nc = L // Q
    tri, triu, expand = _ssd_consts()

    def body(dy_ref, xbc_ref, dtw_ref, hs_ref, dtb_ref, arow_ref, dsk_ref, t_ref, u_ref, e_ref, dp_in,
             dxbc_ref, ddtw_ref, da_ref, ddx_ref, ddtb_ref, dh_scr):
        del dp_in
        i = pl.program_id(0)

        @pl.when(i == 0)
        def _():
            dh_scr[...] = jnp.zeros_like(dh_scr)

        pre, dt, acs, acs_t, acs_x, dt_x, xs = _ssd_common(xbc_ref, dtw_ref, dtb_ref, arow_ref, t_ref, e_ref)
        dyv = dy_ref[...]
        xdt = xs * dt_x
        eacs = jnp.exp(acs_x)
        acs_last = acs_x[Q - 1:Q, :]
        dec = jnp.exp(acs_last - acs_x)
        gy = dyv * eacs
        causal = lax.broadcasted_iota(jnp.int32, (Q, Q), 0) >= lax.broadcasted_iota(jnp.int32, (Q, Q), 1)
        first = lax.broadcasted_iota(jnp.int32, (Q, 128), 1) < HP
        lane_h = lax.broadcasted_iota(jnp.int32, (Q, 128), 1)
        sub_h = lax.broadcasted_iota(jnp.int32, (128, Q), 0)
        last_row = lax.broadcasted_iota(jnp.int32, (Q, GW), 0) == Q - 1
        dacs = jnp.zeros((Q, 128), F32)
        dacs_t = jnp.zeros((128, Q), F32)
        ddt = jnp.zeros((Q, 128), F32)
        for g in range(NG):
            bg = xbc_ref[:, DI + g * NS:DI + (g + 1) * NS]
            cg = xbc_ref[:, DI + NG * NS + g * NS:DI + NG * NS + (g + 1) * NS]
            s = _dot(cg, bg, NT)
            sl = slice(g * GW, (g + 1) * GW)
            hg = hs_ref[0, :, sl].astype(F32)
            dhn = dh_scr[:, sl]
            eal = eacs[Q - 1:Q, sl]
            gg = gy[:, sl]
            dax = gg * _dot(cg, hg, NN)
            dcg = _dot(gg, hg, NT)
            dh_scr[:, sl] = _dot(cg, gg, TN) + dhn * eal
            dal = eal * _colsum(dhn * hg)
            xdd = xdt[:, sl] * dec[:, sl]
            dbg = _dot(xdd, dhn, NT)
            wv = _dot(bg, dhn, NN)
            dd = wv * xdd
            dax = dax - dd
            dal = dal + _colsum(dd)
            dax = dax + jnp.where(last_row, dal, 0.0)
            dxdt_g = wv * dec[:, sl]
            ds = jnp.zeros((Q, Q), F32)
            dxdt_blocks = []
            for j in range(4):
                lo = g * GW + j * 128
                xb = xdt[:, lo:lo + 128]
                dyb = dyv[:, lo:lo + 128]
                dxb = dxdt_g[:, j * 128:(j + 1) * 128]
                for e in range(2):
                    h = g * 8 + j * 2 + e
                    lm = jnp.exp(jnp.where(causal, acs[:, h:h + 1] - acs_t[h:h + 1, :], NEG))
                    m = s * lm
                    dym = jnp.where(first if e == 0 else jnp.logical_not(first), dyb, 0.0)
                    dm = _dot(dym, xb, NT)
                    r = dm * m
                    dacs = dacs + jnp.where(lane_h == h, jnp.sum(r, axis=1, keepdims=True), 0.0)
                    dacs_t = dacs_t + jnp.where(sub_h == h, _colsum(r), 0.0)
                    ds = ds + dm * lm
                    dxb = dxb + _dot(m, dym, TN)
                dxdt_blocks.append(dxb)
            dxdt = jnp.concatenate(dxdt_blocks, axis=1)
            dcg = dcg + _dot(ds, bg, NN)
            dbg = dbg + _dot(ds, cg, TN)
            dxbc_ref[:, DI + g * NS:DI + (g + 1) * NS] = dbg
            dxbc_ref[:, DI + NG * NS + g * NS:DI + NG * NS + (g + 1) * NS] = dcg
            dxbc_ref[:, sl] = dsk_ref[:, sl] * dyv[:, sl] + dxdt * dt_x[:, sl]
            ddt_g, dacs_g = _reduce_heads([dxdt * xs[:, sl], dax], e_ref[0:128, sl])
            ddt = ddt + ddt_g
            dacs = dacs + dacs_g
        dacs = dacs - dacs_t.T
        ddta = _tri_sum(u_ref, dacs)
        ddt = ddt + ddta * arow_ref[...]
        ddtw = jnp.where(lane_h < NH, ddt * _sigmoid(pre), 0.0)
        ddtw_ref[...] = jnp.concatenate([ddtw, jnp.zeros((Q, DT_PAD - 128), F32)], axis=1).astype(ddtw_ref.dtype)
        _acc_out(da_ref, _colsum(ddta * dt), i)
        _acc_out(ddx_ref, _colsum(dyv * xs), i)
        _acc_out(ddtb_ref, _colsum(ddtw), i)

    rev = lambda c: (nc - 1 - c, 0)
    const = lambda c: (0, 0)
    return pl.pallas_call(
        body, name=name, grid=(nc,),
        in_specs=[pl.BlockSpec((Q, DI), rev), pl.BlockSpec((Q, XBC), rev),
                  pl.BlockSpec((Q, DT_PAD), rev),
                  pl.BlockSpec((1, NS, DI), lambda c: (nc - 1 - c, 0, 0)),
                  pl.BlockSpec((1, 128), const), pl.BlockSpec((1, 128), const), pl.BlockSpec((1, DI), const),
                  pl.BlockSpec((Q, Q), const), pl.BlockSpec((Q, Q), const), pl.BlockSpec((256, DI), const),
                  pl.BlockSpec(memory_space=pl.ANY)],
        out_specs=[pl.BlockSpec((Q, XBC), rev),
                   pl.BlockSpec((Q, DT_PAD), lambda c: (nc - 1 - c, C_DT // DT_PAD)),
                   pl.BlockSpec((1, 128), const), pl.BlockSpec((1, DI), const), pl.BlockSpec((1, 128), const)],
        out_shape=[jax.ShapeDtypeStruct((L, XBC), F32), jax.ShapeDtypeStruct((L, NPROJ), BF16),
                   jax.ShapeDtypeStruct((1, 128), F32), jax.ShapeDtypeStruct((1, DI), F32),
                   jax.ShapeDtypeStruct((1, 128), F32)],
        scratch_shapes=[pltpu.VMEM((NS, DI), F32)],
        input_output_aliases={10: 1},
        compiler_params=_params(("arbitrary",), VMEM_BIG))(dy, xbc, proj, hs, dtb, arow, dsk_x, tri, triu,
                                                          expand, dproj)


def _adam_update(wv, gv, mv, vv):
    c1 = 1.0 - ADAM_B1 ** ADAM_STEP
    c2 = 1.0 - ADAM_B2 ** ADAM_STEP
    mn = ADAM_B1 * mv + (1.0 - ADAM_B1) * gv
    vn = ADAM_B2 * vv + (1.0 - ADAM_B2) * (gv * gv)
    return -ADAM_LR * ((mn / c1) / (jnp.sqrt(vn / c2) + ADAM_EPS) + ADAM_WD * wv), mn, vn


def _adamw(w, g, m, v, *, name, tr=None):
    R = w.shape[0]
    rest = tuple(w.shape[1:])
    if tr is None:
        tr = _pick(R, (256, 128, 64, 32, 16, 8))
    assert R % tr == 0

    def body(w_ref, g_ref, m_ref, v_ref, d_ref, mo_ref, vo_ref):
        d_ref[...], mo_ref[...], vo_ref[...] = _adam_update(w_ref[...], g_ref[...], m_ref[...], v_ref[...])

    zeros = (0,) * len(rest)
    spec = pl.BlockSpec((tr,) + rest, lambda i: (i,) + zeros)
    return pl.pallas_call(body, name=name, grid=(R // tr,), in_specs=[spec] * 4, out_specs=[spec] * 3,
                          out_shape=[jax.ShapeDtypeStruct(w.shape, F32)] * 3,
                          compiler_params=_params(("parallel",)))(w, g, m, v)


def _adamw_small(svrow, g_conv, params, *, name):
    n = len(params)

    def body(*refs):
        sv_ref, gc_ref = refs[0], refs[1]
        ins, outs = refs[2:2 + 3 * n], refs[2 + 3 * n:]
        for p, (key, w, _, _) in enumerate(params):
            w_ref, m_ref, v_ref = ins[3 * p:3 * p + 3]
            g_ref, d_ref, mo_ref, vo_ref = outs[4 * p:4 * p + 4]
            gv = gc_ref[...] if key == "conv_w" else sv_ref[:, SV_OFF[key]:SV_OFF[key] + w.shape[1]]
            g_ref[...] = gv
            d_ref[...], mo_ref[...], vo_ref[...] = _adam_update(w_ref[...], gv, m_ref[...], v_ref[...])

    vm = pl.BlockSpec(memory_space=pltpu.VMEM)
    args = [svrow, g_conv]
    shapes = []
    for _, w, m, v in params:
        args += [w, m, v]
        shapes += [jax.ShapeDtypeStruct(w.shape, F32)] * 4
    res = pl.pallas_call(body, name=name, in_specs=[vm] * len(args), out_specs=[vm] * len(shapes),
                         out_shape=shapes)(*args)
    return {key: tuple(res[4 * p:4 * p + 4]) for p, (key, _, _, _) in enumerate(params)}


def _slab_sum(recv, *, tile, name):
    rows = recv.shape[1]
    assert rows % tile == 0 and tile % 16 == 0

    def body(r_ref, o_ref):
        acc = r_ref[0].astype(F32)
        for j in range(1, N_DEV):
            acc = acc + r_ref[j].astype(F32)
        o_ref[...] = acc

    return pl.pallas_call(body, name=name, grid=(rows // tile,),
                          in_specs=[pl.BlockSpec((N_DEV, tile, D), lambda i: (0, i, 0))],
                          out_specs=pl.BlockSpec((tile, D), lambda i: (i, 0)),
                          out_shape=jax.ShapeDtypeStruct((rows, D), F32),
                          compiler_params=_params(("parallel",)))(recv)


MESH = pl.DeviceIdType.MESH


def _coords():
    return lax.axis_index("x"), lax.axis_index("y"), lax.axis_index("c")


def _peer(k):
    x, y, c = _coords()
    px = 1 - x if k & 4 else x
    py = 1 - y if k & 2 else y
    pc = 1 - c if k & 1 else c
    return (px, py, pc), 4 * px + 2 * py + pc


def _rcopy(src, dst, ssem, rsem, dev):
    return pltpu.make_async_remote_copy(src_ref=src, dst_ref=dst, send_sem=ssem, recv_sem=rsem,
                                        device_id=dev, device_id_type=MESH)


def _exchange_all(src_of, dst_slot, send_sems, recv_sems):
    x, y, c = _coords()
    me = 4 * x + 2 * y + c
    sent = []
    for k in range(1, N_DEV):
        dev, pidx = _peer(k)
        cp = _rcopy(src_of(pidx), dst_slot(me), send_sems.at[k - 1], recv_sems.at[k - 1], dev)
        cp.start()
        sent.append(cp)
    for k in range(1, N_DEV):
        dev, pidx = _peer(k)
        _rcopy(src_of(pidx), dst_slot(pidx), send_sems.at[k - 1], recv_sems.at[k - 1], dev).wait_recv()
    for cp in sent:
        cp.wait_send()


def _rows_of_slots(buf, nslots):
    rows = lax.broadcasted_iota(jnp.int32, (8, buf.shape[-1]), 0)
    out = jnp.zeros((8, buf.shape[-1]), F32)
    for j in range(nslots):
        out = out + jnp.where(rows == j, buf[j], 0.0)
    return out


def _ada_fwd(c, w_ada, b_r, *, name):
    wloc = w_ada.shape[1]

    def body(c_ref, w_ref, b_ref, mod_ref, call_ref, csrc, cbuf, psrc, pbuf, s1, r1, s2, r2):
        x, y, cc = _coords()
        me = 4 * x + 2 * y + cc
        csrc[...] = jnp.broadcast_to(c_ref[...], (8, D))
        cbuf[me] = csrc[...]
        _exchange_all(lambda p: csrc, lambda s: cbuf.at[s], s1, r1)
        call = _rows_of_slots(cbuf, N_DEV)
        call_ref[...] = call
        prod = _dot_hi(_silu(call), w_ref[...])
        for b in range(N_DEV):
            psrc[b] = jnp.broadcast_to(prod[b:b + 1, :], (8, wloc))
        pbuf[me] = psrc[me]
        _exchange_all(lambda p: psrc.at[p], lambda s: pbuf.at[s], s2, r2)
        mod_ref[...] = _rows_of_slots(pbuf, N_DEV) + b_ref[...]

    vm = pl.BlockSpec(memory_space=pltpu.VMEM)
    return pl.pallas_call(
        body, name=name, in_specs=[vm, vm, vm], out_specs=[vm, vm],
        out_shape=[jax.ShapeDtypeStruct((N_DEV, wloc), F32), jax.ShapeDtypeStruct((N_DEV, D), F32)],
        scratch_shapes=[pltpu.VMEM((8, D), F32), pltpu.VMEM((N_DEV, 8, D), F32),
                        pltpu.VMEM((N_DEV, 8, wloc), F32), pltpu.VMEM((N_DEV, 8, wloc), F32),
                        pltpu.SemaphoreType.DMA((N_DEV - 1,)), pltpu.SemaphoreType.DMA((N_DEV - 1,)),
                        pltpu.SemaphoreType.DMA((N_DEV - 1,)), pltpu.SemaphoreType.DMA((N_DEV - 1,))],
        compiler_params=pltpu.CompilerParams(vmem_limit_bytes=VMEM_BIG))(c, w_ada, b_r)


def _gather_slabs(slab, *, name):
    def body(x_ref, out_ref, send_sems, recv_sems, local_sem):
        x, y, c = _coords()
        me, sibling = (x, y, c), (x, y, 1 - c)
        chips = [(1 - x, y), (x, 1 - y), (1 - x, 1 - y)]

        def slot(px, py, pc):
            return out_ref.at[4 * px + 2 * py + pc]

        def copy(k, block, to, src=None):
            return _rcopy(slot(*block) if src is None else src, slot(*block), send_sems.at[k], recv_sems.at[k], to)

        mine = pltpu.make_async_copy(x_ref, slot(*me), local_sem)
        mine.start()
        first = [copy(0, me, sibling, src=x_ref)]
        first += [copy(1 + j, me, (*chip, c), src=x_ref) for j, chip in enumerate(chips)]
        for cp in first:
            cp.start()
        passed = [copy(4 + j, (*chip, c), sibling) for j, chip in enumerate(chips)]
        for j, chip in enumerate(chips):
            copy(1 + j, (*chip, c), me).wait_recv()
            passed[j].start()
        copy(0, sibling, me).wait_recv()
        for j, chip in enumerate(chips):
            copy(4 + j, (*chip, 1 - c), me).wait_recv()
        for cp in first + passed:
            cp.wait_send()
        mine.wait()

    anyspec = pl.BlockSpec(memory_space=pl.ANY)
    return pl.pallas_call(
        body, name=name, in_specs=[anyspec], out_specs=anyspec,
        out_shape=jax.ShapeDtypeStruct((N_DEV,) + slab.shape, slab.dtype),
        scratch_shapes=[pltpu.SemaphoreType.DMA((7,)), pltpu.SemaphoreType.DMA((7,)), pltpu.SemaphoreType.DMA],
    )(slab)


_HBM =pl.BlockSpec(memory_space=pltpu.HBM)
_SEM = pl.BlockSpec(memory_space=pltpu.SEMAPHORE)
_EFFECT = pltpu.SideEffectType.DATAFLOW_SIDE_EFFECTING


def _xchg_src(src_ref, pidx, per_peer):
    return src_ref.at[pidx] if per_peer else src_ref


def _xchg_start(src, *, per_peer, name):
    rows = src.shape[-2]
    land_shape = (N_DEV, rows, D)

    def body(src_ref, land_ref, send_sems, recv_sems, src_thru, land_thru, token):
        del src_thru, land_thru
        x, y, c = _coords()
        me = 4 * x + 2 * y + c
        for k in range(1, N_DEV):
            dev, pidx = _peer(k)
            _rcopy(_xchg_src(src_ref, pidx, per_peer), land_ref.at[me], send_sems.at[k - 1],
                   recv_sems.at[k - 1], dev).start()
        token[...] = jnp.zeros_like(token)

    return pl.pallas_call(
        body, name=name,
        out_shape=(pltpu.SemaphoreType.DMA((N_DEV - 1,)), pltpu.SemaphoreType.DMA((N_DEV - 1,)),
                   pltpu.HBM(src.shape, src.dtype), pltpu.HBM(land_shape, src.dtype),
                   jax.ShapeDtypeStruct((8, 128), F32)),
        in_specs=(_HBM, _HBM),
        out_specs=(_SEM, _SEM, _HBM, _HBM, pl.BlockSpec(memory_space=pltpu.VMEM)),
        input_output_aliases={0: 2, 1: 3},
        compiler_params=pltpu.CompilerParams(has_side_effects=_EFFECT),
    )(pltpu.with_memory_space_constraint(src, pltpu.HBM),
      pltpu.with_memory_space_constraint(lax.empty(land_shape, src.dtype), pltpu.HBM))


def _xchg_wait(started, after, *, per_peer, name):
    send_sems, recv_sems, src_thru, land_thru, _ = started

    def body(src_ref, land_ref, send_sems, recv_sems, after_ref, src_dead, got_ref):
        del after_ref, src_dead, got_ref
        for k in range(1, N_DEV):
            dev, pidx = _peer(k)
            cp = _rcopy(_xchg_src(src_ref, pidx, per_peer), land_ref.at[pidx], send_sems.at[k - 1],
                        recv_sems.at[k - 1], dev)
            cp.wait_send()
            cp.wait_recv()

    return pl.pallas_call(
        body, name=name,
        out_shape=(pltpu.HBM(src_thru.shape, src_thru.dtype), pltpu.HBM(land_thru.shape, land_thru.dtype)),
        in_specs=(_HBM, _HBM, _SEM, _SEM, pl.BlockSpec(memory_space=pl.ANY)),
        out_specs=(_HBM, _HBM),
        input_output_aliases={0: 0, 1: 1},
        compiler_params=pltpu.CompilerParams(has_side_effects=_EFFECT),
    )(src_thru, land_thru, send_sems, recv_sems, after)


def _dep(token):
    return (token, (8, 128), lambda i, j, k: (0, 0))


def _small_allsum(sv, *, name):
    def body(sv_ref, all_ref, sum_ref, send_sems, recv_sems):
        x, y, c = _coords()
        me = 4 * x + 2 * y + c
        all_ref[me] = sv_ref[...]
        _exchange_all(lambda p: sv_ref, lambda s: all_ref.at[s], send_sems, recv_sems)
        acc = all_ref[0]
        for j in range(1, N_DEV):
            acc = acc + all_ref[j]
        sum_ref[...] = acc

    vm = pl.BlockSpec(memory_space=pltpu.VMEM)
    return pl.pallas_call(
        body, name=name, in_specs=[vm], out_specs=[vm, vm],
        out_shape=[jax.ShapeDtypeStruct((N_DEV, SV_ROWS, 128), F32), jax.ShapeDtypeStruct((SV_ROWS, 128), F32)],
        scratch_shapes=[pltpu.SemaphoreType.DMA((7,)), pltpu.SemaphoreType.DMA((7,))],
    )(sv)


def _ada_bwd(call, dmod_loc, *, name):
    wloc = dmod_loc.shape[1]

    def body(c_ref, d_ref, o_ref):
        o_ref[...] = _dot_hi(_silu(c_ref[...]), d_ref[...], TN)

    vm = pl.BlockSpec(memory_space=pltpu.VMEM)
    return pl.pallas_call(body, name=name, in_specs=[vm, vm], out_specs=vm,
                          out_shape=jax.ShapeDtypeStruct((D, wloc), F32),
                          compiler_params=pltpu.CompilerParams(vmem_limit_bytes=VMEM_BIG))(call, dmod_loc)


def _pad_rows(a, rows):
    return jnp.pad(a, ((0, rows - a.shape[0]), (0, 0)))


IN_SHIFT = tuple((IN_ROWS * j) % 16 for j in range(N_DEV))
IN_BASE = tuple(IN_ROWS * j - IN_SHIFT[j] for j in range(N_DEV))
IN_SEGMENTS = ((2048, XBC, C_XBC), (5152, 1024, C_POOL), (0, 2048, C_Z), (6176, 2048, C_GATE), (5120, 32, C_DT))


def _global_pieces(gs):
    pieces = []
    for j in range(N_DEV):
        lo, hi = 0, IN_ROWS_P
        if j > 0 and IN_BASE[j - 1] + IN_ROWS_P > IN_BASE[j]:
            pieces.append((IN_BASE[j], 16, gs[j - 1, IN_ROWS_P - 16:IN_ROWS_P] + gs[j, 0:16]))
            lo = 16
        if j + 1 < N_DEV and IN_BASE[j] + IN_ROWS_P > IN_BASE[j + 1]:
            hi = IN_ROWS_P - 16
        pieces.append((IN_BASE[j] + lo, hi - lo, gs[j, lo:hi]))
    return pieces


def _reorder_in_rows(gs):
    pieces = _global_pieces(gs)
    parts = []
    for lo, n, _ in IN_SEGMENTS:
        for p0, pn, arr in pieces:
            a, b = max(lo, p0), min(lo + n, p0 + pn)
            if a < b:
                parts.append(arr[a - p0:b - p0])
    parts.append(jnp.zeros((DT_PAD - 32, D), gs.dtype))
    return jnp.concatenate(parts, axis=0)


def _restore_in_shards(d):
    slabs = []
    for j in range(N_DEV):
        parts = []
        r, end = IN_BASE[j], IN_BASE[j] + IN_ROWS_P
        while r < end:
            lo, n, new = next(s for s in IN_SEGMENTS if s[0] <= r < s[0] + s[1])
            e = min(end, lo + n)
            parts.append(d[new + r - lo:new + e - lo])
            r = e
        slabs.append(jnp.concatenate(parts, axis=0))
    return jnp.stack(slabs, axis=0)


def _pack_sv(parts):
    flat = []
    for n, size in SV_PARTS:
        v = parts[n].reshape(-1).astype(F32)
        flat.append(jnp.pad(v, (0, size - v.shape[0])))
    v = jnp.concatenate(flat)
    return jnp.pad(v, (0, SV_ROWS * 128 - v.shape[0])).reshape(SV_ROWS, 128)


def _sv_get(flat, n, size):
    return flat[SV_OFF[n]:SV_OFF[n] + size]


def kernel(x, c, w_ada, b_ada, norm_mix_w, w_in, conv_w, conv_b, dt_bias, a_log, d_skip, ssd_norm_w, w_branch_ssd, pool_w, pool_scale, w_branch_pool, w_out, norm_mlp_w, w_up, w_down, norm_final_w, loss_target, m_w_ada, m_b_ada, m_norm_mix_w, m_w_in, m_conv_w, m_conv_b, m_dt_bias, m_a_log, m_d_skip, m_ssd_norm_w, m_w_branch_ssd, m_pool_w, m_pool_scale, m_w_branch_pool, m_w_out, m_norm_mlp_w, m_w_up, m_w_down, m_norm_final_w, v_w_ada, v_b_ada, v_norm_mix_w, v_w_in, v_conv_w, v_conv_b, v_dt_bias, v_a_log, v_d_skip, v_ssd_norm_w, v_w_branch_ssd, v_pool_w, v_pool_scale, v_w_branch_pool, v_w_out, v_norm_mlp_w, v_w_up, v_w_down, v_norm_final_w):
    xs_ = x[0]
    tgt = loss_target[0]
    L = xs_.shape[0]
    me = 4 * lax.axis_index("x") + 2 * lax.axis_index("y") + lax.axis_index("c")
    wloc = w_ada.shape[2]

    mod_p, c_all = _ada_fwd(c, w_ada[0], b_ada.reshape(N_DEV, wloc), name="ada_fwd")
    mod = mod_p.reshape(6, D)
    shift_m, scale_m, gate_m, shift_f, scale_f, gate_f = [mod[i:i + 1] for i in range(6)]

    conv_bits = lax.bitcast_convert_type(conv_w[0], SLAB_DT).reshape(3, D)
    in_shift = (IN_ROWS * me) % 16
    slab_in = lax.dynamic_update_slice(jnp.zeros((IN_ROWS_P, D), SLAB_DT), w_in[0].T.astype(SLAB_DT),
                                       (in_shift, 0))
    slab_in = jnp.concatenate([slab_in, _pad_rows(conv_bits, CONV_ROWS)], axis=0)
    slab_rest = jnp.concatenate([
        w_branch_ssd[0].astype(SLAB_DT),
        pool_w[0].reshape(32, D).astype(SLAB_DT),
        w_branch_pool[0].astype(SLAB_DT),
        w_out[0].astype(SLAB_DT),
        w_up[0].T.astype(SLAB_DT),
        w_down[0].astype(SLAB_DT)], axis=0)
    slab_in, mod_p = lax.optimization_barrier((slab_in, mod_p))
    gs_in = _gather_slabs(slab_in, name="gather_w_in")
    slab_rest, gs_in = lax.optimization_barrier((slab_rest, gs_in))
    rest_started = _xchg_start(slab_rest, per_peer=False, name="gather_rest_start")
    gather_token = rest_started[4]

    w_in_t = _reorder_in_rows(gs_in)
    conv_full = lax.bitcast_convert_type(
        gs_in[:, IN_ROWS_P:IN_ROWS_P + 3].reshape(N_DEV, 4, XBC // N_DEV, 2), F32)
    conv_full = conv_full.transpose(1, 0, 2).reshape(4, XBC)

    dtb = jnp.pad(dt_bias, ((0, 0), (0, 128 - NH)))
    arow = jnp.pad(-jnp.exp(a_log), ((0, 0), (0, 128 - NH)))
    dsk_x = jnp.repeat(d_skip, HP, axis=1)

    tm = _pick(L, (1024, 512, 256, 128))
    tm2 = _pick(L, (2048, 1024, 512, 256, 128))
    tkl = _pick(L, (4096, 2048, 1024, 512, 256, 128))
    tkl2 = _pick(L, (2048, 1024, 512, 256, 128))

    tmh = _pick(L, (512, 256, 128))
    zcol = C_Z // DI
    gcol = C_GATE // (2 * D)

    def whole_rows(w):
        return lambda t: ((L, w), BF16, (t, w), lambda i, j, k: (i, 0))

    def norm1_pro(x_ref, ex, outs, j):
        @pl.when(j == 0)
        def _():
            xv = x_ref[...]
            r = lax.rsqrt(jnp.mean(xv * xv, axis=-1, keepdims=True) + EPS)
            outs[1][...] = (xv * r * ex[0][...] * (1.0 + ex[1][...]) + ex[2][...]).astype(outs[1].dtype)

        return outs[1][...]

    def proj_ep(acc, ex, outs):
        outs[0][...] = acc

        @pl.when(pl.program_id(1) == NPROJ // 768 - 1)
        def _():
            outs[2][...] = acc[:, 768 - DT_PAD:]

    proj, h1, dtp = _mm(
        xs_, w_in_t, "nt", name="in_proj", tm=tm2, tn=768, tk=D,
        extras=[(norm_mix_w, *_vecs()), (scale_m, *_vecs()), (shift_m, *_vecs()), _dep(gather_token)],
        outs=[F32, whole_rows(D)(tm2), ((L, DT_PAD), F32, (tm2, DT_PAD), lambda i, j, k: (i, 0))],
        prologue=norm1_pro, epilogue=proj_ep)
    xbc_raw = proj
    xbc = _conv_fwd(xbc_raw, conv_full, conv_b, name="conv_fwd")
    y_ssm, hs = _ssd_fwd(xbc, dtp, dtb, arow, dsk_x, name="ssd_fwd")

    slab_rest, gs = _xchg_wait(rest_started, y_ssm, per_peer=False, name="gather_rest_wait")
    gs = lax.dynamic_update_slice(gs, slab_rest[None], (me, 0, 0))

    def part(n, rows):
        return gs[:, REST_OFF[n]:REST_OFF[n] + rows]

    w_bssd = part("bssd", 256).reshape(DI, D)
    w_pool = part("pool", 32).reshape(N_DEV, 4, 32, PGW).transpose(1, 0, 2, 3).reshape(POOL_W, PGW)
    w_bpool = part("bpool", 128).reshape(POOL_W, D)
    w_o = part("out", 128).reshape(D, D)
    w_up_t = part("up", 512).reshape(DFF, D)
    w_dn = part("down", 512).reshape(DFF, D)

    def gnorm_pro(y_ref, ex, outs, j):
        z_ref, w_ref = ex
        yg = y_ref[...] * _silu(z_ref[...].astype(F32))
        segs = []
        for k in range(NG):
            sl = slice(k * GW, (k + 1) * GW)
            seg = yg[:, sl]
            r = lax.rsqrt(jnp.mean(seg * seg, axis=-1, keepdims=True) + EPS)
            segs.append((seg * r * w_ref[:, sl]).astype(BF16))
        yn_v = jnp.concatenate(segs, axis=1)
        outs[1][...] = yn_v
        return yn_v

    y_ssd, yn = _mm(y_ssm, w_bssd, "nn", name="branch_ssd", tm=tmh, tn=D, tk=DI,
                    extras=[(proj, *_rows(tmh, DI, zcol)), (ssd_norm_w, *_vecs(DI))],
                    outs=[F32, whole_rows(DI)(tmh)], prologue=gnorm_pro)
    pooled = _pool_fwd(proj, name="pool_fwd")
    wp_spec = ((POOL_W, PGW), lambda i, j, k: (0, 0))

    def pool_pro(a_ref, ex, outs, j):
        wp_ref, s_ref = ex
        segs = []
        for g in range(4):
            sl = slice(g * PGW, (g + 1) * PGW)
            p = _dot(a_ref[:, sl], wp_ref[sl, :], NN)
            outs[1][:, sl] = p.astype(BF16)
            segs.append((p * s_ref[:, sl]).astype(BF16))
        yp1_v = jnp.concatenate(segs, axis=1)
        outs[2][...] = yp1_v
        return yp1_v

    y_pool, yp0, yp1 = _mm(pooled, w_bpool, "nn", name="branch_pool", tm=tm, tn=D, tk=D,
                           extras=[(w_pool, *wp_spec), (pool_scale, *_vecs())],
                           outs=[F32, whole_rows(D)(tm), whole_rows(D)(tm)], prologue=pool_pro)

    def merge_pro(a_ref, ex, outs, j):
        s = _sigmoid(ex[1][...].astype(F32))
        mv = (s[:, :D] * a_ref[...] + s[:, D:] * ex[0][...]).astype(BF16)
        outs[3][...] = mv
        return mv

    mix, x1, h2, m = _mm(y_ssd, w_o, "nn", name="out_proj", tm=tmh, tn=D, tk=D,
                         extras=[(y_pool, *_rows(tmh)), (proj, *_rows(tmh, 2 * D, gcol)),
                                 (xs_, *_rows(tmh)), (gate_m, *_vecs()), (norm_mlp_w, *_vecs()),
                                 (scale_f, *_vecs()), (shift_f, *_vecs())],
                         outs=[BF16, F32, BF16, whole_rows(D)(tmh)], prologue=merge_pro,
                         epilogue=lambda acc, ex, outs: _ep_resid_norm(acc, ex[2:], outs[:3]))

    def relu2(acc, ex, outs):
        r = jnp.maximum(acc, 0.0)
        outs[0][...] = acc.astype(BF16)
        outs[1][...] = (r * r).astype(BF16)

    up, act = _mm(h2, w_up_t, "nt", name="mlp_up", outs=[BF16, BF16], tm=tm2, tn=1024, tk=D, epilogue=relu2)

    dx2, ddown, loss_p, dnwf, dgate_f = _mm(
        act, w_dn, "nn", name="mlp_down", tm=tmh, tn=D, tk=DFF,
        extras=[(x1, *_rows(tmh)), (tgt, *_rows(tmh)), (gate_f, *_vecs()), (norm_final_w.reshape(1, D), *_vecs())],
        outs=[F32, BF16, _sum_out(128), _sum_out(), _sum_out()], epilogue=_ep_final)

    def drelu2(acc, ex, outs):
        outs[0][...] = (acc * (2.0 * jnp.maximum(ex[0][...].astype(F32), 0.0))).astype(BF16)

    def dep_last(ep):
        return lambda acc, ex, outs: ep(acc, ex[:-1], outs)

    dup = _mm(ddown, w_dn, "nt", name="mlp_down_dx", outs=[BF16], tm=tm2, tn=1024, tk=D,
              extras=[(up, (tm2, 1024), lambda i, j, k: (i, j))], epilogue=drelu2)
    g_dn = _mm(act, ddown, "tn", name="mlp_down_dw", outs=[SLAB_DT], tm=1024, tn=D, tk=tkl)
    g_up_t = _mm(dup, h2, "tn", name="mlp_up_dw", outs=[SLAB_DT], tm=1024, tn=D, tk=tkl)
    gslab_mlp = jnp.concatenate([g_up_t.reshape(N_DEV, 512, D), g_dn.reshape(N_DEV, 512, D)], axis=1)
    mlp_started = _xchg_start(gslab_mlp, per_peer=True, name="scatter_mlp_start")
    dx1, p2, q2, dmix, dgate_m = _mm(
        dup, w_up_t, "nn", name="mlp_up_dx", tm=tmh, tn=D, tk=DFF,
        extras=[(x1, *_rows(tmh)), (dx2, *_rows(tmh)), (norm_mlp_w, *_vecs()), (scale_f, *_vecs()),
                (mix, *_rows(tmh)), (gate_m, *_vecs()), _dep(mlp_started[4])],
        outs=[F32, _sum_out(), _sum_out(), BF16, _sum_out()], epilogue=dep_last(_ep_norm_bwd))
    gcol = C_GATE // (2 * D)
    dy_ssd, dy_pool, dproj = _mm(
        dmix, w_o, "nt", name="out_proj_dx", tm=tmh, tn=D, tk=D,
        extras=[(y_ssd, *_rows(tmh)), (y_pool, *_rows(tmh)), (proj, *_rows(tmh, 2 * D, gcol))],
        outs=[BF16, BF16, ((L, NPROJ), BF16, *_rows(tmh, 2 * D, gcol))], epilogue=_ep_merge_bwd)
    g_o = _mm(m, dmix, "tn", name="out_proj_dw", outs=[SLAB_DT], tm=D, tn=D, tk=tkl)
    zcol = C_Z // DI
    dy_ssm, dproj, d_snw = _mm(
        dy_ssd, w_bssd, "nt", name="branch_ssd_dx", tm=tmh, tn=DI, tk=D,
        extras=[(y_ssm, *_rows(tmh, DI)), (proj, *_rows(tmh, DI, zcol)), (ssd_norm_w, *_vecs(DI)),
                (dproj, None, None)],
        outs=[F32, ((L, NPROJ), BF16, *_rows(tmh, DI, zcol)), _sum_out(DI)],
        epilogue=_ep_gated_norm_bwd, aliases={3: 1})
    g_bssd = _mm(yn, dy_ssd, "tn", name="branch_ssd_dw", outs=[SLAB_DT], tm=1024, tn=D, tk=tkl)
    dxbc, dproj, d_a, d_dx, d_dtb = _ssd_bwd(dy_ssm, xbc, dtp, hs, dtb, arow, dsk_x, dproj, name="ssd_bwd")
    dproj, d_cw, d_cb = _conv_bwd(xbc_raw, dxbc, conv_full, conv_b, dproj, name="conv_bwd")
    def pool_bwd_ep(acc, ex, outs):
        y_ref, s_ref, wp_ref = ex
        o_ref, ds_ref, dpool_ref = outs
        dyp0_v = (acc * s_ref[...]).astype(BF16)
        o_ref[...] = dyp0_v
        _acc_out(ds_ref, _colsum(acc * y_ref[...].astype(F32)), _row_step())
        for g in range(4):
            sl = slice(g * PGW, (g + 1) * PGW)
            dpool_ref[:, sl] = _dot(dyp0_v[:, sl], wp_ref[sl, :], NT)

    dyp0, d_ps, dpooled = _mm(dy_pool, w_bpool, "nt", name="branch_pool_dx", tm=tm, tn=D, tk=D,
                              extras=[(yp0, *_rows(tm)), (pool_scale, *_vecs()), (w_pool, *wp_spec)],
                              outs=[BF16, _sum_out(), F32], epilogue=pool_bwd_ep)
    g_bpool = _mm(yp1, dy_pool, "tn", name="branch_pool_dw", outs=[SLAB_DT], tm=D, tn=D, tk=tkl)
    g_pool = _mm_pool_tn(pooled, dyp0, name="pool_mix_dw", tk=tkl)
    gslab_mix = jnp.concatenate([
        g_bssd.reshape(N_DEV, 256, D),
        g_pool.reshape(4, N_DEV, 32, PGW).transpose(1, 0, 2, 3).reshape(N_DEV, 32, D).astype(SLAB_DT),
        g_bpool.reshape(N_DEV, 128, D),
        g_o.reshape(N_DEV, 128, D)], axis=1)
    mix_started = _xchg_start(gslab_mix, per_peer=True, name="scatter_mix_start")
    dproj = _pool_bwd(dpooled, dproj, name="pool_bwd")
    g_in_t = _mm(dproj, h1, "tn", name="in_proj_dw", outs=[SLAB_DT], tm=1408, tn=D, tk=tkl2,
                 extras=[_dep(mix_started[4])])
    gslab_in = _restore_in_shards(g_in_t)
    in_started = _xchg_start(gslab_in, per_peer=True, name="scatter_in_start")
    grad_x, p1, q1 = _mm(
        dproj, w_in_t, "nn", name="in_proj_dx", tm=tmh, tn=D, tk=2816,
        extras=[(xs_, *_rows(tmh)), (dx1, *_rows(tmh)), (norm_mix_w, *_vecs()), (scale_m, *_vecs()),
                _dep(in_started[4])],
        outs=[F32, _sum_out(), _sum_out()], epilogue=dep_last(_ep_norm_bwd))

    def landed(started, after, tile, name):
        src, land = _xchg_wait(started, after, per_peer=True, name=name + "_wait")
        own = lax.dynamic_slice_in_dim(src, me, 1, axis=0)
        return _slab_sum(lax.dynamic_update_slice(land, own, (me, 0, 0)), tile=tile, name=name + "_sum")

    gsum_mlp = landed(mlp_started, grad_x, 256, "scatter_mlp")
    gsum_mix = landed(mix_started, grad_x, 272, "scatter_mix")
    gsum_in = landed(in_started, grad_x, 208, "scatter_in")

    dmod = jnp.concatenate([q1, p1 * norm_mix_w, dgate_m, q2, p2 * norm_mlp_w, dgate_f], axis=1)
    d_alog = d_a[:, :NH] * (-jnp.exp(a_log))
    sv = _pack_sv({
        "b_ada": dmod, "norm_mix_w": p1 * (1.0 + scale_m), "conv_b": d_cb, "dt_bias": d_dtb[:, :NH],
        "a_log": d_alog, "d_skip": d_dx.reshape(NH, HP).sum(axis=1), "ssd_norm_w": d_snw,
        "pool_scale": d_ps, "norm_mlp_w": p2 * (1.0 + scale_f), "norm_final_w": dnwf, "conv_w": d_cw,
        "loss": loss_p[:, :1]})
    sv_all, sv_sum = _small_allsum(sv, name="small_allsum")
    flat = sv_sum.reshape(-1)
    loss = flat[SV_OFF["loss"]]
    dmod_all = sv_all.reshape(N_DEV, SV_ROWS * 128)[:, :6 * D]
    g_w_ada = _ada_bwd(c_all, lax.dynamic_slice_in_dim(dmod_all, me * wloc, wloc, axis=1), name="ada_bwd")

    g_conv_w = lax.dynamic_slice_in_dim(_sv_get(flat, "conv_w", 4 * XBC).reshape(4, XBC),
                                        me * (XBC // N_DEV), XBC // N_DEV, axis=1)
    small = [("b_ada", b_ada, m_b_ada, v_b_ada), ("norm_mix_w", norm_mix_w, m_norm_mix_w, v_norm_mix_w),
             ("conv_b", conv_b, m_conv_b, v_conv_b), ("dt_bias", dt_bias, m_dt_bias, v_dt_bias),
             ("a_log", a_log, m_a_log, v_a_log), ("d_skip", d_skip, m_d_skip, v_d_skip),
             ("ssd_norm_w", ssd_norm_w, m_ssd_norm_w, v_ssd_norm_w),
             ("pool_scale", pool_scale, m_pool_scale, v_pool_scale),
             ("norm_mlp_w", norm_mlp_w, m_norm_mlp_w, v_norm_mlp_w),
             ("norm_final_w", norm_final_w[None], m_norm_final_w[None], v_norm_final_w[None]),
             ("conv_w", conv_w[0], m_conv_w[0], v_conv_w[0])]
    small_out = _adamw_small(sv_sum.reshape(1, SV_ROWS * 128), g_conv_w, small, name="adamw_small")
    small_out["norm_final_w"] = tuple(a[0] for a in small_out["norm_final_w"])
    small_out["conv_w"] = tuple(a[None] for a in small_out["conv_w"])

    def gpart(n, rows_):
        return gsum_mix[MIX_OFF[n]:MIX_OFF[n] + rows_]

    def lin(a):
        return a[0].T.reshape(IN_ROWS * 8, 128)

    g_lin = lax.dynamic_slice_in_dim(gsum_in, in_shift, IN_ROWS, axis=0).reshape(IN_ROWS * 8, 128)
    dlt, mn, vn = _adamw(lin(w_in), g_lin, lin(m_w_in), lin(v_w_in), name="adamw_w_in", tr=IN_ROWS * 2)
    big_in = tuple(a.reshape(IN_ROWS, D).T[None] for a in (g_lin, dlt, mn, vn))

    big = {
        "w_ada": (w_ada, m_w_ada, v_w_ada, g_w_ada, (D, wloc)),
        "w_branch_ssd": (w_branch_ssd, m_w_branch_ssd, v_w_branch_ssd, gpart("bssd", 256), (256, D)),
        "pool_w": (pool_w, m_pool_w, v_pool_w, gpart("pool", 32).reshape(128, PGW), (128, PGW)),
        "w_branch_pool": (w_branch_pool, m_w_branch_pool, v_w_branch_pool, gpart("bpool", 128), (128, D)),
        "w_out": (w_out, m_w_out, v_w_out, gpart("out", 128), (128, D)),
        "w_up": (w_up, m_w_up, v_w_up, gsum_mlp[:512].T, (D, 512)),
        "w_down": (w_down, m_w_down, v_w_down, gsum_mlp[512:], (512, D)),
    }
    big_out = {}
    for n, (w, mm_, vv, g, shp2) in big.items():
        dlt, mn, vn = _adamw(w.reshape(shp2), g, mm_.reshape(shp2), vv.reshape(shp2), name="adamw_" + n)
        big_out[n] = (g.reshape(w.shape), dlt.reshape(w.shape), mn.reshape(w.shape), vn.reshape(w.shape))

    order = ["w_ada", "b_ada", "norm_mix_w", "w_in", "conv_w", "conv_b", "dt_bias", "a_log", "d_skip",
             "ssd_norm_w", "w_branch_ssd", "pool_w", "pool_scale", "w_branch_pool", "w_out", "norm_mlp_w",
             "w_up", "w_down", "norm_final_w"]
    big_out["w_in"] = big_in
    res = {**small_out, **big_out}
    outs = [loss, grad_x.reshape(x.shape)]
    for k in range(4):
        outs += [res[n][k] for n in order]
    return tuple(outs)
```

```python
import functools

import numpy as np
import jax
import jax.numpy as jnp
from jax import lax
from jax.experimental import pallas as pl
from jax.experimental.pallas import tpu as pltpu

F32 = jnp.float32
BF16 = jnp.bfloat16
SLAB_DT = jnp.bfloat16
_MXU_DTYPE = jnp.bfloat16

N_DEV = 8
D = 1024
DI = 2048
NH = 32
HP = 64
NG = 4
NS = 128
Q = 128
XBC = DI + 2 * NG * NS
DFF = 4096
N_IN = 8224
EPS = 1e-5
POOL_W = 1024
PGW = 256

C_XBC, C_POOL, C_Z, C_GATE, C_DT = 0, 3072, 4096, 6144, 8192
DT_PAD = 256
NPROJ = C_DT + DT_PAD

IN_ROWS = N_IN // N_DEV
IN_ROWS_P = 1040
CONV_ROWS = 16
REST_PARTS = (("bssd", 256), ("pool", 32), ("bpool", 128), ("out", 128), ("up", 512), ("down", 512))
REST_OFF = {}
_o = 0
for _n, _r in REST_PARTS:
    REST_OFF[_n] = _o
    _o += _r
REST_ROWS = _o
MIX_PARTS = (("bssd", 256), ("pool", 32), ("bpool", 128), ("out", 128))
MIX_OFF = {}
_o = 0
for _n, _r in MIX_PARTS:
    MIX_OFF[_n] = _o
    _o += _r
MIX_ROWS = _o

SV_PARTS = (("b_ada", 6144), ("norm_mix_w", 1024), ("conv_b", 3072), ("dt_bias", 128), ("a_log", 128),
            ("d_skip", 128), ("ssd_norm_w", 2048), ("pool_scale", 1024), ("norm_mlp_w", 1024),
            ("norm_final_w", 1024), ("conv_w", 4 * XBC), ("loss", 128))
SV_OFF = {}
_o = 0
for _n, _r in SV_PARTS:
    SV_OFF[_n] = _o
    _o += _r
SV_ROWS = 224
assert _o <= SV_ROWS * 128

ADAM_LR, ADAM_B1, ADAM_B2, ADAM_EPS, ADAM_WD, ADAM_STEP = 0.001, 0.9, 0.999, 1e-08, 0.01, 10

VMEM_BIG = 56 * 1024 * 1024
NEG = -1e30

NN = ((1,), (0,))
NT = ((1,), (1,))
TN = ((0,), (0,))


def _dot(a, b, dims=NN):
    return lax.dot_general(a.astype(_MXU_DTYPE), b.astype(_MXU_DTYPE), (dims, ((), ())),
                           preferred_element_type=F32)


def _dot_hi(a, b, dims=NN):
    return lax.dot_general(a.astype(F32), b.astype(F32), (dims, ((), ())),
                           precision=lax.Precision.HIGHEST, preferred_element_type=F32)


def _pick(n, cands):
    for c in cands:
        if n % c == 0:
            return c
    return n


def _sigmoid(x):
    return 1.0 / (1.0 + jnp.exp(-x))


def _silu(x):
    return x * _sigmoid(x)


def _dsilu(x):
    s = _sigmoid(x)
    return s * (1.0 + x * (1.0 - s))


def _softplus(x):
    return jnp.maximum(x, 0.0) + jnp.log(1.0 + jnp.exp(-jnp.abs(x)))


def _params(sem, vmem=None):
    return pltpu.CompilerParams(dimension_semantics=sem, vmem_limit_bytes=vmem)


def _row_step():
    return pl.program_id(0)


def _mm(a, b, mode, *, name, outs, tm, tn, tk, extras=(), epilogue=None, aliases=None, prologue=None):
    if mode == "tn":
        K, M = a.shape
        N = b.shape[1]
        a_spec = pl.BlockSpec((tk, tm), lambda i, j, k: (k, i))
        b_spec = pl.BlockSpec((tk, tn), lambda i, j, k: (k, j))
        dims = TN
    else:
        M = a.shape[0]
        K = b.shape[0] if mode == "nn" else b.shape[1]
        if prologue is None:
            assert a.shape[1] == K
            a_spec = pl.BlockSpec((tm, tk), lambda i, j, k: (i, k))
        else:
            assert tk == K
            a_spec = pl.BlockSpec((tm, a.shape[1]), lambda i, j, k: (i, 0))
        if mode == "nn":
            N = b.shape[1]
            b_spec = pl.BlockSpec((tk, tn), lambda i, j, k: (k, j))
            dims = NN
        else:
            N = b.shape[0]
            b_spec = pl.BlockSpec((tn, tk), lambda i, j, k: (j, k))
            dims = NT
    assert M % tm == 0 and N % tn == 0 and K % tk == 0, (name, M, N, K, tm, tn, tk)
    nk = K // tk
    ne, no = len(extras), len(outs)
    if epilogue is None:
        def epilogue(acc, ex, out_refs):
            out_refs[0][...] = acc.astype(out_refs[0].dtype)

    def body(a_ref, b_ref, *rest):
        ex, out_refs = rest[:ne], rest[ne:ne + no]
        lhs = a_ref[...] if prologue is None else prologue(a_ref, ex, out_refs, pl.program_id(1))
        p = _dot(lhs, b_ref[...], dims)
        if nk == 1:
            epilogue(p, ex, out_refs)
        else:
            acc = rest[-1]
            k = pl.program_id(2)

            @pl.when(k == 0)
            def _():
                acc[...] = p

            @pl.when(jnp.logical_and(k > 0, k < nk - 1))
            def _():
                acc[...] += p

            @pl.when(k == nk - 1)
            def _():
                epilogue(acc[...] + p, ex, out_refs)

    out_specs, out_shape = [], []
    for o in outs:
        if isinstance(o, tuple):
            shape, dt, bs, im = o
            out_specs.append(pl.BlockSpec(bs, im))
            out_shape.append(jax.ShapeDtypeStruct(shape, dt))
        else:
            out_specs.append(pl.BlockSpec((tm, tn), lambda i, j, k: (i, j)))
            out_shape.append(jax.ShapeDtypeStruct((M, N), o))
    in_specs = [a_spec, b_spec]
    for _, bs, im in extras:
        in_specs.append(pl.BlockSpec(memory_space=pl.ANY) if bs is None else pl.BlockSpec(bs, im))
    res = pl.pallas_call(
        body, name=name,
        grid=(M // tm, N // tn, nk),
        in_specs=in_specs, out_specs=out_specs, out_shape=out_shape,
        scratch_shapes=[pltpu.VMEM((tm, tn), F32)] if nk > 1 else [],
        input_output_aliases={2 + e: o for e, o in (aliases or {}).items()},
        compiler_params=_params(("arbitrary", "arbitrary", "arbitrary"), VMEM_BIG),
    )(a, b, *[e[0] for e in extras])
    return res if no > 1 else res[0]


def _rows(tm, w=D, col=0):
    return (tm, w), lambda i, j, k, c=col: (i, c)


def _vecs(w=D, col=0):
    return (1, w), lambda i, j, k, c=col: (0, c)


def _sum_out(w=D):
    return ((1, w), F32, (1, w), lambda i, j, k: (0, 0))


def _mm_pool_tn(a, b, *, name, tk):
    L = a.shape[0]

    def body(a_ref, b_ref, o_ref):
        p = _dot(a_ref[...], b_ref[...], TN)

        @pl.when(pl.program_id(1) == 0)
        def _():
            o_ref[...] = p

        @pl.when(pl.program_id(1) > 0)
        def _():
            o_ref[...] += p

    blk = pl.BlockSpec((tk, PGW), lambda g, k: (k, g))
    return pl.pallas_call(body, name=name, grid=(4, L // tk), in_specs=[blk, blk],
                          out_specs=pl.BlockSpec((PGW, PGW), lambda g, k: (g, 0)),
                          out_shape=jax.ShapeDtypeStruct((POOL_W, PGW), F32),
                          compiler_params=_params(("parallel", "arbitrary")))(a, b)


def _acc_out(ref, val, i):
    @pl.when(i == 0)
    def _():
        ref[...] = val

    @pl.when(i > 0)
    def _():
        ref[...] += val


def _colsum(v):
    return jnp.sum(v, axis=0, keepdims=True)


def _ep_resid_norm(acc, ex, outs):
    x_ref, g_ref, nw_ref, sc_ref, sh_ref = ex
    mix_ref, x1_ref, h_ref = outs
    mix_ref[...] = acc.astype(mix_ref.dtype)
    xv = x_ref[...] + g_ref[...] * acc
    x1_ref[...] = xv
    r = lax.rsqrt(jnp.mean(xv * xv, axis=-1, keepdims=True) + EPS)
    h_ref[...] = (xv * r * nw_ref[...] * (1.0 + sc_ref[...]) + sh_ref[...]).astype(h_ref.dtype)


def _ep_final(acc, ex, outs):
    x1_ref, t_ref, g_ref, nw_ref = ex
    dx2_ref, dd_ref, loss_ref, dnw_ref, dg_ref = outs
    i = _row_step()
    x2 = x1_ref[...] + g_ref[...] * acc
    r = lax.rsqrt(jnp.mean(x2 * x2, axis=-1, keepdims=True) + EPS)
    xh = x2 * r
    e = xh * nw_ref[...] - t_ref[...]
    part = 0.5 * jnp.sum(jnp.mean(e * e, axis=-1, keepdims=True), axis=0, keepdims=True)
    dy = e * (1.0 / D)
    g = dy * nw_ref[...]
    dx2 = r * (g - xh * jnp.mean(g * xh, axis=-1, keepdims=True))
    dx2_ref[...] = dx2
    dd_ref[...] = (dx2 * g_ref[...]).astype(dd_ref.dtype)
    _acc_out(loss_ref, jnp.broadcast_to(part, (1, 128)), i)
    _acc_out(dnw_ref, _colsum(dy * xh), i)
    _acc_out(dg_ref, _colsum(dx2 * acc), i)


def _ep_norm_bwd(acc, ex, outs):
    x_ref, dr_ref, nw_ref, sc_ref = ex[:4]
    dx_ref, p_ref, q_ref = outs[:3]
    i = _row_step()
    xv = x_ref[...]
    r = lax.rsqrt(jnp.mean(xv * xv, axis=-1, keepdims=True) + EPS)
    xh = xv * r
    g = acc * (nw_ref[...] * (1.0 + sc_ref[...]))
    dx = dr_ref[...] + r * (g - xh * jnp.mean(g * xh, axis=-1, keepdims=True))
    dx_ref[...] = dx
    _acc_out(p_ref, _colsum(acc * xh), i)
    _acc_out(q_ref, _colsum(acc), i)
    if len(ex) > 4:
        m_ref, g_ref = ex[4:]
        dm_ref, dg_ref = outs[3:]
        dm_ref[...] = (dx * g_ref[...]).astype(dm_ref.dtype)
        _acc_out(dg_ref, _colsum(dx * m_ref[...].astype(F32)), i)


def _ep_merge_bwd(acc, ex, outs):
    a_ref, b_ref, gl_ref = ex
    da_ref, db_ref, dgl_ref = outs
    s = _sigmoid(gl_ref[...].astype(F32))
    s1, s2 = s[:, :D], s[:, D:]
    da_ref[...] = (acc * s1).astype(da_ref.dtype)
    db_ref[...] = (acc * s2).astype(db_ref.dtype)
    dgl_ref[:, :D] = (acc * a_ref[...] * s1 * (1.0 - s1)).astype(dgl_ref.dtype)
    dgl_ref[:, D:] = (acc * b_ref[...] * s2 * (1.0 - s2)).astype(dgl_ref.dtype)


GW = DI // NG


def _ep_gated_norm_bwd(acc, ex, outs):
    y_ref, z_ref, w_ref, _ = ex
    dy_ref, dz_ref, dw_ref = outs
    zv = z_ref[...].astype(F32)
    yv = y_ref[...].astype(F32)
    sz = _silu(zv)
    yg = yv * sz
    dsz = _dsilu(zv)
    dws = []
    for k in range(NG):
        sl = slice(k * GW, (k + 1) * GW)
        seg = yg[:, sl]
        r = lax.rsqrt(jnp.mean(seg * seg, axis=-1, keepdims=True) + EPS)
        sh = seg * r
        dn = acc[:, sl]
        g = dn * w_ref[:, sl]
        dyg = r * (g - sh * jnp.mean(g * sh, axis=-1, keepdims=True))
        dy_ref[:, sl] = dyg * sz[:, sl]
        dz_ref[:, sl] = (dyg * yv[:, sl] * dsz[:, sl]).astype(dz_ref.dtype)
        dws.append(_colsum(dn * sh))
    _acc_out(dw_ref, jnp.concatenate(dws, axis=1), _row_step())


CONV_CB = 128
HALO = 16


def _time_chunk(L):
    return _pick(L, (256, 128))


def _with_halo(x_ref, i, r0, rc):
    p0 = pl.multiple_of(jnp.maximum(r0 - HALO, 0), HALO)
    prev = jnp.where(i > 0, x_ref[pl.ds(p0, HALO), :].astype(F32), 0.0)
    return jnp.concatenate([prev, x_ref[pl.ds(r0, rc), :].astype(F32)], axis=0)


def _conv_fwd(proj, w, b, *, name):
    L = proj.shape[0]
    rc = _time_chunk(L)
    n = L // rc

    def body(x_ref, w_ref, b_ref, o_ref):
        wv = w_ref[...]
        bv = b_ref[...]

        def step(i, c):
            r0 = pl.multiple_of(i * rc, rc)
            ext = _with_halo(x_ref, i, r0, rc)
            acc = bv + ext * wv[3:4]
            for j in (1, 2, 3):
                acc = acc + pltpu.roll(ext, j, 0) * wv[3 - j:4 - j]
            acc = acc[HALO:]
            o_ref[pl.ds(r0, rc), :] = acc * _sigmoid(acc)
            return c

        lax.fori_loop(0, n, step, 0)

    return pl.pallas_call(
        body, name=name, grid=(XBC // CONV_CB,),
        in_specs=[pl.BlockSpec((L, CONV_CB), lambda j: (0, j + C_XBC // CONV_CB)),
                  pl.BlockSpec((4, CONV_CB), lambda j: (0, j)), pl.BlockSpec((1, CONV_CB), lambda j: (0, j))],
        out_specs=pl.BlockSpec((L, CONV_CB), lambda j: (0, j)),
        out_shape=jax.ShapeDtypeStruct((L, XBC), F32),
        compiler_params=_params(("parallel",), VMEM_BIG))(proj, w, b)


def _conv_bwd(proj, dy, w, b, dproj, *, name):
    L = proj.shape[0]
    rc = _time_chunk(L)
    n = L // rc

    def body(x_ref, dy_ref, w_ref, b_ref, dp_in, dx_ref, dw_ref, db_ref):
        del dp_in
        wv = w_ref[...]
        bv = b_ref[...]

        def step(k, carry):
            nxt, db, d0, d1, d2, d3 = carry
            i = n - 1 - k
            r0 = pl.multiple_of(i * rc, rc)
            ext = _with_halo(x_ref, i, r0, rc)
            xk = [ext[HALO:]] + [pltpu.roll(ext, j, 0)[HALO:] for j in (1, 2, 3)]
            pre = bv
            for j in range(4):
                pre = pre + xk[j] * wv[3 - j:4 - j]
            dpre = dy_ref[pl.ds(r0, rc), :] * _dsilu(pre)
            dext = jnp.concatenate([dpre, nxt], axis=0)
            acc = dext * wv[3:4]
            for j in (1, 2, 3):
                acc = acc + pltpu.roll(dext, rc + HALO - j, 0) * wv[3 - j:4 - j]
            dx_ref[pl.ds(r0, rc), :] = acc[:rc].astype(dx_ref.dtype)
            return (dpre[:HALO], db + _colsum(dpre), d0 + _colsum(dpre * xk[3]), d1 + _colsum(dpre * xk[2]),
                    d2 + _colsum(dpre * xk[1]), d3 + _colsum(dpre * xk[0]))

        z = jnp.zeros((1, CONV_CB), F32)
        _, db, d0, d1, d2, d3 = lax.fori_loop(0, n, step, (jnp.zeros((HALO, CONV_CB), F32), z, z, z, z, z))
        db_ref[...] = db
        dw_ref[...] = jnp.concatenate([d0, d1, d2, d3], axis=0)

    nb = XBC // CONV_CB
    return pl.pallas_call(
        body, name=name, grid=(nb,),
        in_specs=[pl.BlockSpec((L, CONV_CB), lambda j: (0, j + C_XBC // CONV_CB)),
                  pl.BlockSpec((L, CONV_CB), lambda j: (0, j)),
                  pl.BlockSpec((4, CONV_CB), lambda j: (0, j)), pl.BlockSpec((1, CONV_CB), lambda j: (0, j)),
                  pl.BlockSpec(memory_space=pl.ANY)],
        out_specs=[pl.BlockSpec((L, CONV_CB), lambda j: (0, j + C_XBC // CONV_CB)),
                   pl.BlockSpec((4, CONV_CB), lambda j: (0, j)), pl.BlockSpec((1, CONV_CB), lambda j: (0, j))],
        out_shape=[jax.ShapeDtypeStruct((L, NPROJ), BF16), jax.ShapeDtypeStruct((4, XBC), F32),
                   jax.ShapeDtypeStruct((1, XBC), F32)],
        input_output_aliases={4: 0},
        compiler_params=_params(("parallel",), VMEM_BIG))(proj, dy, w, b, dproj)


def _pool_fwd(proj, *, name):
    L = proj.shape[0]
    rc = _time_chunk(L)
    n = L // rc

    def body(x_ref, o_ref, pad):
        g = pl.program_id(0)
        pad[0:HALO, :] = jnp.zeros((HALO, PGW), F32)

        def fill(i, c):
            r0 = pl.multiple_of(i * rc, rc)
            pad[pl.ds(r0 + HALO, rc), :] = x_ref[pl.ds(r0, rc), :].astype(F32)
            return c

        lax.fori_loop(0, n, fill, 0)
        rows = lax.broadcasted_iota(jnp.int32, (rc, PGW), 0)

        for gi in range(4):
            win = 2 << gi

            @pl.when(g == gi)
            def _(gi=gi, win=win):
                def step(i, c):
                    r0 = pl.multiple_of(i * rc, rc)
                    ext = pad[pl.ds(r0, rc + HALO), :]
                    s = ext
                    sh = 1
                    while sh < win:
                        s = s + pltpu.roll(s, sh, 0)
                        sh *= 2
                    cnt = jnp.minimum(rows + (r0 + 1), win).astype(F32)
                    o_ref[pl.ds(r0, rc), :] = (s[HALO:] / cnt - ext[HALO:]).astype(o_ref.dtype)
                    return c

                lax.fori_loop(0, n, step, 0)

    return pl.pallas_call(
        body, name=name, grid=(4,),
        in_specs=[pl.BlockSpec((L, PGW), lambda j: (0, j + C_POOL // PGW))],
        out_specs=pl.BlockSpec((L, PGW), lambda j: (0, j)),
        out_shape=jax.ShapeDtypeStruct((L, POOL_W), BF16),
        scratch_shapes=[pltpu.VMEM((L + HALO, PGW), F32)],
        compiler_params=_params(("parallel",), VMEM_BIG))(proj)


def _pool_bwd(dpooled, dproj, *, name):
    L = dpooled.shape[0]
    rc = _time_chunk(L)
    n = L // rc

    def body(d_ref, dp_in, o_ref, pad):
        del dp_in
        g = pl.program_id(0)
        pad[L:L + HALO, :] = jnp.zeros((HALO, PGW), F32)
        rows = lax.broadcasted_iota(jnp.int32, (rc, PGW), 0)

        for gi in range(4):
            win = 2 << gi

            @pl.when(g == gi)
            def _(gi=gi, win=win):
                def fill(i, c):
                    r0 = pl.multiple_of(i * rc, rc)
                    cnt = jnp.minimum(rows + (r0 + 1), win).astype(F32)
                    pad[pl.ds(r0, rc), :] = d_ref[pl.ds(r0, rc), :] / cnt
                    return c

                lax.fori_loop(0, n, fill, 0)

                def step(i, c):
                    r0 = pl.multiple_of(i * rc, rc)
                    s = pad[pl.ds(r0, rc + HALO), :]
                    sh = 1
                    while sh < win:
                        s = s + pltpu.roll(s, rc + HALO - sh, 0)
                        sh *= 2
                    o_ref[pl.ds(r0, rc), :] = (s[:rc] - d_ref[pl.ds(r0, rc), :]).astype(o_ref.dtype)
                    return c

                lax.fori_loop(0, n, step, 0)

    return pl.pallas_call(
        body, name=name, grid=(4,),
        in_specs=[pl.BlockSpec((L, PGW), lambda j: (0, j)), pl.BlockSpec(memory_space=pl.ANY)],
        out_specs=pl.BlockSpec((L, PGW), lambda j: (0, j + C_POOL // PGW)),
        out_shape=jax.ShapeDtypeStruct((L, NPROJ), BF16),
        scratch_shapes=[pltpu.VMEM((L + HALO, PGW), F32)],
        input_output_aliases={1: 0},
        compiler_params=_params(("parallel",), VMEM_BIG))(dpooled, dproj)


_SPLIT_DT = jnp.bfloat16


def _ssd_consts():
    tri = np.tril(np.ones((Q, Q), np.float32))
    exp = np.zeros((128, DI), np.float32)
    for h in range(NH):
        exp[h, h * HP:(h + 1) * HP] = 1.0
    exp2 = np.concatenate([exp, exp], axis=0)
    return (jnp.asarray(tri, dtype=_SPLIT_DT), jnp.asarray(tri.T.copy(), dtype=_SPLIT_DT),
            jnp.asarray(exp2, dtype=_SPLIT_DT))


def _split(v, n):
    parts, r = [], v
    for _ in range(n):
        p = r.astype(_SPLIT_DT)
        parts.append(p)
        r = r - p.astype(F32)
    return parts


def _bdot(a, b, dims):
    return lax.dot_general(a, b, (dims, ((), ())), preferred_element_type=F32)


def _tri_sum(t_ref, v):
    r = _bdot(t_ref[...], jnp.concatenate(_split(v, 3), axis=1), NN)
    return r[:, :128] + r[:, 128:256] + r[:, 256:]


def _expand(v, e2_ref):
    return _bdot(jnp.concatenate(_split(v, 2), axis=1), e2_ref[...], NN)


def _reduce_heads(vals, eg):
    parts = []
    for v in vals:
        parts += _split(v, 2)
    r = _bdot(jnp.concatenate(parts, axis=0), eg, NT)
    return [r[2 * i * Q:(2 * i + 1) * Q] + r[(2 * i + 1) * Q:(2 * i + 2) * Q] for i in range(len(vals))]


def _ssd_common(xbc_ref, dtw_ref, dtb_ref, arow_ref, t_ref, e_ref):
    pre = dtw_ref[:, :128] + dtb_ref[...]
    dt = _softplus(pre)
    acs = _tri_sum(t_ref, dt * arow_ref[...])
    acs_x = _expand(acs, e_ref)
    dt_x = _expand(dt, e_ref)
    xs = xbc_ref[:, 0:DI]
    return pre, dt, acs, acs.T, acs_x, dt_x, xs


def _ssd_fwd(xbc, proj, dtb, arow, dsk_x, *, name):
    L = xbc.shape[0]
    nc = L // Q
    tri, _, expand = _ssd_consts()

    def body(xbc_ref, dtw_ref, dtb_ref, arow_ref, dsk_ref, t_ref, e_ref, y_ref, hs_ref, h_scr):
        @pl.when(pl.program_id(0) == 0)
        def _():
            h_scr[...] = jnp.zeros_like(h_scr)

        _, dt, acs, acs_t, acs_x, dt_x, xs = _ssd_common(xbc_ref, dtw_ref, dtb_ref, arow_ref, t_ref, e_ref)
        xdt = xs * dt_x
        eacs = jnp.exp(acs_x)
        acs_last = acs_x[Q - 1:Q, :]
        dec = jnp.exp(acs_last - acs_x)
        hs_ref[0] = h_scr[...].astype(hs_ref.dtype)
        causal = lax.broadcasted_iota(jnp.int32, (Q, Q), 0) >= lax.broadcasted_iota(jnp.int32, (Q, Q), 1)
        first = lax.broadcasted_iota(jnp.int32, (Q, 128), 1) < HP
        for g in range(NG):
            bg = xbc_ref[:, DI + g * NS:DI + (g + 1) * NS]
            cg = xbc_ref[:, DI + NG * NS + g * NS:DI + NG * NS + (g + 1) * NS]
            s = _dot(cg, bg, NT)
            sl = slice(g * GW, (g + 1) * GW)
            hg = h_scr[:, sl]
            yoff = _dot(cg, hg, NN) * eacs[:, sl]
            st = _dot(bg, xdt[:, sl] * dec[:, sl], TN)
            h_scr[:, sl] = hg * eacs[Q - 1:Q, sl] + st
            for j in range(4):
                lo = g * GW + j * 128
                xb = xdt[:, lo:lo + 128]
                yp = yoff[:, j * 128:(j + 1) * 128] + dsk_ref[:, lo:lo + 128] * xs[:, lo:lo + 128]
                for e in range(2):
                    h = g * 8 + j * 2 + e
                    lm = jnp.exp(jnp.where(causal, acs[:, h:h + 1] - acs_t[h:h + 1, :], NEG))
                    xm = jnp.where(first if e == 0 else jnp.logical_not(first), xb, 0.0)
                    yp = yp + _dot(s * lm, xm, NN)
                y_ref[:, lo:lo + 128] = yp.astype(y_ref.dtype)

    return pl.pallas_call(
        body, name=name, grid=(nc,),
        in_specs=[pl.BlockSpec((Q, XBC), lambda c: (c, 0)),
                  pl.BlockSpec((Q, DT_PAD), lambda c: (c, 0)),
                  pl.BlockSpec((1, 128), lambda c: (0, 0)), pl.BlockSpec((1, 128), lambda c: (0, 0)),
                  pl.BlockSpec((1, DI), lambda c: (0, 0)),
                  pl.BlockSpec((Q, Q), lambda c: (0, 0)), pl.BlockSpec((256, DI), lambda c: (0, 0))],
        out_specs=[pl.BlockSpec((Q, DI), lambda c: (c, 0)), pl.BlockSpec((1, NS, DI), lambda c: (c, 0, 0))],
        out_shape=[jax.ShapeDtypeStruct((L, DI), BF16), jax.ShapeDtypeStruct((nc, NS, DI), F32)],
        scratch_shapes=[pltpu.VMEM((NS, DI), F32)],
        compiler_params=_params(("arbitrary",), VMEM_BIG))(xbc, proj, dtb, arow, dsk_x, tri, expand)


def _ssd_bwd(dy, xbc, proj, hs, dtb, arow, dsk_x, dproj, *, name):
    L = xbc.shape[0]
    nc = L // Q
    tri, triu, expand = _ssd_consts()

    def body(dy_ref, xbc_ref, dtw_ref, hs_ref, dtb_ref, arow_ref, dsk_ref, t_ref, u_ref, e_ref, dp_in,
             dxbc_ref, ddtw_ref, da_ref, ddx_ref, ddtb_ref, dh_scr):
        del dp_in
        i = pl.program_id(0)

        @pl.when(i == 0)
        def _():
            dh_scr[...] = jnp.zeros_like(dh_scr)

        pre, dt, acs, acs_t, acs_x, dt_x, xs = _ssd_common(xbc_ref, dtw_ref, dtb_ref, arow_ref, t_ref, e_ref)
        dyv = dy_ref[...]
        xdt = xs * dt_x
        eacs = jnp.exp(acs_x)
        acs_last = acs_x[Q - 1:Q, :]
        dec = jnp.exp(acs_last - acs_x)
        gy = dyv * eacs
        causal = lax.broadcasted_iota(jnp.int32, (Q, Q), 0) >= lax.broadcasted_iota(jnp.int32, (Q, Q), 1)
        first = lax.broadcasted_iota(jnp.int32, (Q, 128), 1) < HP
        lane_h = lax.broadcasted_iota(jnp.int32, (Q, 128), 1)
        sub_h = lax.broadcasted_iota(jnp.int32, (128, Q), 0)
        last_row = lax.broadcasted_iota(jnp.int32, (Q, GW), 0) == Q - 1
        dacs = jnp.zeros((Q, 128), F32)
        dacs_t = jnp.zeros((128, Q), F32)
        ddt = jnp.zeros((Q, 128), F32)
        for g in range(NG):
            bg = xbc_ref[:, DI + g * NS:DI + (g + 1) * NS]
            cg = xbc_ref[:, DI + NG * NS + g * NS:DI + NG * NS + (g + 1) * NS]
            s = _dot(cg, bg, NT)
            sl = slice(g * GW, (g + 1) * GW)
            hg = hs_ref[0, :, sl].astype(F32)
            dhn = dh_scr[:, sl]
            eal = eacs[Q - 1:Q, sl]
            gg = gy[:, sl]
            dax = gg * _dot(cg, hg, NN)
            dcg = _dot(gg, hg, NT)
            dh_scr[:, sl] = _dot(cg, gg, TN) + dhn * eal
            dal = eal * _colsum(dhn * hg)
            xdd = xdt[:, sl] * dec[:, sl]
            dbg = _dot(xdd, dhn, NT)
            wv = _dot(bg, dhn, NN)
            dd = wv * xdd
            dax = dax - dd
            dal = dal + _colsum(dd)
            dax = dax + jnp.where(last_row, dal, 0.0)
            dxdt_g = wv * dec[:, sl]
            ds = jnp.zeros((Q, Q), F32)
            dxdt_blocks = []
            for j in range(4):
                lo = g * GW + j * 128
                xb = xdt[:, lo:lo + 128]
                dyb = dyv[:, lo:lo + 128]
                dxb = dxdt_g[:, j * 128:(j + 1) * 128]
                for e in range(2):
                    h = g * 8 + j * 2 + e
                    lm = jnp.exp(jnp.where(causal, acs[:, h:h + 1] - acs_t[h:h + 1, :], NEG))
                    m = s * lm
                    dym = jnp.where(first if e == 0 else jnp.logical_not(first), dyb, 0.0)
                    dm = _dot(dym, xb, NT)
                    r = dm * m
                    dacs = dacs + jnp.where(lane_h == h, jnp.sum(r, axis=1, keepdims=True), 0.0)
                    dacs_t = dacs_t + jnp.where(sub_h == h, _colsum(r), 0.0)
                    ds = ds + dm * lm
                    dxb = dxb + _dot(m, dym, TN)
                dxdt_blocks.append(dxb)
            dxdt = jnp.concatenate(dxdt_blocks, axis=1)
            dcg = dcg + _dot(ds, bg, NN)
            dbg = dbg + _dot(ds, cg, TN)
            dxbc_ref[:, DI + g * NS:DI + (g + 1) * NS] = dbg
            dxbc_ref[:, DI + NG * NS + g * NS:DI + NG * NS + (g + 1) * NS] = dcg
            dxbc_ref[:, sl] = dsk_ref[:, sl] * dyv[:, sl] + dxdt * dt_x[:, sl]
            ddt_g, dacs_g = _reduce_heads([dxdt * xs[:, sl], dax], e_ref[0:128, sl])
            ddt = ddt + ddt_g
            dacs = dacs + dacs_g
        dacs = dacs - dacs_t.T
        ddta = _tri_sum(u_ref, dacs)
        ddt = ddt + ddta * arow_ref[...]
        ddtw = jnp.where(lane_h < NH, ddt * _sigmoid(pre), 0.0)
        ddtw_ref[...] = jnp.concatenate([ddtw, jnp.zeros((Q, DT_PAD - 128), F32)], axis=1).astype(ddtw_ref.dtype)
        _acc_out(da_ref, _colsum(ddta * dt), i)
        _acc_out(ddx_ref, _colsum(dyv * xs), i)
        _acc_out(ddtb_ref, _colsum(ddtw), i)

    rev = lambda c: (nc - 1 - c, 0)
    const = lambda c: (0, 0)
    return pl.pallas_call(
        body, name=name, grid=(nc,),
        in_specs=[pl.BlockSpec((Q, DI), rev), pl.BlockSpec((Q, XBC), rev),
                  pl.BlockSpec((Q, DT_PAD), rev),
                  pl.BlockSpec((1, NS, DI), lambda c: (nc - 1 - c, 0, 0)),
                  pl.BlockSpec((1, 128), const), pl.BlockSpec((1, 128), const), pl.BlockSpec((1, DI), const),
                  pl.BlockSpec((Q, Q), const), pl.BlockSpec((Q, Q), const), pl.BlockSpec((256, DI), const),
                  pl.BlockSpec(memory_space=pl.ANY)],
        out_specs=[pl.BlockSpec((Q, XBC), rev),
                   pl.BlockSpec((Q, DT_PAD), lambda c: (nc - 1 - c, C_DT // DT_PAD)),
                   pl.BlockSpec((1, 128), const), pl.BlockSpec((1, DI), const), pl.BlockSpec((1, 128), const)],
        out_shape=[jax.ShapeDtypeStruct((L, XBC), F32), jax.ShapeDtypeStruct((L, NPROJ), BF16),
                   jax.ShapeDtypeStruct((1, 128), F32), jax.ShapeDtypeStruct((1, DI), F32),
                   jax.ShapeDtypeStruct((1, 128), F32)],
        scratch_shapes=[pltpu.VMEM((NS, DI), F32)],
        input_output_aliases={10: 1},
        compiler_params=_params(("arbitrary",), VMEM_BIG))(dy, xbc, proj, hs, dtb, arow, dsk_x, tri, triu,
                                                          expand, dproj)


def _adam_update(wv, gv, mv, vv):
    c1 = 1.0 - ADAM_B1 ** ADAM_STEP
    c2 = 1.0 - ADAM_B2 ** ADAM_STEP
    mn = ADAM_B1 * mv + (1.0 - ADAM_B1) * gv
    vn = ADAM_B2 * vv + (1.0 - ADAM_B2) * (gv * gv)
    return -ADAM_LR * ((mn / c1) / (jnp.sqrt(vn / c2) + ADAM_EPS) + ADAM_WD * wv), mn, vn


def _adamw(w, g, m, v, *, name, tr=None):
    R = w.shape[0]
    rest = tuple(w.shape[1:])
    if tr is None:
        tr = _pick(R, (256, 128, 64, 32, 16, 8))
    assert R % tr == 0

    def body(w_ref, g_ref, m_ref, v_ref, d_ref, mo_ref, vo_ref):
        d_ref[...], mo_ref[...], vo_ref[...] = _adam_update(w_ref[...], g_ref[...], m_ref[...], v_ref[...])

    zeros = (0,) * len(rest)
    spec = pl.BlockSpec((tr,) + rest, lambda i: (i,) + zeros)
    return pl.pallas_call(body, name=name, grid=(R // tr,), in_specs=[spec] * 4, out_specs=[spec] * 3,
                          out_shape=[jax.ShapeDtypeStruct(w.shape, F32)] * 3,
                          compiler_params=_params(("parallel",)))(w, g, m, v)


def _adamw_small(svrow, g_conv, params, *, name):
    n = len(params)

    def body(*refs):
        sv_ref, gc_ref = refs[0], refs[1]
        ins, outs = refs[2:2 + 3 * n], refs[2 + 3 * n:]
        for p, (key, w, _, _) in enumerate(params):
            w_ref, m_ref, v_ref = ins[3 * p:3 * p + 3]
            g_ref, d_ref, mo_ref, vo_ref = outs[4 * p:4 * p + 4]
            gv = gc_ref[...] if key == "conv_w" else sv_ref[:, SV_OFF[key]:SV_OFF[key] + w.shape[1]]
            g_ref[...] = gv
            d_ref[...], mo_ref[...], vo_ref[...] = _adam_update(w_ref[...], gv, m_ref[...], v_ref[...])

    vm = pl.BlockSpec(memory_space=pltpu.VMEM)
    args = [svrow, g_conv]
    shapes = []
    for _, w, m, v in params:
        args += [w, m, v]
        shapes += [jax.ShapeDtypeStruct(w.shape, F32)] * 4
    res = pl.pallas_call(body, name=name, in_specs=[vm] * len(args), out_specs=[vm] * len(shapes),
                         out_shape=shapes)(*args)
    return {key: tuple(res[4 * p:4 * p + 4]) for p, (key, _, _, _) in enumerate(params)}


def _slab_sum(recv, *, tile, name):
    rows = recv.shape[1]
    assert rows % tile == 0 and tile % 16 == 0

    def body(r_ref, o_ref):
        acc = r_ref[0].astype(F32)
        for j in range(1, N_DEV):
            acc = acc + r_ref[j].astype(F32)
        o_ref[...] = acc

    return pl.pallas_call(body, name=name, grid=(rows // tile,),
                          in_specs=[pl.BlockSpec((N_DEV, tile, D), lambda i: (0, i, 0))],
                          out_specs=pl.BlockSpec((tile, D), lambda i: (i, 0)),
                          out_shape=jax.ShapeDtypeStruct((rows, D), F32),
                          compiler_params=_params(("parallel",)))(recv)


MESH = pl.DeviceIdType.MESH


def _coords():
    return lax.axis_index("x"), lax.axis_index("y"), lax.axis_index("c")


def _peer(k):
    x, y, c = _coords()
    px = 1 - x if k & 4 else x
    py = 1 - y if k & 2 else y
    pc = 1 - c if k & 1 else c
    return (px, py, pc), 4 * px + 2 * py + pc


def _rcopy(src, dst, ssem, rsem, dev):
    return pltpu.make_async_remote_copy(src_ref=src, dst_ref=dst, send_sem=ssem, recv_sem=rsem,
                                        device_id=dev, device_id_type=MESH)


def _exchange_all(src_of, dst_slot, send_sems, recv_sems):
    x, y, c = _coords()
    me = 4 * x + 2 * y + c
    sent = []
    for k in range(1, N_DEV):
        dev, pidx = _peer(k)
        cp = _rcopy(src_of(pidx), dst_slot(me), send_sems.at[k - 1], recv_sems.at[k - 1], dev)
        cp.start()
        sent.append(cp)
    for k in range(1, N_DEV):
        dev, pidx = _peer(k)
        _rcopy(src_of(pidx), dst_slot(pidx), send_sems.at[k - 1], recv_sems.at[k - 1], dev).wait_recv()
    for cp in sent:
        cp.wait_send()


def _rows_of_slots(buf, nslots):
    rows = lax.broadcasted_iota(jnp.int32, (8, buf.shape[-1]), 0)
    out = jnp.zeros((8, buf.shape[-1]), F32)
    for j in range(nslots):
        out = out + jnp.where(rows == j, buf[j], 0.0)
    return out


def _ada_fwd(c, w_ada, b_r, *, name):
    wloc = w_ada.shape[1]

    def body(c_ref, w_ref, b_ref, mod_ref, call_ref, csrc, cbuf, psrc, pbuf, s1, r1, s2, r2):
        x, y, cc = _coords()
        me = 4 * x + 2 * y + cc
        csrc[...] = jnp.broadcast_to(c_ref[...], (8, D))
        cbuf[me] = csrc[...]
        _exchange_all(lambda p: csrc, lambda s: cbuf.at[s], s1, r1)
        call = _rows_of_slots(cbuf, N_DEV)
        call_ref[...] = call
        prod = _dot_hi(_silu(call), w_ref[...])
        for b in range(N_DEV):
            psrc[b] = jnp.broadcast_to(prod[b:b + 1, :], (8, wloc))
        pbuf[me] = psrc[me]
        _exchange_all(lambda p: psrc.at[p], lambda s: pbuf.at[s], s2, r2)
        mod_ref[...] = _rows_of_slots(pbuf, N_DEV) + b_ref[...]

    vm = pl.BlockSpec(memory_space=pltpu.VMEM)
    return pl.pallas_call(
        body, name=name, in_specs=[vm, vm, vm], out_specs=[vm, vm],
        out_shape=[jax.ShapeDtypeStruct((N_DEV, wloc), F32), jax.ShapeDtypeStruct((N_DEV, D), F32)],
        scratch_shapes=[pltpu.VMEM((8, D), F32), pltpu.VMEM((N_DEV, 8, D), F32),
                        pltpu.VMEM((N_DEV, 8, wloc), F32), pltpu.VMEM((N_DEV, 8, wloc), F32),
                        pltpu.SemaphoreType.DMA((N_DEV - 1,)), pltpu.SemaphoreType.DMA((N_DEV - 1,)),
                        pltpu.SemaphoreType.DMA((N_DEV - 1,)), pltpu.SemaphoreType.DMA((N_DEV - 1,))],
        compiler_params=pltpu.CompilerParams(vmem_limit_bytes=VMEM_BIG))(c, w_ada, b_r)


def _gather_slabs(slab, *, name):
    def body(x_ref, out_ref, send_sems, recv_sems, local_sem):
        x, y, c = _coords()
        me, sibling = (x, y, c), (x, y, 1 - c)
        chips = [(1 - x, y), (x, 1 - y), (1 - x, 1 - y)]

        def slot(px, py, pc):
            return out_ref.at[4 * px + 2 * py + pc]

        def copy(k, block, to, src=None):
            return _rcopy(slot(*block) if src is None else src, slot(*block), send_sems.at[k], recv_sems.at[k], to)

        mine = pltpu.make_async_copy(x_ref, slot(*me), local_sem)
        mine.start()
        first = [copy(0, me, sibling, src=x_ref)]
        first += [copy(1 + j, me, (*chip, c), src=x_ref) for j, chip in enumerate(chips)]
        for cp in first:
            cp.start()
        passed = [copy(4 + j, (*chip, c), sibling) for j, chip in enumerate(chips)]
        for j, chip in enumerate(chips):
            copy(1 + j, (*chip, c), me).wait_recv()
            passed[j].start()
        copy(0, sibling, me).wait_recv()
        for j, chip in enumerate(chips):
            copy(4 + j, (*chip, 1 - c), me).wait_recv()
        for cp in first + passed:
            cp.wait_send()
        mine.wait()

    anyspec = pl.BlockSpec(memory_space=pl.ANY)
    return pl.pallas_call(
        body, name=name, in_specs=[anyspec], out_specs=anyspec,
        out_shape=jax.ShapeDtypeStruct((N_DEV,) + slab.shape, slab.dtype),
        scratch_shapes=[pltpu.SemaphoreType.DMA((7,)), pltpu.SemaphoreType.DMA((7,)), pltpu.SemaphoreType.DMA],
    )(slab)


_HBM =pl.BlockSpec(memory_space=pltpu.HBM)
_SEM = pl.BlockSpec(memory_space=pltpu.SEMAPHORE)
_EFFECT = pltpu.SideEffectType.DATAFLOW_SIDE_EFFECTING


def _xchg_src(src_ref, pidx, per_peer):
    return src_ref.at[pidx] if per_peer else src_ref


def _xchg_start(src, *, per_peer, name):
    rows = src.shape[-2]
    land_shape = (N_DEV, rows, D)

    def body(src_ref, land_ref, send_sems, recv_sems, src_thru, land_thru, token):
        del src_thru, land_thru
        x, y, c = _coords()
        me = 4 * x + 2 * y + c
        for k in range(1, N_DEV):
            dev, pidx = _peer(k)
            _rcopy(_xchg_src(src_ref, pidx, per_peer), land_ref.at[me], send_sems.at[k - 1],
                   recv_sems.at[k - 1], dev).start()
        token[...] = jnp.zeros_like(token)

    return pl.pallas_call(
        body, name=name,
        out_shape=(pltpu.SemaphoreType.DMA((N_DEV - 1,)), pltpu.SemaphoreType.DMA((N_DEV - 1,)),
                   pltpu.HBM(src.shape, src.dtype), pltpu.HBM(land_shape, src.dtype),
                   jax.ShapeDtypeStruct((8, 128), F32)),
        in_specs=(_HBM, _HBM),
        out_specs=(_SEM, _SEM, _HBM, _HBM, pl.BlockSpec(memory_space=pltpu.VMEM)),
        input_output_aliases={0: 2, 1: 3},
        compiler_params=pltpu.CompilerParams(has_side_effects=_EFFECT),
    )(pltpu.with_memory_space_constraint(src, pltpu.HBM),
      pltpu.with_memory_space_constraint(lax.empty(land_shape, src.dtype), pltpu.HBM))


def _xchg_wait(started, after, *, per_peer, name):
    send_sems, recv_sems, src_thru, land_thru, _ = started

    def body(src_ref, land_ref, send_sems, recv_sems, after_ref, src_dead, got_ref):
        del after_ref, src_dead, got_ref
        for k in range(1, N_DEV):
            dev, pidx = _peer(k)
            cp = _rcopy(_xchg_src(src_ref, pidx, per_peer), land_ref.at[pidx], send_sems.at[k - 1],
                        recv_sems.at[k - 1], dev)
            cp.wait_send()
            cp.wait_recv()

    return pl.pallas_call(
        body, name=name,
        out_shape=(pltpu.HBM(src_thru.shape, src_thru.dtype), pltpu.HBM(land_thru.shape, land_thru.dtype)),
        in_specs=(_HBM, _HBM, _SEM, _SEM, pl.BlockSpec(memory_space=pl.ANY)),
        out_specs=(_HBM, _HBM),
        input_output_aliases={0: 0, 1: 1},
        compiler_params=pltpu.CompilerParams(has_side_effects=_EFFECT),
    )(src_thru, land_thru, send_sems, recv_sems, after)


def _dep(token):
    return (token, (8, 128), lambda i, j, k: (0, 0))


def _small_allsum(sv, *, name):
    def body(sv_ref, all_ref, sum_ref, send_sems, recv_sems):
        x, y, c = _coords()
        me = 4 * x + 2 * y + c
        all_ref[me] = sv_ref[...]
        _exchange_all(lambda p: sv_ref, lambda s: all_ref.at[s], send_sems, recv_sems)
        acc = all_ref[0]
        for j in range(1, N_DEV):
            acc = acc + all_ref[j]
        sum_ref[...] = acc

    vm = pl.BlockSpec(memory_space=pltpu.VMEM)
    return pl.pallas_call(
        body, name=name, in_specs=[vm], out_specs=[vm, vm],
        out_shape=[jax.ShapeDtypeStruct((N_DEV, SV_ROWS, 128), F32), jax.ShapeDtypeStruct((SV_ROWS, 128), F32)],
        scratch_shapes=[pltpu.SemaphoreType.DMA((7,)), pltpu.SemaphoreType.DMA((7,))],
    )(sv)


def _ada_bwd(call, dmod_loc, *, name):
    wloc = dmod_loc.shape[1]

    def body(c_ref, d_ref, o_ref):
        o_ref[...] = _dot_hi(_silu(c_ref[...]), d_ref[...], TN)

    vm = pl.BlockSpec(memory_space=pltpu.VMEM)
    return pl.pallas_call(body, name=name, in_specs=[vm, vm], out_specs=vm,
                          out_shape=jax.ShapeDtypeStruct((D, wloc), F32),
                          compiler_params=pltpu.CompilerParams(vmem_limit_bytes=VMEM_BIG))(call, dmod_loc)


def _pad_rows(a, rows):
    return jnp.pad(a, ((0, rows - a.shape[0]), (0, 0)))


IN_SHIFT = tuple((IN_ROWS * j) % 16 for j in range(N_DEV))
IN_BASE = tuple(IN_ROWS * j - IN_SHIFT[j] for j in range(N_DEV))
IN_SEGMENTS = ((2048, XBC, C_XBC), (5152, 1024, C_POOL), (0, 2048, C_Z), (6176, 2048, C_GATE), (5120, 32, C_DT))


def _global_pieces(gs):
    pieces = []
    for j in range(N_DEV):
        lo, hi = 0, IN_ROWS_P
        if j > 0 and IN_BASE[j - 1] + IN_ROWS_P > IN_BASE[j]:
            pieces.append((IN_BASE[j], 16, gs[j - 1, IN_ROWS_P - 16:IN_ROWS_P] + gs[j, 0:16]))
            lo = 16
        if j + 1 < N_DEV and IN_BASE[j] + IN_ROWS_P > IN_BASE[j + 1]:
            hi = IN_ROWS_P - 16
        pieces.append((IN_BASE[j] + lo, hi - lo, gs[j, lo:hi]))
    return pieces


def _reorder_in_rows(gs):
    pieces = _global_pieces(gs)
    parts = []
    for lo, n, _ in IN_SEGMENTS:
        for p0, pn, arr in pieces:
            a, b = max(lo, p0), min(lo + n, p0 + pn)
            if a < b:
                parts.append(arr[a - p0:b - p0])
    parts.append(jnp.zeros((DT_PAD - 32, D), gs.dtype))
    return jnp.concatenate(parts, axis=0)


def _restore_in_shards(d):
    slabs = []
    for j in range(N_DEV):
        parts = []
        r, end = IN_BASE[j], IN_BASE[j] + IN_ROWS_P
        while r < end:
            lo, n, new = next(s for s in IN_SEGMENTS if s[0] <= r < s[0] + s[1])
            e = min(end, lo + n)
            parts.append(d[new + r - lo:new + e - lo])
            r = e
        slabs.append(jnp.concatenate(parts, axis=0))
    return jnp.stack(slabs, axis=0)


def _pack_sv(parts):
    flat = []
    for n, size in SV_PARTS:
        v = parts[n].reshape(-1).astype(F32)
        flat.append(jnp.pad(v, (0, size - v.shape[0])))
    v = jnp.concatenate(flat)
    return jnp.pad(v, (0, SV_ROWS * 128 - v.shape[0])).reshape(SV_ROWS, 128)


def _sv_get(flat, n, size):
    return flat[SV_OFF[n]:SV_OFF[n] + size]


def kernel(x, c, w_ada, b_ada, norm_mix_w, w_in, conv_w, conv_b, dt_bias, a_log, d_skip, ssd_norm_w, w_branch_ssd, pool_w, pool_scale, w_branch_pool, w_out, norm_mlp_w, w_up, w_down, norm_final_w, loss_target, m_w_ada, m_b_ada, m_norm_mix_w, m_w_in, m_conv_w, m_conv_b, m_dt_bias, m_a_log, m_d_skip, m_ssd_norm_w, m_w_branch_ssd, m_pool_w, m_pool_scale, m_w_branch_pool, m_w_out, m_norm_mlp_w, m_w_up, m_w_down, m_norm_final_w, v_w_ada, v_b_ada, v_norm_mix_w, v_w_in, v_conv_w, v_conv_b, v_dt_bias, v_a_log, v_d_skip, v_ssd_norm_w, v_w_branch_ssd, v_pool_w, v_pool_scale, v_w_branch_pool, v_w_out, v_norm_mlp_w, v_w_up, v_w_down, v_norm_final_w):
    xs_ = x[0]
    tgt = loss_target[0]
    L = xs_.shape[0]
    me = 4 * lax.axis_index("x") + 2 * lax.axis_index("y") + lax.axis_index("c")
    wloc = w_ada.shape[2]

    mod_p, c_all = _ada_fwd(c, w_ada[0], b_ada.reshape(N_DEV, wloc), name="ada_fwd")
    mod = mod_p.reshape(6, D)
    shift_m, scale_m, gate_m, shift_f, scale_f, gate_f = [mod[i:i + 1] for i in range(6)]

    conv_bits = lax.bitcast_convert_type(conv_w[0], SLAB_DT).reshape(3, D)
    in_shift = (IN_ROWS * me) % 16
    slab_in = lax.dynamic_update_slice(jnp.zeros((IN_ROWS_P, D), SLAB_DT), w_in[0].T.astype(SLAB_DT),
                                       (in_shift, 0))
    slab_in = jnp.concatenate([slab_in, _pad_rows(conv_bits, CONV_ROWS)], axis=0)
    slab_rest = jnp.concatenate([
        w_branch_ssd[0].astype(SLAB_DT),
        pool_w[0].reshape(32, D).astype(SLAB_DT),
        w_branch_pool[0].astype(SLAB_DT),
        w_out[0].astype(SLAB_DT),
        w_up[0].T.astype(SLAB_DT),
        w_down[0].astype(SLAB_DT)], axis=0)
    slab_in, mod_p = lax.optimization_barrier((slab_in, mod_p))
    gs_in = _gather_slabs(slab_in, name="gather_w_in")
    slab_rest, gs_in = lax.optimization_barrier((slab_rest, gs_in))
    rest_started = _xchg_start(slab_rest, per_peer=False, name="gather_rest_start")
    gather_token = rest_started[4]

    w_in_t = _reorder_in_rows(gs_in)
    conv_full = lax.bitcast_convert_type(
        gs_in[:, IN_ROWS_P:IN_ROWS_P + 3].reshape(N_DEV, 4, XBC // N_DEV, 2), F32)
    conv_full = conv_full.transpose(1, 0, 2).reshape(4, XBC)

    dtb = jnp.pad(dt_bias, ((0, 0), (0, 128 - NH)))
    arow = jnp.pad(-jnp.exp(a_log), ((0, 0), (0, 128 - NH)))
    dsk_x = jnp.repeat(d_skip, HP, axis=1)

    tm = _pick(L, (1024, 512, 256, 128))
    tm2 = _pick(L, (2048, 1024, 512, 256, 128))
    tkl = _pick(L, (4096, 2048, 1024, 512, 256, 128))
    tkl2 = _pick(L, (2048, 1024, 512, 256, 128))

    tmh = _pick(L, (512, 256, 128))
    zcol = C_Z // DI
    gcol = C_GATE // (2 * D)

    def whole_rows(w):
        return lambda t: ((L, w), BF16, (t, w), lambda i, j, k: (i, 0))

    def norm1_pro(x_ref, ex, outs, j):
        @pl.when(j == 0)
        def _():
            xv = x_ref[...]
            r = lax.rsqrt(jnp.mean(xv * xv, axis=-1, keepdims=True) + EPS)
            outs[1][...] = (xv * r * ex[0][...] * (1.0 + ex[1][...]) + ex[2][...]).astype(outs[1].dtype)

        return outs[1][...]

    def proj_ep(acc, ex, outs):
        outs[0][...] = acc

        @pl.when(pl.program_id(1) == NPROJ // 768 - 1)
        def _():
            outs[2][...] = acc[:, 768 - DT_PAD:]

    proj, h1, dtp = _mm(
        xs_, w_in_t, "nt", name="in_proj", tm=tm2, tn=768, tk=D,
        extras=[(norm_mix_w, *_vecs()), (scale_m, *_vecs()), (shift_m, *_vecs()), _dep(gather_token)],
        outs=[F32, whole_rows(D)(tm2), ((L, DT_PAD), F32, (tm2, DT_PAD), lambda i, j, k: (i, 0))],
        prologue=norm1_pro, epilogue=proj_ep)
    xbc_raw = proj
    xbc = _conv_fwd(xbc_raw, conv_full, conv_b, name="conv_fwd")
    y_ssm, hs = _ssd_fwd(xbc, dtp, dtb, arow, dsk_x, name="ssd_fwd")

    slab_rest, gs = _xchg_wait(rest_started, y_ssm, per_peer=False, name="gather_rest_wait")
    gs = lax.dynamic_update_slice(gs, slab_rest[None], (me, 0, 0))

    def part(n, rows):
        return gs[:, REST_OFF[n]:REST_OFF[n] + rows]

    w_bssd = part("bssd", 256).reshape(DI, D)
    w_pool = part("pool", 32).reshape(N_DEV, 4, 32, PGW).transpose(1, 0, 2, 3).reshape(POOL_W, PGW)
    w_bpool = part("bpool", 128).reshape(POOL_W, D)
    w_o = part("out", 128).reshape(D, D)
    w_up_t = part("up", 512).reshape(DFF, D)
    w_dn = part("down", 512).reshape(DFF, D)

    def gnorm_pro(y_ref, ex, outs, j):
        z_ref, w_ref = ex
        yg = y_ref[...].astype(F32) * _silu(z_ref[...].astype(F32))
        segs = []
        for k in range(NG):
            sl = slice(k * GW, (k + 1) * GW)
            seg = yg[:, sl]
            r = lax.rsqrt(jnp.mean(seg * seg, axis=-1, keepdims=True) + EPS)
            segs.append((seg * r * w_ref[:, sl]).astype(BF16))
        yn_v = jnp.concatenate(segs, axis=1)
        outs[1][...] = yn_v
        return yn_v

    y_ssd, yn = _mm(y_ssm, w_bssd, "nn", name="branch_ssd", tm=tmh, tn=D, tk=DI,
                    extras=[(proj, *_rows(tmh, DI, zcol)), (ssd_norm_w, *_vecs(DI))],
                    outs=[F32, whole_rows(DI)(tmh)], prologue=gnorm_pro)
    pooled = _pool_fwd(proj, name="pool_fwd")
    wp_spec = ((POOL_W, PGW), lambda i, j, k: (0, 0))

    def pool_pro(a_ref, ex, outs, j):
        wp_ref, s_ref = ex
        segs = []
        for g in range(4):
            sl = slice(g * PGW, (g + 1) * PGW)
            p = _dot(a_ref[:, sl], wp_ref[sl, :], NN)
            outs[1][:, sl] = p.astype(BF16)
            segs.append((p * s_ref[:, sl]).astype(BF16))
        yp1_v = jnp.concatenate(segs, axis=1)
        outs[2][...] = yp1_v
        return yp1_v

    y_pool, yp0, yp1 = _mm(pooled, w_bpool, "nn", name="branch_pool", tm=tm, tn=D, tk=D,
                           extras=[(w_pool, *wp_spec), (pool_scale, *_vecs())],
                           outs=[F32, whole_rows(D)(tm), whole_rows(D)(tm)], prologue=pool_pro)

    def merge_pro(a_ref, ex, outs, j):
        s = _sigmoid(ex[1][...].astype(F32))
        mv = (s[:, :D] * a_ref[...] + s[:, D:] * ex[0][...]).astype(BF16)
        outs[3][...] = mv
        return mv

    mix, x1, h2, m = _mm(y_ssd, w_o, "nn", name="out_proj", tm=tmh, tn=D, tk=D,
                         extras=[(y_pool, *_rows(tmh)), (proj, *_rows(tmh, 2 * D, gcol)),
                                 (xs_, *_rows(tmh)), (gate_m, *_vecs()), (norm_mlp_w, *_vecs()),
                                 (scale_f, *_vecs()), (shift_f, *_vecs())],
                         outs=[BF16, F32, BF16, whole_rows(D)(tmh)], prologue=merge_pro,
                         epilogue=lambda acc, ex, outs: _ep_resid_norm(acc, ex[2:], outs[:3]))

    def relu2(acc, ex, outs):
        r = jnp.maximum(acc, 0.0)
        outs[0][...] = acc.astype(BF16)
        outs[1][...] = (r * r).astype(BF16)

    up, act = _mm(h2, w_up_t, "nt", name="mlp_up", outs=[BF16, BF16], tm=tm2, tn=1024, tk=D, epilogue=relu2)

    dx2, ddown, loss_p, dnwf, dgate_f = _mm(
        act, w_dn, "nn", name="mlp_down", tm=tmh, tn=D, tk=DFF,
        extras=[(x1, *_rows(tmh)), (tgt, *_rows(tmh)), (gate_f, *_vecs()), (norm_final_w.reshape(1, D), *_vecs())],
        outs=[F32, BF16, _sum_out(128), _sum_out(), _sum_out()], epilogue=_ep_final)

    def drelu2(acc, ex, outs):
        outs[0][...] = (acc * (2.0 * jnp.maximum(ex[0][...].astype(F32), 0.0))).astype(BF16)

    def dep_last(ep):
        return lambda acc, ex, outs: ep(acc, ex[:-1], outs)

    dup = _mm(ddown, w_dn, "nt", name="mlp_down_dx", outs=[BF16], tm=tm2, tn=1024, tk=D,
              extras=[(up, (tm2, 1024), lambda i, j, k: (i, j))], epilogue=drelu2)
    g_dn = _mm(act, ddown, "tn", name="mlp_down_dw", outs=[SLAB_DT], tm=1024, tn=D, tk=tkl)
    g_up_t = _mm(dup, h2, "tn", name="mlp_up_dw", outs=[SLAB_DT], tm=1024, tn=D, tk=tkl)
    gslab_mlp = jnp.concatenate([g_up_t.reshape(N_DEV, 512, D), g_dn.reshape(N_DEV, 512, D)], axis=1)
    mlp_started = _xchg_start(gslab_mlp, per_peer=True, name="scatter_mlp_start")
    dx1, p2, q2, dmix, dgate_m = _mm(
        dup, w_up_t, "nn", name="mlp_up_dx", tm=tmh, tn=D, tk=DFF,
        extras=[(x1, *_rows(tmh)), (dx2, *_rows(tmh)), (norm_mlp_w, *_vecs()), (scale_f, *_vecs()),
                (mix, *_rows(tmh)), (gate_m, *_vecs()), _dep(mlp_started[4])],
        outs=[F32, _sum_out(), _sum_out(), BF16, _sum_out()], epilogue=dep_last(_ep_norm_bwd))
    gcol = C_GATE // (2 * D)
    dy_ssd, dy_pool, dproj = _mm(
        dmix, w_o, "nt", name="out_proj_dx", tm=tmh, tn=D, tk=D,
        extras=[(y_ssd, *_rows(tmh)), (y_pool, *_rows(tmh)), (proj, *_rows(tmh, 2 * D, gcol))],
        outs=[BF16, BF16, ((L, NPROJ), BF16, *_rows(tmh, 2 * D, gcol))], epilogue=_ep_merge_bwd)
    g_o = _mm(m, dmix, "tn", name="out_proj_dw", outs=[SLAB_DT], tm=D, tn=D, tk=tkl)
    zcol = C_Z // DI
    dy_ssm, dproj, d_snw = _mm(
        dy_ssd, w_bssd, "nt", name="branch_ssd_dx", tm=tmh, tn=DI, tk=D,
        extras=[(y_ssm, *_rows(tmh, DI)), (proj, *_rows(tmh, DI, zcol)), (ssd_norm_w, *_vecs(DI)),
                (dproj, None, None)],
        outs=[F32, ((L, NPROJ), BF16, *_rows(tmh, DI, zcol)), _sum_out(DI)],
        epilogue=_ep_gated_norm_bwd, aliases={3: 1})
    g_bssd = _mm(yn, dy_ssd, "tn", name="branch_ssd_dw", outs=[SLAB_DT], tm=1024, tn=D, tk=tkl)
    dxbc, dproj, d_a, d_dx, d_dtb = _ssd_bwd(dy_ssm, xbc, dtp, hs, dtb, arow, dsk_x, dproj, name="ssd_bwd")
    dproj, d_cw, d_cb = _conv_bwd(xbc_raw, dxbc, conv_full, conv_b, dproj, name="conv_bwd")
    def pool_bwd_ep(acc, ex, outs):
        y_ref, s_ref, wp_ref = ex
        o_ref, ds_ref, dpool_ref = outs
        dyp0_v = (acc * s_ref[...]).astype(BF16)
        o_ref[...] = dyp0_v
        _acc_out(ds_ref, _colsum(acc * y_ref[...].astype(F32)), _row_step())
        for g in range(4):
            sl = slice(g * PGW, (g + 1) * PGW)
            dpool_ref[:, sl] = _dot(dyp0_v[:, sl], wp_ref[sl, :], NT)

    dyp0, d_ps, dpooled = _mm(dy_pool, w_bpool, "nt", name="branch_pool_dx", tm=tm, tn=D, tk=D,
                              extras=[(yp0, *_rows(tm)), (pool_scale, *_vecs()), (w_pool, *wp_spec)],
                              outs=[BF16, _sum_out(), F32], epilogue=pool_bwd_ep)
    g_bpool = _mm(yp1, dy_pool, "tn", name="branch_pool_dw", outs=[SLAB_DT], tm=D, tn=D, tk=tkl)
    g_pool = _mm_pool_tn(pooled, dyp0, name="pool_mix_dw", tk=tkl)
    gslab_mix = jnp.concatenate([
        g_bssd.reshape(N_DEV, 256, D),
        g_pool.reshape(4, N_DEV, 32, PGW).transpose(1, 0, 2, 3).reshape(N_DEV, 32, D).astype(SLAB_DT),
        g_bpool.reshape(N_DEV, 128, D),
        g_o.reshape(N_DEV, 128, D)], axis=1)
    mix_started = _xchg_start(gslab_mix, per_peer=True, name="scatter_mix_start")
    dproj = _pool_bwd(dpooled, dproj, name="pool_bwd")
    g_in_t = _mm(dproj, h1, "tn", name="in_proj_dw", outs=[SLAB_DT], tm=1408, tn=D, tk=tkl2,
                 extras=[_dep(mix_started[4])])
    gslab_in = _restore_in_shards(g_in_t)
    in_started = _xchg_start(gslab_in, per_peer=True, name="scatter_in_start")
    grad_x, p1, q1 = _mm(
        dproj, w_in_t, "nn", name="in_proj_dx", tm=tmh, tn=D, tk=2816,
        extras=[(xs_, *_rows(tmh)), (dx1, *_rows(tmh)), (norm_mix_w, *_vecs()), (scale_m, *_vecs()),
                _dep(in_started[4])],
        outs=[F32, _sum_out(), _sum_out()], epilogue=dep_last(_ep_norm_bwd))

    def landed(started, after, tile, name):
        src, land = _xchg_wait(started, after, per_peer=True, name=name + "_wait")
        own = lax.dynamic_slice_in_dim(src, me, 1, axis=0)
        return _slab_sum(lax.dynamic_update_slice(land, own, (me, 0, 0)), tile=tile, name=name + "_sum")

    gsum_mlp = landed(mlp_started, grad_x, 256, "scatter_mlp")
    gsum_mix = landed(mix_started, grad_x, 272, "scatter_mix")
    gsum_in = landed(in_started, grad_x, 208, "scatter_in")

    dmod = jnp.concatenate([q1, p1 * norm_mix_w, dgate_m, q2, p2 * norm_mlp_w, dgate_f], axis=1)
    d_alog = d_a[:, :NH] * (-jnp.exp(a_log))
    sv = _pack_sv({
        "b_ada": dmod, "norm_mix_w": p1 * (1.0 + scale_m), "conv_b": d_cb, "dt_bias": d_dtb[:, :NH],
        "a_log": d_alog, "d_skip": d_dx.reshape(NH, HP).sum(axis=1), "ssd_norm_w": d_snw,
        "pool_scale": d_ps, "norm_mlp_w": p2 * (1.0 + scale_f), "norm_final_w": dnwf, "conv_w": d_cw,
        "loss": loss_p[:, :1]})
    sv_all, sv_sum = _small_allsum(sv, name="small_allsum")
    flat = sv_sum.reshape(-1)
    loss = flat[SV_OFF["loss"]]
    dmod_all = sv_all.reshape(N_DEV, SV_ROWS * 128)[:, :6 * D]
    g_w_ada = _ada_bwd(c_all, lax.dynamic_slice_in_dim(dmod_all, me * wloc, wloc, axis=1), name="ada_bwd")

    g_conv_w = lax.dynamic_slice_in_dim(_sv_get(flat, "conv_w", 4 * XBC).reshape(4, XBC),
                                        me * (XBC // N_DEV), XBC // N_DEV, axis=1)
    small = [("b_ada", b_ada, m_b_ada, v_b_ada), ("norm_mix_w", norm_mix_w, m_norm_mix_w, v_norm_mix_w),
             ("conv_b", conv_b, m_conv_b, v_conv_b), ("dt_bias", dt_bias, m_dt_bias, v_dt_bias),
             ("a_log", a_log, m_a_log, v_a_log), ("d_skip", d_skip, m_d_skip, v_d_skip),
             ("ssd_norm_w", ssd_norm_w, m_ssd_norm_w, v_ssd_norm_w),
             ("pool_scale", pool_scale, m_pool_scale, v_pool_scale),
             ("norm_mlp_w", norm_mlp_w, m_norm_mlp_w, v_norm_mlp_w),
             ("norm_final_w", norm_final_w[None], m_norm_final_w[None], v_norm_final_w[None]),
             ("conv_w", conv_w[0], m_conv_w[0], v_conv_w[0])]
    small_out = _adamw_small(sv_sum.reshape(1, SV_ROWS * 128), g_conv_w, small, name="adamw_small")
    small_out["norm_final_w"] = tuple(a[0] for a in small_out["norm_final_w"])
    small_out["conv_w"] = tuple(a[None] for a in small_out["conv_w"])

    def gpart(n, rows_):
        return gsum_mix[MIX_OFF[n]:MIX_OFF[n] + rows_]

    def lin(a):
        return a[0].T.reshape(IN_ROWS * 8, 128)

    g_lin = lax.dynamic_slice_in_dim(gsum_in, in_shift, IN_ROWS, axis=0).reshape(IN_ROWS * 8, 128)
    dlt, mn, vn = _adamw(lin(w_in), g_lin, lin(m_w_in), lin(v_w_in), name="adamw_w_in", tr=IN_ROWS * 2)
    big_in = tuple(a.reshape(IN_ROWS, D).T[None] for a in (g_lin, dlt, mn, vn))

    big = {
        "w_ada": (w_ada, m_w_ada, v_w_ada, g_w_ada, (D, wloc)),
        "w_branch_ssd": (w_branch_ssd, m_w_branch_ssd, v_w_branch_ssd, gpart("bssd", 256), (256, D)),
        "pool_w": (pool_w, m_pool_w, v_pool_w, gpart("pool", 32).reshape(128, PGW), (128, PGW)),
        "w_branch_pool": (w_branch_pool, m_w_branch_pool, v_w_branch_pool, gpart("bpool", 128), (128, D)),
        "w_out": (w_out, m_w_out, v_w_out, gpart("out", 128), (128, D)),
        "w_up": (w_up, m_w_up, v_w_up, gsum_mlp[:512].T, (D, 512)),
        "w_down": (w_down, m_w_down, v_w_down, gsum_mlp[512:], (512, D)),
    }
    big_out = {}
    for n, (w, mm_, vv, g, shp2) in big.items():
        dlt, mn, vn = _adamw(w.reshape(shp2), g, mm_.reshape(shp2), vv.reshape(shp2), name="adamw_" + n)
        big_out[n] = (g.reshape(w.shape), dlt.reshape(w.shape), mn.reshape(w.shape), vn.reshape(w.shape))

    order = ["w_ada", "b_ada", "norm_mix_w", "w_in", "conv_w", "conv_b", "dt_bias", "a_log", "d_skip",
             "ssd_norm_w", "w_branch_ssd", "pool_w", "pool_scale", "w_branch_pool", "w_out", "norm_mlp_w",
             "w_up", "w_down", "norm_final_w"]
    big_out["w_in"] = big_in
    res = {**small_out, **big_out}
    outs = [loss, grad_x.reshape(x.shape)]
    for k in range(4):
        outs += [res[n][k] for n in order]
    return tuple(outs)
```

```python
import functools

import numpy as np
import jax
import jax.numpy as jnp
from jax import lax
from jax.experimental import pallas as pl
from jax.experimental.pallas import tpu as pltpu

F32 = jnp.float32
BF16 = jnp.bfloat16
SLAB_DT = jnp.bfloat16
_MXU_DTYPE = jnp.bfloat16

N_DEV = 8
D = 1024
DI = 2048
NH = 32
HP = 64
NG = 4
NS = 128
Q = 128
XBC = DI + 2 * NG * NS
DFF = 4096
N_IN = 8224
EPS = 1e-5
POOL_W = 1024
PGW = 256

C_XBC, C_POOL, C_Z, C_GATE, C_DT = 0, 3072, 4096, 6144, 8192
DT_PAD = 256
NPROJ = C_DT + DT_PAD

IN_ROWS = N_IN // N_DEV
IN_ROWS_P = 1040
CONV_ROWS = 16
REST_PARTS = (("bssd", 256), ("pool", 32), ("bpool", 128), ("out", 128), ("up", 512), ("down", 512))
REST_OFF = {}
_o = 0
for _n, _r in REST_PARTS:
    REST_OFF[_n] = _o
    _o += _r
REST_ROWS = _o
MIX_PARTS = (("bssd", 256), ("pool", 32), ("bpool", 128), ("out", 128))
MIX_OFF = {}
_o = 0
for _n, _r in MIX_PARTS:
    MIX_OFF[_n] = _o
    _o += _r
MIX_ROWS = _o

SV_PARTS = (("b_ada", 6144), ("norm_mix_w", 1024), ("conv_b", 3072), ("dt_bias", 128), ("a_log", 128),
            ("d_skip", 128), ("ssd_norm_w", 2048), ("pool_scale", 1024), ("norm_mlp_w", 1024),
            ("norm_final_w", 1024), ("conv_w", 4 * XBC), ("loss", 128))
SV_OFF = {}
_o = 0
for _n, _r in SV_PARTS:
    SV_OFF[_n] = _o
    _o += _r
SV_ROWS = 224
assert _o <= SV_ROWS * 128

ADAM_LR, ADAM_B1, ADAM_B2, ADAM_EPS, ADAM_WD, ADAM_STEP = 0.001, 0.9, 0.999, 1e-08, 0.01, 10

VMEM_BIG = 56 * 1024 * 1024
NEG = -1e30

NN = ((1,), (0,))
NT = ((1,), (1,))
TN = ((0,), (0,))


def _dot(a, b, dims=NN):
    return lax.dot_general(a.astype(_MXU_DTYPE), b.astype(_MXU_DTYPE), (dims, ((), ())),
                           preferred_element_type=F32)


def _dot_hi(a, b, dims=NN):
    return lax.dot_general(a.astype(F32), b.astype(F32), (dims, ((), ())),
                           precision=lax.Precision.HIGHEST, preferred_element_type=F32)


def _pick(n, cands):
    for c in cands:
        if n % c == 0:
            return c
    return n


def _sigmoid(x):
    return 1.0 / (1.0 + jnp.exp(-x))


def _silu(x):
    return x * _sigmoid(x)


def _dsilu(x):
    s = _sigmoid(x)
    return s * (1.0 + x * (1.0 - s))


def _softplus(x):
    return jnp.maximum(x, 0.0) + jnp.log(1.0 + jnp.exp(-jnp.abs(x)))


def _params(sem, vmem=None):
    return pltpu.CompilerParams(dimension_semantics=sem, vmem_limit_bytes=vmem)


def _row_step():
    return pl.program_id(0)


def _mm(a, b, mode, *, name, outs, tm, tn, tk, extras=(), epilogue=None, aliases=None, prologue=None):
    if mode == "tn":
        K, M = a.shape
        N = b.shape[1]
        a_spec = pl.BlockSpec((tk, tm), lambda i, j, k: (k, i))
        b_spec = pl.BlockSpec((tk, tn), lambda i, j, k: (k, j))
        dims = TN
    else:
        M = a.shape[0]
        K = b.shape[0] if mode == "nn" else b.shape[1]
        if prologue is None:
            assert a.shape[1] == K
            a_spec = pl.BlockSpec((tm, tk), lambda i, j, k: (i, k))
        else:
            assert tk == K
            a_spec = pl.BlockSpec((tm, a.shape[1]), lambda i, j, k: (i, 0))
        if mode == "nn":
            N = b.shape[1]
            b_spec = pl.BlockSpec((tk, tn), lambda i, j, k: (k, j))
            dims = NN
        else:
            N = b.shape[0]
            b_spec = pl.BlockSpec((tn, tk), lambda i, j, k: (j, k))
            dims = NT
    assert M % tm == 0 and N % tn == 0 and K % tk == 0, (name, M, N, K, tm, tn, tk)
    nk = K // tk
    ne, no = len(extras), len(outs)
    if epilogue is None:
        def epilogue(acc, ex, out_refs):
            out_refs[0][...] = acc.astype(out_refs[0].dtype)

    def body(a_ref, b_ref, *rest):
        ex, out_refs = rest[:ne], rest[ne:ne + no]
        lhs = a_ref[...] if prologue is None else prologue(a_ref, ex, out_refs, pl.program_id(1))
        p = _dot(lhs, b_ref[...], dims)
        if nk == 1:
            epilogue(p, ex, out_refs)
        else:
            acc = rest[-1]
            k = pl.program_id(2)

            @pl.when(k == 0)
            def _():
                acc[...] = p

            @pl.when(jnp.logical_and(k > 0, k < nk - 1))
            def _():
                acc[...] += p

            @pl.when(k == nk - 1)
            def _():
                epilogue(acc[...] + p, ex, out_refs)

    out_specs, out_shape = [], []
    for o in outs:
        if isinstance(o, tuple):
            shape, dt, bs, im = o
            out_specs.append(pl.BlockSpec(bs, im))
            out_shape.append(jax.ShapeDtypeStruct(shape, dt))
        else:
            out_specs.append(pl.BlockSpec((tm, tn), lambda i, j, k: (i, j)))
            out_shape.append(jax.ShapeDtypeStruct((M, N), o))
    in_specs = [a_spec, b_spec]
    for _, bs, im in extras:
        in_specs.append(pl.BlockSpec(memory_space=pl.ANY) if bs is None else pl.BlockSpec(bs, im))
    res = pl.pallas_call(
        body, name=name,
        grid=(M // tm, N // tn, nk),
        in_specs=in_specs, out_specs=out_specs, out_shape=out_shape,
        scratch_shapes=[pltpu.VMEM((tm, tn), F32)] if nk > 1 else [],
        input_output_aliases={2 + e: o for e, o in (aliases or {}).items()},
        compiler_params=_params(("arbitrary", "arbitrary", "arbitrary"), VMEM_BIG),
    )(a, b, *[e[0] for e in extras])
    return res if no > 1 else res[0]


def _rows(tm, w=D, col=0):
    return (tm, w), lambda i, j, k, c=col: (i, c)


def _vecs(w=D, col=0):
    return (1, w), lambda i, j, k, c=col: (0, c)


def _sum_out(w=D):
    return ((1, w), F32, (1, w), lambda i, j, k: (0, 0))


def _mm_pool_tn(a, b, *, name, tk):
    L = a.shape[0]

    def body(a_ref, b_ref, o_ref):
        p = _dot(a_ref[...], b_ref[...], TN)

        @pl.when(pl.program_id(1) == 0)
        def _():
            o_ref[...] = p

        @pl.when(pl.program_id(1) > 0)
        def _():
            o_ref[...] += p

    blk = pl.BlockSpec((tk, PGW), lambda g, k: (k, g))
    return pl.pallas_call(body, name=name, grid=(4, L // tk), in_specs=[blk, blk],
                          out_specs=pl.BlockSpec((PGW, PGW), lambda g, k: (g, 0)),
                          out_shape=jax.ShapeDtypeStruct((POOL_W, PGW), F32),
                          compiler_params=_params(("parallel", "arbitrary")))(a, b)


def _acc_out(ref, val, i):
    @pl.when(i == 0)
    def _():
        ref[...] = val

    @pl.when(i > 0)
    def _():
        ref[...] += val


def _colsum(v):
    return jnp.sum(v, axis=0, keepdims=True)


def _ep_resid_norm(acc, ex, outs):
    x_ref, g_ref, nw_ref, sc_ref, sh_ref = ex
    mix_ref, x1_ref, h_ref = outs
    mix_ref[...] = acc.astype(mix_ref.dtype)
    xv = x_ref[...] + g_ref[...] * acc
    x1_ref[...] = xv
    r = lax.rsqrt(jnp.mean(xv * xv, axis=-1, keepdims=True) + EPS)
    h_ref[...] = (xv * r * nw_ref[...] * (1.0 + sc_ref[...]) + sh_ref[...]).astype(h_ref.dtype)


def _ep_final(acc, ex, outs):
    x1_ref, t_ref, g_ref, nw_ref = ex
    dx2_ref, dd_ref, loss_ref, dnw_ref, dg_ref = outs
    i = _row_step()
    x2 = x1_ref[...] + g_ref[...] * acc
    r = lax.rsqrt(jnp.mean(x2 * x2, axis=-1, keepdims=True) + EPS)
    xh = x2 * r
    e = xh * nw_ref[...] - t_ref[...]
    part = 0.5 * jnp.sum(jnp.mean(e * e, axis=-1, keepdims=True), axis=0, keepdims=True)
    dy = e * (1.0 / D)
    g = dy * nw_ref[...]
    dx2 = r * (g - xh * jnp.mean(g * xh, axis=-1, keepdims=True))
    dx2_ref[...] = dx2
    dd_ref[...] = (dx2 * g_ref[...]).astype(dd_ref.dtype)
    _acc_out(loss_ref, jnp.broadcast_to(part, (1, 128)), i)
    _acc_out(dnw_ref, _colsum(dy * xh), i)
    _acc_out(dg_ref, _colsum(dx2 * acc), i)


def _ep_norm_bwd(acc, ex, outs):
    x_ref, dr_ref, nw_ref, sc_ref = ex[:4]
    dx_ref, p_ref, q_ref = outs[:3]
    i = _row_step()
    xv = x_ref[...]
    r = lax.rsqrt(jnp.mean(xv * xv, axis=-1, keepdims=True) + EPS)
    xh = xv * r
    g = acc * (nw_ref[...] * (1.0 + sc_ref[...]))
    dx = dr_ref[...] + r * (g - xh * jnp.mean(g * xh, axis=-1, keepdims=True))
    dx_ref[...] = dx
    _acc_out(p_ref, _colsum(acc * xh), i)
    _acc_out(q_ref, _colsum(acc), i)
    if len(ex) > 4:
        m_ref, g_ref = ex[4:]
        dm_ref, dg_ref = outs[3:]
        dm_ref[...] = (dx * g_ref[...]).astype(dm_ref.dtype)
        _acc_out(dg_ref, _colsum(dx * m_ref[...].astype(F32)), i)


def _ep_merge_bwd(acc, ex, outs):
    a_ref, b_ref, gl_ref = ex
    da_ref, db_ref, dgl_ref = outs
    s = _sigmoid(gl_ref[...].astype(F32))
    s1, s2 = s[:, :D], s[:, D:]
    da_ref[...] = (acc * s1).astype(da_ref.dtype)
    db_ref[...] = (acc * s2).astype(db_ref.dtype)
    dgl_ref[:, :D] = (acc * a_ref[...] * s1 * (1.0 - s1)).astype(dgl_ref.dtype)
    dgl_ref[:, D:] = (acc * b_ref[...] * s2 * (1.0 - s2)).astype(dgl_ref.dtype)


GW = DI // NG


def _ep_gated_norm_bwd(acc, ex, outs):
    y_ref, z_ref, w_ref, _ = ex
    dy_ref, dz_ref, dw_ref = outs
    zv = z_ref[...].astype(F32)
    yv = y_ref[...].astype(F32)
    sg = _sigmoid(zv)
    sz = zv * sg
    yg = yv * sz
    dsz = sg * (1.0 + zv * (1.0 - sg))
    dws = []
    for k in range(NG):
        sl = slice(k * GW, (k + 1) * GW)
        seg = yg[:, sl]
        r = lax.rsqrt(jnp.mean(seg * seg, axis=-1, keepdims=True) + EPS)
        sh = seg * r
        dn = acc[:, sl]
        g = dn * w_ref[:, sl]
        dyg = r * (g - sh * jnp.mean(g * sh, axis=-1, keepdims=True))
        dy_ref[:, sl] = dyg * sz[:, sl]
        dz_ref[:, sl] = (dyg * yv[:, sl] * dsz[:, sl]).astype(dz_ref.dtype)
        dws.append(_colsum(dn * sh))
    _acc_out(dw_ref, jnp.concatenate(dws, axis=1), _row_step())


CONV_CB = 128
HALO = 16


def _time_chunk(L):
    return _pick(L, (256, 128))


def _with_halo(x_ref, i, r0, rc):
    p0 = pl.multiple_of(jnp.maximum(r0 - HALO, 0), HALO)
    prev = jnp.where(i > 0, x_ref[pl.ds(p0, HALO), :].astype(F32), 0.0)
    return jnp.concatenate([prev, x_ref[pl.ds(r0, rc), :].astype(F32)], axis=0)


def _conv_bwd(proj, dy, w, b, dproj, *, name):
    L = proj.shape[0]
    rc = _time_chunk(L)
    n = L // rc

    def body(x_ref, dy_ref, w_ref, b_ref, dp_in, dx_ref, dw_ref, db_ref):
        del dp_in
        wv = w_ref[...]
        bv = b_ref[...]

        def step(k, carry):
            nxt, db, d0, d1, d2, d3 = carry
            i = n - 1 - k
            r0 = pl.multiple_of(i * rc, rc)
            ext = _with_halo(x_ref, i, r0, rc)
            xk = [ext[HALO:]] + [pltpu.roll(ext, j, 0)[HALO:] for j in (1, 2, 3)]
            pre = bv
            for j in range(4):
                pre = pre + xk[j] * wv[3 - j:4 - j]
            dpre = dy_ref[pl.ds(r0, rc), :] * _dsilu(pre)
            dext = jnp.concatenate([dpre, nxt], axis=0)
            acc = dext * wv[3:4]
            for j in (1, 2, 3):
                acc = acc + pltpu.roll(dext, rc + HALO - j, 0) * wv[3 - j:4 - j]
            dx_ref[pl.ds(r0, rc), :] = acc[:rc].astype(dx_ref.dtype)
            return (dpre[:HALO], db + _colsum(dpre), d0 + _colsum(dpre * xk[3]), d1 + _colsum(dpre * xk[2]),
                    d2 + _colsum(dpre * xk[1]), d3 + _colsum(dpre * xk[0]))

        z = jnp.zeros((1, CONV_CB), F32)
        _, db, d0, d1, d2, d3 = lax.fori_loop(0, n, step, (jnp.zeros((HALO, CONV_CB), F32), z, z, z, z, z))
        db_ref[...] = db
        dw_ref[...] = jnp.concatenate([d0, d1, d2, d3], axis=0)

    nb = XBC // CONV_CB
    return pl.pallas_call(
        body, name=name, grid=(nb,),
        in_specs=[pl.BlockSpec((L, CONV_CB), lambda j: (0, j + C_XBC // CONV_CB)),
                  pl.BlockSpec((L, CONV_CB), lambda j: (0, j)),
                  pl.BlockSpec((4, CONV_CB), lambda j: (0, j)), pl.BlockSpec((1, CONV_CB), lambda j: (0, j)),
                  pl.BlockSpec(memory_space=pl.ANY)],
        out_specs=[pl.BlockSpec((L, CONV_CB), lambda j: (0, j + C_XBC // CONV_CB)),
                   pl.BlockSpec((4, CONV_CB), lambda j: (0, j)), pl.BlockSpec((1, CONV_CB), lambda j: (0, j))],
        out_shape=[jax.ShapeDtypeStruct((L, NPROJ), BF16), jax.ShapeDtypeStruct((4, XBC), F32),
                   jax.ShapeDtypeStruct((1, XBC), F32)],
        input_output_aliases={4: 0},
        compiler_params=_params(("parallel",), VMEM_BIG))(proj, dy, w, b, dproj)


def _pool_fwd(proj, *, name):
    L = proj.shape[0]
    rc = _time_chunk(L)
    n = L // rc

    def body(x_ref, o_ref, pad):
        g = pl.program_id(0)
        pad[0:HALO, :] = jnp.zeros((HALO, PGW), F32)

        def fill(i, c):
            r0 = pl.multiple_of(i * rc, rc)
            pad[pl.ds(r0 + HALO, rc), :] = x_ref[pl.ds(r0, rc), :].astype(F32)
            return c

        lax.fori_loop(0, n, fill, 0)
        rows = lax.broadcasted_iota(jnp.int32, (rc, PGW), 0)

        for gi in range(4):
            win = 2 << gi

            @pl.when(g == gi)
            def _(gi=gi, win=win):
                def step(i, c):
                    r0 = pl.multiple_of(i * rc, rc)
                    ext = pad[pl.ds(r0, rc + HALO), :]
                    s = ext
                    sh = 1
                    while sh < win:
                        s = s + pltpu.roll(s, sh, 0)
                        sh *= 2
                    cnt = jnp.minimum(rows + (r0 + 1), win).astype(F32)
                    o_ref[pl.ds(r0, rc), :] = (s[HALO:] / cnt - ext[HALO:]).astype(o_ref.dtype)
                    return c

                lax.fori_loop(0, n, step, 0)

    return pl.pallas_call(
        body, name=name, grid=(4,),
        in_specs=[pl.BlockSpec((L, PGW), lambda j: (0, j + C_POOL // PGW))],
        out_specs=pl.BlockSpec((L, PGW), lambda j: (0, j)),
        out_shape=jax.ShapeDtypeStruct((L, POOL_W), BF16),
        scratch_shapes=[pltpu.VMEM((L + HALO, PGW), F32)],
        compiler_params=_params(("parallel",), VMEM_BIG))(proj)


def _pool_bwd(dpooled, dproj, *, name):
    L = dpooled.shape[0]
    rc = _time_chunk(L)
    n = L // rc

    def body(d_ref, dp_in, o_ref, pad):
        del dp_in
        g = pl.program_id(0)
        pad[L:L + HALO, :] = jnp.zeros((HALO, PGW), F32)
        rows = lax.broadcasted_iota(jnp.int32, (rc, PGW), 0)

        for gi in range(4):
            win = 2 << gi

            @pl.when(g == gi)
            def _(gi=gi, win=win):
                def fill(i, c):
                    r0 = pl.multiple_of(i * rc, rc)
                    cnt = jnp.minimum(rows + (r0 + 1), win).astype(F32)
                    pad[pl.ds(r0, rc), :] = d_ref[pl.ds(r0, rc), :] / cnt
                    return c

                lax.fori_loop(0, n, fill, 0)

                def step(i, c):
                    r0 = pl.multiple_of(i * rc, rc)
                    s = pad[pl.ds(r0, rc + HALO), :]
                    sh = 1
                    while sh < win:
                        s = s + pltpu.roll(s, rc + HALO - sh, 0)
                        sh *= 2
                    o_ref[pl.ds(r0, rc), :] = (s[:rc] - d_ref[pl.ds(r0, rc), :]).astype(o_ref.dtype)
                    return c

                lax.fori_loop(0, n, step, 0)

    return pl.pallas_call(
        body, name=name, grid=(4,),
        in_specs=[pl.BlockSpec((L, PGW), lambda j: (0, j)), pl.BlockSpec(memory_space=pl.ANY)],
        out_specs=pl.BlockSpec((L, PGW), lambda j: (0, j + C_POOL // PGW)),
        out_shape=jax.ShapeDtypeStruct((L, NPROJ), BF16),
        scratch_shapes=[pltpu.VMEM((L + HALO, PGW), F32)],
        input_output_aliases={1: 0},
        compiler_params=_params(("parallel",), VMEM_BIG))(dpooled, dproj)


_SPLIT_DT = jnp.bfloat16


def _ssd_consts():
    tri = np.tril(np.ones((Q, Q), np.float32))
    exp = np.zeros((128, DI), np.float32)
    for h in range(NH):
        exp[h, h * HP:(h + 1) * HP] = 1.0
    exp2 = np.concatenate([exp, exp], axis=0)
    return (jnp.asarray(tri, dtype=_SPLIT_DT), jnp.asarray(tri.T.copy(), dtype=_SPLIT_DT),
            jnp.asarray(exp2, dtype=_SPLIT_DT))


def _split(v, n):
    parts, r = [], v
    for _ in range(n):
        p = r.astype(_SPLIT_DT)
        parts.append(p)
        r = r - p.astype(F32)
    return parts


def _bdot(a, b, dims):
    return lax.dot_general(a, b, (dims, ((), ())), preferred_element_type=F32)


def _tri_sum(t_ref, v):
    r = _bdot(t_ref[...], jnp.concatenate(_split(v, 3), axis=1), NN)
    return r[:, :128] + r[:, 128:256] + r[:, 256:]


def _expand(v, e2_ref):
    return _bdot(jnp.concatenate(_split(v, 2), axis=1), e2_ref[...], NN)


def _reduce_heads(vals, eg):
    parts = []
    for v in vals:
        parts += _split(v, 2)
    r = _bdot(jnp.concatenate(parts, axis=0), eg, NT)
    return [r[2 * i * Q:(2 * i + 1) * Q] + r[(2 * i + 1) * Q:(2 * i + 2) * Q] for i in range(len(vals))]


def _ssd_common(xbc_ref, dtw_ref, dtb_ref, arow_ref, t_ref, e_ref):
    pre = dtw_ref[:, :128] + dtb_ref[...]
    dt = _softplus(pre)
    acs = _tri_sum(t_ref, dt * arow_ref[...])
    acs_x = _expand(acs, e_ref)
    dt_x = _expand(dt, e_ref)
    xs = xbc_ref[:, 0:DI]
    return pre, dt, acs, acs.T, acs_x, dt_x, xs


CONV_SLAB = 512


def _ssd_fwd(raw, dtp, cw, cb, dtb, arow, dsk_x, *, name):
    L = raw.shape[0]
    nc = L // Q
    tri, _, expand = _ssd_consts()

    def body(raw_ref, halo_ref, cw_ref, cb_ref, dtw_ref, dtb_ref, arow_ref, dsk_ref, t_ref, e_ref,
             y_ref, hs_ref, xbc_ref, h_scr):
        c = pl.program_id(0)

        @pl.when(c == 0)
        def _():
            h_scr[...] = jnp.zeros_like(h_scr)

        for lo in range(0, XBC, CONV_SLAB):
            sl = slice(lo, lo + CONV_SLAB)
            prev = jnp.where(c > 0, halo_ref[:, sl], 0.0)
            ext = jnp.concatenate([prev, raw_ref[:, sl]], axis=0)
            acc = cb_ref[:, sl] + ext * cw_ref[3:4, sl]
            for j in (1, 2, 3):
                acc = acc + pltpu.roll(ext, j, 0) * cw_ref[3 - j:4 - j, sl]
            acc = acc[8:]
            xbc_ref[:, sl] = acc * _sigmoid(acc)

        _, dt, acs, acs_t, acs_x, dt_x, xs = _ssd_common(xbc_ref, dtw_ref, dtb_ref, arow_ref, t_ref, e_ref)
        xdt = xs * dt_x
        eacs = jnp.exp(acs_x)
        acs_last = acs_x[Q - 1:Q, :]
        dec = jnp.exp(acs_last - acs_x)
        hs_ref[0] = h_scr[...].astype(hs_ref.dtype)
        causal = lax.broadcasted_iota(jnp.int32, (Q, Q), 0) >= lax.broadcasted_iota(jnp.int32, (Q, Q), 1)
        first = lax.broadcasted_iota(jnp.int32, (Q, 128), 1) < HP
        for g in range(NG):
            bg = xbc_ref[:, DI + g * NS:DI + (g + 1) * NS]
            cg = xbc_ref[:, DI + NG * NS + g * NS:DI + NG * NS + (g + 1) * NS]
            s = _dot(cg, bg, NT)
            sl = slice(g * GW, (g + 1) * GW)
            hg = h_scr[:, sl]
            yoff = _dot(cg, hg, NN) * eacs[:, sl]
            st = _dot(bg, xdt[:, sl] * dec[:, sl], TN)
            h_scr[:, sl] = hg * eacs[Q - 1:Q, sl] + st
            for j in range(4):
                lo = g * GW + j * 128
                xb = xdt[:, lo:lo + 128]
                yp = yoff[:, j * 128:(j + 1) * 128] + dsk_ref[:, lo:lo + 128] * xs[:, lo:lo + 128]
                for e in range(2):
                    h = g * 8 + j * 2 + e
                    lm = jnp.exp(jnp.where(causal, acs[:, h:h + 1] - acs_t[h:h + 1, :], NEG))
                    xm = jnp.where(first if e == 0 else jnp.logical_not(first), xb, 0.0)
                    yp = yp + _dot(s * lm, xm, NN)
                y_ref[:, lo:lo + 128] = yp.astype(y_ref.dtype)

    const = lambda c: (0, 0)
    return pl.pallas_call(
        body, name=name, grid=(nc,),
        in_specs=[pl.BlockSpec((Q, XBC), lambda c: (c, 0)),
                  pl.BlockSpec((8, XBC), lambda c: (jnp.maximum(c * (Q // 8) - 1, 0), 0)),
                  pl.BlockSpec((4, XBC), const), pl.BlockSpec((1, XBC), const),
                  pl.BlockSpec((Q, DT_PAD), lambda c: (c, 0)),
                  pl.BlockSpec((1, 128), const), pl.BlockSpec((1, 128), const), pl.BlockSpec((1, DI), const),
                  pl.BlockSpec((Q, Q), const), pl.BlockSpec((256, DI), const)],
        out_specs=[pl.BlockSpec((Q, DI), lambda c: (c, 0)), pl.BlockSpec((1, NS, DI), lambda c: (c, 0, 0)),
                   pl.BlockSpec((Q, XBC), lambda c: (c, 0))],
        out_shape=[jax.ShapeDtypeStruct((L, DI), BF16), jax.ShapeDtypeStruct((nc, NS, DI), F32),
                   jax.ShapeDtypeStruct((L, XBC), F32)],
        scratch_shapes=[pltpu.VMEM((NS, DI), F32)],
        compiler_params=_params(("arbitrary",), VMEM_BIG))(raw, raw, cw, cb, dtp, dtb, arow, dsk_x, tri, expand)


def _ssd_bwd(dy, xbc, proj, hs, dtb, arow, dsk_x, dproj, *, name):
    L = xbc.shape[0]
    nc = L // Q
    tri, triu, expand = _ssd_consts()

    def body(dy_ref, xbc_ref, dtw_ref, hs_ref, dtb_ref, arow_ref, dsk_ref, t_ref, u_ref, e_ref, dp_in,
             dxbc_ref, ddtw_ref, da_ref, ddx_ref, ddtb_ref, dh_scr):
        del dp_in
        i = pl.program_id(0)

        @pl.when(i == 0)
        def _():
            dh_scr[...] = jnp.zeros_like(dh_scr)

        pre, dt, acs, acs_t, acs_x, dt_x, xs = _ssd_common(xbc_ref, dtw_ref, dtb_ref, arow_ref, t_ref, e_ref)
        dyv = dy_ref[...]
        xdt = xs * dt_x
        eacs = jnp.exp(acs_x)
        acs_last = acs_x[Q - 1:Q, :]
        dec = jnp.exp(acs_last - acs_x)
        gy = dyv * eacs
        causal = lax.broadcasted_iota(jnp.int32, (Q, Q), 0) >= lax.broadcasted_iota(jnp.int32, (Q, Q), 1)
        first = lax.broadcasted_iota(jnp.int32, (Q, 128), 1) < HP
        lane_h = lax.broadcasted_iota(jnp.int32, (Q, 128), 1)
        sub_h = lax.broadcasted_iota(jnp.int32, (128, Q), 0)
        last_row = lax.broadcasted_iota(jnp.int32, (Q, GW), 0) == Q - 1
        dacs = jnp.zeros((Q, 128), F32)
        dacs_t = jnp.zeros((128, Q), F32)
        ddt = jnp.zeros((Q, 128), F32)
        for g in range(NG):
            bg = xbc_ref[:, DI + g * NS:DI + (g + 1) * NS]
            cg = xbc_ref[:, DI + NG * NS + g * NS:DI + NG * NS + (g + 1) * NS]
            s = _dot(cg, bg, NT)
            sl = slice(g * GW, (g + 1) * GW)
            hg = hs_ref[0, :, sl].astype(F32)
            dhn = dh_scr[:, sl]
            eal = eacs[Q - 1:Q, sl]
            gg = gy[:, sl]
            dax = gg * _dot(cg, hg, NN)
            dcg = _dot(gg, hg, NT)
            dh_scr[:, sl] = _dot(cg, gg, TN) + dhn * eal
            dal = eal * _colsum(dhn * hg)
            xdd = xdt[:, sl] * dec[:, sl]
            dbg = _dot(xdd, dhn, NT)
            wv = _dot(bg, dhn, NN)
            dd = wv * xdd
            dax = dax - dd
            dal = dal + _colsum(dd)
            dax = dax + jnp.where(last_row, dal, 0.0)
            dxdt_g = wv * dec[:, sl]
            ds = jnp.zeros((Q, Q), F32)
            dxdt_blocks = []
            for j in range(4):
                lo = g * GW + j * 128
                xb = xdt[:, lo:lo + 128]
                dyb = dyv[:, lo:lo + 128]
                dxb = dxdt_g[:, j * 128:(j + 1) * 128]
                for e in range(2):
                    h = g * 8 + j * 2 + e
                    lm = jnp.exp(jnp.where(causal, acs[:, h:h + 1] - acs_t[h:h + 1, :], NEG))
                    m = s * lm
                    dym = jnp.where(first if e == 0 else jnp.logical_not(first), dyb, 0.0)
                    dm = _dot(dym, xb, NT)
                    r = dm * m
                    dacs = dacs + jnp.where(lane_h == h, jnp.sum(r, axis=1, keepdims=True), 0.0)
                    dacs_t = dacs_t + jnp.where(sub_h == h, _colsum(r), 0.0)
                    ds = ds + dm * lm
                    dxb = dxb + _dot(m, dym, TN)
                dxdt_blocks.append(dxb)
            dxdt = jnp.concatenate(dxdt_blocks, axis=1)
            dcg = dcg + _dot(ds, bg, NN)
            dbg = dbg + _dot(ds, cg, TN)
            dxbc_ref[:, DI + g * NS:DI + (g + 1) * NS] = dbg
            dxbc_ref[:, DI + NG * NS + g * NS:DI + NG * NS + (g + 1) * NS] = dcg
            dxbc_ref[:, sl] = dsk_ref[:, sl] * dyv[:, sl] + dxdt * dt_x[:, sl]
            ddt_g, dacs_g = _reduce_heads([dxdt * xs[:, sl], dax], e_ref[0:128, sl])
            ddt = ddt + ddt_g
            dacs = dacs + dacs_g
        dacs = dacs - dacs_t.T
        ddta = _tri_sum(u_ref, dacs)
        ddt = ddt + ddta * arow_ref[...]
        ddtw = jnp.where(lane_h < NH, ddt * _sigmoid(pre), 0.0)
        ddtw_ref[...] = jnp.concatenate([ddtw, jnp.zeros((Q, DT_PAD - 128), F32)], axis=1).astype(ddtw_ref.dtype)
        _acc_out(da_ref, _colsum(ddta * dt), i)
        _acc_out(ddx_ref, _colsum(dyv * xs), i)
        _acc_out(ddtb_ref, _colsum(ddtw), i)

    rev = lambda c: (nc - 1 - c, 0)
    const = lambda c: (0, 0)
    return pl.pallas_call(
        body, name=name, grid=(nc,),
        in_specs=[pl.BlockSpec((Q, DI), rev), pl.BlockSpec((Q, XBC), rev),
                  pl.BlockSpec((Q, DT_PAD), rev),
                  pl.BlockSpec((1, NS, DI), lambda c: (nc - 1 - c, 0, 0)),
                  pl.BlockSpec((1, 128), const), pl.BlockSpec((1, 128), const), pl.BlockSpec((1, DI), const),
                  pl.BlockSpec((Q, Q), const), pl.BlockSpec((Q, Q), const), pl.BlockSpec((256, DI), const),
                  pl.BlockSpec(memory_space=pl.ANY)],
        out_specs=[pl.BlockSpec((Q, XBC), rev),
                   pl.BlockSpec((Q, DT_PAD), lambda c: (nc - 1 - c, C_DT // DT_PAD)),
                   pl.BlockSpec((1, 128), const), pl.BlockSpec((1, DI), const), pl.BlockSpec((1, 128), const)],
        out_shape=[jax.ShapeDtypeStruct((L, XBC), F32), jax.ShapeDtypeStruct((L, NPROJ), BF16),
                   jax.ShapeDtypeStruct((1, 128), F32), jax.ShapeDtypeStruct((1, DI), F32),
                   jax.ShapeDtypeStruct((1, 128), F32)],
        scratch_shapes=[pltpu.VMEM((NS, DI), F32)],
        input_output_aliases={10: 1},
        compiler_params=_params(("arbitrary",), VMEM_BIG))(dy, xbc, proj, hs, dtb, arow, dsk_x, tri, triu,
                                                          expand, dproj)


def _adam_update(wv, gv, mv, vv):
    c1 = 1.0 - ADAM_B1 ** ADAM_STEP
    c2 = 1.0 - ADAM_B2 ** ADAM_STEP
    mn = ADAM_B1 * mv + (1.0 - ADAM_B1) * gv
    vn = ADAM_B2 * vv + (1.0 - ADAM_B2) * (gv * gv)
    return -ADAM_LR * ((mn / c1) / (jnp.sqrt(vn / c2) + ADAM_EPS) + ADAM_WD * wv), mn, vn


def _adamw(w, g, m, v, *, name, tr=None):
    R = w.shape[0]
    rest = tuple(w.shape[1:])
    if tr is None:
        tr = _pick(R, (256, 128, 64, 32, 16, 8))
    assert R % tr == 0

    def body(w_ref, g_ref, m_ref, v_ref, d_ref, mo_ref, vo_ref):
        d_ref[...], mo_ref[...], vo_ref[...] = _adam_update(w_ref[...], g_ref[...], m_ref[...], v_ref[...])

    zeros = (0,) * len(rest)
    spec = pl.BlockSpec((tr,) + rest, lambda i: (i,) + zeros)
    return pl.pallas_call(body, name=name, grid=(R // tr,), in_specs=[spec] * 4, out_specs=[spec] * 3,
                          out_shape=[jax.ShapeDtypeStruct(w.shape, F32)] * 3,
                          compiler_params=_params(("parallel",)))(w, g, m, v)


def _adamw_small(svrow, g_conv, params, *, name):
    n = len(params)

    def body(*refs):
        sv_ref, gc_ref = refs[0], refs[1]
        ins, outs = refs[2:2 + 3 * n], refs[2 + 3 * n:]
        for p, (key, w, _, _) in enumerate(params):
            w_ref, m_ref, v_ref = ins[3 * p:3 * p + 3]
            g_ref, d_ref, mo_ref, vo_ref = outs[4 * p:4 * p + 4]
            gv = gc_ref[...] if key == "conv_w" else sv_ref[:, SV_OFF[key]:SV_OFF[key] + w.shape[1]]
            g_ref[...] = gv
            d_ref[...], mo_ref[...], vo_ref[...] = _adam_update(w_ref[...], gv, m_ref[...], v_ref[...])

    vm = pl.BlockSpec(memory_space=pltpu.VMEM)
    args = [svrow, g_conv]
    shapes = []
    for _, w, m, v in params:
        args += [w, m, v]
        shapes += [jax.ShapeDtypeStruct(w.shape, F32)] * 4
    res = pl.pallas_call(body, name=name, in_specs=[vm] * len(args), out_specs=[vm] * len(shapes),
                         out_shape=shapes)(*args)
    return {key: tuple(res[4 * p:4 * p + 4]) for p, (key, _, _, _) in enumerate(params)}


def _slab_sum(recv, *, tile, name):
    rows = recv.shape[1]
    assert rows % tile == 0 and tile % 16 == 0

    def body(r_ref, o_ref):
        acc = r_ref[0].astype(F32)
        for j in range(1, N_DEV):
            acc = acc + r_ref[j].astype(F32)
        o_ref[...] = acc

    return pl.pallas_call(body, name=name, grid=(rows // tile,),
                          in_specs=[pl.BlockSpec((N_DEV, tile, D), lambda i: (0, i, 0))],
                          out_specs=pl.BlockSpec((tile, D), lambda i: (i, 0)),
                          out_shape=jax.ShapeDtypeStruct((rows, D), F32),
                          compiler_params=_params(("parallel",)))(recv)


MESH = pl.DeviceIdType.MESH


def _coords():
    return lax.axis_index("x"), lax.axis_index("y"), lax.axis_index("c")


def _peer(k):
    x, y, c = _coords()
    px = 1 - x if k & 4 else x
    py = 1 - y if k & 2 else y
    pc = 1 - c if k & 1 else c
    return (px, py, pc), 4 * px + 2 * py + pc


def _rcopy(src, dst, ssem, rsem, dev):
    return pltpu.make_async_remote_copy(src_ref=src, dst_ref=dst, send_sem=ssem, recv_sem=rsem,
                                        device_id=dev, device_id_type=MESH)


def _exchange_all(src_of, dst_slot, send_sems, recv_sems):
    x, y, c = _coords()
    me = 4 * x + 2 * y + c
    sent = []
    for k in range(1, N_DEV):
        dev, pidx = _peer(k)
        cp = _rcopy(src_of(pidx), dst_slot(me), send_sems.at[k - 1], recv_sems.at[k - 1], dev)
        cp.start()
        sent.append(cp)
    for k in range(1, N_DEV):
        dev, pidx = _peer(k)
        _rcopy(src_of(pidx), dst_slot(pidx), send_sems.at[k - 1], recv_sems.at[k - 1], dev).wait_recv()
    for cp in sent:
        cp.wait_send()


def _rows_of_slots(buf, nslots):
    rows = lax.broadcasted_iota(jnp.int32, (8, buf.shape[-1]), 0)
    out = jnp.zeros((8, buf.shape[-1]), F32)
    for j in range(nslots):
        out = out + jnp.where(rows == j, buf[j], 0.0)
    return out


def _ada_fwd(c, w_ada, b_r, *, name):
    wloc = w_ada.shape[1]

    def body(c_ref, w_ref, b_ref, mod_ref, call_ref, csrc, cbuf, psrc, pbuf, s1, r1, s2, r2):
        x, y, cc = _coords()
        me = 4 * x + 2 * y + cc
        csrc[...] = jnp.broadcast_to(c_ref[...], (8, D))
        cbuf[me] = csrc[...]
        _exchange_all(lambda p: csrc, lambda s: cbuf.at[s], s1, r1)
        call = _rows_of_slots(cbuf, N_DEV)
        call_ref[...] = call
        prod = _dot_hi(_silu(call), w_ref[...])
        for b in range(N_DEV):
            psrc[b] = jnp.broadcast_to(prod[b:b + 1, :], (8, wloc))
        pbuf[me] = psrc[me]
        _exchange_all(lambda p: psrc.at[p], lambda s: pbuf.at[s], s2, r2)
        mod_ref[...] = _rows_of_slots(pbuf, N_DEV) + b_ref[...]

    vm = pl.BlockSpec(memory_space=pltpu.VMEM)
    return pl.pallas_call(
        body, name=name, in_specs=[vm, vm, vm], out_specs=[vm, vm],
        out_shape=[jax.ShapeDtypeStruct((N_DEV, wloc), F32), jax.ShapeDtypeStruct((N_DEV, D), F32)],
        scratch_shapes=[pltpu.VMEM((8, D), F32), pltpu.VMEM((N_DEV, 8, D), F32),
                        pltpu.VMEM((N_DEV, 8, wloc), F32), pltpu.VMEM((N_DEV, 8, wloc), F32),
                        pltpu.SemaphoreType.DMA((N_DEV - 1,)), pltpu.SemaphoreType.DMA((N_DEV - 1,)),
                        pltpu.SemaphoreType.DMA((N_DEV - 1,)), pltpu.SemaphoreType.DMA((N_DEV - 1,))],
        compiler_params=pltpu.CompilerParams(vmem_limit_bytes=VMEM_BIG))(c, w_ada, b_r)


def _gather_slabs(slab, *, name):
    def body(x_ref, out_ref, send_sems, recv_sems, local_sem):
        x, y, c = _coords()
        me, sibling = (x, y, c), (x, y, 1 - c)
        chips = [(1 - x, y), (x, 1 - y), (1 - x, 1 - y)]

        def slot(px, py, pc):
            return out_ref.at[4 * px + 2 * py + pc]

        def copy(k, block, to, src=None):
            return _rcopy(slot(*block) if src is None else src, slot(*block), send_sems.at[k], recv_sems.at[k], to)

        mine = pltpu.make_async_copy(x_ref, slot(*me), local_sem)
        mine.start()
        first = [copy(0, me, sibling, src=x_ref)]
        first += [copy(1 + j, me, (*chip, c), src=x_ref) for j, chip in enumerate(chips)]
        for cp in first:
            cp.start()
        passed = [copy(4 + j, (*chip, c), sibling) for j, chip in enumerate(chips)]
        for j, chip in enumerate(chips):
            copy(1 + j, (*chip, c), me).wait_recv()
            passed[j].start()
        copy(0, sibling, me).wait_recv()
        for j, chip in enumerate(chips):
            copy(4 + j, (*chip, 1 - c), me).wait_recv()
        for cp in first + passed:
            cp.wait_send()
        mine.wait()

    anyspec = pl.BlockSpec(memory_space=pl.ANY)
    return pl.pallas_call(
        body, name=name, in_specs=[anyspec], out_specs=anyspec,
        out_shape=jax.ShapeDtypeStruct((N_DEV,) + slab.shape, slab.dtype),
        scratch_shapes=[pltpu.SemaphoreType.DMA((7,)), pltpu.SemaphoreType.DMA((7,)), pltpu.SemaphoreType.DMA],
    )(slab)


_HBM =pl.BlockSpec(memory_space=pltpu.HBM)
_SEM = pl.BlockSpec(memory_space=pltpu.SEMAPHORE)
_EFFECT = pltpu.SideEffectType.DATAFLOW_SIDE_EFFECTING


def _xchg_src(src_ref, pidx, per_peer):
    return src_ref.at[pidx] if per_peer else src_ref


def _xchg_start(src, *, per_peer, name):
    rows = src.shape[-2]
    land_shape = (N_DEV, rows, D)

    def body(src_ref, land_ref, send_sems, recv_sems, src_thru, land_thru, token):
        del src_thru, land_thru
        x, y, c = _coords()
        me = 4 * x + 2 * y + c
        for k in range(1, N_DEV):
            dev, pidx = _peer(k)
            _rcopy(_xchg_src(src_ref, pidx, per_peer), land_ref.at[me], send_sems.at[k - 1],
                   recv_sems.at[k - 1], dev).start()
        token[...] = jnp.zeros_like(token)

    return pl.pallas_call(
        body, name=name,
        out_shape=(pltpu.SemaphoreType.DMA((N_DEV - 1,)), pltpu.SemaphoreType.DMA((N_DEV - 1,)),
                   pltpu.HBM(src.shape, src.dtype), pltpu.HBM(land_shape, src.dtype),
                   jax.ShapeDtypeStruct((8, 128), F32)),
        in_specs=(_HBM, _HBM),
        out_specs=(_SEM, _SEM, _HBM, _HBM, pl.BlockSpec(memory_space=pltpu.VMEM)),
        input_output_aliases={0: 2, 1: 3},
        compiler_params=pltpu.CompilerParams(has_side_effects=_EFFECT),
    )(pltpu.with_memory_space_constraint(src, pltpu.HBM),
      pltpu.with_memory_space_constraint(lax.empty(land_shape, src.dtype), pltpu.HBM))


def _xchg_wait(started, after, *, per_peer, name):
    send_sems, recv_sems, src_thru, land_thru, _ = started

    def body(src_ref, land_ref, send_sems, recv_sems, after_ref, src_dead, got_ref):
        del after_ref, src_dead, got_ref
        for k in range(1, N_DEV):
            dev, pidx = _peer(k)
            cp = _rcopy(_xchg_src(src_ref, pidx, per_peer), land_ref.at[pidx], send_sems.at[k - 1],
                        recv_sems.at[k - 1], dev)
            cp.wait_send()
            cp.wait_recv()

    return pl.pallas_call(
        body, name=name,
        out_shape=(pltpu.HBM(src_thru.shape, src_thru.dtype), pltpu.HBM(land_thru.shape, land_thru.dtype)),
        in_specs=(_HBM, _HBM, _SEM, _SEM, pl.BlockSpec(memory_space=pl.ANY)),
        out_specs=(_HBM, _HBM),
        input_output_aliases={0: 0, 1: 1},
        compiler_params=pltpu.CompilerParams(has_side_effects=_EFFECT),
    )(src_thru, land_thru, send_sems, recv_sems, after)


def _dep(token):
    return (token, (8, 128), lambda i, j, k: (0, 0))


def _small_allsum(sv, *, name):
    def body(sv_ref, all_ref, sum_ref, send_sems, recv_sems):
        x, y, c = _coords()
        me = 4 * x + 2 * y + c
        all_ref[me] = sv_ref[...]
        _exchange_all(lambda p: sv_ref, lambda s: all_ref.at[s], send_sems, recv_sems)
        acc = all_ref[0]
        for j in range(1, N_DEV):
            acc = acc + all_ref[j]
        sum_ref[...] = acc

    vm = pl.BlockSpec(memory_space=pltpu.VMEM)
    return pl.pallas_call(
        body, name=name, in_specs=[vm], out_specs=[vm, vm],
        out_shape=[jax.ShapeDtypeStruct((N_DEV, SV_ROWS, 128), F32), jax.ShapeDtypeStruct((SV_ROWS, 128), F32)],
        scratch_shapes=[pltpu.SemaphoreType.DMA((7,)), pltpu.SemaphoreType.DMA((7,))],
    )(sv)


def _ada_bwd(call, dmod_loc, *, name):
    wloc = dmod_loc.shape[1]

    def body(c_ref, d_ref, o_ref):
        o_ref[...] = _dot_hi(_silu(c_ref[...]), d_ref[...], TN)

    vm = pl.BlockSpec(memory_space=pltpu.VMEM)
    return pl.pallas_call(body, name=name, in_specs=[vm, vm], out_specs=vm,
                          out_shape=jax.ShapeDtypeStruct((D, wloc), F32),
                          compiler_params=pltpu.CompilerParams(vmem_limit_bytes=VMEM_BIG))(call, dmod_loc)


def _pad_rows(a, rows):
    return jnp.pad(a, ((0, rows - a.shape[0]), (0, 0)))


IN_SHIFT = tuple((IN_ROWS * j) % 16 for j in range(N_DEV))
IN_BASE = tuple(IN_ROWS * j - IN_SHIFT[j] for j in range(N_DEV))
IN_SEGMENTS = ((2048, XBC, C_XBC), (5152, 1024, C_POOL), (0, 2048, C_Z), (6176, 2048, C_GATE), (5120, 32, C_DT))


def _global_pieces(gs):
    pieces = []
    for j in range(N_DEV):
        lo, hi = 0, IN_ROWS_P
        if j > 0 and IN_BASE[j - 1] + IN_ROWS_P > IN_BASE[j]:
            pieces.append((IN_BASE[j], 16, gs[j - 1, IN_ROWS_P - 16:IN_ROWS_P] + gs[j, 0:16]))
            lo = 16
        if j + 1 < N_DEV and IN_BASE[j] + IN_ROWS_P > IN_BASE[j + 1]:
            hi = IN_ROWS_P - 16
        pieces.append((IN_BASE[j] + lo, hi - lo, gs[j, lo:hi]))
    return pieces


def _reorder_in_rows(gs):
    pieces = _global_pieces(gs)
    parts = []
    for lo, n, _ in IN_SEGMENTS:
        for p0, pn, arr in pieces:
            a, b = max(lo, p0), min(lo + n, p0 + pn)
            if a < b:
                parts.append(arr[a - p0:b - p0])
    parts.append(jnp.zeros((DT_PAD - 32, D), gs.dtype))
    return jnp.concatenate(parts, axis=0)


def _restore_in_shards(d):
    slabs = []
    for j in range(N_DEV):
        parts = []
        r, end = IN_BASE[j], IN_BASE[j] + IN_ROWS_P
        while r < end:
            lo, n, new = next(s for s in IN_SEGMENTS if s[0] <= r < s[0] + s[1])
            e = min(end, lo + n)
            parts.append(d[new + r - lo:new + e - lo])
            r = e
        slabs.append(jnp.concatenate(parts, axis=0))
    return jnp.stack(slabs, axis=0)


def _pack_sv(parts):
    flat = []
    for n, size in SV_PARTS:
        v = parts[n].reshape(-1).astype(F32)
        flat.append(jnp.pad(v, (0, size - v.shape[0])))
    v = jnp.concatenate(flat)
    return jnp.pad(v, (0, SV_ROWS * 128 - v.shape[0])).reshape(SV_ROWS, 128)


def _sv_get(flat, n, size):
    return flat[SV_OFF[n]:SV_OFF[n] + size]


def kernel(x, c, w_ada, b_ada, norm_mix_w, w_in, conv_w, conv_b, dt_bias, a_log, d_skip, ssd_norm_w, w_branch_ssd, pool_w, pool_scale, w_branch_pool, w_out, norm_mlp_w, w_up, w_down, norm_final_w, loss_target, m_w_ada, m_b_ada, m_norm_mix_w, m_w_in, m_conv_w, m_conv_b, m_dt_bias, m_a_log, m_d_skip, m_ssd_norm_w, m_w_branch_ssd, m_pool_w, m_pool_scale, m_w_branch_pool, m_w_out, m_norm_mlp_w, m_w_up, m_w_down, m_norm_final_w, v_w_ada, v_b_ada, v_norm_mix_w, v_w_in, v_conv_w, v_conv_b, v_dt_bias, v_a_log, v_d_skip, v_ssd_norm_w, v_w_branch_ssd, v_pool_w, v_pool_scale, v_w_branch_pool, v_w_out, v_norm_mlp_w, v_w_up, v_w_down, v_norm_final_w):
    xs_ = x[0]
    tgt = loss_target[0]
    L = xs_.shape[0]
    me = 4 * lax.axis_index("x") + 2 * lax.axis_index("y") + lax.axis_index("c")
    wloc = w_ada.shape[2]

    mod_p, c_all = _ada_fwd(c, w_ada[0], b_ada.reshape(N_DEV, wloc), name="ada_fwd")
    mod = mod_p.reshape(6, D)
    shift_m, scale_m, gate_m, shift_f, scale_f, gate_f = [mod[i:i + 1] for i in range(6)]

    conv_bits = lax.bitcast_convert_type(conv_w[0], SLAB_DT).reshape(3, D)
    in_shift = (IN_ROWS * me) % 16
    slab_in = lax.dynamic_update_slice(jnp.zeros((IN_ROWS_P, D), SLAB_DT), w_in[0].T.astype(SLAB_DT),
                                       (in_shift, 0))
    slab_in = jnp.concatenate([slab_in, _pad_rows(conv_bits, CONV_ROWS)], axis=0)
    slab_rest = jnp.concatenate([
        w_branch_ssd[0].astype(SLAB_DT),
        pool_w[0].reshape(32, D).astype(SLAB_DT),
        w_branch_pool[0].astype(SLAB_DT),
        w_out[0].astype(SLAB_DT),
        w_up[0].T.astype(SLAB_DT),
        w_down[0].astype(SLAB_DT)], axis=0)
    slab_in, mod_p = lax.optimization_barrier((slab_in, mod_p))
    gs_in = _gather_slabs(slab_in, name="gather_w_in")
    slab_rest, gs_in = lax.optimization_barrier((slab_rest, gs_in))
    rest_started = _xchg_start(slab_rest, per_peer=False, name="gather_rest_start")
    gather_token = rest_started[4]

    w_in_t = _reorder_in_rows(gs_in)
    conv_full = lax.bitcast_convert_type(
        gs_in[:, IN_ROWS_P:IN_ROWS_P + 3].reshape(N_DEV, 4, XBC // N_DEV, 2), F32)
    conv_full = conv_full.transpose(1, 0, 2).reshape(4, XBC)

    dtb = jnp.pad(dt_bias, ((0, 0), (0, 128 - NH)))
    arow = jnp.pad(-jnp.exp(a_log), ((0, 0), (0, 128 - NH)))
    dsk_x = jnp.repeat(d_skip, HP, axis=1)

    tm = _pick(L, (1024, 512, 256, 128))
    tm2 = _pick(L, (2048, 1024, 512, 256, 128))
    tkl = _pick(L, (4096, 2048, 1024, 512, 256, 128))
    tkl2 = _pick(L, (2048, 1024, 512, 256, 128))

    tmh = _pick(L, (512, 256, 128))
    zcol = C_Z // DI
    gcol = C_GATE // (2 * D)

    def whole_rows(w):
        return lambda t: ((L, w), BF16, (t, w), lambda i, j, k: (i, 0))

    def norm1_pro(x_ref, ex, outs, j):
        @pl.when(j == 0)
        def _():
            xv = x_ref[...]
            r = lax.rsqrt(jnp.mean(xv * xv, axis=-1, keepdims=True) + EPS)
            outs[1][...] = (xv * r * ex[0][...] * (1.0 + ex[1][...]) + ex[2][...]).astype(outs[1].dtype)

        return outs[1][...]

    def proj_ep(acc, ex, outs):
        outs[0][...] = acc

        @pl.when(pl.program_id(1) == NPROJ // 768 - 1)
        def _():
            outs[2][...] = acc[:, 768 - DT_PAD:]

    proj, h1, dtp = _mm(
        xs_, w_in_t, "nt", name="in_proj", tm=tm2, tn=768, tk=D,
        extras=[(norm_mix_w, *_vecs()), (scale_m, *_vecs()), (shift_m, *_vecs()), _dep(gather_token)],
        outs=[F32, whole_rows(D)(tm2), ((L, DT_PAD), F32, (tm2, DT_PAD), lambda i, j, k: (i, 0))],
        prologue=norm1_pro, epilogue=proj_ep)
    xbc_raw = proj
    y_ssm, hs, xbc = _ssd_fwd(xbc_raw, dtp, conv_full, conv_b, dtb, arow, dsk_x, name="ssd_fwd")

    slab_rest, gs = _xchg_wait(rest_started, y_ssm, per_peer=False, name="gather_rest_wait")
    gs = lax.dynamic_update_slice(gs, slab_rest[None], (me, 0, 0))

    def part(n, rows):
        return gs[:, REST_OFF[n]:REST_OFF[n] + rows]

    w_bssd = part("bssd", 256).reshape(DI, D)
    w_pool = part("pool", 32).reshape(N_DEV, 4, 32, PGW).transpose(1, 0, 2, 3).reshape(POOL_W, PGW)
    w_bpool = part("bpool", 128).reshape(POOL_W, D)
    w_o = part("out", 128).reshape(D, D)
    w_up_t = part("up", 512).reshape(DFF, D)
    w_dn = part("down", 512).reshape(DFF, D)

    def gnorm_pro(y_ref, ex, outs, j):
        z_ref, w_ref = ex
        yg = y_ref[...].astype(F32) * _silu(z_ref[...].astype(F32))
        segs = []
        for k in range(NG):
            sl = slice(k * GW, (k + 1) * GW)
            seg = yg[:, sl]
            r = lax.rsqrt(jnp.mean(seg * seg, axis=-1, keepdims=True) + EPS)
            segs.append((seg * r * w_ref[:, sl]).astype(BF16))
        yn_v = jnp.concatenate(segs, axis=1)
        outs[1][...] = yn_v
        return yn_v

    y_ssd, yn = _mm(y_ssm, w_bssd, "nn", name="branch_ssd", tm=tmh, tn=D, tk=DI,
                    extras=[(proj, *_rows(tmh, DI, zcol)), (ssd_norm_w, *_vecs(DI))],
                    outs=[F32, whole_rows(DI)(tmh)], prologue=gnorm_pro)
    pooled = _pool_fwd(proj, name="pool_fwd")
    wp_spec = ((POOL_W, PGW), lambda i, j, k: (0, 0))

    def pool_pro(a_ref, ex, outs, j):
        wp_ref, s_ref = ex
        segs = []
        for g in range(4):
            sl = slice(g * PGW, (g + 1) * PGW)
            p = _dot(a_ref[:, sl], wp_ref[sl, :], NN)
            outs[1][:, sl] = p.astype(BF16)
            segs.append((p * s_ref[:, sl]).astype(BF16))
        yp1_v = jnp.concatenate(segs, axis=1)
        outs[2][...] = yp1_v
        return yp1_v

    y_pool, yp0, yp1 = _mm(pooled, w_bpool, "nn", name="branch_pool", tm=tm, tn=D, tk=D,
                           extras=[(w_pool, *wp_spec), (pool_scale, *_vecs())],
                           outs=[F32, whole_rows(D)(tm), whole_rows(D)(tm)], prologue=pool_pro)

    def merge_pro(a_ref, ex, outs, j):
        s = _sigmoid(ex[1][...].astype(F32))
        mv = (s[:, :D] * a_ref[...] + s[:, D:] * ex[0][...]).astype(BF16)
        outs[3][...] = mv
        return mv

    mix, x1, h2, m = _mm(y_ssd, w_o, "nn", name="out_proj", tm=tmh, tn=D, tk=D,
                         extras=[(y_pool, *_rows(tmh)), (proj, *_rows(tmh, 2 * D, gcol)),
                                 (xs_, *_rows(tmh)), (gate_m, *_vecs()), (norm_mlp_w, *_vecs()),
                                 (scale_f, *_vecs()), (shift_f, *_vecs())],
                         outs=[BF16, F32, BF16, whole_rows(D)(tmh)], prologue=merge_pro,
                         epilogue=lambda acc, ex, outs: _ep_resid_norm(acc, ex[2:], outs[:3]))

    def relu2(acc, ex, outs):
        r = jnp.maximum(acc, 0.0)
        outs[0][...] = acc.astype(BF16)
        outs[1][...] = (r * r).astype(BF16)

    up, act = _mm(h2, w_up_t, "nt", name="mlp_up", outs=[BF16, BF16], tm=tm2, tn=1024, tk=D, epilogue=relu2)

    dx2, ddown, loss_p, dnwf, dgate_f = _mm(
        act, w_dn, "nn", name="mlp_down", tm=tmh, tn=D, tk=DFF,
        extras=[(x1, *_rows(tmh)), (tgt, *_rows(tmh)), (gate_f, *_vecs()), (norm_final_w.reshape(1, D), *_vecs())],
        outs=[F32, BF16, _sum_out(128), _sum_out(), _sum_out()], epilogue=_ep_final)

    def drelu2(acc, ex, outs):
        outs[0][...] = (acc * (2.0 * jnp.maximum(ex[0][...].astype(F32), 0.0))).astype(BF16)

    def dep_last(ep):
        return lambda acc, ex, outs: ep(acc, ex[:-1], outs)

    dup = _mm(ddown, w_dn, "nt", name="mlp_down_dx", outs=[BF16], tm=tm2, tn=1024, tk=D,
              extras=[(up, (tm2, 1024), lambda i, j, k: (i, j))], epilogue=drelu2)
    g_dn = _mm(act, ddown, "tn", name="mlp_down_dw", outs=[SLAB_DT], tm=1024, tn=D, tk=tkl)
    g_up_t = _mm(dup, h2, "tn", name="mlp_up_dw", outs=[SLAB_DT], tm=1024, tn=D, tk=tkl)
    gslab_mlp = jnp.concatenate([g_up_t.reshape(N_DEV, 512, D), g_dn.reshape(N_DEV, 512, D)], axis=1)
    mlp_started = _xchg_start(gslab_mlp, per_peer=True, name="scatter_mlp_start")
    dx1, p2, q2, dmix, dgate_m = _mm(
        dup, w_up_t, "nn", name="mlp_up_dx", tm=tmh, tn=D, tk=DFF,
        extras=[(x1, *_rows(tmh)), (dx2, *_rows(tmh)), (norm_mlp_w, *_vecs()), (scale_f, *_vecs()),
                (mix, *_rows(tmh)), (gate_m, *_vecs()), _dep(mlp_started[4])],
        outs=[F32, _sum_out(), _sum_out(), BF16, _sum_out()], epilogue=dep_last(_ep_norm_bwd))
    gcol = C_GATE // (2 * D)
    dy_ssd, dy_pool, dproj = _mm(
        dmix, w_o, "nt", name="out_proj_dx", tm=tmh, tn=D, tk=D,
        extras=[(y_ssd, *_rows(tmh)), (y_pool, *_rows(tmh)), (proj, *_rows(tmh, 2 * D, gcol))],
        outs=[BF16, BF16, ((L, NPROJ), BF16, *_rows(tmh, 2 * D, gcol))], epilogue=_ep_merge_bwd)
    g_o = _mm(m, dmix, "tn", name="out_proj_dw", outs=[SLAB_DT], tm=D, tn=D, tk=tkl)
    zcol = C_Z // DI
    dy_ssm, dproj, d_snw = _mm(
        dy_ssd, w_bssd, "nt", name="branch_ssd_dx", tm=tmh, tn=DI, tk=D,
        extras=[(y_ssm, *_rows(tmh, DI)), (proj, *_rows(tmh, DI, zcol)), (ssd_norm_w, *_vecs(DI)),
                (dproj, None, None)],
        outs=[F32, ((L, NPROJ), BF16, *_rows(tmh, DI, zcol)), _sum_out(DI)],
        epilogue=_ep_gated_norm_bwd, aliases={3: 1})
    g_bssd = _mm(yn, dy_ssd, "tn", name="branch_ssd_dw", outs=[SLAB_DT], tm=1024, tn=D, tk=tkl)
    dxbc, dproj, d_a, d_dx, d_dtb = _ssd_bwd(dy_ssm, xbc, dtp, hs, dtb, arow, dsk_x, dproj, name="ssd_bwd")
    dproj, d_cw, d_cb = _conv_bwd(xbc_raw, dxbc, conv_full, conv_b, dproj, name="conv_bwd")
    def pool_bwd_ep(acc, ex, outs):
        y_ref, s_ref, wp_ref = ex
        o_ref, ds_ref, dpool_ref = outs
        dyp0_v = (acc * s_ref[...]).astype(BF16)
        o_ref[...] = dyp0_v
        _acc_out(ds_ref, _colsum(acc * y_ref[...].astype(F32)), _row_step())
        for g in range(4):
            sl = slice(g * PGW, (g + 1) * PGW)
            dpool_ref[:, sl] = _dot(dyp0_v[:, sl], wp_ref[sl, :], NT)

    dyp0, d_ps, dpooled = _mm(dy_pool, w_bpool, "nt", name="branch_pool_dx", tm=tm, tn=D, tk=D,
                              extras=[(yp0, *_rows(tm)), (pool_scale, *_vecs()), (w_pool, *wp_spec)],
                              outs=[BF16, _sum_out(), F32], epilogue=pool_bwd_ep)
    g_bpool = _mm(yp1, dy_pool, "tn", name="branch_pool_dw", outs=[SLAB_DT], tm=D, tn=D, tk=tkl)
    g_pool = _mm_pool_tn(pooled, dyp0, name="pool_mix_dw", tk=tkl)
    gslab_mix = jnp.concatenate([
        g_bssd.reshape(N_DEV, 256, D),
        g_pool.reshape(4, N_DEV, 32, PGW).transpose(1, 0, 2, 3).reshape(N_DEV, 32, D).astype(SLAB_DT),
        g_bpool.reshape(N_DEV, 128, D),
        g_o.reshape(N_DEV, 128, D)], axis=1)
    mix_started = _xchg_start(gslab_mix, per_peer=True, name="scatter_mix_start")
    dproj = _pool_bwd(dpooled, dproj, name="pool_bwd")
    g_in_t = _mm(dproj, h1, "tn", name="in_proj_dw", outs=[SLAB_DT], tm=1408, tn=D, tk=tkl2,
                 extras=[_dep(mix_started[4])])
    gslab_in = _restore_in_shards(g_in_t)
    in_started = _xchg_start(gslab_in, per_peer=True, name="scatter_in_start")
    grad_x, p1, q1 = _mm(
        dproj, w_in_t, "nn", name="in_proj_dx", tm=tmh, tn=D, tk=2816,
        extras=[(xs_, *_rows(tmh)), (dx1, *_rows(tmh)), (norm_mix_w, *_vecs()), (scale_m, *_vecs()),
                _dep(in_started[4])],
        outs=[F32, _sum_out(), _sum_out()], epilogue=dep_last(_ep_norm_bwd))

    def landed(started, after, tile, name):
        src, land = _xchg_wait(started, after, per_peer=True, name=name + "_wait")
        own = lax.dynamic_slice_in_dim(src, me, 1, axis=0)
        return _slab_sum(lax.dynamic_update_slice(land, own, (me, 0, 0)), tile=tile, name=name + "_sum")

    gsum_mlp = landed(mlp_started, grad_x, 256, "scatter_mlp")
    gsum_mix = landed(mix_started, grad_x, 272, "scatter_mix")
    gsum_in = landed(in_started, grad_x, 208, "scatter_in")

    dmod = jnp.concatenate([q1, p1 * norm_mix_w, dgate_m, q2, p2 * norm_mlp_w, dgate_f], axis=1)
    d_alog = d_a[:, :NH] * (-jnp.exp(a_log))
    sv = _pack_sv({
        "b_ada": dmod, "norm_mix_w": p1 * (1.0 + scale_m), "conv_b": d_cb, "dt_bias": d_dtb[:, :NH],
        "a_log": d_alog, "d_skip": d_dx.reshape(NH, HP).sum(axis=1), "ssd_norm_w": d_snw,
        "pool_scale": d_ps, "norm_mlp_w": p2 * (1.0 + scale_f), "norm_final_w": dnwf, "conv_w": d_cw,
        "loss": loss_p[:, :1]})
    sv_all, sv_sum = _small_allsum(sv, name="small_allsum")
    flat = sv_sum.reshape(-1)
    loss = flat[SV_OFF["loss"]]
    dmod_all = sv_all.reshape(N_DEV, SV_ROWS * 128)[:, :6 * D]
    g_w_ada = _ada_bwd(c_all, lax.dynamic_slice_in_dim(dmod_all, me * wloc, wloc, axis=1), name="ada_bwd")

    g_conv_w = lax.dynamic_slice_in_dim(_sv_get(flat, "conv_w", 4 * XBC).reshape(4, XBC),
                                        me * (XBC // N_DEV), XBC // N_DEV, axis=1)
    small = [("b_ada", b_ada, m_b_ada, v_b_ada), ("norm_mix_w", norm_mix_w, m_norm_mix_w, v_norm_mix_w),
             ("conv_b", conv_b, m_conv_b, v_conv_b), ("dt_bias", dt_bias, m_dt_bias, v_dt_bias),
             ("a_log", a_log, m_a_log, v_a_log), ("d_skip", d_skip, m_d_skip, v_d_skip),
             ("ssd_norm_w", ssd_norm_w, m_ssd_norm_w, v_ssd_norm_w),
             ("pool_scale", pool_scale, m_pool_scale, v_pool_scale),
             ("norm_mlp_w", norm_mlp_w, m_norm_mlp_w, v_norm_mlp_w),
             ("norm_final_w", norm_final_w[None], m_norm_final_w[None], v_norm_final_w[None]),
             ("conv_w", conv_w[0], m_conv_w[0], v_conv_w[0])]
    small_out = _adamw_small(sv_sum.reshape(1, SV_ROWS * 128), g_conv_w, small, name="adamw_small")
    small_out["norm_final_w"] = tuple(a[0] for a in small_out["norm_final_w"])
    small_out["conv_w"] = tuple(a[None] for a in small_out["conv_w"])

    def gpart(n, rows_):
        return gsum_mix[MIX_OFF[n]:MIX_OFF[n] + rows_]

    def lin(a):
        return a[0].T.reshape(IN_ROWS * 8, 128)

    g_lin = lax.dynamic_slice_in_dim(gsum_in, in_shift, IN_ROWS, axis=0).reshape(IN_ROWS * 8, 128)
    dlt, mn, vn = _adamw(lin(w_in), g_lin, lin(m_w_in), lin(v_w_in), name="adamw_w_in", tr=IN_ROWS * 2)
    big_in = tuple(a.reshape(IN_ROWS, D).T[None] for a in (g_lin, dlt, mn, vn))

    big = {
        "w_ada": (w_ada, m_w_ada, v_w_ada, g_w_ada, (D, wloc)),
        "w_branch_ssd": (w_branch_ssd, m_w_branch_ssd, v_w_branch_ssd, gpart("bssd", 256), (256, D)),
        "pool_w": (pool_w, m_pool_w, v_pool_w, gpart("pool", 32).reshape(128, PGW), (128, PGW)),
        "w_branch_pool": (w_branch_pool, m_w_branch_pool, v_w_branch_pool, gpart("bpool", 128), (128, D)),
        "w_out": (w_out, m_w_out, v_w_out, gpart("out", 128), (128, D)),
        "w_up": (w_up, m_w_up, v_w_up, gsum_mlp[:512].T, (D, 512)),
        "w_down": (w_down, m_w_down, v_w_down, gsum_mlp[512:], (512, D)),
    }
    big_out = {}
    for n, (w, mm_, vv, g, shp2) in big.items():
        dlt, mn, vn = _adamw(w.reshape(shp2), g, mm_.reshape(shp2), vv.reshape(shp2), name="adamw_" + n)
        big_out[n] = (g.reshape(w.shape), dlt.reshape(w.shape), mn.reshape(w.shape), vn.reshape(w.shape))

    order = ["w_ada", "b_ada", "norm_mix_w", "w_in", "conv_w", "conv_b", "dt_bias", "a_log", "d_skip",
             "ssd_norm_w", "w_branch_ssd", "pool_w", "pool_scale", "w_branch_pool", "w_out", "norm_mlp_w",
             "w_up", "w_down", "norm_final_w"]
    big_out["w_in"] = big_in
    res = {**small_out, **big_out}
    outs = [loss, grad_x.reshape(x.shape)]
    for k in range(4):
        outs += [res[n][k] for n in order]
    return tuple(outs)
```

```python
import functools

import numpy as np
import jax
import jax.numpy as jnp
from jax import lax
from jax.experimental import pallas as pl
from jax.experimental.pallas import tpu as pltpu

F32 = jnp.float32
BF16 = jnp.bfloat16
SLAB_DT = jnp.bfloat16
_MXU_DTYPE = jnp.bfloat16

N_DEV = 8
D = 1024
DI = 2048
NH = 32
HP = 64
NG = 4
NS = 128
Q = 128
XBC = DI + 2 * NG * NS
DFF = 4096
N_IN = 8224
EPS = 1e-5
POOL_W = 1024
PGW = 256

C_XBC, C_POOL, C_Z, C_GATE, C_DT = 0, 3072, 4096, 6144, 8192
DT_PAD = 256
NPROJ = C_DT + DT_PAD

IN_ROWS = N_IN // N_DEV
IN_ROWS_P = 1040
CONV_ROWS = 16
REST_PARTS = (("bssd", 256), ("pool", 32), ("bpool", 128), ("out", 128), ("up", 512), ("down", 512))
REST_OFF = {}
_o = 0
for _n, _r in REST_PARTS:
    REST_OFF[_n] = _o
    _o += _r
REST_ROWS = _o
MIX_PARTS = (("bssd", 256), ("pool", 32), ("bpool", 128), ("out", 128))
MIX_OFF = {}
_o = 0
for _n, _r in MIX_PARTS:
    MIX_OFF[_n] = _o
    _o += _r
MIX_ROWS = _o

SV_PARTS = (("b_ada", 6144), ("norm_mix_w", 1024), ("conv_b", 3072), ("dt_bias", 128), ("a_log", 128),
            ("d_skip", 128), ("ssd_norm_w", 2048), ("pool_scale", 1024), ("norm_mlp_w", 1024),
            ("norm_final_w", 1024), ("conv_w", 4 * XBC), ("loss", 128))
SV_OFF = {}
_o = 0
for _n, _r in SV_PARTS:
    SV_OFF[_n] = _o
    _o += _r
SV_ROWS = 224
assert _o <= SV_ROWS * 128

ADAM_LR, ADAM_B1, ADAM_B2, ADAM_EPS, ADAM_WD, ADAM_STEP = 0.001, 0.9, 0.999, 1e-08, 0.01, 10

VMEM_BIG = 56 * 1024 * 1024
NEG = -1e30

NN = ((1,), (0,))
NT = ((1,), (1,))
TN = ((0,), (0,))


def _dot(a, b, dims=NN):
    return lax.dot_general(a.astype(_MXU_DTYPE), b.astype(_MXU_DTYPE), (dims, ((), ())),
                           preferred_element_type=F32)


def _dot_hi(a, b, dims=NN):
    return lax.dot_general(a.astype(F32), b.astype(F32), (dims, ((), ())),
                           precision=lax.Precision.HIGHEST, preferred_element_type=F32)


def _pick(n, cands):
    for c in cands:
        if n % c == 0:
            return c
    return n


def _sigmoid(x):
    return 1.0 / (1.0 + jnp.exp(-x))


def _silu(x):
    return x * _sigmoid(x)


def _dsilu(x):
    s = _sigmoid(x)
    return s * (1.0 + x * (1.0 - s))


def _softplus(x):
    return jnp.maximum(x, 0.0) + jnp.log(1.0 + jnp.exp(-jnp.abs(x)))


def _params(sem, vmem=None):
    return pltpu.CompilerParams(dimension_semantics=sem, vmem_limit_bytes=vmem)


def _row_step():
    return pl.program_id(0)


def _mm(a, b, mode, *, name, outs, tm, tn, tk, extras=(), epilogue=None, aliases=None, prologue=None):
    if mode == "tn":
        K, M = a.shape
        N = b.shape[1]
        a_spec = pl.BlockSpec((tk, tm), lambda i, j, k: (k, i))
        b_spec = pl.BlockSpec((tk, tn), lambda i, j, k: (k, j))
        dims = TN
    else:
        M = a.shape[0]
        K = b.shape[0] if mode == "nn" else b.shape[1]
        if prologue is None:
            assert a.shape[1] == K
            a_spec = pl.BlockSpec((tm, tk), lambda i, j, k: (i, k))
        else:
            assert tk == K
            a_spec = pl.BlockSpec((tm, a.shape[1]), lambda i, j, k: (i, 0))
        if mode == "nn":
            N = b.shape[1]
            b_spec = pl.BlockSpec((tk, tn), lambda i, j, k: (k, j))
            dims = NN
        else:
            N = b.shape[0]
            b_spec = pl.BlockSpec((tn, tk), lambda i, j, k: (j, k))
            dims = NT
    assert M % tm == 0 and N % tn == 0 and K % tk == 0, (name, M, N, K, tm, tn, tk)
    nk = K // tk
    ne, no = len(extras), len(outs)
    if epilogue is None:
        def epilogue(acc, ex, out_refs):
            out_refs[0][...] = acc.astype(out_refs[0].dtype)

    def body(a_ref, b_ref, *rest):
        ex, out_refs = rest[:ne], rest[ne:ne + no]
        lhs = a_ref[...] if prologue is None else prologue(a_ref, ex, out_refs, pl.program_id(1))
        p = _dot(lhs, b_ref[...], dims)
        if nk == 1:
            epilogue(p, ex, out_refs)
        else:
            acc = rest[-1]
            k = pl.program_id(2)

            @pl.when(k == 0)
            def _():
                acc[...] = p

            @pl.when(jnp.logical_and(k > 0, k < nk - 1))
            def _():
                acc[...] += p

            @pl.when(k == nk - 1)
            def _():
                epilogue(acc[...] + p, ex, out_refs)

    out_specs, out_shape = [], []
    for o in outs:
        if isinstance(o, tuple):
            shape, dt, bs, im = o
            out_specs.append(pl.BlockSpec(bs, im))
            out_shape.append(jax.ShapeDtypeStruct(shape, dt))
        else:
            out_specs.append(pl.BlockSpec((tm, tn), lambda i, j, k: (i, j)))
            out_shape.append(jax.ShapeDtypeStruct((M, N), o))
    in_specs = [a_spec, b_spec]
    for _, bs, im in extras:
        in_specs.append(pl.BlockSpec(memory_space=pl.ANY) if bs is None else pl.BlockSpec(bs, im))
    res = pl.pallas_call(
        body, name=name,
        grid=(M // tm, N // tn, nk),
        in_specs=in_specs, out_specs=out_specs, out_shape=out_shape,
        scratch_shapes=[pltpu.VMEM((tm, tn), F32)] if nk > 1 else [],
        input_output_aliases={2 + e: o for e, o in (aliases or {}).items()},
        compiler_params=_params(("arbitrary", "arbitrary", "arbitrary"), VMEM_BIG),
    )(a, b, *[e[0] for e in extras])
    return res if no > 1 else res[0]


def _rows(tm, w=D, col=0):
    return (tm, w), lambda i, j, k, c=col: (i, c)


def _vecs(w=D, col=0):
    return (1, w), lambda i, j, k, c=col: (0, c)


def _sum_out(w=D):
    return ((1, w), F32, (1, w), lambda i, j, k: (0, 0))


def _mm_pool_tn(a, b, *, name, tk):
    L = a.shape[0]

    def body(a_ref, b_ref, o_ref):
        p = _dot(a_ref[...], b_ref[...], TN)

        @pl.when(pl.program_id(1) == 0)
        def _():
            o_ref[...] = p

        @pl.when(pl.program_id(1) > 0)
        def _():
            o_ref[...] += p

    blk = pl.BlockSpec((tk, PGW), lambda g, k: (k, g))
    return pl.pallas_call(body, name=name, grid=(4, L // tk), in_specs=[blk, blk],
                          out_specs=pl.BlockSpec((PGW, PGW), lambda g, k: (g, 0)),
                          out_shape=jax.ShapeDtypeStruct((POOL_W, PGW), F32),
                          compiler_params=_params(("parallel", "arbitrary")))(a, b)


def _acc_out(ref, val, i):
    @pl.when(i == 0)
    def _():
        ref[...] = val

    @pl.when(i > 0)
    def _():
        ref[...] += val


def _colsum(v):
    return jnp.sum(v, axis=0, keepdims=True)


def _ep_resid_norm(acc, ex, outs):
    x_ref, g_ref, nw_ref, sc_ref, sh_ref = ex
    mix_ref, x1_ref, h_ref = outs
    mix_ref[...] = acc.astype(mix_ref.dtype)
    xv = x_ref[...] + g_ref[...] * acc
    x1_ref[...] = xv
    r = lax.rsqrt(jnp.mean(xv * xv, axis=-1, keepdims=True) + EPS)
    h_ref[...] = (xv * r * nw_ref[...] * (1.0 + sc_ref[...]) + sh_ref[...]).astype(h_ref.dtype)


def _ep_final(acc, ex, outs):
    x1_ref, t_ref, g_ref, nw_ref = ex
    dx2_ref, dd_ref, loss_ref, dnw_ref, dg_ref = outs
    i = _row_step()
    x2 = x1_ref[...] + g_ref[...] * acc
    r = lax.rsqrt(jnp.mean(x2 * x2, axis=-1, keepdims=True) + EPS)
    xh = x2 * r
    e = xh * nw_ref[...] - t_ref[...]
    part = 0.5 * jnp.sum(jnp.mean(e * e, axis=-1, keepdims=True), axis=0, keepdims=True)
    dy = e * (1.0 / D)
    g = dy * nw_ref[...]
    dx2 = r * (g - xh * jnp.mean(g * xh, axis=-1, keepdims=True))
    dx2_ref[...] = dx2
    dd_ref[...] = (dx2 * g_ref[...]).astype(dd_ref.dtype)
    _acc_out(loss_ref, jnp.broadcast_to(part, (1, 128)), i)
    _acc_out(dnw_ref, _colsum(dy * xh), i)
    _acc_out(dg_ref, _colsum(dx2 * acc), i)


def _ep_norm_bwd(acc, ex, outs):
    x_ref, dr_ref, nw_ref, sc_ref = ex[:4]
    dx_ref, p_ref, q_ref = outs[:3]
    i = _row_step()
    xv = x_ref[...]
    r = lax.rsqrt(jnp.mean(xv * xv, axis=-1, keepdims=True) + EPS)
    xh = xv * r
    g = acc * (nw_ref[...] * (1.0 + sc_ref[...]))
    dx = dr_ref[...] + r * (g - xh * jnp.mean(g * xh, axis=-1, keepdims=True))
    dx_ref[...] = dx
    _acc_out(p_ref, _colsum(acc * xh), i)
    _acc_out(q_ref, _colsum(acc), i)
    if len(ex) > 4:
        m_ref, g_ref = ex[4:]
        dm_ref, dg_ref = outs[3:]
        dm_ref[...] = (dx * g_ref[...]).astype(dm_ref.dtype)
        _acc_out(dg_ref, _colsum(dx * m_ref[...].astype(F32)), i)


def _ep_merge_bwd(acc, ex, outs):
    a_ref, b_ref, gl_ref = ex
    da_ref, db_ref, dgl_ref = outs
    s = _sigmoid(gl_ref[...].astype(F32))
    s1, s2 = s[:, :D], s[:, D:]
    da_ref[...] = (acc * s1).astype(da_ref.dtype)
    db_ref[...] = (acc * s2).astype(db_ref.dtype)
    dgl_ref[:, :D] = (acc * a_ref[...] * s1 * (1.0 - s1)).astype(dgl_ref.dtype)
    dgl_ref[:, D:] = (acc * b_ref[...] * s2 * (1.0 - s2)).astype(dgl_ref.dtype)


GW = DI // NG


def _ep_gated_norm_bwd(acc, ex, outs):
    y_ref, z_ref, w_ref, _ = ex
    dy_ref, dz_ref, dw_ref = outs
    zv = z_ref[...].astype(F32)
    yv = y_ref[...].astype(F32)
    sg = _sigmoid(zv)
    sz = zv * sg
    yg = yv * sz
    dsz = sg * (1.0 + zv * (1.0 - sg))
    dws = []
    for k in range(NG):
        sl = slice(k * GW, (k + 1) * GW)
        seg = yg[:, sl]
        r = lax.rsqrt(jnp.mean(seg * seg, axis=-1, keepdims=True) + EPS)
        sh = seg * r
        dn = acc[:, sl]
        g = dn * w_ref[:, sl]
        dyg = r * (g - sh * jnp.mean(g * sh, axis=-1, keepdims=True))
        dy_ref[:, sl] = dyg * sz[:, sl]
        dz_ref[:, sl] = (dyg * yv[:, sl] * dsz[:, sl]).astype(dz_ref.dtype)
        dws.append(_colsum(dn * sh))
    _acc_out(dw_ref, jnp.concatenate(dws, axis=1), _row_step())


CONV_CB = 128
HALO = 16


def _time_chunk(L):
    return _pick(L, (256, 128))


def _with_halo(x_ref, i, r0, rc):
    p0 = pl.multiple_of(jnp.maximum(r0 - HALO, 0), HALO)
    prev = jnp.where(i > 0, x_ref[pl.ds(p0, HALO), :].astype(F32), 0.0)
    return jnp.concatenate([prev, x_ref[pl.ds(r0, rc), :].astype(F32)], axis=0)


def _conv_bwd(proj, dy, w, b, dproj, *, name):
    L = proj.shape[0]
    rc = _time_chunk(L)
    n = L // rc

    def body(x_ref, dy_ref, w_ref, b_ref, dp_in, dx_ref, dw_ref, db_ref, xpad, dpad):
        del dp_in
        wv = w_ref[...]
        bv = b_ref[...]
        dpad[rc:rc + HALO, :] = jnp.zeros((HALO, CONV_CB), F32)

        def step(k, carry):
            db, d0, d1, d2, d3 = carry
            i = n - 1 - k
            r0 = pl.multiple_of(i * rc, rc)
            p0 = pl.multiple_of(jnp.maximum(r0 - HALO, 0), HALO)
            xpad[0:HALO, :] = jnp.where(i > 0, x_ref[pl.ds(p0, HALO), :].astype(F32), 0.0)
            xpad[HALO:HALO + rc, :] = x_ref[pl.ds(r0, rc), :].astype(F32)
            xk = [xpad[HALO - j:HALO - j + rc, :] for j in range(4)]
            pre = bv
            for j in range(4):
                pre = pre + xk[j] * wv[3 - j:4 - j]
            dpre = dy_ref[pl.ds(r0, rc), :] * _dsilu(pre)
            dpad[0:rc, :] = dpre
            acc = dpre * wv[3:4]
            for j in (1, 2, 3):
                acc = acc + dpad[j:j + rc, :] * wv[3 - j:4 - j]
            dx_ref[pl.ds(r0, rc), :] = acc.astype(dx_ref.dtype)
            dpad[rc:rc + HALO, :] = dpre[:HALO]
            return (db + _colsum(dpre), d0 + _colsum(dpre * xk[3]), d1 + _colsum(dpre * xk[2]),
                    d2 + _colsum(dpre * xk[1]), d3 + _colsum(dpre * xk[0]))

        z = jnp.zeros((1, CONV_CB), F32)
        db, d0, d1, d2, d3 = lax.fori_loop(0, n, step, (z, z, z, z, z))
        db_ref[...] = db
        dw_ref[...] = jnp.concatenate([d0, d1, d2, d3], axis=0)

    nb = XBC // CONV_CB
    return pl.pallas_call(
        body, name=name, grid=(nb,),
        in_specs=[pl.BlockSpec((L, CONV_CB), lambda j: (0, j + C_XBC // CONV_CB)),
                  pl.BlockSpec((L, CONV_CB), lambda j: (0, j)),
                  pl.BlockSpec((4, CONV_CB), lambda j: (0, j)), pl.BlockSpec((1, CONV_CB), lambda j: (0, j)),
                  pl.BlockSpec(memory_space=pl.ANY)],
        out_specs=[pl.BlockSpec((L, CONV_CB), lambda j: (0, j + C_XBC // CONV_CB)),
                   pl.BlockSpec((4, CONV_CB), lambda j: (0, j)), pl.BlockSpec((1, CONV_CB), lambda j: (0, j))],
        out_shape=[jax.ShapeDtypeStruct((L, NPROJ), BF16), jax.ShapeDtypeStruct((4, XBC), F32),
                   jax.ShapeDtypeStruct((1, XBC), F32)],
        scratch_shapes=[pltpu.VMEM((rc + HALO, CONV_CB), F32), pltpu.VMEM((rc + HALO, CONV_CB), F32)],
        input_output_aliases={4: 0},
        compiler_params=_params(("parallel",), VMEM_BIG))(proj, dy, w, b, dproj)


def _pool_fwd(proj, *, name):
    L = proj.shape[0]
    rc = _time_chunk(L)
    n = L // rc

    def body(x_ref, o_ref, pad):
        g = pl.program_id(0)
        pad[0:HALO, :] = jnp.zeros((HALO, PGW), F32)

        def fill(i, c):
            r0 = pl.multiple_of(i * rc, rc)
            pad[pl.ds(r0 + HALO, rc), :] = x_ref[pl.ds(r0, rc), :].astype(F32)
            return c

        lax.fori_loop(0, n, fill, 0)
        rows = lax.broadcasted_iota(jnp.int32, (rc, PGW), 0)

        for gi in range(4):
            win = 2 << gi

            @pl.when(g == gi)
            def _(gi=gi, win=win):
                def step(i, c):
                    r0 = pl.multiple_of(i * rc, rc)
                    ext = pad[pl.ds(r0, rc + HALO), :]
                    s = ext
                    sh = 1
                    while sh < win:
                        s = s + pltpu.roll(s, sh, 0)
                        sh *= 2
                    cnt = jnp.minimum(rows + (r0 + 1), win).astype(F32)
                    o_ref[pl.ds(r0, rc), :] = (s[HALO:] / cnt - ext[HALO:]).astype(o_ref.dtype)
                    return c

                lax.fori_loop(0, n, step, 0)

    return pl.pallas_call(
        body, name=name, grid=(4,),
        in_specs=[pl.BlockSpec((L, PGW), lambda j: (0, j + C_POOL // PGW))],
        out_specs=pl.BlockSpec((L, PGW), lambda j: (0, j)),
        out_shape=jax.ShapeDtypeStruct((L, POOL_W), BF16),
        scratch_shapes=[pltpu.VMEM((L + HALO, PGW), F32)],
        compiler_params=_params(("parallel",), VMEM_BIG))(proj)


def _pool_bwd(dpooled, dproj, *, name):
    L = dpooled.shape[0]
    rc = _time_chunk(L)
    n = L // rc

    def body(d_ref, dp_in, o_ref, pad):
        del dp_in
        g = pl.program_id(0)
        pad[L:L + HALO, :] = jnp.zeros((HALO, PGW), F32)
        rows = lax.broadcasted_iota(jnp.int32, (rc, PGW), 0)

        for gi in range(4):
            win = 2 << gi

            @pl.when(g == gi)
            def _(gi=gi, win=win):
                def fill(i, c):
                    r0 = pl.multiple_of(i * rc, rc)
                    cnt = jnp.minimum(rows + (r0 + 1), win).astype(F32)
                    pad[pl.ds(r0, rc), :] = d_ref[pl.ds(r0, rc), :] / cnt
                    return c

                lax.fori_loop(0, n, fill, 0)

                def step(i, c):
                    r0 = pl.multiple_of(i * rc, rc)
                    s = pad[pl.ds(r0, rc + HALO), :]
                    sh = 1
                    while sh < win:
                        s = s + pltpu.roll(s, rc + HALO - sh, 0)
                        sh *= 2
                    o_ref[pl.ds(r0, rc), :] = (s[:rc] - d_ref[pl.ds(r0, rc), :]).astype(o_ref.dtype)
                    return c

                lax.fori_loop(0, n, step, 0)

    return pl.pallas_call(
        body, name=name, grid=(4,),
        in_specs=[pl.BlockSpec((L, PGW), lambda j: (0, j)), pl.BlockSpec(memory_space=pl.ANY)],
        out_specs=pl.BlockSpec((L, PGW), lambda j: (0, j + C_POOL // PGW)),
        out_shape=jax.ShapeDtypeStruct((L, NPROJ), BF16),
        scratch_shapes=[pltpu.VMEM((L + HALO, PGW), F32)],
        input_output_aliases={1: 0},
        compiler_params=_params(("parallel",), VMEM_BIG))(dpooled, dproj)


_SPLIT_DT = jnp.bfloat16


def _ssd_consts():
    tri = np.tril(np.ones((Q, Q), np.float32))
    exp = np.zeros((128, DI), np.float32)
    for h in range(NH):
        exp[h, h * HP:(h + 1) * HP] = 1.0
    exp2 = np.concatenate([exp, exp], axis=0)
    return (jnp.asarray(tri, dtype=_SPLIT_DT), jnp.asarray(tri.T.copy(), dtype=_SPLIT_DT),
            jnp.asarray(exp2, dtype=_SPLIT_DT))


def _split(v, n):
    parts, r = [], v
    for _ in range(n):
        p = r.astype(_SPLIT_DT)
        parts.append(p)
        r = r - p.astype(F32)
    return parts


def _bdot(a, b, dims):
    return lax.dot_general(a, b, (dims, ((), ())), preferred_element_type=F32)


def _tri_sum(t_ref, v):
    r = _bdot(t_ref[...], jnp.concatenate(_split(v, 3), axis=1), NN)
    return r[:, :128] + r[:, 128:256] + r[:, 256:]


def _expand(v, e2_ref):
    return _bdot(jnp.concatenate(_split(v, 2), axis=1), e2_ref[...], NN)


def _reduce_heads(vals, eg):
    parts = []
    for v in vals:
        parts += _split(v, 2)
    r = _bdot(jnp.concatenate(parts, axis=0), eg, NT)
    return [r[2 * i * Q:(2 * i + 1) * Q] + r[(2 * i + 1) * Q:(2 * i + 2) * Q] for i in range(len(vals))]


def _ssd_common(xbc_ref, dtw_ref, dtb_ref, arow_ref, t_ref, e_ref):
    pre = dtw_ref[:, :128] + dtb_ref[...]
    dt = _softplus(pre)
    acs = _tri_sum(t_ref, dt * arow_ref[...])
    acs_x = _expand(acs, e_ref)
    dt_x = _expand(dt, e_ref)
    xs = xbc_ref[:, 0:DI]
    return pre, dt, acs, acs.T, acs_x, dt_x, xs


CONV_SLAB = 512


def _ssd_fwd(raw, dtp, cw, cb, dtb, arow, dsk_x, *, name):
    L = raw.shape[0]
    nc = L // Q
    tri, _, expand = _ssd_consts()

    def body(raw_ref, halo_ref, cw_ref, cb_ref, dtw_ref, dtb_ref, arow_ref, dsk_ref, t_ref, e_ref,
             y_ref, hs_ref, xbc_ref, h_scr, cpad):
        c = pl.program_id(0)

        @pl.when(c == 0)
        def _():
            h_scr[...] = jnp.zeros_like(h_scr)

        cpad[0:8, :] = jnp.where(c > 0, halo_ref[...], 0.0)
        cpad[8:8 + Q, :] = raw_ref[...]
        for lo in range(0, XBC, CONV_SLAB):
            sl = slice(lo, lo + CONV_SLAB)
            acc = cb_ref[:, sl]
            for j in range(4):
                acc = acc + cpad[8 - j:8 - j + Q, sl] * cw_ref[3 - j:4 - j, sl]
            xbc_ref[:, sl] = acc * _sigmoid(acc)

        _, dt, acs, acs_t, acs_x, dt_x, xs = _ssd_common(xbc_ref, dtw_ref, dtb_ref, arow_ref, t_ref, e_ref)
        xdt = xs * dt_x
        eacs = jnp.exp(acs_x)
        acs_last = acs_x[Q - 1:Q, :]
        dec = jnp.exp(acs_last - acs_x)
        hs_ref[0] = h_scr[...].astype(hs_ref.dtype)
        causal = lax.broadcasted_iota(jnp.int32, (Q, Q), 0) >= lax.broadcasted_iota(jnp.int32, (Q, Q), 1)
        first = lax.broadcasted_iota(jnp.int32, (Q, 128), 1) < HP
        for g in range(NG):
            bg = xbc_ref[:, DI + g * NS:DI + (g + 1) * NS]
            cg = xbc_ref[:, DI + NG * NS + g * NS:DI + NG * NS + (g + 1) * NS]
            s = _dot(cg, bg, NT)
            sl = slice(g * GW, (g + 1) * GW)
            hg = h_scr[:, sl]
            yoff = _dot(cg, hg, NN) * eacs[:, sl]
            st = _dot(bg, xdt[:, sl] * dec[:, sl], TN)
            h_scr[:, sl] = hg * eacs[Q - 1:Q, sl] + st
            for j in range(4):
                lo = g * GW + j * 128
                xb = xdt[:, lo:lo + 128]
                yp = yoff[:, j * 128:(j + 1) * 128] + dsk_ref[:, lo:lo + 128] * xs[:, lo:lo + 128]
                for e in range(2):
                    h = g * 8 + j * 2 + e
                    lm = jnp.exp(jnp.where(causal, acs[:, h:h + 1] - acs_t[h:h + 1, :], NEG))
                    xm = jnp.where(first if e == 0 else jnp.logical_not(first), xb, 0.0)
                    yp = yp + _dot(s * lm, xm, NN)
                y_ref[:, lo:lo + 128] = yp.astype(y_ref.dtype)

    const = lambda c: (0, 0)
    return pl.pallas_call(
        body, name=name, grid=(nc,),
        in_specs=[pl.BlockSpec((Q, XBC), lambda c: (c, 0)),
                  pl.BlockSpec((8, XBC), lambda c: (jnp.maximum(c * (Q // 8) - 1, 0), 0)),
                  pl.BlockSpec((4, XBC), const), pl.BlockSpec((1, XBC), const),
                  pl.BlockSpec((Q, DT_PAD), lambda c: (c, 0)),
                  pl.BlockSpec((1, 128), const), pl.BlockSpec((1, 128), const), pl.BlockSpec((1, DI), const),
                  pl.BlockSpec((Q, Q), const), pl.BlockSpec((256, DI), const)],
        out_specs=[pl.BlockSpec((Q, DI), lambda c: (c, 0)), pl.BlockSpec((1, NS, DI), lambda c: (c, 0, 0)),
                   pl.BlockSpec((Q, XBC), lambda c: (c, 0))],
        out_shape=[jax.ShapeDtypeStruct((L, DI), BF16), jax.ShapeDtypeStruct((nc, NS, DI), F32),
                   jax.ShapeDtypeStruct((L, XBC), F32)],
        scratch_shapes=[pltpu.VMEM((NS, DI), F32), pltpu.VMEM((8 + Q, XBC), F32)],
        compiler_params=_params(("arbitrary",), VMEM_BIG))(raw, raw, cw, cb, dtp, dtb, arow, dsk_x, tri, expand)


def _ssd_bwd(dy, xbc, proj, hs, dtb, arow, dsk_x, dproj, *, name):
    L = xbc.shape[0]
    nc = L // Q
    tri, triu, expand = _ssd_consts()

    def body(dy_ref, xbc_ref, dtw_ref, hs_ref, dtb_ref, arow_ref, dsk_ref, t_ref, u_ref, e_ref, dp_in,
             dxbc_ref, ddtw_ref, da_ref, ddx_ref, ddtb_ref, dh_scr):
        del dp_in
        i = pl.program_id(0)

        @pl.when(i == 0)
        def _():
            dh_scr[...] = jnp.zeros_like(dh_scr)

        pre, dt, acs, acs_t, acs_x, dt_x, xs = _ssd_common(xbc_ref, dtw_ref, dtb_ref, arow_ref, t_ref, e_ref)
        dyv = dy_ref[...]
        xdt = xs * dt_x
        eacs = jnp.exp(acs_x)
        acs_last = acs_x[Q - 1:Q, :]
        dec = jnp.exp(acs_last - acs_x)
        gy = dyv * eacs
        causal = lax.broadcasted_iota(jnp.int32, (Q, Q), 0) >= lax.broadcasted_iota(jnp.int32, (Q, Q), 1)
        first = lax.broadcasted_iota(jnp.int32, (Q, 128), 1) < HP
        lane_h = lax.broadcasted_iota(jnp.int32, (Q, 128), 1)
        sub_h = lax.broadcasted_iota(jnp.int32, (128, Q), 0)
        last_row = lax.broadcasted_iota(jnp.int32, (Q, GW), 0) == Q - 1
        dacs = jnp.zeros((Q, 128), F32)
        dacs_t = jnp.zeros((128, Q), F32)
        ddt = jnp.zeros((Q, 128), F32)
        for g in range(NG):
            bg = xbc_ref[:, DI + g * NS:DI + (g + 1) * NS]
            cg = xbc_ref[:, DI + NG * NS + g * NS:DI + NG * NS + (g + 1) * NS]
            s = _dot(cg, bg, NT)
            sl = slice(g * GW, (g + 1) * GW)
            hg = hs_ref[0, :, sl].astype(F32)
            dhn = dh_scr[:, sl]
            eal = eacs[Q - 1:Q, sl]
            gg = gy[:, sl]
            dax = gg * _dot(cg, hg, NN)
            dcg = _dot(gg, hg, NT)
            dh_scr[:, sl] = _dot(cg, gg, TN) + dhn * eal
            dal = eal * _colsum(dhn * hg)
            xdd = xdt[:, sl] * dec[:, sl]
            dbg = _dot(xdd, dhn, NT)
            wv = _dot(bg, dhn, NN)
            dd = wv * xdd
            dax = dax - dd
            dal = dal + _colsum(dd)
            dax = dax + jnp.where(last_row, dal, 0.0)
            dxdt_g = wv * dec[:, sl]
            ds = jnp.zeros((Q, Q), F32)
            dxdt_blocks = []
            for j in range(4):
                lo = g * GW + j * 128
                xb = xdt[:, lo:lo + 128]
                dyb = dyv[:, lo:lo + 128]
                dxb = dxdt_g[:, j * 128:(j + 1) * 128]
                for e in range(2):
                    h = g * 8 + j * 2 + e
                    lm = jnp.exp(jnp.where(causal, acs[:, h:h + 1] - acs_t[h:h + 1, :], NEG))
                    m = s * lm
                    dym = jnp.where(first if e == 0 else jnp.logical_not(first), dyb, 0.0)
                    dm = _dot(dym, xb, NT)
                    r = dm * m
                    dacs = dacs + jnp.where(lane_h == h, jnp.sum(r, axis=1, keepdims=True), 0.0)
                    dacs_t = dacs_t + jnp.where(sub_h == h, _colsum(r), 0.0)
                    ds = ds + dm * lm
                    dxb = dxb + _dot(m, dym, TN)
                dxdt_blocks.append(dxb)
            dxdt = jnp.concatenate(dxdt_blocks, axis=1)
            dcg = dcg + _dot(ds, bg, NN)
            dbg = dbg + _dot(ds, cg, TN)
            dxbc_ref[:, DI + g * NS:DI + (g + 1) * NS] = dbg
            dxbc_ref[:, DI + NG * NS + g * NS:DI + NG * NS + (g + 1) * NS] = dcg
            dxbc_ref[:, sl] = dsk_ref[:, sl] * dyv[:, sl] + dxdt * dt_x[:, sl]
            ddt_g, dacs_g = _reduce_heads([dxdt * xs[:, sl], dax], e_ref[0:128, sl])
            ddt = ddt + ddt_g
            dacs = dacs + dacs_g
        dacs = dacs - dacs_t.T
        ddta = _tri_sum(u_ref, dacs)
        ddt = ddt + ddta * arow_ref[...]
        ddtw = jnp.where(lane_h < NH, ddt * _sigmoid(pre), 0.0)
        ddtw_ref[...] = jnp.concatenate([ddtw, jnp.zeros((Q, DT_PAD - 128), F32)], axis=1).astype(ddtw_ref.dtype)
        _acc_out(da_ref, _colsum(ddta * dt), i)
        _acc_out(ddx_ref, _colsum(dyv * xs), i)
        _acc_out(ddtb_ref, _colsum(ddtw), i)

    rev = lambda c: (nc - 1 - c, 0)
    const = lambda c: (0, 0)
    return pl.pallas_call(
        body, name=name, grid=(nc,),
        in_specs=[pl.BlockSpec((Q, DI), rev), pl.BlockSpec((Q, XBC), rev),
                  pl.BlockSpec((Q, DT_PAD), rev),
                  pl.BlockSpec((1, NS, DI), lambda c: (nc - 1 - c, 0, 0)),
                  pl.BlockSpec((1, 128), const), pl.BlockSpec((1, 128), const), pl.BlockSpec((1, DI), const),
                  pl.BlockSpec((Q, Q), const), pl.BlockSpec((Q, Q), const), pl.BlockSpec((256, DI), const),
                  pl.BlockSpec(memory_space=pl.ANY)],
        out_specs=[pl.BlockSpec((Q, XBC), rev),
                   pl.BlockSpec((Q, DT_PAD), lambda c: (nc - 1 - c, C_DT // DT_PAD)),
                   pl.BlockSpec((1, 128), const), pl.BlockSpec((1, DI), const), pl.BlockSpec((1, 128), const)],
        out_shape=[jax.ShapeDtypeStruct((L, XBC), F32), jax.ShapeDtypeStruct((L, NPROJ), BF16),
                   jax.ShapeDtypeStruct((1, 128), F32), jax.ShapeDtypeStruct((1, DI), F32),
                   jax.ShapeDtypeStruct((1, 128), F32)],
        scratch_shapes=[pltpu.VMEM((NS, DI), F32)],
        input_output_aliases={10: 1},
        compiler_params=_params(("arbitrary",), VMEM_BIG))(dy, xbc, proj, hs, dtb, arow, dsk_x, tri, triu,
                                                          expand, dproj)


def _adam_update(wv, gv, mv, vv):
    c1 = 1.0 - ADAM_B1 ** ADAM_STEP
    c2 = 1.0 - ADAM_B2 ** ADAM_STEP
    mn = ADAM_B1 * mv + (1.0 - ADAM_B1) * gv
    vn = ADAM_B2 * vv + (1.0 - ADAM_B2) * (gv * gv)
    return -ADAM_LR * ((mn / c1) / (jnp.sqrt(vn / c2) + ADAM_EPS) + ADAM_WD * wv), mn, vn


def _adamw(w, g, m, v, *, name, tr=None):
    R = w.shape[0]
    rest = tuple(w.shape[1:])
    if tr is None:
        tr = _pick(R, (256, 128, 64, 32, 16, 8))
    assert R % tr == 0

    def body(w_ref, g_ref, m_ref, v_ref, d_ref, mo_ref, vo_ref):
        d_ref[...], mo_ref[...], vo_ref[...] = _adam_update(w_ref[...], g_ref[...], m_ref[...], v_ref[...])

    zeros = (0,) * len(rest)
    spec = pl.BlockSpec((tr,) + rest, lambda i: (i,) + zeros)
    return pl.pallas_call(body, name=name, grid=(R // tr,), in_specs=[spec] * 4, out_specs=[spec] * 3,
                          out_shape=[jax.ShapeDtypeStruct(w.shape, F32)] * 3,
                          compiler_params=_params(("parallel",)))(w, g, m, v)


def _adamw_small(svrow, g_conv, params, *, name):
    n = len(params)

    def body(*refs):
        sv_ref, gc_ref = refs[0], refs[1]
        ins, outs = refs[2:2 + 3 * n], refs[2 + 3 * n:]
        for p, (key, w, _, _) in enumerate(params):
            w_ref, m_ref, v_ref = ins[3 * p:3 * p + 3]
            g_ref, d_ref, mo_ref, vo_ref = outs[4 * p:4 * p + 4]
            gv = gc_ref[...] if key == "conv_w" else sv_ref[:, SV_OFF[key]:SV_OFF[key] + w.shape[1]]
            g_ref[...] = gv
            d_ref[...], mo_ref[...], vo_ref[...] = _adam_update(w_ref[...], gv, m_ref[...], v_ref[...])

    vm = pl.BlockSpec(memory_space=pltpu.VMEM)
    args = [svrow, g_conv]
    shapes = []
    for _, w, m, v in params:
        args += [w, m, v]
        shapes += [jax.ShapeDtypeStruct(w.shape, F32)] * 4
    res = pl.pallas_call(body, name=name, in_specs=[vm] * len(args), out_specs=[vm] * len(shapes),
                         out_shape=shapes)(*args)
    return {key: tuple(res[4 * p:4 * p + 4]) for p, (key, _, _, _) in enumerate(params)}


def _slab_sum(recv, *, tile, name):
    rows = recv.shape[1]
    assert rows % tile == 0 and tile % 16 == 0

    def body(r_ref, o_ref):
        acc = r_ref[0].astype(F32)
        for j in range(1, N_DEV):
            acc = acc + r_ref[j].astype(F32)
        o_ref[...] = acc

    return pl.pallas_call(body, name=name, grid=(rows // tile,),
                          in_specs=[pl.BlockSpec((N_DEV, tile, D), lambda i: (0, i, 0))],
                          out_specs=pl.BlockSpec((tile, D), lambda i: (i, 0)),
                          out_shape=jax.ShapeDtypeStruct((rows, D), F32),
                          compiler_params=_params(("parallel",)))(recv)


MESH = pl.DeviceIdType.MESH


def _coords():
    return lax.axis_index("x"), lax.axis_index("y"), lax.axis_index("c")


def _peer(k):
    x, y, c = _coords()
    px = 1 - x if k & 4 else x
    py = 1 - y if k & 2 else y
    pc = 1 - c if k & 1 else c
    return (px, py, pc), 4 * px + 2 * py + pc


def _rcopy(src, dst, ssem, rsem, dev):
    return pltpu.make_async_remote_copy(src_ref=src, dst_ref=dst, send_sem=ssem, recv_sem=rsem,
                                        device_id=dev, device_id_type=MESH)


def _exchange_all(src_of, dst_slot, send_sems, recv_sems):
    x, y, c = _coords()
    me = 4 * x + 2 * y + c
    sent = []
    for k in range(1, N_DEV):
        dev, pidx = _peer(k)
        cp = _rcopy(src_of(pidx), dst_slot(me), send_sems.at[k - 1], recv_sems.at[k - 1], dev)
        cp.start()
        sent.append(cp)
    for k in range(1, N_DEV):
        dev, pidx = _peer(k)
        _rcopy(src_of(pidx), dst_slot(pidx), send_sems.at[k - 1], recv_sems.at[k - 1], dev).wait_recv()
    for cp in sent:
        cp.wait_send()


def _rows_of_slots(buf, nslots):
    rows = lax.broadcasted_iota(jnp.int32, (8, buf.shape[-1]), 0)
    out = jnp.zeros((8, buf.shape[-1]), F32)
    for j in range(nslots):
        out = out + jnp.where(rows == j, buf[j], 0.0)
    return out


def _ada_fwd(c, w_ada, b_r, *, name):
    wloc = w_ada.shape[1]

    def body(c_ref, w_ref, b_ref, mod_ref, call_ref, csrc, cbuf, psrc, pbuf, s1, r1, s2, r2):
        x, y, cc = _coords()
        me = 4 * x + 2 * y + cc
        csrc[...] = jnp.broadcast_to(c_ref[...], (8, D))
        cbuf[me] = csrc[...]
        _exchange_all(lambda p: csrc, lambda s: cbuf.at[s], s1, r1)
        call = _rows_of_slots(cbuf, N_DEV)
        call_ref[...] = call
        prod = _dot_hi(_silu(call), w_ref[...])
        for b in range(N_DEV):
            psrc[b] = jnp.broadcast_to(prod[b:b + 1, :], (8, wloc))
        pbuf[me] = psrc[me]
        _exchange_all(lambda p: psrc.at[p], lambda s: pbuf.at[s], s2, r2)
        mod_ref[...] = _rows_of_slots(pbuf, N_DEV) + b_ref[...]

    vm = pl.BlockSpec(memory_space=pltpu.VMEM)
    return pl.pallas_call(
        body, name=name, in_specs=[vm, vm, vm], out_specs=[vm, vm],
        out_shape=[jax.ShapeDtypeStruct((N_DEV, wloc), F32), jax.ShapeDtypeStruct((N_DEV, D), F32)],
        scratch_shapes=[pltpu.VMEM((8, D), F32), pltpu.VMEM((N_DEV, 8, D), F32),
                        pltpu.VMEM((N_DEV, 8, wloc), F32), pltpu.VMEM((N_DEV, 8, wloc), F32),
                        pltpu.SemaphoreType.DMA((N_DEV - 1,)), pltpu.SemaphoreType.DMA((N_DEV - 1,)),
                        pltpu.SemaphoreType.DMA((N_DEV - 1,)), pltpu.SemaphoreType.DMA((N_DEV - 1,))],
        compiler_params=pltpu.CompilerParams(vmem_limit_bytes=VMEM_BIG))(c, w_ada, b_r)


def _gather_slabs(slab, *, name):
    def body(x_ref, out_ref, send_sems, recv_sems, local_sem):
        x, y, c = _coords()
        me, sibling = (x, y, c), (x, y, 1 - c)
        chips = [(1 - x, y), (x, 1 - y), (1 - x, 1 - y)]

        def slot(px, py, pc):
            return out_ref.at[4 * px + 2 * py + pc]

        def copy(k, block, to, src=None):
            return _rcopy(slot(*block) if src is None else src, slot(*block), send_sems.at[k], recv_sems.at[k], to)

        mine = pltpu.make_async_copy(x_ref, slot(*me), local_sem)
        mine.start()
        first = [copy(0, me, sibling, src=x_ref)]
        first += [copy(1 + j, me, (*chip, c), src=x_ref) for j, chip in enumerate(chips)]
        for cp in first:
            cp.start()
        passed = [copy(4 + j, (*chip, c), sibling) for j, chip in enumerate(chips)]
        for j, chip in enumerate(chips):
            copy(1 + j, (*chip, c), me).wait_recv()
            passed[j].start()
        copy(0, sibling, me).wait_recv()
        for j, chip in enumerate(chips):
            copy(4 + j, (*chip, 1 - c), me).wait_recv()
        for cp in first + passed:
            cp.wait_send()
        mine.wait()

    anyspec = pl.BlockSpec(memory_space=pl.ANY)
    return pl.pallas_call(
        body, name=name, in_specs=[anyspec], out_specs=anyspec,
        out_shape=jax.ShapeDtypeStruct((N_DEV,) + slab.shape, slab.dtype),
        scratch_shapes=[pltpu.SemaphoreType.DMA((7,)), pltpu.SemaphoreType.DMA((7,)), pltpu.SemaphoreType.DMA],
    )(slab)


_HBM =pl.BlockSpec(memory_space=pltpu.HBM)
_SEM = pl.BlockSpec(memory_space=pltpu.SEMAPHORE)
_EFFECT = pltpu.SideEffectType.DATAFLOW_SIDE_EFFECTING


def _xchg_src(src_ref, pidx, per_peer):
    return src_ref.at[pidx] if per_peer else src_ref


def _xchg_start(src, *, per_peer, name):
    rows = src.shape[-2]
    land_shape = (N_DEV, rows, D)

    def body(src_ref, land_ref, send_sems, recv_sems, src_thru, land_thru, token):
        del src_thru, land_thru
        x, y, c = _coords()
        me = 4 * x + 2 * y + c
        for k in range(1, N_DEV):
            dev, pidx = _peer(k)
            _rcopy(_xchg_src(src_ref, pidx, per_peer), land_ref.at[me], send_sems.at[k - 1],
                   recv_sems.at[k - 1], dev).start()
        token[...] = jnp.zeros_like(token)

    return pl.pallas_call(
        body, name=name,
        out_shape=(pltpu.SemaphoreType.DMA((N_DEV - 1,)), pltpu.SemaphoreType.DMA((N_DEV - 1,)),
                   pltpu.HBM(src.shape, src.dtype), pltpu.HBM(land_shape, src.dtype),
                   jax.ShapeDtypeStruct((8, 128), F32)),
        in_specs=(_HBM, _HBM),
        out_specs=(_SEM, _SEM, _HBM, _HBM, pl.BlockSpec(memory_space=pltpu.VMEM)),
        input_output_aliases={0: 2, 1: 3},
        compiler_params=pltpu.CompilerParams(has_side_effects=_EFFECT),
    )(pltpu.with_memory_space_constraint(src, pltpu.HBM),
      pltpu.with_memory_space_constraint(lax.empty(land_shape, src.dtype), pltpu.HBM))


def _xchg_wait(started, after, *, per_peer, name):
    send_sems, recv_sems, src_thru, land_thru, _ = started

    def body(src_ref, land_ref, send_sems, recv_sems, after_ref, src_dead, got_ref):
        del after_ref, src_dead, got_ref
        for k in range(1, N_DEV):
            dev, pidx = _peer(k)
            cp = _rcopy(_xchg_src(src_ref, pidx, per_peer), land_ref.at[pidx], send_sems.at[k - 1],
                        recv_sems.at[k - 1], dev)
            cp.wait_send()
            cp.wait_recv()

    return pl.pallas_call(
        body, name=name,
        out_shape=(pltpu.HBM(src_thru.shape, src_thru.dtype), pltpu.HBM(land_thru.shape, land_thru.dtype)),
        in_specs=(_HBM, _HBM, _SEM, _SEM, pl.BlockSpec(memory_space=pl.ANY)),
        out_specs=(_HBM, _HBM),
        input_output_aliases={0: 0, 1: 1},
        compiler_params=pltpu.CompilerParams(has_side_effects=_EFFECT),
    )(src_thru, land_thru, send_sems, recv_sems, after)


def _dep(token):
    return (token, (8, 128), lambda i, j, k: (0, 0))


def _small_allsum(sv, *, name):
    def body(sv_ref, all_ref, sum_ref, send_sems, recv_sems):
        x, y, c = _coords()
        me = 4 * x + 2 * y + c
        all_ref[me] = sv_ref[...]
        _exchange_all(lambda p: sv_ref, lambda s: all_ref.at[s], send_sems, recv_sems)
        acc = all_ref[0]
        for j in range(1, N_DEV):
            acc = acc + all_ref[j]
        sum_ref[...] = acc

    vm = pl.BlockSpec(memory_space=pltpu.VMEM)
    return pl.pallas_call(
        body, name=name, in_specs=[vm], out_specs=[vm, vm],
        out_shape=[jax.ShapeDtypeStruct((N_DEV, SV_ROWS, 128), F32), jax.ShapeDtypeStruct((SV_ROWS, 128), F32)],
        scratch_shapes=[pltpu.SemaphoreType.DMA((7,)), pltpu.SemaphoreType.DMA((7,))],
    )(sv)


def _ada_bwd(call, dmod_loc, *, name):
    wloc = dmod_loc.shape[1]

    def body(c_ref, d_ref, o_ref):
        o_ref[...] = _dot_hi(_silu(c_ref[...]), d_ref[...], TN)

    vm = pl.BlockSpec(memory_space=pltpu.VMEM)
    return pl.pallas_call(body, name=name, in_specs=[vm, vm], out_specs=vm,
                          out_shape=jax.ShapeDtypeStruct((D, wloc), F32),
                          compiler_params=pltpu.CompilerParams(vmem_limit_bytes=VMEM_BIG))(call, dmod_loc)


def _pad_rows(a, rows):
    return jnp.pad(a, ((0, rows - a.shape[0]), (0, 0)))


IN_SHIFT = tuple((IN_ROWS * j) % 16 for j in range(N_DEV))
IN_BASE = tuple(IN_ROWS * j - IN_SHIFT[j] for j in range(N_DEV))
IN_SEGMENTS = ((2048, XBC, C_XBC), (5152, 1024, C_POOL), (0, 2048, C_Z), (6176, 2048, C_GATE), (5120, 32, C_DT))


def _global_pieces(gs):
    pieces = []
    for j in range(N_DEV):
        lo, hi = 0, IN_ROWS_P
        if j > 0 and IN_BASE[j - 1] + IN_ROWS_P > IN_BASE[j]:
            pieces.append((IN_BASE[j], 16, gs[j - 1, IN_ROWS_P - 16:IN_ROWS_P] + gs[j, 0:16]))
            lo = 16
        if j + 1 < N_DEV and IN_BASE[j] + IN_ROWS_P > IN_BASE[j + 1]:
            hi = IN_ROWS_P - 16
        pieces.append((IN_BASE[j] + lo, hi - lo, gs[j, lo:hi]))
    return pieces


def _reorder_in_rows(gs):
    pieces = _global_pieces(gs)
    parts = []
    for lo, n, _ in IN_SEGMENTS:
        for p0, pn, arr in pieces:
            a, b = max(lo, p0), min(lo + n, p0 + pn)
            if a < b:
                parts.append(arr[a - p0:b - p0])
    parts.append(jnp.zeros((DT_PAD - 32, D), gs.dtype))
    return jnp.concatenate(parts, axis=0)


def _restore_in_shards(d):
    slabs = []
    for j in range(N_DEV):
        parts = []
        r, end = IN_BASE[j], IN_BASE[j] + IN_ROWS_P
        while r < end:
            lo, n, new = next(s for s in IN_SEGMENTS if s[0] <= r < s[0] + s[1])
            e = min(end, lo + n)
            parts.append(d[new + r - lo:new + e - lo])
            r = e
        slabs.append(jnp.concatenate(parts, axis=0))
    return jnp.stack(slabs, axis=0)


def _pack_sv(parts):
    flat = []
    for n, size in SV_PARTS:
        v = parts[n].reshape(-1).astype(F32)
        flat.append(jnp.pad(v, (0, size - v.shape[0])))
    v = jnp.concatenate(flat)
    return jnp.pad(v, (0, SV_ROWS * 128 - v.shape[0])).reshape(SV_ROWS, 128)


def _sv_get(flat, n, size):
    return flat[SV_OFF[n]:SV_OFF[n] + size]


def kernel(x, c, w_ada, b_ada, norm_mix_w, w_in, conv_w, conv_b, dt_bias, a_log, d_skip, ssd_norm_w, w_branch_ssd, pool_w, pool_scale, w_branch_pool, w_out, norm_mlp_w, w_up, w_down, norm_final_w, loss_target, m_w_ada, m_b_ada, m_norm_mix_w, m_w_in, m_conv_w, m_conv_b, m_dt_bias, m_a_log, m_d_skip, m_ssd_norm_w, m_w_branch_ssd, m_pool_w, m_pool_scale, m_w_branch_pool, m_w_out, m_norm_mlp_w, m_w_up, m_w_down, m_norm_final_w, v_w_ada, v_b_ada, v_norm_mix_w, v_w_in, v_conv_w, v_conv_b, v_dt_bias, v_a_log, v_d_skip, v_ssd_norm_w, v_w_branch_ssd, v_pool_w, v_pool_scale, v_w_branch_pool, v_w_out, v_norm_mlp_w, v_w_up, v_w_down, v_norm_final_w):
    xs_ = x[0]
    tgt = loss_target[0]
    L = xs_.shape[0]
    me = 4 * lax.axis_index("x") + 2 * lax.axis_index("y") + lax.axis_index("c")
    wloc = w_ada.shape[2]

    mod_p, c_all = _ada_fwd(c, w_ada[0], b_ada.reshape(N_DEV, wloc), name="ada_fwd")
    mod = mod_p.reshape(6, D)
    shift_m, scale_m, gate_m, shift_f, scale_f, gate_f = [mod[i:i + 1] for i in range(6)]

    conv_bits = lax.bitcast_convert_type(conv_w[0], SLAB_DT).reshape(3, D)
    in_shift = (IN_ROWS * me) % 16
    slab_in = lax.dynamic_update_slice(jnp.zeros((IN_ROWS_P, D), SLAB_DT), w_in[0].T.astype(SLAB_DT),
                                       (in_shift, 0))
    slab_in = jnp.concatenate([slab_in, _pad_rows(conv_bits, CONV_ROWS)], axis=0)
    slab_rest = jnp.concatenate([
        w_branch_ssd[0].astype(SLAB_DT),
        pool_w[0].reshape(32, D).astype(SLAB_DT),
        w_branch_pool[0].astype(SLAB_DT),
        w_out[0].astype(SLAB_DT),
        w_up[0].T.astype(SLAB_DT),
        w_down[0].astype(SLAB_DT)], axis=0)
    slab_in, mod_p = lax.optimization_barrier((slab_in, mod_p))
    gs_in = _gather_slabs(slab_in, name="gather_w_in")
    slab_rest, gs_in = lax.optimization_barrier((slab_rest, gs_in))
    rest_started = _xchg_start(slab_rest, per_peer=False, name="gather_rest_start")
    gather_token = rest_started[4]

    w_in_t = _reorder_in_rows(gs_in)
    conv_full = lax.bitcast_convert_type(
        gs_in[:, IN_ROWS_P:IN_ROWS_P + 3].reshape(N_DEV, 4, XBC // N_DEV, 2), F32)
    conv_full = conv_full.transpose(1, 0, 2).reshape(4, XBC)

    dtb = jnp.pad(dt_bias, ((0, 0), (0, 128 - NH)))
    arow = jnp.pad(-jnp.exp(a_log), ((0, 0), (0, 128 - NH)))
    dsk_x = jnp.repeat(d_skip, HP, axis=1)

    tm = _pick(L, (1024, 512, 256, 128))
    tm2 = _pick(L, (2048, 1024, 512, 256, 128))
    tkl = _pick(L, (4096, 2048, 1024, 512, 256, 128))
    tkl2 = _pick(L, (2048, 1024, 512, 256, 128))

    tmh = _pick(L, (512, 256, 128))
    zcol = C_Z // DI
    gcol = C_GATE // (2 * D)

    def whole_rows(w):
        return lambda t: ((L, w), BF16, (t, w), lambda i, j, k: (i, 0))

    def norm1_pro(x_ref, ex, outs, j):
        @pl.when(j == 0)
        def _():
            xv = x_ref[...]
            r = lax.rsqrt(jnp.mean(xv * xv, axis=-1, keepdims=True) + EPS)
            outs[1][...] = (xv * r * ex[0][...] * (1.0 + ex[1][...]) + ex[2][...]).astype(outs[1].dtype)

        return outs[1][...]

    def proj_ep(acc, ex, outs):
        outs[0][...] = acc

        @pl.when(pl.program_id(1) == NPROJ // 768 - 1)
        def _():
            outs[2][...] = acc[:, 768 - DT_PAD:]

    proj, h1, dtp = _mm(
        xs_, w_in_t, "nt", name="in_proj", tm=tm2, tn=768, tk=D,
        extras=[(norm_mix_w, *_vecs()), (scale_m, *_vecs()), (shift_m, *_vecs()), _dep(gather_token)],
        outs=[F32, whole_rows(D)(tm2), ((L, DT_PAD), F32, (tm2, DT_PAD), lambda i, j, k: (i, 0))],
        prologue=norm1_pro, epilogue=proj_ep)
    xbc_raw = proj
    y_ssm, hs, xbc = _ssd_fwd(xbc_raw, dtp, conv_full, conv_b, dtb, arow, dsk_x, name="ssd_fwd")

    slab_rest, gs = _xchg_wait(rest_started, y_ssm, per_peer=False, name="gather_rest_wait")
    gs = lax.dynamic_update_slice(gs, slab_rest[None], (me, 0, 0))

    def part(n, rows):
        return gs[:, REST_OFF[n]:REST_OFF[n] + rows]

    w_bssd = part("bssd", 256).reshape(DI, D)
    w_pool = part("pool", 32).reshape(N_DEV, 4, 32, PGW).transpose(1, 0, 2, 3).reshape(POOL_W, PGW)
    w_bpool = part("bpool", 128).reshape(POOL_W, D)
    w_o = part("out", 128).reshape(D, D)
    w_up_t = part("up", 512).reshape(DFF, D)
    w_dn = part("down", 512).reshape(DFF, D)

    def gnorm_pro(y_ref, ex, outs, j):
        z_ref, w_ref = ex
        yg = y_ref[...].astype(F32) * _silu(z_ref[...].astype(F32))
        segs = []
        for k in range(NG):
            sl = slice(k * GW, (k + 1) * GW)
            seg = yg[:, sl]
            r = lax.rsqrt(jnp.mean(seg * seg, axis=-1, keepdims=True) + EPS)
            segs.append((seg * r * w_ref[:, sl]).astype(BF16))
        yn_v = jnp.concatenate(segs, axis=1)
        outs[1][...] = yn_v
        return yn_v

    y_ssd, yn = _mm(y_ssm, w_bssd, "nn", name="branch_ssd", tm=tmh, tn=D, tk=DI,
                    extras=[(proj, *_rows(tmh, DI, zcol)), (ssd_norm_w, *_vecs(DI))],
                    outs=[F32, whole_rows(DI)(tmh)], prologue=gnorm_pro)
    pooled = _pool_fwd(proj, name="pool_fwd")
    wp_spec = ((POOL_W, PGW), lambda i, j, k: (0, 0))

    def pool_pro(a_ref, ex, outs, j):
        wp_ref, s_ref = ex
        segs = []
        for g in range(4):
            sl = slice(g * PGW, (g + 1) * PGW)
            p = _dot(a_ref[:, sl], wp_ref[sl, :], NN)
            outs[1][:, sl] = p.astype(BF16)
            segs.append((p * s_ref[:, sl]).astype(BF16))
        yp1_v = jnp.concatenate(segs, axis=1)
        outs[2][...] = yp1_v
        return yp1_v

    y_pool, yp0, yp1 = _mm(pooled, w_bpool, "nn", name="branch_pool", tm=tm, tn=D, tk=D,
                           extras=[(w_pool, *wp_spec), (pool_scale, *_vecs())],
                           outs=[F32, whole_rows(D)(tm), whole_rows(D)(tm)], prologue=pool_pro)

    def merge_pro(a_ref, ex, outs, j):
        s = _sigmoid(ex[1][...].astype(F32))
        mv = (s[:, :D] * a_ref[...] + s[:, D:] * ex[0][...]).astype(BF16)
        outs[3][...] = mv
        return mv

    mix, x1, h2, m = _mm(y_ssd, w_o, "nn", name="out_proj", tm=tmh, tn=D, tk=D,
                         extras=[(y_pool, *_rows(tmh)), (proj, *_rows(tmh, 2 * D, gcol)),
                                 (xs_, *_rows(tmh)), (gate_m, *_vecs()), (norm_mlp_w, *_vecs()),
                                 (scale_f, *_vecs()), (shift_f, *_vecs())],
                         outs=[BF16, F32, BF16, whole_rows(D)(tmh)], prologue=merge_pro,
                         epilogue=lambda acc, ex, outs: _ep_resid_norm(acc, ex[2:], outs[:3]))

    def relu2(acc, ex, outs):
        r = jnp.maximum(acc, 0.0)
        outs[0][...] = acc.astype(BF16)
        outs[1][...] = (r * r).astype(BF16)

    up, act = _mm(h2, w_up_t, "nt", name="mlp_up", outs=[BF16, BF16], tm=tm2, tn=1024, tk=D, epilogue=relu2)

    dx2, ddown, loss_p, dnwf, dgate_f = _mm(
        act, w_dn, "nn", name="mlp_down", tm=tmh, tn=D, tk=DFF,
        extras=[(x1, *_rows(tmh)), (tgt, *_rows(tmh)), (gate_f, *_vecs()), (norm_final_w.reshape(1, D), *_vecs())],
        outs=[F32, BF16, _sum_out(128), _sum_out(), _sum_out()], epilogue=_ep_final)

    def drelu2(acc, ex, outs):
        outs[0][...] = (acc * (2.0 * jnp.maximum(ex[0][...].astype(F32), 0.0))).astype(BF16)

    def dep_last(ep):
        return lambda acc, ex, outs: ep(acc, ex[:-1], outs)

    dup = _mm(ddown, w_dn, "nt", name="mlp_down_dx", outs=[BF16], tm=tm2, tn=1024, tk=D,
              extras=[(up, (tm2, 1024), lambda i, j, k: (i, j))], epilogue=drelu2)
    g_dn = _mm(act, ddown, "tn", name="mlp_down_dw", outs=[SLAB_DT], tm=1024, tn=D, tk=tkl)
    g_up_t = _mm(dup, h2, "tn", name="mlp_up_dw", outs=[SLAB_DT], tm=1024, tn=D, tk=tkl)
    gslab_mlp = jnp.concatenate([g_up_t.reshape(N_DEV, 512, D), g_dn.reshape(N_DEV, 512, D)], axis=1)
    mlp_started = _xchg_start(gslab_mlp, per_peer=True, name="scatter_mlp_start")
    dx1, p2, q2, dmix, dgate_m = _mm(
        dup, w_up_t, "nn", name="mlp_up_dx", tm=tmh, tn=D, tk=DFF,
        extras=[(x1, *_rows(tmh)), (dx2, *_rows(tmh)), (norm_mlp_w, *_vecs()), (scale_f, *_vecs()),
                (mix, *_rows(tmh)), (gate_m, *_vecs()), _dep(mlp_started[4])],
        outs=[F32, _sum_out(), _sum_out(), BF16, _sum_out()], epilogue=dep_last(_ep_norm_bwd))
    gcol = C_GATE // (2 * D)
    dy_ssd, dy_pool, dproj = _mm(
        dmix, w_o, "nt", name="out_proj_dx", tm=tmh, tn=D, tk=D,
        extras=[(y_ssd, *_rows(tmh)), (y_pool, *_rows(tmh)), (proj, *_rows(tmh, 2 * D, gcol))],
        outs=[BF16, BF16, ((L, NPROJ), BF16, *_rows(tmh, 2 * D, gcol))], epilogue=_ep_merge_bwd)
    g_o = _mm(m, dmix, "tn", name="out_proj_dw", outs=[SLAB_DT], tm=D, tn=D, tk=tkl)
    zcol = C_Z // DI
    dy_ssm, dproj, d_snw = _mm(
        dy_ssd, w_bssd, "nt", name="branch_ssd_dx", tm=tmh, tn=DI, tk=D,
        extras=[(y_ssm, *_rows(tmh, DI)), (proj, *_rows(tmh, DI, zcol)), (ssd_norm_w, *_vecs(DI)),
                (dproj, None, None)],
        outs=[F32, ((L, NPROJ), BF16, *_rows(tmh, DI, zcol)), _sum_out(DI)],
        epilogue=_ep_gated_norm_bwd, aliases={3: 1})
    g_bssd = _mm(yn, dy_ssd, "tn", name="branch_ssd_dw", outs=[SLAB_DT], tm=1024, tn=D, tk=tkl)
    dxbc, dproj, d_a, d_dx, d_dtb = _ssd_bwd(dy_ssm, xbc, dtp, hs, dtb, arow, dsk_x, dproj, name="ssd_bwd")
    dproj, d_cw, d_cb = _conv_bwd(xbc_raw, dxbc, conv_full, conv_b, dproj, name="conv_bwd")
    def pool_bwd_ep(acc, ex, outs):
        y_ref, s_ref, wp_ref = ex
        o_ref, ds_ref, dpool_ref = outs
        dyp0_v = (acc * s_ref[...]).astype(BF16)
        o_ref[...] = dyp0_v
        _acc_out(ds_ref, _colsum(acc * y_ref[...].astype(F32)), _row_step())
        for g in range(4):
            sl = slice(g * PGW, (g + 1) * PGW)
            dpool_ref[:, sl] = _dot(dyp0_v[:, sl], wp_ref[sl, :], NT)

    dyp0, d_ps, dpooled = _mm(dy_pool, w_bpool, "nt", name="branch_pool_dx", tm=tm, tn=D, tk=D,
                              extras=[(yp0, *_rows(tm)), (pool_scale, *_vecs()), (w_pool, *wp_spec)],
                              outs=[BF16, _sum_out(), F32], epilogue=pool_bwd_ep)
    g_bpool = _mm(yp1, dy_pool, "tn", name="branch_pool_dw", outs=[SLAB_DT], tm=D, tn=D, tk=tkl)
    g_pool = _mm_pool_tn(pooled, dyp0, name="pool_mix_dw", tk=tkl)
    gslab_mix = jnp.concatenate([
        g_bssd.reshape(N_DEV, 256, D),
        g_pool.reshape(4, N_DEV, 32, PGW).transpose(1, 0, 2, 3).reshape(N_DEV, 32, D).astype(SLAB_DT),
        g_bpool.reshape(N_DEV, 128, D),
        g_o.reshape(N_DEV, 128, D)], axis=1)
    mix_started = _xchg_start(gslab_mix, per_peer=True, name="scatter_mix_start")
    dproj = _pool_bwd(dpooled, dproj, name="pool_bwd")
    g_in_t = _mm(dproj, h1, "tn", name="in_proj_dw", outs=[SLAB_DT], tm=1408, tn=D, tk=tkl2,
                 extras=[_dep(mix_started[4])])
    gslab_in = _restore_in_shards(g_in_t)
    in_started = _xchg_start(gslab_in, per_peer=True, name="scatter_in_start")
    grad_x, p1, q1 = _mm(
        dproj, w_in_t, "nn", name="in_proj_dx", tm=tmh, tn=D, tk=2816,
        extras=[(xs_, *_rows(tmh)), (dx1, *_rows(tmh)), (norm_mix_w, *_vecs()), (scale_m, *_vecs()),
                _dep(in_started[4])],
        outs=[F32, _sum_out(), _sum_out()], epilogue=dep_last(_ep_norm_bwd))

    def landed(started, after, tile, name):
        src, land = _xchg_wait(started, after, per_peer=True, name=name + "_wait")
        own = lax.dynamic_slice_in_dim(src, me, 1, axis=0)
        return _slab_sum(lax.dynamic_update_slice(land, own, (me, 0, 0)), tile=tile, name=name + "_sum")

    gsum_mlp = landed(mlp_started, grad_x, 256, "scatter_mlp")
    gsum_mix = landed(mix_started, grad_x, 272, "scatter_mix")
    gsum_in = landed(in_started, grad_x, 208, "scatter_in")

    dmod = jnp.concatenate([q1, p1 * norm_mix_w, dgate_m, q2, p2 * norm_mlp_w, dgate_f], axis=1)
    d_alog = d_a[:, :NH] * (-jnp.exp(a_log))
    sv = _pack_sv({
        "b_ada": dmod, "norm_mix_w": p1 * (1.0 + scale_m), "conv_b": d_cb, "dt_bias": d_dtb[:, :NH],
        "a_log": d_alog, "d_skip": d_dx.reshape(NH, HP).sum(axis=1), "ssd_norm_w": d_snw,
        "pool_scale": d_ps, "norm_mlp_w": p2 * (1.0 + scale_f), "norm_final_w": dnwf, "conv_w": d_cw,
        "loss": loss_p[:, :1]})
    sv_all, sv_sum = _small_allsum(sv, name="small_allsum")
    flat = sv_sum.reshape(-1)
    loss = flat[SV_OFF["loss"]]
    dmod_all = sv_all.reshape(N_DEV, SV_ROWS * 128)[:, :6 * D]
    g_w_ada = _ada_bwd(c_all, lax.dynamic_slice_in_dim(dmod_all, me * wloc, wloc, axis=1), name="ada_bwd")

    g_conv_w = lax.dynamic_slice_in_dim(_sv_get(flat, "conv_w", 4 * XBC).reshape(4, XBC),
                                        me * (XBC // N_DEV), XBC // N_DEV, axis=1)
    small = [("b_ada", b_ada, m_b_ada, v_b_ada), ("norm_mix_w", norm_mix_w, m_norm_mix_w, v_norm_mix_w),
             ("conv_b", conv_b, m_conv_b, v_conv_b), ("dt_bias", dt_bias, m_dt_bias, v_dt_bias),
             ("a_log", a_log, m_a_log, v_a_log), ("d_skip", d_skip, m_d_skip, v_d_skip),
             ("ssd_norm_w", ssd_norm_w, m_ssd_norm_w, v_ssd_norm_w),
             ("pool_scale", pool_scale, m_pool_scale, v_pool_scale),
             ("norm_mlp_w", norm_mlp_w, m_norm_mlp_w, v_norm_mlp_w),
             ("norm_final_w", norm_final_w[None], m_norm_final_w[None], v_norm_final_w[None]),
             ("conv_w", conv_w[0], m_conv_w[0], v_conv_w[0])]
    small_out = _adamw_small(sv_sum.reshape(1, SV_ROWS * 128), g_conv_w, small, name="adamw_small")
    small_out["norm_final_w"] = tuple(a[0] for a in small_out["norm_final_w"])
    small_out["conv_w"] = tuple(a[None] for a in small_out["conv_w"])

    def gpart(n, rows_):
        return gsum_mix[MIX_OFF[n]:MIX_OFF[n] + rows_]

    def lin(a):
        return a[0].T.reshape(IN_ROWS * 8, 128)

    g_lin = lax.dynamic_slice_in_dim(gsum_in, in_shift, IN_ROWS, axis=0).reshape(IN_ROWS * 8, 128)
    dlt, mn, vn = _adamw(lin(w_in), g_lin, lin(m_w_in), lin(v_w_in), name="adamw_w_in", tr=IN_ROWS * 2)
    big_in = tuple(a.reshape(IN_ROWS, D).T[None] for a in (g_lin, dlt, mn, vn))

    big = {
        "w_ada": (w_ada, m_w_ada, v_w_ada, g_w_ada, (D, wloc)),
        "w_branch_ssd": (w_branch_ssd, m_w_branch_ssd, v_w_branch_ssd, gpart("bssd", 256), (256, D)),
        "pool_w": (pool_w, m_pool_w, v_pool_w, gpart("pool", 32).reshape(128, PGW), (128, PGW)),
        "w_branch_pool": (w_branch_pool, m_w_branch_pool, v_w_branch_pool, gpart("bpool", 128), (128, D)),
        "w_out": (w_out, m_w_out, v_w_out, gpart("out", 128), (128, D)),
        "w_up": (w_up, m_w_up, v_w_up, gsum_mlp[:512].T, (D, 512)),
        "w_down": (w_down, m_w_down, v_w_down, gsum_mlp[512:], (512, D)),
    }
    big_out = {}
    for n, (w, mm_, vv, g, shp2) in big.items():
        dlt, mn, vn = _adamw(w.reshape(shp2), g, mm_.reshape(shp2), vv.reshape(shp2), name="adamw_" + n)
        big_out[n] = (g.reshape(w.shape), dlt.reshape(w.shape), mn.reshape(w.shape), vn.reshape(w.shape))

    order = ["w_ada", "b_ada", "norm_mix_w", "w_in", "conv_w", "conv_b", "dt_bias", "a_log", "d_skip",
             "ssd_norm_w", "w_branch_ssd", "pool_w", "pool_scale", "w_branch_pool", "w_out", "norm_mlp_w",
             "w_up", "w_down", "norm_final_w"]
    big_out["w_in"] = big_in
    res = {**small_out, **big_out}
    outs = [loss, grad_x.reshape(x.shape)]
    for k in range(4):
        outs += [res[n][k] for n in order]
    return tuple(outs)
```

```python
import functools

import numpy as np
import jax
import jax.numpy as jnp
from jax import lax
from jax.experimental import pallas as pl
from jax.experimental.pallas import tpu as pltpu

F32 = jnp.float32
BF16 = jnp.bfloat16
SLAB_DT = jnp.bfloat16
_MXU_DTYPE = jnp.bfloat16

N_DEV = 8
D = 1024
DI = 2048
NH = 32
HP = 64
NG = 4
NS = 128
Q = 128
XBC = DI + 2 * NG * NS
DFF = 4096
N_IN = 8224
EPS = 1e-5
POOL_W = 1024
PGW = 256

C_XBC, C_POOL, C_Z, C_GATE, C_DT = 0, 3072, 4096, 6144, 8192
DT_PAD = 256
NPROJ = C_DT + DT_PAD

IN_ROWS = N_IN // N_DEV
IN_ROWS_P = 1040
CONV_ROWS = 16
REST_PARTS = (("bssd", 256), ("pool", 32), ("bpool", 128), ("out", 128), ("up", 512), ("down", 512))
REST_OFF = {}
_o = 0
for _n, _r in REST_PARTS:
    REST_OFF[_n] = _o
    _o += _r
REST_ROWS = _o
MIX_PARTS = (("bssd", 256), ("pool", 32), ("bpool", 128), ("out", 128))
MIX_OFF = {}
_o = 0
for _n, _r in MIX_PARTS:
    MIX_OFF[_n] = _o
    _o += _r
MIX_ROWS = _o

SV_PARTS = (("b_ada", 6144), ("norm_mix_w", 1024), ("conv_b", 3072), ("dt_bias", 128), ("a_log", 128),
            ("d_skip", 128), ("ssd_norm_w", 2048), ("pool_scale", 1024), ("norm_mlp_w", 1024),
            ("norm_final_w", 1024), ("conv_w", 4 * XBC), ("loss", 128))
SV_OFF = {}
_o = 0
for _n, _r in SV_PARTS:
    SV_OFF[_n] = _o
    _o += _r
SV_ROWS = 224
assert _o <= SV_ROWS * 128

ADAM_LR, ADAM_B1, ADAM_B2, ADAM_EPS, ADAM_WD, ADAM_STEP = 0.001, 0.9, 0.999, 1e-08, 0.01, 10

VMEM_BIG = 56 * 1024 * 1024
NEG = -1e30

NN = ((1,), (0,))
NT = ((1,), (1,))
TN = ((0,), (0,))


def _dot(a, b, dims=NN):
    return lax.dot_general(a.astype(_MXU_DTYPE), b.astype(_MXU_DTYPE), (dims, ((), ())),
                           preferred_element_type=F32)


def _dot_hi(a, b, dims=NN):
    return lax.dot_general(a.astype(F32), b.astype(F32), (dims, ((), ())),
                           precision=lax.Precision.HIGHEST, preferred_element_type=F32)


def _pick(n, cands):
    for c in cands:
        if n % c == 0:
            return c
    return n


def _sigmoid(x):
    return 1.0 / (1.0 + jnp.exp(-x))


def _silu(x):
    return x * _sigmoid(x)


def _dsilu(x):
    s = _sigmoid(x)
    return s * (1.0 + x * (1.0 - s))


def _softplus(x):
    return jnp.maximum(x, 0.0) + jnp.log(1.0 + jnp.exp(-jnp.abs(x)))


def _params(sem, vmem=None):
    return pltpu.CompilerParams(dimension_semantics=sem, vmem_limit_bytes=vmem)


def _row_step():
    return pl.program_id(0)


def _mm(a, b, mode, *, name, outs, tm, tn, tk, extras=(), epilogue=None, aliases=None, prologue=None):
    if mode == "tn":
        K, M = a.shape
        N = b.shape[1]
        a_spec = pl.BlockSpec((tk, tm), lambda i, j, k: (k, i))
        b_spec = pl.BlockSpec((tk, tn), lambda i, j, k: (k, j))
        dims = TN
    else:
        M = a.shape[0]
        K = b.shape[0] if mode == "nn" else b.shape[1]
        if prologue is None:
            assert a.shape[1] == K
            a_spec = pl.BlockSpec((tm, tk), lambda i, j, k: (i, k))
        else:
            assert tk == K
            a_spec = pl.BlockSpec((tm, a.shape[1]), lambda i, j, k: (i, 0))
        if mode == "nn":
            N = b.shape[1]
            b_spec = pl.BlockSpec((tk, tn), lambda i, j, k: (k, j))
            dims = NN
        else:
            N = b.shape[0]
            b_spec = pl.BlockSpec((tn, tk), lambda i, j, k: (j, k))
            dims = NT
    assert M % tm == 0 and N % tn == 0 and K % tk == 0, (name, M, N, K, tm, tn, tk)
    nk = K // tk
    ne, no = len(extras), len(outs)
    if epilogue is None:
        def epilogue(acc, ex, out_refs):
            out_refs[0][...] = acc.astype(out_refs[0].dtype)

    def body(a_ref, b_ref, *rest):
        ex, out_refs = rest[:ne], rest[ne:ne + no]
        lhs = a_ref[...] if prologue is None else prologue(a_ref, ex, out_refs, pl.program_id(1))
        p = _dot(lhs, b_ref[...], dims)
        if nk == 1:
            epilogue(p, ex, out_refs)
        else:
            acc = rest[-1]
            k = pl.program_id(2)

            @pl.when(k == 0)
            def _():
                acc[...] = p

            @pl.when(jnp.logical_and(k > 0, k < nk - 1))
            def _():
                acc[...] += p

            @pl.when(k == nk - 1)
            def _():
                epilogue(acc[...] + p, ex, out_refs)

    out_specs, out_shape = [], []
    for o in outs:
        if isinstance(o, tuple):
            shape, dt, bs, im = o
            out_specs.append(pl.BlockSpec(bs, im))
            out_shape.append(jax.ShapeDtypeStruct(shape, dt))
        else:
            out_specs.append(pl.BlockSpec((tm, tn), lambda i, j, k: (i, j)))
            out_shape.append(jax.ShapeDtypeStruct((M, N), o))
    in_specs = [a_spec, b_spec]
    for _, bs, im in extras:
        in_specs.append(pl.BlockSpec(memory_space=pl.ANY) if bs is None else pl.BlockSpec(bs, im))
    res = pl.pallas_call(
        body, name=name,
        grid=(M // tm, N // tn, nk),
        in_specs=in_specs, out_specs=out_specs, out_shape=out_shape,
        scratch_shapes=[pltpu.VMEM((tm, tn), F32)] if nk > 1 else [],
        input_output_aliases={2 + e: o for e, o in (aliases or {}).items()},
        compiler_params=_params(("arbitrary", "arbitrary", "arbitrary"), VMEM_BIG),
    )(a, b, *[e[0] for e in extras])
    return res if no > 1 else res[0]


def _rows(tm, w=D, col=0):
    return (tm, w), lambda i, j, k, c=col: (i, c)


def _vecs(w=D, col=0):
    return (1, w), lambda i, j, k, c=col: (0, c)


def _sum_out(w=D):
    return ((1, w), F32, (1, w), lambda i, j, k: (0, 0))


def _mm_pool_tn(a, b, *, name, tk):
    L = a.shape[0]

    def body(a_ref, b_ref, o_ref):
        p = _dot(a_ref[...], b_ref[...], TN)

        @pl.when(pl.program_id(1) == 0)
        def _():
            o_ref[...] = p

        @pl.when(pl.program_id(1) > 0)
        def _():
            o_ref[...] += p

    blk = pl.BlockSpec((tk, PGW), lambda g, k: (k, g))
    return pl.pallas_call(body, name=name, grid=(4, L // tk), in_specs=[blk, blk],
                          out_specs=pl.BlockSpec((PGW, PGW), lambda g, k: (g, 0)),
                          out_shape=jax.ShapeDtypeStruct((POOL_W, PGW), F32),
                          compiler_params=_params(("parallel", "arbitrary")))(a, b)


def _acc_out(ref, val, i):
    @pl.when(i == 0)
    def _():
        ref[...] = val

    @pl.when(i > 0)
    def _():
        ref[...] += val


def _colsum(v):
    return jnp.sum(v, axis=0, keepdims=True)


def _ep_resid_norm(acc, ex, outs):
    x_ref, g_ref, nw_ref, sc_ref, sh_ref = ex
    mix_ref, x1_ref, h_ref = outs
    mix_ref[...] = acc.astype(mix_ref.dtype)
    xv = x_ref[...] + g_ref[...] * acc
    x1_ref[...] = xv
    r = lax.rsqrt(jnp.mean(xv * xv, axis=-1, keepdims=True) + EPS)
    h_ref[...] = (xv * r * nw_ref[...] * (1.0 + sc_ref[...]) + sh_ref[...]).astype(h_ref.dtype)


def _ep_final(acc, ex, outs):
    x1_ref, t_ref, g_ref, nw_ref = ex
    dx2_ref, dd_ref, loss_ref, dnw_ref, dg_ref = outs
    i = _row_step()
    x2 = x1_ref[...] + g_ref[...] * acc
    r = lax.rsqrt(jnp.mean(x2 * x2, axis=-1, keepdims=True) + EPS)
    xh = x2 * r
    e = xh * nw_ref[...] - t_ref[...]
    part = 0.5 * jnp.sum(jnp.mean(e * e, axis=-1, keepdims=True), axis=0, keepdims=True)
    dy = e * (1.0 / D)
    g = dy * nw_ref[...]
    dx2 = r * (g - xh * jnp.mean(g * xh, axis=-1, keepdims=True))
    dx2_ref[...] = dx2
    dd_ref[...] = (dx2 * g_ref[...]).astype(dd_ref.dtype)
    _acc_out(loss_ref, jnp.broadcast_to(part, (1, 128)), i)
    _acc_out(dnw_ref, _colsum(dy * xh), i)
    _acc_out(dg_ref, _colsum(dx2 * acc), i)


def _ep_norm_bwd(acc, ex, outs):
    x_ref, dr_ref, nw_ref, sc_ref = ex[:4]
    dx_ref, p_ref, q_ref = outs[:3]
    i = _row_step()
    xv = x_ref[...]
    r = lax.rsqrt(jnp.mean(xv * xv, axis=-1, keepdims=True) + EPS)
    xh = xv * r
    g = acc * (nw_ref[...] * (1.0 + sc_ref[...]))
    dx = dr_ref[...] + r * (g - xh * jnp.mean(g * xh, axis=-1, keepdims=True))
    dx_ref[...] = dx
    _acc_out(p_ref, _colsum(acc * xh), i)
    _acc_out(q_ref, _colsum(acc), i)
    if len(ex) > 4:
        m_ref, g_ref = ex[4:]
        dm_ref, dg_ref = outs[3:]
        dm_ref[...] = (dx * g_ref[...]).astype(dm_ref.dtype)
        _acc_out(dg_ref, _colsum(dx * m_ref[...].astype(F32)), i)


def _ep_merge_bwd(acc, ex, outs):
    a_ref, b_ref, gl_ref = ex
    da_ref, db_ref, dgl_ref = outs
    s = _sigmoid(gl_ref[...].astype(F32))
    s1, s2 = s[:, :D], s[:, D:]
    da_ref[...] = (acc * s1).astype(da_ref.dtype)
    db_ref[...] = (acc * s2).astype(db_ref.dtype)
    dgl_ref[:, :D] = (acc * a_ref[...].astype(F32) * s1 * (1.0 - s1)).astype(dgl_ref.dtype)
    dgl_ref[:, D:] = (acc * b_ref[...].astype(F32) * s2 * (1.0 - s2)).astype(dgl_ref.dtype)


GW = DI // NG


def _ep_gated_norm_bwd(acc, ex, outs):
    y_ref, z_ref, w_ref, _ = ex
    dy_ref, dz_ref, dw_ref = outs
    zv = z_ref[...].astype(F32)
    yv = y_ref[...].astype(F32)
    sg = _sigmoid(zv)
    sz = zv * sg
    yg = yv * sz
    dsz = sg * (1.0 + zv * (1.0 - sg))
    dws = []
    for k in range(NG):
        sl = slice(k * GW, (k + 1) * GW)
        seg = yg[:, sl]
        r = lax.rsqrt(jnp.mean(seg * seg, axis=-1, keepdims=True) + EPS)
        sh = seg * r
        dn = acc[:, sl]
        g = dn * w_ref[:, sl]
        dyg = r * (g - sh * jnp.mean(g * sh, axis=-1, keepdims=True))
        dy_ref[:, sl] = dyg * sz[:, sl]
        dz_ref[:, sl] = (dyg * yv[:, sl] * dsz[:, sl]).astype(dz_ref.dtype)
        dws.append(_colsum(dn * sh))
    _acc_out(dw_ref, jnp.concatenate(dws, axis=1), _row_step())


CONV_CB = 128
HALO = 16


def _time_chunk(L):
    return _pick(L, (256, 128))


def _with_halo(x_ref, i, r0, rc):
    p0 = pl.multiple_of(jnp.maximum(r0 - HALO, 0), HALO)
    prev = jnp.where(i > 0, x_ref[pl.ds(p0, HALO), :].astype(F32), 0.0)
    return jnp.concatenate([prev, x_ref[pl.ds(r0, rc), :].astype(F32)], axis=0)


def _conv_bwd(proj, dy, w, b, dproj, *, name):
    L = proj.shape[0]
    rc = _time_chunk(L)
    n = L // rc

    def body(x_ref, dy_ref, w_ref, b_ref, dp_in, dx_ref, dw_ref, db_ref, xpad, dpad):
        del dp_in
        wv = w_ref[...]
        bv = b_ref[...]
        dpad[rc:rc + HALO, :] = jnp.zeros((HALO, CONV_CB), F32)

        def step(k, carry):
            db, d0, d1, d2, d3 = carry
            i = n - 1 - k
            r0 = pl.multiple_of(i * rc, rc)
            p0 = pl.multiple_of(jnp.maximum(r0 - HALO, 0), HALO)
            xpad[0:HALO, :] = jnp.where(i > 0, x_ref[pl.ds(p0, HALO), :].astype(F32), 0.0)
            xpad[HALO:HALO + rc, :] = x_ref[pl.ds(r0, rc), :].astype(F32)
            xk = [xpad[HALO - j:HALO - j + rc, :] for j in range(4)]
            pre = bv
            for j in range(4):
                pre = pre + xk[j] * wv[3 - j:4 - j]
            dpre = dy_ref[pl.ds(r0, rc), :] * _dsilu(pre)
            dpad[0:rc, :] = dpre
            acc = dpre * wv[3:4]
            for j in (1, 2, 3):
                acc = acc + dpad[j:j + rc, :] * wv[3 - j:4 - j]
            dx_ref[pl.ds(r0, rc), :] = acc.astype(dx_ref.dtype)
            dpad[rc:rc + HALO, :] = dpre[:HALO]
            return (db + _colsum(dpre), d0 + _colsum(dpre * xk[3]), d1 + _colsum(dpre * xk[2]),
                    d2 + _colsum(dpre * xk[1]), d3 + _colsum(dpre * xk[0]))

        z = jnp.zeros((1, CONV_CB), F32)
        db, d0, d1, d2, d3 = lax.fori_loop(0, n, step, (z, z, z, z, z))
        db_ref[...] = db
        dw_ref[...] = jnp.concatenate([d0, d1, d2, d3], axis=0)

    nb = XBC // CONV_CB
    return pl.pallas_call(
        body, name=name, grid=(nb,),
        in_specs=[pl.BlockSpec((L, CONV_CB), lambda j: (0, j + C_XBC // CONV_CB)),
                  pl.BlockSpec((L, CONV_CB), lambda j: (0, j)),
                  pl.BlockSpec((4, CONV_CB), lambda j: (0, j)), pl.BlockSpec((1, CONV_CB), lambda j: (0, j)),
                  pl.BlockSpec(memory_space=pl.ANY)],
        out_specs=[pl.BlockSpec((L, CONV_CB), lambda j: (0, j + C_XBC // CONV_CB)),
                   pl.BlockSpec((4, CONV_CB), lambda j: (0, j)), pl.BlockSpec((1, CONV_CB), lambda j: (0, j))],
        out_shape=[jax.ShapeDtypeStruct((L, NPROJ), BF16), jax.ShapeDtypeStruct((4, XBC), F32),
                   jax.ShapeDtypeStruct((1, XBC), F32)],
        scratch_shapes=[pltpu.VMEM((rc + HALO, CONV_CB), F32), pltpu.VMEM((rc + HALO, CONV_CB), F32)],
        input_output_aliases={4: 0},
        compiler_params=_params(("parallel",), VMEM_BIG))(proj, dy, w, b, dproj)


def _pool_fwd(proj, *, name):
    L = proj.shape[0]
    rc = _time_chunk(L)
    n = L // rc

    def body(x_ref, o_ref, pad):
        g = pl.program_id(0)
        pad[0:HALO, :] = jnp.zeros((HALO, PGW), F32)

        def fill(i, c):
            r0 = pl.multiple_of(i * rc, rc)
            pad[pl.ds(r0 + HALO, rc), :] = x_ref[pl.ds(r0, rc), :].astype(F32)
            return c

        lax.fori_loop(0, n, fill, 0)
        rows = lax.broadcasted_iota(jnp.int32, (rc, PGW), 0)

        for gi in range(4):
            win = 2 << gi

            @pl.when(g == gi)
            def _(gi=gi, win=win):
                def step(i, c):
                    r0 = pl.multiple_of(i * rc, rc)
                    ext = pad[pl.ds(r0, rc + HALO), :]
                    s = ext
                    sh = 1
                    while sh < win:
                        s = s + pltpu.roll(s, sh, 0)
                        sh *= 2
                    cnt = jnp.minimum(rows + (r0 + 1), win).astype(F32)
                    o_ref[pl.ds(r0, rc), :] = (s[HALO:] / cnt - ext[HALO:]).astype(o_ref.dtype)
                    return c

                lax.fori_loop(0, n, step, 0)

    return pl.pallas_call(
        body, name=name, grid=(4,),
        in_specs=[pl.BlockSpec((L, PGW), lambda j: (0, j + C_POOL // PGW))],
        out_specs=pl.BlockSpec((L, PGW), lambda j: (0, j)),
        out_shape=jax.ShapeDtypeStruct((L, POOL_W), BF16),
        scratch_shapes=[pltpu.VMEM((L + HALO, PGW), F32)],
        compiler_params=_params(("parallel",), VMEM_BIG))(proj)


def _pool_bwd(dpooled, dproj, *, name):
    L = dpooled.shape[0]
    rc = _time_chunk(L)
    n = L // rc

    def body(d_ref, dp_in, o_ref, pad):
        del dp_in
        g = pl.program_id(0)
        pad[L:L + HALO, :] = jnp.zeros((HALO, PGW), F32)
        rows = lax.broadcasted_iota(jnp.int32, (rc, PGW), 0)

        for gi in range(4):
            win = 2 << gi

            @pl.when(g == gi)
            def _(gi=gi, win=win):
                def fill(i, c):
                    r0 = pl.multiple_of(i * rc, rc)
                    cnt = jnp.minimum(rows + (r0 + 1), win).astype(F32)
                    pad[pl.ds(r0, rc), :] = d_ref[pl.ds(r0, rc), :] / cnt
                    return c

                lax.fori_loop(0, n, fill, 0)

                def step(i, c):
                    r0 = pl.multiple_of(i * rc, rc)
                    s = pad[pl.ds(r0, rc + HALO), :]
                    sh = 1
                    while sh < win:
                        s = s + pltpu.roll(s, rc + HALO - sh, 0)
                        sh *= 2
                    o_ref[pl.ds(r0, rc), :] = (s[:rc] - d_ref[pl.ds(r0, rc), :]).astype(o_ref.dtype)
                    return c

                lax.fori_loop(0, n, step, 0)

    return pl.pallas_call(
        body, name=name, grid=(4,),
        in_specs=[pl.BlockSpec((L, PGW), lambda j: (0, j)), pl.BlockSpec(memory_space=pl.ANY)],
        out_specs=pl.BlockSpec((L, PGW), lambda j: (0, j + C_POOL // PGW)),
        out_shape=jax.ShapeDtypeStruct((L, NPROJ), BF16),
        scratch_shapes=[pltpu.VMEM((L + HALO, PGW), F32)],
        input_output_aliases={1: 0},
        compiler_params=_params(("parallel",), VMEM_BIG))(dpooled, dproj)


_SPLIT_DT = jnp.bfloat16


def _ssd_consts():
    tri = np.tril(np.ones((Q, Q), np.float32))
    exp = np.zeros((128, DI), np.float32)
    for h in range(NH):
        exp[h, h * HP:(h + 1) * HP] = 1.0
    exp2 = np.concatenate([exp, exp], axis=0)
    return (jnp.asarray(tri, dtype=_SPLIT_DT), jnp.asarray(tri.T.copy(), dtype=_SPLIT_DT),
            jnp.asarray(exp2, dtype=_SPLIT_DT))


def _split(v, n):
    parts, r = [], v
    for _ in range(n):
        p = r.astype(_SPLIT_DT)
        parts.append(p)
        r = r - p.astype(F32)
    return parts


def _bdot(a, b, dims):
    return lax.dot_general(a, b, (dims, ((), ())), preferred_element_type=F32)


def _tri_sum(t_ref, v):
    r = _bdot(t_ref[...], jnp.concatenate(_split(v, 3), axis=1), NN)
    return r[:, :128] + r[:, 128:256] + r[:, 256:]


def _expand(v, e2_ref):
    return _bdot(jnp.concatenate(_split(v, 2), axis=1), e2_ref[...], NN)


def _reduce_heads(vals, eg):
    parts = []
    for v in vals:
        parts += _split(v, 2)
    r = _bdot(jnp.concatenate(parts, axis=0), eg, NT)
    return [r[2 * i * Q:(2 * i + 1) * Q] + r[(2 * i + 1) * Q:(2 * i + 2) * Q] for i in range(len(vals))]


def _ssd_common(xbc_ref, dtw_ref, dtb_ref, arow_ref, t_ref, e_ref):
    pre = dtw_ref[:, :128] + dtb_ref[...]
    dt = _softplus(pre)
    acs = _tri_sum(t_ref, dt * arow_ref[...])
    acs_x = _expand(acs, e_ref)
    dt_x = _expand(dt, e_ref)
    xs = xbc_ref[:, 0:DI]
    return pre, dt, acs, acs.T, acs_x, dt_x, xs


CONV_SLAB = 512


def _ssd_fwd(raw, dtp, cw, cb, dtb, arow, dsk_x, *, name):
    L = raw.shape[0]
    nc = L // Q
    tri, _, expand = _ssd_consts()

    def body(raw_ref, halo_ref, cw_ref, cb_ref, dtw_ref, dtb_ref, arow_ref, dsk_ref, t_ref, e_ref,
             y_ref, hs_ref, xbc_ref, h_scr, cpad):
        c = pl.program_id(0)

        @pl.when(c == 0)
        def _():
            h_scr[...] = jnp.zeros_like(h_scr)

        cpad[0:8, :] = jnp.where(c > 0, halo_ref[...], 0.0)
        cpad[8:8 + Q, :] = raw_ref[...]
        for lo in range(0, XBC, CONV_SLAB):
            sl = slice(lo, lo + CONV_SLAB)
            acc = cb_ref[:, sl]
            for j in range(4):
                acc = acc + cpad[8 - j:8 - j + Q, sl] * cw_ref[3 - j:4 - j, sl]
            xbc_ref[:, sl] = acc * _sigmoid(acc)

        _, dt, acs, acs_t, acs_x, dt_x, xs = _ssd_common(xbc_ref, dtw_ref, dtb_ref, arow_ref, t_ref, e_ref)
        xdt = xs * dt_x
        eacs = jnp.exp(acs_x)
        acs_last = acs_x[Q - 1:Q, :]
        dec = jnp.exp(acs_last - acs_x)
        hs_ref[0] = h_scr[...].astype(hs_ref.dtype)
        causal = lax.broadcasted_iota(jnp.int32, (Q, Q), 0) >= lax.broadcasted_iota(jnp.int32, (Q, Q), 1)
        first = lax.broadcasted_iota(jnp.int32, (Q, 128), 1) < HP
        for g in range(NG):
            bg = xbc_ref[:, DI + g * NS:DI + (g + 1) * NS]
            cg = xbc_ref[:, DI + NG * NS + g * NS:DI + NG * NS + (g + 1) * NS]
            s = _dot(cg, bg, NT)
            sl = slice(g * GW, (g + 1) * GW)
            hg = h_scr[:, sl]
            yoff = _dot(cg, hg, NN) * eacs[:, sl]
            st = _dot(bg, xdt[:, sl] * dec[:, sl], TN)
            h_scr[:, sl] = hg * eacs[Q - 1:Q, sl] + st
            for j in range(4):
                lo = g * GW + j * 128
                xb = xdt[:, lo:lo + 128]
                yp = yoff[:, j * 128:(j + 1) * 128] + dsk_ref[:, lo:lo + 128] * xs[:, lo:lo + 128]
                for e in range(2):
                    h = g * 8 + j * 2 + e
                    lm = jnp.exp(jnp.where(causal, acs[:, h:h + 1] - acs_t[h:h + 1, :], NEG))
                    xm = jnp.where(first if e == 0 else jnp.logical_not(first), xb, 0.0)
                    yp = yp + _dot(s * lm, xm, NN)
                y_ref[:, lo:lo + 128] = yp.astype(y_ref.dtype)

    const = lambda c: (0, 0)
    return pl.pallas_call(
        body, name=name, grid=(nc,),
        in_specs=[pl.BlockSpec((Q, XBC), lambda c: (c, 0)),
                  pl.BlockSpec((8, XBC), lambda c: (jnp.maximum(c * (Q // 8) - 1, 0), 0)),
                  pl.BlockSpec((4, XBC), const), pl.BlockSpec((1, XBC), const),
                  pl.BlockSpec((Q, DT_PAD), lambda c: (c, 0)),
                  pl.BlockSpec((1, 128), const), pl.BlockSpec((1, 128), const), pl.BlockSpec((1, DI), const),
                  pl.BlockSpec((Q, Q), const), pl.BlockSpec((256, DI), const)],
        out_specs=[pl.BlockSpec((Q, DI), lambda c: (c, 0)), pl.BlockSpec((1, NS, DI), lambda c: (c, 0, 0)),
                   pl.BlockSpec((Q, XBC), lambda c: (c, 0))],
        out_shape=[jax.ShapeDtypeStruct((L, DI), BF16), jax.ShapeDtypeStruct((nc, NS, DI), F32),
                   jax.ShapeDtypeStruct((L, XBC), F32)],
        scratch_shapes=[pltpu.VMEM((NS, DI), F32), pltpu.VMEM((8 + Q, XBC), F32)],
        compiler_params=_params(("arbitrary",), VMEM_BIG))(raw, raw, cw, cb, dtp, dtb, arow, dsk_x, tri, expand)


def _ssd_bwd(dy, xbc, proj, hs, dtb, arow, dsk_x, dproj, *, name):
    L = xbc.shape[0]
    nc = L // Q
    tri, triu, expand = _ssd_consts()

    def body(dy_ref, xbc_ref, dtw_ref, hs_ref, dtb_ref, arow_ref, dsk_ref, t_ref, u_ref, e_ref, dp_in,
             dxbc_ref, ddtw_ref, da_ref, ddx_ref, ddtb_ref, dh_scr):
        del dp_in
        i = pl.program_id(0)

        @pl.when(i == 0)
        def _():
            dh_scr[...] = jnp.zeros_like(dh_scr)

        pre, dt, acs, acs_t, acs_x, dt_x, xs = _ssd_common(xbc_ref, dtw_ref, dtb_ref, arow_ref, t_ref, e_ref)
        dyv = dy_ref[...]
        xdt = xs * dt_x
        eacs = jnp.exp(acs_x)
        acs_last = acs_x[Q - 1:Q, :]
        dec = jnp.exp(acs_last - acs_x)
        gy = dyv * eacs
        causal = lax.broadcasted_iota(jnp.int32, (Q, Q), 0) >= lax.broadcasted_iota(jnp.int32, (Q, Q), 1)
        first = lax.broadcasted_iota(jnp.int32, (Q, 128), 1) < HP
        lane_h = lax.broadcasted_iota(jnp.int32, (Q, 128), 1)
        sub_h = lax.broadcasted_iota(jnp.int32, (128, Q), 0)
        last_row = lax.broadcasted_iota(jnp.int32, (Q, GW), 0) == Q - 1
        dacs = jnp.zeros((Q, 128), F32)
        dacs_t = jnp.zeros((128, Q), F32)
        ddt = jnp.zeros((Q, 128), F32)
        for g in range(NG):
            bg = xbc_ref[:, DI + g * NS:DI + (g + 1) * NS]
            cg = xbc_ref[:, DI + NG * NS + g * NS:DI + NG * NS + (g + 1) * NS]
            s = _dot(cg, bg, NT)
            sl = slice(g * GW, (g + 1) * GW)
            hg = hs_ref[0, :, sl].astype(F32)
            dhn = dh_scr[:, sl]
            eal = eacs[Q - 1:Q, sl]
            gg = gy[:, sl]
            dax = gg * _dot(cg, hg, NN)
            dcg = _dot(gg, hg, NT)
            dh_scr[:, sl] = _dot(cg, gg, TN) + dhn * eal
            dal = eal * _colsum(dhn * hg)
            xdd = xdt[:, sl] * dec[:, sl]
            dbg = _dot(xdd, dhn, NT)
            wv = _dot(bg, dhn, NN)
            dd = wv * xdd
            dax = dax - dd
            dal = dal + _colsum(dd)
            dax = dax + jnp.where(last_row, dal, 0.0)
            dxdt_g = wv * dec[:, sl]
            ds = jnp.zeros((Q, Q), F32)
            dxdt_blocks = []
            for j in range(4):
                lo = g * GW + j * 128
                xb = xdt[:, lo:lo + 128]
                dyb = dyv[:, lo:lo + 128]
                dxb = dxdt_g[:, j * 128:(j + 1) * 128]
                for e in range(2):
                    h = g * 8 + j * 2 + e
                    lm = jnp.exp(jnp.where(causal, acs[:, h:h + 1] - acs_t[h:h + 1, :], NEG))
                    m = s * lm
                    dym = jnp.where(first if e == 0 else jnp.logical_not(first), dyb, 0.0)
                    dm = _dot(dym, xb, NT)
                    r = dm * m
                    dacs = dacs + jnp.where(lane_h == h, jnp.sum(r, axis=1, keepdims=True), 0.0)
                    dacs_t = dacs_t + jnp.where(sub_h == h, _colsum(r), 0.0)
                    ds = ds + dm * lm
                    dxb = dxb + _dot(m, dym, TN)
                dxdt_blocks.append(dxb)
            dxdt = jnp.concatenate(dxdt_blocks, axis=1)
            dcg = dcg + _dot(ds, bg, NN)
            dbg = dbg + _dot(ds, cg, TN)
            dxbc_ref[:, DI + g * NS:DI + (g + 1) * NS] = dbg
            dxbc_ref[:, DI + NG * NS + g * NS:DI + NG * NS + (g + 1) * NS] = dcg
            dxbc_ref[:, sl] = dsk_ref[:, sl] * dyv[:, sl] + dxdt * dt_x[:, sl]
            ddt_g, dacs_g = _reduce_heads([dxdt * xs[:, sl], dax], e_ref[0:128, sl])
            ddt = ddt + ddt_g
            dacs = dacs + dacs_g
        dacs = dacs - dacs_t.T
        ddta = _tri_sum(u_ref, dacs)
        ddt = ddt + ddta * arow_ref[...]
        ddtw = jnp.where(lane_h < NH, ddt * _sigmoid(pre), 0.0)
        ddtw_ref[...] = jnp.concatenate([ddtw, jnp.zeros((Q, DT_PAD - 128), F32)], axis=1).astype(ddtw_ref.dtype)
        _acc_out(da_ref, _colsum(ddta * dt), i)
        _acc_out(ddx_ref, _colsum(dyv * xs), i)
        _acc_out(ddtb_ref, _colsum(ddtw), i)

    rev = lambda c: (nc - 1 - c, 0)
    const = lambda c: (0, 0)
    return pl.pallas_call(
        body, name=name, grid=(nc,),
        in_specs=[pl.BlockSpec((Q, DI), rev), pl.BlockSpec((Q, XBC), rev),
                  pl.BlockSpec((Q, DT_PAD), rev),
                  pl.BlockSpec((1, NS, DI), lambda c: (nc - 1 - c, 0, 0)),
                  pl.BlockSpec((1, 128), const), pl.BlockSpec((1, 128), const), pl.BlockSpec((1, DI), const),
                  pl.BlockSpec((Q, Q), const), pl.BlockSpec((Q, Q), const), pl.BlockSpec((256, DI), const),
                  pl.BlockSpec(memory_space=pl.ANY)],
        out_specs=[pl.BlockSpec((Q, XBC), rev),
                   pl.BlockSpec((Q, DT_PAD), lambda c: (nc - 1 - c, C_DT // DT_PAD)),
                   pl.BlockSpec((1, 128), const), pl.BlockSpec((1, DI), const), pl.BlockSpec((1, 128), const)],
        out_shape=[jax.ShapeDtypeStruct((L, XBC), F32), jax.ShapeDtypeStruct((L, NPROJ), BF16),
                   jax.ShapeDtypeStruct((1, 128), F32), jax.ShapeDtypeStruct((1, DI), F32),
                   jax.ShapeDtypeStruct((1, 128), F32)],
        scratch_shapes=[pltpu.VMEM((NS, DI), F32)],
        input_output_aliases={10: 1},
        compiler_params=_params(("arbitrary",), VMEM_BIG))(dy, xbc, proj, hs, dtb, arow, dsk_x, tri, triu,
                                                          expand, dproj)


def _adam_update(wv, gv, mv, vv):
    c1 = 1.0 - ADAM_B1 ** ADAM_STEP
    c2 = 1.0 - ADAM_B2 ** ADAM_STEP
    mn = ADAM_B1 * mv + (1.0 - ADAM_B1) * gv
    vn = ADAM_B2 * vv + (1.0 - ADAM_B2) * (gv * gv)
    return -ADAM_LR * ((mn / c1) / (jnp.sqrt(vn / c2) + ADAM_EPS) + ADAM_WD * wv), mn, vn


def _adamw(w, g, m, v, *, name, tr=None):
    R = w.shape[0]
    rest = tuple(w.shape[1:])
    if tr is None:
        tr = _pick(R, (256, 128, 64, 32, 16, 8))
    assert R % tr == 0

    def body(w_ref, g_ref, m_ref, v_ref, d_ref, mo_ref, vo_ref):
        d_ref[...], mo_ref[...], vo_ref[...] = _adam_update(w_ref[...], g_ref[...], m_ref[...], v_ref[...])

    zeros = (0,) * len(rest)
    spec = pl.BlockSpec((tr,) + rest, lambda i: (i,) + zeros)
    return pl.pallas_call(body, name=name, grid=(R // tr,), in_specs=[spec] * 4, out_specs=[spec] * 3,
                          out_shape=[jax.ShapeDtypeStruct(w.shape, F32)] * 3,
                          compiler_params=_params(("parallel",)))(w, g, m, v)


def _adamw_small(svrow, g_conv, params, *, name):
    n = len(params)

    def body(*refs):
        sv_ref, gc_ref = refs[0], refs[1]
        ins, outs = refs[2:2 + 3 * n], refs[2 + 3 * n:]
        for p, (key, w, _, _) in enumerate(params):
            w_ref, m_ref, v_ref = ins[3 * p:3 * p + 3]
            g_ref, d_ref, mo_ref, vo_ref = outs[4 * p:4 * p + 4]
            gv = gc_ref[...] if key == "conv_w" else sv_ref[:, SV_OFF[key]:SV_OFF[key] + w.shape[1]]
            g_ref[...] = gv
            d_ref[...], mo_ref[...], vo_ref[...] = _adam_update(w_ref[...], gv, m_ref[...], v_ref[...])

    vm = pl.BlockSpec(memory_space=pltpu.VMEM)
    args = [svrow, g_conv]
    shapes = []
    for _, w, m, v in params:
        args += [w, m, v]
        shapes += [jax.ShapeDtypeStruct(w.shape, F32)] * 4
    res = pl.pallas_call(body, name=name, in_specs=[vm] * len(args), out_specs=[vm] * len(shapes),
                         out_shape=shapes)(*args)
    return {key: tuple(res[4 * p:4 * p + 4]) for p, (key, _, _, _) in enumerate(params)}


def _slab_sum(recv, *, tile, name):
    rows = recv.shape[1]
    assert rows % tile == 0 and tile % 16 == 0

    def body(r_ref, o_ref):
        acc = r_ref[0].astype(F32)
        for j in range(1, N_DEV):
            acc = acc + r_ref[j].astype(F32)
        o_ref[...] = acc

    return pl.pallas_call(body, name=name, grid=(rows // tile,),
                          in_specs=[pl.BlockSpec((N_DEV, tile, D), lambda i: (0, i, 0))],
                          out_specs=pl.BlockSpec((tile, D), lambda i: (i, 0)),
                          out_shape=jax.ShapeDtypeStruct((rows, D), F32),
                          compiler_params=_params(("parallel",)))(recv)


MESH = pl.DeviceIdType.MESH


def _coords():
    return lax.axis_index("x"), lax.axis_index("y"), lax.axis_index("c")


def _peer(k):
    x, y, c = _coords()
    px = 1 - x if k & 4 else x
    py = 1 - y if k & 2 else y
    pc = 1 - c if k & 1 else c
    return (px, py, pc), 4 * px + 2 * py + pc


def _rcopy(src, dst, ssem, rsem, dev):
    return pltpu.make_async_remote_copy(src_ref=src, dst_ref=dst, send_sem=ssem, recv_sem=rsem,
                                        device_id=dev, device_id_type=MESH)


def _exchange_all(src_of, dst_slot, send_sems, recv_sems):
    x, y, c = _coords()
    me = 4 * x + 2 * y + c
    sent = []
    for k in range(1, N_DEV):
        dev, pidx = _peer(k)
        cp = _rcopy(src_of(pidx), dst_slot(me), send_sems.at[k - 1], recv_sems.at[k - 1], dev)
        cp.start()
        sent.append(cp)
    for k in range(1, N_DEV):
        dev, pidx = _peer(k)
        _rcopy(src_of(pidx), dst_slot(pidx), send_sems.at[k - 1], recv_sems.at[k - 1], dev).wait_recv()
    for cp in sent:
        cp.wait_send()


def _rows_of_slots(buf, nslots):
    rows = lax.broadcasted_iota(jnp.int32, (8, buf.shape[-1]), 0)
    out = jnp.zeros((8, buf.shape[-1]), F32)
    for j in range(nslots):
        out = out + jnp.where(rows == j, buf[j], 0.0)
    return out


def _ada_fwd(c, w_ada, b_r, *, name):
    wloc = w_ada.shape[1]

    def body(c_ref, w_ref, b_ref, mod_ref, call_ref, csrc, cbuf, psrc, pbuf, s1, r1, s2, r2):
        x, y, cc = _coords()
        me = 4 * x + 2 * y + cc
        csrc[...] = jnp.broadcast_to(c_ref[...], (8, D))
        cbuf[me] = csrc[...]
        _exchange_all(lambda p: csrc, lambda s: cbuf.at[s], s1, r1)
        call = _rows_of_slots(cbuf, N_DEV)
        call_ref[...] = call
        prod = _dot_hi(_silu(call), w_ref[...])
        for b in range(N_DEV):
            psrc[b] = jnp.broadcast_to(prod[b:b + 1, :], (8, wloc))
        pbuf[me] = psrc[me]
        _exchange_all(lambda p: psrc.at[p], lambda s: pbuf.at[s], s2, r2)
        mod_ref[...] = _rows_of_slots(pbuf, N_DEV) + b_ref[...]

    vm = pl.BlockSpec(memory_space=pltpu.VMEM)
    return pl.pallas_call(
        body, name=name, in_specs=[vm, vm, vm], out_specs=[vm, vm],
        out_shape=[jax.ShapeDtypeStruct((N_DEV, wloc), F32), jax.ShapeDtypeStruct((N_DEV, D), F32)],
        scratch_shapes=[pltpu.VMEM((8, D), F32), pltpu.VMEM((N_DEV, 8, D), F32),
                        pltpu.VMEM((N_DEV, 8, wloc), F32), pltpu.VMEM((N_DEV, 8, wloc), F32),
                        pltpu.SemaphoreType.DMA((N_DEV - 1,)), pltpu.SemaphoreType.DMA((N_DEV - 1,)),
                        pltpu.SemaphoreType.DMA((N_DEV - 1,)), pltpu.SemaphoreType.DMA((N_DEV - 1,))],
        compiler_params=pltpu.CompilerParams(vmem_limit_bytes=VMEM_BIG))(c, w_ada, b_r)


def _gather_slabs(slab, *, name):
    def body(x_ref, out_ref, send_sems, recv_sems, local_sem):
        x, y, c = _coords()
        me, sibling = (x, y, c), (x, y, 1 - c)
        chips = [(1 - x, y), (x, 1 - y), (1 - x, 1 - y)]

        def slot(px, py, pc):
            return out_ref.at[4 * px + 2 * py + pc]

        def copy(k, block, to, src=None):
            return _rcopy(slot(*block) if src is None else src, slot(*block), send_sems.at[k], recv_sems.at[k], to)

        mine = pltpu.make_async_copy(x_ref, slot(*me), local_sem)
        mine.start()
        first = [copy(0, me, sibling, src=x_ref)]
        first += [copy(1 + j, me, (*chip, c), src=x_ref) for j, chip in enumerate(chips)]
        for cp in first:
            cp.start()
        passed = [copy(4 + j, (*chip, c), sibling) for j, chip in enumerate(chips)]
        for j, chip in enumerate(chips):
            copy(1 + j, (*chip, c), me).wait_recv()
            passed[j].start()
        copy(0, sibling, me).wait_recv()
        for j, chip in enumerate(chips):
            copy(4 + j, (*chip, 1 - c), me).wait_recv()
        for cp in first + passed:
            cp.wait_send()
        mine.wait()

    anyspec = pl.BlockSpec(memory_space=pl.ANY)
    return pl.pallas_call(
        body, name=name, in_specs=[anyspec], out_specs=anyspec,
        out_shape=jax.ShapeDtypeStruct((N_DEV,) + slab.shape, slab.dtype),
        scratch_shapes=[pltpu.SemaphoreType.DMA((7,)), pltpu.SemaphoreType.DMA((7,)), pltpu.SemaphoreType.DMA],
    )(slab)


_HBM =pl.BlockSpec(memory_space=pltpu.HBM)
_SEM = pl.BlockSpec(memory_space=pltpu.SEMAPHORE)
_EFFECT = pltpu.SideEffectType.DATAFLOW_SIDE_EFFECTING


def _xchg_src(src_ref, pidx, per_peer):
    return src_ref.at[pidx] if per_peer else src_ref


def _xchg_start(src, *, per_peer, name):
    rows = src.shape[-2]
    land_shape = (N_DEV, rows, D)

    def body(src_ref, land_ref, send_sems, recv_sems, src_thru, land_thru, token):
        del src_thru, land_thru
        x, y, c = _coords()
        me = 4 * x + 2 * y + c
        for k in range(1, N_DEV):
            dev, pidx = _peer(k)
            _rcopy(_xchg_src(src_ref, pidx, per_peer), land_ref.at[me], send_sems.at[k - 1],
                   recv_sems.at[k - 1], dev).start()
        token[...] = jnp.zeros_like(token)

    return pl.pallas_call(
        body, name=name,
        out_shape=(pltpu.SemaphoreType.DMA((N_DEV - 1,)), pltpu.SemaphoreType.DMA((N_DEV - 1,)),
                   pltpu.HBM(src.shape, src.dtype), pltpu.HBM(land_shape, src.dtype),
                   jax.ShapeDtypeStruct((8, 128), F32)),
        in_specs=(_HBM, _HBM),
        out_specs=(_SEM, _SEM, _HBM, _HBM, pl.BlockSpec(memory_space=pltpu.VMEM)),
        input_output_aliases={0: 2, 1: 3},
        compiler_params=pltpu.CompilerParams(has_side_effects=_EFFECT),
    )(pltpu.with_memory_space_constraint(src, pltpu.HBM),
      pltpu.with_memory_space_constraint(lax.empty(land_shape, src.dtype), pltpu.HBM))


def _xchg_wait(started, after, *, per_peer, name):
    send_sems, recv_sems, src_thru, land_thru, _ = started

    def body(src_ref, land_ref, send_sems, recv_sems, after_ref, src_dead, got_ref):
        del after_ref, src_dead, got_ref
        for k in range(1, N_DEV):
            dev, pidx = _peer(k)
            cp = _rcopy(_xchg_src(src_ref, pidx, per_peer), land_ref.at[pidx], send_sems.at[k - 1],
                        recv_sems.at[k - 1], dev)
            cp.wait_send()
            cp.wait_recv()

    return pl.pallas_call(
        body, name=name,
        out_shape=(pltpu.HBM(src_thru.shape, src_thru.dtype), pltpu.HBM(land_thru.shape, land_thru.dtype)),
        in_specs=(_HBM, _HBM, _SEM, _SEM, pl.BlockSpec(memory_space=pl.ANY)),
        out_specs=(_HBM, _HBM),
        input_output_aliases={0: 0, 1: 1},
        compiler_params=pltpu.CompilerParams(has_side_effects=_EFFECT),
    )(src_thru, land_thru, send_sems, recv_sems, after)


def _dep(token):
    return (token, (8, 128), lambda i, j, k: (0, 0))


def _small_allsum(sv, *, name):
    def body(sv_ref, all_ref, sum_ref, send_sems, recv_sems):
        x, y, c = _coords()
        me = 4 * x + 2 * y + c
        all_ref[me] = sv_ref[...]
        _exchange_all(lambda p: sv_ref, lambda s: all_ref.at[s], send_sems, recv_sems)
        acc = all_ref[0]
        for j in range(1, N_DEV):
            acc = acc + all_ref[j]
        sum_ref[...] = acc

    vm = pl.BlockSpec(memory_space=pltpu.VMEM)
    return pl.pallas_call(
        body, name=name, in_specs=[vm], out_specs=[vm, vm],
        out_shape=[jax.ShapeDtypeStruct((N_DEV, SV_ROWS, 128), F32), jax.ShapeDtypeStruct((SV_ROWS, 128), F32)],
        scratch_shapes=[pltpu.SemaphoreType.DMA((7,)), pltpu.SemaphoreType.DMA((7,))],
    )(sv)


def _ada_bwd(call, dmod_loc, *, name):
    wloc = dmod_loc.shape[1]

    def body(c_ref, d_ref, o_ref):
        o_ref[...] = _dot_hi(_silu(c_ref[...]), d_ref[...], TN)

    vm = pl.BlockSpec(memory_space=pltpu.VMEM)
    return pl.pallas_call(body, name=name, in_specs=[vm, vm], out_specs=vm,
                          out_shape=jax.ShapeDtypeStruct((D, wloc), F32),
                          compiler_params=pltpu.CompilerParams(vmem_limit_bytes=VMEM_BIG))(call, dmod_loc)


def _pad_rows(a, rows):
    return jnp.pad(a, ((0, rows - a.shape[0]), (0, 0)))


IN_SHIFT = tuple((IN_ROWS * j) % 16 for j in range(N_DEV))
IN_BASE = tuple(IN_ROWS * j - IN_SHIFT[j] for j in range(N_DEV))
IN_SEGMENTS = ((2048, XBC, C_XBC), (5152, 1024, C_POOL), (0, 2048, C_Z), (6176, 2048, C_GATE), (5120, 32, C_DT))


def _global_pieces(gs):
    pieces = []
    for j in range(N_DEV):
        lo, hi = 0, IN_ROWS_P
        if j > 0 and IN_BASE[j - 1] + IN_ROWS_P > IN_BASE[j]:
            pieces.append((IN_BASE[j], 16, gs[j - 1, IN_ROWS_P - 16:IN_ROWS_P] + gs[j, 0:16]))
            lo = 16
        if j + 1 < N_DEV and IN_BASE[j] + IN_ROWS_P > IN_BASE[j + 1]:
            hi = IN_ROWS_P - 16
        pieces.append((IN_BASE[j] + lo, hi - lo, gs[j, lo:hi]))
    return pieces


def _reorder_in_rows(gs):
    pieces = _global_pieces(gs)
    parts = []
    for lo, n, _ in IN_SEGMENTS:
        for p0, pn, arr in pieces:
            a, b = max(lo, p0), min(lo + n, p0 + pn)
            if a < b:
                parts.append(arr[a - p0:b - p0])
    parts.append(jnp.zeros((DT_PAD - 32, D), gs.dtype))
    return jnp.concatenate(parts, axis=0)


def _restore_in_shards(d):
    slabs = []
    for j in range(N_DEV):
        parts = []
        r, end = IN_BASE[j], IN_BASE[j] + IN_ROWS_P
        while r < end:
            lo, n, new = next(s for s in IN_SEGMENTS if s[0] <= r < s[0] + s[1])
            e = min(end, lo + n)
            parts.append(d[new + r - lo:new + e - lo])
            r = e
        slabs.append(jnp.concatenate(parts, axis=0))
    return jnp.stack(slabs, axis=0)


def _pack_sv(parts):
    flat = []
    for n, size in SV_PARTS:
        v = parts[n].reshape(-1).astype(F32)
        flat.append(jnp.pad(v, (0, size - v.shape[0])))
    v = jnp.concatenate(flat)
    return jnp.pad(v, (0, SV_ROWS * 128 - v.shape[0])).reshape(SV_ROWS, 128)


def _sv_get(flat, n, size):
    return flat[SV_OFF[n]:SV_OFF[n] + size]


def kernel(x, c, w_ada, b_ada, norm_mix_w, w_in, conv_w, conv_b, dt_bias, a_log, d_skip, ssd_norm_w, w_branch_ssd, pool_w, pool_scale, w_branch_pool, w_out, norm_mlp_w, w_up, w_down, norm_final_w, loss_target, m_w_ada, m_b_ada, m_norm_mix_w, m_w_in, m_conv_w, m_conv_b, m_dt_bias, m_a_log, m_d_skip, m_ssd_norm_w, m_w_branch_ssd, m_pool_w, m_pool_scale, m_w_branch_pool, m_w_out, m_norm_mlp_w, m_w_up, m_w_down, m_norm_final_w, v_w_ada, v_b_ada, v_norm_mix_w, v_w_in, v_conv_w, v_conv_b, v_dt_bias, v_a_log, v_d_skip, v_ssd_norm_w, v_w_branch_ssd, v_pool_w, v_pool_scale, v_w_branch_pool, v_w_out, v_norm_mlp_w, v_w_up, v_w_down, v_norm_final_w):
    xs_ = x[0]
    tgt = loss_target[0]
    L = xs_.shape[0]
    me = 4 * lax.axis_index("x") + 2 * lax.axis_index("y") + lax.axis_index("c")
    wloc = w_ada.shape[2]

    mod_p, c_all = _ada_fwd(c, w_ada[0], b_ada.reshape(N_DEV, wloc), name="ada_fwd")
    mod = mod_p.reshape(6, D)
    shift_m, scale_m, gate_m, shift_f, scale_f, gate_f = [mod[i:i + 1] for i in range(6)]

    conv_bits = lax.bitcast_convert_type(conv_w[0], SLAB_DT).reshape(3, D)
    in_shift = (IN_ROWS * me) % 16
    slab_in = lax.dynamic_update_slice(jnp.zeros((IN_ROWS_P, D), SLAB_DT), w_in[0].T.astype(SLAB_DT),
                                       (in_shift, 0))
    slab_in = jnp.concatenate([slab_in, _pad_rows(conv_bits, CONV_ROWS)], axis=0)
    slab_rest = jnp.concatenate([
        w_branch_ssd[0].astype(SLAB_DT),
        pool_w[0].reshape(32, D).astype(SLAB_DT),
        w_branch_pool[0].astype(SLAB_DT),
        w_out[0].astype(SLAB_DT),
        w_up[0].T.astype(SLAB_DT),
        w_down[0].astype(SLAB_DT)], axis=0)
    slab_in, mod_p = lax.optimization_barrier((slab_in, mod_p))
    gs_in = _gather_slabs(slab_in, name="gather_w_in")
    slab_rest, gs_in = lax.optimization_barrier((slab_rest, gs_in))
    rest_started = _xchg_start(slab_rest, per_peer=False, name="gather_rest_start")
    gather_token = rest_started[4]

    w_in_t = _reorder_in_rows(gs_in)
    conv_full = lax.bitcast_convert_type(
        gs_in[:, IN_ROWS_P:IN_ROWS_P + 3].reshape(N_DEV, 4, XBC // N_DEV, 2), F32)
    conv_full = conv_full.transpose(1, 0, 2).reshape(4, XBC)

    dtb = jnp.pad(dt_bias, ((0, 0), (0, 128 - NH)))
    arow = jnp.pad(-jnp.exp(a_log), ((0, 0), (0, 128 - NH)))
    dsk_x = jnp.repeat(d_skip, HP, axis=1)

    tm = _pick(L, (1024, 512, 256, 128))
    tm2 = _pick(L, (2048, 1024, 512, 256, 128))
    tkl = _pick(L, (4096, 2048, 1024, 512, 256, 128))
    tkl2 = _pick(L, (2048, 1024, 512, 256, 128))

    tmh = _pick(L, (512, 256, 128))
    zcol = C_Z // DI
    gcol = C_GATE // (2 * D)

    def whole_rows(w):
        return lambda t: ((L, w), BF16, (t, w), lambda i, j, k: (i, 0))

    def norm1_pro(x_ref, ex, outs, j):
        @pl.when(j == 0)
        def _():
            xv = x_ref[...]
            r = lax.rsqrt(jnp.mean(xv * xv, axis=-1, keepdims=True) + EPS)
            outs[1][...] = (xv * r * ex[0][...] * (1.0 + ex[1][...]) + ex[2][...]).astype(outs[1].dtype)

        return outs[1][...]

    def proj_ep(acc, ex, outs):
        outs[0][...] = acc

        @pl.when(pl.program_id(1) == NPROJ // 768 - 1)
        def _():
            outs[2][...] = acc[:, 768 - DT_PAD:]

    proj, h1, dtp = _mm(
        xs_, w_in_t, "nt", name="in_proj", tm=tm2, tn=768, tk=D,
        extras=[(norm_mix_w, *_vecs()), (scale_m, *_vecs()), (shift_m, *_vecs()), _dep(gather_token)],
        outs=[F32, whole_rows(D)(tm2), ((L, DT_PAD), F32, (tm2, DT_PAD), lambda i, j, k: (i, 0))],
        prologue=norm1_pro, epilogue=proj_ep)
    xbc_raw = proj
    y_ssm, hs, xbc = _ssd_fwd(xbc_raw, dtp, conv_full, conv_b, dtb, arow, dsk_x, name="ssd_fwd")

    slab_rest, gs = _xchg_wait(rest_started, y_ssm, per_peer=False, name="gather_rest_wait")
    gs = lax.dynamic_update_slice(gs, slab_rest[None], (me, 0, 0))

    def part(n, rows):
        return gs[:, REST_OFF[n]:REST_OFF[n] + rows]

    w_bssd = part("bssd", 256).reshape(DI, D)
    w_pool = part("pool", 32).reshape(N_DEV, 4, 32, PGW).transpose(1, 0, 2, 3).reshape(POOL_W, PGW)
    w_bpool = part("bpool", 128).reshape(POOL_W, D)
    w_o = part("out", 128).reshape(D, D)
    w_up_t = part("up", 512).reshape(DFF, D)
    w_dn = part("down", 512).reshape(DFF, D)

    def gnorm_pro(y_ref, ex, outs, j):
        z_ref, w_ref = ex
        yg = y_ref[...].astype(F32) * _silu(z_ref[...].astype(F32))
        segs = []
        for k in range(NG):
            sl = slice(k * GW, (k + 1) * GW)
            seg = yg[:, sl]
            r = lax.rsqrt(jnp.mean(seg * seg, axis=-1, keepdims=True) + EPS)
            segs.append((seg * r * w_ref[:, sl]).astype(BF16))
        yn_v = jnp.concatenate(segs, axis=1)
        outs[1][...] = yn_v
        return yn_v

    y_ssd, yn = _mm(y_ssm, w_bssd, "nn", name="branch_ssd", tm=tmh, tn=D, tk=DI,
                    extras=[(proj, *_rows(tmh, DI, zcol)), (ssd_norm_w, *_vecs(DI))],
                    outs=[BF16, whole_rows(DI)(tmh)], prologue=gnorm_pro)
    pooled = _pool_fwd(proj, name="pool_fwd")
    wp_spec = ((POOL_W, PGW), lambda i, j, k: (0, 0))

    def pool_pro(a_ref, ex, outs, j):
        wp_ref, s_ref = ex
        segs = []
        for g in range(4):
            sl = slice(g * PGW, (g + 1) * PGW)
            p = _dot(a_ref[:, sl], wp_ref[sl, :], NN)
            outs[1][:, sl] = p.astype(BF16)
            segs.append((p * s_ref[:, sl]).astype(BF16))
        yp1_v = jnp.concatenate(segs, axis=1)
        outs[2][...] = yp1_v
        return yp1_v

    y_pool, yp0, yp1 = _mm(pooled, w_bpool, "nn", name="branch_pool", tm=tm, tn=D, tk=D,
                           extras=[(w_pool, *wp_spec), (pool_scale, *_vecs())],
                           outs=[BF16, whole_rows(D)(tm), whole_rows(D)(tm)], prologue=pool_pro)

    def merge_pro(a_ref, ex, outs, j):
        s = _sigmoid(ex[1][...].astype(F32))
        mv = (s[:, :D] * a_ref[...].astype(F32) + s[:, D:] * ex[0][...].astype(F32)).astype(BF16)
        outs[3][...] = mv
        return mv

    mix, x1, h2, m = _mm(y_ssd, w_o, "nn", name="out_proj", tm=tmh, tn=D, tk=D,
                         extras=[(y_pool, *_rows(tmh)), (proj, *_rows(tmh, 2 * D, gcol)),
                                 (xs_, *_rows(tmh)), (gate_m, *_vecs()), (norm_mlp_w, *_vecs()),
                                 (scale_f, *_vecs()), (shift_f, *_vecs())],
                         outs=[BF16, F32, BF16, whole_rows(D)(tmh)], prologue=merge_pro,
                         epilogue=lambda acc, ex, outs: _ep_resid_norm(acc, ex[2:], outs[:3]))

    def relu2(acc, ex, outs):
        r = jnp.maximum(acc, 0.0)
        outs[0][...] = acc.astype(BF16)
        outs[1][...] = (r * r).astype(BF16)

    up, act = _mm(h2, w_up_t, "nt", name="mlp_up", outs=[BF16, BF16], tm=tm2, tn=1024, tk=D, epilogue=relu2)

    dx2, ddown, loss_p, dnwf, dgate_f = _mm(
        act, w_dn, "nn", name="mlp_down", tm=tmh, tn=D, tk=DFF,
        extras=[(x1, *_rows(tmh)), (tgt, *_rows(tmh)), (gate_f, *_vecs()), (norm_final_w.reshape(1, D), *_vecs())],
        outs=[F32, BF16, _sum_out(128), _sum_out(), _sum_out()], epilogue=_ep_final)

    def drelu2(acc, ex, outs):
        outs[0][...] = (acc * (2.0 * jnp.maximum(ex[0][...].astype(F32), 0.0))).astype(BF16)

    def dep_last(ep):
        return lambda acc, ex, outs: ep(acc, ex[:-1], outs)

    dup = _mm(ddown, w_dn, "nt", name="mlp_down_dx", outs=[BF16], tm=tm2, tn=1024, tk=D,
              extras=[(up, (tm2, 1024), lambda i, j, k: (i, j))], epilogue=drelu2)
    g_dn = _mm(act, ddown, "tn", name="mlp_down_dw", outs=[SLAB_DT], tm=1024, tn=D, tk=tkl)
    g_up_t = _mm(dup, h2, "tn", name="mlp_up_dw", outs=[SLAB_DT], tm=1024, tn=D, tk=tkl)
    gslab_mlp = jnp.concatenate([g_up_t.reshape(N_DEV, 512, D), g_dn.reshape(N_DEV, 512, D)], axis=1)
    mlp_started = _xchg_start(gslab_mlp, per_peer=True, name="scatter_mlp_start")
    dx1, p2, q2, dmix, dgate_m = _mm(
        dup, w_up_t, "nn", name="mlp_up_dx", tm=tmh, tn=D, tk=DFF,
        extras=[(x1, *_rows(tmh)), (dx2, *_rows(tmh)), (norm_mlp_w, *_vecs()), (scale_f, *_vecs()),
                (mix, *_rows(tmh)), (gate_m, *_vecs()), _dep(mlp_started[4])],
        outs=[F32, _sum_out(), _sum_out(), BF16, _sum_out()], epilogue=dep_last(_ep_norm_bwd))
    gcol = C_GATE // (2 * D)
    dy_ssd, dy_pool, dproj = _mm(
        dmix, w_o, "nt", name="out_proj_dx", tm=tmh, tn=D, tk=D,
        extras=[(y_ssd, *_rows(tmh)), (y_pool, *_rows(tmh)), (proj, *_rows(tmh, 2 * D, gcol))],
        outs=[BF16, BF16, ((L, NPROJ), BF16, *_rows(tmh, 2 * D, gcol))], epilogue=_ep_merge_bwd)
    g_o = _mm(m, dmix, "tn", name="out_proj_dw", outs=[SLAB_DT], tm=D, tn=D, tk=tkl)
    zcol = C_Z // DI
    dy_ssm, dproj, d_snw = _mm(
        dy_ssd, w_bssd, "nt", name="branch_ssd_dx", tm=tmh, tn=DI, tk=D,
        extras=[(y_ssm, *_rows(tmh, DI)), (proj, *_rows(tmh, DI, zcol)), (ssd_norm_w, *_vecs(DI)),
                (dproj, None, None)],
        outs=[F32, ((L, NPROJ), BF16, *_rows(tmh, DI, zcol)), _sum_out(DI)],
        epilogue=_ep_gated_norm_bwd, aliases={3: 1})
    g_bssd = _mm(yn, dy_ssd, "tn", name="branch_ssd_dw", outs=[SLAB_DT], tm=1024, tn=D, tk=tkl)
    dxbc, dproj, d_a, d_dx, d_dtb = _ssd_bwd(dy_ssm, xbc, dtp, hs, dtb, arow, dsk_x, dproj, name="ssd_bwd")
    dproj, d_cw, d_cb = _conv_bwd(xbc_raw, dxbc, conv_full, conv_b, dproj, name="conv_bwd")
    def pool_bwd_ep(acc, ex, outs):
        y_ref, s_ref, wp_ref = ex
        o_ref, ds_ref, dpool_ref = outs
        dyp0_v = (acc * s_ref[...]).astype(BF16)
        o_ref[...] = dyp0_v
        _acc_out(ds_ref, _colsum(acc * y_ref[...].astype(F32)), _row_step())
        for g in range(4):
            sl = slice(g * PGW, (g + 1) * PGW)
            dpool_ref[:, sl] = _dot(dyp0_v[:, sl], wp_ref[sl, :], NT)

    dyp0, d_ps, dpooled = _mm(dy_pool, w_bpool, "nt", name="branch_pool_dx", tm=tm, tn=D, tk=D,
                              extras=[(yp0, *_rows(tm)), (pool_scale, *_vecs()), (w_pool, *wp_spec)],
                              outs=[BF16, _sum_out(), F32], epilogue=pool_bwd_ep)
    g_bpool = _mm(yp1, dy_pool, "tn", name="branch_pool_dw", outs=[SLAB_DT], tm=D, tn=D, tk=tkl)
    g_pool = _mm_pool_tn(pooled, dyp0, name="pool_mix_dw", tk=tkl)
    gslab_mix = jnp.concatenate([
        g_bssd.reshape(N_DEV, 256, D),
        g_pool.reshape(4, N_DEV, 32, PGW).transpose(1, 0, 2, 3).reshape(N_DEV, 32, D).astype(SLAB_DT),
        g_bpool.reshape(N_DEV, 128, D),
        g_o.reshape(N_DEV, 128, D)], axis=1)
    mix_started = _xchg_start(gslab_mix, per_peer=True, name="scatter_mix_start")
    dproj = _pool_bwd(dpooled, dproj, name="pool_bwd")
    g_in_t = _mm(dproj, h1, "tn", name="in_proj_dw", outs=[SLAB_DT], tm=1408, tn=D, tk=tkl2,
                 extras=[_dep(mix_started[4])])
    gslab_in = _restore_in_shards(g_in_t)
    in_started = _xchg_start(gslab_in, per_peer=True, name="scatter_in_start")
    grad_x, p1, q1 = _mm(
        dproj, w_in_t, "nn", name="in_proj_dx", tm=tmh, tn=D, tk=2816,
        extras=[(xs_, *_rows(tmh)), (dx1, *_rows(tmh)), (norm_mix_w, *_vecs()), (scale_m, *_vecs()),
                _dep(in_started[4])],
        outs=[F32, _sum_out(), _sum_out()], epilogue=dep_last(_ep_norm_bwd))

    def landed(started, after, tile, name):
        src, land = _xchg_wait(started, after, per_peer=True, name=name + "_wait")
        own = lax.dynamic_slice_in_dim(src, me, 1, axis=0)
        return _slab_sum(lax.dynamic_update_slice(land, own, (me, 0, 0)), tile=tile, name=name + "_sum")

    gsum_mlp = landed(mlp_started, grad_x, 256, "scatter_mlp")
    gsum_mix = landed(mix_started, grad_x, 272, "scatter_mix")
    gsum_in = landed(in_started, grad_x, 208, "scatter_in")

    dmod = jnp.concatenate([q1, p1 * norm_mix_w, dgate_m, q2, p2 * norm_mlp_w, dgate_f], axis=1)
    d_alog = d_a[:, :NH] * (-jnp.exp(a_log))
    sv = _pack_sv({
        "b_ada": dmod, "norm_mix_w": p1 * (1.0 + scale_m), "conv_b": d_cb, "dt_bias": d_dtb[:, :NH],
        "a_log": d_alog, "d_skip": d_dx.reshape(NH, HP).sum(axis=1), "ssd_norm_w": d_snw,
        "pool_scale": d_ps, "norm_mlp_w": p2 * (1.0 + scale_f), "norm_final_w": dnwf, "conv_w": d_cw,
        "loss": loss_p[:, :1]})
    sv_all, sv_sum = _small_allsum(sv, name="small_allsum")
    flat = sv_sum.reshape(-1)
    loss = flat[SV_OFF["loss"]]
    dmod_all = sv_all.reshape(N_DEV, SV_ROWS * 128)[:, :6 * D]
    g_w_ada = _ada_bwd(c_all, lax.dynamic_slice_in_dim(dmod_all, me * wloc, wloc, axis=1), name="ada_bwd")

    g_conv_w = lax.dynamic_slice_in_dim(_sv_get(flat, "conv_w", 4 * XBC).reshape(4, XBC),
                                        me * (XBC // N_DEV), XBC // N_DEV, axis=1)
    small = [("b_ada", b_ada, m_b_ada, v_b_ada), ("norm_mix_w", norm_mix_w, m_norm_mix_w, v_norm_mix_w),
             ("conv_b", conv_b, m_conv_b, v_conv_b), ("dt_bias", dt_bias, m_dt_bias, v_dt_bias),
             ("a_log", a_log, m_a_log, v_a_log), ("d_skip", d_skip, m_d_skip, v_d_skip),
             ("ssd_norm_w", ssd_norm_w, m_ssd_norm_w, v_ssd_norm_w),
             ("pool_scale", pool_scale, m_pool_scale, v_pool_scale),
             ("norm_mlp_w", norm_mlp_w, m_norm_mlp_w, v_norm_mlp_w),
             ("norm_final_w", norm_final_w[None], m_norm_final_w[None], v_norm_final_w[None]),
             ("conv_w", conv_w[0], m_conv_w[0], v_conv_w[0])]
    small_out = _adamw_small(sv_sum.reshape(1, SV_ROWS * 128), g_conv_w, small, name="adamw_small")
    small_out["norm_final_w"] = tuple(a[0] for a in small_out["norm_final_w"])
    small_out["conv_w"] = tuple(a[None] for a in small_out["conv_w"])

    def gpart(n, rows_):
        return gsum_mix[MIX_OFF[n]:MIX_OFF[n] + rows_]

    def lin(a):
        return a[0].T.reshape(IN_ROWS * 8, 128)

    g_lin = lax.dynamic_slice_in_dim(gsum_in, in_shift, IN_ROWS, axis=0).reshape(IN_ROWS * 8, 128)
    dlt, mn, vn = _adamw(lin(w_in), g_lin, lin(m_w_in), lin(v_w_in), name="adamw_w_in", tr=IN_ROWS * 2)
    big_in = tuple(a.reshape(IN_ROWS, D).T[None] for a in (g_lin, dlt, mn, vn))

    big = {
        "w_ada": (w_ada, m_w_ada, v_w_ada, g_w_ada, (D, wloc)),
        "w_branch_ssd": (w_branch_ssd, m_w_branch_ssd, v_w_branch_ssd, gpart("bssd", 256), (256, D)),
        "pool_w": (pool_w, m_pool_w, v_pool_w, gpart("pool", 32).reshape(128, PGW), (128, PGW)),
        "w_branch_pool": (w_branch_pool, m_w_branch_pool, v_w_branch_pool, gpart("bpool", 128), (128, D)),
        "w_out": (w_out, m_w_out, v_w_out, gpart("out", 128), (128, D)),
        "w_up": (w_up, m_w_up, v_w_up, gsum_mlp[:512].T, (D, 512)),
        "w_down": (w_down, m_w_down, v_w_down, gsum_mlp[512:], (512, D)),
    }
    big_out = {}
    for n, (w, mm_, vv, g, shp2) in big.items():
        dlt, mn, vn = _adamw(w.reshape(shp2), g, mm_.reshape(shp2), vv.reshape(shp2), name="adamw_" + n)
        big_out[n] = (g.reshape(w.shape), dlt.reshape(w.shape), mn.reshape(w.shape), vn.reshape(w.shape))

    order = ["w_ada", "b_ada", "norm_mix_w", "w_in", "conv_w", "conv_b", "dt_bias", "a_log", "d_skip",
             "ssd_norm_w", "w_branch_ssd", "pool_w", "pool_scale", "w_branch_pool", "w_out", "norm_mlp_w",
             "w_up", "w_down", "norm_final_w"]
    big_out["w_in"] = big_in
    res = {**small_out, **big_out}
    outs = [loss, grad_x.reshape(x.shape)]
    for k in range(4):
        outs += [res[n][k] for n in order]
    return tuple(outs)
```

```python
import functools

import numpy as np
import jax
import jax.numpy as jnp
from jax import lax
from jax.experimental import pallas as pl
from jax.experimental.pallas import tpu as pltpu

F32 = jnp.float32
BF16 = jnp.bfloat16
SLAB_DT = jnp.bfloat16
_MXU_DTYPE = jnp.bfloat16

N_DEV = 8
D = 1024
DI = 2048
NH = 32
HP = 64
NG = 4
NS = 128
Q = 128
XBC = DI + 2 * NG * NS
DFF = 4096
N_IN = 8224
EPS = 1e-5
POOL_W = 1024
PGW = 256

C_XBC, C_POOL, C_Z, C_GATE, C_DT = 0, 3072, 4096, 6144, 8192
DT_PAD = 256
NPROJ = C_DT + DT_PAD

IN_ROWS = N_IN // N_DEV
IN_ROWS_P = 1040
CONV_ROWS = 16
REST_PARTS = (("bssd", 256), ("pool", 32), ("bpool", 128), ("out", 128), ("up", 512), ("down", 512))
REST_OFF = {}
_o = 0
for _n, _r in REST_PARTS:
    REST_OFF[_n] = _o
    _o += _r
REST_ROWS = _o
MIX_PARTS = (("bssd", 256), ("pool", 32), ("bpool", 128), ("out", 128))
MIX_OFF = {}
_o = 0
for _n, _r in MIX_PARTS:
    MIX_OFF[_n] = _o
    _o += _r
MIX_ROWS = _o

SV_PARTS = (("b_ada", 6144), ("norm_mix_w", 1024), ("conv_b", 3072), ("dt_bias", 128), ("a_log", 128),
            ("d_skip", 128), ("ssd_norm_w", 2048), ("pool_scale", 1024), ("norm_mlp_w", 1024),
            ("norm_final_w", 1024), ("conv_w", 4 * XBC), ("loss", 128))
SV_OFF = {}
_o = 0
for _n, _r in SV_PARTS:
    SV_OFF[_n] = _o
    _o += _r
SV_ROWS = 224
assert _o <= SV_ROWS * 128

ADAM_LR, ADAM_B1, ADAM_B2, ADAM_EPS, ADAM_WD, ADAM_STEP = 0.001, 0.9, 0.999, 1e-08, 0.01, 10

VMEM_BIG = 56 * 1024 * 1024
NEG = -1e30

NN = ((1,), (0,))
NT = ((1,), (1,))
TN = ((0,), (0,))


def _dot(a, b, dims=NN):
    return lax.dot_general(a.astype(_MXU_DTYPE), b.astype(_MXU_DTYPE), (dims, ((), ())),
                           preferred_element_type=F32)


def _dot_hi(a, b, dims=NN):
    return lax.dot_general(a.astype(F32), b.astype(F32), (dims, ((), ())),
                           precision=lax.Precision.HIGHEST, preferred_element_type=F32)


def _pick(n, cands):
    for c in cands:
        if n % c == 0:
            return c
    return n


def _sigmoid(x):
    return 1.0 / (1.0 + jnp.exp(-x))


def _silu(x):
    return x * _sigmoid(x)


def _dsilu(x):
    s = _sigmoid(x)
    return s * (1.0 + x * (1.0 - s))


def _softplus(x):
    return jnp.maximum(x, 0.0) + jnp.log(1.0 + jnp.exp(-jnp.abs(x)))


def _params(sem, vmem=None):
    return pltpu.CompilerParams(dimension_semantics=sem, vmem_limit_bytes=vmem)


def _row_step():
    return pl.program_id(0)


def _mm(a, b, mode, *, name, outs, tm, tn, tk, extras=(), epilogue=None, aliases=None, prologue=None):
    if mode == "tn":
        K, M = a.shape
        N = b.shape[1]
        a_spec = pl.BlockSpec((tk, tm), lambda i, j, k: (k, i))
        b_spec = pl.BlockSpec((tk, tn), lambda i, j, k: (k, j))
        dims = TN
    else:
        M = a.shape[0]
        K = b.shape[0] if mode == "nn" else b.shape[1]
        if prologue is None:
            assert a.shape[1] == K
            a_spec = pl.BlockSpec((tm, tk), lambda i, j, k: (i, k))
        else:
            assert tk == K
            a_spec = pl.BlockSpec((tm, a.shape[1]), lambda i, j, k: (i, 0))
        if mode == "nn":
            N = b.shape[1]
            b_spec = pl.BlockSpec((tk, tn), lambda i, j, k: (k, j))
            dims = NN
        else:
            N = b.shape[0]
            b_spec = pl.BlockSpec((tn, tk), lambda i, j, k: (j, k))
            dims = NT
    assert M % tm == 0 and N % tn == 0 and K % tk == 0, (name, M, N, K, tm, tn, tk)
    nk = K // tk
    ne, no = len(extras), len(outs)
    if epilogue is None:
        def epilogue(acc, ex, out_refs):
            out_refs[0][...] = acc.astype(out_refs[0].dtype)

    def body(a_ref, b_ref, *rest):
        ex, out_refs = rest[:ne], rest[ne:ne + no]
        lhs = a_ref[...] if prologue is None else prologue(a_ref, ex, out_refs, pl.program_id(1))
        p = _dot(lhs, b_ref[...], dims)
        if nk == 1:
            epilogue(p, ex, out_refs)
        else:
            acc = rest[-1]
            k = pl.program_id(2)

            @pl.when(k == 0)
            def _():
                acc[...] = p

            @pl.when(jnp.logical_and(k > 0, k < nk - 1))
            def _():
                acc[...] += p

            @pl.when(k == nk - 1)
            def _():
                epilogue(acc[...] + p, ex, out_refs)

    out_specs, out_shape = [], []
    for o in outs:
        if isinstance(o, tuple):
            shape, dt, bs, im = o
            out_specs.append(pl.BlockSpec(bs, im))
            out_shape.append(jax.ShapeDtypeStruct(shape, dt))
        else:
            out_specs.append(pl.BlockSpec((tm, tn), lambda i, j, k: (i, j)))
            out_shape.append(jax.ShapeDtypeStruct((M, N), o))
    in_specs = [a_spec, b_spec]
    for _, bs, im in extras:
        in_specs.append(pl.BlockSpec(memory_space=pl.ANY) if bs is None else pl.BlockSpec(bs, im))
    res = pl.pallas_call(
        body, name=name,
        grid=(M // tm, N // tn, nk),
        in_specs=in_specs, out_specs=out_specs, out_shape=out_shape,
        scratch_shapes=[pltpu.VMEM((tm, tn), F32)] if nk > 1 else [],
        input_output_aliases={2 + e: o for e, o in (aliases or {}).items()},
        compiler_params=_params(("arbitrary", "arbitrary", "arbitrary"), VMEM_BIG),
    )(a, b, *[e[0] for e in extras])
    return res if no > 1 else res[0]


def _rows(tm, w=D, col=0):
    return (tm, w), lambda i, j, k, c=col: (i, c)


def _vecs(w=D, col=0):
    return (1, w), lambda i, j, k, c=col: (0, c)


def _sum_out(w=D):
    return ((1, w), F32, (1, w), lambda i, j, k: (0, 0))


def _mm_pool_tn(a, b, *, name, tk):
    L = a.shape[0]

    def body(a_ref, b_ref, o_ref):
        p = _dot(a_ref[...], b_ref[...], TN)

        @pl.when(pl.program_id(1) == 0)
        def _():
            o_ref[...] = p

        @pl.when(pl.program_id(1) > 0)
        def _():
            o_ref[...] += p

    blk = pl.BlockSpec((tk, PGW), lambda g, k: (k, g))
    return pl.pallas_call(body, name=name, grid=(4, L // tk), in_specs=[blk, blk],
                          out_specs=pl.BlockSpec((PGW, PGW), lambda g, k: (g, 0)),
                          out_shape=jax.ShapeDtypeStruct((POOL_W, PGW), F32),
                          compiler_params=_params(("parallel", "arbitrary")))(a, b)


def _acc_out(ref, val, i):
    @pl.when(i == 0)
    def _():
        ref[...] = val

    @pl.when(i > 0)
    def _():
        ref[...] += val


def _colsum(v):
    return jnp.sum(v, axis=0, keepdims=True)


def _ep_resid_norm(acc, ex, outs):
    x_ref, g_ref, nw_ref, sc_ref, sh_ref = ex
    mix_ref, x1_ref, h_ref = outs
    mix_ref[...] = acc.astype(mix_ref.dtype)
    xv = x_ref[...] + g_ref[...] * acc
    x1_ref[...] = xv
    r = lax.rsqrt(jnp.mean(xv * xv, axis=-1, keepdims=True) + EPS)
    h_ref[...] = (xv * r * nw_ref[...] * (1.0 + sc_ref[...]) + sh_ref[...]).astype(h_ref.dtype)


def _ep_final(acc, ex, outs):
    x1_ref, t_ref, g_ref, nw_ref = ex
    dx2_ref, dd_ref, loss_ref, dnw_ref, dg_ref = outs
    i = _row_step()
    x2 = x1_ref[...] + g_ref[...] * acc
    r = lax.rsqrt(jnp.mean(x2 * x2, axis=-1, keepdims=True) + EPS)
    xh = x2 * r
    e = xh * nw_ref[...] - t_ref[...]
    part = 0.5 * jnp.sum(jnp.mean(e * e, axis=-1, keepdims=True), axis=0, keepdims=True)
    dy = e * (1.0 / D)
    g = dy * nw_ref[...]
    dx2 = r * (g - xh * jnp.mean(g * xh, axis=-1, keepdims=True))
    dx2_ref[...] = dx2
    dd_ref[...] = (dx2 * g_ref[...]).astype(dd_ref.dtype)
    _acc_out(loss_ref, jnp.broadcast_to(part, (1, 128)), i)
    _acc_out(dnw_ref, _colsum(dy * xh), i)
    _acc_out(dg_ref, _colsum(dx2 * acc), i)


def _ep_norm_bwd(acc, ex, outs):
    x_ref, dr_ref, nw_ref, sc_ref = ex[:4]
    dx_ref, p_ref, q_ref = outs[:3]
    i = _row_step()
    xv = x_ref[...]
    r = lax.rsqrt(jnp.mean(xv * xv, axis=-1, keepdims=True) + EPS)
    xh = xv * r
    g = acc * (nw_ref[...] * (1.0 + sc_ref[...]))
    dx = dr_ref[...] + r * (g - xh * jnp.mean(g * xh, axis=-1, keepdims=True))
    dx_ref[...] = dx
    _acc_out(p_ref, _colsum(acc * xh), i)
    _acc_out(q_ref, _colsum(acc), i)
    if len(ex) > 4:
        m_ref, g_ref = ex[4:]
        dm_ref, dg_ref = outs[3:]
        dm_ref[...] = (dx * g_ref[...]).astype(dm_ref.dtype)
        _acc_out(dg_ref, _colsum(dx * m_ref[...].astype(F32)), i)


def _ep_merge_bwd(acc, ex, outs):
    a_ref, b_ref, gl_ref = ex
    da_ref, db_ref, dgl_ref = outs
    s = _sigmoid(gl_ref[...].astype(F32))
    s1, s2 = s[:, :D], s[:, D:]
    da_ref[...] = (acc * s1).astype(da_ref.dtype)
    db_ref[...] = (acc * s2).astype(db_ref.dtype)
    dgl_ref[:, :D] = (acc * a_ref[...].astype(F32) * s1 * (1.0 - s1)).astype(dgl_ref.dtype)
    dgl_ref[:, D:] = (acc * b_ref[...].astype(F32) * s2 * (1.0 - s2)).astype(dgl_ref.dtype)


GW = DI // NG


def _ep_gated_norm_bwd(acc, ex, outs):
    y_ref, z_ref, w_ref, _ = ex
    dy_ref, dz_ref, dw_ref = outs
    zv = z_ref[...].astype(F32)
    yv = y_ref[...].astype(F32)
    sg = _sigmoid(zv)
    sz = zv * sg
    yg = yv * sz
    dsz = sg * (1.0 + zv * (1.0 - sg))
    dws = []
    for k in range(NG):
        sl = slice(k * GW, (k + 1) * GW)
        seg = yg[:, sl]
        r = lax.rsqrt(jnp.mean(seg * seg, axis=-1, keepdims=True) + EPS)
        sh = seg * r
        dn = acc[:, sl]
        g = dn * w_ref[:, sl]
        dyg = r * (g - sh * jnp.mean(g * sh, axis=-1, keepdims=True))
        dy_ref[:, sl] = dyg * sz[:, sl]
        dz_ref[:, sl] = (dyg * yv[:, sl] * dsz[:, sl]).astype(dz_ref.dtype)
        dws.append(_colsum(dn * sh))
    _acc_out(dw_ref, jnp.concatenate(dws, axis=1), _row_step())


CONV_CB = 128
HALO = 16


def _time_chunk(L):
    return _pick(L, (256, 128))


def _with_halo(x_ref, i, r0, rc):
    p0 = pl.multiple_of(jnp.maximum(r0 - HALO, 0), HALO)
    prev = jnp.where(i > 0, x_ref[pl.ds(p0, HALO), :].astype(F32), 0.0)
    return jnp.concatenate([prev, x_ref[pl.ds(r0, rc), :].astype(F32)], axis=0)


def _conv_bwd(proj, dy, w, b, dproj, *, name):
    L = proj.shape[0]
    rc = _time_chunk(L)
    n = L // rc

    def body(x_ref, dy_ref, w_ref, b_ref, dp_in, dx_ref, dw_ref, db_ref, xpad, dpad):
        del dp_in
        wv = w_ref[...]
        bv = b_ref[...]
        dpad[rc:rc + HALO, :] = jnp.zeros((HALO, CONV_CB), F32)

        def step(k, carry):
            db, d0, d1, d2, d3 = carry
            i = n - 1 - k
            r0 = pl.multiple_of(i * rc, rc)
            p0 = pl.multiple_of(jnp.maximum(r0 - HALO, 0), HALO)
            xpad[0:HALO, :] = jnp.where(i > 0, x_ref[pl.ds(p0, HALO), :].astype(F32), 0.0)
            xpad[HALO:HALO + rc, :] = x_ref[pl.ds(r0, rc), :].astype(F32)
            xk = [xpad[HALO - j:HALO - j + rc, :] for j in range(4)]
            pre = bv
            for j in range(4):
                pre = pre + xk[j] * wv[3 - j:4 - j]
            dpre = dy_ref[pl.ds(r0, rc), :] * _dsilu(pre)
            dpad[0:rc, :] = dpre
            acc = dpre * wv[3:4]
            for j in (1, 2, 3):
                acc = acc + dpad[j:j + rc, :] * wv[3 - j:4 - j]
            dx_ref[pl.ds(r0, rc), :] = acc.astype(dx_ref.dtype)
            dpad[rc:rc + HALO, :] = dpre[:HALO]
            return (db + _colsum(dpre), d0 + _colsum(dpre * xk[3]), d1 + _colsum(dpre * xk[2]),
                    d2 + _colsum(dpre * xk[1]), d3 + _colsum(dpre * xk[0]))

        z = jnp.zeros((1, CONV_CB), F32)
        db, d0, d1, d2, d3 = lax.fori_loop(0, n, step, (z, z, z, z, z))
        db_ref[...] = db
        dw_ref[...] = jnp.concatenate([d0, d1, d2, d3], axis=0)

    nb = XBC // CONV_CB
    return pl.pallas_call(
        body, name=name, grid=(nb,),
        in_specs=[pl.BlockSpec((L, CONV_CB), lambda j: (0, j + C_XBC // CONV_CB)),
                  pl.BlockSpec((L, CONV_CB), lambda j: (0, j)),
                  pl.BlockSpec((4, CONV_CB), lambda j: (0, j)), pl.BlockSpec((1, CONV_CB), lambda j: (0, j)),
                  pl.BlockSpec(memory_space=pl.ANY)],
        out_specs=[pl.BlockSpec((L, CONV_CB), lambda j: (0, j + C_XBC // CONV_CB)),
                   pl.BlockSpec((4, CONV_CB), lambda j: (0, j)), pl.BlockSpec((1, CONV_CB), lambda j: (0, j))],
        out_shape=[jax.ShapeDtypeStruct((L, NPROJ), BF16), jax.ShapeDtypeStruct((4, XBC), F32),
                   jax.ShapeDtypeStruct((1, XBC), F32)],
        scratch_shapes=[pltpu.VMEM((rc + HALO, CONV_CB), F32), pltpu.VMEM((rc + HALO, CONV_CB), F32)],
        input_output_aliases={4: 0},
        compiler_params=_params(("parallel",), VMEM_BIG))(proj, dy, w, b, dproj)


def _pool_fwd(proj, *, name):
    L = proj.shape[0]
    rc = _time_chunk(L)
    n = L // rc

    def body(x_ref, o_ref, pad):
        g = pl.program_id(0)
        pad[0:HALO, :] = jnp.zeros((HALO, PGW), F32)

        def fill(i, c):
            r0 = pl.multiple_of(i * rc, rc)
            pad[pl.ds(r0 + HALO, rc), :] = x_ref[pl.ds(r0, rc), :].astype(F32)
            return c

        lax.fori_loop(0, n, fill, 0)
        rows = lax.broadcasted_iota(jnp.int32, (rc, PGW), 0)

        for gi in range(4):
            win = 2 << gi

            @pl.when(g == gi)
            def _(gi=gi, win=win):
                def step(i, c):
                    r0 = pl.multiple_of(i * rc, rc)
                    ext = pad[pl.ds(r0, rc + HALO), :]
                    s = ext
                    sh = 1
                    while sh < win:
                        s = s + pltpu.roll(s, sh, 0)
                        sh *= 2
                    cnt = jnp.minimum(rows + (r0 + 1), win).astype(F32)
                    o_ref[pl.ds(r0, rc), :] = (s[HALO:] / cnt - ext[HALO:]).astype(o_ref.dtype)
                    return c

                lax.fori_loop(0, n, step, 0)

    return pl.pallas_call(
        body, name=name, grid=(4,),
        in_specs=[pl.BlockSpec((L, PGW), lambda j: (0, j + C_POOL // PGW))],
        out_specs=pl.BlockSpec((L, PGW), lambda j: (0, j)),
        out_shape=jax.ShapeDtypeStruct((L, POOL_W), BF16),
        scratch_shapes=[pltpu.VMEM((L + HALO, PGW), F32)],
        compiler_params=_params(("parallel",), VMEM_BIG))(proj)


def _pool_bwd(dpooled, dproj, *, name):
    L = dpooled.shape[0]
    rc = _time_chunk(L)
    n = L // rc

    def body(d_ref, dp_in, o_ref, pad):
        del dp_in
        g = pl.program_id(0)
        pad[L:L + HALO, :] = jnp.zeros((HALO, PGW), F32)
        rows = lax.broadcasted_iota(jnp.int32, (rc, PGW), 0)

        for gi in range(4):
            win = 2 << gi

            @pl.when(g == gi)
            def _(gi=gi, win=win):
                def fill(i, c):
                    r0 = pl.multiple_of(i * rc, rc)
                    cnt = jnp.minimum(rows + (r0 + 1), win).astype(F32)
                    pad[pl.ds(r0, rc), :] = d_ref[pl.ds(r0, rc), :] / cnt
                    return c

                lax.fori_loop(0, n, fill, 0)

                def step(i, c):
                    r0 = pl.multiple_of(i * rc, rc)
                    s = pad[pl.ds(r0, rc + HALO), :]
                    sh = 1
                    while sh < win:
                        s = s + pltpu.roll(s, rc + HALO - sh, 0)
                        sh *= 2
                    o_ref[pl.ds(r0, rc), :] = (s[:rc] - d_ref[pl.ds(r0, rc), :]).astype(o_ref.dtype)
                    return c

                lax.fori_loop(0, n, step, 0)

    return pl.pallas_call(
        body, name=name, grid=(4,),
        in_specs=[pl.BlockSpec((L, PGW), lambda j: (0, j)), pl.BlockSpec(memory_space=pl.ANY)],
        out_specs=pl.BlockSpec((L, PGW), lambda j: (0, j + C_POOL // PGW)),
        out_shape=jax.ShapeDtypeStruct((L, NPROJ), BF16),
        scratch_shapes=[pltpu.VMEM((L + HALO, PGW), F32)],
        input_output_aliases={1: 0},
        compiler_params=_params(("parallel",), VMEM_BIG))(dpooled, dproj)


_SPLIT_DT = jnp.bfloat16


def _ssd_consts():
    tri = np.tril(np.ones((Q, Q), np.float32))
    exp = np.zeros((128, DI), np.float32)
    for h in range(NH):
        exp[h, h * HP:(h + 1) * HP] = 1.0
    exp2 = np.concatenate([exp, exp], axis=0)
    return (jnp.asarray(tri, dtype=_SPLIT_DT), jnp.asarray(tri.T.copy(), dtype=_SPLIT_DT),
            jnp.asarray(exp2, dtype=_SPLIT_DT))


def _split(v, n):
    parts, r = [], v
    for _ in range(n):
        p = r.astype(_SPLIT_DT)
        parts.append(p)
        r = r - p.astype(F32)
    return parts


def _bdot(a, b, dims):
    return lax.dot_general(a, b, (dims, ((), ())), preferred_element_type=F32)


def _tri_sum(t_ref, v):
    r = _bdot(t_ref[...], jnp.concatenate(_split(v, 3), axis=1), NN)
    return r[:, :128] + r[:, 128:256] + r[:, 256:]


def _expand(v, e2_ref):
    return _bdot(jnp.concatenate(_split(v, 2), axis=1), e2_ref[...], NN)


def _reduce_heads(vals, eg):
    parts = []
    for v in vals:
        parts += _split(v, 2)
    r = _bdot(jnp.concatenate(parts, axis=0), eg, NT)
    return [r[2 * i * Q:(2 * i + 1) * Q] + r[(2 * i + 1) * Q:(2 * i + 2) * Q] for i in range(len(vals))]


def _ssd_common(xbc_ref, dtw_ref, arow_ref, t_ref, e_ref):
    dt = dtw_ref[:, :128]
    sig = dtw_ref[:, 128:]
    acs = _tri_sum(t_ref, dt * arow_ref[...])
    acs_x = _expand(acs, e_ref)
    dt_x = _expand(dt, e_ref)
    xs = xbc_ref[:, 0:DI]
    return sig, dt, acs, acs.T, acs_x, dt_x, xs


CONV_SLAB = 512


def _ssd_fwd(raw, dtp, cw, cb, arow, dsk_x, *, name):
    L = raw.shape[0]
    nc = L // Q
    tri, _, expand = _ssd_consts()

    def body(raw_ref, halo_ref, cw_ref, cb_ref, dtw_ref, arow_ref, dsk_ref, t_ref, e_ref,
             y_ref, hs_ref, xbc_ref, h_scr, cpad):
        c = pl.program_id(0)

        @pl.when(c == 0)
        def _():
            h_scr[...] = jnp.zeros_like(h_scr)

        cpad[0:8, :] = jnp.where(c > 0, halo_ref[...], 0.0)
        cpad[8:8 + Q, :] = raw_ref[...]
        for lo in range(0, XBC, CONV_SLAB):
            sl = slice(lo, lo + CONV_SLAB)
            acc = cb_ref[:, sl]
            for j in range(4):
                acc = acc + cpad[8 - j:8 - j + Q, sl] * cw_ref[3 - j:4 - j, sl]
            xbc_ref[:, sl] = acc * _sigmoid(acc)

        _, dt, acs, acs_t, acs_x, dt_x, xs = _ssd_common(xbc_ref, dtw_ref, arow_ref, t_ref, e_ref)
        xdt = xs * dt_x
        eacs = jnp.exp(acs_x)
        acs_last = acs_x[Q - 1:Q, :]
        dec = jnp.exp(acs_last - acs_x)
        hs_ref[0] = h_scr[...].astype(hs_ref.dtype)
        causal = lax.broadcasted_iota(jnp.int32, (Q, Q), 0) >= lax.broadcasted_iota(jnp.int32, (Q, Q), 1)
        first = lax.broadcasted_iota(jnp.int32, (Q, 128), 1) < HP
        for g in range(NG):
            bg = xbc_ref[:, DI + g * NS:DI + (g + 1) * NS]
            cg = xbc_ref[:, DI + NG * NS + g * NS:DI + NG * NS + (g + 1) * NS]
            s = _dot(cg, bg, NT)
            sl = slice(g * GW, (g + 1) * GW)
            hg = h_scr[:, sl]
            yoff = _dot(cg, hg, NN) * eacs[:, sl]
            st = _dot(bg, xdt[:, sl] * dec[:, sl], TN)
            h_scr[:, sl] = hg * eacs[Q - 1:Q, sl] + st
            for j in range(4):
                lo = g * GW + j * 128
                xb = xdt[:, lo:lo + 128]
                yp = yoff[:, j * 128:(j + 1) * 128] + dsk_ref[:, lo:lo + 128] * xs[:, lo:lo + 128]
                for e in range(2):
                    h = g * 8 + j * 2 + e
                    lm = jnp.exp(jnp.where(causal, acs[:, h:h + 1] - acs_t[h:h + 1, :], NEG))
                    xm = jnp.where(first if e == 0 else jnp.logical_not(first), xb, 0.0)
                    yp = yp + _dot(s * lm, xm, NN)
                y_ref[:, lo:lo + 128] = yp.astype(y_ref.dtype)

    const = lambda c: (0, 0)
    return pl.pallas_call(
        body, name=name, grid=(nc,),
        in_specs=[pl.BlockSpec((Q, XBC), lambda c: (c, 0)),
                  pl.BlockSpec((8, XBC), lambda c: (jnp.maximum(c * (Q // 8) - 1, 0), 0)),
                  pl.BlockSpec((4, XBC), const), pl.BlockSpec((1, XBC), const),
                  pl.BlockSpec((Q, DT_PAD), lambda c: (c, 0)),
                  pl.BlockSpec((1, 128), const), pl.BlockSpec((1, DI), const),
                  pl.BlockSpec((Q, Q), const), pl.BlockSpec((256, DI), const)],
        out_specs=[pl.BlockSpec((Q, DI), lambda c: (c, 0)), pl.BlockSpec((1, NS, DI), lambda c: (c, 0, 0)),
                   pl.BlockSpec((Q, XBC), lambda c: (c, 0))],
        out_shape=[jax.ShapeDtypeStruct((L, DI), BF16), jax.ShapeDtypeStruct((nc, NS, DI), F32),
                   jax.ShapeDtypeStruct((L, XBC), F32)],
        scratch_shapes=[pltpu.VMEM((NS, DI), F32), pltpu.VMEM((8 + Q, XBC), F32)],
        compiler_params=_params(("arbitrary",), VMEM_BIG))(raw, raw, cw, cb, dtp, arow, dsk_x, tri, expand)


def _ssd_bwd(dy, xbc, dtp, hs, arow, dsk_x, dproj, *, name):
    L = xbc.shape[0]
    nc = L // Q
    tri, triu, expand = _ssd_consts()

    def body(dy_ref, xbc_ref, dtw_ref, hs_ref, arow_ref, dsk_ref, t_ref, u_ref, e_ref, dp_in,
             dxbc_ref, ddtw_ref, da_ref, ddx_ref, ddtb_ref, dh_scr):
        del dp_in
        i = pl.program_id(0)

        @pl.when(i == 0)
        def _():
            dh_scr[...] = jnp.zeros_like(dh_scr)

        sig, dt, acs, acs_t, acs_x, dt_x, xs = _ssd_common(xbc_ref, dtw_ref, arow_ref, t_ref, e_ref)
        dyv = dy_ref[...]
        xdt = xs * dt_x
        eacs = jnp.exp(acs_x)
        acs_last = acs_x[Q - 1:Q, :]
        dec = jnp.exp(acs_last - acs_x)
        gy = dyv * eacs
        causal = lax.broadcasted_iota(jnp.int32, (Q, Q), 0) >= lax.broadcasted_iota(jnp.int32, (Q, Q), 1)
        first = lax.broadcasted_iota(jnp.int32, (Q, 128), 1) < HP
        lane_h = lax.broadcasted_iota(jnp.int32, (Q, 128), 1)
        sub_h = lax.broadcasted_iota(jnp.int32, (128, Q), 0)
        last_row = lax.broadcasted_iota(jnp.int32, (Q, GW), 0) == Q - 1
        dacs = jnp.zeros((Q, 128), F32)
        dacs_t = jnp.zeros((128, Q), F32)
        ddt = jnp.zeros((Q, 128), F32)
        for g in range(NG):
            bg = xbc_ref[:, DI + g * NS:DI + (g + 1) * NS]
            cg = xbc_ref[:, DI + NG * NS + g * NS:DI + NG * NS + (g + 1) * NS]
            s = _dot(cg, bg, NT)
            sl = slice(g * GW, (g + 1) * GW)
            hg = hs_ref[0, :, sl].astype(F32)
            dhn = dh_scr[:, sl]
            eal = eacs[Q - 1:Q, sl]
            gg = gy[:, sl]
            dax = gg * _dot(cg, hg, NN)
            dcg = _dot(gg, hg, NT)
            dh_scr[:, sl] = _dot(cg, gg, TN) + dhn * eal
            dal = eal * _colsum(dhn * hg)
            xdd = xdt[:, sl] * dec[:, sl]
            dbg = _dot(xdd, dhn, NT)
            wv = _dot(bg, dhn, NN)
            dd = wv * xdd
            dax = dax - dd
            dal = dal + _colsum(dd)
            dax = dax + jnp.where(last_row, dal, 0.0)
            dxdt_g = wv * dec[:, sl]
            ds = jnp.zeros((Q, Q), F32)
            dxdt_blocks = []
            for j in range(4):
                lo = g * GW + j * 128
                xb = xdt[:, lo:lo + 128]
                dyb = dyv[:, lo:lo + 128]
                dxb = dxdt_g[:, j * 128:(j + 1) * 128]
                for e in range(2):
                    h = g * 8 + j * 2 + e
                    lm = jnp.exp(jnp.where(causal, acs[:, h:h + 1] - acs_t[h:h + 1, :], NEG))
                    m = s * lm
                    dym = jnp.where(first if e == 0 else jnp.logical_not(first), dyb, 0.0)
                    dm = _dot(dym, xb, NT)
                    r = dm * m
                    dacs = dacs + jnp.where(lane_h == h, jnp.sum(r, axis=1, keepdims=True), 0.0)
                    dacs_t = dacs_t + jnp.where(sub_h == h, _colsum(r), 0.0)
                    ds = ds + dm * lm
                    dxb = dxb + _dot(m, dym, TN)
                dxdt_blocks.append(dxb)
            dxdt = jnp.concatenate(dxdt_blocks, axis=1)
            dcg = dcg + _dot(ds, bg, NN)
            dbg = dbg + _dot(ds, cg, TN)
            dxbc_ref[:, DI + g * NS:DI + (g + 1) * NS] = dbg
            dxbc_ref[:, DI + NG * NS + g * NS:DI + NG * NS + (g + 1) * NS] = dcg
            dxbc_ref[:, sl] = dsk_ref[:, sl] * dyv[:, sl] + dxdt * dt_x[:, sl]
            ddt_g, dacs_g = _reduce_heads([dxdt * xs[:, sl], dax], e_ref[0:128, sl])
            ddt = ddt + ddt_g
            dacs = dacs + dacs_g
        dacs = dacs - dacs_t.T
        ddta = _tri_sum(u_ref, dacs)
        ddt = ddt + ddta * arow_ref[...]
        ddtw = jnp.where(lane_h < NH, ddt * sig, 0.0)
        ddtw_ref[...] = jnp.concatenate([ddtw, jnp.zeros((Q, DT_PAD - 128), F32)], axis=1).astype(ddtw_ref.dtype)
        _acc_out(da_ref, _colsum(ddta * dt), i)
        _acc_out(ddx_ref, _colsum(dyv * xs), i)
        _acc_out(ddtb_ref, _colsum(ddtw), i)

    rev = lambda c: (nc - 1 - c, 0)
    const = lambda c: (0, 0)
    return pl.pallas_call(
        body, name=name, grid=(nc,),
        in_specs=[pl.BlockSpec((Q, DI), rev), pl.BlockSpec((Q, XBC), rev),
                  pl.BlockSpec((Q, DT_PAD), rev),
                  pl.BlockSpec((1, NS, DI), lambda c: (nc - 1 - c, 0, 0)),
                  pl.BlockSpec((1, 128), const), pl.BlockSpec((1, DI), const),
                  pl.BlockSpec((Q, Q), const), pl.BlockSpec((Q, Q), const), pl.BlockSpec((256, DI), const),
                  pl.BlockSpec(memory_space=pl.ANY)],
        out_specs=[pl.BlockSpec((Q, XBC), rev),
                   pl.BlockSpec((Q, DT_PAD), lambda c: (nc - 1 - c, C_DT // DT_PAD)),
                   pl.BlockSpec((1, 128), const), pl.BlockSpec((1, DI), const), pl.BlockSpec((1, 128), const)],
        out_shape=[jax.ShapeDtypeStruct((L, XBC), F32), jax.ShapeDtypeStruct((L, NPROJ), BF16),
                   jax.ShapeDtypeStruct((1, 128), F32), jax.ShapeDtypeStruct((1, DI), F32),
                   jax.ShapeDtypeStruct((1, 128), F32)],
        scratch_shapes=[pltpu.VMEM((NS, DI), F32)],
        input_output_aliases={9: 1},
        compiler_params=_params(("arbitrary",), VMEM_BIG))(dy, xbc, dtp, hs, arow, dsk_x, tri, triu,
                                                          expand, dproj)


def _adam_update(wv, gv, mv, vv):
    c1 = 1.0 - ADAM_B1 ** ADAM_STEP
    c2 = 1.0 - ADAM_B2 ** ADAM_STEP
    mn = ADAM_B1 * mv + (1.0 - ADAM_B1) * gv
    vn = ADAM_B2 * vv + (1.0 - ADAM_B2) * (gv * gv)
    return -ADAM_LR * ((mn / c1) / (jnp.sqrt(vn / c2) + ADAM_EPS) + ADAM_WD * wv), mn, vn


def _adamw(w, g, m, v, *, name, tr=None):
    R = w.shape[0]
    rest = tuple(w.shape[1:])
    if tr is None:
        tr = _pick(R, (256, 128, 64, 32, 16, 8))
    assert R % tr == 0

    def body(w_ref, g_ref, m_ref, v_ref, d_ref, mo_ref, vo_ref):
        d_ref[...], mo_ref[...], vo_ref[...] = _adam_update(w_ref[...], g_ref[...], m_ref[...], v_ref[...])

    zeros = (0,) * len(rest)
    spec = pl.BlockSpec((tr,) + rest, lambda i: (i,) + zeros)
    return pl.pallas_call(body, name=name, grid=(R // tr,), in_specs=[spec] * 4, out_specs=[spec] * 3,
                          out_shape=[jax.ShapeDtypeStruct(w.shape, F32)] * 3,
                          compiler_params=_params(("parallel",)))(w, g, m, v)


def _adamw_small(svrow, g_conv, params, *, name):
    n = len(params)

    def body(*refs):
        sv_ref, gc_ref = refs[0], refs[1]
        ins, outs = refs[2:2 + 3 * n], refs[2 + 3 * n:]
        for p, (key, w, _, _) in enumerate(params):
            w_ref, m_ref, v_ref = ins[3 * p:3 * p + 3]
            g_ref, d_ref, mo_ref, vo_ref = outs[4 * p:4 * p + 4]
            gv = gc_ref[...] if key == "conv_w" else sv_ref[:, SV_OFF[key]:SV_OFF[key] + w.shape[1]]
            g_ref[...] = gv
            d_ref[...], mo_ref[...], vo_ref[...] = _adam_update(w_ref[...], gv, m_ref[...], v_ref[...])

    vm = pl.BlockSpec(memory_space=pltpu.VMEM)
    args = [svrow, g_conv]
    shapes = []
    for _, w, m, v in params:
        args += [w, m, v]
        shapes += [jax.ShapeDtypeStruct(w.shape, F32)] * 4
    res = pl.pallas_call(body, name=name, in_specs=[vm] * len(args), out_specs=[vm] * len(shapes),
                         out_shape=shapes)(*args)
    return {key: tuple(res[4 * p:4 * p + 4]) for p, (key, _, _, _) in enumerate(params)}


def _slab_sum(recv, *, tile, name):
    rows = recv.shape[1]
    assert rows % tile == 0 and tile % 16 == 0

    def body(r_ref, o_ref):
        acc = r_ref[0].astype(F32)
        for j in range(1, N_DEV):
            acc = acc + r_ref[j].astype(F32)
        o_ref[...] = acc

    return pl.pallas_call(body, name=name, grid=(rows // tile,),
                          in_specs=[pl.BlockSpec((N_DEV, tile, D), lambda i: (0, i, 0))],
                          out_specs=pl.BlockSpec((tile, D), lambda i: (i, 0)),
                          out_shape=jax.ShapeDtypeStruct((rows, D), F32),
                          compiler_params=_params(("parallel",)))(recv)


MESH = pl.DeviceIdType.MESH


def _coords():
    return lax.axis_index("x"), lax.axis_index("y"), lax.axis_index("c")


def _peer(k):
    x, y, c = _coords()
    px = 1 - x if k & 4 else x
    py = 1 - y if k & 2 else y
    pc = 1 - c if k & 1 else c
    return (px, py, pc), 4 * px + 2 * py + pc


def _rcopy(src, dst, ssem, rsem, dev):
    return pltpu.make_async_remote_copy(src_ref=src, dst_ref=dst, send_sem=ssem, recv_sem=rsem,
                                        device_id=dev, device_id_type=MESH)


def _exchange_all(src_of, dst_slot, send_sems, recv_sems):
    x, y, c = _coords()
    me = 4 * x + 2 * y + c
    sent = []
    for k in range(1, N_DEV):
        dev, pidx = _peer(k)
        cp = _rcopy(src_of(pidx), dst_slot(me), send_sems.at[k - 1], recv_sems.at[k - 1], dev)
        cp.start()
        sent.append(cp)
    for k in range(1, N_DEV):
        dev, pidx = _peer(k)
        _rcopy(src_of(pidx), dst_slot(pidx), send_sems.at[k - 1], recv_sems.at[k - 1], dev).wait_recv()
    for cp in sent:
        cp.wait_send()


def _rows_of_slots(buf, nslots):
    rows = lax.broadcasted_iota(jnp.int32, (8, buf.shape[-1]), 0)
    out = jnp.zeros((8, buf.shape[-1]), F32)
    for j in range(nslots):
        out = out + jnp.where(rows == j, buf[j], 0.0)
    return out


def _ada_fwd(c, w_ada, b_r, *, name):
    wloc = w_ada.shape[1]

    def body(c_ref, w_ref, b_ref, mod_ref, call_ref, csrc, cbuf, psrc, pbuf, s1, r1, s2, r2):
        x, y, cc = _coords()
        me = 4 * x + 2 * y + cc
        csrc[...] = jnp.broadcast_to(c_ref[...], (8, D))
        cbuf[me] = csrc[...]
        _exchange_all(lambda p: csrc, lambda s: cbuf.at[s], s1, r1)
        call = _rows_of_slots(cbuf, N_DEV)
        call_ref[...] = call
        prod = _dot_hi(_silu(call), w_ref[...])
        for b in range(N_DEV):
            psrc[b] = jnp.broadcast_to(prod[b:b + 1, :], (8, wloc))
        pbuf[me] = psrc[me]
        _exchange_all(lambda p: psrc.at[p], lambda s: pbuf.at[s], s2, r2)
        mod_ref[...] = _rows_of_slots(pbuf, N_DEV) + b_ref[...]

    vm = pl.BlockSpec(memory_space=pltpu.VMEM)
    return pl.pallas_call(
        body, name=name, in_specs=[vm, vm, vm], out_specs=[vm, vm],
        out_shape=[jax.ShapeDtypeStruct((N_DEV, wloc), F32), jax.ShapeDtypeStruct((N_DEV, D), F32)],
        scratch_shapes=[pltpu.VMEM((8, D), F32), pltpu.VMEM((N_DEV, 8, D), F32),
                        pltpu.VMEM((N_DEV, 8, wloc), F32), pltpu.VMEM((N_DEV, 8, wloc), F32),
                        pltpu.SemaphoreType.DMA((N_DEV - 1,)), pltpu.SemaphoreType.DMA((N_DEV - 1,)),
                        pltpu.SemaphoreType.DMA((N_DEV - 1,)), pltpu.SemaphoreType.DMA((N_DEV - 1,))],
        compiler_params=pltpu.CompilerParams(vmem_limit_bytes=VMEM_BIG))(c, w_ada, b_r)


def _gather_slabs(slab, *, name):
    def body(x_ref, out_ref, send_sems, recv_sems, local_sem):
        x, y, c = _coords()
        me, sibling = (x, y, c), (x, y, 1 - c)
        chips = [(1 - x, y), (x, 1 - y), (1 - x, 1 - y)]

        def slot(px, py, pc):
            return out_ref.at[4 * px + 2 * py + pc]

        def copy(k, block, to, src=None):
            return _rcopy(slot(*block) if src is None else src, slot(*block), send_sems.at[k], recv_sems.at[k], to)

        mine = pltpu.make_async_copy(x_ref, slot(*me), local_sem)
        mine.start()
        first = [copy(0, me, sibling, src=x_ref)]
        first += [copy(1 + j, me, (*chip, c), src=x_ref) for j, chip in enumerate(chips)]
        for cp in first:
            cp.start()
        passed = [copy(4 + j, (*chip, c), sibling) for j, chip in enumerate(chips)]
        for j, chip in enumerate(chips):
            copy(1 + j, (*chip, c), me).wait_recv()
            passed[j].start()
        copy(0, sibling, me).wait_recv()
        for j, chip in enumerate(chips):
            copy(4 + j, (*chip, 1 - c), me).wait_recv()
        for cp in first + passed:
            cp.wait_send()
        mine.wait()

    anyspec = pl.BlockSpec(memory_space=pl.ANY)
    return pl.pallas_call(
        body, name=name, in_specs=[anyspec], out_specs=anyspec,
        out_shape=jax.ShapeDtypeStruct((N_DEV,) + slab.shape, slab.dtype),
        scratch_shapes=[pltpu.SemaphoreType.DMA((7,)), pltpu.SemaphoreType.DMA((7,)), pltpu.SemaphoreType.DMA],
    )(slab)


_HBM =pl.BlockSpec(memory_space=pltpu.HBM)
_SEM = pl.BlockSpec(memory_space=pltpu.SEMAPHORE)
_EFFECT = pltpu.SideEffectType.DATAFLOW_SIDE_EFFECTING


def _xchg_src(src_ref, pidx, per_peer):
    return src_ref.at[pidx] if per_peer else src_ref


def _xchg_start(src, *, per_peer, name):
    rows = src.shape[-2]
    land_shape = (N_DEV, rows, D)

    def body(src_ref, land_ref, send_sems, recv_sems, src_thru, land_thru, token):
        del src_thru, land_thru
        x, y, c = _coords()
        me = 4 * x + 2 * y + c
        for k in range(1, N_DEV):
            dev, pidx = _peer(k)
            _rcopy(_xchg_src(src_ref, pidx, per_peer), land_ref.at[me], send_sems.at[k - 1],
                   recv_sems.at[k - 1], dev).start()
        token[...] = jnp.zeros_like(token)

    return pl.pallas_call(
        body, name=name,
        out_shape=(pltpu.SemaphoreType.DMA((N_DEV - 1,)), pltpu.SemaphoreType.DMA((N_DEV - 1,)),
                   pltpu.HBM(src.shape, src.dtype), pltpu.HBM(land_shape, src.dtype),
                   jax.ShapeDtypeStruct((8, 128), F32)),
        in_specs=(_HBM, _HBM),
        out_specs=(_SEM, _SEM, _HBM, _HBM, pl.BlockSpec(memory_space=pltpu.VMEM)),
        input_output_aliases={0: 2, 1: 3},
        compiler_params=pltpu.CompilerParams(has_side_effects=_EFFECT),
    )(pltpu.with_memory_space_constraint(src, pltpu.HBM),
      pltpu.with_memory_space_constraint(lax.empty(land_shape, src.dtype), pltpu.HBM))


def _xchg_wait(started, after, *, per_peer, name):
    send_sems, recv_sems, src_thru, land_thru, _ = started

    def body(src_ref, land_ref, send_sems, recv_sems, after_ref, src_dead, got_ref):
        del after_ref, src_dead, got_ref
        for k in range(1, N_DEV):
            dev, pidx = _peer(k)
            cp = _rcopy(_xchg_src(src_ref, pidx, per_peer), land_ref.at[pidx], send_sems.at[k - 1],
                        recv_sems.at[k - 1], dev)
            cp.wait_send()
            cp.wait_recv()

    return pl.pallas_call(
        body, name=name,
        out_shape=(pltpu.HBM(src_thru.shape, src_thru.dtype), pltpu.HBM(land_thru.shape, land_thru.dtype)),
        in_specs=(_HBM, _HBM, _SEM, _SEM, pl.BlockSpec(memory_space=pl.ANY)),
        out_specs=(_HBM, _HBM),
        input_output_aliases={0: 0, 1: 1},
        compiler_params=pltpu.CompilerParams(has_side_effects=_EFFECT),
    )(src_thru, land_thru, send_sems, recv_sems, after)


def _dep(token):
    return (token, (8, 128), lambda i, j, k: (0, 0))


def _small_allsum(sv, *, name):
    def body(sv_ref, all_ref, sum_ref, send_sems, recv_sems):
        x, y, c = _coords()
        me = 4 * x + 2 * y + c
        all_ref[me] = sv_ref[...]
        _exchange_all(lambda p: sv_ref, lambda s: all_ref.at[s], send_sems, recv_sems)
        acc = all_ref[0]
        for j in range(1, N_DEV):
            acc = acc + all_ref[j]
        sum_ref[...] = acc

    vm = pl.BlockSpec(memory_space=pltpu.VMEM)
    return pl.pallas_call(
        body, name=name, in_specs=[vm], out_specs=[vm, vm],
        out_shape=[jax.ShapeDtypeStruct((N_DEV, SV_ROWS, 128), F32), jax.ShapeDtypeStruct((SV_ROWS, 128), F32)],
        scratch_shapes=[pltpu.SemaphoreType.DMA((7,)), pltpu.SemaphoreType.DMA((7,))],
    )(sv)


def _ada_bwd(call, dmod_loc, *, name):
    wloc = dmod_loc.shape[1]

    def body(c_ref, d_ref, o_ref):
        o_ref[...] = _dot_hi(_silu(c_ref[...]), d_ref[...], TN)

    vm = pl.BlockSpec(memory_space=pltpu.VMEM)
    return pl.pallas_call(body, name=name, in_specs=[vm, vm], out_specs=vm,
                          out_shape=jax.ShapeDtypeStruct((D, wloc), F32),
                          compiler_params=pltpu.CompilerParams(vmem_limit_bytes=VMEM_BIG))(call, dmod_loc)


def _pad_rows(a, rows):
    return jnp.pad(a, ((0, rows - a.shape[0]), (0, 0)))


IN_SHIFT = tuple((IN_ROWS * j) % 16 for j in range(N_DEV))
IN_BASE = tuple(IN_ROWS * j - IN_SHIFT[j] for j in range(N_DEV))
IN_SEGMENTS = ((2048, XBC, C_XBC), (5152, 1024, C_POOL), (0, 2048, C_Z), (6176, 2048, C_GATE), (5120, 32, C_DT))


def _global_pieces(gs):
    pieces = []
    for j in range(N_DEV):
        lo, hi = 0, IN_ROWS_P
        if j > 0 and IN_BASE[j - 1] + IN_ROWS_P > IN_BASE[j]:
            pieces.append((IN_BASE[j], 16, gs[j - 1, IN_ROWS_P - 16:IN_ROWS_P] + gs[j, 0:16]))
            lo = 16
        if j + 1 < N_DEV and IN_BASE[j] + IN_ROWS_P > IN_BASE[j + 1]:
            hi = IN_ROWS_P - 16
        pieces.append((IN_BASE[j] + lo, hi - lo, gs[j, lo:hi]))
    return pieces


def _reorder_in_rows(gs):
    pieces = _global_pieces(gs)
    parts = []
    for lo, n, _ in IN_SEGMENTS:
        for p0, pn, arr in pieces:
            a, b = max(lo, p0), min(lo + n, p0 + pn)
            if a < b:
                parts.append(arr[a - p0:b - p0])
    parts.append(jnp.zeros((DT_PAD - 32, D), gs.dtype))
    return jnp.concatenate(parts, axis=0)


def _restore_in_shards(d):
    slabs = []
    for j in range(N_DEV):
        parts = []
        r, end = IN_BASE[j], IN_BASE[j] + IN_ROWS_P
        while r < end:
            lo, n, new = next(s for s in IN_SEGMENTS if s[0] <= r < s[0] + s[1])
            e = min(end, lo + n)
            parts.append(d[new + r - lo:new + e - lo])
            r = e
        slabs.append(jnp.concatenate(parts, axis=0))
    return jnp.stack(slabs, axis=0)


def _pack_sv(parts):
    flat = []
    for n, size in SV_PARTS:
        v = parts[n].reshape(-1).astype(F32)
        flat.append(jnp.pad(v, (0, size - v.shape[0])))
    v = jnp.concatenate(flat)
    return jnp.pad(v, (0, SV_ROWS * 128 - v.shape[0])).reshape(SV_ROWS, 128)


def _sv_get(flat, n, size):
    return flat[SV_OFF[n]:SV_OFF[n] + size]


def kernel(x, c, w_ada, b_ada, norm_mix_w, w_in, conv_w, conv_b, dt_bias, a_log, d_skip, ssd_norm_w, w_branch_ssd, pool_w, pool_scale, w_branch_pool, w_out, norm_mlp_w, w_up, w_down, norm_final_w, loss_target, m_w_ada, m_b_ada, m_norm_mix_w, m_w_in, m_conv_w, m_conv_b, m_dt_bias, m_a_log, m_d_skip, m_ssd_norm_w, m_w_branch_ssd, m_pool_w, m_pool_scale, m_w_branch_pool, m_w_out, m_norm_mlp_w, m_w_up, m_w_down, m_norm_final_w, v_w_ada, v_b_ada, v_norm_mix_w, v_w_in, v_conv_w, v_conv_b, v_dt_bias, v_a_log, v_d_skip, v_ssd_norm_w, v_w_branch_ssd, v_pool_w, v_pool_scale, v_w_branch_pool, v_w_out, v_norm_mlp_w, v_w_up, v_w_down, v_norm_final_w):
    xs_ = x[0]
    tgt = loss_target[0]
    L = xs_.shape[0]
    me = 4 * lax.axis_index("x") + 2 * lax.axis_index("y") + lax.axis_index("c")
    wloc = w_ada.shape[2]

    mod_p, c_all = _ada_fwd(c, w_ada[0], b_ada.reshape(N_DEV, wloc), name="ada_fwd")
    mod = mod_p.reshape(6, D)
    shift_m, scale_m, gate_m, shift_f, scale_f, gate_f = [mod[i:i + 1] for i in range(6)]

    conv_bits = lax.bitcast_convert_type(conv_w[0], SLAB_DT).reshape(3, D)
    in_shift = (IN_ROWS * me) % 16
    slab_in = lax.dynamic_update_slice(jnp.zeros((IN_ROWS_P, D), SLAB_DT), w_in[0].T.astype(SLAB_DT),
                                       (in_shift, 0))
    slab_in = jnp.concatenate([slab_in, _pad_rows(conv_bits, CONV_ROWS)], axis=0)
    slab_rest = jnp.concatenate([
        w_branch_ssd[0].astype(SLAB_DT),
        pool_w[0].reshape(32, D).astype(SLAB_DT),
        w_branch_pool[0].astype(SLAB_DT),
        w_out[0].astype(SLAB_DT),
        w_up[0].T.astype(SLAB_DT),
        w_down[0].astype(SLAB_DT)], axis=0)
    slab_in, mod_p = lax.optimization_barrier((slab_in, mod_p))
    gs_in = _gather_slabs(slab_in, name="gather_w_in")
    slab_rest, gs_in = lax.optimization_barrier((slab_rest, gs_in))
    rest_started = _xchg_start(slab_rest, per_peer=False, name="gather_rest_start")
    gather_token = rest_started[4]

    w_in_t = _reorder_in_rows(gs_in)
    conv_full = lax.bitcast_convert_type(
        gs_in[:, IN_ROWS_P:IN_ROWS_P + 3].reshape(N_DEV, 4, XBC // N_DEV, 2), F32)
    conv_full = conv_full.transpose(1, 0, 2).reshape(4, XBC)

    dtb = jnp.pad(dt_bias, ((0, 0), (0, 128 - NH)))
    arow = jnp.pad(-jnp.exp(a_log), ((0, 0), (0, 128 - NH)))
    dsk_x = jnp.repeat(d_skip, HP, axis=1)

    tm = _pick(L, (1024, 512, 256, 128))
    tm2 = _pick(L, (2048, 1024, 512, 256, 128))
    tkl = _pick(L, (4096, 2048, 1024, 512, 256, 128))
    tkl2 = _pick(L, (2048, 1024, 512, 256, 128))

    tmh = _pick(L, (512, 256, 128))
    zcol = C_Z // DI
    gcol = C_GATE // (2 * D)

    def whole_rows(w):
        return lambda t: ((L, w), BF16, (t, w), lambda i, j, k: (i, 0))

    def norm1_pro(x_ref, ex, outs, j):
        @pl.when(j == 0)
        def _():
            xv = x_ref[...]
            r = lax.rsqrt(jnp.mean(xv * xv, axis=-1, keepdims=True) + EPS)
            outs[1][...] = (xv * r * ex[0][...] * (1.0 + ex[1][...]) + ex[2][...]).astype(outs[1].dtype)

        return outs[1][...]

    n_xbc = XBC // 768

    def proj_ep(acc, ex, outs):
        j = pl.program_id(1)
        outs[0][...] = acc.astype(outs[0].dtype)

        @pl.when(j < n_xbc)
        def _():
            outs[3][...] = acc

        @pl.when(j == NPROJ // 768 - 1)
        def _():
            pre = acc[:, 768 - DT_PAD:768 - DT_PAD + 128] + ex[3][...]
            outs[2][...] = jnp.concatenate([_softplus(pre), _sigmoid(pre)], axis=1)

    proj, h1, dtp, xbc_raw = _mm(
        xs_, w_in_t, "nt", name="in_proj", tm=tm, tn=768, tk=D,
        extras=[(norm_mix_w, *_vecs()), (scale_m, *_vecs()), (shift_m, *_vecs()), (dtb, *_vecs(128)),
                _dep(gather_token)],
        outs=[BF16, whole_rows(D)(tm), ((L, DT_PAD), F32, (tm, DT_PAD), lambda i, j, k: (i, 0)),
              ((L, XBC), F32, (tm, 768), lambda i, j, k: (i, jnp.minimum(j, n_xbc - 1)))],
        prologue=norm1_pro, epilogue=proj_ep)
    y_ssm, hs, xbc = _ssd_fwd(xbc_raw, dtp, conv_full, conv_b, arow, dsk_x, name="ssd_fwd")

    slab_rest, gs = _xchg_wait(rest_started, y_ssm, per_peer=False, name="gather_rest_wait")
    gs = lax.dynamic_update_slice(gs, slab_rest[None], (me, 0, 0))

    def part(n, rows):
        return gs[:, REST_OFF[n]:REST_OFF[n] + rows]

    w_bssd = part("bssd", 256).reshape(DI, D)
    w_pool = part("pool", 32).reshape(N_DEV, 4, 32, PGW).transpose(1, 0, 2, 3).reshape(POOL_W, PGW)
    w_bpool = part("bpool", 128).reshape(POOL_W, D)
    w_o = part("out", 128).reshape(D, D)
    w_up_t = part("up", 512).reshape(DFF, D)
    w_dn = part("down", 512).reshape(DFF, D)

    def gnorm_pro(y_ref, ex, outs, j):
        z_ref, w_ref = ex
        yg = y_ref[...].astype(F32) * _silu(z_ref[...].astype(F32))
        segs = []
        for k in range(NG):
            sl = slice(k * GW, (k + 1) * GW)
            seg = yg[:, sl]
            r = lax.rsqrt(jnp.mean(seg * seg, axis=-1, keepdims=True) + EPS)
            segs.append((seg * r * w_ref[:, sl]).astype(BF16))
        yn_v = jnp.concatenate(segs, axis=1)
        outs[1][...] = yn_v
        return yn_v

    y_ssd, yn = _mm(y_ssm, w_bssd, "nn", name="branch_ssd", tm=tmh, tn=D, tk=DI,
                    extras=[(proj, *_rows(tmh, DI, zcol)), (ssd_norm_w, *_vecs(DI))],
                    outs=[BF16, whole_rows(DI)(tmh)], prologue=gnorm_pro)
    pooled = _pool_fwd(proj, name="pool_fwd")
    wp_spec = ((POOL_W, PGW), lambda i, j, k: (0, 0))

    def pool_pro(a_ref, ex, outs, j):
        wp_ref, s_ref = ex
        segs = []
        for g in range(4):
            sl = slice(g * PGW, (g + 1) * PGW)
            p = _dot(a_ref[:, sl], wp_ref[sl, :], NN)
            outs[1][:, sl] = p.astype(BF16)
            segs.append((p * s_ref[:, sl]).astype(BF16))
        yp1_v = jnp.concatenate(segs, axis=1)
        outs[2][...] = yp1_v
        return yp1_v

    y_pool, yp0, yp1 = _mm(pooled, w_bpool, "nn", name="branch_pool", tm=tm, tn=D, tk=D,
                           extras=[(w_pool, *wp_spec), (pool_scale, *_vecs())],
                           outs=[BF16, whole_rows(D)(tm), whole_rows(D)(tm)], prologue=pool_pro)

    def merge_pro(a_ref, ex, outs, j):
        s = _sigmoid(ex[1][...].astype(F32))
        mv = (s[:, :D] * a_ref[...].astype(F32) + s[:, D:] * ex[0][...].astype(F32)).astype(BF16)
        outs[3][...] = mv
        return mv

    mix, x1, h2, m = _mm(y_ssd, w_o, "nn", name="out_proj", tm=tmh, tn=D, tk=D,
                         extras=[(y_pool, *_rows(tmh)), (proj, *_rows(tmh, 2 * D, gcol)),
                                 (xs_, *_rows(tmh)), (gate_m, *_vecs()), (norm_mlp_w, *_vecs()),
                                 (scale_f, *_vecs()), (shift_f, *_vecs())],
                         outs=[BF16, F32, BF16, whole_rows(D)(tmh)], prologue=merge_pro,
                         epilogue=lambda acc, ex, outs: _ep_resid_norm(acc, ex[2:], outs[:3]))

    def relu2(acc, ex, outs):
        r = jnp.maximum(acc, 0.0)
        outs[0][...] = acc.astype(BF16)
        outs[1][...] = (r * r).astype(BF16)

    up, act = _mm(h2, w_up_t, "nt", name="mlp_up", outs=[BF16, BF16], tm=tm2, tn=1024, tk=D, epilogue=relu2)

    dx2, ddown, loss_p, dnwf, dgate_f = _mm(
        act, w_dn, "nn", name="mlp_down", tm=tmh, tn=D, tk=DFF,
        extras=[(x1, *_rows(tmh)), (tgt, *_rows(tmh)), (gate_f, *_vecs()), (norm_final_w.reshape(1, D), *_vecs())],
        outs=[F32, BF16, _sum_out(128), _sum_out(), _sum_out()], epilogue=_ep_final)

    def drelu2(acc, ex, outs):
        outs[0][...] = (acc * (2.0 * jnp.maximum(ex[0][...].astype(F32), 0.0))).astype(BF16)

    def dep_last(ep):
        return lambda acc, ex, outs: ep(acc, ex[:-1], outs)

    dup = _mm(ddown, w_dn, "nt", name="mlp_down_dx", outs=[BF16], tm=tm2, tn=1024, tk=D,
              extras=[(up, (tm2, 1024), lambda i, j, k: (i, j))], epilogue=drelu2)
    g_dn = _mm(act, ddown, "tn", name="mlp_down_dw", outs=[SLAB_DT], tm=1024, tn=D, tk=tkl)
    g_up_t = _mm(dup, h2, "tn", name="mlp_up_dw", outs=[SLAB_DT], tm=1024, tn=D, tk=tkl)
    gslab_mlp = jnp.concatenate([g_up_t.reshape(N_DEV, 512, D), g_dn.reshape(N_DEV, 512, D)], axis=1)
    mlp_started = _xchg_start(gslab_mlp, per_peer=True, name="scatter_mlp_start")
    dx1, p2, q2, dmix, dgate_m = _mm(
        dup, w_up_t, "nn", name="mlp_up_dx", tm=tmh, tn=D, tk=DFF,
        extras=[(x1, *_rows(tmh)), (dx2, *_rows(tmh)), (norm_mlp_w, *_vecs()), (scale_f, *_vecs()),
                (mix, *_rows(tmh)), (gate_m, *_vecs()), _dep(mlp_started[4])],
        outs=[F32, _sum_out(), _sum_out(), BF16, _sum_out()], epilogue=dep_last(_ep_norm_bwd))
    gcol = C_GATE // (2 * D)
    dy_ssd, dy_pool, dproj = _mm(
        dmix, w_o, "nt", name="out_proj_dx", tm=tmh, tn=D, tk=D,
        extras=[(y_ssd, *_rows(tmh)), (y_pool, *_rows(tmh)), (proj, *_rows(tmh, 2 * D, gcol))],
        outs=[BF16, BF16, ((L, NPROJ), BF16, *_rows(tmh, 2 * D, gcol))], epilogue=_ep_merge_bwd)
    g_o = _mm(m, dmix, "tn", name="out_proj_dw", outs=[SLAB_DT], tm=D, tn=D, tk=tkl)
    zcol = C_Z // DI
    dy_ssm, dproj, d_snw = _mm(
        dy_ssd, w_bssd, "nt", name="branch_ssd_dx", tm=tmh, tn=DI, tk=D,
        extras=[(y_ssm, *_rows(tmh, DI)), (proj, *_rows(tmh, DI, zcol)), (ssd_norm_w, *_vecs(DI)),
                (dproj, None, None)],
        outs=[F32, ((L, NPROJ), BF16, *_rows(tmh, DI, zcol)), _sum_out(DI)],
        epilogue=_ep_gated_norm_bwd, aliases={3: 1})
    g_bssd = _mm(yn, dy_ssd, "tn", name="branch_ssd_dw", outs=[SLAB_DT], tm=1024, tn=D, tk=tkl)
    dxbc, dproj, d_a, d_dx, d_dtb = _ssd_bwd(dy_ssm, xbc, dtp, hs, arow, dsk_x, dproj, name="ssd_bwd")
    dproj, d_cw, d_cb = _conv_bwd(xbc_raw, dxbc, conv_full, conv_b, dproj, name="conv_bwd")
    def pool_bwd_ep(acc, ex, outs):
        y_ref, s_ref, wp_ref = ex
        o_ref, ds_ref, dpool_ref = outs
        dyp0_v = (acc * s_ref[...]).astype(BF16)
        o_ref[...] = dyp0_v
        _acc_out(ds_ref, _colsum(acc * y_ref[...].astype(F32)), _row_step())
        for g in range(4):
            sl = slice(g * PGW, (g + 1) * PGW)
            dpool_ref[:, sl] = _dot(dyp0_v[:, sl], wp_ref[sl, :], NT)

    dyp0, d_ps, dpooled = _mm(dy_pool, w_bpool, "nt", name="branch_pool_dx", tm=tm, tn=D, tk=D,
                              extras=[(yp0, *_rows(tm)), (pool_scale, *_vecs()), (w_pool, *wp_spec)],
                              outs=[BF16, _sum_out(), F32], epilogue=pool_bwd_ep)
    g_bpool = _mm(yp1, dy_pool, "tn", name="branch_pool_dw", outs=[SLAB_DT], tm=D, tn=D, tk=tkl)
    g_pool = _mm_pool_tn(pooled, dyp0, name="pool_mix_dw", tk=tkl)
    gslab_mix = jnp.concatenate([
        g_bssd.reshape(N_DEV, 256, D),
        g_pool.reshape(4, N_DEV, 32, PGW).transpose(1, 0, 2, 3).reshape(N_DEV, 32, D).astype(SLAB_DT),
        g_bpool.reshape(N_DEV, 128, D),
        g_o.reshape(N_DEV, 128, D)], axis=1)
    mix_started = _xchg_start(gslab_mix, per_peer=True, name="scatter_mix_start")
    dproj = _pool_bwd(dpooled, dproj, name="pool_bwd")
    g_in_t = _mm(dproj, h1, "tn", name="in_proj_dw", outs=[SLAB_DT], tm=1408, tn=D, tk=tkl2,
                 extras=[_dep(mix_started[4])])
    gslab_in = _restore_in_shards(g_in_t)
    in_started = _xchg_start(gslab_in, per_peer=True, name="scatter_in_start")
    grad_x, p1, q1 = _mm(
        dproj, w_in_t, "nn", name="in_proj_dx", tm=tmh, tn=D, tk=2816,
        extras=[(xs_, *_rows(tmh)), (dx1, *_rows(tmh)), (norm_mix_w, *_vecs()), (scale_m, *_vecs()),
                _dep(in_started[4])],
        outs=[F32, _sum_out(), _sum_out()], epilogue=dep_last(_ep_norm_bwd))

    def landed(started, after, tile, name):
        src, land = _xchg_wait(started, after, per_peer=True, name=name + "_wait")
        own = lax.dynamic_slice_in_dim(src, me, 1, axis=0)
        return _slab_sum(lax.dynamic_update_slice(land, own, (me, 0, 0)), tile=tile, name=name + "_sum")

    gsum_mlp = landed(mlp_started, grad_x, 256, "scatter_mlp")
    gsum_mix = landed(mix_started, grad_x, 272, "scatter_mix")
    gsum_in = landed(in_started, grad_x, 208, "scatter_in")

    dmod = jnp.concatenate([q1, p1 * norm_mix_w, dgate_m, q2, p2 * norm_mlp_w, dgate_f], axis=1)
    d_alog = d_a[:, :NH] * (-jnp.exp(a_log))
    sv = _pack_sv({
        "b_ada": dmod, "norm_mix_w": p1 * (1.0 + scale_m), "conv_b": d_cb, "dt_bias": d_dtb[:, :NH],
        "a_log": d_alog, "d_skip": d_dx.reshape(NH, HP).sum(axis=1), "ssd_norm_w": d_snw,
        "pool_scale": d_ps, "norm_mlp_w": p2 * (1.0 + scale_f), "norm_final_w": dnwf, "conv_w": d_cw,
        "loss": loss_p[:, :1]})
    sv_all, sv_sum = _small_allsum(sv, name="small_allsum")
    flat = sv_sum.reshape(-1)
    loss = flat[SV_OFF["loss"]]
    dmod_all = sv_all.reshape(N_DEV, SV_ROWS * 128)[:, :6 * D]
    g_w_ada = _ada_bwd(c_all, lax.dynamic_slice_in_dim(dmod_all, me * wloc, wloc, axis=1), name="ada_bwd")

    g_conv_w = lax.dynamic_slice_in_dim(_sv_get(flat, "conv_w", 4 * XBC).reshape(4, XBC),
                                        me * (XBC // N_DEV), XBC // N_DEV, axis=1)
    small = [("b_ada", b_ada, m_b_ada, v_b_ada), ("norm_mix_w", norm_mix_w, m_norm_mix_w, v_norm_mix_w),
             ("conv_b", conv_b, m_conv_b, v_conv_b), ("dt_bias", dt_bias, m_dt_bias, v_dt_bias),
             ("a_log", a_log, m_a_log, v_a_log), ("d_skip", d_skip, m_d_skip, v_d_skip),
             ("ssd_norm_w", ssd_norm_w, m_ssd_norm_w, v_ssd_norm_w),
             ("pool_scale", pool_scale, m_pool_scale, v_pool_scale),
             ("norm_mlp_w", norm_mlp_w, m_norm_mlp_w, v_norm_mlp_w),
             ("norm_final_w", norm_final_w[None], m_norm_final_w[None], v_norm_final_w[None]),
             ("conv_w", conv_w[0], m_conv_w[0], v_conv_w[0])]
    small_out = _adamw_small(sv_sum.reshape(1, SV_ROWS * 128), g_conv_w, small, name="adamw_small")
    small_out["norm_final_w"] = tuple(a[0] for a in small_out["norm_final_w"])
    small_out["conv_w"] = tuple(a[None] for a in small_out["conv_w"])

    def gpart(n, rows_):
        return gsum_mix[MIX_OFF[n]:MIX_OFF[n] + rows_]

    def lin(a):
        return a[0].T.reshape(IN_ROWS * 8, 128)

    g_lin = lax.dynamic_slice_in_dim(gsum_in, in_shift, IN_ROWS, axis=0).reshape(IN_ROWS * 8, 128)
    dlt, mn, vn = _adamw(lin(w_in), g_lin, lin(m_w_in), lin(v_w_in), name="adamw_w_in", tr=IN_ROWS * 2)
    big_in = tuple(a.reshape(IN_ROWS, D).T[None] for a in (g_lin, dlt, mn, vn))

    big = {
        "w_ada": (w_ada, m_w_ada, v_w_ada, g_w_ada, (D, wloc)),
        "w_branch_ssd": (w_branch_ssd, m_w_branch_ssd, v_w_branch_ssd, gpart("bssd", 256), (256, D)),
        "pool_w": (pool_w, m_pool_w, v_pool_w, gpart("pool", 32).reshape(128, PGW), (128, PGW)),
        "w_branch_pool": (w_branch_pool, m_w_branch_pool, v_w_branch_pool, gpart("bpool", 128), (128, D)),
        "w_out": (w_out, m_w_out, v_w_out, gpart("out", 128), (128, D)),
        "w_up": (w_up, m_w_up, v_w_up, gsum_mlp[:512].T, (D, 512)),
        "w_down": (w_down, m_w_down, v_w_down, gsum_mlp[512:], (512, D)),
    }
    big_out = {}
    for n, (w, mm_, vv, g, shp2) in big.items():
        dlt, mn, vn = _adamw(w.reshape(shp2), g, mm_.reshape(shp2), vv.reshape(shp2), name="adamw_" + n)
        big_out[n] = (g.reshape(w.shape), dlt.reshape(w.shape), mn.reshape(w.shape), vn.reshape(w.shape))

    order = ["w_ada", "b_ada", "norm_mix_w", "w_in", "conv_w", "conv_b", "dt_bias", "a_log", "d_skip",
             "ssd_norm_w", "w_branch_ssd", "pool_w", "pool_scale", "w_branch_pool", "w_out", "norm_mlp_w",
             "w_up", "w_down", "norm_final_w"]
    big_out["w_in"] = big_in
    res = {**small_out, **big_out}
    outs = [loss, grad_x.reshape(x.shape)]
    for k in range(4):
        outs += [res[n][k] for n in order]
    return tuple(outs)
```

```python
import functools

import numpy as np
import jax
import jax.numpy as jnp
from jax import lax
from jax.experimental import pallas as pl
from jax.experimental.pallas import tpu as pltpu

F32 = jnp.float32
BF16 = jnp.bfloat16
SLAB_DT = jnp.bfloat16
_MXU_DTYPE = jnp.bfloat16

N_DEV = 8
D = 1024
DI = 2048
NH = 32
HP = 64
NG = 4
NS = 128
Q = 128
XBC = DI + 2 * NG * NS
DFF = 4096
N_IN = 8224
EPS = 1e-5
POOL_W = 1024
PGW = 256

C_XBC, C_POOL, C_Z, C_GATE, C_DT = 0, 3072, 4096, 6144, 8192
DT_PAD = 256
NPROJ = C_DT + DT_PAD

IN_ROWS = N_IN // N_DEV
IN_ROWS_P = 1040
CONV_ROWS = 16
REST_PARTS = (("bssd", 256), ("pool", 32), ("bpool", 128), ("out", 128), ("up", 512), ("down", 512))
REST_OFF = {}
_o = 0
for _n, _r in REST_PARTS:
    REST_OFF[_n] = _o
    _o += _r
REST_ROWS = _o
MIX_PARTS = (("bssd", 256), ("pool", 32), ("bpool", 128), ("out", 128))
MIX_OFF = {}
_o = 0
for _n, _r in MIX_PARTS:
    MIX_OFF[_n] = _o
    _o += _r
MIX_ROWS = _o

SV_PARTS = (("b_ada", 6144), ("norm_mix_w", 1024), ("conv_b", 3072), ("dt_bias", 128), ("a_log", 128),
            ("d_skip", 128), ("ssd_norm_w", 2048), ("pool_scale", 1024), ("norm_mlp_w", 1024),
            ("norm_final_w", 1024), ("conv_w", 4 * XBC), ("loss", 128))
SV_OFF = {}
_o = 0
for _n, _r in SV_PARTS:
    SV_OFF[_n] = _o
    _o += _r
SV_ROWS = 224
assert _o <= SV_ROWS * 128

ADAM_LR, ADAM_B1, ADAM_B2, ADAM_EPS, ADAM_WD, ADAM_STEP = 0.001, 0.9, 0.999, 1e-08, 0.01, 10

VMEM_BIG = 56 * 1024 * 1024
NEG = -1e30

NN = ((1,), (0,))
NT = ((1,), (1,))
TN = ((0,), (0,))


def _dot(a, b, dims=NN):
    return lax.dot_general(a.astype(_MXU_DTYPE), b.astype(_MXU_DTYPE), (dims, ((), ())),
                           preferred_element_type=F32)


def _dot_hi(a, b, dims=NN):
    return lax.dot_general(a.astype(F32), b.astype(F32), (dims, ((), ())),
                           precision=lax.Precision.HIGHEST, preferred_element_type=F32)


def _pick(n, cands):
    for c in cands:
        if n % c == 0:
            return c
    return n


def _sigmoid(x):
    return 1.0 / (1.0 + jnp.exp(-x))


def _silu(x):
    return x * _sigmoid(x)


def _dsilu(x):
    s = _sigmoid(x)
    return s * (1.0 + x * (1.0 - s))


def _softplus(x):
    return jnp.maximum(x, 0.0) + jnp.log(1.0 + jnp.exp(-jnp.abs(x)))


def _params(sem, vmem=None):
    return pltpu.CompilerParams(dimension_semantics=sem, vmem_limit_bytes=vmem)


def _row_step():
    return pl.program_id(0)


def _mm(a, b, mode, *, name, outs, tm, tn, tk, extras=(), epilogue=None, aliases=None, prologue=None):
    if mode == "tn":
        K, M = a.shape
        N = b.shape[1]
        a_spec = pl.BlockSpec((tk, tm), lambda i, j, k: (k, i))
        b_spec = pl.BlockSpec((tk, tn), lambda i, j, k: (k, j))
        dims = TN
    else:
        M = a.shape[0]
        K = b.shape[0] if mode == "nn" else b.shape[1]
        if prologue is None:
            assert a.shape[1] == K
            a_spec = pl.BlockSpec((tm, tk), lambda i, j, k: (i, k))
        else:
            assert tk == K
            a_spec = pl.BlockSpec((tm, a.shape[1]), lambda i, j, k: (i, 0))
        if mode == "nn":
            N = b.shape[1]
            b_spec = pl.BlockSpec((tk, tn), lambda i, j, k: (k, j))
            dims = NN
        else:
            N = b.shape[0]
            b_spec = pl.BlockSpec((tn, tk), lambda i, j, k: (j, k))
            dims = NT
    assert M % tm == 0 and N % tn == 0 and K % tk == 0, (name, M, N, K, tm, tn, tk)
    nk = K // tk
    ne, no = len(extras), len(outs)
    if epilogue is None:
        def epilogue(acc, ex, out_refs):
            out_refs[0][...] = acc.astype(out_refs[0].dtype)

    def body(a_ref, b_ref, *rest):
        ex, out_refs = rest[:ne], rest[ne:ne + no]
        lhs = a_ref[...] if prologue is None else prologue(a_ref, ex, out_refs, pl.program_id(1))
        p = _dot(lhs, b_ref[...], dims)
        if nk == 1:
            epilogue(p, ex, out_refs)
        else:
            acc = rest[-1]
            k = pl.program_id(2)

            @pl.when(k == 0)
            def _():
                acc[...] = p

            @pl.when(jnp.logical_and(k > 0, k < nk - 1))
            def _():
                acc[...] += p

            @pl.when(k == nk - 1)
            def _():
                epilogue(acc[...] + p, ex, out_refs)

    out_specs, out_shape = [], []
    for o in outs:
        if isinstance(o, tuple):
            shape, dt, bs, im = o
            out_specs.append(pl.BlockSpec(bs, im))
            out_shape.append(jax.ShapeDtypeStruct(shape, dt))
        else:
            out_specs.append(pl.BlockSpec((tm, tn), lambda i, j, k: (i, j)))
            out_shape.append(jax.ShapeDtypeStruct((M, N), o))
    in_specs = [a_spec, b_spec]
    for _, bs, im in extras:
        in_specs.append(pl.BlockSpec(memory_space=pl.ANY) if bs is None else pl.BlockSpec(bs, im))
    res = pl.pallas_call(
        body, name=name,
        grid=(M // tm, N // tn, nk),
        in_specs=in_specs, out_specs=out_specs, out_shape=out_shape,
        scratch_shapes=[pltpu.VMEM((tm, tn), F32)] if nk > 1 else [],
        input_output_aliases={2 + e: o for e, o in (aliases or {}).items()},
        compiler_params=_params(("arbitrary", "arbitrary", "arbitrary"), VMEM_BIG),
    )(a, b, *[e[0] for e in extras])
    return res if no > 1 else res[0]


def _rows(tm, w=D, col=0):
    return (tm, w), lambda i, j, k, c=col: (i, c)


def _vecs(w=D, col=0):
    return (1, w), lambda i, j, k, c=col: (0, c)


def _sum_out(w=D):
    return ((1, w), F32, (1, w), lambda i, j, k: (0, 0))


def _mm_pool_tn(a, b, *, name, tk):
    L = a.shape[0]

    def body(a_ref, b_ref, o_ref):
        p = _dot(a_ref[...], b_ref[...], TN)

        @pl.when(pl.program_id(1) == 0)
        def _():
            o_ref[...] = p

        @pl.when(pl.program_id(1) > 0)
        def _():
            o_ref[...] += p

    blk = pl.BlockSpec((tk, PGW), lambda g, k: (k, g))
    return pl.pallas_call(body, name=name, grid=(4, L // tk), in_specs=[blk, blk],
                          out_specs=pl.BlockSpec((PGW, PGW), lambda g, k: (g, 0)),
                          out_shape=jax.ShapeDtypeStruct((POOL_W, PGW), F32),
                          compiler_params=_params(("parallel", "arbitrary")))(a, b)


def _acc_out(ref, val, i):
    @pl.when(i == 0)
    def _():
        ref[...] = val

    @pl.when(i > 0)
    def _():
        ref[...] += val


def _colsum(v):
    return jnp.sum(v, axis=0, keepdims=True)


def _ep_resid_norm(acc, ex, outs):
    x_ref, g_ref, nw_ref, sc_ref, sh_ref = ex
    mix_ref, x1_ref, h_ref = outs
    mix_ref[...] = acc.astype(mix_ref.dtype)
    xv = x_ref[...] + g_ref[...] * acc
    x1_ref[...] = xv
    r = lax.rsqrt(jnp.mean(xv * xv, axis=-1, keepdims=True) + EPS)
    h_ref[...] = (xv * r * nw_ref[...] * (1.0 + sc_ref[...]) + sh_ref[...]).astype(h_ref.dtype)


def _ep_final(acc, ex, outs):
    x1_ref, t_ref, g_ref, nw_ref = ex
    dx2_ref, dd_ref, loss_ref, dnw_ref, dg_ref = outs
    i = _row_step()
    x2 = x1_ref[...] + g_ref[...] * acc
    r = lax.rsqrt(jnp.mean(x2 * x2, axis=-1, keepdims=True) + EPS)
    xh = x2 * r
    e = xh * nw_ref[...] - t_ref[...]
    part = 0.5 * jnp.sum(jnp.mean(e * e, axis=-1, keepdims=True), axis=0, keepdims=True)
    dy = e * (1.0 / D)
    g = dy * nw_ref[...]
    dx2 = r * (g - xh * jnp.mean(g * xh, axis=-1, keepdims=True))
    dx2_ref[...] = dx2
    dd_ref[...] = (dx2 * g_ref[...]).astype(dd_ref.dtype)
    _acc_out(loss_ref, jnp.broadcast_to(part, (1, 128)), i)
    _acc_out(dnw_ref, _colsum(dy * xh), i)
    _acc_out(dg_ref, _colsum(dx2 * acc), i)


def _ep_norm_bwd(acc, ex, outs):
    x_ref, dr_ref, nw_ref, sc_ref = ex[:4]
    dx_ref, p_ref, q_ref = outs[:3]
    i = _row_step()
    xv = x_ref[...]
    r = lax.rsqrt(jnp.mean(xv * xv, axis=-1, keepdims=True) + EPS)
    xh = xv * r
    g = acc * (nw_ref[...] * (1.0 + sc_ref[...]))
    dx = dr_ref[...] + r * (g - xh * jnp.mean(g * xh, axis=-1, keepdims=True))
    dx_ref[...] = dx
    _acc_out(p_ref, _colsum(acc * xh), i)
    _acc_out(q_ref, _colsum(acc), i)
    if len(ex) > 4:
        m_ref, g_ref = ex[4:]
        dm_ref, dg_ref = outs[3:]
        dm_ref[...] = (dx * g_ref[...]).astype(dm_ref.dtype)
        _acc_out(dg_ref, _colsum(dx * m_ref[...].astype(F32)), i)


def _ep_merge_bwd(acc, ex, outs):
    a_ref, b_ref, gl_ref = ex
    da_ref, db_ref, dgl_ref = outs
    s = _sigmoid(gl_ref[...].astype(F32))
    s1, s2 = s[:, :D], s[:, D:]
    da_ref[...] = (acc * s1).astype(da_ref.dtype)
    db_ref[...] = (acc * s2).astype(db_ref.dtype)
    dgl_ref[:, :D] = (acc * a_ref[...].astype(F32) * s1 * (1.0 - s1)).astype(dgl_ref.dtype)
    dgl_ref[:, D:] = (acc * b_ref[...].astype(F32) * s2 * (1.0 - s2)).astype(dgl_ref.dtype)


GW = DI // NG


def _ep_gated_norm_bwd(acc, ex, outs):
    y_ref, z_ref, w_ref, _ = ex
    dy_ref, dz_ref, dw_ref = outs
    zv = z_ref[...].astype(F32)
    yv = y_ref[...].astype(F32)
    sg = _sigmoid(zv)
    sz = zv * sg
    yg = yv * sz
    dsz = sg * (1.0 + zv * (1.0 - sg))
    dws = []
    for k in range(NG):
        sl = slice(k * GW, (k + 1) * GW)
        seg = yg[:, sl]
        r = lax.rsqrt(jnp.mean(seg * seg, axis=-1, keepdims=True) + EPS)
        sh = seg * r
        dn = acc[:, sl]
        g = dn * w_ref[:, sl]
        dyg = r * (g - sh * jnp.mean(g * sh, axis=-1, keepdims=True))
        dy_ref[:, sl] = dyg * sz[:, sl]
        dz_ref[:, sl] = (dyg * yv[:, sl] * dsz[:, sl]).astype(dz_ref.dtype)
        dws.append(_colsum(dn * sh))
    _acc_out(dw_ref, jnp.concatenate(dws, axis=1), _row_step())


CONV_CB = 128
HALO = 16


def _time_chunk(L):
    return _pick(L, (256, 128))


def _with_halo(x_ref, i, r0, rc):
    p0 = pl.multiple_of(jnp.maximum(r0 - HALO, 0), HALO)
    prev = jnp.where(i > 0, x_ref[pl.ds(p0, HALO), :].astype(F32), 0.0)
    return jnp.concatenate([prev, x_ref[pl.ds(r0, rc), :].astype(F32)], axis=0)


def _conv_bwd(proj, dy, w, b, dproj, *, name):
    L = proj.shape[0]
    rc = _time_chunk(L)
    n = L // rc

    def body(x_ref, dy_ref, w_ref, b_ref, dp_in, dx_ref, dw_ref, db_ref, xpad, dpad):
        del dp_in
        wv = w_ref[...]
        bv = b_ref[...]
        dpad[rc:rc + HALO, :] = jnp.zeros((HALO, CONV_CB), F32)

        def step(k, carry):
            db, d0, d1, d2, d3 = carry
            i = n - 1 - k
            r0 = pl.multiple_of(i * rc, rc)
            p0 = pl.multiple_of(jnp.maximum(r0 - HALO, 0), HALO)
            xpad[0:HALO, :] = jnp.where(i > 0, x_ref[pl.ds(p0, HALO), :].astype(F32), 0.0)
            xpad[HALO:HALO + rc, :] = x_ref[pl.ds(r0, rc), :].astype(F32)
            xk = [xpad[HALO - j:HALO - j + rc, :] for j in range(4)]
            pre = bv
            for j in range(4):
                pre = pre + xk[j] * wv[3 - j:4 - j]
            dpre = dy_ref[pl.ds(r0, rc), :] * _dsilu(pre)
            dpad[0:rc, :] = dpre
            acc = dpre * wv[3:4]
            for j in (1, 2, 3):
                acc = acc + dpad[j:j + rc, :] * wv[3 - j:4 - j]
            dx_ref[pl.ds(r0, rc), :] = acc.astype(dx_ref.dtype)
            dpad[rc:rc + HALO, :] = dpre[:HALO]
            return (db + _colsum(dpre), d0 + _colsum(dpre * xk[3]), d1 + _colsum(dpre * xk[2]),
                    d2 + _colsum(dpre * xk[1]), d3 + _colsum(dpre * xk[0]))

        z = jnp.zeros((1, CONV_CB), F32)
        db, d0, d1, d2, d3 = lax.fori_loop(0, n, step, (z, z, z, z, z))
        db_ref[...] = db
        dw_ref[...] = jnp.concatenate([d0, d1, d2, d3], axis=0)

    nb = XBC // CONV_CB
    return pl.pallas_call(
        body, name=name, grid=(nb,),
        in_specs=[pl.BlockSpec((L, CONV_CB), lambda j: (0, j + C_XBC // CONV_CB)),
                  pl.BlockSpec((L, CONV_CB), lambda j: (0, j)),
                  pl.BlockSpec((4, CONV_CB), lambda j: (0, j)), pl.BlockSpec((1, CONV_CB), lambda j: (0, j)),
                  pl.BlockSpec(memory_space=pl.ANY)],
        out_specs=[pl.BlockSpec((L, CONV_CB), lambda j: (0, j + C_XBC // CONV_CB)),
                   pl.BlockSpec((4, CONV_CB), lambda j: (0, j)), pl.BlockSpec((1, CONV_CB), lambda j: (0, j))],
        out_shape=[jax.ShapeDtypeStruct((L, NPROJ), BF16), jax.ShapeDtypeStruct((4, XBC), F32),
                   jax.ShapeDtypeStruct((1, XBC), F32)],
        scratch_shapes=[pltpu.VMEM((rc + HALO, CONV_CB), F32), pltpu.VMEM((rc + HALO, CONV_CB), F32)],
        input_output_aliases={4: 0},
        compiler_params=_params(("parallel",), VMEM_BIG))(proj, dy, w, b, dproj)


def _pool_fwd(proj, *, name):
    L = proj.shape[0]
    rc = _time_chunk(L)
    n = L // rc

    def body(x_ref, o_ref, pad):
        g = pl.program_id(0)
        pad[0:HALO, :] = jnp.zeros((HALO, PGW), F32)

        def fill(i, c):
            r0 = pl.multiple_of(i * rc, rc)
            pad[pl.ds(r0 + HALO, rc), :] = x_ref[pl.ds(r0, rc), :].astype(F32)
            return c

        lax.fori_loop(0, n, fill, 0)
        rows = lax.broadcasted_iota(jnp.int32, (rc, PGW), 0)

        for gi in range(4):
            win = 2 << gi

            @pl.when(g == gi)
            def _(gi=gi, win=win):
                def step(i, c):
                    r0 = pl.multiple_of(i * rc, rc)
                    ext = pad[pl.ds(r0, rc + HALO), :]
                    s = ext
                    sh = 1
                    while sh < win:
                        s = s + pltpu.roll(s, sh, 0)
                        sh *= 2
                    cnt = jnp.minimum(rows + (r0 + 1), win).astype(F32)
                    o_ref[pl.ds(r0, rc), :] = (s[HALO:] / cnt - ext[HALO:]).astype(o_ref.dtype)
                    return c

                lax.fori_loop(0, n, step, 0)

    return pl.pallas_call(
        body, name=name, grid=(4,),
        in_specs=[pl.BlockSpec((L, PGW), lambda j: (0, j + C_POOL // PGW))],
        out_specs=pl.BlockSpec((L, PGW), lambda j: (0, j)),
        out_shape=jax.ShapeDtypeStruct((L, POOL_W), BF16),
        scratch_shapes=[pltpu.VMEM((L + HALO, PGW), F32)],
        compiler_params=_params(("parallel",), VMEM_BIG))(proj)


def _pool_bwd(dpooled, dproj, *, name):
    L = dpooled.shape[0]
    rc = _time_chunk(L)
    n = L // rc

    def body(d_ref, dp_in, o_ref, pad):
        del dp_in
        g = pl.program_id(0)
        pad[L:L + HALO, :] = jnp.zeros((HALO, PGW), F32)
        rows = lax.broadcasted_iota(jnp.int32, (rc, PGW), 0)

        for gi in range(4):
            win = 2 << gi

            @pl.when(g == gi)
            def _(gi=gi, win=win):
                def fill(i, c):
                    r0 = pl.multiple_of(i * rc, rc)
                    cnt = jnp.minimum(rows + (r0 + 1), win).astype(F32)
                    pad[pl.ds(r0, rc), :] = d_ref[pl.ds(r0, rc), :] / cnt
                    return c

                lax.fori_loop(0, n, fill, 0)

                def step(i, c):
                    r0 = pl.multiple_of(i * rc, rc)
                    s = pad[pl.ds(r0, rc + HALO), :]
                    sh = 1
                    while sh < win:
                        s = s + pltpu.roll(s, rc + HALO - sh, 0)
                        sh *= 2
                    o_ref[pl.ds(r0, rc), :] = (s[:rc] - d_ref[pl.ds(r0, rc), :]).astype(o_ref.dtype)
                    return c

                lax.fori_loop(0, n, step, 0)

    return pl.pallas_call(
        body, name=name, grid=(4,),
        in_specs=[pl.BlockSpec((L, PGW), lambda j: (0, j)), pl.BlockSpec(memory_space=pl.ANY)],
        out_specs=pl.BlockSpec((L, PGW), lambda j: (0, j + C_POOL // PGW)),
        out_shape=jax.ShapeDtypeStruct((L, NPROJ), BF16),
        scratch_shapes=[pltpu.VMEM((L + HALO, PGW), F32)],
        input_output_aliases={1: 0},
        compiler_params=_params(("parallel",), VMEM_BIG))(dpooled, dproj)


_SPLIT_DT = jnp.bfloat16


def _ssd_consts():
    tri = np.tril(np.ones((Q, Q), np.float32))
    exp = np.zeros((128, DI), np.float32)
    for h in range(NH):
        exp[h, h * HP:(h + 1) * HP] = 1.0
    exp2 = np.concatenate([exp, exp], axis=0)
    return (jnp.asarray(tri, dtype=_SPLIT_DT), jnp.asarray(tri.T.copy(), dtype=_SPLIT_DT),
            jnp.asarray(exp2, dtype=_SPLIT_DT))


def _split(v, n):
    parts, r = [], v
    for _ in range(n):
        p = r.astype(_SPLIT_DT)
        parts.append(p)
        r = r - p.astype(F32)
    return parts


def _bdot(a, b, dims):
    return lax.dot_general(a, b, (dims, ((), ())), preferred_element_type=F32)


def _tri_sum(t_ref, v):
    r = _bdot(t_ref[...], jnp.concatenate(_split(v, 3), axis=1), NN)
    return r[:, :128] + r[:, 128:256] + r[:, 256:]


def _expand(v, e2_ref):
    return _bdot(jnp.concatenate(_split(v, 2), axis=1), e2_ref[...], NN)


def _reduce_heads(vals, eg):
    parts = []
    for v in vals:
        parts += _split(v, 2)
    r = _bdot(jnp.concatenate(parts, axis=0), eg, NT)
    return [r[2 * i * Q:(2 * i + 1) * Q] + r[(2 * i + 1) * Q:(2 * i + 2) * Q] for i in range(len(vals))]


def _ssd_common(xbc_ref, dtw_ref, arow_ref, t_ref, e_ref):
    dt = dtw_ref[:, :128]
    sig = dtw_ref[:, 128:]
    acs = _tri_sum(t_ref, dt * arow_ref[...])
    acs_x = _expand(acs, e_ref)
    dt_x = _expand(dt, e_ref)
    xs = xbc_ref[:, 0:DI]
    return sig, dt, acs, acs.T, acs_x, dt_x, xs


CONV_SLAB = 512


def _ssd_fwd(raw, dtp, cw, cb, arow, dsk_x, *, name):
    L = raw.shape[0]
    nc = L // Q
    tri, _, expand = _ssd_consts()

    def body(raw_ref, halo_ref, cw_ref, cb_ref, dtw_ref, arow_ref, dsk_ref, t_ref, e_ref,
             y_ref, hs_ref, xbc_ref, h_scr, cpad):
        c = pl.program_id(0)

        @pl.when(c == 0)
        def _():
            h_scr[...] = jnp.zeros_like(h_scr)

        cpad[0:8, :] = jnp.where(c > 0, halo_ref[...], 0.0)
        cpad[8:8 + Q, :] = raw_ref[...]
        for lo in range(0, XBC, CONV_SLAB):
            sl = slice(lo, lo + CONV_SLAB)
            acc = cb_ref[:, sl]
            for j in range(4):
                acc = acc + cpad[8 - j:8 - j + Q, sl] * cw_ref[3 - j:4 - j, sl]
            xbc_ref[:, sl] = acc * _sigmoid(acc)

        _, dt, acs, acs_t, acs_x, dt_x, xs = _ssd_common(xbc_ref, dtw_ref, arow_ref, t_ref, e_ref)
        xdt = xs * dt_x
        eacs = jnp.exp(acs_x)
        acs_last = acs_x[Q - 1:Q, :]
        dec = jnp.exp(acs_last - acs_x)
        hs_ref[0] = h_scr[...].astype(hs_ref.dtype)
        causal = lax.broadcasted_iota(jnp.int32, (Q, Q), 0) >= lax.broadcasted_iota(jnp.int32, (Q, Q), 1)
        first = lax.broadcasted_iota(jnp.int32, (Q, 128), 1) < HP
        for g in range(NG):
            bg = xbc_ref[:, DI + g * NS:DI + (g + 1) * NS]
            cg = xbc_ref[:, DI + NG * NS + g * NS:DI + NG * NS + (g + 1) * NS]
            s = _dot(cg, bg, NT)
            sl = slice(g * GW, (g + 1) * GW)
            hg = h_scr[:, sl]
            yoff = _dot(cg, hg, NN) * eacs[:, sl]
            st = _dot(bg, xdt[:, sl] * dec[:, sl], TN)
            h_scr[:, sl] = hg * eacs[Q - 1:Q, sl] + st
            for j in range(4):
                lo = g * GW + j * 128
                xb = xdt[:, lo:lo + 128]
                yp = yoff[:, j * 128:(j + 1) * 128] + dsk_ref[:, lo:lo + 128] * xs[:, lo:lo + 128]
                for e in range(2):
                    h = g * 8 + j * 2 + e
                    lm = jnp.exp(jnp.where(causal, acs[:, h:h + 1] - acs_t[h:h + 1, :], NEG))
                    xm = jnp.where(first if e == 0 else jnp.logical_not(first), xb, 0.0)
                    yp = yp + _dot(s * lm, xm, NN)
                y_ref[:, lo:lo + 128] = yp.astype(y_ref.dtype)

    const = lambda c: (0, 0)
    return pl.pallas_call(
        body, name=name, grid=(nc,),
        in_specs=[pl.BlockSpec((Q, XBC), lambda c: (c, 0)),
                  pl.BlockSpec((8, XBC), lambda c: (jnp.maximum(c * (Q // 8) - 1, 0), 0)),
                  pl.BlockSpec((4, XBC), const), pl.BlockSpec((1, XBC), const),
                  pl.BlockSpec((Q, DT_PAD), lambda c: (c, 0)),
                  pl.BlockSpec((1, 128), const), pl.BlockSpec((1, DI), const),
                  pl.BlockSpec((Q, Q), const), pl.BlockSpec((256, DI), const)],
        out_specs=[pl.BlockSpec((Q, DI), lambda c: (c, 0)), pl.BlockSpec((1, NS, DI), lambda c: (c, 0, 0)),
                   pl.BlockSpec((Q, XBC), lambda c: (c, 0))],
        out_shape=[jax.ShapeDtypeStruct((L, DI), BF16), jax.ShapeDtypeStruct((nc, NS, DI), F32),
                   jax.ShapeDtypeStruct((L, XBC), F32)],
        scratch_shapes=[pltpu.VMEM((NS, DI), F32), pltpu.VMEM((8 + Q, XBC), F32)],
        compiler_params=_params(("arbitrary",), VMEM_BIG))(raw, raw, cw, cb, dtp, arow, dsk_x, tri, expand)


def _ssd_bwd(dy, xbc, dtp, hs, arow, dsk_x, dproj, *, name):
    L = xbc.shape[0]
    nc = L // Q
    tri, triu, expand = _ssd_consts()

    def body(dy_ref, xbc_ref, dtw_ref, hs_ref, arow_ref, dsk_ref, t_ref, u_ref, e_ref, dp_in,
             dxbc_ref, ddtw_ref, da_ref, ddx_ref, ddtb_ref, dh_scr):
        del dp_in
        i = pl.program_id(0)

        @pl.when(i == 0)
        def _():
            dh_scr[...] = jnp.zeros_like(dh_scr)

        sig, dt, acs, acs_t, acs_x, dt_x, xs = _ssd_common(xbc_ref, dtw_ref, arow_ref, t_ref, e_ref)
        dyv = dy_ref[...]
        xdt = xs * dt_x
        eacs = jnp.exp(acs_x)
        acs_last = acs_x[Q - 1:Q, :]
        dec = jnp.exp(acs_last - acs_x)
        gy = dyv * eacs
        causal = lax.broadcasted_iota(jnp.int32, (Q, Q), 0) >= lax.broadcasted_iota(jnp.int32, (Q, Q), 1)
        first = lax.broadcasted_iota(jnp.int32, (Q, 128), 1) < HP
        lane_h = lax.broadcasted_iota(jnp.int32, (Q, 128), 1)
        sub_h = lax.broadcasted_iota(jnp.int32, (128, Q), 0)
        last_row = lax.broadcasted_iota(jnp.int32, (Q, GW), 0) == Q - 1
        dacs = jnp.zeros((Q, 128), F32)
        dacs_t = jnp.zeros((128, Q), F32)
        ddt = jnp.zeros((Q, 128), F32)
        for g in range(NG):
            bg = xbc_ref[:, DI + g * NS:DI + (g + 1) * NS]
            cg = xbc_ref[:, DI + NG * NS + g * NS:DI + NG * NS + (g + 1) * NS]
            s = _dot(cg, bg, NT)
            sl = slice(g * GW, (g + 1) * GW)
            hg = hs_ref[0, :, sl].astype(F32)
            dhn = dh_scr[:, sl]
            eal = eacs[Q - 1:Q, sl]
            gg = gy[:, sl]
            dax = gg * _dot(cg, hg, NN)
            dcg = _dot(gg, hg, NT)
            dh_scr[:, sl] = _dot(cg, gg, TN) + dhn * eal
            dal = eal * _colsum(dhn * hg)
            xdd = xdt[:, sl] * dec[:, sl]
            dbg = _dot(xdd, dhn, NT)
            wv = _dot(bg, dhn, NN)
            dd = wv * xdd
            dax = dax - dd
            dal = dal + _colsum(dd)
            dax = dax + jnp.where(last_row, dal, 0.0)
            dxdt_g = wv * dec[:, sl]
            ds = jnp.zeros((Q, Q), F32)
            dxdt_blocks = []
            for j in range(4):
                lo = g * GW + j * 128
                xb = xdt[:, lo:lo + 128]
                dyb = dyv[:, lo:lo + 128]
                dxb = dxdt_g[:, j * 128:(j + 1) * 128]
                for e in range(2):
                    h = g * 8 + j * 2 + e
                    lm = jnp.exp(jnp.where(causal, acs[:, h:h + 1] - acs_t[h:h + 1, :], NEG))
                    m = s * lm
                    dym = jnp.where(first if e == 0 else jnp.logical_not(first), dyb, 0.0)
                    dm = _dot(dym, xb, NT)
                    r = dm * m
                    dacs = dacs + jnp.where(lane_h == h, jnp.sum(r, axis=1, keepdims=True), 0.0)
                    dacs_t = dacs_t + jnp.where(sub_h == h, _colsum(r), 0.0)
                    ds = ds + dm * lm
                    dxb = dxb + _dot(m, dym, TN)
                dxdt_blocks.append(dxb)
            dxdt = jnp.concatenate(dxdt_blocks, axis=1)
            dcg = dcg + _dot(ds, bg, NN)
            dbg = dbg + _dot(ds, cg, TN)
            dxbc_ref[:, DI + g * NS:DI + (g + 1) * NS] = dbg
            dxbc_ref[:, DI + NG * NS + g * NS:DI + NG * NS + (g + 1) * NS] = dcg
            dxbc_ref[:, sl] = dsk_ref[:, sl] * dyv[:, sl] + dxdt * dt_x[:, sl]
            ddt_g, dacs_g = _reduce_heads([dxdt * xs[:, sl], dax], e_ref[0:128, sl])
            ddt = ddt + ddt_g
            dacs = dacs + dacs_g
        dacs = dacs - dacs_t.T
        ddta = _tri_sum(u_ref, dacs)
        ddt = ddt + ddta * arow_ref[...]
        ddtw = jnp.where(lane_h < NH, ddt * sig, 0.0)
        ddtw_ref[...] = jnp.concatenate([ddtw, jnp.zeros((Q, DT_PAD - 128), F32)], axis=1).astype(ddtw_ref.dtype)
        _acc_out(da_ref, _colsum(ddta * dt), i)
        _acc_out(ddx_ref, _colsum(dyv * xs), i)
        _acc_out(ddtb_ref, _colsum(ddtw), i)

    rev = lambda c: (nc - 1 - c, 0)
    const = lambda c: (0, 0)
    return pl.pallas_call(
        body, name=name, grid=(nc,),
        in_specs=[pl.BlockSpec((Q, DI), rev), pl.BlockSpec((Q, XBC), rev),
                  pl.BlockSpec((Q, DT_PAD), rev),
                  pl.BlockSpec((1, NS, DI), lambda c: (nc - 1 - c, 0, 0)),
                  pl.BlockSpec((1, 128), const), pl.BlockSpec((1, DI), const),
                  pl.BlockSpec((Q, Q), const), pl.BlockSpec((Q, Q), const), pl.BlockSpec((256, DI), const),
                  pl.BlockSpec(memory_space=pl.ANY)],
        out_specs=[pl.BlockSpec((Q, XBC), rev),
                   pl.BlockSpec((Q, DT_PAD), lambda c: (nc - 1 - c, C_DT // DT_PAD)),
                   pl.BlockSpec((1, 128), const), pl.BlockSpec((1, DI), const), pl.BlockSpec((1, 128), const)],
        out_shape=[jax.ShapeDtypeStruct((L, XBC), F32), jax.ShapeDtypeStruct((L, NPROJ), BF16),
                   jax.ShapeDtypeStruct((1, 128), F32), jax.ShapeDtypeStruct((1, DI), F32),
                   jax.ShapeDtypeStruct((1, 128), F32)],
        scratch_shapes=[pltpu.VMEM((NS, DI), F32)],
        input_output_aliases={9: 1},
        compiler_params=_params(("arbitrary",), VMEM_BIG))(dy, xbc, dtp, hs, arow, dsk_x, tri, triu,
                                                          expand, dproj)


def _adam_update(wv, gv, mv, vv):
    c1 = 1.0 - ADAM_B1 ** ADAM_STEP
    c2 = 1.0 - ADAM_B2 ** ADAM_STEP
    mn = ADAM_B1 * mv + (1.0 - ADAM_B1) * gv
    vn = ADAM_B2 * vv + (1.0 - ADAM_B2) * (gv * gv)
    return -ADAM_LR * ((mn / c1) / (jnp.sqrt(vn / c2) + ADAM_EPS) + ADAM_WD * wv), mn, vn


def _adamw(w, g, m, v, *, name, tr=None):
    R = w.shape[0]
    rest = tuple(w.shape[1:])
    if tr is None:
        tr = _pick(R, (256, 128, 64, 32, 16, 8))
    assert R % tr == 0

    def body(w_ref, g_ref, m_ref, v_ref, d_ref, mo_ref, vo_ref):
        d_ref[...], mo_ref[...], vo_ref[...] = _adam_update(w_ref[...], g_ref[...], m_ref[...], v_ref[...])

    zeros = (0,) * len(rest)
    spec = pl.BlockSpec((tr,) + rest, lambda i: (i,) + zeros)
    return pl.pallas_call(body, name=name, grid=(R // tr,), in_specs=[spec] * 4, out_specs=[spec] * 3,
                          out_shape=[jax.ShapeDtypeStruct(w.shape, F32)] * 3,
                          compiler_params=_params(("parallel",)))(w, g, m, v)


def _adamw_small(svrow, g_conv, params, *, name):
    n = len(params)

    def body(*refs):
        sv_ref, gc_ref = refs[0], refs[1]
        ins, outs = refs[2:2 + 3 * n], refs[2 + 3 * n:]
        for p, (key, w, _, _) in enumerate(params):
            w_ref, m_ref, v_ref = ins[3 * p:3 * p + 3]
            g_ref, d_ref, mo_ref, vo_ref = outs[4 * p:4 * p + 4]
            gv = gc_ref[...] if key == "conv_w" else sv_ref[:, SV_OFF[key]:SV_OFF[key] + w.shape[1]]
            g_ref[...] = gv
            d_ref[...], mo_ref[...], vo_ref[...] = _adam_update(w_ref[...], gv, m_ref[...], v_ref[...])

    vm = pl.BlockSpec(memory_space=pltpu.VMEM)
    args = [svrow, g_conv]
    shapes = []
    for _, w, m, v in params:
        args += [w, m, v]
        shapes += [jax.ShapeDtypeStruct(w.shape, F32)] * 4
    res = pl.pallas_call(body, name=name, in_specs=[vm] * len(args), out_specs=[vm] * len(shapes),
                         out_shape=shapes)(*args)
    return {key: tuple(res[4 * p:4 * p + 4]) for p, (key, _, _, _) in enumerate(params)}


def _slab_sum(recv, *, tile, name):
    rows = recv.shape[1]
    assert rows % tile == 0 and tile % 16 == 0

    def body(r_ref, o_ref):
        acc = r_ref[0].astype(F32)
        for j in range(1, N_DEV):
            acc = acc + r_ref[j].astype(F32)
        o_ref[...] = acc

    return pl.pallas_call(body, name=name, grid=(rows // tile,),
                          in_specs=[pl.BlockSpec((N_DEV, tile, D), lambda i: (0, i, 0))],
                          out_specs=pl.BlockSpec((tile, D), lambda i: (i, 0)),
                          out_shape=jax.ShapeDtypeStruct((rows, D), F32),
                          compiler_params=_params(("parallel",)))(recv)


MESH = pl.DeviceIdType.MESH


def _coords():
    return lax.axis_index("x"), lax.axis_index("y"), lax.axis_index("c")


def _peer(k):
    x, y, c = _coords()
    px = 1 - x if k & 4 else x
    py = 1 - y if k & 2 else y
    pc = 1 - c if k & 1 else c
    return (px, py, pc), 4 * px + 2 * py + pc


def _rcopy(src, dst, ssem, rsem, dev):
    return pltpu.make_async_remote_copy(src_ref=src, dst_ref=dst, send_sem=ssem, recv_sem=rsem,
                                        device_id=dev, device_id_type=MESH)


def _exchange_all(src_of, dst_slot, send_sems, recv_sems):
    x, y, c = _coords()
    me = 4 * x + 2 * y + c
    sent = []
    for k in range(1, N_DEV):
        dev, pidx = _peer(k)
        cp = _rcopy(src_of(pidx), dst_slot(me), send_sems.at[k - 1], recv_sems.at[k - 1], dev)
        cp.start()
        sent.append(cp)
    for k in range(1, N_DEV):
        dev, pidx = _peer(k)
        _rcopy(src_of(pidx), dst_slot(pidx), send_sems.at[k - 1], recv_sems.at[k - 1], dev).wait_recv()
    for cp in sent:
        cp.wait_send()


def _rows_of_slots(buf, nslots):
    rows = lax.broadcasted_iota(jnp.int32, (8, buf.shape[-1]), 0)
    out = jnp.zeros((8, buf.shape[-1]), F32)
    for j in range(nslots):
        out = out + jnp.where(rows == j, buf[j], 0.0)
    return out


def _ada_fwd(c, w_ada, b_r, *, name):
    wloc = w_ada.shape[1]

    def body(c_ref, w_ref, b_ref, mod_ref, call_ref, csrc, cbuf, psrc, pbuf, s1, r1, s2, r2):
        x, y, cc = _coords()
        me = 4 * x + 2 * y + cc
        csrc[...] = jnp.broadcast_to(c_ref[...], (8, D))
        cbuf[me] = csrc[...]
        _exchange_all(lambda p: csrc, lambda s: cbuf.at[s], s1, r1)
        call = _rows_of_slots(cbuf, N_DEV)
        call_ref[...] = call
        prod = _dot_hi(_silu(call), w_ref[...])
        for b in range(N_DEV):
            psrc[b] = jnp.broadcast_to(prod[b:b + 1, :], (8, wloc))
        pbuf[me] = psrc[me]
        _exchange_all(lambda p: psrc.at[p], lambda s: pbuf.at[s], s2, r2)
        mod_ref[...] = _rows_of_slots(pbuf, N_DEV) + b_ref[...]

    vm = pl.BlockSpec(memory_space=pltpu.VMEM)
    return pl.pallas_call(
        body, name=name, in_specs=[vm, vm, vm], out_specs=[vm, vm],
        out_shape=[jax.ShapeDtypeStruct((N_DEV, wloc), F32), jax.ShapeDtypeStruct((N_DEV, D), F32)],
        scratch_shapes=[pltpu.VMEM((8, D), F32), pltpu.VMEM((N_DEV, 8, D), F32),
                        pltpu.VMEM((N_DEV, 8, wloc), F32), pltpu.VMEM((N_DEV, 8, wloc), F32),
                        pltpu.SemaphoreType.DMA((N_DEV - 1,)), pltpu.SemaphoreType.DMA((N_DEV - 1,)),
                        pltpu.SemaphoreType.DMA((N_DEV - 1,)), pltpu.SemaphoreType.DMA((N_DEV - 1,))],
        compiler_params=pltpu.CompilerParams(vmem_limit_bytes=VMEM_BIG))(c, w_ada, b_r)


def _gather_slabs(slab, *, name):
    def body(x_ref, out_ref, send_sems, recv_sems, local_sem):
        x, y, c = _coords()
        me, sibling = (x, y, c), (x, y, 1 - c)
        chips = [(1 - x, y), (x, 1 - y), (1 - x, 1 - y)]

        def slot(px, py, pc):
            return out_ref.at[4 * px + 2 * py + pc]

        def copy(k, block, to, src=None):
            return _rcopy(slot(*block) if src is None else src, slot(*block), send_sems.at[k], recv_sems.at[k], to)

        mine = pltpu.make_async_copy(x_ref, slot(*me), local_sem)
        mine.start()
        first = [copy(0, me, sibling, src=x_ref)]
        first += [copy(1 + j, me, (*chip, c), src=x_ref) for j, chip in enumerate(chips)]
        for cp in first:
            cp.start()
        passed = [copy(4 + j, (*chip, c), sibling) for j, chip in enumerate(chips)]
        for j, chip in enumerate(chips):
            copy(1 + j, (*chip, c), me).wait_recv()
            passed[j].start()
        copy(0, sibling, me).wait_recv()
        for j, chip in enumerate(chips):
            copy(4 + j, (*chip, 1 - c), me).wait_recv()
        for cp in first + passed:
            cp.wait_send()
        mine.wait()

    anyspec = pl.BlockSpec(memory_space=pl.ANY)
    return pl.pallas_call(
        body, name=name, in_specs=[anyspec], out_specs=anyspec,
        out_shape=jax.ShapeDtypeStruct((N_DEV,) + slab.shape, slab.dtype),
        scratch_shapes=[pltpu.SemaphoreType.DMA((7,)), pltpu.SemaphoreType.DMA((7,)), pltpu.SemaphoreType.DMA],
    )(slab)


_HBM =pl.BlockSpec(memory_space=pltpu.HBM)
_SEM = pl.BlockSpec(memory_space=pltpu.SEMAPHORE)
_EFFECT = pltpu.SideEffectType.DATAFLOW_SIDE_EFFECTING


def _xchg_src(src_ref, pidx, per_peer):
    return src_ref.at[pidx] if per_peer else src_ref


def _xchg_start(src, *, per_peer, name):
    rows = src.shape[-2]
    land_shape = (N_DEV, rows, D)

    def body(src_ref, land_ref, send_sems, recv_sems, src_thru, land_thru, token):
        del src_thru, land_thru
        x, y, c = _coords()
        me = 4 * x + 2 * y + c
        for k in range(1, N_DEV):
            dev, pidx = _peer(k)
            _rcopy(_xchg_src(src_ref, pidx, per_peer), land_ref.at[me], send_sems.at[k - 1],
                   recv_sems.at[k - 1], dev).start()
        token[...] = jnp.zeros_like(token)

    return pl.pallas_call(
        body, name=name,
        out_shape=(pltpu.SemaphoreType.DMA((N_DEV - 1,)), pltpu.SemaphoreType.DMA((N_DEV - 1,)),
                   pltpu.HBM(src.shape, src.dtype), pltpu.HBM(land_shape, src.dtype),
                   jax.ShapeDtypeStruct((8, 128), F32)),
        in_specs=(_HBM, _HBM),
        out_specs=(_SEM, _SEM, _HBM, _HBM, pl.BlockSpec(memory_space=pltpu.VMEM)),
        input_output_aliases={0: 2, 1: 3},
        compiler_params=pltpu.CompilerParams(has_side_effects=_EFFECT),
    )(pltpu.with_memory_space_constraint(src, pltpu.HBM),
      pltpu.with_memory_space_constraint(lax.empty(land_shape, src.dtype), pltpu.HBM))


def _xchg_wait(started, after, *, per_peer, name):
    send_sems, recv_sems, src_thru, land_thru, _ = started

    def body(src_ref, land_ref, send_sems, recv_sems, after_ref, src_dead, got_ref):
        del after_ref, src_dead, got_ref
        for k in range(1, N_DEV):
            dev, pidx = _peer(k)
            cp = _rcopy(_xchg_src(src_ref, pidx, per_peer), land_ref.at[pidx], send_sems.at[k - 1],
                        recv_sems.at[k - 1], dev)
            cp.wait_send()
            cp.wait_recv()

    return pl.pallas_call(
        body, name=name,
        out_shape=(pltpu.HBM(src_thru.shape, src_thru.dtype), pltpu.HBM(land_thru.shape, land_thru.dtype)),
        in_specs=(_HBM, _HBM, _SEM, _SEM, pl.BlockSpec(memory_space=pl.ANY)),
        out_specs=(_HBM, _HBM),
        input_output_aliases={0: 0, 1: 1},
        compiler_params=pltpu.CompilerParams(has_side_effects=_EFFECT),
    )(src_thru, land_thru, send_sems, recv_sems, after)


def _dep(token):
    return (token, (8, 128), lambda i, j, k: (0, 0))


def _ada_bwd(call, dmod_loc, *, name):
    wloc = dmod_loc.shape[1]

    def body(c_ref, d_ref, o_ref):
        o_ref[...] = _dot_hi(_silu(c_ref[...]), d_ref[...], TN)

    vm = pl.BlockSpec(memory_space=pltpu.VMEM)
    return pl.pallas_call(body, name=name, in_specs=[vm, vm], out_specs=vm,
                          out_shape=jax.ShapeDtypeStruct((D, wloc), F32),
                          compiler_params=pltpu.CompilerParams(vmem_limit_bytes=VMEM_BIG))(call, dmod_loc)


def _pad_rows(a, rows):
    return jnp.pad(a, ((0, rows - a.shape[0]), (0, 0)))


IN_SHIFT = tuple((IN_ROWS * j) % 16 for j in range(N_DEV))
IN_BASE = tuple(IN_ROWS * j - IN_SHIFT[j] for j in range(N_DEV))
IN_SEGMENTS = ((2048, XBC, C_XBC), (5152, 1024, C_POOL), (0, 2048, C_Z), (6176, 2048, C_GATE), (5120, 32, C_DT))


def _global_pieces(gs):
    pieces = []
    for j in range(N_DEV):
        lo, hi = 0, IN_ROWS_P
        if j > 0 and IN_BASE[j - 1] + IN_ROWS_P > IN_BASE[j]:
            pieces.append((IN_BASE[j], 16, gs[j - 1, IN_ROWS_P - 16:IN_ROWS_P] + gs[j, 0:16]))
            lo = 16
        if j + 1 < N_DEV and IN_BASE[j] + IN_ROWS_P > IN_BASE[j + 1]:
            hi = IN_ROWS_P - 16
        pieces.append((IN_BASE[j] + lo, hi - lo, gs[j, lo:hi]))
    return pieces


def _reorder_in_rows(gs):
    pieces = _global_pieces(gs)
    parts = []
    for lo, n, _ in IN_SEGMENTS:
        for p0, pn, arr in pieces:
            a, b = max(lo, p0), min(lo + n, p0 + pn)
            if a < b:
                parts.append(arr[a - p0:b - p0])
    parts.append(jnp.zeros((DT_PAD - 32, D), gs.dtype))
    return jnp.concatenate(parts, axis=0)


def _restore_in_shards(d):
    slabs = []
    for j in range(N_DEV):
        parts = []
        r, end = IN_BASE[j], IN_BASE[j] + IN_ROWS_P
        while r < end:
            lo, n, new = next(s for s in IN_SEGMENTS if s[0] <= r < s[0] + s[1])
            e = min(end, lo + n)
            parts.append(d[new + r - lo:new + e - lo])
            r = e
        slabs.append(jnp.concatenate(parts, axis=0))
    return jnp.stack(slabs, axis=0)


def _pack_sv(parts):
    flat = []
    for n, size in SV_PARTS:
        v = parts[n].reshape(-1).astype(F32)
        flat.append(jnp.pad(v, (0, size - v.shape[0])))
    v = jnp.concatenate(flat)
    return jnp.pad(v, (0, SV_ROWS * 128 - v.shape[0])).reshape(SV_ROWS, 128)


def _sv_get(flat, n, size):
    return flat[SV_OFF[n]:SV_OFF[n] + size]


def kernel(x, c, w_ada, b_ada, norm_mix_w, w_in, conv_w, conv_b, dt_bias, a_log, d_skip, ssd_norm_w, w_branch_ssd, pool_w, pool_scale, w_branch_pool, w_out, norm_mlp_w, w_up, w_down, norm_final_w, loss_target, m_w_ada, m_b_ada, m_norm_mix_w, m_w_in, m_conv_w, m_conv_b, m_dt_bias, m_a_log, m_d_skip, m_ssd_norm_w, m_w_branch_ssd, m_pool_w, m_pool_scale, m_w_branch_pool, m_w_out, m_norm_mlp_w, m_w_up, m_w_down, m_norm_final_w, v_w_ada, v_b_ada, v_norm_mix_w, v_w_in, v_conv_w, v_conv_b, v_dt_bias, v_a_log, v_d_skip, v_ssd_norm_w, v_w_branch_ssd, v_pool_w, v_pool_scale, v_w_branch_pool, v_w_out, v_norm_mlp_w, v_w_up, v_w_down, v_norm_final_w):
    xs_ = x[0]
    tgt = loss_target[0]
    L = xs_.shape[0]
    me = 4 * lax.axis_index("x") + 2 * lax.axis_index("y") + lax.axis_index("c")
    wloc = w_ada.shape[2]

    mod_p, c_all = _ada_fwd(c, w_ada[0], b_ada.reshape(N_DEV, wloc), name="ada_fwd")
    mod = mod_p.reshape(6, D)
    shift_m, scale_m, gate_m, shift_f, scale_f, gate_f = [mod[i:i + 1] for i in range(6)]

    conv_bits = lax.bitcast_convert_type(conv_w[0], SLAB_DT).reshape(3, D)
    in_shift = (IN_ROWS * me) % 16
    slab_in = lax.dynamic_update_slice(jnp.zeros((IN_ROWS_P, D), SLAB_DT), w_in[0].T.astype(SLAB_DT),
                                       (in_shift, 0))
    slab_in = jnp.concatenate([slab_in, _pad_rows(conv_bits, CONV_ROWS)], axis=0)
    slab_rest = jnp.concatenate([
        w_branch_ssd[0].astype(SLAB_DT),
        pool_w[0].reshape(32, D).astype(SLAB_DT),
        w_branch_pool[0].astype(SLAB_DT),
        w_out[0].astype(SLAB_DT),
        w_up[0].T.astype(SLAB_DT),
        w_down[0].astype(SLAB_DT)], axis=0)
    slab_in, mod_p = lax.optimization_barrier((slab_in, mod_p))
    gs_in = _gather_slabs(slab_in, name="gather_w_in")
    slab_rest, gs_in = lax.optimization_barrier((slab_rest, gs_in))
    rest_started = _xchg_start(slab_rest, per_peer=False, name="gather_rest_start")
    gather_token = rest_started[4]

    w_in_t = _reorder_in_rows(gs_in)
    conv_full = lax.bitcast_convert_type(
        gs_in[:, IN_ROWS_P:IN_ROWS_P + 3].reshape(N_DEV, 4, XBC // N_DEV, 2), F32)
    conv_full = conv_full.transpose(1, 0, 2).reshape(4, XBC)

    dtb = jnp.pad(dt_bias, ((0, 0), (0, 128 - NH)))
    arow = jnp.pad(-jnp.exp(a_log), ((0, 0), (0, 128 - NH)))
    dsk_x = jnp.repeat(d_skip, HP, axis=1)

    tm = _pick(L, (1024, 512, 256, 128))
    tm2 = _pick(L, (2048, 1024, 512, 256, 128))
    tkl = _pick(L, (4096, 2048, 1024, 512, 256, 128))
    tkl2 = _pick(L, (2048, 1024, 512, 256, 128))

    tmh = _pick(L, (512, 256, 128))
    zcol = C_Z // DI
    gcol = C_GATE // (2 * D)

    def whole_rows(w):
        return lambda t: ((L, w), BF16, (t, w), lambda i, j, k: (i, 0))

    def norm1_pro(x_ref, ex, outs, j):
        @pl.when(j == 0)
        def _():
            xv = x_ref[...]
            r = lax.rsqrt(jnp.mean(xv * xv, axis=-1, keepdims=True) + EPS)
            outs[1][...] = (xv * r * ex[0][...] * (1.0 + ex[1][...]) + ex[2][...]).astype(outs[1].dtype)

        return outs[1][...]

    def proj_ep(acc, ex, outs):
        outs[0][...] = acc

        @pl.when(pl.program_id(1) == NPROJ // 768 - 1)
        def _():
            pre = acc[:, 768 - DT_PAD:768 - DT_PAD + 128] + ex[3][...]
            outs[2][...] = jnp.concatenate([_softplus(pre), _sigmoid(pre)], axis=1)

    proj, h1, dtp = _mm(
        xs_, w_in_t, "nt", name="in_proj", tm=tm2, tn=768, tk=D,
        extras=[(norm_mix_w, *_vecs()), (scale_m, *_vecs()), (shift_m, *_vecs()), (dtb, *_vecs(128)),
                _dep(gather_token)],
        outs=[F32, whole_rows(D)(tm2), ((L, DT_PAD), F32, (tm2, DT_PAD), lambda i, j, k: (i, 0))],
        prologue=norm1_pro, epilogue=proj_ep)
    xbc_raw = proj
    y_ssm, hs, xbc = _ssd_fwd(xbc_raw, dtp, conv_full, conv_b, arow, dsk_x, name="ssd_fwd")

    slab_rest, gs = _xchg_wait(rest_started, y_ssm, per_peer=False, name="gather_rest_wait")
    gs = lax.dynamic_update_slice(gs, slab_rest[None], (me, 0, 0))

    def part(n, rows):
        return gs[:, REST_OFF[n]:REST_OFF[n] + rows]

    w_bssd = part("bssd", 256).reshape(DI, D)
    w_pool = part("pool", 32).reshape(N_DEV, 4, 32, PGW).transpose(1, 0, 2, 3).reshape(POOL_W, PGW)
    w_bpool = part("bpool", 128).reshape(POOL_W, D)
    w_o = part("out", 128).reshape(D, D)
    w_up_t = part("up", 512).reshape(DFF, D)
    w_dn = part("down", 512).reshape(DFF, D)

    def gnorm_pro(y_ref, ex, outs, j):
        z_ref, w_ref = ex
        yg = y_ref[...].astype(F32) * _silu(z_ref[...].astype(F32))
        segs = []
        for k in range(NG):
            sl = slice(k * GW, (k + 1) * GW)
            seg = yg[:, sl]
            r = lax.rsqrt(jnp.mean(seg * seg, axis=-1, keepdims=True) + EPS)
            segs.append((seg * r * w_ref[:, sl]).astype(BF16))
        yn_v = jnp.concatenate(segs, axis=1)
        outs[1][...] = yn_v
        return yn_v

    y_ssd, yn = _mm(y_ssm, w_bssd, "nn", name="branch_ssd", tm=tmh, tn=D, tk=DI,
                    extras=[(proj, *_rows(tmh, DI, zcol)), (ssd_norm_w, *_vecs(DI))],
                    outs=[BF16, whole_rows(DI)(tmh)], prologue=gnorm_pro)
    pooled = _pool_fwd(proj, name="pool_fwd")
    wp_spec = ((POOL_W, PGW), lambda i, j, k: (0, 0))

    def pool_pro(a_ref, ex, outs, j):
        wp_ref, s_ref = ex
        segs = []
        for g in range(4):
            sl = slice(g * PGW, (g + 1) * PGW)
            p = _dot(a_ref[:, sl], wp_ref[sl, :], NN)
            outs[1][:, sl] = p.astype(BF16)
            segs.append((p * s_ref[:, sl]).astype(BF16))
        yp1_v = jnp.concatenate(segs, axis=1)
        outs[2][...] = yp1_v
        return yp1_v

    y_pool, yp0, yp1 = _mm(pooled, w_bpool, "nn", name="branch_pool", tm=tm, tn=D, tk=D,
                           extras=[(w_pool, *wp_spec), (pool_scale, *_vecs())],
                           outs=[BF16, whole_rows(D)(tm), whole_rows(D)(tm)], prologue=pool_pro)

    def merge_pro(a_ref, ex, outs, j):
        s = _sigmoid(ex[1][...].astype(F32))
        mv = (s[:, :D] * a_ref[...].astype(F32) + s[:, D:] * ex[0][...].astype(F32)).astype(BF16)
        outs[3][...] = mv
        return mv

    mix, x1, h2, m = _mm(y_ssd, w_o, "nn", name="out_proj", tm=tmh, tn=D, tk=D,
                         extras=[(y_pool, *_rows(tmh)), (proj, *_rows(tmh, 2 * D, gcol)),
                                 (xs_, *_rows(tmh)), (gate_m, *_vecs()), (norm_mlp_w, *_vecs()),
                                 (scale_f, *_vecs()), (shift_f, *_vecs())],
                         outs=[BF16, F32, BF16, whole_rows(D)(tmh)], prologue=merge_pro,
                         epilogue=lambda acc, ex, outs: _ep_resid_norm(acc, ex[2:], outs[:3]))

    def relu2(acc, ex, outs):
        r = jnp.maximum(acc, 0.0)
        outs[0][...] = acc.astype(BF16)
        outs[1][...] = (r * r).astype(BF16)

    up, act = _mm(h2, w_up_t, "nt", name="mlp_up", outs=[BF16, BF16], tm=tm2, tn=1024, tk=D, epilogue=relu2)

    dx2, ddown, loss_p, dnwf, dgate_f = _mm(
        act, w_dn, "nn", name="mlp_down", tm=tmh, tn=D, tk=DFF,
        extras=[(x1, *_rows(tmh)), (tgt, *_rows(tmh)), (gate_f, *_vecs()), (norm_final_w.reshape(1, D), *_vecs())],
        outs=[F32, BF16, _sum_out(128), _sum_out(), _sum_out()], epilogue=_ep_final)

    def drelu2(acc, ex, outs):
        outs[0][...] = (acc * (2.0 * jnp.maximum(ex[0][...].astype(F32), 0.0))).astype(BF16)

    def dep_last(ep):
        return lambda acc, ex, outs: ep(acc, ex[:-1], outs)

    dup = _mm(ddown, w_dn, "nt", name="mlp_down_dx", outs=[BF16], tm=tm2, tn=1024, tk=D,
              extras=[(up, (tm2, 1024), lambda i, j, k: (i, j))], epilogue=drelu2)
    g_dn = _mm(act, ddown, "tn", name="mlp_down_dw", outs=[SLAB_DT], tm=1024, tn=D, tk=tkl)
    g_up_t = _mm(dup, h2, "tn", name="mlp_up_dw", outs=[SLAB_DT], tm=1024, tn=D, tk=tkl)
    gslab_mlp = jnp.concatenate([g_up_t.reshape(N_DEV, 512, D), g_dn.reshape(N_DEV, 512, D)], axis=1)
    mlp_started = _xchg_start(gslab_mlp, per_peer=True, name="scatter_mlp_start")
    dx1, p2, q2, dmix, dgate_m = _mm(
        dup, w_up_t, "nn", name="mlp_up_dx", tm=tmh, tn=D, tk=DFF,
        extras=[(x1, *_rows(tmh)), (dx2, *_rows(tmh)), (norm_mlp_w, *_vecs()), (scale_f, *_vecs()),
                (mix, *_rows(tmh)), (gate_m, *_vecs()), _dep(mlp_started[4])],
        outs=[F32, _sum_out(), _sum_out(), BF16, _sum_out()], epilogue=dep_last(_ep_norm_bwd))
    gcol = C_GATE // (2 * D)
    dy_ssd, dy_pool, dproj = _mm(
        dmix, w_o, "nt", name="out_proj_dx", tm=tmh, tn=D, tk=D,
        extras=[(y_ssd, *_rows(tmh)), (y_pool, *_rows(tmh)), (proj, *_rows(tmh, 2 * D, gcol))],
        outs=[BF16, BF16, ((L, NPROJ), BF16, *_rows(tmh, 2 * D, gcol))], epilogue=_ep_merge_bwd)
    g_o = _mm(m, dmix, "tn", name="out_proj_dw", outs=[SLAB_DT], tm=D, tn=D, tk=tkl)
    zcol = C_Z // DI
    dy_ssm, dproj, d_snw = _mm(
        dy_ssd, w_bssd, "nt", name="branch_ssd_dx", tm=tmh, tn=DI, tk=D,
        extras=[(y_ssm, *_rows(tmh, DI)), (proj, *_rows(tmh, DI, zcol)), (ssd_norm_w, *_vecs(DI)),
                (dproj, None, None)],
        outs=[F32, ((L, NPROJ), BF16, *_rows(tmh, DI, zcol)), _sum_out(DI)],
        epilogue=_ep_gated_norm_bwd, aliases={3: 1})
    g_bssd = _mm(yn, dy_ssd, "tn", name="branch_ssd_dw", outs=[SLAB_DT], tm=1024, tn=D, tk=tkl)
    dxbc, dproj, d_a, d_dx, d_dtb = _ssd_bwd(dy_ssm, xbc, dtp, hs, arow, dsk_x, dproj, name="ssd_bwd")
    dproj, d_cw, d_cb = _conv_bwd(xbc_raw, dxbc, conv_full, conv_b, dproj, name="conv_bwd")
    def pool_bwd_ep(acc, ex, outs):
        y_ref, s_ref, wp_ref = ex
        o_ref, ds_ref, dpool_ref = outs
        dyp0_v = (acc * s_ref[...]).astype(BF16)
        o_ref[...] = dyp0_v
        _acc_out(ds_ref, _colsum(acc * y_ref[...].astype(F32)), _row_step())
        for g in range(4):
            sl = slice(g * PGW, (g + 1) * PGW)
            dpool_ref[:, sl] = _dot(dyp0_v[:, sl], wp_ref[sl, :], NT)

    dyp0, d_ps, dpooled = _mm(dy_pool, w_bpool, "nt", name="branch_pool_dx", tm=tm, tn=D, tk=D,
                              extras=[(yp0, *_rows(tm)), (pool_scale, *_vecs()), (w_pool, *wp_spec)],
                              outs=[BF16, _sum_out(), F32], epilogue=pool_bwd_ep)
    g_bpool = _mm(yp1, dy_pool, "tn", name="branch_pool_dw", outs=[SLAB_DT], tm=D, tn=D, tk=tkl)
    g_pool = _mm_pool_tn(pooled, dyp0, name="pool_mix_dw", tk=tkl)
    gslab_mix = jnp.concatenate([
        g_bssd.reshape(N_DEV, 256, D),
        g_pool.reshape(4, N_DEV, 32, PGW).transpose(1, 0, 2, 3).reshape(N_DEV, 32, D).astype(SLAB_DT),
        g_bpool.reshape(N_DEV, 128, D),
        g_o.reshape(N_DEV, 128, D)], axis=1)
    mix_started = _xchg_start(gslab_mix, per_peer=True, name="scatter_mix_start")
    dproj = _pool_bwd(dpooled, dproj, name="pool_bwd")
    g_in_t = _mm(dproj, h1, "tn", name="in_proj_dw", outs=[SLAB_DT], tm=1408, tn=D, tk=tkl2,
                 extras=[_dep(mix_started[4])])
    gslab_in = _restore_in_shards(g_in_t)
    in_started = _xchg_start(gslab_in, per_peer=True, name="scatter_in_start")
    grad_x, p1, q1 = _mm(
        dproj, w_in_t, "nn", name="in_proj_dx", tm=tmh, tn=D, tk=2816,
        extras=[(xs_, *_rows(tmh)), (dx1, *_rows(tmh)), (norm_mix_w, *_vecs()), (scale_m, *_vecs()),
                _dep(in_started[4])],
        outs=[F32, _sum_out(), _sum_out()], epilogue=dep_last(_ep_norm_bwd))

    dmod = jnp.concatenate([q1, p1 * norm_mix_w, dgate_m, q2, p2 * norm_mlp_w, dgate_f], axis=1)
    d_alog = d_a[:, :NH] * (-jnp.exp(a_log))
    sv = _pack_sv({
        "b_ada": dmod, "norm_mix_w": p1 * (1.0 + scale_m), "conv_b": d_cb, "dt_bias": d_dtb[:, :NH],
        "a_log": d_alog, "d_skip": d_dx.reshape(NH, HP).sum(axis=1), "ssd_norm_w": d_snw,
        "pool_scale": d_ps, "norm_mlp_w": p2 * (1.0 + scale_f), "norm_final_w": dnwf, "conv_w": d_cw,
        "loss": loss_p[:, :1]})
    sv_rows = SV_ROWS * 128 // D
    small_started = _xchg_start(_pad_rows(sv.reshape(sv_rows, D), 32), per_peer=False, name="small_start")
    after = lax.optimization_barrier((grad_x, small_started[4]))[0]

    def landed(started, tile, name):
        src, land = _xchg_wait(started, after, per_peer=True, name=name + "_wait")
        own = lax.dynamic_slice_in_dim(src, me, 1, axis=0)
        return _slab_sum(lax.dynamic_update_slice(land, own, (me, 0, 0)), tile=tile, name=name + "_sum")

    gsum_mlp = landed(mlp_started, 256, "scatter_mlp")
    gsum_mix = landed(mix_started, 272, "scatter_mix")
    gsum_in = landed(in_started, 208, "scatter_in")

    def gpart(n, rows_):
        return gsum_mix[MIX_OFF[n]:MIX_OFF[n] + rows_]

    def lin(a):
        return a[0].T.reshape(IN_ROWS * 8, 128)

    g_lin = lax.dynamic_slice_in_dim(gsum_in, in_shift, IN_ROWS, axis=0).reshape(IN_ROWS * 8, 128)
    dlt, mn, vn = _adamw(lin(w_in), g_lin, lin(m_w_in), lin(v_w_in), name="adamw_w_in", tr=IN_ROWS * 2)
    big_in = tuple(a.reshape(IN_ROWS, D).T[None] for a in (g_lin, dlt, mn, vn))

    big = {
        "w_branch_ssd": (w_branch_ssd, m_w_branch_ssd, v_w_branch_ssd, gpart("bssd", 256), (256, D)),
        "pool_w": (pool_w, m_pool_w, v_pool_w, gpart("pool", 32).reshape(128, PGW), (128, PGW)),
        "w_branch_pool": (w_branch_pool, m_w_branch_pool, v_w_branch_pool, gpart("bpool", 128), (128, D)),
        "w_out": (w_out, m_w_out, v_w_out, gpart("out", 128), (128, D)),
        "w_up": (w_up, m_w_up, v_w_up, gsum_mlp[:512].T, (D, 512)),
        "w_down": (w_down, m_w_down, v_w_down, gsum_mlp[512:], (512, D)),
    }
    big_out = {}

    def update(n, w, mm_, vv, g, shp2):
        dlt, mn, vn = _adamw(w.reshape(shp2), g, mm_.reshape(shp2), vv.reshape(shp2), name="adamw_" + n)
        big_out[n] = (g.reshape(w.shape), dlt.reshape(w.shape), mn.reshape(w.shape), vn.reshape(w.shape))

    for n, args in big.items():
        update(n, *args)

    sv_src, sv_land = _xchg_wait(small_started, big_out["w_down"][1], per_peer=False, name="small_wait")
    sv_land = lax.dynamic_update_slice(sv_land, sv_src[None], (me, 0, 0))
    flat = _slab_sum(sv_land, tile=32, name="small_sum").reshape(-1)
    loss = flat[SV_OFF["loss"]]
    dmod_all = sv_land.reshape(N_DEV, 32 * D)[:, :6 * D]
    g_w_ada = _ada_bwd(c_all, lax.dynamic_slice_in_dim(dmod_all, me * wloc, wloc, axis=1), name="ada_bwd")
    update("w_ada", w_ada, m_w_ada, v_w_ada, g_w_ada, (D, wloc))

    g_conv_w = lax.dynamic_slice_in_dim(_sv_get(flat, "conv_w", 4 * XBC).reshape(4, XBC),
                                        me * (XBC // N_DEV), XBC // N_DEV, axis=1)
    small = [("b_ada", b_ada, m_b_ada, v_b_ada), ("norm_mix_w", norm_mix_w, m_norm_mix_w, v_norm_mix_w),
             ("conv_b", conv_b, m_conv_b, v_conv_b), ("dt_bias", dt_bias, m_dt_bias, v_dt_bias),
             ("a_log", a_log, m_a_log, v_a_log), ("d_skip", d_skip, m_d_skip, v_d_skip),
             ("ssd_norm_w", ssd_norm_w, m_ssd_norm_w, v_ssd_norm_w),
             ("pool_scale", pool_scale, m_pool_scale, v_pool_scale),
             ("norm_mlp_w", norm_mlp_w, m_norm_mlp_w, v_norm_mlp_w),
             ("norm_final_w", norm_final_w[None], m_norm_final_w[None], v_norm_final_w[None]),
             ("conv_w", conv_w[0], m_conv_w[0], v_conv_w[0])]
    small_out = _adamw_small(flat[:SV_ROWS * 128].reshape(1, SV_ROWS * 128), g_conv_w, small, name="adamw_small")
    small_out["norm_final_w"] = tuple(a[0] for a in small_out["norm_final_w"])
    small_out["conv_w"] = tuple(a[None] for a in small_out["conv_w"])

    order = ["w_ada", "b_ada", "norm_mix_w", "w_in", "conv_w", "conv_b", "dt_bias", "a_log", "d_skip",
             "ssd_norm_w", "w_branch_ssd", "pool_w", "pool_scale", "w_branch_pool", "w_out", "norm_mlp_w",
             "w_up", "w_down", "norm_final_w"]
    big_out["w_in"] = big_in
    res = {**small_out, **big_out}
    outs = [loss, grad_x.reshape(x.shape)]
    for k in range(4):
        outs += [res[n][k] for n in order]
    return tuple(outs)
```

```python
import functools

import numpy as np
import jax
import jax.numpy as jnp
from jax import lax
from jax.experimental import pallas as pl
from jax.experimental.pallas import tpu as pltpu

F32 = jnp.float32
BF16 = jnp.bfloat16
SLAB_DT = jnp.bfloat16
_MXU_DTYPE = jnp.bfloat16

N_DEV = 8
D = 1024
DI = 2048
NH = 32
HP = 64
NG = 4
NS = 128
Q = 128
XBC = DI + 2 * NG * NS
DFF = 4096
N_IN = 8224
EPS = 1e-5
POOL_W = 1024
PGW = 256

C_XBC, C_POOL, C_Z, C_GATE, C_DT = 0, 3072, 4096, 6144, 8192
DT_PAD = 256
NPROJ = C_DT + DT_PAD

IN_ROWS = N_IN // N_DEV
IN_ROWS_P = 1040
CONV_ROWS = 16
REST_PARTS = (("bssd", 256), ("pool", 32), ("bpool", 128), ("out", 128), ("up", 512), ("down", 512))
REST_OFF = {}
_o = 0
for _n, _r in REST_PARTS:
    REST_OFF[_n] = _o
    _o += _r
REST_ROWS = _o
MIX_PARTS = (("bssd", 256), ("pool", 32), ("bpool", 128), ("out", 128))
MIX_OFF = {}
_o = 0
for _n, _r in MIX_PARTS:
    MIX_OFF[_n] = _o
    _o += _r
MIX_ROWS = _o

SV_PARTS = (("b_ada", 6144), ("norm_mix_w", 1024), ("conv_b", 3072), ("dt_bias", 128), ("a_log", 128),
            ("d_skip", 128), ("ssd_norm_w", 2048), ("pool_scale", 1024), ("norm_mlp_w", 1024),
            ("norm_final_w", 1024), ("conv_w", 4 * XBC), ("loss", 128))
SV_OFF = {}
_o = 0
for _n, _r in SV_PARTS:
    SV_OFF[_n] = _o
    _o += _r
SV_ROWS = 224
assert _o <= SV_ROWS * 128

ADAM_LR, ADAM_B1, ADAM_B2, ADAM_EPS, ADAM_WD, ADAM_STEP = 0.001, 0.9, 0.999, 1e-08, 0.01, 10

VMEM_BIG = 56 * 1024 * 1024
NEG = -1e30

NN = ((1,), (0,))
NT = ((1,), (1,))
TN = ((0,), (0,))


def _dot(a, b, dims=NN):
    return lax.dot_general(a.astype(_MXU_DTYPE), b.astype(_MXU_DTYPE), (dims, ((), ())),
                           preferred_element_type=F32)


def _dot_hi(a, b, dims=NN):
    return lax.dot_general(a.astype(F32), b.astype(F32), (dims, ((), ())),
                           precision=lax.Precision.HIGHEST, preferred_element_type=F32)


def _pick(n, cands):
    for c in cands:
        if n % c == 0:
            return c
    return n


def _sigmoid(x):
    return 1.0 / (1.0 + jnp.exp(-x))


def _silu(x):
    return x * _sigmoid(x)


def _dsilu(x):
    s = _sigmoid(x)
    return s * (1.0 + x * (1.0 - s))


def _softplus(x):
    return jnp.maximum(x, 0.0) + jnp.log(1.0 + jnp.exp(-jnp.abs(x)))


def _params(sem, vmem=None):
    return pltpu.CompilerParams(dimension_semantics=sem, vmem_limit_bytes=vmem)


def _row_step():
    return pl.program_id(0)


def _mm(a, b, mode, *, name, outs, tm, tn, tk, extras=(), epilogue=None, aliases=None, prologue=None):
    if mode == "tn":
        K, M = a.shape
        N = b.shape[1]
        a_spec = pl.BlockSpec((tk, tm), lambda i, j, k: (k, i))
        b_spec = pl.BlockSpec((tk, tn), lambda i, j, k: (k, j))
        dims = TN
    else:
        M = a.shape[0]
        K = b.shape[0] if mode == "nn" else b.shape[1]
        if prologue is None:
            assert a.shape[1] == K
            a_spec = pl.BlockSpec((tm, tk), lambda i, j, k: (i, k))
        else:
            assert tk == K
            a_spec = pl.BlockSpec((tm, a.shape[1]), lambda i, j, k: (i, 0))
        if mode == "nn":
            N = b.shape[1]
            b_spec = pl.BlockSpec((tk, tn), lambda i, j, k: (k, j))
            dims = NN
        else:
            N = b.shape[0]
            b_spec = pl.BlockSpec((tn, tk), lambda i, j, k: (j, k))
            dims = NT
    assert M % tm == 0 and N % tn == 0 and K % tk == 0, (name, M, N, K, tm, tn, tk)
    nk = K // tk
    ne, no = len(extras), len(outs)
    if epilogue is None:
        def epilogue(acc, ex, out_refs):
            out_refs[0][...] = acc.astype(out_refs[0].dtype)

    def body(a_ref, b_ref, *rest):
        ex, out_refs = rest[:ne], rest[ne:ne + no]
        lhs = a_ref[...] if prologue is None else prologue(a_ref, ex, out_refs, pl.program_id(1))
        p = _dot(lhs, b_ref[...], dims)
        if nk == 1:
            epilogue(p, ex, out_refs)
        else:
            acc = rest[-1]
            k = pl.program_id(2)

            @pl.when(k == 0)
            def _():
                acc[...] = p

            @pl.when(jnp.logical_and(k > 0, k < nk - 1))
            def _():
                acc[...] += p

            @pl.when(k == nk - 1)
            def _():
                epilogue(acc[...] + p, ex, out_refs)

    out_specs, out_shape = [], []
    for o in outs:
        if isinstance(o, tuple):
            shape, dt, bs, im = o
            out_specs.append(pl.BlockSpec(bs, im))
            out_shape.append(jax.ShapeDtypeStruct(shape, dt))
        else:
            out_specs.append(pl.BlockSpec((tm, tn), lambda i, j, k: (i, j)))
            out_shape.append(jax.ShapeDtypeStruct((M, N), o))
    in_specs = [a_spec, b_spec]
    for _, bs, im in extras:
        in_specs.append(pl.BlockSpec(memory_space=pl.ANY) if bs is None else pl.BlockSpec(bs, im))
    res = pl.pallas_call(
        body, name=name,
        grid=(M // tm, N // tn, nk),
        in_specs=in_specs, out_specs=out_specs, out_shape=out_shape,
        scratch_shapes=[pltpu.VMEM((tm, tn), F32)] if nk > 1 else [],
        input_output_aliases={2 + e: o for e, o in (aliases or {}).items()},
        compiler_params=_params(("arbitrary", "arbitrary", "arbitrary"), VMEM_BIG),
    )(a, b, *[e[0] for e in extras])
    return res if no > 1 else res[0]


def _rows(tm, w=D, col=0):
    return (tm, w), lambda i, j, k, c=col: (i, c)


def _vecs(w=D, col=0):
    return (1, w), lambda i, j, k, c=col: (0, c)


def _sum_out(w=D):
    return ((1, w), F32, (1, w), lambda i, j, k: (0, 0))


def _mm_pool_tn(a, b, *, name, tk):
    L = a.shape[0]

    def body(a_ref, b_ref, o_ref):
        p = _dot(a_ref[...], b_ref[...], TN)

        @pl.when(pl.program_id(1) == 0)
        def _():
            o_ref[...] = p

        @pl.when(pl.program_id(1) > 0)
        def _():
            o_ref[...] += p

    blk = pl.BlockSpec((tk, PGW), lambda g, k: (k, g))
    return pl.pallas_call(body, name=name, grid=(4, L // tk), in_specs=[blk, blk],
                          out_specs=pl.BlockSpec((PGW, PGW), lambda g, k: (g, 0)),
                          out_shape=jax.ShapeDtypeStruct((POOL_W, PGW), F32),
                          compiler_params=_params(("parallel", "arbitrary")))(a, b)


def _acc_out(ref, val, i):
    @pl.when(i == 0)
    def _():
        ref[...] = val

    @pl.when(i > 0)
    def _():
        ref[...] += val


def _colsum(v):
    return jnp.sum(v, axis=0, keepdims=True)


def _ep_resid_norm(acc, ex, outs):
    x_ref, g_ref, nw_ref, sc_ref, sh_ref = ex
    mix_ref, x1_ref, h_ref = outs
    mix_ref[...] = acc.astype(mix_ref.dtype)
    xv = x_ref[...] + g_ref[...] * acc
    x1_ref[...] = xv
    r = lax.rsqrt(jnp.mean(xv * xv, axis=-1, keepdims=True) + EPS)
    h_ref[...] = (xv * r * nw_ref[...] * (1.0 + sc_ref[...]) + sh_ref[...]).astype(h_ref.dtype)


def _ep_final(acc, ex, outs):
    x1_ref, t_ref, g_ref, nw_ref = ex
    dx2_ref, dd_ref, loss_ref, dnw_ref, dg_ref = outs
    i = _row_step()
    x2 = x1_ref[...] + g_ref[...] * acc
    r = lax.rsqrt(jnp.mean(x2 * x2, axis=-1, keepdims=True) + EPS)
    xh = x2 * r
    e = xh * nw_ref[...] - t_ref[...]
    part = 0.5 * jnp.sum(jnp.mean(e * e, axis=-1, keepdims=True), axis=0, keepdims=True)
    dy = e * (1.0 / D)
    g = dy * nw_ref[...]
    dx2 = r * (g - xh * jnp.mean(g * xh, axis=-1, keepdims=True))
    dx2_ref[...] = dx2
    dd_ref[...] = (dx2 * g_ref[...]).astype(dd_ref.dtype)
    _acc_out(loss_ref, jnp.broadcast_to(part, (1, 128)), i)
    _acc_out(dnw_ref, _colsum(dy * xh), i)
    _acc_out(dg_ref, _colsum(dx2 * acc), i)


def _ep_norm_bwd(acc, ex, outs):
    x_ref, dr_ref, nw_ref, sc_ref = ex[:4]
    dx_ref, p_ref, q_ref = outs[:3]
    i = _row_step()
    xv = x_ref[...]
    r = lax.rsqrt(jnp.mean(xv * xv, axis=-1, keepdims=True) + EPS)
    xh = xv * r
    g = acc * (nw_ref[...] * (1.0 + sc_ref[...]))
    dx = dr_ref[...] + r * (g - xh * jnp.mean(g * xh, axis=-1, keepdims=True))
    dx_ref[...] = dx
    _acc_out(p_ref, _colsum(acc * xh), i)
    _acc_out(q_ref, _colsum(acc), i)
    if len(ex) > 4:
        m_ref, g_ref = ex[4:]
        dm_ref, dg_ref = outs[3:]
        dm_ref[...] = (dx * g_ref[...]).astype(dm_ref.dtype)
        _acc_out(dg_ref, _colsum(dx * m_ref[...].astype(F32)), i)


def _ep_merge_bwd(acc, ex, outs):
    a_ref, b_ref, gl_ref = ex
    da_ref, db_ref, dgl_ref = outs
    s = _sigmoid(gl_ref[...].astype(F32))
    s1, s2 = s[:, :D], s[:, D:]
    da_ref[...] = (acc * s1).astype(da_ref.dtype)
    db_ref[...] = (acc * s2).astype(db_ref.dtype)
    dgl_ref[:, :D] = (acc * a_ref[...].astype(F32) * s1 * (1.0 - s1)).astype(dgl_ref.dtype)
    dgl_ref[:, D:] = (acc * b_ref[...].astype(F32) * s2 * (1.0 - s2)).astype(dgl_ref.dtype)


GW = DI // NG


def _ep_gated_norm_bwd(acc, ex, outs):
    y_ref, z_ref, w_ref, _ = ex
    dy_ref, dz_ref, dw_ref = outs
    zv = z_ref[...].astype(F32)
    yv = y_ref[...].astype(F32)
    sg = _sigmoid(zv)
    sz = zv * sg
    yg = yv * sz
    dsz = sg * (1.0 + zv * (1.0 - sg))
    dws = []
    for k in range(NG):
        sl = slice(k * GW, (k + 1) * GW)
        seg = yg[:, sl]
        r = lax.rsqrt(jnp.mean(seg * seg, axis=-1, keepdims=True) + EPS)
        sh = seg * r
        dn = acc[:, sl]
        g = dn * w_ref[:, sl]
        dyg = r * (g - sh * jnp.mean(g * sh, axis=-1, keepdims=True))
        dy_ref[:, sl] = dyg * sz[:, sl]
        dz_ref[:, sl] = (dyg * yv[:, sl] * dsz[:, sl]).astype(dz_ref.dtype)
        dws.append(_colsum(dn * sh))
    _acc_out(dw_ref, jnp.concatenate(dws, axis=1), _row_step())


CONV_CB = 128
HALO = 16


def _time_chunk(L):
    return _pick(L, (256, 128))


def _with_halo(x_ref, i, r0, rc):
    p0 = pl.multiple_of(jnp.maximum(r0 - HALO, 0), HALO)
    prev = jnp.where(i > 0, x_ref[pl.ds(p0, HALO), :].astype(F32), 0.0)
    return jnp.concatenate([prev, x_ref[pl.ds(r0, rc), :].astype(F32)], axis=0)


def _conv_bwd(proj, dy, w, b, dproj, *, name):
    L = proj.shape[0]
    rc = _time_chunk(L)
    n = L // rc

    def body(x_ref, dy_ref, w_ref, b_ref, dp_in, dx_ref, dw_ref, db_ref, xpad, dpad):
        del dp_in
        wv = w_ref[...]
        bv = b_ref[...]
        dpad[rc:rc + HALO, :] = jnp.zeros((HALO, CONV_CB), F32)

        def step(k, carry):
            db, d0, d1, d2, d3 = carry
            i = n - 1 - k
            r0 = pl.multiple_of(i * rc, rc)
            p0 = pl.multiple_of(jnp.maximum(r0 - HALO, 0), HALO)
            xpad[0:HALO, :] = jnp.where(i > 0, x_ref[pl.ds(p0, HALO), :].astype(F32), 0.0)
            xpad[HALO:HALO + rc, :] = x_ref[pl.ds(r0, rc), :].astype(F32)
            xk = [xpad[HALO - j:HALO - j + rc, :] for j in range(4)]
            pre = bv
            for j in range(4):
                pre = pre + xk[j] * wv[3 - j:4 - j]
            dpre = dy_ref[pl.ds(r0, rc), :] * _dsilu(pre)
            dpad[0:rc, :] = dpre
            acc = dpre * wv[3:4]
            for j in (1, 2, 3):
                acc = acc + dpad[j:j + rc, :] * wv[3 - j:4 - j]
            dx_ref[pl.ds(r0, rc), :] = acc.astype(dx_ref.dtype)
            dpad[rc:rc + HALO, :] = dpre[:HALO]
            return (db + _colsum(dpre), d0 + _colsum(dpre * xk[3]), d1 + _colsum(dpre * xk[2]),
                    d2 + _colsum(dpre * xk[1]), d3 + _colsum(dpre * xk[0]))

        z = jnp.zeros((1, CONV_CB), F32)
        db, d0, d1, d2, d3 = lax.fori_loop(0, n, step, (z, z, z, z, z))
        db_ref[...] = db
        dw_ref[...] = jnp.concatenate([d0, d1, d2, d3], axis=0)

    nb = XBC // CONV_CB
    return pl.pallas_call(
        body, name=name, grid=(nb,),
        in_specs=[pl.BlockSpec((L, CONV_CB), lambda j: (0, j + C_XBC // CONV_CB)),
                  pl.BlockSpec((L, CONV_CB), lambda j: (0, j)),
                  pl.BlockSpec((4, CONV_CB), lambda j: (0, j)), pl.BlockSpec((1, CONV_CB), lambda j: (0, j)),
                  pl.BlockSpec(memory_space=pl.ANY)],
        out_specs=[pl.BlockSpec((L, CONV_CB), lambda j: (0, j + C_XBC // CONV_CB)),
                   pl.BlockSpec((4, CONV_CB), lambda j: (0, j)), pl.BlockSpec((1, CONV_CB), lambda j: (0, j))],
        out_shape=[jax.ShapeDtypeStruct((L, NPROJ), BF16), jax.ShapeDtypeStruct((4, XBC), F32),
                   jax.ShapeDtypeStruct((1, XBC), F32)],
        scratch_shapes=[pltpu.VMEM((rc + HALO, CONV_CB), F32), pltpu.VMEM((rc + HALO, CONV_CB), F32)],
        input_output_aliases={4: 0},
        compiler_params=_params(("parallel",), VMEM_BIG))(proj, dy, w, b, dproj)


def _pool_fwd(proj, *, name):
    L = proj.shape[0]
    rc = _time_chunk(L)
    n = L // rc

    def body(x_ref, o_ref, pad):
        g = pl.program_id(0)
        pad[0:HALO, :] = jnp.zeros((HALO, PGW), F32)

        def fill(i, c):
            r0 = pl.multiple_of(i * rc, rc)
            pad[pl.ds(r0 + HALO, rc), :] = x_ref[pl.ds(r0, rc), :].astype(F32)
            return c

        lax.fori_loop(0, n, fill, 0)
        rows = lax.broadcasted_iota(jnp.int32, (rc, PGW), 0)

        for gi in range(4):
            win = 2 << gi

            @pl.when(g == gi)
            def _(gi=gi, win=win):
                def step(i, c):
                    r0 = pl.multiple_of(i * rc, rc)
                    ext = pad[pl.ds(r0, rc + HALO), :]
                    s = ext
                    sh = 1
                    while sh < win:
                        s = s + pltpu.roll(s, sh, 0)
                        sh *= 2
                    cnt = jnp.minimum(rows + (r0 + 1), win).astype(F32)
                    o_ref[pl.ds(r0, rc), :] = (s[HALO:] / cnt - ext[HALO:]).astype(o_ref.dtype)
                    return c

                lax.fori_loop(0, n, step, 0)

    return pl.pallas_call(
        body, name=name, grid=(4,),
        in_specs=[pl.BlockSpec((L, PGW), lambda j: (0, j + C_POOL // PGW))],
        out_specs=pl.BlockSpec((L, PGW), lambda j: (0, j)),
        out_shape=jax.ShapeDtypeStruct((L, POOL_W), BF16),
        scratch_shapes=[pltpu.VMEM((L + HALO, PGW), F32)],
        compiler_params=_params(("parallel",), VMEM_BIG))(proj)


def _pool_bwd(dpooled, dproj, *, name):
    L = dpooled.shape[0]
    rc = _time_chunk(L)
    n = L // rc

    def body(d_ref, dp_in, o_ref, pad):
        del dp_in
        g = pl.program_id(0)
        pad[L:L + HALO, :] = jnp.zeros((HALO, PGW), F32)
        rows = lax.broadcasted_iota(jnp.int32, (rc, PGW), 0)

        for gi in range(4):
            win = 2 << gi

            @pl.when(g == gi)
            def _(gi=gi, win=win):
                def fill(i, c):
                    r0 = pl.multiple_of(i * rc, rc)
                    cnt = jnp.minimum(rows + (r0 + 1), win).astype(F32)
                    pad[pl.ds(r0, rc), :] = d_ref[pl.ds(r0, rc), :] / cnt
                    return c

                lax.fori_loop(0, n, fill, 0)

                def step(i, c):
                    r0 = pl.multiple_of(i * rc, rc)
                    s = pad[pl.ds(r0, rc + HALO), :]
                    sh = 1
                    while sh < win:
                        s = s + pltpu.roll(s, rc + HALO - sh, 0)
                        sh *= 2
                    o_ref[pl.ds(r0, rc), :] = (s[:rc] - d_ref[pl.ds(r0, rc), :]).astype(o_ref.dtype)
                    return c

                lax.fori_loop(0, n, step, 0)

    return pl.pallas_call(
        body, name=name, grid=(4,),
        in_specs=[pl.BlockSpec((L, PGW), lambda j: (0, j)), pl.BlockSpec(memory_space=pl.ANY)],
        out_specs=pl.BlockSpec((L, PGW), lambda j: (0, j + C_POOL // PGW)),
        out_shape=jax.ShapeDtypeStruct((L, NPROJ), BF16),
        scratch_shapes=[pltpu.VMEM((L + HALO, PGW), F32)],
        input_output_aliases={1: 0},
        compiler_params=_params(("parallel",), VMEM_BIG))(dpooled, dproj)


_SPLIT_DT = jnp.bfloat16


def _ssd_consts():
    tri = np.tril(np.ones((Q, Q), np.float32))
    exp = np.zeros((128, DI), np.float32)
    for h in range(NH):
        exp[h, h * HP:(h + 1) * HP] = 1.0
    exp2 = np.concatenate([exp, exp], axis=0)
    return (jnp.asarray(tri, dtype=_SPLIT_DT), jnp.asarray(tri.T.copy(), dtype=_SPLIT_DT),
            jnp.asarray(exp2, dtype=_SPLIT_DT))


def _split(v, n):
    parts, r = [], v
    for _ in range(n):
        p = r.astype(_SPLIT_DT)
        parts.append(p)
        r = r - p.astype(F32)
    return parts


def _bdot(a, b, dims):
    return lax.dot_general(a, b, (dims, ((), ())), preferred_element_type=F32)


def _tri_sum(t_ref, v):
    r = _bdot(t_ref[...], jnp.concatenate(_split(v, 3), axis=1), NN)
    return r[:, :128] + r[:, 128:256] + r[:, 256:]


def _expand(v, e2_ref):
    return _bdot(jnp.concatenate(_split(v, 2), axis=1), e2_ref[...], NN)


def _reduce_heads(vals, eg):
    parts = []
    for v in vals:
        parts += _split(v, 2)
    r = _bdot(jnp.concatenate(parts, axis=0), eg, NT)
    return [r[2 * i * Q:(2 * i + 1) * Q] + r[(2 * i + 1) * Q:(2 * i + 2) * Q] for i in range(len(vals))]


def _ssd_common(xbc_ref, dtw_ref, arow_ref, t_ref, e_ref):
    dt = dtw_ref[:, :128]
    sig = dtw_ref[:, 128:]
    acs = _tri_sum(t_ref, dt * arow_ref[...])
    acs_x = _expand(acs, e_ref)
    dt_x = _expand(dt, e_ref)
    xs = xbc_ref[:, 0:DI]
    return sig, dt, acs, acs.T, acs_x, dt_x, xs


CONV_SLAB = 512


def _ssd_fwd(raw, dtp, cw, cb, arow, dsk_x, *, name):
    L = raw.shape[0]
    nc = L // Q
    tri, _, expand = _ssd_consts()

    def body(raw_ref, halo_ref, cw_ref, cb_ref, dtw_ref, arow_ref, dsk_ref, t_ref, e_ref,
             y_ref, hs_ref, xbc_ref, h_scr, cpad):
        c = pl.program_id(0)

        @pl.when(c == 0)
        def _():
            h_scr[...] = jnp.zeros_like(h_scr)

        cpad[0:8, :] = jnp.where(c > 0, halo_ref[...], 0.0)
        cpad[8:8 + Q, :] = raw_ref[...]
        for lo in range(0, XBC, CONV_SLAB):
            sl = slice(lo, lo + CONV_SLAB)
            acc = cb_ref[:, sl]
            for j in range(4):
                acc = acc + cpad[8 - j:8 - j + Q, sl] * cw_ref[3 - j:4 - j, sl]
            xbc_ref[:, sl] = acc * _sigmoid(acc)

        _, dt, acs, acs_t, acs_x, dt_x, xs = _ssd_common(xbc_ref, dtw_ref, arow_ref, t_ref, e_ref)
        xdt = xs * dt_x
        eacs = jnp.exp(acs_x)
        acs_last = acs_x[Q - 1:Q, :]
        dec = jnp.exp(acs_last - acs_x)
        hs_ref[0] = h_scr[...].astype(hs_ref.dtype)
        causal = lax.broadcasted_iota(jnp.int32, (Q, Q), 0) >= lax.broadcasted_iota(jnp.int32, (Q, Q), 1)
        first = lax.broadcasted_iota(jnp.int32, (Q, 128), 1) < HP
        for g in range(NG):
            bg = xbc_ref[:, DI + g * NS:DI + (g + 1) * NS]
            cg = xbc_ref[:, DI + NG * NS + g * NS:DI + NG * NS + (g + 1) * NS]
            s = _dot(cg, bg, NT)
            sl = slice(g * GW, (g + 1) * GW)
            hg = h_scr[:, sl]
            yoff = _dot(cg, hg, NN) * eacs[:, sl]
            st = _dot(bg, xdt[:, sl] * dec[:, sl], TN)
            h_scr[:, sl] = hg * eacs[Q - 1:Q, sl] + st
            for j in range(4):
                lo = g * GW + j * 128
                xb = xdt[:, lo:lo + 128]
                yp = yoff[:, j * 128:(j + 1) * 128] + dsk_ref[:, lo:lo + 128] * xs[:, lo:lo + 128]
                for e in range(2):
                    h = g * 8 + j * 2 + e
                    lm = jnp.exp(jnp.where(causal, acs[:, h:h + 1] - acs_t[h:h + 1, :], NEG))
                    xm = jnp.where(first if e == 0 else jnp.logical_not(first), xb, 0.0)
                    yp = yp + _dot(s * lm, xm, NN)
                y_ref[:, lo:lo + 128] = yp.astype(y_ref.dtype)

    const = lambda c: (0, 0)
    return pl.pallas_call(
        body, name=name, grid=(nc,),
        in_specs=[pl.BlockSpec((Q, XBC), lambda c: (c, 0)),
                  pl.BlockSpec((8, XBC), lambda c: (jnp.maximum(c * (Q // 8) - 1, 0), 0)),
                  pl.BlockSpec((4, XBC), const), pl.BlockSpec((1, XBC), const),
                  pl.BlockSpec((Q, DT_PAD), lambda c: (c, 0)),
                  pl.BlockSpec((1, 128), const), pl.BlockSpec((1, DI), const),
                  pl.BlockSpec((Q, Q), const), pl.BlockSpec((256, DI), const)],
        out_specs=[pl.BlockSpec((Q, DI), lambda c: (c, 0)), pl.BlockSpec((1, NS, DI), lambda c: (c, 0, 0)),
                   pl.BlockSpec((Q, XBC), lambda c: (c, 0))],
        out_shape=[jax.ShapeDtypeStruct((L, DI), BF16), jax.ShapeDtypeStruct((nc, NS, DI), F32),
                   jax.ShapeDtypeStruct((L, XBC), F32)],
        scratch_shapes=[pltpu.VMEM((NS, DI), F32), pltpu.VMEM((8 + Q, XBC), F32)],
        compiler_params=_params(("arbitrary",), VMEM_BIG))(raw, raw, cw, cb, dtp, arow, dsk_x, tri, expand)


def _ssd_bwd(dy, xbc, dtp, hs, arow, dsk_x, dproj, *, name):
    L = xbc.shape[0]
    nc = L // Q
    tri, triu, expand = _ssd_consts()

    def body(dy_ref, xbc_ref, dtw_ref, hs_ref, arow_ref, dsk_ref, t_ref, u_ref, e_ref, dp_in,
             dxbc_ref, ddtw_ref, da_ref, ddx_ref, ddtb_ref, dh_scr):
        del dp_in
        i = pl.program_id(0)

        @pl.when(i == 0)
        def _():
            dh_scr[...] = jnp.zeros_like(dh_scr)

        sig, dt, acs, acs_t, acs_x, dt_x, xs = _ssd_common(xbc_ref, dtw_ref, arow_ref, t_ref, e_ref)
        dyv = dy_ref[...]
        xdt = xs * dt_x
        eacs = jnp.exp(acs_x)
        acs_last = acs_x[Q - 1:Q, :]
        dec = jnp.exp(acs_last - acs_x)
        gy = dyv * eacs
        causal = lax.broadcasted_iota(jnp.int32, (Q, Q), 0) >= lax.broadcasted_iota(jnp.int32, (Q, Q), 1)
        first = lax.broadcasted_iota(jnp.int32, (Q, 128), 1) < HP
        lane_h = lax.broadcasted_iota(jnp.int32, (Q, 128), 1)
        sub_h = lax.broadcasted_iota(jnp.int32, (128, Q), 0)
        last_row = lax.broadcasted_iota(jnp.int32, (Q, GW), 0) == Q - 1
        dacs = jnp.zeros((Q, 128), F32)
        dacs_t = jnp.zeros((128, Q), F32)
        ddt = jnp.zeros((Q, 128), F32)
        for g in range(NG):
            bg = xbc_ref[:, DI + g * NS:DI + (g + 1) * NS]
            cg = xbc_ref[:, DI + NG * NS + g * NS:DI + NG * NS + (g + 1) * NS]
            s = _dot(cg, bg, NT)
            sl = slice(g * GW, (g + 1) * GW)
            hg = hs_ref[0, :, sl].astype(F32)
            dhn = dh_scr[:, sl]
            eal = eacs[Q - 1:Q, sl]
            gg = gy[:, sl]
            dax = gg * _dot(cg, hg, NN)
            dcg = _dot(gg, hg, NT)
            dh_scr[:, sl] = _dot(cg, gg, TN) + dhn * eal
            dal = eal * _colsum(dhn * hg)
            xdd = xdt[:, sl] * dec[:, sl]
            dbg = _dot(xdd, dhn, NT)
            wv = _dot(bg, dhn, NN)
            dd = wv * xdd
            dax = dax - dd
            dal = dal + _colsum(dd)
            dax = dax + jnp.where(last_row, dal, 0.0)
            dxdt_g = wv * dec[:, sl]
            ds = jnp.zeros((Q, Q), F32)
            dxdt_blocks = []
            for j in range(4):
                lo = g * GW + j * 128
                xb = xdt[:, lo:lo + 128]
                dyb = dyv[:, lo:lo + 128]
                dxb = dxdt_g[:, j * 128:(j + 1) * 128]
                for e in range(2):
                    h = g * 8 + j * 2 + e
                    lm = jnp.exp(jnp.where(causal, acs[:, h:h + 1] - acs_t[h:h + 1, :], NEG))
                    m = s * lm
                    dym = jnp.where(first if e == 0 else jnp.logical_not(first), dyb, 0.0)
                    dm = _dot(dym, xb, NT)
                    r = dm * m
                    dacs = dacs + jnp.where(lane_h == h, jnp.sum(r, axis=1, keepdims=True), 0.0)
                    dacs_t = dacs_t + jnp.where(sub_h == h, _colsum(r), 0.0)
                    ds = ds + dm * lm
                    dxb = dxb + _dot(m, dym, TN)
                dxdt_blocks.append(dxb)
            dxdt = jnp.concatenate(dxdt_blocks, axis=1)
            dcg = dcg + _dot(ds, bg, NN)
            dbg = dbg + _dot(ds, cg, TN)
            dxbc_ref[:, DI + g * NS:DI + (g + 1) * NS] = dbg
            dxbc_ref[:, DI + NG * NS + g * NS:DI + NG * NS + (g + 1) * NS] = dcg
            dxbc_ref[:, sl] = dsk_ref[:, sl] * dyv[:, sl] + dxdt * dt_x[:, sl]
            ddt_g, dacs_g = _reduce_heads([dxdt * xs[:, sl], dax], e_ref[0:128, sl])
            ddt = ddt + ddt_g
            dacs = dacs + dacs_g
        dacs = dacs - dacs_t.T
        ddta = _tri_sum(u_ref, dacs)
        ddt = ddt + ddta * arow_ref[...]
        ddtw = jnp.where(lane_h < NH, ddt * sig, 0.0)
        ddtw_ref[...] = jnp.concatenate([ddtw, jnp.zeros((Q, DT_PAD - 128), F32)], axis=1).astype(ddtw_ref.dtype)
        _acc_out(da_ref, _colsum(ddta * dt), i)
        _acc_out(ddx_ref, _colsum(dyv * xs), i)
        _acc_out(ddtb_ref, _colsum(ddtw), i)

    rev = lambda c: (nc - 1 - c, 0)
    const = lambda c: (0, 0)
    return pl.pallas_call(
        body, name=name, grid=(nc,),
        in_specs=[pl.BlockSpec((Q, DI), rev), pl.BlockSpec((Q, XBC), rev),
                  pl.BlockSpec((Q, DT_PAD), rev),
                  pl.BlockSpec((1, NS, DI), lambda c: (nc - 1 - c, 0, 0)),
                  pl.BlockSpec((1, 128), const), pl.BlockSpec((1, DI), const),
                  pl.BlockSpec((Q, Q), const), pl.BlockSpec((Q, Q), const), pl.BlockSpec((256, DI), const),
                  pl.BlockSpec(memory_space=pl.ANY)],
        out_specs=[pl.BlockSpec((Q, XBC), rev),
                   pl.BlockSpec((Q, DT_PAD), lambda c: (nc - 1 - c, C_DT // DT_PAD)),
                   pl.BlockSpec((1, 128), const), pl.BlockSpec((1, DI), const), pl.BlockSpec((1, 128), const)],
        out_shape=[jax.ShapeDtypeStruct((L, XBC), F32), jax.ShapeDtypeStruct((L, NPROJ), BF16),
                   jax.ShapeDtypeStruct((1, 128), F32), jax.ShapeDtypeStruct((1, DI), F32),
                   jax.ShapeDtypeStruct((1, 128), F32)],
        scratch_shapes=[pltpu.VMEM((NS, DI), F32)],
        input_output_aliases={9: 1},
        compiler_params=_params(("arbitrary",), VMEM_BIG))(dy, xbc, dtp, hs, arow, dsk_x, tri, triu,
                                                          expand, dproj)


def _adam_update(wv, gv, mv, vv):
    c1 = 1.0 - ADAM_B1 ** ADAM_STEP
    c2 = 1.0 - ADAM_B2 ** ADAM_STEP
    mn = ADAM_B1 * mv + (1.0 - ADAM_B1) * gv
    vn = ADAM_B2 * vv + (1.0 - ADAM_B2) * (gv * gv)
    return -ADAM_LR * ((mn / c1) / (jnp.sqrt(vn / c2) + ADAM_EPS) + ADAM_WD * wv), mn, vn


def _adamw(w, g, m, v, *, name, tr=None):
    R = w.shape[0]
    rest = tuple(w.shape[1:])
    if tr is None:
        tr = _pick(R, (256, 128, 64, 32, 16, 8))
    assert R % tr == 0

    def body(w_ref, g_ref, m_ref, v_ref, d_ref, mo_ref, vo_ref):
        d_ref[...], mo_ref[...], vo_ref[...] = _adam_update(w_ref[...], g_ref[...], m_ref[...], v_ref[...])

    zeros = (0,) * len(rest)
    spec = pl.BlockSpec((tr,) + rest, lambda i: (i,) + zeros)
    return pl.pallas_call(body, name=name, grid=(R // tr,), in_specs=[spec] * 4, out_specs=[spec] * 3,
                          out_shape=[jax.ShapeDtypeStruct(w.shape, F32)] * 3,
                          compiler_params=_params(("parallel",)))(w, g, m, v)


def _adamw_small(svrow, g_conv, params, *, name):
    n = len(params)

    def body(*refs):
        sv_ref, gc_ref = refs[0], refs[1]
        ins, outs = refs[2:2 + 3 * n], refs[2 + 3 * n:]
        for p, (key, w, _, _) in enumerate(params):
            w_ref, m_ref, v_ref = ins[3 * p:3 * p + 3]
            g_ref, d_ref, mo_ref, vo_ref = outs[4 * p:4 * p + 4]
            gv = gc_ref[...] if key == "conv_w" else sv_ref[:, SV_OFF[key]:SV_OFF[key] + w.shape[1]]
            g_ref[...] = gv
            d_ref[...], mo_ref[...], vo_ref[...] = _adam_update(w_ref[...], gv, m_ref[...], v_ref[...])

    vm = pl.BlockSpec(memory_space=pltpu.VMEM)
    args = [svrow, g_conv]
    shapes = []
    for _, w, m, v in params:
        args += [w, m, v]
        shapes += [jax.ShapeDtypeStruct(w.shape, F32)] * 4
    res = pl.pallas_call(body, name=name, in_specs=[vm] * len(args), out_specs=[vm] * len(shapes),
                         out_shape=shapes)(*args)
    return {key: tuple(res[4 * p:4 * p + 4]) for p, (key, _, _, _) in enumerate(params)}


def _slab_sum(recv, *, tile, name):
    rows = recv.shape[1]
    assert rows % tile == 0 and tile % 16 == 0

    def body(r_ref, o_ref):
        acc = r_ref[0].astype(F32)
        for j in range(1, N_DEV):
            acc = acc + r_ref[j].astype(F32)
        o_ref[...] = acc

    return pl.pallas_call(body, name=name, grid=(rows // tile,),
                          in_specs=[pl.BlockSpec((N_DEV, tile, D), lambda i: (0, i, 0))],
                          out_specs=pl.BlockSpec((tile, D), lambda i: (i, 0)),
                          out_shape=jax.ShapeDtypeStruct((rows, D), F32),
                          compiler_params=_params(("parallel",)))(recv)


MESH = pl.DeviceIdType.MESH


def _coords():
    return lax.axis_index("x"), lax.axis_index("y"), lax.axis_index("c")


def _peer(k):
    x, y, c = _coords()
    px = 1 - x if k & 4 else x
    py = 1 - y if k & 2 else y
    pc = 1 - c if k & 1 else c
    return (px, py, pc), 4 * px + 2 * py + pc


def _rcopy(src, dst, ssem, rsem, dev):
    return pltpu.make_async_remote_copy(src_ref=src, dst_ref=dst, send_sem=ssem, recv_sem=rsem,
                                        device_id=dev, device_id_type=MESH)


def _exchange_all(src_of, dst_slot, send_sems, recv_sems):
    x, y, c = _coords()
    me = 4 * x + 2 * y + c
    sent = []
    for k in range(1, N_DEV):
        dev, pidx = _peer(k)
        cp = _rcopy(src_of(pidx), dst_slot(me), send_sems.at[k - 1], recv_sems.at[k - 1], dev)
        cp.start()
        sent.append(cp)
    for k in range(1, N_DEV):
        dev, pidx = _peer(k)
        _rcopy(src_of(pidx), dst_slot(pidx), send_sems.at[k - 1], recv_sems.at[k - 1], dev).wait_recv()
    for cp in sent:
        cp.wait_send()


def _rows_of_slots(buf, nslots):
    rows = lax.broadcasted_iota(jnp.int32, (8, buf.shape[-1]), 0)
    out = jnp.zeros((8, buf.shape[-1]), F32)
    for j in range(nslots):
        out = out + jnp.where(rows == j, buf[j], 0.0)
    return out


def _exchange_start(src_of, dst_slot, send_sems, recv_sems):
    x, y, c = _coords()
    me = 4 * x + 2 * y + c
    sent = []
    for k in range(1, N_DEV):
        dev, pidx = _peer(k)
        cp = _rcopy(src_of(pidx), dst_slot(me), send_sems.at[k - 1], recv_sems.at[k - 1], dev)
        cp.start()
        sent.append(cp)
    return sent


def _exchange_finish(sent, src_of, dst_slot, send_sems, recv_sems):
    for k in range(1, N_DEV):
        dev, pidx = _peer(k)
        _rcopy(src_of(pidx), dst_slot(pidx), send_sems.at[k - 1], recv_sems.at[k - 1], dev).wait_recv()
    for cp in sent:
        cp.wait_send()


def _ada_gather(c, w_ada, b_r, slab, *, name):
    wloc = w_ada.shape[1]

    def body(c_ref, w_ref, b_ref, x_ref, mod_ref, call_ref, out_ref,
             csrc, cbuf, psrc, pbuf, s1, r1, s2, r2, send_sems, recv_sems, local_sem):
        x, y, cc = _coords()
        me_i = 4 * x + 2 * y + cc
        me, sibling = (x, y, cc), (x, y, 1 - cc)
        chips = [(1 - x, y), (x, 1 - y), (1 - x, 1 - y)]

        def slot(px, py, pc):
            return out_ref.at[4 * px + 2 * py + pc]

        def copy(k, block, to, src=None):
            return _rcopy(slot(*block) if src is None else src, slot(*block), send_sems.at[k], recv_sems.at[k], to)

        csrc[...] = jnp.broadcast_to(c_ref[...], (8, D))
        cbuf[me_i] = csrc[...]
        c_of, c_slot = (lambda p: csrc), (lambda s: cbuf.at[s])
        sent1 = _exchange_start(c_of, c_slot, s1, r1)

        mine = pltpu.make_async_copy(x_ref, slot(*me), local_sem)
        mine.start()
        first = [copy(0, me, sibling, src=x_ref)]
        first += [copy(1 + j, me, (*chip, cc), src=x_ref) for j, chip in enumerate(chips)]
        for cp in first:
            cp.start()

        _exchange_finish(sent1, c_of, c_slot, s1, r1)
        call = _rows_of_slots(cbuf, N_DEV)
        call_ref[...] = call
        prod = _dot_hi(_silu(call), w_ref[...])
        for b in range(N_DEV):
            psrc[b] = jnp.broadcast_to(prod[b:b + 1, :], (8, wloc))
        pbuf[me_i] = psrc[me_i]
        p_of, p_slot = (lambda p: psrc.at[p]), (lambda s: pbuf.at[s])
        sent2 = _exchange_start(p_of, p_slot, s2, r2)

        passed = [copy(4 + j, (*chip, cc), sibling) for j, chip in enumerate(chips)]
        for j, chip in enumerate(chips):
            copy(1 + j, (*chip, cc), me).wait_recv()
            passed[j].start()
        copy(0, sibling, me).wait_recv()
        for j, chip in enumerate(chips):
            copy(4 + j, (*chip, 1 - cc), me).wait_recv()

        _exchange_finish(sent2, p_of, p_slot, s2, r2)
        mod_ref[...] = _rows_of_slots(pbuf, N_DEV) + b_ref[...]
        for cp in first + passed:
            cp.wait_send()
        mine.wait()

    vm = pl.BlockSpec(memory_space=pltpu.VMEM)
    anyspec = pl.BlockSpec(memory_space=pl.ANY)
    return pl.pallas_call(
        body, name=name, in_specs=[vm, vm, vm, anyspec], out_specs=[vm, vm, anyspec],
        out_shape=[jax.ShapeDtypeStruct((N_DEV, wloc), F32), jax.ShapeDtypeStruct((N_DEV, D), F32),
                   jax.ShapeDtypeStruct((N_DEV,) + slab.shape, slab.dtype)],
        scratch_shapes=[pltpu.VMEM((8, D), F32), pltpu.VMEM((N_DEV, 8, D), F32),
                        pltpu.VMEM((N_DEV, 8, wloc), F32), pltpu.VMEM((N_DEV, 8, wloc), F32),
                        pltpu.SemaphoreType.DMA((N_DEV - 1,)), pltpu.SemaphoreType.DMA((N_DEV - 1,)),
                        pltpu.SemaphoreType.DMA((N_DEV - 1,)), pltpu.SemaphoreType.DMA((N_DEV - 1,)),
                        pltpu.SemaphoreType.DMA((7,)), pltpu.SemaphoreType.DMA((7,)), pltpu.SemaphoreType.DMA],
        compiler_params=pltpu.CompilerParams(vmem_limit_bytes=VMEM_BIG))(c, w_ada, b_r, slab)


_HBM =pl.BlockSpec(memory_space=pltpu.HBM)
_SEM = pl.BlockSpec(memory_space=pltpu.SEMAPHORE)
_EFFECT = pltpu.SideEffectType.DATAFLOW_SIDE_EFFECTING


def _xchg_src(src_ref, pidx, per_peer):
    return src_ref.at[pidx] if per_peer else src_ref


def _xchg_start(src, *, per_peer, name):
    rows = src.shape[-2]
    land_shape = (N_DEV, rows, D)

    def body(src_ref, land_ref, send_sems, recv_sems, src_thru, land_thru, token):
        del src_thru, land_thru
        x, y, c = _coords()
        me = 4 * x + 2 * y + c
        for k in range(1, N_DEV):
            dev, pidx = _peer(k)
            _rcopy(_xchg_src(src_ref, pidx, per_peer), land_ref.at[me], send_sems.at[k - 1],
                   recv_sems.at[k - 1], dev).start()
        token[...] = jnp.zeros_like(token)

    return pl.pallas_call(
        body, name=name,
        out_shape=(pltpu.SemaphoreType.DMA((N_DEV - 1,)), pltpu.SemaphoreType.DMA((N_DEV - 1,)),
                   pltpu.HBM(src.shape, src.dtype), pltpu.HBM(land_shape, src.dtype),
                   jax.ShapeDtypeStruct((8, 128), F32)),
        in_specs=(_HBM, _HBM),
        out_specs=(_SEM, _SEM, _HBM, _HBM, pl.BlockSpec(memory_space=pltpu.VMEM)),
        input_output_aliases={0: 2, 1: 3},
        compiler_params=pltpu.CompilerParams(has_side_effects=_EFFECT),
    )(pltpu.with_memory_space_constraint(src, pltpu.HBM),
      pltpu.with_memory_space_constraint(lax.empty(land_shape, src.dtype), pltpu.HBM))


def _xchg_wait(started, after, *, per_peer, name):
    send_sems, recv_sems, src_thru, land_thru, _ = started

    def body(src_ref, land_ref, send_sems, recv_sems, after_ref, src_dead, got_ref):
        del after_ref, src_dead, got_ref
        for k in range(1, N_DEV):
            dev, pidx = _peer(k)
            cp = _rcopy(_xchg_src(src_ref, pidx, per_peer), land_ref.at[pidx], send_sems.at[k - 1],
                        recv_sems.at[k - 1], dev)
            cp.wait_send()
            cp.wait_recv()

    return pl.pallas_call(
        body, name=name,
        out_shape=(pltpu.HBM(src_thru.shape, src_thru.dtype), pltpu.HBM(land_thru.shape, land_thru.dtype)),
        in_specs=(_HBM, _HBM, _SEM, _SEM, pl.BlockSpec(memory_space=pl.ANY)),
        out_specs=(_HBM, _HBM),
        input_output_aliases={0: 0, 1: 1},
        compiler_params=pltpu.CompilerParams(has_side_effects=_EFFECT),
    )(src_thru, land_thru, send_sems, recv_sems, after)


def _dep(token):
    return (token, (8, 128), lambda i, j, k: (0, 0))


def _small_allsum(sv, *, name):
    def body(sv_ref, all_ref, sum_ref, send_sems, recv_sems):
        x, y, c = _coords()
        me = 4 * x + 2 * y + c
        all_ref[me] = sv_ref[...]
        _exchange_all(lambda p: sv_ref, lambda s: all_ref.at[s], send_sems, recv_sems)
        acc = all_ref[0]
        for j in range(1, N_DEV):
            acc = acc + all_ref[j]
        sum_ref[...] = acc

    vm = pl.BlockSpec(memory_space=pltpu.VMEM)
    return pl.pallas_call(
        body, name=name, in_specs=[vm], out_specs=[vm, vm],
        out_shape=[jax.ShapeDtypeStruct((N_DEV, SV_ROWS, 128), F32), jax.ShapeDtypeStruct((SV_ROWS, 128), F32)],
        scratch_shapes=[pltpu.SemaphoreType.DMA((7,)), pltpu.SemaphoreType.DMA((7,))],
    )(sv)


def _ada_bwd(call, dmod_loc, *, name):
    wloc = dmod_loc.shape[1]

    def body(c_ref, d_ref, o_ref):
        o_ref[...] = _dot_hi(_silu(c_ref[...]), d_ref[...], TN)

    vm = pl.BlockSpec(memory_space=pltpu.VMEM)
    return pl.pallas_call(body, name=name, in_specs=[vm, vm], out_specs=vm,
                          out_shape=jax.ShapeDtypeStruct((D, wloc), F32),
                          compiler_params=pltpu.CompilerParams(vmem_limit_bytes=VMEM_BIG))(call, dmod_loc)


def _pad_rows(a, rows):
    return jnp.pad(a, ((0, rows - a.shape[0]), (0, 0)))


IN_SHIFT = tuple((IN_ROWS * j) % 16 for j in range(N_DEV))
IN_BASE = tuple(IN_ROWS * j - IN_SHIFT[j] for j in range(N_DEV))
IN_SEGMENTS = ((2048, XBC, C_XBC), (5152, 1024, C_POOL), (0, 2048, C_Z), (6176, 2048, C_GATE), (5120, 32, C_DT))


def _global_pieces(gs):
    pieces = []
    for j in range(N_DEV):
        lo, hi = 0, IN_ROWS_P
        if j > 0 and IN_BASE[j - 1] + IN_ROWS_P > IN_BASE[j]:
            pieces.append((IN_BASE[j], 16, gs[j - 1, IN_ROWS_P - 16:IN_ROWS_P] + gs[j, 0:16]))
            lo = 16
        if j + 1 < N_DEV and IN_BASE[j] + IN_ROWS_P > IN_BASE[j + 1]:
            hi = IN_ROWS_P - 16
        pieces.append((IN_BASE[j] + lo, hi - lo, gs[j, lo:hi]))
    return pieces


def _reorder_in_rows(gs):
    pieces = _global_pieces(gs)
    parts = []
    for lo, n, _ in IN_SEGMENTS:
        for p0, pn, arr in pieces:
            a, b = max(lo, p0), min(lo + n, p0 + pn)
            if a < b:
                parts.append(arr[a - p0:b - p0])
    parts.append(jnp.zeros((DT_PAD - 32, D), gs.dtype))
    return jnp.concatenate(parts, axis=0)


def _restore_in_shards(d):
    slabs = []
    for j in range(N_DEV):
        parts = []
        r, end = IN_BASE[j], IN_BASE[j] + IN_ROWS_P
        while r < end:
            lo, n, new = next(s for s in IN_SEGMENTS if s[0] <= r < s[0] + s[1])
            e = min(end, lo + n)
            parts.append(d[new + r - lo:new + e - lo])
            r = e
        slabs.append(jnp.concatenate(parts, axis=0))
    return jnp.stack(slabs, axis=0)


def _pack_sv(parts):
    flat = []
    for n, size in SV_PARTS:
        v = parts[n].reshape(-1).astype(F32)
        flat.append(jnp.pad(v, (0, size - v.shape[0])))
    v = jnp.concatenate(flat)
    return jnp.pad(v, (0, SV_ROWS * 128 - v.shape[0])).reshape(SV_ROWS, 128)


def _sv_get(flat, n, size):
    return flat[SV_OFF[n]:SV_OFF[n] + size]


def kernel(x, c, w_ada, b_ada, norm_mix_w, w_in, conv_w, conv_b, dt_bias, a_log, d_skip, ssd_norm_w, w_branch_ssd, pool_w, pool_scale, w_branch_pool, w_out, norm_mlp_w, w_up, w_down, norm_final_w, loss_target, m_w_ada, m_b_ada, m_norm_mix_w, m_w_in, m_conv_w, m_conv_b, m_dt_bias, m_a_log, m_d_skip, m_ssd_norm_w, m_w_branch_ssd, m_pool_w, m_pool_scale, m_w_branch_pool, m_w_out, m_norm_mlp_w, m_w_up, m_w_down, m_norm_final_w, v_w_ada, v_b_ada, v_norm_mix_w, v_w_in, v_conv_w, v_conv_b, v_dt_bias, v_a_log, v_d_skip, v_ssd_norm_w, v_w_branch_ssd, v_pool_w, v_pool_scale, v_w_branch_pool, v_w_out, v_norm_mlp_w, v_w_up, v_w_down, v_norm_final_w):
    xs_ = x[0]
    tgt = loss_target[0]
    L = xs_.shape[0]
    me = 4 * lax.axis_index("x") + 2 * lax.axis_index("y") + lax.axis_index("c")
    wloc = w_ada.shape[2]

    conv_bits = lax.bitcast_convert_type(conv_w[0], SLAB_DT).reshape(3, D)
    in_shift = (IN_ROWS * me) % 16
    slab_in = lax.dynamic_update_slice(jnp.zeros((IN_ROWS_P, D), SLAB_DT), w_in[0].T.astype(SLAB_DT),
                                       (in_shift, 0))
    slab_in = jnp.concatenate([slab_in, _pad_rows(conv_bits, CONV_ROWS)], axis=0)
    slab_rest = jnp.concatenate([
        w_branch_ssd[0].astype(SLAB_DT),
        pool_w[0].reshape(32, D).astype(SLAB_DT),
        w_branch_pool[0].astype(SLAB_DT),
        w_out[0].astype(SLAB_DT),
        w_up[0].T.astype(SLAB_DT),
        w_down[0].astype(SLAB_DT)], axis=0)
    mod_p, c_all, gs_in = _ada_gather(c, w_ada[0], b_ada.reshape(N_DEV, wloc), slab_in,
                                      name="ada_gather_w_in")
    mod = mod_p.reshape(6, D)
    shift_m, scale_m, gate_m, shift_f, scale_f, gate_f = [mod[i:i + 1] for i in range(6)]
    slab_rest, gs_in = lax.optimization_barrier((slab_rest, gs_in))
    rest_started = _xchg_start(slab_rest, per_peer=False, name="gather_rest_start")
    gather_token = rest_started[4]

    w_in_t = _reorder_in_rows(gs_in)
    conv_full = lax.bitcast_convert_type(
        gs_in[:, IN_ROWS_P:IN_ROWS_P + 3].reshape(N_DEV, 4, XBC // N_DEV, 2), F32)
    conv_full = conv_full.transpose(1, 0, 2).reshape(4, XBC)

    dtb = jnp.pad(dt_bias, ((0, 0), (0, 128 - NH)))
    arow = jnp.pad(-jnp.exp(a_log), ((0, 0), (0, 128 - NH)))
    dsk_x = jnp.repeat(d_skip, HP, axis=1)

    tm = _pick(L, (1024, 512, 256, 128))
    tm2 = _pick(L, (2048, 1024, 512, 256, 128))
    tkl = _pick(L, (4096, 2048, 1024, 512, 256, 128))
    tkl2 = _pick(L, (2048, 1024, 512, 256, 128))

    tmh = _pick(L, (512, 256, 128))
    zcol = C_Z // DI
    gcol = C_GATE // (2 * D)

    def whole_rows(w):
        return lambda t: ((L, w), BF16, (t, w), lambda i, j, k: (i, 0))

    def norm1_pro(x_ref, ex, outs, j):
        @pl.when(j == 0)
        def _():
            xv = x_ref[...]
            r = lax.rsqrt(jnp.mean(xv * xv, axis=-1, keepdims=True) + EPS)
            outs[1][...] = (xv * r * ex[0][...] * (1.0 + ex[1][...]) + ex[2][...]).astype(outs[1].dtype)

        return outs[1][...]

    def proj_ep(acc, ex, outs):
        outs[0][...] = acc

        @pl.when(pl.program_id(1) == NPROJ // 768 - 1)
        def _():
            pre = acc[:, 768 - DT_PAD:768 - DT_PAD + 128] + ex[3][...]
            outs[2][...] = jnp.concatenate([_softplus(pre), _sigmoid(pre)], axis=1)

    proj, h1, dtp = _mm(
        xs_, w_in_t, "nt", name="in_proj", tm=tm2, tn=768, tk=D,
        extras=[(norm_mix_w, *_vecs()), (scale_m, *_vecs()), (shift_m, *_vecs()), (dtb, *_vecs(128)),
                _dep(gather_token)],
        outs=[F32, whole_rows(D)(tm2), ((L, DT_PAD), F32, (tm2, DT_PAD), lambda i, j, k: (i, 0))],
        prologue=norm1_pro, epilogue=proj_ep)
    xbc_raw = proj
    y_ssm, hs, xbc = _ssd_fwd(xbc_raw, dtp, conv_full, conv_b, arow, dsk_x, name="ssd_fwd")

    slab_rest, gs = _xchg_wait(rest_started, y_ssm, per_peer=False, name="gather_rest_wait")
    gs = lax.dynamic_update_slice(gs, slab_rest[None], (me, 0, 0))

    def part(n, rows):
        return gs[:, REST_OFF[n]:REST_OFF[n] + rows]

    w_bssd = part("bssd", 256).reshape(DI, D)
    w_pool = part("pool", 32).reshape(N_DEV, 4, 32, PGW).transpose(1, 0, 2, 3).reshape(POOL_W, PGW)
    w_bpool = part("bpool", 128).reshape(POOL_W, D)
    w_o = part("out", 128).reshape(D, D)
    w_up_t = part("up", 512).reshape(DFF, D)
    w_dn = part("down", 512).reshape(DFF, D)

    def gnorm_pro(y_ref, ex, outs, j):
        z_ref, w_ref = ex
        yg = y_ref[...].astype(F32) * _silu(z_ref[...].astype(F32))
        segs = []
        for k in range(NG):
            sl = slice(k * GW, (k + 1) * GW)
            seg = yg[:, sl]
            r = lax.rsqrt(jnp.mean(seg * seg, axis=-1, keepdims=True) + EPS)
            segs.append((seg * r * w_ref[:, sl]).astype(BF16))
        yn_v = jnp.concatenate(segs, axis=1)
        outs[1][...] = yn_v
        return yn_v

    y_ssd, yn = _mm(y_ssm, w_bssd, "nn", name="branch_ssd", tm=tmh, tn=D, tk=DI,
                    extras=[(proj, *_rows(tmh, DI, zcol)), (ssd_norm_w, *_vecs(DI))],
                    outs=[BF16, whole_rows(DI)(tmh)], prologue=gnorm_pro)
    pooled = _pool_fwd(proj, name="pool_fwd")
    wp_spec = ((POOL_W, PGW), lambda i, j, k: (0, 0))

    def pool_pro(a_ref, ex, outs, j):
        wp_ref, s_ref = ex
        segs = []
        for g in range(4):
            sl = slice(g * PGW, (g + 1) * PGW)
            p = _dot(a_ref[:, sl], wp_ref[sl, :], NN)
            outs[1][:, sl] = p.astype(BF16)
            segs.append((p * s_ref[:, sl]).astype(BF16))
        yp1_v = jnp.concatenate(segs, axis=1)
        outs[2][...] = yp1_v
        return yp1_v

    y_pool, yp0, yp1 = _mm(pooled, w_bpool, "nn", name="branch_pool", tm=tm, tn=D, tk=D,
                           extras=[(w_pool, *wp_spec), (pool_scale, *_vecs())],
                           outs=[BF16, whole_rows(D)(tm), whole_rows(D)(tm)], prologue=pool_pro)

    def merge_pro(a_ref, ex, outs, j):
        s = _sigmoid(ex[1][...].astype(F32))
        mv = (s[:, :D] * a_ref[...].astype(F32) + s[:, D:] * ex[0][...].astype(F32)).astype(BF16)
        outs[3][...] = mv
        return mv

    mix, x1, h2, m = _mm(y_ssd, w_o, "nn", name="out_proj", tm=tmh, tn=D, tk=D,
                         extras=[(y_pool, *_rows(tmh)), (proj, *_rows(tmh, 2 * D, gcol)),
                                 (xs_, *_rows(tmh)), (gate_m, *_vecs()), (norm_mlp_w, *_vecs()),
                                 (scale_f, *_vecs()), (shift_f, *_vecs())],
                         outs=[BF16, F32, BF16, whole_rows(D)(tmh)], prologue=merge_pro,
                         epilogue=lambda acc, ex, outs: _ep_resid_norm(acc, ex[2:], outs[:3]))

    def relu2(acc, ex, outs):
        r = jnp.maximum(acc, 0.0)
        outs[0][...] = acc.astype(BF16)
        outs[1][...] = (r * r).astype(BF16)

    up, act = _mm(h2, w_up_t, "nt", name="mlp_up", outs=[BF16, BF16], tm=tm2, tn=1024, tk=D, epilogue=relu2)

    dx2, ddown, loss_p, dnwf, dgate_f = _mm(
        act, w_dn, "nn", name="mlp_down", tm=tmh, tn=D, tk=DFF,
        extras=[(x1, *_rows(tmh)), (tgt, *_rows(tmh)), (gate_f, *_vecs()), (norm_final_w.reshape(1, D), *_vecs())],
        outs=[F32, BF16, _sum_out(128), _sum_out(), _sum_out()], epilogue=_ep_final)

    def drelu2(acc, ex, outs):
        outs[0][...] = (acc * (2.0 * jnp.maximum(ex[0][...].astype(F32), 0.0))).astype(BF16)

    def dep_last(ep):
        return lambda acc, ex, outs: ep(acc, ex[:-1], outs)

    dup = _mm(ddown, w_dn, "nt", name="mlp_down_dx", outs=[BF16], tm=tm2, tn=1024, tk=D,
              extras=[(up, (tm2, 1024), lambda i, j, k: (i, j))], epilogue=drelu2)
    g_dn = _mm(act, ddown, "tn", name="mlp_down_dw", outs=[SLAB_DT], tm=1024, tn=D, tk=tkl)
    g_up_t = _mm(dup, h2, "tn", name="mlp_up_dw", outs=[SLAB_DT], tm=1024, tn=D, tk=tkl)
    gslab_mlp = jnp.concatenate([g_up_t.reshape(N_DEV, 512, D), g_dn.reshape(N_DEV, 512, D)], axis=1)
    mlp_started = _xchg_start(gslab_mlp, per_peer=True, name="scatter_mlp_start")
    dx1, p2, q2, dmix, dgate_m = _mm(
        dup, w_up_t, "nn", name="mlp_up_dx", tm=tmh, tn=D, tk=DFF,
        extras=[(x1, *_rows(tmh)), (dx2, *_rows(tmh)), (norm_mlp_w, *_vecs()), (scale_f, *_vecs()),
                (mix, *_rows(tmh)), (gate_m, *_vecs()), _dep(mlp_started[4])],
        outs=[F32, _sum_out(), _sum_out(), BF16, _sum_out()], epilogue=dep_last(_ep_norm_bwd))
    gcol = C_GATE // (2 * D)
    dy_ssd, dy_pool, dproj = _mm(
        dmix, w_o, "nt", name="out_proj_dx", tm=tmh, tn=D, tk=D,
        extras=[(y_ssd, *_rows(tmh)), (y_pool, *_rows(tmh)), (proj, *_rows(tmh, 2 * D, gcol))],
        outs=[BF16, BF16, ((L, NPROJ), BF16, *_rows(tmh, 2 * D, gcol))], epilogue=_ep_merge_bwd)
    g_o = _mm(m, dmix, "tn", name="out_proj_dw", outs=[SLAB_DT], tm=D, tn=D, tk=tkl)
    zcol = C_Z // DI
    dy_ssm, dproj, d_snw = _mm(
        dy_ssd, w_bssd, "nt", name="branch_ssd_dx", tm=tmh, tn=DI, tk=D,
        extras=[(y_ssm, *_rows(tmh, DI)), (proj, *_rows(tmh, DI, zcol)), (ssd_norm_w, *_vecs(DI)),
                (dproj, None, None)],
        outs=[F32, ((L, NPROJ), BF16, *_rows(tmh, DI, zcol)), _sum_out(DI)],
        epilogue=_ep_gated_norm_bwd, aliases={3: 1})
    g_bssd = _mm(yn, dy_ssd, "tn", name="branch_ssd_dw", outs=[SLAB_DT], tm=1024, tn=D, tk=tkl)
    dxbc, dproj, d_a, d_dx, d_dtb = _ssd_bwd(dy_ssm, xbc, dtp, hs, arow, dsk_x, dproj, name="ssd_bwd")
    dproj, d_cw, d_cb = _conv_bwd(xbc_raw, dxbc, conv_full, conv_b, dproj, name="conv_bwd")
    def pool_bwd_ep(acc, ex, outs):
        y_ref, s_ref, wp_ref = ex
        o_ref, ds_ref, dpool_ref = outs
        dyp0_v = (acc * s_ref[...]).astype(BF16)
        o_ref[...] = dyp0_v
        _acc_out(ds_ref, _colsum(acc * y_ref[...].astype(F32)), _row_step())
        for g in range(4):
            sl = slice(g * PGW, (g + 1) * PGW)
            dpool_ref[:, sl] = _dot(dyp0_v[:, sl], wp_ref[sl, :], NT)

    dyp0, d_ps, dpooled = _mm(dy_pool, w_bpool, "nt", name="branch_pool_dx", tm=tm, tn=D, tk=D,
                              extras=[(yp0, *_rows(tm)), (pool_scale, *_vecs()), (w_pool, *wp_spec)],
                              outs=[BF16, _sum_out(), F32], epilogue=pool_bwd_ep)
    g_bpool = _mm(yp1, dy_pool, "tn", name="branch_pool_dw", outs=[SLAB_DT], tm=D, tn=D, tk=tkl)
    g_pool = _mm_pool_tn(pooled, dyp0, name="pool_mix_dw", tk=tkl)
    gslab_mix = jnp.concatenate([
        g_bssd.reshape(N_DEV, 256, D),
        g_pool.reshape(4, N_DEV, 32, PGW).transpose(1, 0, 2, 3).reshape(N_DEV, 32, D).astype(SLAB_DT),
        g_bpool.reshape(N_DEV, 128, D),
        g_o.reshape(N_DEV, 128, D)], axis=1)
    mix_started = _xchg_start(gslab_mix, per_peer=True, name="scatter_mix_start")
    dproj = _pool_bwd(dpooled, dproj, name="pool_bwd")
    g_in_t = _mm(dproj, h1, "tn", name="in_proj_dw", outs=[SLAB_DT], tm=1408, tn=D, tk=tkl2,
                 extras=[_dep(mix_started[4])])
    gslab_in = _restore_in_shards(g_in_t)
    in_started = _xchg_start(gslab_in, per_peer=True, name="scatter_in_start")
    grad_x, p1, q1 = _mm(
        dproj, w_in_t, "nn", name="in_proj_dx", tm=tmh, tn=D, tk=2816,
        extras=[(xs_, *_rows(tmh)), (dx1, *_rows(tmh)), (norm_mix_w, *_vecs()), (scale_m, *_vecs()),
                _dep(in_started[4])],
        outs=[F32, _sum_out(), _sum_out()], epilogue=dep_last(_ep_norm_bwd))

    def landed(started, after, tile, name):
        src, land = _xchg_wait(started, after, per_peer=True, name=name + "_wait")
        own = lax.dynamic_slice_in_dim(src, me, 1, axis=0)
        return _slab_sum(lax.dynamic_update_slice(land, own, (me, 0, 0)), tile=tile, name=name + "_sum")

    gsum_mlp = landed(mlp_started, grad_x, 256, "scatter_mlp")
    gsum_mix = landed(mix_started, grad_x, 272, "scatter_mix")
    gsum_in = landed(in_started, grad_x, 208, "scatter_in")

    dmod = jnp.concatenate([q1, p1 * norm_mix_w, dgate_m, q2, p2 * norm_mlp_w, dgate_f], axis=1)
    d_alog = d_a[:, :NH] * (-jnp.exp(a_log))
    sv = _pack_sv({
        "b_ada": dmod, "norm_mix_w": p1 * (1.0 + scale_m), "conv_b": d_cb, "dt_bias": d_dtb[:, :NH],
        "a_log": d_alog, "d_skip": d_dx.reshape(NH, HP).sum(axis=1), "ssd_norm_w": d_snw,
        "pool_scale": d_ps, "norm_mlp_w": p2 * (1.0 + scale_f), "norm_final_w": dnwf, "conv_w": d_cw,
        "loss": loss_p[:, :1]})
    sv_all, sv_sum = _small_allsum(sv, name="small_allsum")
    flat = sv_sum.reshape(-1)
    loss = flat[SV_OFF["loss"]]
    dmod_all = sv_all.reshape(N_DEV, SV_ROWS * 128)[:, :6 * D]
    g_w_ada = _ada_bwd(c_all, lax.dynamic_slice_in_dim(dmod_all, me * wloc, wloc, axis=1), name="ada_bwd")

    g_conv_w = lax.dynamic_slice_in_dim(_sv_get(flat, "conv_w", 4 * XBC).reshape(4, XBC),
                                        me * (XBC // N_DEV), XBC // N_DEV, axis=1)
    small = [("b_ada", b_ada, m_b_ada, v_b_ada), ("norm_mix_w", norm_mix_w, m_norm_mix_w, v_norm_mix_w),
             ("conv_b", conv_b, m_conv_b, v_conv_b), ("dt_bias", dt_bias, m_dt_bias, v_dt_bias),
             ("a_log", a_log, m_a_log, v_a_log), ("d_skip", d_skip, m_d_skip, v_d_skip),
             ("ssd_norm_w", ssd_norm_w, m_ssd_norm_w, v_ssd_norm_w),
             ("pool_scale", pool_scale, m_pool_scale, v_pool_scale),
             ("norm_mlp_w", norm_mlp_w, m_norm_mlp_w, v_norm_mlp_w),
             ("norm_final_w", norm_final_w[None], m_norm_final_w[None], v_norm_final_w[None]),
             ("conv_w", conv_w[0], m_conv_w[0], v_conv_w[0])]
    small_out = _adamw_small(sv_sum.reshape(1, SV_ROWS * 128), g_conv_w, small, name="adamw_small")
    small_out["norm_final_w"] = tuple(a[0] for a in small_out["norm_final_w"])
    small_out["conv_w"] = tuple(a[None] for a in small_out["conv_w"])

    def gpart(n, rows_):
        return gsum_mix[MIX_OFF[n]:MIX_OFF[n] + rows_]

    def lin(a):
        return a[0].T.reshape(IN_ROWS * 8, 128)

    g_lin = lax.dynamic_slice_in_dim(gsum_in, in_shift, IN_ROWS, axis=0).reshape(IN_ROWS * 8, 128)
    dlt, mn, vn = _adamw(lin(w_in), g_lin, lin(m_w_in), lin(v_w_in), name="adamw_w_in", tr=IN_ROWS * 2)
    big_in = tuple(a.reshape(IN_ROWS, D).T[None] for a in (g_lin, dlt, mn, vn))

    big = {
        "w_ada": (w_ada, m_w_ada, v_w_ada, g_w_ada, (D, wloc)),
        "w_branch_ssd": (w_branch_ssd, m_w_branch_ssd, v_w_branch_ssd, gpart("bssd", 256), (256, D)),
        "pool_w": (pool_w, m_pool_w, v_pool_w, gpart("pool", 32).reshape(128, PGW), (128, PGW)),
        "w_branch_pool": (w_branch_pool, m_w_branch_pool, v_w_branch_pool, gpart("bpool", 128), (128, D)),
        "w_out": (w_out, m_w_out, v_w_out, gpart("out", 128), (128, D)),
        "w_up": (w_up, m_w_up, v_w_up, gsum_mlp[:512].T, (D, 512)),
        "w_down": (w_down, m_w_down, v_w_down, gsum_mlp[512:], (512, D)),
    }
    big_out = {}
    for n, (w, mm_, vv, g, shp2) in big.items():
        dlt, mn, vn = _adamw(w.reshape(shp2), g, mm_.reshape(shp2), vv.reshape(shp2), name="adamw_" + n)
        big_out[n] = (g.reshape(w.shape), dlt.reshape(w.shape), mn.reshape(w.shape), vn.reshape(w.shape))

    order = ["w_ada", "b_ada", "norm_mix_w", "w_in", "conv_w", "conv_b", "dt_bias", "a_log", "d_skip",
             "ssd_norm_w", "w_branch_ssd", "pool_w", "pool_scale", "w_branch_pool", "w_out", "norm_mlp_w",
             "w_up", "w_down", "norm_final_w"]
    big_out["w_in"] = big_in
    res = {**small_out, **big_out}
    outs = [loss, grad_x.reshape(x.shape)]
    for k in range(4):
        outs += [res[n][k] for n in order]
    return tuple(outs)
```

```python
import functools

import numpy as np
import jax
import jax.numpy as jnp
from jax import lax
from jax.experimental import pallas as pl
from jax.experimental.pallas import tpu as pltpu

F32 = jnp.float32
BF16 = jnp.bfloat16
SLAB_DT = jnp.bfloat16
_MXU_DTYPE = jnp.bfloat16

N_DEV = 8
D = 1024
DI = 2048
NH = 32
HP = 64
NG = 4
NS = 128
Q = 128
XBC = DI + 2 * NG * NS
DFF = 4096
N_IN = 8224
EPS = 1e-5
POOL_W = 1024
PGW = 256

C_XBC, C_POOL, C_Z, C_GATE, C_DT = 0, 3072, 4096, 6144, 8192
DT_PAD = 256
NPROJ = C_DT + DT_PAD

IN_ROWS = N_IN // N_DEV
IN_ROWS_P = 1040
CONV_ROWS = 16
REST_PARTS = (("bssd", 256), ("pool", 32), ("bpool", 128), ("out", 128), ("up", 512), ("down", 512))
REST_OFF = {}
_o = 0
for _n, _r in REST_PARTS:
    REST_OFF[_n] = _o
    _o += _r
REST_ROWS = _o
MIX_PARTS = (("bssd", 256), ("pool", 32), ("bpool", 128), ("out", 128))
MIX_OFF = {}
_o = 0
for _n, _r in MIX_PARTS:
    MIX_OFF[_n] = _o
    _o += _r
MIX_ROWS = _o

SV_PARTS = (("b_ada", 6144), ("norm_mix_w", 1024), ("conv_b", 3072), ("dt_bias", 128), ("a_log", 128),
            ("d_skip", 128), ("ssd_norm_w", 2048), ("pool_scale", 1024), ("norm_mlp_w", 1024),
            ("norm_final_w", 1024), ("conv_w", 4 * XBC), ("loss", 128))
SV_OFF = {}
_o = 0
for _n, _r in SV_PARTS:
    SV_OFF[_n] = _o
    _o += _r
SV_ROWS = 224
assert _o <= SV_ROWS * 128

ADAM_LR, ADAM_B1, ADAM_B2, ADAM_EPS, ADAM_WD, ADAM_STEP = 0.001, 0.9, 0.999, 1e-08, 0.01, 10

VMEM_BIG = 56 * 1024 * 1024
NEG = -1e30

NN = ((1,), (0,))
NT = ((1,), (1,))
TN = ((0,), (0,))


def _dot(a, b, dims=NN):
    return lax.dot_general(a.astype(_MXU_DTYPE), b.astype(_MXU_DTYPE), (dims, ((), ())),
                           preferred_element_type=F32)


def _dot_hi(a, b, dims=NN):
    return lax.dot_general(a.astype(F32), b.astype(F32), (dims, ((), ())),
                           precision=lax.Precision.HIGHEST, preferred_element_type=F32)


def _pick(n, cands):
    for c in cands:
        if n % c == 0:
            return c
    return n


def _sigmoid(x):
    return 1.0 / (1.0 + jnp.exp(-x))


def _silu(x):
    return x * _sigmoid(x)


def _dsilu(x):
    s = _sigmoid(x)
    return s * (1.0 + x * (1.0 - s))


def _softplus(x):
    return jnp.maximum(x, 0.0) + jnp.log(1.0 + jnp.exp(-jnp.abs(x)))


def _params(sem, vmem=None):
    return pltpu.CompilerParams(dimension_semantics=sem, vmem_limit_bytes=vmem)


def _row_step():
    return pl.program_id(0)


def _mm(a, b, mode, *, name, outs, tm, tn, tk, extras=(), epilogue=None, aliases=None, prologue=None):
    if mode == "tn":
        K, M = a.shape
        N = b.shape[1]
        a_spec = pl.BlockSpec((tk, tm), lambda i, j, k: (k, i))
        b_spec = pl.BlockSpec((tk, tn), lambda i, j, k: (k, j))
        dims = TN
    else:
        M = a.shape[0]
        K = b.shape[0] if mode == "nn" else b.shape[1]
        if prologue is None:
            assert a.shape[1] == K
            a_spec = pl.BlockSpec((tm, tk), lambda i, j, k: (i, k))
        else:
            assert tk == K
            a_spec = pl.BlockSpec((tm, a.shape[1]), lambda i, j, k: (i, 0))
        if mode == "nn":
            N = b.shape[1]
            b_spec = pl.BlockSpec((tk, tn), lambda i, j, k: (k, j))
            dims = NN
        else:
            N = b.shape[0]
            b_spec = pl.BlockSpec((tn, tk), lambda i, j, k: (j, k))
            dims = NT
    assert M % tm == 0 and N % tn == 0 and K % tk == 0, (name, M, N, K, tm, tn, tk)
    nk = K // tk
    ne, no = len(extras), len(outs)
    if epilogue is None:
        def epilogue(acc, ex, out_refs):
            out_refs[0][...] = acc.astype(out_refs[0].dtype)

    def body(a_ref, b_ref, *rest):
        ex, out_refs = rest[:ne], rest[ne:ne + no]
        lhs = a_ref[...] if prologue is None else prologue(a_ref, ex, out_refs, pl.program_id(1))
        p = _dot(lhs, b_ref[...], dims)
        if nk == 1:
            epilogue(p, ex, out_refs)
        else:
            acc = rest[-1]
            k = pl.program_id(2)

            @pl.when(k == 0)
            def _():
                acc[...] = p

            @pl.when(jnp.logical_and(k > 0, k < nk - 1))
            def _():
                acc[...] += p

            @pl.when(k == nk - 1)
            def _():
                epilogue(acc[...] + p, ex, out_refs)

    out_specs, out_shape = [], []
    for o in outs:
        if isinstance(o, tuple):
            shape, dt, bs, im = o
            out_specs.append(pl.BlockSpec(bs, im))
            out_shape.append(jax.ShapeDtypeStruct(shape, dt))
        else:
            out_specs.append(pl.BlockSpec((tm, tn), lambda i, j, k: (i, j)))
            out_shape.append(jax.ShapeDtypeStruct((M, N), o))
    in_specs = [a_spec, b_spec]
    for _, bs, im in extras:
        in_specs.append(pl.BlockSpec(memory_space=pl.ANY) if bs is None else pl.BlockSpec(bs, im))
    res = pl.pallas_call(
        body, name=name,
        grid=(M // tm, N // tn, nk),
        in_specs=in_specs, out_specs=out_specs, out_shape=out_shape,
        scratch_shapes=[pltpu.VMEM((tm, tn), F32)] if nk > 1 else [],
        input_output_aliases={2 + e: o for e, o in (aliases or {}).items()},
        compiler_params=_params(("arbitrary", "arbitrary", "arbitrary"), VMEM_BIG),
    )(a, b, *[e[0] for e in extras])
    return res if no > 1 else res[0]


def _rows(tm, w=D, col=0):
    return (tm, w), lambda i, j, k, c=col: (i, c)


def _vecs(w=D, col=0):
    return (1, w), lambda i, j, k, c=col: (0, c)


def _sum_out(w=D):
    return ((1, w), F32, (1, w), lambda i, j, k: (0, 0))


def _mm_pool_tn(a, b, *, name, tk):
    L = a.shape[0]

    def body(a_ref, b_ref, o_ref):
        p = _dot(a_ref[...], b_ref[...], TN)

        @pl.when(pl.program_id(1) == 0)
        def _():
            o_ref[...] = p

        @pl.when(pl.program_id(1) > 0)
        def _():
            o_ref[...] += p

    blk = pl.BlockSpec((tk, PGW), lambda g, k: (k, g))
    return pl.pallas_call(body, name=name, grid=(4, L // tk), in_specs=[blk, blk],
                          out_specs=pl.BlockSpec((PGW, PGW), lambda g, k: (g, 0)),
                          out_shape=jax.ShapeDtypeStruct((POOL_W, PGW), F32),
                          compiler_params=_params(("parallel", "arbitrary")))(a, b)


def _acc_out(ref, val, i):
    @pl.when(i == 0)
    def _():
        ref[...] = val

    @pl.when(i > 0)
    def _():
        ref[...] += val


def _colsum(v):
    return jnp.sum(v, axis=0, keepdims=True)


def _ep_resid_norm(acc, ex, outs):
    x_ref, g_ref, nw_ref, sc_ref, sh_ref = ex
    mix_ref, x1_ref, h_ref = outs
    mix_ref[...] = acc.astype(mix_ref.dtype)
    xv = x_ref[...] + g_ref[...] * acc
    x1_ref[...] = xv
    r = lax.rsqrt(jnp.mean(xv * xv, axis=-1, keepdims=True) + EPS)
    h_ref[...] = (xv * r * nw_ref[...] * (1.0 + sc_ref[...]) + sh_ref[...]).astype(h_ref.dtype)


def _ep_final(acc, ex, outs):
    x1_ref, t_ref, g_ref, nw_ref = ex
    dx2_ref, dd_ref, loss_ref, dnw_ref, dg_ref = outs
    i = _row_step()
    x2 = x1_ref[...] + g_ref[...] * acc
    r = lax.rsqrt(jnp.mean(x2 * x2, axis=-1, keepdims=True) + EPS)
    xh = x2 * r
    e = xh * nw_ref[...] - t_ref[...]
    part = 0.5 * jnp.sum(jnp.mean(e * e, axis=-1, keepdims=True), axis=0, keepdims=True)
    dy = e * (1.0 / D)
    g = dy * nw_ref[...]
    dx2 = r * (g - xh * jnp.mean(g * xh, axis=-1, keepdims=True))
    dx2_ref[...] = dx2
    dd_ref[...] = (dx2 * g_ref[...]).astype(dd_ref.dtype)
    _acc_out(loss_ref, jnp.broadcast_to(part, (1, 128)), i)
    _acc_out(dnw_ref, _colsum(dy * xh), i)
    _acc_out(dg_ref, _colsum(dx2 * acc), i)


def _ep_norm_bwd(acc, ex, outs):
    x_ref, dr_ref, nw_ref, sc_ref = ex[:4]
    dx_ref, p_ref, q_ref = outs[:3]
    i = _row_step()
    xv = x_ref[...]
    r = lax.rsqrt(jnp.mean(xv * xv, axis=-1, keepdims=True) + EPS)
    xh = xv * r
    g = acc * (nw_ref[...] * (1.0 + sc_ref[...]))
    dx = dr_ref[...] + r * (g - xh * jnp.mean(g * xh, axis=-1, keepdims=True))
    dx_ref[...] = dx
    _acc_out(p_ref, _colsum(acc * xh), i)
    _acc_out(q_ref, _colsum(acc), i)
    if len(ex) > 4:
        m_ref, g_ref = ex[4:]
        dm_ref, dg_ref = outs[3:]
        dm_ref[...] = (dx * g_ref[...]).astype(dm_ref.dtype)
        _acc_out(dg_ref, _colsum(dx * m_ref[...].astype(F32)), i)


def _ep_merge_bwd(acc, ex, outs):
    a_ref, b_ref, gl_ref = ex
    da_ref, db_ref, dgl_ref = outs
    s = _sigmoid(gl_ref[...].astype(F32))
    s1, s2 = s[:, :D], s[:, D:]
    da_ref[...] = (acc * s1).astype(da_ref.dtype)
    db_ref[...] = (acc * s2).astype(db_ref.dtype)
    dgl_ref[:, :D] = (acc * a_ref[...].astype(F32) * s1 * (1.0 - s1)).astype(dgl_ref.dtype)
    dgl_ref[:, D:] = (acc * b_ref[...].astype(F32) * s2 * (1.0 - s2)).astype(dgl_ref.dtype)


GW = DI // NG


def _ep_gated_norm_bwd(acc, ex, outs):
    y_ref, z_ref, w_ref, _ = ex
    dy_ref, dz_ref, dw_ref = outs
    zv = z_ref[...].astype(F32)
    yv = y_ref[...].astype(F32)
    sg = _sigmoid(zv)
    sz = zv * sg
    yg = yv * sz
    dsz = sg * (1.0 + zv * (1.0 - sg))
    dws = []
    for k in range(NG):
        sl = slice(k * GW, (k + 1) * GW)
        seg = yg[:, sl]
        r = lax.rsqrt(jnp.mean(seg * seg, axis=-1, keepdims=True) + EPS)
        sh = seg * r
        dn = acc[:, sl]
        g = dn * w_ref[:, sl]
        dyg = r * (g - sh * jnp.mean(g * sh, axis=-1, keepdims=True))
        dy_ref[:, sl] = dyg * sz[:, sl]
        dz_ref[:, sl] = (dyg * yv[:, sl] * dsz[:, sl]).astype(dz_ref.dtype)
        dws.append(_colsum(dn * sh))
    _acc_out(dw_ref, jnp.concatenate(dws, axis=1), _row_step())


CONV_CB = 128
HALO = 16


def _time_chunk(L):
    return _pick(L, (256, 128))


def _with_halo(x_ref, i, r0, rc):
    p0 = pl.multiple_of(jnp.maximum(r0 - HALO, 0), HALO)
    prev = jnp.where(i > 0, x_ref[pl.ds(p0, HALO), :].astype(F32), 0.0)
    return jnp.concatenate([prev, x_ref[pl.ds(r0, rc), :].astype(F32)], axis=0)


def _conv_bwd(proj, dy, w, b, dproj, *, name):
    L = proj.shape[0]
    rc = _time_chunk(L)
    n = L // rc

    def body(x_ref, dy_ref, w_ref, b_ref, dp_in, dx_ref, dw_ref, db_ref, xpad, dpad):
        del dp_in
        wv = w_ref[...]
        bv = b_ref[...]
        dpad[rc:rc + HALO, :] = jnp.zeros((HALO, CONV_CB), F32)

        def step(k, carry):
            db, d0, d1, d2, d3 = carry
            i = n - 1 - k
            r0 = pl.multiple_of(i * rc, rc)
            p0 = pl.multiple_of(jnp.maximum(r0 - HALO, 0), HALO)
            xpad[0:HALO, :] = jnp.where(i > 0, x_ref[pl.ds(p0, HALO), :].astype(F32), 0.0)
            xpad[HALO:HALO + rc, :] = x_ref[pl.ds(r0, rc), :].astype(F32)
            xk = [xpad[HALO - j:HALO - j + rc, :] for j in range(4)]
            pre = bv
            for j in range(4):
                pre = pre + xk[j] * wv[3 - j:4 - j]
            dpre = dy_ref[pl.ds(r0, rc), :] * _dsilu(pre)
            dpad[0:rc, :] = dpre
            acc = dpre * wv[3:4]
            for j in (1, 2, 3):
                acc = acc + dpad[j:j + rc, :] * wv[3 - j:4 - j]
            dx_ref[pl.ds(r0, rc), :] = acc.astype(dx_ref.dtype)
            dpad[rc:rc + HALO, :] = dpre[:HALO]
            return (db + _colsum(dpre), d0 + _colsum(dpre * xk[3]), d1 + _colsum(dpre * xk[2]),
                    d2 + _colsum(dpre * xk[1]), d3 + _colsum(dpre * xk[0]))

        z = jnp.zeros((1, CONV_CB), F32)
        db, d0, d1, d2, d3 = lax.fori_loop(0, n, step, (z, z, z, z, z))
        db_ref[...] = db
        dw_ref[...] = jnp.concatenate([d0, d1, d2, d3], axis=0)

    nb = XBC // CONV_CB
    return pl.pallas_call(
        body, name=name, grid=(nb,),
        in_specs=[pl.BlockSpec((L, CONV_CB), lambda j: (0, j + C_XBC // CONV_CB)),
                  pl.BlockSpec((L, CONV_CB), lambda j: (0, j)),
                  pl.BlockSpec((4, CONV_CB), lambda j: (0, j)), pl.BlockSpec((1, CONV_CB), lambda j: (0, j)),
                  pl.BlockSpec(memory_space=pl.ANY)],
        out_specs=[pl.BlockSpec((L, CONV_CB), lambda j: (0, j + C_XBC // CONV_CB)),
                   pl.BlockSpec((4, CONV_CB), lambda j: (0, j)), pl.BlockSpec((1, CONV_CB), lambda j: (0, j))],
        out_shape=[jax.ShapeDtypeStruct((L, NPROJ), BF16), jax.ShapeDtypeStruct((4, XBC), F32),
                   jax.ShapeDtypeStruct((1, XBC), F32)],
        scratch_shapes=[pltpu.VMEM((rc + HALO, CONV_CB), F32), pltpu.VMEM((rc + HALO, CONV_CB), F32)],
        input_output_aliases={4: 0},
        compiler_params=_params(("parallel",), VMEM_BIG))(proj, dy, w, b, dproj)


def _pool_fwd(proj, *, name):
    L = proj.shape[0]
    rc = _time_chunk(L)
    n = L // rc

    def body(x_ref, o_ref, pad):
        g = pl.program_id(0)
        pad[0:HALO, :] = jnp.zeros((HALO, PGW), F32)

        def fill(i, c):
            r0 = pl.multiple_of(i * rc, rc)
            pad[pl.ds(r0 + HALO, rc), :] = x_ref[pl.ds(r0, rc), :].astype(F32)
            return c

        lax.fori_loop(0, n, fill, 0)
        rows = lax.broadcasted_iota(jnp.int32, (rc, PGW), 0)

        for gi in range(4):
            win = 2 << gi

            @pl.when(g == gi)
            def _(gi=gi, win=win):
                def step(i, c):
                    r0 = pl.multiple_of(i * rc, rc)
                    ext = pad[pl.ds(r0, rc + HALO), :]
                    s = ext
                    sh = 1
                    while sh < win:
                        s = s + pltpu.roll(s, sh, 0)
                        sh *= 2
                    cnt = jnp.minimum(rows + (r0 + 1), win).astype(F32)
                    o_ref[pl.ds(r0, rc), :] = (s[HALO:] / cnt - ext[HALO:]).astype(o_ref.dtype)
                    return c

                lax.fori_loop(0, n, step, 0)

    return pl.pallas_call(
        body, name=name, grid=(4,),
        in_specs=[pl.BlockSpec((L, PGW), lambda j: (0, j + C_POOL // PGW))],
        out_specs=pl.BlockSpec((L, PGW), lambda j: (0, j)),
        out_shape=jax.ShapeDtypeStruct((L, POOL_W), BF16),
        scratch_shapes=[pltpu.VMEM((L + HALO, PGW), F32)],
        compiler_params=_params(("parallel",), VMEM_BIG))(proj)


def _pool_bwd(dpooled, dproj, *, name):
    L = dpooled.shape[0]
    rc = _time_chunk(L)
    n = L // rc

    def body(d_ref, dp_in, o_ref, pad):
        del dp_in
        g = pl.program_id(0)
        pad[L:L + HALO, :] = jnp.zeros((HALO, PGW), F32)
        rows = lax.broadcasted_iota(jnp.int32, (rc, PGW), 0)

        for gi in range(4):
            win = 2 << gi

            @pl.when(g == gi)
            def _(gi=gi, win=win):
                def fill(i, c):
                    r0 = pl.multiple_of(i * rc, rc)
                    cnt = jnp.minimum(rows + (r0 + 1), win).astype(F32)
                    pad[pl.ds(r0, rc), :] = d_ref[pl.ds(r0, rc), :] / cnt
                    return c

                lax.fori_loop(0, n, fill, 0)

                def step(i, c):
                    r0 = pl.multiple_of(i * rc, rc)
                    s = pad[pl.ds(r0, rc + HALO), :]
                    sh = 1
                    while sh < win:
                        s = s + pltpu.roll(s, rc + HALO - sh, 0)
                        sh *= 2
                    o_ref[pl.ds(r0, rc), :] = (s[:rc] - d_ref[pl.ds(r0, rc), :]).astype(o_ref.dtype)
                    return c

                lax.fori_loop(0, n, step, 0)

    return pl.pallas_call(
        body, name=name, grid=(4,),
        in_specs=[pl.BlockSpec((L, PGW), lambda j: (0, j)), pl.BlockSpec(memory_space=pl.ANY)],
        out_specs=pl.BlockSpec((L, PGW), lambda j: (0, j + C_POOL // PGW)),
        out_shape=jax.ShapeDtypeStruct((L, NPROJ), BF16),
        scratch_shapes=[pltpu.VMEM((L + HALO, PGW), F32)],
        input_output_aliases={1: 0},
        compiler_params=_params(("parallel",), VMEM_BIG))(dpooled, dproj)


_SPLIT_DT = jnp.bfloat16


def _ssd_consts():
    tri = np.tril(np.ones((Q, Q), np.float32))
    exp = np.zeros((128, DI), np.float32)
    for h in range(NH):
        exp[h, h * HP:(h + 1) * HP] = 1.0
    exp2 = np.concatenate([exp, exp], axis=0)
    return (jnp.asarray(tri, dtype=_SPLIT_DT), jnp.asarray(tri.T.copy(), dtype=_SPLIT_DT),
            jnp.asarray(exp2, dtype=_SPLIT_DT))


def _split(v, n):
    parts, r = [], v
    for _ in range(n):
        p = r.astype(_SPLIT_DT)
        parts.append(p)
        r = r - p.astype(F32)
    return parts


def _bdot(a, b, dims):
    return lax.dot_general(a, b, (dims, ((), ())), preferred_element_type=F32)


def _tri_sum(t_ref, v):
    r = _bdot(t_ref[...], jnp.concatenate(_split(v, 3), axis=1), NN)
    return r[:, :128] + r[:, 128:256] + r[:, 256:]


def _expand(v, e2_ref):
    return _bdot(jnp.concatenate(_split(v, 2), axis=1), e2_ref[...], NN)


def _reduce_heads(vals, eg):
    parts = []
    for v in vals:
        parts += _split(v, 2)
    r = _bdot(jnp.concatenate(parts, axis=0), eg, NT)
    return [r[2 * i * Q:(2 * i + 1) * Q] + r[(2 * i + 1) * Q:(2 * i + 2) * Q] for i in range(len(vals))]


def _ssd_common(xbc_ref, dtw_ref, arow_ref, t_ref, e_ref):
    dt = dtw_ref[:, :128]
    sig = dtw_ref[:, 128:]
    acs = _tri_sum(t_ref, dt * arow_ref[...])
    acs_x = _expand(acs, e_ref)
    dt_x = _expand(dt, e_ref)
    xs = xbc_ref[:, 0:DI]
    return sig, dt, acs, acs.T, acs_x, dt_x, xs


CONV_SLAB = 512


def _ssd_fwd(raw, dtp, cw, cb, arow, dsk_x, *, name):
    L = raw.shape[0]
    nc = L // Q
    tri, _, expand = _ssd_consts()

    def body(raw_ref, halo_ref, cw_ref, cb_ref, dtw_ref, arow_ref, dsk_ref, t_ref, e_ref,
             y_ref, hs_ref, xbc_ref, h_scr, cpad):
        c = pl.program_id(0)

        @pl.when(c == 0)
        def _():
            h_scr[...] = jnp.zeros_like(h_scr)

        cpad[0:8, :] = jnp.where(c > 0, halo_ref[...], 0.0)
        cpad[8:8 + Q, :] = raw_ref[...]
        for lo in range(0, XBC, CONV_SLAB):
            sl = slice(lo, lo + CONV_SLAB)
            acc = cb_ref[:, sl]
            for j in range(4):
                acc = acc + cpad[8 - j:8 - j + Q, sl] * cw_ref[3 - j:4 - j, sl]
            xbc_ref[:, sl] = acc * _sigmoid(acc)

        _, dt, acs, acs_t, acs_x, dt_x, xs = _ssd_common(xbc_ref, dtw_ref, arow_ref, t_ref, e_ref)
        xdt = xs * dt_x
        eacs = jnp.exp(acs_x)
        acs_last = acs_x[Q - 1:Q, :]
        dec = jnp.exp(acs_last - acs_x)
        hs_ref[0] = h_scr[...].astype(hs_ref.dtype)
        causal = lax.broadcasted_iota(jnp.int32, (Q, Q), 0) >= lax.broadcasted_iota(jnp.int32, (Q, Q), 1)
        first = lax.broadcasted_iota(jnp.int32, (Q, 128), 1) < HP
        for g in range(NG):
            bg = xbc_ref[:, DI + g * NS:DI + (g + 1) * NS]
            cg = xbc_ref[:, DI + NG * NS + g * NS:DI + NG * NS + (g + 1) * NS]
            s = _dot(cg, bg, NT)
            sl = slice(g * GW, (g + 1) * GW)
            hg = h_scr[:, sl]
            yoff = _dot(cg, hg, NN) * eacs[:, sl]
            st = _dot(bg, xdt[:, sl] * dec[:, sl], TN)
            h_scr[:, sl] = hg * eacs[Q - 1:Q, sl] + st
            for j in range(4):
                lo = g * GW + j * 128
                xb = xdt[:, lo:lo + 128]
                yp = yoff[:, j * 128:(j + 1) * 128] + dsk_ref[:, lo:lo + 128] * xs[:, lo:lo + 128]
                for e in range(2):
                    h = g * 8 + j * 2 + e
                    lm = jnp.exp(jnp.where(causal, acs[:, h:h + 1] - acs_t[h:h + 1, :], NEG))
                    xm = jnp.where(first if e == 0 else jnp.logical_not(first), xb, 0.0)
                    yp = yp + _dot(s * lm, xm, NN)
                y_ref[:, lo:lo + 128] = yp.astype(y_ref.dtype)

    const = lambda c: (0, 0)
    return pl.pallas_call(
        body, name=name, grid=(nc,),
        in_specs=[pl.BlockSpec((Q, XBC), lambda c: (c, 0)),
                  pl.BlockSpec((8, XBC), lambda c: (jnp.maximum(c * (Q // 8) - 1, 0), 0)),
                  pl.BlockSpec((4, XBC), const), pl.BlockSpec((1, XBC), const),
                  pl.BlockSpec((Q, DT_PAD), lambda c: (c, 0)),
                  pl.BlockSpec((1, 128), const), pl.BlockSpec((1, DI), const),
                  pl.BlockSpec((Q, Q), const), pl.BlockSpec((256, DI), const)],
        out_specs=[pl.BlockSpec((Q, DI), lambda c: (c, 0)), pl.BlockSpec((1, NS, DI), lambda c: (c, 0, 0)),
                   pl.BlockSpec((Q, XBC), lambda c: (c, 0))],
        out_shape=[jax.ShapeDtypeStruct((L, DI), BF16), jax.ShapeDtypeStruct((nc, NS, DI), F32),
                   jax.ShapeDtypeStruct((L, XBC), F32)],
        scratch_shapes=[pltpu.VMEM((NS, DI), F32), pltpu.VMEM((8 + Q, XBC), F32)],
        compiler_params=_params(("arbitrary",), VMEM_BIG))(raw, raw, cw, cb, dtp, arow, dsk_x, tri, expand)


def _ssd_bwd(dy, xbc, dtp, hs, arow, dsk_x, dproj, *, name):
    L = xbc.shape[0]
    nc = L // Q
    tri, triu, expand = _ssd_consts()

    def body(dy_ref, xbc_ref, dtw_ref, hs_ref, arow_ref, dsk_ref, t_ref, u_ref, e_ref, dp_in,
             dxbc_ref, ddtw_ref, da_ref, ddx_ref, ddtb_ref, dh_scr):
        del dp_in
        i = pl.program_id(0)

        @pl.when(i == 0)
        def _():
            dh_scr[...] = jnp.zeros_like(dh_scr)

        sig, dt, acs, acs_t, acs_x, dt_x, xs = _ssd_common(xbc_ref, dtw_ref, arow_ref, t_ref, e_ref)
        dyv = dy_ref[...]
        xdt = xs * dt_x
        eacs = jnp.exp(acs_x)
        acs_last = acs_x[Q - 1:Q, :]
        dec = jnp.exp(acs_last - acs_x)
        gy = dyv * eacs
        causal = lax.broadcasted_iota(jnp.int32, (Q, Q), 0) >= lax.broadcasted_iota(jnp.int32, (Q, Q), 1)
        first = lax.broadcasted_iota(jnp.int32, (Q, 128), 1) < HP
        lane_h = lax.broadcasted_iota(jnp.int32, (Q, 128), 1)
        sub_h = lax.broadcasted_iota(jnp.int32, (128, Q), 0)
        last_row = lax.broadcasted_iota(jnp.int32, (Q, GW), 0) == Q - 1
        dacs = jnp.zeros((Q, 128), F32)
        dacs_t = jnp.zeros((128, Q), F32)
        ddt = jnp.zeros((Q, 128), F32)
        for g in range(NG):
            bg = xbc_ref[:, DI + g * NS:DI + (g + 1) * NS]
            cg = xbc_ref[:, DI + NG * NS + g * NS:DI + NG * NS + (g + 1) * NS]
            s = _dot(cg, bg, NT)
            sl = slice(g * GW, (g + 1) * GW)
            hg = hs_ref[0, :, sl].astype(F32)
            dhn = dh_scr[:, sl]
            eal = eacs[Q - 1:Q, sl]
            gg = gy[:, sl]
            dax = gg * _dot(cg, hg, NN)
            dcg = _dot(gg, hg, NT)
            dh_scr[:, sl] = _dot(cg, gg, TN) + dhn * eal
            dal = eal * _colsum(dhn * hg)
            xdd = xdt[:, sl] * dec[:, sl]
            dbg = _dot(xdd, dhn, NT)
            wv = _dot(bg, dhn, NN)
            dd = wv * xdd
            dax = dax - dd
            dal = dal + _colsum(dd)
            dax = dax + jnp.where(last_row, dal, 0.0)
            dxdt_g = wv * dec[:, sl]
            ds = jnp.zeros((Q, Q), F32)
            dxdt_blocks = []
            for j in range(4):
                lo = g * GW + j * 128
                xb = xdt[:, lo:lo + 128]
                dyb = dyv[:, lo:lo + 128]
                dxb = dxdt_g[:, j * 128:(j + 1) * 128]
                for e in range(2):
                    h = g * 8 + j * 2 + e
                    lm = jnp.exp(jnp.where(causal, acs[:, h:h + 1] - acs_t[h:h + 1, :], NEG))
                    m = s * lm
                    dym = jnp.where(first if e == 0 else jnp.logical_not(first), dyb, 0.0)
                    dm = _dot(dym, xb, NT)
                    r = dm * m
                    dacs = dacs + jnp.where(lane_h == h, jnp.sum(r, axis=1, keepdims=True), 0.0)
                    dacs_t = dacs_t + jnp.where(sub_h == h, _colsum(r), 0.0)
                    ds = ds + dm * lm
                    dxb = dxb + _dot(m, dym, TN)
                dxdt_blocks.append(dxb)
            dxdt = jnp.concatenate(dxdt_blocks, axis=1)
            dcg = dcg + _dot(ds, bg, NN)
            dbg = dbg + _dot(ds, cg, TN)
            dxbc_ref[:, DI + g * NS:DI + (g + 1) * NS] = dbg
            dxbc_ref[:, DI + NG * NS + g * NS:DI + NG * NS + (g + 1) * NS] = dcg
            dxbc_ref[:, sl] = dsk_ref[:, sl] * dyv[:, sl] + dxdt * dt_x[:, sl]
            ddt_g, dacs_g = _reduce_heads([dxdt * xs[:, sl], dax], e_ref[0:128, sl])
            ddt = ddt + ddt_g
            dacs = dacs + dacs_g
        dacs = dacs - dacs_t.T
        ddta = _tri_sum(u_ref, dacs)
        ddt = ddt + ddta * arow_ref[...]
        ddtw = jnp.where(lane_h < NH, ddt * sig, 0.0)
        ddtw_ref[...] = jnp.concatenate([ddtw, jnp.zeros((Q, DT_PAD - 128), F32)], axis=1).astype(ddtw_ref.dtype)
        _acc_out(da_ref, _colsum(ddta * dt), i)
        _acc_out(ddx_ref, _colsum(dyv * xs), i)
        _acc_out(ddtb_ref, _colsum(ddtw), i)

    rev = lambda c: (nc - 1 - c, 0)
    const = lambda c: (0, 0)
    return pl.pallas_call(
        body, name=name, grid=(nc,),
        in_specs=[pl.BlockSpec((Q, DI), rev), pl.BlockSpec((Q, XBC), rev),
                  pl.BlockSpec((Q, DT_PAD), rev),
                  pl.BlockSpec((1, NS, DI), lambda c: (nc - 1 - c, 0, 0)),
                  pl.BlockSpec((1, 128), const), pl.BlockSpec((1, DI), const),
                  pl.BlockSpec((Q, Q), const), pl.BlockSpec((Q, Q), const), pl.BlockSpec((256, DI), const),
                  pl.BlockSpec(memory_space=pl.ANY)],
        out_specs=[pl.BlockSpec((Q, XBC), rev),
                   pl.BlockSpec((Q, DT_PAD), lambda c: (nc - 1 - c, C_DT // DT_PAD)),
                   pl.BlockSpec((1, 128), const), pl.BlockSpec((1, DI), const), pl.BlockSpec((1, 128), const)],
        out_shape=[jax.ShapeDtypeStruct((L, XBC), F32), jax.ShapeDtypeStruct((L, NPROJ), BF16),
                   jax.ShapeDtypeStruct((1, 128), F32), jax.ShapeDtypeStruct((1, DI), F32),
                   jax.ShapeDtypeStruct((1, 128), F32)],
        scratch_shapes=[pltpu.VMEM((NS, DI), F32)],
        input_output_aliases={9: 1},
        compiler_params=_params(("arbitrary",), VMEM_BIG))(dy, xbc, dtp, hs, arow, dsk_x, tri, triu,
                                                          expand, dproj)


def _adam_update(wv, gv, mv, vv):
    c1 = 1.0 - ADAM_B1 ** ADAM_STEP
    c2 = 1.0 - ADAM_B2 ** ADAM_STEP
    mn = ADAM_B1 * mv + (1.0 - ADAM_B1) * gv
    vn = ADAM_B2 * vv + (1.0 - ADAM_B2) * (gv * gv)
    return -ADAM_LR * ((mn / c1) / (jnp.sqrt(vn / c2) + ADAM_EPS) + ADAM_WD * wv), mn, vn


def _adamw(w, g, m, v, *, name, tr=None):
    R = w.shape[0]
    rest = tuple(w.shape[1:])
    if tr is None:
        tr = _pick(R, (256, 128, 64, 32, 16, 8))
    assert R % tr == 0

    def body(w_ref, g_ref, m_ref, v_ref, d_ref, mo_ref, vo_ref):
        d_ref[...], mo_ref[...], vo_ref[...] = _adam_update(w_ref[...], g_ref[...], m_ref[...], v_ref[...])

    zeros = (0,) * len(rest)
    spec = pl.BlockSpec((tr,) + rest, lambda i: (i,) + zeros)
    return pl.pallas_call(body, name=name, grid=(R // tr,), in_specs=[spec] * 4, out_specs=[spec] * 3,
                          out_shape=[jax.ShapeDtypeStruct(w.shape, F32)] * 3,
                          compiler_params=_params(("parallel",)))(w, g, m, v)


def _adamw_small(svrow, g_conv, params, *, name):
    n = len(params)

    def body(*refs):
        sv_ref, gc_ref = refs[0], refs[1]
        ins, outs = refs[2:2 + 3 * n], refs[2 + 3 * n:]
        for p, (key, w, _, _) in enumerate(params):
            w_ref, m_ref, v_ref = ins[3 * p:3 * p + 3]
            g_ref, d_ref, mo_ref, vo_ref = outs[4 * p:4 * p + 4]
            gv = gc_ref[...] if key == "conv_w" else sv_ref[:, SV_OFF[key]:SV_OFF[key] + w.shape[1]]
            g_ref[...] = gv
            d_ref[...], mo_ref[...], vo_ref[...] = _adam_update(w_ref[...], gv, m_ref[...], v_ref[...])

    vm = pl.BlockSpec(memory_space=pltpu.VMEM)
    args = [svrow, g_conv]
    shapes = []
    for _, w, m, v in params:
        args += [w, m, v]
        shapes += [jax.ShapeDtypeStruct(w.shape, F32)] * 4
    res = pl.pallas_call(body, name=name, in_specs=[vm] * len(args), out_specs=[vm] * len(shapes),
                         out_shape=shapes)(*args)
    return {key: tuple(res[4 * p:4 * p + 4]) for p, (key, _, _, _) in enumerate(params)}


def _slab_sum(recv, *, tile, name):
    rows = recv.shape[1]
    assert rows % tile == 0 and tile % 16 == 0

    def body(r_ref, o_ref):
        acc = r_ref[0].astype(F32)
        for j in range(1, N_DEV):
            acc = acc + r_ref[j].astype(F32)
        o_ref[...] = acc

    return pl.pallas_call(body, name=name, grid=(rows // tile,),
                          in_specs=[pl.BlockSpec((N_DEV, tile, D), lambda i: (0, i, 0))],
                          out_specs=pl.BlockSpec((tile, D), lambda i: (i, 0)),
                          out_shape=jax.ShapeDtypeStruct((rows, D), F32),
                          compiler_params=_params(("parallel",)))(recv)


MESH = pl.DeviceIdType.MESH


def _coords():
    return lax.axis_index("x"), lax.axis_index("y"), lax.axis_index("c")


def _peer(k):
    x, y, c = _coords()
    px = 1 - x if k & 4 else x
    py = 1 - y if k & 2 else y
    pc = 1 - c if k & 1 else c
    return (px, py, pc), 4 * px + 2 * py + pc


def _rcopy(src, dst, ssem, rsem, dev):
    return pltpu.make_async_remote_copy(src_ref=src, dst_ref=dst, send_sem=ssem, recv_sem=rsem,
                                        device_id=dev, device_id_type=MESH)


def _exchange_all(src_of, dst_slot, send_sems, recv_sems):
    x, y, c = _coords()
    me = 4 * x + 2 * y + c
    sent = []
    for k in range(1, N_DEV):
        dev, pidx = _peer(k)
        cp = _rcopy(src_of(pidx), dst_slot(me), send_sems.at[k - 1], recv_sems.at[k - 1], dev)
        cp.start()
        sent.append(cp)
    for k in range(1, N_DEV):
        dev, pidx = _peer(k)
        _rcopy(src_of(pidx), dst_slot(pidx), send_sems.at[k - 1], recv_sems.at[k - 1], dev).wait_recv()
    for cp in sent:
        cp.wait_send()


def _rows_of_slots(buf, nslots):
    rows = lax.broadcasted_iota(jnp.int32, (8, buf.shape[-1]), 0)
    out = jnp.zeros((8, buf.shape[-1]), F32)
    for j in range(nslots):
        out = out + jnp.where(rows == j, buf[j], 0.0)
    return out


def _exchange_start(src_of, dst_slot, send_sems, recv_sems):
    x, y, c = _coords()
    me = 4 * x + 2 * y + c
    sent = []
    for k in range(1, N_DEV):
        dev, pidx = _peer(k)
        cp = _rcopy(src_of(pidx), dst_slot(me), send_sems.at[k - 1], recv_sems.at[k - 1], dev)
        cp.start()
        sent.append(cp)
    return sent


def _exchange_finish(sent, src_of, dst_slot, send_sems, recv_sems):
    for k in range(1, N_DEV):
        dev, pidx = _peer(k)
        _rcopy(src_of(pidx), dst_slot(pidx), send_sems.at[k - 1], recv_sems.at[k - 1], dev).wait_recv()
    for cp in sent:
        cp.wait_send()


def _ada_gather(c, w_ada, b_r, slab, *, name):
    wloc = w_ada.shape[1]

    def body(c_ref, w_ref, b_ref, x_ref, mod_ref, call_ref, out_ref,
             csrc, cbuf, psrc, pbuf, s1, r1, s2, r2, send_sems, recv_sems, local_sem):
        x, y, cc = _coords()
        me_i = 4 * x + 2 * y + cc
        me, sibling = (x, y, cc), (x, y, 1 - cc)
        chips = [(1 - x, y), (x, 1 - y), (1 - x, 1 - y)]

        def slot(px, py, pc):
            return out_ref.at[4 * px + 2 * py + pc]

        def copy(k, block, to, src=None):
            return _rcopy(slot(*block) if src is None else src, slot(*block), send_sems.at[k], recv_sems.at[k], to)

        csrc[...] = jnp.broadcast_to(c_ref[...], (8, D))
        cbuf[me_i] = csrc[...]
        c_of, c_slot = (lambda p: csrc), (lambda s: cbuf.at[s])
        sent1 = _exchange_start(c_of, c_slot, s1, r1)

        mine = pltpu.make_async_copy(x_ref, slot(*me), local_sem)
        mine.start()
        first = [copy(0, me, sibling, src=x_ref)]
        first += [copy(1 + j, me, (*chip, cc), src=x_ref) for j, chip in enumerate(chips)]
        for cp in first:
            cp.start()

        _exchange_finish(sent1, c_of, c_slot, s1, r1)
        call = _rows_of_slots(cbuf, N_DEV)
        call_ref[...] = call
        prod = _dot_hi(_silu(call), w_ref[...])
        for b in range(N_DEV):
            psrc[b] = jnp.broadcast_to(prod[b:b + 1, :], (8, wloc))
        pbuf[me_i] = psrc[me_i]
        p_of, p_slot = (lambda p: psrc.at[p]), (lambda s: pbuf.at[s])
        sent2 = _exchange_start(p_of, p_slot, s2, r2)

        passed = [copy(4 + j, (*chip, cc), sibling) for j, chip in enumerate(chips)]
        for j, chip in enumerate(chips):
            copy(1 + j, (*chip, cc), me).wait_recv()
            passed[j].start()
        copy(0, sibling, me).wait_recv()
        for j, chip in enumerate(chips):
            copy(4 + j, (*chip, 1 - cc), me).wait_recv()

        _exchange_finish(sent2, p_of, p_slot, s2, r2)
        mod_ref[...] = _rows_of_slots(pbuf, N_DEV) + b_ref[...]
        for cp in first + passed:
            cp.wait_send()
        mine.wait()

    vm = pl.BlockSpec(memory_space=pltpu.VMEM)
    anyspec = pl.BlockSpec(memory_space=pl.ANY)
    return pl.pallas_call(
        body, name=name, in_specs=[vm, vm, vm, anyspec], out_specs=[vm, vm, anyspec],
        out_shape=[jax.ShapeDtypeStruct((N_DEV, wloc), F32), jax.ShapeDtypeStruct((N_DEV, D), F32),
                   jax.ShapeDtypeStruct((N_DEV,) + slab.shape, slab.dtype)],
        scratch_shapes=[pltpu.VMEM((8, D), F32), pltpu.VMEM((N_DEV, 8, D), F32),
                        pltpu.VMEM((N_DEV, 8, wloc), F32), pltpu.VMEM((N_DEV, 8, wloc), F32),
                        pltpu.SemaphoreType.DMA((N_DEV - 1,)), pltpu.SemaphoreType.DMA((N_DEV - 1,)),
                        pltpu.SemaphoreType.DMA((N_DEV - 1,)), pltpu.SemaphoreType.DMA((N_DEV - 1,)),
                        pltpu.SemaphoreType.DMA((7,)), pltpu.SemaphoreType.DMA((7,)), pltpu.SemaphoreType.DMA],
        compiler_params=pltpu.CompilerParams(vmem_limit_bytes=VMEM_BIG))(c, w_ada, b_r, slab)


_HBM =pl.BlockSpec(memory_space=pltpu.HBM)
_SEM = pl.BlockSpec(memory_space=pltpu.SEMAPHORE)
_EFFECT = pltpu.SideEffectType.DATAFLOW_SIDE_EFFECTING


def _xchg_src(src_ref, pidx, per_peer):
    return src_ref.at[pidx] if per_peer else src_ref


def _xchg_start(src, *, per_peer, name):
    rows = src.shape[-2]
    land_shape = (N_DEV, rows, D)

    def body(src_ref, land_ref, send_sems, recv_sems, src_thru, land_thru, token):
        del src_thru, land_thru
        x, y, c = _coords()
        me = 4 * x + 2 * y + c
        for k in range(1, N_DEV):
            dev, pidx = _peer(k)
            _rcopy(_xchg_src(src_ref, pidx, per_peer), land_ref.at[me], send_sems.at[k - 1],
                   recv_sems.at[k - 1], dev).start()
        token[...] = jnp.zeros_like(token)

    return pl.pallas_call(
        body, name=name,
        out_shape=(pltpu.SemaphoreType.DMA((N_DEV - 1,)), pltpu.SemaphoreType.DMA((N_DEV - 1,)),
                   pltpu.HBM(src.shape, src.dtype), pltpu.HBM(land_shape, src.dtype),
                   jax.ShapeDtypeStruct((8, 128), F32)),
        in_specs=(_HBM, _HBM),
        out_specs=(_SEM, _SEM, _HBM, _HBM, pl.BlockSpec(memory_space=pltpu.VMEM)),
        input_output_aliases={0: 2, 1: 3},
        compiler_params=pltpu.CompilerParams(has_side_effects=_EFFECT),
    )(pltpu.with_memory_space_constraint(src, pltpu.HBM),
      pltpu.with_memory_space_constraint(lax.empty(land_shape, src.dtype), pltpu.HBM))


def _xchg_wait(started, after, *, per_peer, name):
    send_sems, recv_sems, src_thru, land_thru, _ = started

    def body(src_ref, land_ref, send_sems, recv_sems, after_ref, src_dead, got_ref):
        del after_ref, src_dead, got_ref
        for k in range(1, N_DEV):
            dev, pidx = _peer(k)
            cp = _rcopy(_xchg_src(src_ref, pidx, per_peer), land_ref.at[pidx], send_sems.at[k - 1],
                        recv_sems.at[k - 1], dev)
            cp.wait_send()
            cp.wait_recv()

    return pl.pallas_call(
        body, name=name,
        out_shape=(pltpu.HBM(src_thru.shape, src_thru.dtype), pltpu.HBM(land_thru.shape, land_thru.dtype)),
        in_specs=(_HBM, _HBM, _SEM, _SEM, pl.BlockSpec(memory_space=pl.ANY)),
        out_specs=(_HBM, _HBM),
        input_output_aliases={0: 0, 1: 1},
        compiler_params=pltpu.CompilerParams(has_side_effects=_EFFECT),
    )(src_thru, land_thru, send_sems, recv_sems, after)


def _dep(token):
    return (token, (8, 128), lambda i, j, k: (0, 0))


def _small_allsum(sv, *, name):
    def body(sv_ref, all_ref, sum_ref, send_sems, recv_sems):
        x, y, c = _coords()
        me = 4 * x + 2 * y + c
        all_ref[me] = sv_ref[...]
        _exchange_all(lambda p: sv_ref, lambda s: all_ref.at[s], send_sems, recv_sems)
        acc = all_ref[0]
        for j in range(1, N_DEV):
            acc = acc + all_ref[j]
        sum_ref[...] = acc

    vm = pl.BlockSpec(memory_space=pltpu.VMEM)
    return pl.pallas_call(
        body, name=name, in_specs=[vm], out_specs=[vm, vm],
        out_shape=[jax.ShapeDtypeStruct((N_DEV, SV_ROWS, 128), F32), jax.ShapeDtypeStruct((SV_ROWS, 128), F32)],
        scratch_shapes=[pltpu.SemaphoreType.DMA((7,)), pltpu.SemaphoreType.DMA((7,))],
    )(sv)


def _ada_bwd(call, dmod_loc, *, name):
    wloc = dmod_loc.shape[1]

    def body(c_ref, d_ref, o_ref):
        o_ref[...] = _dot_hi(_silu(c_ref[...]), d_ref[...], TN)

    vm = pl.BlockSpec(memory_space=pltpu.VMEM)
    return pl.pallas_call(body, name=name, in_specs=[vm, vm], out_specs=vm,
                          out_shape=jax.ShapeDtypeStruct((D, wloc), F32),
                          compiler_params=pltpu.CompilerParams(vmem_limit_bytes=VMEM_BIG))(call, dmod_loc)


def _pad_rows(a, rows):
    return jnp.pad(a, ((0, rows - a.shape[0]), (0, 0)))


IN_SHIFT = tuple((IN_ROWS * j) % 16 for j in range(N_DEV))
IN_BASE = tuple(IN_ROWS * j - IN_SHIFT[j] for j in range(N_DEV))
IN_SEGMENTS = ((2048, XBC, C_XBC), (5152, 1024, C_POOL), (0, 2048, C_Z), (6176, 2048, C_GATE), (5120, 32, C_DT))


def _global_pieces(gs):
    pieces = []
    for j in range(N_DEV):
        lo, hi = 0, IN_ROWS_P
        if j > 0 and IN_BASE[j - 1] + IN_ROWS_P > IN_BASE[j]:
            pieces.append((IN_BASE[j], 16, gs[j - 1, IN_ROWS_P - 16:IN_ROWS_P] + gs[j, 0:16]))
            lo = 16
        if j + 1 < N_DEV and IN_BASE[j] + IN_ROWS_P > IN_BASE[j + 1]:
            hi = IN_ROWS_P - 16
        pieces.append((IN_BASE[j] + lo, hi - lo, gs[j, lo:hi]))
    return pieces


def _reorder_in_rows(gs):
    pieces = _global_pieces(gs)
    parts = []
    for lo, n, _ in IN_SEGMENTS:
        for p0, pn, arr in pieces:
            a, b = max(lo, p0), min(lo + n, p0 + pn)
            if a < b:
                parts.append(arr[a - p0:b - p0])
    parts.append(jnp.zeros((DT_PAD - 32, D), gs.dtype))
    return jnp.concatenate(parts, axis=0)


def _restore_in_shards(d):
    parts = []
    for j in range(N_DEV):
        r, end = IN_BASE[j], IN_BASE[j] + IN_ROWS_P
        while r < end:
            lo, n, new = next(s for s in IN_SEGMENTS if s[0] <= r < s[0] + s[1])
            e = min(end, lo + n)
            parts.append(d[new + r - lo:new + e - lo])
            r = e
    return jnp.concatenate(parts, axis=0).reshape(N_DEV, IN_ROWS_P, D)


def _pack_sv(parts):
    flat = []
    for n, size in SV_PARTS:
        v = parts[n].reshape(-1).astype(F32)
        flat.append(jnp.pad(v, (0, size - v.shape[0])))
    v = jnp.concatenate(flat)
    return jnp.pad(v, (0, SV_ROWS * 128 - v.shape[0])).reshape(SV_ROWS, 128)


def _sv_get(flat, n, size):
    return flat[SV_OFF[n]:SV_OFF[n] + size]


def kernel(x, c, w_ada, b_ada, norm_mix_w, w_in, conv_w, conv_b, dt_bias, a_log, d_skip, ssd_norm_w, w_branch_ssd, pool_w, pool_scale, w_branch_pool, w_out, norm_mlp_w, w_up, w_down, norm_final_w, loss_target, m_w_ada, m_b_ada, m_norm_mix_w, m_w_in, m_conv_w, m_conv_b, m_dt_bias, m_a_log, m_d_skip, m_ssd_norm_w, m_w_branch_ssd, m_pool_w, m_pool_scale, m_w_branch_pool, m_w_out, m_norm_mlp_w, m_w_up, m_w_down, m_norm_final_w, v_w_ada, v_b_ada, v_norm_mix_w, v_w_in, v_conv_w, v_conv_b, v_dt_bias, v_a_log, v_d_skip, v_ssd_norm_w, v_w_branch_ssd, v_pool_w, v_pool_scale, v_w_branch_pool, v_w_out, v_norm_mlp_w, v_w_up, v_w_down, v_norm_final_w):
    xs_ = x[0]
    tgt = loss_target[0]
    L = xs_.shape[0]
    me = 4 * lax.axis_index("x") + 2 * lax.axis_index("y") + lax.axis_index("c")
    wloc = w_ada.shape[2]

    conv_bits = lax.bitcast_convert_type(conv_w[0], SLAB_DT).reshape(3, D)
    in_shift = (IN_ROWS * me) % 16
    slab_in = lax.dynamic_update_slice(jnp.zeros((IN_ROWS_P, D), SLAB_DT), w_in[0].T.astype(SLAB_DT),
                                       (in_shift, 0))
    slab_in = jnp.concatenate([slab_in, _pad_rows(conv_bits, CONV_ROWS)], axis=0)
    slab_rest = jnp.concatenate([
        w_branch_ssd[0].astype(SLAB_DT),
        pool_w[0].reshape(32, D).astype(SLAB_DT),
        w_branch_pool[0].astype(SLAB_DT),
        w_out[0].astype(SLAB_DT),
        w_up[0].T.astype(SLAB_DT),
        w_down[0].astype(SLAB_DT)], axis=0)
    mod_p, c_all, gs_in = _ada_gather(c, w_ada[0], b_ada.reshape(N_DEV, wloc), slab_in,
                                      name="ada_gather_w_in")
    mod = mod_p.reshape(6, D)
    shift_m, scale_m, gate_m, shift_f, scale_f, gate_f = [mod[i:i + 1] for i in range(6)]
    slab_rest, gs_in = lax.optimization_barrier((slab_rest, gs_in))
    rest_started = _xchg_start(slab_rest, per_peer=False, name="gather_rest_start")
    gather_token = rest_started[4]

    w_in_t = _reorder_in_rows(gs_in)
    conv_full = lax.bitcast_convert_type(
        gs_in[:, IN_ROWS_P:IN_ROWS_P + 3].reshape(N_DEV, 4, XBC // N_DEV, 2), F32)
    conv_full = conv_full.transpose(1, 0, 2).reshape(4, XBC)

    dtb = jnp.pad(dt_bias, ((0, 0), (0, 128 - NH)))
    arow = jnp.pad(-jnp.exp(a_log), ((0, 0), (0, 128 - NH)))
    dsk_x = jnp.repeat(d_skip, HP, axis=1)

    tm = _pick(L, (1024, 512, 256, 128))
    tm2 = _pick(L, (2048, 1024, 512, 256, 128))
    tkl = _pick(L, (4096, 2048, 1024, 512, 256, 128))
    tkl2 = _pick(L, (2048, 1024, 512, 256, 128))

    tmh = _pick(L, (512, 256, 128))
    zcol = C_Z // DI
    gcol = C_GATE // (2 * D)

    def whole_rows(w):
        return lambda t: ((L, w), BF16, (t, w), lambda i, j, k: (i, 0))

    def norm1_pro(x_ref, ex, outs, j):
        @pl.when(j == 0)
        def _():
            xv = x_ref[...]
            r = lax.rsqrt(jnp.mean(xv * xv, axis=-1, keepdims=True) + EPS)
            outs[1][...] = (xv * r * ex[0][...] * (1.0 + ex[1][...]) + ex[2][...]).astype(outs[1].dtype)

        return outs[1][...]

    def proj_ep(acc, ex, outs):
        outs[0][...] = acc

        @pl.when(pl.program_id(1) == NPROJ // 768 - 1)
        def _():
            pre = acc[:, 768 - DT_PAD:768 - DT_PAD + 128] + ex[3][...]
            outs[2][...] = jnp.concatenate([_softplus(pre), _sigmoid(pre)], axis=1)

    proj, h1, dtp = _mm(
        xs_, w_in_t, "nt", name="in_proj", tm=tm2, tn=768, tk=D,
        extras=[(norm_mix_w, *_vecs()), (scale_m, *_vecs()), (shift_m, *_vecs()), (dtb, *_vecs(128)),
                _dep(gather_token)],
        outs=[F32, whole_rows(D)(tm2), ((L, DT_PAD), F32, (tm2, DT_PAD), lambda i, j, k: (i, 0))],
        prologue=norm1_pro, epilogue=proj_ep)
    xbc_raw = proj
    y_ssm, hs, xbc = _ssd_fwd(xbc_raw, dtp, conv_full, conv_b, arow, dsk_x, name="ssd_fwd")

    slab_rest, gs = _xchg_wait(rest_started, y_ssm, per_peer=False, name="gather_rest_wait")
    gs = lax.dynamic_update_slice(gs, slab_rest[None], (me, 0, 0))

    def part(n, rows):
        return gs[:, REST_OFF[n]:REST_OFF[n] + rows]

    w_bssd = part("bssd", 256).reshape(DI, D)
    w_pool = part("pool", 32).reshape(N_DEV, 4, 32, PGW).transpose(1, 0, 2, 3).reshape(POOL_W, PGW)
    w_bpool = part("bpool", 128).reshape(POOL_W, D)
    w_o = part("out", 128).reshape(D, D)
    w_up_t = part("up", 512).reshape(DFF, D)
    w_dn = part("down", 512).reshape(DFF, D)

    def gnorm_pro(y_ref, ex, outs, j):
        z_ref, w_ref = ex
        yg = y_ref[...].astype(F32) * _silu(z_ref[...].astype(F32))
        segs = []
        for k in range(NG):
            sl = slice(k * GW, (k + 1) * GW)
            seg = yg[:, sl]
            r = lax.rsqrt(jnp.mean(seg * seg, axis=-1, keepdims=True) + EPS)
            segs.append((seg * r * w_ref[:, sl]).astype(BF16))
        yn_v = jnp.concatenate(segs, axis=1)
        outs[1][...] = yn_v
        return yn_v

    y_ssd, yn = _mm(y_ssm, w_bssd, "nn", name="branch_ssd", tm=tmh, tn=D, tk=DI,
                    extras=[(proj, *_rows(tmh, DI, zcol)), (ssd_norm_w, *_vecs(DI))],
                    outs=[BF16, whole_rows(DI)(tmh)], prologue=gnorm_pro)
    pooled = _pool_fwd(proj, name="pool_fwd")
    wp_spec = ((POOL_W, PGW), lambda i, j, k: (0, 0))

    def pool_pro(a_ref, ex, outs, j):
        wp_ref, s_ref = ex
        segs = []
        for g in range(4):
            sl = slice(g * PGW, (g + 1) * PGW)
            p = _dot(a_ref[:, sl], wp_ref[sl, :], NN)
            outs[1][:, sl] = p.astype(BF16)
            segs.append((p * s_ref[:, sl]).astype(BF16))
        yp1_v = jnp.concatenate(segs, axis=1)
        outs[2][...] = yp1_v
        return yp1_v

    y_pool, yp0, yp1 = _mm(pooled, w_bpool, "nn", name="branch_pool", tm=tm, tn=D, tk=D,
                           extras=[(w_pool, *wp_spec), (pool_scale, *_vecs())],
                           outs=[BF16, whole_rows(D)(tm), whole_rows(D)(tm)], prologue=pool_pro)

    def merge_pro(a_ref, ex, outs, j):
        s = _sigmoid(ex[1][...].astype(F32))
        mv = (s[:, :D] * a_ref[...].astype(F32) + s[:, D:] * ex[0][...].astype(F32)).astype(BF16)
        outs[3][...] = mv
        return mv

    mix, x1, h2, m = _mm(y_ssd, w_o, "nn", name="out_proj", tm=tmh, tn=D, tk=D,
                         extras=[(y_pool, *_rows(tmh)), (proj, *_rows(tmh, 2 * D, gcol)),
                                 (xs_, *_rows(tmh)), (gate_m, *_vecs()), (norm_mlp_w, *_vecs()),
                                 (scale_f, *_vecs()), (shift_f, *_vecs())],
                         outs=[BF16, F32, BF16, whole_rows(D)(tmh)], prologue=merge_pro,
                         epilogue=lambda acc, ex, outs: _ep_resid_norm(acc, ex[2:], outs[:3]))

    def relu2(acc, ex, outs):
        r = jnp.maximum(acc, 0.0)
        outs[0][...] = acc.astype(BF16)
        outs[1][...] = (r * r).astype(BF16)

    up, act = _mm(h2, w_up_t, "nt", name="mlp_up", outs=[BF16, BF16], tm=tm2, tn=1024, tk=D, epilogue=relu2)

    dx2, ddown, loss_p, dnwf, dgate_f = _mm(
        act, w_dn, "nn", name="mlp_down", tm=tmh, tn=D, tk=DFF,
        extras=[(x1, *_rows(tmh)), (tgt, *_rows(tmh)), (gate_f, *_vecs()), (norm_final_w.reshape(1, D), *_vecs())],
        outs=[F32, BF16, _sum_out(128), _sum_out(), _sum_out()], epilogue=_ep_final)

    def drelu2(acc, ex, outs):
        outs[0][...] = (acc * (2.0 * jnp.maximum(ex[0][...].astype(F32), 0.0))).astype(BF16)

    def dep_last(ep):
        return lambda acc, ex, outs: ep(acc, ex[:-1], outs)

    dup = _mm(ddown, w_dn, "nt", name="mlp_down_dx", outs=[BF16], tm=tm2, tn=1024, tk=D,
              extras=[(up, (tm2, 1024), lambda i, j, k: (i, j))], epilogue=drelu2)
    g_dn = _mm(act, ddown, "tn", name="mlp_down_dw", outs=[SLAB_DT], tm=1024, tn=D, tk=tkl)
    g_up_t = _mm(dup, h2, "tn", name="mlp_up_dw", outs=[SLAB_DT], tm=1024, tn=D, tk=tkl)
    gslab_mlp = jnp.concatenate([g_up_t.reshape(N_DEV, 512, D), g_dn.reshape(N_DEV, 512, D)], axis=1)
    mlp_started = _xchg_start(gslab_mlp, per_peer=True, name="scatter_mlp_start")
    dx1, p2, q2, dmix, dgate_m = _mm(
        dup, w_up_t, "nn", name="mlp_up_dx", tm=tmh, tn=D, tk=DFF,
        extras=[(x1, *_rows(tmh)), (dx2, *_rows(tmh)), (norm_mlp_w, *_vecs()), (scale_f, *_vecs()),
                (mix, *_rows(tmh)), (gate_m, *_vecs()), _dep(mlp_started[4])],
        outs=[F32, _sum_out(), _sum_out(), BF16, _sum_out()], epilogue=dep_last(_ep_norm_bwd))
    gcol = C_GATE // (2 * D)
    dy_ssd, dy_pool, dproj = _mm(
        dmix, w_o, "nt", name="out_proj_dx", tm=tmh, tn=D, tk=D,
        extras=[(y_ssd, *_rows(tmh)), (y_pool, *_rows(tmh)), (proj, *_rows(tmh, 2 * D, gcol))],
        outs=[BF16, BF16, ((L, NPROJ), BF16, *_rows(tmh, 2 * D, gcol))], epilogue=_ep_merge_bwd)
    g_o = _mm(m, dmix, "tn", name="out_proj_dw", outs=[SLAB_DT], tm=D, tn=D, tk=tkl)
    zcol = C_Z // DI
    dy_ssm, dproj, d_snw = _mm(
        dy_ssd, w_bssd, "nt", name="branch_ssd_dx", tm=tmh, tn=DI, tk=D,
        extras=[(y_ssm, *_rows(tmh, DI)), (proj, *_rows(tmh, DI, zcol)), (ssd_norm_w, *_vecs(DI)),
                (dproj, None, None)],
        outs=[F32, ((L, NPROJ), BF16, *_rows(tmh, DI, zcol)), _sum_out(DI)],
        epilogue=_ep_gated_norm_bwd, aliases={3: 1})
    g_bssd = _mm(yn, dy_ssd, "tn", name="branch_ssd_dw", outs=[SLAB_DT], tm=1024, tn=D, tk=tkl)
    dxbc, dproj, d_a, d_dx, d_dtb = _ssd_bwd(dy_ssm, xbc, dtp, hs, arow, dsk_x, dproj, name="ssd_bwd")
    dproj, d_cw, d_cb = _conv_bwd(xbc_raw, dxbc, conv_full, conv_b, dproj, name="conv_bwd")
    def pool_bwd_ep(acc, ex, outs):
        y_ref, s_ref, wp_ref = ex
        o_ref, ds_ref, dpool_ref = outs
        dyp0_v = (acc * s_ref[...]).astype(BF16)
        o_ref[...] = dyp0_v
        _acc_out(ds_ref, _colsum(acc * y_ref[...].astype(F32)), _row_step())
        for g in range(4):
            sl = slice(g * PGW, (g + 1) * PGW)
            dpool_ref[:, sl] = _dot(dyp0_v[:, sl], wp_ref[sl, :], NT)

    dyp0, d_ps, dpooled = _mm(dy_pool, w_bpool, "nt", name="branch_pool_dx", tm=tm, tn=D, tk=D,
                              extras=[(yp0, *_rows(tm)), (pool_scale, *_vecs()), (w_pool, *wp_spec)],
                              outs=[BF16, _sum_out(), F32], epilogue=pool_bwd_ep)
    g_bpool = _mm(yp1, dy_pool, "tn", name="branch_pool_dw", outs=[SLAB_DT], tm=D, tn=D, tk=tkl)
    g_pool = _mm_pool_tn(pooled, dyp0, name="pool_mix_dw", tk=tkl)
    gslab_mix = jnp.concatenate([
        g_bssd.reshape(N_DEV, 256, D),
        g_pool.reshape(4, N_DEV, 32, PGW).transpose(1, 0, 2, 3).reshape(N_DEV, 32, D).astype(SLAB_DT),
        g_bpool.reshape(N_DEV, 128, D),
        g_o.reshape(N_DEV, 128, D)], axis=1)
    mix_started = _xchg_start(gslab_mix, per_peer=True, name="scatter_mix_start")
    dproj = _pool_bwd(dpooled, dproj, name="pool_bwd")
    g_in_t = _mm(dproj, h1, "tn", name="in_proj_dw", outs=[SLAB_DT], tm=1408, tn=D, tk=tkl2,
                 extras=[_dep(mix_started[4])])
    gslab_in = _restore_in_shards(g_in_t)
    in_started = _xchg_start(gslab_in, per_peer=True, name="scatter_in_start")
    grad_x, p1, q1 = _mm(
        dproj, w_in_t, "nn", name="in_proj_dx", tm=tmh, tn=D, tk=2816,
        extras=[(xs_, *_rows(tmh)), (dx1, *_rows(tmh)), (norm_mix_w, *_vecs()), (scale_m, *_vecs()),
                _dep(in_started[4])],
        outs=[F32, _sum_out(), _sum_out()], epilogue=dep_last(_ep_norm_bwd))

    def landed(started, after, tile, name):
        src, land = _xchg_wait(started, after, per_peer=True, name=name + "_wait")
        own = lax.dynamic_slice_in_dim(src, me, 1, axis=0)
        return _slab_sum(lax.dynamic_update_slice(land, own, (me, 0, 0)), tile=tile, name=name + "_sum")

    gsum_mlp = landed(mlp_started, grad_x, 256, "scatter_mlp")
    gsum_mix = landed(mix_started, grad_x, 272, "scatter_mix")
    gsum_in = landed(in_started, grad_x, 208, "scatter_in")

    dmod = jnp.concatenate([q1, p1 * norm_mix_w, dgate_m, q2, p2 * norm_mlp_w, dgate_f], axis=1)
    d_alog = d_a[:, :NH] * (-jnp.exp(a_log))
    sv = _pack_sv({
        "b_ada": dmod, "norm_mix_w": p1 * (1.0 + scale_m), "conv_b": d_cb, "dt_bias": d_dtb[:, :NH],
        "a_log": d_alog, "d_skip": d_dx.reshape(NH, HP).sum(axis=1), "ssd_norm_w": d_snw,
        "pool_scale": d_ps, "norm_mlp_w": p2 * (1.0 + scale_f), "norm_final_w": dnwf, "conv_w": d_cw,
        "loss": loss_p[:, :1]})
    sv_all, sv_sum = _small_allsum(sv, name="small_allsum")
    flat = sv_sum.reshape(-1)
    loss = flat[SV_OFF["loss"]]
    dmod_all = sv_all.reshape(N_DEV, SV_ROWS * 128)[:, :6 * D]
    g_w_ada = _ada_bwd(c_all, lax.dynamic_slice_in_dim(dmod_all, me * wloc, wloc, axis=1), name="ada_bwd")

    g_conv_w = lax.dynamic_slice_in_dim(_sv_get(flat, "conv_w", 4 * XBC).reshape(4, XBC),
                                        me * (XBC // N_DEV), XBC // N_DEV, axis=1)
    small = [("b_ada", b_ada, m_b_ada, v_b_ada), ("norm_mix_w", norm_mix_w, m_norm_mix_w, v_norm_mix_w),
             ("conv_b", conv_b, m_conv_b, v_conv_b), ("dt_bias", dt_bias, m_dt_bias, v_dt_bias),
             ("a_log", a_log, m_a_log, v_a_log), ("d_skip", d_skip, m_d_skip, v_d_skip),
             ("ssd_norm_w", ssd_norm_w, m_ssd_norm_w, v_ssd_norm_w),
             ("pool_scale", pool_scale, m_pool_scale, v_pool_scale),
             ("norm_mlp_w", norm_mlp_w, m_norm_mlp_w, v_norm_mlp_w),
             ("norm_final_w", norm_final_w[None], m_norm_final_w[None], v_norm_final_w[None]),
             ("conv_w", conv_w[0], m_conv_w[0], v_conv_w[0])]
    small_out = _adamw_small(sv_sum.reshape(1, SV_ROWS * 128), g_conv_w, small, name="adamw_small")
    small_out["norm_final_w"] = tuple(a[0] for a in small_out["norm_final_w"])
    small_out["conv_w"] = tuple(a[None] for a in small_out["conv_w"])

    def gpart(n, rows_):
        return gsum_mix[MIX_OFF[n]:MIX_OFF[n] + rows_]

    def lin(a):
        return a[0].T.reshape(IN_ROWS * 8, 128)

    g_lin = lax.dynamic_slice_in_dim(gsum_in, in_shift, IN_ROWS, axis=0).reshape(IN_ROWS * 8, 128)
    dlt, mn, vn = _adamw(lin(w_in), g_lin, lin(m_w_in), lin(v_w_in), name="adamw_w_in", tr=IN_ROWS * 2)
    big_in = tuple(a.reshape(IN_ROWS, D).T[None] for a in (g_lin, dlt, mn, vn))

    big = {
        "w_ada": (w_ada, m_w_ada, v_w_ada, g_w_ada, (D, wloc)),
        "w_branch_ssd": (w_branch_ssd, m_w_branch_ssd, v_w_branch_ssd, gpart("bssd", 256), (256, D)),
        "pool_w": (pool_w, m_pool_w, v_pool_w, gpart("pool", 32).reshape(128, PGW), (128, PGW)),
        "w_branch_pool": (w_branch_pool, m_w_branch_pool, v_w_branch_pool, gpart("bpool", 128), (128, D)),
        "w_out": (w_out, m_w_out, v_w_out, gpart("out", 128), (128, D)),
        "w_up": (w_up, m_w_up, v_w_up, gsum_mlp[:512].T, (D, 512)),
        "w_down": (w_down, m_w_down, v_w_down, gsum_mlp[512:], (512, D)),
    }
    big_out = {}
    for n, (w, mm_, vv, g, shp2) in big.items():
        dlt, mn, vn = _adamw(w.reshape(shp2), g, mm_.reshape(shp2), vv.reshape(shp2), name="adamw_" + n)
        big_out[n] = (g.reshape(w.shape), dlt.reshape(w.shape), mn.reshape(w.shape), vn.reshape(w.shape))

    order = ["w_ada", "b_ada", "norm_mix_w", "w_in", "conv_w", "conv_b", "dt_bias", "a_log", "d_skip",
             "ssd_norm_w", "w_branch_ssd", "pool_w", "pool_scale", "w_branch_pool", "w_out", "norm_mlp_w",
             "w_up", "w_down", "norm_final_w"]
    big_out["w_in"] = big_in
    res = {**small_out, **big_out}
    outs = [loss, grad_x.reshape(x.shape)]
    for k in range(4):
        outs += [res[n][k] for n in order]
    return tuple(outs)
```

```python
import functools

import numpy as np
import jax
import jax.numpy as jnp
from jax import lax
from jax.experimental import pallas as pl
from jax.experimental.pallas import tpu as pltpu

F32 = jnp.float32
BF16 = jnp.bfloat16
SLAB_DT = jnp.bfloat16
_MXU_DTYPE = jnp.bfloat16

N_DEV = 8
D = 1024
DI = 2048
NH = 32
HP = 64
NG = 4
NS = 128
Q = 128
XBC = DI + 2 * NG * NS
DFF = 4096
N_IN = 8224
EPS = 1e-5
POOL_W = 1024
PGW = 256

C_XBC, C_POOL, C_Z, C_GATE, C_DT = 0, 3072, 4096, 6144, 8192
DT_PAD = 256
NPROJ = C_DT + DT_PAD

IN_ROWS = N_IN // N_DEV
IN_ROWS_P = 1040
CONV_ROWS = 16
REST_PARTS = (("bssd", 256), ("pool", 32), ("bpool", 128), ("out", 128), ("up", 512), ("down", 512))
REST_OFF = {}
_o = 0
for _n, _r in REST_PARTS:
    REST_OFF[_n] = _o
    _o += _r
REST_ROWS = _o
MIX_PARTS = (("bssd", 256), ("pool", 32), ("bpool", 128), ("out", 128))
MIX_OFF = {}
_o = 0
for _n, _r in MIX_PARTS:
    MIX_OFF[_n] = _o
    _o += _r
MIX_ROWS = _o

SV_PARTS = (("b_ada", 6144), ("norm_mix_w", 1024), ("conv_b", 3072), ("dt_bias", 128), ("a_log", 128),
            ("d_skip", 128), ("ssd_norm_w", 2048), ("pool_scale", 1024), ("norm_mlp_w", 1024),
            ("norm_final_w", 1024), ("conv_w", 4 * XBC), ("loss", 128))
SV_OFF = {}
_o = 0
for _n, _r in SV_PARTS:
    SV_OFF[_n] = _o
    _o += _r
SV_ROWS = 224
assert _o <= SV_ROWS * 128

ADAM_LR, ADAM_B1, ADAM_B2, ADAM_EPS, ADAM_WD, ADAM_STEP = 0.001, 0.9, 0.999, 1e-08, 0.01, 10

VMEM_BIG = 56 * 1024 * 1024
NEG = -1e30

NN = ((1,), (0,))
NT = ((1,), (1,))
TN = ((0,), (0,))


def _dot(a, b, dims=NN):
    return lax.dot_general(a.astype(_MXU_DTYPE), b.astype(_MXU_DTYPE), (dims, ((), ())),
                           preferred_element_type=F32)


def _dot_hi(a, b, dims=NN):
    return lax.dot_general(a.astype(F32), b.astype(F32), (dims, ((), ())),
                           precision=lax.Precision.HIGHEST, preferred_element_type=F32)


def _pick(n, cands):
    for c in cands:
        if n % c == 0:
            return c
    return n


def _sigmoid(x):
    return 1.0 / (1.0 + jnp.exp(-x))


def _silu(x):
    return x * _sigmoid(x)


def _dsilu(x):
    s = _sigmoid(x)
    return s * (1.0 + x * (1.0 - s))


def _softplus(x):
    return jnp.maximum(x, 0.0) + jnp.log(1.0 + jnp.exp(-jnp.abs(x)))


def _params(sem, vmem=None):
    return pltpu.CompilerParams(dimension_semantics=sem, vmem_limit_bytes=vmem)


def _row_step():
    return pl.program_id(0)


def _mm(a, b, mode, *, name, outs, tm, tn, tk, extras=(), epilogue=None, aliases=None, prologue=None):
    if mode == "tn":
        K, M = a.shape
        N = b.shape[1]
        a_spec = pl.BlockSpec((tk, tm), lambda i, j, k: (k, i))
        b_spec = pl.BlockSpec((tk, tn), lambda i, j, k: (k, j))
        dims = TN
    else:
        M = a.shape[0]
        K = b.shape[0] if mode == "nn" else b.shape[1]
        if prologue is None:
            assert a.shape[1] == K
            a_spec = pl.BlockSpec((tm, tk), lambda i, j, k: (i, k))
        else:
            assert tk == K
            a_spec = pl.BlockSpec((tm, a.shape[1]), lambda i, j, k: (i, 0))
        if mode == "nn":
            N = b.shape[1]
            b_spec = pl.BlockSpec((tk, tn), lambda i, j, k: (k, j))
            dims = NN
        else:
            N = b.shape[0]
            b_spec = pl.BlockSpec((tn, tk), lambda i, j, k: (j, k))
            dims = NT
    assert M % tm == 0 and N % tn == 0 and K % tk == 0, (name, M, N, K, tm, tn, tk)
    nk = K // tk
    ne, no = len(extras), len(outs)
    if epilogue is None:
        def epilogue(acc, ex, out_refs):
            out_refs[0][...] = acc.astype(out_refs[0].dtype)

    def body(a_ref, b_ref, *rest):
        ex, out_refs = rest[:ne], rest[ne:ne + no]
        lhs = a_ref[...] if prologue is None else prologue(a_ref, ex, out_refs, pl.program_id(1))
        p = _dot(lhs, b_ref[...], dims)
        if nk == 1:
            epilogue(p, ex, out_refs)
        else:
            acc = rest[-1]
            k = pl.program_id(2)

            @pl.when(k == 0)
            def _():
                acc[...] = p

            @pl.when(jnp.logical_and(k > 0, k < nk - 1))
            def _():
                acc[...] += p

            @pl.when(k == nk - 1)
            def _():
                epilogue(acc[...] + p, ex, out_refs)

    out_specs, out_shape = [], []
    for o in outs:
        if isinstance(o, tuple):
            shape, dt, bs, im = o
            out_specs.append(pl.BlockSpec(bs, im))
            out_shape.append(jax.ShapeDtypeStruct(shape, dt))
        else:
            out_specs.append(pl.BlockSpec((tm, tn), lambda i, j, k: (i, j)))
            out_shape.append(jax.ShapeDtypeStruct((M, N), o))
    in_specs = [a_spec, b_spec]
    for _, bs, im in extras:
        in_specs.append(pl.BlockSpec(memory_space=pl.ANY) if bs is None else pl.BlockSpec(bs, im))
    res = pl.pallas_call(
        body, name=name,
        grid=(M // tm, N // tn, nk),
        in_specs=in_specs, out_specs=out_specs, out_shape=out_shape,
        scratch_shapes=[pltpu.VMEM((tm, tn), F32)] if nk > 1 else [],
        input_output_aliases={2 + e: o for e, o in (aliases or {}).items()},
        compiler_params=_params(("arbitrary", "arbitrary", "arbitrary"), VMEM_BIG),
    )(a, b, *[e[0] for e in extras])
    return res if no > 1 else res[0]


def _rows(tm, w=D, col=0):
    return (tm, w), lambda i, j, k, c=col: (i, c)


def _vecs(w=D, col=0):
    return (1, w), lambda i, j, k, c=col: (0, c)


def _sum_out(w=D):
    return ((1, w), F32, (1, w), lambda i, j, k: (0, 0))


def _mm_pool_tn(a, b, *, name, tk):
    L = a.shape[0]

    def body(a_ref, b_ref, o_ref):
        p = _dot(a_ref[...], b_ref[...], TN)

        @pl.when(pl.program_id(1) == 0)
        def _():
            o_ref[...] = p

        @pl.when(pl.program_id(1) > 0)
        def _():
            o_ref[...] += p

    blk = pl.BlockSpec((tk, PGW), lambda g, k: (k, g))
    return pl.pallas_call(body, name=name, grid=(4, L // tk), in_specs=[blk, blk],
                          out_specs=pl.BlockSpec((PGW, PGW), lambda g, k: (g, 0)),
                          out_shape=jax.ShapeDtypeStruct((POOL_W, PGW), F32),
                          compiler_params=_params(("parallel", "arbitrary")))(a, b)


def _acc_out(ref, val, i):
    @pl.when(i == 0)
    def _():
        ref[...] = val

    @pl.when(i > 0)
    def _():
        ref[...] += val


def _colsum(v):
    return jnp.sum(v, axis=0, keepdims=True)


def _ep_resid_norm(acc, ex, outs):
    x_ref, g_ref, nw_ref, sc_ref, sh_ref = ex
    mix_ref, x1_ref, h_ref = outs
    mix_ref[...] = acc.astype(mix_ref.dtype)
    xv = x_ref[...] + g_ref[...] * acc
    x1_ref[...] = xv
    r = lax.rsqrt(jnp.mean(xv * xv, axis=-1, keepdims=True) + EPS)
    h_ref[...] = (xv * r * nw_ref[...] * (1.0 + sc_ref[...]) + sh_ref[...]).astype(h_ref.dtype)


def _ep_final(acc, ex, outs):
    x1_ref, t_ref, g_ref, nw_ref = ex
    dx2_ref, dd_ref, loss_ref, dnw_ref, dg_ref = outs
    i = _row_step()
    x2 = x1_ref[...] + g_ref[...] * acc
    r = lax.rsqrt(jnp.mean(x2 * x2, axis=-1, keepdims=True) + EPS)
    xh = x2 * r
    e = xh * nw_ref[...] - t_ref[...]
    part = 0.5 * jnp.sum(jnp.mean(e * e, axis=-1, keepdims=True), axis=0, keepdims=True)
    dy = e * (1.0 / D)
    g = dy * nw_ref[...]
    dx2 = r * (g - xh * jnp.mean(g * xh, axis=-1, keepdims=True))
    dx2_ref[...] = dx2
    dd_ref[...] = (dx2 * g_ref[...]).astype(dd_ref.dtype)
    _acc_out(loss_ref, jnp.broadcast_to(part, (1, 128)), i)
    _acc_out(dnw_ref, _colsum(dy * xh), i)
    _acc_out(dg_ref, _colsum(dx2 * acc), i)


def _ep_norm_bwd(acc, ex, outs):
    x_ref, dr_ref, nw_ref, sc_ref = ex[:4]
    dx_ref, p_ref, q_ref = outs[:3]
    i = _row_step()
    xv = x_ref[...]
    r = lax.rsqrt(jnp.mean(xv * xv, axis=-1, keepdims=True) + EPS)
    xh = xv * r
    g = acc * (nw_ref[...] * (1.0 + sc_ref[...]))
    dx = dr_ref[...] + r * (g - xh * jnp.mean(g * xh, axis=-1, keepdims=True))
    dx_ref[...] = dx
    _acc_out(p_ref, _colsum(acc * xh), i)
    _acc_out(q_ref, _colsum(acc), i)
    if len(ex) > 4:
        m_ref, g_ref = ex[4:]
        dm_ref, dg_ref = outs[3:]
        dm_ref[...] = (dx * g_ref[...]).astype(dm_ref.dtype)
        _acc_out(dg_ref, _colsum(dx * m_ref[...].astype(F32)), i)


def _ep_merge_bwd(acc, ex, outs):
    a_ref, b_ref, gl_ref = ex
    da_ref, db_ref, dgl_ref = outs
    s = _sigmoid(gl_ref[...].astype(F32))
    s1, s2 = s[:, :D], s[:, D:]
    da_ref[...] = (acc * s1).astype(da_ref.dtype)
    db_ref[...] = (acc * s2).astype(db_ref.dtype)
    dgl_ref[:, :D] = (acc * a_ref[...].astype(F32) * s1 * (1.0 - s1)).astype(dgl_ref.dtype)
    dgl_ref[:, D:] = (acc * b_ref[...].astype(F32) * s2 * (1.0 - s2)).astype(dgl_ref.dtype)


GW = DI // NG


def _ep_gated_norm_bwd(acc, ex, outs):
    y_ref, z_ref, w_ref, _ = ex
    dy_ref, dz_ref, dw_ref = outs
    zv = z_ref[...].astype(F32)
    yv = y_ref[...].astype(F32)
    sg = _sigmoid(zv)
    sz = zv * sg
    yg = yv * sz
    dsz = sg * (1.0 + zv * (1.0 - sg))
    dws = []
    for k in range(NG):
        sl = slice(k * GW, (k + 1) * GW)
        seg = yg[:, sl]
        r = lax.rsqrt(jnp.mean(seg * seg, axis=-1, keepdims=True) + EPS)
        sh = seg * r
        dn = acc[:, sl]
        g = dn * w_ref[:, sl]
        dyg = r * (g - sh * jnp.mean(g * sh, axis=-1, keepdims=True))
        dy_ref[:, sl] = dyg * sz[:, sl]
        dz_ref[:, sl] = (dyg * yv[:, sl] * dsz[:, sl]).astype(dz_ref.dtype)
        dws.append(_colsum(dn * sh))
    _acc_out(dw_ref, jnp.concatenate(dws, axis=1), _row_step())


CONV_CB = 128
HALO = 16


def _time_chunk(L):
    return _pick(L, (256, 128))


def _with_halo(x_ref, i, r0, rc):
    p0 = pl.multiple_of(jnp.maximum(r0 - HALO, 0), HALO)
    prev = jnp.where(i > 0, x_ref[pl.ds(p0, HALO), :].astype(F32), 0.0)
    return jnp.concatenate([prev, x_ref[pl.ds(r0, rc), :].astype(F32)], axis=0)


def _conv_bwd(proj, dy, w, b, dproj, *, name):
    L = proj.shape[0]
    rc = _time_chunk(L)
    n = L // rc

    def body(x_ref, dy_ref, w_ref, b_ref, dp_in, dx_ref, dw_ref, db_ref, xpad, dpad):
        del dp_in
        wv = w_ref[...]
        bv = b_ref[...]
        dpad[rc:rc + HALO, :] = jnp.zeros((HALO, CONV_CB), F32)

        def step(k, carry):
            db, d0, d1, d2, d3 = carry
            i = n - 1 - k
            r0 = pl.multiple_of(i * rc, rc)
            p0 = pl.multiple_of(jnp.maximum(r0 - HALO, 0), HALO)
            xpad[0:HALO, :] = jnp.where(i > 0, x_ref[pl.ds(p0, HALO), :].astype(F32), 0.0)
            xpad[HALO:HALO + rc, :] = x_ref[pl.ds(r0, rc), :].astype(F32)
            xk = [xpad[HALO - j:HALO - j + rc, :] for j in range(4)]
            pre = bv
            for j in range(4):
                pre = pre + xk[j] * wv[3 - j:4 - j]
            dpre = dy_ref[pl.ds(r0, rc), :] * _dsilu(pre)
            dpad[0:rc, :] = dpre
            acc = dpre * wv[3:4]
            for j in (1, 2, 3):
                acc = acc + dpad[j:j + rc, :] * wv[3 - j:4 - j]
            dx_ref[pl.ds(r0, rc), :] = acc.astype(dx_ref.dtype)
            dpad[rc:rc + HALO, :] = dpre[:HALO]
            return (db + _colsum(dpre), d0 + _colsum(dpre * xk[3]), d1 + _colsum(dpre * xk[2]),
                    d2 + _colsum(dpre * xk[1]), d3 + _colsum(dpre * xk[0]))

        z = jnp.zeros((1, CONV_CB), F32)
        db, d0, d1, d2, d3 = lax.fori_loop(0, n, step, (z, z, z, z, z))
        db_ref[...] = db
        dw_ref[...] = jnp.concatenate([d0, d1, d2, d3], axis=0)

    nb = XBC // CONV_CB
    return pl.pallas_call(
        body, name=name, grid=(nb,),
        in_specs=[pl.BlockSpec((L, CONV_CB), lambda j: (0, j + C_XBC // CONV_CB)),
                  pl.BlockSpec((L, CONV_CB), lambda j: (0, j)),
                  pl.BlockSpec((4, CONV_CB), lambda j: (0, j)), pl.BlockSpec((1, CONV_CB), lambda j: (0, j)),
                  pl.BlockSpec(memory_space=pl.ANY)],
        out_specs=[pl.BlockSpec((L, CONV_CB), lambda j: (0, j + C_XBC // CONV_CB)),
                   pl.BlockSpec((4, CONV_CB), lambda j: (0, j)), pl.BlockSpec((1, CONV_CB), lambda j: (0, j))],
        out_shape=[jax.ShapeDtypeStruct((L, NPROJ), BF16), jax.ShapeDtypeStruct((4, XBC), F32),
                   jax.ShapeDtypeStruct((1, XBC), F32)],
        scratch_shapes=[pltpu.VMEM((rc + HALO, CONV_CB), F32), pltpu.VMEM((rc + HALO, CONV_CB), F32)],
        input_output_aliases={4: 0},
        compiler_params=_params(("parallel",), VMEM_BIG))(proj, dy, w, b, dproj)


def _pool_fwd(proj, *, name):
    L = proj.shape[0]
    rc = _time_chunk(L)
    n = L // rc

    def body(x_ref, o_ref, pad):
        g = pl.program_id(0)
        pad[0:HALO, :] = jnp.zeros((HALO, PGW), F32)

        def fill(i, c):
            r0 = pl.multiple_of(i * rc, rc)
            pad[pl.ds(r0 + HALO, rc), :] = x_ref[pl.ds(r0, rc), :].astype(F32)
            return c

        lax.fori_loop(0, n, fill, 0)
        rows = lax.broadcasted_iota(jnp.int32, (rc, PGW), 0)

        for gi in range(4):
            win = 2 << gi

            @pl.when(g == gi)
            def _(gi=gi, win=win):
                def step(i, c):
                    r0 = pl.multiple_of(i * rc, rc)
                    ext = pad[pl.ds(r0, rc + HALO), :]
                    s = ext
                    sh = 1
                    while sh < win:
                        s = s + pltpu.roll(s, sh, 0)
                        sh *= 2
                    cnt = jnp.minimum(rows + (r0 + 1), win).astype(F32)
                    o_ref[pl.ds(r0, rc), :] = (s[HALO:] / cnt - ext[HALO:]).astype(o_ref.dtype)
                    return c

                lax.fori_loop(0, n, step, 0)

    return pl.pallas_call(
        body, name=name, grid=(4,),
        in_specs=[pl.BlockSpec((L, PGW), lambda j: (0, j + C_POOL // PGW))],
        out_specs=pl.BlockSpec((L, PGW), lambda j: (0, j)),
        out_shape=jax.ShapeDtypeStruct((L, POOL_W), BF16),
        scratch_shapes=[pltpu.VMEM((L + HALO, PGW), F32)],
        compiler_params=_params(("parallel",), VMEM_BIG))(proj)


def _pool_bwd(dpooled, dproj, *, name):
    L = dpooled.shape[0]
    rc = _time_chunk(L)
    n = L // rc

    def body(d_ref, dp_in, o_ref, pad):
        del dp_in
        g = pl.program_id(0)
        pad[L:L + HALO, :] = jnp.zeros((HALO, PGW), F32)
        rows = lax.broadcasted_iota(jnp.int32, (rc, PGW), 0)

        for gi in range(4):
            win = 2 << gi

            @pl.when(g == gi)
            def _(gi=gi, win=win):
                def fill(i, c):
                    r0 = pl.multiple_of(i * rc, rc)
                    cnt = jnp.minimum(rows + (r0 + 1), win).astype(F32)
                    pad[pl.ds(r0, rc), :] = d_ref[pl.ds(r0, rc), :] / cnt
                    return c

                lax.fori_loop(0, n, fill, 0)

                def step(i, c):
                    r0 = pl.multiple_of(i * rc, rc)
                    s = pad[pl.ds(r0, rc + HALO), :]
                    sh = 1
                    while sh < win:
                        s = s + pltpu.roll(s, rc + HALO - sh, 0)
                        sh *= 2
                    o_ref[pl.ds(r0, rc), :] = (s[:rc] - d_ref[pl.ds(r0, rc), :]).astype(o_ref.dtype)
                    return c

                lax.fori_loop(0, n, step, 0)

    return pl.pallas_call(
        body, name=name, grid=(4,),
        in_specs=[pl.BlockSpec((L, PGW), lambda j: (0, j)), pl.BlockSpec(memory_space=pl.ANY)],
        out_specs=pl.BlockSpec((L, PGW), lambda j: (0, j + C_POOL // PGW)),
        out_shape=jax.ShapeDtypeStruct((L, NPROJ), BF16),
        scratch_shapes=[pltpu.VMEM((L + HALO, PGW), F32)],
        input_output_aliases={1: 0},
        compiler_params=_params(("parallel",), VMEM_BIG))(dpooled, dproj)


_SPLIT_DT = jnp.bfloat16


def _ssd_consts():
    tri = np.tril(np.ones((Q, Q), np.float32))
    exp = np.zeros((128, DI), np.float32)
    for h in range(NH):
        exp[h, h * HP:(h + 1) * HP] = 1.0
    exp2 = np.concatenate([exp, exp], axis=0)
    return (jnp.asarray(tri, dtype=_SPLIT_DT), jnp.asarray(tri.T.copy(), dtype=_SPLIT_DT),
            jnp.asarray(exp2, dtype=_SPLIT_DT))


def _split(v, n):
    parts, r = [], v
    for _ in range(n):
        p = r.astype(_SPLIT_DT)
        parts.append(p)
        r = r - p.astype(F32)
    return parts


def _bdot(a, b, dims):
    return lax.dot_general(a, b, (dims, ((), ())), preferred_element_type=F32)


def _tri_sum(t_ref, v):
    r = _bdot(t_ref[...], jnp.concatenate(_split(v, 3), axis=1), NN)
    return r[:, :128] + r[:, 128:256] + r[:, 256:]


def _expand(v, e2_ref):
    return _bdot(jnp.concatenate(_split(v, 2), axis=1), e2_ref[...], NN)


def _reduce_heads(vals, eg):
    parts = []
    for v in vals:
        parts += _split(v, 2)
    r = _bdot(jnp.concatenate(parts, axis=0), eg, NT)
    return [r[2 * i * Q:(2 * i + 1) * Q] + r[(2 * i + 1) * Q:(2 * i + 2) * Q] for i in range(len(vals))]


def _ssd_common(xbc_ref, dtw_ref, arow_ref, t_ref, e_ref):
    dt = dtw_ref[:, :128]
    sig = dtw_ref[:, 128:]
    acs = _tri_sum(t_ref, dt * arow_ref[...])
    acs_x = _expand(acs, e_ref)
    dt_x = _expand(dt, e_ref)
    xs = xbc_ref[:, 0:DI]
    return sig, dt, acs, acs.T, acs_x, dt_x, xs


CONV_SLAB = 512


def _ssd_fwd(raw, dtp, cw, cb, arow, dsk_x, *, name):
    L = raw.shape[0]
    nc = L // Q
    tri, _, expand = _ssd_consts()

    def body(raw_ref, halo_ref, cw_ref, cb_ref, dtw_ref, arow_ref, dsk_ref, t_ref, e_ref,
             y_ref, hs_ref, xbc_ref, h_scr, cpad):
        c = pl.program_id(0)

        @pl.when(c == 0)
        def _():
            h_scr[...] = jnp.zeros_like(h_scr)

        cpad[0:8, :] = jnp.where(c > 0, halo_ref[...], 0.0)
        cpad[8:8 + Q, :] = raw_ref[...]
        for lo in range(0, XBC, CONV_SLAB):
            sl = slice(lo, lo + CONV_SLAB)
            acc = cb_ref[:, sl]
            for j in range(4):
                acc = acc + cpad[8 - j:8 - j + Q, sl] * cw_ref[3 - j:4 - j, sl]
            xbc_ref[:, sl] = acc * _sigmoid(acc)

        _, dt, acs, acs_t, acs_x, dt_x, xs = _ssd_common(xbc_ref, dtw_ref, arow_ref, t_ref, e_ref)
        xdt = xs * dt_x
        eacs = jnp.exp(acs_x)
        acs_last = acs_x[Q - 1:Q, :]
        dec = jnp.exp(acs_last - acs_x)
        hs_ref[0] = h_scr[...].astype(hs_ref.dtype)
        causal = lax.broadcasted_iota(jnp.int32, (Q, Q), 0) >= lax.broadcasted_iota(jnp.int32, (Q, Q), 1)
        first = lax.broadcasted_iota(jnp.int32, (Q, 128), 1) < HP
        for g in range(NG):
            bg = xbc_ref[:, DI + g * NS:DI + (g + 1) * NS]
            cg = xbc_ref[:, DI + NG * NS + g * NS:DI + NG * NS + (g + 1) * NS]
            s = _dot(cg, bg, NT)
            sl = slice(g * GW, (g + 1) * GW)
            hg = h_scr[:, sl]
            yoff = _dot(cg, hg, NN) * eacs[:, sl]
            st = _dot(bg, xdt[:, sl] * dec[:, sl], TN)
            h_scr[:, sl] = hg * eacs[Q - 1:Q, sl] + st
            for j in range(4):
                lo = g * GW + j * 128
                xb = xdt[:, lo:lo + 128]
                yp = yoff[:, j * 128:(j + 1) * 128] + dsk_ref[:, lo:lo + 128] * xs[:, lo:lo + 128]
                for e in range(2):
                    h = g * 8 + j * 2 + e
                    lm = jnp.exp(jnp.where(causal, acs[:, h:h + 1] - acs_t[h:h + 1, :], NEG))
                    xm = jnp.where(first if e == 0 else jnp.logical_not(first), xb, 0.0)
                    yp = yp + _dot(s * lm, xm, NN)
                y_ref[:, lo:lo + 128] = yp.astype(y_ref.dtype)

    const = lambda c: (0, 0)
    return pl.pallas_call(
        body, name=name, grid=(nc,),
        in_specs=[pl.BlockSpec((Q, XBC), lambda c: (c, 0)),
                  pl.BlockSpec((8, XBC), lambda c: (jnp.maximum(c * (Q // 8) - 1, 0), 0)),
                  pl.BlockSpec((4, XBC), const), pl.BlockSpec((1, XBC), const),
                  pl.BlockSpec((Q, DT_PAD), lambda c: (c, 0)),
                  pl.BlockSpec((1, 128), const), pl.BlockSpec((1, DI), const),
                  pl.BlockSpec((Q, Q), const), pl.BlockSpec((256, DI), const)],
        out_specs=[pl.BlockSpec((Q, DI), lambda c: (c, 0)), pl.BlockSpec((1, NS, DI), lambda c: (c, 0, 0)),
                   pl.BlockSpec((Q, XBC), lambda c: (c, 0))],
        out_shape=[jax.ShapeDtypeStruct((L, DI), BF16), jax.ShapeDtypeStruct((nc, NS, DI), F32),
                   jax.ShapeDtypeStruct((L, XBC), F32)],
        scratch_shapes=[pltpu.VMEM((NS, DI), F32), pltpu.VMEM((8 + Q, XBC), F32)],
        compiler_params=_params(("arbitrary",), VMEM_BIG))(raw, raw, cw, cb, dtp, arow, dsk_x, tri, expand)


def _ssd_bwd(dy, xbc, dtp, hs, arow, dsk_x, dproj, *, name):
    L = xbc.shape[0]
    nc = L // Q
    tri, triu, expand = _ssd_consts()

    def body(dy_ref, xbc_ref, dtw_ref, hs_ref, arow_ref, dsk_ref, t_ref, u_ref, e_ref, dp_in,
             dxbc_ref, ddtw_ref, da_ref, ddx_ref, ddtb_ref, dh_scr):
        del dp_in
        i = pl.program_id(0)

        @pl.when(i == 0)
        def _():
            dh_scr[...] = jnp.zeros_like(dh_scr)

        sig, dt, acs, acs_t, acs_x, dt_x, xs = _ssd_common(xbc_ref, dtw_ref, arow_ref, t_ref, e_ref)
        dyv = dy_ref[...]
        xdt = xs * dt_x
        eacs = jnp.exp(acs_x)
        acs_last = acs_x[Q - 1:Q, :]
        dec = jnp.exp(acs_last - acs_x)
        gy = dyv * eacs
        causal = lax.broadcasted_iota(jnp.int32, (Q, Q), 0) >= lax.broadcasted_iota(jnp.int32, (Q, Q), 1)
        first = lax.broadcasted_iota(jnp.int32, (Q, 128), 1) < HP
        lane_h = lax.broadcasted_iota(jnp.int32, (Q, 128), 1)
        sub_h = lax.broadcasted_iota(jnp.int32, (128, Q), 0)
        last_row = lax.broadcasted_iota(jnp.int32, (Q, GW), 0) == Q - 1
        dacs = jnp.zeros((Q, 128), F32)
        dacs_t = jnp.zeros((128, Q), F32)
        ddt = jnp.zeros((Q, 128), F32)
        for g in range(NG):
            bg = xbc_ref[:, DI + g * NS:DI + (g + 1) * NS]
            cg = xbc_ref[:, DI + NG * NS + g * NS:DI + NG * NS + (g + 1) * NS]
            s = _dot(cg, bg, NT)
            sl = slice(g * GW, (g + 1) * GW)
            hg = hs_ref[0, :, sl].astype(F32)
            dhn = dh_scr[:, sl]
            eal = eacs[Q - 1:Q, sl]
            gg = gy[:, sl]
            dax = gg * _dot(cg, hg, NN)
            dcg = _dot(gg, hg, NT)
            dh_scr[:, sl] = _dot(cg, gg, TN) + dhn * eal
            dal = eal * _colsum(dhn * hg)
            xdd = xdt[:, sl] * dec[:, sl]
            dbg = _dot(xdd, dhn, NT)
            wv = _dot(bg, dhn, NN)
            dd = wv * xdd
            dax = dax - dd
            dal = dal + _colsum(dd)
            dax = dax + jnp.where(last_row, dal, 0.0)
            dxdt_g = wv * dec[:, sl]
            ds = jnp.zeros((Q, Q), F32)
            dxdt_blocks = []
            for j in range(4):
                lo = g * GW + j * 128
                xb = xdt[:, lo:lo + 128]
                dyb = dyv[:, lo:lo + 128]
                dxb = dxdt_g[:, j * 128:(j + 1) * 128]
                for e in range(2):
                    h = g * 8 + j * 2 + e
                    lm = jnp.exp(jnp.where(causal, acs[:, h:h + 1] - acs_t[h:h + 1, :], NEG))
                    m = s * lm
                    dym = jnp.where(first if e == 0 else jnp.logical_not(first), dyb, 0.0)
                    dm = _dot(dym, xb, NT)
                    r = dm * m
                    dacs = dacs + jnp.where(lane_h == h, jnp.sum(r, axis=1, keepdims=True), 0.0)
                    dacs_t = dacs_t + jnp.where(sub_h == h, _colsum(r), 0.0)
                    ds = ds + dm * lm
                    dxb = dxb + _dot(m, dym, TN)
                dxdt_blocks.append(dxb)
            dxdt = jnp.concatenate(dxdt_blocks, axis=1)
            dcg = dcg + _dot(ds, bg, NN)
            dbg = dbg + _dot(ds, cg, TN)
            dxbc_ref[:, DI + g * NS:DI + (g + 1) * NS] = dbg
            dxbc_ref[:, DI + NG * NS + g * NS:DI + NG * NS + (g + 1) * NS] = dcg
            dxbc_ref[:, sl] = dsk_ref[:, sl] * dyv[:, sl] + dxdt * dt_x[:, sl]
            ddt_g, dacs_g = _reduce_heads([dxdt * xs[:, sl], dax], e_ref[0:128, sl])
            ddt = ddt + ddt_g
            dacs = dacs + dacs_g
        dacs = dacs - dacs_t.T
        ddta = _tri_sum(u_ref, dacs)
        ddt = ddt + ddta * arow_ref[...]
        ddtw = jnp.where(lane_h < NH, ddt * sig, 0.0)
        ddtw_ref[...] = jnp.concatenate([ddtw, jnp.zeros((Q, DT_PAD - 128), F32)], axis=1).astype(ddtw_ref.dtype)
        _acc_out(da_ref, _colsum(ddta * dt), i)
        _acc_out(ddx_ref, _colsum(dyv * xs), i)
        _acc_out(ddtb_ref, _colsum(ddtw), i)

    rev = lambda c: (nc - 1 - c, 0)
    const = lambda c: (0, 0)
    return pl.pallas_call(
        body, name=name, grid=(nc,),
        in_specs=[pl.BlockSpec((Q, DI), rev), pl.BlockSpec((Q, XBC), rev),
                  pl.BlockSpec((Q, DT_PAD), rev),
                  pl.BlockSpec((1, NS, DI), lambda c: (nc - 1 - c, 0, 0)),
                  pl.BlockSpec((1, 128), const), pl.BlockSpec((1, DI), const),
                  pl.BlockSpec((Q, Q), const), pl.BlockSpec((Q, Q), const), pl.BlockSpec((256, DI), const),
                  pl.BlockSpec(memory_space=pl.ANY)],
        out_specs=[pl.BlockSpec((Q, XBC), rev),
                   pl.BlockSpec((Q, DT_PAD), lambda c: (nc - 1 - c, C_DT // DT_PAD)),
                   pl.BlockSpec((1, 128), const), pl.BlockSpec((1, DI), const), pl.BlockSpec((1, 128), const)],
        out_shape=[jax.ShapeDtypeStruct((L, XBC), F32), jax.ShapeDtypeStruct((L, NPROJ), BF16),
                   jax.ShapeDtypeStruct((1, 128), F32), jax.ShapeDtypeStruct((1, DI), F32),
                   jax.ShapeDtypeStruct((1, 128), F32)],
        scratch_shapes=[pltpu.VMEM((NS, DI), F32)],
        input_output_aliases={9: 1},
        compiler_params=_params(("arbitrary",), VMEM_BIG))(dy, xbc, dtp, hs, arow, dsk_x, tri, triu,
                                                          expand, dproj)


def _adam_update(wv, gv, mv, vv):
    c1 = 1.0 - ADAM_B1 ** ADAM_STEP
    c2 = 1.0 - ADAM_B2 ** ADAM_STEP
    mn = ADAM_B1 * mv + (1.0 - ADAM_B1) * gv
    vn = ADAM_B2 * vv + (1.0 - ADAM_B2) * (gv * gv)
    return -ADAM_LR * ((mn / c1) / (jnp.sqrt(vn / c2) + ADAM_EPS) + ADAM_WD * wv), mn, vn


def _adamw(w, g, m, v, *, name, tr=None):
    R = w.shape[0]
    rest = tuple(w.shape[1:])
    if tr is None:
        tr = _pick(R, (256, 128, 64, 32, 16, 8))
    assert R % tr == 0

    def body(w_ref, g_ref, m_ref, v_ref, d_ref, mo_ref, vo_ref):
        d_ref[...], mo_ref[...], vo_ref[...] = _adam_update(w_ref[...], g_ref[...], m_ref[...], v_ref[...])

    zeros = (0,) * len(rest)
    spec = pl.BlockSpec((tr,) + rest, lambda i: (i,) + zeros)
    return pl.pallas_call(body, name=name, grid=(R // tr,), in_specs=[spec] * 4, out_specs=[spec] * 3,
                          out_shape=[jax.ShapeDtypeStruct(w.shape, F32)] * 3,
                          compiler_params=_params(("parallel",)))(w, g, m, v)


def _adamw_small(svrow, g_conv, params, *, name):
    n = len(params)

    def body(*refs):
        sv_ref, gc_ref = refs[0], refs[1]
        ins, outs = refs[2:2 + 3 * n], refs[2 + 3 * n:]
        for p, (key, w, _, _) in enumerate(params):
            w_ref, m_ref, v_ref = ins[3 * p:3 * p + 3]
            g_ref, d_ref, mo_ref, vo_ref = outs[4 * p:4 * p + 4]
            gv = gc_ref[...] if key == "conv_w" else sv_ref[:, SV_OFF[key]:SV_OFF[key] + w.shape[1]]
            g_ref[...] = gv
            d_ref[...], mo_ref[...], vo_ref[...] = _adam_update(w_ref[...], gv, m_ref[...], v_ref[...])

    vm = pl.BlockSpec(memory_space=pltpu.VMEM)
    args = [svrow, g_conv]
    shapes = []
    for _, w, m, v in params:
        args += [w, m, v]
        shapes += [jax.ShapeDtypeStruct(w.shape, F32)] * 4
    res = pl.pallas_call(body, name=name, in_specs=[vm] * len(args), out_specs=[vm] * len(shapes),
                         out_shape=shapes)(*args)
    return {key: tuple(res[4 * p:4 * p + 4]) for p, (key, _, _, _) in enumerate(params)}


def _slab_sum(recv, *, tile, name):
    rows = recv.shape[1]
    assert rows % tile == 0 and tile % 16 == 0

    def body(r_ref, o_ref):
        acc = r_ref[0].astype(F32)
        for j in range(1, N_DEV):
            acc = acc + r_ref[j].astype(F32)
        o_ref[...] = acc

    return pl.pallas_call(body, name=name, grid=(rows // tile,),
                          in_specs=[pl.BlockSpec((N_DEV, tile, D), lambda i: (0, i, 0))],
                          out_specs=pl.BlockSpec((tile, D), lambda i: (i, 0)),
                          out_shape=jax.ShapeDtypeStruct((rows, D), F32),
                          compiler_params=_params(("parallel",)))(recv)


MESH = pl.DeviceIdType.MESH


def _coords():
    return lax.axis_index("x"), lax.axis_index("y"), lax.axis_index("c")


def _peer(k):
    x, y, c = _coords()
    px = 1 - x if k & 4 else x
    py = 1 - y if k & 2 else y
    pc = 1 - c if k & 1 else c
    return (px, py, pc), 4 * px + 2 * py + pc


def _rcopy(src, dst, ssem, rsem, dev):
    return pltpu.make_async_remote_copy(src_ref=src, dst_ref=dst, send_sem=ssem, recv_sem=rsem,
                                        device_id=dev, device_id_type=MESH)


def _exchange_all(src_of, dst_slot, send_sems, recv_sems):
    x, y, c = _coords()
    me = 4 * x + 2 * y + c
    sent = []
    for k in range(1, N_DEV):
        dev, pidx = _peer(k)
        cp = _rcopy(src_of(pidx), dst_slot(me), send_sems.at[k - 1], recv_sems.at[k - 1], dev)
        cp.start()
        sent.append(cp)
    for k in range(1, N_DEV):
        dev, pidx = _peer(k)
        _rcopy(src_of(pidx), dst_slot(pidx), send_sems.at[k - 1], recv_sems.at[k - 1], dev).wait_recv()
    for cp in sent:
        cp.wait_send()


def _rows_of_slots(buf, nslots):
    rows = lax.broadcasted_iota(jnp.int32, (8, buf.shape[-1]), 0)
    out = jnp.zeros((8, buf.shape[-1]), F32)
    for j in range(nslots):
        out = out + jnp.where(rows == j, buf[j], 0.0)
    return out


def _exchange_start(src_of, dst_slot, send_sems, recv_sems):
    x, y, c = _coords()
    me = 4 * x + 2 * y + c
    sent = []
    for k in range(1, N_DEV):
        dev, pidx = _peer(k)
        cp = _rcopy(src_of(pidx), dst_slot(me), send_sems.at[k - 1], recv_sems.at[k - 1], dev)
        cp.start()
        sent.append(cp)
    return sent


def _exchange_finish(sent, src_of, dst_slot, send_sems, recv_sems):
    for k in range(1, N_DEV):
        dev, pidx = _peer(k)
        _rcopy(src_of(pidx), dst_slot(pidx), send_sems.at[k - 1], recv_sems.at[k - 1], dev).wait_recv()
    for cp in sent:
        cp.wait_send()


def _ada_gather(c, w_ada, b_r, slab, *, name):
    wloc = w_ada.shape[1]

    def body(c_ref, w_ref, b_ref, x_ref, mod_ref, call_ref, out_ref,
             csrc, cbuf, psrc, pbuf, s1, r1, s2, r2, send_sems, recv_sems, local_sem):
        x, y, cc = _coords()
        me_i = 4 * x + 2 * y + cc
        me, sibling = (x, y, cc), (x, y, 1 - cc)
        chips = [(1 - x, y), (x, 1 - y), (1 - x, 1 - y)]

        def slot(px, py, pc):
            return out_ref.at[4 * px + 2 * py + pc]

        def copy(k, block, to, src=None):
            return _rcopy(slot(*block) if src is None else src, slot(*block), send_sems.at[k], recv_sems.at[k], to)

        csrc[...] = jnp.broadcast_to(c_ref[...], (8, D))
        cbuf[me_i] = csrc[...]
        c_of, c_slot = (lambda p: csrc), (lambda s: cbuf.at[s])
        sent1 = _exchange_start(c_of, c_slot, s1, r1)

        mine = pltpu.make_async_copy(x_ref, slot(*me), local_sem)
        mine.start()
        first = [copy(0, me, sibling, src=x_ref)]
        first += [copy(1 + j, me, (*chip, cc), src=x_ref) for j, chip in enumerate(chips)]
        for cp in first:
            cp.start()

        _exchange_finish(sent1, c_of, c_slot, s1, r1)
        call = _rows_of_slots(cbuf, N_DEV)
        call_ref[...] = call
        prod = _dot_hi(_silu(call), w_ref[...])
        for b in range(N_DEV):
            psrc[b] = jnp.broadcast_to(prod[b:b + 1, :], (8, wloc))
        pbuf[me_i] = psrc[me_i]
        p_of, p_slot = (lambda p: psrc.at[p]), (lambda s: pbuf.at[s])
        sent2 = _exchange_start(p_of, p_slot, s2, r2)

        passed = [copy(4 + j, (*chip, cc), sibling) for j, chip in enumerate(chips)]
        for j, chip in enumerate(chips):
            copy(1 + j, (*chip, cc), me).wait_recv()
            passed[j].start()
        copy(0, sibling, me).wait_recv()
        for j, chip in enumerate(chips):
            copy(4 + j, (*chip, 1 - cc), me).wait_recv()

        _exchange_finish(sent2, p_of, p_slot, s2, r2)
        mod_ref[...] = _rows_of_slots(pbuf, N_DEV) + b_ref[...]
        for cp in first + passed:
            cp.wait_send()
        mine.wait()

    vm = pl.BlockSpec(memory_space=pltpu.VMEM)
    anyspec = pl.BlockSpec(memory_space=pl.ANY)
    return pl.pallas_call(
        body, name=name, in_specs=[vm, vm, vm, anyspec], out_specs=[vm, vm, anyspec],
        out_shape=[jax.ShapeDtypeStruct((N_DEV, wloc), F32), jax.ShapeDtypeStruct((N_DEV, D), F32),
                   jax.ShapeDtypeStruct((N_DEV,) + slab.shape, slab.dtype)],
        scratch_shapes=[pltpu.VMEM((8, D), F32), pltpu.VMEM((N_DEV, 8, D), F32),
                        pltpu.VMEM((N_DEV, 8, wloc), F32), pltpu.VMEM((N_DEV, 8, wloc), F32),
                        pltpu.SemaphoreType.DMA((N_DEV - 1,)), pltpu.SemaphoreType.DMA((N_DEV - 1,)),
                        pltpu.SemaphoreType.DMA((N_DEV - 1,)), pltpu.SemaphoreType.DMA((N_DEV - 1,)),
                        pltpu.SemaphoreType.DMA((7,)), pltpu.SemaphoreType.DMA((7,)), pltpu.SemaphoreType.DMA],
        compiler_params=pltpu.CompilerParams(vmem_limit_bytes=VMEM_BIG))(c, w_ada, b_r, slab)


_HBM =pl.BlockSpec(memory_space=pltpu.HBM)
_SEM = pl.BlockSpec(memory_space=pltpu.SEMAPHORE)
_EFFECT = pltpu.SideEffectType.DATAFLOW_SIDE_EFFECTING


def _xchg_src(src_ref, pidx, per_peer):
    return src_ref.at[pidx] if per_peer else src_ref


def _xchg_start(src, *, per_peer, name):
    rows = src.shape[-2]
    land_shape = (N_DEV, rows, D)

    def body(src_ref, land_ref, send_sems, recv_sems, src_thru, land_thru, token):
        del src_thru, land_thru
        x, y, c = _coords()
        me = 4 * x + 2 * y + c
        for k in range(1, N_DEV):
            dev, pidx = _peer(k)
            _rcopy(_xchg_src(src_ref, pidx, per_peer), land_ref.at[me], send_sems.at[k - 1],
                   recv_sems.at[k - 1], dev).start()
        token[...] = jnp.zeros_like(token)

    return pl.pallas_call(
        body, name=name,
        out_shape=(pltpu.SemaphoreType.DMA((N_DEV - 1,)), pltpu.SemaphoreType.DMA((N_DEV - 1,)),
                   pltpu.HBM(src.shape, src.dtype), pltpu.HBM(land_shape, src.dtype),
                   jax.ShapeDtypeStruct((8, 128), F32)),
        in_specs=(_HBM, _HBM),
        out_specs=(_SEM, _SEM, _HBM, _HBM, pl.BlockSpec(memory_space=pltpu.VMEM)),
        input_output_aliases={0: 2, 1: 3},
        compiler_params=pltpu.CompilerParams(has_side_effects=_EFFECT),
    )(pltpu.with_memory_space_constraint(src, pltpu.HBM),
      pltpu.with_memory_space_constraint(lax.empty(land_shape, src.dtype), pltpu.HBM))


def _xchg_wait(started, after, *, per_peer, name):
    send_sems, recv_sems, src_thru, land_thru, _ = started

    def body(src_ref, land_ref, send_sems, recv_sems, after_ref, src_dead, got_ref):
        del after_ref, src_dead, got_ref
        for k in range(1, N_DEV):
            dev, pidx = _peer(k)
            cp = _rcopy(_xchg_src(src_ref, pidx, per_peer), land_ref.at[pidx], send_sems.at[k - 1],
                        recv_sems.at[k - 1], dev)
            cp.wait_send()
            cp.wait_recv()

    return pl.pallas_call(
        body, name=name,
        out_shape=(pltpu.HBM(src_thru.shape, src_thru.dtype), pltpu.HBM(land_thru.shape, land_thru.dtype)),
        in_specs=(_HBM, _HBM, _SEM, _SEM, pl.BlockSpec(memory_space=pl.ANY)),
        out_specs=(_HBM, _HBM),
        input_output_aliases={0: 0, 1: 1},
        compiler_params=pltpu.CompilerParams(has_side_effects=_EFFECT),
    )(src_thru, land_thru, send_sems, recv_sems, after)


def _dep(token):
    return (token, (8, 128), lambda i, j, k: (0, 0))


def _small_allsum(sv, *, name):
    def body(sv_ref, all_ref, sum_ref, send_sems, recv_sems):
        x, y, c = _coords()
        me = 4 * x + 2 * y + c
        all_ref[me] = sv_ref[...]
        _exchange_all(lambda p: sv_ref, lambda s: all_ref.at[s], send_sems, recv_sems)
        acc = all_ref[0]
        for j in range(1, N_DEV):
            acc = acc + all_ref[j]
        sum_ref[...] = acc

    vm = pl.BlockSpec(memory_space=pltpu.VMEM)
    return pl.pallas_call(
        body, name=name, in_specs=[vm], out_specs=[vm, vm],
        out_shape=[jax.ShapeDtypeStruct((N_DEV, SV_ROWS, 128), F32), jax.ShapeDtypeStruct((SV_ROWS, 128), F32)],
        scratch_shapes=[pltpu.SemaphoreType.DMA((7,)), pltpu.SemaphoreType.DMA((7,))],
    )(sv)


def _ada_bwd(call, dmod_loc, *, name):
    wloc = dmod_loc.shape[1]

    def body(c_ref, d_ref, o_ref):
        o_ref[...] = _dot_hi(_silu(c_ref[...]), d_ref[...], TN)

    vm = pl.BlockSpec(memory_space=pltpu.VMEM)
    return pl.pallas_call(body, name=name, in_specs=[vm, vm], out_specs=vm,
                          out_shape=jax.ShapeDtypeStruct((D, wloc), F32),
                          compiler_params=pltpu.CompilerParams(vmem_limit_bytes=VMEM_BIG))(call, dmod_loc)


def _pad_rows(a, rows):
    return jnp.pad(a, ((0, rows - a.shape[0]), (0, 0)))


IN_SHIFT = tuple((IN_ROWS * j) % 16 for j in range(N_DEV))
IN_BASE = tuple(IN_ROWS * j - IN_SHIFT[j] for j in range(N_DEV))
IN_SEGMENTS = ((2048, XBC, C_XBC), (5152, 1024, C_POOL), (0, 2048, C_Z), (6176, 2048, C_GATE), (5120, 32, C_DT))


def _global_pieces(gs):
    pieces = []
    for j in range(N_DEV):
        lo, hi = 0, IN_ROWS_P
        if j > 0 and IN_BASE[j - 1] + IN_ROWS_P > IN_BASE[j]:
            pieces.append((IN_BASE[j], 16, gs[j - 1, IN_ROWS_P - 16:IN_ROWS_P] + gs[j, 0:16]))
            lo = 16
        if j + 1 < N_DEV and IN_BASE[j] + IN_ROWS_P > IN_BASE[j + 1]:
            hi = IN_ROWS_P - 16
        pieces.append((IN_BASE[j] + lo, hi - lo, gs[j, lo:hi]))
    return pieces


def _reorder_in_rows(gs):
    pieces = _global_pieces(gs)
    parts = []
    for lo, n, _ in IN_SEGMENTS:
        for p0, pn, arr in pieces:
            a, b = max(lo, p0), min(lo + n, p0 + pn)
            if a < b:
                parts.append(arr[a - p0:b - p0])
    parts.append(jnp.zeros((DT_PAD - 32, D), gs.dtype))
    return jnp.concatenate(parts, axis=0)


def _restore_in_shards(d):
    slabs = []
    for j in range(N_DEV):
        parts = []
        r, end = IN_BASE[j], IN_BASE[j] + IN_ROWS_P
        while r < end:
            lo, n, new = next(s for s in IN_SEGMENTS if s[0] <= r < s[0] + s[1])
            e = min(end, lo + n)
            parts.append(d[new + r - lo:new + e - lo])
            r = e
        slabs.append(jnp.concatenate(parts, axis=0))
    return jnp.stack(slabs, axis=0)


def _pack_sv(parts):
    flat = []
    for n, size in SV_PARTS:
        v = parts[n].reshape(-1).astype(F32)
        flat.append(jnp.pad(v, (0, size - v.shape[0])))
    v = jnp.concatenate(flat)
    return jnp.pad(v, (0, SV_ROWS * 128 - v.shape[0])).reshape(SV_ROWS, 128)


def _sv_get(flat, n, size):
    return flat[SV_OFF[n]:SV_OFF[n] + size]


def kernel(x, c, w_ada, b_ada, norm_mix_w, w_in, conv_w, conv_b, dt_bias, a_log, d_skip, ssd_norm_w, w_branch_ssd, pool_w, pool_scale, w_branch_pool, w_out, norm_mlp_w, w_up, w_down, norm_final_w, loss_target, m_w_ada, m_b_ada, m_norm_mix_w, m_w_in, m_conv_w, m_conv_b, m_dt_bias, m_a_log, m_d_skip, m_ssd_norm_w, m_w_branch_ssd, m_pool_w, m_pool_scale, m_w_branch_pool, m_w_out, m_norm_mlp_w, m_w_up, m_w_down, m_norm_final_w, v_w_ada, v_b_ada, v_norm_mix_w, v_w_in, v_conv_w, v_conv_b, v_dt_bias, v_a_log, v_d_skip, v_ssd_norm_w, v_w_branch_ssd, v_pool_w, v_pool_scale, v_w_branch_pool, v_w_out, v_norm_mlp_w, v_w_up, v_w_down, v_norm_final_w):
    xs_ = x[0]
    tgt = loss_target[0]
    L = xs_.shape[0]
    me = 4 * lax.axis_index("x") + 2 * lax.axis_index("y") + lax.axis_index("c")
    wloc = w_ada.shape[2]

    conv_bits = lax.bitcast_convert_type(conv_w[0], SLAB_DT).reshape(3, D)
    in_shift = (IN_ROWS * me) % 16
    slab_in = lax.dynamic_update_slice(jnp.zeros((IN_ROWS_P, D), SLAB_DT), w_in[0].T.astype(SLAB_DT),
                                       (in_shift, 0))
    slab_in = jnp.concatenate([slab_in, _pad_rows(conv_bits, CONV_ROWS)], axis=0)
    slab_rest = jnp.concatenate([
        w_branch_ssd[0].astype(SLAB_DT),
        pool_w[0].reshape(32, D).astype(SLAB_DT),
        w_branch_pool[0].astype(SLAB_DT),
        w_out[0].astype(SLAB_DT),
        w_up[0].T.astype(SLAB_DT),
        w_down[0].astype(SLAB_DT)], axis=0)
    mod_p, c_all, gs_in = _ada_gather(c, w_ada[0], b_ada.reshape(N_DEV, wloc), slab_in,
                                      name="ada_gather_w_in")
    mod = mod_p.reshape(6, D)
    shift_m, scale_m, gate_m, shift_f, scale_f, gate_f = [mod[i:i + 1] for i in range(6)]
    slab_rest, gs_in = lax.optimization_barrier((slab_rest, gs_in))
    rest_started = _xchg_start(slab_rest, per_peer=False, name="gather_rest_start")
    gather_token = rest_started[4]

    w_in_t = _reorder_in_rows(gs_in)
    conv_full = lax.bitcast_convert_type(
        gs_in[:, IN_ROWS_P:IN_ROWS_P + 3].reshape(N_DEV, 4, XBC // N_DEV, 2), F32)
    conv_full = conv_full.transpose(1, 0, 2).reshape(4, XBC)

    dtb = jnp.pad(dt_bias, ((0, 0), (0, 128 - NH)))
    arow = jnp.pad(-jnp.exp(a_log), ((0, 0), (0, 128 - NH)))
    dsk_x = jnp.repeat(d_skip, HP, axis=1)

    tm = _pick(L, (1024, 512, 256, 128))
    tm2 = _pick(L, (2048, 1024, 512, 256, 128))
    tkl = _pick(L, (4096, 2048, 1024, 512, 256, 128))
    tkl2 = _pick(L, (2048, 1024, 512, 256, 128))

    tmh = _pick(L, (512, 256, 128))
    zcol = C_Z // DI
    gcol = C_GATE // (2 * D)

    def whole_rows(w):
        return lambda t: ((L, w), BF16, (t, w), lambda i, j, k: (i, 0))

    def norm1_pro(x_ref, ex, outs, j):
        @pl.when(j == 0)
        def _():
            xv = x_ref[...]
            r = lax.rsqrt(jnp.mean(xv * xv, axis=-1, keepdims=True) + EPS)
            outs[1][...] = (xv * r * ex[0][...] * (1.0 + ex[1][...]) + ex[2][...]).astype(outs[1].dtype)

        return outs[1][...]

    def proj_ep(acc, ex, outs):
        outs[0][...] = acc

        @pl.when(pl.program_id(1) == NPROJ // 768 - 1)
        def _():
            pre = acc[:, 768 - DT_PAD:768 - DT_PAD + 128] + ex[3][...]
            outs[2][...] = jnp.concatenate([_softplus(pre), _sigmoid(pre)], axis=1)

    proj, h1, dtp = _mm(
        xs_, w_in_t, "nt", name="in_proj", tm=tm2, tn=768, tk=D,
        extras=[(norm_mix_w, *_vecs()), (scale_m, *_vecs()), (shift_m, *_vecs()), (dtb, *_vecs(128)),
                _dep(gather_token)],
        outs=[F32, whole_rows(D)(tm2), ((L, DT_PAD), F32, (tm2, DT_PAD), lambda i, j, k: (i, 0))],
        prologue=norm1_pro, epilogue=proj_ep)
    xbc_raw = proj
    y_ssm, hs, xbc = _ssd_fwd(xbc_raw, dtp, conv_full, conv_b, arow, dsk_x, name="ssd_fwd")

    slab_rest, gs = _xchg_wait(rest_started, y_ssm, per_peer=False, name="gather_rest_wait")
    gs = lax.dynamic_update_slice(gs, slab_rest[None], (me, 0, 0))

    def part(n, rows):
        return gs[:, REST_OFF[n]:REST_OFF[n] + rows]

    w_bssd = part("bssd", 256).reshape(DI, D)
    w_pool = part("pool", 32).reshape(N_DEV, 4, 32, PGW).transpose(1, 0, 2, 3).reshape(POOL_W, PGW)
    w_bpool = part("bpool", 128).reshape(POOL_W, D)
    w_o = part("out", 128).reshape(D, D)
    w_up_t = part("up", 512).reshape(DFF, D)
    w_dn = part("down", 512).reshape(DFF, D)

    def gnorm_pro(y_ref, ex, outs, j):
        z_ref, w_ref = ex
        yg = y_ref[...].astype(F32) * _silu(z_ref[...].astype(F32))
        segs = []
        for k in range(NG):
            sl = slice(k * GW, (k + 1) * GW)
            seg = yg[:, sl]
            r = lax.rsqrt(jnp.mean(seg * seg, axis=-1, keepdims=True) + EPS)
            segs.append((seg * r * w_ref[:, sl]).astype(BF16))
        yn_v = jnp.concatenate(segs, axis=1)
        outs[1][...] = yn_v
        return yn_v

    y_ssd, yn = _mm(y_ssm, w_bssd, "nn", name="branch_ssd", tm=tmh, tn=D, tk=DI,
                    extras=[(proj, *_rows(tmh, DI, zcol)), (ssd_norm_w, *_vecs(DI))],
                    outs=[BF16, whole_rows(DI)(tmh)], prologue=gnorm_pro)
    pooled = _pool_fwd(proj, name="pool_fwd")
    wp_spec = ((POOL_W, PGW), lambda i, j, k: (0, 0))

    def pool_pro(a_ref, ex, outs, j):
        wp_ref, s_ref = ex
        segs = []
        for g in range(4):
            sl = slice(g * PGW, (g + 1) * PGW)
            p = _dot(a_ref[:, sl], wp_ref[sl, :], NN)
            outs[1][:, sl] = p.astype(BF16)
            segs.append((p * s_ref[:, sl]).astype(BF16))
        yp1_v = jnp.concatenate(segs, axis=1)
        outs[2][...] = yp1_v
        return yp1_v

    y_pool, yp0, yp1 = _mm(pooled, w_bpool, "nn", name="branch_pool", tm=tm, tn=D, tk=D,
                           extras=[(w_pool, *wp_spec), (pool_scale, *_vecs())],
                           outs=[BF16, whole_rows(D)(tm), whole_rows(D)(tm)], prologue=pool_pro)

    def merge_pro(a_ref, ex, outs, j):
        s = _sigmoid(ex[1][...].astype(F32))
        mv = (s[:, :D] * a_ref[...].astype(F32) + s[:, D:] * ex[0][...].astype(F32)).astype(BF16)
        outs[3][...] = mv
        return mv

    mix, x1, h2, m = _mm(y_ssd, w_o, "nn", name="out_proj", tm=tmh, tn=D, tk=D,
                         extras=[(y_pool, *_rows(tmh)), (proj, *_rows(tmh, 2 * D, gcol)),
                                 (xs_, *_rows(tmh)), (gate_m, *_vecs()), (norm_mlp_w, *_vecs()),
                                 (scale_f, *_vecs()), (shift_f, *_vecs())],
                         outs=[BF16, F32, BF16, whole_rows(D)(tmh)], prologue=merge_pro,
                         epilogue=lambda acc, ex, outs: _ep_resid_norm(acc, ex[2:], outs[:3]))

    def relu2(acc, ex, outs):
        r = jnp.maximum(acc, 0.0)
        outs[0][...] = acc.astype(BF16)
        outs[1][...] = (r * r).astype(BF16)

    up, act = _mm(h2, w_up_t, "nt", name="mlp_up", outs=[BF16, BF16], tm=tm2, tn=1024, tk=D, epilogue=relu2)

    dx2, ddown, loss_p, dnwf, dgate_f = _mm(
        act, w_dn, "nn", name="mlp_down", tm=tmh, tn=D, tk=DFF,
        extras=[(x1, *_rows(tmh)), (tgt, *_rows(tmh)), (gate_f, *_vecs()), (norm_final_w.reshape(1, D), *_vecs())],
        outs=[F32, BF16, _sum_out(128), _sum_out(), _sum_out()], epilogue=_ep_final)

    def drelu2(acc, ex, outs):
        outs[0][...] = (acc * (2.0 * jnp.maximum(ex[0][...].astype(F32), 0.0))).astype(BF16)

    def dep_last(ep):
        return lambda acc, ex, outs: ep(acc, ex[:-1], outs)

    dup = _mm(ddown, w_dn, "nt", name="mlp_down_dx", outs=[BF16], tm=tm2, tn=1024, tk=D,
              extras=[(up, (tm2, 1024), lambda i, j, k: (i, j))], epilogue=drelu2)
    g_dn = _mm(act, ddown, "tn", name="mlp_down_dw", outs=[SLAB_DT], tm=1024, tn=D, tk=tkl)
    g_up_t = _mm(dup, h2, "tn", name="mlp_up_dw", outs=[SLAB_DT], tm=1024, tn=D, tk=tkl)
    gslab_mlp = jnp.concatenate([g_up_t.reshape(N_DEV, 512, D), g_dn.reshape(N_DEV, 512, D)], axis=1)
    mlp_started = _xchg_start(gslab_mlp, per_peer=True, name="scatter_mlp_start")
    dx1, p2, q2, dmix, dgate_m = _mm(
        dup, w_up_t, "nn", name="mlp_up_dx", tm=tmh, tn=D, tk=DFF,
        extras=[(x1, *_rows(tmh)), (dx2, *_rows(tmh)), (norm_mlp_w, *_vecs()), (scale_f, *_vecs()),
                (mix, *_rows(tmh)), (gate_m, *_vecs()), _dep(mlp_started[4])],
        outs=[F32, _sum_out(), _sum_out(), BF16, _sum_out()], epilogue=dep_last(_ep_norm_bwd))
    gcol = C_GATE // (2 * D)
    dy_ssd, dy_pool, dproj = _mm(
        dmix, w_o, "nt", name="out_proj_dx", tm=tmh, tn=D, tk=D,
        extras=[(y_ssd, *_rows(tmh)), (y_pool, *_rows(tmh)), (proj, *_rows(tmh, 2 * D, gcol))],
        outs=[BF16, BF16, ((L, NPROJ), BF16, *_rows(tmh, 2 * D, gcol))], epilogue=_ep_merge_bwd)
    g_o = _mm(m, dmix, "tn", name="out_proj_dw", outs=[SLAB_DT], tm=D, tn=D, tk=tkl)
    zcol = C_Z // DI
    dy_ssm, dproj, d_snw = _mm(
        dy_ssd, w_bssd, "nt", name="branch_ssd_dx", tm=tmh, tn=DI, tk=D,
        extras=[(y_ssm, *_rows(tmh, DI)), (proj, *_rows(tmh, DI, zcol)), (ssd_norm_w, *_vecs(DI)),
                (dproj, None, None)],
        outs=[F32, ((L, NPROJ), BF16, *_rows(tmh, DI, zcol)), _sum_out(DI)],
        epilogue=_ep_gated_norm_bwd, aliases={3: 1})
    g_bssd = _mm(yn, dy_ssd, "tn", name="branch_ssd_dw", outs=[SLAB_DT], tm=1024, tn=D, tk=tkl)
    dxbc, dproj, d_a, d_dx, d_dtb = _ssd_bwd(dy_ssm, xbc, dtp, hs, arow, dsk_x, dproj, name="ssd_bwd")
    dproj, d_cw, d_cb = _conv_bwd(xbc_raw, dxbc, conv_full, conv_b, dproj, name="conv_bwd")
    def pool_bwd_ep(acc, ex, outs):
        y_ref, s_ref, wp_ref = ex
        o_ref, ds_ref, dpool_ref = outs
        dyp0_v = (acc * s_ref[...]).astype(BF16)
        o_ref[...] = dyp0_v
        _acc_out(ds_ref, _colsum(acc * y_ref[...].astype(F32)), _row_step())
        for g in range(4):
            sl = slice(g * PGW, (g + 1) * PGW)
            dpool_ref[:, sl] = _dot(dyp0_v[:, sl], wp_ref[sl, :], NT)

    dyp0, d_ps, dpooled = _mm(dy_pool, w_bpool, "nt", name="branch_pool_dx", tm=tm, tn=D, tk=D,
                              extras=[(yp0, *_rows(tm)), (pool_scale, *_vecs()), (w_pool, *wp_spec)],
                              outs=[BF16, _sum_out(), F32], epilogue=pool_bwd_ep)
    g_bpool = _mm(yp1, dy_pool, "tn", name="branch_pool_dw", outs=[SLAB_DT], tm=D, tn=D, tk=tkl)
    g_pool = _mm_pool_tn(pooled, dyp0, name="pool_mix_dw", tk=tkl)
    gslab_mix = jnp.concatenate([
        g_bssd.reshape(N_DEV, 256, D),
        g_pool.reshape(4, N_DEV, 32, PGW).transpose(1, 0, 2, 3).reshape(N_DEV, 32, D).astype(SLAB_DT),
        g_bpool.reshape(N_DEV, 128, D),
        g_o.reshape(N_DEV, 128, D)], axis=1)
    mix_started = _xchg_start(gslab_mix, per_peer=True, name="scatter_mix_start")
    dproj = _pool_bwd(dpooled, dproj, name="pool_bwd")
    g_in_t = _mm(dproj, h1, "tn", name="in_proj_dw", outs=[SLAB_DT], tm=1408, tn=D, tk=tkl2,
                 extras=[_dep(mix_started[4])])
    gslab_in = _restore_in_shards(g_in_t)
    in_started = _xchg_start(gslab_in, per_peer=True, name="scatter_in_start")
    grad_x, p1, q1 = _mm(
        dproj, w_in_t, "nn", name="in_proj_dx", tm=tmh, tn=D, tk=4224,
        extras=[(xs_, *_rows(tmh)), (dx1, *_rows(tmh)), (norm_mix_w, *_vecs()), (scale_m, *_vecs()),
                _dep(in_started[4])],
        outs=[F32, _sum_out(), _sum_out()], epilogue=dep_last(_ep_norm_bwd))

    def landed(started, after, tile, name):
        src, land = _xchg_wait(started, after, per_peer=True, name=name + "_wait")
        own = lax.dynamic_slice_in_dim(src, me, 1, axis=0)
        return _slab_sum(lax.dynamic_update_slice(land, own, (me, 0, 0)), tile=tile, name=name + "_sum")

    gsum_mlp = landed(mlp_started, grad_x, 256, "scatter_mlp")
    gsum_mix = landed(mix_started, grad_x, 272, "scatter_mix")
    gsum_in = landed(in_started, grad_x, 208, "scatter_in")

    dmod = jnp.concatenate([q1, p1 * norm_mix_w, dgate_m, q2, p2 * norm_mlp_w, dgate_f], axis=1)
    d_alog = d_a[:, :NH] * (-jnp.exp(a_log))
    sv = _pack_sv({
        "b_ada": dmod, "norm_mix_w": p1 * (1.0 + scale_m), "conv_b": d_cb, "dt_bias": d_dtb[:, :NH],
        "a_log": d_alog, "d_skip": d_dx.reshape(NH, HP).sum(axis=1), "ssd_norm_w": d_snw,
        "pool_scale": d_ps, "norm_mlp_w": p2 * (1.0 + scale_f), "norm_final_w": dnwf, "conv_w": d_cw,
        "loss": loss_p[:, :1]})
    sv_all, sv_sum = _small_allsum(sv, name="small_allsum")
    flat = sv_sum.reshape(-1)
    loss = flat[SV_OFF["loss"]]
    dmod_all = sv_all.reshape(N_DEV, SV_ROWS * 128)[:, :6 * D]
    g_w_ada = _ada_bwd(c_all, lax.dynamic_slice_in_dim(dmod_all, me * wloc, wloc, axis=1), name="ada_bwd")

    g_conv_w = lax.dynamic_slice_in_dim(_sv_get(flat, "conv_w", 4 * XBC).reshape(4, XBC),
                                        me * (XBC // N_DEV), XBC // N_DEV, axis=1)
    small = [("b_ada", b_ada, m_b_ada, v_b_ada), ("norm_mix_w", norm_mix_w, m_norm_mix_w, v_norm_mix_w),
             ("conv_b", conv_b, m_conv_b, v_conv_b), ("dt_bias", dt_bias, m_dt_bias, v_dt_bias),
             ("a_log", a_log, m_a_log, v_a_log), ("d_skip", d_skip, m_d_skip, v_d_skip),
             ("ssd_norm_w", ssd_norm_w, m_ssd_norm_w, v_ssd_norm_w),
             ("pool_scale", pool_scale, m_pool_scale, v_pool_scale),
             ("norm_mlp_w", norm_mlp_w, m_norm_mlp_w, v_norm_mlp_w),
             ("norm_final_w", norm_final_w[None], m_norm_final_w[None], v_norm_final_w[None]),
             ("conv_w", conv_w[0], m_conv_w[0], v_conv_w[0])]
    small_out = _adamw_small(sv_sum.reshape(1, SV_ROWS * 128), g_conv_w, small, name="adamw_small")
    small_out["norm_final_w"] = tuple(a[0] for a in small_out["norm_final_w"])
    small_out["conv_w"] = tuple(a[None] for a in small_out["conv_w"])

    def gpart(n, rows_):
        return gsum_mix[MIX_OFF[n]:MIX_OFF[n] + rows_]

    def lin(a):
        return a[0].T.reshape(IN_ROWS * 8, 128)

    g_lin = lax.dynamic_slice_in_dim(gsum_in, in_shift, IN_ROWS, axis=0).reshape(IN_ROWS * 8, 128)
    dlt, mn, vn = _adamw(lin(w_in), g_lin, lin(m_w_in), lin(v_w_in), name="adamw_w_in", tr=IN_ROWS * 2)
    big_in = tuple(a.reshape(IN_ROWS, D).T[None] for a in (g_lin, dlt, mn, vn))

    big = {
        "w_ada": (w_ada, m_w_ada, v_w_ada, g_w_ada, (D, wloc)),
        "w_branch_ssd": (w_branch_ssd, m_w_branch_ssd, v_w_branch_ssd, gpart("bssd", 256), (256, D)),
        "pool_w": (pool_w, m_pool_w, v_pool_w, gpart("pool", 32).reshape(128, PGW), (128, PGW)),
        "w_branch_pool": (w_branch_pool, m_w_branch_pool, v_w_branch_pool, gpart("bpool", 128), (128, D)),
        "w_out": (w_out, m_w_out, v_w_out, gpart("out", 128), (128, D)),
        "w_up": (w_up, m_w_up, v_w_up, gsum_mlp[:512].T, (D, 512)),
        "w_down": (w_down, m_w_down, v_w_down, gsum_mlp[512:], (512, D)),
    }
    big_out = {}
    for n, (w, mm_, vv, g, shp2) in big.items():
        dlt, mn, vn = _adamw(w.reshape(shp2), g, mm_.reshape(shp2), vv.reshape(shp2), name="adamw_" + n)
        big_out[n] = (g.reshape(w.shape), dlt.reshape(w.shape), mn.reshape(w.shape), vn.reshape(w.shape))

    order = ["w_ada", "b_ada", "norm_mix_w", "w_in", "conv_w", "conv_b", "dt_bias", "a_log", "d_skip",
             "ssd_norm_w", "w_branch_ssd", "pool_w", "pool_scale", "w_branch_pool", "w_out", "norm_mlp_w",
             "w_up", "w_down", "norm_final_w"]
    big_out["w_in"] = big_in
    res = {**small_out, **big_out}
    outs = [loss, grad_x.reshape(x.shape)]
    for k in range(4):
        outs += [res[n][k] for n in order]
    return tuple(outs)
```

```python
import functools

import numpy as np
import jax
import jax.numpy as jnp
from jax import lax
from jax.experimental import pallas as pl
from jax.experimental.pallas import tpu as pltpu

F32 = jnp.float32
BF16 = jnp.bfloat16
SLAB_DT = jnp.bfloat16
_MXU_DTYPE = jnp.bfloat16

N_DEV = 8
D = 1024
DI = 2048
NH = 32
HP = 64
NG = 4
NS = 128
Q = 128
XBC = DI + 2 * NG * NS
DFF = 4096
N_IN = 8224
EPS = 1e-5
POOL_W = 1024
PGW = 256

C_XBC, C_POOL, C_Z, C_GATE, C_DT = 0, 3072, 4096, 6144, 8192
DT_PAD = 256
NPROJ = C_DT + DT_PAD

IN_ROWS = N_IN // N_DEV
IN_ROWS_P = 1040
CONV_ROWS = 16
REST_PARTS = (("bssd", 256), ("pool", 32), ("bpool", 128), ("out", 128), ("up", 512), ("down", 512))
REST_OFF = {}
_o = 0
for _n, _r in REST_PARTS:
    REST_OFF[_n] = _o
    _o += _r
REST_ROWS = _o
MIX_PARTS = (("bssd", 256), ("pool", 32), ("bpool", 128), ("out", 128))
MIX_OFF = {}
_o = 0
for _n, _r in MIX_PARTS:
    MIX_OFF[_n] = _o
    _o += _r
MIX_ROWS = _o

SV_PARTS = (("b_ada", 6144), ("norm_mix_w", 1024), ("conv_b", 3072), ("dt_bias", 128), ("a_log", 128),
            ("d_skip", 128), ("ssd_norm_w", 2048), ("pool_scale", 1024), ("norm_mlp_w", 1024),
            ("norm_final_w", 1024), ("conv_w", 4 * XBC), ("loss", 128))
SV_OFF = {}
_o = 0
for _n, _r in SV_PARTS:
    SV_OFF[_n] = _o
    _o += _r
SV_ROWS = 224
assert _o <= SV_ROWS * 128

ADAM_LR, ADAM_B1, ADAM_B2, ADAM_EPS, ADAM_WD, ADAM_STEP = 0.001, 0.9, 0.999, 1e-08, 0.01, 10

VMEM_BIG = 56 * 1024 * 1024
NEG = -1e30

NN = ((1,), (0,))
NT = ((1,), (1,))
TN = ((0,), (0,))


def _dot(a, b, dims=NN):
    return lax.dot_general(a.astype(_MXU_DTYPE), b.astype(_MXU_DTYPE), (dims, ((), ())),
                           preferred_element_type=F32)


def _dot_hi(a, b, dims=NN):
    return lax.dot_general(a.astype(F32), b.astype(F32), (dims, ((), ())),
                           precision=lax.Precision.HIGHEST, preferred_element_type=F32)


def _pick(n, cands):
    for c in cands:
        if n % c == 0:
            return c
    return n


def _sigmoid(x):
    return 1.0 / (1.0 + jnp.exp(-x))


def _silu(x):
    return x * _sigmoid(x)


def _dsilu(x):
    s = _sigmoid(x)
    return s * (1.0 + x * (1.0 - s))


def _softplus(x):
    return jnp.maximum(x, 0.0) + jnp.log(1.0 + jnp.exp(-jnp.abs(x)))


def _params(sem, vmem=None):
    return pltpu.CompilerParams(dimension_semantics=sem, vmem_limit_bytes=vmem)


def _row_step():
    return pl.program_id(0)


def _mm(a, b, mode, *, name, outs, tm, tn, tk, extras=(), epilogue=None, aliases=None, prologue=None):
    if mode == "tn":
        K, M = a.shape
        N = b.shape[1]
        a_spec = pl.BlockSpec((tk, tm), lambda i, j, k: (k, i))
        b_spec = pl.BlockSpec((tk, tn), lambda i, j, k: (k, j))
        dims = TN
    else:
        M = a.shape[0]
        K = b.shape[0] if mode == "nn" else b.shape[1]
        if prologue is None:
            assert a.shape[1] == K
            a_spec = pl.BlockSpec((tm, tk), lambda i, j, k: (i, k))
        else:
            assert tk == K
            a_spec = pl.BlockSpec((tm, a.shape[1]), lambda i, j, k: (i, 0))
        if mode == "nn":
            N = b.shape[1]
            b_spec = pl.BlockSpec((tk, tn), lambda i, j, k: (k, j))
            dims = NN
        else:
            N = b.shape[0]
            b_spec = pl.BlockSpec((tn, tk), lambda i, j, k: (j, k))
            dims = NT
    assert M % tm == 0 and N % tn == 0 and K % tk == 0, (name, M, N, K, tm, tn, tk)
    nk = K // tk
    ne, no = len(extras), len(outs)
    if epilogue is None:
        def epilogue(acc, ex, out_refs):
            out_refs[0][...] = acc.astype(out_refs[0].dtype)

    def body(a_ref, b_ref, *rest):
        ex, out_refs = rest[:ne], rest[ne:ne + no]
        lhs = a_ref[...] if prologue is None else prologue(a_ref, ex, out_refs, pl.program_id(1))
        p = _dot(lhs, b_ref[...], dims)
        if nk == 1:
            epilogue(p, ex, out_refs)
        else:
            acc = rest[-1]
            k = pl.program_id(2)

            @pl.when(k == 0)
            def _():
                acc[...] = p

            @pl.when(jnp.logical_and(k > 0, k < nk - 1))
            def _():
                acc[...] += p

            @pl.when(k == nk - 1)
            def _():
                epilogue(acc[...] + p, ex, out_refs)

    out_specs, out_shape = [], []
    for o in outs:
        if isinstance(o, tuple):
            shape, dt, bs, im = o
            out_specs.append(pl.BlockSpec(bs, im))
            out_shape.append(jax.ShapeDtypeStruct(shape, dt))
        else:
            out_specs.append(pl.BlockSpec((tm, tn), lambda i, j, k: (i, j)))
            out_shape.append(jax.ShapeDtypeStruct((M, N), o))
    in_specs = [a_spec, b_spec]
    for _, bs, im in extras:
        in_specs.append(pl.BlockSpec(memory_space=pl.ANY) if bs is None else pl.BlockSpec(bs, im))
    res = pl.pallas_call(
        body, name=name,
        grid=(M // tm, N // tn, nk),
        in_specs=in_specs, out_specs=out_specs, out_shape=out_shape,
        scratch_shapes=[pltpu.VMEM((tm, tn), F32)] if nk > 1 else [],
        input_output_aliases={2 + e: o for e, o in (aliases or {}).items()},
        compiler_params=_params(("arbitrary", "arbitrary", "arbitrary"), VMEM_BIG),
    )(a, b, *[e[0] for e in extras])
    return res if no > 1 else res[0]


def _rows(tm, w=D, col=0):
    return (tm, w), lambda i, j, k, c=col: (i, c)


def _vecs(w=D, col=0):
    return (1, w), lambda i, j, k, c=col: (0, c)


def _sum_out(w=D):
    return ((1, w), F32, (1, w), lambda i, j, k: (0, 0))


def _mm_pool_tn(a, b, *, name, tk):
    L = a.shape[0]

    def body(a_ref, b_ref, o_ref):
        p = _dot(a_ref[...], b_ref[...], TN)

        @pl.when(pl.program_id(1) == 0)
        def _():
            o_ref[...] = p

        @pl.when(pl.program_id(1) > 0)
        def _():
            o_ref[...] += p

    blk = pl.BlockSpec((tk, PGW), lambda g, k: (k, g))
    return pl.pallas_call(body, name=name, grid=(4, L // tk), in_specs=[blk, blk],
                          out_specs=pl.BlockSpec((PGW, PGW), lambda g, k: (g, 0)),
                          out_shape=jax.ShapeDtypeStruct((POOL_W, PGW), F32),
                          compiler_params=_params(("parallel", "arbitrary")))(a, b)


def _acc_out(ref, val, i):
    @pl.when(i == 0)
    def _():
        ref[...] = val

    @pl.when(i > 0)
    def _():
        ref[...] += val


def _colsum(v):
    return jnp.sum(v, axis=0, keepdims=True)


def _ep_resid_norm(acc, ex, outs):
    x_ref, g_ref, nw_ref, sc_ref, sh_ref = ex
    mix_ref, x1_ref, h_ref = outs
    mix_ref[...] = acc.astype(mix_ref.dtype)
    xv = x_ref[...] + g_ref[...] * acc
    x1_ref[...] = xv
    r = lax.rsqrt(jnp.mean(xv * xv, axis=-1, keepdims=True) + EPS)
    h_ref[...] = (xv * r * nw_ref[...] * (1.0 + sc_ref[...]) + sh_ref[...]).astype(h_ref.dtype)


def _ep_final(acc, ex, outs):
    x1_ref, t_ref, g_ref, nw_ref = ex
    dx2_ref, dd_ref, loss_ref, dnw_ref, dg_ref = outs
    i = _row_step()
    x2 = x1_ref[...] + g_ref[...] * acc
    r = lax.rsqrt(jnp.mean(x2 * x2, axis=-1, keepdims=True) + EPS)
    xh = x2 * r
    e = xh * nw_ref[...] - t_ref[...]
    part = 0.5 * jnp.sum(jnp.mean(e * e, axis=-1, keepdims=True), axis=0, keepdims=True)
    dy = e * (1.0 / D)
    g = dy * nw_ref[...]
    dx2 = r * (g - xh * jnp.mean(g * xh, axis=-1, keepdims=True))
    dx2_ref[...] = dx2
    dd_ref[...] = (dx2 * g_ref[...]).astype(dd_ref.dtype)
    _acc_out(loss_ref, jnp.broadcast_to(part, (1, 128)), i)
    _acc_out(dnw_ref, _colsum(dy * xh), i)
    _acc_out(dg_ref, _colsum(dx2 * acc), i)


def _ep_norm_bwd(acc, ex, outs):
    x_ref, dr_ref, nw_ref, sc_ref = ex[:4]
    dx_ref, p_ref, q_ref = outs[:3]
    i = _row_step()
    xv = x_ref[...]
    r = lax.rsqrt(jnp.mean(xv * xv, axis=-1, keepdims=True) + EPS)
    xh = xv * r
    g = acc * (nw_ref[...] * (1.0 + sc_ref[...]))
    dx = dr_ref[...] + r * (g - xh * jnp.mean(g * xh, axis=-1, keepdims=True))
    dx_ref[...] = dx
    _acc_out(p_ref, _colsum(acc * xh), i)
    _acc_out(q_ref, _colsum(acc), i)
    if len(ex) > 4:
        m_ref, g_ref = ex[4:]
        dm_ref, dg_ref = outs[3:]
        dm_ref[...] = (dx * g_ref[...]).astype(dm_ref.dtype)
        _acc_out(dg_ref, _colsum(dx * m_ref[...].astype(F32)), i)


def _ep_merge_bwd(acc, ex, outs):
    a_ref, b_ref, gl_ref = ex
    da_ref, db_ref, dgl_ref = outs
    s = _sigmoid(gl_ref[...].astype(F32))
    s1, s2 = s[:, :D], s[:, D:]
    da_ref[...] = (acc * s1).astype(da_ref.dtype)
    db_ref[...] = (acc * s2).astype(db_ref.dtype)
    dgl_ref[:, :D] = (acc * a_ref[...].astype(F32) * s1 * (1.0 - s1)).astype(dgl_ref.dtype)
    dgl_ref[:, D:] = (acc * b_ref[...].astype(F32) * s2 * (1.0 - s2)).astype(dgl_ref.dtype)


GW = DI // NG


def _ep_gated_norm_bwd(acc, ex, outs):
    y_ref, z_ref, w_ref, _ = ex
    dy_ref, dz_ref, dw_ref = outs
    zv = z_ref[...].astype(F32)
    yv = y_ref[...].astype(F32)
    sg = _sigmoid(zv)
    sz = zv * sg
    yg = yv * sz
    dsz = sg * (1.0 + zv * (1.0 - sg))
    dws = []
    for k in range(NG):
        sl = slice(k * GW, (k + 1) * GW)
        seg = yg[:, sl]
        r = lax.rsqrt(jnp.mean(seg * seg, axis=-1, keepdims=True) + EPS)
        sh = seg * r
        dn = acc[:, sl]
        g = dn * w_ref[:, sl]
        dyg = r * (g - sh * jnp.mean(g * sh, axis=-1, keepdims=True))
        dy_ref[:, sl] = dyg * sz[:, sl]
        dz_ref[:, sl] = (dyg * yv[:, sl] * dsz[:, sl]).astype(dz_ref.dtype)
        dws.append(_colsum(dn * sh))
    _acc_out(dw_ref, jnp.concatenate(dws, axis=1), _row_step())


CONV_CB = 128
HALO = 16


def _time_chunk(L):
    return _pick(L, (256, 128))


def _with_halo(x_ref, i, r0, rc):
    p0 = pl.multiple_of(jnp.maximum(r0 - HALO, 0), HALO)
    prev = jnp.where(i > 0, x_ref[pl.ds(p0, HALO), :].astype(F32), 0.0)
    return jnp.concatenate([prev, x_ref[pl.ds(r0, rc), :].astype(F32)], axis=0)


def _conv_bwd(proj, dy, w, b, dproj, *, name):
    L = proj.shape[0]
    rc = _time_chunk(L)
    n = L // rc

    def body(x_ref, dy_ref, w_ref, b_ref, dp_in, dx_ref, dw_ref, db_ref, xpad, dpad):
        del dp_in
        wv = w_ref[...]
        bv = b_ref[...]
        dpad[rc:rc + HALO, :] = jnp.zeros((HALO, CONV_CB), F32)

        def step(k, carry):
            db, d0, d1, d2, d3 = carry
            i = n - 1 - k
            r0 = pl.multiple_of(i * rc, rc)
            p0 = pl.multiple_of(jnp.maximum(r0 - HALO, 0), HALO)
            xpad[0:HALO, :] = jnp.where(i > 0, x_ref[pl.ds(p0, HALO), :].astype(F32), 0.0)
            xpad[HALO:HALO + rc, :] = x_ref[pl.ds(r0, rc), :].astype(F32)
            xk = [xpad[HALO - j:HALO - j + rc, :] for j in range(4)]
            pre = bv
            for j in range(4):
                pre = pre + xk[j] * wv[3 - j:4 - j]
            dpre = dy_ref[pl.ds(r0, rc), :] * _dsilu(pre)
            dpad[0:rc, :] = dpre
            acc = dpre * wv[3:4]
            for j in (1, 2, 3):
                acc = acc + dpad[j:j + rc, :] * wv[3 - j:4 - j]
            dx_ref[pl.ds(r0, rc), :] = acc.astype(dx_ref.dtype)
            dpad[rc:rc + HALO, :] = dpre[:HALO]
            return (db + _colsum(dpre), d0 + _colsum(dpre * xk[3]), d1 + _colsum(dpre * xk[2]),
                    d2 + _colsum(dpre * xk[1]), d3 + _colsum(dpre * xk[0]))

        z = jnp.zeros((1, CONV_CB), F32)
        db, d0, d1, d2, d3 = lax.fori_loop(0, n, step, (z, z, z, z, z))
        db_ref[...] = db
        dw_ref[...] = jnp.concatenate([d0, d1, d2, d3], axis=0)

    nb = XBC // CONV_CB
    return pl.pallas_call(
        body, name=name, grid=(nb,),
        in_specs=[pl.BlockSpec((L, CONV_CB), lambda j: (0, j + C_XBC // CONV_CB)),
                  pl.BlockSpec((L, CONV_CB), lambda j: (0, j)),
                  pl.BlockSpec((4, CONV_CB), lambda j: (0, j)), pl.BlockSpec((1, CONV_CB), lambda j: (0, j)),
                  pl.BlockSpec(memory_space=pl.ANY)],
        out_specs=[pl.BlockSpec((L, CONV_CB), lambda j: (0, j + C_XBC // CONV_CB)),
                   pl.BlockSpec((4, CONV_CB), lambda j: (0, j)), pl.BlockSpec((1, CONV_CB), lambda j: (0, j))],
        out_shape=[jax.ShapeDtypeStruct((L, NPROJ), BF16), jax.ShapeDtypeStruct((4, XBC), F32),
                   jax.ShapeDtypeStruct((1, XBC), F32)],
        scratch_shapes=[pltpu.VMEM((rc + HALO, CONV_CB), F32), pltpu.VMEM((rc + HALO, CONV_CB), F32)],
        input_output_aliases={4: 0},
        compiler_params=_params(("parallel",), VMEM_BIG))(proj, dy, w, b, dproj)


def _pool_fwd(proj, *, name):
    L = proj.shape[0]
    rc = _time_chunk(L)
    n = L // rc

    def body(x_ref, o_ref, pad):
        g = pl.program_id(0)
        pad[0:HALO, :] = jnp.zeros((HALO, PGW), F32)

        def fill(i, c):
            r0 = pl.multiple_of(i * rc, rc)
            pad[pl.ds(r0 + HALO, rc), :] = x_ref[pl.ds(r0, rc), :].astype(F32)
            return c

        lax.fori_loop(0, n, fill, 0)
        rows = lax.broadcasted_iota(jnp.int32, (rc, PGW), 0)

        for gi in range(4):
            win = 2 << gi

            @pl.when(g == gi)
            def _(gi=gi, win=win):
                def step(i, c):
                    r0 = pl.multiple_of(i * rc, rc)
                    ext = pad[pl.ds(r0, rc + HALO), :]
                    s = ext
                    sh = 1
                    while sh < win:
                        s = s + pltpu.roll(s, sh, 0)
                        sh *= 2
                    cnt = jnp.minimum(rows + (r0 + 1), win).astype(F32)
                    o_ref[pl.ds(r0, rc), :] = (s[HALO:] / cnt - ext[HALO:]).astype(o_ref.dtype)
                    return c

                lax.fori_loop(0, n, step, 0)

    return pl.pallas_call(
        body, name=name, grid=(4,),
        in_specs=[pl.BlockSpec((L, PGW), lambda j: (0, j + C_POOL // PGW))],
        out_specs=pl.BlockSpec((L, PGW), lambda j: (0, j)),
        out_shape=jax.ShapeDtypeStruct((L, POOL_W), BF16),
        scratch_shapes=[pltpu.VMEM((L + HALO, PGW), F32)],
        compiler_params=_params(("parallel",), VMEM_BIG))(proj)


def _pool_bwd(dpooled, dproj, *, name):
    L = dpooled.shape[0]
    rc = _time_chunk(L)
    n = L // rc

    def body(d_ref, dp_in, o_ref, pad):
        del dp_in
        g = pl.program_id(0)
        pad[L:L + HALO, :] = jnp.zeros((HALO, PGW), F32)
        rows = lax.broadcasted_iota(jnp.int32, (rc, PGW), 0)

        for gi in range(4):
            win = 2 << gi

            @pl.when(g == gi)
            def _(gi=gi, win=win):
                def fill(i, c):
                    r0 = pl.multiple_of(i * rc, rc)
                    cnt = jnp.minimum(rows + (r0 + 1), win).astype(F32)
                    pad[pl.ds(r0, rc), :] = d_ref[pl.ds(r0, rc), :] / cnt
                    return c

                lax.fori_loop(0, n, fill, 0)

                def step(i, c):
                    r0 = pl.multiple_of(i * rc, rc)
                    s = pad[pl.ds(r0, rc + HALO), :]
                    sh = 1
                    while sh < win:
                        s = s + pltpu.roll(s, rc + HALO - sh, 0)
                        sh *= 2
                    o_ref[pl.ds(r0, rc), :] = (s[:rc] - d_ref[pl.ds(r0, rc), :]).astype(o_ref.dtype)
                    return c

                lax.fori_loop(0, n, step, 0)

    return pl.pallas_call(
        body, name=name, grid=(4,),
        in_specs=[pl.BlockSpec((L, PGW), lambda j: (0, j)), pl.BlockSpec(memory_space=pl.ANY)],
        out_specs=pl.BlockSpec((L, PGW), lambda j: (0, j + C_POOL // PGW)),
        out_shape=jax.ShapeDtypeStruct((L, NPROJ), BF16),
        scratch_shapes=[pltpu.VMEM((L + HALO, PGW), F32)],
        input_output_aliases={1: 0},
        compiler_params=_params(("parallel",), VMEM_BIG))(dpooled, dproj)


_SPLIT_DT = jnp.bfloat16


def _ssd_consts():
    tri = np.tril(np.ones((Q, Q), np.float32))
    exp = np.zeros((128, DI), np.float32)
    for h in range(NH):
        exp[h, h * HP:(h + 1) * HP] = 1.0
    exp2 = np.concatenate([exp, exp], axis=0)
    return (jnp.asarray(tri, dtype=_SPLIT_DT), jnp.asarray(tri.T.copy(), dtype=_SPLIT_DT),
            jnp.asarray(exp2, dtype=_SPLIT_DT))


def _split(v, n):
    parts, r = [], v
    for _ in range(n):
        p = r.astype(_SPLIT_DT)
        parts.append(p)
        r = r - p.astype(F32)
    return parts


def _bdot(a, b, dims):
    return lax.dot_general(a, b, (dims, ((), ())), preferred_element_type=F32)


def _tri_sum(t_ref, v):
    r = _bdot(t_ref[...], jnp.concatenate(_split(v, 3), axis=1), NN)
    return r[:, :128] + r[:, 128:256] + r[:, 256:]


def _expand(v, e2_ref):
    return _bdot(jnp.concatenate(_split(v, 2), axis=1), e2_ref[...], NN)


def _reduce_heads(vals, eg):
    parts = []
    for v in vals:
        parts += _split(v, 2)
    r = _bdot(jnp.concatenate(parts, axis=0), eg, NT)
    return [r[2 * i * Q:(2 * i + 1) * Q] + r[(2 * i + 1) * Q:(2 * i + 2) * Q] for i in range(len(vals))]


def _ssd_common(xbc_ref, dtw_ref, arow_ref, t_ref, e_ref):
    dt = dtw_ref[:, :128]
    sig = dtw_ref[:, 128:]
    acs = _tri_sum(t_ref, dt * arow_ref[...])
    acs_x = _expand(acs, e_ref)
    dt_x = _expand(dt, e_ref)
    xs = xbc_ref[:, 0:DI]
    return sig, dt, acs, acs.T, acs_x, dt_x, xs


CONV_SLAB = 512


def _ssd_fwd(raw, dtp, cw, cb, arow, dsk_x, *, name):
    L = raw.shape[0]
    nc = L // Q
    tri, _, expand = _ssd_consts()

    def body(raw_ref, halo_ref, cw_ref, cb_ref, dtw_ref, arow_ref, dsk_ref, t_ref, e_ref,
             y_ref, hs_ref, xbc_ref, h_scr, cpad):
        c = pl.program_id(0)

        @pl.when(c == 0)
        def _():
            h_scr[...] = jnp.zeros_like(h_scr)

        cpad[0:8, :] = jnp.where(c > 0, halo_ref[...], 0.0)
        cpad[8:8 + Q, :] = raw_ref[...]
        for lo in range(0, XBC, CONV_SLAB):
            sl = slice(lo, lo + CONV_SLAB)
            acc = cb_ref[:, sl]
            for j in range(4):
                acc = acc + cpad[8 - j:8 - j + Q, sl] * cw_ref[3 - j:4 - j, sl]
            xbc_ref[:, sl] = acc * _sigmoid(acc)

        _, dt, acs, acs_t, acs_x, dt_x, xs = _ssd_common(xbc_ref, dtw_ref, arow_ref, t_ref, e_ref)
        xdt = xs * dt_x
        eacs = jnp.exp(acs_x)
        acs_last = acs_x[Q - 1:Q, :]
        dec = jnp.exp(acs_last - acs_x)
        hs_ref[0] = h_scr[...].astype(hs_ref.dtype)
        causal = lax.broadcasted_iota(jnp.int32, (Q, Q), 0) >= lax.broadcasted_iota(jnp.int32, (Q, Q), 1)
        first = lax.broadcasted_iota(jnp.int32, (Q, 128), 1) < HP
        for g in range(NG):
            bg = xbc_ref[:, DI + g * NS:DI + (g + 1) * NS]
            cg = xbc_ref[:, DI + NG * NS + g * NS:DI + NG * NS + (g + 1) * NS]
            s = _dot(cg, bg, NT)
            sl = slice(g * GW, (g + 1) * GW)
            hg = h_scr[:, sl]
            yoff = _dot(cg, hg, NN) * eacs[:, sl]
            st = _dot(bg, xdt[:, sl] * dec[:, sl], TN)
            h_scr[:, sl] = hg * eacs[Q - 1:Q, sl] + st
            for j in range(4):
                lo = g * GW + j * 128
                xb = xdt[:, lo:lo + 128]
                yp = yoff[:, j * 128:(j + 1) * 128] + dsk_ref[:, lo:lo + 128] * xs[:, lo:lo + 128]
                for e in range(2):
                    h = g * 8 + j * 2 + e
                    lm = jnp.exp(jnp.where(causal, acs[:, h:h + 1] - acs_t[h:h + 1, :], NEG))
                    xm = jnp.where(first if e == 0 else jnp.logical_not(first), xb, 0.0)
                    yp = yp + _dot(s * lm, xm, NN)
                y_ref[:, lo:lo + 128] = yp.astype(y_ref.dtype)

    const = lambda c: (0, 0)
    return pl.pallas_call(
        body, name=name, grid=(nc,),
        in_specs=[pl.BlockSpec((Q, XBC), lambda c: (c, 0)),
                  pl.BlockSpec((8, XBC), lambda c: (jnp.maximum(c * (Q // 8) - 1, 0), 0)),
                  pl.BlockSpec((4, XBC), const), pl.BlockSpec((1, XBC), const),
                  pl.BlockSpec((Q, DT_PAD), lambda c: (c, 0)),
                  pl.BlockSpec((1, 128), const), pl.BlockSpec((1, DI), const),
                  pl.BlockSpec((Q, Q), const), pl.BlockSpec((256, DI), const)],
        out_specs=[pl.BlockSpec((Q, DI), lambda c: (c, 0)), pl.BlockSpec((1, NS, DI), lambda c: (c, 0, 0)),
                   pl.BlockSpec((Q, XBC), lambda c: (c, 0))],
        out_shape=[jax.ShapeDtypeStruct((L, DI), BF16), jax.ShapeDtypeStruct((nc, NS, DI), F32),
                   jax.ShapeDtypeStruct((L, XBC), F32)],
        scratch_shapes=[pltpu.VMEM((NS, DI), F32), pltpu.VMEM((8 + Q, XBC), F32)],
        compiler_params=_params(("arbitrary",), VMEM_BIG))(raw, raw, cw, cb, dtp, arow, dsk_x, tri, expand)


def _ssd_bwd(dy, xbc, dtp, hs, arow, dsk_x, dproj, *, name):
    L = xbc.shape[0]
    nc = L // Q
    tri, triu, expand = _ssd_consts()

    def body(dy_ref, xbc_ref, dtw_ref, hs_ref, arow_ref, dsk_ref, t_ref, u_ref, e_ref, dp_in,
             dxbc_ref, ddtw_ref, da_ref, ddx_ref, ddtb_ref, dh_scr):
        del dp_in
        i = pl.program_id(0)

        @pl.when(i == 0)
        def _():
            dh_scr[...] = jnp.zeros_like(dh_scr)

        sig, dt, acs, acs_t, acs_x, dt_x, xs = _ssd_common(xbc_ref, dtw_ref, arow_ref, t_ref, e_ref)
        dyv = dy_ref[...]
        xdt = xs * dt_x
        eacs = jnp.exp(acs_x)
        acs_last = acs_x[Q - 1:Q, :]
        dec = jnp.exp(acs_last - acs_x)
        gy = dyv * eacs
        causal = lax.broadcasted_iota(jnp.int32, (Q, Q), 0) >= lax.broadcasted_iota(jnp.int32, (Q, Q), 1)
        first = lax.broadcasted_iota(jnp.int32, (Q, 128), 1) < HP
        lane_h = lax.broadcasted_iota(jnp.int32, (Q, 128), 1)
        sub_h = lax.broadcasted_iota(jnp.int32, (128, Q), 0)
        last_row = lax.broadcasted_iota(jnp.int32, (Q, GW), 0) == Q - 1
        dacs = jnp.zeros((Q, 128), F32)
        dacs_t = jnp.zeros((128, Q), F32)
        ddt = jnp.zeros((Q, 128), F32)
        for g in range(NG):
            bg = xbc_ref[:, DI + g * NS:DI + (g + 1) * NS]
            cg = xbc_ref[:, DI + NG * NS + g * NS:DI + NG * NS + (g + 1) * NS]
            s = _dot(cg, bg, NT)
            sl = slice(g * GW, (g + 1) * GW)
            hg = hs_ref[0, :, sl].astype(F32)
            dhn = dh_scr[:, sl]
            eal = eacs[Q - 1:Q, sl]
            gg = gy[:, sl]
            dax = gg * _dot(cg, hg, NN)
            dcg = _dot(gg, hg, NT)
            dh_scr[:, sl] = _dot(cg, gg, TN) + dhn * eal
            dal = eal * _colsum(dhn * hg)
            xdd = xdt[:, sl] * dec[:, sl]
            dbg = _dot(xdd, dhn, NT)
            wv = _dot(bg, dhn, NN)
            dd = wv * xdd
            dax = dax - dd
            dal = dal + _colsum(dd)
            dax = dax + jnp.where(last_row, dal, 0.0)
            dxdt_g = wv * dec[:, sl]
            ds = jnp.zeros((Q, Q), F32)
            dxdt_blocks = []
            for j in range(4):
                lo = g * GW + j * 128
                xb = xdt[:, lo:lo + 128]
                dyb = dyv[:, lo:lo + 128]
                dxb = dxdt_g[:, j * 128:(j + 1) * 128]
                for e in range(2):
                    h = g * 8 + j * 2 + e
                    lm = jnp.exp(jnp.where(causal, acs[:, h:h + 1] - acs_t[h:h + 1, :], NEG))
                    m = s * lm
                    dym = jnp.where(first if e == 0 else jnp.logical_not(first), dyb, 0.0)
                    dm = _dot(dym, xb, NT)
                    r = dm * m
                    dacs = dacs + jnp.where(lane_h == h, jnp.sum(r, axis=1, keepdims=True), 0.0)
                    dacs_t = dacs_t + jnp.where(sub_h == h, _colsum(r), 0.0)
                    ds = ds + dm * lm
                    dxb = dxb + _dot(m, dym, TN)
                dxdt_blocks.append(dxb)
            dxdt = jnp.concatenate(dxdt_blocks, axis=1)
            dcg = dcg + _dot(ds, bg, NN)
            dbg = dbg + _dot(ds, cg, TN)
            dxbc_ref[:, DI + g * NS:DI + (g + 1) * NS] = dbg
            dxbc_ref[:, DI + NG * NS + g * NS:DI + NG * NS + (g + 1) * NS] = dcg
            dxbc_ref[:, sl] = dsk_ref[:, sl] * dyv[:, sl] + dxdt * dt_x[:, sl]
            ddt_g, dacs_g = _reduce_heads([dxdt * xs[:, sl], dax], e_ref[0:128, sl])
            ddt = ddt + ddt_g
            dacs = dacs + dacs_g
        dacs = dacs - dacs_t.T
        ddta = _tri_sum(u_ref, dacs)
        ddt = ddt + ddta * arow_ref[...]
        ddtw = jnp.where(lane_h < NH, ddt * sig, 0.0)
        ddtw_ref[...] = jnp.concatenate([ddtw, jnp.zeros((Q, DT_PAD - 128), F32)], axis=1).astype(ddtw_ref.dtype)
        _acc_out(da_ref, _colsum(ddta * dt), i)
        _acc_out(ddx_ref, _colsum(dyv * xs), i)
        _acc_out(ddtb_ref, _colsum(ddtw), i)

    rev = lambda c: (nc - 1 - c, 0)
    const = lambda c: (0, 0)
    return pl.pallas_call(
        body, name=name, grid=(nc,),
        in_specs=[pl.BlockSpec((Q, DI), rev), pl.BlockSpec((Q, XBC), rev),
                  pl.BlockSpec((Q, DT_PAD), rev),
                  pl.BlockSpec((1, NS, DI), lambda c: (nc - 1 - c, 0, 0)),
                  pl.BlockSpec((1, 128), const), pl.BlockSpec((1, DI), const),
                  pl.BlockSpec((Q, Q), const), pl.BlockSpec((Q, Q), const), pl.BlockSpec((256, DI), const),
                  pl.BlockSpec(memory_space=pl.ANY)],
        out_specs=[pl.BlockSpec((Q, XBC), rev),
                   pl.BlockSpec((Q, DT_PAD), lambda c: (nc - 1 - c, C_DT // DT_PAD)),
                   pl.BlockSpec((1, 128), const), pl.BlockSpec((1, DI), const), pl.BlockSpec((1, 128), const)],
        out_shape=[jax.ShapeDtypeStruct((L, XBC), F32), jax.ShapeDtypeStruct((L, NPROJ), BF16),
                   jax.ShapeDtypeStruct((1, 128), F32), jax.ShapeDtypeStruct((1, DI), F32),
                   jax.ShapeDtypeStruct((1, 128), F32)],
        scratch_shapes=[pltpu.VMEM((NS, DI), F32)],
        input_output_aliases={9: 1},
        compiler_params=_params(("arbitrary",), VMEM_BIG))(dy, xbc, dtp, hs, arow, dsk_x, tri, triu,
                                                          expand, dproj)


def _adam_update(wv, gv, mv, vv):
    c1 = 1.0 - ADAM_B1 ** ADAM_STEP
    c2 = 1.0 - ADAM_B2 ** ADAM_STEP
    mn = ADAM_B1 * mv + (1.0 - ADAM_B1) * gv
    vn = ADAM_B2 * vv + (1.0 - ADAM_B2) * (gv * gv)
    return -ADAM_LR * ((mn / c1) / (jnp.sqrt(vn / c2) + ADAM_EPS) + ADAM_WD * wv), mn, vn


def _adamw(w, g, m, v, *, name, tr=None):
    R = w.shape[0]
    rest = tuple(w.shape[1:])
    if tr is None:
        tr = _pick(R, (256, 128, 64, 32, 16, 8))
    assert R % tr == 0

    def body(w_ref, g_ref, m_ref, v_ref, d_ref, mo_ref, vo_ref):
        d_ref[...], mo_ref[...], vo_ref[...] = _adam_update(w_ref[...], g_ref[...], m_ref[...], v_ref[...])

    zeros = (0,) * len(rest)
    spec = pl.BlockSpec((tr,) + rest, lambda i: (i,) + zeros)
    return pl.pallas_call(body, name=name, grid=(R // tr,), in_specs=[spec] * 4, out_specs=[spec] * 3,
                          out_shape=[jax.ShapeDtypeStruct(w.shape, F32)] * 3,
                          compiler_params=_params(("parallel",)))(w, g, m, v)


def _adamw_small(svrow, g_conv, params, *, name):
    n = len(params)

    def body(*refs):
        sv_ref, gc_ref = refs[0], refs[1]
        ins, outs = refs[2:2 + 3 * n], refs[2 + 3 * n:]
        for p, (key, w, _, _) in enumerate(params):
            w_ref, m_ref, v_ref = ins[3 * p:3 * p + 3]
            g_ref, d_ref, mo_ref, vo_ref = outs[4 * p:4 * p + 4]
            gv = gc_ref[...] if key == "conv_w" else sv_ref[:, SV_OFF[key]:SV_OFF[key] + w.shape[1]]
            g_ref[...] = gv
            d_ref[...], mo_ref[...], vo_ref[...] = _adam_update(w_ref[...], gv, m_ref[...], v_ref[...])

    vm = pl.BlockSpec(memory_space=pltpu.VMEM)
    args = [svrow, g_conv]
    shapes = []
    for _, w, m, v in params:
        args += [w, m, v]
        shapes += [jax.ShapeDtypeStruct(w.shape, F32)] * 4
    res = pl.pallas_call(body, name=name, in_specs=[vm] * len(args), out_specs=[vm] * len(shapes),
                         out_shape=shapes)(*args)
    return {key: tuple(res[4 * p:4 * p + 4]) for p, (key, _, _, _) in enumerate(params)}


def _slab_sum(recv, *, tile, name):
    rows = recv.shape[1]
    assert rows % tile == 0 and tile % 16 == 0

    def body(r_ref, o_ref):
        acc = r_ref[0].astype(F32)
        for j in range(1, N_DEV):
            acc = acc + r_ref[j].astype(F32)
        o_ref[...] = acc

    return pl.pallas_call(body, name=name, grid=(rows // tile,),
                          in_specs=[pl.BlockSpec((N_DEV, tile, D), lambda i: (0, i, 0))],
                          out_specs=pl.BlockSpec((tile, D), lambda i: (i, 0)),
                          out_shape=jax.ShapeDtypeStruct((rows, D), F32),
                          compiler_params=_params(("parallel",)))(recv)


MESH = pl.DeviceIdType.MESH


def _coords():
    return lax.axis_index("x"), lax.axis_index("y"), lax.axis_index("c")


def _peer(k):
    x, y, c = _coords()
    px = 1 - x if k & 4 else x
    py = 1 - y if k & 2 else y
    pc = 1 - c if k & 1 else c
    return (px, py, pc), 4 * px + 2 * py + pc


def _rcopy(src, dst, ssem, rsem, dev):
    return pltpu.make_async_remote_copy(src_ref=src, dst_ref=dst, send_sem=ssem, recv_sem=rsem,
                                        device_id=dev, device_id_type=MESH)


def _exchange_all(src_of, dst_slot, send_sems, recv_sems):
    x, y, c = _coords()
    me = 4 * x + 2 * y + c
    sent = []
    for k in range(1, N_DEV):
        dev, pidx = _peer(k)
        cp = _rcopy(src_of(pidx), dst_slot(me), send_sems.at[k - 1], recv_sems.at[k - 1], dev)
        cp.start()
        sent.append(cp)
    for k in range(1, N_DEV):
        dev, pidx = _peer(k)
        _rcopy(src_of(pidx), dst_slot(pidx), send_sems.at[k - 1], recv_sems.at[k - 1], dev).wait_recv()
    for cp in sent:
        cp.wait_send()


def _rows_of_slots(buf, nslots):
    rows = lax.broadcasted_iota(jnp.int32, (8, buf.shape[-1]), 0)
    out = jnp.zeros((8, buf.shape[-1]), F32)
    for j in range(nslots):
        out = out + jnp.where(rows == j, buf[j], 0.0)
    return out


def _exchange_start(src_of, dst_slot, send_sems, recv_sems):
    x, y, c = _coords()
    me = 4 * x + 2 * y + c
    sent = []
    for k in range(1, N_DEV):
        dev, pidx = _peer(k)
        cp = _rcopy(src_of(pidx), dst_slot(me), send_sems.at[k - 1], recv_sems.at[k - 1], dev)
        cp.start()
        sent.append(cp)
    return sent


def _exchange_finish(sent, src_of, dst_slot, send_sems, recv_sems):
    for k in range(1, N_DEV):
        dev, pidx = _peer(k)
        _rcopy(src_of(pidx), dst_slot(pidx), send_sems.at[k - 1], recv_sems.at[k - 1], dev).wait_recv()
    for cp in sent:
        cp.wait_send()


def _ada_gather(c, w_ada, b_r, slab, *, name):
    wloc = w_ada.shape[1]

    def body(c_ref, w_ref, b_ref, x_ref, mod_ref, call_ref, out_ref,
             csrc, cbuf, psrc, pbuf, s1, r1, s2, r2, send_sems, recv_sems, local_sem):
        x, y, cc = _coords()
        me_i = 4 * x + 2 * y + cc
        me, sibling = (x, y, cc), (x, y, 1 - cc)
        chips = [(1 - x, y), (x, 1 - y), (1 - x, 1 - y)]

        def slot(px, py, pc):
            return out_ref.at[4 * px + 2 * py + pc]

        def copy(k, block, to, src=None):
            return _rcopy(slot(*block) if src is None else src, slot(*block), send_sems.at[k], recv_sems.at[k], to)

        csrc[...] = jnp.broadcast_to(c_ref[...], (8, D))
        cbuf[me_i] = csrc[...]
        c_of, c_slot = (lambda p: csrc), (lambda s: cbuf.at[s])
        sent1 = _exchange_start(c_of, c_slot, s1, r1)

        mine = pltpu.make_async_copy(x_ref, slot(*me), local_sem)
        mine.start()
        first = [copy(0, me, sibling, src=x_ref)]
        first += [copy(1 + j, me, (*chip, cc), src=x_ref) for j, chip in enumerate(chips)]
        for cp in first:
            cp.start()

        _exchange_finish(sent1, c_of, c_slot, s1, r1)
        call = _rows_of_slots(cbuf, N_DEV)
        call_ref[...] = call
        prod = _dot_hi(_silu(call), w_ref[...])
        for b in range(N_DEV):
            psrc[b] = jnp.broadcast_to(prod[b:b + 1, :], (8, wloc))
        pbuf[me_i] = psrc[me_i]
        p_of, p_slot = (lambda p: psrc.at[p]), (lambda s: pbuf.at[s])
        sent2 = _exchange_start(p_of, p_slot, s2, r2)

        passed = [copy(4 + j, (*chip, cc), sibling) for j, chip in enumerate(chips)]
        for j, chip in enumerate(chips):
            copy(1 + j, (*chip, cc), me).wait_recv()
            passed[j].start()
        copy(0, sibling, me).wait_recv()
        for j, chip in enumerate(chips):
            copy(4 + j, (*chip, 1 - cc), me).wait_recv()

        _exchange_finish(sent2, p_of, p_slot, s2, r2)
        mod_ref[...] = _rows_of_slots(pbuf, N_DEV) + b_ref[...]
        for cp in first + passed:
            cp.wait_send()
        mine.wait()

    vm = pl.BlockSpec(memory_space=pltpu.VMEM)
    anyspec = pl.BlockSpec(memory_space=pl.ANY)
    return pl.pallas_call(
        body, name=name, in_specs=[vm, vm, vm, anyspec], out_specs=[vm, vm, anyspec],
        out_shape=[jax.ShapeDtypeStruct((N_DEV, wloc), F32), jax.ShapeDtypeStruct((N_DEV, D), F32),
                   jax.ShapeDtypeStruct((N_DEV,) + slab.shape, slab.dtype)],
        scratch_shapes=[pltpu.VMEM((8, D), F32), pltpu.VMEM((N_DEV, 8, D), F32),
                        pltpu.VMEM((N_DEV, 8, wloc), F32), pltpu.VMEM((N_DEV, 8, wloc), F32),
                        pltpu.SemaphoreType.DMA((N_DEV - 1,)), pltpu.SemaphoreType.DMA((N_DEV - 1,)),
                        pltpu.SemaphoreType.DMA((N_DEV - 1,)), pltpu.SemaphoreType.DMA((N_DEV - 1,)),
                        pltpu.SemaphoreType.DMA((7,)), pltpu.SemaphoreType.DMA((7,)), pltpu.SemaphoreType.DMA],
        compiler_params=pltpu.CompilerParams(vmem_limit_bytes=VMEM_BIG))(c, w_ada, b_r, slab)


_HBM =pl.BlockSpec(memory_space=pltpu.HBM)
_SEM = pl.BlockSpec(memory_space=pltpu.SEMAPHORE)
_EFFECT = pltpu.SideEffectType.DATAFLOW_SIDE_EFFECTING


def _xchg_src(src_ref, pidx, per_peer):
    return src_ref.at[pidx] if per_peer else src_ref


def _xchg_start(src, *, per_peer, name):
    rows = src.shape[-2]
    land_shape = (N_DEV, rows, D)

    def body(src_ref, land_ref, send_sems, recv_sems, src_thru, land_thru, token):
        del src_thru, land_thru
        x, y, c = _coords()
        me = 4 * x + 2 * y + c
        for k in range(1, N_DEV):
            dev, pidx = _peer(k)
            _rcopy(_xchg_src(src_ref, pidx, per_peer), land_ref.at[me], send_sems.at[k - 1],
                   recv_sems.at[k - 1], dev).start()
        token[...] = jnp.zeros_like(token)

    return pl.pallas_call(
        body, name=name,
        out_shape=(pltpu.SemaphoreType.DMA((N_DEV - 1,)), pltpu.SemaphoreType.DMA((N_DEV - 1,)),
                   pltpu.HBM(src.shape, src.dtype), pltpu.HBM(land_shape, src.dtype),
                   jax.ShapeDtypeStruct((8, 128), F32)),
        in_specs=(_HBM, _HBM),
        out_specs=(_SEM, _SEM, _HBM, _HBM, pl.BlockSpec(memory_space=pltpu.VMEM)),
        input_output_aliases={0: 2, 1: 3},
        compiler_params=pltpu.CompilerParams(has_side_effects=_EFFECT),
    )(pltpu.with_memory_space_constraint(src, pltpu.HBM),
      pltpu.with_memory_space_constraint(lax.empty(land_shape, src.dtype), pltpu.HBM))


def _xchg_wait(started, after, *, per_peer, name):
    send_sems, recv_sems, src_thru, land_thru, _ = started

    def body(src_ref, land_ref, send_sems, recv_sems, after_ref, src_dead, got_ref):
        del after_ref, src_dead, got_ref
        for k in range(1, N_DEV):
            dev, pidx = _peer(k)
            cp = _rcopy(_xchg_src(src_ref, pidx, per_peer), land_ref.at[pidx], send_sems.at[k - 1],
                        recv_sems.at[k - 1], dev)
            cp.wait_send()
            cp.wait_recv()

    return pl.pallas_call(
        body, name=name,
        out_shape=(pltpu.HBM(src_thru.shape, src_thru.dtype), pltpu.HBM(land_thru.shape, land_thru.dtype)),
        in_specs=(_HBM, _HBM, _SEM, _SEM, pl.BlockSpec(memory_space=pl.ANY)),
        out_specs=(_HBM, _HBM),
        input_output_aliases={0: 0, 1: 1},
        compiler_params=pltpu.CompilerParams(has_side_effects=_EFFECT),
    )(src_thru, land_thru, send_sems, recv_sems, after)


def _dep(token):
    return (token, (8, 128), lambda i, j, k: (0, 0))


def _small_allsum(sv, *, name):
    def body(sv_ref, all_ref, sum_ref, send_sems, recv_sems):
        x, y, c = _coords()
        me = 4 * x + 2 * y + c
        all_ref[me] = sv_ref[...]
        _exchange_all(lambda p: sv_ref, lambda s: all_ref.at[s], send_sems, recv_sems)
        acc = all_ref[0]
        for j in range(1, N_DEV):
            acc = acc + all_ref[j]
        sum_ref[...] = acc

    vm = pl.BlockSpec(memory_space=pltpu.VMEM)
    return pl.pallas_call(
        body, name=name, in_specs=[vm], out_specs=[vm, vm],
        out_shape=[jax.ShapeDtypeStruct((N_DEV, SV_ROWS, 128), F32), jax.ShapeDtypeStruct((SV_ROWS, 128), F32)],
        scratch_shapes=[pltpu.SemaphoreType.DMA((7,)), pltpu.SemaphoreType.DMA((7,))],
    )(sv)


def _ada_bwd(call, dmod_loc, *, name):
    wloc = dmod_loc.shape[1]

    def body(c_ref, d_ref, o_ref):
        o_ref[...] = _dot_hi(_silu(c_ref[...]), d_ref[...], TN)

    vm = pl.BlockSpec(memory_space=pltpu.VMEM)
    return pl.pallas_call(body, name=name, in_specs=[vm, vm], out_specs=vm,
                          out_shape=jax.ShapeDtypeStruct((D, wloc), F32),
                          compiler_params=pltpu.CompilerParams(vmem_limit_bytes=VMEM_BIG))(call, dmod_loc)


def _pad_rows(a, rows):
    return jnp.pad(a, ((0, rows - a.shape[0]), (0, 0)))


IN_SHIFT = tuple((IN_ROWS * j) % 16 for j in range(N_DEV))
IN_BASE = tuple(IN_ROWS * j - IN_SHIFT[j] for j in range(N_DEV))
IN_SEGMENTS = ((2048, XBC, C_XBC), (5152, 1024, C_POOL), (0, 2048, C_Z), (6176, 2048, C_GATE), (5120, 32, C_DT))


def _global_pieces(gs):
    pieces = []
    for j in range(N_DEV):
        lo, hi = 0, IN_ROWS_P
        if j > 0 and IN_BASE[j - 1] + IN_ROWS_P > IN_BASE[j]:
            pieces.append((IN_BASE[j], 16, gs[j - 1, IN_ROWS_P - 16:IN_ROWS_P] + gs[j, 0:16]))
            lo = 16
        if j + 1 < N_DEV and IN_BASE[j] + IN_ROWS_P > IN_BASE[j + 1]:
            hi = IN_ROWS_P - 16
        pieces.append((IN_BASE[j] + lo, hi - lo, gs[j, lo:hi]))
    return pieces


def _reorder_in_rows(gs):
    pieces = _global_pieces(gs)
    parts = []
    for lo, n, _ in IN_SEGMENTS:
        for p0, pn, arr in pieces:
            a, b = max(lo, p0), min(lo + n, p0 + pn)
            if a < b:
                parts.append(arr[a - p0:b - p0])
    parts.append(jnp.zeros((DT_PAD - 32, D), gs.dtype))
    return jnp.concatenate(parts, axis=0)


def _restore_in_shards(d):
    slabs = []
    for j in range(N_DEV):
        parts = []
        r, end = IN_BASE[j], IN_BASE[j] + IN_ROWS_P
        while r < end:
            lo, n, new = next(s for s in IN_SEGMENTS if s[0] <= r < s[0] + s[1])
            e = min(end, lo + n)
            parts.append(d[new + r - lo:new + e - lo])
            r = e
        slabs.append(jnp.concatenate(parts, axis=0))
    return jnp.stack(slabs, axis=0)


def _pack_sv(parts):
    flat = []
    for n, size in SV_PARTS:
        v = parts[n].reshape(-1).astype(F32)
        flat.append(jnp.pad(v, (0, size - v.shape[0])))
    v = jnp.concatenate(flat)
    return jnp.pad(v, (0, SV_ROWS * 128 - v.shape[0])).reshape(SV_ROWS, 128)


def _sv_get(flat, n, size):
    return flat[SV_OFF[n]:SV_OFF[n] + size]


def kernel(x, c, w_ada, b_ada, norm_mix_w, w_in, conv_w, conv_b, dt_bias, a_log, d_skip, ssd_norm_w, w_branch_ssd, pool_w, pool_scale, w_branch_pool, w_out, norm_mlp_w, w_up, w_down, norm_final_w, loss_target, m_w_ada, m_b_ada, m_norm_mix_w, m_w_in, m_conv_w, m_conv_b, m_dt_bias, m_a_log, m_d_skip, m_ssd_norm_w, m_w_branch_ssd, m_pool_w, m_pool_scale, m_w_branch_pool, m_w_out, m_norm_mlp_w, m_w_up, m_w_down, m_norm_final_w, v_w_ada, v_b_ada, v_norm_mix_w, v_w_in, v_conv_w, v_conv_b, v_dt_bias, v_a_log, v_d_skip, v_ssd_norm_w, v_w_branch_ssd, v_pool_w, v_pool_scale, v_w_branch_pool, v_w_out, v_norm_mlp_w, v_w_up, v_w_down, v_norm_final_w):
    xs_ = x[0]
    tgt = loss_target[0]
    L = xs_.shape[0]
    me = 4 * lax.axis_index("x") + 2 * lax.axis_index("y") + lax.axis_index("c")
    wloc = w_ada.shape[2]

    conv_bits = lax.bitcast_convert_type(conv_w[0], SLAB_DT).reshape(3, D)
    in_shift = (IN_ROWS * me) % 16
    slab_in = lax.dynamic_update_slice(jnp.zeros((IN_ROWS_P, D), SLAB_DT), w_in[0].T.astype(SLAB_DT),
                                       (in_shift, 0))
    slab_in = jnp.concatenate([slab_in, _pad_rows(conv_bits, CONV_ROWS)], axis=0)
    slab_rest = jnp.concatenate([
        w_branch_ssd[0].astype(SLAB_DT),
        pool_w[0].reshape(32, D).astype(SLAB_DT),
        w_branch_pool[0].astype(SLAB_DT),
        w_out[0].astype(SLAB_DT),
        w_up[0].T.astype(SLAB_DT),
        w_down[0].astype(SLAB_DT)], axis=0)
    mod_p, c_all, gs_in = _ada_gather(c, w_ada[0], b_ada.reshape(N_DEV, wloc), slab_in,
                                      name="ada_gather_w_in")
    mod = mod_p.reshape(6, D)
    shift_m, scale_m, gate_m, shift_f, scale_f, gate_f = [mod[i:i + 1] for i in range(6)]
    slab_rest, gs_in = lax.optimization_barrier((slab_rest, gs_in))
    rest_started = _xchg_start(slab_rest, per_peer=False, name="gather_rest_start")
    gather_token = rest_started[4]

    w_in_t = _reorder_in_rows(gs_in)
    conv_full = lax.bitcast_convert_type(
        gs_in[:, IN_ROWS_P:IN_ROWS_P + 3].reshape(N_DEV, 4, XBC // N_DEV, 2), F32)
    conv_full = conv_full.transpose(1, 0, 2).reshape(4, XBC)

    dtb = jnp.pad(dt_bias, ((0, 0), (0, 128 - NH)))
    arow = jnp.pad(-jnp.exp(a_log), ((0, 0), (0, 128 - NH)))
    dsk_x = jnp.repeat(d_skip, HP, axis=1)

    tm = _pick(L, (1024, 512, 256, 128))
    tm2 = _pick(L, (2048, 1024, 512, 256, 128))
    tkl = _pick(L, (4096, 2048, 1024, 512, 256, 128))
    tkl2 = _pick(L, (2048, 1024, 512, 256, 128))

    tmh = _pick(L, (512, 256, 128))
    zcol = C_Z // DI
    gcol = C_GATE // (2 * D)

    def whole_rows(w):
        return lambda t: ((L, w), BF16, (t, w), lambda i, j, k: (i, 0))

    def norm1_pro(x_ref, ex, outs, j):
        @pl.when(j == 0)
        def _():
            xv = x_ref[...]
            r = lax.rsqrt(jnp.mean(xv * xv, axis=-1, keepdims=True) + EPS)
            outs[1][...] = (xv * r * ex[0][...] * (1.0 + ex[1][...]) + ex[2][...]).astype(outs[1].dtype)

        return outs[1][...]

    def proj_ep(acc, ex, outs):
        outs[0][...] = acc

        @pl.when(pl.program_id(1) == NPROJ // 768 - 1)
        def _():
            pre = acc[:, 768 - DT_PAD:768 - DT_PAD + 128] + ex[3][...]
            outs[2][...] = jnp.concatenate([_softplus(pre), _sigmoid(pre)], axis=1)

    proj, h1, dtp = _mm(
        xs_, w_in_t, "nt", name="in_proj", tm=tm2, tn=768, tk=D,
        extras=[(norm_mix_w, *_vecs()), (scale_m, *_vecs()), (shift_m, *_vecs()), (dtb, *_vecs(128)),
                _dep(gather_token)],
        outs=[F32, whole_rows(D)(tm2), ((L, DT_PAD), F32, (tm2, DT_PAD), lambda i, j, k: (i, 0))],
        prologue=norm1_pro, epilogue=proj_ep)
    xbc_raw = proj
    y_ssm, hs, xbc = _ssd_fwd(xbc_raw, dtp, conv_full, conv_b, arow, dsk_x, name="ssd_fwd")

    slab_rest, gs = _xchg_wait(rest_started, y_ssm, per_peer=False, name="gather_rest_wait")
    gs = lax.dynamic_update_slice(gs, slab_rest[None], (me, 0, 0))

    def part(n, rows):
        return gs[:, REST_OFF[n]:REST_OFF[n] + rows]

    w_bssd = part("bssd", 256).reshape(DI, D)
    w_pool = part("pool", 32).reshape(N_DEV, 4, 32, PGW).transpose(1, 0, 2, 3).reshape(POOL_W, PGW)
    w_bpool = part("bpool", 128).reshape(POOL_W, D)
    w_o = part("out", 128).reshape(D, D)
    w_up_t = part("up", 512).reshape(DFF, D)
    w_dn = part("down", 512).reshape(DFF, D)

    def gnorm_pro(y_ref, ex, outs, j):
        z_ref, w_ref = ex
        yg = y_ref[...].astype(F32) * _silu(z_ref[...].astype(F32))
        segs = []
        for k in range(NG):
            sl = slice(k * GW, (k + 1) * GW)
            seg = yg[:, sl]
            r = lax.rsqrt(jnp.mean(seg * seg, axis=-1, keepdims=True) + EPS)
            segs.append((seg * r * w_ref[:, sl]).astype(BF16))
        yn_v = jnp.concatenate(segs, axis=1)
        outs[1][...] = yn_v
        return yn_v

    y_ssd, yn = _mm(y_ssm, w_bssd, "nn", name="branch_ssd", tm=tmh, tn=D, tk=DI,
                    extras=[(proj, *_rows(tmh, DI, zcol)), (ssd_norm_w, *_vecs(DI))],
                    outs=[BF16, whole_rows(DI)(tmh)], prologue=gnorm_pro)
    pooled = _pool_fwd(proj, name="pool_fwd")
    wp_spec = ((POOL_W, PGW), lambda i, j, k: (0, 0))

    def pool_pro(a_ref, ex, outs, j):
        wp_ref, s_ref = ex
        segs = []
        for g in range(4):
            sl = slice(g * PGW, (g + 1) * PGW)
            p = _dot(a_ref[:, sl], wp_ref[sl, :], NN)
            outs[1][:, sl] = p.astype(BF16)
            segs.append((p * s_ref[:, sl]).astype(BF16))
        yp1_v = jnp.concatenate(segs, axis=1)
        outs[2][...] = yp1_v
        return yp1_v

    y_pool, yp0, yp1 = _mm(pooled, w_bpool, "nn", name="branch_pool", tm=tm, tn=D, tk=D,
                           extras=[(w_pool, *wp_spec), (pool_scale, *_vecs())],
                           outs=[BF16, whole_rows(D)(tm), whole_rows(D)(tm)], prologue=pool_pro)

    def merge_pro(a_ref, ex, outs, j):
        s = _sigmoid(ex[1][...].astype(F32))
        mv = (s[:, :D] * a_ref[...].astype(F32) + s[:, D:] * ex[0][...].astype(F32)).astype(BF16)
        outs[3][...] = mv
        return mv

    mix, x1, h2, m = _mm(y_ssd, w_o, "nn", name="out_proj", tm=tmh, tn=D, tk=D,
                         extras=[(y_pool, *_rows(tmh)), (proj, *_rows(tmh, 2 * D, gcol)),
                                 (xs_, *_rows(tmh)), (gate_m, *_vecs()), (norm_mlp_w, *_vecs()),
                                 (scale_f, *_vecs()), (shift_f, *_vecs())],
                         outs=[BF16, F32, BF16, whole_rows(D)(tmh)], prologue=merge_pro,
                         epilogue=lambda acc, ex, outs: _ep_resid_norm(acc, ex[2:], outs[:3]))

    def relu2(acc, ex, outs):
        r = jnp.maximum(acc, 0.0)
        outs[0][...] = acc.astype(BF16)
        outs[1][...] = (r * r).astype(BF16)

    up, act = _mm(h2, w_up_t, "nt", name="mlp_up", outs=[BF16, BF16], tm=tm2, tn=1024, tk=D, epilogue=relu2)

    dx2, ddown, loss_p, dnwf, dgate_f = _mm(
        act, w_dn, "nn", name="mlp_down", tm=tmh, tn=D, tk=DFF,
        extras=[(x1, *_rows(tmh)), (tgt, *_rows(tmh)), (gate_f, *_vecs()), (norm_final_w.reshape(1, D), *_vecs())],
        outs=[F32, BF16, _sum_out(128), _sum_out(), _sum_out()], epilogue=_ep_final)

    def drelu2(acc, ex, outs):
        outs[0][...] = (acc * (2.0 * jnp.maximum(ex[0][...].astype(F32), 0.0))).astype(BF16)

    def dep_last(ep):
        return lambda acc, ex, outs: ep(acc, ex[:-1], outs)

    dup = _mm(ddown, w_dn, "nt", name="mlp_down_dx", outs=[BF16], tm=tm2, tn=1024, tk=D,
              extras=[(up, (tm2, 1024), lambda i, j, k: (i, j))], epilogue=drelu2)
    g_dn = _mm(act, ddown, "tn", name="mlp_down_dw", outs=[SLAB_DT], tm=1024, tn=D, tk=tkl)
    g_up_t = _mm(dup, h2, "tn", name="mlp_up_dw", outs=[SLAB_DT], tm=1024, tn=D, tk=tkl)
    gslab_mlp = jnp.concatenate([g_up_t.reshape(N_DEV, 512, D), g_dn.reshape(N_DEV, 512, D)], axis=1)
    mlp_started = _xchg_start(gslab_mlp, per_peer=True, name="scatter_mlp_start")
    dx1, p2, q2, dmix, dgate_m = _mm(
        dup, w_up_t, "nn", name="mlp_up_dx", tm=tmh, tn=D, tk=DFF,
        extras=[(x1, *_rows(tmh)), (dx2, *_rows(tmh)), (norm_mlp_w, *_vecs()), (scale_f, *_vecs()),
                (mix, *_rows(tmh)), (gate_m, *_vecs()), _dep(mlp_started[4])],
        outs=[F32, _sum_out(), _sum_out(), BF16, _sum_out()], epilogue=dep_last(_ep_norm_bwd))
    gcol = C_GATE // (2 * D)
    dy_ssd, dy_pool, dproj = _mm(
        dmix, w_o, "nt", name="out_proj_dx", tm=tmh, tn=D, tk=D,
        extras=[(y_ssd, *_rows(tmh)), (y_pool, *_rows(tmh)), (proj, *_rows(tmh, 2 * D, gcol))],
        outs=[BF16, BF16, ((L, NPROJ), BF16, *_rows(tmh, 2 * D, gcol))], epilogue=_ep_merge_bwd)
    g_o = _mm(m, dmix, "tn", name="out_proj_dw", outs=[SLAB_DT], tm=D, tn=D, tk=tkl)
    zcol = C_Z // DI
    dy_ssm, dproj, d_snw = _mm(
        dy_ssd, w_bssd, "nt", name="branch_ssd_dx", tm=tmh, tn=DI, tk=D,
        extras=[(y_ssm, *_rows(tmh, DI)), (proj, *_rows(tmh, DI, zcol)), (ssd_norm_w, *_vecs(DI)),
                (dproj, None, None)],
        outs=[F32, ((L, NPROJ), BF16, *_rows(tmh, DI, zcol)), _sum_out(DI)],
        epilogue=_ep_gated_norm_bwd, aliases={3: 1})
    g_bssd = _mm(yn, dy_ssd, "tn", name="branch_ssd_dw", outs=[SLAB_DT], tm=1024, tn=D, tk=tkl)
    dxbc, dproj, d_a, d_dx, d_dtb = _ssd_bwd(dy_ssm, xbc, dtp, hs, arow, dsk_x, dproj, name="ssd_bwd")
    dproj, d_cw, d_cb = _conv_bwd(xbc_raw, dxbc, conv_full, conv_b, dproj, name="conv_bwd")
    def pool_bwd_ep(acc, ex, outs):
        y_ref, s_ref, wp_ref = ex
        o_ref, ds_ref, dpool_ref = outs
        dyp0_v = (acc * s_ref[...]).astype(BF16)
        o_ref[...] = dyp0_v
        _acc_out(ds_ref, _colsum(acc * y_ref[...].astype(F32)), _row_step())
        for g in range(4):
            sl = slice(g * PGW, (g + 1) * PGW)
            dpool_ref[:, sl] = _dot(dyp0_v[:, sl], wp_ref[sl, :], NT)

    dyp0, d_ps, dpooled = _mm(dy_pool, w_bpool, "nt", name="branch_pool_dx", tm=tm, tn=D, tk=D,
                              extras=[(yp0, *_rows(tm)), (pool_scale, *_vecs()), (w_pool, *wp_spec)],
                              outs=[BF16, _sum_out(), F32], epilogue=pool_bwd_ep)
    g_bpool = _mm(yp1, dy_pool, "tn", name="branch_pool_dw", outs=[SLAB_DT], tm=D, tn=D, tk=tkl)
    g_pool = _mm_pool_tn(pooled, dyp0, name="pool_mix_dw", tk=tkl)
    gslab_mix = jnp.concatenate([
        g_bssd.reshape(N_DEV, 256, D),
        g_pool.reshape(4, N_DEV, 32, PGW).transpose(1, 0, 2, 3).reshape(N_DEV, 32, D).astype(SLAB_DT),
        g_bpool.reshape(N_DEV, 128, D),
        g_o.reshape(N_DEV, 128, D)], axis=1)
    mix_started = _xchg_start(gslab_mix, per_peer=True, name="scatter_mix_start")
    dproj = _pool_bwd(dpooled, dproj, name="pool_bwd")
    g_in_t = _mm(dproj, h1, "tn", name="in_proj_dw", outs=[SLAB_DT], tm=1408, tn=D, tk=tkl2,
                 extras=[_dep(mix_started[4])])
    gslab_in = _restore_in_shards(g_in_t)
    in_started = _xchg_start(gslab_in, per_peer=True, name="scatter_in_start")
    tmq = _pick(L, (256, 128))
    grad_x, p1, q1 = _mm(
        dproj, w_in_t, "nn", name="in_proj_dx", tm=tmq, tn=D, tk=NPROJ,
        extras=[(xs_, *_rows(tmq)), (dx1, *_rows(tmq)), (norm_mix_w, *_vecs()), (scale_m, *_vecs()),
                _dep(in_started[4])],
        outs=[F32, _sum_out(), _sum_out()], epilogue=dep_last(_ep_norm_bwd))

    def landed(started, after, tile, name):
        src, land = _xchg_wait(started, after, per_peer=True, name=name + "_wait")
        own = lax.dynamic_slice_in_dim(src, me, 1, axis=0)
        return _slab_sum(lax.dynamic_update_slice(land, own, (me, 0, 0)), tile=tile, name=name + "_sum")

    gsum_mlp = landed(mlp_started, grad_x, 256, "scatter_mlp")
    gsum_mix = landed(mix_started, grad_x, 272, "scatter_mix")
    gsum_in = landed(in_started, grad_x, 208, "scatter_in")

    dmod = jnp.concatenate([q1, p1 * norm_mix_w, dgate_m, q2, p2 * norm_mlp_w, dgate_f], axis=1)
    d_alog = d_a[:, :NH] * (-jnp.exp(a_log))
    sv = _pack_sv({
        "b_ada": dmod, "norm_mix_w": p1 * (1.0 + scale_m), "conv_b": d_cb, "dt_bias": d_dtb[:, :NH],
        "a_log": d_alog, "d_skip": d_dx.reshape(NH, HP).sum(axis=1), "ssd_norm_w": d_snw,
        "pool_scale": d_ps, "norm_mlp_w": p2 * (1.0 + scale_f), "norm_final_w": dnwf, "conv_w": d_cw,
        "loss": loss_p[:, :1]})
    sv_all, sv_sum = _small_allsum(sv, name="small_allsum")
    flat = sv_sum.reshape(-1)
    loss = flat[SV_OFF["loss"]]
    dmod_all = sv_all.reshape(N_DEV, SV_ROWS * 128)[:, :6 * D]
    g_w_ada = _ada_bwd(c_all, lax.dynamic_slice_in_dim(dmod_all, me * wloc, wloc, axis=1), name="ada_bwd")

    g_conv_w = lax.dynamic_slice_in_dim(_sv_get(flat, "conv_w", 4 * XBC).reshape(4, XBC),
                                        me * (XBC // N_DEV), XBC // N_DEV, axis=1)
    small = [("b_ada", b_ada, m_b_ada, v_b_ada), ("norm_mix_w", norm_mix_w, m_norm_mix_w, v_norm_mix_w),
             ("conv_b", conv_b, m_conv_b, v_conv_b), ("dt_bias", dt_bias, m_dt_bias, v_dt_bias),
             ("a_log", a_log, m_a_log, v_a_log), ("d_skip", d_skip, m_d_skip, v_d_skip),
             ("ssd_norm_w", ssd_norm_w, m_ssd_norm_w, v_ssd_norm_w),
             ("pool_scale", pool_scale, m_pool_scale, v_pool_scale),
             ("norm_mlp_w", norm_mlp_w, m_norm_mlp_w, v_norm_mlp_w),
             ("norm_final_w", norm_final_w[None], m_norm_final_w[None], v_norm_final_w[None]),
             ("conv_w", conv_w[0], m_conv_w[0], v_conv_w[0])]
    small_out = _adamw_small(sv_sum.reshape(1, SV_ROWS * 128), g_conv_w, small, name="adamw_small")
    small_out["norm_final_w"] = tuple(a[0] for a in small_out["norm_final_w"])
    small_out["conv_w"] = tuple(a[None] for a in small_out["conv_w"])

    def gpart(n, rows_):
        return gsum_mix[MIX_OFF[n]:MIX_OFF[n] + rows_]

    def lin(a):
        return a[0].T.reshape(IN_ROWS * 8, 128)

    g_lin = lax.dynamic_slice_in_dim(gsum_in, in_shift, IN_ROWS, axis=0).reshape(IN_ROWS * 8, 128)
    dlt, mn, vn = _adamw(lin(w_in), g_lin, lin(m_w_in), lin(v_w_in), name="adamw_w_in", tr=IN_ROWS * 2)
    big_in = tuple(a.reshape(IN_ROWS, D).T[None] for a in (g_lin, dlt, mn, vn))

    big = {
        "w_ada": (w_ada, m_w_ada, v_w_ada, g_w_ada, (D, wloc)),
        "w_branch_ssd": (w_branch_ssd, m_w_branch_ssd, v_w_branch_ssd, gpart("bssd", 256), (256, D)),
        "pool_w": (pool_w, m_pool_w, v_pool_w, gpart("pool", 32).reshape(128, PGW), (128, PGW)),
        "w_branch_pool": (w_branch_pool, m_w_branch_pool, v_w_branch_pool, gpart("bpool", 128), (128, D)),
        "w_out": (w_out, m_w_out, v_w_out, gpart("out", 128), (128, D)),
        "w_up": (w_up, m_w_up, v_w_up, gsum_mlp[:512].T, (D, 512)),
        "w_down": (w_down, m_w_down, v_w_down, gsum_mlp[512:], (512, D)),
    }
    big_out = {}
    for n, (w, mm_, vv, g, shp2) in big.items():
        dlt, mn, vn = _adamw(w.reshape(shp2), g, mm_.reshape(shp2), vv.reshape(shp2), name="adamw_" + n)
        big_out[n] = (g.reshape(w.shape), dlt.reshape(w.shape), mn.reshape(w.shape), vn.reshape(w.shape))

    order = ["w_ada", "b_ada", "norm_mix_w", "w_in", "conv_w", "conv_b", "dt_bias", "a_log", "d_skip",
             "ssd_norm_w", "w_branch_ssd", "pool_w", "pool_scale", "w_branch_pool", "w_out", "norm_mlp_w",
             "w_up", "w_down", "norm_final_w"]
    big_out["w_in"] = big_in
    res = {**small_out, **big_out}
    outs = [loss, grad_x.reshape(x.shape)]
    for k in range(4):
        outs += [res[n][k] for n in order]
    return tuple(outs)
```

```python
import functools

import numpy as np
import jax
import jax.numpy as jnp
from jax import lax
from jax.experimental import pallas as pl
from jax.experimental.pallas import tpu as pltpu

F32 = jnp.float32
BF16 = jnp.bfloat16
SLAB_DT = jnp.bfloat16
_MXU_DTYPE = jnp.bfloat16

N_DEV = 8
D = 1024
DI = 2048
NH = 32
HP = 64
NG = 4
NS = 128
Q = 128
XBC = DI + 2 * NG * NS
DFF = 4096
N_IN = 8224
EPS = 1e-5
POOL_W = 1024
PGW = 256

C_XBC, C_POOL, C_Z, C_GATE, C_DT = 0, 3072, 4096, 6144, 8192
DT_PAD = 256
NPROJ = C_DT + DT_PAD

IN_ROWS = N_IN // N_DEV
IN_ROWS_P = 1040
CONV_ROWS = 16
REST_PARTS = (("bssd", 256), ("pool", 32), ("bpool", 128), ("out", 128), ("up", 512), ("down", 512))
REST_OFF = {}
_o = 0
for _n, _r in REST_PARTS:
    REST_OFF[_n] = _o
    _o += _r
REST_ROWS = _o
MIX_PARTS = (("bssd", 256), ("pool", 32), ("bpool", 128), ("out", 128))
MIX_OFF = {}
_o = 0
for _n, _r in MIX_PARTS:
    MIX_OFF[_n] = _o
    _o += _r
MIX_ROWS = _o

SV_PARTS = (("b_ada", 6144), ("norm_mix_w", 1024), ("conv_b", 3072), ("dt_bias", 128), ("a_log", 128),
            ("d_skip", 128), ("ssd_norm_w", 2048), ("pool_scale", 1024), ("norm_mlp_w", 1024),
            ("norm_final_w", 1024), ("conv_w", 4 * XBC), ("loss", 128))
SV_OFF = {}
_o = 0
for _n, _r in SV_PARTS:
    SV_OFF[_n] = _o
    _o += _r
SV_ROWS = 224
assert _o <= SV_ROWS * 128

ADAM_LR, ADAM_B1, ADAM_B2, ADAM_EPS, ADAM_WD, ADAM_STEP = 0.001, 0.9, 0.999, 1e-08, 0.01, 10

VMEM_BIG = 56 * 1024 * 1024
NEG = -1e30

NN = ((1,), (0,))
NT = ((1,), (1,))
TN = ((0,), (0,))


def _dot(a, b, dims=NN):
    return lax.dot_general(a.astype(_MXU_DTYPE), b.astype(_MXU_DTYPE), (dims, ((), ())),
                           preferred_element_type=F32)


def _dot_hi(a, b, dims=NN):
    return lax.dot_general(a.astype(F32), b.astype(F32), (dims, ((), ())),
                           precision=lax.Precision.HIGHEST, preferred_element_type=F32)


def _pick(n, cands):
    for c in cands:
        if n % c == 0:
            return c
    return n


def _sigmoid(x):
    return 1.0 / (1.0 + jnp.exp(-x))


def _silu(x):
    return x * _sigmoid(x)


def _dsilu(x):
    s = _sigmoid(x)
    return s * (1.0 + x * (1.0 - s))


def _softplus(x):
    return jnp.maximum(x, 0.0) + jnp.log(1.0 + jnp.exp(-jnp.abs(x)))


def _params(sem, vmem=None):
    return pltpu.CompilerParams(dimension_semantics=sem, vmem_limit_bytes=vmem)


def _row_step():
    return pl.program_id(0)


def _mm(a, b, mode, *, name, outs, tm, tn, tk, extras=(), epilogue=None, aliases=None, prologue=None):
    if mode == "tn":
        K, M = a.shape
        N = b.shape[1]
        a_spec = pl.BlockSpec((tk, tm), lambda i, j, k: (k, i))
        b_spec = pl.BlockSpec((tk, tn), lambda i, j, k: (k, j))
        dims = TN
    else:
        M = a.shape[0]
        K = b.shape[0] if mode == "nn" else b.shape[1]
        if prologue is None:
            assert a.shape[1] == K
            a_spec = pl.BlockSpec((tm, tk), lambda i, j, k: (i, k))
        else:
            assert tk == K
            a_spec = pl.BlockSpec((tm, a.shape[1]), lambda i, j, k: (i, 0))
        if mode == "nn":
            N = b.shape[1]
            b_spec = pl.BlockSpec((tk, tn), lambda i, j, k: (k, j))
            dims = NN
        else:
            N = b.shape[0]
            b_spec = pl.BlockSpec((tn, tk), lambda i, j, k: (j, k))
            dims = NT
    assert M % tm == 0 and N % tn == 0 and K % tk == 0, (name, M, N, K, tm, tn, tk)
    nk = K // tk
    ne, no = len(extras), len(outs)
    if epilogue is None:
        def epilogue(acc, ex, out_refs):
            out_refs[0][...] = acc.astype(out_refs[0].dtype)

    def body(a_ref, b_ref, *rest):
        ex, out_refs = rest[:ne], rest[ne:ne + no]
        lhs = a_ref[...] if prologue is None else prologue(a_ref, ex, out_refs, pl.program_id(1))
        p = _dot(lhs, b_ref[...], dims)
        if nk == 1:
            epilogue(p, ex, out_refs)
        else:
            acc = rest[-1]
            k = pl.program_id(2)

            @pl.when(k == 0)
            def _():
                acc[...] = p

            @pl.when(jnp.logical_and(k > 0, k < nk - 1))
            def _():
                acc[...] += p

            @pl.when(k == nk - 1)
            def _():
                epilogue(acc[...] + p, ex, out_refs)

    out_specs, out_shape = [], []
    for o in outs:
        if isinstance(o, tuple):
            shape, dt, bs, im = o
            out_specs.append(pl.BlockSpec(bs, im))
            out_shape.append(jax.ShapeDtypeStruct(shape, dt))
        else:
            out_specs.append(pl.BlockSpec((tm, tn), lambda i, j, k: (i, j)))
            out_shape.append(jax.ShapeDtypeStruct((M, N), o))
    in_specs = [a_spec, b_spec]
    for _, bs, im in extras:
        in_specs.append(pl.BlockSpec(memory_space=pl.ANY) if bs is None else pl.BlockSpec(bs, im))
    res = pl.pallas_call(
        body, name=name,
        grid=(M // tm, N // tn, nk),
        in_specs=in_specs, out_specs=out_specs, out_shape=out_shape,
        scratch_shapes=[pltpu.VMEM((tm, tn), F32)] if nk > 1 else [],
        input_output_aliases={2 + e: o for e, o in (aliases or {}).items()},
        compiler_params=_params(("arbitrary", "arbitrary", "arbitrary"), VMEM_BIG),
    )(a, b, *[e[0] for e in extras])
    return res if no > 1 else res[0]


def _rows(tm, w=D, col=0):
    return (tm, w), lambda i, j, k, c=col: (i, c)


def _vecs(w=D, col=0):
    return (1, w), lambda i, j, k, c=col: (0, c)


def _sum_out(w=D):
    return ((1, w), F32, (1, w), lambda i, j, k: (0, 0))


def _mm_pool_tn(a, b, *, name, tk):
    L = a.shape[0]

    def body(a_ref, b_ref, o_ref):
        p = _dot(a_ref[...], b_ref[...], TN)

        @pl.when(pl.program_id(1) == 0)
        def _():
            o_ref[...] = p

        @pl.when(pl.program_id(1) > 0)
        def _():
            o_ref[...] += p

    blk = pl.BlockSpec((tk, PGW), lambda g, k: (k, g))
    return pl.pallas_call(body, name=name, grid=(4, L // tk), in_specs=[blk, blk],
                          out_specs=pl.BlockSpec((PGW, PGW), lambda g, k: (g, 0)),
                          out_shape=jax.ShapeDtypeStruct((POOL_W, PGW), F32),
                          compiler_params=_params(("parallel", "arbitrary")))(a, b)


def _acc_out(ref, val, i):
    @pl.when(i == 0)
    def _():
        ref[...] = val

    @pl.when(i > 0)
    def _():
        ref[...] += val


def _colsum(v):
    return jnp.sum(v, axis=0, keepdims=True)


def _ep_resid_norm(acc, ex, outs):
    x_ref, g_ref, nw_ref, sc_ref, sh_ref = ex
    mix_ref, x1_ref, h_ref = outs
    mix_ref[...] = acc.astype(mix_ref.dtype)
    xv = x_ref[...] + g_ref[...] * acc
    x1_ref[...] = xv
    r = lax.rsqrt(jnp.mean(xv * xv, axis=-1, keepdims=True) + EPS)
    h_ref[...] = (xv * r * nw_ref[...] * (1.0 + sc_ref[...]) + sh_ref[...]).astype(h_ref.dtype)


def _ep_final(acc, ex, outs):
    x1_ref, t_ref, g_ref, nw_ref = ex
    dx2_ref, dd_ref, loss_ref, dnw_ref, dg_ref = outs
    i = _row_step()
    x2 = x1_ref[...] + g_ref[...] * acc
    r = lax.rsqrt(jnp.mean(x2 * x2, axis=-1, keepdims=True) + EPS)
    xh = x2 * r
    e = xh * nw_ref[...] - t_ref[...]
    part = 0.5 * jnp.sum(jnp.mean(e * e, axis=-1, keepdims=True), axis=0, keepdims=True)
    dy = e * (1.0 / D)
    g = dy * nw_ref[...]
    dx2 = r * (g - xh * jnp.mean(g * xh, axis=-1, keepdims=True))
    dx2_ref[...] = dx2
    dd_ref[...] = (dx2 * g_ref[...]).astype(dd_ref.dtype)
    _acc_out(loss_ref, jnp.broadcast_to(part, (1, 128)), i)
    _acc_out(dnw_ref, _colsum(dy * xh), i)
    _acc_out(dg_ref, _colsum(dx2 * acc), i)


def _ep_norm_bwd(acc, ex, outs):
    x_ref, dr_ref, nw_ref, sc_ref = ex[:4]
    dx_ref, p_ref, q_ref = outs[:3]
    i = _row_step()
    xv = x_ref[...]
    r = lax.rsqrt(jnp.mean(xv * xv, axis=-1, keepdims=True) + EPS)
    xh = xv * r
    g = acc * (nw_ref[...] * (1.0 + sc_ref[...]))
    dx = dr_ref[...] + r * (g - xh * jnp.mean(g * xh, axis=-1, keepdims=True))
    dx_ref[...] = dx
    _acc_out(p_ref, _colsum(acc * xh), i)
    _acc_out(q_ref, _colsum(acc), i)
    if len(ex) > 4:
        m_ref, g_ref = ex[4:]
        dm_ref, dg_ref = outs[3:]
        dm_ref[...] = (dx * g_ref[...]).astype(dm_ref.dtype)
        _acc_out(dg_ref, _colsum(dx * m_ref[...].astype(F32)), i)


def _ep_merge_bwd(acc, ex, outs):
    a_ref, b_ref, gl_ref = ex
    da_ref, db_ref, dgl_ref = outs
    s = _sigmoid(gl_ref[...].astype(F32))
    s1, s2 = s[:, :D], s[:, D:]
    da_ref[...] = (acc * s1).astype(da_ref.dtype)
    db_ref[...] = (acc * s2).astype(db_ref.dtype)
    dgl_ref[:, :D] = (acc * a_ref[...].astype(F32) * s1 * (1.0 - s1)).astype(dgl_ref.dtype)
    dgl_ref[:, D:] = (acc * b_ref[...].astype(F32) * s2 * (1.0 - s2)).astype(dgl_ref.dtype)


GW = DI // NG


def _ep_gated_norm_bwd(acc, ex, outs):
    y_ref, z_ref, w_ref, _ = ex
    dy_ref, dz_ref, dw_ref = outs
    zv = z_ref[...].astype(F32)
    yv = y_ref[...].astype(F32)
    sg = _sigmoid(zv)
    sz = zv * sg
    yg = yv * sz
    dsz = sg * (1.0 + zv * (1.0 - sg))
    dws = []
    for k in range(NG):
        sl = slice(k * GW, (k + 1) * GW)
        seg = yg[:, sl]
        r = lax.rsqrt(jnp.mean(seg * seg, axis=-1, keepdims=True) + EPS)
        sh = seg * r
        dn = acc[:, sl]
        g = dn * w_ref[:, sl]
        dyg = r * (g - sh * jnp.mean(g * sh, axis=-1, keepdims=True))
        dy_ref[:, sl] = dyg * sz[:, sl]
        dz_ref[:, sl] = (dyg * yv[:, sl] * dsz[:, sl]).astype(dz_ref.dtype)
        dws.append(_colsum(dn * sh))
    _acc_out(dw_ref, jnp.concatenate(dws, axis=1), _row_step())


CONV_CB = 128
HALO = 16


def _time_chunk(L):
    return _pick(L, (256, 128))


def _with_halo(x_ref, i, r0, rc):
    p0 = pl.multiple_of(jnp.maximum(r0 - HALO, 0), HALO)
    prev = jnp.where(i > 0, x_ref[pl.ds(p0, HALO), :].astype(F32), 0.0)
    return jnp.concatenate([prev, x_ref[pl.ds(r0, rc), :].astype(F32)], axis=0)


def _conv_bwd(proj, dy, w, b, dproj, *, name):
    L = proj.shape[0]
    rc = _time_chunk(L)
    n = L // rc

    def body(x_ref, dy_ref, w_ref, b_ref, dp_in, dx_ref, dw_ref, db_ref, xpad, dpad):
        del dp_in
        wv = w_ref[...]
        bv = b_ref[...]
        dpad[rc:rc + HALO, :] = jnp.zeros((HALO, CONV_CB), F32)

        def step(k, carry):
            db, d0, d1, d2, d3 = carry
            i = n - 1 - k
            r0 = pl.multiple_of(i * rc, rc)
            p0 = pl.multiple_of(jnp.maximum(r0 - HALO, 0), HALO)
            xpad[0:HALO, :] = jnp.where(i > 0, x_ref[pl.ds(p0, HALO), :].astype(F32), 0.0)
            xpad[HALO:HALO + rc, :] = x_ref[pl.ds(r0, rc), :].astype(F32)
            xk = [xpad[HALO - j:HALO - j + rc, :] for j in range(4)]
            pre = bv
            for j in range(4):
                pre = pre + xk[j] * wv[3 - j:4 - j]
            dpre = dy_ref[pl.ds(r0, rc), :] * _dsilu(pre)
            dpad[0:rc, :] = dpre
            acc = dpre * wv[3:4]
            for j in (1, 2, 3):
                acc = acc + dpad[j:j + rc, :] * wv[3 - j:4 - j]
            dx_ref[pl.ds(r0, rc), :] = acc.astype(dx_ref.dtype)
            dpad[rc:rc + HALO, :] = dpre[:HALO]
            return (db + _colsum(dpre), d0 + _colsum(dpre * xk[3]), d1 + _colsum(dpre * xk[2]),
                    d2 + _colsum(dpre * xk[1]), d3 + _colsum(dpre * xk[0]))

        z = jnp.zeros((1, CONV_CB), F32)
        db, d0, d1, d2, d3 = lax.fori_loop(0, n, step, (z, z, z, z, z))
        db_ref[...] = db
        dw_ref[...] = jnp.concatenate([d0, d1, d2, d3], axis=0)

    nb = XBC // CONV_CB
    return pl.pallas_call(
        body, name=name, grid=(nb,),
        in_specs=[pl.BlockSpec((L, CONV_CB), lambda j: (0, j + C_XBC // CONV_CB)),
                  pl.BlockSpec((L, CONV_CB), lambda j: (0, j)),
                  pl.BlockSpec((4, CONV_CB), lambda j: (0, j)), pl.BlockSpec((1, CONV_CB), lambda j: (0, j)),
                  pl.BlockSpec(memory_space=pl.ANY)],
        out_specs=[pl.BlockSpec((L, CONV_CB), lambda j: (0, j + C_XBC // CONV_CB)),
                   pl.BlockSpec((4, CONV_CB), lambda j: (0, j)), pl.BlockSpec((1, CONV_CB), lambda j: (0, j))],
        out_shape=[jax.ShapeDtypeStruct((L, NPROJ), BF16), jax.ShapeDtypeStruct((4, XBC), F32),
                   jax.ShapeDtypeStruct((1, XBC), F32)],
        scratch_shapes=[pltpu.VMEM((rc + HALO, CONV_CB), F32), pltpu.VMEM((rc + HALO, CONV_CB), F32)],
        input_output_aliases={4: 0},
        compiler_params=_params(("parallel",), VMEM_BIG))(proj, dy, w, b, dproj)


def _pool_fwd(proj, *, name):
    L = proj.shape[0]
    rc = _time_chunk(L)
    n = L // rc

    def body(x_ref, o_ref, pad):
        g = pl.program_id(0)
        pad[0:HALO, :] = jnp.zeros((HALO, PGW), F32)

        def fill(i, c):
            r0 = pl.multiple_of(i * rc, rc)
            pad[pl.ds(r0 + HALO, rc), :] = x_ref[pl.ds(r0, rc), :].astype(F32)
            return c

        lax.fori_loop(0, n, fill, 0)
        rows = lax.broadcasted_iota(jnp.int32, (rc, PGW), 0)

        for gi in range(4):
            win = 2 << gi

            @pl.when(g == gi)
            def _(gi=gi, win=win):
                def step(i, c):
                    r0 = pl.multiple_of(i * rc, rc)
                    ext = pad[pl.ds(r0, rc + HALO), :]
                    s = ext
                    sh = 1
                    while sh < win:
                        s = s + pltpu.roll(s, sh, 0)
                        sh *= 2
                    cnt = jnp.minimum(rows + (r0 + 1), win).astype(F32)
                    o_ref[pl.ds(r0, rc), :] = (s[HALO:] / cnt - ext[HALO:]).astype(o_ref.dtype)
                    return c

                lax.fori_loop(0, n, step, 0)

    return pl.pallas_call(
        body, name=name, grid=(4,),
        in_specs=[pl.BlockSpec((L, PGW), lambda j: (0, j + C_POOL // PGW))],
        out_specs=pl.BlockSpec((L, PGW), lambda j: (0, j)),
        out_shape=jax.ShapeDtypeStruct((L, POOL_W), BF16),
        scratch_shapes=[pltpu.VMEM((L + HALO, PGW), F32)],
        compiler_params=_params(("parallel",), VMEM_BIG))(proj)


def _pool_bwd(dpooled, dproj, *, name):
    L = dpooled.shape[0]
    rc = _time_chunk(L)
    n = L // rc

    def body(d_ref, dp_in, o_ref, pad):
        del dp_in
        g = pl.program_id(0)
        pad[L:L + HALO, :] = jnp.zeros((HALO, PGW), F32)
        rows = lax.broadcasted_iota(jnp.int32, (rc, PGW), 0)

        for gi in range(4):
            win = 2 << gi

            @pl.when(g == gi)
            def _(gi=gi, win=win):
                def fill(i, c):
                    r0 = pl.multiple_of(i * rc, rc)
                    cnt = jnp.minimum(rows + (r0 + 1), win).astype(F32)
                    pad[pl.ds(r0, rc), :] = d_ref[pl.ds(r0, rc), :] / cnt
                    return c

                lax.fori_loop(0, n, fill, 0)

                def step(i, c):
                    r0 = pl.multiple_of(i * rc, rc)
                    s = pad[pl.ds(r0, rc + HALO), :]
                    sh = 1
                    while sh < win:
                        s = s + pltpu.roll(s, rc + HALO - sh, 0)
                        sh *= 2
                    o_ref[pl.ds(r0, rc), :] = (s[:rc] - d_ref[pl.ds(r0, rc), :]).astype(o_ref.dtype)
                    return c

                lax.fori_loop(0, n, step, 0)

    return pl.pallas_call(
        body, name=name, grid=(4,),
        in_specs=[pl.BlockSpec((L, PGW), lambda j: (0, j)), pl.BlockSpec(memory_space=pl.ANY)],
        out_specs=pl.BlockSpec((L, PGW), lambda j: (0, j + C_POOL // PGW)),
        out_shape=jax.ShapeDtypeStruct((L, NPROJ), BF16),
        scratch_shapes=[pltpu.VMEM((L + HALO, PGW), F32)],
        input_output_aliases={1: 0},
        compiler_params=_params(("parallel",), VMEM_BIG))(dpooled, dproj)


_SPLIT_DT = jnp.bfloat16


def _ssd_consts():
    tri = np.tril(np.ones((Q, Q), np.float32))
    exp = np.zeros((128, DI), np.float32)
    for h in range(NH):
        exp[h, h * HP:(h + 1) * HP] = 1.0
    exp2 = np.concatenate([exp, exp], axis=0)
    return (jnp.asarray(tri, dtype=_SPLIT_DT), jnp.asarray(tri.T.copy(), dtype=_SPLIT_DT),
            jnp.asarray(exp2, dtype=_SPLIT_DT))


def _split(v, n):
    parts, r = [], v
    for _ in range(n):
        p = r.astype(_SPLIT_DT)
        parts.append(p)
        r = r - p.astype(F32)
    return parts


def _bdot(a, b, dims):
    return lax.dot_general(a, b, (dims, ((), ())), preferred_element_type=F32)


def _tri_sum(t_ref, v):
    r = _bdot(t_ref[...], jnp.concatenate(_split(v, 3), axis=1), NN)
    return r[:, :128] + r[:, 128:256] + r[:, 256:]


def _expand(v, e2_ref):
    return _bdot(jnp.concatenate(_split(v, 2), axis=1), e2_ref[...], NN)


def _reduce_heads(vals, eg):
    parts = []
    for v in vals:
        parts += _split(v, 2)
    r = _bdot(jnp.concatenate(parts, axis=0), eg, NT)
    return [r[2 * i * Q:(2 * i + 1) * Q] + r[(2 * i + 1) * Q:(2 * i + 2) * Q] for i in range(len(vals))]


def _ssd_common(xbc_ref, dtw_ref, arow_ref, t_ref, e_ref):
    dt = dtw_ref[:, :128]
    sig = dtw_ref[:, 128:]
    acs = _tri_sum(t_ref, dt * arow_ref[...])
    acs_x = _expand(acs, e_ref)
    dt_x = _expand(dt, e_ref)
    xs = xbc_ref[:, 0:DI]
    return sig, dt, acs, acs.T, acs_x, dt_x, xs


CONV_SLAB = 512


def _ssd_fwd(raw, dtp, cw, cb, arow, dsk_x, *, name):
    L = raw.shape[0]
    nc = L // Q
    tri, _, expand = _ssd_consts()

    def body(raw_ref, halo_ref, cw_ref, cb_ref, dtw_ref, arow_ref, dsk_ref, t_ref, e_ref,
             y_ref, hs_ref, xbc_ref, h_scr, cpad):
        c = pl.program_id(0)

        @pl.when(c == 0)
        def _():
            h_scr[...] = jnp.zeros_like(h_scr)

        cpad[0:8, :] = jnp.where(c > 0, halo_ref[...], 0.0)
        cpad[8:8 + Q, :] = raw_ref[...]
        for lo in range(0, XBC, CONV_SLAB):
            sl = slice(lo, lo + CONV_SLAB)
            acc = cb_ref[:, sl]
            for j in range(4):
                acc = acc + cpad[8 - j:8 - j + Q, sl] * cw_ref[3 - j:4 - j, sl]
            xbc_ref[:, sl] = acc * _sigmoid(acc)

        _, dt, acs, acs_t, acs_x, dt_x, xs = _ssd_common(xbc_ref, dtw_ref, arow_ref, t_ref, e_ref)
        xdt = xs * dt_x
        eacs = jnp.exp(acs_x)
        acs_last = acs_x[Q - 1:Q, :]
        dec = jnp.exp(acs_last - acs_x)
        hs_ref[0] = h_scr[...].astype(hs_ref.dtype)
        causal = lax.broadcasted_iota(jnp.int32, (Q, Q), 0) >= lax.broadcasted_iota(jnp.int32, (Q, Q), 1)
        first = lax.broadcasted_iota(jnp.int32, (Q, 128), 1) < HP
        for g in range(NG):
            bg = xbc_ref[:, DI + g * NS:DI + (g + 1) * NS]
            cg = xbc_ref[:, DI + NG * NS + g * NS:DI + NG * NS + (g + 1) * NS]
            s = _dot(cg, bg, NT)
            sl = slice(g * GW, (g + 1) * GW)
            hg = h_scr[:, sl]
            yoff = _dot(cg, hg, NN) * eacs[:, sl]
            st = _dot(bg, xdt[:, sl] * dec[:, sl], TN)
            h_scr[:, sl] = hg * eacs[Q - 1:Q, sl] + st
            for j in range(4):
                lo = g * GW + j * 128
                xb = xdt[:, lo:lo + 128]
                yp = yoff[:, j * 128:(j + 1) * 128] + dsk_ref[:, lo:lo + 128] * xs[:, lo:lo + 128]
                for e in range(2):
                    h = g * 8 + j * 2 + e
                    lm = jnp.exp(jnp.where(causal, acs[:, h:h + 1] - acs_t[h:h + 1, :], NEG))
                    xm = jnp.where(first if e == 0 else jnp.logical_not(first), xb, 0.0)
                    yp = yp + _dot(s * lm, xm, NN)
                y_ref[:, lo:lo + 128] = yp.astype(y_ref.dtype)

    const = lambda c: (0, 0)
    return pl.pallas_call(
        body, name=name, grid=(nc,),
        in_specs=[pl.BlockSpec((Q, XBC), lambda c: (c, 0)),
                  pl.BlockSpec((8, XBC), lambda c: (jnp.maximum(c * (Q // 8) - 1, 0), 0)),
                  pl.BlockSpec((4, XBC), const), pl.BlockSpec((1, XBC), const),
                  pl.BlockSpec((Q, DT_PAD), lambda c: (c, 0)),
                  pl.BlockSpec((1, 128), const), pl.BlockSpec((1, DI), const),
                  pl.BlockSpec((Q, Q), const), pl.BlockSpec((256, DI), const)],
        out_specs=[pl.BlockSpec((Q, DI), lambda c: (c, 0)), pl.BlockSpec((1, NS, DI), lambda c: (c, 0, 0)),
                   pl.BlockSpec((Q, XBC), lambda c: (c, 0))],
        out_shape=[jax.ShapeDtypeStruct((L, DI), BF16), jax.ShapeDtypeStruct((nc, NS, DI), F32),
                   jax.ShapeDtypeStruct((L, XBC), F32)],
        scratch_shapes=[pltpu.VMEM((NS, DI), F32), pltpu.VMEM((8 + Q, XBC), F32)],
        compiler_params=_params(("arbitrary",), VMEM_BIG))(raw, raw, cw, cb, dtp, arow, dsk_x, tri, expand)


def _ssd_bwd(dy, xbc, dtp, hs, arow, dsk_x, dproj, *, name):
    L = xbc.shape[0]
    nc = L // Q
    tri, triu, expand = _ssd_consts()

    def body(dy_ref, xbc_ref, dtw_ref, hs_ref, arow_ref, dsk_ref, t_ref, u_ref, e_ref, dp_in,
             dxbc_ref, ddtw_ref, da_ref, ddx_ref, ddtb_ref, dh_scr):
        del dp_in
        i = pl.program_id(0)

        @pl.when(i == 0)
        def _():
            dh_scr[...] = jnp.zeros_like(dh_scr)

        sig, dt, acs, acs_t, acs_x, dt_x, xs = _ssd_common(xbc_ref, dtw_ref, arow_ref, t_ref, e_ref)
        dyv = dy_ref[...]
        xdt = xs * dt_x
        eacs = jnp.exp(acs_x)
        acs_last = acs_x[Q - 1:Q, :]
        dec = jnp.exp(acs_last - acs_x)
        gy = dyv * eacs
        causal = lax.broadcasted_iota(jnp.int32, (Q, Q), 0) >= lax.broadcasted_iota(jnp.int32, (Q, Q), 1)
        first = lax.broadcasted_iota(jnp.int32, (Q, 128), 1) < HP
        lane_h = lax.broadcasted_iota(jnp.int32, (Q, 128), 1)
        sub_h = lax.broadcasted_iota(jnp.int32, (128, Q), 0)
        last_row = lax.broadcasted_iota(jnp.int32, (Q, GW), 0) == Q - 1
        dacs = jnp.zeros((Q, 128), F32)
        dacs_t = jnp.zeros((128, Q), F32)
        ddt = jnp.zeros((Q, 128), F32)
        for g in range(NG):
            bg = xbc_ref[:, DI + g * NS:DI + (g + 1) * NS]
            cg = xbc_ref[:, DI + NG * NS + g * NS:DI + NG * NS + (g + 1) * NS]
            s = _dot(cg, bg, NT)
            sl = slice(g * GW, (g + 1) * GW)
            hg = hs_ref[0, :, sl].astype(F32)
            dhn = dh_scr[:, sl]
            eal = eacs[Q - 1:Q, sl]
            gg = gy[:, sl]
            dax = gg * _dot(cg, hg, NN)
            dcg = _dot(gg, hg, NT)
            dh_scr[:, sl] = _dot(cg, gg, TN) + dhn * eal
            dal = eal * _colsum(dhn * hg)
            xdd = xdt[:, sl] * dec[:, sl]
            dbg = _dot(xdd, dhn, NT)
            wv = _dot(bg, dhn, NN)
            dd = wv * xdd
            dax = dax - dd
            dal = dal + _colsum(dd)
            dax = dax + jnp.where(last_row, dal, 0.0)
            dxdt_g = wv * dec[:, sl]
            ds = jnp.zeros((Q, Q), F32)
            dxdt_blocks = []
            for j in range(4):
                lo = g * GW + j * 128
                xb = xdt[:, lo:lo + 128]
                dyb = dyv[:, lo:lo + 128]
                dxb = dxdt_g[:, j * 128:(j + 1) * 128]
                for e in range(2):
                    h = g * 8 + j * 2 + e
                    lm = jnp.exp(jnp.where(causal, acs[:, h:h + 1] - acs_t[h:h + 1, :], NEG))
                    m = s * lm
                    dym = jnp.where(first if e == 0 else jnp.logical_not(first), dyb, 0.0)
                    dm = _dot(dym, xb, NT)
                    r = dm * m
                    dacs = dacs + jnp.where(lane_h == h, jnp.sum(r, axis=1, keepdims=True), 0.0)
                    dacs_t = dacs_t + jnp.where(sub_h == h, _colsum(r), 0.0)
                    ds = ds + dm * lm
                    dxb = dxb + _dot(m, dym, TN)
                dxdt_blocks.append(dxb)
            dxdt = jnp.concatenate(dxdt_blocks, axis=1)
            dcg = dcg + _dot(ds, bg, NN)
            dbg = dbg + _dot(ds, cg, TN)
            dxbc_ref[:, DI + g * NS:DI + (g + 1) * NS] = dbg
            dxbc_ref[:, DI + NG * NS + g * NS:DI + NG * NS + (g + 1) * NS] = dcg
            dxbc_ref[:, sl] = dsk_ref[:, sl] * dyv[:, sl] + dxdt * dt_x[:, sl]
            ddt_g, dacs_g = _reduce_heads([dxdt * xs[:, sl], dax], e_ref[0:128, sl])
            ddt = ddt + ddt_g
            dacs = dacs + dacs_g
        dacs = dacs - dacs_t.T
        ddta = _tri_sum(u_ref, dacs)
        ddt = ddt + ddta * arow_ref[...]
        ddtw = jnp.where(lane_h < NH, ddt * sig, 0.0)
        ddtw_ref[...] = jnp.concatenate([ddtw, jnp.zeros((Q, DT_PAD - 128), F32)], axis=1).astype(ddtw_ref.dtype)
        _acc_out(da_ref, _colsum(ddta * dt), i)
        _acc_out(ddx_ref, _colsum(dyv * xs), i)
        _acc_out(ddtb_ref, _colsum(ddtw), i)

    rev = lambda c: (nc - 1 - c, 0)
    const = lambda c: (0, 0)
    return pl.pallas_call(
        body, name=name, grid=(nc,),
        in_specs=[pl.BlockSpec((Q, DI), rev), pl.BlockSpec((Q, XBC), rev),
                  pl.BlockSpec((Q, DT_PAD), rev),
                  pl.BlockSpec((1, NS, DI), lambda c: (nc - 1 - c, 0, 0)),
                  pl.BlockSpec((1, 128), const), pl.BlockSpec((1, DI), const),
                  pl.BlockSpec((Q, Q), const), pl.BlockSpec((Q, Q), const), pl.BlockSpec((256, DI), const),
                  pl.BlockSpec(memory_space=pl.ANY)],
        out_specs=[pl.BlockSpec((Q, XBC), rev),
                   pl.BlockSpec((Q, DT_PAD), lambda c: (nc - 1 - c, C_DT // DT_PAD)),
                   pl.BlockSpec((1, 128), const), pl.BlockSpec((1, DI), const), pl.BlockSpec((1, 128), const)],
        out_shape=[jax.ShapeDtypeStruct((L, XBC), F32), jax.ShapeDtypeStruct((L, NPROJ), BF16),
                   jax.ShapeDtypeStruct((1, 128), F32), jax.ShapeDtypeStruct((1, DI), F32),
                   jax.ShapeDtypeStruct((1, 128), F32)],
        scratch_shapes=[pltpu.VMEM((NS, DI), F32)],
        input_output_aliases={9: 1},
        compiler_params=_params(("arbitrary",), VMEM_BIG))(dy, xbc, dtp, hs, arow, dsk_x, tri, triu,
                                                          expand, dproj)


def _adam_update(wv, gv, mv, vv):
    c1 = 1.0 - ADAM_B1 ** ADAM_STEP
    c2 = 1.0 - ADAM_B2 ** ADAM_STEP
    mn = ADAM_B1 * mv + (1.0 - ADAM_B1) * gv
    vn = ADAM_B2 * vv + (1.0 - ADAM_B2) * (gv * gv)
    return -ADAM_LR * ((mn / c1) / (jnp.sqrt(vn / c2) + ADAM_EPS) + ADAM_WD * wv), mn, vn


def _adamw(w, g, m, v, *, name, tr=None):
    R = w.shape[0]
    rest = tuple(w.shape[1:])
    if tr is None:
        tr = _pick(R, (256, 128, 64, 32, 16, 8))
    assert R % tr == 0

    def body(w_ref, g_ref, m_ref, v_ref, d_ref, mo_ref, vo_ref):
        d_ref[...], mo_ref[...], vo_ref[...] = _adam_update(w_ref[...], g_ref[...], m_ref[...], v_ref[...])

    zeros = (0,) * len(rest)
    spec = pl.BlockSpec((tr,) + rest, lambda i: (i,) + zeros)
    return pl.pallas_call(body, name=name, grid=(R // tr,), in_specs=[spec] * 4, out_specs=[spec] * 3,
                          out_shape=[jax.ShapeDtypeStruct(w.shape, F32)] * 3,
                          compiler_params=_params(("parallel",)))(w, g, m, v)


def _adamw_small(svrow, g_conv, params, *, name):
    n = len(params)

    def body(*refs):
        sv_ref, gc_ref = refs[0], refs[1]
        ins, outs = refs[2:2 + 3 * n], refs[2 + 3 * n:]
        for p, (key, w, _, _) in enumerate(params):
            w_ref, m_ref, v_ref = ins[3 * p:3 * p + 3]
            g_ref, d_ref, mo_ref, vo_ref = outs[4 * p:4 * p + 4]
            gv = gc_ref[...] if key == "conv_w" else sv_ref[:, SV_OFF[key]:SV_OFF[key] + w.shape[1]]
            g_ref[...] = gv
            d_ref[...], mo_ref[...], vo_ref[...] = _adam_update(w_ref[...], gv, m_ref[...], v_ref[...])

    vm = pl.BlockSpec(memory_space=pltpu.VMEM)
    args = [svrow, g_conv]
    shapes = []
    for _, w, m, v in params:
        args += [w, m, v]
        shapes += [jax.ShapeDtypeStruct(w.shape, F32)] * 4
    res = pl.pallas_call(body, name=name, in_specs=[vm] * len(args), out_specs=[vm] * len(shapes),
                         out_shape=shapes)(*args)
    return {key: tuple(res[4 * p:4 * p + 4]) for p, (key, _, _, _) in enumerate(params)}


def _slab_sum(recv, *, tile, name):
    rows = recv.shape[1]
    assert rows % tile == 0 and tile % 16 == 0

    def body(r_ref, o_ref):
        acc = r_ref[0].astype(F32)
        for j in range(1, N_DEV):
            acc = acc + r_ref[j].astype(F32)
        o_ref[...] = acc

    return pl.pallas_call(body, name=name, grid=(rows // tile,),
                          in_specs=[pl.BlockSpec((N_DEV, tile, D), lambda i: (0, i, 0))],
                          out_specs=pl.BlockSpec((tile, D), lambda i: (i, 0)),
                          out_shape=jax.ShapeDtypeStruct((rows, D), F32),
                          compiler_params=_params(("parallel",)))(recv)


MESH = pl.DeviceIdType.MESH


def _coords():
    return lax.axis_index("x"), lax.axis_index("y"), lax.axis_index("c")


def _peer(k):
    x, y, c = _coords()
    px = 1 - x if k & 4 else x
    py = 1 - y if k & 2 else y
    pc = 1 - c if k & 1 else c
    return (px, py, pc), 4 * px + 2 * py + pc


def _rcopy(src, dst, ssem, rsem, dev):
    return pltpu.make_async_remote_copy(src_ref=src, dst_ref=dst, send_sem=ssem, recv_sem=rsem,
                                        device_id=dev, device_id_type=MESH)


def _exchange_all(src_of, dst_slot, send_sems, recv_sems):
    x, y, c = _coords()
    me = 4 * x + 2 * y + c
    sent = []
    for k in range(1, N_DEV):
        dev, pidx = _peer(k)
        cp = _rcopy(src_of(pidx), dst_slot(me), send_sems.at[k - 1], recv_sems.at[k - 1], dev)
        cp.start()
        sent.append(cp)
    for k in range(1, N_DEV):
        dev, pidx = _peer(k)
        _rcopy(src_of(pidx), dst_slot(pidx), send_sems.at[k - 1], recv_sems.at[k - 1], dev).wait_recv()
    for cp in sent:
        cp.wait_send()


def _rows_of_slots(buf, nslots):
    rows = lax.broadcasted_iota(jnp.int32, (8, buf.shape[-1]), 0)
    out = jnp.zeros((8, buf.shape[-1]), F32)
    for j in range(nslots):
        out = out + jnp.where(rows == j, buf[j], 0.0)
    return out


def _exchange_start(src_of, dst_slot, send_sems, recv_sems):
    x, y, c = _coords()
    me = 4 * x + 2 * y + c
    sent = []
    for k in range(1, N_DEV):
        dev, pidx = _peer(k)
        cp = _rcopy(src_of(pidx), dst_slot(me), send_sems.at[k - 1], recv_sems.at[k - 1], dev)
        cp.start()
        sent.append(cp)
    return sent


def _exchange_finish(sent, src_of, dst_slot, send_sems, recv_sems):
    for k in range(1, N_DEV):
        dev, pidx = _peer(k)
        _rcopy(src_of(pidx), dst_slot(pidx), send_sems.at[k - 1], recv_sems.at[k - 1], dev).wait_recv()
    for cp in sent:
        cp.wait_send()


def _ada_gather(c, w_ada, b_r, slab, *, name):
    wloc = w_ada.shape[1]

    def body(c_ref, w_ref, b_ref, x_ref, mod_ref, call_ref, out_ref,
             csrc, cbuf, psrc, pbuf, s1, r1, s2, r2, send_sems, recv_sems, local_sem):
        x, y, cc = _coords()
        me_i = 4 * x + 2 * y + cc
        me, sibling = (x, y, cc), (x, y, 1 - cc)
        chips = [(1 - x, y), (x, 1 - y), (1 - x, 1 - y)]

        def slot(px, py, pc):
            return out_ref.at[4 * px + 2 * py + pc]

        def copy(k, block, to, src=None):
            return _rcopy(slot(*block) if src is None else src, slot(*block), send_sems.at[k], recv_sems.at[k], to)

        csrc[...] = jnp.broadcast_to(c_ref[...], (8, D))
        cbuf[me_i] = csrc[...]
        c_of, c_slot = (lambda p: csrc), (lambda s: cbuf.at[s])
        sent1 = _exchange_start(c_of, c_slot, s1, r1)

        mine = pltpu.make_async_copy(x_ref, slot(*me), local_sem)
        mine.start()
        first = [copy(0, me, sibling, src=x_ref)]
        first += [copy(1 + j, me, (*chip, cc), src=x_ref) for j, chip in enumerate(chips)]
        for cp in first:
            cp.start()

        _exchange_finish(sent1, c_of, c_slot, s1, r1)
        call = _rows_of_slots(cbuf, N_DEV)
        call_ref[...] = call
        prod = _dot_hi(_silu(call), w_ref[...])
        for b in range(N_DEV):
            psrc[b] = jnp.broadcast_to(prod[b:b + 1, :], (8, wloc))
        pbuf[me_i] = psrc[me_i]
        p_of, p_slot = (lambda p: psrc.at[p]), (lambda s: pbuf.at[s])
        sent2 = _exchange_start(p_of, p_slot, s2, r2)

        passed = [copy(4 + j, (*chip, cc), sibling) for j, chip in enumerate(chips)]
        for j, chip in enumerate(chips):
            copy(1 + j, (*chip, cc), me).wait_recv()
            passed[j].start()
        copy(0, sibling, me).wait_recv()
        for j, chip in enumerate(chips):
            copy(4 + j, (*chip, 1 - cc), me).wait_recv()

        _exchange_finish(sent2, p_of, p_slot, s2, r2)
        mod_ref[...] = _rows_of_slots(pbuf, N_DEV) + b_ref[...]
        for cp in first + passed:
            cp.wait_send()
        mine.wait()

    vm = pl.BlockSpec(memory_space=pltpu.VMEM)
    anyspec = pl.BlockSpec(memory_space=pl.ANY)
    return pl.pallas_call(
        body, name=name, in_specs=[vm, vm, vm, anyspec], out_specs=[vm, vm, anyspec],
        out_shape=[jax.ShapeDtypeStruct((N_DEV, wloc), F32), jax.ShapeDtypeStruct((N_DEV, D), F32),
                   jax.ShapeDtypeStruct((N_DEV,) + slab.shape, slab.dtype)],
        scratch_shapes=[pltpu.VMEM((8, D), F32), pltpu.VMEM((N_DEV, 8, D), F32),
                        pltpu.VMEM((N_DEV, 8, wloc), F32), pltpu.VMEM((N_DEV, 8, wloc), F32),
                        pltpu.SemaphoreType.DMA((N_DEV - 1,)), pltpu.SemaphoreType.DMA((N_DEV - 1,)),
                        pltpu.SemaphoreType.DMA((N_DEV - 1,)), pltpu.SemaphoreType.DMA((N_DEV - 1,)),
                        pltpu.SemaphoreType.DMA((7,)), pltpu.SemaphoreType.DMA((7,)), pltpu.SemaphoreType.DMA],
        compiler_params=pltpu.CompilerParams(vmem_limit_bytes=VMEM_BIG))(c, w_ada, b_r, slab)


_HBM =pl.BlockSpec(memory_space=pltpu.HBM)
_SEM = pl.BlockSpec(memory_space=pltpu.SEMAPHORE)
_EFFECT = pltpu.SideEffectType.DATAFLOW_SIDE_EFFECTING


def _xchg_src(src_ref, pidx, per_peer):
    return src_ref.at[pidx] if per_peer else src_ref


def _xchg_start(src, *, per_peer, name):
    rows = src.shape[-2]
    land_shape = (N_DEV, rows, D)

    def body(src_ref, land_ref, send_sems, recv_sems, src_thru, land_thru, token):
        del src_thru, land_thru
        x, y, c = _coords()
        me = 4 * x + 2 * y + c
        for k in range(1, N_DEV):
            dev, pidx = _peer(k)
            _rcopy(_xchg_src(src_ref, pidx, per_peer), land_ref.at[me], send_sems.at[k - 1],
                   recv_sems.at[k - 1], dev).start()
        token[...] = jnp.zeros_like(token)

    return pl.pallas_call(
        body, name=name,
        out_shape=(pltpu.SemaphoreType.DMA((N_DEV - 1,)), pltpu.SemaphoreType.DMA((N_DEV - 1,)),
                   pltpu.HBM(src.shape, src.dtype), pltpu.HBM(land_shape, src.dtype),
                   jax.ShapeDtypeStruct((8, 128), F32)),
        in_specs=(_HBM, _HBM),
        out_specs=(_SEM, _SEM, _HBM, _HBM, pl.BlockSpec(memory_space=pltpu.VMEM)),
        input_output_aliases={0: 2, 1: 3},
        compiler_params=pltpu.CompilerParams(has_side_effects=_EFFECT),
    )(pltpu.with_memory_space_constraint(src, pltpu.HBM),
      pltpu.with_memory_space_constraint(lax.empty(land_shape, src.dtype), pltpu.HBM))


def _xchg_wait(started, after, *, per_peer, name):
    send_sems, recv_sems, src_thru, land_thru, _ = started

    def body(src_ref, land_ref, send_sems, recv_sems, after_ref, src_dead, got_ref):
        del after_ref, src_dead, got_ref
        for k in range(1, N_DEV):
            dev, pidx = _peer(k)
            cp = _rcopy(_xchg_src(src_ref, pidx, per_peer), land_ref.at[pidx], send_sems.at[k - 1],
                        recv_sems.at[k - 1], dev)
            cp.wait_send()
            cp.wait_recv()

    return pl.pallas_call(
        body, name=name,
        out_shape=(pltpu.HBM(src_thru.shape, src_thru.dtype), pltpu.HBM(land_thru.shape, land_thru.dtype)),
        in_specs=(_HBM, _HBM, _SEM, _SEM, pl.BlockSpec(memory_space=pl.ANY)),
        out_specs=(_HBM, _HBM),
        input_output_aliases={0: 0, 1: 1},
        compiler_params=pltpu.CompilerParams(has_side_effects=_EFFECT),
    )(src_thru, land_thru, send_sems, recv_sems, after)


def _dep(token):
    return (token, (8, 128), lambda i, j, k: (0, 0))


def _small_allsum(sv, *, name):
    def body(sv_ref, all_ref, sum_ref, send_sems, recv_sems):
        x, y, c = _coords()
        me = 4 * x + 2 * y + c
        all_ref[me] = sv_ref[...]
        _exchange_all(lambda p: sv_ref, lambda s: all_ref.at[s], send_sems, recv_sems)
        acc = all_ref[0]
        for j in range(1, N_DEV):
            acc = acc + all_ref[j]
        sum_ref[...] = acc

    vm = pl.BlockSpec(memory_space=pltpu.VMEM)
    return pl.pallas_call(
        body, name=name, in_specs=[vm], out_specs=[vm, vm],
        out_shape=[jax.ShapeDtypeStruct((N_DEV, SV_ROWS, 128), F32), jax.ShapeDtypeStruct((SV_ROWS, 128), F32)],
        scratch_shapes=[pltpu.SemaphoreType.DMA((7,)), pltpu.SemaphoreType.DMA((7,))],
    )(sv)


def _ada_bwd(call, dmod_loc, *, name):
    wloc = dmod_loc.shape[1]

    def body(c_ref, d_ref, o_ref):
        o_ref[...] = _dot_hi(_silu(c_ref[...]), d_ref[...], TN)

    vm = pl.BlockSpec(memory_space=pltpu.VMEM)
    return pl.pallas_call(body, name=name, in_specs=[vm, vm], out_specs=vm,
                          out_shape=jax.ShapeDtypeStruct((D, wloc), F32),
                          compiler_params=pltpu.CompilerParams(vmem_limit_bytes=VMEM_BIG))(call, dmod_loc)


def _pad_rows(a, rows):
    return jnp.pad(a, ((0, rows - a.shape[0]), (0, 0)))


IN_SHIFT = tuple((IN_ROWS * j) % 16 for j in range(N_DEV))
IN_BASE = tuple(IN_ROWS * j - IN_SHIFT[j] for j in range(N_DEV))
IN_SEGMENTS = ((2048, XBC, C_XBC), (5152, 1024, C_POOL), (0, 2048, C_Z), (6176, 2048, C_GATE), (5120, 32, C_DT))


def _global_pieces(gs):
    pieces = []
    for j in range(N_DEV):
        lo, hi = 0, IN_ROWS_P
        if j > 0 and IN_BASE[j - 1] + IN_ROWS_P > IN_BASE[j]:
            pieces.append((IN_BASE[j], 16, gs[j - 1, IN_ROWS_P - 16:IN_ROWS_P] + gs[j, 0:16]))
            lo = 16
        if j + 1 < N_DEV and IN_BASE[j] + IN_ROWS_P > IN_BASE[j + 1]:
            hi = IN_ROWS_P - 16
        pieces.append((IN_BASE[j] + lo, hi - lo, gs[j, lo:hi]))
    return pieces


def _reorder_in_rows(gs):
    pieces = _global_pieces(gs)
    parts = []
    for lo, n, _ in IN_SEGMENTS:
        for p0, pn, arr in pieces:
            a, b = max(lo, p0), min(lo + n, p0 + pn)
            if a < b:
                parts.append(arr[a - p0:b - p0])
    parts.append(jnp.zeros((DT_PAD - 32, D), gs.dtype))
    return jnp.concatenate(parts, axis=0)


def _restore_in_shards(d):
    slabs = []
    for j in range(N_DEV):
        parts = []
        r, end = IN_BASE[j], IN_BASE[j] + IN_ROWS_P
        while r < end:
            lo, n, new = next(s for s in IN_SEGMENTS if s[0] <= r < s[0] + s[1])
            e = min(end, lo + n)
            parts.append(d[new + r - lo:new + e - lo])
            r = e
        slabs.append(jnp.concatenate(parts, axis=0))
    return jnp.stack(slabs, axis=0)


def _pack_sv(parts):
    flat = []
    for n, size in SV_PARTS:
        v = parts[n].reshape(-1).astype(F32)
        flat.append(jnp.pad(v, (0, size - v.shape[0])))
    v = jnp.concatenate(flat)
    return jnp.pad(v, (0, SV_ROWS * 128 - v.shape[0])).reshape(SV_ROWS, 128)


def _sv_get(flat, n, size):
    return flat[SV_OFF[n]:SV_OFF[n] + size]


def kernel(x, c, w_ada, b_ada, norm_mix_w, w_in, conv_w, conv_b, dt_bias, a_log, d_skip, ssd_norm_w, w_branch_ssd, pool_w, pool_scale, w_branch_pool, w_out, norm_mlp_w, w_up, w_down, norm_final_w, loss_target, m_w_ada, m_b_ada, m_norm_mix_w, m_w_in, m_conv_w, m_conv_b, m_dt_bias, m_a_log, m_d_skip, m_ssd_norm_w, m_w_branch_ssd, m_pool_w, m_pool_scale, m_w_branch_pool, m_w_out, m_norm_mlp_w, m_w_up, m_w_down, m_norm_final_w, v_w_ada, v_b_ada, v_norm_mix_w, v_w_in, v_conv_w, v_conv_b, v_dt_bias, v_a_log, v_d_skip, v_ssd_norm_w, v_w_branch_ssd, v_pool_w, v_pool_scale, v_w_branch_pool, v_w_out, v_norm_mlp_w, v_w_up, v_w_down, v_norm_final_w):
    xs_ = x[0]
    tgt = loss_target[0]
    L = xs_.shape[0]
    me = 4 * lax.axis_index("x") + 2 * lax.axis_index("y") + lax.axis_index("c")
    wloc = w_ada.shape[2]

    conv_bits = lax.bitcast_convert_type(conv_w[0], SLAB_DT).reshape(3, D)
    in_shift = (IN_ROWS * me) % 16
    slab_in = lax.dynamic_update_slice(jnp.zeros((IN_ROWS_P, D), SLAB_DT), w_in[0].T.astype(SLAB_DT),
                                       (in_shift, 0))
    slab_in = jnp.concatenate([slab_in, _pad_rows(conv_bits, CONV_ROWS)], axis=0)
    slab_rest = jnp.concatenate([
        w_branch_ssd[0].astype(SLAB_DT),
        pool_w[0].reshape(32, D).astype(SLAB_DT),
        w_branch_pool[0].astype(SLAB_DT),
        w_out[0].astype(SLAB_DT),
        w_up[0].T.astype(SLAB_DT),
        w_down[0].astype(SLAB_DT)], axis=0)
    mod_p, c_all, gs_in = _ada_gather(c, w_ada[0], b_ada.reshape(N_DEV, wloc), slab_in,
                                      name="ada_gather_w_in")
    mod = mod_p.reshape(6, D)
    shift_m, scale_m, gate_m, shift_f, scale_f, gate_f = [mod[i:i + 1] for i in range(6)]
    slab_rest, gs_in = lax.optimization_barrier((slab_rest, gs_in))
    rest_started = _xchg_start(slab_rest, per_peer=False, name="gather_rest_start")
    gather_token = rest_started[4]

    w_in_t = _reorder_in_rows(gs_in)
    conv_full = lax.bitcast_convert_type(
        gs_in[:, IN_ROWS_P:IN_ROWS_P + 3].reshape(N_DEV, 4, XBC // N_DEV, 2), F32)
    conv_full = conv_full.transpose(1, 0, 2).reshape(4, XBC)

    dtb = jnp.pad(dt_bias, ((0, 0), (0, 128 - NH)))
    arow = jnp.pad(-jnp.exp(a_log), ((0, 0), (0, 128 - NH)))
    dsk_x = jnp.repeat(d_skip, HP, axis=1)

    tm = _pick(L, (1024, 512, 256, 128))
    tm2 = _pick(L, (2048, 1024, 512, 256, 128))
    tkl = _pick(L, (4096, 2048, 1024, 512, 256, 128))
    tkl2 = _pick(L, (2048, 1024, 512, 256, 128))

    tmh = _pick(L, (512, 256, 128))
    tmq = _pick(L, (256, 128))
    zcol = C_Z // DI
    gcol = C_GATE // (2 * D)

    def whole_rows(w):
        return lambda t: ((L, w), BF16, (t, w), lambda i, j, k: (i, 0))

    def norm1_pro(x_ref, ex, outs, j):
        @pl.when(j == 0)
        def _():
            xv = x_ref[...]
            r = lax.rsqrt(jnp.mean(xv * xv, axis=-1, keepdims=True) + EPS)
            outs[1][...] = (xv * r * ex[0][...] * (1.0 + ex[1][...]) + ex[2][...]).astype(outs[1].dtype)

        return outs[1][...]

    def proj_ep(acc, ex, outs):
        outs[0][...] = acc

        @pl.when(pl.program_id(1) == NPROJ // 768 - 1)
        def _():
            pre = acc[:, 768 - DT_PAD:768 - DT_PAD + 128] + ex[3][...]
            outs[2][...] = jnp.concatenate([_softplus(pre), _sigmoid(pre)], axis=1)

    proj, h1, dtp = _mm(
        xs_, w_in_t, "nt", name="in_proj", tm=tm2, tn=768, tk=D,
        extras=[(norm_mix_w, *_vecs()), (scale_m, *_vecs()), (shift_m, *_vecs()), (dtb, *_vecs(128)),
                _dep(gather_token)],
        outs=[F32, whole_rows(D)(tm2), ((L, DT_PAD), F32, (tm2, DT_PAD), lambda i, j, k: (i, 0))],
        prologue=norm1_pro, epilogue=proj_ep)
    xbc_raw = proj
    y_ssm, hs, xbc = _ssd_fwd(xbc_raw, dtp, conv_full, conv_b, arow, dsk_x, name="ssd_fwd")

    slab_rest, gs = _xchg_wait(rest_started, y_ssm, per_peer=False, name="gather_rest_wait")
    gs = lax.dynamic_update_slice(gs, slab_rest[None], (me, 0, 0))

    def part(n, rows):
        return gs[:, REST_OFF[n]:REST_OFF[n] + rows]

    w_bssd = part("bssd", 256).reshape(DI, D)
    w_pool = part("pool", 32).reshape(N_DEV, 4, 32, PGW).transpose(1, 0, 2, 3).reshape(POOL_W, PGW)
    w_bpool = part("bpool", 128).reshape(POOL_W, D)
    w_o = part("out", 128).reshape(D, D)
    w_up_t = part("up", 512).reshape(DFF, D)
    w_dn = part("down", 512).reshape(DFF, D)

    def gnorm_pro(y_ref, ex, outs, j):
        z_ref, w_ref = ex
        yg = y_ref[...].astype(F32) * _silu(z_ref[...].astype(F32))
        segs = []
        for k in range(NG):
            sl = slice(k * GW, (k + 1) * GW)
            seg = yg[:, sl]
            r = lax.rsqrt(jnp.mean(seg * seg, axis=-1, keepdims=True) + EPS)
            segs.append((seg * r * w_ref[:, sl]).astype(BF16))
        yn_v = jnp.concatenate(segs, axis=1)
        outs[1][...] = yn_v
        return yn_v

    y_ssd, yn = _mm(y_ssm, w_bssd, "nn", name="branch_ssd", tm=tmh, tn=D, tk=DI,
                    extras=[(proj, *_rows(tmh, DI, zcol)), (ssd_norm_w, *_vecs(DI))],
                    outs=[BF16, whole_rows(DI)(tmh)], prologue=gnorm_pro)
    pooled = _pool_fwd(proj, name="pool_fwd")
    wp_spec = ((POOL_W, PGW), lambda i, j, k: (0, 0))

    def pool_pro(a_ref, ex, outs, j):
        wp_ref, s_ref = ex
        segs = []
        for g in range(4):
            sl = slice(g * PGW, (g + 1) * PGW)
            p = _dot(a_ref[:, sl], wp_ref[sl, :], NN)
            outs[1][:, sl] = p.astype(BF16)
            segs.append((p * s_ref[:, sl]).astype(BF16))
        yp1_v = jnp.concatenate(segs, axis=1)
        outs[2][...] = yp1_v
        return yp1_v

    y_pool, yp0, yp1 = _mm(pooled, w_bpool, "nn", name="branch_pool", tm=tm, tn=D, tk=D,
                           extras=[(w_pool, *wp_spec), (pool_scale, *_vecs())],
                           outs=[BF16, whole_rows(D)(tm), whole_rows(D)(tm)], prologue=pool_pro)

    def merge_pro(a_ref, ex, outs, j):
        s = _sigmoid(ex[1][...].astype(F32))
        mv = (s[:, :D] * a_ref[...].astype(F32) + s[:, D:] * ex[0][...].astype(F32)).astype(BF16)
        outs[3][...] = mv
        return mv

    mix, x1, h2, m = _mm(y_ssd, w_o, "nn", name="out_proj", tm=tmh, tn=D, tk=D,
                         extras=[(y_pool, *_rows(tmh)), (proj, *_rows(tmh, 2 * D, gcol)),
                                 (xs_, *_rows(tmh)), (gate_m, *_vecs()), (norm_mlp_w, *_vecs()),
                                 (scale_f, *_vecs()), (shift_f, *_vecs())],
                         outs=[BF16, F32, BF16, whole_rows(D)(tmh)], prologue=merge_pro,
                         epilogue=lambda acc, ex, outs: _ep_resid_norm(acc, ex[2:], outs[:3]))

    def relu2(acc, ex, outs):
        r = jnp.maximum(acc, 0.0)
        outs[0][...] = acc.astype(BF16)
        outs[1][...] = (r * r).astype(BF16)

    up, act = _mm(h2, w_up_t, "nt", name="mlp_up", outs=[BF16, BF16], tm=tm2, tn=1024, tk=D, epilogue=relu2)

    dx2, ddown, loss_p, dnwf, dgate_f = _mm(
        act, w_dn, "nn", name="mlp_down", tm=tmh, tn=D, tk=DFF,
        extras=[(x1, *_rows(tmh)), (tgt, *_rows(tmh)), (gate_f, *_vecs()), (norm_final_w.reshape(1, D), *_vecs())],
        outs=[F32, BF16, _sum_out(128), _sum_out(), _sum_out()], epilogue=_ep_final)

    def drelu2(acc, ex, outs):
        outs[0][...] = (acc * (2.0 * jnp.maximum(ex[0][...].astype(F32), 0.0))).astype(BF16)

    def dep_last(ep):
        return lambda acc, ex, outs: ep(acc, ex[:-1], outs)

    dup = _mm(ddown, w_dn, "nt", name="mlp_down_dx", outs=[BF16], tm=tm2, tn=1024, tk=D,
              extras=[(up, (tm2, 1024), lambda i, j, k: (i, j))], epilogue=drelu2)
    g_dn = _mm(act, ddown, "tn", name="mlp_down_dw", outs=[SLAB_DT], tm=tmq, tn=D, tk=L)
    g_up_t = _mm(dup, h2, "tn", name="mlp_up_dw", outs=[SLAB_DT], tm=tmq, tn=D, tk=L)
    gslab_mlp = jnp.concatenate([g_up_t.reshape(N_DEV, 512, D), g_dn.reshape(N_DEV, 512, D)], axis=1)
    mlp_started = _xchg_start(gslab_mlp, per_peer=True, name="scatter_mlp_start")
    dx1, p2, q2, dmix, dgate_m = _mm(
        dup, w_up_t, "nn", name="mlp_up_dx", tm=tmh, tn=D, tk=DFF,
        extras=[(x1, *_rows(tmh)), (dx2, *_rows(tmh)), (norm_mlp_w, *_vecs()), (scale_f, *_vecs()),
                (mix, *_rows(tmh)), (gate_m, *_vecs()), _dep(mlp_started[4])],
        outs=[F32, _sum_out(), _sum_out(), BF16, _sum_out()], epilogue=dep_last(_ep_norm_bwd))
    gcol = C_GATE // (2 * D)
    dy_ssd, dy_pool, dproj = _mm(
        dmix, w_o, "nt", name="out_proj_dx", tm=tmh, tn=D, tk=D,
        extras=[(y_ssd, *_rows(tmh)), (y_pool, *_rows(tmh)), (proj, *_rows(tmh, 2 * D, gcol))],
        outs=[BF16, BF16, ((L, NPROJ), BF16, *_rows(tmh, 2 * D, gcol))], epilogue=_ep_merge_bwd)
    g_o = _mm(m, dmix, "tn", name="out_proj_dw", outs=[SLAB_DT], tm=tmq, tn=D, tk=L)
    zcol = C_Z // DI
    dy_ssm, dproj, d_snw = _mm(
        dy_ssd, w_bssd, "nt", name="branch_ssd_dx", tm=tmh, tn=DI, tk=D,
        extras=[(y_ssm, *_rows(tmh, DI)), (proj, *_rows(tmh, DI, zcol)), (ssd_norm_w, *_vecs(DI)),
                (dproj, None, None)],
        outs=[F32, ((L, NPROJ), BF16, *_rows(tmh, DI, zcol)), _sum_out(DI)],
        epilogue=_ep_gated_norm_bwd, aliases={3: 1})
    g_bssd = _mm(yn, dy_ssd, "tn", name="branch_ssd_dw", outs=[SLAB_DT], tm=tmq, tn=D, tk=L)
    dxbc, dproj, d_a, d_dx, d_dtb = _ssd_bwd(dy_ssm, xbc, dtp, hs, arow, dsk_x, dproj, name="ssd_bwd")
    dproj, d_cw, d_cb = _conv_bwd(xbc_raw, dxbc, conv_full, conv_b, dproj, name="conv_bwd")
    def pool_bwd_ep(acc, ex, outs):
        y_ref, s_ref, wp_ref = ex
        o_ref, ds_ref, dpool_ref = outs
        dyp0_v = (acc * s_ref[...]).astype(BF16)
        o_ref[...] = dyp0_v
        _acc_out(ds_ref, _colsum(acc * y_ref[...].astype(F32)), _row_step())
        for g in range(4):
            sl = slice(g * PGW, (g + 1) * PGW)
            dpool_ref[:, sl] = _dot(dyp0_v[:, sl], wp_ref[sl, :], NT)

    dyp0, d_ps, dpooled = _mm(dy_pool, w_bpool, "nt", name="branch_pool_dx", tm=tm, tn=D, tk=D,
                              extras=[(yp0, *_rows(tm)), (pool_scale, *_vecs()), (w_pool, *wp_spec)],
                              outs=[BF16, _sum_out(), F32], epilogue=pool_bwd_ep)
    g_bpool = _mm(yp1, dy_pool, "tn", name="branch_pool_dw", outs=[SLAB_DT], tm=tmq, tn=D, tk=L)
    g_pool = _mm_pool_tn(pooled, dyp0, name="pool_mix_dw", tk=tkl)
    gslab_mix = jnp.concatenate([
        g_bssd.reshape(N_DEV, 256, D),
        g_pool.reshape(4, N_DEV, 32, PGW).transpose(1, 0, 2, 3).reshape(N_DEV, 32, D).astype(SLAB_DT),
        g_bpool.reshape(N_DEV, 128, D),
        g_o.reshape(N_DEV, 128, D)], axis=1)
    mix_started = _xchg_start(gslab_mix, per_peer=True, name="scatter_mix_start")
    dproj = _pool_bwd(dpooled, dproj, name="pool_bwd")
    g_in_t = _mm(dproj, h1, "tn", name="in_proj_dw", outs=[SLAB_DT], tm=tmq, tn=D, tk=L,
                 extras=[_dep(mix_started[4])])
    gslab_in = _restore_in_shards(g_in_t)
    in_started = _xchg_start(gslab_in, per_peer=True, name="scatter_in_start")
    grad_x, p1, q1 = _mm(
        dproj, w_in_t, "nn", name="in_proj_dx", tm=tmq, tn=D, tk=NPROJ,
        extras=[(xs_, *_rows(tmq)), (dx1, *_rows(tmq)), (norm_mix_w, *_vecs()), (scale_m, *_vecs()),
                _dep(in_started[4])],
        outs=[F32, _sum_out(), _sum_out()], epilogue=dep_last(_ep_norm_bwd))

    def landed(started, after, tile, name):
        src, land = _xchg_wait(started, after, per_peer=True, name=name + "_wait")
        own = lax.dynamic_slice_in_dim(src, me, 1, axis=0)
        return _slab_sum(lax.dynamic_update_slice(land, own, (me, 0, 0)), tile=tile, name=name + "_sum")

    gsum_mlp = landed(mlp_started, grad_x, 256, "scatter_mlp")
    gsum_mix = landed(mix_started, grad_x, 272, "scatter_mix")
    gsum_in = landed(in_started, grad_x, 208, "scatter_in")

    dmod = jnp.concatenate([q1, p1 * norm_mix_w, dgate_m, q2, p2 * norm_mlp_w, dgate_f], axis=1)
    d_alog = d_a[:, :NH] * (-jnp.exp(a_log))
    sv = _pack_sv({
        "b_ada": dmod, "norm_mix_w": p1 * (1.0 + scale_m), "conv_b": d_cb, "dt_bias": d_dtb[:, :NH],
        "a_log": d_alog, "d_skip": d_dx.reshape(NH, HP).sum(axis=1), "ssd_norm_w": d_snw,
        "pool_scale": d_ps, "norm_mlp_w": p2 * (1.0 + scale_f), "norm_final_w": dnwf, "conv_w": d_cw,
        "loss": loss_p[:, :1]})
    sv_all, sv_sum = _small_allsum(sv, name="small_allsum")
    flat = sv_sum.reshape(-1)
    loss = flat[SV_OFF["loss"]]
    dmod_all = sv_all.reshape(N_DEV, SV_ROWS * 128)[:, :6 * D]
    g_w_ada = _ada_bwd(c_all, lax.dynamic_slice_in_dim(dmod_all, me * wloc, wloc, axis=1), name="ada_bwd")

    g_conv_w = lax.dynamic_slice_in_dim(_sv_get(flat, "conv_w", 4 * XBC).reshape(4, XBC),
                                        me * (XBC // N_DEV), XBC // N_DEV, axis=1)
    small = [("b_ada", b_ada, m_b_ada, v_b_ada), ("norm_mix_w", norm_mix_w, m_norm_mix_w, v_norm_mix_w),
             ("conv_b", conv_b, m_conv_b, v_conv_b), ("dt_bias", dt_bias, m_dt_bias, v_dt_bias),
             ("a_log", a_log, m_a_log, v_a_log), ("d_skip", d_skip, m_d_skip, v_d_skip),
             ("ssd_norm_w", ssd_norm_w, m_ssd_norm_w, v_ssd_norm_w),
             ("pool_scale", pool_scale, m_pool_scale, v_pool_scale),
             ("norm_mlp_w", norm_mlp_w, m_norm_mlp_w, v_norm_mlp_w),
             ("norm_final_w", norm_final_w[None], m_norm_final_w[None], v_norm_final_w[None]),
             ("conv_w", conv_w[0], m_conv_w[0], v_conv_w[0])]
    small_out = _adamw_small(sv_sum.reshape(1, SV_ROWS * 128), g_conv_w, small, name="adamw_small")
    small_out["norm_final_w"] = tuple(a[0] for a in small_out["norm_final_w"])
    small_out["conv_w"] = tuple(a[None] for a in small_out["conv_w"])

    def gpart(n, rows_):
        return gsum_mix[MIX_OFF[n]:MIX_OFF[n] + rows_]

    def lin(a):
        return a[0].T.reshape(IN_ROWS * 8, 128)

    g_lin = lax.dynamic_slice_in_dim(gsum_in, in_shift, IN_ROWS, axis=0).reshape(IN_ROWS * 8, 128)
    dlt, mn, vn = _adamw(lin(w_in), g_lin, lin(m_w_in), lin(v_w_in), name="adamw_w_in", tr=IN_ROWS * 2)
    big_in = tuple(a.reshape(IN_ROWS, D).T[None] for a in (g_lin, dlt, mn, vn))

    big = {
        "w_ada": (w_ada, m_w_ada, v_w_ada, g_w_ada, (D, wloc)),
        "w_branch_ssd": (w_branch_ssd, m_w_branch_ssd, v_w_branch_ssd, gpart("bssd", 256), (256, D)),
        "pool_w": (pool_w, m_pool_w, v_pool_w, gpart("pool", 32).reshape(128, PGW), (128, PGW)),
        "w_branch_pool": (w_branch_pool, m_w_branch_pool, v_w_branch_pool, gpart("bpool", 128), (128, D)),
        "w_out": (w_out, m_w_out, v_w_out, gpart("out", 128), (128, D)),
        "w_up": (w_up, m_w_up, v_w_up, gsum_mlp[:512].T, (D, 512)),
        "w_down": (w_down, m_w_down, v_w_down, gsum_mlp[512:], (512, D)),
    }
    big_out = {}
    for n, (w, mm_, vv, g, shp2) in big.items():
        dlt, mn, vn = _adamw(w.reshape(shp2), g, mm_.reshape(shp2), vv.reshape(shp2), name="adamw_" + n)
        big_out[n] = (g.reshape(w.shape), dlt.reshape(w.shape), mn.reshape(w.shape), vn.reshape(w.shape))

    order = ["w_ada", "b_ada", "norm_mix_w", "w_in", "conv_w", "conv_b", "dt_bias", "a_log", "d_skip",
             "ssd_norm_w", "w_branch_ssd", "pool_w", "pool_scale", "w_branch_pool", "w_out", "norm_mlp_w",
             "w_up", "w_down", "norm_final_w"]
    big_out["w_in"] = big_in
    res = {**small_out, **big_out}
    outs = [loss, grad_x.reshape(x.shape)]
    for k in range(4):
        outs += [res[n][k] for n in order]
    return tuple(outs)
```

```python
import functools

import numpy as np
import jax
import jax.numpy as jnp
from jax import lax
from jax.experimental import pallas as pl
from jax.experimental.pallas import tpu as pltpu

F32 = jnp.float32
BF16 = jnp.bfloat16
SLAB_DT = jnp.bfloat16
_MXU_DTYPE = jnp.bfloat16

N_DEV = 8
D = 1024
DI = 2048
NH = 32
HP = 64
NG = 4
NS = 128
Q = 128
XBC = DI + 2 * NG * NS
DFF = 4096
N_IN = 8224
EPS = 1e-5
POOL_W = 1024
PGW = 256

C_XBC, C_POOL, C_Z, C_GATE, C_DT = 0, 3072, 4096, 6144, 8192
DT_PAD = 256
NPROJ = C_DT + DT_PAD

IN_ROWS = N_IN // N_DEV
IN_ROWS_P = 1040
CONV_ROWS = 16
REST_PARTS = (("bssd", 256), ("pool", 32), ("bpool", 128), ("out", 128), ("up", 512), ("down", 512))
REST_OFF = {}
_o = 0
for _n, _r in REST_PARTS:
    REST_OFF[_n] = _o
    _o += _r
REST_ROWS = _o
MIX_PARTS = (("bssd", 256), ("pool", 32), ("bpool", 128), ("out", 128))
MIX_OFF = {}
_o = 0
for _n, _r in MIX_PARTS:
    MIX_OFF[_n] = _o
    _o += _r
MIX_ROWS = _o

SV_PARTS = (("b_ada", 6144), ("norm_mix_w", 1024), ("conv_b", 3072), ("dt_bias", 128), ("a_log", 128),
            ("d_skip", 128), ("ssd_norm_w", 2048), ("pool_scale", 1024), ("norm_mlp_w", 1024),
            ("norm_final_w", 1024), ("conv_w", 4 * XBC), ("loss", 128))
SV_OFF = {}
_o = 0
for _n, _r in SV_PARTS:
    SV_OFF[_n] = _o
    _o += _r
SV_ROWS = 224
assert _o <= SV_ROWS * 128

ADAM_LR, ADAM_B1, ADAM_B2, ADAM_EPS, ADAM_WD, ADAM_STEP = 0.001, 0.9, 0.999, 1e-08, 0.01, 10

VMEM_BIG = 56 * 1024 * 1024
NEG = -1e30

NN = ((1,), (0,))
NT = ((1,), (1,))
TN = ((0,), (0,))


def _dot(a, b, dims=NN):
    return lax.dot_general(a.astype(_MXU_DTYPE), b.astype(_MXU_DTYPE), (dims, ((), ())),
                           preferred_element_type=F32)


def _dot_hi(a, b, dims=NN):
    return lax.dot_general(a.astype(F32), b.astype(F32), (dims, ((), ())),
                           precision=lax.Precision.HIGHEST, preferred_element_type=F32)


def _pick(n, cands):
    for c in cands:
        if n % c == 0:
            return c
    return n


def _sigmoid(x):
    return 1.0 / (1.0 + jnp.exp(-x))


def _silu(x):
    return x * _sigmoid(x)


def _dsilu(x):
    s = _sigmoid(x)
    return s * (1.0 + x * (1.0 - s))


def _softplus(x):
    return jnp.maximum(x, 0.0) + jnp.log(1.0 + jnp.exp(-jnp.abs(x)))


def _params(sem, vmem=None):
    return pltpu.CompilerParams(dimension_semantics=sem, vmem_limit_bytes=vmem)


def _row_step():
    return pl.program_id(0)


def _mm(a, b, mode, *, name, outs, tm, tn, tk, extras=(), epilogue=None, aliases=None, prologue=None):
    if mode == "tn":
        K, M = a.shape
        N = b.shape[1]
        a_spec = pl.BlockSpec((tk, tm), lambda i, j, k: (k, i))
        b_spec = pl.BlockSpec((tk, tn), lambda i, j, k: (k, j))
        dims = TN
    else:
        M = a.shape[0]
        K = b.shape[0] if mode == "nn" else b.shape[1]
        if prologue is None:
            assert a.shape[1] == K
            a_spec = pl.BlockSpec((tm, tk), lambda i, j, k: (i, k))
        else:
            assert tk == K
            a_spec = pl.BlockSpec((tm, a.shape[1]), lambda i, j, k: (i, 0))
        if mode == "nn":
            N = b.shape[1]
            b_spec = pl.BlockSpec((tk, tn), lambda i, j, k: (k, j))
            dims = NN
        else:
            N = b.shape[0]
            b_spec = pl.BlockSpec((tn, tk), lambda i, j, k: (j, k))
            dims = NT
    assert M % tm == 0 and N % tn == 0 and K % tk == 0, (name, M, N, K, tm, tn, tk)
    nk = K // tk
    ne, no = len(extras), len(outs)
    if epilogue is None:
        def epilogue(acc, ex, out_refs):
            out_refs[0][...] = acc.astype(out_refs[0].dtype)

    def body(a_ref, b_ref, *rest):
        ex, out_refs = rest[:ne], rest[ne:ne + no]
        lhs = a_ref[...] if prologue is None else prologue(a_ref, ex, out_refs, pl.program_id(1))
        p = _dot(lhs, b_ref[...], dims)
        if nk == 1:
            epilogue(p, ex, out_refs)
        else:
            acc = rest[-1]
            k = pl.program_id(2)

            @pl.when(k == 0)
            def _():
                acc[...] = p

            @pl.when(jnp.logical_and(k > 0, k < nk - 1))
            def _():
                acc[...] += p

            @pl.when(k == nk - 1)
            def _():
                epilogue(acc[...] + p, ex, out_refs)

    out_specs, out_shape = [], []
    for o in outs:
        if isinstance(o, tuple):
            shape, dt, bs, im = o
            out_specs.append(pl.BlockSpec(bs, im))
            out_shape.append(jax.ShapeDtypeStruct(shape, dt))
        else:
            out_specs.append(pl.BlockSpec((tm, tn), lambda i, j, k: (i, j)))
            out_shape.append(jax.ShapeDtypeStruct((M, N), o))
    in_specs = [a_spec, b_spec]
    for _, bs, im in extras:
        in_specs.append(pl.BlockSpec(memory_space=pl.ANY) if bs is None else pl.BlockSpec(bs, im))
    res = pl.pallas_call(
        body, name=name,
        grid=(M // tm, N // tn, nk),
        in_specs=in_specs, out_specs=out_specs, out_shape=out_shape,
        scratch_shapes=[pltpu.VMEM((tm, tn), F32)] if nk > 1 else [],
        input_output_aliases={2 + e: o for e, o in (aliases or {}).items()},
        compiler_params=_params(("arbitrary", "arbitrary", "arbitrary"), VMEM_BIG),
    )(a, b, *[e[0] for e in extras])
    return res if no > 1 else res[0]


def _rows(tm, w=D, col=0):
    return (tm, w), lambda i, j, k, c=col: (i, c)


def _vecs(w=D, col=0):
    return (1, w), lambda i, j, k, c=col: (0, c)


def _sum_out(w=D):
    return ((1, w), F32, (1, w), lambda i, j, k: (0, 0))


def _mm_pool_tn(a, b, *, name, tk):
    L = a.shape[0]

    def body(a_ref, b_ref, o_ref):
        p = _dot(a_ref[...], b_ref[...], TN)

        @pl.when(pl.program_id(1) == 0)
        def _():
            o_ref[...] = p

        @pl.when(pl.program_id(1) > 0)
        def _():
            o_ref[...] += p

    blk = pl.BlockSpec((tk, PGW), lambda g, k: (k, g))
    return pl.pallas_call(body, name=name, grid=(4, L // tk), in_specs=[blk, blk],
                          out_specs=pl.BlockSpec((PGW, PGW), lambda g, k: (g, 0)),
                          out_shape=jax.ShapeDtypeStruct((POOL_W, PGW), F32),
                          compiler_params=_params(("parallel", "arbitrary")))(a, b)


def _acc_out(ref, val, i):
    @pl.when(i == 0)
    def _():
        ref[...] = val

    @pl.when(i > 0)
    def _():
        ref[...] += val


def _colsum(v):
    return jnp.sum(v, axis=0, keepdims=True)


def _ep_resid_norm(acc, ex, outs):
    x_ref, g_ref, nw_ref, sc_ref, sh_ref = ex
    mix_ref, x1_ref, h_ref = outs
    mix_ref[...] = acc.astype(mix_ref.dtype)
    xv = x_ref[...] + g_ref[...] * acc
    x1_ref[...] = xv
    r = lax.rsqrt(jnp.mean(xv * xv, axis=-1, keepdims=True) + EPS)
    h_ref[...] = (xv * r * nw_ref[...] * (1.0 + sc_ref[...]) + sh_ref[...]).astype(h_ref.dtype)


def _ep_final(acc, ex, outs):
    x1_ref, t_ref, g_ref, nw_ref = ex
    dx2_ref, dd_ref, loss_ref, dnw_ref, dg_ref = outs
    i = _row_step()
    x2 = x1_ref[...] + g_ref[...] * acc
    r = lax.rsqrt(jnp.mean(x2 * x2, axis=-1, keepdims=True) + EPS)
    xh = x2 * r
    e = xh * nw_ref[...] - t_ref[...]
    part = 0.5 * jnp.sum(jnp.mean(e * e, axis=-1, keepdims=True), axis=0, keepdims=True)
    dy = e * (1.0 / D)
    g = dy * nw_ref[...]
    dx2 = r * (g - xh * jnp.mean(g * xh, axis=-1, keepdims=True))
    dx2_ref[...] = dx2
    dd_ref[...] = (dx2 * g_ref[...]).astype(dd_ref.dtype)
    _acc_out(loss_ref, jnp.broadcast_to(part, (1, 128)), i)
    _acc_out(dnw_ref, _colsum(dy * xh), i)
    _acc_out(dg_ref, _colsum(dx2 * acc), i)


def _ep_norm_bwd(acc, ex, outs):
    x_ref, dr_ref, nw_ref, sc_ref = ex[:4]
    dx_ref, p_ref, q_ref = outs[:3]
    i = _row_step()
    xv = x_ref[...]
    r = lax.rsqrt(jnp.mean(xv * xv, axis=-1, keepdims=True) + EPS)
    xh = xv * r
    g = acc * (nw_ref[...] * (1.0 + sc_ref[...]))
    dx = dr_ref[...] + r * (g - xh * jnp.mean(g * xh, axis=-1, keepdims=True))
    dx_ref[...] = dx
    _acc_out(p_ref, _colsum(acc * xh), i)
    _acc_out(q_ref, _colsum(acc), i)
    if len(ex) > 4:
        m_ref, g_ref = ex[4:]
        dm_ref, dg_ref = outs[3:]
        dm_ref[...] = (dx * g_ref[...]).astype(dm_ref.dtype)
        _acc_out(dg_ref, _colsum(dx * m_ref[...].astype(F32)), i)


def _ep_merge_bwd(acc, ex, outs):
    a_ref, b_ref, gl_ref = ex
    da_ref, db_ref, dgl_ref = outs
    s = _sigmoid(gl_ref[...].astype(F32))
    s1, s2 = s[:, :D], s[:, D:]
    da_ref[...] = (acc * s1).astype(da_ref.dtype)
    db_ref[...] = (acc * s2).astype(db_ref.dtype)
    dgl_ref[:, :D] = (acc * a_ref[...].astype(F32) * s1 * (1.0 - s1)).astype(dgl_ref.dtype)
    dgl_ref[:, D:] = (acc * b_ref[...].astype(F32) * s2 * (1.0 - s2)).astype(dgl_ref.dtype)


GW = DI // NG


def _ep_gated_norm_bwd(acc, ex, outs):
    y_ref, z_ref, w_ref, _ = ex
    dy_ref, dz_ref, dw_ref = outs
    zv = z_ref[...].astype(F32)
    yv = y_ref[...].astype(F32)
    sg = _sigmoid(zv)
    sz = zv * sg
    yg = yv * sz
    dsz = sg * (1.0 + zv * (1.0 - sg))
    dws = []
    for k in range(NG):
        sl = slice(k * GW, (k + 1) * GW)
        seg = yg[:, sl]
        r = lax.rsqrt(jnp.mean(seg * seg, axis=-1, keepdims=True) + EPS)
        sh = seg * r
        dn = acc[:, sl]
        g = dn * w_ref[:, sl]
        dyg = r * (g - sh * jnp.mean(g * sh, axis=-1, keepdims=True))
        dy_ref[:, sl] = dyg * sz[:, sl]
        dz_ref[:, sl] = (dyg * yv[:, sl] * dsz[:, sl]).astype(dz_ref.dtype)
        dws.append(_colsum(dn * sh))
    _acc_out(dw_ref, jnp.concatenate(dws, axis=1), _row_step())


CONV_CB = 128
HALO = 16


def _time_chunk(L):
    return _pick(L, (256, 128))


def _with_halo(x_ref, i, r0, rc):
    p0 = pl.multiple_of(jnp.maximum(r0 - HALO, 0), HALO)
    prev = jnp.where(i > 0, x_ref[pl.ds(p0, HALO), :].astype(F32), 0.0)
    return jnp.concatenate([prev, x_ref[pl.ds(r0, rc), :].astype(F32)], axis=0)


def _conv_bwd(proj, dy, w, b, dproj, *, name):
    L = proj.shape[0]
    rc = _time_chunk(L)
    n = L // rc

    def body(x_ref, dy_ref, w_ref, b_ref, dp_in, dx_ref, dw_ref, db_ref, xpad, dpad):
        del dp_in
        wv = w_ref[...]
        bv = b_ref[...]
        dpad[rc:rc + HALO, :] = jnp.zeros((HALO, CONV_CB), F32)

        def step(k, carry):
            db, d0, d1, d2, d3 = carry
            i = n - 1 - k
            r0 = pl.multiple_of(i * rc, rc)
            p0 = pl.multiple_of(jnp.maximum(r0 - HALO, 0), HALO)
            xpad[0:HALO, :] = jnp.where(i > 0, x_ref[pl.ds(p0, HALO), :].astype(F32), 0.0)
            xpad[HALO:HALO + rc, :] = x_ref[pl.ds(r0, rc), :].astype(F32)
            xk = [xpad[HALO - j:HALO - j + rc, :] for j in range(4)]
            pre = bv
            for j in range(4):
                pre = pre + xk[j] * wv[3 - j:4 - j]
            dpre = dy_ref[pl.ds(r0, rc), :] * _dsilu(pre)
            dpad[0:rc, :] = dpre
            acc = dpre * wv[3:4]
            for j in (1, 2, 3):
                acc = acc + dpad[j:j + rc, :] * wv[3 - j:4 - j]
            dx_ref[pl.ds(r0, rc), :] = acc.astype(dx_ref.dtype)
            dpad[rc:rc + HALO, :] = dpre[:HALO]
            return (db + _colsum(dpre), d0 + _colsum(dpre * xk[3]), d1 + _colsum(dpre * xk[2]),
                    d2 + _colsum(dpre * xk[1]), d3 + _colsum(dpre * xk[0]))

        z = jnp.zeros((1, CONV_CB), F32)
        db, d0, d1, d2, d3 = lax.fori_loop(0, n, step, (z, z, z, z, z))
        db_ref[...] = db
        dw_ref[...] = jnp.concatenate([d0, d1, d2, d3], axis=0)

    nb = XBC // CONV_CB
    return pl.pallas_call(
        body, name=name, grid=(nb,),
        in_specs=[pl.BlockSpec((L, CONV_CB), lambda j: (0, j + C_XBC // CONV_CB)),
                  pl.BlockSpec((L, CONV_CB), lambda j: (0, j)),
                  pl.BlockSpec((4, CONV_CB), lambda j: (0, j)), pl.BlockSpec((1, CONV_CB), lambda j: (0, j)),
                  pl.BlockSpec(memory_space=pl.ANY)],
        out_specs=[pl.BlockSpec((L, CONV_CB), lambda j: (0, j + C_XBC // CONV_CB)),
                   pl.BlockSpec((4, CONV_CB), lambda j: (0, j)), pl.BlockSpec((1, CONV_CB), lambda j: (0, j))],
        out_shape=[jax.ShapeDtypeStruct((L, NPROJ), BF16), jax.ShapeDtypeStruct((4, XBC), F32),
                   jax.ShapeDtypeStruct((1, XBC), F32)],
        scratch_shapes=[pltpu.VMEM((rc + HALO, CONV_CB), F32), pltpu.VMEM((rc + HALO, CONV_CB), F32)],
        input_output_aliases={4: 0},
        compiler_params=_params(("parallel",), VMEM_BIG))(proj, dy, w, b, dproj)


def _pool_fwd(proj, *, name):
    L = proj.shape[0]
    rc = _time_chunk(L)
    n = L // rc

    def body(x_ref, o_ref, pad):
        g = pl.program_id(0)
        pad[0:HALO, :] = jnp.zeros((HALO, PGW), F32)

        def fill(i, c):
            r0 = pl.multiple_of(i * rc, rc)
            pad[pl.ds(r0 + HALO, rc), :] = x_ref[pl.ds(r0, rc), :].astype(F32)
            return c

        lax.fori_loop(0, n, fill, 0)
        rows = lax.broadcasted_iota(jnp.int32, (rc, PGW), 0)

        for gi in range(4):
            win = 2 << gi

            @pl.when(g == gi)
            def _(gi=gi, win=win):
                def step(i, c):
                    r0 = pl.multiple_of(i * rc, rc)
                    ext = pad[pl.ds(r0, rc + HALO), :]
                    s = ext
                    sh = 1
                    while sh < win:
                        s = s + pltpu.roll(s, sh, 0)
                        sh *= 2
                    cnt = jnp.minimum(rows + (r0 + 1), win).astype(F32)
                    o_ref[pl.ds(r0, rc), :] = (s[HALO:] / cnt - ext[HALO:]).astype(o_ref.dtype)
                    return c

                lax.fori_loop(0, n, step, 0)

    return pl.pallas_call(
        body, name=name, grid=(4,),
        in_specs=[pl.BlockSpec((L, PGW), lambda j: (0, j + C_POOL // PGW))],
        out_specs=pl.BlockSpec((L, PGW), lambda j: (0, j)),
        out_shape=jax.ShapeDtypeStruct((L, POOL_W), BF16),
        scratch_shapes=[pltpu.VMEM((L + HALO, PGW), F32)],
        compiler_params=_params(("parallel",), VMEM_BIG))(proj)


def _pool_bwd(dpooled, dproj, *, name):
    L = dpooled.shape[0]
    rc = _time_chunk(L)
    n = L // rc

    def body(d_ref, dp_in, o_ref, pad):
        del dp_in
        g = pl.program_id(0)
        pad[L:L + HALO, :] = jnp.zeros((HALO, PGW), F32)
        rows = lax.broadcasted_iota(jnp.int32, (rc, PGW), 0)

        for gi in range(4):
            win = 2 << gi

            @pl.when(g == gi)
            def _(gi=gi, win=win):
                def fill(i, c):
                    r0 = pl.multiple_of(i * rc, rc)
                    cnt = jnp.minimum(rows + (r0 + 1), win).astype(F32)
                    pad[pl.ds(r0, rc), :] = d_ref[pl.ds(r0, rc), :] / cnt
                    return c

                lax.fori_loop(0, n, fill, 0)

                def step(i, c):
                    r0 = pl.multiple_of(i * rc, rc)
                    s = pad[pl.ds(r0, rc + HALO), :]
                    sh = 1
                    while sh < win:
                        s = s + pltpu.roll(s, rc + HALO - sh, 0)
                        sh *= 2
                    o_ref[pl.ds(r0, rc), :] = (s[:rc] - d_ref[pl.ds(r0, rc), :]).astype(o_ref.dtype)
                    return c

                lax.fori_loop(0, n, step, 0)

    return pl.pallas_call(
        body, name=name, grid=(4,),
        in_specs=[pl.BlockSpec((L, PGW), lambda j: (0, j)), pl.BlockSpec(memory_space=pl.ANY)],
        out_specs=pl.BlockSpec((L, PGW), lambda j: (0, j + C_POOL // PGW)),
        out_shape=jax.ShapeDtypeStruct((L, NPROJ), BF16),
        scratch_shapes=[pltpu.VMEM((L + HALO, PGW), F32)],
        input_output_aliases={1: 0},
        compiler_params=_params(("parallel",), VMEM_BIG))(dpooled, dproj)


_SPLIT_DT = jnp.bfloat16


def _ssd_consts():
    tri = np.tril(np.ones((Q, Q), np.float32))
    exp = np.zeros((128, DI), np.float32)
    for h in range(NH):
        exp[h, h * HP:(h + 1) * HP] = 1.0
    exp2 = np.concatenate([exp, exp], axis=0)
    return (jnp.asarray(tri, dtype=_SPLIT_DT), jnp.asarray(tri.T.copy(), dtype=_SPLIT_DT),
            jnp.asarray(exp2, dtype=_SPLIT_DT))


def _split(v, n):
    parts, r = [], v
    for _ in range(n):
        p = r.astype(_SPLIT_DT)
        parts.append(p)
        r = r - p.astype(F32)
    return parts


def _bdot(a, b, dims):
    return lax.dot_general(a, b, (dims, ((), ())), preferred_element_type=F32)


def _tri_sum(t_ref, v):
    r = _bdot(t_ref[...], jnp.concatenate(_split(v, 3), axis=1), NN)
    return r[:, :128] + r[:, 128:256] + r[:, 256:]


def _expand(v, e2_ref):
    return _bdot(jnp.concatenate(_split(v, 2), axis=1), e2_ref[...], NN)


def _reduce_heads(vals, eg):
    parts = []
    for v in vals:
        parts += _split(v, 2)
    r = _bdot(jnp.concatenate(parts, axis=0), eg, NT)
    return [r[2 * i * Q:(2 * i + 1) * Q] + r[(2 * i + 1) * Q:(2 * i + 2) * Q] for i in range(len(vals))]


def _ssd_common(xbc_ref, dtw_ref, arow_ref, t_ref, e_ref):
    dt = dtw_ref[:, :128]
    sig = dtw_ref[:, 128:]
    acs = _tri_sum(t_ref, dt * arow_ref[...])
    acs_x = _expand(acs, e_ref)
    dt_x = _expand(dt, e_ref)
    xs = xbc_ref[:, 0:DI]
    return sig, dt, acs, acs.T, acs_x, dt_x, xs


CONV_SLAB = 512


def _ssd_fwd(raw, dtp, cw, cb, arow, dsk_x, *, name):
    L = raw.shape[0]
    nc = L // Q
    tri, _, expand = _ssd_consts()

    def body(raw_ref, halo_ref, cw_ref, cb_ref, dtw_ref, arow_ref, dsk_ref, t_ref, e_ref,
             y_ref, hs_ref, xbc_ref, h_scr, cpad):
        c = pl.program_id(0)

        @pl.when(c == 0)
        def _():
            h_scr[...] = jnp.zeros_like(h_scr)

        cpad[0:8, :] = jnp.where(c > 0, halo_ref[...], 0.0)
        cpad[8:8 + Q, :] = raw_ref[...]
        for lo in range(0, XBC, CONV_SLAB):
            sl = slice(lo, lo + CONV_SLAB)
            acc = cb_ref[:, sl]
            for j in range(4):
                acc = acc + cpad[8 - j:8 - j + Q, sl] * cw_ref[3 - j:4 - j, sl]
            xbc_ref[:, sl] = acc * _sigmoid(acc)

        _, dt, acs, acs_t, acs_x, dt_x, xs = _ssd_common(xbc_ref, dtw_ref, arow_ref, t_ref, e_ref)
        xdt = xs * dt_x
        eacs = jnp.exp(acs_x)
        acs_last = acs_x[Q - 1:Q, :]
        dec = jnp.exp(acs_last - acs_x)
        hs_ref[0] = h_scr[...].astype(hs_ref.dtype)
        causal = lax.broadcasted_iota(jnp.int32, (Q, Q), 0) >= lax.broadcasted_iota(jnp.int32, (Q, Q), 1)
        first = lax.broadcasted_iota(jnp.int32, (Q, 128), 1) < HP
        for g in range(NG):
            bg = xbc_ref[:, DI + g * NS:DI + (g + 1) * NS]
            cg = xbc_ref[:, DI + NG * NS + g * NS:DI + NG * NS + (g + 1) * NS]
            s = _dot(cg, bg, NT)
            sl = slice(g * GW, (g + 1) * GW)
            hg = h_scr[:, sl]
            yoff = _dot(cg, hg, NN) * eacs[:, sl]
            st = _dot(bg, xdt[:, sl] * dec[:, sl], TN)
            h_scr[:, sl] = hg * eacs[Q - 1:Q, sl] + st
            for j in range(4):
                lo = g * GW + j * 128
                xb = xdt[:, lo:lo + 128]
                yp = yoff[:, j * 128:(j + 1) * 128] + dsk_ref[:, lo:lo + 128] * xs[:, lo:lo + 128]
                for e in range(2):
                    h = g * 8 + j * 2 + e
                    lm = jnp.exp(jnp.where(causal, acs[:, h:h + 1] - acs_t[h:h + 1, :], NEG))
                    xm = jnp.where(first if e == 0 else jnp.logical_not(first), xb, 0.0)
                    yp = yp + _dot(s * lm, xm, NN)
                y_ref[:, lo:lo + 128] = yp.astype(y_ref.dtype)

    const = lambda c: (0, 0)
    return pl.pallas_call(
        body, name=name, grid=(nc,),
        in_specs=[pl.BlockSpec((Q, XBC), lambda c: (c, 0)),
                  pl.BlockSpec((8, XBC), lambda c: (jnp.maximum(c * (Q // 8) - 1, 0), 0)),
                  pl.BlockSpec((4, XBC), const), pl.BlockSpec((1, XBC), const),
                  pl.BlockSpec((Q, DT_PAD), lambda c: (c, 0)),
                  pl.BlockSpec((1, 128), const), pl.BlockSpec((1, DI), const),
                  pl.BlockSpec((Q, Q), const), pl.BlockSpec((256, DI), const)],
        out_specs=[pl.BlockSpec((Q, DI), lambda c: (c, 0)), pl.BlockSpec((1, NS, DI), lambda c: (c, 0, 0)),
                   pl.BlockSpec((Q, XBC), lambda c: (c, 0))],
        out_shape=[jax.ShapeDtypeStruct((L, DI), BF16), jax.ShapeDtypeStruct((nc, NS, DI), F32),
                   jax.ShapeDtypeStruct((L, XBC), F32)],
        scratch_shapes=[pltpu.VMEM((NS, DI), F32), pltpu.VMEM((8 + Q, XBC), F32)],
        compiler_params=_params(("arbitrary",), VMEM_BIG))(raw, raw, cw, cb, dtp, arow, dsk_x, tri, expand)


def _ssd_bwd(dy, xbc, dtp, hs, arow, dsk_x, dproj, *, name):
    L = xbc.shape[0]
    nc = L // Q
    tri, triu, expand = _ssd_consts()

    def body(dy_ref, xbc_ref, dtw_ref, hs_ref, arow_ref, dsk_ref, t_ref, u_ref, e_ref, dp_in,
             dxbc_ref, ddtw_ref, da_ref, ddx_ref, ddtb_ref, dh_scr):
        del dp_in
        i = pl.program_id(0)

        @pl.when(i == 0)
        def _():
            dh_scr[...] = jnp.zeros_like(dh_scr)

        sig, dt, acs, acs_t, acs_x, dt_x, xs = _ssd_common(xbc_ref, dtw_ref, arow_ref, t_ref, e_ref)
        dyv = dy_ref[...]
        xdt = xs * dt_x
        eacs = jnp.exp(acs_x)
        acs_last = acs_x[Q - 1:Q, :]
        dec = jnp.exp(acs_last - acs_x)
        gy = dyv * eacs
        causal = lax.broadcasted_iota(jnp.int32, (Q, Q), 0) >= lax.broadcasted_iota(jnp.int32, (Q, Q), 1)
        first = lax.broadcasted_iota(jnp.int32, (Q, 128), 1) < HP
        lane_h = lax.broadcasted_iota(jnp.int32, (Q, 128), 1)
        sub_h = lax.broadcasted_iota(jnp.int32, (128, Q), 0)
        last_row = lax.broadcasted_iota(jnp.int32, (Q, GW), 0) == Q - 1
        dacs = jnp.zeros((Q, 128), F32)
        dacs_t = jnp.zeros((128, Q), F32)
        ddt = jnp.zeros((Q, 128), F32)
        for g in range(NG):
            bg = xbc_ref[:, DI + g * NS:DI + (g + 1) * NS]
            cg = xbc_ref[:, DI + NG * NS + g * NS:DI + NG * NS + (g + 1) * NS]
            s = _dot(cg, bg, NT)
            sl = slice(g * GW, (g + 1) * GW)
            hg = hs_ref[0, :, sl].astype(F32)
            dhn = dh_scr[:, sl]
            eal = eacs[Q - 1:Q, sl]
            gg = gy[:, sl]
            dax = gg * _dot(cg, hg, NN)
            dcg = _dot(gg, hg, NT)
            dh_scr[:, sl] = _dot(cg, gg, TN) + dhn * eal
            dal = eal * _colsum(dhn * hg)
            xdd = xdt[:, sl] * dec[:, sl]
            dbg = _dot(xdd, dhn, NT)
            wv = _dot(bg, dhn, NN)
            dd = wv * xdd
            dax = dax - dd
            dal = dal + _colsum(dd)
            dax = dax + jnp.where(last_row, dal, 0.0)
            dxdt_g = wv * dec[:, sl]
            ds = jnp.zeros((Q, Q), F32)
            dxdt_blocks = []
            for j in range(4):
                lo = g * GW + j * 128
                xb = xdt[:, lo:lo + 128]
                dyb = dyv[:, lo:lo + 128]
                dxb = dxdt_g[:, j * 128:(j + 1) * 128]
                for e in range(2):
                    h = g * 8 + j * 2 + e
                    lm = jnp.exp(jnp.where(causal, acs[:, h:h + 1] - acs_t[h:h + 1, :], NEG))
                    m = s * lm
                    dym = jnp.where(first if e == 0 else jnp.logical_not(first), dyb, 0.0)
                    dm = _dot(dym, xb, NT)
                    r = dm * m
                    dacs = dacs + jnp.where(lane_h == h, jnp.sum(r, axis=1, keepdims=True), 0.0)
                    dacs_t = dacs_t + jnp.where(sub_h == h, _colsum(r), 0.0)
                    ds = ds + dm * lm
                    dxb = dxb + _dot(m, dym, TN)
                dxdt_blocks.append(dxb)
            dxdt = jnp.concatenate(dxdt_blocks, axis=1)
            dcg = dcg + _dot(ds, bg, NN)
            dbg = dbg + _dot(ds, cg, TN)
            dxbc_ref[:, DI + g * NS:DI + (g + 1) * NS] = dbg
            dxbc_ref[:, DI + NG * NS + g * NS:DI + NG * NS + (g + 1) * NS] = dcg
            dxbc_ref[:, sl] = dsk_ref[:, sl] * dyv[:, sl] + dxdt * dt_x[:, sl]
            ddt_g, dacs_g = _reduce_heads([dxdt * xs[:, sl], dax], e_ref[0:128, sl])
            ddt = ddt + ddt_g
            dacs = dacs + dacs_g
        dacs = dacs - dacs_t.T
        ddta = _tri_sum(u_ref, dacs)
        ddt = ddt + ddta * arow_ref[...]
        ddtw = jnp.where(lane_h < NH, ddt * sig, 0.0)
        ddtw_ref[...] = jnp.concatenate([ddtw, jnp.zeros((Q, DT_PAD - 128), F32)], axis=1).astype(ddtw_ref.dtype)
        _acc_out(da_ref, _colsum(ddta * dt), i)
        _acc_out(ddx_ref, _colsum(dyv * xs), i)
        _acc_out(ddtb_ref, _colsum(ddtw), i)

    rev = lambda c: (nc - 1 - c, 0)
    const = lambda c: (0, 0)
    return pl.pallas_call(
        body, name=name, grid=(nc,),
        in_specs=[pl.BlockSpec((Q, DI), rev), pl.BlockSpec((Q, XBC), rev),
                  pl.BlockSpec((Q, DT_PAD), rev),
                  pl.BlockSpec((1, NS, DI), lambda c: (nc - 1 - c, 0, 0)),
                  pl.BlockSpec((1, 128), const), pl.BlockSpec((1, DI), const),
                  pl.BlockSpec((Q, Q), const), pl.BlockSpec((Q, Q), const), pl.BlockSpec((256, DI), const),
                  pl.BlockSpec(memory_space=pl.ANY)],
        out_specs=[pl.BlockSpec((Q, XBC), rev),
                   pl.BlockSpec((Q, DT_PAD), lambda c: (nc - 1 - c, C_DT // DT_PAD)),
                   pl.BlockSpec((1, 128), const), pl.BlockSpec((1, DI), const), pl.BlockSpec((1, 128), const)],
        out_shape=[jax.ShapeDtypeStruct((L, XBC), F32), jax.ShapeDtypeStruct((L, NPROJ), BF16),
                   jax.ShapeDtypeStruct((1, 128), F32), jax.ShapeDtypeStruct((1, DI), F32),
                   jax.ShapeDtypeStruct((1, 128), F32)],
        scratch_shapes=[pltpu.VMEM((NS, DI), F32)],
        input_output_aliases={9: 1},
        compiler_params=_params(("arbitrary",), VMEM_BIG))(dy, xbc, dtp, hs, arow, dsk_x, tri, triu,
                                                          expand, dproj)


def _adam_update(wv, gv, mv, vv):
    c1 = 1.0 - ADAM_B1 ** ADAM_STEP
    c2 = 1.0 - ADAM_B2 ** ADAM_STEP
    mn = ADAM_B1 * mv + (1.0 - ADAM_B1) * gv
    vn = ADAM_B2 * vv + (1.0 - ADAM_B2) * (gv * gv)
    return -ADAM_LR * ((mn / c1) / (jnp.sqrt(vn / c2) + ADAM_EPS) + ADAM_WD * wv), mn, vn


def _adamw(w, g, m, v, *, name, tr=None):
    R = w.shape[0]
    rest = tuple(w.shape[1:])
    if tr is None:
        tr = _pick(R, (256, 128, 64, 32, 16, 8))
    assert R % tr == 0

    def body(w_ref, g_ref, m_ref, v_ref, d_ref, mo_ref, vo_ref):
        d_ref[...], mo_ref[...], vo_ref[...] = _adam_update(w_ref[...], g_ref[...], m_ref[...], v_ref[...])

    zeros = (0,) * len(rest)
    spec = pl.BlockSpec((tr,) + rest, lambda i: (i,) + zeros)
    return pl.pallas_call(body, name=name, grid=(R // tr,), in_specs=[spec] * 4, out_specs=[spec] * 3,
                          out_shape=[jax.ShapeDtypeStruct(w.shape, F32)] * 3,
                          compiler_params=_params(("parallel",)))(w, g, m, v)


def _adamw_small(svrow, g_conv, params, *, name):
    n = len(params)

    def body(*refs):
        sv_ref, gc_ref = refs[0], refs[1]
        ins, outs = refs[2:2 + 3 * n], refs[2 + 3 * n:]
        for p, (key, w, _, _) in enumerate(params):
            w_ref, m_ref, v_ref = ins[3 * p:3 * p + 3]
            g_ref, d_ref, mo_ref, vo_ref = outs[4 * p:4 * p + 4]
            gv = gc_ref[...] if key == "conv_w" else sv_ref[:, SV_OFF[key]:SV_OFF[key] + w.shape[1]]
            g_ref[...] = gv
            d_ref[...], mo_ref[...], vo_ref[...] = _adam_update(w_ref[...], gv, m_ref[...], v_ref[...])

    vm = pl.BlockSpec(memory_space=pltpu.VMEM)
    args = [svrow, g_conv]
    shapes = []
    for _, w, m, v in params:
        args += [w, m, v]
        shapes += [jax.ShapeDtypeStruct(w.shape, F32)] * 4
    res = pl.pallas_call(body, name=name, in_specs=[vm] * len(args), out_specs=[vm] * len(shapes),
                         out_shape=shapes)(*args)
    return {key: tuple(res[4 * p:4 * p + 4]) for p, (key, _, _, _) in enumerate(params)}


def _slab_sum(recv, *, tile, name):
    rows = recv.shape[1]
    assert rows % tile == 0 and tile % 16 == 0

    def body(r_ref, o_ref):
        acc = r_ref[0].astype(F32)
        for j in range(1, N_DEV):
            acc = acc + r_ref[j].astype(F32)
        o_ref[...] = acc

    return pl.pallas_call(body, name=name, grid=(rows // tile,),
                          in_specs=[pl.BlockSpec((N_DEV, tile, D), lambda i: (0, i, 0))],
                          out_specs=pl.BlockSpec((tile, D), lambda i: (i, 0)),
                          out_shape=jax.ShapeDtypeStruct((rows, D), F32),
                          compiler_params=_params(("parallel",)))(recv)


MESH = pl.DeviceIdType.MESH


def _coords():
    return lax.axis_index("x"), lax.axis_index("y"), lax.axis_index("c")


def _peer(k):
    x, y, c = _coords()
    px = 1 - x if k & 4 else x
    py = 1 - y if k & 2 else y
    pc = 1 - c if k & 1 else c
    return (px, py, pc), 4 * px + 2 * py + pc


def _rcopy(src, dst, ssem, rsem, dev):
    return pltpu.make_async_remote_copy(src_ref=src, dst_ref=dst, send_sem=ssem, recv_sem=rsem,
                                        device_id=dev, device_id_type=MESH)


def _exchange_all(src_of, dst_slot, send_sems, recv_sems):
    x, y, c = _coords()
    me = 4 * x + 2 * y + c
    sent = []
    for k in range(1, N_DEV):
        dev, pidx = _peer(k)
        cp = _rcopy(src_of(pidx), dst_slot(me), send_sems.at[k - 1], recv_sems.at[k - 1], dev)
        cp.start()
        sent.append(cp)
    for k in range(1, N_DEV):
        dev, pidx = _peer(k)
        _rcopy(src_of(pidx), dst_slot(pidx), send_sems.at[k - 1], recv_sems.at[k - 1], dev).wait_recv()
    for cp in sent:
        cp.wait_send()


def _rows_of_slots(buf, nslots):
    rows = lax.broadcasted_iota(jnp.int32, (8, buf.shape[-1]), 0)
    out = jnp.zeros((8, buf.shape[-1]), F32)
    for j in range(nslots):
        out = out + jnp.where(rows == j, buf[j], 0.0)
    return out


def _exchange_start(src_of, dst_slot, send_sems, recv_sems):
    x, y, c = _coords()
    me = 4 * x + 2 * y + c
    sent = []
    for k in range(1, N_DEV):
        dev, pidx = _peer(k)
        cp = _rcopy(src_of(pidx), dst_slot(me), send_sems.at[k - 1], recv_sems.at[k - 1], dev)
        cp.start()
        sent.append(cp)
    return sent


def _exchange_finish(sent, src_of, dst_slot, send_sems, recv_sems):
    for k in range(1, N_DEV):
        dev, pidx = _peer(k)
        _rcopy(src_of(pidx), dst_slot(pidx), send_sems.at[k - 1], recv_sems.at[k - 1], dev).wait_recv()
    for cp in sent:
        cp.wait_send()


def _ada_gather(c, w_ada, b_r, slab, *, name):
    wloc = w_ada.shape[1]

    def body(c_ref, w_ref, b_ref, x_ref, mod_ref, call_ref, out_ref,
             csrc, cbuf, psrc, pbuf, s1, r1, s2, r2, send_sems, recv_sems, local_sem):
        x, y, cc = _coords()
        me_i = 4 * x + 2 * y + cc
        me, sibling = (x, y, cc), (x, y, 1 - cc)
        chips = [(1 - x, y), (x, 1 - y), (1 - x, 1 - y)]

        def slot(px, py, pc):
            return out_ref.at[4 * px + 2 * py + pc]

        def copy(k, block, to, src=None):
            return _rcopy(slot(*block) if src is None else src, slot(*block), send_sems.at[k], recv_sems.at[k], to)

        csrc[...] = jnp.broadcast_to(c_ref[...], (8, D))
        cbuf[me_i] = csrc[...]
        c_of, c_slot = (lambda p: csrc), (lambda s: cbuf.at[s])
        sent1 = _exchange_start(c_of, c_slot, s1, r1)

        mine = pltpu.make_async_copy(x_ref, slot(*me), local_sem)
        mine.start()
        first = [copy(0, me, sibling, src=x_ref)]
        first += [copy(1 + j, me, (*chip, cc), src=x_ref) for j, chip in enumerate(chips)]
        for cp in first:
            cp.start()

        _exchange_finish(sent1, c_of, c_slot, s1, r1)
        call = _rows_of_slots(cbuf, N_DEV)
        call_ref[...] = call
        prod = _dot_hi(_silu(call), w_ref[...])
        for b in range(N_DEV):
            psrc[b] = jnp.broadcast_to(prod[b:b + 1, :], (8, wloc))
        pbuf[me_i] = psrc[me_i]
        p_of, p_slot = (lambda p: psrc.at[p]), (lambda s: pbuf.at[s])
        sent2 = _exchange_start(p_of, p_slot, s2, r2)

        passed = [copy(4 + j, (*chip, cc), sibling) for j, chip in enumerate(chips)]
        for j, chip in enumerate(chips):
            copy(1 + j, (*chip, cc), me).wait_recv()
            passed[j].start()
        copy(0, sibling, me).wait_recv()
        for j, chip in enumerate(chips):
            copy(4 + j, (*chip, 1 - cc), me).wait_recv()

        _exchange_finish(sent2, p_of, p_slot, s2, r2)
        mod_ref[...] = _rows_of_slots(pbuf, N_DEV) + b_ref[...]
        for cp in first + passed:
            cp.wait_send()
        mine.wait()

    vm = pl.BlockSpec(memory_space=pltpu.VMEM)
    anyspec = pl.BlockSpec(memory_space=pl.ANY)
    return pl.pallas_call(
        body, name=name, in_specs=[vm, vm, vm, anyspec], out_specs=[vm, vm, anyspec],
        out_shape=[jax.ShapeDtypeStruct((N_DEV, wloc), F32), jax.ShapeDtypeStruct((N_DEV, D), F32),
                   jax.ShapeDtypeStruct((N_DEV,) + slab.shape, slab.dtype)],
        scratch_shapes=[pltpu.VMEM((8, D), F32), pltpu.VMEM((N_DEV, 8, D), F32),
                        pltpu.VMEM((N_DEV, 8, wloc), F32), pltpu.VMEM((N_DEV, 8, wloc), F32),
                        pltpu.SemaphoreType.DMA((N_DEV - 1,)), pltpu.SemaphoreType.DMA((N_DEV - 1,)),
                        pltpu.SemaphoreType.DMA((N_DEV - 1,)), pltpu.SemaphoreType.DMA((N_DEV - 1,)),
                        pltpu.SemaphoreType.DMA((7,)), pltpu.SemaphoreType.DMA((7,)), pltpu.SemaphoreType.DMA],
        compiler_params=pltpu.CompilerParams(vmem_limit_bytes=VMEM_BIG))(c, w_ada, b_r, slab)


_HBM =pl.BlockSpec(memory_space=pltpu.HBM)
_SEM = pl.BlockSpec(memory_space=pltpu.SEMAPHORE)
_EFFECT = pltpu.SideEffectType.DATAFLOW_SIDE_EFFECTING


def _xchg_src(src_ref, pidx, per_peer):
    return src_ref.at[pidx] if per_peer else src_ref


def _xchg_start(src, *, per_peer, name):
    rows = src.shape[-2]
    land_shape = (N_DEV, rows, D)

    def body(src_ref, land_ref, send_sems, recv_sems, src_thru, land_thru, token):
        del src_thru, land_thru
        x, y, c = _coords()
        me = 4 * x + 2 * y + c
        for k in range(1, N_DEV):
            dev, pidx = _peer(k)
            _rcopy(_xchg_src(src_ref, pidx, per_peer), land_ref.at[me], send_sems.at[k - 1],
                   recv_sems.at[k - 1], dev).start()
        token[...] = jnp.zeros_like(token)

    return pl.pallas_call(
        body, name=name,
        out_shape=(pltpu.SemaphoreType.DMA((N_DEV - 1,)), pltpu.SemaphoreType.DMA((N_DEV - 1,)),
                   pltpu.HBM(src.shape, src.dtype), pltpu.HBM(land_shape, src.dtype),
                   jax.ShapeDtypeStruct((8, 128), F32)),
        in_specs=(_HBM, _HBM),
        out_specs=(_SEM, _SEM, _HBM, _HBM, pl.BlockSpec(memory_space=pltpu.VMEM)),
        input_output_aliases={0: 2, 1: 3},
        compiler_params=pltpu.CompilerParams(has_side_effects=_EFFECT),
    )(pltpu.with_memory_space_constraint(src, pltpu.HBM),
      pltpu.with_memory_space_constraint(lax.empty(land_shape, src.dtype), pltpu.HBM))


def _xchg_wait(started, after, *, per_peer, name):
    send_sems, recv_sems, src_thru, land_thru, _ = started

    def body(src_ref, land_ref, send_sems, recv_sems, after_ref, src_dead, got_ref):
        del after_ref, src_dead, got_ref
        for k in range(1, N_DEV):
            dev, pidx = _peer(k)
            cp = _rcopy(_xchg_src(src_ref, pidx, per_peer), land_ref.at[pidx], send_sems.at[k - 1],
                        recv_sems.at[k - 1], dev)
            cp.wait_send()
            cp.wait_recv()

    return pl.pallas_call(
        body, name=name,
        out_shape=(pltpu.HBM(src_thru.shape, src_thru.dtype), pltpu.HBM(land_thru.shape, land_thru.dtype)),
        in_specs=(_HBM, _HBM, _SEM, _SEM, pl.BlockSpec(memory_space=pl.ANY)),
        out_specs=(_HBM, _HBM),
        input_output_aliases={0: 0, 1: 1},
        compiler_params=pltpu.CompilerParams(has_side_effects=_EFFECT),
    )(src_thru, land_thru, send_sems, recv_sems, after)


def _dep(token):
    return (token, (8, 128), lambda i, j, k: (0, 0))


def _small_allsum(sv, *, name):
    def body(sv_ref, all_ref, sum_ref, send_sems, recv_sems):
        x, y, c = _coords()
        me = 4 * x + 2 * y + c
        all_ref[me] = sv_ref[...]
        _exchange_all(lambda p: sv_ref, lambda s: all_ref.at[s], send_sems, recv_sems)
        acc = all_ref[0]
        for j in range(1, N_DEV):
            acc = acc + all_ref[j]
        sum_ref[...] = acc

    vm = pl.BlockSpec(memory_space=pltpu.VMEM)
    return pl.pallas_call(
        body, name=name, in_specs=[vm], out_specs=[vm, vm],
        out_shape=[jax.ShapeDtypeStruct((N_DEV, SV_ROWS, 128), F32), jax.ShapeDtypeStruct((SV_ROWS, 128), F32)],
        scratch_shapes=[pltpu.SemaphoreType.DMA((7,)), pltpu.SemaphoreType.DMA((7,))],
    )(sv)


def _ada_bwd(call, dmod_loc, *, name):
    wloc = dmod_loc.shape[1]

    def body(c_ref, d_ref, o_ref):
        o_ref[...] = _dot_hi(_silu(c_ref[...]), d_ref[...], TN)

    vm = pl.BlockSpec(memory_space=pltpu.VMEM)
    return pl.pallas_call(body, name=name, in_specs=[vm, vm], out_specs=vm,
                          out_shape=jax.ShapeDtypeStruct((D, wloc), F32),
                          compiler_params=pltpu.CompilerParams(vmem_limit_bytes=VMEM_BIG))(call, dmod_loc)


def _pad_rows(a, rows):
    return jnp.pad(a, ((0, rows - a.shape[0]), (0, 0)))


IN_SHIFT = tuple((IN_ROWS * j) % 16 for j in range(N_DEV))
IN_BASE = tuple(IN_ROWS * j - IN_SHIFT[j] for j in range(N_DEV))
IN_SEGMENTS = ((2048, XBC, C_XBC), (5152, 1024, C_POOL), (0, 2048, C_Z), (6176, 2048, C_GATE), (5120, 32, C_DT))


def _global_pieces(gs):
    pieces = []
    for j in range(N_DEV):
        lo, hi = 0, IN_ROWS_P
        if j > 0 and IN_BASE[j - 1] + IN_ROWS_P > IN_BASE[j]:
            pieces.append((IN_BASE[j], 16, gs[j - 1, IN_ROWS_P - 16:IN_ROWS_P] + gs[j, 0:16]))
            lo = 16
        if j + 1 < N_DEV and IN_BASE[j] + IN_ROWS_P > IN_BASE[j + 1]:
            hi = IN_ROWS_P - 16
        pieces.append((IN_BASE[j] + lo, hi - lo, gs[j, lo:hi]))
    return pieces


def _reorder_in_rows(gs):
    pieces = _global_pieces(gs)
    parts = []
    for lo, n, _ in IN_SEGMENTS:
        for p0, pn, arr in pieces:
            a, b = max(lo, p0), min(lo + n, p0 + pn)
            if a < b:
                parts.append(arr[a - p0:b - p0])
    parts.append(jnp.zeros((DT_PAD - 32, D), gs.dtype))
    return jnp.concatenate(parts, axis=0)


def _restore_in_shards(d):
    slabs = []
    for j in range(N_DEV):
        parts = []
        r, end = IN_BASE[j], IN_BASE[j] + IN_ROWS_P
        while r < end:
            lo, n, new = next(s for s in IN_SEGMENTS if s[0] <= r < s[0] + s[1])
            e = min(end, lo + n)
            parts.append(d[new + r - lo:new + e - lo])
            r = e
        slabs.append(jnp.concatenate(parts, axis=0))
    return jnp.stack(slabs, axis=0)


def _pack_sv(parts):
    flat = []
    for n, size in SV_PARTS:
        v = parts[n].reshape(-1).astype(F32)
        flat.append(jnp.pad(v, (0, size - v.shape[0])))
    v = jnp.concatenate(flat)
    return jnp.pad(v, (0, SV_ROWS * 128 - v.shape[0])).reshape(SV_ROWS, 128)


def _sv_get(flat, n, size):
    return flat[SV_OFF[n]:SV_OFF[n] + size]


def kernel(x, c, w_ada, b_ada, norm_mix_w, w_in, conv_w, conv_b, dt_bias, a_log, d_skip, ssd_norm_w, w_branch_ssd, pool_w, pool_scale, w_branch_pool, w_out, norm_mlp_w, w_up, w_down, norm_final_w, loss_target, m_w_ada, m_b_ada, m_norm_mix_w, m_w_in, m_conv_w, m_conv_b, m_dt_bias, m_a_log, m_d_skip, m_ssd_norm_w, m_w_branch_ssd, m_pool_w, m_pool_scale, m_w_branch_pool, m_w_out, m_norm_mlp_w, m_w_up, m_w_down, m_norm_final_w, v_w_ada, v_b_ada, v_norm_mix_w, v_w_in, v_conv_w, v_conv_b, v_dt_bias, v_a_log, v_d_skip, v_ssd_norm_w, v_w_branch_ssd, v_pool_w, v_pool_scale, v_w_branch_pool, v_w_out, v_norm_mlp_w, v_w_up, v_w_down, v_norm_final_w):
    xs_ = x[0]
    tgt = loss_target[0]
    L = xs_.shape[0]
    me = 4 * lax.axis_index("x") + 2 * lax.axis_index("y") + lax.axis_index("c")
    wloc = w_ada.shape[2]

    conv_bits = lax.bitcast_convert_type(conv_w[0], SLAB_DT).reshape(3, D)
    in_shift = (IN_ROWS * me) % 16
    slab_in = lax.dynamic_update_slice(jnp.zeros((IN_ROWS_P, D), SLAB_DT), w_in[0].T.astype(SLAB_DT),
                                       (in_shift, 0))
    slab_in = jnp.concatenate([slab_in, _pad_rows(conv_bits, CONV_ROWS)], axis=0)
    slab_rest = jnp.concatenate([
        w_branch_ssd[0].astype(SLAB_DT),
        pool_w[0].reshape(32, D).astype(SLAB_DT),
        w_branch_pool[0].astype(SLAB_DT),
        w_out[0].astype(SLAB_DT),
        w_up[0].T.astype(SLAB_DT),
        w_down[0].astype(SLAB_DT)], axis=0)
    mod_p, c_all, gs_in = _ada_gather(c, w_ada[0], b_ada.reshape(N_DEV, wloc), slab_in,
                                      name="ada_gather_w_in")
    mod = mod_p.reshape(6, D)
    shift_m, scale_m, gate_m, shift_f, scale_f, gate_f = [mod[i:i + 1] for i in range(6)]
    slab_rest, gs_in = lax.optimization_barrier((slab_rest, gs_in))
    rest_started = _xchg_start(slab_rest, per_peer=False, name="gather_rest_start")
    gather_token = rest_started[4]

    w_in_t = _reorder_in_rows(gs_in)
    conv_full = lax.bitcast_convert_type(
        gs_in[:, IN_ROWS_P:IN_ROWS_P + 3].reshape(N_DEV, 4, XBC // N_DEV, 2), F32)
    conv_full = conv_full.transpose(1, 0, 2).reshape(4, XBC)

    dtb = jnp.pad(dt_bias, ((0, 0), (0, 128 - NH)))
    arow = jnp.pad(-jnp.exp(a_log), ((0, 0), (0, 128 - NH)))
    dsk_x = jnp.repeat(d_skip, HP, axis=1)

    tm = _pick(L, (1024, 512, 256, 128))
    tm2 = _pick(L, (2048, 1024, 512, 256, 128))
    tkl = _pick(L, (4096, 2048, 1024, 512, 256, 128))
    tkl2 = _pick(L, (2048, 1024, 512, 256, 128))

    tmh = _pick(L, (512, 256, 128))
    tmq = _pick(L, (256, 128))
    zcol = C_Z // DI
    gcol = C_GATE // (2 * D)

    def whole_rows(w):
        return lambda t: ((L, w), BF16, (t, w), lambda i, j, k: (i, 0))

    def norm1_pro(x_ref, ex, outs, j):
        @pl.when(j == 0)
        def _():
            xv = x_ref[...]
            r = lax.rsqrt(jnp.mean(xv * xv, axis=-1, keepdims=True) + EPS)
            outs[1][...] = (xv * r * ex[0][...] * (1.0 + ex[1][...]) + ex[2][...]).astype(outs[1].dtype)

        return outs[1][...]

    def proj_ep(acc, ex, outs):
        outs[0][...] = acc

        @pl.when(pl.program_id(1) == NPROJ // 768 - 1)
        def _():
            pre = acc[:, 768 - DT_PAD:768 - DT_PAD + 128] + ex[3][...]
            outs[2][...] = jnp.concatenate([_softplus(pre), _sigmoid(pre)], axis=1)

    proj, h1, dtp = _mm(
        xs_, w_in_t, "nt", name="in_proj", tm=tm2, tn=768, tk=D,
        extras=[(norm_mix_w, *_vecs()), (scale_m, *_vecs()), (shift_m, *_vecs()), (dtb, *_vecs(128)),
                _dep(gather_token)],
        outs=[F32, whole_rows(D)(tm2), ((L, DT_PAD), F32, (tm2, DT_PAD), lambda i, j, k: (i, 0))],
        prologue=norm1_pro, epilogue=proj_ep)
    xbc_raw = proj
    y_ssm, hs, xbc = _ssd_fwd(xbc_raw, dtp, conv_full, conv_b, arow, dsk_x, name="ssd_fwd")

    slab_rest, gs = _xchg_wait(rest_started, y_ssm, per_peer=False, name="gather_rest_wait")
    gs = lax.dynamic_update_slice(gs, slab_rest[None], (me, 0, 0))

    def part(n, rows):
        return gs[:, REST_OFF[n]:REST_OFF[n] + rows]

    w_bssd = part("bssd", 256).reshape(DI, D)
    w_pool = part("pool", 32).reshape(N_DEV, 4, 32, PGW).transpose(1, 0, 2, 3).reshape(POOL_W, PGW)
    w_bpool = part("bpool", 128).reshape(POOL_W, D)
    w_o = part("out", 128).reshape(D, D)
    w_up_t = part("up", 512).reshape(DFF, D)
    w_dn = part("down", 512).reshape(DFF, D)

    def gnorm_pro(y_ref, ex, outs, j):
        z_ref, w_ref = ex
        yg = y_ref[...].astype(F32) * _silu(z_ref[...].astype(F32))
        segs = []
        for k in range(NG):
            sl = slice(k * GW, (k + 1) * GW)
            seg = yg[:, sl]
            r = lax.rsqrt(jnp.mean(seg * seg, axis=-1, keepdims=True) + EPS)
            segs.append((seg * r * w_ref[:, sl]).astype(BF16))
        yn_v = jnp.concatenate(segs, axis=1)
        outs[1][...] = yn_v
        return yn_v

    y_ssd, yn = _mm(y_ssm, w_bssd, "nn", name="branch_ssd", tm=tmh, tn=D, tk=DI,
                    extras=[(proj, *_rows(tmh, DI, zcol)), (ssd_norm_w, *_vecs(DI))],
                    outs=[BF16, whole_rows(DI)(tmh)], prologue=gnorm_pro)
    pooled = _pool_fwd(proj, name="pool_fwd")
    wp_spec = ((POOL_W, PGW), lambda i, j, k: (0, 0))

    def pool_pro(a_ref, ex, outs, j):
        wp_ref, s_ref = ex
        segs = []
        for g in range(4):
            sl = slice(g * PGW, (g + 1) * PGW)
            p = _dot(a_ref[:, sl], wp_ref[sl, :], NN)
            outs[1][:, sl] = p.astype(BF16)
            segs.append((p * s_ref[:, sl]).astype(BF16))
        yp1_v = jnp.concatenate(segs, axis=1)
        outs[2][...] = yp1_v
        return yp1_v

    y_pool, yp0, yp1 = _mm(pooled, w_bpool, "nn", name="branch_pool", tm=tm, tn=D, tk=D,
                           extras=[(w_pool, *wp_spec), (pool_scale, *_vecs())],
                           outs=[BF16, whole_rows(D)(tm), whole_rows(D)(tm)], prologue=pool_pro)

    def merge_pro(a_ref, ex, outs, j):
        s = _sigmoid(ex[1][...].astype(F32))
        mv = (s[:, :D] * a_ref[...].astype(F32) + s[:, D:] * ex[0][...].astype(F32)).astype(BF16)
        outs[3][...] = mv
        return mv

    mix, x1, h2, m = _mm(y_ssd, w_o, "nn", name="out_proj", tm=tmh, tn=D, tk=D,
                         extras=[(y_pool, *_rows(tmh)), (proj, *_rows(tmh, 2 * D, gcol)),
                                 (xs_, *_rows(tmh)), (gate_m, *_vecs()), (norm_mlp_w, *_vecs()),
                                 (scale_f, *_vecs()), (shift_f, *_vecs())],
                         outs=[BF16, F32, BF16, whole_rows(D)(tmh)], prologue=merge_pro,
                         epilogue=lambda acc, ex, outs: _ep_resid_norm(acc, ex[2:], outs[:3]))

    def relu2(acc, ex, outs):
        r = jnp.maximum(acc, 0.0)
        outs[0][...] = acc.astype(BF16)
        outs[1][...] = (r * r).astype(BF16)

    up, act = _mm(h2, w_up_t, "nt", name="mlp_up", outs=[BF16, BF16], tm=tm2, tn=1024, tk=D, epilogue=relu2)

    dx2, ddown, loss_p, dnwf, dgate_f = _mm(
        act, w_dn, "nn", name="mlp_down", tm=tmh, tn=D, tk=DFF,
        extras=[(x1, *_rows(tmh)), (tgt, *_rows(tmh)), (gate_f, *_vecs()), (norm_final_w.reshape(1, D), *_vecs())],
        outs=[F32, BF16, _sum_out(128), _sum_out(), _sum_out()], epilogue=_ep_final)

    def drelu2(acc, ex, outs):
        outs[0][...] = (acc * (2.0 * jnp.maximum(ex[0][...].astype(F32), 0.0))).astype(BF16)

    def dep_last(ep):
        return lambda acc, ex, outs: ep(acc, ex[:-1], outs)

    dup = _mm(ddown, w_dn, "nt", name="mlp_down_dx", outs=[BF16], tm=tm2, tn=1024, tk=D,
              extras=[(up, (tm2, 1024), lambda i, j, k: (i, j))], epilogue=drelu2)
    g_dn = _mm(act, ddown, "tn", name="mlp_down_dw", outs=[SLAB_DT], tm=1024, tn=D, tk=tkl)
    g_up_t = _mm(dup, h2, "tn", name="mlp_up_dw", outs=[SLAB_DT], tm=1024, tn=D, tk=tkl)
    gslab_mlp = jnp.concatenate([g_up_t.reshape(N_DEV, 512, D), g_dn.reshape(N_DEV, 512, D)], axis=1)
    mlp_started = _xchg_start(gslab_mlp, per_peer=True, name="scatter_mlp_start")
    dx1, p2, q2, dmix, dgate_m = _mm(
        dup, w_up_t, "nn", name="mlp_up_dx", tm=tmh, tn=D, tk=DFF,
        extras=[(x1, *_rows(tmh)), (dx2, *_rows(tmh)), (norm_mlp_w, *_vecs()), (scale_f, *_vecs()),
                (mix, *_rows(tmh)), (gate_m, *_vecs()), _dep(mlp_started[4])],
        outs=[F32, _sum_out(), _sum_out(), BF16, _sum_out()], epilogue=dep_last(_ep_norm_bwd))
    gcol = C_GATE // (2 * D)
    dy_ssd, dy_pool, dproj = _mm(
        dmix, w_o, "nt", name="out_proj_dx", tm=tmh, tn=D, tk=D,
        extras=[(y_ssd, *_rows(tmh)), (y_pool, *_rows(tmh)), (proj, *_rows(tmh, 2 * D, gcol))],
        outs=[BF16, BF16, ((L, NPROJ), BF16, *_rows(tmh, 2 * D, gcol))], epilogue=_ep_merge_bwd)
    g_o = _mm(m, dmix, "tn", name="out_proj_dw", outs=[SLAB_DT], tm=D, tn=D, tk=tkl)
    zcol = C_Z // DI
    dy_ssm, dproj, d_snw = _mm(
        dy_ssd, w_bssd, "nt", name="branch_ssd_dx", tm=tmh, tn=DI, tk=D,
        extras=[(y_ssm, *_rows(tmh, DI)), (proj, *_rows(tmh, DI, zcol)), (ssd_norm_w, *_vecs(DI)),
                (dproj, None, None)],
        outs=[F32, ((L, NPROJ), BF16, *_rows(tmh, DI, zcol)), _sum_out(DI)],
        epilogue=_ep_gated_norm_bwd, aliases={3: 1})
    g_bssd = _mm(yn, dy_ssd, "tn", name="branch_ssd_dw", outs=[SLAB_DT], tm=1024, tn=D, tk=tkl)
    dxbc, dproj, d_a, d_dx, d_dtb = _ssd_bwd(dy_ssm, xbc, dtp, hs, arow, dsk_x, dproj, name="ssd_bwd")
    dproj, d_cw, d_cb = _conv_bwd(xbc_raw, dxbc, conv_full, conv_b, dproj, name="conv_bwd")
    def pool_bwd_ep(acc, ex, outs):
        y_ref, s_ref, wp_ref = ex
        o_ref, ds_ref, dpool_ref = outs
        dyp0_v = (acc * s_ref[...]).astype(BF16)
        o_ref[...] = dyp0_v
        _acc_out(ds_ref, _colsum(acc * y_ref[...].astype(F32)), _row_step())
        for g in range(4):
            sl = slice(g * PGW, (g + 1) * PGW)
            dpool_ref[:, sl] = _dot(dyp0_v[:, sl], wp_ref[sl, :], NT)

    dyp0, d_ps, dpooled = _mm(dy_pool, w_bpool, "nt", name="branch_pool_dx", tm=tm, tn=D, tk=D,
                              extras=[(yp0, *_rows(tm)), (pool_scale, *_vecs()), (w_pool, *wp_spec)],
                              outs=[BF16, _sum_out(), F32], epilogue=pool_bwd_ep)
    g_bpool = _mm(yp1, dy_pool, "tn", name="branch_pool_dw", outs=[SLAB_DT], tm=D, tn=D, tk=tkl)
    g_pool = _mm_pool_tn(pooled, dyp0, name="pool_mix_dw", tk=tkl)
    gslab_mix = jnp.concatenate([
        g_bssd.reshape(N_DEV, 256, D),
        g_pool.reshape(4, N_DEV, 32, PGW).transpose(1, 0, 2, 3).reshape(N_DEV, 32, D).astype(SLAB_DT),
        g_bpool.reshape(N_DEV, 128, D),
        g_o.reshape(N_DEV, 128, D)], axis=1)
    mix_started = _xchg_start(gslab_mix, per_peer=True, name="scatter_mix_start")
    dproj = _pool_bwd(dpooled, dproj, name="pool_bwd")
    g_in_t = _mm(dproj, h1, "tn", name="in_proj_dw", outs=[SLAB_DT], tm=tmq, tn=D, tk=L,
                 extras=[_dep(mix_started[4])])
    gslab_in = _restore_in_shards(g_in_t)
    in_started = _xchg_start(gslab_in, per_peer=True, name="scatter_in_start")
    grad_x, p1, q1 = _mm(
        dproj, w_in_t, "nn", name="in_proj_dx", tm=tmq, tn=D, tk=NPROJ,
        extras=[(xs_, *_rows(tmq)), (dx1, *_rows(tmq)), (norm_mix_w, *_vecs()), (scale_m, *_vecs()),
                _dep(in_started[4])],
        outs=[F32, _sum_out(), _sum_out()], epilogue=dep_last(_ep_norm_bwd))

    def landed(started, after, tile, name):
        src, land = _xchg_wait(started, after, per_peer=True, name=name + "_wait")
        own = lax.dynamic_slice_in_dim(src, me, 1, axis=0)
        return _slab_sum(lax.dynamic_update_slice(land, own, (me, 0, 0)), tile=tile, name=name + "_sum")

    gsum_mlp = landed(mlp_started, grad_x, 256, "scatter_mlp")
    gsum_mix = landed(mix_started, grad_x, 272, "scatter_mix")
    gsum_in = landed(in_started, grad_x, 208, "scatter_in")

    dmod = jnp.concatenate([q1, p1 * norm_mix_w, dgate_m, q2, p2 * norm_mlp_w, dgate_f], axis=1)
    d_alog = d_a[:, :NH] * (-jnp.exp(a_log))
    sv = _pack_sv({
        "b_ada": dmod, "norm_mix_w": p1 * (1.0 + scale_m), "conv_b": d_cb, "dt_bias": d_dtb[:, :NH],
        "a_log": d_alog, "d_skip": d_dx.reshape(NH, HP).sum(axis=1), "ssd_norm_w": d_snw,
        "pool_scale": d_ps, "norm_mlp_w": p2 * (1.0 + scale_f), "norm_final_w": dnwf, "conv_w": d_cw,
        "loss": loss_p[:, :1]})
    sv_all, sv_sum = _small_allsum(sv, name="small_allsum")
    flat = sv_sum.reshape(-1)
    loss = flat[SV_OFF["loss"]]
    dmod_all = sv_all.reshape(N_DEV, SV_ROWS * 128)[:, :6 * D]
    g_w_ada = _ada_bwd(c_all, lax.dynamic_slice_in_dim(dmod_all, me * wloc, wloc, axis=1), name="ada_bwd")

    g_conv_w = lax.dynamic_slice_in_dim(_sv_get(flat, "conv_w", 4 * XBC).reshape(4, XBC),
                                        me * (XBC // N_DEV), XBC // N_DEV, axis=1)
    small = [("b_ada", b_ada, m_b_ada, v_b_ada), ("norm_mix_w", norm_mix_w, m_norm_mix_w, v_norm_mix_w),
             ("conv_b", conv_b, m_conv_b, v_conv_b), ("dt_bias", dt_bias, m_dt_bias, v_dt_bias),
             ("a_log", a_log, m_a_log, v_a_log), ("d_skip", d_skip, m_d_skip, v_d_skip),
             ("ssd_norm_w", ssd_norm_w, m_ssd_norm_w, v_ssd_norm_w),
             ("pool_scale", pool_scale, m_pool_scale, v_pool_scale),
             ("norm_mlp_w", norm_mlp_w, m_norm_mlp_w, v_norm_mlp_w),
             ("norm_final_w", norm_final_w[None], m_norm_final_w[None], v_norm_final_w[None]),
             ("conv_w", conv_w[0], m_conv_w[0], v_conv_w[0])]
    small_out = _adamw_small(sv_sum.reshape(1, SV_ROWS * 128), g_conv_w, small, name="adamw_small")
    small_out["norm_final_w"] = tuple(a[0] for a in small_out["norm_final_w"])
    small_out["conv_w"] = tuple(a[None] for a in small_out["conv_w"])

    def gpart(n, rows_):
        return gsum_mix[MIX_OFF[n]:MIX_OFF[n] + rows_]

    def lin(a):
        return a[0].T.reshape(IN_ROWS * 8, 128)

    g_lin = lax.dynamic_slice_in_dim(gsum_in, in_shift, IN_ROWS, axis=0).reshape(IN_ROWS * 8, 128)
    dlt, mn, vn = _adamw(lin(w_in), g_lin, lin(m_w_in), lin(v_w_in), name="adamw_w_in", tr=IN_ROWS * 2)
    big_in = tuple(a.reshape(IN_ROWS, D).T[None] for a in (g_lin, dlt, mn, vn))

    big = {
        "w_ada": (w_ada, m_w_ada, v_w_ada, g_w_ada, (D, wloc)),
        "w_branch_ssd": (w_branch_ssd, m_w_branch_ssd, v_w_branch_ssd, gpart("bssd", 256), (256, D)),
        "pool_w": (pool_w, m_pool_w, v_pool_w, gpart("pool", 32).reshape(128, PGW), (128, PGW)),
        "w_branch_pool": (w_branch_pool, m_w_branch_pool, v_w_branch_pool, gpart("bpool", 128), (128, D)),
        "w_out": (w_out, m_w_out, v_w_out, gpart("out", 128), (128, D)),
        "w_up": (w_up, m_w_up, v_w_up, gsum_mlp[:512].T, (D, 512)),
        "w_down": (w_down, m_w_down, v_w_down, gsum_mlp[512:], (512, D)),
    }
    big_out = {}
    for n, (w, mm_, vv, g, shp2) in big.items():
        dlt, mn, vn = _adamw(w.reshape(shp2), g, mm_.reshape(shp2), vv.reshape(shp2), name="adamw_" + n)
        big_out[n] = (g.reshape(w.shape), dlt.reshape(w.shape), mn.reshape(w.shape), vn.reshape(w.shape))

    order = ["w_ada", "b_ada", "norm_mix_w", "w_in", "conv_w", "conv_b", "dt_bias", "a_log", "d_skip",
             "ssd_norm_w", "w_branch_ssd", "pool_w", "pool_scale", "w_branch_pool", "w_out", "norm_mlp_w",
             "w_up", "w_down", "norm_final_w"]
    big_out["w_in"] = big_in
    res = {**small_out, **big_out}
    outs = [loss, grad_x.reshape(x.shape)]
    for k in range(4):
        outs += [res[n][k] for n in order]
    return tuple(outs)
```

```python
import functools

import numpy as np
import jax
import jax.numpy as jnp
from jax import lax
from jax.experimental import pallas as pl
from jax.experimental.pallas import tpu as pltpu

F32 = jnp.float32
BF16 = jnp.bfloat16
SLAB_DT = jnp.bfloat16
_MXU_DTYPE = jnp.bfloat16

N_DEV = 8
D = 1024
DI = 2048
NH = 32
HP = 64
NG = 4
NS = 128
Q = 128
XBC = DI + 2 * NG * NS
DFF = 4096
N_IN = 8224
EPS = 1e-5
POOL_W = 1024
PGW = 256

C_XBC, C_POOL, C_Z, C_GATE, C_DT = 0, 3072, 4096, 6144, 8192
DT_PAD = 256
NPROJ = C_DT + DT_PAD

IN_ROWS = N_IN // N_DEV
IN_ROWS_P = 1040
CONV_ROWS = 16
REST_PARTS = (("bssd", 256), ("pool", 32), ("bpool", 128), ("out", 128), ("up", 512), ("down", 512))
REST_OFF = {}
_o = 0
for _n, _r in REST_PARTS:
    REST_OFF[_n] = _o
    _o += _r
REST_ROWS = _o
MIX_PARTS = (("bssd", 256), ("pool", 32), ("bpool", 128), ("out", 128))
MIX_OFF = {}
_o = 0
for _n, _r in MIX_PARTS:
    MIX_OFF[_n] = _o
    _o += _r
MIX_ROWS = _o

SV_PARTS = (("b_ada", 6144), ("norm_mix_w", 1024), ("conv_b", 3072), ("dt_bias", 128), ("a_log", 128),
            ("d_skip", 128), ("ssd_norm_w", 2048), ("pool_scale", 1024), ("norm_mlp_w", 1024),
            ("norm_final_w", 1024), ("conv_w", 4 * XBC), ("loss", 128))
SV_OFF = {}
_o = 0
for _n, _r in SV_PARTS:
    SV_OFF[_n] = _o
    _o += _r
SV_ROWS = 224
assert _o <= SV_ROWS * 128

ADAM_LR, ADAM_B1, ADAM_B2, ADAM_EPS, ADAM_WD, ADAM_STEP = 0.001, 0.9, 0.999, 1e-08, 0.01, 10

VMEM_BIG = 56 * 1024 * 1024
NEG = -1e30

NN = ((1,), (0,))
NT = ((1,), (1,))
TN = ((0,), (0,))


def _dot(a, b, dims=NN):
    return lax.dot_general(a.astype(_MXU_DTYPE), b.astype(_MXU_DTYPE), (dims, ((), ())),
                           preferred_element_type=F32)


def _dot_hi(a, b, dims=NN):
    return lax.dot_general(a.astype(F32), b.astype(F32), (dims, ((), ())),
                           precision=lax.Precision.HIGHEST, preferred_element_type=F32)


def _pick(n, cands):
    for c in cands:
        if n % c == 0:
            return c
    return n


def _sigmoid(x):
    return 1.0 / (1.0 + jnp.exp(-x))


def _silu(x):
    return x * _sigmoid(x)


def _dsilu(x):
    s = _sigmoid(x)
    return s * (1.0 + x * (1.0 - s))


def _softplus(x):
    return jnp.maximum(x, 0.0) + jnp.log(1.0 + jnp.exp(-jnp.abs(x)))


def _params(sem, vmem=None):
    return pltpu.CompilerParams(dimension_semantics=sem, vmem_limit_bytes=vmem)


def _row_step():
    return pl.program_id(0)


def _mm(a, b, mode, *, name, outs, tm, tn, tk, extras=(), epilogue=None, aliases=None, prologue=None):
    if mode == "tn":
        K, M = a.shape
        N = b.shape[1]
        a_spec = pl.BlockSpec((tk, tm), lambda i, j, k: (k, i))
        b_spec = pl.BlockSpec((tk, tn), lambda i, j, k: (k, j))
        dims = TN
    else:
        M = a.shape[0]
        K = b.shape[0] if mode == "nn" else b.shape[1]
        if prologue is None:
            assert a.shape[1] == K
            a_spec = pl.BlockSpec((tm, tk), lambda i, j, k: (i, k))
        else:
            assert tk == K
            a_spec = pl.BlockSpec((tm, a.shape[1]), lambda i, j, k: (i, 0))
        if mode == "nn":
            N = b.shape[1]
            b_spec = pl.BlockSpec((tk, tn), lambda i, j, k: (k, j))
            dims = NN
        else:
            N = b.shape[0]
            b_spec = pl.BlockSpec((tn, tk), lambda i, j, k: (j, k))
            dims = NT
    assert M % tm == 0 and N % tn == 0 and K % tk == 0, (name, M, N, K, tm, tn, tk)
    nk = K // tk
    ne, no = len(extras), len(outs)
    if epilogue is None:
        def epilogue(acc, ex, out_refs):
            out_refs[0][...] = acc.astype(out_refs[0].dtype)

    def body(a_ref, b_ref, *rest):
        ex, out_refs = rest[:ne], rest[ne:ne + no]
        lhs = a_ref[...] if prologue is None else prologue(a_ref, ex, out_refs, pl.program_id(1))
        p = _dot(lhs, b_ref[...], dims)
        if nk == 1:
            epilogue(p, ex, out_refs)
        else:
            acc = rest[-1]
            k = pl.program_id(2)

            @pl.when(k == 0)
            def _():
                acc[...] = p

            @pl.when(jnp.logical_and(k > 0, k < nk - 1))
            def _():
                acc[...] += p

            @pl.when(k == nk - 1)
            def _():
                epilogue(acc[...] + p, ex, out_refs)

    out_specs, out_shape = [], []
    for o in outs:
        if isinstance(o, tuple):
            shape, dt, bs, im = o
            out_specs.append(pl.BlockSpec(bs, im))
            out_shape.append(jax.ShapeDtypeStruct(shape, dt))
        else:
            out_specs.append(pl.BlockSpec((tm, tn), lambda i, j, k: (i, j)))
            out_shape.append(jax.ShapeDtypeStruct((M, N), o))
    in_specs = [a_spec, b_spec]
    for _, bs, im in extras:
        in_specs.append(pl.BlockSpec(memory_space=pl.ANY) if bs is None else pl.BlockSpec(bs, im))
    res = pl.pallas_call(
        body, name=name,
        grid=(M // tm, N // tn, nk),
        in_specs=in_specs, out_specs=out_specs, out_shape=out_shape,
        scratch_shapes=[pltpu.VMEM((tm, tn), F32)] if nk > 1 else [],
        input_output_aliases={2 + e: o for e, o in (aliases or {}).items()},
        compiler_params=_params(("arbitrary", "arbitrary", "arbitrary"), VMEM_BIG),
    )(a, b, *[e[0] for e in extras])
    return res if no > 1 else res[0]


def _rows(tm, w=D, col=0):
    return (tm, w), lambda i, j, k, c=col: (i, c)


def _vecs(w=D, col=0):
    return (1, w), lambda i, j, k, c=col: (0, c)


def _sum_out(w=D):
    return ((1, w), F32, (1, w), lambda i, j, k: (0, 0))


def _mm_pool_tn(a, b, *, name, tk):
    L = a.shape[0]

    def body(a_ref, b_ref, o_ref):
        p = _dot(a_ref[...], b_ref[...], TN)

        @pl.when(pl.program_id(1) == 0)
        def _():
            o_ref[...] = p

        @pl.when(pl.program_id(1) > 0)
        def _():
            o_ref[...] += p

    blk = pl.BlockSpec((tk, PGW), lambda g, k: (k, g))
    return pl.pallas_call(body, name=name, grid=(4, L // tk), in_specs=[blk, blk],
                          out_specs=pl.BlockSpec((PGW, PGW), lambda g, k: (g, 0)),
                          out_shape=jax.ShapeDtypeStruct((POOL_W, PGW), F32),
                          compiler_params=_params(("parallel", "arbitrary")))(a, b)


def _acc_out(ref, val, i):
    @pl.when(i == 0)
    def _():
        ref[...] = val

    @pl.when(i > 0)
    def _():
        ref[...] += val


def _colsum(v):
    return jnp.sum(v, axis=0, keepdims=True)


def _ep_resid_norm(acc, ex, outs):
    x_ref, g_ref, nw_ref, sc_ref, sh_ref = ex
    mix_ref, x1_ref, h_ref = outs
    mix_ref[...] = acc.astype(mix_ref.dtype)
    xv = x_ref[...] + g_ref[...] * acc
    x1_ref[...] = xv
    r = lax.rsqrt(jnp.mean(xv * xv, axis=-1, keepdims=True) + EPS)
    h_ref[...] = (xv * r * nw_ref[...] * (1.0 + sc_ref[...]) + sh_ref[...]).astype(h_ref.dtype)


def _ep_final(acc, ex, outs):
    x1_ref, t_ref, g_ref, nw_ref = ex
    dx2_ref, dd_ref, loss_ref, dnw_ref, dg_ref = outs
    i = _row_step()
    x2 = x1_ref[...] + g_ref[...] * acc
    r = lax.rsqrt(jnp.mean(x2 * x2, axis=-1, keepdims=True) + EPS)
    xh = x2 * r
    e = xh * nw_ref[...] - t_ref[...]
    part = 0.5 * jnp.sum(jnp.mean(e * e, axis=-1, keepdims=True), axis=0, keepdims=True)
    dy = e * (1.0 / D)
    g = dy * nw_ref[...]
    dx2 = r * (g - xh * jnp.mean(g * xh, axis=-1, keepdims=True))
    dx2_ref[...] = dx2
    dd_ref[...] = (dx2 * g_ref[...]).astype(dd_ref.dtype)
    _acc_out(loss_ref, jnp.broadcast_to(part, (1, 128)), i)
    _acc_out(dnw_ref, _colsum(dy * xh), i)
    _acc_out(dg_ref, _colsum(dx2 * acc), i)


def _ep_norm_bwd(acc, ex, outs):
    x_ref, dr_ref, nw_ref, sc_ref = ex[:4]
    dx_ref, p_ref, q_ref = outs[:3]
    i = _row_step()
    xv = x_ref[...]
    r = lax.rsqrt(jnp.mean(xv * xv, axis=-1, keepdims=True) + EPS)
    xh = xv * r
    g = acc * (nw_ref[...] * (1.0 + sc_ref[...]))
    dx = dr_ref[...] + r * (g - xh * jnp.mean(g * xh, axis=-1, keepdims=True))
    dx_ref[...] = dx
    _acc_out(p_ref, _colsum(acc * xh), i)
    _acc_out(q_ref, _colsum(acc), i)
    if len(ex) > 4:
        m_ref, g_ref = ex[4:]
        dm_ref, dg_ref = outs[3:]
        dm_ref[...] = (dx * g_ref[...]).astype(dm_ref.dtype)
        _acc_out(dg_ref, _colsum(dx * m_ref[...].astype(F32)), i)


def _ep_merge_bwd(acc, ex, outs):
    a_ref, b_ref, gl_ref = ex
    da_ref, db_ref, dgl_ref = outs
    s = _sigmoid(gl_ref[...].astype(F32))
    s1, s2 = s[:, :D], s[:, D:]
    da_ref[...] = (acc * s1).astype(da_ref.dtype)
    db_ref[...] = (acc * s2).astype(db_ref.dtype)
    dgl_ref[:, :D] = (acc * a_ref[...].astype(F32) * s1 * (1.0 - s1)).astype(dgl_ref.dtype)
    dgl_ref[:, D:] = (acc * b_ref[...].astype(F32) * s2 * (1.0 - s2)).astype(dgl_ref.dtype)


GW = DI // NG


def _ep_gated_norm_bwd(acc, ex, outs):
    y_ref, z_ref, w_ref, _ = ex
    dy_ref, dz_ref, dw_ref = outs
    zv = z_ref[...].astype(F32)
    yv = y_ref[...].astype(F32)
    sg = _sigmoid(zv)
    sz = zv * sg
    yg = yv * sz
    dsz = sg * (1.0 + zv * (1.0 - sg))
    dws = []
    for k in range(NG):
        sl = slice(k * GW, (k + 1) * GW)
        seg = yg[:, sl]
        r = lax.rsqrt(jnp.mean(seg * seg, axis=-1, keepdims=True) + EPS)
        sh = seg * r
        dn = acc[:, sl]
        g = dn * w_ref[:, sl]
        dyg = r * (g - sh * jnp.mean(g * sh, axis=-1, keepdims=True))
        dy_ref[:, sl] = dyg * sz[:, sl]
        dz_ref[:, sl] = (dyg * yv[:, sl] * dsz[:, sl]).astype(dz_ref.dtype)
        dws.append(_colsum(dn * sh))
    _acc_out(dw_ref, jnp.concatenate(dws, axis=1), _row_step())


CONV_CB = 128
HALO = 16


def _time_chunk(L):
    return _pick(L, (256, 128))


def _with_halo(x_ref, i, r0, rc):
    p0 = pl.multiple_of(jnp.maximum(r0 - HALO, 0), HALO)
    prev = jnp.where(i > 0, x_ref[pl.ds(p0, HALO), :].astype(F32), 0.0)
    return jnp.concatenate([prev, x_ref[pl.ds(r0, rc), :].astype(F32)], axis=0)


def _conv_bwd(proj, dy, w, b, dproj, *, name):
    L = proj.shape[0]
    rc = _time_chunk(L)
    n = L // rc

    def body(x_ref, dy_ref, w_ref, b_ref, dp_in, dx_ref, dw_ref, db_ref, xpad, dpad):
        del dp_in
        wv = w_ref[...]
        bv = b_ref[...]
        dpad[rc:rc + HALO, :] = jnp.zeros((HALO, CONV_CB), F32)

        def step(k, carry):
            db, d0, d1, d2, d3 = carry
            i = n - 1 - k
            r0 = pl.multiple_of(i * rc, rc)
            p0 = pl.multiple_of(jnp.maximum(r0 - HALO, 0), HALO)
            xpad[0:HALO, :] = jnp.where(i > 0, x_ref[pl.ds(p0, HALO), :].astype(F32), 0.0)
            xpad[HALO:HALO + rc, :] = x_ref[pl.ds(r0, rc), :].astype(F32)
            xk = [xpad[HALO - j:HALO - j + rc, :] for j in range(4)]
            pre = bv
            for j in range(4):
                pre = pre + xk[j] * wv[3 - j:4 - j]
            dpre = dy_ref[pl.ds(r0, rc), :] * _dsilu(pre)
            dpad[0:rc, :] = dpre
            acc = dpre * wv[3:4]
            for j in (1, 2, 3):
                acc = acc + dpad[j:j + rc, :] * wv[3 - j:4 - j]
            dx_ref[pl.ds(r0, rc), :] = acc.astype(dx_ref.dtype)
            dpad[rc:rc + HALO, :] = dpre[:HALO]
            return (db + _colsum(dpre), d0 + _colsum(dpre * xk[3]), d1 + _colsum(dpre * xk[2]),
                    d2 + _colsum(dpre * xk[1]), d3 + _colsum(dpre * xk[0]))

        z = jnp.zeros((1, CONV_CB), F32)
        db, d0, d1, d2, d3 = lax.fori_loop(0, n, step, (z, z, z, z, z))
        db_ref[...] = db
        dw_ref[...] = jnp.concatenate([d0, d1, d2, d3], axis=0)

    nb = XBC // CONV_CB
    return pl.pallas_call(
        body, name=name, grid=(nb,),
        in_specs=[pl.BlockSpec((L, CONV_CB), lambda j: (0, j + C_XBC // CONV_CB)),
                  pl.BlockSpec((L, CONV_CB), lambda j: (0, j)),
                  pl.BlockSpec((4, CONV_CB), lambda j: (0, j)), pl.BlockSpec((1, CONV_CB), lambda j: (0, j)),
                  pl.BlockSpec(memory_space=pl.ANY)],
        out_specs=[pl.BlockSpec((L, CONV_CB), lambda j: (0, j + C_XBC // CONV_CB)),
                   pl.BlockSpec((4, CONV_CB), lambda j: (0, j)), pl.BlockSpec((1, CONV_CB), lambda j: (0, j))],
        out_shape=[jax.ShapeDtypeStruct((L, NPROJ), BF16), jax.ShapeDtypeStruct((4, XBC), F32),
                   jax.ShapeDtypeStruct((1, XBC), F32)],
        scratch_shapes=[pltpu.VMEM((rc + HALO, CONV_CB), F32), pltpu.VMEM((rc + HALO, CONV_CB), F32)],
        input_output_aliases={4: 0},
        compiler_params=_params(("parallel",), VMEM_BIG))(proj, dy, w, b, dproj)


def _pool_fwd(proj, *, name):
    L = proj.shape[0]
    rc = _time_chunk(L)
    n = L // rc

    def body(x_ref, o_ref, pad):
        g = pl.program_id(0)
        pad[0:HALO, :] = jnp.zeros((HALO, PGW), F32)

        def fill(i, c):
            r0 = pl.multiple_of(i * rc, rc)
            pad[pl.ds(r0 + HALO, rc), :] = x_ref[pl.ds(r0, rc), :].astype(F32)
            return c

        lax.fori_loop(0, n, fill, 0)
        rows = lax.broadcasted_iota(jnp.int32, (rc, PGW), 0)

        for gi in range(4):
            win = 2 << gi

            @pl.when(g == gi)
            def _(gi=gi, win=win):
                def step(i, c):
                    r0 = pl.multiple_of(i * rc, rc)
                    ext = pad[pl.ds(r0, rc + HALO), :]
                    s = ext
                    sh = 1
                    while sh < win:
                        s = s + pltpu.roll(s, sh, 0)
                        sh *= 2
                    cnt = jnp.minimum(rows + (r0 + 1), win).astype(F32)
                    o_ref[pl.ds(r0, rc), :] = (s[HALO:] / cnt - ext[HALO:]).astype(o_ref.dtype)
                    return c

                lax.fori_loop(0, n, step, 0)

    return pl.pallas_call(
        body, name=name, grid=(4,),
        in_specs=[pl.BlockSpec((L, PGW), lambda j: (0, j + C_POOL // PGW))],
        out_specs=pl.BlockSpec((L, PGW), lambda j: (0, j)),
        out_shape=jax.ShapeDtypeStruct((L, POOL_W), BF16),
        scratch_shapes=[pltpu.VMEM((L + HALO, PGW), F32)],
        compiler_params=_params(("parallel",), VMEM_BIG))(proj)


def _pool_bwd(dpooled, dproj, *, name):
    L = dpooled.shape[0]
    rc = _time_chunk(L)
    n = L // rc

    def body(d_ref, dp_in, o_ref, pad):
        del dp_in
        g = pl.program_id(0)
        pad[L:L + HALO, :] = jnp.zeros((HALO, PGW), F32)
        rows = lax.broadcasted_iota(jnp.int32, (rc, PGW), 0)

        for gi in range(4):
            win = 2 << gi

            @pl.when(g == gi)
            def _(gi=gi, win=win):
                def fill(i, c):
                    r0 = pl.multiple_of(i * rc, rc)
                    cnt = jnp.minimum(rows + (r0 + 1), win).astype(F32)
                    pad[pl.ds(r0, rc), :] = d_ref[pl.ds(r0, rc), :] / cnt
                    return c

                lax.fori_loop(0, n, fill, 0)

                def step(i, c):
                    r0 = pl.multiple_of(i * rc, rc)
                    s = pad[pl.ds(r0, rc + HALO), :]
                    sh = 1
                    while sh < win:
                        s = s + pltpu.roll(s, rc + HALO - sh, 0)
                        sh *= 2
                    o_ref[pl.ds(r0, rc), :] = (s[:rc] - d_ref[pl.ds(r0, rc), :]).astype(o_ref.dtype)
                    return c

                lax.fori_loop(0, n, step, 0)

    return pl.pallas_call(
        body, name=name, grid=(4,),
        in_specs=[pl.BlockSpec((L, PGW), lambda j: (0, j)), pl.BlockSpec(memory_space=pl.ANY)],
        out_specs=pl.BlockSpec((L, PGW), lambda j: (0, j + C_POOL // PGW)),
        out_shape=jax.ShapeDtypeStruct((L, NPROJ), BF16),
        scratch_shapes=[pltpu.VMEM((L + HALO, PGW), F32)],
        input_output_aliases={1: 0},
        compiler_params=_params(("parallel",), VMEM_BIG))(dpooled, dproj)


_SPLIT_DT = jnp.bfloat16


def _ssd_consts():
    tri = np.tril(np.ones((Q, Q), np.float32))
    exp = np.zeros((128, DI), np.float32)
    for h in range(NH):
        exp[h, h * HP:(h + 1) * HP] = 1.0
    exp2 = np.concatenate([exp, exp], axis=0)
    return (jnp.asarray(tri, dtype=_SPLIT_DT), jnp.asarray(tri.T.copy(), dtype=_SPLIT_DT),
            jnp.asarray(exp2, dtype=_SPLIT_DT))


def _split(v, n):
    parts, r = [], v
    for _ in range(n):
        p = r.astype(_SPLIT_DT)
        parts.append(p)
        r = r - p.astype(F32)
    return parts


def _bdot(a, b, dims):
    return lax.dot_general(a, b, (dims, ((), ())), preferred_element_type=F32)


def _tri_sum(t_ref, v):
    r = _bdot(t_ref[...], jnp.concatenate(_split(v, 3), axis=1), NN)
    return r[:, :128] + r[:, 128:256] + r[:, 256:]


def _expand(v, e2_ref):
    return _bdot(jnp.concatenate(_split(v, 2), axis=1), e2_ref[...], NN)


def _reduce_heads(vals, eg):
    parts = []
    for v in vals:
        parts += _split(v, 2)
    r = _bdot(jnp.concatenate(parts, axis=0), eg, NT)
    return [r[2 * i * Q:(2 * i + 1) * Q] + r[(2 * i + 1) * Q:(2 * i + 2) * Q] for i in range(len(vals))]


def _ssd_common(xbc_ref, dtw_ref, arow_ref, t_ref, e_ref):
    dt = dtw_ref[:, :128]
    sig = dtw_ref[:, 128:]
    acs = _tri_sum(t_ref, dt * arow_ref[...])
    acs_x = _expand(acs, e_ref)
    dt_x = _expand(dt, e_ref)
    xs = xbc_ref[:, 0:DI]
    return sig, dt, acs, acs.T, acs_x, dt_x, xs


CONV_SLAB = 512


def _ssd_fwd(raw, dtp, cw, cb, arow, dsk_x, *, name):
    L = raw.shape[0]
    nc = L // Q
    tri, _, expand = _ssd_consts()

    def body(raw_ref, halo_ref, cw_ref, cb_ref, dtw_ref, arow_ref, dsk_ref, t_ref, e_ref,
             y_ref, hs_ref, xbc_ref, h_scr, cpad):
        c = pl.program_id(0)

        @pl.when(c == 0)
        def _():
            h_scr[...] = jnp.zeros_like(h_scr)

        cpad[0:8, :] = jnp.where(c > 0, halo_ref[...], 0.0)
        cpad[8:8 + Q, :] = raw_ref[...]
        for lo in range(0, XBC, CONV_SLAB):
            sl = slice(lo, lo + CONV_SLAB)
            acc = cb_ref[:, sl]
            for j in range(4):
                acc = acc + cpad[8 - j:8 - j + Q, sl] * cw_ref[3 - j:4 - j, sl]
            xbc_ref[:, sl] = acc * _sigmoid(acc)

        _, dt, acs, acs_t, acs_x, dt_x, xs = _ssd_common(xbc_ref, dtw_ref, arow_ref, t_ref, e_ref)
        xdt = xs * dt_x
        eacs = jnp.exp(acs_x)
        acs_last = acs_x[Q - 1:Q, :]
        dec = jnp.exp(acs_last - acs_x)
        hs_ref[0] = h_scr[...].astype(hs_ref.dtype)
        causal = lax.broadcasted_iota(jnp.int32, (Q, Q), 0) >= lax.broadcasted_iota(jnp.int32, (Q, Q), 1)
        first = lax.broadcasted_iota(jnp.int32, (Q, 128), 1) < HP
        for g in range(NG):
            bg = xbc_ref[:, DI + g * NS:DI + (g + 1) * NS]
            cg = xbc_ref[:, DI + NG * NS + g * NS:DI + NG * NS + (g + 1) * NS]
            s = _dot(cg, bg, NT)
            sl = slice(g * GW, (g + 1) * GW)
            hg = h_scr[:, sl]
            yoff = _dot(cg, hg, NN) * eacs[:, sl]
            st = _dot(bg, xdt[:, sl] * dec[:, sl], TN)
            h_scr[:, sl] = hg * eacs[Q - 1:Q, sl] + st
            for j in range(4):
                lo = g * GW + j * 128
                xb = xdt[:, lo:lo + 128]
                yp = yoff[:, j * 128:(j + 1) * 128] + dsk_ref[:, lo:lo + 128] * xs[:, lo:lo + 128]
                for e in range(2):
                    h = g * 8 + j * 2 + e
                    lm = jnp.exp(jnp.where(causal, acs[:, h:h + 1] - acs_t[h:h + 1, :], NEG))
                    xm = jnp.where(first if e == 0 else jnp.logical_not(first), xb, 0.0)
                    yp = yp + _dot(s * lm, xm, NN)
                y_ref[:, lo:lo + 128] = yp.astype(y_ref.dtype)

    const = lambda c: (0, 0)
    return pl.pallas_call(
        body, name=name, grid=(nc,),
        in_specs=[pl.BlockSpec((Q, XBC), lambda c: (c, 0)),
                  pl.BlockSpec((8, XBC), lambda c: (jnp.maximum(c * (Q // 8) - 1, 0), 0)),
                  pl.BlockSpec((4, XBC), const), pl.BlockSpec((1, XBC), const),
                  pl.BlockSpec((Q, DT_PAD), lambda c: (c, 0)),
                  pl.BlockSpec((1, 128), const), pl.BlockSpec((1, DI), const),
                  pl.BlockSpec((Q, Q), const), pl.BlockSpec((256, DI), const)],
        out_specs=[pl.BlockSpec((Q, DI), lambda c: (c, 0)), pl.BlockSpec((1, NS, DI), lambda c: (c, 0, 0)),
                   pl.BlockSpec((Q, XBC), lambda c: (c, 0))],
        out_shape=[jax.ShapeDtypeStruct((L, DI), BF16), jax.ShapeDtypeStruct((nc, NS, DI), F32),
                   jax.ShapeDtypeStruct((L, XBC), F32)],
        scratch_shapes=[pltpu.VMEM((NS, DI), F32), pltpu.VMEM((8 + Q, XBC), F32)],
        compiler_params=_params(("arbitrary",), VMEM_BIG))(raw, raw, cw, cb, dtp, arow, dsk_x, tri, expand)


def _ssd_bwd(dy, xbc, dtp, hs, arow, dsk_x, dproj, *, name):
    L = xbc.shape[0]
    nc = L // Q
    tri, triu, expand = _ssd_consts()

    def body(dy_ref, xbc_ref, dtw_ref, hs_ref, arow_ref, dsk_ref, t_ref, u_ref, e_ref, dp_in,
             dxbc_ref, ddtw_ref, da_ref, ddx_ref, ddtb_ref, dh_scr):
        del dp_in
        i = pl.program_id(0)

        @pl.when(i == 0)
        def _():
            dh_scr[...] = jnp.zeros_like(dh_scr)

        sig, dt, acs, acs_t, acs_x, dt_x, xs = _ssd_common(xbc_ref, dtw_ref, arow_ref, t_ref, e_ref)
        dyv = dy_ref[...]
        xdt = xs * dt_x
        eacs = jnp.exp(acs_x)
        acs_last = acs_x[Q - 1:Q, :]
        dec = jnp.exp(acs_last - acs_x)
        gy = dyv * eacs
        causal = lax.broadcasted_iota(jnp.int32, (Q, Q), 0) >= lax.broadcasted_iota(jnp.int32, (Q, Q), 1)
        first = lax.broadcasted_iota(jnp.int32, (Q, 128), 1) < HP
        lane_h = lax.broadcasted_iota(jnp.int32, (Q, 128), 1)
        sub_h = lax.broadcasted_iota(jnp.int32, (128, Q), 0)
        last_row = lax.broadcasted_iota(jnp.int32, (Q, GW), 0) == Q - 1
        dacs = jnp.zeros((Q, 128), F32)
        dacs_t = jnp.zeros((128, Q), F32)
        ddt = jnp.zeros((Q, 128), F32)
        for g in range(NG):
            bg = xbc_ref[:, DI + g * NS:DI + (g + 1) * NS]
            cg = xbc_ref[:, DI + NG * NS + g * NS:DI + NG * NS + (g + 1) * NS]
            s = _dot(cg, bg, NT)
            sl = slice(g * GW, (g + 1) * GW)
            hg = hs_ref[0, :, sl].astype(F32)
            dhn = dh_scr[:, sl]
            eal = eacs[Q - 1:Q, sl]
            gg = gy[:, sl]
            dax = gg * _dot(cg, hg, NN)
            dcg = _dot(gg, hg, NT)
            dh_scr[:, sl] = _dot(cg, gg, TN) + dhn * eal
            dal = eal * _colsum(dhn * hg)
            xdd = xdt[:, sl] * dec[:, sl]
            dbg = _dot(xdd, dhn, NT)
            wv = _dot(bg, dhn, NN)
            dd = wv * xdd
            dax = dax - dd
            dal = dal + _colsum(dd)
            dax = dax + jnp.where(last_row, dal, 0.0)
            dxdt_g = wv * dec[:, sl]
            ds = jnp.zeros((Q, Q), F32)
            dxdt_blocks = []
            for j in range(4):
                lo = g * GW + j * 128
                xb = xdt[:, lo:lo + 128]
                dyb = dyv[:, lo:lo + 128]
                dxb = dxdt_g[:, j * 128:(j + 1) * 128]
                for e in range(2):
                    h = g * 8 + j * 2 + e
                    lm = jnp.exp(jnp.where(causal, acs[:, h:h + 1] - acs_t[h:h + 1, :], NEG))
                    m = s * lm
                    dym = jnp.where(first if e == 0 else jnp.logical_not(first), dyb, 0.0)
                    dm = _dot(dym, xb, NT)
                    r = dm * m
                    dacs = dacs + jnp.where(lane_h == h, jnp.sum(r, axis=1, keepdims=True), 0.0)
                    dacs_t = dacs_t + jnp.where(sub_h == h, _colsum(r), 0.0)
                    ds = ds + dm * lm
                    dxb = dxb + _dot(m, dym, TN)
                dxdt_blocks.append(dxb)
            dxdt = jnp.concatenate(dxdt_blocks, axis=1)
            dcg = dcg + _dot(ds, bg, NN)
            dbg = dbg + _dot(ds, cg, TN)
            dxbc_ref[:, DI + g * NS:DI + (g + 1) * NS] = dbg
            dxbc_ref[:, DI + NG * NS + g * NS:DI + NG * NS + (g + 1) * NS] = dcg
            dxbc_ref[:, sl] = dsk_ref[:, sl] * dyv[:, sl] + dxdt * dt_x[:, sl]
            ddt_g, dacs_g = _reduce_heads([dxdt * xs[:, sl], dax], e_ref[0:128, sl])
            ddt = ddt + ddt_g
            dacs = dacs + dacs_g
        dacs = dacs - dacs_t.T
        ddta = _tri_sum(u_ref, dacs)
        ddt = ddt + ddta * arow_ref[...]
        ddtw = jnp.where(lane_h < NH, ddt * sig, 0.0)
        ddtw_ref[...] = jnp.concatenate([ddtw, jnp.zeros((Q, DT_PAD - 128), F32)], axis=1).astype(ddtw_ref.dtype)
        _acc_out(da_ref, _colsum(ddta * dt), i)
        _acc_out(ddx_ref, _colsum(dyv * xs), i)
        _acc_out(ddtb_ref, _colsum(ddtw), i)

    rev = lambda c: (nc - 1 - c, 0)
    const = lambda c: (0, 0)
    return pl.pallas_call(
        body, name=name, grid=(nc,),
        in_specs=[pl.BlockSpec((Q, DI), rev), pl.BlockSpec((Q, XBC), rev),
                  pl.BlockSpec((Q, DT_PAD), rev),
                  pl.BlockSpec((1, NS, DI), lambda c: (nc - 1 - c, 0, 0)),
                  pl.BlockSpec((1, 128), const), pl.BlockSpec((1, DI), const),
                  pl.BlockSpec((Q, Q), const), pl.BlockSpec((Q, Q), const), pl.BlockSpec((256, DI), const),
                  pl.BlockSpec(memory_space=pl.ANY)],
        out_specs=[pl.BlockSpec((Q, XBC), rev),
                   pl.BlockSpec((Q, DT_PAD), lambda c: (nc - 1 - c, C_DT // DT_PAD)),
                   pl.BlockSpec((1, 128), const), pl.BlockSpec((1, DI), const), pl.BlockSpec((1, 128), const)],
        out_shape=[jax.ShapeDtypeStruct((L, XBC), F32), jax.ShapeDtypeStruct((L, NPROJ), BF16),
                   jax.ShapeDtypeStruct((1, 128), F32), jax.ShapeDtypeStruct((1, DI), F32),
                   jax.ShapeDtypeStruct((1, 128), F32)],
        scratch_shapes=[pltpu.VMEM((NS, DI), F32)],
        input_output_aliases={9: 1},
        compiler_params=_params(("arbitrary",), VMEM_BIG))(dy, xbc, dtp, hs, arow, dsk_x, tri, triu,
                                                          expand, dproj)


def _adam_update(wv, gv, mv, vv):
    c1 = 1.0 - ADAM_B1 ** ADAM_STEP
    c2 = 1.0 - ADAM_B2 ** ADAM_STEP
    mn = ADAM_B1 * mv + (1.0 - ADAM_B1) * gv
    vn = ADAM_B2 * vv + (1.0 - ADAM_B2) * (gv * gv)
    return -ADAM_LR * ((mn / c1) / (jnp.sqrt(vn / c2) + ADAM_EPS) + ADAM_WD * wv), mn, vn


def _adamw(w, g, m, v, *, name, tr=None):
    R = w.shape[0]
    rest = tuple(w.shape[1:])
    if tr is None:
        tr = _pick(R, (256, 128, 64, 32, 16, 8))
    assert R % tr == 0

    def body(w_ref, g_ref, m_ref, v_ref, d_ref, mo_ref, vo_ref):
        d_ref[...], mo_ref[...], vo_ref[...] = _adam_update(w_ref[...], g_ref[...], m_ref[...], v_ref[...])

    zeros = (0,) * len(rest)
    spec = pl.BlockSpec((tr,) + rest, lambda i: (i,) + zeros)
    return pl.pallas_call(body, name=name, grid=(R // tr,), in_specs=[spec] * 4, out_specs=[spec] * 3,
                          out_shape=[jax.ShapeDtypeStruct(w.shape, F32)] * 3,
                          compiler_params=_params(("parallel",)))(w, g, m, v)


def _adamw_small(svrow, g_conv, params, *, name):
    n = len(params)

    def body(*refs):
        sv_ref, gc_ref = refs[0], refs[1]
        ins, outs = refs[2:2 + 3 * n], refs[2 + 3 * n:]
        for p, (key, w, _, _) in enumerate(params):
            w_ref, m_ref, v_ref = ins[3 * p:3 * p + 3]
            g_ref, d_ref, mo_ref, vo_ref = outs[4 * p:4 * p + 4]
            gv = gc_ref[...] if key == "conv_w" else sv_ref[:, SV_OFF[key]:SV_OFF[key] + w.shape[1]]
            g_ref[...] = gv
            d_ref[...], mo_ref[...], vo_ref[...] = _adam_update(w_ref[...], gv, m_ref[...], v_ref[...])

    vm = pl.BlockSpec(memory_space=pltpu.VMEM)
    args = [svrow, g_conv]
    shapes = []
    for _, w, m, v in params:
        args += [w, m, v]
        shapes += [jax.ShapeDtypeStruct(w.shape, F32)] * 4
    res = pl.pallas_call(body, name=name, in_specs=[vm] * len(args), out_specs=[vm] * len(shapes),
                         out_shape=shapes)(*args)
    return {key: tuple(res[4 * p:4 * p + 4]) for p, (key, _, _, _) in enumerate(params)}


def _slab_sum(recv, *, tile, name):
    rows = recv.shape[1]
    assert rows % tile == 0 and tile % 16 == 0

    def body(r_ref, o_ref):
        acc = r_ref[0].astype(F32)
        for j in range(1, N_DEV):
            acc = acc + r_ref[j].astype(F32)
        o_ref[...] = acc

    return pl.pallas_call(body, name=name, grid=(rows // tile,),
                          in_specs=[pl.BlockSpec((N_DEV, tile, D), lambda i: (0, i, 0))],
                          out_specs=pl.BlockSpec((tile, D), lambda i: (i, 0)),
                          out_shape=jax.ShapeDtypeStruct((rows, D), F32),
                          compiler_params=_params(("parallel",)))(recv)


MESH = pl.DeviceIdType.MESH


def _coords():
    return lax.axis_index("x"), lax.axis_index("y"), lax.axis_index("c")


def _peer(k):
    x, y, c = _coords()
    px = 1 - x if k & 4 else x
    py = 1 - y if k & 2 else y
    pc = 1 - c if k & 1 else c
    return (px, py, pc), 4 * px + 2 * py + pc


def _rcopy(src, dst, ssem, rsem, dev):
    return pltpu.make_async_remote_copy(src_ref=src, dst_ref=dst, send_sem=ssem, recv_sem=rsem,
                                        device_id=dev, device_id_type=MESH)


def _exchange_all(src_of, dst_slot, send_sems, recv_sems):
    x, y, c = _coords()
    me = 4 * x + 2 * y + c
    sent = []
    for k in range(1, N_DEV):
        dev, pidx = _peer(k)
        cp = _rcopy(src_of(pidx), dst_slot(me), send_sems.at[k - 1], recv_sems.at[k - 1], dev)
        cp.start()
        sent.append(cp)
    for k in range(1, N_DEV):
        dev, pidx = _peer(k)
        _rcopy(src_of(pidx), dst_slot(pidx), send_sems.at[k - 1], recv_sems.at[k - 1], dev).wait_recv()
    for cp in sent:
        cp.wait_send()


def _rows_of_slots(buf, nslots):
    rows = lax.broadcasted_iota(jnp.int32, (8, buf.shape[-1]), 0)
    out = jnp.zeros((8, buf.shape[-1]), F32)
    for j in range(nslots):
        out = out + jnp.where(rows == j, buf[j], 0.0)
    return out


def _exchange_start(src_of, dst_slot, send_sems, recv_sems):
    x, y, c = _coords()
    me = 4 * x + 2 * y + c
    sent = []
    for k in range(1, N_DEV):
        dev, pidx = _peer(k)
        cp = _rcopy(src_of(pidx), dst_slot(me), send_sems.at[k - 1], recv_sems.at[k - 1], dev)
        cp.start()
        sent.append(cp)
    return sent


def _exchange_finish(sent, src_of, dst_slot, send_sems, recv_sems):
    for k in range(1, N_DEV):
        dev, pidx = _peer(k)
        _rcopy(src_of(pidx), dst_slot(pidx), send_sems.at[k - 1], recv_sems.at[k - 1], dev).wait_recv()
    for cp in sent:
        cp.wait_send()


def _ada_gather(c, w_ada, b_r, slab, *, name):
    wloc = w_ada.shape[1]

    def body(c_ref, w_ref, b_ref, x_ref, mod_ref, call_ref, out_ref,
             csrc, cbuf, psrc, pbuf, s1, r1, s2, r2, send_sems, recv_sems, local_sem):
        x, y, cc = _coords()
        me_i = 4 * x + 2 * y + cc
        me, sibling = (x, y, cc), (x, y, 1 - cc)
        chips = [(1 - x, y), (x, 1 - y), (1 - x, 1 - y)]

        def slot(px, py, pc):
            return out_ref.at[4 * px + 2 * py + pc]

        def copy(k, block, to, src=None):
            return _rcopy(slot(*block) if src is None else src, slot(*block), send_sems.at[k], recv_sems.at[k], to)

        csrc[...] = jnp.broadcast_to(c_ref[...], (8, D))
        cbuf[me_i] = csrc[...]
        c_of, c_slot = (lambda p: csrc), (lambda s: cbuf.at[s])
        sent1 = _exchange_start(c_of, c_slot, s1, r1)

        mine = pltpu.make_async_copy(x_ref, slot(*me), local_sem)
        mine.start()
        first = [copy(0, me, sibling, src=x_ref)]
        first += [copy(1 + j, me, (*chip, cc), src=x_ref) for j, chip in enumerate(chips)]
        for cp in first:
            cp.start()

        _exchange_finish(sent1, c_of, c_slot, s1, r1)
        call = _rows_of_slots(cbuf, N_DEV)
        call_ref[...] = call
        prod = _dot_hi(_silu(call), w_ref[...])
        for b in range(N_DEV):
            psrc[b] = jnp.broadcast_to(prod[b:b + 1, :], (8, wloc))
        pbuf[me_i] = psrc[me_i]
        p_of, p_slot = (lambda p: psrc.at[p]), (lambda s: pbuf.at[s])
        sent2 = _exchange_start(p_of, p_slot, s2, r2)

        passed = [copy(4 + j, (*chip, cc), sibling) for j, chip in enumerate(chips)]
        for j, chip in enumerate(chips):
            copy(1 + j, (*chip, cc), me).wait_recv()
            passed[j].start()
        copy(0, sibling, me).wait_recv()
        for j, chip in enumerate(chips):
            copy(4 + j, (*chip, 1 - cc), me).wait_recv()

        _exchange_finish(sent2, p_of, p_slot, s2, r2)
        mod_ref[...] = _rows_of_slots(pbuf, N_DEV) + b_ref[...]
        for cp in first + passed:
            cp.wait_send()
        mine.wait()

    vm = pl.BlockSpec(memory_space=pltpu.VMEM)
    anyspec = pl.BlockSpec(memory_space=pl.ANY)
    return pl.pallas_call(
        body, name=name, in_specs=[vm, vm, vm, anyspec], out_specs=[vm, vm, anyspec],
        out_shape=[jax.ShapeDtypeStruct((N_DEV, wloc), F32), jax.ShapeDtypeStruct((N_DEV, D), F32),
                   jax.ShapeDtypeStruct((N_DEV,) + slab.shape, slab.dtype)],
        scratch_shapes=[pltpu.VMEM((8, D), F32), pltpu.VMEM((N_DEV, 8, D), F32),
                        pltpu.VMEM((N_DEV, 8, wloc), F32), pltpu.VMEM((N_DEV, 8, wloc), F32),
                        pltpu.SemaphoreType.DMA((N_DEV - 1,)), pltpu.SemaphoreType.DMA((N_DEV - 1,)),
                        pltpu.SemaphoreType.DMA((N_DEV - 1,)), pltpu.SemaphoreType.DMA((N_DEV - 1,)),
                        pltpu.SemaphoreType.DMA((7,)), pltpu.SemaphoreType.DMA((7,)), pltpu.SemaphoreType.DMA],
        compiler_params=pltpu.CompilerParams(vmem_limit_bytes=VMEM_BIG))(c, w_ada, b_r, slab)


_HBM =pl.BlockSpec(memory_space=pltpu.HBM)
_SEM = pl.BlockSpec(memory_space=pltpu.SEMAPHORE)
_EFFECT = pltpu.SideEffectType.DATAFLOW_SIDE_EFFECTING


def _xchg_src(src_ref, pidx, per_peer):
    return src_ref.at[pidx] if per_peer else src_ref


def _xchg_start(src, *, per_peer, name):
    rows = src.shape[-2]
    land_shape = (N_DEV, rows, D)

    def body(src_ref, land_ref, send_sems, recv_sems, src_thru, land_thru, token):
        del src_thru, land_thru
        x, y, c = _coords()
        me = 4 * x + 2 * y + c
        for k in range(1, N_DEV):
            dev, pidx = _peer(k)
            _rcopy(_xchg_src(src_ref, pidx, per_peer), land_ref.at[me], send_sems.at[k - 1],
                   recv_sems.at[k - 1], dev).start()
        token[...] = jnp.zeros_like(token)

    return pl.pallas_call(
        body, name=name,
        out_shape=(pltpu.SemaphoreType.DMA((N_DEV - 1,)), pltpu.SemaphoreType.DMA((N_DEV - 1,)),
                   pltpu.HBM(src.shape, src.dtype), pltpu.HBM(land_shape, src.dtype),
                   jax.ShapeDtypeStruct((8, 128), F32)),
        in_specs=(_HBM, _HBM),
        out_specs=(_SEM, _SEM, _HBM, _HBM, pl.BlockSpec(memory_space=pltpu.VMEM)),
        input_output_aliases={0: 2, 1: 3},
        compiler_params=pltpu.CompilerParams(has_side_effects=_EFFECT),
    )(pltpu.with_memory_space_constraint(src, pltpu.HBM),
      pltpu.with_memory_space_constraint(lax.empty(land_shape, src.dtype), pltpu.HBM))


def _xchg_wait(started, after, *, per_peer, name):
    send_sems, recv_sems, src_thru, land_thru, _ = started

    def body(src_ref, land_ref, send_sems, recv_sems, after_ref, src_dead, got_ref):
        del after_ref, src_dead, got_ref
        for k in range(1, N_DEV):
            dev, pidx = _peer(k)
            cp = _rcopy(_xchg_src(src_ref, pidx, per_peer), land_ref.at[pidx], send_sems.at[k - 1],
                        recv_sems.at[k - 1], dev)
            cp.wait_send()
            cp.wait_recv()

    return pl.pallas_call(
        body, name=name,
        out_shape=(pltpu.HBM(src_thru.shape, src_thru.dtype), pltpu.HBM(land_thru.shape, land_thru.dtype)),
        in_specs=(_HBM, _HBM, _SEM, _SEM, pl.BlockSpec(memory_space=pl.ANY)),
        out_specs=(_HBM, _HBM),
        input_output_aliases={0: 0, 1: 1},
        compiler_params=pltpu.CompilerParams(has_side_effects=_EFFECT),
    )(src_thru, land_thru, send_sems, recv_sems, after)


def _dep(token):
    return (token, (8, 128), lambda i, j, k: (0, 0))


def _small_allsum(sv, *, name):
    def body(sv_ref, all_ref, sum_ref, send_sems, recv_sems):
        x, y, c = _coords()
        me = 4 * x + 2 * y + c
        all_ref[me] = sv_ref[...]
        _exchange_all(lambda p: sv_ref, lambda s: all_ref.at[s], send_sems, recv_sems)
        acc = all_ref[0]
        for j in range(1, N_DEV):
            acc = acc + all_ref[j]
        sum_ref[...] = acc

    vm = pl.BlockSpec(memory_space=pltpu.VMEM)
    return pl.pallas_call(
        body, name=name, in_specs=[vm], out_specs=[vm, vm],
        out_shape=[jax.ShapeDtypeStruct((N_DEV, SV_ROWS, 128), F32), jax.ShapeDtypeStruct((SV_ROWS, 128), F32)],
        scratch_shapes=[pltpu.SemaphoreType.DMA((7,)), pltpu.SemaphoreType.DMA((7,))],
    )(sv)


def _ada_bwd(call, dmod_loc, *, name):
    wloc = dmod_loc.shape[1]

    def body(c_ref, d_ref, o_ref):
        o_ref[...] = _dot_hi(_silu(c_ref[...]), d_ref[...], TN)

    vm = pl.BlockSpec(memory_space=pltpu.VMEM)
    return pl.pallas_call(body, name=name, in_specs=[vm, vm], out_specs=vm,
                          out_shape=jax.ShapeDtypeStruct((D, wloc), F32),
                          compiler_params=pltpu.CompilerParams(vmem_limit_bytes=VMEM_BIG))(call, dmod_loc)


def _pad_rows(a, rows):
    return jnp.pad(a, ((0, rows - a.shape[0]), (0, 0)))


IN_SHIFT = tuple((IN_ROWS * j) % 16 for j in range(N_DEV))
IN_BASE = tuple(IN_ROWS * j - IN_SHIFT[j] for j in range(N_DEV))
IN_SEGMENTS = ((2048, XBC, C_XBC), (5152, 1024, C_POOL), (0, 2048, C_Z), (6176, 2048, C_GATE), (5120, 32, C_DT))


def _global_pieces(gs):
    pieces = []
    for j in range(N_DEV):
        lo, hi = 0, IN_ROWS_P
        if j > 0 and IN_BASE[j - 1] + IN_ROWS_P > IN_BASE[j]:
            pieces.append((IN_BASE[j], 16, gs[j - 1, IN_ROWS_P - 16:IN_ROWS_P] + gs[j, 0:16]))
            lo = 16
        if j + 1 < N_DEV and IN_BASE[j] + IN_ROWS_P > IN_BASE[j + 1]:
            hi = IN_ROWS_P - 16
        pieces.append((IN_BASE[j] + lo, hi - lo, gs[j, lo:hi]))
    return pieces


def _reorder_in_rows(gs):
    pieces = _global_pieces(gs)
    parts = []
    for lo, n, _ in IN_SEGMENTS:
        for p0, pn, arr in pieces:
            a, b = max(lo, p0), min(lo + n, p0 + pn)
            if a < b:
                parts.append(arr[a - p0:b - p0])
    parts.append(jnp.zeros((DT_PAD - 32, D), gs.dtype))
    return jnp.concatenate(parts, axis=0)


def _restore_in_shards(d):
    slabs = []
    for j in range(N_DEV):
        parts = []
        r, end = IN_BASE[j], IN_BASE[j] + IN_ROWS_P
        while r < end:
            lo, n, new = next(s for s in IN_SEGMENTS if s[0] <= r < s[0] + s[1])
            e = min(end, lo + n)
            parts.append(d[new + r - lo:new + e - lo])
            r = e
        slabs.append(jnp.concatenate(parts, axis=0))
    return jnp.stack(slabs, axis=0)


def _pack_sv(parts):
    flat = []
    for n, size in SV_PARTS:
        v = parts[n].reshape(-1).astype(F32)
        flat.append(jnp.pad(v, (0, size - v.shape[0])))
    v = jnp.concatenate(flat)
    return jnp.pad(v, (0, SV_ROWS * 128 - v.shape[0])).reshape(SV_ROWS, 128)


def _sv_get(flat, n, size):
    return flat[SV_OFF[n]:SV_OFF[n] + size]


def kernel(x, c, w_ada, b_ada, norm_mix_w, w_in, conv_w, conv_b, dt_bias, a_log, d_skip, ssd_norm_w, w_branch_ssd, pool_w, pool_scale, w_branch_pool, w_out, norm_mlp_w, w_up, w_down, norm_final_w, loss_target, m_w_ada, m_b_ada, m_norm_mix_w, m_w_in, m_conv_w, m_conv_b, m_dt_bias, m_a_log, m_d_skip, m_ssd_norm_w, m_w_branch_ssd, m_pool_w, m_pool_scale, m_w_branch_pool, m_w_out, m_norm_mlp_w, m_w_up, m_w_down, m_norm_final_w, v_w_ada, v_b_ada, v_norm_mix_w, v_w_in, v_conv_w, v_conv_b, v_dt_bias, v_a_log, v_d_skip, v_ssd_norm_w, v_w_branch_ssd, v_pool_w, v_pool_scale, v_w_branch_pool, v_w_out, v_norm_mlp_w, v_w_up, v_w_down, v_norm_final_w):
    xs_ = x[0]
    tgt = loss_target[0]
    L = xs_.shape[0]
    me = 4 * lax.axis_index("x") + 2 * lax.axis_index("y") + lax.axis_index("c")
    wloc = w_ada.shape[2]

    conv_bits = lax.bitcast_convert_type(conv_w[0], SLAB_DT).reshape(3, D)
    in_shift = (IN_ROWS * me) % 16
    slab_in = lax.dynamic_update_slice(jnp.zeros((IN_ROWS_P, D), SLAB_DT), w_in[0].T.astype(SLAB_DT),
                                       (in_shift, 0))
    slab_in = jnp.concatenate([slab_in, _pad_rows(conv_bits, CONV_ROWS)], axis=0)
    slab_rest = jnp.concatenate([
        w_branch_ssd[0].astype(SLAB_DT),
        pool_w[0].reshape(32, D).astype(SLAB_DT),
        w_branch_pool[0].astype(SLAB_DT),
        w_out[0].astype(SLAB_DT),
        w_up[0].T.astype(SLAB_DT),
        w_down[0].astype(SLAB_DT)], axis=0)
    mod_p, c_all, gs_in = _ada_gather(c, w_ada[0], b_ada.reshape(N_DEV, wloc), slab_in,
                                      name="ada_gather_w_in")
    mod = mod_p.reshape(6, D)
    shift_m, scale_m, gate_m, shift_f, scale_f, gate_f = [mod[i:i + 1] for i in range(6)]
    slab_rest, gs_in = lax.optimization_barrier((slab_rest, gs_in))
    rest_started = _xchg_start(slab_rest, per_peer=False, name="gather_rest_start")
    gather_token = rest_started[4]

    w_in_t = _reorder_in_rows(gs_in)
    conv_full = lax.bitcast_convert_type(
        gs_in[:, IN_ROWS_P:IN_ROWS_P + 3].reshape(N_DEV, 4, XBC // N_DEV, 2), F32)
    conv_full = conv_full.transpose(1, 0, 2).reshape(4, XBC)

    dtb = jnp.pad(dt_bias, ((0, 0), (0, 128 - NH)))
    arow = jnp.pad(-jnp.exp(a_log), ((0, 0), (0, 128 - NH)))
    dsk_x = jnp.repeat(d_skip, HP, axis=1)

    tm = _pick(L, (1024, 512, 256, 128))
    tm2 = _pick(L, (2048, 1024, 512, 256, 128))
    tkl = _pick(L, (4096, 2048, 1024, 512, 256, 128))
    tkl2 = _pick(L, (2048, 1024, 512, 256, 128))

    tmh = _pick(L, (512, 256, 128))
    tmq = _pick(L, (256, 128))
    zcol = C_Z // DI
    gcol = C_GATE // (2 * D)

    def whole_rows(w):
        return lambda t: ((L, w), BF16, (t, w), lambda i, j, k: (i, 0))

    def norm1_pro(x_ref, ex, outs, j):
        @pl.when(j == 0)
        def _():
            xv = x_ref[...]
            r = lax.rsqrt(jnp.mean(xv * xv, axis=-1, keepdims=True) + EPS)
            outs[1][...] = (xv * r * ex[0][...] * (1.0 + ex[1][...]) + ex[2][...]).astype(outs[1].dtype)

        return outs[1][...]

    def proj_ep(acc, ex, outs):
        outs[0][...] = acc

        @pl.when(pl.program_id(1) == NPROJ // 768 - 1)
        def _():
            pre = acc[:, 768 - DT_PAD:768 - DT_PAD + 128] + ex[3][...]
            outs[2][...] = jnp.concatenate([_softplus(pre), _sigmoid(pre)], axis=1)

    proj, h1, dtp = _mm(
        xs_, w_in_t, "nt", name="in_proj", tm=tm2, tn=768, tk=D,
        extras=[(norm_mix_w, *_vecs()), (scale_m, *_vecs()), (shift_m, *_vecs()), (dtb, *_vecs(128)),
                _dep(gather_token)],
        outs=[F32, whole_rows(D)(tm2), ((L, DT_PAD), F32, (tm2, DT_PAD), lambda i, j, k: (i, 0))],
        prologue=norm1_pro, epilogue=proj_ep)
    xbc_raw = proj
    y_ssm, hs, xbc = _ssd_fwd(xbc_raw, dtp, conv_full, conv_b, arow, dsk_x, name="ssd_fwd")

    slab_rest, gs = _xchg_wait(rest_started, y_ssm, per_peer=False, name="gather_rest_wait")
    gs = lax.dynamic_update_slice(gs, slab_rest[None], (me, 0, 0))

    def part(n, rows):
        return gs[:, REST_OFF[n]:REST_OFF[n] + rows]

    w_bssd = part("bssd", 256).reshape(DI, D)
    w_pool = part("pool", 32).reshape(N_DEV, 4, 32, PGW).transpose(1, 0, 2, 3).reshape(POOL_W, PGW)
    w_bpool = part("bpool", 128).reshape(POOL_W, D)
    w_o = part("out", 128).reshape(D, D)
    w_up_t = part("up", 512).reshape(DFF, D)
    w_dn = part("down", 512).reshape(DFF, D)

    def gnorm_pro(y_ref, ex, outs, j):
        z_ref, w_ref = ex
        yg = y_ref[...].astype(F32) * _silu(z_ref[...].astype(F32))
        segs = []
        for k in range(NG):
            sl = slice(k * GW, (k + 1) * GW)
            seg = yg[:, sl]
            r = lax.rsqrt(jnp.mean(seg * seg, axis=-1, keepdims=True) + EPS)
            segs.append((seg * r * w_ref[:, sl]).astype(BF16))
        yn_v = jnp.concatenate(segs, axis=1)
        outs[1][...] = yn_v
        return yn_v

    y_ssd, yn = _mm(y_ssm, w_bssd, "nn", name="branch_ssd", tm=tmh, tn=D, tk=DI,
                    extras=[(proj, *_rows(tmh, DI, zcol)), (ssd_norm_w, *_vecs(DI))],
                    outs=[BF16, whole_rows(DI)(tmh)], prologue=gnorm_pro)
    pooled = _pool_fwd(proj, name="pool_fwd")
    wp_spec = ((POOL_W, PGW), lambda i, j, k: (0, 0))

    def pool_pro(a_ref, ex, outs, j):
        wp_ref, s_ref = ex
        segs = []
        for g in range(4):
            sl = slice(g * PGW, (g + 1) * PGW)
            p = _dot(a_ref[:, sl], wp_ref[sl, :], NN)
            outs[1][:, sl] = p.astype(BF16)
            segs.append((p * s_ref[:, sl]).astype(BF16))
        yp1_v = jnp.concatenate(segs, axis=1)
        outs[2][...] = yp1_v
        return yp1_v

    y_pool, yp0, yp1 = _mm(pooled, w_bpool, "nn", name="branch_pool", tm=tm, tn=D, tk=D,
                           extras=[(w_pool, *wp_spec), (pool_scale, *_vecs())],
                           outs=[BF16, whole_rows(D)(tm), whole_rows(D)(tm)], prologue=pool_pro)

    def merge_pro(a_ref, ex, outs, j):
        s = _sigmoid(ex[1][...].astype(F32))
        mv = (s[:, :D] * a_ref[...].astype(F32) + s[:, D:] * ex[0][...].astype(F32)).astype(BF16)
        outs[3][...] = mv
        return mv

    mix, x1, h2, m = _mm(y_ssd, w_o, "nn", name="out_proj", tm=tmh, tn=D, tk=D,
                         extras=[(y_pool, *_rows(tmh)), (proj, *_rows(tmh, 2 * D, gcol)),
                                 (xs_, *_rows(tmh)), (gate_m, *_vecs()), (norm_mlp_w, *_vecs()),
                                 (scale_f, *_vecs()), (shift_f, *_vecs())],
                         outs=[BF16, F32, BF16, whole_rows(D)(tmh)], prologue=merge_pro,
                         epilogue=lambda acc, ex, outs: _ep_resid_norm(acc, ex[2:], outs[:3]))

    def relu2(acc, ex, outs):
        r = jnp.maximum(acc, 0.0)
        outs[0][...] = acc.astype(BF16)
        outs[1][...] = (r * r).astype(BF16)

    up, act = _mm(h2, w_up_t, "nt", name="mlp_up", outs=[BF16, BF16], tm=tmh, tn=DFF, tk=D, epilogue=relu2)

    dx2, ddown, loss_p, dnwf, dgate_f = _mm(
        act, w_dn, "nn", name="mlp_down", tm=tmh, tn=D, tk=DFF,
        extras=[(x1, *_rows(tmh)), (tgt, *_rows(tmh)), (gate_f, *_vecs()), (norm_final_w.reshape(1, D), *_vecs())],
        outs=[F32, BF16, _sum_out(128), _sum_out(), _sum_out()], epilogue=_ep_final)

    def drelu2(acc, ex, outs):
        outs[0][...] = (acc * (2.0 * jnp.maximum(ex[0][...].astype(F32), 0.0))).astype(BF16)

    def dep_last(ep):
        return lambda acc, ex, outs: ep(acc, ex[:-1], outs)

    dup = _mm(ddown, w_dn, "nt", name="mlp_down_dx", outs=[BF16], tm=tmh, tn=DFF, tk=D,
              extras=[(up, (tmh, DFF), lambda i, j, k: (i, j))], epilogue=drelu2)
    g_dn = _mm(act, ddown, "tn", name="mlp_down_dw", outs=[SLAB_DT], tm=1024, tn=D, tk=tkl)
    g_up_t = _mm(dup, h2, "tn", name="mlp_up_dw", outs=[SLAB_DT], tm=1024, tn=D, tk=tkl)
    gslab_mlp = jnp.concatenate([g_up_t.reshape(N_DEV, 512, D), g_dn.reshape(N_DEV, 512, D)], axis=1)
    mlp_started = _xchg_start(gslab_mlp, per_peer=True, name="scatter_mlp_start")
    dx1, p2, q2, dmix, dgate_m = _mm(
        dup, w_up_t, "nn", name="mlp_up_dx", tm=tmh, tn=D, tk=DFF,
        extras=[(x1, *_rows(tmh)), (dx2, *_rows(tmh)), (norm_mlp_w, *_vecs()), (scale_f, *_vecs()),
                (mix, *_rows(tmh)), (gate_m, *_vecs()), _dep(mlp_started[4])],
        outs=[F32, _sum_out(), _sum_out(), BF16, _sum_out()], epilogue=dep_last(_ep_norm_bwd))
    gcol = C_GATE // (2 * D)
    dy_ssd, dy_pool, dproj = _mm(
        dmix, w_o, "nt", name="out_proj_dx", tm=tmh, tn=D, tk=D,
        extras=[(y_ssd, *_rows(tmh)), (y_pool, *_rows(tmh)), (proj, *_rows(tmh, 2 * D, gcol))],
        outs=[BF16, BF16, ((L, NPROJ), BF16, *_rows(tmh, 2 * D, gcol))], epilogue=_ep_merge_bwd)
    g_o = _mm(m, dmix, "tn", name="out_proj_dw", outs=[SLAB_DT], tm=D, tn=D, tk=tkl)
    zcol = C_Z // DI
    dy_ssm, dproj, d_snw = _mm(
        dy_ssd, w_bssd, "nt", name="branch_ssd_dx", tm=tmh, tn=DI, tk=D,
        extras=[(y_ssm, *_rows(tmh, DI)), (proj, *_rows(tmh, DI, zcol)), (ssd_norm_w, *_vecs(DI)),
                (dproj, None, None)],
        outs=[F32, ((L, NPROJ), BF16, *_rows(tmh, DI, zcol)), _sum_out(DI)],
        epilogue=_ep_gated_norm_bwd, aliases={3: 1})
    g_bssd = _mm(yn, dy_ssd, "tn", name="branch_ssd_dw", outs=[SLAB_DT], tm=1024, tn=D, tk=tkl)
    dxbc, dproj, d_a, d_dx, d_dtb = _ssd_bwd(dy_ssm, xbc, dtp, hs, arow, dsk_x, dproj, name="ssd_bwd")
    dproj, d_cw, d_cb = _conv_bwd(xbc_raw, dxbc, conv_full, conv_b, dproj, name="conv_bwd")
    def pool_bwd_ep(acc, ex, outs):
        y_ref, s_ref, wp_ref = ex
        o_ref, ds_ref, dpool_ref = outs
        dyp0_v = (acc * s_ref[...]).astype(BF16)
        o_ref[...] = dyp0_v
        _acc_out(ds_ref, _colsum(acc * y_ref[...].astype(F32)), _row_step())
        for g in range(4):
            sl = slice(g * PGW, (g + 1) * PGW)
            dpool_ref[:, sl] = _dot(dyp0_v[:, sl], wp_ref[sl, :], NT)

    dyp0, d_ps, dpooled = _mm(dy_pool, w_bpool, "nt", name="branch_pool_dx", tm=tm, tn=D, tk=D,
                              extras=[(yp0, *_rows(tm)), (pool_scale, *_vecs()), (w_pool, *wp_spec)],
                              outs=[BF16, _sum_out(), F32], epilogue=pool_bwd_ep)
    g_bpool = _mm(yp1, dy_pool, "tn", name="branch_pool_dw", outs=[SLAB_DT], tm=D, tn=D, tk=tkl)
    g_pool = _mm_pool_tn(pooled, dyp0, name="pool_mix_dw", tk=tkl)
    gslab_mix = jnp.concatenate([
        g_bssd.reshape(N_DEV, 256, D),
        g_pool.reshape(4, N_DEV, 32, PGW).transpose(1, 0, 2, 3).reshape(N_DEV, 32, D).astype(SLAB_DT),
        g_bpool.reshape(N_DEV, 128, D),
        g_o.reshape(N_DEV, 128, D)], axis=1)
    mix_started = _xchg_start(gslab_mix, per_peer=True, name="scatter_mix_start")
    dproj = _pool_bwd(dpooled, dproj, name="pool_bwd")
    g_in_t = _mm(dproj, h1, "tn", name="in_proj_dw", outs=[SLAB_DT], tm=tmq, tn=D, tk=L,
                 extras=[_dep(mix_started[4])])
    gslab_in = _restore_in_shards(g_in_t)
    in_started = _xchg_start(gslab_in, per_peer=True, name="scatter_in_start")
    grad_x, p1, q1 = _mm(
        dproj, w_in_t, "nn", name="in_proj_dx", tm=tmq, tn=D, tk=NPROJ,
        extras=[(xs_, *_rows(tmq)), (dx1, *_rows(tmq)), (norm_mix_w, *_vecs()), (scale_m, *_vecs()),
                _dep(in_started[4])],
        outs=[F32, _sum_out(), _sum_out()], epilogue=dep_last(_ep_norm_bwd))

    def landed(started, after, tile, name):
        src, land = _xchg_wait(started, after, per_peer=True, name=name + "_wait")
        own = lax.dynamic_slice_in_dim(src, me, 1, axis=0)
        return _slab_sum(lax.dynamic_update_slice(land, own, (me, 0, 0)), tile=tile, name=name + "_sum")

    gsum_mlp = landed(mlp_started, grad_x, 256, "scatter_mlp")
    gsum_mix = landed(mix_started, grad_x, 272, "scatter_mix")
    gsum_in = landed(in_started, grad_x, 208, "scatter_in")

    dmod = jnp.concatenate([q1, p1 * norm_mix_w, dgate_m, q2, p2 * norm_mlp_w, dgate_f], axis=1)
    d_alog = d_a[:, :NH] * (-jnp.exp(a_log))
    sv = _pack_sv({
        "b_ada": dmod, "norm_mix_w": p1 * (1.0 + scale_m), "conv_b": d_cb, "dt_bias": d_dtb[:, :NH],
        "a_log": d_alog, "d_skip": d_dx.reshape(NH, HP).sum(axis=1), "ssd_norm_w": d_snw,
        "pool_scale": d_ps, "norm_mlp_w": p2 * (1.0 + scale_f), "norm_final_w": dnwf, "conv_w": d_cw,
        "loss": loss_p[:, :1]})
    sv_all, sv_sum = _small_allsum(sv, name="small_allsum")
    flat = sv_sum.reshape(-1)
    loss = flat[SV_OFF["loss"]]
    dmod_all = sv_all.reshape(N_DEV, SV_ROWS * 128)[:, :6 * D]
    g_w_ada = _ada_bwd(c_all, lax.dynamic_slice_in_dim(dmod_all, me * wloc, wloc, axis=1), name="ada_bwd")

    g_conv_w = lax.dynamic_slice_in_dim(_sv_get(flat, "conv_w", 4 * XBC).reshape(4, XBC),
                                        me * (XBC // N_DEV), XBC // N_DEV, axis=1)
    small = [("b_ada", b_ada, m_b_ada, v_b_ada), ("norm_mix_w", norm_mix_w, m_norm_mix_w, v_norm_mix_w),
             ("conv_b", conv_b, m_conv_b, v_conv_b), ("dt_bias", dt_bias, m_dt_bias, v_dt_bias),
             ("a_log", a_log, m_a_log, v_a_log), ("d_skip", d_skip, m_d_skip, v_d_skip),
             ("ssd_norm_w", ssd_norm_w, m_ssd_norm_w, v_ssd_norm_w),
             ("pool_scale", pool_scale, m_pool_scale, v_pool_scale),
             ("norm_mlp_w", norm_mlp_w, m_norm_mlp_w, v_norm_mlp_w),
             ("norm_final_w", norm_final_w[None], m_norm_final_w[None], v_norm_final_w[None]),
             ("conv_w", conv_w[0], m_conv_w[0], v_conv_w[0])]
    small_out = _adamw_small(sv_sum.reshape(1, SV_ROWS * 128), g_conv_w, small, name="adamw_small")
    small_out["norm_final_w"] = tuple(a[0] for a in small_out["norm_final_w"])
    small_out["conv_w"] = tuple(a[None] for a in small_out["conv_w"])

    def gpart(n, rows_):
        return gsum_mix[MIX_OFF[n]:MIX_OFF[n] + rows_]

    def lin(a):
        return a[0].T.reshape(IN_ROWS * 8, 128)

    g_lin = lax.dynamic_slice_in_dim(gsum_in, in_shift, IN_ROWS, axis=0).reshape(IN_ROWS * 8, 128)
    dlt, mn, vn = _adamw(lin(w_in), g_lin, lin(m_w_in), lin(v_w_in), name="adamw_w_in", tr=IN_ROWS * 2)
    big_in = tuple(a.reshape(IN_ROWS, D).T[None] for a in (g_lin, dlt, mn, vn))

    big = {
        "w_ada": (w_ada, m_w_ada, v_w_ada, g_w_ada, (D, wloc)),
        "w_branch_ssd": (w_branch_ssd, m_w_branch_ssd, v_w_branch_ssd, gpart("bssd", 256), (256, D)),
        "pool_w": (pool_w, m_pool_w, v_pool_w, gpart("pool", 32).reshape(128, PGW), (128, PGW)),
        "w_branch_pool": (w_branch_pool, m_w_branch_pool, v_w_branch_pool, gpart("bpool", 128), (128, D)),
        "w_out": (w_out, m_w_out, v_w_out, gpart("out", 128), (128, D)),
        "w_up": (w_up, m_w_up, v_w_up, gsum_mlp[:512].T, (D, 512)),
        "w_down": (w_down, m_w_down, v_w_down, gsum_mlp[512:], (512, D)),
    }
    big_out = {}
    for n, (w, mm_, vv, g, shp2) in big.items():
        dlt, mn, vn = _adamw(w.reshape(shp2), g, mm_.reshape(shp2), vv.reshape(shp2), name="adamw_" + n)
        big_out[n] = (g.reshape(w.shape), dlt.reshape(w.shape), mn.reshape(w.shape), vn.reshape(w.shape))

    order = ["w_ada", "b_ada", "norm_mix_w", "w_in", "conv_w", "conv_b", "dt_bias", "a_log", "d_skip",
             "ssd_norm_w", "w_branch_ssd", "pool_w", "pool_scale", "w_branch_pool", "w_out", "norm_mlp_w",
             "w_up", "w_down", "norm_final_w"]
    big_out["w_in"] = big_in
    res = {**small_out, **big_out}
    outs = [loss, grad_x.reshape(x.shape)]
    for k in range(4):
        outs += [res[n][k] for n in order]
    return tuple(outs)
```

```python
import functools

import numpy as np
import jax
import jax.numpy as jnp
from jax import lax
from jax.experimental import pallas as pl
from jax.experimental.pallas import tpu as pltpu

F32 = jnp.float32
BF16 = jnp.bfloat16
SLAB_DT = jnp.bfloat16
_MXU_DTYPE = jnp.bfloat16

N_DEV = 8
D = 1024
DI = 2048
NH = 32
HP = 64
NG = 4
NS = 128
Q = 128
XBC = DI + 2 * NG * NS
DFF = 4096
N_IN = 8224
EPS = 1e-5
POOL_W = 1024
PGW = 256

C_XBC, C_POOL, C_Z, C_GATE, C_DT = 0, 3072, 4096, 6144, 8192
DT_PAD = 256
NPROJ = C_DT + DT_PAD

IN_ROWS = N_IN // N_DEV
IN_ROWS_P = 1040
CONV_ROWS = 16
REST_PARTS = (("bssd", 256), ("pool", 32), ("bpool", 128), ("out", 128), ("up", 512), ("down", 512))
REST_OFF = {}
_o = 0
for _n, _r in REST_PARTS:
    REST_OFF[_n] = _o
    _o += _r
REST_ROWS = _o
MIX_PARTS = (("bssd", 256), ("pool", 32), ("bpool", 128), ("out", 128))
MIX_OFF = {}
_o = 0
for _n, _r in MIX_PARTS:
    MIX_OFF[_n] = _o
    _o += _r
MIX_ROWS = _o

SV_PARTS = (("b_ada", 6144), ("norm_mix_w", 1024), ("conv_b", 3072), ("dt_bias", 128), ("a_log", 128),
            ("d_skip", 128), ("ssd_norm_w", 2048), ("pool_scale", 1024), ("norm_mlp_w", 1024),
            ("norm_final_w", 1024), ("conv_w", 4 * XBC), ("loss", 128))
SV_OFF = {}
_o = 0
for _n, _r in SV_PARTS:
    SV_OFF[_n] = _o
    _o += _r
SV_ROWS = 224
assert _o <= SV_ROWS * 128

ADAM_LR, ADAM_B1, ADAM_B2, ADAM_EPS, ADAM_WD, ADAM_STEP = 0.001, 0.9, 0.999, 1e-08, 0.01, 10

VMEM_BIG = 56 * 1024 * 1024
NEG = -1e30

NN = ((1,), (0,))
NT = ((1,), (1,))
TN = ((0,), (0,))


def _dot(a, b, dims=NN):
    return lax.dot_general(a.astype(_MXU_DTYPE), b.astype(_MXU_DTYPE), (dims, ((), ())),
                           preferred_element_type=F32)


def _dot_hi(a, b, dims=NN):
    return lax.dot_general(a.astype(F32), b.astype(F32), (dims, ((), ())),
                           precision=lax.Precision.HIGHEST, preferred_element_type=F32)


def _pick(n, cands):
    for c in cands:
        if n % c == 0:
            return c
    return n


def _sigmoid(x):
    return 1.0 / (1.0 + jnp.exp(-x))


def _silu(x):
    return x * _sigmoid(x)


def _dsilu(x):
    s = _sigmoid(x)
    return s * (1.0 + x * (1.0 - s))


def _softplus(x):
    return jnp.maximum(x, 0.0) + jnp.log(1.0 + jnp.exp(-jnp.abs(x)))


def _params(sem, vmem=None):
    return pltpu.CompilerParams(dimension_semantics=sem, vmem_limit_bytes=vmem)


def _row_step():
    return pl.program_id(0)


def _mm(a, b, mode, *, name, outs, tm, tn, tk, extras=(), epilogue=None, aliases=None, prologue=None):
    if mode == "tn":
        K, M = a.shape
        N = b.shape[1]
        a_spec = pl.BlockSpec((tk, tm), lambda i, j, k: (k, i))
        b_spec = pl.BlockSpec((tk, tn), lambda i, j, k: (k, j))
        dims = TN
    else:
        M = a.shape[0]
        K = b.shape[0] if mode == "nn" else b.shape[1]
        if prologue is None:
            assert a.shape[1] == K
            a_spec = pl.BlockSpec((tm, tk), lambda i, j, k: (i, k))
        else:
            assert tk == K
            a_spec = pl.BlockSpec((tm, a.shape[1]), lambda i, j, k: (i, 0))
        if mode == "nn":
            N = b.shape[1]
            b_spec = pl.BlockSpec((tk, tn), lambda i, j, k: (k, j))
            dims = NN
        else:
            N = b.shape[0]
            b_spec = pl.BlockSpec((tn, tk), lambda i, j, k: (j, k))
            dims = NT
    assert M % tm == 0 and N % tn == 0 and K % tk == 0, (name, M, N, K, tm, tn, tk)
    nk = K // tk
    ne, no = len(extras), len(outs)
    if epilogue is None:
        def epilogue(acc, ex, out_refs):
            out_refs[0][...] = acc.astype(out_refs[0].dtype)

    def body(a_ref, b_ref, *rest):
        ex, out_refs = rest[:ne], rest[ne:ne + no]
        lhs = a_ref[...] if prologue is None else prologue(a_ref, ex, out_refs, pl.program_id(1))
        p = _dot(lhs, b_ref[...], dims)
        if nk == 1:
            epilogue(p, ex, out_refs)
        else:
            acc = rest[-1]
            k = pl.program_id(2)

            @pl.when(k == 0)
            def _():
                acc[...] = p

            @pl.when(jnp.logical_and(k > 0, k < nk - 1))
            def _():
                acc[...] += p

            @pl.when(k == nk - 1)
            def _():
                epilogue(acc[...] + p, ex, out_refs)

    out_specs, out_shape = [], []
    for o in outs:
        if isinstance(o, tuple):
            shape, dt, bs, im = o
            out_specs.append(pl.BlockSpec(bs, im))
            out_shape.append(jax.ShapeDtypeStruct(shape, dt))
        else:
            out_specs.append(pl.BlockSpec((tm, tn), lambda i, j, k: (i, j)))
            out_shape.append(jax.ShapeDtypeStruct((M, N), o))
    in_specs = [a_spec, b_spec]
    for _, bs, im in extras:
        in_specs.append(pl.BlockSpec(memory_space=pl.ANY) if bs is None else pl.BlockSpec(bs, im))
    res = pl.pallas_call(
        body, name=name,
        grid=(M // tm, N // tn, nk),
        in_specs=in_specs, out_specs=out_specs, out_shape=out_shape,
        scratch_shapes=[pltpu.VMEM((tm, tn), F32)] if nk > 1 else [],
        input_output_aliases={2 + e: o for e, o in (aliases or {}).items()},
        compiler_params=_params(("arbitrary", "arbitrary", "arbitrary"), VMEM_BIG),
    )(a, b, *[e[0] for e in extras])
    return res if no > 1 else res[0]


def _rows(tm, w=D, col=0):
    return (tm, w), lambda i, j, k, c=col: (i, c)


def _vecs(w=D, col=0):
    return (1, w), lambda i, j, k, c=col: (0, c)


def _sum_out(w=D):
    return ((1, w), F32, (1, w), lambda i, j, k: (0, 0))


def _mm_pool_tn(a, b, *, name, tk):
    L = a.shape[0]

    def body(a_ref, b_ref, o_ref):
        p = _dot(a_ref[...], b_ref[...], TN)

        @pl.when(pl.program_id(1) == 0)
        def _():
            o_ref[...] = p

        @pl.when(pl.program_id(1) > 0)
        def _():
            o_ref[...] += p

    blk = pl.BlockSpec((tk, PGW), lambda g, k: (k, g))
    return pl.pallas_call(body, name=name, grid=(4, L // tk), in_specs=[blk, blk],
                          out_specs=pl.BlockSpec((PGW, PGW), lambda g, k: (g, 0)),
                          out_shape=jax.ShapeDtypeStruct((POOL_W, PGW), F32),
                          compiler_params=_params(("parallel", "arbitrary")))(a, b)


def _acc_out(ref, val, i):
    @pl.when(i == 0)
    def _():
        ref[...] = val

    @pl.when(i > 0)
    def _():
        ref[...] += val


def _colsum(v):
    return jnp.sum(v, axis=0, keepdims=True)


def _ep_resid_norm(acc, ex, outs):
    x_ref, g_ref, nw_ref, sc_ref, sh_ref = ex
    mix_ref, x1_ref, h_ref = outs
    mix_ref[...] = acc.astype(mix_ref.dtype)
    xv = x_ref[...] + g_ref[...] * acc
    x1_ref[...] = xv
    r = lax.rsqrt(jnp.mean(xv * xv, axis=-1, keepdims=True) + EPS)
    h_ref[...] = (xv * r * nw_ref[...] * (1.0 + sc_ref[...]) + sh_ref[...]).astype(h_ref.dtype)


def _ep_final(acc, ex, outs):
    x1_ref, t_ref, g_ref, nw_ref = ex
    dx2_ref, dd_ref, loss_ref, dnw_ref, dg_ref = outs
    i = _row_step()
    x2 = x1_ref[...] + g_ref[...] * acc
    r = lax.rsqrt(jnp.mean(x2 * x2, axis=-1, keepdims=True) + EPS)
    xh = x2 * r
    e = xh * nw_ref[...] - t_ref[...]
    part = 0.5 * jnp.sum(jnp.mean(e * e, axis=-1, keepdims=True), axis=0, keepdims=True)
    dy = e * (1.0 / D)
    g = dy * nw_ref[...]
    dx2 = r * (g - xh * jnp.mean(g * xh, axis=-1, keepdims=True))
    dx2_ref[...] = dx2
    dd_ref[...] = (dx2 * g_ref[...]).astype(dd_ref.dtype)
    _acc_out(loss_ref, jnp.broadcast_to(part, (1, 128)), i)
    _acc_out(dnw_ref, _colsum(dy * xh), i)
    _acc_out(dg_ref, _colsum(dx2 * acc), i)


def _ep_norm_bwd(acc, ex, outs):
    x_ref, dr_ref, nw_ref, sc_ref = ex[:4]
    dx_ref, p_ref, q_ref = outs[:3]
    i = _row_step()
    xv = x_ref[...]
    r = lax.rsqrt(jnp.mean(xv * xv, axis=-1, keepdims=True) + EPS)
    xh = xv * r
    g = acc * (nw_ref[...] * (1.0 + sc_ref[...]))
    dx = dr_ref[...] + r * (g - xh * jnp.mean(g * xh, axis=-1, keepdims=True))
    dx_ref[...] = dx
    _acc_out(p_ref, _colsum(acc * xh), i)
    _acc_out(q_ref, _colsum(acc), i)
    if len(ex) > 4:
        m_ref, g_ref = ex[4:]
        dm_ref, dg_ref = outs[3:]
        dm_ref[...] = (dx * g_ref[...]).astype(dm_ref.dtype)
        _acc_out(dg_ref, _colsum(dx * m_ref[...].astype(F32)), i)


def _ep_merge_bwd(acc, ex, outs):
    a_ref, b_ref, gl_ref = ex
    da_ref, db_ref, dgl_ref = outs
    s = _sigmoid(gl_ref[...].astype(F32))
    s1, s2 = s[:, :D], s[:, D:]
    da_ref[...] = (acc * s1).astype(da_ref.dtype)
    db_ref[...] = (acc * s2).astype(db_ref.dtype)
    dgl_ref[:, :D] = (acc * a_ref[...].astype(F32) * s1 * (1.0 - s1)).astype(dgl_ref.dtype)
    dgl_ref[:, D:] = (acc * b_ref[...].astype(F32) * s2 * (1.0 - s2)).astype(dgl_ref.dtype)


GW = DI // NG


def _ep_gated_norm_bwd(acc, ex, outs):
    y_ref, z_ref, w_ref, _ = ex
    dy_ref, dz_ref, dw_ref = outs
    zv = z_ref[...].astype(F32)
    yv = y_ref[...].astype(F32)
    sg = _sigmoid(zv)
    sz = zv * sg
    yg = yv * sz
    dsz = sg * (1.0 + zv * (1.0 - sg))
    dws = []
    for k in range(NG):
        sl = slice(k * GW, (k + 1) * GW)
        seg = yg[:, sl]
        r = lax.rsqrt(jnp.mean(seg * seg, axis=-1, keepdims=True) + EPS)
        sh = seg * r
        dn = acc[:, sl]
        g = dn * w_ref[:, sl]
        dyg = r * (g - sh * jnp.mean(g * sh, axis=-1, keepdims=True))
        dy_ref[:, sl] = dyg * sz[:, sl]
        dz_ref[:, sl] = (dyg * yv[:, sl] * dsz[:, sl]).astype(dz_ref.dtype)
        dws.append(_colsum(dn * sh))
    _acc_out(dw_ref, jnp.concatenate(dws, axis=1), _row_step())


CONV_CB = 128
HALO = 16


def _time_chunk(L):
    return _pick(L, (256, 128))


def _with_halo(x_ref, i, r0, rc):
    p0 = pl.multiple_of(jnp.maximum(r0 - HALO, 0), HALO)
    prev = jnp.where(i > 0, x_ref[pl.ds(p0, HALO), :].astype(F32), 0.0)
    return jnp.concatenate([prev, x_ref[pl.ds(r0, rc), :].astype(F32)], axis=0)


def _conv_bwd(proj, dy, w, b, dproj, *, name):
    L = proj.shape[0]
    rc = _time_chunk(L)
    n = L // rc

    def body(x_ref, dy_ref, w_ref, b_ref, dp_in, dx_ref, dw_ref, db_ref, xpad, dpad):
        del dp_in
        wv = w_ref[...]
        bv = b_ref[...]
        dpad[rc:rc + HALO, :] = jnp.zeros((HALO, CONV_CB), F32)

        def step(k, carry):
            db, d0, d1, d2, d3 = carry
            i = n - 1 - k
            r0 = pl.multiple_of(i * rc, rc)
            p0 = pl.multiple_of(jnp.maximum(r0 - HALO, 0), HALO)
            xpad[0:HALO, :] = jnp.where(i > 0, x_ref[pl.ds(p0, HALO), :].astype(F32), 0.0)
            xpad[HALO:HALO + rc, :] = x_ref[pl.ds(r0, rc), :].astype(F32)
            xk = [xpad[HALO - j:HALO - j + rc, :] for j in range(4)]
            pre = bv
            for j in range(4):
                pre = pre + xk[j] * wv[3 - j:4 - j]
            dpre = dy_ref[pl.ds(r0, rc), :] * _dsilu(pre)
            dpad[0:rc, :] = dpre
            acc = dpre * wv[3:4]
            for j in (1, 2, 3):
                acc = acc + dpad[j:j + rc, :] * wv[3 - j:4 - j]
            dx_ref[pl.ds(r0, rc), :] = acc.astype(dx_ref.dtype)
            dpad[rc:rc + HALO, :] = dpre[:HALO]
            return (db + _colsum(dpre), d0 + _colsum(dpre * xk[3]), d1 + _colsum(dpre * xk[2]),
                    d2 + _colsum(dpre * xk[1]), d3 + _colsum(dpre * xk[0]))

        z = jnp.zeros((1, CONV_CB), F32)
        db, d0, d1, d2, d3 = lax.fori_loop(0, n, step, (z, z, z, z, z))
        db_ref[...] = db
        dw_ref[...] = jnp.concatenate([d0, d1, d2, d3], axis=0)

    nb = XBC // CONV_CB
    return pl.pallas_call(
        body, name=name, grid=(nb,),
        in_specs=[pl.BlockSpec((L, CONV_CB), lambda j: (0, j + C_XBC // CONV_CB)),
                  pl.BlockSpec((L, CONV_CB), lambda j: (0, j)),
                  pl.BlockSpec((4, CONV_CB), lambda j: (0, j)), pl.BlockSpec((1, CONV_CB), lambda j: (0, j)),
                  pl.BlockSpec(memory_space=pl.ANY)],
        out_specs=[pl.BlockSpec((L, CONV_CB), lambda j: (0, j + C_XBC // CONV_CB)),
                   pl.BlockSpec((4, CONV_CB), lambda j: (0, j)), pl.BlockSpec((1, CONV_CB), lambda j: (0, j))],
        out_shape=[jax.ShapeDtypeStruct((L, NPROJ), BF16), jax.ShapeDtypeStruct((4, XBC), F32),
                   jax.ShapeDtypeStruct((1, XBC), F32)],
        scratch_shapes=[pltpu.VMEM((rc + HALO, CONV_CB), F32), pltpu.VMEM((rc + HALO, CONV_CB), F32)],
        input_output_aliases={4: 0},
        compiler_params=_params(("parallel",), VMEM_BIG))(proj, dy, w, b, dproj)


def _pool_fwd(proj, *, name):
    L = proj.shape[0]
    rc = _time_chunk(L)
    n = L // rc

    def body(x_ref, o_ref, pad):
        g = pl.program_id(0)
        pad[0:HALO, :] = jnp.zeros((HALO, PGW), F32)

        def fill(i, c):
            r0 = pl.multiple_of(i * rc, rc)
            pad[pl.ds(r0 + HALO, rc), :] = x_ref[pl.ds(r0, rc), :].astype(F32)
            return c

        lax.fori_loop(0, n, fill, 0)
        rows = lax.broadcasted_iota(jnp.int32, (rc, PGW), 0)

        for gi in range(4):
            win = 2 << gi

            @pl.when(g == gi)
            def _(gi=gi, win=win):
                def step(i, c):
                    r0 = pl.multiple_of(i * rc, rc)
                    ext = pad[pl.ds(r0, rc + HALO), :]
                    s = ext
                    sh = 1
                    while sh < win:
                        s = s + pltpu.roll(s, sh, 0)
                        sh *= 2
                    cnt = jnp.minimum(rows + (r0 + 1), win).astype(F32)
                    o_ref[pl.ds(r0, rc), :] = (s[HALO:] / cnt - ext[HALO:]).astype(o_ref.dtype)
                    return c

                lax.fori_loop(0, n, step, 0)

    return pl.pallas_call(
        body, name=name, grid=(4,),
        in_specs=[pl.BlockSpec((L, PGW), lambda j: (0, j + C_POOL // PGW))],
        out_specs=pl.BlockSpec((L, PGW), lambda j: (0, j)),
        out_shape=jax.ShapeDtypeStruct((L, POOL_W), BF16),
        scratch_shapes=[pltpu.VMEM((L + HALO, PGW), F32)],
        compiler_params=_params(("parallel",), VMEM_BIG))(proj)


def _pool_bwd(dpooled, dproj, *, name):
    L = dpooled.shape[0]
    rc = _time_chunk(L)
    n = L // rc

    def body(d_ref, dp_in, o_ref, pad):
        del dp_in
        g = pl.program_id(0)
        pad[L:L + HALO, :] = jnp.zeros((HALO, PGW), F32)
        rows = lax.broadcasted_iota(jnp.int32, (rc, PGW), 0)

        for gi in range(4):
            win = 2 << gi

            @pl.when(g == gi)
            def _(gi=gi, win=win):
                def fill(i, c):
                    r0 = pl.multiple_of(i * rc, rc)
                    cnt = jnp.minimum(rows + (r0 + 1), win).astype(F32)
                    pad[pl.ds(r0, rc), :] = d_ref[pl.ds(r0, rc), :] / cnt
                    return c

                lax.fori_loop(0, n, fill, 0)

                def step(i, c):
                    r0 = pl.multiple_of(i * rc, rc)
                    s = pad[pl.ds(r0, rc + HALO), :]
                    sh = 1
                    while sh < win:
                        s = s + pltpu.roll(s, rc + HALO - sh, 0)
                        sh *= 2
                    o_ref[pl.ds(r0, rc), :] = (s[:rc] - d_ref[pl.ds(r0, rc), :]).astype(o_ref.dtype)
                    return c

                lax.fori_loop(0, n, step, 0)

    return pl.pallas_call(
        body, name=name, grid=(4,),
        in_specs=[pl.BlockSpec((L, PGW), lambda j: (0, j)), pl.BlockSpec(memory_space=pl.ANY)],
        out_specs=pl.BlockSpec((L, PGW), lambda j: (0, j + C_POOL // PGW)),
        out_shape=jax.ShapeDtypeStruct((L, NPROJ), BF16),
        scratch_shapes=[pltpu.VMEM((L + HALO, PGW), F32)],
        input_output_aliases={1: 0},
        compiler_params=_params(("parallel",), VMEM_BIG))(dpooled, dproj)


_SPLIT_DT = jnp.bfloat16


def _ssd_consts():
    tri = np.tril(np.ones((Q, Q), np.float32))
    exp = np.zeros((128, DI), np.float32)
    for h in range(NH):
        exp[h, h * HP:(h + 1) * HP] = 1.0
    exp2 = np.concatenate([exp, exp], axis=0)
    return (jnp.asarray(tri, dtype=_SPLIT_DT), jnp.asarray(tri.T.copy(), dtype=_SPLIT_DT),
            jnp.asarray(exp2, dtype=_SPLIT_DT))


def _split(v, n):
    parts, r = [], v
    for _ in range(n):
        p = r.astype(_SPLIT_DT)
        parts.append(p)
        r = r - p.astype(F32)
    return parts


def _bdot(a, b, dims):
    return lax.dot_general(a, b, (dims, ((), ())), preferred_element_type=F32)


def _tri_sum(t_ref, v):
    r = _bdot(t_ref[...], jnp.concatenate(_split(v, 3), axis=1), NN)
    return r[:, :128] + r[:, 128:256] + r[:, 256:]


def _expand(v, e2_ref):
    return _bdot(jnp.concatenate(_split(v, 2), axis=1), e2_ref[...], NN)


def _reduce_heads(vals, eg):
    parts = []
    for v in vals:
        parts += _split(v, 2)
    r = _bdot(jnp.concatenate(parts, axis=0), eg, NT)
    return [r[2 * i * Q:(2 * i + 1) * Q] + r[(2 * i + 1) * Q:(2 * i + 2) * Q] for i in range(len(vals))]


def _ssd_common(xbc_ref, dtw_ref, arow_ref, t_ref, e_ref):
    dt = dtw_ref[:, :128]
    sig = dtw_ref[:, 128:]
    acs = _tri_sum(t_ref, dt * arow_ref[...])
    acs_x = _expand(acs, e_ref)
    dt_x = _expand(dt, e_ref)
    xs = xbc_ref[:, 0:DI]
    return sig, dt, acs, acs.T, acs_x, dt_x, xs


CONV_SLAB = 512


def _ssd_fwd(raw, dtp, cw, cb, arow, dsk_x, *, name):
    L = raw.shape[0]
    nc = L // Q
    tri, _, expand = _ssd_consts()

    def body(raw_ref, halo_ref, cw_ref, cb_ref, dtw_ref, arow_ref, dsk_ref, t_ref, e_ref,
             y_ref, hs_ref, xbc_ref, h_scr, cpad):
        c = pl.program_id(0)

        @pl.when(c == 0)
        def _():
            h_scr[...] = jnp.zeros_like(h_scr)

        cpad[0:8, :] = jnp.where(c > 0, halo_ref[8:16, :].astype(F32), 0.0)
        cpad[8:8 + Q, :] = raw_ref[...].astype(F32)
        for lo in range(0, XBC, CONV_SLAB):
            sl = slice(lo, lo + CONV_SLAB)
            acc = cb_ref[:, sl]
            for j in range(4):
                acc = acc + cpad[8 - j:8 - j + Q, sl] * cw_ref[3 - j:4 - j, sl]
            xbc_ref[:, sl] = acc * _sigmoid(acc)

        _, dt, acs, acs_t, acs_x, dt_x, xs = _ssd_common(xbc_ref, dtw_ref, arow_ref, t_ref, e_ref)
        xdt = xs * dt_x
        eacs = jnp.exp(acs_x)
        acs_last = acs_x[Q - 1:Q, :]
        dec = jnp.exp(acs_last - acs_x)
        hs_ref[0] = h_scr[...].astype(hs_ref.dtype)
        causal = lax.broadcasted_iota(jnp.int32, (Q, Q), 0) >= lax.broadcasted_iota(jnp.int32, (Q, Q), 1)
        first = lax.broadcasted_iota(jnp.int32, (Q, 128), 1) < HP
        for g in range(NG):
            bg = xbc_ref[:, DI + g * NS:DI + (g + 1) * NS]
            cg = xbc_ref[:, DI + NG * NS + g * NS:DI + NG * NS + (g + 1) * NS]
            s = _dot(cg, bg, NT)
            sl = slice(g * GW, (g + 1) * GW)
            hg = h_scr[:, sl]
            yoff = _dot(cg, hg, NN) * eacs[:, sl]
            st = _dot(bg, xdt[:, sl] * dec[:, sl], TN)
            h_scr[:, sl] = hg * eacs[Q - 1:Q, sl] + st
            for j in range(4):
                lo = g * GW + j * 128
                xb = xdt[:, lo:lo + 128]
                yp = yoff[:, j * 128:(j + 1) * 128] + dsk_ref[:, lo:lo + 128] * xs[:, lo:lo + 128]
                for e in range(2):
                    h = g * 8 + j * 2 + e
                    lm = jnp.exp(jnp.where(causal, acs[:, h:h + 1] - acs_t[h:h + 1, :], NEG))
                    xm = jnp.where(first if e == 0 else jnp.logical_not(first), xb, 0.0)
                    yp = yp + _dot(s * lm, xm, NN)
                y_ref[:, lo:lo + 128] = yp.astype(y_ref.dtype)

    const = lambda c: (0, 0)
    return pl.pallas_call(
        body, name=name, grid=(nc,),
        in_specs=[pl.BlockSpec((Q, XBC), lambda c: (c, 0)),
                  pl.BlockSpec((16, XBC), lambda c: (jnp.maximum(c * (Q // 16) - 1, 0), 0)),
                  pl.BlockSpec((4, XBC), const), pl.BlockSpec((1, XBC), const),
                  pl.BlockSpec((Q, DT_PAD), lambda c: (c, 0)),
                  pl.BlockSpec((1, 128), const), pl.BlockSpec((1, DI), const),
                  pl.BlockSpec((Q, Q), const), pl.BlockSpec((256, DI), const)],
        out_specs=[pl.BlockSpec((Q, DI), lambda c: (c, 0)), pl.BlockSpec((1, NS, DI), lambda c: (c, 0, 0)),
                   pl.BlockSpec((Q, XBC), lambda c: (c, 0))],
        out_shape=[jax.ShapeDtypeStruct((L, DI), BF16), jax.ShapeDtypeStruct((nc, NS, DI), F32),
                   jax.ShapeDtypeStruct((L, XBC), F32)],
        scratch_shapes=[pltpu.VMEM((NS, DI), F32), pltpu.VMEM((8 + Q, XBC), F32)],
        compiler_params=_params(("arbitrary",), VMEM_BIG))(raw, raw, cw, cb, dtp, arow, dsk_x, tri, expand)


def _ssd_bwd(dy, xbc, dtp, hs, arow, dsk_x, dproj, *, name):
    L = xbc.shape[0]
    nc = L // Q
    tri, triu, expand = _ssd_consts()

    def body(dy_ref, xbc_ref, dtw_ref, hs_ref, arow_ref, dsk_ref, t_ref, u_ref, e_ref, dp_in,
             dxbc_ref, ddtw_ref, da_ref, ddx_ref, ddtb_ref, dh_scr):
        del dp_in
        i = pl.program_id(0)

        @pl.when(i == 0)
        def _():
            dh_scr[...] = jnp.zeros_like(dh_scr)

        sig, dt, acs, acs_t, acs_x, dt_x, xs = _ssd_common(xbc_ref, dtw_ref, arow_ref, t_ref, e_ref)
        dyv = dy_ref[...]
        xdt = xs * dt_x
        eacs = jnp.exp(acs_x)
        acs_last = acs_x[Q - 1:Q, :]
        dec = jnp.exp(acs_last - acs_x)
        gy = dyv * eacs
        causal = lax.broadcasted_iota(jnp.int32, (Q, Q), 0) >= lax.broadcasted_iota(jnp.int32, (Q, Q), 1)
        first = lax.broadcasted_iota(jnp.int32, (Q, 128), 1) < HP
        lane_h = lax.broadcasted_iota(jnp.int32, (Q, 128), 1)
        sub_h = lax.broadcasted_iota(jnp.int32, (128, Q), 0)
        last_row = lax.broadcasted_iota(jnp.int32, (Q, GW), 0) == Q - 1
        dacs = jnp.zeros((Q, 128), F32)
        dacs_t = jnp.zeros((128, Q), F32)
        ddt = jnp.zeros((Q, 128), F32)
        for g in range(NG):
            bg = xbc_ref[:, DI + g * NS:DI + (g + 1) * NS]
            cg = xbc_ref[:, DI + NG * NS + g * NS:DI + NG * NS + (g + 1) * NS]
            s = _dot(cg, bg, NT)
            sl = slice(g * GW, (g + 1) * GW)
            hg = hs_ref[0, :, sl].astype(F32)
            dhn = dh_scr[:, sl]
            eal = eacs[Q - 1:Q, sl]
            gg = gy[:, sl]
            dax = gg * _dot(cg, hg, NN)
            dcg = _dot(gg, hg, NT)
            dh_scr[:, sl] = _dot(cg, gg, TN) + dhn * eal
            dal = eal * _colsum(dhn * hg)
            xdd = xdt[:, sl] * dec[:, sl]
            dbg = _dot(xdd, dhn, NT)
            wv = _dot(bg, dhn, NN)
            dd = wv * xdd
            dax = dax - dd
            dal = dal + _colsum(dd)
            dax = dax + jnp.where(last_row, dal, 0.0)
            dxdt_g = wv * dec[:, sl]
            ds = jnp.zeros((Q, Q), F32)
            dxdt_blocks = []
            for j in range(4):
                lo = g * GW + j * 128
                xb = xdt[:, lo:lo + 128]
                dyb = dyv[:, lo:lo + 128]
                dxb = dxdt_g[:, j * 128:(j + 1) * 128]
                for e in range(2):
                    h = g * 8 + j * 2 + e
                    lm = jnp.exp(jnp.where(causal, acs[:, h:h + 1] - acs_t[h:h + 1, :], NEG))
                    m = s * lm
                    dym = jnp.where(first if e == 0 else jnp.logical_not(first), dyb, 0.0)
                    dm = _dot(dym, xb, NT)
                    r = dm * m
                    dacs = dacs + jnp.where(lane_h == h, jnp.sum(r, axis=1, keepdims=True), 0.0)
                    dacs_t = dacs_t + jnp.where(sub_h == h, _colsum(r), 0.0)
                    ds = ds + dm * lm
                    dxb = dxb + _dot(m, dym, TN)
                dxdt_blocks.append(dxb)
            dxdt = jnp.concatenate(dxdt_blocks, axis=1)
            dcg = dcg + _dot(ds, bg, NN)
            dbg = dbg + _dot(ds, cg, TN)
            dxbc_ref[:, DI + g * NS:DI + (g + 1) * NS] = dbg
            dxbc_ref[:, DI + NG * NS + g * NS:DI + NG * NS + (g + 1) * NS] = dcg
            dxbc_ref[:, sl] = dsk_ref[:, sl] * dyv[:, sl] + dxdt * dt_x[:, sl]
            ddt_g, dacs_g = _reduce_heads([dxdt * xs[:, sl], dax], e_ref[0:128, sl])
            ddt = ddt + ddt_g
            dacs = dacs + dacs_g
        dacs = dacs - dacs_t.T
        ddta = _tri_sum(u_ref, dacs)
        ddt = ddt + ddta * arow_ref[...]
        ddtw = jnp.where(lane_h < NH, ddt * sig, 0.0)
        ddtw_ref[...] = jnp.concatenate([ddtw, jnp.zeros((Q, DT_PAD - 128), F32)], axis=1).astype(ddtw_ref.dtype)
        _acc_out(da_ref, _colsum(ddta * dt), i)
        _acc_out(ddx_ref, _colsum(dyv * xs), i)
        _acc_out(ddtb_ref, _colsum(ddtw), i)

    rev = lambda c: (nc - 1 - c, 0)
    const = lambda c: (0, 0)
    return pl.pallas_call(
        body, name=name, grid=(nc,),
        in_specs=[pl.BlockSpec((Q, DI), rev), pl.BlockSpec((Q, XBC), rev),
                  pl.BlockSpec((Q, DT_PAD), rev),
                  pl.BlockSpec((1, NS, DI), lambda c: (nc - 1 - c, 0, 0)),
                  pl.BlockSpec((1, 128), const), pl.BlockSpec((1, DI), const),
                  pl.BlockSpec((Q, Q), const), pl.BlockSpec((Q, Q), const), pl.BlockSpec((256, DI), const),
                  pl.BlockSpec(memory_space=pl.ANY)],
        out_specs=[pl.BlockSpec((Q, XBC), rev),
                   pl.BlockSpec((Q, DT_PAD), lambda c: (nc - 1 - c, C_DT // DT_PAD)),
                   pl.BlockSpec((1, 128), const), pl.BlockSpec((1, DI), const), pl.BlockSpec((1, 128), const)],
        out_shape=[jax.ShapeDtypeStruct((L, XBC), F32), jax.ShapeDtypeStruct((L, NPROJ), BF16),
                   jax.ShapeDtypeStruct((1, 128), F32), jax.ShapeDtypeStruct((1, DI), F32),
                   jax.ShapeDtypeStruct((1, 128), F32)],
        scratch_shapes=[pltpu.VMEM((NS, DI), F32)],
        input_output_aliases={9: 1},
        compiler_params=_params(("arbitrary",), VMEM_BIG))(dy, xbc, dtp, hs, arow, dsk_x, tri, triu,
                                                          expand, dproj)


def _adam_update(wv, gv, mv, vv):
    c1 = 1.0 - ADAM_B1 ** ADAM_STEP
    c2 = 1.0 - ADAM_B2 ** ADAM_STEP
    mn = ADAM_B1 * mv + (1.0 - ADAM_B1) * gv
    vn = ADAM_B2 * vv + (1.0 - ADAM_B2) * (gv * gv)
    return -ADAM_LR * ((mn / c1) / (jnp.sqrt(vn / c2) + ADAM_EPS) + ADAM_WD * wv), mn, vn


def _adamw(w, g, m, v, *, name, tr=None):
    R = w.shape[0]
    rest = tuple(w.shape[1:])
    if tr is None:
        tr = _pick(R, (256, 128, 64, 32, 16, 8))
    assert R % tr == 0

    def body(w_ref, g_ref, m_ref, v_ref, d_ref, mo_ref, vo_ref):
        d_ref[...], mo_ref[...], vo_ref[...] = _adam_update(w_ref[...], g_ref[...], m_ref[...], v_ref[...])

    zeros = (0,) * len(rest)
    spec = pl.BlockSpec((tr,) + rest, lambda i: (i,) + zeros)
    return pl.pallas_call(body, name=name, grid=(R // tr,), in_specs=[spec] * 4, out_specs=[spec] * 3,
                          out_shape=[jax.ShapeDtypeStruct(w.shape, F32)] * 3,
                          compiler_params=_params(("parallel",)))(w, g, m, v)


def _adamw_small(svrow, g_conv, params, *, name):
    n = len(params)

    def body(*refs):
        sv_ref, gc_ref = refs[0], refs[1]
        ins, outs = refs[2:2 + 3 * n], refs[2 + 3 * n:]
        for p, (key, w, _, _) in enumerate(params):
            w_ref, m_ref, v_ref = ins[3 * p:3 * p + 3]
            g_ref, d_ref, mo_ref, vo_ref = outs[4 * p:4 * p + 4]
            gv = gc_ref[...] if key == "conv_w" else sv_ref[:, SV_OFF[key]:SV_OFF[key] + w.shape[1]]
            g_ref[...] = gv
            d_ref[...], mo_ref[...], vo_ref[...] = _adam_update(w_ref[...], gv, m_ref[...], v_ref[...])

    vm = pl.BlockSpec(memory_space=pltpu.VMEM)
    args = [svrow, g_conv]
    shapes = []
    for _, w, m, v in params:
        args += [w, m, v]
        shapes += [jax.ShapeDtypeStruct(w.shape, F32)] * 4
    res = pl.pallas_call(body, name=name, in_specs=[vm] * len(args), out_specs=[vm] * len(shapes),
                         out_shape=shapes)(*args)
    return {key: tuple(res[4 * p:4 * p + 4]) for p, (key, _, _, _) in enumerate(params)}


def _slab_sum(recv, *, tile, name):
    rows = recv.shape[1]
    assert rows % tile == 0 and tile % 16 == 0

    def body(r_ref, o_ref):
        acc = r_ref[0].astype(F32)
        for j in range(1, N_DEV):
            acc = acc + r_ref[j].astype(F32)
        o_ref[...] = acc

    return pl.pallas_call(body, name=name, grid=(rows // tile,),
                          in_specs=[pl.BlockSpec((N_DEV, tile, D), lambda i: (0, i, 0))],
                          out_specs=pl.BlockSpec((tile, D), lambda i: (i, 0)),
                          out_shape=jax.ShapeDtypeStruct((rows, D), F32),
                          compiler_params=_params(("parallel",)))(recv)


MESH = pl.DeviceIdType.MESH


def _coords():
    return lax.axis_index("x"), lax.axis_index("y"), lax.axis_index("c")


def _peer(k):
    x, y, c = _coords()
    px = 1 - x if k & 4 else x
    py = 1 - y if k & 2 else y
    pc = 1 - c if k & 1 else c
    return (px, py, pc), 4 * px + 2 * py + pc


def _rcopy(src, dst, ssem, rsem, dev):
    return pltpu.make_async_remote_copy(src_ref=src, dst_ref=dst, send_sem=ssem, recv_sem=rsem,
                                        device_id=dev, device_id_type=MESH)


def _exchange_all(src_of, dst_slot, send_sems, recv_sems):
    x, y, c = _coords()
    me = 4 * x + 2 * y + c
    sent = []
    for k in range(1, N_DEV):
        dev, pidx = _peer(k)
        cp = _rcopy(src_of(pidx), dst_slot(me), send_sems.at[k - 1], recv_sems.at[k - 1], dev)
        cp.start()
        sent.append(cp)
    for k in range(1, N_DEV):
        dev, pidx = _peer(k)
        _rcopy(src_of(pidx), dst_slot(pidx), send_sems.at[k - 1], recv_sems.at[k - 1], dev).wait_recv()
    for cp in sent:
        cp.wait_send()


def _rows_of_slots(buf, nslots):
    rows = lax.broadcasted_iota(jnp.int32, (8, buf.shape[-1]), 0)
    out = jnp.zeros((8, buf.shape[-1]), F32)
    for j in range(nslots):
        out = out + jnp.where(rows == j, buf[j], 0.0)
    return out


def _exchange_start(src_of, dst_slot, send_sems, recv_sems):
    x, y, c = _coords()
    me = 4 * x + 2 * y + c
    sent = []
    for k in range(1, N_DEV):
        dev, pidx = _peer(k)
        cp = _rcopy(src_of(pidx), dst_slot(me), send_sems.at[k - 1], recv_sems.at[k - 1], dev)
        cp.start()
        sent.append(cp)
    return sent


def _exchange_finish(sent, src_of, dst_slot, send_sems, recv_sems):
    for k in range(1, N_DEV):
        dev, pidx = _peer(k)
        _rcopy(src_of(pidx), dst_slot(pidx), send_sems.at[k - 1], recv_sems.at[k - 1], dev).wait_recv()
    for cp in sent:
        cp.wait_send()


def _ada_gather(c, w_ada, b_r, slab, *, name):
    wloc = w_ada.shape[1]

    def body(c_ref, w_ref, b_ref, x_ref, mod_ref, call_ref, out_ref,
             csrc, cbuf, psrc, pbuf, s1, r1, s2, r2, send_sems, recv_sems, local_sem):
        x, y, cc = _coords()
        me_i = 4 * x + 2 * y + cc
        me, sibling = (x, y, cc), (x, y, 1 - cc)
        chips = [(1 - x, y), (x, 1 - y), (1 - x, 1 - y)]

        def slot(px, py, pc):
            return out_ref.at[4 * px + 2 * py + pc]

        def copy(k, block, to, src=None):
            return _rcopy(slot(*block) if src is None else src, slot(*block), send_sems.at[k], recv_sems.at[k], to)

        csrc[...] = jnp.broadcast_to(c_ref[...], (8, D))
        cbuf[me_i] = csrc[...]
        c_of, c_slot = (lambda p: csrc), (lambda s: cbuf.at[s])
        sent1 = _exchange_start(c_of, c_slot, s1, r1)

        mine = pltpu.make_async_copy(x_ref, slot(*me), local_sem)
        mine.start()
        first = [copy(0, me, sibling, src=x_ref)]
        first += [copy(1 + j, me, (*chip, cc), src=x_ref) for j, chip in enumerate(chips)]
        for cp in first:
            cp.start()

        _exchange_finish(sent1, c_of, c_slot, s1, r1)
        call = _rows_of_slots(cbuf, N_DEV)
        call_ref[...] = call
        prod = _dot_hi(_silu(call), w_ref[...])
        for b in range(N_DEV):
            psrc[b] = jnp.broadcast_to(prod[b:b + 1, :], (8, wloc))
        pbuf[me_i] = psrc[me_i]
        p_of, p_slot = (lambda p: psrc.at[p]), (lambda s: pbuf.at[s])
        sent2 = _exchange_start(p_of, p_slot, s2, r2)

        passed = [copy(4 + j, (*chip, cc), sibling) for j, chip in enumerate(chips)]
        for j, chip in enumerate(chips):
            copy(1 + j, (*chip, cc), me).wait_recv()
            passed[j].start()
        copy(0, sibling, me).wait_recv()
        for j, chip in enumerate(chips):
            copy(4 + j, (*chip, 1 - cc), me).wait_recv()

        _exchange_finish(sent2, p_of, p_slot, s2, r2)
        mod_ref[...] = _rows_of_slots(pbuf, N_DEV) + b_ref[...]
        for cp in first + passed:
            cp.wait_send()
        mine.wait()

    vm = pl.BlockSpec(memory_space=pltpu.VMEM)
    anyspec = pl.BlockSpec(memory_space=pl.ANY)
    return pl.pallas_call(
        body, name=name, in_specs=[vm, vm, vm, anyspec], out_specs=[vm, vm, anyspec],
        out_shape=[jax.ShapeDtypeStruct((N_DEV, wloc), F32), jax.ShapeDtypeStruct((N_DEV, D), F32),
                   jax.ShapeDtypeStruct((N_DEV,) + slab.shape, slab.dtype)],
        scratch_shapes=[pltpu.VMEM((8, D), F32), pltpu.VMEM((N_DEV, 8, D), F32),
                        pltpu.VMEM((N_DEV, 8, wloc), F32), pltpu.VMEM((N_DEV, 8, wloc), F32),
                        pltpu.SemaphoreType.DMA((N_DEV - 1,)), pltpu.SemaphoreType.DMA((N_DEV - 1,)),
                        pltpu.SemaphoreType.DMA((N_DEV - 1,)), pltpu.SemaphoreType.DMA((N_DEV - 1,)),
                        pltpu.SemaphoreType.DMA((7,)), pltpu.SemaphoreType.DMA((7,)), pltpu.SemaphoreType.DMA],
        compiler_params=pltpu.CompilerParams(vmem_limit_bytes=VMEM_BIG))(c, w_ada, b_r, slab)


_HBM =pl.BlockSpec(memory_space=pltpu.HBM)
_SEM = pl.BlockSpec(memory_space=pltpu.SEMAPHORE)
_EFFECT = pltpu.SideEffectType.DATAFLOW_SIDE_EFFECTING


def _xchg_src(src_ref, pidx, per_peer):
    return src_ref.at[pidx] if per_peer else src_ref


def _xchg_start(src, *, per_peer, name):
    rows = src.shape[-2]
    land_shape = (N_DEV, rows, D)

    def body(src_ref, land_ref, send_sems, recv_sems, src_thru, land_thru, token):
        del src_thru, land_thru
        x, y, c = _coords()
        me = 4 * x + 2 * y + c
        for k in range(1, N_DEV):
            dev, pidx = _peer(k)
            _rcopy(_xchg_src(src_ref, pidx, per_peer), land_ref.at[me], send_sems.at[k - 1],
                   recv_sems.at[k - 1], dev).start()
        token[...] = jnp.zeros_like(token)

    return pl.pallas_call(
        body, name=name,
        out_shape=(pltpu.SemaphoreType.DMA((N_DEV - 1,)), pltpu.SemaphoreType.DMA((N_DEV - 1,)),
                   pltpu.HBM(src.shape, src.dtype), pltpu.HBM(land_shape, src.dtype),
                   jax.ShapeDtypeStruct((8, 128), F32)),
        in_specs=(_HBM, _HBM),
        out_specs=(_SEM, _SEM, _HBM, _HBM, pl.BlockSpec(memory_space=pltpu.VMEM)),
        input_output_aliases={0: 2, 1: 3},
        compiler_params=pltpu.CompilerParams(has_side_effects=_EFFECT),
    )(pltpu.with_memory_space_constraint(src, pltpu.HBM),
      pltpu.with_memory_space_constraint(lax.empty(land_shape, src.dtype), pltpu.HBM))


def _xchg_wait(started, after, *, per_peer, name):
    send_sems, recv_sems, src_thru, land_thru, _ = started

    def body(src_ref, land_ref, send_sems, recv_sems, after_ref, src_dead, got_ref):
        del after_ref, src_dead, got_ref
        for k in range(1, N_DEV):
            dev, pidx = _peer(k)
            cp = _rcopy(_xchg_src(src_ref, pidx, per_peer), land_ref.at[pidx], send_sems.at[k - 1],
                        recv_sems.at[k - 1], dev)
            cp.wait_send()
            cp.wait_recv()

    return pl.pallas_call(
        body, name=name,
        out_shape=(pltpu.HBM(src_thru.shape, src_thru.dtype), pltpu.HBM(land_thru.shape, land_thru.dtype)),
        in_specs=(_HBM, _HBM, _SEM, _SEM, pl.BlockSpec(memory_space=pl.ANY)),
        out_specs=(_HBM, _HBM),
        input_output_aliases={0: 0, 1: 1},
        compiler_params=pltpu.CompilerParams(has_side_effects=_EFFECT),
    )(src_thru, land_thru, send_sems, recv_sems, after)


def _dep(token):
    return (token, (8, 128), lambda i, j, k: (0, 0))


def _small_allsum(sv, *, name):
    def body(sv_ref, all_ref, sum_ref, send_sems, recv_sems):
        x, y, c = _coords()
        me = 4 * x + 2 * y + c
        all_ref[me] = sv_ref[...]
        _exchange_all(lambda p: sv_ref, lambda s: all_ref.at[s], send_sems, recv_sems)
        acc = all_ref[0]
        for j in range(1, N_DEV):
            acc = acc + all_ref[j]
        sum_ref[...] = acc

    vm = pl.BlockSpec(memory_space=pltpu.VMEM)
    return pl.pallas_call(
        body, name=name, in_specs=[vm], out_specs=[vm, vm],
        out_shape=[jax.ShapeDtypeStruct((N_DEV, SV_ROWS, 128), F32), jax.ShapeDtypeStruct((SV_ROWS, 128), F32)],
        scratch_shapes=[pltpu.SemaphoreType.DMA((7,)), pltpu.SemaphoreType.DMA((7,))],
    )(sv)


def _ada_bwd(call, dmod_loc, *, name):
    wloc = dmod_loc.shape[1]

    def body(c_ref, d_ref, o_ref):
        o_ref[...] = _dot_hi(_silu(c_ref[...]), d_ref[...], TN)

    vm = pl.BlockSpec(memory_space=pltpu.VMEM)
    return pl.pallas_call(body, name=name, in_specs=[vm, vm], out_specs=vm,
                          out_shape=jax.ShapeDtypeStruct((D, wloc), F32),
                          compiler_params=pltpu.CompilerParams(vmem_limit_bytes=VMEM_BIG))(call, dmod_loc)


def _pad_rows(a, rows):
    return jnp.pad(a, ((0, rows - a.shape[0]), (0, 0)))


IN_SHIFT = tuple((IN_ROWS * j) % 16 for j in range(N_DEV))
IN_BASE = tuple(IN_ROWS * j - IN_SHIFT[j] for j in range(N_DEV))
IN_SEGMENTS = ((2048, XBC, C_XBC), (5152, 1024, C_POOL), (0, 2048, C_Z), (6176, 2048, C_GATE), (5120, 32, C_DT))


def _global_pieces(gs):
    pieces = []
    for j in range(N_DEV):
        lo, hi = 0, IN_ROWS_P
        if j > 0 and IN_BASE[j - 1] + IN_ROWS_P > IN_BASE[j]:
            pieces.append((IN_BASE[j], 16, gs[j - 1, IN_ROWS_P - 16:IN_ROWS_P] + gs[j, 0:16]))
            lo = 16
        if j + 1 < N_DEV and IN_BASE[j] + IN_ROWS_P > IN_BASE[j + 1]:
            hi = IN_ROWS_P - 16
        pieces.append((IN_BASE[j] + lo, hi - lo, gs[j, lo:hi]))
    return pieces


def _reorder_in_rows(gs):
    pieces = _global_pieces(gs)
    parts = []
    for lo, n, _ in IN_SEGMENTS:
        for p0, pn, arr in pieces:
            a, b = max(lo, p0), min(lo + n, p0 + pn)
            if a < b:
                parts.append(arr[a - p0:b - p0])
    parts.append(jnp.zeros((DT_PAD - 32, D), gs.dtype))
    return jnp.concatenate(parts, axis=0)


def _restore_in_shards(d):
    slabs = []
    for j in range(N_DEV):
        parts = []
        r, end = IN_BASE[j], IN_BASE[j] + IN_ROWS_P
        while r < end:
            lo, n, new = next(s for s in IN_SEGMENTS if s[0] <= r < s[0] + s[1])
            e = min(end, lo + n)
            parts.append(d[new + r - lo:new + e - lo])
            r = e
        slabs.append(jnp.concatenate(parts, axis=0))
    return jnp.stack(slabs, axis=0)


def _pack_sv(parts):
    flat = []
    for n, size in SV_PARTS:
        v = parts[n].reshape(-1).astype(F32)
        flat.append(jnp.pad(v, (0, size - v.shape[0])))
    v = jnp.concatenate(flat)
    return jnp.pad(v, (0, SV_ROWS * 128 - v.shape[0])).reshape(SV_ROWS, 128)


def _sv_get(flat, n, size):
    return flat[SV_OFF[n]:SV_OFF[n] + size]


def kernel(x, c, w_ada, b_ada, norm_mix_w, w_in, conv_w, conv_b, dt_bias, a_log, d_skip, ssd_norm_w, w_branch_ssd, pool_w, pool_scale, w_branch_pool, w_out, norm_mlp_w, w_up, w_down, norm_final_w, loss_target, m_w_ada, m_b_ada, m_norm_mix_w, m_w_in, m_conv_w, m_conv_b, m_dt_bias, m_a_log, m_d_skip, m_ssd_norm_w, m_w_branch_ssd, m_pool_w, m_pool_scale, m_w_branch_pool, m_w_out, m_norm_mlp_w, m_w_up, m_w_down, m_norm_final_w, v_w_ada, v_b_ada, v_norm_mix_w, v_w_in, v_conv_w, v_conv_b, v_dt_bias, v_a_log, v_d_skip, v_ssd_norm_w, v_w_branch_ssd, v_pool_w, v_pool_scale, v_w_branch_pool, v_w_out, v_norm_mlp_w, v_w_up, v_w_down, v_norm_final_w):
    xs_ = x[0]
    tgt = loss_target[0]
    L = xs_.shape[0]
    me = 4 * lax.axis_index("x") + 2 * lax.axis_index("y") + lax.axis_index("c")
    wloc = w_ada.shape[2]

    conv_bits = lax.bitcast_convert_type(conv_w[0], SLAB_DT).reshape(3, D)
    in_shift = (IN_ROWS * me) % 16
    slab_in = lax.dynamic_update_slice(jnp.zeros((IN_ROWS_P, D), SLAB_DT), w_in[0].T.astype(SLAB_DT),
                                       (in_shift, 0))
    slab_in = jnp.concatenate([slab_in, _pad_rows(conv_bits, CONV_ROWS)], axis=0)
    slab_rest = jnp.concatenate([
        w_branch_ssd[0].astype(SLAB_DT),
        pool_w[0].reshape(32, D).astype(SLAB_DT),
        w_branch_pool[0].astype(SLAB_DT),
        w_out[0].astype(SLAB_DT),
        w_up[0].T.astype(SLAB_DT),
        w_down[0].astype(SLAB_DT)], axis=0)
    mod_p, c_all, gs_in = _ada_gather(c, w_ada[0], b_ada.reshape(N_DEV, wloc), slab_in,
                                      name="ada_gather_w_in")
    mod = mod_p.reshape(6, D)
    shift_m, scale_m, gate_m, shift_f, scale_f, gate_f = [mod[i:i + 1] for i in range(6)]
    slab_rest, gs_in = lax.optimization_barrier((slab_rest, gs_in))
    rest_started = _xchg_start(slab_rest, per_peer=False, name="gather_rest_start")
    gather_token = rest_started[4]

    w_in_t = _reorder_in_rows(gs_in)
    conv_full = lax.bitcast_convert_type(
        gs_in[:, IN_ROWS_P:IN_ROWS_P + 3].reshape(N_DEV, 4, XBC // N_DEV, 2), F32)
    conv_full = conv_full.transpose(1, 0, 2).reshape(4, XBC)

    dtb = jnp.pad(dt_bias, ((0, 0), (0, 128 - NH)))
    arow = jnp.pad(-jnp.exp(a_log), ((0, 0), (0, 128 - NH)))
    dsk_x = jnp.repeat(d_skip, HP, axis=1)

    tm = _pick(L, (1024, 512, 256, 128))
    tm2 = _pick(L, (2048, 1024, 512, 256, 128))
    tkl = _pick(L, (4096, 2048, 1024, 512, 256, 128))
    tkl2 = _pick(L, (2048, 1024, 512, 256, 128))

    tmh = _pick(L, (512, 256, 128))
    tmq = _pick(L, (256, 128))
    zcol = C_Z // DI
    gcol = C_GATE // (2 * D)

    def whole_rows(w):
        return lambda t: ((L, w), BF16, (t, w), lambda i, j, k: (i, 0))

    def norm1_pro(x_ref, ex, outs, j):
        @pl.when(j == 0)
        def _():
            xv = x_ref[...]
            r = lax.rsqrt(jnp.mean(xv * xv, axis=-1, keepdims=True) + EPS)
            outs[1][...] = (xv * r * ex[0][...] * (1.0 + ex[1][...]) + ex[2][...]).astype(outs[1].dtype)

        return outs[1][...]

    def proj_ep(acc, ex, outs):
        outs[0][...] = acc.astype(outs[0].dtype)

        @pl.when(pl.program_id(1) == NPROJ // 768 - 1)
        def _():
            pre = acc[:, 768 - DT_PAD:768 - DT_PAD + 128] + ex[3][...]
            outs[2][...] = jnp.concatenate([_softplus(pre), _sigmoid(pre)], axis=1)

    proj, h1, dtp = _mm(
        xs_, w_in_t, "nt", name="in_proj", tm=tm2, tn=768, tk=D,
        extras=[(norm_mix_w, *_vecs()), (scale_m, *_vecs()), (shift_m, *_vecs()), (dtb, *_vecs(128)),
                _dep(gather_token)],
        outs=[BF16, whole_rows(D)(tm2), ((L, DT_PAD), F32, (tm2, DT_PAD), lambda i, j, k: (i, 0))],
        prologue=norm1_pro, epilogue=proj_ep)
    xbc_raw = proj
    y_ssm, hs, xbc = _ssd_fwd(xbc_raw, dtp, conv_full, conv_b, arow, dsk_x, name="ssd_fwd")

    slab_rest, gs = _xchg_wait(rest_started, y_ssm, per_peer=False, name="gather_rest_wait")
    gs = lax.dynamic_update_slice(gs, slab_rest[None], (me, 0, 0))

    def part(n, rows):
        return gs[:, REST_OFF[n]:REST_OFF[n] + rows]

    w_bssd = part("bssd", 256).reshape(DI, D)
    w_pool = part("pool", 32).reshape(N_DEV, 4, 32, PGW).transpose(1, 0, 2, 3).reshape(POOL_W, PGW)
    w_bpool = part("bpool", 128).reshape(POOL_W, D)
    w_o = part("out", 128).reshape(D, D)
    w_up_t = part("up", 512).reshape(DFF, D)
    w_dn = part("down", 512).reshape(DFF, D)

    def gnorm_pro(y_ref, ex, outs, j):
        z_ref, w_ref = ex
        yg = y_ref[...].astype(F32) * _silu(z_ref[...].astype(F32))
        segs = []
        for k in range(NG):
            sl = slice(k * GW, (k + 1) * GW)
            seg = yg[:, sl]
            r = lax.rsqrt(jnp.mean(seg * seg, axis=-1, keepdims=True) + EPS)
            segs.append((seg * r * w_ref[:, sl]).astype(BF16))
        yn_v = jnp.concatenate(segs, axis=1)
        outs[1][...] = yn_v
        return yn_v

    y_ssd, yn = _mm(y_ssm, w_bssd, "nn", name="branch_ssd", tm=tmh, tn=D, tk=DI,
                    extras=[(proj, *_rows(tmh, DI, zcol)), (ssd_norm_w, *_vecs(DI))],
                    outs=[BF16, whole_rows(DI)(tmh)], prologue=gnorm_pro)
    pooled = _pool_fwd(proj, name="pool_fwd")
    wp_spec = ((POOL_W, PGW), lambda i, j, k: (0, 0))

    def pool_pro(a_ref, ex, outs, j):
        wp_ref, s_ref = ex
        segs = []
        for g in range(4):
            sl = slice(g * PGW, (g + 1) * PGW)
            p = _dot(a_ref[:, sl], wp_ref[sl, :], NN)
            outs[1][:, sl] = p.astype(BF16)
            segs.append((p * s_ref[:, sl]).astype(BF16))
        yp1_v = jnp.concatenate(segs, axis=1)
        outs[2][...] = yp1_v
        return yp1_v

    y_pool, yp0, yp1 = _mm(pooled, w_bpool, "nn", name="branch_pool", tm=tm, tn=D, tk=D,
                           extras=[(w_pool, *wp_spec), (pool_scale, *_vecs())],
                           outs=[BF16, whole_rows(D)(tm), whole_rows(D)(tm)], prologue=pool_pro)

    def merge_pro(a_ref, ex, outs, j):
        s = _sigmoid(ex[1][...].astype(F32))
        mv = (s[:, :D] * a_ref[...].astype(F32) + s[:, D:] * ex[0][...].astype(F32)).astype(BF16)
        outs[3][...] = mv
        return mv

    mix, x1, h2, m = _mm(y_ssd, w_o, "nn", name="out_proj", tm=tmh, tn=D, tk=D,
                         extras=[(y_pool, *_rows(tmh)), (proj, *_rows(tmh, 2 * D, gcol)),
                                 (xs_, *_rows(tmh)), (gate_m, *_vecs()), (norm_mlp_w, *_vecs()),
                                 (scale_f, *_vecs()), (shift_f, *_vecs())],
                         outs=[BF16, F32, BF16, whole_rows(D)(tmh)], prologue=merge_pro,
                         epilogue=lambda acc, ex, outs: _ep_resid_norm(acc, ex[2:], outs[:3]))

    def relu2(acc, ex, outs):
        r = jnp.maximum(acc, 0.0)
        outs[0][...] = acc.astype(BF16)
        outs[1][...] = (r * r).astype(BF16)

    up, act = _mm(h2, w_up_t, "nt", name="mlp_up", outs=[BF16, BF16], tm=tmh, tn=DFF, tk=D, epilogue=relu2)

    dx2, ddown, loss_p, dnwf, dgate_f = _mm(
        act, w_dn, "nn", name="mlp_down", tm=tmh, tn=D, tk=DFF,
        extras=[(x1, *_rows(tmh)), (tgt, *_rows(tmh)), (gate_f, *_vecs()), (norm_final_w.reshape(1, D), *_vecs())],
        outs=[F32, BF16, _sum_out(128), _sum_out(), _sum_out()], epilogue=_ep_final)

    def drelu2(acc, ex, outs):
        outs[0][...] = (acc * (2.0 * jnp.maximum(ex[0][...].astype(F32), 0.0))).astype(BF16)

    def dep_last(ep):
        return lambda acc, ex, outs: ep(acc, ex[:-1], outs)

    dup = _mm(ddown, w_dn, "nt", name="mlp_down_dx", outs=[BF16], tm=tmh, tn=DFF, tk=D,
              extras=[(up, (tmh, DFF), lambda i, j, k: (i, j))], epilogue=drelu2)
    g_dn = _mm(act, ddown, "tn", name="mlp_down_dw", outs=[SLAB_DT], tm=1024, tn=D, tk=tkl)
    g_up_t = _mm(dup, h2, "tn", name="mlp_up_dw", outs=[SLAB_DT], tm=1024, tn=D, tk=tkl)
    gslab_mlp = jnp.concatenate([g_up_t.reshape(N_DEV, 512, D), g_dn.reshape(N_DEV, 512, D)], axis=1)
    mlp_started = _xchg_start(gslab_mlp, per_peer=True, name="scatter_mlp_start")
    dx1, p2, q2, dmix, dgate_m = _mm(
        dup, w_up_t, "nn", name="mlp_up_dx", tm=tmh, tn=D, tk=DFF,
        extras=[(x1, *_rows(tmh)), (dx2, *_rows(tmh)), (norm_mlp_w, *_vecs()), (scale_f, *_vecs()),
                (mix, *_rows(tmh)), (gate_m, *_vecs()), _dep(mlp_started[4])],
        outs=[F32, _sum_out(), _sum_out(), BF16, _sum_out()], epilogue=dep_last(_ep_norm_bwd))
    gcol = C_GATE // (2 * D)
    dy_ssd, dy_pool, dproj = _mm(
        dmix, w_o, "nt", name="out_proj_dx", tm=tmh, tn=D, tk=D,
        extras=[(y_ssd, *_rows(tmh)), (y_pool, *_rows(tmh)), (proj, *_rows(tmh, 2 * D, gcol))],
        outs=[BF16, BF16, ((L, NPROJ), BF16, *_rows(tmh, 2 * D, gcol))], epilogue=_ep_merge_bwd)
    g_o = _mm(m, dmix, "tn", name="out_proj_dw", outs=[SLAB_DT], tm=D, tn=D, tk=tkl)
    zcol = C_Z // DI
    dy_ssm, dproj, d_snw = _mm(
        dy_ssd, w_bssd, "nt", name="branch_ssd_dx", tm=tmh, tn=DI, tk=D,
        extras=[(y_ssm, *_rows(tmh, DI)), (proj, *_rows(tmh, DI, zcol)), (ssd_norm_w, *_vecs(DI)),
                (dproj, None, None)],
        outs=[F32, ((L, NPROJ), BF16, *_rows(tmh, DI, zcol)), _sum_out(DI)],
        epilogue=_ep_gated_norm_bwd, aliases={3: 1})
    g_bssd = _mm(yn, dy_ssd, "tn", name="branch_ssd_dw", outs=[SLAB_DT], tm=1024, tn=D, tk=tkl)
    dxbc, dproj, d_a, d_dx, d_dtb = _ssd_bwd(dy_ssm, xbc, dtp, hs, arow, dsk_x, dproj, name="ssd_bwd")
    dproj, d_cw, d_cb = _conv_bwd(xbc_raw, dxbc, conv_full, conv_b, dproj, name="conv_bwd")
    def pool_bwd_ep(acc, ex, outs):
        y_ref, s_ref, wp_ref = ex
        o_ref, ds_ref, dpool_ref = outs
        dyp0_v = (acc * s_ref[...]).astype(BF16)
        o_ref[...] = dyp0_v
        _acc_out(ds_ref, _colsum(acc * y_ref[...].astype(F32)), _row_step())
        for g in range(4):
            sl = slice(g * PGW, (g + 1) * PGW)
            dpool_ref[:, sl] = _dot(dyp0_v[:, sl], wp_ref[sl, :], NT)

    dyp0, d_ps, dpooled = _mm(dy_pool, w_bpool, "nt", name="branch_pool_dx", tm=tm, tn=D, tk=D,
                              extras=[(yp0, *_rows(tm)), (pool_scale, *_vecs()), (w_pool, *wp_spec)],
                              outs=[BF16, _sum_out(), F32], epilogue=pool_bwd_ep)
    g_bpool = _mm(yp1, dy_pool, "tn", name="branch_pool_dw", outs=[SLAB_DT], tm=D, tn=D, tk=tkl)
    g_pool = _mm_pool_tn(pooled, dyp0, name="pool_mix_dw", tk=tkl)
    gslab_mix = jnp.concatenate([
        g_bssd.reshape(N_DEV, 256, D),
        g_pool.reshape(4, N_DEV, 32, PGW).transpose(1, 0, 2, 3).reshape(N_DEV, 32, D).astype(SLAB_DT),
        g_bpool.reshape(N_DEV, 128, D),
        g_o.reshape(N_DEV, 128, D)], axis=1)
    mix_started = _xchg_start(gslab_mix, per_peer=True, name="scatter_mix_start")
    dproj = _pool_bwd(dpooled, dproj, name="pool_bwd")
    g_in_t = _mm(dproj, h1, "tn", name="in_proj_dw", outs=[SLAB_DT], tm=tmq, tn=D, tk=L,
                 extras=[_dep(mix_started[4])])
    gslab_in = _restore_in_shards(g_in_t)
    in_started = _xchg_start(gslab_in, per_peer=True, name="scatter_in_start")
    grad_x, p1, q1 = _mm(
        dproj, w_in_t, "nn", name="in_proj_dx", tm=tmq, tn=D, tk=NPROJ,
        extras=[(xs_, *_rows(tmq)), (dx1, *_rows(tmq)), (norm_mix_w, *_vecs()), (scale_m, *_vecs()),
                _dep(in_started[4])],
        outs=[F32, _sum_out(), _sum_out()], epilogue=dep_last(_ep_norm_bwd))

    def landed(started, after, tile, name):
        src, land = _xchg_wait(started, after, per_peer=True, name=name + "_wait")
        own = lax.dynamic_slice_in_dim(src, me, 1, axis=0)
        return _slab_sum(lax.dynamic_update_slice(land, own, (me, 0, 0)), tile=tile, name=name + "_sum")

    gsum_mlp = landed(mlp_started, grad_x, 256, "scatter_mlp")
    gsum_mix = landed(mix_started, grad_x, 272, "scatter_mix")
    gsum_in = landed(in_started, grad_x, 208, "scatter_in")

    dmod = jnp.concatenate([q1, p1 * norm_mix_w, dgate_m, q2, p2 * norm_mlp_w, dgate_f], axis=1)
    d_alog = d_a[:, :NH] * (-jnp.exp(a_log))
    sv = _pack_sv({
        "b_ada": dmod, "norm_mix_w": p1 * (1.0 + scale_m), "conv_b": d_cb, "dt_bias": d_dtb[:, :NH],
        "a_log": d_alog, "d_skip": d_dx.reshape(NH, HP).sum(axis=1), "ssd_norm_w": d_snw,
        "pool_scale": d_ps, "norm_mlp_w": p2 * (1.0 + scale_f), "norm_final_w": dnwf, "conv_w": d_cw,
        "loss": loss_p[:, :1]})
    sv_all, sv_sum = _small_allsum(sv, name="small_allsum")
    flat = sv_sum.reshape(-1)
    loss = flat[SV_OFF["loss"]]
    dmod_all = sv_all.reshape(N_DEV, SV_ROWS * 128)[:, :6 * D]
    g_w_ada = _ada_bwd(c_all, lax.dynamic_slice_in_dim(dmod_all, me * wloc, wloc, axis=1), name="ada_bwd")

    g_conv_w = lax.dynamic_slice_in_dim(_sv_get(flat, "conv_w", 4 * XBC).reshape(4, XBC),
                                        me * (XBC // N_DEV), XBC // N_DEV, axis=1)
    small = [("b_ada", b_ada, m_b_ada, v_b_ada), ("norm_mix_w", norm_mix_w, m_norm_mix_w, v_norm_mix_w),
             ("conv_b", conv_b, m_conv_b, v_conv_b), ("dt_bias", dt_bias, m_dt_bias, v_dt_bias),
             ("a_log", a_log, m_a_log, v_a_log), ("d_skip", d_skip, m_d_skip, v_d_skip),
             ("ssd_norm_w", ssd_norm_w, m_ssd_norm_w, v_ssd_norm_w),
             ("pool_scale", pool_scale, m_pool_scale, v_pool_scale),
             ("norm_mlp_w", norm_mlp_w, m_norm_mlp_w, v_norm_mlp_w),
             ("norm_final_w", norm_final_w[None], m_norm_final_w[None], v_norm_final_w[None]),
             ("conv_w", conv_w[0], m_conv_w[0], v_conv_w[0])]
    small_out = _adamw_small(sv_sum.reshape(1, SV_ROWS * 128), g_conv_w, small, name="adamw_small")
    small_out["norm_final_w"] = tuple(a[0] for a in small_out["norm_final_w"])
    small_out["conv_w"] = tuple(a[None] for a in small_out["conv_w"])

    def gpart(n, rows_):
        return gsum_mix[MIX_OFF[n]:MIX_OFF[n] + rows_]

    def lin(a):
        return a[0].T.reshape(IN_ROWS * 8, 128)

    g_lin = lax.dynamic_slice_in_dim(gsum_in, in_shift, IN_ROWS, axis=0).reshape(IN_ROWS * 8, 128)
    dlt, mn, vn = _adamw(lin(w_in), g_lin, lin(m_w_in), lin(v_w_in), name="adamw_w_in", tr=IN_ROWS * 2)
    big_in = tuple(a.reshape(IN_ROWS, D).T[None] for a in (g_lin, dlt, mn, vn))

    big = {
        "w_ada": (w_ada, m_w_ada, v_w_ada, g_w_ada, (D, wloc)),
        "w_branch_ssd": (w_branch_ssd, m_w_branch_ssd, v_w_branch_ssd, gpart("bssd", 256), (256, D)),
        "pool_w": (pool_w, m_pool_w, v_pool_w, gpart("pool", 32).reshape(128, PGW), (128, PGW)),
        "w_branch_pool": (w_branch_pool, m_w_branch_pool, v_w_branch_pool, gpart("bpool", 128), (128, D)),
        "w_out": (w_out, m_w_out, v_w_out, gpart("out", 128), (128, D)),
        "w_up": (w_up, m_w_up, v_w_up, gsum_mlp[:512].T, (D, 512)),
        "w_down": (w_down, m_w_down, v_w_down, gsum_mlp[512:], (512, D)),
    }
    big_out = {}
    for n, (w, mm_, vv, g, shp2) in big.items():
        dlt, mn, vn = _adamw(w.reshape(shp2), g, mm_.reshape(shp2), vv.reshape(shp2), name="adamw_" + n)
        big_out[n] = (g.reshape(w.shape), dlt.reshape(w.shape), mn.reshape(w.shape), vn.reshape(w.shape))

    order = ["w_ada", "b_ada", "norm_mix_w", "w_in", "conv_w", "conv_b", "dt_bias", "a_log", "d_skip",
             "ssd_norm_w", "w_branch_ssd", "pool_w", "pool_scale", "w_branch_pool", "w_out", "norm_mlp_w",
             "w_up", "w_down", "norm_final_w"]
    big_out["w_in"] = big_in
    res = {**small_out, **big_out}
    outs = [loss, grad_x.reshape(x.shape)]
    for k in range(4):
        outs += [res[n][k] for n in order]
    return tuple(outs)
```

```python
import functools

import numpy as np
import jax
import jax.numpy as jnp
from jax import lax
from jax.experimental import pallas as pl
from jax.experimental.pallas import tpu as pltpu

F32 = jnp.float32
BF16 = jnp.bfloat16
SLAB_DT = jnp.bfloat16
_MXU_DTYPE = jnp.bfloat16

N_DEV = 8
D = 1024
DI = 2048
NH = 32
HP = 64
NG = 4
NS = 128
Q = 128
XBC = DI + 2 * NG * NS
DFF = 4096
N_IN = 8224
EPS = 1e-5
POOL_W = 1024
PGW = 256

C_XBC, C_POOL, C_Z, C_GATE, C_DT = 0, 3072, 4096, 6144, 8192
DT_PAD = 256
NPROJ = C_DT + DT_PAD

IN_ROWS = N_IN // N_DEV
IN_ROWS_P = 1040
CONV_ROWS = 16
REST_PARTS = (("bssd", 256), ("pool", 32), ("bpool", 128), ("out", 128), ("up", 512), ("down", 512))
REST_OFF = {}
_o = 0
for _n, _r in REST_PARTS:
    REST_OFF[_n] = _o
    _o += _r
REST_ROWS = _o
MIX_PARTS = (("bssd", 256), ("pool", 32), ("bpool", 128), ("out", 128))
MIX_OFF = {}
_o = 0
for _n, _r in MIX_PARTS:
    MIX_OFF[_n] = _o
    _o += _r
MIX_ROWS = _o

SV_PARTS = (("b_ada", 6144), ("norm_mix_w", 1024), ("conv_b", 3072), ("dt_bias", 128), ("a_log", 128),
            ("d_skip", 128), ("ssd_norm_w", 2048), ("pool_scale", 1024), ("norm_mlp_w", 1024),
            ("norm_final_w", 1024), ("conv_w", 4 * XBC), ("loss", 128))
SV_OFF = {}
_o = 0
for _n, _r in SV_PARTS:
    SV_OFF[_n] = _o
    _o += _r
SV_ROWS = 224
assert _o <= SV_ROWS * 128

ADAM_LR, ADAM_B1, ADAM_B2, ADAM_EPS, ADAM_WD, ADAM_STEP = 0.001, 0.9, 0.999, 1e-08, 0.01, 10

VMEM_BIG = 56 * 1024 * 1024
NEG = -1e30

NN = ((1,), (0,))
NT = ((1,), (1,))
TN = ((0,), (0,))


def _dot(a, b, dims=NN):
    return lax.dot_general(a.astype(_MXU_DTYPE), b.astype(_MXU_DTYPE), (dims, ((), ())),
                           preferred_element_type=F32)


def _dot_hi(a, b, dims=NN):
    return lax.dot_general(a.astype(F32), b.astype(F32), (dims, ((), ())),
                           precision=lax.Precision.HIGHEST, preferred_element_type=F32)


def _pick(n, cands):
    for c in cands:
        if n % c == 0:
            return c
    return n


def _sigmoid(x):
    return 1.0 / (1.0 + jnp.exp(-x))


def _silu(x):
    return x * _sigmoid(x)


def _dsilu(x):
    s = _sigmoid(x)
    return s * (1.0 + x * (1.0 - s))


def _softplus(x):
    return jnp.maximum(x, 0.0) + jnp.log(1.0 + jnp.exp(-jnp.abs(x)))


def _params(sem, vmem=None):
    return pltpu.CompilerParams(dimension_semantics=sem, vmem_limit_bytes=vmem)


def _row_step():
    return pl.program_id(0)


def _mm(a, b, mode, *, name, outs, tm, tn, tk, extras=(), epilogue=None, aliases=None, prologue=None):
    if mode == "tn":
        K, M = a.shape
        N = b.shape[1]
        a_spec = pl.BlockSpec((tk, tm), lambda i, j, k: (k, i))
        b_spec = pl.BlockSpec((tk, tn), lambda i, j, k: (k, j))
        dims = TN
    else:
        M = a.shape[0]
        K = b.shape[0] if mode == "nn" else b.shape[1]
        if prologue is None:
            assert a.shape[1] == K
            a_spec = pl.BlockSpec((tm, tk), lambda i, j, k: (i, k))
        else:
            assert tk == K
            a_spec = pl.BlockSpec((tm, a.shape[1]), lambda i, j, k: (i, 0))
        if mode == "nn":
            N = b.shape[1]
            b_spec = pl.BlockSpec((tk, tn), lambda i, j, k: (k, j))
            dims = NN
        else:
            N = b.shape[0]
            b_spec = pl.BlockSpec((tn, tk), lambda i, j, k: (j, k))
            dims = NT
    assert M % tm == 0 and N % tn == 0 and K % tk == 0, (name, M, N, K, tm, tn, tk)
    nk = K // tk
    ne, no = len(extras), len(outs)
    if epilogue is None:
        def epilogue(acc, ex, out_refs):
            out_refs[0][...] = acc.astype(out_refs[0].dtype)

    def body(a_ref, b_ref, *rest):
        ex, out_refs = rest[:ne], rest[ne:ne + no]
        lhs = a_ref[...] if prologue is None else prologue(a_ref, ex, out_refs, pl.program_id(1))
        p = _dot(lhs, b_ref[...], dims)
        if nk == 1:
            epilogue(p, ex, out_refs)
        else:
            acc = rest[-1]
            k = pl.program_id(2)

            @pl.when(k == 0)
            def _():
                acc[...] = p

            @pl.when(jnp.logical_and(k > 0, k < nk - 1))
            def _():
                acc[...] += p

            @pl.when(k == nk - 1)
            def _():
                epilogue(acc[...] + p, ex, out_refs)

    out_specs, out_shape = [], []
    for o in outs:
        if isinstance(o, tuple):
            shape, dt, bs, im = o
            out_specs.append(pl.BlockSpec(bs, im))
            out_shape.append(jax.ShapeDtypeStruct(shape, dt))
        else:
            out_specs.append(pl.BlockSpec((tm, tn), lambda i, j, k: (i, j)))
            out_shape.append(jax.ShapeDtypeStruct((M, N), o))
    in_specs = [a_spec, b_spec]
    for _, bs, im in extras:
        in_specs.append(pl.BlockSpec(memory_space=pl.ANY) if bs is None else pl.BlockSpec(bs, im))
    res = pl.pallas_call(
        body, name=name,
        grid=(M // tm, N // tn, nk),
        in_specs=in_specs, out_specs=out_specs, out_shape=out_shape,
        scratch_shapes=[pltpu.VMEM((tm, tn), F32)] if nk > 1 else [],
        input_output_aliases={2 + e: o for e, o in (aliases or {}).items()},
        compiler_params=_params(("arbitrary", "arbitrary", "arbitrary"), VMEM_BIG),
    )(a, b, *[e[0] for e in extras])
    return res if no > 1 else res[0]


def _rows(tm, w=D, col=0):
    return (tm, w), lambda i, j, k, c=col: (i, c)


def _vecs(w=D, col=0):
    return (1, w), lambda i, j, k, c=col: (0, c)


def _sum_out(w=D):
    return ((1, w), F32, (1, w), lambda i, j, k: (0, 0))


def _mm_pool_tn(a, b, *, name, tk):
    L = a.shape[0]

    def body(a_ref, b_ref, o_ref):
        p = _dot(a_ref[...], b_ref[...], TN)

        @pl.when(pl.program_id(1) == 0)
        def _():
            o_ref[...] = p

        @pl.when(pl.program_id(1) > 0)
        def _():
            o_ref[...] += p

    blk = pl.BlockSpec((tk, PGW), lambda g, k: (k, g))
    return pl.pallas_call(body, name=name, grid=(4, L // tk), in_specs=[blk, blk],
                          out_specs=pl.BlockSpec((PGW, PGW), lambda g, k: (g, 0)),
                          out_shape=jax.ShapeDtypeStruct((POOL_W, PGW), F32),
                          compiler_params=_params(("parallel", "arbitrary")))(a, b)


def _acc_out(ref, val, i):
    @pl.when(i == 0)
    def _():
        ref[...] = val

    @pl.when(i > 0)
    def _():
        ref[...] += val


def _colsum(v):
    return jnp.sum(v, axis=0, keepdims=True)


def _ep_resid_norm(acc, ex, outs):
    x_ref, g_ref, nw_ref, sc_ref, sh_ref = ex
    mix_ref, x1_ref, h_ref = outs
    mix_ref[...] = acc.astype(mix_ref.dtype)
    xv = x_ref[...] + g_ref[...] * acc
    x1_ref[...] = xv
    r = lax.rsqrt(jnp.mean(xv * xv, axis=-1, keepdims=True) + EPS)
    h_ref[...] = (xv * r * nw_ref[...] * (1.0 + sc_ref[...]) + sh_ref[...]).astype(h_ref.dtype)


def _ep_final(acc, ex, outs):
    x1_ref, t_ref, g_ref, nw_ref = ex
    dx2_ref, dd_ref, loss_ref, dnw_ref, dg_ref = outs
    i = _row_step()
    x2 = x1_ref[...] + g_ref[...] * acc
    r = lax.rsqrt(jnp.mean(x2 * x2, axis=-1, keepdims=True) + EPS)
    xh = x2 * r
    e = xh * nw_ref[...] - t_ref[...]
    part = 0.5 * jnp.sum(jnp.mean(e * e, axis=-1, keepdims=True), axis=0, keepdims=True)
    dy = e * (1.0 / D)
    g = dy * nw_ref[...]
    dx2 = r * (g - xh * jnp.mean(g * xh, axis=-1, keepdims=True))
    dx2_ref[...] = dx2
    dd_ref[...] = (dx2 * g_ref[...]).astype(dd_ref.dtype)
    _acc_out(loss_ref, jnp.broadcast_to(part, (1, 128)), i)
    _acc_out(dnw_ref, _colsum(dy * xh), i)
    _acc_out(dg_ref, _colsum(dx2 * acc), i)


def _ep_norm_bwd(acc, ex, outs):
    x_ref, dr_ref, nw_ref, sc_ref = ex[:4]
    dx_ref, p_ref, q_ref = outs[:3]
    i = _row_step()
    xv = x_ref[...]
    r = lax.rsqrt(jnp.mean(xv * xv, axis=-1, keepdims=True) + EPS)
    xh = xv * r
    g = acc * (nw_ref[...] * (1.0 + sc_ref[...]))
    dx = dr_ref[...] + r * (g - xh * jnp.mean(g * xh, axis=-1, keepdims=True))
    dx_ref[...] = dx
    _acc_out(p_ref, _colsum(acc * xh), i)
    _acc_out(q_ref, _colsum(acc), i)
    if len(ex) > 4:
        m_ref, g_ref = ex[4:]
        dm_ref, dg_ref = outs[3:]
        dm_ref[...] = (dx * g_ref[...]).astype(dm_ref.dtype)
        _acc_out(dg_ref, _colsum(dx * m_ref[...].astype(F32)), i)


def _ep_merge_bwd(acc, ex, outs):
    a_ref, b_ref, gl_ref = ex
    da_ref, db_ref, dgl_ref = outs
    s = _sigmoid(gl_ref[...].astype(F32))
    s1, s2 = s[:, :D], s[:, D:]
    da_ref[...] = (acc * s1).astype(da_ref.dtype)
    db_ref[...] = (acc * s2).astype(db_ref.dtype)
    dgl_ref[:, :D] = (acc * a_ref[...].astype(F32) * s1 * (1.0 - s1)).astype(dgl_ref.dtype)
    dgl_ref[:, D:] = (acc * b_ref[...].astype(F32) * s2 * (1.0 - s2)).astype(dgl_ref.dtype)


GW = DI // NG


def _ep_gated_norm_bwd(acc, ex, outs):
    y_ref, z_ref, w_ref, _ = ex
    dy_ref, dz_ref, dw_ref = outs
    zv = z_ref[...].astype(F32)
    yv = y_ref[...].astype(F32)
    sg = _sigmoid(zv)
    sz = zv * sg
    yg = yv * sz
    dsz = sg * (1.0 + zv * (1.0 - sg))
    dws = []
    for k in range(NG):
        sl = slice(k * GW, (k + 1) * GW)
        seg = yg[:, sl]
        r = lax.rsqrt(jnp.mean(seg * seg, axis=-1, keepdims=True) + EPS)
        sh = seg * r
        dn = acc[:, sl]
        g = dn * w_ref[:, sl]
        dyg = r * (g - sh * jnp.mean(g * sh, axis=-1, keepdims=True))
        dy_ref[:, sl] = dyg * sz[:, sl]
        dz_ref[:, sl] = (dyg * yv[:, sl] * dsz[:, sl]).astype(dz_ref.dtype)
        dws.append(_colsum(dn * sh))
    _acc_out(dw_ref, jnp.concatenate(dws, axis=1), _row_step())


CONV_CB = 128
HALO = 16


def _time_chunk(L):
    return _pick(L, (256, 128))


def _with_halo(x_ref, i, r0, rc):
    p0 = pl.multiple_of(jnp.maximum(r0 - HALO, 0), HALO)
    prev = jnp.where(i > 0, x_ref[pl.ds(p0, HALO), :].astype(F32), 0.0)
    return jnp.concatenate([prev, x_ref[pl.ds(r0, rc), :].astype(F32)], axis=0)


def _conv_bwd(proj, dy, w, b, dproj, *, name):
    L = proj.shape[0]
    rc = _time_chunk(L)
    n = L // rc

    def body(x_ref, dy_ref, w_ref, b_ref, dp_in, dx_ref, dw_ref, db_ref, xpad, dpad):
        del dp_in
        wv = w_ref[...]
        bv = b_ref[...]
        dpad[rc:rc + HALO, :] = jnp.zeros((HALO, CONV_CB), F32)

        def step(k, carry):
            db, d0, d1, d2, d3 = carry
            i = n - 1 - k
            r0 = pl.multiple_of(i * rc, rc)
            p0 = pl.multiple_of(jnp.maximum(r0 - HALO, 0), HALO)
            xpad[0:HALO, :] = jnp.where(i > 0, x_ref[pl.ds(p0, HALO), :].astype(F32), 0.0)
            xpad[HALO:HALO + rc, :] = x_ref[pl.ds(r0, rc), :].astype(F32)
            xk = [xpad[HALO - j:HALO - j + rc, :] for j in range(4)]
            pre = bv
            for j in range(4):
                pre = pre + xk[j] * wv[3 - j:4 - j]
            dpre = dy_ref[pl.ds(r0, rc), :] * _dsilu(pre)
            dpad[0:rc, :] = dpre
            acc = dpre * wv[3:4]
            for j in (1, 2, 3):
                acc = acc + dpad[j:j + rc, :] * wv[3 - j:4 - j]
            dx_ref[pl.ds(r0, rc), :] = acc.astype(dx_ref.dtype)
            dpad[rc:rc + HALO, :] = dpre[:HALO]
            return (db + _colsum(dpre), d0 + _colsum(dpre * xk[3]), d1 + _colsum(dpre * xk[2]),
                    d2 + _colsum(dpre * xk[1]), d3 + _colsum(dpre * xk[0]))

        z = jnp.zeros((1, CONV_CB), F32)
        db, d0, d1, d2, d3 = lax.fori_loop(0, n, step, (z, z, z, z, z))
        db_ref[...] = db
        dw_ref[...] = jnp.concatenate([d0, d1, d2, d3], axis=0)

    nb = XBC // CONV_CB
    return pl.pallas_call(
        body, name=name, grid=(nb,),
        in_specs=[pl.BlockSpec((L, CONV_CB), lambda j: (0, j + C_XBC // CONV_CB)),
                  pl.BlockSpec((L, CONV_CB), lambda j: (0, j)),
                  pl.BlockSpec((4, CONV_CB), lambda j: (0, j)), pl.BlockSpec((1, CONV_CB), lambda j: (0, j)),
                  pl.BlockSpec(memory_space=pl.ANY)],
        out_specs=[pl.BlockSpec((L, CONV_CB), lambda j: (0, j + C_XBC // CONV_CB)),
                   pl.BlockSpec((4, CONV_CB), lambda j: (0, j)), pl.BlockSpec((1, CONV_CB), lambda j: (0, j))],
        out_shape=[jax.ShapeDtypeStruct((L, NPROJ), BF16), jax.ShapeDtypeStruct((4, XBC), F32),
                   jax.ShapeDtypeStruct((1, XBC), F32)],
        scratch_shapes=[pltpu.VMEM((rc + HALO, CONV_CB), F32), pltpu.VMEM((rc + HALO, CONV_CB), F32)],
        input_output_aliases={4: 0},
        compiler_params=_params(("parallel",), VMEM_BIG))(proj, dy, w, b, dproj)


def _pool_fwd(proj, *, name):
    L = proj.shape[0]
    rc = _time_chunk(L)
    n = L // rc

    def body(x_ref, o_ref, pad):
        g = pl.program_id(0)
        pad[0:HALO, :] = jnp.zeros((HALO, PGW), F32)

        def fill(i, c):
            r0 = pl.multiple_of(i * rc, rc)
            pad[pl.ds(r0 + HALO, rc), :] = x_ref[pl.ds(r0, rc), :].astype(F32)
            return c

        lax.fori_loop(0, n, fill, 0)
        rows = lax.broadcasted_iota(jnp.int32, (rc, PGW), 0)

        for gi in range(4):
            win = 2 << gi

            @pl.when(g == gi)
            def _(gi=gi, win=win):
                def step(i, c):
                    r0 = pl.multiple_of(i * rc, rc)
                    ext = pad[pl.ds(r0, rc + HALO), :]
                    s = ext
                    sh = 1
                    while sh < win:
                        s = s + pltpu.roll(s, sh, 0)
                        sh *= 2
                    cnt = jnp.minimum(rows + (r0 + 1), win).astype(F32)
                    o_ref[pl.ds(r0, rc), :] = (s[HALO:] / cnt - ext[HALO:]).astype(o_ref.dtype)
                    return c

                lax.fori_loop(0, n, step, 0)

    return pl.pallas_call(
        body, name=name, grid=(4,),
        in_specs=[pl.BlockSpec((L, PGW), lambda j: (0, j + C_POOL // PGW))],
        out_specs=pl.BlockSpec((L, PGW), lambda j: (0, j)),
        out_shape=jax.ShapeDtypeStruct((L, POOL_W), BF16),
        scratch_shapes=[pltpu.VMEM((L + HALO, PGW), F32)],
        compiler_params=_params(("parallel",), VMEM_BIG))(proj)


def _pool_bwd(dpooled, dproj, *, name):
    L = dpooled.shape[0]
    rc = _time_chunk(L)
    n = L // rc

    def body(d_ref, dp_in, o_ref, pad):
        del dp_in
        g = pl.program_id(0)
        pad[L:L + HALO, :] = jnp.zeros((HALO, PGW), F32)
        rows = lax.broadcasted_iota(jnp.int32, (rc, PGW), 0)

        for gi in range(4):
            win = 2 << gi

            @pl.when(g == gi)
            def _(gi=gi, win=win):
                def fill(i, c):
                    r0 = pl.multiple_of(i * rc, rc)
                    cnt = jnp.minimum(rows + (r0 + 1), win).astype(F32)
                    pad[pl.ds(r0, rc), :] = d_ref[pl.ds(r0, rc), :] / cnt
                    return c

                lax.fori_loop(0, n, fill, 0)

                def step(i, c):
                    r0 = pl.multiple_of(i * rc, rc)
                    s = pad[pl.ds(r0, rc + HALO), :]
                    sh = 1
                    while sh < win:
                        s = s + pltpu.roll(s, rc + HALO - sh, 0)
                        sh *= 2
                    o_ref[pl.ds(r0, rc), :] = (s[:rc] - d_ref[pl.ds(r0, rc), :]).astype(o_ref.dtype)
                    return c

                lax.fori_loop(0, n, step, 0)

    return pl.pallas_call(
        body, name=name, grid=(4,),
        in_specs=[pl.BlockSpec((L, PGW), lambda j: (0, j)), pl.BlockSpec(memory_space=pl.ANY)],
        out_specs=pl.BlockSpec((L, PGW), lambda j: (0, j + C_POOL // PGW)),
        out_shape=jax.ShapeDtypeStruct((L, NPROJ), BF16),
        scratch_shapes=[pltpu.VMEM((L + HALO, PGW), F32)],
        input_output_aliases={1: 0},
        compiler_params=_params(("parallel",), VMEM_BIG))(dpooled, dproj)


_SPLIT_DT = jnp.bfloat16


def _ssd_consts():
    tri = np.tril(np.ones((Q, Q), np.float32))
    exp = np.zeros((128, DI), np.float32)
    for h in range(NH):
        exp[h, h * HP:(h + 1) * HP] = 1.0
    exp2 = np.concatenate([exp, exp], axis=0)
    return (jnp.asarray(tri, dtype=_SPLIT_DT), jnp.asarray(tri.T.copy(), dtype=_SPLIT_DT),
            jnp.asarray(exp2, dtype=_SPLIT_DT))


def _split(v, n):
    parts, r = [], v
    for _ in range(n):
        p = r.astype(_SPLIT_DT)
        parts.append(p)
        r = r - p.astype(F32)
    return parts


def _bdot(a, b, dims):
    return lax.dot_general(a, b, (dims, ((), ())), preferred_element_type=F32)


def _tri_sum(t_ref, v):
    r = _bdot(t_ref[...], jnp.concatenate(_split(v, 3), axis=1), NN)
    return r[:, :128] + r[:, 128:256] + r[:, 256:]


def _expand(v, e2_ref):
    return _bdot(jnp.concatenate(_split(v, 2), axis=1), e2_ref[...], NN)


def _reduce_heads(vals, eg):
    parts = []
    for v in vals:
        parts += _split(v, 2)
    r = _bdot(jnp.concatenate(parts, axis=0), eg, NT)
    return [r[2 * i * Q:(2 * i + 1) * Q] + r[(2 * i + 1) * Q:(2 * i + 2) * Q] for i in range(len(vals))]


def _ssd_common(xbc_ref, dtw_ref, arow_ref, t_ref, e_ref):
    dt = dtw_ref[:, :128]
    sig = dtw_ref[:, 128:]
    acs = _tri_sum(t_ref, dt * arow_ref[...])
    acs_x = _expand(acs, e_ref)
    dt_x = _expand(dt, e_ref)
    xs = xbc_ref[:, 0:DI]
    return sig, dt, acs, acs.T, acs_x, dt_x, xs


CONV_SLAB = 512


def _ssd_fwd(raw, dtp, cw, cb, arow, dsk_x, *, name):
    L = raw.shape[0]
    nc = L // Q
    tri, _, expand = _ssd_consts()

    def body(raw_ref, halo_ref, cw_ref, cb_ref, dtw_ref, arow_ref, dsk_ref, t_ref, e_ref,
             y_ref, hs_ref, xbc_ref, h_scr, cpad):
        c = pl.program_id(0)

        @pl.when(c == 0)
        def _():
            h_scr[...] = jnp.zeros_like(h_scr)

        cpad[0:8, :] = jnp.where(c > 0, halo_ref[8:16, :].astype(F32), 0.0)
        cpad[8:8 + Q, :] = raw_ref[...].astype(F32)
        for lo in range(0, XBC, CONV_SLAB):
            sl = slice(lo, lo + CONV_SLAB)
            acc = cb_ref[:, sl]
            for j in range(4):
                acc = acc + cpad[8 - j:8 - j + Q, sl] * cw_ref[3 - j:4 - j, sl]
            xbc_ref[:, sl] = acc * _sigmoid(acc)

        _, dt, acs, acs_t, acs_x, dt_x, xs = _ssd_common(xbc_ref, dtw_ref, arow_ref, t_ref, e_ref)
        xdt = xs * dt_x
        eacs = jnp.exp(acs_x)
        acs_last = acs_x[Q - 1:Q, :]
        dec = jnp.exp(acs_last - acs_x)
        hs_ref[0] = h_scr[...].astype(hs_ref.dtype)
        causal = lax.broadcasted_iota(jnp.int32, (Q, Q), 0) >= lax.broadcasted_iota(jnp.int32, (Q, Q), 1)
        first = lax.broadcasted_iota(jnp.int32, (Q, 128), 1) < HP
        for g in range(NG):
            bg = xbc_ref[:, DI + g * NS:DI + (g + 1) * NS]
            cg = xbc_ref[:, DI + NG * NS + g * NS:DI + NG * NS + (g + 1) * NS]
            s = _dot(cg, bg, NT)
            sl = slice(g * GW, (g + 1) * GW)
            hg = h_scr[:, sl]
            yoff = _dot(cg, hg, NN) * eacs[:, sl]
            st = _dot(bg, xdt[:, sl] * dec[:, sl], TN)
            h_scr[:, sl] = hg * eacs[Q - 1:Q, sl] + st
            for j in range(4):
                lo = g * GW + j * 128
                xb = xdt[:, lo:lo + 128]
                yp = yoff[:, j * 128:(j + 1) * 128] + dsk_ref[:, lo:lo + 128] * xs[:, lo:lo + 128]
                for e in range(2):
                    h = g * 8 + j * 2 + e
                    lm = jnp.exp(jnp.where(causal, acs[:, h:h + 1] - acs_t[h:h + 1, :], NEG))
                    xm = jnp.where(first if e == 0 else jnp.logical_not(first), xb, 0.0)
                    yp = yp + _dot(s * lm, xm, NN)
                y_ref[:, lo:lo + 128] = yp.astype(y_ref.dtype)

    const = lambda c: (0, 0)
    return pl.pallas_call(
        body, name=name, grid=(nc,),
        in_specs=[pl.BlockSpec((Q, XBC), lambda c: (c, 0)),
                  pl.BlockSpec((16, XBC), lambda c: (jnp.maximum(c * (Q // 16) - 1, 0), 0)),
                  pl.BlockSpec((4, XBC), const), pl.BlockSpec((1, XBC), const),
                  pl.BlockSpec((Q, DT_PAD), lambda c: (c, 0)),
                  pl.BlockSpec((1, 128), const), pl.BlockSpec((1, DI), const),
                  pl.BlockSpec((Q, Q), const), pl.BlockSpec((256, DI), const)],
        out_specs=[pl.BlockSpec((Q, DI), lambda c: (c, 0)), pl.BlockSpec((1, NS, DI), lambda c: (c, 0, 0)),
                   pl.BlockSpec((Q, XBC), lambda c: (c, 0))],
        out_shape=[jax.ShapeDtypeStruct((L, DI), BF16), jax.ShapeDtypeStruct((nc, NS, DI), F32),
                   jax.ShapeDtypeStruct((L, XBC), F32)],
        scratch_shapes=[pltpu.VMEM((NS, DI), F32), pltpu.VMEM((8 + Q, XBC), F32)],
        compiler_params=_params(("arbitrary",), VMEM_BIG))(raw, raw, cw, cb, dtp, arow, dsk_x, tri, expand)


def _ssd_bwd(dy, xbc, dtp, hs, arow, dsk_x, dproj, *, name):
    L = xbc.shape[0]
    nc = L // Q
    tri, triu, expand = _ssd_consts()

    def body(dy_ref, xbc_ref, dtw_ref, hs_ref, arow_ref, dsk_ref, t_ref, u_ref, e_ref, dp_in,
             dxbc_ref, ddtw_ref, da_ref, ddx_ref, ddtb_ref, dh_scr):
        del dp_in
        i = pl.program_id(0)

        @pl.when(i == 0)
        def _():
            dh_scr[...] = jnp.zeros_like(dh_scr)

        sig, dt, acs, acs_t, acs_x, dt_x, xs = _ssd_common(xbc_ref, dtw_ref, arow_ref, t_ref, e_ref)
        dyv = dy_ref[...]
        xdt = xs * dt_x
        eacs = jnp.exp(acs_x)
        acs_last = acs_x[Q - 1:Q, :]
        dec = jnp.exp(acs_last - acs_x)
        gy = dyv * eacs
        causal = lax.broadcasted_iota(jnp.int32, (Q, Q), 0) >= lax.broadcasted_iota(jnp.int32, (Q, Q), 1)
        first = lax.broadcasted_iota(jnp.int32, (Q, 128), 1) < HP
        lane_h = lax.broadcasted_iota(jnp.int32, (Q, 128), 1)
        sub_h = lax.broadcasted_iota(jnp.int32, (128, Q), 0)
        last_row = lax.broadcasted_iota(jnp.int32, (Q, GW), 0) == Q - 1
        dacs = jnp.zeros((Q, 128), F32)
        dacs_t = jnp.zeros((128, Q), F32)
        ddt = jnp.zeros((Q, 128), F32)
        for g in range(NG):
            bg = xbc_ref[:, DI + g * NS:DI + (g + 1) * NS]
            cg = xbc_ref[:, DI + NG * NS + g * NS:DI + NG * NS + (g + 1) * NS]
            s = _dot(cg, bg, NT)
            sl = slice(g * GW, (g + 1) * GW)
            hg = hs_ref[0, :, sl].astype(F32)
            dhn = dh_scr[:, sl]
            eal = eacs[Q - 1:Q, sl]
            gg = gy[:, sl]
            dax = gg * _dot(cg, hg, NN)
            dcg = _dot(gg, hg, NT)
            dh_scr[:, sl] = _dot(cg, gg, TN) + dhn * eal
            dal = eal * _colsum(dhn * hg)
            xdd = xdt[:, sl] * dec[:, sl]
            dbg = _dot(xdd, dhn, NT)
            wv = _dot(bg, dhn, NN)
            dd = wv * xdd
            dax = dax - dd
            dal = dal + _colsum(dd)
            dax = dax + jnp.where(last_row, dal, 0.0)
            dxdt_g = wv * dec[:, sl]
            ds = jnp.zeros((Q, Q), F32)
            dxdt_blocks = []
            for j in range(4):
                lo = g * GW + j * 128
                xb = xdt[:, lo:lo + 128]
                dyb = dyv[:, lo:lo + 128]
                dxb = dxdt_g[:, j * 128:(j + 1) * 128]
                for e in range(2):
                    h = g * 8 + j * 2 + e
                    lm = jnp.exp(jnp.where(causal, acs[:, h:h + 1] - acs_t[h:h + 1, :], NEG))
                    m = s * lm
                    dym = jnp.where(first if e == 0 else jnp.logical_not(first), dyb, 0.0)
                    dm = _dot(dym, xb, NT)
                    r = dm * m
                    dacs = dacs + jnp.where(lane_h == h, jnp.sum(r, axis=1, keepdims=True), 0.0)
                    dacs_t = dacs_t + jnp.where(sub_h == h, _colsum(r), 0.0)
                    ds = ds + dm * lm
                    dxb = dxb + _dot(m, dym, TN)
                dxdt_blocks.append(dxb)
            dxdt = jnp.concatenate(dxdt_blocks, axis=1)
            dcg = dcg + _dot(ds, bg, NN)
            dbg = dbg + _dot(ds, cg, TN)
            dxbc_ref[:, DI + g * NS:DI + (g + 1) * NS] = dbg
            dxbc_ref[:, DI + NG * NS + g * NS:DI + NG * NS + (g + 1) * NS] = dcg
            dxbc_ref[:, sl] = dsk_ref[:, sl] * dyv[:, sl] + dxdt * dt_x[:, sl]
            ddt_g, dacs_g = _reduce_heads([dxdt * xs[:, sl], dax], e_ref[0:128, sl])
            ddt = ddt + ddt_g
            dacs = dacs + dacs_g
        dacs = dacs - dacs_t.T
        ddta = _tri_sum(u_ref, dacs)
        ddt = ddt + ddta * arow_ref[...]
        ddtw = jnp.where(lane_h < NH, ddt * sig, 0.0)
        ddtw_ref[...] = jnp.concatenate([ddtw, jnp.zeros((Q, DT_PAD - 128), F32)], axis=1).astype(ddtw_ref.dtype)
        _acc_out(da_ref, _colsum(ddta * dt), i)
        _acc_out(ddx_ref, _colsum(dyv * xs), i)
        _acc_out(ddtb_ref, _colsum(ddtw), i)

    rev = lambda c: (nc - 1 - c, 0)
    const = lambda c: (0, 0)
    return pl.pallas_call(
        body, name=name, grid=(nc,),
        in_specs=[pl.BlockSpec((Q, DI), rev), pl.BlockSpec((Q, XBC), rev),
                  pl.BlockSpec((Q, DT_PAD), rev),
                  pl.BlockSpec((1, NS, DI), lambda c: (nc - 1 - c, 0, 0)),
                  pl.BlockSpec((1, 128), const), pl.BlockSpec((1, DI), const),
                  pl.BlockSpec((Q, Q), const), pl.BlockSpec((Q, Q), const), pl.BlockSpec((256, DI), const),
                  pl.BlockSpec(memory_space=pl.ANY)],
        out_specs=[pl.BlockSpec((Q, XBC), rev),
                   pl.BlockSpec((Q, DT_PAD), lambda c: (nc - 1 - c, C_DT // DT_PAD)),
                   pl.BlockSpec((1, 128), const), pl.BlockSpec((1, DI), const), pl.BlockSpec((1, 128), const)],
        out_shape=[jax.ShapeDtypeStruct((L, XBC), F32), jax.ShapeDtypeStruct((L, NPROJ), BF16),
                   jax.ShapeDtypeStruct((1, 128), F32), jax.ShapeDtypeStruct((1, DI), F32),
                   jax.ShapeDtypeStruct((1, 128), F32)],
        scratch_shapes=[pltpu.VMEM((NS, DI), F32)],
        input_output_aliases={9: 1},
        compiler_params=_params(("arbitrary",), VMEM_BIG))(dy, xbc, dtp, hs, arow, dsk_x, tri, triu,
                                                          expand, dproj)


def _adam_update(wv, gv, mv, vv):
    c1 = 1.0 - ADAM_B1 ** ADAM_STEP
    c2 = 1.0 - ADAM_B2 ** ADAM_STEP
    mn = ADAM_B1 * mv + (1.0 - ADAM_B1) * gv
    vn = ADAM_B2 * vv + (1.0 - ADAM_B2) * (gv * gv)
    return -ADAM_LR * ((mn / c1) / (jnp.sqrt(vn / c2) + ADAM_EPS) + ADAM_WD * wv), mn, vn


def _adamw(w, g, m, v, *, name, tr=None):
    R = w.shape[0]
    rest = tuple(w.shape[1:])
    if tr is None:
        tr = _pick(R, (256, 128, 64, 32, 16, 8))
    assert R % tr == 0

    def body(w_ref, g_ref, m_ref, v_ref, d_ref, mo_ref, vo_ref):
        d_ref[...], mo_ref[...], vo_ref[...] = _adam_update(w_ref[...], g_ref[...], m_ref[...], v_ref[...])

    zeros = (0,) * len(rest)
    spec = pl.BlockSpec((tr,) + rest, lambda i: (i,) + zeros)
    return pl.pallas_call(body, name=name, grid=(R // tr,), in_specs=[spec] * 4, out_specs=[spec] * 3,
                          out_shape=[jax.ShapeDtypeStruct(w.shape, F32)] * 3,
                          compiler_params=_params(("parallel",)))(w, g, m, v)


def _adamw_small(svrow, g_conv, params, *, name):
    n = len(params)

    def body(*refs):
        sv_ref, gc_ref = refs[0], refs[1]
        ins, outs = refs[2:2 + 3 * n], refs[2 + 3 * n:]
        for p, (key, w, _, _) in enumerate(params):
            w_ref, m_ref, v_ref = ins[3 * p:3 * p + 3]
            g_ref, d_ref, mo_ref, vo_ref = outs[4 * p:4 * p + 4]
            gv = gc_ref[...] if key == "conv_w" else sv_ref[:, SV_OFF[key]:SV_OFF[key] + w.shape[1]]
            g_ref[...] = gv
            d_ref[...], mo_ref[...], vo_ref[...] = _adam_update(w_ref[...], gv, m_ref[...], v_ref[...])

    vm = pl.BlockSpec(memory_space=pltpu.VMEM)
    args = [svrow, g_conv]
    shapes = []
    for _, w, m, v in params:
        args += [w, m, v]
        shapes += [jax.ShapeDtypeStruct(w.shape, F32)] * 4
    res = pl.pallas_call(body, name=name, in_specs=[vm] * len(args), out_specs=[vm] * len(shapes),
                         out_shape=shapes)(*args)
    return {key: tuple(res[4 * p:4 * p + 4]) for p, (key, _, _, _) in enumerate(params)}


def _slab_sum(recv, *, tile, name):
    rows = recv.shape[1]
    assert rows % tile == 0 and tile % 16 == 0

    def body(r_ref, o_ref):
        acc = r_ref[0].astype(F32)
        for j in range(1, N_DEV):
            acc = acc + r_ref[j].astype(F32)
        o_ref[...] = acc

    return pl.pallas_call(body, name=name, grid=(rows // tile,),
                          in_specs=[pl.BlockSpec((N_DEV, tile, D), lambda i: (0, i, 0))],
                          out_specs=pl.BlockSpec((tile, D), lambda i: (i, 0)),
                          out_shape=jax.ShapeDtypeStruct((rows, D), F32),
                          compiler_params=_params(("parallel",)))(recv)


MESH = pl.DeviceIdType.MESH


def _coords():
    return lax.axis_index("x"), lax.axis_index("y"), lax.axis_index("c")


def _peer(k):
    x, y, c = _coords()
    px = 1 - x if k & 4 else x
    py = 1 - y if k & 2 else y
    pc = 1 - c if k & 1 else c
    return (px, py, pc), 4 * px + 2 * py + pc


def _rcopy(src, dst, ssem, rsem, dev):
    return pltpu.make_async_remote_copy(src_ref=src, dst_ref=dst, send_sem=ssem, recv_sem=rsem,
                                        device_id=dev, device_id_type=MESH)


def _exchange_all(src_of, dst_slot, send_sems, recv_sems):
    x, y, c = _coords()
    me = 4 * x + 2 * y + c
    sent = []
    for k in range(1, N_DEV):
        dev, pidx = _peer(k)
        cp = _rcopy(src_of(pidx), dst_slot(me), send_sems.at[k - 1], recv_sems.at[k - 1], dev)
        cp.start()
        sent.append(cp)
    for k in range(1, N_DEV):
        dev, pidx = _peer(k)
        _rcopy(src_of(pidx), dst_slot(pidx), send_sems.at[k - 1], recv_sems.at[k - 1], dev).wait_recv()
    for cp in sent:
        cp.wait_send()


def _rows_of_slots(buf, nslots):
    rows = lax.broadcasted_iota(jnp.int32, (8, buf.shape[-1]), 0)
    out = jnp.zeros((8, buf.shape[-1]), F32)
    for j in range(nslots):
        out = out + jnp.where(rows == j, buf[j], 0.0)
    return out


def _exchange_start(src_of, dst_slot, send_sems, recv_sems):
    x, y, c = _coords()
    me = 4 * x + 2 * y + c
    sent = []
    for k in range(1, N_DEV):
        dev, pidx = _peer(k)
        cp = _rcopy(src_of(pidx), dst_slot(me), send_sems.at[k - 1], recv_sems.at[k - 1], dev)
        cp.start()
        sent.append(cp)
    return sent


def _exchange_finish(sent, src_of, dst_slot, send_sems, recv_sems):
    for k in range(1, N_DEV):
        dev, pidx = _peer(k)
        _rcopy(src_of(pidx), dst_slot(pidx), send_sems.at[k - 1], recv_sems.at[k - 1], dev).wait_recv()
    for cp in sent:
        cp.wait_send()


def _ada_gather(c, w_ada, b_r, slab, *, name):
    wloc = w_ada.shape[1]

    def body(c_ref, w_ref, b_ref, x_ref, mod_ref, call_ref, out_ref,
             csrc, cbuf, psrc, pbuf, s1, r1, s2, r2, send_sems, recv_sems, local_sem):
        x, y, cc = _coords()
        me_i = 4 * x + 2 * y + cc
        me, sibling = (x, y, cc), (x, y, 1 - cc)
        chips = [(1 - x, y), (x, 1 - y), (1 - x, 1 - y)]

        def slot(px, py, pc):
            return out_ref.at[4 * px + 2 * py + pc]

        def copy(k, block, to, src=None):
            return _rcopy(slot(*block) if src is None else src, slot(*block), send_sems.at[k], recv_sems.at[k], to)

        csrc[...] = jnp.broadcast_to(c_ref[...], (8, D))
        cbuf[me_i] = csrc[...]
        c_of, c_slot = (lambda p: csrc), (lambda s: cbuf.at[s])
        sent1 = _exchange_start(c_of, c_slot, s1, r1)

        mine = pltpu.make_async_copy(x_ref, slot(*me), local_sem)
        mine.start()
        first = [copy(0, me, sibling, src=x_ref)]
        first += [copy(1 + j, me, (*chip, cc), src=x_ref) for j, chip in enumerate(chips)]
        for cp in first:
            cp.start()

        _exchange_finish(sent1, c_of, c_slot, s1, r1)
        call = _rows_of_slots(cbuf, N_DEV)
        call_ref[...] = call
        prod = _dot_hi(_silu(call), w_ref[...])
        for b in range(N_DEV):
            psrc[b] = jnp.broadcast_to(prod[b:b + 1, :], (8, wloc))
        pbuf[me_i] = psrc[me_i]
        p_of, p_slot = (lambda p: psrc.at[p]), (lambda s: pbuf.at[s])
        sent2 = _exchange_start(p_of, p_slot, s2, r2)

        passed = [copy(4 + j, (*chip, cc), sibling) for j, chip in enumerate(chips)]
        for j, chip in enumerate(chips):
            copy(1 + j, (*chip, cc), me).wait_recv()
            passed[j].start()
        copy(0, sibling, me).wait_recv()
        for j, chip in enumerate(chips):
            copy(4 + j, (*chip, 1 - cc), me).wait_recv()

        _exchange_finish(sent2, p_of, p_slot, s2, r2)
        mod_ref[...] = _rows_of_slots(pbuf, N_DEV) + b_ref[...]
        for cp in first + passed:
            cp.wait_send()
        mine.wait()

    vm = pl.BlockSpec(memory_space=pltpu.VMEM)
    anyspec = pl.BlockSpec(memory_space=pl.ANY)
    return pl.pallas_call(
        body, name=name, in_specs=[vm, vm, vm, anyspec], out_specs=[vm, vm, anyspec],
        out_shape=[jax.ShapeDtypeStruct((N_DEV, wloc), F32), jax.ShapeDtypeStruct((N_DEV, D), F32),
                   jax.ShapeDtypeStruct((N_DEV,) + slab.shape, slab.dtype)],
        scratch_shapes=[pltpu.VMEM((8, D), F32), pltpu.VMEM((N_DEV, 8, D), F32),
                        pltpu.VMEM((N_DEV, 8, wloc), F32), pltpu.VMEM((N_DEV, 8, wloc), F32),
                        pltpu.SemaphoreType.DMA((N_DEV - 1,)), pltpu.SemaphoreType.DMA((N_DEV - 1,)),
                        pltpu.SemaphoreType.DMA((N_DEV - 1,)), pltpu.SemaphoreType.DMA((N_DEV - 1,)),
                        pltpu.SemaphoreType.DMA((7,)), pltpu.SemaphoreType.DMA((7,)), pltpu.SemaphoreType.DMA],
        compiler_params=pltpu.CompilerParams(vmem_limit_bytes=VMEM_BIG))(c, w_ada, b_r, slab)


_HBM =pl.BlockSpec(memory_space=pltpu.HBM)
_SEM = pl.BlockSpec(memory_space=pltpu.SEMAPHORE)
_EFFECT = pltpu.SideEffectType.DATAFLOW_SIDE_EFFECTING


def _xchg_src(src_ref, pidx, per_peer):
    return src_ref.at[pidx] if per_peer else src_ref


def _xchg_start(src, *, per_peer, name):
    rows = src.shape[-2]
    land_shape = (N_DEV, rows, D)

    def body(src_ref, land_ref, send_sems, recv_sems, src_thru, land_thru, token):
        del src_thru, land_thru
        x, y, c = _coords()
        me = 4 * x + 2 * y + c
        for k in range(1, N_DEV):
            dev, pidx = _peer(k)
            _rcopy(_xchg_src(src_ref, pidx, per_peer), land_ref.at[me], send_sems.at[k - 1],
                   recv_sems.at[k - 1], dev).start()
        token[...] = jnp.zeros_like(token)

    return pl.pallas_call(
        body, name=name,
        out_shape=(pltpu.SemaphoreType.DMA((N_DEV - 1,)), pltpu.SemaphoreType.DMA((N_DEV - 1,)),
                   pltpu.HBM(src.shape, src.dtype), pltpu.HBM(land_shape, src.dtype),
                   jax.ShapeDtypeStruct((8, 128), F32)),
        in_specs=(_HBM, _HBM),
        out_specs=(_SEM, _SEM, _HBM, _HBM, pl.BlockSpec(memory_space=pltpu.VMEM)),
        input_output_aliases={0: 2, 1: 3},
        compiler_params=pltpu.CompilerParams(has_side_effects=_EFFECT),
    )(pltpu.with_memory_space_constraint(src, pltpu.HBM),
      pltpu.with_memory_space_constraint(lax.empty(land_shape, src.dtype), pltpu.HBM))


def _xchg_wait(started, after, *, per_peer, name):
    send_sems, recv_sems, src_thru, land_thru, _ = started

    def body(src_ref, land_ref, send_sems, recv_sems, after_ref, src_dead, got_ref):
        del after_ref, src_dead, got_ref
        for k in range(1, N_DEV):
            dev, pidx = _peer(k)
            cp = _rcopy(_xchg_src(src_ref, pidx, per_peer), land_ref.at[pidx], send_sems.at[k - 1],
                        recv_sems.at[k - 1], dev)
            cp.wait_send()
            cp.wait_recv()

    return pl.pallas_call(
        body, name=name,
        out_shape=(pltpu.HBM(src_thru.shape, src_thru.dtype), pltpu.HBM(land_thru.shape, land_thru.dtype)),
        in_specs=(_HBM, _HBM, _SEM, _SEM, pl.BlockSpec(memory_space=pl.ANY)),
        out_specs=(_HBM, _HBM),
        input_output_aliases={0: 0, 1: 1},
        compiler_params=pltpu.CompilerParams(has_side_effects=_EFFECT),
    )(src_thru, land_thru, send_sems, recv_sems, after)


def _dep(token):
    return (token, (8, 128), lambda i, j, k: (0, 0))


def _small_allsum(sv, *, name):
    def body(sv_ref, all_ref, sum_ref, send_sems, recv_sems):
        x, y, c = _coords()
        me = 4 * x + 2 * y + c
        all_ref[me] = sv_ref[...]
        _exchange_all(lambda p: sv_ref, lambda s: all_ref.at[s], send_sems, recv_sems)
        acc = all_ref[0]
        for j in range(1, N_DEV):
            acc = acc + all_ref[j]
        sum_ref[...] = acc

    vm = pl.BlockSpec(memory_space=pltpu.VMEM)
    return pl.pallas_call(
        body, name=name, in_specs=[vm], out_specs=[vm, vm],
        out_shape=[jax.ShapeDtypeStruct((N_DEV, SV_ROWS, 128), F32), jax.ShapeDtypeStruct((SV_ROWS, 128), F32)],
        scratch_shapes=[pltpu.SemaphoreType.DMA((7,)), pltpu.SemaphoreType.DMA((7,))],
    )(sv)


def _ada_bwd(call, dmod_loc, *, name):
    wloc = dmod_loc.shape[1]

    def body(c_ref, d_ref, o_ref):
        o_ref[...] = _dot_hi(_silu(c_ref[...]), d_ref[...], TN)

    vm = pl.BlockSpec(memory_space=pltpu.VMEM)
    return pl.pallas_call(body, name=name, in_specs=[vm, vm], out_specs=vm,
                          out_shape=jax.ShapeDtypeStruct((D, wloc), F32),
                          compiler_params=pltpu.CompilerParams(vmem_limit_bytes=VMEM_BIG))(call, dmod_loc)


def _pad_rows(a, rows):
    return jnp.pad(a, ((0, rows - a.shape[0]), (0, 0)))


IN_SHIFT = tuple((IN_ROWS * j) % 16 for j in range(N_DEV))
IN_BASE = tuple(IN_ROWS * j - IN_SHIFT[j] for j in range(N_DEV))
IN_SEGMENTS = ((2048, XBC, C_XBC), (5152, 1024, C_POOL), (0, 2048, C_Z), (6176, 2048, C_GATE), (5120, 32, C_DT))


def _global_pieces(gs):
    pieces = []
    for j in range(N_DEV):
        lo, hi = 0, IN_ROWS_P
        if j > 0 and IN_BASE[j - 1] + IN_ROWS_P > IN_BASE[j]:
            pieces.append((IN_BASE[j], 16, gs[j - 1, IN_ROWS_P - 16:IN_ROWS_P] + gs[j, 0:16]))
            lo = 16
        if j + 1 < N_DEV and IN_BASE[j] + IN_ROWS_P > IN_BASE[j + 1]:
            hi = IN_ROWS_P - 16
        pieces.append((IN_BASE[j] + lo, hi - lo, gs[j, lo:hi]))
    return pieces


def _reorder_in_rows(gs):
    pieces = _global_pieces(gs)
    parts = []
    for lo, n, _ in IN_SEGMENTS:
        for p0, pn, arr in pieces:
            a, b = max(lo, p0), min(lo + n, p0 + pn)
            if a < b:
                parts.append(arr[a - p0:b - p0])
    parts.append(jnp.zeros((DT_PAD - 32, D), gs.dtype))
    return jnp.concatenate(parts, axis=0)


def _restore_in_shards(d):
    slabs = []
    for j in range(N_DEV):
        parts = []
        r, end = IN_BASE[j], IN_BASE[j] + IN_ROWS_P
        while r < end:
            lo, n, new = next(s for s in IN_SEGMENTS if s[0] <= r < s[0] + s[1])
            e = min(end, lo + n)
            parts.append(d[new + r - lo:new + e - lo])
            r = e
        slabs.append(jnp.concatenate(parts, axis=0))
    return jnp.stack(slabs, axis=0)


def _pack_sv(parts):
    flat = []
    for n, size in SV_PARTS:
        v = parts[n].reshape(-1).astype(F32)
        flat.append(jnp.pad(v, (0, size - v.shape[0])))
    v = jnp.concatenate(flat)
    return jnp.pad(v, (0, SV_ROWS * 128 - v.shape[0])).reshape(SV_ROWS, 128)


def _sv_get(flat, n, size):
    return flat[SV_OFF[n]:SV_OFF[n] + size]


def kernel(x, c, w_ada, b_ada, norm_mix_w, w_in, conv_w, conv_b, dt_bias, a_log, d_skip, ssd_norm_w, w_branch_ssd, pool_w, pool_scale, w_branch_pool, w_out, norm_mlp_w, w_up, w_down, norm_final_w, loss_target, m_w_ada, m_b_ada, m_norm_mix_w, m_w_in, m_conv_w, m_conv_b, m_dt_bias, m_a_log, m_d_skip, m_ssd_norm_w, m_w_branch_ssd, m_pool_w, m_pool_scale, m_w_branch_pool, m_w_out, m_norm_mlp_w, m_w_up, m_w_down, m_norm_final_w, v_w_ada, v_b_ada, v_norm_mix_w, v_w_in, v_conv_w, v_conv_b, v_dt_bias, v_a_log, v_d_skip, v_ssd_norm_w, v_w_branch_ssd, v_pool_w, v_pool_scale, v_w_branch_pool, v_w_out, v_norm_mlp_w, v_w_up, v_w_down, v_norm_final_w):
    xs_ = x[0]
    tgt = loss_target[0]
    L = xs_.shape[0]
    me = 4 * lax.axis_index("x") + 2 * lax.axis_index("y") + lax.axis_index("c")
    wloc = w_ada.shape[2]

    conv_bits = lax.bitcast_convert_type(conv_w[0], SLAB_DT).reshape(3, D)
    in_shift = (IN_ROWS * me) % 16
    slab_in = lax.dynamic_update_slice(jnp.zeros((IN_ROWS_P, D), SLAB_DT), w_in[0].T.astype(SLAB_DT),
                                       (in_shift, 0))
    slab_in = jnp.concatenate([slab_in, _pad_rows(conv_bits, CONV_ROWS)], axis=0)
    slab_rest = jnp.concatenate([
        w_branch_ssd[0].astype(SLAB_DT),
        pool_w[0].reshape(32, D).astype(SLAB_DT),
        w_branch_pool[0].astype(SLAB_DT),
        w_out[0].astype(SLAB_DT),
        w_up[0].T.astype(SLAB_DT),
        w_down[0].astype(SLAB_DT)], axis=0)
    mod_p, c_all, gs_in = _ada_gather(c, w_ada[0], b_ada.reshape(N_DEV, wloc), slab_in,
                                      name="ada_gather_w_in")
    mod = mod_p.reshape(6, D)
    shift_m, scale_m, gate_m, shift_f, scale_f, gate_f = [mod[i:i + 1] for i in range(6)]
    slab_rest, gs_in = lax.optimization_barrier((slab_rest, gs_in))
    rest_started = _xchg_start(slab_rest, per_peer=False, name="gather_rest_start")
    gather_token = rest_started[4]

    w_in_t = _reorder_in_rows(gs_in)
    conv_full = lax.bitcast_convert_type(
        gs_in[:, IN_ROWS_P:IN_ROWS_P + 3].reshape(N_DEV, 4, XBC // N_DEV, 2), F32)
    conv_full = conv_full.transpose(1, 0, 2).reshape(4, XBC)

    dtb = jnp.pad(dt_bias, ((0, 0), (0, 128 - NH)))
    arow = jnp.pad(-jnp.exp(a_log), ((0, 0), (0, 128 - NH)))
    dsk_x = jnp.repeat(d_skip, HP, axis=1)

    tm = _pick(L, (1024, 512, 256, 128))
    tm2 = _pick(L, (2048, 1024, 512, 256, 128))
    tkl = _pick(L, (4096, 2048, 1024, 512, 256, 128))
    tkl2 = _pick(L, (2048, 1024, 512, 256, 128))

    tmh = _pick(L, (512, 256, 128))
    tmq = _pick(L, (256, 128))
    zcol = C_Z // DI
    gcol = C_GATE // (2 * D)

    def whole_rows(w):
        return lambda t: ((L, w), BF16, (t, w), lambda i, j, k: (i, 0))

    def norm1_pro(x_ref, ex, outs, j):
        @pl.when(j == 0)
        def _():
            xv = x_ref[...]
            r = lax.rsqrt(jnp.mean(xv * xv, axis=-1, keepdims=True) + EPS)
            outs[1][...] = (xv * r * ex[0][...] * (1.0 + ex[1][...]) + ex[2][...]).astype(outs[1].dtype)

        return outs[1][...]

    tn_in = 1408

    def proj_ep(acc, ex, outs):
        outs[0][...] = acc.astype(outs[0].dtype)

        @pl.when(pl.program_id(1) == NPROJ // tn_in - 1)
        def _():
            pre = acc[:, tn_in - DT_PAD:tn_in - DT_PAD + 128] + ex[3][...]
            outs[2][...] = jnp.concatenate([_softplus(pre), _sigmoid(pre)], axis=1)

    proj, h1, dtp = _mm(
        xs_, w_in_t, "nt", name="in_proj", tm=tm2, tn=tn_in, tk=D,
        extras=[(norm_mix_w, *_vecs()), (scale_m, *_vecs()), (shift_m, *_vecs()), (dtb, *_vecs(128)),
                _dep(gather_token)],
        outs=[BF16, whole_rows(D)(tm2), ((L, DT_PAD), F32, (tm2, DT_PAD), lambda i, j, k: (i, 0))],
        prologue=norm1_pro, epilogue=proj_ep)
    xbc_raw = proj
    y_ssm, hs, xbc = _ssd_fwd(xbc_raw, dtp, conv_full, conv_b, arow, dsk_x, name="ssd_fwd")

    slab_rest, gs = _xchg_wait(rest_started, y_ssm, per_peer=False, name="gather_rest_wait")
    gs = lax.dynamic_update_slice(gs, slab_rest[None], (me, 0, 0))

    def part(n, rows):
        return gs[:, REST_OFF[n]:REST_OFF[n] + rows]

    w_bssd = part("bssd", 256).reshape(DI, D)
    w_pool = part("pool", 32).reshape(N_DEV, 4, 32, PGW).transpose(1, 0, 2, 3).reshape(POOL_W, PGW)
    w_bpool = part("bpool", 128).reshape(POOL_W, D)
    w_o = part("out", 128).reshape(D, D)
    w_up_t = part("up", 512).reshape(DFF, D)
    w_dn = part("down", 512).reshape(DFF, D)

    def gnorm_pro(y_ref, ex, outs, j):
        z_ref, w_ref = ex
        yg = y_ref[...].astype(F32) * _silu(z_ref[...].astype(F32))
        segs = []
        for k in range(NG):
            sl = slice(k * GW, (k + 1) * GW)
            seg = yg[:, sl]
            r = lax.rsqrt(jnp.mean(seg * seg, axis=-1, keepdims=True) + EPS)
            segs.append((seg * r * w_ref[:, sl]).astype(BF16))
        yn_v = jnp.concatenate(segs, axis=1)
        outs[1][...] = yn_v
        return yn_v

    y_ssd, yn = _mm(y_ssm, w_bssd, "nn", name="branch_ssd", tm=tmh, tn=D, tk=DI,
                    extras=[(proj, *_rows(tmh, DI, zcol)), (ssd_norm_w, *_vecs(DI))],
                    outs=[BF16, whole_rows(DI)(tmh)], prologue=gnorm_pro)
    pooled = _pool_fwd(proj, name="pool_fwd")
    wp_spec = ((POOL_W, PGW), lambda i, j, k: (0, 0))

    def pool_pro(a_ref, ex, outs, j):
        wp_ref, s_ref = ex
        segs = []
        for g in range(4):
            sl = slice(g * PGW, (g + 1) * PGW)
            p = _dot(a_ref[:, sl], wp_ref[sl, :], NN)
            outs[1][:, sl] = p.astype(BF16)
            segs.append((p * s_ref[:, sl]).astype(BF16))
        yp1_v = jnp.concatenate(segs, axis=1)
        outs[2][...] = yp1_v
        return yp1_v

    y_pool, yp0, yp1 = _mm(pooled, w_bpool, "nn", name="branch_pool", tm=tm, tn=D, tk=D,
                           extras=[(w_pool, *wp_spec), (pool_scale, *_vecs())],
                           outs=[BF16, whole_rows(D)(tm), whole_rows(D)(tm)], prologue=pool_pro)

    def merge_pro(a_ref, ex, outs, j):
        s = _sigmoid(ex[1][...].astype(F32))
        mv = (s[:, :D] * a_ref[...].astype(F32) + s[:, D:] * ex[0][...].astype(F32)).astype(BF16)
        outs[3][...] = mv
        return mv

    mix, x1, h2, m = _mm(y_ssd, w_o, "nn", name="out_proj", tm=tmh, tn=D, tk=D,
                         extras=[(y_pool, *_rows(tmh)), (proj, *_rows(tmh, 2 * D, gcol)),
                                 (xs_, *_rows(tmh)), (gate_m, *_vecs()), (norm_mlp_w, *_vecs()),
                                 (scale_f, *_vecs()), (shift_f, *_vecs())],
                         outs=[BF16, F32, BF16, whole_rows(D)(tmh)], prologue=merge_pro,
                         epilogue=lambda acc, ex, outs: _ep_resid_norm(acc, ex[2:], outs[:3]))

    def relu2(acc, ex, outs):
        r = jnp.maximum(acc, 0.0)
        outs[0][...] = acc.astype(BF16)
        outs[1][...] = (r * r).astype(BF16)

    up, act = _mm(h2, w_up_t, "nt", name="mlp_up", outs=[BF16, BF16], tm=tmh, tn=DFF, tk=D, epilogue=relu2)

    dx2, ddown, loss_p, dnwf, dgate_f = _mm(
        act, w_dn, "nn", name="mlp_down", tm=tmh, tn=D, tk=DFF,
        extras=[(x1, *_rows(tmh)), (tgt, *_rows(tmh)), (gate_f, *_vecs()), (norm_final_w.reshape(1, D), *_vecs())],
        outs=[F32, BF16, _sum_out(128), _sum_out(), _sum_out()], epilogue=_ep_final)

    def drelu2(acc, ex, outs):
        outs[0][...] = (acc * (2.0 * jnp.maximum(ex[0][...].astype(F32), 0.0))).astype(BF16)

    def dep_last(ep):
        return lambda acc, ex, outs: ep(acc, ex[:-1], outs)

    dup = _mm(ddown, w_dn, "nt", name="mlp_down_dx", outs=[BF16], tm=tmh, tn=DFF, tk=D,
              extras=[(up, (tmh, DFF), lambda i, j, k: (i, j))], epilogue=drelu2)
    g_dn = _mm(act, ddown, "tn", name="mlp_down_dw", outs=[SLAB_DT], tm=1024, tn=D, tk=tkl)
    g_up_t = _mm(dup, h2, "tn", name="mlp_up_dw", outs=[SLAB_DT], tm=1024, tn=D, tk=tkl)
    gslab_mlp = jnp.concatenate([g_up_t.reshape(N_DEV, 512, D), g_dn.reshape(N_DEV, 512, D)], axis=1)
    mlp_started = _xchg_start(gslab_mlp, per_peer=True, name="scatter_mlp_start")
    dx1, p2, q2, dmix, dgate_m = _mm(
        dup, w_up_t, "nn", name="mlp_up_dx", tm=tmh, tn=D, tk=DFF,
        extras=[(x1, *_rows(tmh)), (dx2, *_rows(tmh)), (norm_mlp_w, *_vecs()), (scale_f, *_vecs()),
                (mix, *_rows(tmh)), (gate_m, *_vecs()), _dep(mlp_started[4])],
        outs=[F32, _sum_out(), _sum_out(), BF16, _sum_out()], epilogue=dep_last(_ep_norm_bwd))
    gcol = C_GATE // (2 * D)
    dy_ssd, dy_pool, dproj = _mm(
        dmix, w_o, "nt", name="out_proj_dx", tm=tmh, tn=D, tk=D,
        extras=[(y_ssd, *_rows(tmh)), (y_pool, *_rows(tmh)), (proj, *_rows(tmh, 2 * D, gcol))],
        outs=[BF16, BF16, ((L, NPROJ), BF16, *_rows(tmh, 2 * D, gcol))], epilogue=_ep_merge_bwd)
    g_o = _mm(m, dmix, "tn", name="out_proj_dw", outs=[SLAB_DT], tm=D, tn=D, tk=tkl)
    zcol = C_Z // DI
    dy_ssm, dproj, d_snw = _mm(
        dy_ssd, w_bssd, "nt", name="branch_ssd_dx", tm=tmh, tn=DI, tk=D,
        extras=[(y_ssm, *_rows(tmh, DI)), (proj, *_rows(tmh, DI, zcol)), (ssd_norm_w, *_vecs(DI)),
                (dproj, None, None)],
        outs=[F32, ((L, NPROJ), BF16, *_rows(tmh, DI, zcol)), _sum_out(DI)],
        epilogue=_ep_gated_norm_bwd, aliases={3: 1})
    g_bssd = _mm(yn, dy_ssd, "tn", name="branch_ssd_dw", outs=[SLAB_DT], tm=1024, tn=D, tk=tkl)
    dxbc, dproj, d_a, d_dx, d_dtb = _ssd_bwd(dy_ssm, xbc, dtp, hs, arow, dsk_x, dproj, name="ssd_bwd")
    dproj, d_cw, d_cb = _conv_bwd(xbc_raw, dxbc, conv_full, conv_b, dproj, name="conv_bwd")
    def pool_bwd_ep(acc, ex, outs):
        y_ref, s_ref, wp_ref = ex
        o_ref, ds_ref, dpool_ref = outs
        dyp0_v = (acc * s_ref[...]).astype(BF16)
        o_ref[...] = dyp0_v
        _acc_out(ds_ref, _colsum(acc * y_ref[...].astype(F32)), _row_step())
        for g in range(4):
            sl = slice(g * PGW, (g + 1) * PGW)
            dpool_ref[:, sl] = _dot(dyp0_v[:, sl], wp_ref[sl, :], NT)

    dyp0, d_ps, dpooled = _mm(dy_pool, w_bpool, "nt", name="branch_pool_dx", tm=tm, tn=D, tk=D,
                              extras=[(yp0, *_rows(tm)), (pool_scale, *_vecs()), (w_pool, *wp_spec)],
                              outs=[BF16, _sum_out(), F32], epilogue=pool_bwd_ep)
    g_bpool = _mm(yp1, dy_pool, "tn", name="branch_pool_dw", outs=[SLAB_DT], tm=D, tn=D, tk=tkl)
    g_pool = _mm_pool_tn(pooled, dyp0, name="pool_mix_dw", tk=tkl)
    gslab_mix = jnp.concatenate([
        g_bssd.reshape(N_DEV, 256, D),
        g_pool.reshape(4, N_DEV, 32, PGW).transpose(1, 0, 2, 3).reshape(N_DEV, 32, D).astype(SLAB_DT),
        g_bpool.reshape(N_DEV, 128, D),
        g_o.reshape(N_DEV, 128, D)], axis=1)
    mix_started = _xchg_start(gslab_mix, per_peer=True, name="scatter_mix_start")
    dproj = _pool_bwd(dpooled, dproj, name="pool_bwd")
    g_in_t = _mm(dproj, h1, "tn", name="in_proj_dw", outs=[SLAB_DT], tm=tmq, tn=D, tk=L,
                 extras=[_dep(mix_started[4])])
    gslab_in = _restore_in_shards(g_in_t)
    in_started = _xchg_start(gslab_in, per_peer=True, name="scatter_in_start")
    grad_x, p1, q1 = _mm(
        dproj, w_in_t, "nn", name="in_proj_dx", tm=tmq, tn=D, tk=NPROJ,
        extras=[(xs_, *_rows(tmq)), (dx1, *_rows(tmq)), (norm_mix_w, *_vecs()), (scale_m, *_vecs()),
                _dep(in_started[4])],
        outs=[F32, _sum_out(), _sum_out()], epilogue=dep_last(_ep_norm_bwd))

    def landed(started, after, tile, name):
        src, land = _xchg_wait(started, after, per_peer=True, name=name + "_wait")
        own = lax.dynamic_slice_in_dim(src, me, 1, axis=0)
        return _slab_sum(lax.dynamic_update_slice(land, own, (me, 0, 0)), tile=tile, name=name + "_sum")

    gsum_mlp = landed(mlp_started, grad_x, 256, "scatter_mlp")
    gsum_mix = landed(mix_started, grad_x, 272, "scatter_mix")
    gsum_in = landed(in_started, grad_x, 208, "scatter_in")

    dmod = jnp.concatenate([q1, p1 * norm_mix_w, dgate_m, q2, p2 * norm_mlp_w, dgate_f], axis=1)
    d_alog = d_a[:, :NH] * (-jnp.exp(a_log))
    sv = _pack_sv({
        "b_ada": dmod, "norm_mix_w": p1 * (1.0 + scale_m), "conv_b": d_cb, "dt_bias": d_dtb[:, :NH],
        "a_log": d_alog, "d_skip": d_dx.reshape(NH, HP).sum(axis=1), "ssd_norm_w": d_snw,
        "pool_scale": d_ps, "norm_mlp_w": p2 * (1.0 + scale_f), "norm_final_w": dnwf, "conv_w": d_cw,
        "loss": loss_p[:, :1]})
    sv_all, sv_sum = _small_allsum(sv, name="small_allsum")
    flat = sv_sum.reshape(-1)
    loss = flat[SV_OFF["loss"]]
    dmod_all = sv_all.reshape(N_DEV, SV_ROWS * 128)[:, :6 * D]
    g_w_ada = _ada_bwd(c_all, lax.dynamic_slice_in_dim(dmod_all, me * wloc, wloc, axis=1), name="ada_bwd")

    g_conv_w = lax.dynamic_slice_in_dim(_sv_get(flat, "conv_w", 4 * XBC).reshape(4, XBC),
                                        me * (XBC // N_DEV), XBC // N_DEV, axis=1)
    small = [("b_ada", b_ada, m_b_ada, v_b_ada), ("norm_mix_w", norm_mix_w, m_norm_mix_w, v_norm_mix_w),
             ("conv_b", conv_b, m_conv_b, v_conv_b), ("dt_bias", dt_bias, m_dt_bias, v_dt_bias),
             ("a_log", a_log, m_a_log, v_a_log), ("d_skip", d_skip, m_d_skip, v_d_skip),
             ("ssd_norm_w", ssd_norm_w, m_ssd_norm_w, v_ssd_norm_w),
             ("pool_scale", pool_scale, m_pool_scale, v_pool_scale),
             ("norm_mlp_w", norm_mlp_w, m_norm_mlp_w, v_norm_mlp_w),
             ("norm_final_w", norm_final_w[None], m_norm_final_w[None], v_norm_final_w[None]),
             ("conv_w", conv_w[0], m_conv_w[0], v_conv_w[0])]
    small_out = _adamw_small(sv_sum.reshape(1, SV_ROWS * 128), g_conv_w, small, name="adamw_small")
    small_out["norm_final_w"] = tuple(a[0] for a in small_out["norm_final_w"])
    small_out["conv_w"] = tuple(a[None] for a in small_out["conv_w"])

    def gpart(n, rows_):
        return gsum_mix[MIX_OFF[n]:MIX_OFF[n] + rows_]

    def lin(a):
        return a[0].T.reshape(IN_ROWS * 8, 128)

    g_lin = lax.dynamic_slice_in_dim(gsum_in, in_shift, IN_ROWS, axis=0).reshape(IN_ROWS * 8, 128)
    dlt, mn, vn = _adamw(lin(w_in), g_lin, lin(m_w_in), lin(v_w_in), name="adamw_w_in", tr=IN_ROWS * 2)
    big_in = tuple(a.reshape(IN_ROWS, D).T[None] for a in (g_lin, dlt, mn, vn))

    big = {
        "w_ada": (w_ada, m_w_ada, v_w_ada, g_w_ada, (D, wloc)),
        "w_branch_ssd": (w_branch_ssd, m_w_branch_ssd, v_w_branch_ssd, gpart("bssd", 256), (256, D)),
        "pool_w": (pool_w, m_pool_w, v_pool_w, gpart("pool", 32).reshape(128, PGW), (128, PGW)),
        "w_branch_pool": (w_branch_pool, m_w_branch_pool, v_w_branch_pool, gpart("bpool", 128), (128, D)),
        "w_out": (w_out, m_w_out, v_w_out, gpart("out", 128), (128, D)),
        "w_up": (w_up, m_w_up, v_w_up, gsum_mlp[:512].T, (D, 512)),
        "w_down": (w_down, m_w_down, v_w_down, gsum_mlp[512:], (512, D)),
    }
    big_out = {}
    for n, (w, mm_, vv, g, shp2) in big.items():
        dlt, mn, vn = _adamw(w.reshape(shp2), g, mm_.reshape(shp2), vv.reshape(shp2), name="adamw_" + n)
        big_out[n] = (g.reshape(w.shape), dlt.reshape(w.shape), mn.reshape(w.shape), vn.reshape(w.shape))

    order = ["w_ada", "b_ada", "norm_mix_w", "w_in", "conv_w", "conv_b", "dt_bias", "a_log", "d_skip",
             "ssd_norm_w", "w_branch_ssd", "pool_w", "pool_scale", "w_branch_pool", "w_out", "norm_mlp_w",
             "w_up", "w_down", "norm_final_w"]
    big_out["w_in"] = big_in
    res = {**small_out, **big_out}
    outs = [loss, grad_x.reshape(x.shape)]
    for k in range(4):
        outs += [res[n][k] for n in order]
    return tuple(outs)
```
